```python
import jax
import jax.numpy as jnp
from jax import lax
import numpy as np

D_MODEL = 1024
BATCH = 8
SEQ = 8192
DEPTH = 1

DSA_PATTERNS = ((128, 1), (512, 4), (2048, 16))
DSA_GROUPS = 3
DSA_HEADS = 8
DSA_HEAD_DIM = 64
DSA_WIDTH = DSA_HEADS * DSA_HEAD_DIM
DSA_BLOCK = 128
ROPE_THETA = 10000.0

GDN_HEADS = 8
GDN_KEY_DIM = 64
GDN_VAL_DIM = 64
GDN_K_WIDTH = GDN_HEADS * GDN_KEY_DIM
GDN_V_WIDTH = GDN_HEADS * GDN_VAL_DIM
GDN_CONV = 4
GDN_CHUNK = 64

NORM_EPS = 1e-6

IN_SIZES = (
    DSA_GROUPS * 3 * DSA_WIDTH,
    DSA_WIDTH,
    2 * GDN_K_WIDTH + GDN_V_WIDTH,
    GDN_V_WIDTH,
    GDN_HEADS,
    GDN_HEADS,
    D_MODEL,
    D_MODEL,
)
IN_WIDTH = sum(IN_SIZES)

kernel_name = 'hybrid_dilated_attn_gated_deltanet_block'


def _rmsnorm(x, w):
    xf = x.astype(jnp.float32)
    y = xf * lax.rsqrt(jnp.mean(xf * xf, axis=-1, keepdims=True) + NORM_EPS)
    return (y * w.astype(jnp.float32)).astype(x.dtype)


def _l2norm(t):
    return t * lax.rsqrt(jnp.sum(t * t, axis=-1, keepdims=True) + NORM_EPS)


def _split_columns(h):
    parts, start = [], 0
    for size in IN_SIZES:
        parts.append(h[..., start:start + size])
        start += size
    return parts


def _rope_tables(seq, dim):
    inv_freq = ROPE_THETA ** (-jnp.arange(0, dim, 2, dtype=jnp.float32) / dim)
    ang = jnp.arange(seq, dtype=jnp.float32)[:, None] * inv_freq[None, :]
    ang = jnp.concatenate([ang, ang], axis=-1)
    return jnp.cos(ang), jnp.sin(ang)


def _apply_rope(t, cos, sin):
    half = t.shape[-1] // 2
    rot = jnp.concatenate([-t[..., half:], t[..., :half]], axis=-1)
    return t * cos[:, None, None, :] + rot * sin[:, None, None, :]


def _dilated_window_attention(q, k, v, window, dilation):
    b, s, h, dh = q.shape
    n_back = window // dilation
    sub_len = s // dilation
    n_blk = -(-sub_len // DSA_BLOCK)
    pad = n_blk * DSA_BLOCK - sub_len

    def to_blocks(t):
        t = t.reshape(b, sub_len, dilation, h, dh).transpose(0, 2, 1, 3, 4)
        t = jnp.pad(t, ((0, 0), (0, 0), (0, pad), (0, 0), (0, 0)))
        return t.reshape(b, dilation, n_blk, DSA_BLOCK, h, dh)

    def with_previous_block(t):
        prev = jnp.pad(t, ((0, 0), (0, 0), (1, 0), (0, 0), (0, 0), (0, 0)))[:, :, :-1]
        return jnp.concatenate([prev, t], axis=3)

    qb = to_blocks(q)
    kb = with_previous_block(to_blocks(k))
    vb = with_previous_block(to_blocks(v))
    scores = jnp.einsum('brnqhd,brnkhd->brnhqk', qb, kb) * (dh ** -0.5)
    qi = jnp.arange(DSA_BLOCK)[:, None]
    kj = jnp.arange(2 * DSA_BLOCK)[None, :]
    dist = qi + DSA_BLOCK - kj
    key_idx = jnp.arange(n_blk)[:, None, None] * DSA_BLOCK + kj - DSA_BLOCK
    valid = (dist >= 0) & (dist <= n_back) & (key_idx >= 0)
    scores = jnp.where(valid[:, None], scores, -jnp.inf)
    m = jnp.max(scores, axis=-1, keepdims=True)
    p = jnp.exp(scores - m)
    den = jnp.sum(p, axis=-1, keepdims=True)
    o = jnp.einsum('brnhqk,brnkhd->brnqhd', p / den, vb)
    lse = (m + jnp.log(den))[..., 0].transpose(0, 1, 2, 4, 3)

    def from_blocks(t):
        t = t.reshape(b, dilation, n_blk * DSA_BLOCK, *t.shape[4:])[:, :, :sub_len]
        t = jnp.moveaxis(t, 1, 2)
        return t.reshape(b, s, *t.shape[3:])

    return from_blocks(o), from_blocks(lse)


def _dilated_mixture(q, k, v):
    outs, lses = [], []
    for g, (window, dilation) in enumerate(DSA_PATTERNS):
        o, lse = _dilated_window_attention(q[:, :, g], k[:, :, g], v[:, :, g], window, dilation)
        outs.append(o)
        lses.append(lse)
    wts = jax.nn.softmax(jnp.stack(lses), axis=0)
    o = jnp.einsum('gbsh,gbshd->bshd', wts, jnp.stack(outs))
    return o.reshape(o.shape[0], o.shape[1], -1)


def _causal_depthwise_conv(x, w):
    k, c = w.shape
    return lax.conv_general_dilated(
        x, w[:, None, :].astype(x.dtype), window_strides=(1,), padding=((k - 1, 0),),
        dimension_numbers=('NWC', 'WIO', 'NWC'), feature_group_count=c)


def _gated_delta_rule(q, k, v, g, beta):
    b, s, h, dk = q.shape
    dv = v.shape[-1]
    c = GDN_CHUNK
    n = s // c

    def chunk(t):
        return t.reshape(b, n, c, h, -1).transpose(0, 3, 1, 2, 4)

    qc, kc, vc = chunk(q), chunk(k), chunk(v)
    gc = g.reshape(b, n, c, h).transpose(0, 3, 1, 2)
    bc = beta.reshape(b, n, c, h).transpose(0, 3, 1, 2)
    G = jnp.cumsum(gc, axis=-1)
    causal = jnp.tril(jnp.ones((c, c), dtype=bool))
    strict = jnp.tril(jnp.ones((c, c), dtype=bool), k=-1)
    decay_incl = jnp.exp(jnp.where(causal, G[..., :, None] - G[..., None, :], -jnp.inf))
    decay_strict = jnp.where(strict, decay_incl, 0.0)
    k_beta = kc * bc[..., None]
    a = jnp.einsum('bhnid,bhnjd->bhnij', k_beta, kc) * decay_strict
    eye = jnp.eye(c, dtype=a.dtype)
    t_inv = lax.linalg.triangular_solve(eye + a, jnp.broadcast_to(eye, a.shape),
                                        left_side=True, lower=True, unit_diagonal=True)
    u = t_inv @ (vc * bc[..., None])
    w = t_inv @ (k_beta * jnp.exp(G)[..., None])
    attn = jnp.einsum('bhnid,bhnjd->bhnij', qc, kc) * decay_incl
    q_dec = qc * jnp.exp(G)[..., None]
    g_last = G[..., -1:]
    k_dec = kc * jnp.exp(g_last - G)[..., None]
    chunk_decay = jnp.exp(g_last[..., 0])
    xs = tuple(jnp.moveaxis(t, 2, 0) for t in (q_dec, attn, u, w, k_dec, chunk_decay))

    def step(state, inp):
        q_e, at, u_c, w_c, k_d, dec = inp
        v_new = u_c - jnp.einsum('bhck,bhkv->bhcv', w_c, state)
        o = jnp.einsum('bhck,bhkv->bhcv', q_e, state) + jnp.einsum('bhij,bhjv->bhiv', at, v_new)
        state = state * dec[..., None, None] + jnp.einsum('bhck,bhcv->bhkv', k_d, v_new)
        return state, o

    state0 = jnp.zeros((b, h, dk, dv), dtype=q.dtype)
    _, o = lax.scan(step, state0, xs)
    return o.transpose(1, 0, 3, 2, 4).reshape(b, s, h, dv)


def _fwd_setup_inputs(seed: int = 0) -> dict:
    key = jax.random.key(seed)
    ks = jax.random.split(key, 12)
    f32 = jnp.float32
    x = jax.random.normal(ks[0], (BATCH, SEQ, D_MODEL), f32)
    norm_w = 1.0 + 0.05 * jax.random.normal(ks[1], (DEPTH, D_MODEL), f32)
    w_in = jax.random.normal(ks[2], (DEPTH, D_MODEL, IN_WIDTH), f32) * D_MODEL ** -0.5
    conv_w = jax.random.normal(ks[3], (DEPTH, GDN_CONV, 2 * GDN_K_WIDTH + GDN_V_WIDTH), f32) * GDN_CONV ** -0.5
    a_log = jnp.log(jax.random.uniform(ks[4], (DEPTH, GDN_HEADS), f32, 1.0, 16.0))
    dt_bias = 0.5 * jax.random.normal(ks[5], (DEPTH, GDN_HEADS), f32)
    gdn_norm_w = 1.0 + 0.05 * jax.random.normal(ks[6], (DEPTH, GDN_VAL_DIM), f32)
    w_up_a = jax.random.normal(ks[7], (DEPTH, DSA_WIDTH, D_MODEL), f32) * DSA_WIDTH ** -0.5
    w_up_b = jax.random.normal(ks[8], (DEPTH, GDN_V_WIDTH, D_MODEL), f32) * GDN_V_WIDTH ** -0.5
    w_out = jax.random.normal(ks[9], (DEPTH, D_MODEL, D_MODEL), f32) * D_MODEL ** -0.5
    final_norm_w = 1.0 + 0.05 * jax.random.normal(ks[10], (D_MODEL,), f32)
    return {'x': x, 'norm_w': norm_w, 'w_in': w_in, 'conv_w': conv_w, 'a_log': a_log,
            'dt_bias': dt_bias, 'gdn_norm_w': gdn_norm_w, 'w_up_a': w_up_a, 'w_up_b': w_up_b,
            'w_out': w_out, 'final_norm_w': final_norm_w}


def _fwd_reference(x, norm_w, w_in, conv_w, a_log, dt_bias, gdn_norm_w, w_up_a, w_up_b, w_out, final_norm_w):
    f32 = jnp.float32
    b, s, _ = x.shape
    cos, sin = _rope_tables(s, DSA_HEAD_DIM)
    for layer in range(DEPTH):
        h = _rmsnorm(x, norm_w[layer])
        proj = h @ w_in[layer]
        dsa_qkv, dsa_z, gdn_qkv, gdn_z, gdn_b, gdn_a, gate_a, gate_b = _split_columns(proj)

        qkv = dsa_qkv.astype(f32).reshape(b, s, DSA_GROUPS, 3, DSA_HEADS, DSA_HEAD_DIM)
        q_a = _apply_rope(qkv[:, :, :, 0], cos, sin)
        k_a = _apply_rope(qkv[:, :, :, 1], cos, sin)
        v_a = qkv[:, :, :, 2]
        o_a = _dilated_mixture(q_a, k_a, v_a).astype(x.dtype)
        y_a = (o_a * jax.nn.silu(dsa_z)) @ w_up_a[layer]

        cqkv = jax.nn.silu(_causal_depthwise_conv(gdn_qkv, conv_w[layer])).astype(f32)
        gq, gk, gv = jnp.split(cqkv, [GDN_K_WIDTH, 2 * GDN_K_WIDTH], axis=-1)
        gq = _l2norm(gq.reshape(b, s, GDN_HEADS, GDN_KEY_DIM)) * GDN_KEY_DIM ** -0.5
        gk = _l2norm(gk.reshape(b, s, GDN_HEADS, GDN_KEY_DIM))
        gv = gv.reshape(b, s, GDN_HEADS, GDN_VAL_DIM)
        beta = jax.nn.sigmoid(gdn_b.astype(f32))
        g = -jnp.exp(a_log[layer].astype(f32)) * jax.nn.softplus(gdn_a.astype(f32) + dt_bias[layer].astype(f32))
        o_b = _gated_delta_rule(gq, gk, gv, g, beta)
        o_b = _rmsnorm(o_b, gdn_norm_w[layer]).reshape(b, s, GDN_V_WIDTH).astype(x.dtype)
        y_b = (o_b * jax.nn.silu(gdn_z)) @ w_up_b[layer]

        merged = jax.nn.sigmoid(gate_a) * y_a + jax.nn.sigmoid(gate_b) * y_b
        x = x + merged @ w_out[layer]
    return _rmsnorm(x, final_norm_w)


import jax as _jax
import jax.numpy as _jnp

TWIN_FORMAT = 'train_step'
FWD_PARAMS = ['x', 'norm_w', 'w_in', 'conv_w', 'a_log', 'dt_bias', 'gdn_norm_w', 'w_up_a', 'w_up_b', 'w_out', 'final_norm_w']
TWIN_WEIGHTS = ['norm_w', 'w_in', 'conv_w', 'a_log', 'dt_bias', 'gdn_norm_w', 'w_up_a', 'w_up_b', 'w_out', 'final_norm_w']
TWIN_DIFF_INPUT = 'x'
TWIN_INPUTS = ['x', 'norm_w', 'w_in', 'conv_w', 'a_log', 'dt_bias', 'gdn_norm_w', 'w_up_a', 'w_up_b', 'w_out', 'final_norm_w', 'loss_target', 'm_norm_w', 'm_w_in', 'm_conv_w', 'm_a_log', 'm_dt_bias', 'm_gdn_norm_w', 'm_w_up_a', 'm_w_up_b', 'm_w_out', 'm_final_norm_w', 'v_norm_w', 'v_w_in', 'v_conv_w', 'v_a_log', 'v_dt_bias', 'v_gdn_norm_w', 'v_w_up_a', 'v_w_up_b', 'v_w_out', 'v_final_norm_w']
TWIN_OUTPUTS = ['loss', 'grad_x', 'grad_norm_w', 'grad_w_in', 'grad_conv_w', 'grad_a_log', 'grad_dt_bias', 'grad_gdn_norm_w', 'grad_w_up_a', 'grad_w_up_b', 'grad_w_out', 'grad_final_norm_w', 'delta_norm_w', 'delta_w_in', 'delta_conv_w', 'delta_a_log', 'delta_dt_bias', 'delta_gdn_norm_w', 'delta_w_up_a', 'delta_w_up_b', 'delta_w_out', 'delta_final_norm_w', 'new_m_norm_w', 'new_m_w_in', 'new_m_conv_w', 'new_m_a_log', 'new_m_dt_bias', 'new_m_gdn_norm_w', 'new_m_w_up_a', 'new_m_w_up_b', 'new_m_w_out', 'new_m_final_norm_w', 'new_v_norm_w', 'new_v_w_in', 'new_v_conv_w', 'new_v_a_log', 'new_v_dt_bias', 'new_v_gdn_norm_w', 'new_v_w_up_a', 'new_v_w_up_b', 'new_v_w_out', 'new_v_final_norm_w']
TWIN_LEAF_KINDS = {'loss': 'loss', 'grad_x': 'grad_x', 'grad_norm_w': 'grad_w', 'grad_w_in': 'grad_w', 'grad_conv_w': 'grad_w', 'grad_a_log': 'grad_w', 'grad_dt_bias': 'grad_w', 'grad_gdn_norm_w': 'grad_w', 'grad_w_up_a': 'grad_w', 'grad_w_up_b': 'grad_w', 'grad_w_out': 'grad_w', 'grad_final_norm_w': 'grad_w', 'delta_norm_w': 'delta_w', 'delta_w_in': 'delta_w', 'delta_conv_w': 'delta_w', 'delta_a_log': 'delta_w', 'delta_dt_bias': 'delta_w', 'delta_gdn_norm_w': 'delta_w', 'delta_w_up_a': 'delta_w', 'delta_w_up_b': 'delta_w', 'delta_w_out': 'delta_w', 'delta_final_norm_w': 'delta_w', 'new_m_norm_w': 'new_m', 'new_m_w_in': 'new_m', 'new_m_conv_w': 'new_m', 'new_m_a_log': 'new_m', 'new_m_dt_bias': 'new_m', 'new_m_gdn_norm_w': 'new_m', 'new_m_w_up_a': 'new_m', 'new_m_w_up_b': 'new_m', 'new_m_w_out': 'new_m', 'new_m_final_norm_w': 'new_m', 'new_v_norm_w': 'new_v', 'new_v_w_in': 'new_v', 'new_v_conv_w': 'new_v', 'new_v_a_log': 'new_v', 'new_v_dt_bias': 'new_v', 'new_v_gdn_norm_w': 'new_v', 'new_v_w_up_a': 'new_v', 'new_v_w_up_b': 'new_v', 'new_v_w_out': 'new_v', 'new_v_final_norm_w': 'new_v'}


def _forward(args):
    return _fwd_reference(*[args[k] for k in FWD_PARAMS])


def _output_shape():
    def fwd():
        inp = _fwd_setup_inputs(0)
        return _fwd_reference(*[inp[k] for k in FWD_PARAMS])
    out = _jax.eval_shape(fwd)
    return out.shape, out.dtype

N_MICROBATCH = 1
ADAM_LR = 0.001
ADAM_B1 = 0.9
ADAM_B2 = 0.999
ADAM_EPS = 1e-08
ADAM_WD = 0.01
ADAM_STEP = 10
PER_EXAMPLE_BATCH_AXIS = {'x': 0, 'loss_target': 0}
SHARED_INPUTS = []
_WEIGHT_DTYPES = {'norm_w': _jnp.float32, 'w_in': _jnp.float32, 'conv_w': _jnp.float32, 'a_log': _jnp.float32, 'dt_bias': _jnp.float32, 'gdn_norm_w': _jnp.float32, 'w_up_a': _jnp.float32, 'w_up_b': _jnp.float32, 'w_out': _jnp.float32, 'final_norm_w': _jnp.float32}
MOMENT_SCALE = {'norm_w': 1.752941e-01, 'w_in': 5.805031e-02, 'conv_w': 1.268689e-01, 'a_log': 4.152100e-01, 'dt_bias': 3.811372e-01, 'gdn_norm_w': 2.768242e-01, 'w_up_a': 1.475523e-02, 'w_up_b': 7.048715e-02, 'w_out': 7.038827e-02, 'final_norm_w': 6.410675e+01}


def _to_microbatches(a, axis):
    t = _jnp.moveaxis(a, axis, 0)
    t = t.reshape((N_MICROBATCH, t.shape[0] // N_MICROBATCH) + t.shape[1:])
    return _jnp.moveaxis(t, 1, axis + 1)


def setup_inputs(seed: int = 0) -> dict:
    inp = _fwd_setup_inputs(seed)
    key = _jax.random.fold_in(_jax.random.key(seed), 7919)
    shape, _ = _output_shape()
    out = dict(inp)
    out["loss_target"] = _jax.random.normal(_jax.random.fold_in(key, 0), shape, _jnp.float32)
    for i, name in enumerate(TWIN_WEIGHTS):
        w = inp[name].astype(_jnp.float32)
        if MOMENT_SCALE is None:
            s = _jnp.sqrt(_jnp.mean(_jnp.square(w)) + 1e-30)
        else:
            s = MOMENT_SCALE[name]
        km, kv = _jax.random.split(_jax.random.fold_in(key, i + 1))
        out[name] = w
        out["m_" + name] = s * _jax.random.normal(km, w.shape, _jnp.float32)
        out["v_" + name] = (s * s) * _jax.random.uniform(kv, w.shape, _jnp.float32, 0.5, 1.5)
    if N_MICROBATCH > 1:
        for name, axis in PER_EXAMPLE_BATCH_AXIS.items():
            out[name] = _to_microbatches(out[name], axis)
    return {'x': out['x'], 'norm_w': out['norm_w'], 'w_in': out['w_in'], 'conv_w': out['conv_w'], 'a_log': out['a_log'], 'dt_bias': out['dt_bias'], 'gdn_norm_w': out['gdn_norm_w'], 'w_up_a': out['w_up_a'], 'w_up_b': out['w_up_b'], 'w_out': out['w_out'], 'final_norm_w': out['final_norm_w'], 'loss_target': out['loss_target'], 'm_norm_w': out['m_norm_w'], 'm_w_in': out['m_w_in'], 'm_conv_w': out['m_conv_w'], 'm_a_log': out['m_a_log'], 'm_dt_bias': out['m_dt_bias'], 'm_gdn_norm_w': out['m_gdn_norm_w'], 'm_w_up_a': out['m_w_up_a'], 'm_w_up_b': out['m_w_up_b'], 'm_w_out': out['m_w_out'], 'm_final_norm_w': out['m_final_norm_w'], 'v_norm_w': out['v_norm_w'], 'v_w_in': out['v_w_in'], 'v_conv_w': out['v_conv_w'], 'v_a_log': out['v_a_log'], 'v_dt_bias': out['v_dt_bias'], 'v_gdn_norm_w': out['v_gdn_norm_w'], 'v_w_up_a': out['v_w_up_a'], 'v_w_up_b': out['v_w_up_b'], 'v_w_out': out['v_w_out'], 'v_final_norm_w': out['v_final_norm_w']}


def _loss(weights, diff, rest, loss_target):
    with _jax.named_scope("forward"):
        args = {**rest, TWIN_DIFF_INPUT: diff, **{k: w.astype(_WEIGHT_DTYPES[k]) for k, w in weights.items()}}
        y = _forward(args)
    with _jax.named_scope("loss_head"):
        err = _jnp.square(y.astype(_jnp.float32) - loss_target)
        return 0.5 * _jnp.sum(_jnp.mean(err, axis=-1)) if err.ndim else 0.5 * err


def _adamw(w, g, m, v):
    m = ADAM_B1 * m + (1.0 - ADAM_B1) * g
    v = ADAM_B2 * v + (1.0 - ADAM_B2) * _jnp.square(g)
    m_hat = m / (1.0 - ADAM_B1 ** ADAM_STEP)
    v_hat = v / (1.0 - ADAM_B2 ** ADAM_STEP)
    delta = -ADAM_LR * (m_hat / (_jnp.sqrt(v_hat) + ADAM_EPS) + ADAM_WD * w)
    return delta, m, v


def reference(x, norm_w, w_in, conv_w, a_log, dt_bias, gdn_norm_w, w_up_a, w_up_b, w_out, final_norm_w, loss_target, m_norm_w, m_w_in, m_conv_w, m_a_log, m_dt_bias, m_gdn_norm_w, m_w_up_a, m_w_up_b, m_w_out, m_final_norm_w, v_norm_w, v_w_in, v_conv_w, v_a_log, v_dt_bias, v_gdn_norm_w, v_w_up_a, v_w_up_b, v_w_out, v_final_norm_w):
    given = dict(x=x, norm_w=norm_w, w_in=w_in, conv_w=conv_w, a_log=a_log, dt_bias=dt_bias, gdn_norm_w=gdn_norm_w, w_up_a=w_up_a, w_up_b=w_up_b, w_out=w_out, final_norm_w=final_norm_w, loss_target=loss_target, m_norm_w=m_norm_w, m_w_in=m_w_in, m_conv_w=m_conv_w, m_a_log=m_a_log, m_dt_bias=m_dt_bias, m_gdn_norm_w=m_gdn_norm_w, m_w_up_a=m_w_up_a, m_w_up_b=m_w_up_b, m_w_out=m_w_out, m_final_norm_w=m_final_norm_w, v_norm_w=v_norm_w, v_w_in=v_w_in, v_conv_w=v_conv_w, v_a_log=v_a_log, v_dt_bias=v_dt_bias, v_gdn_norm_w=v_gdn_norm_w, v_w_up_a=v_w_up_a, v_w_up_b=v_w_up_b, v_w_out=v_w_out, v_final_norm_w=v_final_norm_w)
    weights = {n: given[n] for n in TWIN_WEIGHTS}
    shared = {n: given[n] for n in SHARED_INPUTS}
    per_example = {n: given[n] for n in ['x']}
    grad_fn = _jax.value_and_grad(_loss, argnums=(0, 1))

    def one_microbatch(ex, loss_target):
        ex = dict(ex)
        diff = ex.pop(TWIN_DIFF_INPUT)
        return grad_fn(weights, diff, {**shared, **ex}, loss_target)

    if N_MICROBATCH == 1:
        loss, (grad_w, grad_x) = one_microbatch(per_example, given["loss_target"])
    else:
        def body(carry, xs):
            loss_sum, grad_sum = carry
            l_k, (gw_k, gx_k) = one_microbatch(xs[0], xs[1])
            with _jax.named_scope("update"):
                return (loss_sum + l_k, _jax.tree.map(_jnp.add, grad_sum, gw_k)), gx_k

        init = (_jnp.zeros((), _jnp.float32), _jax.tree.map(_jnp.zeros_like, weights))
        (loss, grad_w), grad_x = _jax.lax.scan(body, init, (per_example, given["loss_target"]))
    with _jax.named_scope("update"):
        delta_w, new_m, new_v = {}, {}, {}
        for n in TWIN_WEIGHTS:
            delta_w[n], new_m[n], new_v[n] = _adamw(weights[n], grad_w[n], given["m_" + n], given["v_" + n])
    return (loss, grad_x, *[grad_w[n] for n in TWIN_WEIGHTS], *[delta_w[n] for n in TWIN_WEIGHTS],
            *[new_m[n] for n in TWIN_WEIGHTS], *[new_v[n] for n in TWIN_WEIGHTS])
```

```python
import functools

import jax
import jax.numpy as jnp
from jax import lax
from jax.experimental import pallas as pl
from jax.experimental.pallas import tpu as pltpu

F32 = jnp.float32
MXU_DTYPE = jnp.bfloat16
HIGHEST = lax.Precision.HIGHEST

D_MODEL = 1024
HEADS = 8
HEAD_DIM = 64
WIDTH = HEADS * HEAD_DIM
NORM_EPS = 1e-6
ROPE_THETA = 10000.0
ATT_BLOCK = 128
DILATIONS = (1, 4, 16)
GDN_CHUNK = 64
GDN_CONV = 4
IN_WIDTH = 9232
SEG_A, SEG_ZA, SEG_B, SEG_ZB, SEG_BA, SEG_GA, SEG_GB, PACKED_WIDTH = 0, 4608, 5120, 6656, 7168, 7680, 8704, 9728
BA_END = 7184
VMEM_LIMIT = 56 * 1024 * 1024

ADAM_LR, ADAM_B1, ADAM_B2, ADAM_EPS, ADAM_WD, ADAM_STEP = 0.001, 0.9, 0.999, 1e-08, 0.01, 10

_NN = (((1,), (0,)), ((), ()))
_NT = (((1,), (1,)), ((), ()))
_TN = (((0,), (0,)), ((), ()))


def _params(sem):
    return pltpu.CompilerParams(dimension_semantics=sem, vmem_limit_bytes=VMEM_LIMIT)


def _mxu(a, b, dims):
    return lax.dot_general(a.astype(MXU_DTYPE), b.astype(MXU_DTYPE), dims, preferred_element_type=F32)


def _sigmoid(x):
    return 1.0 / (1.0 + jnp.exp(-x))


def _softplus(x):
    return jnp.maximum(x, 0.0) + jnp.log(1.0 + jnp.exp(-jnp.abs(x)))


def _iota(shape, axis):
    return lax.broadcasted_iota(jnp.int32, shape, axis)


def _matmul(a, b, mode, name, out_dtype=F32, tm=512, tn=512, tk=512):
    if mode == "nn":
        (m, k), (k2, n) = a.shape, b.shape
    elif mode == "nt":
        (m, k), (n, k2) = a.shape, b.shape
    else:
        (k, m), (k2, n) = a.shape, b.shape
    assert k == k2
    tm, tn, tk = min(tm, m), min(tn, n), min(tk, k)
    assert m % tm == 0 and n % tn == 0 and k % tk == 0
    nk = k // tk
    dims = {"nn": _NN, "nt": _NT, "tn": _TN}[mode]

    def body(a_ref, b_ref, o_ref, acc_ref):
        kk = pl.program_id(2)

        @pl.when(kk == 0)
        def _():
            acc_ref[...] = jnp.zeros_like(acc_ref)

        acc_ref[...] += _mxu(a_ref[...], b_ref[...], dims)

        @pl.when(kk == nk - 1)
        def _():
            o_ref[...] = acc_ref[...].astype(o_ref.dtype)

    a_spec = pl.BlockSpec((tk, tm), lambda i, j, kk: (kk, i)) if mode == "tn" else pl.BlockSpec((tm, tk), lambda i, j, kk: (i, kk))
    b_spec = pl.BlockSpec((tn, tk), lambda i, j, kk: (j, kk)) if mode == "nt" else pl.BlockSpec((tk, tn), lambda i, j, kk: (kk, j))
    return pl.pallas_call(
        body, name=name, grid=(m // tm, n // tn, nk), in_specs=[a_spec, b_spec],
        out_specs=pl.BlockSpec((tm, tn), lambda i, j, kk: (i, j)),
        out_shape=jax.ShapeDtypeStruct((m, n), out_dtype),
        scratch_shapes=[pltpu.VMEM((tm, tn), F32)],
        compiler_params=_params(("parallel", "parallel", "arbitrary")),
    )(a, b)


def _norm_proj(x, norm_w, wp, tm=512, tn=512):
    t = x.shape[0]

    def body(x_ref, nw_ref, w_ref, proj_ref, h_ref):
        @pl.when(pl.program_id(1) == 0)
        def _():
            xf = x_ref[...]
            r = lax.rsqrt(jnp.mean(xf * xf, axis=-1, keepdims=True) + NORM_EPS)
            h_ref[...] = (xf * r * nw_ref[...]).astype(h_ref.dtype)

        proj_ref[...] = jnp.dot(h_ref[...], w_ref[...], preferred_element_type=F32)

    return pl.pallas_call(
        body, name="norm_proj", grid=(t // tm, PACKED_WIDTH // tn),
        in_specs=[pl.BlockSpec((tm, D_MODEL), lambda i, j: (i, 0)),
                  pl.BlockSpec((1, D_MODEL), lambda i, j: (0, 0)),
                  pl.BlockSpec((D_MODEL, tn), lambda i, j: (0, j))],
        out_specs=[pl.BlockSpec((tm, tn), lambda i, j: (i, j)),
                   pl.BlockSpec((tm, D_MODEL), lambda i, j: (i, 0))],
        out_shape=[jax.ShapeDtypeStruct((t, PACKED_WIDTH), F32), jax.ShapeDtypeStruct((t, D_MODEL), MXU_DTYPE)],
        compiler_params=_params(("parallel", "arbitrary")),
    )(x, norm_w, wp)


def _rope_tables(t):
    lane = jnp.arange(128)
    inv_freq = ROPE_THETA ** (-jnp.arange(0, HEAD_DIM, 2, dtype=F32) / HEAD_DIM)
    ang = jnp.arange(t, dtype=F32)[:, None] * inv_freq[None, :]
    ang = jnp.concatenate([ang, ang, ang, ang], axis=-1)
    first_half = (lane % HEAD_DIM) < HEAD_DIM // 2
    cos, sin = jnp.cos(ang), jnp.sin(ang)
    return cos, jnp.where(first_half, -sin, 0.0), jnp.where(first_half, 0.0, sin)


def _rope_block(x, cos, sin_lo, sin_hi, sign):
    outs = []
    for c in range(8):
        xc = x[:, c * 128:(c + 1) * 128]
        rot = pltpu.roll(xc, 96, 1) * sin_lo + pltpu.roll(xc, 32, 1) * sin_hi
        outs.append(xc * cos + sign * rot)
    outs.append(x[:, 2 * WIDTH:])
    return jnp.concatenate(outs, axis=1)


def _rope_fwd(proj, tables, tm=256):
    t = proj.shape[0]

    def body(x_ref, c_ref, sl_ref, sh_ref, o_ref):
        o_ref[...] = _rope_block(x_ref[...], c_ref[...], sl_ref[...], sh_ref[...], 1.0).astype(o_ref.dtype)

    tab = pl.BlockSpec((tm, 128), lambda i, g: (i, 0))
    return pl.pallas_call(
        body, name="rope_fwd", grid=(t // tm, 3),
        in_specs=[pl.BlockSpec((tm, 3 * WIDTH), lambda i, g: (i, g)), tab, tab, tab],
        out_specs=pl.BlockSpec((tm, 3 * WIDTH), lambda i, g: (i, g)),
        out_shape=jax.ShapeDtypeStruct((t, 9 * WIDTH), MXU_DTYPE),
        compiler_params=_params(("parallel", "parallel")),
    )(proj, *tables)


def _rope_bwd(dproj, dqkv, group, tables, tm=256):
    t = dqkv.shape[0]

    def body(*refs):
        x_ref, c_ref, sl_ref, sh_ref, o_ref = refs[-5:]
        o_ref[...] = _rope_block(x_ref[...], c_ref[...], sl_ref[...], sh_ref[...], -1.0).astype(o_ref.dtype)

    tab = pl.BlockSpec((tm, 128), lambda i: (i, 0))
    specs = [pl.BlockSpec((tm, 3 * WIDTH), lambda i: (i, 0)), tab, tab, tab]
    args = [dqkv, *tables]
    aliases = {}
    if dproj is not None:
        specs = [pl.BlockSpec(memory_space=pl.ANY)] + specs
        args = [dproj] + args
        aliases = {0: 0}
    return pl.pallas_call(
        body, name=f"rope_bwd{group}", grid=(t // tm,), in_specs=specs,
        out_specs=pl.BlockSpec((tm, 3 * WIDTH), lambda i: (i, group)),
        out_shape=jax.ShapeDtypeStruct((t, PACKED_WIDTH), MXU_DTYPE),
        input_output_aliases=aliases,
        compiler_params=_params(("parallel",)),
    )(*args)


def _att_masks():
    qi = _iota((ATT_BLOCK, ATT_BLOCK), 0)
    kj = _iota((ATT_BLOCK, ATT_BLOCK), 1)
    return kj <= qi, kj >= qi


def _att_fwd(qkv, d, name):
    rows = qkv.shape[0]
    nb = rows // ATT_BLOCK
    scale = HEAD_DIM ** -0.5

    def body(q_ref, kc_ref, kp_ref, vc_ref, vp_ref, o_ref, lse_ref):
        has_prev = pl.program_id(1) > 0
        m_cur, m_prev = _att_masks()
        m_prev = m_prev & has_prev
        for h in range(HEADS):
            sl = slice(h * HEAD_DIM, (h + 1) * HEAD_DIM)
            q = q_ref[:, sl]
            s_c = jnp.where(m_cur, _mxu(q, kc_ref[:, sl], _NT) * scale, -jnp.inf)
            s_p = jnp.where(m_prev, _mxu(q, kp_ref[:, sl], _NT) * scale, -jnp.inf)
            m = jnp.maximum(jnp.max(s_c, axis=1, keepdims=True), jnp.max(s_p, axis=1, keepdims=True))
            p_c = jnp.exp(s_c - m)
            p_p = jnp.exp(s_p - m)
            den = jnp.sum(p_c, axis=1, keepdims=True) + jnp.sum(p_p, axis=1, keepdims=True)
            o = (_mxu(p_c, vc_ref[:, sl], _NN) + _mxu(p_p, vp_ref[:, sl], _NN)) / den
            o_ref[:, sl] = o
            lse_ref[:, sl] = jnp.broadcast_to(m + jnp.log(den), (ATT_BLOCK, HEAD_DIM))

    def cur(c):
        return pl.BlockSpec((ATT_BLOCK, WIDTH), lambda r, i: (i, 3 * r + c))

    def prev(c):
        return pl.BlockSpec((ATT_BLOCK, WIDTH), lambda r, i: (jnp.maximum(i - 1, 0), 3 * r + c))

    out = pl.BlockSpec((ATT_BLOCK, WIDTH), lambda r, i: (i, r))
    return pl.pallas_call(
        body, name=name, grid=(d, nb), in_specs=[cur(0), cur(1), prev(1), cur(2), prev(2)],
        out_specs=[out, out],
        out_shape=[jax.ShapeDtypeStruct((rows, d * WIDTH), F32)] * 2,
        compiler_params=_params(("parallel", "arbitrary")),
    )(qkv, qkv, qkv, qkv, qkv)


def _att_bwd(qkv, do, lse, delta, d, name):
    rows = qkv.shape[0]
    nb = rows // ATT_BLOCK
    scale = HEAD_DIM ** -0.5

    def body(q_ref, qn_ref, kc_ref, kp_ref, vc_ref, vp_ref, do_ref, don_ref, l_ref, ln_ref, dl_ref, dln_ref, o_ref):
        i = pl.program_id(1)
        m_cur, m_prev = _att_masks()
        m_p = m_prev & (i > 0)
        m_n = m_prev & (i < nb - 1)

        def probs(q, k, lse_col, mask):
            return jnp.where(mask, jnp.exp(_mxu(q, k, _NT) * scale - lse_col), 0.0)

        for h in range(HEADS):
            sl = slice(h * HEAD_DIM, (h + 1) * HEAD_DIM)
            c1 = slice(h * HEAD_DIM, h * HEAD_DIM + 1)
            q, qn, kc, kp, vc, vp = q_ref[:, sl], qn_ref[:, sl], kc_ref[:, sl], kp_ref[:, sl], vc_ref[:, sl], vp_ref[:, sl]
            do_c, do_n = do_ref[:, sl], don_ref[:, sl]
            lse_c, lse_n, dl_c, dl_n = l_ref[:, c1], ln_ref[:, c1], dl_ref[:, c1], dln_ref[:, c1]
            p = probs(q, kc, lse_c, m_cur)
            ds = p * (_mxu(do_c, vc, _NT) - dl_c)
            dq = _mxu(ds, kc, _NN)
            dk = _mxu(ds, q, _TN)
            dv = _mxu(p, do_c, _TN)
            p = probs(q, kp, lse_c, m_p)
            ds = p * (_mxu(do_c, vp, _NT) - dl_c)
            dq += _mxu(ds, kp, _NN)
            p = probs(qn, kc, lse_n, m_n)
            ds = p * (_mxu(do_n, vc, _NT) - dl_n)
            dk += _mxu(ds, qn, _TN)
            dv += _mxu(p, do_n, _TN)
            o_ref[:, sl] = dq * scale
            o_ref[:, WIDTH + h * HEAD_DIM:WIDTH + (h + 1) * HEAD_DIM] = dk * scale
            o_ref[:, 2 * WIDTH + h * HEAD_DIM:2 * WIDTH + (h + 1) * HEAD_DIM] = dv

    def qkv_spec(c, shift):
        def idx(r, i):
            return (jnp.clip(i + shift, 0, nb - 1), 3 * r + c)
        return pl.BlockSpec((ATT_BLOCK, WIDTH), idx)

    def tok_spec(shift):
        def idx(r, i):
            return (jnp.clip(i + shift, 0, nb - 1), r)
        return pl.BlockSpec((ATT_BLOCK, WIDTH), idx)

    return pl.pallas_call(
        body, name=name, grid=(d, nb),
        in_specs=[qkv_spec(0, 0), qkv_spec(0, 1), qkv_spec(1, 0), qkv_spec(1, -1), qkv_spec(2, 0), qkv_spec(2, -1),
                  tok_spec(0), tok_spec(1), tok_spec(0), tok_spec(1), tok_spec(0), tok_spec(1)],
        out_specs=pl.BlockSpec((ATT_BLOCK, 3 * WIDTH), lambda r, i: (i, r)),
        out_shape=jax.ShapeDtypeStruct((rows, d * 3 * WIDTH), F32),
        compiler_params=_params(("parallel", "arbitrary")),
    )(qkv, qkv, qkv, qkv, qkv, qkv, do, do, lse, lse, delta, delta)


def _att_merge(os_, lses, proj, tm=256):
    t = proj.shape[0]

    def body(o0, o1, o2, l0, l1, l2, z_ref, oz_ref, o_ref, lse_ref):
        a, b, c = l0[...], l1[...], l2[...]
        m = jnp.maximum(jnp.maximum(a, b), c)
        wa, wb, wc = jnp.exp(a - m), jnp.exp(b - m), jnp.exp(c - m)
        den = wa + wb + wc
        o = (wa * o0[...] + wb * o1[...] + wc * o2[...]) / den
        z = z_ref[...]
        o_ref[...] = o
        lse_ref[...] = m + jnp.log(den)
        oz_ref[...] = (o * z * _sigmoid(z)).astype(oz_ref.dtype)

    tok = pl.BlockSpec((tm, WIDTH), lambda i: (i, 0))
    return pl.pallas_call(
        body, name="att_merge", grid=(t // tm,),
        in_specs=[tok] * 6 + [pl.BlockSpec((tm, WIDTH), lambda i: (i, SEG_ZA // WIDTH))],
        out_specs=[tok, tok, tok],
        out_shape=[jax.ShapeDtypeStruct((t, WIDTH), MXU_DTYPE), jax.ShapeDtypeStruct((t, WIDTH), F32),
                   jax.ShapeDtypeStruct((t, WIDTH), F32)],
        compiler_params=_params(("parallel",)),
    )(*os_, *lses, proj)


def _att_merge_bwd(dproj, d_oz, o, proj, tm=256):
    t = proj.shape[0]

    def body(dp_ref, doz_ref, o_ref, z_ref, dz_ref, do_ref, dl_ref):
        z, ov, g = z_ref[...], o_ref[...], doz_ref[...]
        sg = _sigmoid(z)
        do = g * z * sg
        dz_ref[...] = (g * ov * sg * (1.0 + z * (1.0 - sg))).astype(dz_ref.dtype)
        do_ref[...] = do.astype(do_ref.dtype)
        prod = do * ov
        for h in range(HEADS):
            sl = slice(h * HEAD_DIM, (h + 1) * HEAD_DIM)
            dl_ref[:, sl] = jnp.broadcast_to(jnp.sum(prod[:, sl], axis=1, keepdims=True), (tm, HEAD_DIM))

    tok = pl.BlockSpec((tm, WIDTH), lambda i: (i, 0))
    seg = pl.BlockSpec((tm, WIDTH), lambda i: (i, SEG_ZA // WIDTH))
    return pl.pallas_call(
        body, name="att_merge_bwd", grid=(t // tm,),
        in_specs=[pl.BlockSpec(memory_space=pl.ANY), tok, tok, seg],
        out_specs=[seg, tok, tok],
        out_shape=[jax.ShapeDtypeStruct((t, PACKED_WIDTH), MXU_DTYPE), jax.ShapeDtypeStruct((t, WIDTH), MXU_DTYPE),
                   jax.ShapeDtypeStruct((t, WIDTH), F32)],
        input_output_aliases={0: 0},
        compiler_params=_params(("parallel",)),
    )(dproj, d_oz, o, proj)


def _shift_down(x, halo, s):
    if s == 0:
        return x
    xs = pltpu.roll(x, s, 0)
    head = jnp.where(_iota((8, x.shape[1]), 0) < s, pltpu.roll(halo, s, 0), xs[0:8])
    return jnp.concatenate([head, xs[8:]], axis=0)


def _shift_up(x, nxt, s):
    if s == 0:
        return x
    n = x.shape[0]
    xs = pltpu.roll(x, n - s, 0)
    tail = jnp.where(_iota((8, x.shape[1]), 0) >= 8 - s, pltpu.roll(nxt, 8 - s, 0), xs[n - 8:])
    return jnp.concatenate([xs[:n - 8], tail], axis=0)


def _conv_fwd(proj, conv_w, tm=256):
    t = proj.shape[0]
    cb = SEG_B // WIDTH

    def body(x_ref, halo_ref, w_ref, c_ref):
        halo = jnp.where(pl.program_id(0) > 0, halo_ref[...], 0.0)
        x = x_ref[...]
        w = w_ref[...]
        acc = jnp.zeros((tm, WIDTH), F32)
        for j in range(GDN_CONV):
            acc += _shift_down(x, halo, GDN_CONV - 1 - j) * w[j:j + 1, :]
        c_ref[...] = acc

    return pl.pallas_call(
        body, name="conv_fwd", grid=(t // tm, 3),
        in_specs=[pl.BlockSpec((tm, WIDTH), lambda i, c: (i, cb + c)),
                  pl.BlockSpec((8, WIDTH), lambda i, c: (jnp.maximum(i * (tm // 8) - 1, 0), cb + c)),
                  pl.BlockSpec((GDN_CONV, WIDTH), lambda i, c: (0, c))],
        out_specs=pl.BlockSpec((tm, WIDTH), lambda i, c: (i, c)),
        out_shape=jax.ShapeDtypeStruct((t, 3 * WIDTH), F32),
        compiler_params=_params(("parallel", "parallel")),
    )(proj, proj, conv_w)


def _conv_bwd(dproj, dc, proj, conv_w, tm=256):
    t = proj.shape[0]
    cb = SEG_B // WIDTH
    nt = t // tm

    def body(dp_ref, dc_ref, dcn_ref, x_ref, halo_ref, w_ref, dx_ref, dw_ref):
        i = pl.program_id(1)
        w = w_ref[...]
        dcn = jnp.where(i < nt - 1, dcn_ref[...], 0.0)
        dcv = dc_ref[...]
        acc = jnp.zeros((tm, WIDTH), F32)
        for j in range(GDN_CONV):
            acc += _shift_up(dcv, dcn, GDN_CONV - 1 - j) * w[j:j + 1, :]
        dx_ref[...] = acc.astype(dx_ref.dtype)
        halo = jnp.where(i > 0, halo_ref[...], 0.0)
        x = x_ref[...]
        row8 = _iota((8, WIDTH), 0)
        part = jnp.zeros((8, WIDTH), F32)
        for j in range(GDN_CONV):
            s = jnp.sum(dcv * _shift_down(x, halo, GDN_CONV - 1 - j), axis=0, keepdims=True)
            part += jnp.where(row8 == j, s, 0.0)

        @pl.when(i == 0)
        def _():
            dw_ref[...] = jnp.zeros_like(dw_ref)

        dw_ref[...] += part

    return pl.pallas_call(
        body, name="conv_bwd", grid=(3, nt),
        in_specs=[pl.BlockSpec(memory_space=pl.ANY),
                  pl.BlockSpec((tm, WIDTH), lambda c, i: (i, c)),
                  pl.BlockSpec((8, WIDTH), lambda c, i: (jnp.minimum((i + 1) * (tm // 8), t // 8 - 1), c)),
                  pl.BlockSpec((tm, WIDTH), lambda c, i: (i, cb + c)),
                  pl.BlockSpec((8, WIDTH), lambda c, i: (jnp.maximum(i * (tm // 8) - 1, 0), cb + c)),
                  pl.BlockSpec((GDN_CONV, WIDTH), lambda c, i: (0, c))],
        out_specs=[pl.BlockSpec((tm, WIDTH), lambda c, i: (i, cb + c)),
                   pl.BlockSpec((8, WIDTH), lambda c, i: (0, c))],
        out_shape=[jax.ShapeDtypeStruct((t, PACKED_WIDTH), MXU_DTYPE), jax.ShapeDtypeStruct((8, 3 * WIDTH), F32)],
        input_output_aliases={0: 0},
        compiler_params=_params(("parallel", "arbitrary")),
    )(dproj, dc, dc, proj, proj, conv_w)


def _chunk_matrices(tm):
    r, c = _iota((tm, tm), 0), _iota((tm, tm), 1)
    same = (r // GDN_CHUNK) == (c // GDN_CHUNK)
    return jnp.where(same & (c <= r), 1.0, 0.0), jnp.where(same, 1.0, 0.0)


def _gdn_gates(ba, a_log, dt_bias):
    al = ba + dt_bias
    return _sigmoid(ba), -jnp.exp(a_log) * _softplus(al), _sigmoid(al)


def _head_lane_eye():
    return jnp.where(_iota((HEADS, 128), 1) == _iota((HEADS, 128), 0) + HEADS, 1.0, 0.0)


def _gdn_prep(conv, proj, a_log, dt_bias, tm=256):
    t = proj.shape[0]
    nc = tm // GDN_CHUNK

    def body(c_ref, ba_ref, al_ref, dt_ref, q_ref, k_ref, v_ref, b_ref, g_ref, gl_ref, grow_ref):
        c = c_ref[...]
        a = c * _sigmoid(c)
        beta, g, _ = _gdn_gates(ba_ref[:, 0:128], al_ref[...], dt_ref[...])
        lmat, cmat = _chunk_matrices(tm)
        gc = jnp.dot(lmat, g, precision=HIGHEST, preferred_element_type=F32)
        gl = jnp.dot(cmat, g, precision=HIGHEST, preferred_element_type=F32)
        grow = lax.dot_general(_head_lane_eye(), gc, _NT, precision=HIGHEST, preferred_element_type=F32)
        for h in range(HEADS):
            sl = slice(h * HEAD_DIM, (h + 1) * HEAD_DIM)
            qh, kh, vh = a[:, sl], a[:, WIDTH + h * HEAD_DIM:WIDTH + (h + 1) * HEAD_DIM], a[:, 2 * WIDTH + h * HEAD_DIM:2 * WIDTH + (h + 1) * HEAD_DIM]
            rq = lax.rsqrt(jnp.sum(qh * qh, axis=1, keepdims=True) + NORM_EPS)
            rk = lax.rsqrt(jnp.sum(kh * kh, axis=1, keepdims=True) + NORM_EPS)
            q_ref[h] = qh * (rq * HEAD_DIM ** -0.5)
            k_ref[h] = kh * rk
            v_ref[h] = vh
            b_ref[h] = jnp.broadcast_to(beta[:, h:h + 1], (tm, HEAD_DIM))
            g_ref[h] = jnp.broadcast_to(gc[:, HEADS + h:HEADS + h + 1], (tm, HEAD_DIM))
            gl_ref[h] = jnp.broadcast_to(gl[:, HEADS + h:HEADS + h + 1], (tm, HEAD_DIM))
            for cc in range(nc):
                grow_ref[h, cc] = grow[h:h + 1, cc * GDN_CHUNK:(cc + 1) * GDN_CHUNK]

    hm = pl.BlockSpec((HEADS, tm, HEAD_DIM), lambda i: (0, i, 0))
    small = pl.BlockSpec((1, 128), lambda i: (0, 0))
    hm_shape = jax.ShapeDtypeStruct((HEADS, t, HEAD_DIM), F32)
    return pl.pallas_call(
        body, name="gdn_prep", grid=(t // tm,),
        in_specs=[pl.BlockSpec((tm, 3 * WIDTH), lambda i: (i, 0)),
                  pl.BlockSpec((tm, WIDTH), lambda i: (i, SEG_BA // WIDTH)), small, small],
        out_specs=[hm] * 6 + [pl.BlockSpec((HEADS, nc, 1, GDN_CHUNK), lambda i: (0, i, 0, 0))],
        out_shape=[hm_shape] * 6 + [jax.ShapeDtypeStruct((HEADS, t // GDN_CHUNK, 1, GDN_CHUNK), F32)],
        compiler_params=_params(("parallel",)),
    )(conv, proj, a_log, dt_bias)


def _gdn_prep_bwd(dproj, conv, proj, a_log, dt_bias, dq, dk, dv, db, dg, dgl, dgrow, tm=256):
    t = proj.shape[0]
    nc = tm // GDN_CHUNK

    def body(dp_ref, c_ref, ba_ref, al_ref, dt_ref, dq_ref, dk_ref, dv_ref, db_ref, dg_ref, dgl_ref, dgrow_ref,
             dba_ref, dc_ref, small_ref, row_scr):
        c = c_ref[...]
        sg = _sigmoid(c)
        a = c * sg
        beta, g, sig_al = _gdn_gates(ba_ref[:, 0:128], al_ref[...], dt_ref[...])
        lane = _iota((1, 128), 1)
        d_beta = jnp.zeros((tm, 128), F32)
        d_gc = jnp.zeros((tm, 128), F32)
        d_gl = jnp.zeros((tm, 128), F32)
        for h in range(HEADS):
            for cc in range(nc):
                row_scr[h:h + 1, cc * GDN_CHUNK:(cc + 1) * GDN_CHUNK] = dgrow_ref[h, cc]
            d_beta += jnp.sum(db_ref[h], axis=1, keepdims=True) * jnp.where(lane == h, 1.0, 0.0)
            d_gc += jnp.sum(dg_ref[h], axis=1, keepdims=True) * jnp.where(lane == HEADS + h, 1.0, 0.0)
            d_gl += jnp.sum(dgl_ref[h], axis=1, keepdims=True) * jnp.where(lane == HEADS + h, 1.0, 0.0)
            qs, ks, vs = (slice(h * HEAD_DIM, (h + 1) * HEAD_DIM), slice(WIDTH + h * HEAD_DIM, WIDTH + (h + 1) * HEAD_DIM),
                          slice(2 * WIDTH + h * HEAD_DIM, 2 * WIDTH + (h + 1) * HEAD_DIM))
            qh, kh = a[:, qs], a[:, ks]
            rq = lax.rsqrt(jnp.sum(qh * qh, axis=1, keepdims=True) + NORM_EPS)
            rk = lax.rsqrt(jnp.sum(kh * kh, axis=1, keepdims=True) + NORM_EPS)
            gq, gk = dq_ref[h] * HEAD_DIM ** -0.5, dk_ref[h]
            da_q = rq * gq - qh * (rq * rq * rq) * jnp.sum(gq * qh, axis=1, keepdims=True)
            da_k = rk * gk - kh * (rk * rk * rk) * jnp.sum(gk * kh, axis=1, keepdims=True)
            dsilu = lambda s, x: s * (1.0 + x * (1.0 - s))
            dc_ref[:, qs] = da_q * dsilu(sg[:, qs], c[:, qs])
            dc_ref[:, ks] = da_k * dsilu(sg[:, ks], c[:, ks])
            dc_ref[:, vs] = dv_ref[h] * dsilu(sg[:, vs], c[:, vs])
        d_gc += lax.dot_general(row_scr[...], _head_lane_eye(), _TN, precision=HIGHEST, preferred_element_type=F32)
        lmat, cmat = _chunk_matrices(tm)
        d_g = (lax.dot_general(lmat, d_gc, _TN, precision=HIGHEST, preferred_element_type=F32)
               + lax.dot_general(cmat, d_gl, _TN, precision=HIGHEST, preferred_element_type=F32))
        d_al = d_g * (-jnp.exp(al_ref[...])) * sig_al
        d_bl = d_beta * beta * (1.0 - beta)
        dba_ref[...] = jnp.concatenate([d_bl + d_al, jnp.zeros((tm, WIDTH - 128), F32)], axis=1).astype(dba_ref.dtype)
        row8 = _iota((8, 128), 0)
        part = (jnp.where(row8 == 0, jnp.sum(d_g * g, axis=0, keepdims=True), 0.0)
                + jnp.where(row8 == 1, jnp.sum(d_al, axis=0, keepdims=True), 0.0))

        @pl.when(pl.program_id(0) == 0)
        def _():
            small_ref[...] = jnp.zeros_like(small_ref)

        small_ref[...] += part

    hm = pl.BlockSpec((HEADS, tm, HEAD_DIM), lambda i: (0, i, 0))
    small = pl.BlockSpec((1, 128), lambda i: (0, 0))
    seg = pl.BlockSpec((tm, WIDTH), lambda i: (i, SEG_BA // WIDTH))
    return pl.pallas_call(
        body, name="gdn_prep_bwd", grid=(t // tm,),
        in_specs=[pl.BlockSpec(memory_space=pl.ANY), pl.BlockSpec((tm, 3 * WIDTH), lambda i: (i, 0)), seg, small, small]
        + [hm] * 6 + [pl.BlockSpec((HEADS, nc, 1, GDN_CHUNK), lambda i: (0, i, 0, 0))],
        out_specs=[seg, pl.BlockSpec((tm, 3 * WIDTH), lambda i: (i, 0)), pl.BlockSpec((8, 128), lambda i: (0, 0))],
        out_shape=[jax.ShapeDtypeStruct((t, PACKED_WIDTH), MXU_DTYPE), jax.ShapeDtypeStruct((t, 3 * WIDTH), F32),
                   jax.ShapeDtypeStruct((8, 128), F32)],
        scratch_shapes=[pltpu.VMEM((HEADS, tm), F32)],
        input_output_aliases={0: 0},
        compiler_params=_params(("arbitrary",)),
    )(dproj, conv, proj, a_log, dt_bias, dq, dk, dv, db, dg, dgl, dgrow)


@jax.custom_vjp
def _MM_NN(a, b):
    return _mxu(a, b, _NN)


@jax.custom_vjp
def _MM_NT(a, b):
    return _mxu(a, b, _NT)


@jax.custom_vjp
def _MM_TN(a, b):
    return _mxu(a, b, _TN)


_MM_NN.defvjp(lambda a, b: (_mxu(a, b, _NN), (a, b)), lambda r, g: (_mxu(g, r[1], _NT), _mxu(r[0], g, _TN)))
_MM_NT.defvjp(lambda a, b: (_mxu(a, b, _NT), (a, b)), lambda r, g: (_mxu(g, r[1], _NN), _mxu(g, r[0], _TN)))
_MM_TN.defvjp(lambda a, b: (_mxu(a, b, _TN), (a, b)), lambda r, g: (_mxu(r[1], g, _NT), _mxu(r[0], g, _NN)))


def _hi(a, b, dims):
    return lax.dot_general(a, b, dims, precision=HIGHEST, preferred_element_type=F32)


def _unit_lower_inverse(a):
    c = GDN_CHUNK
    eye = jnp.where(_iota((c, c), 0) == _iota((c, c), 1), 1.0, 0.0)
    x = eye - a
    p = a
    for _ in range(5):
        p = _hi(p, p, _NN)
        x = x + _hi(x, p, _NN)
    return x


@jax.custom_vjp
def _TRI_INV(a):
    return _unit_lower_inverse(a)


def _tri_inv_fwd(a):
    x = _unit_lower_inverse(a)
    return x, x


def _tri_inv_bwd(x, g):
    return (-_hi(x, _hi(g, x, _NT), _TN),)


_TRI_INV.defvjp(_tri_inv_fwd, _tri_inv_bwd)


def _gdn_chunk(q, k, v, beta, g1, g2, gl, state, differentiable=True):
    c = GDN_CHUNK
    if differentiable:
        _mm_nn, _mm_nt, _mm_tn, _tri_inv = _MM_NN, _MM_NT, _MM_TN, _TRI_INV
    else:
        _mm_nn, _mm_nt, _mm_tn = (functools.partial(_mxu, dims=dd) for dd in (_NN, _NT, _TN))
        _tri_inv = _unit_lower_inverse
    row, col = _iota((c, c), 0), _iota((c, c), 1)
    incl, strict = row >= col, row > col
    decay = jnp.where(incl, jnp.exp(jnp.where(incl, g1 - g2, 0.0)), 0.0)
    eg = jnp.exp(g1)
    kb = k * beta
    a = _mm_nt(kb, k) * jnp.where(strict, decay, 0.0)
    t_inv = _tri_inv(a)
    u = _mm_nn(t_inv, v * beta)
    w = _mm_nn(t_inv, kb * eg)
    attn = _mm_nt(q, k) * decay
    v_new = u - _mm_nn(w, state)
    o = _mm_nn(q * eg, state) + _mm_nn(attn, v_new)
    new_state = state * jnp.exp(gl) + _mm_tn(k * jnp.exp(gl - g1), v_new)
    return o, new_state


def _gdn_fwd(q, k, v, beta, g, gl, grow, cpb=4):
    t = q.shape[1]
    rows = cpb * GDN_CHUNK

    def body(q_ref, k_ref, v_ref, b_ref, g_ref, gl_ref, grow_ref, o_ref, st_ref, state):
        @pl.when(pl.program_id(1) == 0)
        def _():
            state[...] = jnp.zeros_like(state)

        s = state[...]
        for cc in range(cpb):
            sl = slice(cc * GDN_CHUNK, (cc + 1) * GDN_CHUNK)
            st_ref[0, cc] = s
            g2 = jnp.broadcast_to(grow_ref[0, cc], (GDN_CHUNK, GDN_CHUNK))
            o, s = _gdn_chunk(q_ref[0, sl, :], k_ref[0, sl, :], v_ref[0, sl, :], b_ref[0, sl, :], g_ref[0, sl, :], g2,
                              gl_ref[0, sl, :], s, differentiable=False)
            o_ref[0, sl, :] = o
        state[...] = s

    hm = pl.BlockSpec((1, rows, HEAD_DIM), lambda h, i: (h, i, 0))
    return pl.pallas_call(
        body, name="gdn_fwd", grid=(HEADS, t // rows),
        in_specs=[hm] * 6 + [pl.BlockSpec((1, cpb, 1, GDN_CHUNK), lambda h, i: (h, i, 0, 0))],
        out_specs=[hm, pl.BlockSpec((1, cpb, GDN_CHUNK, HEAD_DIM), lambda h, i: (h, i, 0, 0))],
        out_shape=[jax.ShapeDtypeStruct((HEADS, t, HEAD_DIM), F32),
                   jax.ShapeDtypeStruct((HEADS, t // GDN_CHUNK, GDN_CHUNK, HEAD_DIM), F32)],
        scratch_shapes=[pltpu.VMEM((GDN_CHUNK, HEAD_DIM), F32)],
        compiler_params=_params(("parallel", "arbitrary")),
    )(q, k, v, beta, g, gl, grow)


def _gdn_bwd(q, k, v, beta, g, gl, grow, states, do, cpb=4):
    t = q.shape[1]
    rows = cpb * GDN_CHUNK
    nsteps = t // rows

    def body(q_ref, k_ref, v_ref, b_ref, g_ref, gl_ref, grow_ref, st_ref, do_ref,
             dq_ref, dk_ref, dv_ref, db_ref, dg_ref, dgl_ref, dgrow_ref, dstate):
        @pl.when(pl.program_id(1) == 0)
        def _():
            dstate[...] = jnp.zeros_like(dstate)

        ds = dstate[...]
        for cc in reversed(range(cpb)):
            sl = slice(cc * GDN_CHUNK, (cc + 1) * GDN_CHUNK)
            g2 = jnp.broadcast_to(grow_ref[0, cc], (GDN_CHUNK, GDN_CHUNK))
            _, vjp = jax.vjp(_gdn_chunk, q_ref[0, sl, :], k_ref[0, sl, :], v_ref[0, sl, :], b_ref[0, sl, :],
                             g_ref[0, sl, :], g2, gl_ref[0, sl, :], st_ref[0, cc])
            gq, gk, gv, gb, gg1, gg2, ggl, ds = vjp((do_ref[0, sl, :], ds))
            dq_ref[0, sl, :] = gq
            dk_ref[0, sl, :] = gk
            dv_ref[0, sl, :] = gv
            db_ref[0, sl, :] = gb
            dg_ref[0, sl, :] = gg1
            dgl_ref[0, sl, :] = ggl
            dgrow_ref[0, cc] = jnp.sum(gg2, axis=0, keepdims=True)
        dstate[...] = ds

    hm = pl.BlockSpec((1, rows, HEAD_DIM), lambda h, i: (h, nsteps - 1 - i, 0))
    rowspec = pl.BlockSpec((1, cpb, 1, GDN_CHUNK), lambda h, i: (h, nsteps - 1 - i, 0, 0))
    hm_shape = jax.ShapeDtypeStruct((HEADS, t, HEAD_DIM), F32)
    return pl.pallas_call(
        body, name="gdn_bwd", grid=(HEADS, nsteps),
        in_specs=[hm] * 6 + [rowspec, pl.BlockSpec((1, cpb, GDN_CHUNK, HEAD_DIM), lambda h, i: (h, nsteps - 1 - i, 0, 0)), hm],
        out_specs=[hm] * 6 + [rowspec],
        out_shape=[hm_shape] * 6 + [jax.ShapeDtypeStruct((HEADS, t // GDN_CHUNK, 1, GDN_CHUNK), F32)],
        scratch_shapes=[pltpu.VMEM((GDN_CHUNK, HEAD_DIM), F32)],
        compiler_params=_params(("parallel", "arbitrary")),
    )(q, k, v, beta, g, gl, grow, states, do)


def _gdn_out(o_hm, gdn_norm_w, proj, tm=256):
    t = proj.shape[0]

    def body(o_ref, w_ref, z_ref, oz_ref):
        z = z_ref[...]
        gate = z * _sigmoid(z)
        w = w_ref[...]
        for h in range(HEADS):
            sl = slice(h * HEAD_DIM, (h + 1) * HEAD_DIM)
            o = o_ref[h]
            r = lax.rsqrt(jnp.mean(o * o, axis=1, keepdims=True) + NORM_EPS)
            oz_ref[:, sl] = (o * r * w * gate[:, sl]).astype(oz_ref.dtype)

    tok = pl.BlockSpec((tm, WIDTH), lambda i: (i, 0))
    return pl.pallas_call(
        body, name="gdn_out", grid=(t // tm,),
        in_specs=[pl.BlockSpec((HEADS, tm, HEAD_DIM), lambda i: (0, i, 0)), pl.BlockSpec((1, HEAD_DIM), lambda i: (0, 0)),
                  pl.BlockSpec((tm, WIDTH), lambda i: (i, SEG_ZB // WIDTH))],
        out_specs=tok, out_shape=jax.ShapeDtypeStruct((t, WIDTH), MXU_DTYPE),
        compiler_params=_params(("parallel",)),
    )(o_hm, gdn_norm_w, proj)


def _gdn_out_bwd(dproj, d_oz, o_hm, gdn_norm_w, proj, tm=256):
    t = proj.shape[0]

    def body(dp_ref, doz_ref, o_ref, w_ref, z_ref, dz_ref, do_ref, dw_ref):
        z = z_ref[...]
        sg = _sigmoid(z)
        gate = z * sg
        dgate = sg * (1.0 + z * (1.0 - sg))
        w = w_ref[...]
        dw = jnp.zeros((1, HEAD_DIM), F32)
        for h in range(HEADS):
            sl = slice(h * HEAD_DIM, (h + 1) * HEAD_DIM)
            o = o_ref[h]
            g = doz_ref[:, sl]
            r = lax.rsqrt(jnp.mean(o * o, axis=1, keepdims=True) + NORM_EPS)
            on = o * r * w
            dz_ref[:, sl] = (g * on * dgate[:, sl]).astype(dz_ref.dtype)
            dn = g * gate[:, sl]
            dw += jnp.sum(dn * o * r, axis=0, keepdims=True)
            dnw = dn * w
            do_ref[h] = r * dnw - o * (r * r * r) * jnp.mean(dnw * o, axis=1, keepdims=True)

        @pl.when(pl.program_id(0) == 0)
        def _():
            dw_ref[...] = jnp.zeros_like(dw_ref)

        dw_ref[...] += jnp.concatenate([dw, jnp.zeros((7, HEAD_DIM), F32)], axis=0)

    tok = pl.BlockSpec((tm, WIDTH), lambda i: (i, 0))
    seg = pl.BlockSpec((tm, WIDTH), lambda i: (i, SEG_ZB // WIDTH))
    hm = pl.BlockSpec((HEADS, tm, HEAD_DIM), lambda i: (0, i, 0))
    return pl.pallas_call(
        body, name="gdn_out_bwd", grid=(t // tm,),
        in_specs=[pl.BlockSpec(memory_space=pl.ANY), tok, hm, pl.BlockSpec((1, HEAD_DIM), lambda i: (0, 0)), seg],
        out_specs=[seg, hm, pl.BlockSpec((8, HEAD_DIM), lambda i: (0, 0))],
        out_shape=[jax.ShapeDtypeStruct((t, PACKED_WIDTH), MXU_DTYPE), jax.ShapeDtypeStruct((HEADS, t, HEAD_DIM), F32),
                   jax.ShapeDtypeStruct((8, HEAD_DIM), F32)],
        input_output_aliases={0: 0},
        compiler_params=_params(("arbitrary",)),
    )(dproj, d_oz, o_hm, gdn_norm_w, proj)


def _merge(y_a, y_b, proj, tm=256):
    t = proj.shape[0]

    def body(ya_ref, yb_ref, ga_ref, gb_ref, m_ref):
        m_ref[...] = (_sigmoid(ga_ref[...]) * ya_ref[...] + _sigmoid(gb_ref[...]) * yb_ref[...]).astype(m_ref.dtype)

    half = pl.BlockSpec((tm, WIDTH), lambda i, c: (i, c))
    return pl.pallas_call(
        body, name="merge", grid=(t // tm, 2),
        in_specs=[half, half, pl.BlockSpec((tm, WIDTH), lambda i, c: (i, SEG_GA // WIDTH + c)),
                  pl.BlockSpec((tm, WIDTH), lambda i, c: (i, SEG_GB // WIDTH + c))],
        out_specs=half, out_shape=jax.ShapeDtypeStruct((t, D_MODEL), MXU_DTYPE),
        compiler_params=_params(("parallel", "parallel")),
    )(y_a, y_b, proj, proj)


def _merge_bwd(dproj, d_m, y, proj, seg, name, tm=256):
    t = proj.shape[0]

    def body(*refs):
        dm_ref, y_ref, g_ref, dg_ref, dy_ref = refs[-5:]
        dm = dm_ref[...]
        s = _sigmoid(g_ref[...])
        dy_ref[...] = (dm * s).astype(dy_ref.dtype)
        dg_ref[...] = (dm * y_ref[...] * s * (1.0 - s)).astype(dg_ref.dtype)

    half = pl.BlockSpec((tm, WIDTH), lambda i, c: (i, c))
    gate = pl.BlockSpec((tm, WIDTH), lambda i, c: (i, seg // WIDTH + c))
    specs, args, aliases = [half, half, gate], [d_m, y, proj], {}
    if dproj is not None:
        specs, args, aliases = [pl.BlockSpec(memory_space=pl.ANY)] + specs, [dproj] + args, {0: 0}
    return pl.pallas_call(
        body, name=name, grid=(t // tm, 2), in_specs=specs, out_specs=[gate, half],
        out_shape=[jax.ShapeDtypeStruct((t, PACKED_WIDTH), MXU_DTYPE), jax.ShapeDtypeStruct((t, D_MODEL), MXU_DTYPE)],
        input_output_aliases=aliases,
        compiler_params=_params(("parallel", "parallel")),
    )(*args)


def _tail(x, mo, final_w, target, tm=256):
    t = x.shape[0]

    def body(x_ref, mo_ref, w_ref, t_ref, dxm_ref, dx_ref, loss_ref, dw_ref):
        x2 = x_ref[...] + mo_ref[...]
        w = w_ref[...]
        r = lax.rsqrt(jnp.mean(x2 * x2, axis=-1, keepdims=True) + NORM_EPS)
        xn = x2 * r
        err = xn * w - t_ref[...]
        dy = err * (1.0 / D_MODEL)
        dyw = dy * w
        dx2 = r * dyw - x2 * (r * r * r) * jnp.mean(dyw * x2, axis=-1, keepdims=True)
        dx_ref[...] = dx2
        dxm_ref[...] = dx2.astype(dxm_ref.dtype)
        loss = 0.5 * jnp.sum(jnp.sum(err * err, axis=-1, keepdims=True) * (1.0 / D_MODEL), axis=0, keepdims=True)
        onehot = jnp.where((_iota((8, 128), 0) == 0) & (_iota((8, 128), 1) == 0), 1.0, 0.0)

        @pl.when(pl.program_id(0) == 0)
        def _():
            loss_ref[...] = jnp.zeros_like(loss_ref)
            dw_ref[...] = jnp.zeros_like(dw_ref)

        loss_ref[...] += loss * onehot
        dw_ref[...] += jnp.where(_iota((8, D_MODEL), 0) == 0, jnp.sum(dy * xn, axis=0, keepdims=True), 0.0)

    tok = pl.BlockSpec((tm, D_MODEL), lambda i: (i, 0))
    return pl.pallas_call(
        body, name="tail", grid=(t // tm,),
        in_specs=[tok, tok, pl.BlockSpec((1, D_MODEL), lambda i: (0, 0)), tok],
        out_specs=[tok, tok, pl.BlockSpec((8, 128), lambda i: (0, 0)), pl.BlockSpec((8, D_MODEL), lambda i: (0, 0))],
        out_shape=[jax.ShapeDtypeStruct((t, D_MODEL), MXU_DTYPE), jax.ShapeDtypeStruct((t, D_MODEL), F32),
                   jax.ShapeDtypeStruct((8, 128), F32), jax.ShapeDtypeStruct((8, D_MODEL), F32)],
        compiler_params=_params(("arbitrary",)),
    )(x, mo, final_w, target)


def _norm_bwd(x, norm_w, dh, dx2, tm=256):
    t = x.shape[0]

    def body(x_ref, w_ref, dh_ref, dx2_ref, dx_ref, dw_ref):
        xf, w, dh_ = x_ref[...], w_ref[...], dh_ref[...]
        r = lax.rsqrt(jnp.mean(xf * xf, axis=-1, keepdims=True) + NORM_EPS)
        dhw = dh_ * w
        dx_ref[...] = dx2_ref[...] + r * dhw - xf * (r * r * r) * jnp.mean(dhw * xf, axis=-1, keepdims=True)

        @pl.when(pl.program_id(0) == 0)
        def _():
            dw_ref[...] = jnp.zeros_like(dw_ref)

        dw_ref[...] += jnp.where(_iota((8, D_MODEL), 0) == 0, jnp.sum(dh_ * xf * r, axis=0, keepdims=True), 0.0)

    tok = pl.BlockSpec((tm, D_MODEL), lambda i: (i, 0))
    return pl.pallas_call(
        body, name="norm_bwd", grid=(t // tm,),
        in_specs=[tok, pl.BlockSpec((1, D_MODEL), lambda i: (0, 0)), tok, tok],
        out_specs=[tok, pl.BlockSpec((8, D_MODEL), lambda i: (0, 0))],
        out_shape=[jax.ShapeDtypeStruct((t, D_MODEL), F32), jax.ShapeDtypeStruct((8, D_MODEL), F32)],
        compiler_params=_params(("arbitrary",)),
    )(x, norm_w, dh, dx2)


def _to_strided(a, d):
    return a if d == 1 else a.reshape(a.shape[0] // d, d * a.shape[1])


def _from_strided(a, d):
    return a if d == 1 else a.reshape(a.shape[0] * d, a.shape[1] // d)


def _local_step(x, target, norm_w, wp, conv_w, a_log, dt_bias, gdn_norm_w, w_up_a, w_up_b, w_out, final_w):
    t = x.shape[0]
    tables = _rope_tables(t)
    a_log = jnp.pad(a_log, ((0, 0), (HEADS, 128 - 2 * HEADS)))
    dt_bias = jnp.pad(dt_bias, ((0, 0), (HEADS, 128 - 2 * HEADS)))

    proj, h = _norm_proj(x, norm_w, wp)
    qkv = _rope_fwd(proj, tables)
    att_in, outs, lses = [], [], []
    for gi, d in enumerate(DILATIONS):
        a_in = qkv if d == 1 else _to_strided(qkv[:, gi * 3 * WIDTH:(gi + 1) * 3 * WIDTH], d)
        o_g, lse_g = _att_fwd(a_in, d, f"att_fwd{gi}")
        att_in.append(a_in)
        outs.append(_from_strided(o_g, d))
        lses.append(_from_strided(lse_g, d))
    oz_a, o_a, lse = _att_merge(outs, lses, proj)
    conv = _conv_fwd(proj, conv_w)
    gq, gk, gv, gb, gg, ggl, grow = _gdn_prep(conv, proj, a_log, dt_bias)
    o_b, states = _gdn_fwd(gq, gk, gv, gb, gg, ggl, grow)
    oz_b = _gdn_out(o_b, gdn_norm_w, proj)
    y_a = _matmul(oz_a, w_up_a, "nn", "up_a")
    y_b = _matmul(oz_b, w_up_b, "nn", "up_b")
    merged = _merge(y_a, y_b, proj)
    mo = _matmul(merged, w_out, "nn", "out_proj")
    dx2_m, dx2, loss_blk, d_final = _tail(x, mo, final_w, target)

    d_wout = _matmul(merged, dx2_m, "tn", "d_w_out")
    d_m = _matmul(dx2_m, w_out, "nt", "d_merged")
    dproj, dy_a = _merge_bwd(None, d_m, y_a, proj, SEG_GA, "merge_bwd_a")
    dproj, dy_b = _merge_bwd(dproj, d_m, y_b, proj, SEG_GB, "merge_bwd_b")
    d_wua = _matmul(oz_a, dy_a, "tn", "d_w_up_a")
    d_wub = _matmul(oz_b, dy_b, "tn", "d_w_up_b")
    d_oz_a = _matmul(dy_a, w_up_a, "nt", "d_oz_a")
    d_oz_b = _matmul(dy_b, w_up_b, "nt", "d_oz_b")
    dproj, do_a, delta = _att_merge_bwd(dproj, d_oz_a, o_a, proj)
    for gi, d in enumerate(DILATIONS):
        dqkv = _att_bwd(att_in[gi], _to_strided(do_a, d), _to_strided(lse, d), _to_strided(delta, d), d, f"att_bwd{gi}")
        dproj = _rope_bwd(dproj, _from_strided(dqkv, d), gi, tables)
    dproj, do_b, d_gnw = _gdn_out_bwd(dproj, d_oz_b, o_b, gdn_norm_w, proj)
    dgq, dgk, dgv, dgb, dgg, dggl, dgrow = _gdn_bwd(gq, gk, gv, gb, gg, ggl, grow, states, do_b)
    dproj, dconv, d_small = _gdn_prep_bwd(dproj, conv, proj, a_log, dt_bias, dgq, dgk, dgv, dgb, dgg, dggl, dgrow)
    dproj, d_convw = _conv_bwd(dproj, dconv, proj, conv_w)
    d_wp = _matmul(h, dproj, "tn", "d_w_in")
    dh = _matmul(dproj, wp, "nt", "d_h", tk=512)
    grad_x, d_norm = _norm_bwd(x, norm_w, dh, dx2)
    return dict(loss=loss_blk, grad_x=grad_x, norm_w=d_norm[0:1], w_in=d_wp, conv_w=d_convw[0:GDN_CONV],
                a_log=d_small[0:1, HEADS:2 * HEADS], dt_bias=d_small[1:2, HEADS:2 * HEADS], gdn_norm_w=d_gnw[0:1],
                w_up_a=d_wua, w_up_b=d_wub,
                w_out=d_wout, final_norm_w=d_final[0:1])


SHARDS = 4
W_IN_SHARD = IN_WIDTH // SHARDS
ROWS_W_IN = D_MODEL * W_IN_SHARD // 128
ROWS_UP = WIDTH * (D_MODEL // SHARDS) // 128
ROWS_OUT = (D_MODEL // SHARDS) * D_MODEL // 128
CONV_SHARD = 3 * WIDTH // SHARDS
ROWS_CONV = 16
SLAB_ROWS = ROWS_W_IN + 2 * ROWS_UP + ROWS_OUT + 2 * ROWS_CONV
HALF_ROWS = SLAB_ROWS // 2
MESH = pl.DeviceIdType.MESH
ANY = pl.BlockSpec(memory_space=pl.ANY)


def _pad_rows(a, rows):
    return jnp.pad(a, ((0, rows - a.shape[0]), (0, 0)))


def _pack_slab(w_in, w_up_a, w_up_b, w_out, conv, conv_lo):
    parts = [w_in.reshape(ROWS_W_IN, 128), w_up_a.reshape(ROWS_UP, 128), w_up_b.reshape(ROWS_UP, 128),
             w_out.reshape(ROWS_OUT, 128), _pad_rows(conv.reshape(-1, 128), ROWS_CONV), _pad_rows(conv_lo.reshape(-1, 128), ROWS_CONV)]
    return jnp.concatenate(parts, axis=0)


def _unpack_slab(slab):
    r0 = 0
    out = []
    for rows, shape in ((ROWS_W_IN, (D_MODEL, W_IN_SHARD)), (ROWS_UP, (WIDTH, D_MODEL // SHARDS)), (ROWS_UP, (WIDTH, D_MODEL // SHARDS)),
                        (ROWS_OUT, (D_MODEL // SHARDS, D_MODEL)), (ROWS_CONV, None), (ROWS_CONV, None)):
        part = slab[r0:r0 + rows]
        out.append(part[:GDN_CONV * CONV_SHARD // 128].reshape(GDN_CONV, CONV_SHARD) if shape is None else part.reshape(shape))
        r0 += rows
    return out


def _mesh_position():
    x, y, c = lax.axis_index("x"), lax.axis_index("y"), lax.axis_index("c")
    return x, y, c, [(1 - x, y), (x, 1 - y), (1 - x, 1 - y)]


def _gather_weights(slab):
    def body(slab_ref, out_ref, send_sems, recv_sems, local_sem):
        x, y, c, chips = _mesh_position()
        mine = pltpu.make_async_copy(slab_ref, out_ref.at[2 * x + y], local_sem)
        mine.start()

        def half(chip, which):
            return out_ref.at[2 * chip[0] + chip[1], pl.ds(which * HALF_ROWS, HALF_ROWS), :]

        def copy(k, src, dst, to):
            return pltpu.make_async_remote_copy(src_ref=src, dst_ref=dst, send_sem=send_sems.at[k], recv_sem=recv_sems.at[k],
                                                device_id=to, device_id_type=MESH)

        first = [copy(j, slab_ref.at[pl.ds(c * HALF_ROWS, HALF_ROWS), :], half((x, y), c), (*chip, c)) for j, chip in enumerate(chips)]
        for cp in first:
            cp.start()
        passed = [copy(3 + j, half(chip, c), half(chip, c), (x, y, 1 - c)) for j, chip in enumerate(chips)]
        for j, chip in enumerate(chips):
            copy(j, half(chip, c), half(chip, c), (x, y, c)).wait_recv()
            passed[j].start()
        for j, chip in enumerate(chips):
            copy(3 + j, half(chip, 1 - c), half(chip, 1 - c), (x, y, c)).wait_recv()
        for cp in first + passed:
            cp.wait_send()
        mine.wait()

    return pl.pallas_call(
        body, name="gather_weights", in_specs=[ANY], out_specs=ANY,
        out_shape=jax.ShapeDtypeStruct((SHARDS, SLAB_ROWS, 128), slab.dtype),
        scratch_shapes=[pltpu.SemaphoreType.DMA((6,)), pltpu.SemaphoreType.DMA((6,)), pltpu.SemaphoreType.DMA],    )(slab)


def _exchange_halves(grads):
    def body(g_ref, out_ref, send_sems, recv_sems):
        x, y, c, _ = _mesh_position()
        copies = [pltpu.make_async_remote_copy(src_ref=g_ref.at[s, pl.ds((1 - c) * HALF_ROWS, HALF_ROWS), :], dst_ref=out_ref.at[s],
                                               send_sem=send_sems.at[s], recv_sem=recv_sems.at[s],
                                               device_id=(x, y, 1 - c), device_id_type=MESH) for s in range(SHARDS)]
        for cp in copies:
            cp.start()
        for cp in copies:
            cp.wait()

    return pl.pallas_call(
        body, name="exchange_halves", in_specs=[ANY], out_specs=ANY,
        out_shape=jax.ShapeDtypeStruct((SHARDS, HALF_ROWS, 128), F32),
        scratch_shapes=[pltpu.SemaphoreType.DMA((SHARDS,)), pltpu.SemaphoreType.DMA((SHARDS,))],    )(grads)


SUM_ROWS = HALF_ROWS // 2


def _pair_sum(grads, recv):
    nblk = HALF_ROWS // SUM_ROWS

    def body(c_ref, g_ref, r_ref, o_ref):
        o_ref[...] = (g_ref[...] + r_ref[...]).astype(o_ref.dtype)

    spec = pl.BlockSpec((1, SUM_ROWS, 128), lambda s, i, c_ref: (s, i, 0))
    return pl.pallas_call(
        body, name="pair_sum",
        grid_spec=pltpu.PrefetchScalarGridSpec(
            num_scalar_prefetch=1, grid=(SHARDS, nblk),
            in_specs=[pl.BlockSpec((1, SUM_ROWS, 128), lambda s, i, c_ref: (s, c_ref[0] * nblk + i, 0)), spec],
            out_specs=spec),
        out_shape=jax.ShapeDtypeStruct((SHARDS, HALF_ROWS, 128), MXU_DTYPE),
        compiler_params=_params(("parallel", "parallel")),
    )(lax.axis_index("c").astype(jnp.int32).reshape(1), grads, recv)


def _scatter_pairs(pairs):
    def body(p_ref, out_ref, send_sems, recv_sems):
        x, y, c, chips = _mesh_position()
        copies = [pltpu.make_async_remote_copy(src_ref=p_ref.at[2 * chip[0] + chip[1]], dst_ref=out_ref.at[j],
                                               send_sem=send_sems.at[j], recv_sem=recv_sems.at[j],
                                               device_id=(*chip, c), device_id_type=MESH) for j, chip in enumerate(chips)]
        for cp in copies:
            cp.start()
        for cp in copies:
            cp.wait()

    return pl.pallas_call(
        body, name="scatter_pairs", in_specs=[ANY], out_specs=ANY,
        out_shape=jax.ShapeDtypeStruct((3, HALF_ROWS, 128), pairs.dtype),
        scratch_shapes=[pltpu.SemaphoreType.DMA((3,)), pltpu.SemaphoreType.DMA((3,))],    )(pairs)


def _chip_sum(pairs, recv):
    def body(own_ref, p_ref, r_ref, o_ref):
        o_ref[...] = ((p_ref[0].astype(F32) + r_ref[0].astype(F32)) + r_ref[1].astype(F32)) + r_ref[2].astype(F32)

    own = (2 * lax.axis_index("x") + lax.axis_index("y")).astype(jnp.int32).reshape(1)
    return pl.pallas_call(
        body, name="chip_sum",
        grid_spec=pltpu.PrefetchScalarGridSpec(
            num_scalar_prefetch=1, grid=(HALF_ROWS // SUM_ROWS,),
            in_specs=[pl.BlockSpec((1, SUM_ROWS, 128), lambda i, own_ref: (own_ref[0], i, 0)),
                      pl.BlockSpec((3, SUM_ROWS, 128), lambda i, own_ref: (0, i, 0))],
            out_specs=pl.BlockSpec((SUM_ROWS, 128), lambda i, own_ref: (i, 0))),
        out_shape=jax.ShapeDtypeStruct((HALF_ROWS, 128), F32),
        compiler_params=_params(("parallel",)),
    )(own, pairs, recv)


def _share_total(total_half):
    def body(t_ref, out_ref, send_sem, recv_sem, local_sem):
        x, y, c, _ = _mesh_position()
        mine = pltpu.make_async_copy(t_ref, out_ref.at[pl.ds(c * HALF_ROWS, HALF_ROWS), :], local_sem)
        mine.start()
        cp = pltpu.make_async_remote_copy(src_ref=t_ref, dst_ref=out_ref.at[pl.ds(c * HALF_ROWS, HALF_ROWS), :],
                                          send_sem=send_sem, recv_sem=recv_sem, device_id=(x, y, 1 - c), device_id_type=MESH)
        cp.start()
        other = out_ref.at[pl.ds((1 - c) * HALF_ROWS, HALF_ROWS), :]
        pltpu.make_async_remote_copy(src_ref=other, dst_ref=other, send_sem=send_sem, recv_sem=recv_sem,
                                     device_id=(x, y, c), device_id_type=MESH).wait_recv()
        cp.wait_send()
        mine.wait()

    return pl.pallas_call(
        body, name="share_total", in_specs=[ANY], out_specs=ANY,
        out_shape=jax.ShapeDtypeStruct((SLAB_ROWS, 128), F32),
        scratch_shapes=[pltpu.SemaphoreType.DMA, pltpu.SemaphoreType.DMA, pltpu.SemaphoreType.DMA],    )(total_half)


def _allreduce_small(block):
    def body(b_ref, out_ref, gath, send_sems, recv_sems):
        x, y, c, _ = _mesh_position()
        me = 4 * x + 2 * y + c
        gath[me] = b_ref[...]
        copies = []
        for k in range(1, 8):
            peer = (x ^ (k >> 2), y ^ ((k >> 1) & 1), c ^ (k & 1))
            copies.append(pltpu.make_async_remote_copy(src_ref=b_ref, dst_ref=gath.at[me], send_sem=send_sems.at[k - 1],
                                                       recv_sem=recv_sems.at[k - 1], device_id=peer, device_id_type=MESH))
        for cp in copies:
            cp.start()
        for k in range(1, 8):
            src = 4 * (x ^ (k >> 2)) + 2 * (y ^ ((k >> 1) & 1)) + (c ^ (k & 1))
            pltpu.make_async_remote_copy(src_ref=b_ref, dst_ref=gath.at[src], send_sem=send_sems.at[k - 1],
                                         recv_sem=recv_sems.at[k - 1], device_id=(x, y, c), device_id_type=MESH).wait_recv()
        for cp in copies:
            cp.wait_send()
        acc = gath[0]
        for d in range(1, 8):
            acc = acc + gath[d]
        out_ref[...] = acc

    vm = pl.BlockSpec(memory_space=pltpu.VMEM)
    return pl.pallas_call(
        body, name="allreduce_small", in_specs=[vm], out_specs=vm,
        out_shape=jax.ShapeDtypeStruct((8, D_MODEL), F32),
        scratch_shapes=[pltpu.VMEM((8, 8, D_MODEL), F32), pltpu.SemaphoreType.DMA((7,)), pltpu.SemaphoreType.DMA((7,))],
    )(block)


def _adamw(w, g, m, v, name):
    rows, cols = w.shape
    tr = 128 if rows % 128 == 0 else rows

    def body(w_ref, g_ref, m_ref, v_ref, d_ref, nm_ref, nv_ref):
        gv = g_ref[...]
        nm = ADAM_B1 * m_ref[...] + (1.0 - ADAM_B1) * gv
        nv = ADAM_B2 * v_ref[...] + (1.0 - ADAM_B2) * (gv * gv)
        m_hat = nm / (1.0 - ADAM_B1 ** ADAM_STEP)
        v_hat = nv / (1.0 - ADAM_B2 ** ADAM_STEP)
        d_ref[...] = -ADAM_LR * (m_hat / (jnp.sqrt(v_hat) + ADAM_EPS) + ADAM_WD * w_ref[...])
        nm_ref[...] = nm
        nv_ref[...] = nv

    spec = pl.BlockSpec((tr, cols), lambda i: (i, 0))
    shape = jax.ShapeDtypeStruct((rows, cols), F32)
    return pl.pallas_call(
        body, name=name, grid=(rows // tr,), in_specs=[spec] * 4, out_specs=[spec] * 3, out_shape=[shape] * 3,
        compiler_params=_params(("parallel",)),
    )(w, g, m, v)


def _pack_w_in(w):
    return jnp.concatenate([w[:, :BA_END], jnp.zeros((D_MODEL, SEG_GA - BA_END), w.dtype), w[:, BA_END:]], axis=1)


def kernel(x, norm_w, w_in, conv_w, a_log, dt_bias, gdn_norm_w, w_up_a, w_up_b, w_out, final_norm_w, loss_target, m_norm_w, m_w_in, m_conv_w, m_a_log, m_dt_bias, m_gdn_norm_w, m_w_up_a, m_w_up_b, m_w_out, m_final_norm_w, v_norm_w, v_w_in, v_conv_w, v_a_log, v_dt_bias, v_gdn_norm_w, v_w_up_a, v_w_up_b, v_w_out, v_final_norm_w):
    conv_hi = conv_w[0].astype(MXU_DTYPE)
    conv_lo = (conv_w[0] - conv_hi.astype(F32)).astype(MXU_DTYPE)
    slab = _pack_slab(w_in[0].astype(MXU_DTYPE), w_up_a[0].astype(MXU_DTYPE), w_up_b[0].astype(MXU_DTYPE),
                      w_out[0].astype(MXU_DTYPE), conv_hi, conv_lo)
    slabs = _gather_weights(slab)
    parts = [_unpack_slab(slabs[s]) for s in range(SHARDS)]
    w_in_full = jnp.concatenate([p[0] for p in parts], axis=1)
    w_up_a_full = jnp.concatenate([p[1] for p in parts], axis=1)
    w_up_b_full = jnp.concatenate([p[2] for p in parts], axis=1)
    w_out_full = jnp.concatenate([p[3] for p in parts], axis=0)
    conv_full = jnp.concatenate([p[4].astype(F32) + p[5].astype(F32) for p in parts], axis=1)

    g = _local_step(x[0], loss_target[0], norm_w, _pack_w_in(w_in_full), conv_full, a_log, dt_bias, gdn_norm_w,
                    w_up_a_full, w_up_b_full, w_out_full, final_norm_w[None])

    d_w_in = jnp.concatenate([g["w_in"][:, :BA_END], g["w_in"][:, SEG_GA:]], axis=1)
    zero_conv = jnp.zeros((GDN_CONV, CONV_SHARD), F32)
    grads = jnp.stack([
        _pack_slab(d_w_in[:, s * W_IN_SHARD:(s + 1) * W_IN_SHARD], g["w_up_a"][:, s * 256:(s + 1) * 256],
                   g["w_up_b"][:, s * 256:(s + 1) * 256], g["w_out"][s * 256:(s + 1) * 256],
                   g["conv_w"][:, s * CONV_SHARD:(s + 1) * CONV_SHARD], zero_conv) for s in range(SHARDS)])
    from_sibling = _exchange_halves(grads)
    pairs = _pair_sum(grads, from_sibling)
    from_chips = _scatter_pairs(pairs)
    total = _share_total(_chip_sum(pairs, from_chips))
    g_w_in, g_w_up_a, g_w_up_b, g_w_out, g_conv, _ = _unpack_slab(total)

    row2 = jnp.concatenate([g["gdn_norm_w"], g["a_log"], g["dt_bias"], g["loss"][0:1, 0:1],
                            jnp.zeros((1, D_MODEL - HEAD_DIM - 2 * HEADS - 1), F32)], axis=1)
    small = _allreduce_small(jnp.concatenate([g["norm_w"], g["final_norm_w"], row2, jnp.zeros((5, D_MODEL), F32)], axis=0))
    g_norm, g_final = small[0:1], small[1]
    g_gnw, g_alog, g_dt = small[2:3, 0:HEAD_DIM], small[2:3, HEAD_DIM:HEAD_DIM + HEADS], small[2:3, HEAD_DIM + HEADS:HEAD_DIM + 2 * HEADS]
    loss = small[2, HEAD_DIM + 2 * HEADS]

    names = ["norm_w", "w_in", "conv_w", "a_log", "dt_bias", "gdn_norm_w", "w_up_a", "w_up_b", "w_out", "final_norm_w"]
    weights = dict(zip(names, (norm_w, w_in, conv_w, a_log, dt_bias, gdn_norm_w, w_up_a, w_up_b, w_out, final_norm_w)))
    ms = dict(zip(names, (m_norm_w, m_w_in, m_conv_w, m_a_log, m_dt_bias, m_gdn_norm_w, m_w_up_a, m_w_up_b, m_w_out, m_final_norm_w)))
    vs = dict(zip(names, (v_norm_w, v_w_in, v_conv_w, v_a_log, v_dt_bias, v_gdn_norm_w, v_w_up_a, v_w_up_b, v_w_out, v_final_norm_w)))
    grads2d = dict(norm_w=g_norm, w_in=g_w_in, conv_w=g_conv, a_log=g_alog, dt_bias=g_dt, gdn_norm_w=g_gnw,
                   w_up_a=g_w_up_a, w_up_b=g_w_up_b, w_out=g_w_out, final_norm_w=g_final[None])
    grad_out, delta, new_m, new_v = [], [], [], []
    for n in names:
        shape = weights[n].shape
        two_d = grads2d[n].shape
        d, nm, nv = _adamw(weights[n].reshape(two_d), grads2d[n], ms[n].reshape(two_d), vs[n].reshape(two_d), f"adamw_{n}")
        grad_out.append(grads2d[n].reshape(shape))
        delta.append(d.reshape(shape))
        new_m.append(nm.reshape(shape))
        new_v.append(nv.reshape(shape))
    return (loss, g["grad_x"][None], *grad_out, *delta, *new_m, *new_v)
```

```python
import functools

import jax
import jax.numpy as jnp
from jax import lax
from jax.experimental import pallas as pl
from jax.experimental.pallas import tpu as pltpu

F32 = jnp.float32
MXU_DTYPE = jnp.bfloat16
HIGHEST = lax.Precision.HIGHEST

D_MODEL = 1024
HEADS = 8
HEAD_DIM = 64
WIDTH = HEADS * HEAD_DIM
NORM_EPS = 1e-6
ROPE_THETA = 10000.0
ATT_BLOCK = 128
DILATIONS = (1, 4, 16)
GDN_CHUNK = 64
GDN_CONV = 4
IN_WIDTH = 9232
SEG_A, SEG_ZA, SEG_B, SEG_ZB, SEG_BA, SEG_GA, SEG_GB, PACKED_WIDTH = 0, 4608, 5120, 6656, 7168, 7680, 8704, 9728
BA_END = 7184
VMEM_LIMIT = 56 * 1024 * 1024

ADAM_LR, ADAM_B1, ADAM_B2, ADAM_EPS, ADAM_WD, ADAM_STEP = 0.001, 0.9, 0.999, 1e-08, 0.01, 10

_NN = (((1,), (0,)), ((), ()))
_NT = (((1,), (1,)), ((), ()))
_TN = (((0,), (0,)), ((), ()))


def _params(sem):
    return pltpu.CompilerParams(dimension_semantics=sem, vmem_limit_bytes=VMEM_LIMIT)


def _mxu(a, b, dims):
    return lax.dot_general(a.astype(MXU_DTYPE), b.astype(MXU_DTYPE), dims, preferred_element_type=F32)


def _sigmoid(x):
    return 1.0 / (1.0 + jnp.exp(-x))


def _softplus(x):
    return jnp.maximum(x, 0.0) + jnp.log(1.0 + jnp.exp(-jnp.abs(x)))


def _iota(shape, axis):
    return lax.broadcasted_iota(jnp.int32, shape, axis)


def _matmul(a, b, mode, name, out_dtype=F32, tm=512, tn=512, tk=512):
    if mode == "nn":
        (m, k), (k2, n) = a.shape, b.shape
    elif mode == "nt":
        (m, k), (n, k2) = a.shape, b.shape
    else:
        (k, m), (k2, n) = a.shape, b.shape
    assert k == k2
    tm, tn, tk = min(tm, m), min(tn, n), min(tk, k)
    assert m % tm == 0 and n % tn == 0 and k % tk == 0
    nk = k // tk
    dims = {"nn": _NN, "nt": _NT, "tn": _TN}[mode]

    def body(a_ref, b_ref, o_ref, acc_ref):
        kk = pl.program_id(2)

        @pl.when(kk == 0)
        def _():
            acc_ref[...] = jnp.zeros_like(acc_ref)

        acc_ref[...] += _mxu(a_ref[...], b_ref[...], dims)

        @pl.when(kk == nk - 1)
        def _():
            o_ref[...] = acc_ref[...].astype(o_ref.dtype)

    a_spec = pl.BlockSpec((tk, tm), lambda i, j, kk: (kk, i)) if mode == "tn" else pl.BlockSpec((tm, tk), lambda i, j, kk: (i, kk))
    b_spec = pl.BlockSpec((tn, tk), lambda i, j, kk: (j, kk)) if mode == "nt" else pl.BlockSpec((tk, tn), lambda i, j, kk: (kk, j))
    return pl.pallas_call(
        body, name=name, grid=(m // tm, n // tn, nk), in_specs=[a_spec, b_spec],
        out_specs=pl.BlockSpec((tm, tn), lambda i, j, kk: (i, j)),
        out_shape=jax.ShapeDtypeStruct((m, n), out_dtype),
        scratch_shapes=[pltpu.VMEM((tm, tn), F32)],
        compiler_params=_params(("parallel", "parallel", "arbitrary")),
    )(a, b)


def _norm_proj(x, norm_w, wp, tm=512, tn=512):
    t = x.shape[0]

    def body(x_ref, nw_ref, w_ref, proj_ref, h_ref):
        @pl.when(pl.program_id(1) == 0)
        def _():
            xf = x_ref[...]
            r = lax.rsqrt(jnp.mean(xf * xf, axis=-1, keepdims=True) + NORM_EPS)
            h_ref[...] = (xf * r * nw_ref[...]).astype(h_ref.dtype)

        proj_ref[...] = jnp.dot(h_ref[...], w_ref[...], preferred_element_type=F32)

    return pl.pallas_call(
        body, name="norm_proj", grid=(t // tm, PACKED_WIDTH // tn),
        in_specs=[pl.BlockSpec((tm, D_MODEL), lambda i, j: (i, 0)),
                  pl.BlockSpec((1, D_MODEL), lambda i, j: (0, 0)),
                  pl.BlockSpec((D_MODEL, tn), lambda i, j: (0, j))],
        out_specs=[pl.BlockSpec((tm, tn), lambda i, j: (i, j)),
                   pl.BlockSpec((tm, D_MODEL), lambda i, j: (i, 0))],
        out_shape=[jax.ShapeDtypeStruct((t, PACKED_WIDTH), F32), jax.ShapeDtypeStruct((t, D_MODEL), MXU_DTYPE)],
        compiler_params=_params(("parallel", "arbitrary")),
    )(x, norm_w, wp)


def _rope_tables(t):
    lane = jnp.arange(128)
    inv_freq = ROPE_THETA ** (-jnp.arange(0, HEAD_DIM, 2, dtype=F32) / HEAD_DIM)
    ang = jnp.arange(t, dtype=F32)[:, None] * inv_freq[None, :]
    ang = jnp.concatenate([ang, ang, ang, ang], axis=-1)
    first_half = (lane % HEAD_DIM) < HEAD_DIM // 2
    cos, sin = jnp.cos(ang), jnp.sin(ang)
    return cos, jnp.where(first_half, -sin, 0.0), jnp.where(first_half, 0.0, sin)


def _rope_block(x, cos, sin_lo, sin_hi, sign):
    outs = []
    for c in range(8):
        xc = x[:, c * 128:(c + 1) * 128]
        rot = pltpu.roll(xc, 96, 1) * sin_lo + pltpu.roll(xc, 32, 1) * sin_hi
        outs.append(xc * cos + sign * rot)
    outs.append(x[:, 2 * WIDTH:])
    return jnp.concatenate(outs, axis=1)


def _rope_fwd(proj, tables, tm=256):
    t = proj.shape[0]

    def body(x_ref, c_ref, sl_ref, sh_ref, o_ref):
        o_ref[...] = _rope_block(x_ref[...], c_ref[...], sl_ref[...], sh_ref[...], 1.0).astype(o_ref.dtype)

    tab = pl.BlockSpec((tm, 128), lambda i, g: (i, 0))
    return pl.pallas_call(
        body, name="rope_fwd", grid=(t // tm, 3),
        in_specs=[pl.BlockSpec((tm, 3 * WIDTH), lambda i, g: (i, g)), tab, tab, tab],
        out_specs=pl.BlockSpec((tm, 3 * WIDTH), lambda i, g: (i, g)),
        out_shape=jax.ShapeDtypeStruct((t, 9 * WIDTH), MXU_DTYPE),
        compiler_params=_params(("parallel", "parallel")),
    )(proj, *tables)


def _rope_bwd(dproj, dqkv, group, tables, tm=256):
    t = dqkv.shape[0]

    def body(*refs):
        x_ref, c_ref, sl_ref, sh_ref, o_ref = refs[-5:]
        o_ref[...] = _rope_block(x_ref[...], c_ref[...], sl_ref[...], sh_ref[...], -1.0).astype(o_ref.dtype)

    tab = pl.BlockSpec((tm, 128), lambda i: (i, 0))
    specs = [pl.BlockSpec((tm, 3 * WIDTH), lambda i: (i, 0)), tab, tab, tab]
    args = [dqkv, *tables]
    aliases = {}
    if dproj is not None:
        specs = [pl.BlockSpec(memory_space=pl.ANY)] + specs
        args = [dproj] + args
        aliases = {0: 0}
    return pl.pallas_call(
        body, name=f"rope_bwd{group}", grid=(t // tm,), in_specs=specs,
        out_specs=pl.BlockSpec((tm, 3 * WIDTH), lambda i: (i, group)),
        out_shape=jax.ShapeDtypeStruct((t, PACKED_WIDTH), MXU_DTYPE),
        input_output_aliases=aliases,
        compiler_params=_params(("parallel",)),
    )(*args)


def _att_masks():
    qi = _iota((ATT_BLOCK, ATT_BLOCK), 0)
    kj = _iota((ATT_BLOCK, ATT_BLOCK), 1)
    return kj <= qi, kj >= qi


def _att_fwd(qkv, d, name):
    rows = qkv.shape[0]
    nb = rows // ATT_BLOCK
    scale = HEAD_DIM ** -0.5

    def body(q_ref, kc_ref, kp_ref, vc_ref, vp_ref, o_ref, lse_ref):
        has_prev = pl.program_id(1) > 0
        m_cur, m_prev = _att_masks()
        m_prev = m_prev & has_prev
        for h in range(HEADS):
            sl = slice(h * HEAD_DIM, (h + 1) * HEAD_DIM)
            q = q_ref[:, sl]
            s_c = jnp.where(m_cur, _mxu(q, kc_ref[:, sl], _NT) * scale, -jnp.inf)
            s_p = jnp.where(m_prev, _mxu(q, kp_ref[:, sl], _NT) * scale, -jnp.inf)
            m = jnp.maximum(jnp.max(s_c, axis=1, keepdims=True), jnp.max(s_p, axis=1, keepdims=True))
            p_c = jnp.exp(s_c - m)
            p_p = jnp.exp(s_p - m)
            den = jnp.sum(p_c, axis=1, keepdims=True) + jnp.sum(p_p, axis=1, keepdims=True)
            o = (_mxu(p_c, vc_ref[:, sl], _NN) + _mxu(p_p, vp_ref[:, sl], _NN)) / den
            o_ref[:, sl] = o
            lse_ref[:, sl] = jnp.broadcast_to(m + jnp.log(den), (ATT_BLOCK, HEAD_DIM))

    def cur(c):
        return pl.BlockSpec((ATT_BLOCK, WIDTH), lambda r, i: (i, 3 * r + c))

    def prev(c):
        return pl.BlockSpec((ATT_BLOCK, WIDTH), lambda r, i: (jnp.maximum(i - 1, 0), 3 * r + c))

    out = pl.BlockSpec((ATT_BLOCK, WIDTH), lambda r, i: (i, r))
    return pl.pallas_call(
        body, name=name, grid=(d, nb), in_specs=[cur(0), cur(1), prev(1), cur(2), prev(2)],
        out_specs=[out, out],
        out_shape=[jax.ShapeDtypeStruct((rows, d * WIDTH), F32)] * 2,
        compiler_params=_params(("parallel", "arbitrary")),
    )(qkv, qkv, qkv, qkv, qkv)


def _att_bwd(qkv, do, lse, delta, d, name):
    rows = qkv.shape[0]
    nb = rows // ATT_BLOCK
    scale = HEAD_DIM ** -0.5

    def body(q_ref, qn_ref, kc_ref, kp_ref, vc_ref, vp_ref, do_ref, don_ref, l_ref, ln_ref, dl_ref, dln_ref, o_ref):
        i = pl.program_id(1)
        m_cur, m_prev = _att_masks()
        m_p = m_prev & (i > 0)
        m_n = m_prev & (i < nb - 1)

        def probs(q, k, lse_col, mask):
            return jnp.where(mask, jnp.exp(_mxu(q, k, _NT) * scale - lse_col), 0.0)

        for h in range(HEADS):
            sl = slice(h * HEAD_DIM, (h + 1) * HEAD_DIM)
            c1 = slice(h * HEAD_DIM, h * HEAD_DIM + 1)
            q, qn, kc, kp, vc, vp = q_ref[:, sl], qn_ref[:, sl], kc_ref[:, sl], kp_ref[:, sl], vc_ref[:, sl], vp_ref[:, sl]
            do_c, do_n = do_ref[:, sl], don_ref[:, sl]
            lse_c, lse_n, dl_c, dl_n = l_ref[:, c1], ln_ref[:, c1], dl_ref[:, c1], dln_ref[:, c1]
            p = probs(q, kc, lse_c, m_cur)
            ds = p * (_mxu(do_c, vc, _NT) - dl_c)
            dq = _mxu(ds, kc, _NN)
            dk = _mxu(ds, q, _TN)
            dv = _mxu(p, do_c, _TN)
            p = probs(q, kp, lse_c, m_p)
            ds = p * (_mxu(do_c, vp, _NT) - dl_c)
            dq += _mxu(ds, kp, _NN)
            p = probs(qn, kc, lse_n, m_n)
            ds = p * (_mxu(do_n, vc, _NT) - dl_n)
            dk += _mxu(ds, qn, _TN)
            dv += _mxu(p, do_n, _TN)
            o_ref[:, sl] = dq * scale
            o_ref[:, WIDTH + h * HEAD_DIM:WIDTH + (h + 1) * HEAD_DIM] = dk * scale
            o_ref[:, 2 * WIDTH + h * HEAD_DIM:2 * WIDTH + (h + 1) * HEAD_DIM] = dv

    def qkv_spec(c, shift):
        def idx(r, i):
            return (jnp.clip(i + shift, 0, nb - 1), 3 * r + c)
        return pl.BlockSpec((ATT_BLOCK, WIDTH), idx)

    def tok_spec(shift):
        def idx(r, i):
            return (jnp.clip(i + shift, 0, nb - 1), r)
        return pl.BlockSpec((ATT_BLOCK, WIDTH), idx)

    return pl.pallas_call(
        body, name=name, grid=(d, nb),
        in_specs=[qkv_spec(0, 0), qkv_spec(0, 1), qkv_spec(1, 0), qkv_spec(1, -1), qkv_spec(2, 0), qkv_spec(2, -1),
                  tok_spec(0), tok_spec(1), tok_spec(0), tok_spec(1), tok_spec(0), tok_spec(1)],
        out_specs=pl.BlockSpec((ATT_BLOCK, 3 * WIDTH), lambda r, i: (i, r)),
        out_shape=jax.ShapeDtypeStruct((rows, d * 3 * WIDTH), F32),
        compiler_params=_params(("parallel", "arbitrary")),
    )(qkv, qkv, qkv, qkv, qkv, qkv, do, do, lse, lse, delta, delta)


def _att_merge(os_, lses, proj, tm=256):
    t = proj.shape[0]

    def body(o0, o1, o2, l0, l1, l2, z_ref, oz_ref, o_ref, lse_ref):
        a, b, c = l0[...], l1[...], l2[...]
        m = jnp.maximum(jnp.maximum(a, b), c)
        wa, wb, wc = jnp.exp(a - m), jnp.exp(b - m), jnp.exp(c - m)
        den = wa + wb + wc
        o = (wa * o0[...] + wb * o1[...] + wc * o2[...]) / den
        z = z_ref[...]
        o_ref[...] = o
        lse_ref[...] = m + jnp.log(den)
        oz_ref[...] = (o * z * _sigmoid(z)).astype(oz_ref.dtype)

    tok = pl.BlockSpec((tm, WIDTH), lambda i: (i, 0))
    return pl.pallas_call(
        body, name="att_merge", grid=(t // tm,),
        in_specs=[tok] * 6 + [pl.BlockSpec((tm, WIDTH), lambda i: (i, SEG_ZA // WIDTH))],
        out_specs=[tok, tok, tok],
        out_shape=[jax.ShapeDtypeStruct((t, WIDTH), MXU_DTYPE), jax.ShapeDtypeStruct((t, WIDTH), F32),
                   jax.ShapeDtypeStruct((t, WIDTH), F32)],
        compiler_params=_params(("parallel",)),
    )(*os_, *lses, proj)


def _att_merge_bwd(dproj, d_oz, o, proj, tm=256):
    t = proj.shape[0]

    def body(dp_ref, doz_ref, o_ref, z_ref, dz_ref, do_ref, dl_ref):
        z, ov, g = z_ref[...], o_ref[...], doz_ref[...]
        sg = _sigmoid(z)
        do = g * z * sg
        dz_ref[...] = (g * ov * sg * (1.0 + z * (1.0 - sg))).astype(dz_ref.dtype)
        do_ref[...] = do.astype(do_ref.dtype)
        prod = do * ov
        for h in range(HEADS):
            sl = slice(h * HEAD_DIM, (h + 1) * HEAD_DIM)
            dl_ref[:, sl] = jnp.broadcast_to(jnp.sum(prod[:, sl], axis=1, keepdims=True), (tm, HEAD_DIM))

    tok = pl.BlockSpec((tm, WIDTH), lambda i: (i, 0))
    seg = pl.BlockSpec((tm, WIDTH), lambda i: (i, SEG_ZA // WIDTH))
    return pl.pallas_call(
        body, name="att_merge_bwd", grid=(t // tm,),
        in_specs=[pl.BlockSpec(memory_space=pl.ANY), tok, tok, seg],
        out_specs=[seg, tok, tok],
        out_shape=[jax.ShapeDtypeStruct((t, PACKED_WIDTH), MXU_DTYPE), jax.ShapeDtypeStruct((t, WIDTH), MXU_DTYPE),
                   jax.ShapeDtypeStruct((t, WIDTH), F32)],
        input_output_aliases={0: 0},
        compiler_params=_params(("parallel",)),
    )(dproj, d_oz, o, proj)


def _shift_down(x, halo, s):
    if s == 0:
        return x
    xs = pltpu.roll(x, s, 0)
    head = jnp.where(_iota((8, x.shape[1]), 0) < s, pltpu.roll(halo, s, 0), xs[0:8])
    return jnp.concatenate([head, xs[8:]], axis=0)


def _shift_up(x, nxt, s):
    if s == 0:
        return x
    n = x.shape[0]
    xs = pltpu.roll(x, n - s, 0)
    tail = jnp.where(_iota((8, x.shape[1]), 0) >= 8 - s, pltpu.roll(nxt, 8 - s, 0), xs[n - 8:])
    return jnp.concatenate([xs[:n - 8], tail], axis=0)


def _conv_fwd(proj, conv_w, tm=256):
    t = proj.shape[0]
    cb = SEG_B // WIDTH

    def body(x_ref, halo_ref, w_ref, c_ref):
        halo = jnp.where(pl.program_id(0) > 0, halo_ref[...], 0.0)
        x = x_ref[...]
        w = w_ref[...]
        acc = jnp.zeros((tm, WIDTH), F32)
        for j in range(GDN_CONV):
            acc += _shift_down(x, halo, GDN_CONV - 1 - j) * w[j:j + 1, :]
        c_ref[...] = acc

    return pl.pallas_call(
        body, name="conv_fwd", grid=(t // tm, 3),
        in_specs=[pl.BlockSpec((tm, WIDTH), lambda i, c: (i, cb + c)),
                  pl.BlockSpec((8, WIDTH), lambda i, c: (jnp.maximum(i * (tm // 8) - 1, 0), cb + c)),
                  pl.BlockSpec((GDN_CONV, WIDTH), lambda i, c: (0, c))],
        out_specs=pl.BlockSpec((tm, WIDTH), lambda i, c: (i, c)),
        out_shape=jax.ShapeDtypeStruct((t, 3 * WIDTH), F32),
        compiler_params=_params(("parallel", "parallel")),
    )(proj, proj, conv_w)


def _conv_bwd(dproj, dc, proj, conv_w, tm=256):
    t = proj.shape[0]
    cb = SEG_B // WIDTH
    nt = t // tm

    def body(dp_ref, dc_ref, dcn_ref, x_ref, halo_ref, w_ref, dx_ref, dw_ref):
        i = pl.program_id(1)
        w = w_ref[...]
        dcn = jnp.where(i < nt - 1, dcn_ref[...], 0.0)
        dcv = dc_ref[...]
        acc = jnp.zeros((tm, WIDTH), F32)
        for j in range(GDN_CONV):
            acc += _shift_up(dcv, dcn, GDN_CONV - 1 - j) * w[j:j + 1, :]
        dx_ref[...] = acc.astype(dx_ref.dtype)
        halo = jnp.where(i > 0, halo_ref[...], 0.0)
        x = x_ref[...]
        row8 = _iota((8, WIDTH), 0)
        part = jnp.zeros((8, WIDTH), F32)
        for j in range(GDN_CONV):
            s = jnp.sum(dcv * _shift_down(x, halo, GDN_CONV - 1 - j), axis=0, keepdims=True)
            part += jnp.where(row8 == j, s, 0.0)

        @pl.when(i == 0)
        def _():
            dw_ref[...] = jnp.zeros_like(dw_ref)

        dw_ref[...] += part

    return pl.pallas_call(
        body, name="conv_bwd", grid=(3, nt),
        in_specs=[pl.BlockSpec(memory_space=pl.ANY),
                  pl.BlockSpec((tm, WIDTH), lambda c, i: (i, c)),
                  pl.BlockSpec((8, WIDTH), lambda c, i: (jnp.minimum((i + 1) * (tm // 8), t // 8 - 1), c)),
                  pl.BlockSpec((tm, WIDTH), lambda c, i: (i, cb + c)),
                  pl.BlockSpec((8, WIDTH), lambda c, i: (jnp.maximum(i * (tm // 8) - 1, 0), cb + c)),
                  pl.BlockSpec((GDN_CONV, WIDTH), lambda c, i: (0, c))],
        out_specs=[pl.BlockSpec((tm, WIDTH), lambda c, i: (i, cb + c)),
                   pl.BlockSpec((8, WIDTH), lambda c, i: (0, c))],
        out_shape=[jax.ShapeDtypeStruct((t, PACKED_WIDTH), MXU_DTYPE), jax.ShapeDtypeStruct((8, 3 * WIDTH), F32)],
        input_output_aliases={0: 0},
        compiler_params=_params(("parallel", "arbitrary")),
    )(dproj, dc, dc, proj, proj, conv_w)


def _chunk_matrices(tm):
    r, c = _iota((tm, tm), 0), _iota((tm, tm), 1)
    same = (r // GDN_CHUNK) == (c // GDN_CHUNK)
    return jnp.where(same & (c <= r), 1.0, 0.0), jnp.where(same, 1.0, 0.0)


def _gdn_gates(ba, a_log, dt_bias):
    al = ba + dt_bias
    return _sigmoid(ba), -jnp.exp(a_log) * _softplus(al), _sigmoid(al)


def _head_lane_eye():
    return jnp.where(_iota((HEADS, 128), 1) == _iota((HEADS, 128), 0) + HEADS, 1.0, 0.0)


def _gdn_prep(conv, proj, a_log, dt_bias, tm=256):
    t = proj.shape[0]
    nc = tm // GDN_CHUNK

    def body(c_ref, ba_ref, al_ref, dt_ref, q_ref, k_ref, v_ref, b_ref, g_ref, gl_ref, grow_ref):
        c = c_ref[...]
        a = c * _sigmoid(c)
        beta, g, _ = _gdn_gates(ba_ref[:, 0:128], al_ref[...], dt_ref[...])
        lmat, cmat = _chunk_matrices(tm)
        gc = jnp.dot(lmat, g, precision=HIGHEST, preferred_element_type=F32)
        gl = jnp.dot(cmat, g, precision=HIGHEST, preferred_element_type=F32)
        grow = lax.dot_general(_head_lane_eye(), gc, _NT, precision=HIGHEST, preferred_element_type=F32)
        for h in range(HEADS):
            sl = slice(h * HEAD_DIM, (h + 1) * HEAD_DIM)
            qh, kh, vh = a[:, sl], a[:, WIDTH + h * HEAD_DIM:WIDTH + (h + 1) * HEAD_DIM], a[:, 2 * WIDTH + h * HEAD_DIM:2 * WIDTH + (h + 1) * HEAD_DIM]
            rq = lax.rsqrt(jnp.sum(qh * qh, axis=1, keepdims=True) + NORM_EPS)
            rk = lax.rsqrt(jnp.sum(kh * kh, axis=1, keepdims=True) + NORM_EPS)
            q_ref[h] = qh * (rq * HEAD_DIM ** -0.5)
            k_ref[h] = kh * rk
            v_ref[h] = vh
            b_ref[h] = jnp.broadcast_to(beta[:, h:h + 1], (tm, HEAD_DIM))
            g_ref[h] = jnp.broadcast_to(gc[:, HEADS + h:HEADS + h + 1], (tm, HEAD_DIM))
            gl_ref[h] = jnp.broadcast_to(gl[:, HEADS + h:HEADS + h + 1], (tm, HEAD_DIM))
            for cc in range(nc):
                grow_ref[h, cc] = grow[h:h + 1, cc * GDN_CHUNK:(cc + 1) * GDN_CHUNK]

    hm = pl.BlockSpec((HEADS, tm, HEAD_DIM), lambda i: (0, i, 0))
    small = pl.BlockSpec((1, 128), lambda i: (0, 0))
    hm_shape = jax.ShapeDtypeStruct((HEADS, t, HEAD_DIM), F32)
    return pl.pallas_call(
        body, name="gdn_prep", grid=(t // tm,),
        in_specs=[pl.BlockSpec((tm, 3 * WIDTH), lambda i: (i, 0)),
                  pl.BlockSpec((tm, WIDTH), lambda i: (i, SEG_BA // WIDTH)), small, small],
        out_specs=[hm] * 6 + [pl.BlockSpec((HEADS, nc, 1, GDN_CHUNK), lambda i: (0, i, 0, 0))],
        out_shape=[hm_shape] * 6 + [jax.ShapeDtypeStruct((HEADS, t // GDN_CHUNK, 1, GDN_CHUNK), F32)],
        compiler_params=_params(("parallel",)),
    )(conv, proj, a_log, dt_bias)


def _gdn_prep_bwd(dproj, conv, proj, a_log, dt_bias, dq, dk, dv, db, dg, dgl, dgrow, tm=256):
    t = proj.shape[0]
    nc = tm // GDN_CHUNK

    def body(dp_ref, c_ref, ba_ref, al_ref, dt_ref, dq_ref, dk_ref, dv_ref, db_ref, dg_ref, dgl_ref, dgrow_ref,
             dba_ref, dc_ref, small_ref, row_scr):
        c = c_ref[...]
        sg = _sigmoid(c)
        a = c * sg
        beta, g, sig_al = _gdn_gates(ba_ref[:, 0:128], al_ref[...], dt_ref[...])
        lane = _iota((1, 128), 1)
        d_beta = jnp.zeros((tm, 128), F32)
        d_gc = jnp.zeros((tm, 128), F32)
        d_gl = jnp.zeros((tm, 128), F32)
        for h in range(HEADS):
            for cc in range(nc):
                row_scr[h:h + 1, cc * GDN_CHUNK:(cc + 1) * GDN_CHUNK] = dgrow_ref[h, cc]
            d_beta += jnp.sum(db_ref[h], axis=1, keepdims=True) * jnp.where(lane == h, 1.0, 0.0)
            d_gc += jnp.sum(dg_ref[h], axis=1, keepdims=True) * jnp.where(lane == HEADS + h, 1.0, 0.0)
            d_gl += jnp.sum(dgl_ref[h], axis=1, keepdims=True) * jnp.where(lane == HEADS + h, 1.0, 0.0)
            qs, ks, vs = (slice(h * HEAD_DIM, (h + 1) * HEAD_DIM), slice(WIDTH + h * HEAD_DIM, WIDTH + (h + 1) * HEAD_DIM),
                          slice(2 * WIDTH + h * HEAD_DIM, 2 * WIDTH + (h + 1) * HEAD_DIM))
            qh, kh = a[:, qs], a[:, ks]
            rq = lax.rsqrt(jnp.sum(qh * qh, axis=1, keepdims=True) + NORM_EPS)
            rk = lax.rsqrt(jnp.sum(kh * kh, axis=1, keepdims=True) + NORM_EPS)
            gq, gk = dq_ref[h] * HEAD_DIM ** -0.5, dk_ref[h]
            da_q = rq * gq - qh * (rq * rq * rq) * jnp.sum(gq * qh, axis=1, keepdims=True)
            da_k = rk * gk - kh * (rk * rk * rk) * jnp.sum(gk * kh, axis=1, keepdims=True)
            dsilu = lambda s, x: s * (1.0 + x * (1.0 - s))
            dc_ref[:, qs] = da_q * dsilu(sg[:, qs], c[:, qs])
            dc_ref[:, ks] = da_k * dsilu(sg[:, ks], c[:, ks])
            dc_ref[:, vs] = dv_ref[h] * dsilu(sg[:, vs], c[:, vs])
        d_gc += lax.dot_general(row_scr[...], _head_lane_eye(), _TN, precision=HIGHEST, preferred_element_type=F32)
        lmat, cmat = _chunk_matrices(tm)
        d_g = (lax.dot_general(lmat, d_gc, _TN, precision=HIGHEST, preferred_element_type=F32)
               + lax.dot_general(cmat, d_gl, _TN, precision=HIGHEST, preferred_element_type=F32))
        d_al = d_g * (-jnp.exp(al_ref[...])) * sig_al
        d_bl = d_beta * beta * (1.0 - beta)
        dba_ref[...] = jnp.concatenate([d_bl + d_al, jnp.zeros((tm, WIDTH - 128), F32)], axis=1).astype(dba_ref.dtype)
        row8 = _iota((8, 128), 0)
        part = (jnp.where(row8 == 0, jnp.sum(d_g * g, axis=0, keepdims=True), 0.0)
                + jnp.where(row8 == 1, jnp.sum(d_al, axis=0, keepdims=True), 0.0))

        @pl.when(pl.program_id(0) == 0)
        def _():
            small_ref[...] = jnp.zeros_like(small_ref)

        small_ref[...] += part

    hm = pl.BlockSpec((HEADS, tm, HEAD_DIM), lambda i: (0, i, 0))
    small = pl.BlockSpec((1, 128), lambda i: (0, 0))
    seg = pl.BlockSpec((tm, WIDTH), lambda i: (i, SEG_BA // WIDTH))
    return pl.pallas_call(
        body, name="gdn_prep_bwd", grid=(t // tm,),
        in_specs=[pl.BlockSpec(memory_space=pl.ANY), pl.BlockSpec((tm, 3 * WIDTH), lambda i: (i, 0)), seg, small, small]
        + [hm] * 6 + [pl.BlockSpec((HEADS, nc, 1, GDN_CHUNK), lambda i: (0, i, 0, 0))],
        out_specs=[seg, pl.BlockSpec((tm, 3 * WIDTH), lambda i: (i, 0)), pl.BlockSpec((8, 128), lambda i: (0, 0))],
        out_shape=[jax.ShapeDtypeStruct((t, PACKED_WIDTH), MXU_DTYPE), jax.ShapeDtypeStruct((t, 3 * WIDTH), F32),
                   jax.ShapeDtypeStruct((8, 128), F32)],
        scratch_shapes=[pltpu.VMEM((HEADS, tm), F32)],
        input_output_aliases={0: 0},
        compiler_params=_params(("arbitrary",)),
    )(dproj, conv, proj, a_log, dt_bias, dq, dk, dv, db, dg, dgl, dgrow)


_BNN = (((2,), (1,)), ((0,), (0,)))
_BNT = (((2,), (2,)), ((0,), (0,)))
_BTN = (((1,), (1,)), ((0,), (0,)))


@jax.custom_vjp
def _MM_NN(a, b):
    return _mxu(a, b, _BNN)


@jax.custom_vjp
def _MM_NT(a, b):
    return _mxu(a, b, _BNT)


@jax.custom_vjp
def _MM_TN(a, b):
    return _mxu(a, b, _BTN)


_MM_NN.defvjp(lambda a, b: (_mxu(a, b, _BNN), (a, b)), lambda r, g: (_mxu(g, r[1], _BNT), _mxu(r[0], g, _BTN)))
_MM_NT.defvjp(lambda a, b: (_mxu(a, b, _BNT), (a, b)), lambda r, g: (_mxu(g, r[1], _BNN), _mxu(g, r[0], _BTN)))
_MM_TN.defvjp(lambda a, b: (_mxu(a, b, _BTN), (a, b)), lambda r, g: (_mxu(r[1], g, _BNT), _mxu(r[0], g, _BNN)))


def _split(a):
    hi = a.astype(MXU_DTYPE)
    return hi, (a - hi.astype(F32)).astype(MXU_DTYPE)


def _dot3(a, b, dims):
    (ah, al), (bh, bl) = a, b
    dot = lambda u, v: lax.dot_general(u, v, dims, preferred_element_type=F32)
    return dot(ah, bh) + (dot(ah, bl) + dot(al, bh))


def _unit_lower_inverse(a):
    c = GDN_CHUNK
    eye = jnp.where(_iota((c, c), 0) == _iota((c, c), 1), 1.0, 0.0)
    x = eye - a
    p = a
    for _ in range(5):
        ps = _split(p)
        p = _dot3(ps, ps, _BNN)
        x = x + _dot3(_split(x), _split(p), _BNN)
    return x


@jax.custom_vjp
def _SAVED_INVERSE(a, t_inv):
    return t_inv


def _saved_inverse_bwd(t_inv, g):
    ts = _split(t_inv)
    return -_dot3(ts, _split(_dot3(_split(g), ts, _BNT)), _BTN), jnp.zeros_like(t_inv)


_SAVED_INVERSE.defvjp(lambda a, t_inv: (t_inv, t_inv), _saved_inverse_bwd)


def _gdn_chunk(q, k, v, beta, g1, g2, gl, state, t_inv=None):
    c = GDN_CHUNK
    if t_inv is None:
        _mm_nn, _mm_nt, _mm_tn = (functools.partial(_mxu, dims=dd) for dd in (_BNN, _BNT, _BTN))
    else:
        _mm_nn, _mm_nt, _mm_tn = _MM_NN, _MM_NT, _MM_TN
    row, col = _iota((c, c), 0), _iota((c, c), 1)
    incl, strict = row >= col, row > col
    decay = jnp.where(incl, jnp.exp(jnp.where(incl, g1 - g2, 0.0)), 0.0)
    eg = jnp.exp(g1)
    kb = k * beta
    a = _mm_nt(kb, k) * jnp.where(strict, decay, 0.0)
    inv = _unit_lower_inverse(a) if t_inv is None else _SAVED_INVERSE(a, t_inv)
    u = _mm_nn(inv, v * beta)
    w = _mm_nn(inv, kb * eg)
    attn = _mm_nt(q, k) * decay
    v_new = u - _mm_nn(w, state)
    o = _mm_nn(q * eg, state) + _mm_nn(attn, v_new)
    new_state = state * jnp.exp(gl) + _mm_tn(k * jnp.exp(gl - g1), v_new)
    return (o, new_state, inv) if t_inv is None else (o, new_state)


def _gdn_fwd(q, k, v, beta, g, gl, grow, cpb=2):
    t = q.shape[1]
    rows = cpb * GDN_CHUNK

    def body(q_ref, k_ref, v_ref, b_ref, g_ref, gl_ref, grow_ref, o_ref, st_ref, inv_ref, state):
        @pl.when(pl.program_id(0) == 0)
        def _():
            state[...] = jnp.zeros_like(state)

        s = state[...]
        for cc in range(cpb):
            sl = slice(cc * GDN_CHUNK, (cc + 1) * GDN_CHUNK)
            st_ref[:, cc] = s
            g2 = jnp.broadcast_to(grow_ref[:, cc], (HEADS, GDN_CHUNK, GDN_CHUNK))
            o, s, inv = _gdn_chunk(q_ref[:, sl, :], k_ref[:, sl, :], v_ref[:, sl, :], b_ref[:, sl, :], g_ref[:, sl, :], g2,
                                   gl_ref[:, sl, :], s)
            o_ref[:, sl, :] = o
            inv_ref[:, cc] = inv
        state[...] = s

    hm = pl.BlockSpec((HEADS, rows, HEAD_DIM), lambda i: (0, i, 0))
    per_chunk = pl.BlockSpec((HEADS, cpb, GDN_CHUNK, HEAD_DIM), lambda i: (0, i, 0, 0))
    chunk_shape = jax.ShapeDtypeStruct((HEADS, t // GDN_CHUNK, GDN_CHUNK, HEAD_DIM), F32)
    return pl.pallas_call(
        body, name="gdn_fwd", grid=(t // rows,),
        in_specs=[hm] * 6 + [pl.BlockSpec((HEADS, cpb, 1, GDN_CHUNK), lambda i: (0, i, 0, 0))],
        out_specs=[hm, per_chunk, per_chunk],
        out_shape=[jax.ShapeDtypeStruct((HEADS, t, HEAD_DIM), F32), chunk_shape, chunk_shape],
        scratch_shapes=[pltpu.VMEM((HEADS, GDN_CHUNK, HEAD_DIM), F32)],
        compiler_params=_params(("arbitrary",)),
    )(q, k, v, beta, g, gl, grow)


def _gdn_bwd(q, k, v, beta, g, gl, grow, states, invs, do, cpb=1):
    t = q.shape[1]
    rows = cpb * GDN_CHUNK
    nsteps = t // rows

    def body(q_ref, k_ref, v_ref, b_ref, g_ref, gl_ref, grow_ref, st_ref, inv_ref, do_ref,
             dq_ref, dk_ref, dv_ref, db_ref, dg_ref, dgl_ref, dgrow_ref, dstate):
        @pl.when(pl.program_id(0) == 0)
        def _():
            dstate[...] = jnp.zeros_like(dstate)

        ds = dstate[...]
        for cc in reversed(range(cpb)):
            sl = slice(cc * GDN_CHUNK, (cc + 1) * GDN_CHUNK)
            g2 = jnp.broadcast_to(grow_ref[:, cc], (HEADS, GDN_CHUNK, GDN_CHUNK))
            _, vjp = jax.vjp(_gdn_chunk, q_ref[:, sl, :], k_ref[:, sl, :], v_ref[:, sl, :], b_ref[:, sl, :],
                             g_ref[:, sl, :], g2, gl_ref[:, sl, :], st_ref[:, cc], inv_ref[:, cc])
            gq, gk, gv, gb, gg1, gg2, ggl, ds, _ = vjp((do_ref[:, sl, :], ds))
            dq_ref[:, sl, :] = gq
            dk_ref[:, sl, :] = gk
            dv_ref[:, sl, :] = gv
            db_ref[:, sl, :] = gb
            dg_ref[:, sl, :] = gg1
            dgl_ref[:, sl, :] = ggl
            dgrow_ref[:, cc] = jnp.sum(gg2, axis=1, keepdims=True)
        dstate[...] = ds

    hm = pl.BlockSpec((HEADS, rows, HEAD_DIM), lambda i: (0, nsteps - 1 - i, 0))
    rowspec = pl.BlockSpec((HEADS, cpb, 1, GDN_CHUNK), lambda i: (0, nsteps - 1 - i, 0, 0))
    per_chunk = pl.BlockSpec((HEADS, cpb, GDN_CHUNK, HEAD_DIM), lambda i: (0, nsteps - 1 - i, 0, 0))
    hm_shape = jax.ShapeDtypeStruct((HEADS, t, HEAD_DIM), F32)
    return pl.pallas_call(
        body, name="gdn_bwd", grid=(nsteps,),
        in_specs=[hm] * 6 + [rowspec, per_chunk, per_chunk, hm],
        out_specs=[hm] * 6 + [rowspec],
        out_shape=[hm_shape] * 6 + [jax.ShapeDtypeStruct((HEADS, t // GDN_CHUNK, 1, GDN_CHUNK), F32)],
        scratch_shapes=[pltpu.VMEM((HEADS, GDN_CHUNK, HEAD_DIM), F32)],
        compiler_params=_params(("arbitrary",)),
    )(q, k, v, beta, g, gl, grow, states, invs, do)


def _gdn_out(o_hm, gdn_norm_w, proj, tm=256):
    t = proj.shape[0]

    def body(o_ref, w_ref, z_ref, oz_ref):
        z = z_ref[...]
        gate = z * _sigmoid(z)
        w = w_ref[...]
        for h in range(HEADS):
            sl = slice(h * HEAD_DIM, (h + 1) * HEAD_DIM)
            o = o_ref[h]
            r = lax.rsqrt(jnp.mean(o * o, axis=1, keepdims=True) + NORM_EPS)
            oz_ref[:, sl] = (o * r * w * gate[:, sl]).astype(oz_ref.dtype)

    tok = pl.BlockSpec((tm, WIDTH), lambda i: (i, 0))
    return pl.pallas_call(
        body, name="gdn_out", grid=(t // tm,),
        in_specs=[pl.BlockSpec((HEADS, tm, HEAD_DIM), lambda i: (0, i, 0)), pl.BlockSpec((1, HEAD_DIM), lambda i: (0, 0)),
                  pl.BlockSpec((tm, WIDTH), lambda i: (i, SEG_ZB // WIDTH))],
        out_specs=tok, out_shape=jax.ShapeDtypeStruct((t, WIDTH), MXU_DTYPE),
        compiler_params=_params(("parallel",)),
    )(o_hm, gdn_norm_w, proj)


def _gdn_out_bwd(dproj, d_oz, o_hm, gdn_norm_w, proj, tm=256):
    t = proj.shape[0]

    def body(dp_ref, doz_ref, o_ref, w_ref, z_ref, dz_ref, do_ref, dw_ref):
        z = z_ref[...]
        sg = _sigmoid(z)
        gate = z * sg
        dgate = sg * (1.0 + z * (1.0 - sg))
        w = w_ref[...]
        dw = jnp.zeros((1, HEAD_DIM), F32)
        for h in range(HEADS):
            sl = slice(h * HEAD_DIM, (h + 1) * HEAD_DIM)
            o = o_ref[h]
            g = doz_ref[:, sl]
            r = lax.rsqrt(jnp.mean(o * o, axis=1, keepdims=True) + NORM_EPS)
            on = o * r * w
            dz_ref[:, sl] = (g * on * dgate[:, sl]).astype(dz_ref.dtype)
            dn = g * gate[:, sl]
            dw += jnp.sum(dn * o * r, axis=0, keepdims=True)
            dnw = dn * w
            do_ref[h] = r * dnw - o * (r * r * r) * jnp.mean(dnw * o, axis=1, keepdims=True)

        @pl.when(pl.program_id(0) == 0)
        def _():
            dw_ref[...] = jnp.zeros_like(dw_ref)

        dw_ref[...] += jnp.concatenate([dw, jnp.zeros((7, HEAD_DIM), F32)], axis=0)

    tok = pl.BlockSpec((tm, WIDTH), lambda i: (i, 0))
    seg = pl.BlockSpec((tm, WIDTH), lambda i: (i, SEG_ZB // WIDTH))
    hm = pl.BlockSpec((HEADS, tm, HEAD_DIM), lambda i: (0, i, 0))
    return pl.pallas_call(
        body, name="gdn_out_bwd", grid=(t // tm,),
        in_specs=[pl.BlockSpec(memory_space=pl.ANY), tok, hm, pl.BlockSpec((1, HEAD_DIM), lambda i: (0, 0)), seg],
        out_specs=[seg, hm, pl.BlockSpec((8, HEAD_DIM), lambda i: (0, 0))],
        out_shape=[jax.ShapeDtypeStruct((t, PACKED_WIDTH), MXU_DTYPE), jax.ShapeDtypeStruct((HEADS, t, HEAD_DIM), F32),
                   jax.ShapeDtypeStruct((8, HEAD_DIM), F32)],
        input_output_aliases={0: 0},
        compiler_params=_params(("arbitrary",)),
    )(dproj, d_oz, o_hm, gdn_norm_w, proj)


def _merge(y_a, y_b, proj, tm=256):
    t = proj.shape[0]

    def body(ya_ref, yb_ref, ga_ref, gb_ref, m_ref):
        m_ref[...] = (_sigmoid(ga_ref[...]) * ya_ref[...] + _sigmoid(gb_ref[...]) * yb_ref[...]).astype(m_ref.dtype)

    half = pl.BlockSpec((tm, WIDTH), lambda i, c: (i, c))
    return pl.pallas_call(
        body, name="merge", grid=(t // tm, 2),
        in_specs=[half, half, pl.BlockSpec((tm, WIDTH), lambda i, c: (i, SEG_GA // WIDTH + c)),
                  pl.BlockSpec((tm, WIDTH), lambda i, c: (i, SEG_GB // WIDTH + c))],
        out_specs=half, out_shape=jax.ShapeDtypeStruct((t, D_MODEL), MXU_DTYPE),
        compiler_params=_params(("parallel", "parallel")),
    )(y_a, y_b, proj, proj)


def _merge_bwd(dproj, d_m, y, proj, seg, name, tm=256):
    t = proj.shape[0]

    def body(*refs):
        dm_ref, y_ref, g_ref, dg_ref, dy_ref = refs[-5:]
        dm = dm_ref[...]
        s = _sigmoid(g_ref[...])
        dy_ref[...] = (dm * s).astype(dy_ref.dtype)
        dg_ref[...] = (dm * y_ref[...] * s * (1.0 - s)).astype(dg_ref.dtype)

    half = pl.BlockSpec((tm, WIDTH), lambda i, c: (i, c))
    gate = pl.BlockSpec((tm, WIDTH), lambda i, c: (i, seg // WIDTH + c))
    specs, args, aliases = [half, half, gate], [d_m, y, proj], {}
    if dproj is not None:
        specs, args, aliases = [pl.BlockSpec(memory_space=pl.ANY)] + specs, [dproj] + args, {0: 0}
    return pl.pallas_call(
        body, name=name, grid=(t // tm, 2), in_specs=specs, out_specs=[gate, half],
        out_shape=[jax.ShapeDtypeStruct((t, PACKED_WIDTH), MXU_DTYPE), jax.ShapeDtypeStruct((t, D_MODEL), MXU_DTYPE)],
        input_output_aliases=aliases,
        compiler_params=_params(("parallel", "parallel")),
    )(*args)


def _tail(x, mo, final_w, target, tm=256):
    t = x.shape[0]

    def body(x_ref, mo_ref, w_ref, t_ref, dxm_ref, dx_ref, loss_ref, dw_ref):
        x2 = x_ref[...] + mo_ref[...]
        w = w_ref[...]
        r = lax.rsqrt(jnp.mean(x2 * x2, axis=-1, keepdims=True) + NORM_EPS)
        xn = x2 * r
        err = xn * w - t_ref[...]
        dy = err * (1.0 / D_MODEL)
        dyw = dy * w
        dx2 = r * dyw - x2 * (r * r * r) * jnp.mean(dyw * x2, axis=-1, keepdims=True)
        dx_ref[...] = dx2
        dxm_ref[...] = dx2.astype(dxm_ref.dtype)
        loss = 0.5 * jnp.sum(jnp.sum(err * err, axis=-1, keepdims=True) * (1.0 / D_MODEL), axis=0, keepdims=True)
        onehot = jnp.where((_iota((8, 128), 0) == 0) & (_iota((8, 128), 1) == 0), 1.0, 0.0)

        @pl.when(pl.program_id(0) == 0)
        def _():
            loss_ref[...] = jnp.zeros_like(loss_ref)
            dw_ref[...] = jnp.zeros_like(dw_ref)

        loss_ref[...] += loss * onehot
        dw_ref[...] += jnp.where(_iota((8, D_MODEL), 0) == 0, jnp.sum(dy * xn, axis=0, keepdims=True), 0.0)

    tok = pl.BlockSpec((tm, D_MODEL), lambda i: (i, 0))
    return pl.pallas_call(
        body, name="tail", grid=(t // tm,),
        in_specs=[tok, tok, pl.BlockSpec((1, D_MODEL), lambda i: (0, 0)), tok],
        out_specs=[tok, tok, pl.BlockSpec((8, 128), lambda i: (0, 0)), pl.BlockSpec((8, D_MODEL), lambda i: (0, 0))],
        out_shape=[jax.ShapeDtypeStruct((t, D_MODEL), MXU_DTYPE), jax.ShapeDtypeStruct((t, D_MODEL), F32),
                   jax.ShapeDtypeStruct((8, 128), F32), jax.ShapeDtypeStruct((8, D_MODEL), F32)],
        compiler_params=_params(("arbitrary",)),
    )(x, mo, final_w, target)


def _norm_bwd(x, norm_w, dh, dx2, tm=256):
    t = x.shape[0]

    def body(x_ref, w_ref, dh_ref, dx2_ref, dx_ref, dw_ref):
        xf, w, dh_ = x_ref[...], w_ref[...], dh_ref[...]
        r = lax.rsqrt(jnp.mean(xf * xf, axis=-1, keepdims=True) + NORM_EPS)
        dhw = dh_ * w
        dx_ref[...] = dx2_ref[...] + r * dhw - xf * (r * r * r) * jnp.mean(dhw * xf, axis=-1, keepdims=True)

        @pl.when(pl.program_id(0) == 0)
        def _():
            dw_ref[...] = jnp.zeros_like(dw_ref)

        dw_ref[...] += jnp.where(_iota((8, D_MODEL), 0) == 0, jnp.sum(dh_ * xf * r, axis=0, keepdims=True), 0.0)

    tok = pl.BlockSpec((tm, D_MODEL), lambda i: (i, 0))
    return pl.pallas_call(
        body, name="norm_bwd", grid=(t // tm,),
        in_specs=[tok, pl.BlockSpec((1, D_MODEL), lambda i: (0, 0)), tok, tok],
        out_specs=[tok, pl.BlockSpec((8, D_MODEL), lambda i: (0, 0))],
        out_shape=[jax.ShapeDtypeStruct((t, D_MODEL), F32), jax.ShapeDtypeStruct((8, D_MODEL), F32)],
        compiler_params=_params(("arbitrary",)),
    )(x, norm_w, dh, dx2)


def _to_strided(a, d):
    return a if d == 1 else a.reshape(a.shape[0] // d, d * a.shape[1])


def _from_strided(a, d):
    return a if d == 1 else a.reshape(a.shape[0] * d, a.shape[1] // d)


def _local_step(x, target, norm_w, wp, conv_w, a_log, dt_bias, gdn_norm_w, w_up_a, w_up_b, w_out, final_w):
    t = x.shape[0]
    tables = _rope_tables(t)
    a_log = jnp.pad(a_log, ((0, 0), (HEADS, 128 - 2 * HEADS)))
    dt_bias = jnp.pad(dt_bias, ((0, 0), (HEADS, 128 - 2 * HEADS)))

    proj, h = _norm_proj(x, norm_w, wp)
    qkv = _rope_fwd(proj, tables)
    att_in, outs, lses = [], [], []
    for gi, d in enumerate(DILATIONS):
        a_in = qkv if d == 1 else _to_strided(qkv[:, gi * 3 * WIDTH:(gi + 1) * 3 * WIDTH], d)
        o_g, lse_g = _att_fwd(a_in, d, f"att_fwd{gi}")
        att_in.append(a_in)
        outs.append(_from_strided(o_g, d))
        lses.append(_from_strided(lse_g, d))
    oz_a, o_a, lse = _att_merge(outs, lses, proj)
    conv = _conv_fwd(proj, conv_w)
    gq, gk, gv, gb, gg, ggl, grow = _gdn_prep(conv, proj, a_log, dt_bias)
    o_b, states, invs = _gdn_fwd(gq, gk, gv, gb, gg, ggl, grow)
    oz_b = _gdn_out(o_b, gdn_norm_w, proj)
    y_a = _matmul(oz_a, w_up_a, "nn", "up_a")
    y_b = _matmul(oz_b, w_up_b, "nn", "up_b")
    merged = _merge(y_a, y_b, proj)
    mo = _matmul(merged, w_out, "nn", "out_proj")
    dx2_m, dx2, loss_blk, d_final = _tail(x, mo, final_w, target)

    d_wout = _matmul(merged, dx2_m, "tn", "d_w_out")
    d_m = _matmul(dx2_m, w_out, "nt", "d_merged")
    dproj, dy_a = _merge_bwd(None, d_m, y_a, proj, SEG_GA, "merge_bwd_a")
    dproj, dy_b = _merge_bwd(dproj, d_m, y_b, proj, SEG_GB, "merge_bwd_b")
    d_wua = _matmul(oz_a, dy_a, "tn", "d_w_up_a")
    d_wub = _matmul(oz_b, dy_b, "tn", "d_w_up_b")
    d_oz_a = _matmul(dy_a, w_up_a, "nt", "d_oz_a")
    d_oz_b = _matmul(dy_b, w_up_b, "nt", "d_oz_b")
    dproj, do_a, delta = _att_merge_bwd(dproj, d_oz_a, o_a, proj)
    for gi, d in enumerate(DILATIONS):
        dqkv = _att_bwd(att_in[gi], _to_strided(do_a, d), _to_strided(lse, d), _to_strided(delta, d), d, f"att_bwd{gi}")
        dproj = _rope_bwd(dproj, _from_strided(dqkv, d), gi, tables)
    dproj, do_b, d_gnw = _gdn_out_bwd(dproj, d_oz_b, o_b, gdn_norm_w, proj)
    dgq, dgk, dgv, dgb, dgg, dggl, dgrow = _gdn_bwd(gq, gk, gv, gb, gg, ggl, grow, states, invs, do_b)
    dproj, dconv, d_small = _gdn_prep_bwd(dproj, conv, proj, a_log, dt_bias, dgq, dgk, dgv, dgb, dgg, dggl, dgrow)
    dproj, d_convw = _conv_bwd(dproj, dconv, proj, conv_w)
    d_wp = _matmul(h, dproj, "tn", "d_w_in")
    dh = _matmul(dproj, wp, "nt", "d_h", tk=512)
    grad_x, d_norm = _norm_bwd(x, norm_w, dh, dx2)
    return dict(loss=loss_blk, grad_x=grad_x, norm_w=d_norm[0:1], w_in=d_wp, conv_w=d_convw[0:GDN_CONV],
                a_log=d_small[0:1, HEADS:2 * HEADS], dt_bias=d_small[1:2, HEADS:2 * HEADS], gdn_norm_w=d_gnw[0:1],
                w_up_a=d_wua, w_up_b=d_wub,
                w_out=d_wout, final_norm_w=d_final[0:1])


SHARDS = 4
W_IN_SHARD = IN_WIDTH // SHARDS
ROWS_W_IN = D_MODEL * W_IN_SHARD // 128
ROWS_UP = WIDTH * (D_MODEL // SHARDS) // 128
ROWS_OUT = (D_MODEL // SHARDS) * D_MODEL // 128
CONV_SHARD = 3 * WIDTH // SHARDS
ROWS_CONV = 16
SLAB_ROWS = ROWS_W_IN + 2 * ROWS_UP + ROWS_OUT + 2 * ROWS_CONV
HALF_ROWS = SLAB_ROWS // 2
MESH = pl.DeviceIdType.MESH
ANY = pl.BlockSpec(memory_space=pl.ANY)


def _pad_rows(a, rows):
    return jnp.pad(a, ((0, rows - a.shape[0]), (0, 0)))


def _pack_slab(w_in, w_up_a, w_up_b, w_out, conv, conv_lo):
    parts = [w_in.reshape(ROWS_W_IN, 128), w_up_a.reshape(ROWS_UP, 128), w_up_b.reshape(ROWS_UP, 128),
             w_out.reshape(ROWS_OUT, 128), _pad_rows(conv.reshape(-1, 128), ROWS_CONV), _pad_rows(conv_lo.reshape(-1, 128), ROWS_CONV)]
    return jnp.concatenate(parts, axis=0)


def _unpack_slab(slab):
    r0 = 0
    out = []
    for rows, shape in ((ROWS_W_IN, (D_MODEL, W_IN_SHARD)), (ROWS_UP, (WIDTH, D_MODEL // SHARDS)), (ROWS_UP, (WIDTH, D_MODEL // SHARDS)),
                        (ROWS_OUT, (D_MODEL // SHARDS, D_MODEL)), (ROWS_CONV, None), (ROWS_CONV, None)):
        part = slab[r0:r0 + rows]
        out.append(part[:GDN_CONV * CONV_SHARD // 128].reshape(GDN_CONV, CONV_SHARD) if shape is None else part.reshape(shape))
        r0 += rows
    return out


def _mesh_position():
    x, y, c = lax.axis_index("x"), lax.axis_index("y"), lax.axis_index("c")
    return x, y, c, [(1 - x, y), (x, 1 - y), (1 - x, 1 - y)]


def _gather_weights(slab):
    def body(slab_ref, out_ref, send_sems, recv_sems, local_sem):
        x, y, c, chips = _mesh_position()
        mine = pltpu.make_async_copy(slab_ref, out_ref.at[2 * x + y], local_sem)
        mine.start()

        def half(chip, which):
            return out_ref.at[2 * chip[0] + chip[1], pl.ds(which * HALF_ROWS, HALF_ROWS), :]

        def copy(k, src, dst, to):
            return pltpu.make_async_remote_copy(src_ref=src, dst_ref=dst, send_sem=send_sems.at[k], recv_sem=recv_sems.at[k],
                                                device_id=to, device_id_type=MESH)

        first = [copy(j, slab_ref.at[pl.ds(c * HALF_ROWS, HALF_ROWS), :], half((x, y), c), (*chip, c)) for j, chip in enumerate(chips)]
        for cp in first:
            cp.start()
        passed = [copy(3 + j, half(chip, c), half(chip, c), (x, y, 1 - c)) for j, chip in enumerate(chips)]
        for j, chip in enumerate(chips):
            copy(j, half(chip, c), half(chip, c), (x, y, c)).wait_recv()
            passed[j].start()
        for j, chip in enumerate(chips):
            copy(3 + j, half(chip, 1 - c), half(chip, 1 - c), (x, y, c)).wait_recv()
        for cp in first + passed:
            cp.wait_send()
        mine.wait()

    return pl.pallas_call(
        body, name="gather_weights", in_specs=[ANY], out_specs=ANY,
        out_shape=jax.ShapeDtypeStruct((SHARDS, SLAB_ROWS, 128), slab.dtype),
        scratch_shapes=[pltpu.SemaphoreType.DMA((6,)), pltpu.SemaphoreType.DMA((6,)), pltpu.SemaphoreType.DMA],    )(slab)


def _exchange_halves(grads):
    def body(g_ref, out_ref, send_sems, recv_sems):
        x, y, c, _ = _mesh_position()
        copies = [pltpu.make_async_remote_copy(src_ref=g_ref.at[s, pl.ds((1 - c) * HALF_ROWS, HALF_ROWS), :], dst_ref=out_ref.at[s],
                                               send_sem=send_sems.at[s], recv_sem=recv_sems.at[s],
                                               device_id=(x, y, 1 - c), device_id_type=MESH) for s in range(SHARDS)]
        for cp in copies:
            cp.start()
        for cp in copies:
            cp.wait()

    return pl.pallas_call(
        body, name="exchange_halves", in_specs=[ANY], out_specs=ANY,
        out_shape=jax.ShapeDtypeStruct((SHARDS, HALF_ROWS, 128), F32),
        scratch_shapes=[pltpu.SemaphoreType.DMA((SHARDS,)), pltpu.SemaphoreType.DMA((SHARDS,))],    )(grads)


SUM_ROWS = HALF_ROWS // 2


def _pair_sum(grads, recv):
    nblk = HALF_ROWS // SUM_ROWS

    def body(c_ref, g_ref, r_ref, o_ref):
        o_ref[...] = (g_ref[...] + r_ref[...]).astype(o_ref.dtype)

    spec = pl.BlockSpec((1, SUM_ROWS, 128), lambda s, i, c_ref: (s, i, 0))
    return pl.pallas_call(
        body, name="pair_sum",
        grid_spec=pltpu.PrefetchScalarGridSpec(
            num_scalar_prefetch=1, grid=(SHARDS, nblk),
            in_specs=[pl.BlockSpec((1, SUM_ROWS, 128), lambda s, i, c_ref: (s, c_ref[0] * nblk + i, 0)), spec],
            out_specs=spec),
        out_shape=jax.ShapeDtypeStruct((SHARDS, HALF_ROWS, 128), MXU_DTYPE),
        compiler_params=_params(("parallel", "parallel")),
    )(lax.axis_index("c").astype(jnp.int32).reshape(1), grads, recv)


def _scatter_pairs(pairs):
    def body(p_ref, out_ref, send_sems, recv_sems):
        x, y, c, chips = _mesh_position()
        copies = [pltpu.make_async_remote_copy(src_ref=p_ref.at[2 * chip[0] + chip[1]], dst_ref=out_ref.at[j],
                                               send_sem=send_sems.at[j], recv_sem=recv_sems.at[j],
                                               device_id=(*chip, c), device_id_type=MESH) for j, chip in enumerate(chips)]
        for cp in copies:
            cp.start()
        for cp in copies:
            cp.wait()

    return pl.pallas_call(
        body, name="scatter_pairs", in_specs=[ANY], out_specs=ANY,
        out_shape=jax.ShapeDtypeStruct((3, HALF_ROWS, 128), pairs.dtype),
        scratch_shapes=[pltpu.SemaphoreType.DMA((3,)), pltpu.SemaphoreType.DMA((3,))],    )(pairs)


def _chip_sum(pairs, recv):
    def body(own_ref, p_ref, r_ref, o_ref):
        o_ref[...] = ((p_ref[0].astype(F32) + r_ref[0].astype(F32)) + r_ref[1].astype(F32)) + r_ref[2].astype(F32)

    own = (2 * lax.axis_index("x") + lax.axis_index("y")).astype(jnp.int32).reshape(1)
    return pl.pallas_call(
        body, name="chip_sum",
        grid_spec=pltpu.PrefetchScalarGridSpec(
            num_scalar_prefetch=1, grid=(HALF_ROWS // SUM_ROWS,),
            in_specs=[pl.BlockSpec((1, SUM_ROWS, 128), lambda i, own_ref: (own_ref[0], i, 0)),
                      pl.BlockSpec((3, SUM_ROWS, 128), lambda i, own_ref: (0, i, 0))],
            out_specs=pl.BlockSpec((SUM_ROWS, 128), lambda i, own_ref: (i, 0))),
        out_shape=jax.ShapeDtypeStruct((HALF_ROWS, 128), F32),
        compiler_params=_params(("parallel",)),
    )(own, pairs, recv)


def _share_total(total_half):
    def body(t_ref, out_ref, send_sem, recv_sem, local_sem):
        x, y, c, _ = _mesh_position()
        mine = pltpu.make_async_copy(t_ref, out_ref.at[pl.ds(c * HALF_ROWS, HALF_ROWS), :], local_sem)
        mine.start()
        cp = pltpu.make_async_remote_copy(src_ref=t_ref, dst_ref=out_ref.at[pl.ds(c * HALF_ROWS, HALF_ROWS), :],
                                          send_sem=send_sem, recv_sem=recv_sem, device_id=(x, y, 1 - c), device_id_type=MESH)
        cp.start()
        other = out_ref.at[pl.ds((1 - c) * HALF_ROWS, HALF_ROWS), :]
        pltpu.make_async_remote_copy(src_ref=other, dst_ref=other, send_sem=send_sem, recv_sem=recv_sem,
                                     device_id=(x, y, c), device_id_type=MESH).wait_recv()
        cp.wait_send()
        mine.wait()

    return pl.pallas_call(
        body, name="share_total", in_specs=[ANY], out_specs=ANY,
        out_shape=jax.ShapeDtypeStruct((SLAB_ROWS, 128), F32),
        scratch_shapes=[pltpu.SemaphoreType.DMA, pltpu.SemaphoreType.DMA, pltpu.SemaphoreType.DMA],    )(total_half)


def _allreduce_small(block):
    def body(b_ref, out_ref, gath, send_sems, recv_sems):
        x, y, c, _ = _mesh_position()
        me = 4 * x + 2 * y + c
        gath[me] = b_ref[...]
        copies = []
        for k in range(1, 8):
            peer = (x ^ (k >> 2), y ^ ((k >> 1) & 1), c ^ (k & 1))
            copies.append(pltpu.make_async_remote_copy(src_ref=b_ref, dst_ref=gath.at[me], send_sem=send_sems.at[k - 1],
                                                       recv_sem=recv_sems.at[k - 1], device_id=peer, device_id_type=MESH))
        for cp in copies:
            cp.start()
        for k in range(1, 8):
            src = 4 * (x ^ (k >> 2)) + 2 * (y ^ ((k >> 1) & 1)) + (c ^ (k & 1))
            pltpu.make_async_remote_copy(src_ref=b_ref, dst_ref=gath.at[src], send_sem=send_sems.at[k - 1],
                                         recv_sem=recv_sems.at[k - 1], device_id=(x, y, c), device_id_type=MESH).wait_recv()
        for cp in copies:
            cp.wait_send()
        acc = gath[0]
        for d in range(1, 8):
            acc = acc + gath[d]
        out_ref[...] = acc

    vm = pl.BlockSpec(memory_space=pltpu.VMEM)
    return pl.pallas_call(
        body, name="allreduce_small", in_specs=[vm], out_specs=vm,
        out_shape=jax.ShapeDtypeStruct((8, D_MODEL), F32),
        scratch_shapes=[pltpu.VMEM((8, 8, D_MODEL), F32), pltpu.SemaphoreType.DMA((7,)), pltpu.SemaphoreType.DMA((7,))],
    )(block)


def _adamw(w, g, m, v, name):
    rows, cols = w.shape
    tr = 128 if rows % 128 == 0 else rows

    def body(w_ref, g_ref, m_ref, v_ref, d_ref, nm_ref, nv_ref):
        gv = g_ref[...]
        nm = ADAM_B1 * m_ref[...] + (1.0 - ADAM_B1) * gv
        nv = ADAM_B2 * v_ref[...] + (1.0 - ADAM_B2) * (gv * gv)
        m_hat = nm / (1.0 - ADAM_B1 ** ADAM_STEP)
        v_hat = nv / (1.0 - ADAM_B2 ** ADAM_STEP)
        d_ref[...] = -ADAM_LR * (m_hat / (jnp.sqrt(v_hat) + ADAM_EPS) + ADAM_WD * w_ref[...])
        nm_ref[...] = nm
        nv_ref[...] = nv

    spec = pl.BlockSpec((tr, cols), lambda i: (i, 0))
    shape = jax.ShapeDtypeStruct((rows, cols), F32)
    return pl.pallas_call(
        body, name=name, grid=(rows // tr,), in_specs=[spec] * 4, out_specs=[spec] * 3, out_shape=[shape] * 3,
        compiler_params=_params(("parallel",)),
    )(w, g, m, v)


def _pack_w_in(w):
    return jnp.concatenate([w[:, :BA_END], jnp.zeros((D_MODEL, SEG_GA - BA_END), w.dtype), w[:, BA_END:]], axis=1)


def kernel(x, norm_w, w_in, conv_w, a_log, dt_bias, gdn_norm_w, w_up_a, w_up_b, w_out, final_norm_w, loss_target, m_norm_w, m_w_in, m_conv_w, m_a_log, m_dt_bias, m_gdn_norm_w, m_w_up_a, m_w_up_b, m_w_out, m_final_norm_w, v_norm_w, v_w_in, v_conv_w, v_a_log, v_dt_bias, v_gdn_norm_w, v_w_up_a, v_w_up_b, v_w_out, v_final_norm_w):
    conv_hi = conv_w[0].astype(MXU_DTYPE)
    conv_lo = (conv_w[0] - conv_hi.astype(F32)).astype(MXU_DTYPE)
    slab = _pack_slab(w_in[0].astype(MXU_DTYPE), w_up_a[0].astype(MXU_DTYPE), w_up_b[0].astype(MXU_DTYPE),
                      w_out[0].astype(MXU_DTYPE), conv_hi, conv_lo)
    slabs = _gather_weights(slab)
    parts = [_unpack_slab(slabs[s]) for s in range(SHARDS)]
    w_in_full = jnp.concatenate([p[0] for p in parts], axis=1)
    w_up_a_full = jnp.concatenate([p[1] for p in parts], axis=1)
    w_up_b_full = jnp.concatenate([p[2] for p in parts], axis=1)
    w_out_full = jnp.concatenate([p[3] for p in parts], axis=0)
    conv_full = jnp.concatenate([p[4].astype(F32) + p[5].astype(F32) for p in parts], axis=1)

    g = _local_step(x[0], loss_target[0], norm_w, _pack_w_in(w_in_full), conv_full, a_log, dt_bias, gdn_norm_w,
                    w_up_a_full, w_up_b_full, w_out_full, final_norm_w[None])

    d_w_in = jnp.concatenate([g["w_in"][:, :BA_END], g["w_in"][:, SEG_GA:]], axis=1)
    zero_conv = jnp.zeros((GDN_CONV, CONV_SHARD), F32)
    grads = jnp.stack([
        _pack_slab(d_w_in[:, s * W_IN_SHARD:(s + 1) * W_IN_SHARD], g["w_up_a"][:, s * 256:(s + 1) * 256],
                   g["w_up_b"][:, s * 256:(s + 1) * 256], g["w_out"][s * 256:(s + 1) * 256],
                   g["conv_w"][:, s * CONV_SHARD:(s + 1) * CONV_SHARD], zero_conv) for s in range(SHARDS)])
    from_sibling = _exchange_halves(grads)
    pairs = _pair_sum(grads, from_sibling)
    from_chips = _scatter_pairs(pairs)
    total = _share_total(_chip_sum(pairs, from_chips))
    g_w_in, g_w_up_a, g_w_up_b, g_w_out, g_conv, _ = _unpack_slab(total)

    row2 = jnp.concatenate([g["gdn_norm_w"], g["a_log"], g["dt_bias"], g["loss"][0:1, 0:1],
                            jnp.zeros((1, D_MODEL - HEAD_DIM - 2 * HEADS - 1), F32)], axis=1)
    small = _allreduce_small(jnp.concatenate([g["norm_w"], g["final_norm_w"], row2, jnp.zeros((5, D_MODEL), F32)], axis=0))
    g_norm, g_final = small[0:1], small[1]
    g_gnw, g_alog, g_dt = small[2:3, 0:HEAD_DIM], small[2:3, HEAD_DIM:HEAD_DIM + HEADS], small[2:3, HEAD_DIM + HEADS:HEAD_DIM + 2 * HEADS]
    loss = small[2, HEAD_DIM + 2 * HEADS]

    names = ["norm_w", "w_in", "conv_w", "a_log", "dt_bias", "gdn_norm_w", "w_up_a", "w_up_b", "w_out", "final_norm_w"]
    weights = dict(zip(names, (norm_w, w_in, conv_w, a_log, dt_bias, gdn_norm_w, w_up_a, w_up_b, w_out, final_norm_w)))
    ms = dict(zip(names, (m_norm_w, m_w_in, m_conv_w, m_a_log, m_dt_bias, m_gdn_norm_w, m_w_up_a, m_w_up_b, m_w_out, m_final_norm_w)))
    vs = dict(zip(names, (v_norm_w, v_w_in, v_conv_w, v_a_log, v_dt_bias, v_gdn_norm_w, v_w_up_a, v_w_up_b, v_w_out, v_final_norm_w)))
    grads2d = dict(norm_w=g_norm, w_in=g_w_in, conv_w=g_conv, a_log=g_alog, dt_bias=g_dt, gdn_norm_w=g_gnw,
                   w_up_a=g_w_up_a, w_up_b=g_w_up_b, w_out=g_w_out, final_norm_w=g_final[None])
    grad_out, delta, new_m, new_v = [], [], [], []
    for n in names:
        shape = weights[n].shape
        two_d = grads2d[n].shape
        d, nm, nv = _adamw(weights[n].reshape(two_d), grads2d[n], ms[n].reshape(two_d), vs[n].reshape(two_d), f"adamw_{n}")
        grad_out.append(grads2d[n].reshape(shape))
        delta.append(d.reshape(shape))
        new_m.append(nm.reshape(shape))
        new_v.append(nv.reshape(shape))
    return (loss, g["grad_x"][None], *grad_out, *delta, *new_m, *new_v)
```

```python
import functools

import jax
import jax.numpy as jnp
from jax import lax
from jax.experimental import pallas as pl
from jax.experimental.pallas import tpu as pltpu

F32 = jnp.float32
MXU_DTYPE = jnp.bfloat16
HIGHEST = lax.Precision.HIGHEST

D_MODEL = 1024
HEADS = 8
HEAD_DIM = 64
WIDTH = HEADS * HEAD_DIM
NORM_EPS = 1e-6
ROPE_THETA = 10000.0
ATT_BLOCK = 128
DILATIONS = (1, 4, 16)
GDN_CHUNK = 64
GDN_CONV = 4
IN_WIDTH = 9232
SEG_A, SEG_ZA, SEG_B, SEG_ZB, SEG_BA, SEG_GA, SEG_GB, PACKED_WIDTH = 0, 4608, 5120, 6656, 7168, 7680, 8704, 9728
BA_END = 7184
VMEM_LIMIT = 56 * 1024 * 1024

ADAM_LR, ADAM_B1, ADAM_B2, ADAM_EPS, ADAM_WD, ADAM_STEP = 0.001, 0.9, 0.999, 1e-08, 0.01, 10

_NN = (((1,), (0,)), ((), ()))
_NT = (((1,), (1,)), ((), ()))
_TN = (((0,), (0,)), ((), ()))


def _params(sem):
    return pltpu.CompilerParams(dimension_semantics=sem, vmem_limit_bytes=VMEM_LIMIT)


def _mxu(a, b, dims):
    return lax.dot_general(a.astype(MXU_DTYPE), b.astype(MXU_DTYPE), dims, preferred_element_type=F32)


def _sigmoid(x):
    return 1.0 / (1.0 + jnp.exp(-x))


def _softplus(x):
    return jnp.maximum(x, 0.0) + jnp.log(1.0 + jnp.exp(-jnp.abs(x)))


def _iota(shape, axis):
    return lax.broadcasted_iota(jnp.int32, shape, axis)


def _matmul(a, b, mode, name, out_dtype=F32, tm=512, tn=512, tk=512):
    if mode == "nn":
        (m, k), (k2, n) = a.shape, b.shape
    elif mode == "nt":
        (m, k), (n, k2) = a.shape, b.shape
    else:
        (k, m), (k2, n) = a.shape, b.shape
    assert k == k2
    tm, tn, tk = min(tm, m), min(tn, n), min(tk, k)
    assert m % tm == 0 and n % tn == 0 and k % tk == 0
    nk = k // tk
    dims = {"nn": _NN, "nt": _NT, "tn": _TN}[mode]

    def body(a_ref, b_ref, o_ref, acc_ref):
        kk = pl.program_id(2)

        @pl.when(kk == 0)
        def _():
            acc_ref[...] = jnp.zeros_like(acc_ref)

        acc_ref[...] += _mxu(a_ref[...], b_ref[...], dims)

        @pl.when(kk == nk - 1)
        def _():
            o_ref[...] = acc_ref[...].astype(o_ref.dtype)

    a_spec = pl.BlockSpec((tk, tm), lambda i, j, kk: (kk, i)) if mode == "tn" else pl.BlockSpec((tm, tk), lambda i, j, kk: (i, kk))
    b_spec = pl.BlockSpec((tn, tk), lambda i, j, kk: (j, kk)) if mode == "nt" else pl.BlockSpec((tk, tn), lambda i, j, kk: (kk, j))
    return pl.pallas_call(
        body, name=name, grid=(m // tm, n // tn, nk), in_specs=[a_spec, b_spec],
        out_specs=pl.BlockSpec((tm, tn), lambda i, j, kk: (i, j)),
        out_shape=jax.ShapeDtypeStruct((m, n), out_dtype),
        scratch_shapes=[pltpu.VMEM((tm, tn), F32)],
        compiler_params=_params(("parallel", "parallel", "arbitrary")),
    )(a, b)


def _norm_proj(x, norm_w, wp, tm=1024, tn=512):
    t = x.shape[0]

    def body(x_ref, nw_ref, w_ref, proj_ref, h_ref):
        @pl.when(pl.program_id(1) == 0)
        def _():
            xf = x_ref[...]
            r = lax.rsqrt(jnp.mean(xf * xf, axis=-1, keepdims=True) + NORM_EPS)
            h_ref[...] = (xf * r * nw_ref[...]).astype(h_ref.dtype)

        proj_ref[...] = jnp.dot(h_ref[...], w_ref[...], preferred_element_type=F32)

    return pl.pallas_call(
        body, name="norm_proj", grid=(t // tm, PACKED_WIDTH // tn),
        in_specs=[pl.BlockSpec((tm, D_MODEL), lambda i, j: (i, 0)),
                  pl.BlockSpec((1, D_MODEL), lambda i, j: (0, 0)),
                  pl.BlockSpec((D_MODEL, tn), lambda i, j: (0, j))],
        out_specs=[pl.BlockSpec((tm, tn), lambda i, j: (i, j)),
                   pl.BlockSpec((tm, D_MODEL), lambda i, j: (i, 0))],
        out_shape=[jax.ShapeDtypeStruct((t, PACKED_WIDTH), F32), jax.ShapeDtypeStruct((t, D_MODEL), MXU_DTYPE)],
        compiler_params=_params(("parallel", "arbitrary")),
    )(x, norm_w, wp)


def _rope_tables(t):
    lane = jnp.arange(128)
    inv_freq = ROPE_THETA ** (-jnp.arange(0, HEAD_DIM, 2, dtype=F32) / HEAD_DIM)
    ang = jnp.arange(t, dtype=F32)[:, None] * inv_freq[None, :]
    ang = jnp.concatenate([ang, ang, ang, ang], axis=-1)
    first_half = (lane % HEAD_DIM) < HEAD_DIM // 2
    cos, sin = jnp.cos(ang), jnp.sin(ang)
    return cos, jnp.where(first_half, -sin, 0.0), jnp.where(first_half, 0.0, sin)


def _rope_block(x, cos, sin_lo, sin_hi, sign):
    outs = []
    for c in range(8):
        xc = x[:, c * 128:(c + 1) * 128]
        rot = pltpu.roll(xc, 96, 1) * sin_lo + pltpu.roll(xc, 32, 1) * sin_hi
        outs.append(xc * cos + sign * rot)
    outs.append(x[:, 2 * WIDTH:])
    return jnp.concatenate(outs, axis=1)


def _rope_fwd(proj, tables, tm=256):
    t = proj.shape[0]

    def body(x_ref, c_ref, sl_ref, sh_ref, o_ref):
        o_ref[...] = _rope_block(x_ref[...], c_ref[...], sl_ref[...], sh_ref[...], 1.0).astype(o_ref.dtype)

    tab = pl.BlockSpec((tm, 128), lambda i, g: (i, 0))
    return pl.pallas_call(
        body, name="rope_fwd", grid=(t // tm, 3),
        in_specs=[pl.BlockSpec((tm, 3 * WIDTH), lambda i, g: (i, g)), tab, tab, tab],
        out_specs=pl.BlockSpec((tm, 3 * WIDTH), lambda i, g: (i, g)),
        out_shape=jax.ShapeDtypeStruct((t, 9 * WIDTH), MXU_DTYPE),
        compiler_params=_params(("parallel", "parallel")),
    )(proj, *tables)


def _rope_bwd(dproj, dqkv, group, tables, tm=256):
    t = dqkv.shape[0]

    def body(*refs):
        x_ref, c_ref, sl_ref, sh_ref, o_ref = refs[-5:]
        o_ref[...] = _rope_block(x_ref[...], c_ref[...], sl_ref[...], sh_ref[...], -1.0).astype(o_ref.dtype)

    tab = pl.BlockSpec((tm, 128), lambda i: (i, 0))
    specs = [pl.BlockSpec((tm, 3 * WIDTH), lambda i: (i, 0)), tab, tab, tab]
    args = [dqkv, *tables]
    aliases = {}
    if dproj is not None:
        specs = [pl.BlockSpec(memory_space=pl.ANY)] + specs
        args = [dproj] + args
        aliases = {0: 0}
    return pl.pallas_call(
        body, name=f"rope_bwd{group}", grid=(t // tm,), in_specs=specs,
        out_specs=pl.BlockSpec((tm, 3 * WIDTH), lambda i: (i, group)),
        out_shape=jax.ShapeDtypeStruct((t, PACKED_WIDTH), MXU_DTYPE),
        input_output_aliases=aliases,
        compiler_params=_params(("parallel",)),
    )(*args)


def _att_masks():
    qi = _iota((ATT_BLOCK, ATT_BLOCK), 0)
    kj = _iota((ATT_BLOCK, ATT_BLOCK), 1)
    return kj <= qi, kj >= qi


def _att_fwd(qkv, d, name):
    rows = qkv.shape[0]
    nb = rows // ATT_BLOCK
    scale = HEAD_DIM ** -0.5

    def body(q_ref, kc_ref, kp_ref, vc_ref, vp_ref, o_ref, lse_ref):
        has_prev = pl.program_id(1) > 0
        m_cur, m_prev = _att_masks()
        m_prev = m_prev & has_prev
        hs = range(HEADS)
        sls = [slice(h * HEAD_DIM, (h + 1) * HEAD_DIM) for h in hs]
        qs = [q_ref[:, sl] for sl in sls]
        s_c = [jnp.where(m_cur, _mxu(qs[h], kc_ref[:, sls[h]], _NT) * scale, -jnp.inf) for h in hs]
        s_p = [jnp.where(m_prev, _mxu(qs[h], kp_ref[:, sls[h]], _NT) * scale, -jnp.inf) for h in hs]
        m = [jnp.maximum(jnp.max(s_c[h], axis=1, keepdims=True), jnp.max(s_p[h], axis=1, keepdims=True)) for h in hs]
        p_c = [jnp.exp(s_c[h] - m[h]) for h in hs]
        p_p = [jnp.exp(s_p[h] - m[h]) for h in hs]
        den = [jnp.sum(p_c[h], axis=1, keepdims=True) + jnp.sum(p_p[h], axis=1, keepdims=True) for h in hs]
        o = [_mxu(p_c[h], vc_ref[:, sls[h]], _NN) + _mxu(p_p[h], vp_ref[:, sls[h]], _NN) for h in hs]
        for h in hs:
            o_ref[:, sls[h]] = o[h] / den[h]
            lse_ref[:, sls[h]] = jnp.broadcast_to(m[h] + jnp.log(den[h]), (ATT_BLOCK, HEAD_DIM))

    def cur(c):
        return pl.BlockSpec((ATT_BLOCK, WIDTH), lambda r, i: (i, 3 * r + c))

    def prev(c):
        return pl.BlockSpec((ATT_BLOCK, WIDTH), lambda r, i: (jnp.maximum(i - 1, 0), 3 * r + c))

    out = pl.BlockSpec((ATT_BLOCK, WIDTH), lambda r, i: (i, r))
    return pl.pallas_call(
        body, name=name, grid=(d, nb), in_specs=[cur(0), cur(1), prev(1), cur(2), prev(2)],
        out_specs=[out, out],
        out_shape=[jax.ShapeDtypeStruct((rows, d * WIDTH), F32)] * 2,
        compiler_params=_params(("parallel", "arbitrary")),
    )(qkv, qkv, qkv, qkv, qkv)


def _att_bwd(qkv, do, lse, delta, d, name):
    rows = qkv.shape[0]
    nb = rows // ATT_BLOCK
    scale = HEAD_DIM ** -0.5

    def body(q_ref, qn_ref, kc_ref, kp_ref, vc_ref, vp_ref, do_ref, don_ref, l_ref, ln_ref, dl_ref, dln_ref, o_ref):
        i = pl.program_id(1)
        m_cur, m_prev = _att_masks()
        m_p = m_prev & (i > 0)
        m_n = m_prev & (i < nb - 1)

        hs = range(HEADS)
        sls = [slice(h * HEAD_DIM, (h + 1) * HEAD_DIM) for h in hs]
        col = [slice(h * HEAD_DIM, h * HEAD_DIM + 1) for h in hs]

        def probs(q_r, k_r, lse_r, mask):
            s = [_mxu(q_r[:, sls[h]], k_r[:, sls[h]], _NT) for h in hs]
            return [jnp.where(mask, jnp.exp(s[h] * scale - lse_r[:, col[h]]), 0.0) for h in hs]

        def dscores(p, do_r, v_r, dl_r):
            dp = [_mxu(do_r[:, sls[h]], v_r[:, sls[h]], _NT) for h in hs]
            return [(p[h] * (dp[h] - dl_r[:, col[h]])).astype(MXU_DTYPE) for h in hs]

        p = probs(q_ref, kc_ref, l_ref, m_cur)
        ds = dscores(p, do_ref, vc_ref, dl_ref)
        dq = [_mxu(ds[h], kc_ref[:, sls[h]], _NN) for h in hs]
        dk = [_mxu(ds[h], q_ref[:, sls[h]], _TN) for h in hs]
        dv = [_mxu(p[h], do_ref[:, sls[h]], _TN) for h in hs]
        p = probs(q_ref, kp_ref, l_ref, m_p)
        ds = dscores(p, do_ref, vp_ref, dl_ref)
        dq = [dq[h] + _mxu(ds[h], kp_ref[:, sls[h]], _NN) for h in hs]
        p = probs(qn_ref, kc_ref, ln_ref, m_n)
        ds = dscores(p, don_ref, vc_ref, dln_ref)
        dk = [dk[h] + _mxu(ds[h], qn_ref[:, sls[h]], _TN) for h in hs]
        dv = [dv[h] + _mxu(p[h], don_ref[:, sls[h]], _TN) for h in hs]
        for h in hs:
            o_ref[:, sls[h]] = dq[h] * scale
            o_ref[:, WIDTH + h * HEAD_DIM:WIDTH + (h + 1) * HEAD_DIM] = dk[h] * scale
            o_ref[:, 2 * WIDTH + h * HEAD_DIM:2 * WIDTH + (h + 1) * HEAD_DIM] = dv[h]

    def qkv_spec(c, shift):
        def idx(r, i):
            return (jnp.clip(i + shift, 0, nb - 1), 3 * r + c)
        return pl.BlockSpec((ATT_BLOCK, WIDTH), idx)

    def tok_spec(shift):
        def idx(r, i):
            return (jnp.clip(i + shift, 0, nb - 1), r)
        return pl.BlockSpec((ATT_BLOCK, WIDTH), idx)

    return pl.pallas_call(
        body, name=name, grid=(d, nb),
        in_specs=[qkv_spec(0, 0), qkv_spec(0, 1), qkv_spec(1, 0), qkv_spec(1, -1), qkv_spec(2, 0), qkv_spec(2, -1),
                  tok_spec(0), tok_spec(1), tok_spec(0), tok_spec(1), tok_spec(0), tok_spec(1)],
        out_specs=pl.BlockSpec((ATT_BLOCK, 3 * WIDTH), lambda r, i: (i, r)),
        out_shape=jax.ShapeDtypeStruct((rows, d * 3 * WIDTH), F32),
        compiler_params=_params(("parallel", "arbitrary")),
    )(qkv, qkv, qkv, qkv, qkv, qkv, do, do, lse, lse, delta, delta)


def _att_merge(os_, lses, proj, tm=256):
    t = proj.shape[0]

    def body(o0, o1, o2, l0, l1, l2, z_ref, oz_ref, o_ref, lse_ref):
        a, b, c = l0[...], l1[...], l2[...]
        m = jnp.maximum(jnp.maximum(a, b), c)
        wa, wb, wc = jnp.exp(a - m), jnp.exp(b - m), jnp.exp(c - m)
        den = wa + wb + wc
        o = (wa * o0[...] + wb * o1[...] + wc * o2[...]) / den
        z = z_ref[...]
        o_ref[...] = o
        lse_ref[...] = m + jnp.log(den)
        oz_ref[...] = (o * z * _sigmoid(z)).astype(oz_ref.dtype)

    tok = pl.BlockSpec((tm, WIDTH), lambda i: (i, 0))
    return pl.pallas_call(
        body, name="att_merge", grid=(t // tm,),
        in_specs=[tok] * 6 + [pl.BlockSpec((tm, WIDTH), lambda i: (i, SEG_ZA // WIDTH))],
        out_specs=[tok, tok, tok],
        out_shape=[jax.ShapeDtypeStruct((t, WIDTH), MXU_DTYPE), jax.ShapeDtypeStruct((t, WIDTH), F32),
                   jax.ShapeDtypeStruct((t, WIDTH), F32)],
        compiler_params=_params(("parallel",)),
    )(*os_, *lses, proj)


def _att_merge_bwd(dproj, d_oz, o, proj, tm=256):
    t = proj.shape[0]

    def body(dp_ref, doz_ref, o_ref, z_ref, dz_ref, do_ref, dl_ref):
        z, ov, g = z_ref[...], o_ref[...], doz_ref[...]
        sg = _sigmoid(z)
        do = g * z * sg
        dz_ref[...] = (g * ov * sg * (1.0 + z * (1.0 - sg))).astype(dz_ref.dtype)
        do_ref[...] = do.astype(do_ref.dtype)
        prod = do * ov
        for h in range(HEADS):
            sl = slice(h * HEAD_DIM, (h + 1) * HEAD_DIM)
            dl_ref[:, sl] = jnp.broadcast_to(jnp.sum(prod[:, sl], axis=1, keepdims=True), (tm, HEAD_DIM))

    tok = pl.BlockSpec((tm, WIDTH), lambda i: (i, 0))
    seg = pl.BlockSpec((tm, WIDTH), lambda i: (i, SEG_ZA // WIDTH))
    return pl.pallas_call(
        body, name="att_merge_bwd", grid=(t // tm,),
        in_specs=[pl.BlockSpec(memory_space=pl.ANY), tok, tok, seg],
        out_specs=[seg, tok, tok],
        out_shape=[jax.ShapeDtypeStruct((t, PACKED_WIDTH), MXU_DTYPE), jax.ShapeDtypeStruct((t, WIDTH), MXU_DTYPE),
                   jax.ShapeDtypeStruct((t, WIDTH), F32)],
        input_output_aliases={0: 0},
        compiler_params=_params(("parallel",)),
    )(dproj, d_oz, o, proj)


def _shift_down(x, halo, s):
    if s == 0:
        return x
    xs = pltpu.roll(x, s, 0)
    head = jnp.where(_iota((8, x.shape[1]), 0) < s, pltpu.roll(halo, s, 0), xs[0:8])
    return jnp.concatenate([head, xs[8:]], axis=0)


def _shift_up(x, nxt, s):
    if s == 0:
        return x
    n = x.shape[0]
    xs = pltpu.roll(x, n - s, 0)
    tail = jnp.where(_iota((8, x.shape[1]), 0) >= 8 - s, pltpu.roll(nxt, 8 - s, 0), xs[n - 8:])
    return jnp.concatenate([xs[:n - 8], tail], axis=0)


def _conv_fwd(proj, conv_w, tm=256):
    t = proj.shape[0]
    cb = SEG_B // WIDTH

    def body(x_ref, halo_ref, w_ref, c_ref):
        halo = jnp.where(pl.program_id(0) > 0, halo_ref[...], 0.0)
        x = x_ref[...]
        w = w_ref[...]
        acc = jnp.zeros((tm, WIDTH), F32)
        for j in range(GDN_CONV):
            acc += _shift_down(x, halo, GDN_CONV - 1 - j) * w[j:j + 1, :]
        c_ref[...] = acc

    return pl.pallas_call(
        body, name="conv_fwd", grid=(t // tm, 3),
        in_specs=[pl.BlockSpec((tm, WIDTH), lambda i, c: (i, cb + c)),
                  pl.BlockSpec((8, WIDTH), lambda i, c: (jnp.maximum(i * (tm // 8) - 1, 0), cb + c)),
                  pl.BlockSpec((GDN_CONV, WIDTH), lambda i, c: (0, c))],
        out_specs=pl.BlockSpec((tm, WIDTH), lambda i, c: (i, c)),
        out_shape=jax.ShapeDtypeStruct((t, 3 * WIDTH), F32),
        compiler_params=_params(("parallel", "parallel")),
    )(proj, proj, conv_w)


def _conv_bwd(dproj, dc, proj, conv_w, tm=256):
    t = proj.shape[0]
    cb = SEG_B // WIDTH
    nt = t // tm

    def body(dp_ref, dc_ref, dcn_ref, x_ref, halo_ref, w_ref, dx_ref, dw_ref):
        i = pl.program_id(1)
        w = w_ref[...]
        dcn = jnp.where(i < nt - 1, dcn_ref[...], 0.0)
        dcv = dc_ref[...]
        acc = jnp.zeros((tm, WIDTH), F32)
        for j in range(GDN_CONV):
            acc += _shift_up(dcv, dcn, GDN_CONV - 1 - j) * w[j:j + 1, :]
        dx_ref[...] = acc.astype(dx_ref.dtype)
        halo = jnp.where(i > 0, halo_ref[...], 0.0)
        x = x_ref[...]
        row8 = _iota((8, WIDTH), 0)
        part = jnp.zeros((8, WIDTH), F32)
        for j in range(GDN_CONV):
            s = jnp.sum(dcv * _shift_down(x, halo, GDN_CONV - 1 - j), axis=0, keepdims=True)
            part += jnp.where(row8 == j, s, 0.0)

        @pl.when(i == 0)
        def _():
            dw_ref[...] = jnp.zeros_like(dw_ref)

        dw_ref[...] += part

    return pl.pallas_call(
        body, name="conv_bwd", grid=(3, nt),
        in_specs=[pl.BlockSpec(memory_space=pl.ANY),
                  pl.BlockSpec((tm, WIDTH), lambda c, i: (i, c)),
                  pl.BlockSpec((8, WIDTH), lambda c, i: (jnp.minimum((i + 1) * (tm // 8), t // 8 - 1), c)),
                  pl.BlockSpec((tm, WIDTH), lambda c, i: (i, cb + c)),
                  pl.BlockSpec((8, WIDTH), lambda c, i: (jnp.maximum(i * (tm // 8) - 1, 0), cb + c)),
                  pl.BlockSpec((GDN_CONV, WIDTH), lambda c, i: (0, c))],
        out_specs=[pl.BlockSpec((tm, WIDTH), lambda c, i: (i, cb + c)),
                   pl.BlockSpec((8, WIDTH), lambda c, i: (0, c))],
        out_shape=[jax.ShapeDtypeStruct((t, PACKED_WIDTH), MXU_DTYPE), jax.ShapeDtypeStruct((8, 3 * WIDTH), F32)],
        input_output_aliases={0: 0},
        compiler_params=_params(("parallel", "arbitrary")),
    )(dproj, dc, dc, proj, proj, conv_w)


def _chunk_matrices(tm):
    r, c = _iota((tm, tm), 0), _iota((tm, tm), 1)
    same = (r // GDN_CHUNK) == (c // GDN_CHUNK)
    return jnp.where(same & (c <= r), 1.0, 0.0), jnp.where(same, 1.0, 0.0)


def _gdn_gates(ba, a_log, dt_bias):
    al = ba + dt_bias
    return _sigmoid(ba), -jnp.exp(a_log) * _softplus(al), _sigmoid(al)


def _head_lane_eye():
    return jnp.where(_iota((HEADS, 128), 1) == _iota((HEADS, 128), 0) + HEADS, 1.0, 0.0)


def _gdn_prep(conv, proj, a_log, dt_bias, tm=256):
    t = proj.shape[0]
    nc = tm // GDN_CHUNK

    def body(c_ref, ba_ref, al_ref, dt_ref, q_ref, k_ref, v_ref, b_ref, g_ref, gl_ref, grow_ref):
        c = c_ref[...]
        a = c * _sigmoid(c)
        beta, g, _ = _gdn_gates(ba_ref[:, 0:128], al_ref[...], dt_ref[...])
        lmat, cmat = _chunk_matrices(tm)
        gc = jnp.dot(lmat, g, precision=HIGHEST, preferred_element_type=F32)
        gl = jnp.dot(cmat, g, precision=HIGHEST, preferred_element_type=F32)
        grow = lax.dot_general(_head_lane_eye(), gc, _NT, precision=HIGHEST, preferred_element_type=F32)
        for h in range(HEADS):
            sl = slice(h * HEAD_DIM, (h + 1) * HEAD_DIM)
            qh, kh, vh = a[:, sl], a[:, WIDTH + h * HEAD_DIM:WIDTH + (h + 1) * HEAD_DIM], a[:, 2 * WIDTH + h * HEAD_DIM:2 * WIDTH + (h + 1) * HEAD_DIM]
            rq = lax.rsqrt(jnp.sum(qh * qh, axis=1, keepdims=True) + NORM_EPS)
            rk = lax.rsqrt(jnp.sum(kh * kh, axis=1, keepdims=True) + NORM_EPS)
            q_ref[h] = qh * (rq * HEAD_DIM ** -0.5)
            k_ref[h] = kh * rk
            v_ref[h] = vh
            b_ref[h] = jnp.broadcast_to(beta[:, h:h + 1], (tm, HEAD_DIM))
            g_ref[h] = jnp.broadcast_to(gc[:, HEADS + h:HEADS + h + 1], (tm, HEAD_DIM))
            gl_ref[h] = jnp.broadcast_to(gl[:, HEADS + h:HEADS + h + 1], (tm, HEAD_DIM))
            for cc in range(nc):
                grow_ref[h, cc] = grow[h:h + 1, cc * GDN_CHUNK:(cc + 1) * GDN_CHUNK]

    hm = pl.BlockSpec((HEADS, tm, HEAD_DIM), lambda i: (0, i, 0))
    small = pl.BlockSpec((1, 128), lambda i: (0, 0))
    hm_shape = jax.ShapeDtypeStruct((HEADS, t, HEAD_DIM), F32)
    return pl.pallas_call(
        body, name="gdn_prep", grid=(t // tm,),
        in_specs=[pl.BlockSpec((tm, 3 * WIDTH), lambda i: (i, 0)),
                  pl.BlockSpec((tm, WIDTH), lambda i: (i, SEG_BA // WIDTH)), small, small],
        out_specs=[hm] * 6 + [pl.BlockSpec((HEADS, nc, 1, GDN_CHUNK), lambda i: (0, i, 0, 0))],
        out_shape=[hm_shape] * 6 + [jax.ShapeDtypeStruct((HEADS, t // GDN_CHUNK, 1, GDN_CHUNK), F32)],
        compiler_params=_params(("parallel",)),
    )(conv, proj, a_log, dt_bias)


def _gdn_prep_bwd(dproj, conv, proj, a_log, dt_bias, dq, dk, dv, db, dg, dgl, dgrow, tm=256):
    t = proj.shape[0]
    nc = tm // GDN_CHUNK

    def body(dp_ref, c_ref, ba_ref, al_ref, dt_ref, dq_ref, dk_ref, dv_ref, db_ref, dg_ref, dgl_ref, dgrow_ref,
             dba_ref, dc_ref, small_ref, row_scr):
        c = c_ref[...]
        sg = _sigmoid(c)
        a = c * sg
        beta, g, sig_al = _gdn_gates(ba_ref[:, 0:128], al_ref[...], dt_ref[...])
        lane = _iota((1, 128), 1)
        d_beta = jnp.zeros((tm, 128), F32)
        d_gc = jnp.zeros((tm, 128), F32)
        d_gl = jnp.zeros((tm, 128), F32)
        for h in range(HEADS):
            for cc in range(nc):
                row_scr[h:h + 1, cc * GDN_CHUNK:(cc + 1) * GDN_CHUNK] = dgrow_ref[h, cc]
            d_beta += jnp.sum(db_ref[h], axis=1, keepdims=True) * jnp.where(lane == h, 1.0, 0.0)
            d_gc += jnp.sum(dg_ref[h], axis=1, keepdims=True) * jnp.where(lane == HEADS + h, 1.0, 0.0)
            d_gl += jnp.sum(dgl_ref[h], axis=1, keepdims=True) * jnp.where(lane == HEADS + h, 1.0, 0.0)
            qs, ks, vs = (slice(h * HEAD_DIM, (h + 1) * HEAD_DIM), slice(WIDTH + h * HEAD_DIM, WIDTH + (h + 1) * HEAD_DIM),
                          slice(2 * WIDTH + h * HEAD_DIM, 2 * WIDTH + (h + 1) * HEAD_DIM))
            qh, kh = a[:, qs], a[:, ks]
            rq = lax.rsqrt(jnp.sum(qh * qh, axis=1, keepdims=True) + NORM_EPS)
            rk = lax.rsqrt(jnp.sum(kh * kh, axis=1, keepdims=True) + NORM_EPS)
            gq, gk = dq_ref[h] * HEAD_DIM ** -0.5, dk_ref[h]
            da_q = rq * gq - qh * (rq * rq * rq) * jnp.sum(gq * qh, axis=1, keepdims=True)
            da_k = rk * gk - kh * (rk * rk * rk) * jnp.sum(gk * kh, axis=1, keepdims=True)
            dsilu = lambda s, x: s * (1.0 + x * (1.0 - s))
            dc_ref[:, qs] = da_q * dsilu(sg[:, qs], c[:, qs])
            dc_ref[:, ks] = da_k * dsilu(sg[:, ks], c[:, ks])
            dc_ref[:, vs] = dv_ref[h] * dsilu(sg[:, vs], c[:, vs])
        d_gc += lax.dot_general(row_scr[...], _head_lane_eye(), _TN, precision=HIGHEST, preferred_element_type=F32)
        lmat, cmat = _chunk_matrices(tm)
        d_g = (lax.dot_general(lmat, d_gc, _TN, precision=HIGHEST, preferred_element_type=F32)
               + lax.dot_general(cmat, d_gl, _TN, precision=HIGHEST, preferred_element_type=F32))
        d_al = d_g * (-jnp.exp(al_ref[...])) * sig_al
        d_bl = d_beta * beta * (1.0 - beta)
        dba_ref[...] = jnp.concatenate([d_bl + d_al, jnp.zeros((tm, WIDTH - 128), F32)], axis=1).astype(dba_ref.dtype)
        row8 = _iota((8, 128), 0)
        part = (jnp.where(row8 == 0, jnp.sum(d_g * g, axis=0, keepdims=True), 0.0)
                + jnp.where(row8 == 1, jnp.sum(d_al, axis=0, keepdims=True), 0.0))

        @pl.when(pl.program_id(0) == 0)
        def _():
            small_ref[...] = jnp.zeros_like(small_ref)

        small_ref[...] += part

    hm = pl.BlockSpec((HEADS, tm, HEAD_DIM), lambda i: (0, i, 0))
    small = pl.BlockSpec((1, 128), lambda i: (0, 0))
    seg = pl.BlockSpec((tm, WIDTH), lambda i: (i, SEG_BA // WIDTH))
    return pl.pallas_call(
        body, name="gdn_prep_bwd", grid=(t // tm,),
        in_specs=[pl.BlockSpec(memory_space=pl.ANY), pl.BlockSpec((tm, 3 * WIDTH), lambda i: (i, 0)), seg, small, small]
        + [hm] * 6 + [pl.BlockSpec((HEADS, nc, 1, GDN_CHUNK), lambda i: (0, i, 0, 0))],
        out_specs=[seg, pl.BlockSpec((tm, 3 * WIDTH), lambda i: (i, 0)), pl.BlockSpec((8, 128), lambda i: (0, 0))],
        out_shape=[jax.ShapeDtypeStruct((t, PACKED_WIDTH), MXU_DTYPE), jax.ShapeDtypeStruct((t, 3 * WIDTH), F32),
                   jax.ShapeDtypeStruct((8, 128), F32)],
        scratch_shapes=[pltpu.VMEM((HEADS, tm), F32)],
        input_output_aliases={0: 0},
        compiler_params=_params(("arbitrary",)),
    )(dproj, conv, proj, a_log, dt_bias, dq, dk, dv, db, dg, dgl, dgrow)


_BNN = (((2,), (1,)), ((0,), (0,)))
_BNT = (((2,), (2,)), ((0,), (0,)))
_BTN = (((1,), (1,)), ((0,), (0,)))


@jax.custom_vjp
def _MM_NN(a, b):
    return _mxu(a, b, _BNN)


@jax.custom_vjp
def _MM_NT(a, b):
    return _mxu(a, b, _BNT)


@jax.custom_vjp
def _MM_TN(a, b):
    return _mxu(a, b, _BTN)


_MM_NN.defvjp(lambda a, b: (_mxu(a, b, _BNN), (a, b)), lambda r, g: (_mxu(g, r[1], _BNT), _mxu(r[0], g, _BTN)))
_MM_NT.defvjp(lambda a, b: (_mxu(a, b, _BNT), (a, b)), lambda r, g: (_mxu(g, r[1], _BNN), _mxu(g, r[0], _BTN)))
_MM_TN.defvjp(lambda a, b: (_mxu(a, b, _BTN), (a, b)), lambda r, g: (_mxu(r[1], g, _BNT), _mxu(r[0], g, _BNN)))


def _split(a):
    hi = a.astype(MXU_DTYPE)
    return hi, (a - hi.astype(F32)).astype(MXU_DTYPE)


def _dot3(a, b, dims):
    (ah, al), (bh, bl) = a, b
    dot = lambda u, v: lax.dot_general(u, v, dims, preferred_element_type=F32)
    return dot(ah, bh) + (dot(ah, bl) + dot(al, bh))


def _unit_lower_inverse(a):
    c = GDN_CHUNK
    eye = jnp.where(_iota((c, c), 0) == _iota((c, c), 1), 1.0, 0.0)
    x = eye - a
    p = a
    for _ in range(5):
        ps = _split(p)
        p = _dot3(ps, ps, _BNN)
        x = x + _dot3(_split(x), _split(p), _BNN)
    return x


@jax.custom_vjp
def _SAVED_INVERSE(a, t_inv):
    return t_inv


def _saved_inverse_bwd(t_inv, g):
    ts = _split(t_inv)
    return -_dot3(ts, _split(_dot3(_split(g), ts, _BNT)), _BTN), jnp.zeros_like(t_inv)


_SAVED_INVERSE.defvjp(lambda a, t_inv: (t_inv, t_inv), _saved_inverse_bwd)


def _gdn_chunk(q, k, v, beta, g1, g2, gl, state, t_inv=None):
    c = GDN_CHUNK
    if t_inv is None:
        _mm_nn, _mm_nt, _mm_tn = (functools.partial(_mxu, dims=dd) for dd in (_BNN, _BNT, _BTN))
    else:
        _mm_nn, _mm_nt, _mm_tn = _MM_NN, _MM_NT, _MM_TN
    row, col = _iota((c, c), 0), _iota((c, c), 1)
    incl, strict = row >= col, row > col
    decay = jnp.where(incl, jnp.exp(jnp.where(incl, g1 - g2, 0.0)), 0.0)
    eg = jnp.exp(g1)
    kb = k * beta
    a = _mm_nt(kb, k) * jnp.where(strict, decay, 0.0)
    inv = _unit_lower_inverse(a) if t_inv is None else _SAVED_INVERSE(a, t_inv)
    u = _mm_nn(inv, v * beta)
    w = _mm_nn(inv, kb * eg)
    attn = _mm_nt(q, k) * decay
    v_new = u - _mm_nn(w, state)
    o = _mm_nn(q * eg, state) + _mm_nn(attn, v_new)
    new_state = state * jnp.exp(gl) + _mm_tn(k * jnp.exp(gl - g1), v_new)
    return (o, new_state, inv) if t_inv is None else (o, new_state)


def _gdn_fwd(q, k, v, beta, g, gl, grow, cpb=2):
    t = q.shape[1]
    rows = cpb * GDN_CHUNK

    def body(q_ref, k_ref, v_ref, b_ref, g_ref, gl_ref, grow_ref, o_ref, st_ref, inv_ref, state):
        @pl.when(pl.program_id(0) == 0)
        def _():
            state[...] = jnp.zeros_like(state)

        s = state[...]
        for cc in range(cpb):
            sl = slice(cc * GDN_CHUNK, (cc + 1) * GDN_CHUNK)
            st_ref[:, cc] = s
            g2 = jnp.broadcast_to(grow_ref[:, cc], (HEADS, GDN_CHUNK, GDN_CHUNK))
            o, s, inv = _gdn_chunk(q_ref[:, sl, :], k_ref[:, sl, :], v_ref[:, sl, :], b_ref[:, sl, :], g_ref[:, sl, :], g2,
                                   gl_ref[:, sl, :], s)
            o_ref[:, sl, :] = o
            inv_ref[:, cc] = inv
        state[...] = s

    hm = pl.BlockSpec((HEADS, rows, HEAD_DIM), lambda i: (0, i, 0))
    per_chunk = pl.BlockSpec((HEADS, cpb, GDN_CHUNK, HEAD_DIM), lambda i: (0, i, 0, 0))
    chunk_shape = jax.ShapeDtypeStruct((HEADS, t // GDN_CHUNK, GDN_CHUNK, HEAD_DIM), F32)
    return pl.pallas_call(
        body, name="gdn_fwd", grid=(t // rows,),
        in_specs=[hm] * 6 + [pl.BlockSpec((HEADS, cpb, 1, GDN_CHUNK), lambda i: (0, i, 0, 0))],
        out_specs=[hm, per_chunk, per_chunk],
        out_shape=[jax.ShapeDtypeStruct((HEADS, t, HEAD_DIM), F32), chunk_shape, chunk_shape],
        scratch_shapes=[pltpu.VMEM((HEADS, GDN_CHUNK, HEAD_DIM), F32)],
        compiler_params=_params(("arbitrary",)),
    )(q, k, v, beta, g, gl, grow)


def _gdn_bwd(q, k, v, beta, g, gl, grow, states, invs, do, cpb=1):
    t = q.shape[1]
    rows = cpb * GDN_CHUNK
    nsteps = t // rows

    def body(q_ref, k_ref, v_ref, b_ref, g_ref, gl_ref, grow_ref, st_ref, inv_ref, do_ref,
             dq_ref, dk_ref, dv_ref, db_ref, dg_ref, dgl_ref, dgrow_ref, dstate):
        @pl.when(pl.program_id(0) == 0)
        def _():
            dstate[...] = jnp.zeros_like(dstate)

        ds = dstate[...]
        for cc in reversed(range(cpb)):
            sl = slice(cc * GDN_CHUNK, (cc + 1) * GDN_CHUNK)
            g2 = jnp.broadcast_to(grow_ref[:, cc], (HEADS, GDN_CHUNK, GDN_CHUNK))
            _, vjp = jax.vjp(_gdn_chunk, q_ref[:, sl, :], k_ref[:, sl, :], v_ref[:, sl, :], b_ref[:, sl, :],
                             g_ref[:, sl, :], g2, gl_ref[:, sl, :], st_ref[:, cc], inv_ref[:, cc])
            gq, gk, gv, gb, gg1, gg2, ggl, ds, _ = vjp((do_ref[:, sl, :], ds))
            dq_ref[:, sl, :] = gq
            dk_ref[:, sl, :] = gk
            dv_ref[:, sl, :] = gv
            db_ref[:, sl, :] = gb
            dg_ref[:, sl, :] = gg1
            dgl_ref[:, sl, :] = ggl
            dgrow_ref[:, cc] = jnp.sum(gg2, axis=1, keepdims=True)
        dstate[...] = ds

    hm = pl.BlockSpec((HEADS, rows, HEAD_DIM), lambda i: (0, nsteps - 1 - i, 0))
    rowspec = pl.BlockSpec((HEADS, cpb, 1, GDN_CHUNK), lambda i: (0, nsteps - 1 - i, 0, 0))
    per_chunk = pl.BlockSpec((HEADS, cpb, GDN_CHUNK, HEAD_DIM), lambda i: (0, nsteps - 1 - i, 0, 0))
    hm_shape = jax.ShapeDtypeStruct((HEADS, t, HEAD_DIM), F32)
    return pl.pallas_call(
        body, name="gdn_bwd", grid=(nsteps,),
        in_specs=[hm] * 6 + [rowspec, per_chunk, per_chunk, hm],
        out_specs=[hm] * 6 + [rowspec],
        out_shape=[hm_shape] * 6 + [jax.ShapeDtypeStruct((HEADS, t // GDN_CHUNK, 1, GDN_CHUNK), F32)],
        scratch_shapes=[pltpu.VMEM((HEADS, GDN_CHUNK, HEAD_DIM), F32)],
        compiler_params=_params(("arbitrary",)),
    )(q, k, v, beta, g, gl, grow, states, invs, do)


def _gdn_out(o_hm, gdn_norm_w, proj, tm=256):
    t = proj.shape[0]

    def body(o_ref, w_ref, z_ref, oz_ref):
        z = z_ref[...]
        gate = z * _sigmoid(z)
        w = w_ref[...]
        for h in range(HEADS):
            sl = slice(h * HEAD_DIM, (h + 1) * HEAD_DIM)
            o = o_ref[h]
            r = lax.rsqrt(jnp.mean(o * o, axis=1, keepdims=True) + NORM_EPS)
            oz_ref[:, sl] = (o * r * w * gate[:, sl]).astype(oz_ref.dtype)

    tok = pl.BlockSpec((tm, WIDTH), lambda i: (i, 0))
    return pl.pallas_call(
        body, name="gdn_out", grid=(t // tm,),
        in_specs=[pl.BlockSpec((HEADS, tm, HEAD_DIM), lambda i: (0, i, 0)), pl.BlockSpec((1, HEAD_DIM), lambda i: (0, 0)),
                  pl.BlockSpec((tm, WIDTH), lambda i: (i, SEG_ZB // WIDTH))],
        out_specs=tok, out_shape=jax.ShapeDtypeStruct((t, WIDTH), MXU_DTYPE),
        compiler_params=_params(("parallel",)),
    )(o_hm, gdn_norm_w, proj)


def _gdn_out_bwd(dproj, d_oz, o_hm, gdn_norm_w, proj, tm=256):
    t = proj.shape[0]

    def body(dp_ref, doz_ref, o_ref, w_ref, z_ref, dz_ref, do_ref, dw_ref):
        z = z_ref[...]
        sg = _sigmoid(z)
        gate = z * sg
        dgate = sg * (1.0 + z * (1.0 - sg))
        w = w_ref[...]
        dw = jnp.zeros((1, HEAD_DIM), F32)
        for h in range(HEADS):
            sl = slice(h * HEAD_DIM, (h + 1) * HEAD_DIM)
            o = o_ref[h]
            g = doz_ref[:, sl]
            r = lax.rsqrt(jnp.mean(o * o, axis=1, keepdims=True) + NORM_EPS)
            on = o * r * w
            dz_ref[:, sl] = (g * on * dgate[:, sl]).astype(dz_ref.dtype)
            dn = g * gate[:, sl]
            dw += jnp.sum(dn * o * r, axis=0, keepdims=True)
            dnw = dn * w
            do_ref[h] = r * dnw - o * (r * r * r) * jnp.mean(dnw * o, axis=1, keepdims=True)

        @pl.when(pl.program_id(0) == 0)
        def _():
            dw_ref[...] = jnp.zeros_like(dw_ref)

        dw_ref[...] += jnp.concatenate([dw, jnp.zeros((7, HEAD_DIM), F32)], axis=0)

    tok = pl.BlockSpec((tm, WIDTH), lambda i: (i, 0))
    seg = pl.BlockSpec((tm, WIDTH), lambda i: (i, SEG_ZB // WIDTH))
    hm = pl.BlockSpec((HEADS, tm, HEAD_DIM), lambda i: (0, i, 0))
    return pl.pallas_call(
        body, name="gdn_out_bwd", grid=(t // tm,),
        in_specs=[pl.BlockSpec(memory_space=pl.ANY), tok, hm, pl.BlockSpec((1, HEAD_DIM), lambda i: (0, 0)), seg],
        out_specs=[seg, hm, pl.BlockSpec((8, HEAD_DIM), lambda i: (0, 0))],
        out_shape=[jax.ShapeDtypeStruct((t, PACKED_WIDTH), MXU_DTYPE), jax.ShapeDtypeStruct((HEADS, t, HEAD_DIM), F32),
                   jax.ShapeDtypeStruct((8, HEAD_DIM), F32)],
        input_output_aliases={0: 0},
        compiler_params=_params(("arbitrary",)),
    )(dproj, d_oz, o_hm, gdn_norm_w, proj)


def _merge(y_a, y_b, proj, tm=256):
    t = proj.shape[0]

    def body(ya_ref, yb_ref, ga_ref, gb_ref, m_ref):
        m_ref[...] = (_sigmoid(ga_ref[...]) * ya_ref[...] + _sigmoid(gb_ref[...]) * yb_ref[...]).astype(m_ref.dtype)

    half = pl.BlockSpec((tm, WIDTH), lambda i, c: (i, c))
    return pl.pallas_call(
        body, name="merge", grid=(t // tm, 2),
        in_specs=[half, half, pl.BlockSpec((tm, WIDTH), lambda i, c: (i, SEG_GA // WIDTH + c)),
                  pl.BlockSpec((tm, WIDTH), lambda i, c: (i, SEG_GB // WIDTH + c))],
        out_specs=half, out_shape=jax.ShapeDtypeStruct((t, D_MODEL), MXU_DTYPE),
        compiler_params=_params(("parallel", "parallel")),
    )(y_a, y_b, proj, proj)


def _merge_bwd(dproj, d_m, y, proj, seg, name, tm=256):
    t = proj.shape[0]

    def body(*refs):
        dm_ref, y_ref, g_ref, dg_ref, dy_ref = refs[-5:]
        dm = dm_ref[...]
        s = _sigmoid(g_ref[...])
        dy_ref[...] = (dm * s).astype(dy_ref.dtype)
        dg_ref[...] = (dm * y_ref[...] * s * (1.0 - s)).astype(dg_ref.dtype)

    half = pl.BlockSpec((tm, WIDTH), lambda i, c: (i, c))
    gate = pl.BlockSpec((tm, WIDTH), lambda i, c: (i, seg // WIDTH + c))
    specs, args, aliases = [half, half, gate], [d_m, y, proj], {}
    if dproj is not None:
        specs, args, aliases = [pl.BlockSpec(memory_space=pl.ANY)] + specs, [dproj] + args, {0: 0}
    return pl.pallas_call(
        body, name=name, grid=(t // tm, 2), in_specs=specs, out_specs=[gate, half],
        out_shape=[jax.ShapeDtypeStruct((t, PACKED_WIDTH), MXU_DTYPE), jax.ShapeDtypeStruct((t, D_MODEL), MXU_DTYPE)],
        input_output_aliases=aliases,
        compiler_params=_params(("parallel", "parallel")),
    )(*args)


def _tail(x, mo, final_w, target, tm=256):
    t = x.shape[0]

    def body(x_ref, mo_ref, w_ref, t_ref, dxm_ref, dx_ref, loss_ref, dw_ref):
        x2 = x_ref[...] + mo_ref[...]
        w = w_ref[...]
        r = lax.rsqrt(jnp.mean(x2 * x2, axis=-1, keepdims=True) + NORM_EPS)
        xn = x2 * r
        err = xn * w - t_ref[...]
        dy = err * (1.0 / D_MODEL)
        dyw = dy * w
        dx2 = r * dyw - x2 * (r * r * r) * jnp.mean(dyw * x2, axis=-1, keepdims=True)
        dx_ref[...] = dx2
        dxm_ref[...] = dx2.astype(dxm_ref.dtype)
        loss = 0.5 * jnp.sum(jnp.sum(err * err, axis=-1, keepdims=True) * (1.0 / D_MODEL), axis=0, keepdims=True)
        onehot = jnp.where((_iota((8, 128), 0) == 0) & (_iota((8, 128), 1) == 0), 1.0, 0.0)

        @pl.when(pl.program_id(0) == 0)
        def _():
            loss_ref[...] = jnp.zeros_like(loss_ref)
            dw_ref[...] = jnp.zeros_like(dw_ref)

        loss_ref[...] += loss * onehot
        dw_ref[...] += jnp.where(_iota((8, D_MODEL), 0) == 0, jnp.sum(dy * xn, axis=0, keepdims=True), 0.0)

    tok = pl.BlockSpec((tm, D_MODEL), lambda i: (i, 0))
    return pl.pallas_call(
        body, name="tail", grid=(t // tm,),
        in_specs=[tok, tok, pl.BlockSpec((1, D_MODEL), lambda i: (0, 0)), tok],
        out_specs=[tok, tok, pl.BlockSpec((8, 128), lambda i: (0, 0)), pl.BlockSpec((8, D_MODEL), lambda i: (0, 0))],
        out_shape=[jax.ShapeDtypeStruct((t, D_MODEL), MXU_DTYPE), jax.ShapeDtypeStruct((t, D_MODEL), F32),
                   jax.ShapeDtypeStruct((8, 128), F32), jax.ShapeDtypeStruct((8, D_MODEL), F32)],
        compiler_params=_params(("arbitrary",)),
    )(x, mo, final_w, target)


def _norm_bwd(x, norm_w, dh, dx2, tm=256):
    t = x.shape[0]

    def body(x_ref, w_ref, dh_ref, dx2_ref, dx_ref, dw_ref):
        xf, w, dh_ = x_ref[...], w_ref[...], dh_ref[...]
        r = lax.rsqrt(jnp.mean(xf * xf, axis=-1, keepdims=True) + NORM_EPS)
        dhw = dh_ * w
        dx_ref[...] = dx2_ref[...] + r * dhw - xf * (r * r * r) * jnp.mean(dhw * xf, axis=-1, keepdims=True)

        @pl.when(pl.program_id(0) == 0)
        def _():
            dw_ref[...] = jnp.zeros_like(dw_ref)

        dw_ref[...] += jnp.where(_iota((8, D_MODEL), 0) == 0, jnp.sum(dh_ * xf * r, axis=0, keepdims=True), 0.0)

    tok = pl.BlockSpec((tm, D_MODEL), lambda i: (i, 0))
    return pl.pallas_call(
        body, name="norm_bwd", grid=(t // tm,),
        in_specs=[tok, pl.BlockSpec((1, D_MODEL), lambda i: (0, 0)), tok, tok],
        out_specs=[tok, pl.BlockSpec((8, D_MODEL), lambda i: (0, 0))],
        out_shape=[jax.ShapeDtypeStruct((t, D_MODEL), F32), jax.ShapeDtypeStruct((8, D_MODEL), F32)],
        compiler_params=_params(("arbitrary",)),
    )(x, norm_w, dh, dx2)


def _to_strided(a, d):
    return a if d == 1 else a.reshape(a.shape[0] // d, d * a.shape[1])


def _from_strided(a, d):
    return a if d == 1 else a.reshape(a.shape[0] * d, a.shape[1] // d)


def _local_step(x, target, norm_w, wp, conv_w, a_log, dt_bias, gdn_norm_w, w_up_a, w_up_b, w_out, final_w):
    t = x.shape[0]
    tables = _rope_tables(t)
    a_log = jnp.pad(a_log, ((0, 0), (HEADS, 128 - 2 * HEADS)))
    dt_bias = jnp.pad(dt_bias, ((0, 0), (HEADS, 128 - 2 * HEADS)))

    proj, h = _norm_proj(x, norm_w, wp)
    qkv = _rope_fwd(proj, tables)
    att_in, outs, lses = [], [], []
    for gi, d in enumerate(DILATIONS):
        a_in = qkv if d == 1 else _to_strided(qkv[:, gi * 3 * WIDTH:(gi + 1) * 3 * WIDTH], d)
        o_g, lse_g = _att_fwd(a_in, d, f"att_fwd{gi}")
        att_in.append(a_in)
        outs.append(_from_strided(o_g, d))
        lses.append(_from_strided(lse_g, d))
    oz_a, o_a, lse = _att_merge(outs, lses, proj)
    conv = _conv_fwd(proj, conv_w)
    gq, gk, gv, gb, gg, ggl, grow = _gdn_prep(conv, proj, a_log, dt_bias)
    o_b, states, invs = _gdn_fwd(gq, gk, gv, gb, gg, ggl, grow)
    oz_b = _gdn_out(o_b, gdn_norm_w, proj)
    big = dict(tm=1024, tn=1024, tk=1024)
    y_a = _matmul(oz_a, w_up_a, "nn", "up_a", **big)
    y_b = _matmul(oz_b, w_up_b, "nn", "up_b", **big)
    merged = _merge(y_a, y_b, proj)
    mo = _matmul(merged, w_out, "nn", "out_proj", **big)
    dx2_m, dx2, loss_blk, d_final = _tail(x, mo, final_w, target)

    d_wout = _matmul(merged, dx2_m, "tn", "d_w_out", **big)
    d_m = _matmul(dx2_m, w_out, "nt", "d_merged", **big)
    dproj, dy_a = _merge_bwd(None, d_m, y_a, proj, SEG_GA, "merge_bwd_a")
    dproj, dy_b = _merge_bwd(dproj, d_m, y_b, proj, SEG_GB, "merge_bwd_b")
    d_wua = _matmul(oz_a, dy_a, "tn", "d_w_up_a", **big)
    d_wub = _matmul(oz_b, dy_b, "tn", "d_w_up_b", **big)
    d_oz_a = _matmul(dy_a, w_up_a, "nt", "d_oz_a", **big)
    d_oz_b = _matmul(dy_b, w_up_b, "nt", "d_oz_b", **big)
    dproj, do_a, delta = _att_merge_bwd(dproj, d_oz_a, o_a, proj)
    for gi, d in enumerate(DILATIONS):
        dqkv = _att_bwd(att_in[gi], _to_strided(do_a, d), _to_strided(lse, d), _to_strided(delta, d), d, f"att_bwd{gi}")
        dproj = _rope_bwd(dproj, _from_strided(dqkv, d), gi, tables)
    dproj, do_b, d_gnw = _gdn_out_bwd(dproj, d_oz_b, o_b, gdn_norm_w, proj)
    dgq, dgk, dgv, dgb, dgg, dggl, dgrow = _gdn_bwd(gq, gk, gv, gb, gg, ggl, grow, states, invs, do_b)
    dproj, dconv, d_small = _gdn_prep_bwd(dproj, conv, proj, a_log, dt_bias, dgq, dgk, dgv, dgb, dgg, dggl, dgrow)
    dproj, d_convw = _conv_bwd(dproj, dconv, proj, conv_w)
    d_wp = _matmul(h, dproj, "tn", "d_w_in", tm=1024, tn=512, tk=1024)
    dh = _matmul(dproj, wp, "nt", "d_h", tm=1024, tn=1024, tk=512)
    grad_x, d_norm = _norm_bwd(x, norm_w, dh, dx2)
    return dict(loss=loss_blk, grad_x=grad_x, norm_w=d_norm[0:1], w_in=d_wp, conv_w=d_convw[0:GDN_CONV],
                a_log=d_small[0:1, HEADS:2 * HEADS], dt_bias=d_small[1:2, HEADS:2 * HEADS], gdn_norm_w=d_gnw[0:1],
                w_up_a=d_wua, w_up_b=d_wub,
                w_out=d_wout, final_norm_w=d_final[0:1])


SHARDS = 4
W_IN_SHARD = IN_WIDTH // SHARDS
ROWS_W_IN = D_MODEL * W_IN_SHARD // 128
ROWS_UP = WIDTH * (D_MODEL // SHARDS) // 128
ROWS_OUT = (D_MODEL // SHARDS) * D_MODEL // 128
CONV_SHARD = 3 * WIDTH // SHARDS
ROWS_CONV = 16
SLAB_ROWS = ROWS_W_IN + 2 * ROWS_UP + ROWS_OUT + 2 * ROWS_CONV
HALF_ROWS = SLAB_ROWS // 2
MESH = pl.DeviceIdType.MESH
ANY = pl.BlockSpec(memory_space=pl.ANY)


def _pad_rows(a, rows):
    return jnp.pad(a, ((0, rows - a.shape[0]), (0, 0)))


def _pack_slab(w_in, w_up_a, w_up_b, w_out, conv, conv_lo):
    parts = [w_in.reshape(ROWS_W_IN, 128), w_up_a.reshape(ROWS_UP, 128), w_up_b.reshape(ROWS_UP, 128),
             w_out.reshape(ROWS_OUT, 128), _pad_rows(conv.reshape(-1, 128), ROWS_CONV), _pad_rows(conv_lo.reshape(-1, 128), ROWS_CONV)]
    return jnp.concatenate(parts, axis=0)


def _unpack_slab(slab):
    r0 = 0
    out = []
    for rows, shape in ((ROWS_W_IN, (D_MODEL, W_IN_SHARD)), (ROWS_UP, (WIDTH, D_MODEL // SHARDS)), (ROWS_UP, (WIDTH, D_MODEL // SHARDS)),
                        (ROWS_OUT, (D_MODEL // SHARDS, D_MODEL)), (ROWS_CONV, None), (ROWS_CONV, None)):
        part = slab[r0:r0 + rows]
        out.append(part[:GDN_CONV * CONV_SHARD // 128].reshape(GDN_CONV, CONV_SHARD) if shape is None else part.reshape(shape))
        r0 += rows
    return out


def _mesh_position():
    x, y, c = lax.axis_index("x"), lax.axis_index("y"), lax.axis_index("c")
    return x, y, c, [(1 - x, y), (x, 1 - y), (1 - x, 1 - y)]


def _gather_weights(slab):
    def body(slab_ref, out_ref, send_sems, recv_sems, local_sem):
        x, y, c, chips = _mesh_position()
        mine = pltpu.make_async_copy(slab_ref, out_ref.at[2 * x + y], local_sem)
        mine.start()

        def half(chip, which):
            return out_ref.at[2 * chip[0] + chip[1], pl.ds(which * HALF_ROWS, HALF_ROWS), :]

        def copy(k, src, dst, to):
            return pltpu.make_async_remote_copy(src_ref=src, dst_ref=dst, send_sem=send_sems.at[k], recv_sem=recv_sems.at[k],
                                                device_id=to, device_id_type=MESH)

        first = [copy(j, slab_ref.at[pl.ds(c * HALF_ROWS, HALF_ROWS), :], half((x, y), c), (*chip, c)) for j, chip in enumerate(chips)]
        for cp in first:
            cp.start()
        passed = [copy(3 + j, half(chip, c), half(chip, c), (x, y, 1 - c)) for j, chip in enumerate(chips)]
        for j, chip in enumerate(chips):
            copy(j, half(chip, c), half(chip, c), (x, y, c)).wait_recv()
            passed[j].start()
        for j, chip in enumerate(chips):
            copy(3 + j, half(chip, 1 - c), half(chip, 1 - c), (x, y, c)).wait_recv()
        for cp in first + passed:
            cp.wait_send()
        mine.wait()

    return pl.pallas_call(
        body, name="gather_weights", in_specs=[ANY], out_specs=ANY,
        out_shape=jax.ShapeDtypeStruct((SHARDS, SLAB_ROWS, 128), slab.dtype),
        scratch_shapes=[pltpu.SemaphoreType.DMA((6,)), pltpu.SemaphoreType.DMA((6,)), pltpu.SemaphoreType.DMA],    )(slab)


def _exchange_halves(grads):
    def body(g_ref, out_ref, send_sems, recv_sems):
        x, y, c, _ = _mesh_position()
        copies = [pltpu.make_async_remote_copy(src_ref=g_ref.at[s, pl.ds((1 - c) * HALF_ROWS, HALF_ROWS), :], dst_ref=out_ref.at[s],
                                               send_sem=send_sems.at[s], recv_sem=recv_sems.at[s],
                                               device_id=(x, y, 1 - c), device_id_type=MESH) for s in range(SHARDS)]
        for cp in copies:
            cp.start()
        for cp in copies:
            cp.wait()

    return pl.pallas_call(
        body, name="exchange_halves", in_specs=[ANY], out_specs=ANY,
        out_shape=jax.ShapeDtypeStruct((SHARDS, HALF_ROWS, 128), F32),
        scratch_shapes=[pltpu.SemaphoreType.DMA((SHARDS,)), pltpu.SemaphoreType.DMA((SHARDS,))],    )(grads)


SUM_ROWS = HALF_ROWS // 2


def _pair_sum(grads, recv):
    nblk = HALF_ROWS // SUM_ROWS

    def body(c_ref, g_ref, r_ref, o_ref):
        o_ref[...] = (g_ref[...] + r_ref[...]).astype(o_ref.dtype)

    spec = pl.BlockSpec((1, SUM_ROWS, 128), lambda s, i, c_ref: (s, i, 0))
    return pl.pallas_call(
        body, name="pair_sum",
        grid_spec=pltpu.PrefetchScalarGridSpec(
            num_scalar_prefetch=1, grid=(SHARDS, nblk),
            in_specs=[pl.BlockSpec((1, SUM_ROWS, 128), lambda s, i, c_ref: (s, c_ref[0] * nblk + i, 0)), spec],
            out_specs=spec),
        out_shape=jax.ShapeDtypeStruct((SHARDS, HALF_ROWS, 128), MXU_DTYPE),
        compiler_params=_params(("parallel", "parallel")),
    )(lax.axis_index("c").astype(jnp.int32).reshape(1), grads, recv)


def _scatter_pairs(pairs):
    def body(p_ref, out_ref, send_sems, recv_sems):
        x, y, c, chips = _mesh_position()
        copies = [pltpu.make_async_remote_copy(src_ref=p_ref.at[2 * chip[0] + chip[1]], dst_ref=out_ref.at[j],
                                               send_sem=send_sems.at[j], recv_sem=recv_sems.at[j],
                                               device_id=(*chip, c), device_id_type=MESH) for j, chip in enumerate(chips)]
        for cp in copies:
            cp.start()
        for cp in copies:
            cp.wait()

    return pl.pallas_call(
        body, name="scatter_pairs", in_specs=[ANY], out_specs=ANY,
        out_shape=jax.ShapeDtypeStruct((3, HALF_ROWS, 128), pairs.dtype),
        scratch_shapes=[pltpu.SemaphoreType.DMA((3,)), pltpu.SemaphoreType.DMA((3,))],    )(pairs)


def _chip_sum(pairs, recv):
    def body(own_ref, p_ref, r_ref, o_ref):
        o_ref[...] = ((p_ref[0].astype(F32) + r_ref[0].astype(F32)) + r_ref[1].astype(F32)) + r_ref[2].astype(F32)

    own = (2 * lax.axis_index("x") + lax.axis_index("y")).astype(jnp.int32).reshape(1)
    return pl.pallas_call(
        body, name="chip_sum",
        grid_spec=pltpu.PrefetchScalarGridSpec(
            num_scalar_prefetch=1, grid=(HALF_ROWS // SUM_ROWS,),
            in_specs=[pl.BlockSpec((1, SUM_ROWS, 128), lambda i, own_ref: (own_ref[0], i, 0)),
                      pl.BlockSpec((3, SUM_ROWS, 128), lambda i, own_ref: (0, i, 0))],
            out_specs=pl.BlockSpec((SUM_ROWS, 128), lambda i, own_ref: (i, 0))),
        out_shape=jax.ShapeDtypeStruct((HALF_ROWS, 128), F32),
        compiler_params=_params(("parallel",)),
    )(own, pairs, recv)


def _share_total(total_half):
    def body(t_ref, out_ref, send_sem, recv_sem, local_sem):
        x, y, c, _ = _mesh_position()
        mine = pltpu.make_async_copy(t_ref, out_ref.at[pl.ds(c * HALF_ROWS, HALF_ROWS), :], local_sem)
        mine.start()
        cp = pltpu.make_async_remote_copy(src_ref=t_ref, dst_ref=out_ref.at[pl.ds(c * HALF_ROWS, HALF_ROWS), :],
                                          send_sem=send_sem, recv_sem=recv_sem, device_id=(x, y, 1 - c), device_id_type=MESH)
        cp.start()
        other = out_ref.at[pl.ds((1 - c) * HALF_ROWS, HALF_ROWS), :]
        pltpu.make_async_remote_copy(src_ref=other, dst_ref=other, send_sem=send_sem, recv_sem=recv_sem,
                                     device_id=(x, y, c), device_id_type=MESH).wait_recv()
        cp.wait_send()
        mine.wait()

    return pl.pallas_call(
        body, name="share_total", in_specs=[ANY], out_specs=ANY,
        out_shape=jax.ShapeDtypeStruct((SLAB_ROWS, 128), F32),
        scratch_shapes=[pltpu.SemaphoreType.DMA, pltpu.SemaphoreType.DMA, pltpu.SemaphoreType.DMA],    )(total_half)


def _allreduce_small(block):
    def body(b_ref, out_ref, gath, send_sems, recv_sems):
        x, y, c, _ = _mesh_position()
        me = 4 * x + 2 * y + c
        gath[me] = b_ref[...]
        copies = []
        for k in range(1, 8):
            peer = (x ^ (k >> 2), y ^ ((k >> 1) & 1), c ^ (k & 1))
            copies.append(pltpu.make_async_remote_copy(src_ref=b_ref, dst_ref=gath.at[me], send_sem=send_sems.at[k - 1],
                                                       recv_sem=recv_sems.at[k - 1], device_id=peer, device_id_type=MESH))
        for cp in copies:
            cp.start()
        for k in range(1, 8):
            src = 4 * (x ^ (k >> 2)) + 2 * (y ^ ((k >> 1) & 1)) + (c ^ (k & 1))
            pltpu.make_async_remote_copy(src_ref=b_ref, dst_ref=gath.at[src], send_sem=send_sems.at[k - 1],
                                         recv_sem=recv_sems.at[k - 1], device_id=(x, y, c), device_id_type=MESH).wait_recv()
        for cp in copies:
            cp.wait_send()
        acc = gath[0]
        for d in range(1, 8):
            acc = acc + gath[d]
        out_ref[...] = acc

    vm = pl.BlockSpec(memory_space=pltpu.VMEM)
    return pl.pallas_call(
        body, name="allreduce_small", in_specs=[vm], out_specs=vm,
        out_shape=jax.ShapeDtypeStruct((8, D_MODEL), F32),
        scratch_shapes=[pltpu.VMEM((8, 8, D_MODEL), F32), pltpu.SemaphoreType.DMA((7,)), pltpu.SemaphoreType.DMA((7,))],
    )(block)


def _adamw(w, g, m, v, name):
    rows, cols = w.shape
    tr = 128 if rows % 128 == 0 else rows

    def body(w_ref, g_ref, m_ref, v_ref, d_ref, nm_ref, nv_ref):
        gv = g_ref[...]
        nm = ADAM_B1 * m_ref[...] + (1.0 - ADAM_B1) * gv
        nv = ADAM_B2 * v_ref[...] + (1.0 - ADAM_B2) * (gv * gv)
        m_hat = nm / (1.0 - ADAM_B1 ** ADAM_STEP)
        v_hat = nv / (1.0 - ADAM_B2 ** ADAM_STEP)
        d_ref[...] = -ADAM_LR * (m_hat / (jnp.sqrt(v_hat) + ADAM_EPS) + ADAM_WD * w_ref[...])
        nm_ref[...] = nm
        nv_ref[...] = nv

    spec = pl.BlockSpec((tr, cols), lambda i: (i, 0))
    shape = jax.ShapeDtypeStruct((rows, cols), F32)
    return pl.pallas_call(
        body, name=name, grid=(rows // tr,), in_specs=[spec] * 4, out_specs=[spec] * 3, out_shape=[shape] * 3,
        compiler_params=_params(("parallel",)),
    )(w, g, m, v)


def _pack_w_in(w):
    return jnp.concatenate([w[:, :BA_END], jnp.zeros((D_MODEL, SEG_GA - BA_END), w.dtype), w[:, BA_END:]], axis=1)


def kernel(x, norm_w, w_in, conv_w, a_log, dt_bias, gdn_norm_w, w_up_a, w_up_b, w_out, final_norm_w, loss_target, m_norm_w, m_w_in, m_conv_w, m_a_log, m_dt_bias, m_gdn_norm_w, m_w_up_a, m_w_up_b, m_w_out, m_final_norm_w, v_norm_w, v_w_in, v_conv_w, v_a_log, v_dt_bias, v_gdn_norm_w, v_w_up_a, v_w_up_b, v_w_out, v_final_norm_w):
    conv_hi = conv_w[0].astype(MXU_DTYPE)
    conv_lo = (conv_w[0] - conv_hi.astype(F32)).astype(MXU_DTYPE)
    slab = _pack_slab(w_in[0].astype(MXU_DTYPE), w_up_a[0].astype(MXU_DTYPE), w_up_b[0].astype(MXU_DTYPE),
                      w_out[0].astype(MXU_DTYPE), conv_hi, conv_lo)
    slabs = _gather_weights(slab)
    parts = [_unpack_slab(slabs[s]) for s in range(SHARDS)]
    w_in_full = jnp.concatenate([p[0] for p in parts], axis=1)
    w_up_a_full = jnp.concatenate([p[1] for p in parts], axis=1)
    w_up_b_full = jnp.concatenate([p[2] for p in parts], axis=1)
    w_out_full = jnp.concatenate([p[3] for p in parts], axis=0)
    conv_full = jnp.concatenate([p[4].astype(F32) + p[5].astype(F32) for p in parts], axis=1)

    g = _local_step(x[0], loss_target[0], norm_w, _pack_w_in(w_in_full), conv_full, a_log, dt_bias, gdn_norm_w,
                    w_up_a_full, w_up_b_full, w_out_full, final_norm_w[None])

    d_w_in = jnp.concatenate([g["w_in"][:, :BA_END], g["w_in"][:, SEG_GA:]], axis=1)
    zero_conv = jnp.zeros((GDN_CONV, CONV_SHARD), F32)
    grads = jnp.stack([
        _pack_slab(d_w_in[:, s * W_IN_SHARD:(s + 1) * W_IN_SHARD], g["w_up_a"][:, s * 256:(s + 1) * 256],
                   g["w_up_b"][:, s * 256:(s + 1) * 256], g["w_out"][s * 256:(s + 1) * 256],
                   g["conv_w"][:, s * CONV_SHARD:(s + 1) * CONV_SHARD], zero_conv) for s in range(SHARDS)])
    from_sibling = _exchange_halves(grads)
    pairs = _pair_sum(grads, from_sibling)
    from_chips = _scatter_pairs(pairs)
    total = _share_total(_chip_sum(pairs, from_chips))
    g_w_in, g_w_up_a, g_w_up_b, g_w_out, g_conv, _ = _unpack_slab(total)

    row2 = jnp.concatenate([g["gdn_norm_w"], g["a_log"], g["dt_bias"], g["loss"][0:1, 0:1],
                            jnp.zeros((1, D_MODEL - HEAD_DIM - 2 * HEADS - 1), F32)], axis=1)
    small = _allreduce_small(jnp.concatenate([g["norm_w"], g["final_norm_w"], row2, jnp.zeros((5, D_MODEL), F32)], axis=0))
    g_norm, g_final = small[0:1], small[1]
    g_gnw, g_alog, g_dt = small[2:3, 0:HEAD_DIM], small[2:3, HEAD_DIM:HEAD_DIM + HEADS], small[2:3, HEAD_DIM + HEADS:HEAD_DIM + 2 * HEADS]
    loss = small[2, HEAD_DIM + 2 * HEADS]

    names = ["norm_w", "w_in", "conv_w", "a_log", "dt_bias", "gdn_norm_w", "w_up_a", "w_up_b", "w_out", "final_norm_w"]
    weights = dict(zip(names, (norm_w, w_in, conv_w, a_log, dt_bias, gdn_norm_w, w_up_a, w_up_b, w_out, final_norm_w)))
    ms = dict(zip(names, (m_norm_w, m_w_in, m_conv_w, m_a_log, m_dt_bias, m_gdn_norm_w, m_w_up_a, m_w_up_b, m_w_out, m_final_norm_w)))
    vs = dict(zip(names, (v_norm_w, v_w_in, v_conv_w, v_a_log, v_dt_bias, v_gdn_norm_w, v_w_up_a, v_w_up_b, v_w_out, v_final_norm_w)))
    grads2d = dict(norm_w=g_norm, w_in=g_w_in, conv_w=g_conv, a_log=g_alog, dt_bias=g_dt, gdn_norm_w=g_gnw,
                   w_up_a=g_w_up_a, w_up_b=g_w_up_b, w_out=g_w_out, final_norm_w=g_final[None])
    grad_out, delta, new_m, new_v = [], [], [], []
    for n in names:
        shape = weights[n].shape
        two_d = grads2d[n].shape
        d, nm, nv = _adamw(weights[n].reshape(two_d), grads2d[n], ms[n].reshape(two_d), vs[n].reshape(two_d), f"adamw_{n}")
        grad_out.append(grads2d[n].reshape(shape))
        delta.append(d.reshape(shape))
        new_m.append(nm.reshape(shape))
        new_v.append(nv.reshape(shape))
    return (loss, g["grad_x"][None], *grad_out, *delta, *new_m, *new_v)
```

```python
import functools

import jax
import jax.numpy as jnp
from jax import lax
from jax.experimental import pallas as pl
from jax.experimental.pallas import tpu as pltpu

F32 = jnp.float32
MXU_DTYPE = jnp.bfloat16
HIGHEST = lax.Precision.HIGHEST

D_MODEL = 1024
HEADS = 8
HEAD_DIM = 64
WIDTH = HEADS * HEAD_DIM
NORM_EPS = 1e-6
ROPE_THETA = 10000.0
ATT_BLOCK = 128
DILATIONS = (1, 4, 16)
GDN_CHUNK = 64
GDN_CONV = 4
IN_WIDTH = 9232
SEG_A, SEG_ZA, SEG_B, SEG_ZB, SEG_BA, SEG_GA, SEG_GB, PACKED_WIDTH = 0, 4608, 5120, 6656, 7168, 7680, 8704, 9728
BA_END = 7184
VMEM_LIMIT = 56 * 1024 * 1024

ADAM_LR, ADAM_B1, ADAM_B2, ADAM_EPS, ADAM_WD, ADAM_STEP = 0.001, 0.9, 0.999, 1e-08, 0.01, 10

_NN = (((1,), (0,)), ((), ()))
_NT = (((1,), (1,)), ((), ()))
_TN = (((0,), (0,)), ((), ()))


def _params(sem):
    return pltpu.CompilerParams(dimension_semantics=sem, vmem_limit_bytes=VMEM_LIMIT)


def _mxu(a, b, dims):
    return lax.dot_general(a.astype(MXU_DTYPE), b.astype(MXU_DTYPE), dims, preferred_element_type=F32)


def _sigmoid(x):
    return 1.0 / (1.0 + jnp.exp(-x))


def _softplus(x):
    return jnp.maximum(x, 0.0) + jnp.log(1.0 + jnp.exp(-jnp.abs(x)))


def _iota(shape, axis):
    return lax.broadcasted_iota(jnp.int32, shape, axis)


def _matmul(a, b, mode, name, out_dtype=F32, tm=512, tn=512, tk=512):
    if mode == "nn":
        (m, k), (k2, n) = a.shape, b.shape
    elif mode == "nt":
        (m, k), (n, k2) = a.shape, b.shape
    else:
        (k, m), (k2, n) = a.shape, b.shape
    assert k == k2
    tm, tn, tk = min(tm, m), min(tn, n), min(tk, k)
    assert m % tm == 0 and n % tn == 0 and k % tk == 0
    nk = k // tk
    dims = {"nn": _NN, "nt": _NT, "tn": _TN}[mode]

    def body(a_ref, b_ref, o_ref, acc_ref):
        kk = pl.program_id(2)

        @pl.when(kk == 0)
        def _():
            acc_ref[...] = jnp.zeros_like(acc_ref)

        acc_ref[...] += _mxu(a_ref[...], b_ref[...], dims)

        @pl.when(kk == nk - 1)
        def _():
            o_ref[...] = acc_ref[...].astype(o_ref.dtype)

    a_spec = pl.BlockSpec((tk, tm), lambda i, j, kk: (kk, i)) if mode == "tn" else pl.BlockSpec((tm, tk), lambda i, j, kk: (i, kk))
    b_spec = pl.BlockSpec((tn, tk), lambda i, j, kk: (j, kk)) if mode == "nt" else pl.BlockSpec((tk, tn), lambda i, j, kk: (kk, j))
    return pl.pallas_call(
        body, name=name, grid=(m // tm, n // tn, nk), in_specs=[a_spec, b_spec],
        out_specs=pl.BlockSpec((tm, tn), lambda i, j, kk: (i, j)),
        out_shape=jax.ShapeDtypeStruct((m, n), out_dtype),
        scratch_shapes=[pltpu.VMEM((tm, tn), F32)],
        compiler_params=_params(("parallel", "parallel", "arbitrary")),
    )(a, b)


def _norm_proj(x, norm_w, wp, tm=1024, tn=512):
    t = x.shape[0]

    def body(x_ref, nw_ref, w_ref, proj_ref, h_ref):
        @pl.when(pl.program_id(1) == 0)
        def _():
            xf = x_ref[...]
            r = lax.rsqrt(jnp.mean(xf * xf, axis=-1, keepdims=True) + NORM_EPS)
            h_ref[...] = (xf * r * nw_ref[...]).astype(h_ref.dtype)

        proj_ref[...] = jnp.dot(h_ref[...], w_ref[...], preferred_element_type=F32)

    return pl.pallas_call(
        body, name="norm_proj", grid=(t // tm, PACKED_WIDTH // tn),
        in_specs=[pl.BlockSpec((tm, D_MODEL), lambda i, j: (i, 0)),
                  pl.BlockSpec((1, D_MODEL), lambda i, j: (0, 0)),
                  pl.BlockSpec((D_MODEL, tn), lambda i, j: (0, j))],
        out_specs=[pl.BlockSpec((tm, tn), lambda i, j: (i, j)),
                   pl.BlockSpec((tm, D_MODEL), lambda i, j: (i, 0))],
        out_shape=[jax.ShapeDtypeStruct((t, PACKED_WIDTH), F32), jax.ShapeDtypeStruct((t, D_MODEL), MXU_DTYPE)],
        compiler_params=_params(("parallel", "arbitrary")),
    )(x, norm_w, wp)


def _rope_tables(t):
    lane = jnp.arange(128)
    inv_freq = ROPE_THETA ** (-jnp.arange(0, HEAD_DIM, 2, dtype=F32) / HEAD_DIM)
    ang = jnp.arange(t, dtype=F32)[:, None] * inv_freq[None, :]
    ang = jnp.concatenate([ang, ang, ang, ang], axis=-1)
    first_half = (lane % HEAD_DIM) < HEAD_DIM // 2
    cos, sin = jnp.cos(ang), jnp.sin(ang)
    return cos, jnp.where(first_half, -sin, 0.0), jnp.where(first_half, 0.0, sin)


def _rope_block(x, cos, sin_lo, sin_hi, sign):
    outs = []
    for c in range(8):
        xc = x[:, c * 128:(c + 1) * 128]
        rot = pltpu.roll(xc, 96, 1) * sin_lo + pltpu.roll(xc, 32, 1) * sin_hi
        outs.append(xc * cos + sign * rot)
    outs.append(x[:, 2 * WIDTH:])
    return jnp.concatenate(outs, axis=1)


def _tile_scratch(tm, cols):
    return pltpu.VMEM((cols // 128, tm, 128), F32)


def _store_tile(scr, y):
    for c in range(scr.shape[0]):
        scr[c] = y[:, c * 128:(c + 1) * 128]


def _load_tile(scr):
    return jnp.concatenate([scr[c] for c in range(scr.shape[0])], axis=1)


def _to_strided_view(scr, o_ref, d):
    n, tm, _ = scr.shape
    for r in range(d):
        for c in range(n):
            o_ref[:, (r * n + c) * 128:(r * n + c + 1) * 128] = scr[c, pl.ds(r, tm // d, stride=d), :].astype(o_ref.dtype)


def _from_strided_view(i_ref, scr, d):
    n, tm, _ = scr.shape
    for r in range(d):
        for c in range(n):
            scr[c, pl.ds(r, tm // d, stride=d), :] = i_ref[:, (r * n + c) * 128:(r * n + c + 1) * 128].astype(F32)


def _strided_spec(tm, d, cols):
    return pl.BlockSpec((tm // d, d * cols), lambda i: (i, 0))


def _rope_fwd(proj, tables, tm=256):
    t = proj.shape[0]
    cols = 3 * WIDTH

    def body(x_ref, c_ref, sl_ref, sh_ref, o0, o1, o2, scr):
        for g, (d, o_ref) in enumerate(zip(DILATIONS, (o0, o1, o2))):
            y = _rope_block(x_ref[:, g * cols:(g + 1) * cols], c_ref[...], sl_ref[...], sh_ref[...], 1.0)
            if d == 1:
                o_ref[...] = y.astype(o_ref.dtype)
            else:
                _store_tile(scr, y)
                _to_strided_view(scr, o_ref, d)

    tab = pl.BlockSpec((tm, 128), lambda i: (i, 0))
    return pl.pallas_call(
        body, name="rope_fwd", grid=(t // tm,),
        in_specs=[pl.BlockSpec((tm, 3 * cols), lambda i: (i, 0)), tab, tab, tab],
        out_specs=[_strided_spec(tm, d, cols) for d in DILATIONS],
        out_shape=[jax.ShapeDtypeStruct((t // d, d * cols), MXU_DTYPE) for d in DILATIONS],
        scratch_shapes=[_tile_scratch(tm, cols)],
        compiler_params=_params(("parallel",)),
    )(proj, *tables)


def _rope_bwd(dproj, dqkvs, tables, tm=256):
    t = dproj.shape[0]
    cols = 3 * WIDTH

    def body(dp_ref, i0, i1, i2, c_ref, sl_ref, sh_ref, o_ref, scr):
        for g, (d, i_ref) in enumerate(zip(DILATIONS, (i0, i1, i2))):
            if d == 1:
                x = i_ref[...]
            else:
                _from_strided_view(i_ref, scr, d)
                x = _load_tile(scr)
            y = _rope_block(x, c_ref[...], sl_ref[...], sh_ref[...], -1.0)
            o_ref[:, g * cols:(g + 1) * cols] = y.astype(o_ref.dtype)

    tab = pl.BlockSpec((tm, 128), lambda i: (i, 0))
    return pl.pallas_call(
        body, name="rope_bwd", grid=(t // tm,),
        in_specs=[pl.BlockSpec(memory_space=pl.ANY)] + [_strided_spec(tm, d, cols) for d in DILATIONS] + [tab, tab, tab],
        out_specs=pl.BlockSpec((tm, 3 * cols), lambda i: (i, 0)),
        out_shape=jax.ShapeDtypeStruct((t, PACKED_WIDTH), MXU_DTYPE),
        scratch_shapes=[_tile_scratch(tm, cols)],
        input_output_aliases={0: 0},
        compiler_params=_params(("parallel",)),
    )(dproj, *dqkvs, *tables)


def _att_masks():
    qi = _iota((ATT_BLOCK, ATT_BLOCK), 0)
    kj = _iota((ATT_BLOCK, ATT_BLOCK), 1)
    return kj <= qi, kj >= qi


def _att_fwd(qkv, d, name):
    rows = qkv.shape[0]
    nb = rows // ATT_BLOCK
    scale = HEAD_DIM ** -0.5

    def body(q_ref, kc_ref, kp_ref, vc_ref, vp_ref, o_ref, lse_ref):
        has_prev = pl.program_id(1) > 0
        m_cur, m_prev = _att_masks()
        m_prev = m_prev & has_prev
        hs = range(HEADS)
        sls = [slice(h * HEAD_DIM, (h + 1) * HEAD_DIM) for h in hs]
        qs = [q_ref[:, sl] for sl in sls]
        s_c = [jnp.where(m_cur, _mxu(qs[h], kc_ref[:, sls[h]], _NT) * scale, -jnp.inf) for h in hs]
        s_p = [jnp.where(m_prev, _mxu(qs[h], kp_ref[:, sls[h]], _NT) * scale, -jnp.inf) for h in hs]
        m = [jnp.maximum(jnp.max(s_c[h], axis=1, keepdims=True), jnp.max(s_p[h], axis=1, keepdims=True)) for h in hs]
        p_c = [jnp.exp(s_c[h] - m[h]) for h in hs]
        p_p = [jnp.exp(s_p[h] - m[h]) for h in hs]
        den = [jnp.sum(p_c[h], axis=1, keepdims=True) + jnp.sum(p_p[h], axis=1, keepdims=True) for h in hs]
        o = [_mxu(p_c[h], vc_ref[:, sls[h]], _NN) + _mxu(p_p[h], vp_ref[:, sls[h]], _NN) for h in hs]
        for h in hs:
            o_ref[:, sls[h]] = o[h] / den[h]
            lse_ref[:, sls[h]] = jnp.broadcast_to(m[h] + jnp.log(den[h]), (ATT_BLOCK, HEAD_DIM))

    def cur(c):
        return pl.BlockSpec((ATT_BLOCK, WIDTH), lambda r, i: (i, 3 * r + c))

    def prev(c):
        return pl.BlockSpec((ATT_BLOCK, WIDTH), lambda r, i: (jnp.maximum(i - 1, 0), 3 * r + c))

    out = pl.BlockSpec((ATT_BLOCK, WIDTH), lambda r, i: (i, r))
    return pl.pallas_call(
        body, name=name, grid=(d, nb), in_specs=[cur(0), cur(1), prev(1), cur(2), prev(2)],
        out_specs=[out, out],
        out_shape=[jax.ShapeDtypeStruct((rows, d * WIDTH), F32)] * 2,
        compiler_params=_params(("parallel", "arbitrary")),
    )(qkv, qkv, qkv, qkv, qkv)


def _att_bwd(qkv, do, lse, delta, d, name):
    rows = qkv.shape[0]
    nb = rows // ATT_BLOCK
    scale = HEAD_DIM ** -0.5

    def body(q_ref, qn_ref, kc_ref, kp_ref, vc_ref, vp_ref, do_ref, don_ref, l_ref, ln_ref, dl_ref, dln_ref, o_ref):
        i = pl.program_id(1)
        m_cur, m_prev = _att_masks()
        m_p = m_prev & (i > 0)
        m_n = m_prev & (i < nb - 1)

        hs = range(HEADS)
        sls = [slice(h * HEAD_DIM, (h + 1) * HEAD_DIM) for h in hs]
        col = [slice(h * HEAD_DIM, h * HEAD_DIM + 1) for h in hs]

        def probs(q_r, k_r, lse_r, mask):
            s = [_mxu(q_r[:, sls[h]], k_r[:, sls[h]], _NT) for h in hs]
            return [jnp.where(mask, jnp.exp(s[h] * scale - lse_r[:, col[h]]), 0.0) for h in hs]

        def dscores(p, do_r, v_r, dl_r):
            dp = [_mxu(do_r[:, sls[h]], v_r[:, sls[h]], _NT) for h in hs]
            return [(p[h] * (dp[h] - dl_r[:, col[h]])).astype(MXU_DTYPE) for h in hs]

        p = probs(q_ref, kc_ref, l_ref, m_cur)
        ds = dscores(p, do_ref, vc_ref, dl_ref)
        dq = [_mxu(ds[h], kc_ref[:, sls[h]], _NN) for h in hs]
        dk = [_mxu(ds[h], q_ref[:, sls[h]], _TN) for h in hs]
        dv = [_mxu(p[h], do_ref[:, sls[h]], _TN) for h in hs]
        p = probs(q_ref, kp_ref, l_ref, m_p)
        ds = dscores(p, do_ref, vp_ref, dl_ref)
        dq = [dq[h] + _mxu(ds[h], kp_ref[:, sls[h]], _NN) for h in hs]
        p = probs(qn_ref, kc_ref, ln_ref, m_n)
        ds = dscores(p, don_ref, vc_ref, dln_ref)
        dk = [dk[h] + _mxu(ds[h], qn_ref[:, sls[h]], _TN) for h in hs]
        dv = [dv[h] + _mxu(p[h], don_ref[:, sls[h]], _TN) for h in hs]
        for h in hs:
            o_ref[:, sls[h]] = dq[h] * scale
            o_ref[:, WIDTH + h * HEAD_DIM:WIDTH + (h + 1) * HEAD_DIM] = dk[h] * scale
            o_ref[:, 2 * WIDTH + h * HEAD_DIM:2 * WIDTH + (h + 1) * HEAD_DIM] = dv[h]

    def qkv_spec(c, shift):
        def idx(r, i):
            return (jnp.clip(i + shift, 0, nb - 1), 3 * r + c)
        return pl.BlockSpec((ATT_BLOCK, WIDTH), idx)

    def tok_spec(shift):
        def idx(r, i):
            return (jnp.clip(i + shift, 0, nb - 1), r)
        return pl.BlockSpec((ATT_BLOCK, WIDTH), idx)

    return pl.pallas_call(
        body, name=name, grid=(d, nb),
        in_specs=[qkv_spec(0, 0), qkv_spec(0, 1), qkv_spec(1, 0), qkv_spec(1, -1), qkv_spec(2, 0), qkv_spec(2, -1),
                  tok_spec(0), tok_spec(1), tok_spec(0), tok_spec(1), tok_spec(0), tok_spec(1)],
        out_specs=pl.BlockSpec((ATT_BLOCK, 3 * WIDTH), lambda r, i: (i, r)),
        out_shape=jax.ShapeDtypeStruct((rows, d * 3 * WIDTH), F32),
        compiler_params=_params(("parallel", "arbitrary")),
    )(qkv, qkv, qkv, qkv, qkv, qkv, do, do, lse, lse, delta, delta)


def _att_merge(os_, lses, proj, tm=256):
    t = proj.shape[0]

    def body(o0, o1, o2, l0, l1, l2, z_ref, oz_ref, o_ref, t0, t1, t2, s_o1, s_l1, s_o2, s_l2, s_t):
        _from_strided_view(o1, s_o1, DILATIONS[1])
        _from_strided_view(l1, s_l1, DILATIONS[1])
        _from_strided_view(o2, s_o2, DILATIONS[2])
        _from_strided_view(l2, s_l2, DILATIONS[2])
        a, b, c = l0[...], _load_tile(s_l1), _load_tile(s_l2)
        m = jnp.maximum(jnp.maximum(a, b), c)
        wa, wb, wc = jnp.exp(a - m), jnp.exp(b - m), jnp.exp(c - m)
        den = wa + wb + wc
        o = (wa * o0[...] + wb * _load_tile(s_o1) + wc * _load_tile(s_o2)) / den
        z = z_ref[...]
        o_ref[...] = o
        oz_ref[...] = (o * z * _sigmoid(z)).astype(oz_ref.dtype)
        total = m + jnp.log(den)
        t0[...] = total
        _store_tile(s_t, total)
        _to_strided_view(s_t, t1, DILATIONS[1])
        _to_strided_view(s_t, t2, DILATIONS[2])

    tok = pl.BlockSpec((tm, WIDTH), lambda i: (i, 0))
    views = [_strided_spec(tm, d, WIDTH) for d in DILATIONS]
    view_shapes = [jax.ShapeDtypeStruct((t // d, d * WIDTH), F32) for d in DILATIONS]
    return pl.pallas_call(
        body, name="att_merge", grid=(t // tm,),
        in_specs=views + views + [pl.BlockSpec((tm, WIDTH), lambda i: (i, SEG_ZA // WIDTH))],
        out_specs=[tok, tok] + views,
        out_shape=[jax.ShapeDtypeStruct((t, WIDTH), MXU_DTYPE), jax.ShapeDtypeStruct((t, WIDTH), F32)] + view_shapes,
        scratch_shapes=[_tile_scratch(tm, WIDTH)] * 5,
        compiler_params=_params(("parallel",)),
    )(*os_, *lses, proj)


def _att_merge_bwd(dproj, d_oz, o, proj, tm=256):
    t = proj.shape[0]

    def body(dp_ref, doz_ref, o_ref, z_ref, dz_ref, do0, do1, do2, dl0, dl1, dl2, s_do, s_dl):
        z, ov, g = z_ref[...], o_ref[...], doz_ref[...]
        sg = _sigmoid(z)
        do = g * z * sg
        dz_ref[...] = (g * ov * sg * (1.0 + z * (1.0 - sg))).astype(dz_ref.dtype)
        do0[...] = do.astype(do0.dtype)
        _store_tile(s_do, do)
        prod = do * ov
        for h in range(HEADS):
            sl = slice(h * HEAD_DIM, (h + 1) * HEAD_DIM)
            half = slice((h % 2) * HEAD_DIM, (h % 2 + 1) * HEAD_DIM)
            s_dl[h // 2, :, half] = jnp.broadcast_to(jnp.sum(prod[:, sl], axis=1, keepdims=True), (tm, HEAD_DIM))
        dl0[...] = _load_tile(s_dl)
        for d, do_v, dl_v in ((DILATIONS[1], do1, dl1), (DILATIONS[2], do2, dl2)):
            _to_strided_view(s_do, do_v, d)
            _to_strided_view(s_dl, dl_v, d)

    tok = pl.BlockSpec((tm, WIDTH), lambda i: (i, 0))
    seg = pl.BlockSpec((tm, WIDTH), lambda i: (i, SEG_ZA // WIDTH))
    views = [_strided_spec(tm, d, WIDTH) for d in DILATIONS]
    return pl.pallas_call(
        body, name="att_merge_bwd", grid=(t // tm,),
        in_specs=[pl.BlockSpec(memory_space=pl.ANY), tok, tok, seg],
        out_specs=[seg] + views + views,
        out_shape=[jax.ShapeDtypeStruct((t, PACKED_WIDTH), MXU_DTYPE)]
        + [jax.ShapeDtypeStruct((t // d, d * WIDTH), MXU_DTYPE) for d in DILATIONS]
        + [jax.ShapeDtypeStruct((t // d, d * WIDTH), F32) for d in DILATIONS],
        scratch_shapes=[_tile_scratch(tm, WIDTH)] * 2,
        input_output_aliases={0: 0},
        compiler_params=_params(("parallel",)),
    )(dproj, d_oz, o, proj)


def _shift_down(x, halo, s):
    if s == 0:
        return x
    xs = pltpu.roll(x, s, 0)
    head = jnp.where(_iota((8, x.shape[1]), 0) < s, pltpu.roll(halo, s, 0), xs[0:8])
    return jnp.concatenate([head, xs[8:]], axis=0)


def _shift_up(x, nxt, s):
    if s == 0:
        return x
    n = x.shape[0]
    xs = pltpu.roll(x, n - s, 0)
    tail = jnp.where(_iota((8, x.shape[1]), 0) >= 8 - s, pltpu.roll(nxt, 8 - s, 0), xs[n - 8:])
    return jnp.concatenate([xs[:n - 8], tail], axis=0)


def _conv_fwd(proj, conv_w, tm=256):
    t = proj.shape[0]
    cb = SEG_B // WIDTH

    def body(x_ref, halo_ref, w_ref, c_ref):
        halo = jnp.where(pl.program_id(0) > 0, halo_ref[...], 0.0)
        x = x_ref[...]
        w = w_ref[...]
        acc = jnp.zeros((tm, WIDTH), F32)
        for j in range(GDN_CONV):
            acc += _shift_down(x, halo, GDN_CONV - 1 - j) * w[j:j + 1, :]
        c_ref[...] = acc

    return pl.pallas_call(
        body, name="conv_fwd", grid=(t // tm, 3),
        in_specs=[pl.BlockSpec((tm, WIDTH), lambda i, c: (i, cb + c)),
                  pl.BlockSpec((8, WIDTH), lambda i, c: (jnp.maximum(i * (tm // 8) - 1, 0), cb + c)),
                  pl.BlockSpec((GDN_CONV, WIDTH), lambda i, c: (0, c))],
        out_specs=pl.BlockSpec((tm, WIDTH), lambda i, c: (i, c)),
        out_shape=jax.ShapeDtypeStruct((t, 3 * WIDTH), F32),
        compiler_params=_params(("parallel", "parallel")),
    )(proj, proj, conv_w)


def _conv_bwd(dproj, dc, proj, conv_w, tm=256):
    t = proj.shape[0]
    cb = SEG_B // WIDTH
    nt = t // tm

    def body(dp_ref, dc_ref, dcn_ref, x_ref, halo_ref, w_ref, dx_ref, dw_ref):
        i = pl.program_id(1)
        w = w_ref[...]
        dcn = jnp.where(i < nt - 1, dcn_ref[...], 0.0)
        dcv = dc_ref[...]
        acc = jnp.zeros((tm, WIDTH), F32)
        for j in range(GDN_CONV):
            acc += _shift_up(dcv, dcn, GDN_CONV - 1 - j) * w[j:j + 1, :]
        dx_ref[...] = acc.astype(dx_ref.dtype)
        halo = jnp.where(i > 0, halo_ref[...], 0.0)
        x = x_ref[...]
        row8 = _iota((8, WIDTH), 0)
        part = jnp.zeros((8, WIDTH), F32)
        for j in range(GDN_CONV):
            s = jnp.sum(dcv * _shift_down(x, halo, GDN_CONV - 1 - j), axis=0, keepdims=True)
            part += jnp.where(row8 == j, s, 0.0)

        @pl.when(i == 0)
        def _():
            dw_ref[...] = jnp.zeros_like(dw_ref)

        dw_ref[...] += part

    return pl.pallas_call(
        body, name="conv_bwd", grid=(3, nt),
        in_specs=[pl.BlockSpec(memory_space=pl.ANY),
                  pl.BlockSpec((tm, WIDTH), lambda c, i: (i, c)),
                  pl.BlockSpec((8, WIDTH), lambda c, i: (jnp.minimum((i + 1) * (tm // 8), t // 8 - 1), c)),
                  pl.BlockSpec((tm, WIDTH), lambda c, i: (i, cb + c)),
                  pl.BlockSpec((8, WIDTH), lambda c, i: (jnp.maximum(i * (tm // 8) - 1, 0), cb + c)),
                  pl.BlockSpec((GDN_CONV, WIDTH), lambda c, i: (0, c))],
        out_specs=[pl.BlockSpec((tm, WIDTH), lambda c, i: (i, cb + c)),
                   pl.BlockSpec((8, WIDTH), lambda c, i: (0, c))],
        out_shape=[jax.ShapeDtypeStruct((t, PACKED_WIDTH), MXU_DTYPE), jax.ShapeDtypeStruct((8, 3 * WIDTH), F32)],
        input_output_aliases={0: 0},
        compiler_params=_params(("parallel", "arbitrary")),
    )(dproj, dc, dc, proj, proj, conv_w)


def _chunk_matrices(tm):
    r, c = _iota((tm, tm), 0), _iota((tm, tm), 1)
    same = (r // GDN_CHUNK) == (c // GDN_CHUNK)
    return jnp.where(same & (c <= r), 1.0, 0.0), jnp.where(same, 1.0, 0.0)


def _gdn_gates(ba, a_log, dt_bias):
    al = ba + dt_bias
    return _sigmoid(ba), -jnp.exp(a_log) * _softplus(al), _sigmoid(al)


def _head_lane_eye():
    return jnp.where(_iota((HEADS, 128), 1) == _iota((HEADS, 128), 0) + HEADS, 1.0, 0.0)


def _gdn_prep(conv, proj, a_log, dt_bias, tm=256):
    t = proj.shape[0]
    nc = tm // GDN_CHUNK

    def body(c_ref, ba_ref, al_ref, dt_ref, q_ref, k_ref, v_ref, b_ref, g_ref, gl_ref, grow_ref):
        c = c_ref[...]
        a = c * _sigmoid(c)
        beta, g, _ = _gdn_gates(ba_ref[:, 0:128], al_ref[...], dt_ref[...])
        lmat, cmat = _chunk_matrices(tm)
        gc = jnp.dot(lmat, g, precision=HIGHEST, preferred_element_type=F32)
        gl = jnp.dot(cmat, g, precision=HIGHEST, preferred_element_type=F32)
        grow = lax.dot_general(_head_lane_eye(), gc, _NT, precision=HIGHEST, preferred_element_type=F32)
        for h in range(HEADS):
            sl = slice(h * HEAD_DIM, (h + 1) * HEAD_DIM)
            qh, kh, vh = a[:, sl], a[:, WIDTH + h * HEAD_DIM:WIDTH + (h + 1) * HEAD_DIM], a[:, 2 * WIDTH + h * HEAD_DIM:2 * WIDTH + (h + 1) * HEAD_DIM]
            rq = lax.rsqrt(jnp.sum(qh * qh, axis=1, keepdims=True) + NORM_EPS)
            rk = lax.rsqrt(jnp.sum(kh * kh, axis=1, keepdims=True) + NORM_EPS)
            q_ref[h] = qh * (rq * HEAD_DIM ** -0.5)
            k_ref[h] = kh * rk
            v_ref[h] = vh
            b_ref[h] = jnp.broadcast_to(beta[:, h:h + 1], (tm, HEAD_DIM))
            g_ref[h] = jnp.broadcast_to(gc[:, HEADS + h:HEADS + h + 1], (tm, HEAD_DIM))
            gl_ref[h] = jnp.broadcast_to(gl[:, HEADS + h:HEADS + h + 1], (tm, HEAD_DIM))
            for cc in range(nc):
                grow_ref[h, cc] = grow[h:h + 1, cc * GDN_CHUNK:(cc + 1) * GDN_CHUNK]

    hm = pl.BlockSpec((HEADS, tm, HEAD_DIM), lambda i: (0, i, 0))
    small = pl.BlockSpec((1, 128), lambda i: (0, 0))
    hm_shape = jax.ShapeDtypeStruct((HEADS, t, HEAD_DIM), F32)
    return pl.pallas_call(
        body, name="gdn_prep", grid=(t // tm,),
        in_specs=[pl.BlockSpec((tm, 3 * WIDTH), lambda i: (i, 0)),
                  pl.BlockSpec((tm, WIDTH), lambda i: (i, SEG_BA // WIDTH)), small, small],
        out_specs=[hm] * 6 + [pl.BlockSpec((HEADS, nc, 1, GDN_CHUNK), lambda i: (0, i, 0, 0))],
        out_shape=[hm_shape] * 6 + [jax.ShapeDtypeStruct((HEADS, t // GDN_CHUNK, 1, GDN_CHUNK), F32)],
        compiler_params=_params(("parallel",)),
    )(conv, proj, a_log, dt_bias)


def _gdn_prep_bwd(dproj, conv, proj, a_log, dt_bias, dq, dk, dv, db, dg, dgl, dgrow, tm=256):
    t = proj.shape[0]
    nc = tm // GDN_CHUNK

    def body(dp_ref, c_ref, ba_ref, al_ref, dt_ref, dq_ref, dk_ref, dv_ref, db_ref, dg_ref, dgl_ref, dgrow_ref,
             dba_ref, dc_ref, small_ref, row_scr):
        c = c_ref[...]
        sg = _sigmoid(c)
        a = c * sg
        beta, g, sig_al = _gdn_gates(ba_ref[:, 0:128], al_ref[...], dt_ref[...])
        lane = _iota((1, 128), 1)
        d_beta = jnp.zeros((tm, 128), F32)
        d_gc = jnp.zeros((tm, 128), F32)
        d_gl = jnp.zeros((tm, 128), F32)
        for h in range(HEADS):
            for cc in range(nc):
                row_scr[h:h + 1, cc * GDN_CHUNK:(cc + 1) * GDN_CHUNK] = dgrow_ref[h, cc]
            d_beta += jnp.sum(db_ref[h], axis=1, keepdims=True) * jnp.where(lane == h, 1.0, 0.0)
            d_gc += jnp.sum(dg_ref[h], axis=1, keepdims=True) * jnp.where(lane == HEADS + h, 1.0, 0.0)
            d_gl += jnp.sum(dgl_ref[h], axis=1, keepdims=True) * jnp.where(lane == HEADS + h, 1.0, 0.0)
            qs, ks, vs = (slice(h * HEAD_DIM, (h + 1) * HEAD_DIM), slice(WIDTH + h * HEAD_DIM, WIDTH + (h + 1) * HEAD_DIM),
                          slice(2 * WIDTH + h * HEAD_DIM, 2 * WIDTH + (h + 1) * HEAD_DIM))
            qh, kh = a[:, qs], a[:, ks]
            rq = lax.rsqrt(jnp.sum(qh * qh, axis=1, keepdims=True) + NORM_EPS)
            rk = lax.rsqrt(jnp.sum(kh * kh, axis=1, keepdims=True) + NORM_EPS)
            gq, gk = dq_ref[h] * HEAD_DIM ** -0.5, dk_ref[h]
            da_q = rq * gq - qh * (rq * rq * rq) * jnp.sum(gq * qh, axis=1, keepdims=True)
            da_k = rk * gk - kh * (rk * rk * rk) * jnp.sum(gk * kh, axis=1, keepdims=True)
            dsilu = lambda s, x: s * (1.0 + x * (1.0 - s))
            dc_ref[:, qs] = da_q * dsilu(sg[:, qs], c[:, qs])
            dc_ref[:, ks] = da_k * dsilu(sg[:, ks], c[:, ks])
            dc_ref[:, vs] = dv_ref[h] * dsilu(sg[:, vs], c[:, vs])
        d_gc += lax.dot_general(row_scr[...], _head_lane_eye(), _TN, precision=HIGHEST, preferred_element_type=F32)
        lmat, cmat = _chunk_matrices(tm)
        d_g = (lax.dot_general(lmat, d_gc, _TN, precision=HIGHEST, preferred_element_type=F32)
               + lax.dot_general(cmat, d_gl, _TN, precision=HIGHEST, preferred_element_type=F32))
        d_al = d_g * (-jnp.exp(al_ref[...])) * sig_al
        d_bl = d_beta * beta * (1.0 - beta)
        dba_ref[...] = jnp.concatenate([d_bl + d_al, jnp.zeros((tm, WIDTH - 128), F32)], axis=1).astype(dba_ref.dtype)
        row8 = _iota((8, 128), 0)
        part = (jnp.where(row8 == 0, jnp.sum(d_g * g, axis=0, keepdims=True), 0.0)
                + jnp.where(row8 == 1, jnp.sum(d_al, axis=0, keepdims=True), 0.0))

        @pl.when(pl.program_id(0) == 0)
        def _():
            small_ref[...] = jnp.zeros_like(small_ref)

        small_ref[...] += part

    hm = pl.BlockSpec((HEADS, tm, HEAD_DIM), lambda i: (0, i, 0))
    small = pl.BlockSpec((1, 128), lambda i: (0, 0))
    seg = pl.BlockSpec((tm, WIDTH), lambda i: (i, SEG_BA // WIDTH))
    return pl.pallas_call(
        body, name="gdn_prep_bwd", grid=(t // tm,),
        in_specs=[pl.BlockSpec(memory_space=pl.ANY), pl.BlockSpec((tm, 3 * WIDTH), lambda i: (i, 0)), seg, small, small]
        + [hm] * 6 + [pl.BlockSpec((HEADS, nc, 1, GDN_CHUNK), lambda i: (0, i, 0, 0))],
        out_specs=[seg, pl.BlockSpec((tm, 3 * WIDTH), lambda i: (i, 0)), pl.BlockSpec((8, 128), lambda i: (0, 0))],
        out_shape=[jax.ShapeDtypeStruct((t, PACKED_WIDTH), MXU_DTYPE), jax.ShapeDtypeStruct((t, 3 * WIDTH), F32),
                   jax.ShapeDtypeStruct((8, 128), F32)],
        scratch_shapes=[pltpu.VMEM((HEADS, tm), F32)],
        input_output_aliases={0: 0},
        compiler_params=_params(("arbitrary",)),
    )(dproj, conv, proj, a_log, dt_bias, dq, dk, dv, db, dg, dgl, dgrow)


_BNN = (((2,), (1,)), ((0,), (0,)))
_BNT = (((2,), (2,)), ((0,), (0,)))
_BTN = (((1,), (1,)), ((0,), (0,)))


@jax.custom_vjp
def _MM_NN(a, b):
    return _mxu(a, b, _BNN)


@jax.custom_vjp
def _MM_NT(a, b):
    return _mxu(a, b, _BNT)


@jax.custom_vjp
def _MM_TN(a, b):
    return _mxu(a, b, _BTN)


_MM_NN.defvjp(lambda a, b: (_mxu(a, b, _BNN), (a, b)), lambda r, g: (_mxu(g, r[1], _BNT), _mxu(r[0], g, _BTN)))
_MM_NT.defvjp(lambda a, b: (_mxu(a, b, _BNT), (a, b)), lambda r, g: (_mxu(g, r[1], _BNN), _mxu(g, r[0], _BTN)))
_MM_TN.defvjp(lambda a, b: (_mxu(a, b, _BTN), (a, b)), lambda r, g: (_mxu(r[1], g, _BNT), _mxu(r[0], g, _BNN)))


def _split(a):
    hi = a.astype(MXU_DTYPE)
    return hi, (a - hi.astype(F32)).astype(MXU_DTYPE)


def _dot3(a, b, dims):
    (ah, al), (bh, bl) = a, b
    dot = lambda u, v: lax.dot_general(u, v, dims, preferred_element_type=F32)
    return dot(ah, bh) + (dot(ah, bl) + dot(al, bh))


def _unit_lower_inverse(a):
    c = GDN_CHUNK
    eye = jnp.where(_iota((c, c), 0) == _iota((c, c), 1), 1.0, 0.0)
    x = eye - a
    p = a
    for _ in range(5):
        ps = _split(p)
        p = _dot3(ps, ps, _BNN)
        x = x + _dot3(_split(x), _split(p), _BNN)
    return x


@jax.custom_vjp
def _SAVED_INVERSE(a, t_inv):
    return t_inv


def _saved_inverse_bwd(t_inv, g):
    ts = _split(t_inv)
    return -_dot3(ts, _split(_dot3(_split(g), ts, _BNT)), _BTN), jnp.zeros_like(t_inv)


_SAVED_INVERSE.defvjp(lambda a, t_inv: (t_inv, t_inv), _saved_inverse_bwd)


def _gdn_chunk(q, k, v, beta, g1, g2, gl, state, t_inv=None):
    c = GDN_CHUNK
    if t_inv is None:
        _mm_nn, _mm_nt, _mm_tn = (functools.partial(_mxu, dims=dd) for dd in (_BNN, _BNT, _BTN))
    else:
        _mm_nn, _mm_nt, _mm_tn = _MM_NN, _MM_NT, _MM_TN
    row, col = _iota((c, c), 0), _iota((c, c), 1)
    incl, strict = row >= col, row > col
    decay = jnp.where(incl, jnp.exp(jnp.where(incl, g1 - g2, 0.0)), 0.0)
    eg = jnp.exp(g1)
    kb = k * beta
    a = _mm_nt(kb, k) * jnp.where(strict, decay, 0.0)
    inv = _unit_lower_inverse(a) if t_inv is None else _SAVED_INVERSE(a, t_inv)
    u = _mm_nn(inv, v * beta)
    w = _mm_nn(inv, kb * eg)
    attn = _mm_nt(q, k) * decay
    v_new = u - _mm_nn(w, state)
    o = _mm_nn(q * eg, state) + _mm_nn(attn, v_new)
    new_state = state * jnp.exp(gl) + _mm_tn(k * jnp.exp(gl - g1), v_new)
    return (o, new_state, inv) if t_inv is None else (o, new_state)


def _gdn_fwd(q, k, v, beta, g, gl, grow, cpb=2):
    t = q.shape[1]
    rows = cpb * GDN_CHUNK

    def body(q_ref, k_ref, v_ref, b_ref, g_ref, gl_ref, grow_ref, o_ref, st_ref, inv_ref, state):
        @pl.when(pl.program_id(0) == 0)
        def _():
            state[...] = jnp.zeros_like(state)

        s = state[...]
        for cc in range(cpb):
            sl = slice(cc * GDN_CHUNK, (cc + 1) * GDN_CHUNK)
            st_ref[:, cc] = s
            g2 = jnp.broadcast_to(grow_ref[:, cc], (HEADS, GDN_CHUNK, GDN_CHUNK))
            o, s, inv = _gdn_chunk(q_ref[:, sl, :], k_ref[:, sl, :], v_ref[:, sl, :], b_ref[:, sl, :], g_ref[:, sl, :], g2,
                                   gl_ref[:, sl, :], s)
            o_ref[:, sl, :] = o
            inv_ref[:, cc] = inv
        state[...] = s

    hm = pl.BlockSpec((HEADS, rows, HEAD_DIM), lambda i: (0, i, 0))
    per_chunk = pl.BlockSpec((HEADS, cpb, GDN_CHUNK, HEAD_DIM), lambda i: (0, i, 0, 0))
    chunk_shape = jax.ShapeDtypeStruct((HEADS, t // GDN_CHUNK, GDN_CHUNK, HEAD_DIM), F32)
    return pl.pallas_call(
        body, name="gdn_fwd", grid=(t // rows,),
        in_specs=[hm] * 6 + [pl.BlockSpec((HEADS, cpb, 1, GDN_CHUNK), lambda i: (0, i, 0, 0))],
        out_specs=[hm, per_chunk, per_chunk],
        out_shape=[jax.ShapeDtypeStruct((HEADS, t, HEAD_DIM), F32), chunk_shape, chunk_shape],
        scratch_shapes=[pltpu.VMEM((HEADS, GDN_CHUNK, HEAD_DIM), F32)],
        compiler_params=_params(("arbitrary",)),
    )(q, k, v, beta, g, gl, grow)


def _gdn_bwd(q, k, v, beta, g, gl, grow, states, invs, do, cpb=1):
    t = q.shape[1]
    rows = cpb * GDN_CHUNK
    nsteps = t // rows

    def body(q_ref, k_ref, v_ref, b_ref, g_ref, gl_ref, grow_ref, st_ref, inv_ref, do_ref,
             dq_ref, dk_ref, dv_ref, db_ref, dg_ref, dgl_ref, dgrow_ref, dstate):
        @pl.when(pl.program_id(0) == 0)
        def _():
            dstate[...] = jnp.zeros_like(dstate)

        ds = dstate[...]
        for cc in reversed(range(cpb)):
            sl = slice(cc * GDN_CHUNK, (cc + 1) * GDN_CHUNK)
            g2 = jnp.broadcast_to(grow_ref[:, cc], (HEADS, GDN_CHUNK, GDN_CHUNK))
            _, vjp = jax.vjp(_gdn_chunk, q_ref[:, sl, :], k_ref[:, sl, :], v_ref[:, sl, :], b_ref[:, sl, :],
                             g_ref[:, sl, :], g2, gl_ref[:, sl, :], st_ref[:, cc], inv_ref[:, cc])
            gq, gk, gv, gb, gg1, gg2, ggl, ds, _ = vjp((do_ref[:, sl, :], ds))
            dq_ref[:, sl, :] = gq
            dk_ref[:, sl, :] = gk
            dv_ref[:, sl, :] = gv
            db_ref[:, sl, :] = gb
            dg_ref[:, sl, :] = gg1
            dgl_ref[:, sl, :] = ggl
            dgrow_ref[:, cc] = jnp.sum(gg2, axis=1, keepdims=True)
        dstate[...] = ds

    hm = pl.BlockSpec((HEADS, rows, HEAD_DIM), lambda i: (0, nsteps - 1 - i, 0))
    rowspec = pl.BlockSpec((HEADS, cpb, 1, GDN_CHUNK), lambda i: (0, nsteps - 1 - i, 0, 0))
    per_chunk = pl.BlockSpec((HEADS, cpb, GDN_CHUNK, HEAD_DIM), lambda i: (0, nsteps - 1 - i, 0, 0))
    hm_shape = jax.ShapeDtypeStruct((HEADS, t, HEAD_DIM), F32)
    return pl.pallas_call(
        body, name="gdn_bwd", grid=(nsteps,),
        in_specs=[hm] * 6 + [rowspec, per_chunk, per_chunk, hm],
        out_specs=[hm] * 6 + [rowspec],
        out_shape=[hm_shape] * 6 + [jax.ShapeDtypeStruct((HEADS, t // GDN_CHUNK, 1, GDN_CHUNK), F32)],
        scratch_shapes=[pltpu.VMEM((HEADS, GDN_CHUNK, HEAD_DIM), F32)],
        compiler_params=_params(("arbitrary",)),
    )(q, k, v, beta, g, gl, grow, states, invs, do)


def _gdn_out(o_hm, gdn_norm_w, proj, tm=256):
    t = proj.shape[0]

    def body(o_ref, w_ref, z_ref, oz_ref):
        z = z_ref[...]
        gate = z * _sigmoid(z)
        w = w_ref[...]
        for h in range(HEADS):
            sl = slice(h * HEAD_DIM, (h + 1) * HEAD_DIM)
            o = o_ref[h]
            r = lax.rsqrt(jnp.mean(o * o, axis=1, keepdims=True) + NORM_EPS)
            oz_ref[:, sl] = (o * r * w * gate[:, sl]).astype(oz_ref.dtype)

    tok = pl.BlockSpec((tm, WIDTH), lambda i: (i, 0))
    return pl.pallas_call(
        body, name="gdn_out", grid=(t // tm,),
        in_specs=[pl.BlockSpec((HEADS, tm, HEAD_DIM), lambda i: (0, i, 0)), pl.BlockSpec((1, HEAD_DIM), lambda i: (0, 0)),
                  pl.BlockSpec((tm, WIDTH), lambda i: (i, SEG_ZB // WIDTH))],
        out_specs=tok, out_shape=jax.ShapeDtypeStruct((t, WIDTH), MXU_DTYPE),
        compiler_params=_params(("parallel",)),
    )(o_hm, gdn_norm_w, proj)


def _gdn_out_bwd(dproj, d_oz, o_hm, gdn_norm_w, proj, tm=256):
    t = proj.shape[0]

    def body(dp_ref, doz_ref, o_ref, w_ref, z_ref, dz_ref, do_ref, dw_ref):
        z = z_ref[...]
        sg = _sigmoid(z)
        gate = z * sg
        dgate = sg * (1.0 + z * (1.0 - sg))
        w = w_ref[...]
        dw = jnp.zeros((1, HEAD_DIM), F32)
        for h in range(HEADS):
            sl = slice(h * HEAD_DIM, (h + 1) * HEAD_DIM)
            o = o_ref[h]
            g = doz_ref[:, sl]
            r = lax.rsqrt(jnp.mean(o * o, axis=1, keepdims=True) + NORM_EPS)
            on = o * r * w
            dz_ref[:, sl] = (g * on * dgate[:, sl]).astype(dz_ref.dtype)
            dn = g * gate[:, sl]
            dw += jnp.sum(dn * o * r, axis=0, keepdims=True)
            dnw = dn * w
            do_ref[h] = r * dnw - o * (r * r * r) * jnp.mean(dnw * o, axis=1, keepdims=True)

        @pl.when(pl.program_id(0) == 0)
        def _():
            dw_ref[...] = jnp.zeros_like(dw_ref)

        dw_ref[...] += jnp.concatenate([dw, jnp.zeros((7, HEAD_DIM), F32)], axis=0)

    tok = pl.BlockSpec((tm, WIDTH), lambda i: (i, 0))
    seg = pl.BlockSpec((tm, WIDTH), lambda i: (i, SEG_ZB // WIDTH))
    hm = pl.BlockSpec((HEADS, tm, HEAD_DIM), lambda i: (0, i, 0))
    return pl.pallas_call(
        body, name="gdn_out_bwd", grid=(t // tm,),
        in_specs=[pl.BlockSpec(memory_space=pl.ANY), tok, hm, pl.BlockSpec((1, HEAD_DIM), lambda i: (0, 0)), seg],
        out_specs=[seg, hm, pl.BlockSpec((8, HEAD_DIM), lambda i: (0, 0))],
        out_shape=[jax.ShapeDtypeStruct((t, PACKED_WIDTH), MXU_DTYPE), jax.ShapeDtypeStruct((HEADS, t, HEAD_DIM), F32),
                   jax.ShapeDtypeStruct((8, HEAD_DIM), F32)],
        input_output_aliases={0: 0},
        compiler_params=_params(("arbitrary",)),
    )(dproj, d_oz, o_hm, gdn_norm_w, proj)


def _merge(y_a, y_b, proj, tm=256):
    t = proj.shape[0]

    def body(ya_ref, yb_ref, ga_ref, gb_ref, m_ref):
        m_ref[...] = (_sigmoid(ga_ref[...]) * ya_ref[...] + _sigmoid(gb_ref[...]) * yb_ref[...]).astype(m_ref.dtype)

    half = pl.BlockSpec((tm, WIDTH), lambda i, c: (i, c))
    return pl.pallas_call(
        body, name="merge", grid=(t // tm, 2),
        in_specs=[half, half, pl.BlockSpec((tm, WIDTH), lambda i, c: (i, SEG_GA // WIDTH + c)),
                  pl.BlockSpec((tm, WIDTH), lambda i, c: (i, SEG_GB // WIDTH + c))],
        out_specs=half, out_shape=jax.ShapeDtypeStruct((t, D_MODEL), MXU_DTYPE),
        compiler_params=_params(("parallel", "parallel")),
    )(y_a, y_b, proj, proj)


def _merge_bwd(dproj, d_m, y, proj, seg, name, tm=256):
    t = proj.shape[0]

    def body(*refs):
        dm_ref, y_ref, g_ref, dg_ref, dy_ref = refs[-5:]
        dm = dm_ref[...]
        s = _sigmoid(g_ref[...])
        dy_ref[...] = (dm * s).astype(dy_ref.dtype)
        dg_ref[...] = (dm * y_ref[...] * s * (1.0 - s)).astype(dg_ref.dtype)

    half = pl.BlockSpec((tm, WIDTH), lambda i, c: (i, c))
    gate = pl.BlockSpec((tm, WIDTH), lambda i, c: (i, seg // WIDTH + c))
    specs, args, aliases = [half, half, gate], [d_m, y, proj], {}
    if dproj is not None:
        specs, args, aliases = [pl.BlockSpec(memory_space=pl.ANY)] + specs, [dproj] + args, {0: 0}
    return pl.pallas_call(
        body, name=name, grid=(t // tm, 2), in_specs=specs, out_specs=[gate, half],
        out_shape=[jax.ShapeDtypeStruct((t, PACKED_WIDTH), MXU_DTYPE), jax.ShapeDtypeStruct((t, D_MODEL), MXU_DTYPE)],
        input_output_aliases=aliases,
        compiler_params=_params(("parallel", "parallel")),
    )(*args)


def _tail(x, mo, final_w, target, tm=256):
    t = x.shape[0]

    def body(x_ref, mo_ref, w_ref, t_ref, dxm_ref, dx_ref, loss_ref, dw_ref):
        x2 = x_ref[...] + mo_ref[...]
        w = w_ref[...]
        r = lax.rsqrt(jnp.mean(x2 * x2, axis=-1, keepdims=True) + NORM_EPS)
        xn = x2 * r
        err = xn * w - t_ref[...]
        dy = err * (1.0 / D_MODEL)
        dyw = dy * w
        dx2 = r * dyw - x2 * (r * r * r) * jnp.mean(dyw * x2, axis=-1, keepdims=True)
        dx_ref[...] = dx2
        dxm_ref[...] = dx2.astype(dxm_ref.dtype)
        loss = 0.5 * jnp.sum(jnp.sum(err * err, axis=-1, keepdims=True) * (1.0 / D_MODEL), axis=0, keepdims=True)
        onehot = jnp.where((_iota((8, 128), 0) == 0) & (_iota((8, 128), 1) == 0), 1.0, 0.0)

        @pl.when(pl.program_id(0) == 0)
        def _():
            loss_ref[...] = jnp.zeros_like(loss_ref)
            dw_ref[...] = jnp.zeros_like(dw_ref)

        loss_ref[...] += loss * onehot
        dw_ref[...] += jnp.where(_iota((8, D_MODEL), 0) == 0, jnp.sum(dy * xn, axis=0, keepdims=True), 0.0)

    tok = pl.BlockSpec((tm, D_MODEL), lambda i: (i, 0))
    return pl.pallas_call(
        body, name="tail", grid=(t // tm,),
        in_specs=[tok, tok, pl.BlockSpec((1, D_MODEL), lambda i: (0, 0)), tok],
        out_specs=[tok, tok, pl.BlockSpec((8, 128), lambda i: (0, 0)), pl.BlockSpec((8, D_MODEL), lambda i: (0, 0))],
        out_shape=[jax.ShapeDtypeStruct((t, D_MODEL), MXU_DTYPE), jax.ShapeDtypeStruct((t, D_MODEL), F32),
                   jax.ShapeDtypeStruct((8, 128), F32), jax.ShapeDtypeStruct((8, D_MODEL), F32)],
        compiler_params=_params(("arbitrary",)),
    )(x, mo, final_w, target)


def _norm_bwd(x, norm_w, dh, dx2, tm=256):
    t = x.shape[0]

    def body(x_ref, w_ref, dh_ref, dx2_ref, dx_ref, dw_ref):
        xf, w, dh_ = x_ref[...], w_ref[...], dh_ref[...]
        r = lax.rsqrt(jnp.mean(xf * xf, axis=-1, keepdims=True) + NORM_EPS)
        dhw = dh_ * w
        dx_ref[...] = dx2_ref[...] + r * dhw - xf * (r * r * r) * jnp.mean(dhw * xf, axis=-1, keepdims=True)

        @pl.when(pl.program_id(0) == 0)
        def _():
            dw_ref[...] = jnp.zeros_like(dw_ref)

        dw_ref[...] += jnp.where(_iota((8, D_MODEL), 0) == 0, jnp.sum(dh_ * xf * r, axis=0, keepdims=True), 0.0)

    tok = pl.BlockSpec((tm, D_MODEL), lambda i: (i, 0))
    return pl.pallas_call(
        body, name="norm_bwd", grid=(t // tm,),
        in_specs=[tok, pl.BlockSpec((1, D_MODEL), lambda i: (0, 0)), tok, tok],
        out_specs=[tok, pl.BlockSpec((8, D_MODEL), lambda i: (0, 0))],
        out_shape=[jax.ShapeDtypeStruct((t, D_MODEL), F32), jax.ShapeDtypeStruct((8, D_MODEL), F32)],
        compiler_params=_params(("arbitrary",)),
    )(x, norm_w, dh, dx2)


def _local_step(x, target, norm_w, wp, conv_w, a_log, dt_bias, gdn_norm_w, w_up_a, w_up_b, w_out, final_w):
    t = x.shape[0]
    tables = _rope_tables(t)
    a_log = jnp.pad(a_log, ((0, 0), (HEADS, 128 - 2 * HEADS)))
    dt_bias = jnp.pad(dt_bias, ((0, 0), (HEADS, 128 - 2 * HEADS)))

    proj, h = _norm_proj(x, norm_w, wp)
    qkvs = _rope_fwd(proj, tables)
    outs, lses = zip(*[_att_fwd(qkvs[gi], d, f"att_fwd{gi}") for gi, d in enumerate(DILATIONS)])
    oz_a, o_a, *lse_views = _att_merge(outs, lses, proj)
    conv = _conv_fwd(proj, conv_w)
    gq, gk, gv, gb, gg, ggl, grow = _gdn_prep(conv, proj, a_log, dt_bias)
    o_b, states, invs = _gdn_fwd(gq, gk, gv, gb, gg, ggl, grow)
    oz_b = _gdn_out(o_b, gdn_norm_w, proj)
    big = dict(tm=1024, tn=1024, tk=1024)
    y_a = _matmul(oz_a, w_up_a, "nn", "up_a", **big)
    y_b = _matmul(oz_b, w_up_b, "nn", "up_b", **big)
    merged = _merge(y_a, y_b, proj)
    mo = _matmul(merged, w_out, "nn", "out_proj", **big)
    dx2_m, dx2, loss_blk, d_final = _tail(x, mo, final_w, target)

    d_wout = _matmul(merged, dx2_m, "tn", "d_w_out", **big)
    d_m = _matmul(dx2_m, w_out, "nt", "d_merged", **big)
    dproj, dy_a = _merge_bwd(None, d_m, y_a, proj, SEG_GA, "merge_bwd_a")
    dproj, dy_b = _merge_bwd(dproj, d_m, y_b, proj, SEG_GB, "merge_bwd_b")
    d_wua = _matmul(oz_a, dy_a, "tn", "d_w_up_a", **big)
    d_wub = _matmul(oz_b, dy_b, "tn", "d_w_up_b", **big)
    d_oz_a = _matmul(dy_a, w_up_a, "nt", "d_oz_a", **big)
    d_oz_b = _matmul(dy_b, w_up_b, "nt", "d_oz_b", **big)
    dproj, *views = _att_merge_bwd(dproj, d_oz_a, o_a, proj)
    do_views, delta_views = views[:3], views[3:]
    dqkvs = [_att_bwd(qkvs[gi], do_views[gi], lse_views[gi], delta_views[gi], d, f"att_bwd{gi}")
             for gi, d in enumerate(DILATIONS)]
    dproj = _rope_bwd(dproj, dqkvs, tables)
    dproj, do_b, d_gnw = _gdn_out_bwd(dproj, d_oz_b, o_b, gdn_norm_w, proj)
    dgq, dgk, dgv, dgb, dgg, dggl, dgrow = _gdn_bwd(gq, gk, gv, gb, gg, ggl, grow, states, invs, do_b)
    dproj, dconv, d_small = _gdn_prep_bwd(dproj, conv, proj, a_log, dt_bias, dgq, dgk, dgv, dgb, dgg, dggl, dgrow)
    dproj, d_convw = _conv_bwd(dproj, dconv, proj, conv_w)
    d_wp = _matmul(h, dproj, "tn", "d_w_in", tm=1024, tn=512, tk=1024)
    dh = _matmul(dproj, wp, "nt", "d_h", tm=1024, tn=1024, tk=512)
    grad_x, d_norm = _norm_bwd(x, norm_w, dh, dx2)
    return dict(loss=loss_blk, grad_x=grad_x, norm_w=d_norm[0:1], w_in=d_wp, conv_w=d_convw[0:GDN_CONV],
                a_log=d_small[0:1, HEADS:2 * HEADS], dt_bias=d_small[1:2, HEADS:2 * HEADS], gdn_norm_w=d_gnw[0:1],
                w_up_a=d_wua, w_up_b=d_wub,
                w_out=d_wout, final_norm_w=d_final[0:1])


SHARDS = 4
W_IN_SHARD = IN_WIDTH // SHARDS
ROWS_W_IN = D_MODEL * W_IN_SHARD // 128
ROWS_UP = WIDTH * (D_MODEL // SHARDS) // 128
ROWS_OUT = (D_MODEL // SHARDS) * D_MODEL // 128
CONV_SHARD = 3 * WIDTH // SHARDS
ROWS_CONV = 16
SLAB_ROWS = ROWS_W_IN + 2 * ROWS_UP + ROWS_OUT + 2 * ROWS_CONV
HALF_ROWS = SLAB_ROWS // 2
MESH = pl.DeviceIdType.MESH
ANY = pl.BlockSpec(memory_space=pl.ANY)


def _pad_rows(a, rows):
    return jnp.pad(a, ((0, rows - a.shape[0]), (0, 0)))


def _pack_slab(w_in, w_up_a, w_up_b, w_out, conv, conv_lo):
    parts = [w_in.reshape(ROWS_W_IN, 128), w_up_a.reshape(ROWS_UP, 128), w_up_b.reshape(ROWS_UP, 128),
             w_out.reshape(ROWS_OUT, 128), _pad_rows(conv.reshape(-1, 128), ROWS_CONV), _pad_rows(conv_lo.reshape(-1, 128), ROWS_CONV)]
    return jnp.concatenate(parts, axis=0)


def _unpack_slab(slab):
    r0 = 0
    out = []
    for rows, shape in ((ROWS_W_IN, (D_MODEL, W_IN_SHARD)), (ROWS_UP, (WIDTH, D_MODEL // SHARDS)), (ROWS_UP, (WIDTH, D_MODEL // SHARDS)),
                        (ROWS_OUT, (D_MODEL // SHARDS, D_MODEL)), (ROWS_CONV, None), (ROWS_CONV, None)):
        part = slab[r0:r0 + rows]
        out.append(part[:GDN_CONV * CONV_SHARD // 128].reshape(GDN_CONV, CONV_SHARD) if shape is None else part.reshape(shape))
        r0 += rows
    return out


def _mesh_position():
    x, y, c = lax.axis_index("x"), lax.axis_index("y"), lax.axis_index("c")
    return x, y, c, [(1 - x, y), (x, 1 - y), (1 - x, 1 - y)]


def _gather_weights(slab):
    def body(slab_ref, out_ref, send_sems, recv_sems, local_sem):
        x, y, c, chips = _mesh_position()
        mine = pltpu.make_async_copy(slab_ref, out_ref.at[2 * x + y], local_sem)
        mine.start()

        def half(chip, which):
            return out_ref.at[2 * chip[0] + chip[1], pl.ds(which * HALF_ROWS, HALF_ROWS), :]

        def copy(k, src, dst, to):
            return pltpu.make_async_remote_copy(src_ref=src, dst_ref=dst, send_sem=send_sems.at[k], recv_sem=recv_sems.at[k],
                                                device_id=to, device_id_type=MESH)

        first = [copy(j, slab_ref.at[pl.ds(c * HALF_ROWS, HALF_ROWS), :], half((x, y), c), (*chip, c)) for j, chip in enumerate(chips)]
        for cp in first:
            cp.start()
        passed = [copy(3 + j, half(chip, c), half(chip, c), (x, y, 1 - c)) for j, chip in enumerate(chips)]
        for j, chip in enumerate(chips):
            copy(j, half(chip, c), half(chip, c), (x, y, c)).wait_recv()
            passed[j].start()
        for j, chip in enumerate(chips):
            copy(3 + j, half(chip, 1 - c), half(chip, 1 - c), (x, y, c)).wait_recv()
        for cp in first + passed:
            cp.wait_send()
        mine.wait()

    return pl.pallas_call(
        body, name="gather_weights", in_specs=[ANY], out_specs=ANY,
        out_shape=jax.ShapeDtypeStruct((SHARDS, SLAB_ROWS, 128), slab.dtype),
        scratch_shapes=[pltpu.SemaphoreType.DMA((6,)), pltpu.SemaphoreType.DMA((6,)), pltpu.SemaphoreType.DMA],    )(slab)


def _exchange_halves(grads):
    def body(g_ref, out_ref, send_sems, recv_sems):
        x, y, c, _ = _mesh_position()
        copies = [pltpu.make_async_remote_copy(src_ref=g_ref.at[s, pl.ds((1 - c) * HALF_ROWS, HALF_ROWS), :], dst_ref=out_ref.at[s],
                                               send_sem=send_sems.at[s], recv_sem=recv_sems.at[s],
                                               device_id=(x, y, 1 - c), device_id_type=MESH) for s in range(SHARDS)]
        for cp in copies:
            cp.start()
        for cp in copies:
            cp.wait()

    return pl.pallas_call(
        body, name="exchange_halves", in_specs=[ANY], out_specs=ANY,
        out_shape=jax.ShapeDtypeStruct((SHARDS, HALF_ROWS, 128), F32),
        scratch_shapes=[pltpu.SemaphoreType.DMA((SHARDS,)), pltpu.SemaphoreType.DMA((SHARDS,))],    )(grads)


SUM_ROWS = HALF_ROWS // 2


def _pair_sum(grads, recv):
    nblk = HALF_ROWS // SUM_ROWS

    def body(c_ref, g_ref, r_ref, o_ref):
        o_ref[...] = (g_ref[...] + r_ref[...]).astype(o_ref.dtype)

    spec = pl.BlockSpec((1, SUM_ROWS, 128), lambda s, i, c_ref: (s, i, 0))
    return pl.pallas_call(
        body, name="pair_sum",
        grid_spec=pltpu.PrefetchScalarGridSpec(
            num_scalar_prefetch=1, grid=(SHARDS, nblk),
            in_specs=[pl.BlockSpec((1, SUM_ROWS, 128), lambda s, i, c_ref: (s, c_ref[0] * nblk + i, 0)), spec],
            out_specs=spec),
        out_shape=jax.ShapeDtypeStruct((SHARDS, HALF_ROWS, 128), MXU_DTYPE),
        compiler_params=_params(("parallel", "parallel")),
    )(lax.axis_index("c").astype(jnp.int32).reshape(1), grads, recv)


def _scatter_pairs(pairs):
    def body(p_ref, out_ref, send_sems, recv_sems):
        x, y, c, chips = _mesh_position()
        copies = [pltpu.make_async_remote_copy(src_ref=p_ref.at[2 * chip[0] + chip[1]], dst_ref=out_ref.at[j],
                                               send_sem=send_sems.at[j], recv_sem=recv_sems.at[j],
                                               device_id=(*chip, c), device_id_type=MESH) for j, chip in enumerate(chips)]
        for cp in copies:
            cp.start()
        for cp in copies:
            cp.wait()

    return pl.pallas_call(
        body, name="scatter_pairs", in_specs=[ANY], out_specs=ANY,
        out_shape=jax.ShapeDtypeStruct((3, HALF_ROWS, 128), pairs.dtype),
        scratch_shapes=[pltpu.SemaphoreType.DMA((3,)), pltpu.SemaphoreType.DMA((3,))],    )(pairs)


def _chip_sum(pairs, recv):
    def body(own_ref, p_ref, r_ref, o_ref):
        o_ref[...] = ((p_ref[0].astype(F32) + r_ref[0].astype(F32)) + r_ref[1].astype(F32)) + r_ref[2].astype(F32)

    own = (2 * lax.axis_index("x") + lax.axis_index("y")).astype(jnp.int32).reshape(1)
    return pl.pallas_call(
        body, name="chip_sum",
        grid_spec=pltpu.PrefetchScalarGridSpec(
            num_scalar_prefetch=1, grid=(HALF_ROWS // SUM_ROWS,),
            in_specs=[pl.BlockSpec((1, SUM_ROWS, 128), lambda i, own_ref: (own_ref[0], i, 0)),
                      pl.BlockSpec((3, SUM_ROWS, 128), lambda i, own_ref: (0, i, 0))],
            out_specs=pl.BlockSpec((SUM_ROWS, 128), lambda i, own_ref: (i, 0))),
        out_shape=jax.ShapeDtypeStruct((HALF_ROWS, 128), F32),
        compiler_params=_params(("parallel",)),
    )(own, pairs, recv)


def _share_total(total_half):
    def body(t_ref, out_ref, send_sem, recv_sem, local_sem):
        x, y, c, _ = _mesh_position()
        mine = pltpu.make_async_copy(t_ref, out_ref.at[pl.ds(c * HALF_ROWS, HALF_ROWS), :], local_sem)
        mine.start()
        cp = pltpu.make_async_remote_copy(src_ref=t_ref, dst_ref=out_ref.at[pl.ds(c * HALF_ROWS, HALF_ROWS), :],
                                          send_sem=send_sem, recv_sem=recv_sem, device_id=(x, y, 1 - c), device_id_type=MESH)
        cp.start()
        other = out_ref.at[pl.ds((1 - c) * HALF_ROWS, HALF_ROWS), :]
        pltpu.make_async_remote_copy(src_ref=other, dst_ref=other, send_sem=send_sem, recv_sem=recv_sem,
                                     device_id=(x, y, c), device_id_type=MESH).wait_recv()
        cp.wait_send()
        mine.wait()

    return pl.pallas_call(
        body, name="share_total", in_specs=[ANY], out_specs=ANY,
        out_shape=jax.ShapeDtypeStruct((SLAB_ROWS, 128), F32),
        scratch_shapes=[pltpu.SemaphoreType.DMA, pltpu.SemaphoreType.DMA, pltpu.SemaphoreType.DMA],    )(total_half)


def _allreduce_small(block):
    def body(b_ref, out_ref, gath, send_sems, recv_sems):
        x, y, c, _ = _mesh_position()
        me = 4 * x + 2 * y + c
        gath[me] = b_ref[...]
        copies = []
        for k in range(1, 8):
            peer = (x ^ (k >> 2), y ^ ((k >> 1) & 1), c ^ (k & 1))
            copies.append(pltpu.make_async_remote_copy(src_ref=b_ref, dst_ref=gath.at[me], send_sem=send_sems.at[k - 1],
                                                       recv_sem=recv_sems.at[k - 1], device_id=peer, device_id_type=MESH))
        for cp in copies:
            cp.start()
        for k in range(1, 8):
            src = 4 * (x ^ (k >> 2)) + 2 * (y ^ ((k >> 1) & 1)) + (c ^ (k & 1))
            pltpu.make_async_remote_copy(src_ref=b_ref, dst_ref=gath.at[src], send_sem=send_sems.at[k - 1],
                                         recv_sem=recv_sems.at[k - 1], device_id=(x, y, c), device_id_type=MESH).wait_recv()
        for cp in copies:
            cp.wait_send()
        acc = gath[0]
        for d in range(1, 8):
            acc = acc + gath[d]
        out_ref[...] = acc

    vm = pl.BlockSpec(memory_space=pltpu.VMEM)
    return pl.pallas_call(
        body, name="allreduce_small", in_specs=[vm], out_specs=vm,
        out_shape=jax.ShapeDtypeStruct((8, D_MODEL), F32),
        scratch_shapes=[pltpu.VMEM((8, 8, D_MODEL), F32), pltpu.SemaphoreType.DMA((7,)), pltpu.SemaphoreType.DMA((7,))],
    )(block)


def _adamw(w, g, m, v, name):
    rows, cols = w.shape
    tr = 128 if rows % 128 == 0 else rows

    def body(w_ref, g_ref, m_ref, v_ref, d_ref, nm_ref, nv_ref):
        gv = g_ref[...]
        nm = ADAM_B1 * m_ref[...] + (1.0 - ADAM_B1) * gv
        nv = ADAM_B2 * v_ref[...] + (1.0 - ADAM_B2) * (gv * gv)
        m_hat = nm / (1.0 - ADAM_B1 ** ADAM_STEP)
        v_hat = nv / (1.0 - ADAM_B2 ** ADAM_STEP)
        d_ref[...] = -ADAM_LR * (m_hat / (jnp.sqrt(v_hat) + ADAM_EPS) + ADAM_WD * w_ref[...])
        nm_ref[...] = nm
        nv_ref[...] = nv

    spec = pl.BlockSpec((tr, cols), lambda i: (i, 0))
    shape = jax.ShapeDtypeStruct((rows, cols), F32)
    return pl.pallas_call(
        body, name=name, grid=(rows // tr,), in_specs=[spec] * 4, out_specs=[spec] * 3, out_shape=[shape] * 3,
        compiler_params=_params(("parallel",)),
    )(w, g, m, v)


def _pack_w_in(w):
    return jnp.concatenate([w[:, :BA_END], jnp.zeros((D_MODEL, SEG_GA - BA_END), w.dtype), w[:, BA_END:]], axis=1)


def kernel(x, norm_w, w_in, conv_w, a_log, dt_bias, gdn_norm_w, w_up_a, w_up_b, w_out, final_norm_w, loss_target, m_norm_w, m_w_in, m_conv_w, m_a_log, m_dt_bias, m_gdn_norm_w, m_w_up_a, m_w_up_b, m_w_out, m_final_norm_w, v_norm_w, v_w_in, v_conv_w, v_a_log, v_dt_bias, v_gdn_norm_w, v_w_up_a, v_w_up_b, v_w_out, v_final_norm_w):
    conv_hi = conv_w[0].astype(MXU_DTYPE)
    conv_lo = (conv_w[0] - conv_hi.astype(F32)).astype(MXU_DTYPE)
    slab = _pack_slab(w_in[0].astype(MXU_DTYPE), w_up_a[0].astype(MXU_DTYPE), w_up_b[0].astype(MXU_DTYPE),
                      w_out[0].astype(MXU_DTYPE), conv_hi, conv_lo)
    slabs = _gather_weights(slab)
    parts = [_unpack_slab(slabs[s]) for s in range(SHARDS)]
    split = BA_END - (SHARDS - 1) * W_IN_SHARD
    wp = jnp.concatenate([p[0] for p in parts[:-1]] + [parts[-1][0][:, :split], jnp.zeros((D_MODEL, SEG_GA - BA_END), MXU_DTYPE),
                                                       parts[-1][0][:, split:]], axis=1)
    w_up_a_full = jnp.concatenate([p[1] for p in parts], axis=1)
    w_up_b_full = jnp.concatenate([p[2] for p in parts], axis=1)
    w_out_full = jnp.concatenate([p[3] for p in parts], axis=0)
    conv_full = jnp.concatenate([p[4].astype(F32) + p[5].astype(F32) for p in parts], axis=1)

    g = _local_step(x[0], loss_target[0], norm_w, wp, conv_full, a_log, dt_bias, gdn_norm_w,
                    w_up_a_full, w_up_b_full, w_out_full, final_norm_w[None])

    d_wp = g["w_in"]
    d_w_in = [d_wp[:, s * W_IN_SHARD:(s + 1) * W_IN_SHARD] for s in range(SHARDS - 1)]
    d_w_in.append(jnp.concatenate([d_wp[:, (SHARDS - 1) * W_IN_SHARD:BA_END], d_wp[:, SEG_GA:]], axis=1))
    zero_conv = jnp.zeros((GDN_CONV, CONV_SHARD), F32)
    grads = jnp.stack([
        _pack_slab(d_w_in[s], g["w_up_a"][:, s * 256:(s + 1) * 256],
                   g["w_up_b"][:, s * 256:(s + 1) * 256], g["w_out"][s * 256:(s + 1) * 256],
                   g["conv_w"][:, s * CONV_SHARD:(s + 1) * CONV_SHARD], zero_conv) for s in range(SHARDS)])
    from_sibling = _exchange_halves(grads)
    pairs = _pair_sum(grads, from_sibling)
    from_chips = _scatter_pairs(pairs)
    total = _share_total(_chip_sum(pairs, from_chips))
    g_w_in, g_w_up_a, g_w_up_b, g_w_out, g_conv, _ = _unpack_slab(total)

    row2 = jnp.concatenate([g["gdn_norm_w"], g["a_log"], g["dt_bias"], g["loss"][0:1, 0:1],
                            jnp.zeros((1, D_MODEL - HEAD_DIM - 2 * HEADS - 1), F32)], axis=1)
    small = _allreduce_small(jnp.concatenate([g["norm_w"], g["final_norm_w"], row2, jnp.zeros((5, D_MODEL), F32)], axis=0))
    g_norm, g_final = small[0:1], small[1]
    g_gnw, g_alog, g_dt = small[2:3, 0:HEAD_DIM], small[2:3, HEAD_DIM:HEAD_DIM + HEADS], small[2:3, HEAD_DIM + HEADS:HEAD_DIM + 2 * HEADS]
    loss = small[2, HEAD_DIM + 2 * HEADS]

    names = ["norm_w", "w_in", "conv_w", "a_log", "dt_bias", "gdn_norm_w", "w_up_a", "w_up_b", "w_out", "final_norm_w"]
    weights = dict(zip(names, (norm_w, w_in, conv_w, a_log, dt_bias, gdn_norm_w, w_up_a, w_up_b, w_out, final_norm_w)))
    ms = dict(zip(names, (m_norm_w, m_w_in, m_conv_w, m_a_log, m_dt_bias, m_gdn_norm_w, m_w_up_a, m_w_up_b, m_w_out, m_final_norm_w)))
    vs = dict(zip(names, (v_norm_w, v_w_in, v_conv_w, v_a_log, v_dt_bias, v_gdn_norm_w, v_w_up_a, v_w_up_b, v_w_out, v_final_norm_w)))
    grads2d = dict(norm_w=g_norm, w_in=g_w_in, conv_w=g_conv, a_log=g_alog, dt_bias=g_dt, gdn_norm_w=g_gnw,
                   w_up_a=g_w_up_a, w_up_b=g_w_up_b, w_out=g_w_out, final_norm_w=g_final[None])
    grad_out, delta, new_m, new_v = [], [], [], []
    for n in names:
        shape = weights[n].shape
        two_d = grads2d[n].shape
        d, nm, nv = _adamw(weights[n].reshape(two_d), grads2d[n], ms[n].reshape(two_d), vs[n].reshape(two_d), f"adamw_{n}")
        grad_out.append(grads2d[n].reshape(shape))
        delta.append(d.reshape(shape))
        new_m.append(nm.reshape(shape))
        new_v.append(nv.reshape(shape))
    return (loss, g["grad_x"][None], *grad_out, *delta, *new_m, *new_v)
```

```python
import functools

import jax
import jax.numpy as jnp
from jax import lax
from jax.experimental import pallas as pl
from jax.experimental.pallas import tpu as pltpu

F32 = jnp.float32
MXU_DTYPE = jnp.bfloat16
HIGHEST = lax.Precision.HIGHEST

D_MODEL = 1024
HEADS = 8
HEAD_DIM = 64
WIDTH = HEADS * HEAD_DIM
NORM_EPS = 1e-6
ROPE_THETA = 10000.0
ATT_BLOCK = 128
DILATIONS = (1, 4, 16)
GDN_CHUNK = 64
GDN_CONV = 4
IN_WIDTH = 9232
SEG_A, SEG_ZA, SEG_B, SEG_ZB, SEG_BA, SEG_GA, SEG_GB, PACKED_WIDTH = 0, 4608, 5120, 6656, 7168, 7680, 8704, 9728
BA_END = 7184
VMEM_LIMIT = 56 * 1024 * 1024

ADAM_LR, ADAM_B1, ADAM_B2, ADAM_EPS, ADAM_WD, ADAM_STEP = 0.001, 0.9, 0.999, 1e-08, 0.01, 10

_NN = (((1,), (0,)), ((), ()))
_NT = (((1,), (1,)), ((), ()))
_TN = (((0,), (0,)), ((), ()))


def _params(sem):
    return pltpu.CompilerParams(dimension_semantics=sem, vmem_limit_bytes=VMEM_LIMIT)


def _mxu(a, b, dims):
    return lax.dot_general(a.astype(MXU_DTYPE), b.astype(MXU_DTYPE), dims, preferred_element_type=F32)


def _sigmoid(x):
    return 1.0 / (1.0 + jnp.exp(-x))


def _softplus(x):
    return jnp.maximum(x, 0.0) + jnp.log(1.0 + jnp.exp(-jnp.abs(x)))


def _iota(shape, axis):
    return lax.broadcasted_iota(jnp.int32, shape, axis)


def _matmul(a, b, mode, name, out_dtype=F32, tm=512, tn=512, tk=512):
    if mode == "nn":
        (m, k), (k2, n) = a.shape, b.shape
    elif mode == "nt":
        (m, k), (n, k2) = a.shape, b.shape
    else:
        (k, m), (k2, n) = a.shape, b.shape
    assert k == k2
    tm, tn, tk = min(tm, m), min(tn, n), min(tk, k)
    assert m % tm == 0 and n % tn == 0 and k % tk == 0
    nk = k // tk
    dims = {"nn": _NN, "nt": _NT, "tn": _TN}[mode]

    def body(a_ref, b_ref, o_ref, acc_ref):
        kk = pl.program_id(2)

        @pl.when(kk == 0)
        def _():
            acc_ref[...] = jnp.zeros_like(acc_ref)

        acc_ref[...] += _mxu(a_ref[...], b_ref[...], dims)

        @pl.when(kk == nk - 1)
        def _():
            o_ref[...] = acc_ref[...].astype(o_ref.dtype)

    a_spec = pl.BlockSpec((tk, tm), lambda i, j, kk: (kk, i)) if mode == "tn" else pl.BlockSpec((tm, tk), lambda i, j, kk: (i, kk))
    b_spec = pl.BlockSpec((tn, tk), lambda i, j, kk: (j, kk)) if mode == "nt" else pl.BlockSpec((tk, tn), lambda i, j, kk: (kk, j))
    return pl.pallas_call(
        body, name=name, grid=(m // tm, n // tn, nk), in_specs=[a_spec, b_spec],
        out_specs=pl.BlockSpec((tm, tn), lambda i, j, kk: (i, j)),
        out_shape=jax.ShapeDtypeStruct((m, n), out_dtype),
        scratch_shapes=[pltpu.VMEM((tm, tn), F32)],
        compiler_params=_params(("parallel", "parallel", "arbitrary")),
    )(a, b)


def _norm_proj(x, norm_w, wp, tm=1024, tn=512):
    t = x.shape[0]

    def body(x_ref, nw_ref, w_ref, proj_ref, h_ref):
        @pl.when(pl.program_id(1) == 0)
        def _():
            xf = x_ref[...]
            r = lax.rsqrt(jnp.mean(xf * xf, axis=-1, keepdims=True) + NORM_EPS)
            h_ref[...] = (xf * r * nw_ref[...]).astype(h_ref.dtype)

        proj_ref[...] = jnp.dot(h_ref[...], w_ref[...], preferred_element_type=F32)

    return pl.pallas_call(
        body, name="norm_proj", grid=(t // tm, PACKED_WIDTH // tn),
        in_specs=[pl.BlockSpec((tm, D_MODEL), lambda i, j: (i, 0)),
                  pl.BlockSpec((1, D_MODEL), lambda i, j: (0, 0)),
                  pl.BlockSpec((D_MODEL, tn), lambda i, j: (0, j))],
        out_specs=[pl.BlockSpec((tm, tn), lambda i, j: (i, j)),
                   pl.BlockSpec((tm, D_MODEL), lambda i, j: (i, 0))],
        out_shape=[jax.ShapeDtypeStruct((t, PACKED_WIDTH), F32), jax.ShapeDtypeStruct((t, D_MODEL), MXU_DTYPE)],
        compiler_params=_params(("parallel", "arbitrary")),
    )(x, norm_w, wp)


def _rope_tables(t):
    lane = jnp.arange(128)
    inv_freq = ROPE_THETA ** (-jnp.arange(0, HEAD_DIM, 2, dtype=F32) / HEAD_DIM)
    ang = jnp.arange(t, dtype=F32)[:, None] * inv_freq[None, :]
    ang = jnp.concatenate([ang, ang, ang, ang], axis=-1)
    first_half = (lane % HEAD_DIM) < HEAD_DIM // 2
    cos, sin = jnp.cos(ang), jnp.sin(ang)
    return cos, jnp.where(first_half, -sin, 0.0), jnp.where(first_half, 0.0, sin)


def _rope_block(x, cos, sin_lo, sin_hi, sign):
    outs = []
    for c in range(8):
        xc = x[:, c * 128:(c + 1) * 128]
        rot = pltpu.roll(xc, 96, 1) * sin_lo + pltpu.roll(xc, 32, 1) * sin_hi
        outs.append(xc * cos + sign * rot)
    outs.append(x[:, 2 * WIDTH:])
    return jnp.concatenate(outs, axis=1)


def _tile_scratch(tm, cols):
    return pltpu.VMEM((cols // 128, tm, 128), F32)


def _store_tile(scr, y):
    for c in range(scr.shape[0]):
        scr[c] = y[:, c * 128:(c + 1) * 128]


def _load_tile(scr):
    return jnp.concatenate([scr[c] for c in range(scr.shape[0])], axis=1)


def _to_strided_view(scr, o_ref, d):
    n, tm, _ = scr.shape
    for r in range(d):
        for c in range(n):
            o_ref[:, (r * n + c) * 128:(r * n + c + 1) * 128] = scr[c, pl.ds(r, tm // d, stride=d), :].astype(o_ref.dtype)


def _from_strided_view(i_ref, scr, d):
    n, tm, _ = scr.shape
    for r in range(d):
        for c in range(n):
            scr[c, pl.ds(r, tm // d, stride=d), :] = i_ref[:, (r * n + c) * 128:(r * n + c + 1) * 128].astype(F32)


def _strided_spec(tm, d, cols):
    return pl.BlockSpec((tm // d, d * cols), lambda i: (i, 0))


def _rope_fwd(proj, tables, tm=256):
    t = proj.shape[0]
    cols = 3 * WIDTH

    def body(x_ref, c_ref, sl_ref, sh_ref, o0, o1, o2, scr):
        for g, (d, o_ref) in enumerate(zip(DILATIONS, (o0, o1, o2))):
            y = _rope_block(x_ref[:, g * cols:(g + 1) * cols], c_ref[...], sl_ref[...], sh_ref[...], 1.0)
            if d == 1:
                o_ref[...] = y.astype(o_ref.dtype)
            else:
                _store_tile(scr, y)
                _to_strided_view(scr, o_ref, d)

    tab = pl.BlockSpec((tm, 128), lambda i: (i, 0))
    return pl.pallas_call(
        body, name="rope_fwd", grid=(t // tm,),
        in_specs=[pl.BlockSpec((tm, 3 * cols), lambda i: (i, 0)), tab, tab, tab],
        out_specs=[_strided_spec(tm, d, cols) for d in DILATIONS],
        out_shape=[jax.ShapeDtypeStruct((t // d, d * cols), MXU_DTYPE) for d in DILATIONS],
        scratch_shapes=[_tile_scratch(tm, cols)],
        compiler_params=_params(("parallel",)),
    )(proj, *tables)


def _rope_bwd(dproj, dqkvs, tables, tm=256):
    t = dproj.shape[0]
    cols = 3 * WIDTH

    def body(dp_ref, i0, i1, i2, c_ref, sl_ref, sh_ref, o_ref, scr):
        for g, (d, i_ref) in enumerate(zip(DILATIONS, (i0, i1, i2))):
            if d == 1:
                x = i_ref[...]
            else:
                _from_strided_view(i_ref, scr, d)
                x = _load_tile(scr)
            y = _rope_block(x, c_ref[...], sl_ref[...], sh_ref[...], -1.0)
            o_ref[:, g * cols:(g + 1) * cols] = y.astype(o_ref.dtype)

    tab = pl.BlockSpec((tm, 128), lambda i: (i, 0))
    return pl.pallas_call(
        body, name="rope_bwd", grid=(t // tm,),
        in_specs=[pl.BlockSpec(memory_space=pl.ANY)] + [_strided_spec(tm, d, cols) for d in DILATIONS] + [tab, tab, tab],
        out_specs=pl.BlockSpec((tm, 3 * cols), lambda i: (i, 0)),
        out_shape=jax.ShapeDtypeStruct((t, PACKED_WIDTH), MXU_DTYPE),
        scratch_shapes=[_tile_scratch(tm, cols)],
        input_output_aliases={0: 0},
        compiler_params=_params(("parallel",)),
    )(dproj, *dqkvs, *tables)


def _att_masks():
    qi = _iota((ATT_BLOCK, ATT_BLOCK), 0)
    kj = _iota((ATT_BLOCK, ATT_BLOCK), 1)
    return kj <= qi, kj >= qi


def _att_fwd(qkv, d, name):
    rows = qkv.shape[0]
    nb = rows // ATT_BLOCK
    scale = HEAD_DIM ** -0.5

    def body(q_ref, kc_ref, kp_ref, vc_ref, vp_ref, o_ref, lse_ref):
        has_prev = pl.program_id(1) > 0
        m_cur, m_prev = _att_masks()
        m_prev = m_prev & has_prev
        hs = range(HEADS)
        sls = [slice(h * HEAD_DIM, (h + 1) * HEAD_DIM) for h in hs]
        qs = [q_ref[:, sl] for sl in sls]
        s_c = [jnp.where(m_cur, _mxu(qs[h], kc_ref[:, sls[h]], _NT) * scale, -jnp.inf) for h in hs]
        s_p = [jnp.where(m_prev, _mxu(qs[h], kp_ref[:, sls[h]], _NT) * scale, -jnp.inf) for h in hs]
        m = [jnp.maximum(jnp.max(s_c[h], axis=1, keepdims=True), jnp.max(s_p[h], axis=1, keepdims=True)) for h in hs]
        p_c = [jnp.exp(s_c[h] - m[h]) for h in hs]
        p_p = [jnp.exp(s_p[h] - m[h]) for h in hs]
        den = [jnp.sum(p_c[h], axis=1, keepdims=True) + jnp.sum(p_p[h], axis=1, keepdims=True) for h in hs]
        o = [_mxu(p_c[h], vc_ref[:, sls[h]], _NN) + _mxu(p_p[h], vp_ref[:, sls[h]], _NN) for h in hs]
        for h in hs:
            o_ref[:, sls[h]] = o[h] / den[h]
            lse_ref[:, sls[h]] = jnp.broadcast_to(m[h] + jnp.log(den[h]), (ATT_BLOCK, HEAD_DIM))

    def cur(c):
        return pl.BlockSpec((ATT_BLOCK, WIDTH), lambda r, i: (i, 3 * r + c))

    def prev(c):
        return pl.BlockSpec((ATT_BLOCK, WIDTH), lambda r, i: (jnp.maximum(i - 1, 0), 3 * r + c))

    out = pl.BlockSpec((ATT_BLOCK, WIDTH), lambda r, i: (i, r))
    return pl.pallas_call(
        body, name=name, grid=(d, nb), in_specs=[cur(0), cur(1), prev(1), cur(2), prev(2)],
        out_specs=[out, out],
        out_shape=[jax.ShapeDtypeStruct((rows, d * WIDTH), F32)] * 2,
        compiler_params=_params(("parallel", "arbitrary")),
    )(qkv, qkv, qkv, qkv, qkv)


def _att_bwd(qkv, do, lse, delta, d, name):
    rows = qkv.shape[0]
    nb = rows // ATT_BLOCK
    scale = HEAD_DIM ** -0.5

    def body(q_ref, qn_ref, kc_ref, kp_ref, vc_ref, vp_ref, do_ref, don_ref, l_ref, ln_ref, dl_ref, dln_ref, o_ref):
        i = pl.program_id(1)
        m_cur, m_prev = _att_masks()
        m_p = m_prev & (i > 0)
        m_n = m_prev & (i < nb - 1)

        hs = range(HEADS)
        sls = [slice(h * HEAD_DIM, (h + 1) * HEAD_DIM) for h in hs]
        col = [slice(h * HEAD_DIM, h * HEAD_DIM + 1) for h in hs]

        def probs(q_r, k_r, lse_r, mask):
            s = [_mxu(q_r[:, sls[h]], k_r[:, sls[h]], _NT) for h in hs]
            return [jnp.where(mask, jnp.exp(s[h] * scale - lse_r[:, col[h]]), 0.0) for h in hs]

        def dscores(p, do_r, v_r, dl_r):
            dp = [_mxu(do_r[:, sls[h]], v_r[:, sls[h]], _NT) for h in hs]
            return [(p[h] * (dp[h] - dl_r[:, col[h]])).astype(MXU_DTYPE) for h in hs]

        p = probs(q_ref, kc_ref, l_ref, m_cur)
        ds = dscores(p, do_ref, vc_ref, dl_ref)
        dq = [_mxu(ds[h], kc_ref[:, sls[h]], _NN) for h in hs]
        dk = [_mxu(ds[h], q_ref[:, sls[h]], _TN) for h in hs]
        dv = [_mxu(p[h], do_ref[:, sls[h]], _TN) for h in hs]
        p = probs(q_ref, kp_ref, l_ref, m_p)
        ds = dscores(p, do_ref, vp_ref, dl_ref)
        dq = [dq[h] + _mxu(ds[h], kp_ref[:, sls[h]], _NN) for h in hs]
        p = probs(qn_ref, kc_ref, ln_ref, m_n)
        ds = dscores(p, don_ref, vc_ref, dln_ref)
        dk = [dk[h] + _mxu(ds[h], qn_ref[:, sls[h]], _TN) for h in hs]
        dv = [dv[h] + _mxu(p[h], don_ref[:, sls[h]], _TN) for h in hs]
        for h in hs:
            o_ref[:, sls[h]] = dq[h] * scale
            o_ref[:, WIDTH + h * HEAD_DIM:WIDTH + (h + 1) * HEAD_DIM] = dk[h] * scale
            o_ref[:, 2 * WIDTH + h * HEAD_DIM:2 * WIDTH + (h + 1) * HEAD_DIM] = dv[h]

    def qkv_spec(c, shift):
        def idx(r, i):
            return (jnp.clip(i + shift, 0, nb - 1), 3 * r + c)
        return pl.BlockSpec((ATT_BLOCK, WIDTH), idx)

    def tok_spec(shift):
        def idx(r, i):
            return (jnp.clip(i + shift, 0, nb - 1), r)
        return pl.BlockSpec((ATT_BLOCK, WIDTH), idx)

    return pl.pallas_call(
        body, name=name, grid=(d, nb),
        in_specs=[qkv_spec(0, 0), qkv_spec(0, 1), qkv_spec(1, 0), qkv_spec(1, -1), qkv_spec(2, 0), qkv_spec(2, -1),
                  tok_spec(0), tok_spec(1), tok_spec(0), tok_spec(1), tok_spec(0), tok_spec(1)],
        out_specs=pl.BlockSpec((ATT_BLOCK, 3 * WIDTH), lambda r, i: (i, r)),
        out_shape=jax.ShapeDtypeStruct((rows, d * 3 * WIDTH), F32),
        compiler_params=_params(("parallel", "arbitrary")),
    )(qkv, qkv, qkv, qkv, qkv, qkv, do, do, lse, lse, delta, delta)


def _att_merge(os_, lses, proj, tm=256):
    t = proj.shape[0]

    def body(o0, o1, o2, l0, l1, l2, z_ref, oz_ref, o_ref, t0, t1, t2, s_o1, s_l1, s_o2, s_l2, s_t):
        _from_strided_view(o1, s_o1, DILATIONS[1])
        _from_strided_view(l1, s_l1, DILATIONS[1])
        _from_strided_view(o2, s_o2, DILATIONS[2])
        _from_strided_view(l2, s_l2, DILATIONS[2])
        a, b, c = l0[...], _load_tile(s_l1), _load_tile(s_l2)
        m = jnp.maximum(jnp.maximum(a, b), c)
        wa, wb, wc = jnp.exp(a - m), jnp.exp(b - m), jnp.exp(c - m)
        den = wa + wb + wc
        o = (wa * o0[...] + wb * _load_tile(s_o1) + wc * _load_tile(s_o2)) / den
        z = z_ref[...]
        o_ref[...] = o
        oz_ref[...] = (o * z * _sigmoid(z)).astype(oz_ref.dtype)
        total = m + jnp.log(den)
        t0[...] = total
        _store_tile(s_t, total)
        _to_strided_view(s_t, t1, DILATIONS[1])
        _to_strided_view(s_t, t2, DILATIONS[2])

    tok = pl.BlockSpec((tm, WIDTH), lambda i: (i, 0))
    views = [_strided_spec(tm, d, WIDTH) for d in DILATIONS]
    view_shapes = [jax.ShapeDtypeStruct((t // d, d * WIDTH), F32) for d in DILATIONS]
    return pl.pallas_call(
        body, name="att_merge", grid=(t // tm,),
        in_specs=views + views + [pl.BlockSpec((tm, WIDTH), lambda i: (i, SEG_ZA // WIDTH))],
        out_specs=[tok, tok] + views,
        out_shape=[jax.ShapeDtypeStruct((t, WIDTH), MXU_DTYPE), jax.ShapeDtypeStruct((t, WIDTH), F32)] + view_shapes,
        scratch_shapes=[_tile_scratch(tm, WIDTH)] * 5,
        compiler_params=_params(("parallel",)),
    )(*os_, *lses, proj)


def _att_merge_bwd(dproj, d_oz, o, proj, tm=256):
    t = proj.shape[0]

    def body(dp_ref, doz_ref, o_ref, z_ref, dz_ref, do0, do1, do2, dl0, dl1, dl2, s_do, s_dl):
        z, ov, g = z_ref[...], o_ref[...], doz_ref[...]
        sg = _sigmoid(z)
        do = g * z * sg
        dz_ref[...] = (g * ov * sg * (1.0 + z * (1.0 - sg))).astype(dz_ref.dtype)
        do0[...] = do.astype(do0.dtype)
        _store_tile(s_do, do)
        prod = do * ov
        for h in range(HEADS):
            sl = slice(h * HEAD_DIM, (h + 1) * HEAD_DIM)
            half = slice((h % 2) * HEAD_DIM, (h % 2 + 1) * HEAD_DIM)
            s_dl[h // 2, :, half] = jnp.broadcast_to(jnp.sum(prod[:, sl], axis=1, keepdims=True), (tm, HEAD_DIM))
        dl0[...] = _load_tile(s_dl)
        for d, do_v, dl_v in ((DILATIONS[1], do1, dl1), (DILATIONS[2], do2, dl2)):
            _to_strided_view(s_do, do_v, d)
            _to_strided_view(s_dl, dl_v, d)

    tok = pl.BlockSpec((tm, WIDTH), lambda i: (i, 0))
    seg = pl.BlockSpec((tm, WIDTH), lambda i: (i, SEG_ZA // WIDTH))
    views = [_strided_spec(tm, d, WIDTH) for d in DILATIONS]
    return pl.pallas_call(
        body, name="att_merge_bwd", grid=(t // tm,),
        in_specs=[pl.BlockSpec(memory_space=pl.ANY), tok, tok, seg],
        out_specs=[seg] + views + views,
        out_shape=[jax.ShapeDtypeStruct((t, PACKED_WIDTH), MXU_DTYPE)]
        + [jax.ShapeDtypeStruct((t // d, d * WIDTH), MXU_DTYPE) for d in DILATIONS]
        + [jax.ShapeDtypeStruct((t // d, d * WIDTH), F32) for d in DILATIONS],
        scratch_shapes=[_tile_scratch(tm, WIDTH)] * 2,
        input_output_aliases={0: 0},
        compiler_params=_params(("parallel",)),
    )(dproj, d_oz, o, proj)


def _shift_down(x, halo, s):
    if s == 0:
        return x
    xs = pltpu.roll(x, s, 0)
    head = jnp.where(_iota((8, x.shape[1]), 0) < s, pltpu.roll(halo, s, 0), xs[0:8])
    return jnp.concatenate([head, xs[8:]], axis=0)


def _shift_up(x, nxt, s):
    if s == 0:
        return x
    n = x.shape[0]
    xs = pltpu.roll(x, n - s, 0)
    tail = jnp.where(_iota((8, x.shape[1]), 0) >= 8 - s, pltpu.roll(nxt, 8 - s, 0), xs[n - 8:])
    return jnp.concatenate([xs[:n - 8], tail], axis=0)


def _conv_fwd(proj, conv_w, tm=256):
    t = proj.shape[0]
    cb = SEG_B // WIDTH

    def body(x_ref, halo_ref, w_ref, c_ref):
        halo = jnp.where(pl.program_id(0) > 0, halo_ref[...], 0.0)
        x = x_ref[...]
        w = w_ref[...]
        acc = jnp.zeros((tm, WIDTH), F32)
        for j in range(GDN_CONV):
            acc += _shift_down(x, halo, GDN_CONV - 1 - j) * w[j:j + 1, :]
        c_ref[...] = acc

    return pl.pallas_call(
        body, name="conv_fwd", grid=(t // tm, 3),
        in_specs=[pl.BlockSpec((tm, WIDTH), lambda i, c: (i, cb + c)),
                  pl.BlockSpec((8, WIDTH), lambda i, c: (jnp.maximum(i * (tm // 8) - 1, 0), cb + c)),
                  pl.BlockSpec((GDN_CONV, WIDTH), lambda i, c: (0, c))],
        out_specs=pl.BlockSpec((tm, WIDTH), lambda i, c: (i, c)),
        out_shape=jax.ShapeDtypeStruct((t, 3 * WIDTH), F32),
        compiler_params=_params(("parallel", "parallel")),
    )(proj, proj, conv_w)


def _conv_bwd(dproj, dc, proj, conv_w, tm=256):
    t = proj.shape[0]
    cb = SEG_B // WIDTH
    nt = t // tm

    def body(dp_ref, dc_ref, dcn_ref, x_ref, halo_ref, w_ref, dx_ref, dw_ref):
        i = pl.program_id(1)
        w = w_ref[...]
        dcn = jnp.where(i < nt - 1, dcn_ref[...], 0.0)
        dcv = dc_ref[...]
        acc = jnp.zeros((tm, WIDTH), F32)
        for j in range(GDN_CONV):
            acc += _shift_up(dcv, dcn, GDN_CONV - 1 - j) * w[j:j + 1, :]
        dx_ref[...] = acc.astype(dx_ref.dtype)
        halo = jnp.where(i > 0, halo_ref[...], 0.0)
        x = x_ref[...]
        row8 = _iota((8, WIDTH), 0)
        part = jnp.zeros((8, WIDTH), F32)
        for j in range(GDN_CONV):
            s = jnp.sum(dcv * _shift_down(x, halo, GDN_CONV - 1 - j), axis=0, keepdims=True)
            part += jnp.where(row8 == j, s, 0.0)

        @pl.when(i == 0)
        def _():
            dw_ref[...] = jnp.zeros_like(dw_ref)

        dw_ref[...] += part

    return pl.pallas_call(
        body, name="conv_bwd", grid=(3, nt),
        in_specs=[pl.BlockSpec(memory_space=pl.ANY),
                  pl.BlockSpec((tm, WIDTH), lambda c, i: (i, c)),
                  pl.BlockSpec((8, WIDTH), lambda c, i: (jnp.minimum((i + 1) * (tm // 8), t // 8 - 1), c)),
                  pl.BlockSpec((tm, WIDTH), lambda c, i: (i, cb + c)),
                  pl.BlockSpec((8, WIDTH), lambda c, i: (jnp.maximum(i * (tm // 8) - 1, 0), cb + c)),
                  pl.BlockSpec((GDN_CONV, WIDTH), lambda c, i: (0, c))],
        out_specs=[pl.BlockSpec((tm, WIDTH), lambda c, i: (i, cb + c)),
                   pl.BlockSpec((8, WIDTH), lambda c, i: (0, c))],
        out_shape=[jax.ShapeDtypeStruct((t, PACKED_WIDTH), MXU_DTYPE), jax.ShapeDtypeStruct((8, 3 * WIDTH), F32)],
        input_output_aliases={0: 0},
        compiler_params=_params(("parallel", "arbitrary")),
    )(dproj, dc, dc, proj, proj, conv_w)


def _chunk_matrices(tm):
    r, c = _iota((tm, tm), 0), _iota((tm, tm), 1)
    same = (r // GDN_CHUNK) == (c // GDN_CHUNK)
    return jnp.where(same & (c <= r), 1.0, 0.0), jnp.where(same, 1.0, 0.0)


def _gdn_gates(ba, a_log, dt_bias):
    al = ba + dt_bias
    return _sigmoid(ba), -jnp.exp(a_log) * _softplus(al), _sigmoid(al)


def _head_lane_eye():
    return jnp.where(_iota((HEADS, 128), 1) == _iota((HEADS, 128), 0) + HEADS, 1.0, 0.0)


def _gdn_prep(conv, proj, a_log, dt_bias, tm=256):
    t = proj.shape[0]
    nc = tm // GDN_CHUNK

    def body(c_ref, ba_ref, al_ref, dt_ref, q_ref, k_ref, v_ref, b_ref, g_ref, gl_ref, grow_ref):
        c = c_ref[...]
        a = c * _sigmoid(c)
        beta, g, _ = _gdn_gates(ba_ref[:, 0:128], al_ref[...], dt_ref[...])
        lmat, cmat = _chunk_matrices(tm)
        gc = jnp.dot(lmat, g, precision=HIGHEST, preferred_element_type=F32)
        gl = jnp.dot(cmat, g, precision=HIGHEST, preferred_element_type=F32)
        grow = lax.dot_general(_head_lane_eye(), gc, _NT, precision=HIGHEST, preferred_element_type=F32)
        for h in range(HEADS):
            sl = slice(h * HEAD_DIM, (h + 1) * HEAD_DIM)
            qh, kh, vh = a[:, sl], a[:, WIDTH + h * HEAD_DIM:WIDTH + (h + 1) * HEAD_DIM], a[:, 2 * WIDTH + h * HEAD_DIM:2 * WIDTH + (h + 1) * HEAD_DIM]
            rq = lax.rsqrt(jnp.sum(qh * qh, axis=1, keepdims=True) + NORM_EPS)
            rk = lax.rsqrt(jnp.sum(kh * kh, axis=1, keepdims=True) + NORM_EPS)
            q_ref[h] = qh * (rq * HEAD_DIM ** -0.5)
            k_ref[h] = kh * rk
            v_ref[h] = vh
            b_ref[h] = jnp.broadcast_to(beta[:, h:h + 1], (tm, HEAD_DIM))
            g_ref[h] = jnp.broadcast_to(gc[:, HEADS + h:HEADS + h + 1], (tm, HEAD_DIM))
            gl_ref[h] = jnp.broadcast_to(gl[:, HEADS + h:HEADS + h + 1], (tm, HEAD_DIM))
            for cc in range(nc):
                grow_ref[h, cc] = grow[h:h + 1, cc * GDN_CHUNK:(cc + 1) * GDN_CHUNK]

    hm = pl.BlockSpec((HEADS, tm, HEAD_DIM), lambda i: (0, i, 0))
    small = pl.BlockSpec((1, 128), lambda i: (0, 0))
    hm_shape = jax.ShapeDtypeStruct((HEADS, t, HEAD_DIM), F32)
    return pl.pallas_call(
        body, name="gdn_prep", grid=(t // tm,),
        in_specs=[pl.BlockSpec((tm, 3 * WIDTH), lambda i: (i, 0)),
                  pl.BlockSpec((tm, WIDTH), lambda i: (i, SEG_BA // WIDTH)), small, small],
        out_specs=[hm] * 6 + [pl.BlockSpec((HEADS, nc, 1, GDN_CHUNK), lambda i: (0, i, 0, 0))],
        out_shape=[hm_shape] * 6 + [jax.ShapeDtypeStruct((HEADS, t // GDN_CHUNK, 1, GDN_CHUNK), F32)],
        compiler_params=_params(("parallel",)),
    )(conv, proj, a_log, dt_bias)


def _gdn_prep_bwd(dproj, conv, proj, a_log, dt_bias, dq, dk, dv, db, dg, dgl, dgrow, tm=256):
    t = proj.shape[0]
    nc = tm // GDN_CHUNK

    def body(dp_ref, c_ref, ba_ref, al_ref, dt_ref, dq_ref, dk_ref, dv_ref, db_ref, dg_ref, dgl_ref, dgrow_ref,
             dba_ref, dc_ref, small_ref, row_scr):
        c = c_ref[...]
        sg = _sigmoid(c)
        a = c * sg
        beta, g, sig_al = _gdn_gates(ba_ref[:, 0:128], al_ref[...], dt_ref[...])
        lane = _iota((1, 128), 1)
        d_beta = jnp.zeros((tm, 128), F32)
        d_gc = jnp.zeros((tm, 128), F32)
        d_gl = jnp.zeros((tm, 128), F32)
        for h in range(HEADS):
            for cc in range(nc):
                row_scr[h:h + 1, cc * GDN_CHUNK:(cc + 1) * GDN_CHUNK] = dgrow_ref[h, cc]
            d_beta += jnp.sum(db_ref[h], axis=1, keepdims=True) * jnp.where(lane == h, 1.0, 0.0)
            d_gc += jnp.sum(dg_ref[h], axis=1, keepdims=True) * jnp.where(lane == HEADS + h, 1.0, 0.0)
            d_gl += jnp.sum(dgl_ref[h], axis=1, keepdims=True) * jnp.where(lane == HEADS + h, 1.0, 0.0)
            qs, ks, vs = (slice(h * HEAD_DIM, (h + 1) * HEAD_DIM), slice(WIDTH + h * HEAD_DIM, WIDTH + (h + 1) * HEAD_DIM),
                          slice(2 * WIDTH + h * HEAD_DIM, 2 * WIDTH + (h + 1) * HEAD_DIM))
            qh, kh = a[:, qs], a[:, ks]
            rq = lax.rsqrt(jnp.sum(qh * qh, axis=1, keepdims=True) + NORM_EPS)
            rk = lax.rsqrt(jnp.sum(kh * kh, axis=1, keepdims=True) + NORM_EPS)
            gq, gk = dq_ref[h] * HEAD_DIM ** -0.5, dk_ref[h]
            da_q = rq * gq - qh * (rq * rq * rq) * jnp.sum(gq * qh, axis=1, keepdims=True)
            da_k = rk * gk - kh * (rk * rk * rk) * jnp.sum(gk * kh, axis=1, keepdims=True)
            dsilu = lambda s, x: s * (1.0 + x * (1.0 - s))
            dc_ref[:, qs] = da_q * dsilu(sg[:, qs], c[:, qs])
            dc_ref[:, ks] = da_k * dsilu(sg[:, ks], c[:, ks])
            dc_ref[:, vs] = dv_ref[h] * dsilu(sg[:, vs], c[:, vs])
        d_gc += lax.dot_general(row_scr[...], _head_lane_eye(), _TN, precision=HIGHEST, preferred_element_type=F32)
        lmat, cmat = _chunk_matrices(tm)
        d_g = (lax.dot_general(lmat, d_gc, _TN, precision=HIGHEST, preferred_element_type=F32)
               + lax.dot_general(cmat, d_gl, _TN, precision=HIGHEST, preferred_element_type=F32))
        d_al = d_g * (-jnp.exp(al_ref[...])) * sig_al
        d_bl = d_beta * beta * (1.0 - beta)
        dba_ref[...] = jnp.concatenate([d_bl + d_al, jnp.zeros((tm, WIDTH - 128), F32)], axis=1).astype(dba_ref.dtype)
        row8 = _iota((8, 128), 0)
        part = (jnp.where(row8 == 0, jnp.sum(d_g * g, axis=0, keepdims=True), 0.0)
                + jnp.where(row8 == 1, jnp.sum(d_al, axis=0, keepdims=True), 0.0))

        @pl.when(pl.program_id(0) == 0)
        def _():
            small_ref[...] = jnp.zeros_like(small_ref)

        small_ref[...] += part

    hm = pl.BlockSpec((HEADS, tm, HEAD_DIM), lambda i: (0, i, 0))
    small = pl.BlockSpec((1, 128), lambda i: (0, 0))
    seg = pl.BlockSpec((tm, WIDTH), lambda i: (i, SEG_BA // WIDTH))
    return pl.pallas_call(
        body, name="gdn_prep_bwd", grid=(t // tm,),
        in_specs=[pl.BlockSpec(memory_space=pl.ANY), pl.BlockSpec((tm, 3 * WIDTH), lambda i: (i, 0)), seg, small, small]
        + [hm] * 6 + [pl.BlockSpec((HEADS, nc, 1, GDN_CHUNK), lambda i: (0, i, 0, 0))],
        out_specs=[seg, pl.BlockSpec((tm, 3 * WIDTH), lambda i: (i, 0)), pl.BlockSpec((8, 128), lambda i: (0, 0))],
        out_shape=[jax.ShapeDtypeStruct((t, PACKED_WIDTH), MXU_DTYPE), jax.ShapeDtypeStruct((t, 3 * WIDTH), F32),
                   jax.ShapeDtypeStruct((8, 128), F32)],
        scratch_shapes=[pltpu.VMEM((HEADS, tm), F32)],
        input_output_aliases={0: 0},
        compiler_params=_params(("arbitrary",)),
    )(dproj, conv, proj, a_log, dt_bias, dq, dk, dv, db, dg, dgl, dgrow)


_BNN = (((2,), (1,)), ((0,), (0,)))
_BNT = (((2,), (2,)), ((0,), (0,)))
_BTN = (((1,), (1,)), ((0,), (0,)))


@jax.custom_vjp
def _MM_NN(a, b):
    return _mxu(a, b, _BNN)


@jax.custom_vjp
def _MM_NT(a, b):
    return _mxu(a, b, _BNT)


@jax.custom_vjp
def _MM_TN(a, b):
    return _mxu(a, b, _BTN)


_MM_NN.defvjp(lambda a, b: (_mxu(a, b, _BNN), (a, b)), lambda r, g: (_mxu(g, r[1], _BNT), _mxu(r[0], g, _BTN)))
_MM_NT.defvjp(lambda a, b: (_mxu(a, b, _BNT), (a, b)), lambda r, g: (_mxu(g, r[1], _BNN), _mxu(g, r[0], _BTN)))
_MM_TN.defvjp(lambda a, b: (_mxu(a, b, _BTN), (a, b)), lambda r, g: (_mxu(r[1], g, _BNT), _mxu(r[0], g, _BNN)))


def _split(a):
    hi = a.astype(MXU_DTYPE)
    return hi, (a - hi.astype(F32)).astype(MXU_DTYPE)


def _dot3(a, b, dims):
    (ah, al), (bh, bl) = a, b
    (ca,), (cb,) = dims[0]
    return lax.dot_general(jnp.concatenate([ah, ah, al], axis=ca), jnp.concatenate([bh, bl, bh], axis=cb), dims,
                           preferred_element_type=F32)


def _unit_lower_inverse(a):
    c = GDN_CHUNK
    eye = jnp.where(_iota((c, c), 0) == _iota((c, c), 1), 1.0, 0.0)
    x = eye - a
    p = a
    for _ in range(5):
        ps = _split(p)
        p = _dot3(ps, ps, _BNN)
        x = x + _dot3(_split(x), _split(p), _BNN)
    return x


@jax.custom_vjp
def _SAVED_INVERSE(a, t_inv):
    return t_inv


def _saved_inverse_bwd(t_inv, g):
    ts = _split(t_inv)
    return -_dot3(ts, _split(_dot3(_split(g), ts, _BNT)), _BTN), jnp.zeros_like(t_inv)


_SAVED_INVERSE.defvjp(lambda a, t_inv: (t_inv, t_inv), _saved_inverse_bwd)


def _gdn_chunk(q, k, v, beta, g1, g2, gl, state, t_inv=None):
    c = GDN_CHUNK
    if t_inv is None:
        _mm_nn, _mm_nt, _mm_tn = (functools.partial(_mxu, dims=dd) for dd in (_BNN, _BNT, _BTN))
    else:
        _mm_nn, _mm_nt, _mm_tn = _MM_NN, _MM_NT, _MM_TN
    row, col = _iota((c, c), 0), _iota((c, c), 1)
    incl, strict = row >= col, row > col
    decay = jnp.where(incl, jnp.exp(jnp.where(incl, g1 - g2, 0.0)), 0.0)
    eg = jnp.exp(g1)
    kb = k * beta
    a = _mm_nt(kb, k) * jnp.where(strict, decay, 0.0)
    inv = _unit_lower_inverse(a) if t_inv is None else _SAVED_INVERSE(a, t_inv)
    u = _mm_nn(inv, v * beta)
    w = _mm_nn(inv, kb * eg)
    attn = _mm_nt(q, k) * decay
    v_new = u - _mm_nn(w, state)
    o = _mm_nn(q * eg, state) + _mm_nn(attn, v_new)
    new_state = state * jnp.exp(gl) + _mm_tn(k * jnp.exp(gl - g1), v_new)
    return (o, new_state, inv) if t_inv is None else (o, new_state)


def _gdn_fwd(q, k, v, beta, g, gl, grow, cpb=2):
    t = q.shape[1]
    rows = cpb * GDN_CHUNK

    def body(q_ref, k_ref, v_ref, b_ref, g_ref, gl_ref, grow_ref, o_ref, st_ref, inv_ref, state):
        @pl.when(pl.program_id(0) == 0)
        def _():
            state[...] = jnp.zeros_like(state)

        s = state[...]
        for cc in range(cpb):
            sl = slice(cc * GDN_CHUNK, (cc + 1) * GDN_CHUNK)
            st_ref[:, cc] = s
            g2 = jnp.broadcast_to(grow_ref[:, cc], (HEADS, GDN_CHUNK, GDN_CHUNK))
            o, s, inv = _gdn_chunk(q_ref[:, sl, :], k_ref[:, sl, :], v_ref[:, sl, :], b_ref[:, sl, :], g_ref[:, sl, :], g2,
                                   gl_ref[:, sl, :], s)
            o_ref[:, sl, :] = o
            inv_ref[:, cc] = inv
        state[...] = s

    hm = pl.BlockSpec((HEADS, rows, HEAD_DIM), lambda i: (0, i, 0))
    per_chunk = pl.BlockSpec((HEADS, cpb, GDN_CHUNK, HEAD_DIM), lambda i: (0, i, 0, 0))
    chunk_shape = jax.ShapeDtypeStruct((HEADS, t // GDN_CHUNK, GDN_CHUNK, HEAD_DIM), F32)
    return pl.pallas_call(
        body, name="gdn_fwd", grid=(t // rows,),
        in_specs=[hm] * 6 + [pl.BlockSpec((HEADS, cpb, 1, GDN_CHUNK), lambda i: (0, i, 0, 0))],
        out_specs=[hm, per_chunk, per_chunk],
        out_shape=[jax.ShapeDtypeStruct((HEADS, t, HEAD_DIM), F32), chunk_shape, chunk_shape],
        scratch_shapes=[pltpu.VMEM((HEADS, GDN_CHUNK, HEAD_DIM), F32)],
        compiler_params=_params(("arbitrary",)),
    )(q, k, v, beta, g, gl, grow)


def _gdn_bwd(q, k, v, beta, g, gl, grow, states, invs, do, cpb=1):
    t = q.shape[1]
    rows = cpb * GDN_CHUNK
    nsteps = t // rows

    def body(q_ref, k_ref, v_ref, b_ref, g_ref, gl_ref, grow_ref, st_ref, inv_ref, do_ref,
             dq_ref, dk_ref, dv_ref, db_ref, dg_ref, dgl_ref, dgrow_ref, dstate):
        @pl.when(pl.program_id(0) == 0)
        def _():
            dstate[...] = jnp.zeros_like(dstate)

        ds = dstate[...]
        for cc in reversed(range(cpb)):
            sl = slice(cc * GDN_CHUNK, (cc + 1) * GDN_CHUNK)
            g2 = jnp.broadcast_to(grow_ref[:, cc], (HEADS, GDN_CHUNK, GDN_CHUNK))
            _, vjp = jax.vjp(_gdn_chunk, q_ref[:, sl, :], k_ref[:, sl, :], v_ref[:, sl, :], b_ref[:, sl, :],
                             g_ref[:, sl, :], g2, gl_ref[:, sl, :], st_ref[:, cc], inv_ref[:, cc])
            gq, gk, gv, gb, gg1, gg2, ggl, ds, _ = vjp((do_ref[:, sl, :], ds))
            dq_ref[:, sl, :] = gq
            dk_ref[:, sl, :] = gk
            dv_ref[:, sl, :] = gv
            db_ref[:, sl, :] = gb
            dg_ref[:, sl, :] = gg1
            dgl_ref[:, sl, :] = ggl
            dgrow_ref[:, cc] = jnp.sum(gg2, axis=1, keepdims=True)
        dstate[...] = ds

    hm = pl.BlockSpec((HEADS, rows, HEAD_DIM), lambda i: (0, nsteps - 1 - i, 0))
    rowspec = pl.BlockSpec((HEADS, cpb, 1, GDN_CHUNK), lambda i: (0, nsteps - 1 - i, 0, 0))
    per_chunk = pl.BlockSpec((HEADS, cpb, GDN_CHUNK, HEAD_DIM), lambda i: (0, nsteps - 1 - i, 0, 0))
    hm_shape = jax.ShapeDtypeStruct((HEADS, t, HEAD_DIM), F32)
    return pl.pallas_call(
        body, name="gdn_bwd", grid=(nsteps,),
        in_specs=[hm] * 6 + [rowspec, per_chunk, per_chunk, hm],
        out_specs=[hm] * 6 + [rowspec],
        out_shape=[hm_shape] * 6 + [jax.ShapeDtypeStruct((HEADS, t // GDN_CHUNK, 1, GDN_CHUNK), F32)],
        scratch_shapes=[pltpu.VMEM((HEADS, GDN_CHUNK, HEAD_DIM), F32)],
        compiler_params=_params(("arbitrary",)),
    )(q, k, v, beta, g, gl, grow, states, invs, do)


def _gdn_out(o_hm, gdn_norm_w, proj, tm=256):
    t = proj.shape[0]

    def body(o_ref, w_ref, z_ref, oz_ref):
        z = z_ref[...]
        gate = z * _sigmoid(z)
        w = w_ref[...]
        for h in range(HEADS):
            sl = slice(h * HEAD_DIM, (h + 1) * HEAD_DIM)
            o = o_ref[h]
            r = lax.rsqrt(jnp.mean(o * o, axis=1, keepdims=True) + NORM_EPS)
            oz_ref[:, sl] = (o * r * w * gate[:, sl]).astype(oz_ref.dtype)

    tok = pl.BlockSpec((tm, WIDTH), lambda i: (i, 0))
    return pl.pallas_call(
        body, name="gdn_out", grid=(t // tm,),
        in_specs=[pl.BlockSpec((HEADS, tm, HEAD_DIM), lambda i: (0, i, 0)), pl.BlockSpec((1, HEAD_DIM), lambda i: (0, 0)),
                  pl.BlockSpec((tm, WIDTH), lambda i: (i, SEG_ZB // WIDTH))],
        out_specs=tok, out_shape=jax.ShapeDtypeStruct((t, WIDTH), MXU_DTYPE),
        compiler_params=_params(("parallel",)),
    )(o_hm, gdn_norm_w, proj)


def _gdn_out_bwd(dproj, d_oz, o_hm, gdn_norm_w, proj, tm=256):
    t = proj.shape[0]

    def body(dp_ref, doz_ref, o_ref, w_ref, z_ref, dz_ref, do_ref, dw_ref):
        z = z_ref[...]
        sg = _sigmoid(z)
        gate = z * sg
        dgate = sg * (1.0 + z * (1.0 - sg))
        w = w_ref[...]
        dw = jnp.zeros((1, HEAD_DIM), F32)
        for h in range(HEADS):
            sl = slice(h * HEAD_DIM, (h + 1) * HEAD_DIM)
            o = o_ref[h]
            g = doz_ref[:, sl]
            r = lax.rsqrt(jnp.mean(o * o, axis=1, keepdims=True) + NORM_EPS)
            on = o * r * w
            dz_ref[:, sl] = (g * on * dgate[:, sl]).astype(dz_ref.dtype)
            dn = g * gate[:, sl]
            dw += jnp.sum(dn * o * r, axis=0, keepdims=True)
            dnw = dn * w
            do_ref[h] = r * dnw - o * (r * r * r) * jnp.mean(dnw * o, axis=1, keepdims=True)

        @pl.when(pl.program_id(0) == 0)
        def _():
            dw_ref[...] = jnp.zeros_like(dw_ref)

        dw_ref[...] += jnp.concatenate([dw, jnp.zeros((7, HEAD_DIM), F32)], axis=0)

    tok = pl.BlockSpec((tm, WIDTH), lambda i: (i, 0))
    seg = pl.BlockSpec((tm, WIDTH), lambda i: (i, SEG_ZB // WIDTH))
    hm = pl.BlockSpec((HEADS, tm, HEAD_DIM), lambda i: (0, i, 0))
    return pl.pallas_call(
        body, name="gdn_out_bwd", grid=(t // tm,),
        in_specs=[pl.BlockSpec(memory_space=pl.ANY), tok, hm, pl.BlockSpec((1, HEAD_DIM), lambda i: (0, 0)), seg],
        out_specs=[seg, hm, pl.BlockSpec((8, HEAD_DIM), lambda i: (0, 0))],
        out_shape=[jax.ShapeDtypeStruct((t, PACKED_WIDTH), MXU_DTYPE), jax.ShapeDtypeStruct((HEADS, t, HEAD_DIM), F32),
                   jax.ShapeDtypeStruct((8, HEAD_DIM), F32)],
        input_output_aliases={0: 0},
        compiler_params=_params(("arbitrary",)),
    )(dproj, d_oz, o_hm, gdn_norm_w, proj)


def _merge(y_a, y_b, proj, tm=256):
    t = proj.shape[0]

    def body(ya_ref, yb_ref, ga_ref, gb_ref, m_ref):
        m_ref[...] = (_sigmoid(ga_ref[...]) * ya_ref[...] + _sigmoid(gb_ref[...]) * yb_ref[...]).astype(m_ref.dtype)

    half = pl.BlockSpec((tm, WIDTH), lambda i, c: (i, c))
    return pl.pallas_call(
        body, name="merge", grid=(t // tm, 2),
        in_specs=[half, half, pl.BlockSpec((tm, WIDTH), lambda i, c: (i, SEG_GA // WIDTH + c)),
                  pl.BlockSpec((tm, WIDTH), lambda i, c: (i, SEG_GB // WIDTH + c))],
        out_specs=half, out_shape=jax.ShapeDtypeStruct((t, D_MODEL), MXU_DTYPE),
        compiler_params=_params(("parallel", "parallel")),
    )(y_a, y_b, proj, proj)


def _merge_bwd(dproj, d_m, y, proj, seg, name, tm=256):
    t = proj.shape[0]

    def body(*refs):
        dm_ref, y_ref, g_ref, dg_ref, dy_ref = refs[-5:]
        dm = dm_ref[...]
        s = _sigmoid(g_ref[...])
        dy_ref[...] = (dm * s).astype(dy_ref.dtype)
        dg_ref[...] = (dm * y_ref[...] * s * (1.0 - s)).astype(dg_ref.dtype)

    half = pl.BlockSpec((tm, WIDTH), lambda i, c: (i, c))
    gate = pl.BlockSpec((tm, WIDTH), lambda i, c: (i, seg // WIDTH + c))
    specs, args, aliases = [half, half, gate], [d_m, y, proj], {}
    if dproj is not None:
        specs, args, aliases = [pl.BlockSpec(memory_space=pl.ANY)] + specs, [dproj] + args, {0: 0}
    return pl.pallas_call(
        body, name=name, grid=(t // tm, 2), in_specs=specs, out_specs=[gate, half],
        out_shape=[jax.ShapeDtypeStruct((t, PACKED_WIDTH), MXU_DTYPE), jax.ShapeDtypeStruct((t, D_MODEL), MXU_DTYPE)],
        input_output_aliases=aliases,
        compiler_params=_params(("parallel", "parallel")),
    )(*args)


def _tail(x, mo, final_w, target, tm=256):
    t = x.shape[0]

    def body(x_ref, mo_ref, w_ref, t_ref, dxm_ref, dx_ref, loss_ref, dw_ref):
        x2 = x_ref[...] + mo_ref[...]
        w = w_ref[...]
        r = lax.rsqrt(jnp.mean(x2 * x2, axis=-1, keepdims=True) + NORM_EPS)
        xn = x2 * r
        err = xn * w - t_ref[...]
        dy = err * (1.0 / D_MODEL)
        dyw = dy * w
        dx2 = r * dyw - x2 * (r * r * r) * jnp.mean(dyw * x2, axis=-1, keepdims=True)
        dx_ref[...] = dx2
        dxm_ref[...] = dx2.astype(dxm_ref.dtype)
        loss = 0.5 * jnp.sum(jnp.sum(err * err, axis=-1, keepdims=True) * (1.0 / D_MODEL), axis=0, keepdims=True)
        onehot = jnp.where((_iota((8, 128), 0) == 0) & (_iota((8, 128), 1) == 0), 1.0, 0.0)

        @pl.when(pl.program_id(0) == 0)
        def _():
            loss_ref[...] = jnp.zeros_like(loss_ref)
            dw_ref[...] = jnp.zeros_like(dw_ref)

        loss_ref[...] += loss * onehot
        dw_ref[...] += jnp.where(_iota((8, D_MODEL), 0) == 0, jnp.sum(dy * xn, axis=0, keepdims=True), 0.0)

    tok = pl.BlockSpec((tm, D_MODEL), lambda i: (i, 0))
    return pl.pallas_call(
        body, name="tail", grid=(t // tm,),
        in_specs=[tok, tok, pl.BlockSpec((1, D_MODEL), lambda i: (0, 0)), tok],
        out_specs=[tok, tok, pl.BlockSpec((8, 128), lambda i: (0, 0)), pl.BlockSpec((8, D_MODEL), lambda i: (0, 0))],
        out_shape=[jax.ShapeDtypeStruct((t, D_MODEL), MXU_DTYPE), jax.ShapeDtypeStruct((t, D_MODEL), F32),
                   jax.ShapeDtypeStruct((8, 128), F32), jax.ShapeDtypeStruct((8, D_MODEL), F32)],
        compiler_params=_params(("arbitrary",)),
    )(x, mo, final_w, target)


def _norm_bwd(x, norm_w, dh, dx2, tm=256):
    t = x.shape[0]

    def body(x_ref, w_ref, dh_ref, dx2_ref, dx_ref, dw_ref):
        xf, w, dh_ = x_ref[...], w_ref[...], dh_ref[...]
        r = lax.rsqrt(jnp.mean(xf * xf, axis=-1, keepdims=True) + NORM_EPS)
        dhw = dh_ * w
        dx_ref[...] = dx2_ref[...] + r * dhw - xf * (r * r * r) * jnp.mean(dhw * xf, axis=-1, keepdims=True)

        @pl.when(pl.program_id(0) == 0)
        def _():
            dw_ref[...] = jnp.zeros_like(dw_ref)

        dw_ref[...] += jnp.where(_iota((8, D_MODEL), 0) == 0, jnp.sum(dh_ * xf * r, axis=0, keepdims=True), 0.0)

    tok = pl.BlockSpec((tm, D_MODEL), lambda i: (i, 0))
    return pl.pallas_call(
        body, name="norm_bwd", grid=(t // tm,),
        in_specs=[tok, pl.BlockSpec((1, D_MODEL), lambda i: (0, 0)), tok, tok],
        out_specs=[tok, pl.BlockSpec((8, D_MODEL), lambda i: (0, 0))],
        out_shape=[jax.ShapeDtypeStruct((t, D_MODEL), F32), jax.ShapeDtypeStruct((8, D_MODEL), F32)],
        compiler_params=_params(("arbitrary",)),
    )(x, norm_w, dh, dx2)


def _local_step(x, target, norm_w, wp, conv_w, a_log, dt_bias, gdn_norm_w, w_up_a, w_up_b, w_out, final_w):
    t = x.shape[0]
    tables = _rope_tables(t)
    a_log = jnp.pad(a_log, ((0, 0), (HEADS, 128 - 2 * HEADS)))
    dt_bias = jnp.pad(dt_bias, ((0, 0), (HEADS, 128 - 2 * HEADS)))

    proj, h = _norm_proj(x, norm_w, wp)
    qkvs = _rope_fwd(proj, tables)
    outs, lses = zip(*[_att_fwd(qkvs[gi], d, f"att_fwd{gi}") for gi, d in enumerate(DILATIONS)])
    oz_a, o_a, *lse_views = _att_merge(outs, lses, proj)
    conv = _conv_fwd(proj, conv_w)
    gq, gk, gv, gb, gg, ggl, grow = _gdn_prep(conv, proj, a_log, dt_bias)
    o_b, states, invs = _gdn_fwd(gq, gk, gv, gb, gg, ggl, grow)
    oz_b = _gdn_out(o_b, gdn_norm_w, proj)
    big = dict(tm=1024, tn=1024, tk=1024)
    y_a = _matmul(oz_a, w_up_a, "nn", "up_a", **big)
    y_b = _matmul(oz_b, w_up_b, "nn", "up_b", **big)
    merged = _merge(y_a, y_b, proj)
    mo = _matmul(merged, w_out, "nn", "out_proj", **big)
    dx2_m, dx2, loss_blk, d_final = _tail(x, mo, final_w, target)

    d_wout = _matmul(merged, dx2_m, "tn", "d_w_out", **big)
    d_m = _matmul(dx2_m, w_out, "nt", "d_merged", **big)
    dproj, dy_a = _merge_bwd(None, d_m, y_a, proj, SEG_GA, "merge_bwd_a")
    dproj, dy_b = _merge_bwd(dproj, d_m, y_b, proj, SEG_GB, "merge_bwd_b")
    d_wua = _matmul(oz_a, dy_a, "tn", "d_w_up_a", **big)
    d_wub = _matmul(oz_b, dy_b, "tn", "d_w_up_b", **big)
    d_oz_a = _matmul(dy_a, w_up_a, "nt", "d_oz_a", **big)
    d_oz_b = _matmul(dy_b, w_up_b, "nt", "d_oz_b", **big)
    dproj, *views = _att_merge_bwd(dproj, d_oz_a, o_a, proj)
    do_views, delta_views = views[:3], views[3:]
    dqkvs = [_att_bwd(qkvs[gi], do_views[gi], lse_views[gi], delta_views[gi], d, f"att_bwd{gi}")
             for gi, d in enumerate(DILATIONS)]
    dproj = _rope_bwd(dproj, dqkvs, tables)
    dproj, do_b, d_gnw = _gdn_out_bwd(dproj, d_oz_b, o_b, gdn_norm_w, proj)
    dgq, dgk, dgv, dgb, dgg, dggl, dgrow = _gdn_bwd(gq, gk, gv, gb, gg, ggl, grow, states, invs, do_b)
    dproj, dconv, d_small = _gdn_prep_bwd(dproj, conv, proj, a_log, dt_bias, dgq, dgk, dgv, dgb, dgg, dggl, dgrow)
    dproj, d_convw = _conv_bwd(dproj, dconv, proj, conv_w)
    d_wp = _matmul(h, dproj, "tn", "d_w_in", tm=1024, tn=512, tk=1024)
    dh = _matmul(dproj, wp, "nt", "d_h", tm=1024, tn=1024, tk=512)
    grad_x, d_norm = _norm_bwd(x, norm_w, dh, dx2)
    return dict(loss=loss_blk, grad_x=grad_x, norm_w=d_norm[0:1], w_in=d_wp, conv_w=d_convw[0:GDN_CONV],
                a_log=d_small[0:1, HEADS:2 * HEADS], dt_bias=d_small[1:2, HEADS:2 * HEADS], gdn_norm_w=d_gnw[0:1],
                w_up_a=d_wua, w_up_b=d_wub,
                w_out=d_wout, final_norm_w=d_final[0:1])


SHARDS = 4
W_IN_SHARD = IN_WIDTH // SHARDS
ROWS_W_IN = D_MODEL * W_IN_SHARD // 128
ROWS_UP = WIDTH * (D_MODEL // SHARDS) // 128
ROWS_OUT = (D_MODEL // SHARDS) * D_MODEL // 128
CONV_SHARD = 3 * WIDTH // SHARDS
ROWS_CONV = 16
SLAB_ROWS = ROWS_W_IN + 2 * ROWS_UP + ROWS_OUT + 2 * ROWS_CONV
HALF_ROWS = SLAB_ROWS // 2
MESH = pl.DeviceIdType.MESH
ANY = pl.BlockSpec(memory_space=pl.ANY)


def _pad_rows(a, rows):
    return jnp.pad(a, ((0, rows - a.shape[0]), (0, 0)))


def _pack_slab(w_in, w_up_a, w_up_b, w_out, conv, conv_lo):
    parts = [w_in.reshape(ROWS_W_IN, 128), w_up_a.reshape(ROWS_UP, 128), w_up_b.reshape(ROWS_UP, 128),
             w_out.reshape(ROWS_OUT, 128), _pad_rows(conv.reshape(-1, 128), ROWS_CONV), _pad_rows(conv_lo.reshape(-1, 128), ROWS_CONV)]
    return jnp.concatenate(parts, axis=0)


def _unpack_slab(slab):
    r0 = 0
    out = []
    for rows, shape in ((ROWS_W_IN, (D_MODEL, W_IN_SHARD)), (ROWS_UP, (WIDTH, D_MODEL // SHARDS)), (ROWS_UP, (WIDTH, D_MODEL // SHARDS)),
                        (ROWS_OUT, (D_MODEL // SHARDS, D_MODEL)), (ROWS_CONV, None), (ROWS_CONV, None)):
        part = slab[r0:r0 + rows]
        out.append(part[:GDN_CONV * CONV_SHARD // 128].reshape(GDN_CONV, CONV_SHARD) if shape is None else part.reshape(shape))
        r0 += rows
    return out


def _mesh_position():
    x, y, c = lax.axis_index("x"), lax.axis_index("y"), lax.axis_index("c")
    return x, y, c, [(1 - x, y), (x, 1 - y), (1 - x, 1 - y)]


def _gather_weights(slab):
    def body(slab_ref, out_ref, send_sems, recv_sems, local_sem):
        x, y, c, chips = _mesh_position()
        mine = pltpu.make_async_copy(slab_ref, out_ref.at[2 * x + y], local_sem)
        mine.start()

        def half(chip, which):
            return out_ref.at[2 * chip[0] + chip[1], which]

        def copy(k, src, dst, to):
            return pltpu.make_async_remote_copy(src_ref=src, dst_ref=dst, send_sem=send_sems.at[k], recv_sem=recv_sems.at[k],
                                                device_id=to, device_id_type=MESH)

        first = [copy(j, slab_ref.at[c], half((x, y), c), (*chip, c)) for j, chip in enumerate(chips)]
        for cp in first:
            cp.start()
        passed = [copy(3 + j, half(chip, c), half(chip, c), (x, y, 1 - c)) for j, chip in enumerate(chips)]
        for j, chip in enumerate(chips):
            copy(j, half(chip, c), half(chip, c), (x, y, c)).wait_recv()
            passed[j].start()
        for j, chip in enumerate(chips):
            copy(3 + j, half(chip, 1 - c), half(chip, 1 - c), (x, y, c)).wait_recv()
        for cp in first + passed:
            cp.wait_send()
        mine.wait()

    return pl.pallas_call(
        body, name="gather_weights", in_specs=[ANY], out_specs=ANY,
        out_shape=jax.ShapeDtypeStruct((SHARDS, 2, HALF_ROWS, 128), slab.dtype),
        scratch_shapes=[pltpu.SemaphoreType.DMA((6,)), pltpu.SemaphoreType.DMA((6,)), pltpu.SemaphoreType.DMA],
    )(slab)


def _exchange_halves(grads):
    def body(g_ref, out_ref, send_sems, recv_sems):
        x, y, c, _ = _mesh_position()
        copies = [pltpu.make_async_remote_copy(src_ref=g_ref.at[s, 1 - c], dst_ref=out_ref.at[s],
                                               send_sem=send_sems.at[s], recv_sem=recv_sems.at[s],
                                               device_id=(x, y, 1 - c), device_id_type=MESH) for s in range(SHARDS)]
        for cp in copies:
            cp.start()
        for cp in copies:
            cp.wait()

    return pl.pallas_call(
        body, name="exchange_halves", in_specs=[ANY], out_specs=ANY,
        out_shape=jax.ShapeDtypeStruct((SHARDS, HALF_ROWS, 128), F32),
        scratch_shapes=[pltpu.SemaphoreType.DMA((SHARDS,)), pltpu.SemaphoreType.DMA((SHARDS,))],    )(grads)


SUM_ROWS = HALF_ROWS // 2


def _pair_sum(grads, recv):
    nblk = HALF_ROWS // SUM_ROWS

    def body(c_ref, g_ref, r_ref, o_ref):
        o_ref[...] = (g_ref[0] + r_ref[...]).astype(o_ref.dtype)

    spec = pl.BlockSpec((1, SUM_ROWS, 128), lambda s, i, c_ref: (s, i, 0))
    return pl.pallas_call(
        body, name="pair_sum",
        grid_spec=pltpu.PrefetchScalarGridSpec(
            num_scalar_prefetch=1, grid=(SHARDS, nblk),
            in_specs=[pl.BlockSpec((1, 1, SUM_ROWS, 128), lambda s, i, c_ref: (s, c_ref[0], i, 0)), spec],
            out_specs=spec),
        out_shape=jax.ShapeDtypeStruct((SHARDS, HALF_ROWS, 128), MXU_DTYPE),
        compiler_params=_params(("parallel", "parallel")),
    )(lax.axis_index("c").astype(jnp.int32).reshape(1), grads, recv)


def _scatter_pairs(pairs):
    def body(p_ref, out_ref, send_sems, recv_sems):
        x, y, c, chips = _mesh_position()
        copies = [pltpu.make_async_remote_copy(src_ref=p_ref.at[2 * chip[0] + chip[1]], dst_ref=out_ref.at[j],
                                               send_sem=send_sems.at[j], recv_sem=recv_sems.at[j],
                                               device_id=(*chip, c), device_id_type=MESH) for j, chip in enumerate(chips)]
        for cp in copies:
            cp.start()
        for cp in copies:
            cp.wait()

    return pl.pallas_call(
        body, name="scatter_pairs", in_specs=[ANY], out_specs=ANY,
        out_shape=jax.ShapeDtypeStruct((3, HALF_ROWS, 128), pairs.dtype),
        scratch_shapes=[pltpu.SemaphoreType.DMA((3,)), pltpu.SemaphoreType.DMA((3,))],    )(pairs)


def _chip_sum(pairs, recv):
    def body(own_ref, p_ref, r_ref, o_ref):
        o_ref[...] = ((p_ref[0].astype(F32) + r_ref[0].astype(F32)) + r_ref[1].astype(F32)) + r_ref[2].astype(F32)

    own = (2 * lax.axis_index("x") + lax.axis_index("y")).astype(jnp.int32).reshape(1)
    return pl.pallas_call(
        body, name="chip_sum",
        grid_spec=pltpu.PrefetchScalarGridSpec(
            num_scalar_prefetch=1, grid=(HALF_ROWS // SUM_ROWS,),
            in_specs=[pl.BlockSpec((1, SUM_ROWS, 128), lambda i, own_ref: (own_ref[0], i, 0)),
                      pl.BlockSpec((3, SUM_ROWS, 128), lambda i, own_ref: (0, i, 0))],
            out_specs=pl.BlockSpec((SUM_ROWS, 128), lambda i, own_ref: (i, 0))),
        out_shape=jax.ShapeDtypeStruct((HALF_ROWS, 128), F32),
        compiler_params=_params(("parallel",)),
    )(own, pairs, recv)


def _share_total(total_half):
    def body(t_ref, out_ref, send_sem, recv_sem, local_sem):
        x, y, c, _ = _mesh_position()
        mine = pltpu.make_async_copy(t_ref, out_ref.at[c], local_sem)
        mine.start()
        cp = pltpu.make_async_remote_copy(src_ref=t_ref, dst_ref=out_ref.at[c],
                                          send_sem=send_sem, recv_sem=recv_sem, device_id=(x, y, 1 - c), device_id_type=MESH)
        cp.start()
        other = out_ref.at[1 - c]
        pltpu.make_async_remote_copy(src_ref=other, dst_ref=other, send_sem=send_sem, recv_sem=recv_sem,
                                     device_id=(x, y, c), device_id_type=MESH).wait_recv()
        cp.wait_send()
        mine.wait()

    return pl.pallas_call(
        body, name="share_total", in_specs=[ANY], out_specs=ANY,
        out_shape=jax.ShapeDtypeStruct((2, HALF_ROWS, 128), F32),
        scratch_shapes=[pltpu.SemaphoreType.DMA, pltpu.SemaphoreType.DMA, pltpu.SemaphoreType.DMA],
    )(total_half)


def _allreduce_small(block):
    def body(b_ref, out_ref, gath, send_sems, recv_sems):
        x, y, c, _ = _mesh_position()
        me = 4 * x + 2 * y + c
        gath[me] = b_ref[...]
        copies = []
        for k in range(1, 8):
            peer = (x ^ (k >> 2), y ^ ((k >> 1) & 1), c ^ (k & 1))
            copies.append(pltpu.make_async_remote_copy(src_ref=b_ref, dst_ref=gath.at[me], send_sem=send_sems.at[k - 1],
                                                       recv_sem=recv_sems.at[k - 1], device_id=peer, device_id_type=MESH))
        for cp in copies:
            cp.start()
        for k in range(1, 8):
            src = 4 * (x ^ (k >> 2)) + 2 * (y ^ ((k >> 1) & 1)) + (c ^ (k & 1))
            pltpu.make_async_remote_copy(src_ref=b_ref, dst_ref=gath.at[src], send_sem=send_sems.at[k - 1],
                                         recv_sem=recv_sems.at[k - 1], device_id=(x, y, c), device_id_type=MESH).wait_recv()
        for cp in copies:
            cp.wait_send()
        acc = gath[0]
        for d in range(1, 8):
            acc = acc + gath[d]
        out_ref[...] = acc

    vm = pl.BlockSpec(memory_space=pltpu.VMEM)
    return pl.pallas_call(
        body, name="allreduce_small", in_specs=[vm], out_specs=vm,
        out_shape=jax.ShapeDtypeStruct((8, D_MODEL), F32),
        scratch_shapes=[pltpu.VMEM((8, 8, D_MODEL), F32), pltpu.SemaphoreType.DMA((7,)), pltpu.SemaphoreType.DMA((7,))],
    )(block)


def _adamw(w, g, m, v, name):
    rows, cols = w.shape
    tr = 128 if rows % 128 == 0 else rows

    def body(w_ref, g_ref, m_ref, v_ref, d_ref, nm_ref, nv_ref):
        gv = g_ref[...]
        nm = ADAM_B1 * m_ref[...] + (1.0 - ADAM_B1) * gv
        nv = ADAM_B2 * v_ref[...] + (1.0 - ADAM_B2) * (gv * gv)
        m_hat = nm / (1.0 - ADAM_B1 ** ADAM_STEP)
        v_hat = nv / (1.0 - ADAM_B2 ** ADAM_STEP)
        d_ref[...] = -ADAM_LR * (m_hat / (jnp.sqrt(v_hat) + ADAM_EPS) + ADAM_WD * w_ref[...])
        nm_ref[...] = nm
        nv_ref[...] = nv

    spec = pl.BlockSpec((tr, cols), lambda i: (i, 0))
    shape = jax.ShapeDtypeStruct((rows, cols), F32)
    return pl.pallas_call(
        body, name=name, grid=(rows // tr,), in_specs=[spec] * 4, out_specs=[spec] * 3, out_shape=[shape] * 3,
        compiler_params=_params(("parallel",)),
    )(w, g, m, v)


def _pack_w_in(w):
    return jnp.concatenate([w[:, :BA_END], jnp.zeros((D_MODEL, SEG_GA - BA_END), w.dtype), w[:, BA_END:]], axis=1)


def kernel(x, norm_w, w_in, conv_w, a_log, dt_bias, gdn_norm_w, w_up_a, w_up_b, w_out, final_norm_w, loss_target, m_norm_w, m_w_in, m_conv_w, m_a_log, m_dt_bias, m_gdn_norm_w, m_w_up_a, m_w_up_b, m_w_out, m_final_norm_w, v_norm_w, v_w_in, v_conv_w, v_a_log, v_dt_bias, v_gdn_norm_w, v_w_up_a, v_w_up_b, v_w_out, v_final_norm_w):
    conv_hi = conv_w[0].astype(MXU_DTYPE)
    conv_lo = (conv_w[0] - conv_hi.astype(F32)).astype(MXU_DTYPE)
    slab = _pack_slab(w_in[0].astype(MXU_DTYPE), w_up_a[0].astype(MXU_DTYPE), w_up_b[0].astype(MXU_DTYPE),
                      w_out[0].astype(MXU_DTYPE), conv_hi, conv_lo)
    slabs = _gather_weights(slab.reshape(2, HALF_ROWS, 128)).reshape(SHARDS, SLAB_ROWS, 128)
    parts = [_unpack_slab(slabs[s]) for s in range(SHARDS)]
    split = BA_END - (SHARDS - 1) * W_IN_SHARD
    wp = jnp.concatenate([p[0] for p in parts[:-1]] + [parts[-1][0][:, :split], jnp.zeros((D_MODEL, SEG_GA - BA_END), MXU_DTYPE),
                                                       parts[-1][0][:, split:]], axis=1)
    w_up_a_full = jnp.concatenate([p[1] for p in parts], axis=1)
    w_up_b_full = jnp.concatenate([p[2] for p in parts], axis=1)
    w_out_full = jnp.concatenate([p[3] for p in parts], axis=0)
    conv_full = jnp.concatenate([p[4].astype(F32) + p[5].astype(F32) for p in parts], axis=1)

    g = _local_step(x[0], loss_target[0], norm_w, wp, conv_full, a_log, dt_bias, gdn_norm_w,
                    w_up_a_full, w_up_b_full, w_out_full, final_norm_w[None])

    d_wp = g["w_in"]
    d_w_in = [d_wp[:, s * W_IN_SHARD:(s + 1) * W_IN_SHARD] for s in range(SHARDS - 1)]
    d_w_in.append(jnp.concatenate([d_wp[:, (SHARDS - 1) * W_IN_SHARD:BA_END], d_wp[:, SEG_GA:]], axis=1))
    zero_conv = jnp.zeros((GDN_CONV, CONV_SHARD), F32)
    grads = jnp.stack([
        _pack_slab(d_w_in[s], g["w_up_a"][:, s * 256:(s + 1) * 256],
                   g["w_up_b"][:, s * 256:(s + 1) * 256], g["w_out"][s * 256:(s + 1) * 256],
                   g["conv_w"][:, s * CONV_SHARD:(s + 1) * CONV_SHARD], zero_conv) for s in range(SHARDS)])
    grads = grads.reshape(SHARDS, 2, HALF_ROWS, 128)
    from_sibling = _exchange_halves(grads)
    pairs = _pair_sum(grads, from_sibling)
    from_chips = _scatter_pairs(pairs)
    total = _share_total(_chip_sum(pairs, from_chips)).reshape(SLAB_ROWS, 128)
    g_w_in, g_w_up_a, g_w_up_b, g_w_out, g_conv, _ = _unpack_slab(total)

    row2 = jnp.concatenate([g["gdn_norm_w"], g["a_log"], g["dt_bias"], g["loss"][0:1, 0:1],
                            jnp.zeros((1, D_MODEL - HEAD_DIM - 2 * HEADS - 1), F32)], axis=1)
    small = _allreduce_small(jnp.concatenate([g["norm_w"], g["final_norm_w"], row2, jnp.zeros((5, D_MODEL), F32)], axis=0))
    g_norm, g_final = small[0:1], small[1]
    g_gnw, g_alog, g_dt = small[2:3, 0:HEAD_DIM], small[2:3, HEAD_DIM:HEAD_DIM + HEADS], small[2:3, HEAD_DIM + HEADS:HEAD_DIM + 2 * HEADS]
    loss = small[2, HEAD_DIM + 2 * HEADS]

    names = ["norm_w", "w_in", "conv_w", "a_log", "dt_bias", "gdn_norm_w", "w_up_a", "w_up_b", "w_out", "final_norm_w"]
    weights = dict(zip(names, (norm_w, w_in, conv_w, a_log, dt_bias, gdn_norm_w, w_up_a, w_up_b, w_out, final_norm_w)))
    ms = dict(zip(names, (m_norm_w, m_w_in, m_conv_w, m_a_log, m_dt_bias, m_gdn_norm_w, m_w_up_a, m_w_up_b, m_w_out, m_final_norm_w)))
    vs = dict(zip(names, (v_norm_w, v_w_in, v_conv_w, v_a_log, v_dt_bias, v_gdn_norm_w, v_w_up_a, v_w_up_b, v_w_out, v_final_norm_w)))
    grads2d = dict(norm_w=g_norm, w_in=g_w_in, conv_w=g_conv, a_log=g_alog, dt_bias=g_dt, gdn_norm_w=g_gnw,
                   w_up_a=g_w_up_a, w_up_b=g_w_up_b, w_out=g_w_out, final_norm_w=g_final[None])
    grad_out, delta, new_m, new_v = [], [], [], []
    for n in names:
        shape = weights[n].shape
        two_d = grads2d[n].shape
        d, nm, nv = _adamw(weights[n].reshape(two_d), grads2d[n], ms[n].reshape(two_d), vs[n].reshape(two_d), f"adamw_{n}")
        grad_out.append(grads2d[n].reshape(shape))
        delta.append(d.reshape(shape))
        new_m.append(nm.reshape(shape))
        new_v.append(nv.reshape(shape))
    return (loss, g["grad_x"][None], *grad_out, *delta, *new_m, *new_v)
```

```python
import functools

import jax
import jax.numpy as jnp
from jax import lax
from jax.experimental import pallas as pl
from jax.experimental.pallas import tpu as pltpu

F32 = jnp.float32
MXU_DTYPE = jnp.bfloat16
HIGHEST = lax.Precision.HIGHEST

D_MODEL = 1024
HEADS = 8
HEAD_DIM = 64
WIDTH = HEADS * HEAD_DIM
NORM_EPS = 1e-6
ROPE_THETA = 10000.0
ATT_BLOCK = 128
DILATIONS = (1, 4, 16)
GDN_CHUNK = 64
GDN_CONV = 4
IN_WIDTH = 9232
SEG_A, SEG_ZA, SEG_B, SEG_ZB, SEG_BA, SEG_GA, SEG_GB, PACKED_WIDTH = 0, 4608, 5120, 6656, 7168, 7680, 8704, 9728
BA_END = 7184
VMEM_LIMIT = 56 * 1024 * 1024

ADAM_LR, ADAM_B1, ADAM_B2, ADAM_EPS, ADAM_WD, ADAM_STEP = 0.001, 0.9, 0.999, 1e-08, 0.01, 10

_NN = (((1,), (0,)), ((), ()))
_NT = (((1,), (1,)), ((), ()))
_TN = (((0,), (0,)), ((), ()))


def _params(sem):
    return pltpu.CompilerParams(dimension_semantics=sem, vmem_limit_bytes=VMEM_LIMIT)


def _mxu(a, b, dims):
    return lax.dot_general(a.astype(MXU_DTYPE), b.astype(MXU_DTYPE), dims, preferred_element_type=F32)


def _sigmoid(x):
    return 1.0 / (1.0 + jnp.exp(-x))


def _softplus(x):
    return jnp.maximum(x, 0.0) + jnp.log(1.0 + jnp.exp(-jnp.abs(x)))


def _iota(shape, axis):
    return lax.broadcasted_iota(jnp.int32, shape, axis)


def _matmul(a, b, mode, name, out_dtype=F32, tm=512, tn=512, tk=512):
    if mode == "nn":
        (m, k), (k2, n) = a.shape, b.shape
    elif mode == "nt":
        (m, k), (n, k2) = a.shape, b.shape
    else:
        (k, m), (k2, n) = a.shape, b.shape
    assert k == k2
    tm, tn, tk = min(tm, m), min(tn, n), min(tk, k)
    assert m % tm == 0 and n % tn == 0 and k % tk == 0
    nk = k // tk
    dims = {"nn": _NN, "nt": _NT, "tn": _TN}[mode]

    def body(a_ref, b_ref, o_ref, acc_ref):
        kk = pl.program_id(2)

        @pl.when(kk == 0)
        def _():
            acc_ref[...] = jnp.zeros_like(acc_ref)

        acc_ref[...] += _mxu(a_ref[...], b_ref[...], dims)

        @pl.when(kk == nk - 1)
        def _():
            o_ref[...] = acc_ref[...].astype(o_ref.dtype)

    a_spec = pl.BlockSpec((tk, tm), lambda i, j, kk: (kk, i)) if mode == "tn" else pl.BlockSpec((tm, tk), lambda i, j, kk: (i, kk))
    b_spec = pl.BlockSpec((tn, tk), lambda i, j, kk: (j, kk)) if mode == "nt" else pl.BlockSpec((tk, tn), lambda i, j, kk: (kk, j))
    return pl.pallas_call(
        body, name=name, grid=(m // tm, n // tn, nk), in_specs=[a_spec, b_spec],
        out_specs=pl.BlockSpec((tm, tn), lambda i, j, kk: (i, j)),
        out_shape=jax.ShapeDtypeStruct((m, n), out_dtype),
        scratch_shapes=[pltpu.VMEM((tm, tn), F32)],
        compiler_params=_params(("parallel", "parallel", "arbitrary")),
    )(a, b)


def _norm_proj(x, norm_w, wp, tm=1024, tn=512):
    t = x.shape[0]

    def body(x_ref, nw_ref, w_ref, proj_ref, h_ref):
        @pl.when(pl.program_id(1) == 0)
        def _():
            xf = x_ref[...]
            r = lax.rsqrt(jnp.mean(xf * xf, axis=-1, keepdims=True) + NORM_EPS)
            h_ref[...] = (xf * r * nw_ref[...]).astype(h_ref.dtype)

        proj_ref[...] = jnp.dot(h_ref[...], w_ref[...], preferred_element_type=F32)

    return pl.pallas_call(
        body, name="norm_proj", grid=(t // tm, PACKED_WIDTH // tn),
        in_specs=[pl.BlockSpec((tm, D_MODEL), lambda i, j: (i, 0)),
                  pl.BlockSpec((1, D_MODEL), lambda i, j: (0, 0)),
                  pl.BlockSpec((D_MODEL, tn), lambda i, j: (0, j))],
        out_specs=[pl.BlockSpec((tm, tn), lambda i, j: (i, j)),
                   pl.BlockSpec((tm, D_MODEL), lambda i, j: (i, 0))],
        out_shape=[jax.ShapeDtypeStruct((t, PACKED_WIDTH), F32), jax.ShapeDtypeStruct((t, D_MODEL), MXU_DTYPE)],
        compiler_params=_params(("parallel", "arbitrary")),
    )(x, norm_w, wp)


def _rope_tables(t):
    lane = jnp.arange(128)
    inv_freq = ROPE_THETA ** (-jnp.arange(0, HEAD_DIM, 2, dtype=F32) / HEAD_DIM)
    ang = jnp.arange(t, dtype=F32)[:, None] * inv_freq[None, :]
    ang = jnp.concatenate([ang, ang, ang, ang], axis=-1)
    first_half = (lane % HEAD_DIM) < HEAD_DIM // 2
    cos, sin = jnp.cos(ang), jnp.sin(ang)
    return cos, jnp.where(first_half, -sin, 0.0), jnp.where(first_half, 0.0, sin)


def _rope_block(x, cos, sin_lo, sin_hi, sign):
    outs = []
    for c in range(8):
        xc = x[:, c * 128:(c + 1) * 128]
        rot = pltpu.roll(xc, 96, 1) * sin_lo + pltpu.roll(xc, 32, 1) * sin_hi
        outs.append(xc * cos + sign * rot)
    outs.append(x[:, 2 * WIDTH:])
    return jnp.concatenate(outs, axis=1)


def _tile_scratch(tm, cols):
    return pltpu.VMEM((cols // 128, tm, 128), F32)


def _store_tile(scr, y):
    for c in range(scr.shape[0]):
        scr[c] = y[:, c * 128:(c + 1) * 128]


def _load_tile(scr):
    return jnp.concatenate([scr[c] for c in range(scr.shape[0])], axis=1)


def _to_strided_view(scr, o_ref, d):
    n, tm, _ = scr.shape
    for r in range(d):
        for c in range(n):
            o_ref[:, (r * n + c) * 128:(r * n + c + 1) * 128] = scr[c, pl.ds(r, tm // d, stride=d), :].astype(o_ref.dtype)


def _from_strided_view(i_ref, scr, d):
    n, tm, _ = scr.shape
    for r in range(d):
        for c in range(n):
            scr[c, pl.ds(r, tm // d, stride=d), :] = i_ref[:, (r * n + c) * 128:(r * n + c + 1) * 128].astype(F32)


def _strided_spec(tm, d, cols):
    return pl.BlockSpec((tm // d, d * cols), lambda i: (i, 0))


def _rope_fwd(proj, tables, tm=256):
    t = proj.shape[0]
    cols = 3 * WIDTH

    def body(x_ref, c_ref, sl_ref, sh_ref, o0, o1, o2, scr):
        for g, (d, o_ref) in enumerate(zip(DILATIONS, (o0, o1, o2))):
            y = _rope_block(x_ref[:, g * cols:(g + 1) * cols], c_ref[...], sl_ref[...], sh_ref[...], 1.0)
            if d == 1:
                o_ref[...] = y.astype(o_ref.dtype)
            else:
                _store_tile(scr, y)
                _to_strided_view(scr, o_ref, d)

    tab = pl.BlockSpec((tm, 128), lambda i: (i, 0))
    return pl.pallas_call(
        body, name="rope_fwd", grid=(t // tm,),
        in_specs=[pl.BlockSpec((tm, 3 * cols), lambda i: (i, 0)), tab, tab, tab],
        out_specs=[_strided_spec(tm, d, cols) for d in DILATIONS],
        out_shape=[jax.ShapeDtypeStruct((t // d, d * cols), MXU_DTYPE) for d in DILATIONS],
        scratch_shapes=[_tile_scratch(tm, cols)],
        compiler_params=_params(("parallel",)),
    )(proj, *tables)


def _rope_bwd(dproj, dqkvs, tables, tm=256):
    t = dproj.shape[0]
    cols = 3 * WIDTH

    def body(dp_ref, i0, i1, i2, c_ref, sl_ref, sh_ref, o_ref, scr):
        for g, (d, i_ref) in enumerate(zip(DILATIONS, (i0, i1, i2))):
            if d == 1:
                x = i_ref[...]
            else:
                _from_strided_view(i_ref, scr, d)
                x = _load_tile(scr)
            y = _rope_block(x, c_ref[...], sl_ref[...], sh_ref[...], -1.0)
            o_ref[:, g * cols:(g + 1) * cols] = y.astype(o_ref.dtype)

    tab = pl.BlockSpec((tm, 128), lambda i: (i, 0))
    return pl.pallas_call(
        body, name="rope_bwd", grid=(t // tm,),
        in_specs=[pl.BlockSpec(memory_space=pl.ANY)] + [_strided_spec(tm, d, cols) for d in DILATIONS] + [tab, tab, tab],
        out_specs=pl.BlockSpec((tm, 3 * cols), lambda i: (i, 0)),
        out_shape=jax.ShapeDtypeStruct((t, PACKED_WIDTH), MXU_DTYPE),
        scratch_shapes=[_tile_scratch(tm, cols)],
        input_output_aliases={0: 0},
        compiler_params=_params(("parallel",)),
    )(dproj, *dqkvs, *tables)


def _att_masks():
    qi = _iota((ATT_BLOCK, ATT_BLOCK), 0)
    kj = _iota((ATT_BLOCK, ATT_BLOCK), 1)
    return kj <= qi, kj >= qi


def _att_fwd(qkv, d, name):
    rows = qkv.shape[0]
    nb = rows // ATT_BLOCK
    scale = HEAD_DIM ** -0.5

    def body(q_ref, kc_ref, kp_ref, vc_ref, vp_ref, o_ref, lse_ref):
        has_prev = pl.program_id(1) > 0
        m_cur, m_prev = _att_masks()
        m_prev = m_prev & has_prev
        hs = range(HEADS)
        sls = [slice(h * HEAD_DIM, (h + 1) * HEAD_DIM) for h in hs]
        qs = [q_ref[:, sl] for sl in sls]
        s_c = [jnp.where(m_cur, _mxu(qs[h], kc_ref[:, sls[h]], _NT) * scale, -jnp.inf) for h in hs]
        s_p = [jnp.where(m_prev, _mxu(qs[h], kp_ref[:, sls[h]], _NT) * scale, -jnp.inf) for h in hs]
        m = [jnp.maximum(jnp.max(s_c[h], axis=1, keepdims=True), jnp.max(s_p[h], axis=1, keepdims=True)) for h in hs]
        p_c = [jnp.exp(s_c[h] - m[h]) for h in hs]
        p_p = [jnp.exp(s_p[h] - m[h]) for h in hs]
        den = [jnp.sum(p_c[h], axis=1, keepdims=True) + jnp.sum(p_p[h], axis=1, keepdims=True) for h in hs]
        o = [_mxu(p_c[h], vc_ref[:, sls[h]], _NN) + _mxu(p_p[h], vp_ref[:, sls[h]], _NN) for h in hs]
        for h in hs:
            o_ref[:, sls[h]] = o[h] / den[h]
            lse_ref[:, sls[h]] = jnp.broadcast_to(m[h] + jnp.log(den[h]), (ATT_BLOCK, HEAD_DIM))

    def cur(c):
        return pl.BlockSpec((ATT_BLOCK, WIDTH), lambda r, i: (i, 3 * r + c))

    def prev(c):
        return pl.BlockSpec((ATT_BLOCK, WIDTH), lambda r, i: (jnp.maximum(i - 1, 0), 3 * r + c))

    out = pl.BlockSpec((ATT_BLOCK, WIDTH), lambda r, i: (i, r))
    return pl.pallas_call(
        body, name=name, grid=(d, nb), in_specs=[cur(0), cur(1), prev(1), cur(2), prev(2)],
        out_specs=[out, out],
        out_shape=[jax.ShapeDtypeStruct((rows, d * WIDTH), F32)] * 2,
        compiler_params=_params(("parallel", "arbitrary")),
    )(qkv, qkv, qkv, qkv, qkv)


def _att_bwd(qkv, do, lse, delta, d, name):
    rows = qkv.shape[0]
    nb = rows // ATT_BLOCK
    scale = HEAD_DIM ** -0.5

    def body(q_ref, qn_ref, kc_ref, kp_ref, vc_ref, vp_ref, do_ref, don_ref, l_ref, ln_ref, dl_ref, dln_ref, o_ref):
        i = pl.program_id(1)
        m_cur, m_prev = _att_masks()
        m_p = m_prev & (i > 0)
        m_n = m_prev & (i < nb - 1)

        hs = range(HEADS)
        sls = [slice(h * HEAD_DIM, (h + 1) * HEAD_DIM) for h in hs]
        col = [slice(h * HEAD_DIM, h * HEAD_DIM + 1) for h in hs]

        def probs(q_r, k_r, lse_r, mask):
            s = [_mxu(q_r[:, sls[h]], k_r[:, sls[h]], _NT) for h in hs]
            return [jnp.where(mask, jnp.exp(s[h] * scale - lse_r[:, col[h]]), 0.0) for h in hs]

        def dscores(p, do_r, v_r, dl_r):
            dp = [_mxu(do_r[:, sls[h]], v_r[:, sls[h]], _NT) for h in hs]
            return [(p[h] * (dp[h] - dl_r[:, col[h]])).astype(MXU_DTYPE) for h in hs]

        p = probs(q_ref, kc_ref, l_ref, m_cur)
        ds = dscores(p, do_ref, vc_ref, dl_ref)
        dq = [_mxu(ds[h], kc_ref[:, sls[h]], _NN) for h in hs]
        dk = [_mxu(ds[h], q_ref[:, sls[h]], _TN) for h in hs]
        dv = [_mxu(p[h], do_ref[:, sls[h]], _TN) for h in hs]
        p = probs(q_ref, kp_ref, l_ref, m_p)
        ds = dscores(p, do_ref, vp_ref, dl_ref)
        dq = [dq[h] + _mxu(ds[h], kp_ref[:, sls[h]], _NN) for h in hs]
        p = probs(qn_ref, kc_ref, ln_ref, m_n)
        ds = dscores(p, don_ref, vc_ref, dln_ref)
        dk = [dk[h] + _mxu(ds[h], qn_ref[:, sls[h]], _TN) for h in hs]
        dv = [dv[h] + _mxu(p[h], don_ref[:, sls[h]], _TN) for h in hs]
        for h in hs:
            o_ref[:, sls[h]] = dq[h] * scale
            o_ref[:, WIDTH + h * HEAD_DIM:WIDTH + (h + 1) * HEAD_DIM] = dk[h] * scale
            o_ref[:, 2 * WIDTH + h * HEAD_DIM:2 * WIDTH + (h + 1) * HEAD_DIM] = dv[h]

    def qkv_spec(c, shift):
        def idx(r, i):
            return (jnp.clip(i + shift, 0, nb - 1), 3 * r + c)
        return pl.BlockSpec((ATT_BLOCK, WIDTH), idx)

    def tok_spec(shift):
        def idx(r, i):
            return (jnp.clip(i + shift, 0, nb - 1), r)
        return pl.BlockSpec((ATT_BLOCK, WIDTH), idx)

    return pl.pallas_call(
        body, name=name, grid=(d, nb),
        in_specs=[qkv_spec(0, 0), qkv_spec(0, 1), qkv_spec(1, 0), qkv_spec(1, -1), qkv_spec(2, 0), qkv_spec(2, -1),
                  tok_spec(0), tok_spec(1), tok_spec(0), tok_spec(1), tok_spec(0), tok_spec(1)],
        out_specs=pl.BlockSpec((ATT_BLOCK, 3 * WIDTH), lambda r, i: (i, r)),
        out_shape=jax.ShapeDtypeStruct((rows, d * 3 * WIDTH), F32),
        compiler_params=_params(("parallel", "arbitrary")),
    )(qkv, qkv, qkv, qkv, qkv, qkv, do, do, lse, lse, delta, delta)


def _att_merge(os_, lses, proj, tm=256):
    t = proj.shape[0]

    def body(o0, o1, o2, l0, l1, l2, z_ref, oz_ref, o_ref, t0, t1, t2, s_o1, s_l1, s_o2, s_l2, s_t):
        _from_strided_view(o1, s_o1, DILATIONS[1])
        _from_strided_view(l1, s_l1, DILATIONS[1])
        _from_strided_view(o2, s_o2, DILATIONS[2])
        _from_strided_view(l2, s_l2, DILATIONS[2])
        a, b, c = l0[...], _load_tile(s_l1), _load_tile(s_l2)
        m = jnp.maximum(jnp.maximum(a, b), c)
        wa, wb, wc = jnp.exp(a - m), jnp.exp(b - m), jnp.exp(c - m)
        den = wa + wb + wc
        o = (wa * o0[...] + wb * _load_tile(s_o1) + wc * _load_tile(s_o2)) / den
        z = z_ref[...]
        o_ref[...] = o
        oz_ref[...] = (o * z * _sigmoid(z)).astype(oz_ref.dtype)
        total = m + jnp.log(den)
        t0[...] = total
        _store_tile(s_t, total)
        _to_strided_view(s_t, t1, DILATIONS[1])
        _to_strided_view(s_t, t2, DILATIONS[2])

    tok = pl.BlockSpec((tm, WIDTH), lambda i: (i, 0))
    views = [_strided_spec(tm, d, WIDTH) for d in DILATIONS]
    view_shapes = [jax.ShapeDtypeStruct((t // d, d * WIDTH), F32) for d in DILATIONS]
    return pl.pallas_call(
        body, name="att_merge", grid=(t // tm,),
        in_specs=views + views + [pl.BlockSpec((tm, WIDTH), lambda i: (i, SEG_ZA // WIDTH))],
        out_specs=[tok, tok] + views,
        out_shape=[jax.ShapeDtypeStruct((t, WIDTH), MXU_DTYPE), jax.ShapeDtypeStruct((t, WIDTH), F32)] + view_shapes,
        scratch_shapes=[_tile_scratch(tm, WIDTH)] * 5,
        compiler_params=_params(("parallel",)),
    )(*os_, *lses, proj)


def _att_merge_bwd(dproj, d_oz, o, proj, tm=256):
    t = proj.shape[0]

    def body(dp_ref, doz_ref, o_ref, z_ref, dz_ref, do0, do1, do2, dl0, dl1, dl2, s_do, s_dl):
        z, ov, g = z_ref[...], o_ref[...], doz_ref[...]
        sg = _sigmoid(z)
        do = g * z * sg
        dz_ref[...] = (g * ov * sg * (1.0 + z * (1.0 - sg))).astype(dz_ref.dtype)
        do0[...] = do.astype(do0.dtype)
        _store_tile(s_do, do)
        prod = do * ov
        for h in range(HEADS):
            sl = slice(h * HEAD_DIM, (h + 1) * HEAD_DIM)
            half = slice((h % 2) * HEAD_DIM, (h % 2 + 1) * HEAD_DIM)
            s_dl[h // 2, :, half] = jnp.broadcast_to(jnp.sum(prod[:, sl], axis=1, keepdims=True), (tm, HEAD_DIM))
        dl0[...] = _load_tile(s_dl)
        for d, do_v, dl_v in ((DILATIONS[1], do1, dl1), (DILATIONS[2], do2, dl2)):
            _to_strided_view(s_do, do_v, d)
            _to_strided_view(s_dl, dl_v, d)

    tok = pl.BlockSpec((tm, WIDTH), lambda i: (i, 0))
    seg = pl.BlockSpec((tm, WIDTH), lambda i: (i, SEG_ZA // WIDTH))
    views = [_strided_spec(tm, d, WIDTH) for d in DILATIONS]
    return pl.pallas_call(
        body, name="att_merge_bwd", grid=(t // tm,),
        in_specs=[pl.BlockSpec(memory_space=pl.ANY), tok, tok, seg],
        out_specs=[seg] + views + views,
        out_shape=[jax.ShapeDtypeStruct((t, PACKED_WIDTH), MXU_DTYPE)]
        + [jax.ShapeDtypeStruct((t // d, d * WIDTH), MXU_DTYPE) for d in DILATIONS]
        + [jax.ShapeDtypeStruct((t // d, d * WIDTH), F32) for d in DILATIONS],
        scratch_shapes=[_tile_scratch(tm, WIDTH)] * 2,
        input_output_aliases={0: 0},
        compiler_params=_params(("parallel",)),
    )(dproj, d_oz, o, proj)


def _shift_down(x, halo, s):
    if s == 0:
        return x
    xs = pltpu.roll(x, s, 0)
    head = jnp.where(_iota((8, x.shape[1]), 0) < s, pltpu.roll(halo, s, 0), xs[0:8])
    return jnp.concatenate([head, xs[8:]], axis=0)


def _shift_up(x, nxt, s):
    if s == 0:
        return x
    n = x.shape[0]
    xs = pltpu.roll(x, n - s, 0)
    tail = jnp.where(_iota((8, x.shape[1]), 0) >= 8 - s, pltpu.roll(nxt, 8 - s, 0), xs[n - 8:])
    return jnp.concatenate([xs[:n - 8], tail], axis=0)


def _conv_fwd(proj, conv_w, tm=256):
    t = proj.shape[0]
    cb = SEG_B // WIDTH

    def body(x_ref, halo_ref, w_ref, c_ref):
        halo = jnp.where(pl.program_id(0) > 0, halo_ref[...], 0.0)
        x = x_ref[...]
        w = w_ref[...]
        acc = jnp.zeros((tm, WIDTH), F32)
        for j in range(GDN_CONV):
            acc += _shift_down(x, halo, GDN_CONV - 1 - j) * w[j:j + 1, :]
        c_ref[...] = acc

    return pl.pallas_call(
        body, name="conv_fwd", grid=(t // tm, 3),
        in_specs=[pl.BlockSpec((tm, WIDTH), lambda i, c: (i, cb + c)),
                  pl.BlockSpec((8, WIDTH), lambda i, c: (jnp.maximum(i * (tm // 8) - 1, 0), cb + c)),
                  pl.BlockSpec((GDN_CONV, WIDTH), lambda i, c: (0, c))],
        out_specs=pl.BlockSpec((tm, WIDTH), lambda i, c: (i, c)),
        out_shape=jax.ShapeDtypeStruct((t, 3 * WIDTH), F32),
        compiler_params=_params(("parallel", "parallel")),
    )(proj, proj, conv_w)


def _conv_bwd(dproj, dc, proj, conv_w, tm=256):
    t = proj.shape[0]
    cb = SEG_B // WIDTH
    nt = t // tm

    def body(dp_ref, dc_ref, dcn_ref, x_ref, halo_ref, w_ref, dx_ref, dw_ref):
        i = pl.program_id(1)
        w = w_ref[...]
        dcn = jnp.where(i < nt - 1, dcn_ref[...], 0.0)
        dcv = dc_ref[...]
        acc = jnp.zeros((tm, WIDTH), F32)
        for j in range(GDN_CONV):
            acc += _shift_up(dcv, dcn, GDN_CONV - 1 - j) * w[j:j + 1, :]
        dx_ref[...] = acc.astype(dx_ref.dtype)
        halo = jnp.where(i > 0, halo_ref[...], 0.0)
        x = x_ref[...]
        row8 = _iota((8, WIDTH), 0)
        part = jnp.zeros((8, WIDTH), F32)
        for j in range(GDN_CONV):
            s = jnp.sum(dcv * _shift_down(x, halo, GDN_CONV - 1 - j), axis=0, keepdims=True)
            part += jnp.where(row8 == j, s, 0.0)

        @pl.when(i == 0)
        def _():
            dw_ref[...] = jnp.zeros_like(dw_ref)

        dw_ref[...] += part

    return pl.pallas_call(
        body, name="conv_bwd", grid=(3, nt),
        in_specs=[pl.BlockSpec(memory_space=pl.ANY),
                  pl.BlockSpec((tm, WIDTH), lambda c, i: (i, c)),
                  pl.BlockSpec((8, WIDTH), lambda c, i: (jnp.minimum((i + 1) * (tm // 8), t // 8 - 1), c)),
                  pl.BlockSpec((tm, WIDTH), lambda c, i: (i, cb + c)),
                  pl.BlockSpec((8, WIDTH), lambda c, i: (jnp.maximum(i * (tm // 8) - 1, 0), cb + c)),
                  pl.BlockSpec((GDN_CONV, WIDTH), lambda c, i: (0, c))],
        out_specs=[pl.BlockSpec((tm, WIDTH), lambda c, i: (i, cb + c)),
                   pl.BlockSpec((8, WIDTH), lambda c, i: (0, c))],
        out_shape=[jax.ShapeDtypeStruct((t, PACKED_WIDTH), MXU_DTYPE), jax.ShapeDtypeStruct((8, 3 * WIDTH), F32)],
        input_output_aliases={0: 0},
        compiler_params=_params(("parallel", "arbitrary")),
    )(dproj, dc, dc, proj, proj, conv_w)


def _chunk_matrices(tm):
    r, c = _iota((tm, tm), 0), _iota((tm, tm), 1)
    same = (r // GDN_CHUNK) == (c // GDN_CHUNK)
    return jnp.where(same & (c <= r), 1.0, 0.0), jnp.where(same, 1.0, 0.0)


def _gdn_gates(ba, a_log, dt_bias):
    al = ba + dt_bias
    return _sigmoid(ba), -jnp.exp(a_log) * _softplus(al), _sigmoid(al)


def _head_lane_eye():
    return jnp.where(_iota((HEADS, 128), 1) == _iota((HEADS, 128), 0) + HEADS, 1.0, 0.0)


def _gdn_prep(conv, proj, a_log, dt_bias, tm=256):
    t = proj.shape[0]
    nc = tm // GDN_CHUNK

    def body(c_ref, ba_ref, al_ref, dt_ref, q_ref, k_ref, v_ref, b_ref, g_ref, gl_ref, grow_ref):
        c = c_ref[...]
        a = c * _sigmoid(c)
        beta, g, _ = _gdn_gates(ba_ref[:, 0:128], al_ref[...], dt_ref[...])
        lmat, cmat = _chunk_matrices(tm)
        gc = jnp.dot(lmat, g, precision=HIGHEST, preferred_element_type=F32)
        gl = jnp.dot(cmat, g, precision=HIGHEST, preferred_element_type=F32)
        grow = lax.dot_general(_head_lane_eye(), gc, _NT, precision=HIGHEST, preferred_element_type=F32)
        for h in range(HEADS):
            sl = slice(h * HEAD_DIM, (h + 1) * HEAD_DIM)
            qh, kh, vh = a[:, sl], a[:, WIDTH + h * HEAD_DIM:WIDTH + (h + 1) * HEAD_DIM], a[:, 2 * WIDTH + h * HEAD_DIM:2 * WIDTH + (h + 1) * HEAD_DIM]
            rq = lax.rsqrt(jnp.sum(qh * qh, axis=1, keepdims=True) + NORM_EPS)
            rk = lax.rsqrt(jnp.sum(kh * kh, axis=1, keepdims=True) + NORM_EPS)
            q_ref[h] = qh * (rq * HEAD_DIM ** -0.5)
            k_ref[h] = kh * rk
            v_ref[h] = vh
            b_ref[h] = jnp.broadcast_to(beta[:, h:h + 1], (tm, HEAD_DIM))
            g_ref[h] = jnp.broadcast_to(gc[:, HEADS + h:HEADS + h + 1], (tm, HEAD_DIM))
            gl_ref[h] = jnp.broadcast_to(gl[:, HEADS + h:HEADS + h + 1], (tm, HEAD_DIM))
            for cc in range(nc):
                grow_ref[h, cc] = grow[h:h + 1, cc * GDN_CHUNK:(cc + 1) * GDN_CHUNK]

    hm = pl.BlockSpec((HEADS, tm, HEAD_DIM), lambda i: (0, i, 0))
    small = pl.BlockSpec((1, 128), lambda i: (0, 0))
    hm_shape = jax.ShapeDtypeStruct((HEADS, t, HEAD_DIM), F32)
    return pl.pallas_call(
        body, name="gdn_prep", grid=(t // tm,),
        in_specs=[pl.BlockSpec((tm, 3 * WIDTH), lambda i: (i, 0)),
                  pl.BlockSpec((tm, WIDTH), lambda i: (i, SEG_BA // WIDTH)), small, small],
        out_specs=[hm] * 6 + [pl.BlockSpec((HEADS, nc, 1, GDN_CHUNK), lambda i: (0, i, 0, 0))],
        out_shape=[hm_shape] * 6 + [jax.ShapeDtypeStruct((HEADS, t // GDN_CHUNK, 1, GDN_CHUNK), F32)],
        compiler_params=_params(("parallel",)),
    )(conv, proj, a_log, dt_bias)


def _gdn_prep_bwd(dproj, conv, proj, a_log, dt_bias, dq, dk, dv, db, dg, dgl, dgrow, tm=256):
    t = proj.shape[0]
    nc = tm // GDN_CHUNK

    def body(dp_ref, c_ref, ba_ref, al_ref, dt_ref, dq_ref, dk_ref, dv_ref, db_ref, dg_ref, dgl_ref, dgrow_ref,
             dba_ref, dc_ref, small_ref, row_scr):
        c = c_ref[...]
        sg = _sigmoid(c)
        a = c * sg
        beta, g, sig_al = _gdn_gates(ba_ref[:, 0:128], al_ref[...], dt_ref[...])
        lane = _iota((1, 128), 1)
        d_beta = jnp.zeros((tm, 128), F32)
        d_gc = jnp.zeros((tm, 128), F32)
        d_gl = jnp.zeros((tm, 128), F32)
        for h in range(HEADS):
            for cc in range(nc):
                row_scr[h:h + 1, cc * GDN_CHUNK:(cc + 1) * GDN_CHUNK] = dgrow_ref[h, cc]
            d_beta += jnp.sum(db_ref[h], axis=1, keepdims=True) * jnp.where(lane == h, 1.0, 0.0)
            d_gc += jnp.sum(dg_ref[h], axis=1, keepdims=True) * jnp.where(lane == HEADS + h, 1.0, 0.0)
            d_gl += jnp.sum(dgl_ref[h], axis=1, keepdims=True) * jnp.where(lane == HEADS + h, 1.0, 0.0)
            qs, ks, vs = (slice(h * HEAD_DIM, (h + 1) * HEAD_DIM), slice(WIDTH + h * HEAD_DIM, WIDTH + (h + 1) * HEAD_DIM),
                          slice(2 * WIDTH + h * HEAD_DIM, 2 * WIDTH + (h + 1) * HEAD_DIM))
            qh, kh = a[:, qs], a[:, ks]
            rq = lax.rsqrt(jnp.sum(qh * qh, axis=1, keepdims=True) + NORM_EPS)
            rk = lax.rsqrt(jnp.sum(kh * kh, axis=1, keepdims=True) + NORM_EPS)
            gq, gk = dq_ref[h] * HEAD_DIM ** -0.5, dk_ref[h]
            da_q = rq * gq - qh * (rq * rq * rq) * jnp.sum(gq * qh, axis=1, keepdims=True)
            da_k = rk * gk - kh * (rk * rk * rk) * jnp.sum(gk * kh, axis=1, keepdims=True)
            dsilu = lambda s, x: s * (1.0 + x * (1.0 - s))
            dc_ref[:, qs] = da_q * dsilu(sg[:, qs], c[:, qs])
            dc_ref[:, ks] = da_k * dsilu(sg[:, ks], c[:, ks])
            dc_ref[:, vs] = dv_ref[h] * dsilu(sg[:, vs], c[:, vs])
        d_gc += lax.dot_general(row_scr[...], _head_lane_eye(), _TN, precision=HIGHEST, preferred_element_type=F32)
        lmat, cmat = _chunk_matrices(tm)
        d_g = (lax.dot_general(lmat, d_gc, _TN, precision=HIGHEST, preferred_element_type=F32)
               + lax.dot_general(cmat, d_gl, _TN, precision=HIGHEST, preferred_element_type=F32))
        d_al = d_g * (-jnp.exp(al_ref[...])) * sig_al
        d_bl = d_beta * beta * (1.0 - beta)
        dba_ref[...] = jnp.concatenate([d_bl + d_al, jnp.zeros((tm, WIDTH - 128), F32)], axis=1).astype(dba_ref.dtype)
        row8 = _iota((8, 128), 0)
        part = (jnp.where(row8 == 0, jnp.sum(d_g * g, axis=0, keepdims=True), 0.0)
                + jnp.where(row8 == 1, jnp.sum(d_al, axis=0, keepdims=True), 0.0))

        @pl.when(pl.program_id(0) == 0)
        def _():
            small_ref[...] = jnp.zeros_like(small_ref)

        small_ref[...] += part

    hm = pl.BlockSpec((HEADS, tm, HEAD_DIM), lambda i: (0, i, 0))
    small = pl.BlockSpec((1, 128), lambda i: (0, 0))
    seg = pl.BlockSpec((tm, WIDTH), lambda i: (i, SEG_BA // WIDTH))
    return pl.pallas_call(
        body, name="gdn_prep_bwd", grid=(t // tm,),
        in_specs=[pl.BlockSpec(memory_space=pl.ANY), pl.BlockSpec((tm, 3 * WIDTH), lambda i: (i, 0)), seg, small, small]
        + [hm] * 6 + [pl.BlockSpec((HEADS, nc, 1, GDN_CHUNK), lambda i: (0, i, 0, 0))],
        out_specs=[seg, pl.BlockSpec((tm, 3 * WIDTH), lambda i: (i, 0)), pl.BlockSpec((8, 128), lambda i: (0, 0))],
        out_shape=[jax.ShapeDtypeStruct((t, PACKED_WIDTH), MXU_DTYPE), jax.ShapeDtypeStruct((t, 3 * WIDTH), F32),
                   jax.ShapeDtypeStruct((8, 128), F32)],
        scratch_shapes=[pltpu.VMEM((HEADS, tm), F32)],
        input_output_aliases={0: 0},
        compiler_params=_params(("arbitrary",)),
    )(dproj, conv, proj, a_log, dt_bias, dq, dk, dv, db, dg, dgl, dgrow)


_BNN = (((2,), (1,)), ((0,), (0,)))
_BNT = (((2,), (2,)), ((0,), (0,)))
_BTN = (((1,), (1,)), ((0,), (0,)))


@jax.custom_vjp
def _MM_NN(a, b):
    return _mxu(a, b, _BNN)


@jax.custom_vjp
def _MM_NT(a, b):
    return _mxu(a, b, _BNT)


@jax.custom_vjp
def _MM_TN(a, b):
    return _mxu(a, b, _BTN)


_MM_NN.defvjp(lambda a, b: (_mxu(a, b, _BNN), (a, b)), lambda r, g: (_mxu(g, r[1], _BNT), _mxu(r[0], g, _BTN)))
_MM_NT.defvjp(lambda a, b: (_mxu(a, b, _BNT), (a, b)), lambda r, g: (_mxu(g, r[1], _BNN), _mxu(g, r[0], _BTN)))
_MM_TN.defvjp(lambda a, b: (_mxu(a, b, _BTN), (a, b)), lambda r, g: (_mxu(r[1], g, _BNT), _mxu(r[0], g, _BNN)))


def _split(a):
    hi = a.astype(MXU_DTYPE)
    return hi, (a - hi.astype(F32)).astype(MXU_DTYPE)


def _dot3(a, b, dims):
    (ah, al), (bh, bl) = a, b
    (ca,), (cb,) = dims[0]
    return lax.dot_general(jnp.concatenate([ah, ah, al], axis=ca), jnp.concatenate([bh, bl, bh], axis=cb), dims,
                           preferred_element_type=F32)


def _unit_lower_inverse(a):
    c = GDN_CHUNK
    eye = jnp.where(_iota((c, c), 0) == _iota((c, c), 1), 1.0, 0.0)
    x = eye - a
    p = a
    for _ in range(5):
        ps = _split(p)
        p = _dot3(ps, ps, _BNN)
        x = x + _dot3(_split(x), _split(p), _BNN)
    return x


@jax.custom_vjp
def _SAVED_INVERSE(a, t_inv):
    return t_inv


def _saved_inverse_bwd(t_inv, g):
    ts = _split(t_inv)
    return -_dot3(ts, _split(_dot3(_split(g), ts, _BNT)), _BTN), jnp.zeros_like(t_inv)


_SAVED_INVERSE.defvjp(lambda a, t_inv: (t_inv, t_inv), _saved_inverse_bwd)


def _gdn_chunk(q, k, v, beta, g1, g2, gl, state, t_inv=None):
    c = GDN_CHUNK
    if t_inv is None:
        _mm_nn, _mm_nt, _mm_tn = (functools.partial(_mxu, dims=dd) for dd in (_BNN, _BNT, _BTN))
    else:
        _mm_nn, _mm_nt, _mm_tn = _MM_NN, _MM_NT, _MM_TN
    row, col = _iota((c, c), 0), _iota((c, c), 1)
    incl, strict = row >= col, row > col
    decay = jnp.where(incl, jnp.exp(jnp.where(incl, g1 - g2, 0.0)), 0.0)
    eg = jnp.exp(g1)
    kb = k * beta
    a = _mm_nt(kb, k) * jnp.where(strict, decay, 0.0)
    inv = _unit_lower_inverse(a) if t_inv is None else _SAVED_INVERSE(a, t_inv)
    u = _mm_nn(inv, v * beta)
    w = _mm_nn(inv, kb * eg)
    attn = _mm_nt(q, k) * decay
    v_new = u - _mm_nn(w, state)
    o = _mm_nn(q * eg, state) + _mm_nn(attn, v_new)
    new_state = state * jnp.exp(gl) + _mm_tn(k * jnp.exp(gl - g1), v_new)
    return (o, new_state, inv) if t_inv is None else (o, new_state)


def _gdn_fwd(q, k, v, beta, g, gl, grow, cpb=2):
    t = q.shape[1]
    rows = cpb * GDN_CHUNK

    def body(q_ref, k_ref, v_ref, b_ref, g_ref, gl_ref, grow_ref, o_ref, st_ref, inv_ref, state):
        @pl.when(pl.program_id(0) == 0)
        def _():
            state[...] = jnp.zeros_like(state)

        s = state[...]
        for cc in range(cpb):
            sl = slice(cc * GDN_CHUNK, (cc + 1) * GDN_CHUNK)
            st_ref[:, cc] = s
            g2 = jnp.broadcast_to(grow_ref[:, cc], (HEADS, GDN_CHUNK, GDN_CHUNK))
            o, s, inv = _gdn_chunk(q_ref[:, sl, :], k_ref[:, sl, :], v_ref[:, sl, :], b_ref[:, sl, :], g_ref[:, sl, :], g2,
                                   gl_ref[:, sl, :], s)
            o_ref[:, sl, :] = o
            inv_ref[:, cc] = inv
        state[...] = s

    hm = pl.BlockSpec((HEADS, rows, HEAD_DIM), lambda i: (0, i, 0))
    per_chunk = pl.BlockSpec((HEADS, cpb, GDN_CHUNK, HEAD_DIM), lambda i: (0, i, 0, 0))
    chunk_shape = jax.ShapeDtypeStruct((HEADS, t // GDN_CHUNK, GDN_CHUNK, HEAD_DIM), F32)
    return pl.pallas_call(
        body, name="gdn_fwd", grid=(t // rows,),
        in_specs=[hm] * 6 + [pl.BlockSpec((HEADS, cpb, 1, GDN_CHUNK), lambda i: (0, i, 0, 0))],
        out_specs=[hm, per_chunk, per_chunk],
        out_shape=[jax.ShapeDtypeStruct((HEADS, t, HEAD_DIM), F32), chunk_shape, chunk_shape],
        scratch_shapes=[pltpu.VMEM((HEADS, GDN_CHUNK, HEAD_DIM), F32)],
        compiler_params=_params(("arbitrary",)),
    )(q, k, v, beta, g, gl, grow)


def _gdn_bwd(q, k, v, beta, g, gl, grow, states, invs, do, cpb=1):
    t = q.shape[1]
    rows = cpb * GDN_CHUNK
    nsteps = t // rows

    def body(q_ref, k_ref, v_ref, b_ref, g_ref, gl_ref, grow_ref, st_ref, inv_ref, do_ref,
             dq_ref, dk_ref, dv_ref, db_ref, dg_ref, dgl_ref, dgrow_ref, dstate):
        @pl.when(pl.program_id(0) == 0)
        def _():
            dstate[...] = jnp.zeros_like(dstate)

        ds = dstate[...]
        for cc in reversed(range(cpb)):
            sl = slice(cc * GDN_CHUNK, (cc + 1) * GDN_CHUNK)
            g2 = jnp.broadcast_to(grow_ref[:, cc], (HEADS, GDN_CHUNK, GDN_CHUNK))
            _, vjp = jax.vjp(_gdn_chunk, q_ref[:, sl, :], k_ref[:, sl, :], v_ref[:, sl, :], b_ref[:, sl, :],
                             g_ref[:, sl, :], g2, gl_ref[:, sl, :], st_ref[:, cc], inv_ref[:, cc])
            gq, gk, gv, gb, gg1, gg2, ggl, ds, _ = vjp((do_ref[:, sl, :], ds))
            dq_ref[:, sl, :] = gq
            dk_ref[:, sl, :] = gk
            dv_ref[:, sl, :] = gv
            db_ref[:, sl, :] = gb
            dg_ref[:, sl, :] = gg1
            dgl_ref[:, sl, :] = ggl
            dgrow_ref[:, cc] = jnp.sum(gg2, axis=1, keepdims=True)
        dstate[...] = ds

    hm = pl.BlockSpec((HEADS, rows, HEAD_DIM), lambda i: (0, nsteps - 1 - i, 0))
    rowspec = pl.BlockSpec((HEADS, cpb, 1, GDN_CHUNK), lambda i: (0, nsteps - 1 - i, 0, 0))
    per_chunk = pl.BlockSpec((HEADS, cpb, GDN_CHUNK, HEAD_DIM), lambda i: (0, nsteps - 1 - i, 0, 0))
    hm_shape = jax.ShapeDtypeStruct((HEADS, t, HEAD_DIM), F32)
    return pl.pallas_call(
        body, name="gdn_bwd", grid=(nsteps,),
        in_specs=[hm] * 6 + [rowspec, per_chunk, per_chunk, hm],
        out_specs=[hm] * 6 + [rowspec],
        out_shape=[hm_shape] * 6 + [jax.ShapeDtypeStruct((HEADS, t // GDN_CHUNK, 1, GDN_CHUNK), F32)],
        scratch_shapes=[pltpu.VMEM((HEADS, GDN_CHUNK, HEAD_DIM), F32)],
        compiler_params=_params(("arbitrary",)),
    )(q, k, v, beta, g, gl, grow, states, invs, do)


def _gdn_out(o_hm, gdn_norm_w, proj, tm=256):
    t = proj.shape[0]

    def body(o_ref, w_ref, z_ref, oz_ref):
        z = z_ref[...]
        gate = z * _sigmoid(z)
        w = w_ref[...]
        for h in range(HEADS):
            sl = slice(h * HEAD_DIM, (h + 1) * HEAD_DIM)
            o = o_ref[h]
            r = lax.rsqrt(jnp.mean(o * o, axis=1, keepdims=True) + NORM_EPS)
            oz_ref[:, sl] = (o * r * w * gate[:, sl]).astype(oz_ref.dtype)

    tok = pl.BlockSpec((tm, WIDTH), lambda i: (i, 0))
    return pl.pallas_call(
        body, name="gdn_out", grid=(t // tm,),
        in_specs=[pl.BlockSpec((HEADS, tm, HEAD_DIM), lambda i: (0, i, 0)), pl.BlockSpec((1, HEAD_DIM), lambda i: (0, 0)),
                  pl.BlockSpec((tm, WIDTH), lambda i: (i, SEG_ZB // WIDTH))],
        out_specs=tok, out_shape=jax.ShapeDtypeStruct((t, WIDTH), MXU_DTYPE),
        compiler_params=_params(("parallel",)),
    )(o_hm, gdn_norm_w, proj)


def _gdn_out_bwd(dproj, d_oz, o_hm, gdn_norm_w, proj, tm=256):
    t = proj.shape[0]

    def body(dp_ref, doz_ref, o_ref, w_ref, z_ref, dz_ref, do_ref, dw_ref):
        z = z_ref[...]
        sg = _sigmoid(z)
        gate = z * sg
        dgate = sg * (1.0 + z * (1.0 - sg))
        w = w_ref[...]
        dw = jnp.zeros((1, HEAD_DIM), F32)
        for h in range(HEADS):
            sl = slice(h * HEAD_DIM, (h + 1) * HEAD_DIM)
            o = o_ref[h]
            g = doz_ref[:, sl]
            r = lax.rsqrt(jnp.mean(o * o, axis=1, keepdims=True) + NORM_EPS)
            on = o * r * w
            dz_ref[:, sl] = (g * on * dgate[:, sl]).astype(dz_ref.dtype)
            dn = g * gate[:, sl]
            dw += jnp.sum(dn * o * r, axis=0, keepdims=True)
            dnw = dn * w
            do_ref[h] = r * dnw - o * (r * r * r) * jnp.mean(dnw * o, axis=1, keepdims=True)

        @pl.when(pl.program_id(0) == 0)
        def _():
            dw_ref[...] = jnp.zeros_like(dw_ref)

        dw_ref[...] += jnp.concatenate([dw, jnp.zeros((7, HEAD_DIM), F32)], axis=0)

    tok = pl.BlockSpec((tm, WIDTH), lambda i: (i, 0))
    seg = pl.BlockSpec((tm, WIDTH), lambda i: (i, SEG_ZB // WIDTH))
    hm = pl.BlockSpec((HEADS, tm, HEAD_DIM), lambda i: (0, i, 0))
    return pl.pallas_call(
        body, name="gdn_out_bwd", grid=(t // tm,),
        in_specs=[pl.BlockSpec(memory_space=pl.ANY), tok, hm, pl.BlockSpec((1, HEAD_DIM), lambda i: (0, 0)), seg],
        out_specs=[seg, hm, pl.BlockSpec((8, HEAD_DIM), lambda i: (0, 0))],
        out_shape=[jax.ShapeDtypeStruct((t, PACKED_WIDTH), MXU_DTYPE), jax.ShapeDtypeStruct((HEADS, t, HEAD_DIM), F32),
                   jax.ShapeDtypeStruct((8, HEAD_DIM), F32)],
        input_output_aliases={0: 0},
        compiler_params=_params(("arbitrary",)),
    )(dproj, d_oz, o_hm, gdn_norm_w, proj)


def _merge(y_a, y_b, proj, tm=256):
    t = proj.shape[0]

    def body(ya_ref, yb_ref, ga_ref, gb_ref, m_ref):
        m_ref[...] = (_sigmoid(ga_ref[...]) * ya_ref[...] + _sigmoid(gb_ref[...]) * yb_ref[...]).astype(m_ref.dtype)

    half = pl.BlockSpec((tm, WIDTH), lambda i, c: (i, c))
    return pl.pallas_call(
        body, name="merge", grid=(t // tm, 2),
        in_specs=[half, half, pl.BlockSpec((tm, WIDTH), lambda i, c: (i, SEG_GA // WIDTH + c)),
                  pl.BlockSpec((tm, WIDTH), lambda i, c: (i, SEG_GB // WIDTH + c))],
        out_specs=half, out_shape=jax.ShapeDtypeStruct((t, D_MODEL), MXU_DTYPE),
        compiler_params=_params(("parallel", "parallel")),
    )(y_a, y_b, proj, proj)


def _merge_bwd(dproj, d_m, y, proj, seg, name, tm=256):
    t = proj.shape[0]

    def body(*refs):
        dm_ref, y_ref, g_ref, dg_ref, dy_ref = refs[-5:]
        dm = dm_ref[...]
        s = _sigmoid(g_ref[...])
        dy_ref[...] = (dm * s).astype(dy_ref.dtype)
        dg_ref[...] = (dm * y_ref[...] * s * (1.0 - s)).astype(dg_ref.dtype)

    half = pl.BlockSpec((tm, WIDTH), lambda i, c: (i, c))
    gate = pl.BlockSpec((tm, WIDTH), lambda i, c: (i, seg // WIDTH + c))
    specs, args, aliases = [half, half, gate], [d_m, y, proj], {}
    if dproj is not None:
        specs, args, aliases = [pl.BlockSpec(memory_space=pl.ANY)] + specs, [dproj] + args, {0: 0}
    return pl.pallas_call(
        body, name=name, grid=(t // tm, 2), in_specs=specs, out_specs=[gate, half],
        out_shape=[jax.ShapeDtypeStruct((t, PACKED_WIDTH), MXU_DTYPE), jax.ShapeDtypeStruct((t, D_MODEL), MXU_DTYPE)],
        input_output_aliases=aliases,
        compiler_params=_params(("parallel", "parallel")),
    )(*args)


def _tail(x, mo, final_w, target, tm=256):
    t = x.shape[0]

    def body(x_ref, mo_ref, w_ref, t_ref, dxm_ref, dx_ref, loss_ref, dw_ref):
        x2 = x_ref[...] + mo_ref[...]
        w = w_ref[...]
        r = lax.rsqrt(jnp.mean(x2 * x2, axis=-1, keepdims=True) + NORM_EPS)
        xn = x2 * r
        err = xn * w - t_ref[...]
        dy = err * (1.0 / D_MODEL)
        dyw = dy * w
        dx2 = r * dyw - x2 * (r * r * r) * jnp.mean(dyw * x2, axis=-1, keepdims=True)
        dx_ref[...] = dx2
        dxm_ref[...] = dx2.astype(dxm_ref.dtype)
        loss = 0.5 * jnp.sum(jnp.sum(err * err, axis=-1, keepdims=True) * (1.0 / D_MODEL), axis=0, keepdims=True)
        onehot = jnp.where((_iota((8, 128), 0) == 0) & (_iota((8, 128), 1) == 0), 1.0, 0.0)

        @pl.when(pl.program_id(0) == 0)
        def _():
            loss_ref[...] = jnp.zeros_like(loss_ref)
            dw_ref[...] = jnp.zeros_like(dw_ref)

        loss_ref[...] += loss * onehot
        dw_ref[...] += jnp.where(_iota((8, D_MODEL), 0) == 0, jnp.sum(dy * xn, axis=0, keepdims=True), 0.0)

    tok = pl.BlockSpec((tm, D_MODEL), lambda i: (i, 0))
    return pl.pallas_call(
        body, name="tail", grid=(t // tm,),
        in_specs=[tok, tok, pl.BlockSpec((1, D_MODEL), lambda i: (0, 0)), tok],
        out_specs=[tok, tok, pl.BlockSpec((8, 128), lambda i: (0, 0)), pl.BlockSpec((8, D_MODEL), lambda i: (0, 0))],
        out_shape=[jax.ShapeDtypeStruct((t, D_MODEL), MXU_DTYPE), jax.ShapeDtypeStruct((t, D_MODEL), F32),
                   jax.ShapeDtypeStruct((8, 128), F32), jax.ShapeDtypeStruct((8, D_MODEL), F32)],
        compiler_params=_params(("arbitrary",)),
    )(x, mo, final_w, target)


def _norm_bwd(x, norm_w, dh, dx2, tm=256):
    t = x.shape[0]

    def body(x_ref, w_ref, dh_ref, dx2_ref, dx_ref, dw_ref):
        xf, w, dh_ = x_ref[...], w_ref[...], dh_ref[...]
        r = lax.rsqrt(jnp.mean(xf * xf, axis=-1, keepdims=True) + NORM_EPS)
        dhw = dh_ * w
        dx_ref[...] = dx2_ref[...] + r * dhw - xf * (r * r * r) * jnp.mean(dhw * xf, axis=-1, keepdims=True)

        @pl.when(pl.program_id(0) == 0)
        def _():
            dw_ref[...] = jnp.zeros_like(dw_ref)

        dw_ref[...] += jnp.where(_iota((8, D_MODEL), 0) == 0, jnp.sum(dh_ * xf * r, axis=0, keepdims=True), 0.0)

    tok = pl.BlockSpec((tm, D_MODEL), lambda i: (i, 0))
    return pl.pallas_call(
        body, name="norm_bwd", grid=(t // tm,),
        in_specs=[tok, pl.BlockSpec((1, D_MODEL), lambda i: (0, 0)), tok, tok],
        out_specs=[tok, pl.BlockSpec((8, D_MODEL), lambda i: (0, 0))],
        out_shape=[jax.ShapeDtypeStruct((t, D_MODEL), F32), jax.ShapeDtypeStruct((8, D_MODEL), F32)],
        compiler_params=_params(("arbitrary",)),
    )(x, norm_w, dh, dx2)


def _local_step(x, target, norm_w, wp, conv_w, a_log, dt_bias, gdn_norm_w, w_up_a, w_up_b, w_out, final_w):
    t = x.shape[0]
    tables = _rope_tables(t)
    a_log = jnp.pad(a_log, ((0, 0), (HEADS, 128 - 2 * HEADS)))
    dt_bias = jnp.pad(dt_bias, ((0, 0), (HEADS, 128 - 2 * HEADS)))

    proj, h = _norm_proj(x, norm_w, wp)
    qkvs = _rope_fwd(proj, tables)
    outs, lses = zip(*[_att_fwd(qkvs[gi], d, f"att_fwd{gi}") for gi, d in enumerate(DILATIONS)])
    oz_a, o_a, *lse_views = _att_merge(outs, lses, proj)
    conv = _conv_fwd(proj, conv_w)
    gq, gk, gv, gb, gg, ggl, grow = _gdn_prep(conv, proj, a_log, dt_bias)
    o_b, states, invs = _gdn_fwd(gq, gk, gv, gb, gg, ggl, grow)
    oz_b = _gdn_out(o_b, gdn_norm_w, proj)
    big = dict(tm=1024, tn=1024, tk=1024)
    y_a = _matmul(oz_a, w_up_a, "nn", "up_a", **big)
    y_b = _matmul(oz_b, w_up_b, "nn", "up_b", **big)
    merged = _merge(y_a, y_b, proj)
    mo = _matmul(merged, w_out, "nn", "out_proj", **big)
    dx2_m, dx2, loss_blk, d_final = _tail(x, mo, final_w, target)

    d_wout = _matmul(merged, dx2_m, "tn", "d_w_out", **big)
    d_m = _matmul(dx2_m, w_out, "nt", "d_merged", **big)
    dproj, dy_a = _merge_bwd(None, d_m, y_a, proj, SEG_GA, "merge_bwd_a")
    dproj, dy_b = _merge_bwd(dproj, d_m, y_b, proj, SEG_GB, "merge_bwd_b")
    d_wua = _matmul(oz_a, dy_a, "tn", "d_w_up_a", **big)
    d_wub = _matmul(oz_b, dy_b, "tn", "d_w_up_b", **big)
    d_oz_a = _matmul(dy_a, w_up_a, "nt", "d_oz_a", **big)
    d_oz_b = _matmul(dy_b, w_up_b, "nt", "d_oz_b", **big)
    dproj, *views = _att_merge_bwd(dproj, d_oz_a, o_a, proj)
    do_views, delta_views = views[:3], views[3:]
    dqkvs = [_att_bwd(qkvs[gi], do_views[gi], lse_views[gi], delta_views[gi], d, f"att_bwd{gi}")
             for gi, d in enumerate(DILATIONS)]
    dproj = _rope_bwd(dproj, dqkvs, tables)
    dproj, do_b, d_gnw = _gdn_out_bwd(dproj, d_oz_b, o_b, gdn_norm_w, proj)
    dgq, dgk, dgv, dgb, dgg, dggl, dgrow = _gdn_bwd(gq, gk, gv, gb, gg, ggl, grow, states, invs, do_b)
    dproj, dconv, d_small = _gdn_prep_bwd(dproj, conv, proj, a_log, dt_bias, dgq, dgk, dgv, dgb, dgg, dggl, dgrow)
    dproj, d_convw = _conv_bwd(dproj, dconv, proj, conv_w)
    d_wp = _matmul(h, dproj, "tn", "d_w_in", tm=1024, tn=512, tk=1024)
    dh = _matmul(dproj, wp, "nt", "d_h", tm=1024, tn=1024, tk=512)
    grad_x, d_norm = _norm_bwd(x, norm_w, dh, dx2)
    return dict(loss=loss_blk, grad_x=grad_x, norm_w=d_norm[0:1], w_in=d_wp, conv_w=d_convw[0:GDN_CONV],
                a_log=d_small[0:1, HEADS:2 * HEADS], dt_bias=d_small[1:2, HEADS:2 * HEADS], gdn_norm_w=d_gnw[0:1],
                w_up_a=d_wua, w_up_b=d_wub,
                w_out=d_wout, final_norm_w=d_final[0:1])


SHARDS = 4
W_IN_SHARD = IN_WIDTH // SHARDS
ROWS_W_IN = D_MODEL * W_IN_SHARD // 128
ROWS_UP = WIDTH * (D_MODEL // SHARDS) // 128
ROWS_OUT = (D_MODEL // SHARDS) * D_MODEL // 128
CONV_SHARD = 3 * WIDTH // SHARDS
ROWS_CONV = 16
SLAB_ROWS = ROWS_W_IN + 2 * ROWS_UP + ROWS_OUT + 2 * ROWS_CONV
HALF_ROWS = SLAB_ROWS // 2
MESH = pl.DeviceIdType.MESH
ANY = pl.BlockSpec(memory_space=pl.ANY)


def _pad_rows(a, rows):
    return jnp.pad(a, ((0, rows - a.shape[0]), (0, 0)))


def _pack_slab(w_in, w_up_a, w_up_b, w_out, conv, conv_lo):
    parts = [w_in.reshape(ROWS_W_IN, 128), w_up_a.reshape(ROWS_UP, 128), w_up_b.reshape(ROWS_UP, 128),
             w_out.reshape(ROWS_OUT, 128), _pad_rows(conv.reshape(-1, 128), ROWS_CONV), _pad_rows(conv_lo.reshape(-1, 128), ROWS_CONV)]
    return jnp.concatenate(parts, axis=0)


def _unpack_slab(slab):
    r0 = 0
    out = []
    for rows, shape in ((ROWS_W_IN, (D_MODEL, W_IN_SHARD)), (ROWS_UP, (WIDTH, D_MODEL // SHARDS)), (ROWS_UP, (WIDTH, D_MODEL // SHARDS)),
                        (ROWS_OUT, (D_MODEL // SHARDS, D_MODEL)), (ROWS_CONV, None), (ROWS_CONV, None)):
        part = slab[r0:r0 + rows]
        out.append(part[:GDN_CONV * CONV_SHARD // 128].reshape(GDN_CONV, CONV_SHARD) if shape is None else part.reshape(shape))
        r0 += rows
    return out


def _mesh_position():
    x, y, c = lax.axis_index("x"), lax.axis_index("y"), lax.axis_index("c")
    return x, y, c, [(1 - x, y), (x, 1 - y), (1 - x, 1 - y)]


def _gather_weights(slab):
    def body(slab_ref, out_ref, send_sems, recv_sems):
        x, y, c, chips = _mesh_position()

        def half(chip, which):
            return out_ref.at[2 * chip[0] + chip[1], which]

        def copy(k, src, dst, to):
            return pltpu.make_async_remote_copy(src_ref=src, dst_ref=dst, send_sem=send_sems.at[k], recv_sem=recv_sems.at[k],
                                                device_id=to, device_id_type=MESH)

        first = [copy(j, slab_ref.at[c], half((x, y), c), (*chip, c)) for j, chip in enumerate(chips)]
        for cp in first:
            cp.start()
        passed = [copy(3 + j, half(chip, c), half(chip, c), (x, y, 1 - c)) for j, chip in enumerate(chips)]
        for j, chip in enumerate(chips):
            copy(j, half(chip, c), half(chip, c), (x, y, c)).wait_recv()
            passed[j].start()
        for j, chip in enumerate(chips):
            copy(3 + j, half(chip, 1 - c), half(chip, 1 - c), (x, y, c)).wait_recv()
        for cp in first + passed:
            cp.wait_send()

    return pl.pallas_call(
        body, name="gather_weights", in_specs=[ANY], out_specs=ANY,
        out_shape=jax.ShapeDtypeStruct((SHARDS, 2, HALF_ROWS, 128), slab.dtype),
        scratch_shapes=[pltpu.SemaphoreType.DMA((6,)), pltpu.SemaphoreType.DMA((6,))],
    )(slab)


def _exchange_halves(grads):
    def body(g_ref, out_ref, send_sems, recv_sems):
        x, y, c, _ = _mesh_position()
        copies = [pltpu.make_async_remote_copy(src_ref=g_ref.at[s, 1 - c], dst_ref=out_ref.at[s],
                                               send_sem=send_sems.at[s], recv_sem=recv_sems.at[s],
                                               device_id=(x, y, 1 - c), device_id_type=MESH) for s in range(SHARDS)]
        for cp in copies:
            cp.start()
        for cp in copies:
            cp.wait()

    return pl.pallas_call(
        body, name="exchange_halves", in_specs=[ANY], out_specs=ANY,
        out_shape=jax.ShapeDtypeStruct((SHARDS, HALF_ROWS, 128), F32),
        scratch_shapes=[pltpu.SemaphoreType.DMA((SHARDS,)), pltpu.SemaphoreType.DMA((SHARDS,))],    )(grads)


SUM_ROWS = HALF_ROWS // 2


def _pair_sum(grads, recv):
    nblk = HALF_ROWS // SUM_ROWS

    def body(c_ref, g_ref, r_ref, o_ref):
        o_ref[...] = (g_ref[0] + r_ref[...]).astype(o_ref.dtype)

    spec = pl.BlockSpec((1, SUM_ROWS, 128), lambda s, i, c_ref: (s, i, 0))
    return pl.pallas_call(
        body, name="pair_sum",
        grid_spec=pltpu.PrefetchScalarGridSpec(
            num_scalar_prefetch=1, grid=(SHARDS, nblk),
            in_specs=[pl.BlockSpec((1, 1, SUM_ROWS, 128), lambda s, i, c_ref: (s, c_ref[0], i, 0)), spec],
            out_specs=spec),
        out_shape=jax.ShapeDtypeStruct((SHARDS, HALF_ROWS, 128), MXU_DTYPE),
        compiler_params=_params(("parallel", "parallel")),
    )(lax.axis_index("c").astype(jnp.int32).reshape(1), grads, recv)


def _scatter_pairs(pairs):
    def body(p_ref, out_ref, send_sems, recv_sems):
        x, y, c, chips = _mesh_position()
        copies = [pltpu.make_async_remote_copy(src_ref=p_ref.at[2 * chip[0] + chip[1]], dst_ref=out_ref.at[j],
                                               send_sem=send_sems.at[j], recv_sem=recv_sems.at[j],
                                               device_id=(*chip, c), device_id_type=MESH) for j, chip in enumerate(chips)]
        for cp in copies:
            cp.start()
        for cp in copies:
            cp.wait()

    return pl.pallas_call(
        body, name="scatter_pairs", in_specs=[ANY], out_specs=ANY,
        out_shape=jax.ShapeDtypeStruct((3, HALF_ROWS, 128), pairs.dtype),
        scratch_shapes=[pltpu.SemaphoreType.DMA((3,)), pltpu.SemaphoreType.DMA((3,))],    )(pairs)


def _chip_sum(pairs, recv):
    def body(pos_ref, p_ref, r_ref, o_ref):
        o_ref[0] = ((p_ref[0].astype(F32) + r_ref[0].astype(F32)) + r_ref[1].astype(F32)) + r_ref[2].astype(F32)

    pos = jnp.stack([2 * lax.axis_index("x") + lax.axis_index("y"), lax.axis_index("c")]).astype(jnp.int32)
    return pl.pallas_call(
        body, name="chip_sum",
        grid_spec=pltpu.PrefetchScalarGridSpec(
            num_scalar_prefetch=1, grid=(HALF_ROWS // SUM_ROWS,),
            in_specs=[pl.BlockSpec((1, SUM_ROWS, 128), lambda i, pos_ref: (pos_ref[0], i, 0)),
                      pl.BlockSpec((3, SUM_ROWS, 128), lambda i, pos_ref: (0, i, 0))],
            out_specs=pl.BlockSpec((1, SUM_ROWS, 128), lambda i, pos_ref: (pos_ref[1], i, 0))),
        out_shape=jax.ShapeDtypeStruct((2, HALF_ROWS, 128), F32),
        compiler_params=_params(("parallel",)),
    )(pos, pairs, recv)


def _share_total(total):
    def body(t_ref, out_ref, send_sem, recv_sem):
        x, y, c, _ = _mesh_position()
        cp = pltpu.make_async_remote_copy(src_ref=t_ref.at[c], dst_ref=out_ref.at[c],
                                          send_sem=send_sem, recv_sem=recv_sem, device_id=(x, y, 1 - c), device_id_type=MESH)
        cp.start()
        other = out_ref.at[1 - c]
        pltpu.make_async_remote_copy(src_ref=other, dst_ref=other, send_sem=send_sem, recv_sem=recv_sem,
                                     device_id=(x, y, c), device_id_type=MESH).wait_recv()
        cp.wait_send()

    return pl.pallas_call(
        body, name="share_total", in_specs=[ANY], out_specs=ANY,
        out_shape=jax.ShapeDtypeStruct((2, HALF_ROWS, 128), F32),
        scratch_shapes=[pltpu.SemaphoreType.DMA, pltpu.SemaphoreType.DMA],
        input_output_aliases={0: 0},
    )(total)


def _allreduce_small(block):
    def body(b_ref, out_ref, gath, send_sems, recv_sems):
        x, y, c, _ = _mesh_position()
        me = 4 * x + 2 * y + c
        gath[me] = b_ref[...]
        copies = []
        for k in range(1, 8):
            peer = (x ^ (k >> 2), y ^ ((k >> 1) & 1), c ^ (k & 1))
            copies.append(pltpu.make_async_remote_copy(src_ref=b_ref, dst_ref=gath.at[me], send_sem=send_sems.at[k - 1],
                                                       recv_sem=recv_sems.at[k - 1], device_id=peer, device_id_type=MESH))
        for cp in copies:
            cp.start()
        for k in range(1, 8):
            src = 4 * (x ^ (k >> 2)) + 2 * (y ^ ((k >> 1) & 1)) + (c ^ (k & 1))
            pltpu.make_async_remote_copy(src_ref=b_ref, dst_ref=gath.at[src], send_sem=send_sems.at[k - 1],
                                         recv_sem=recv_sems.at[k - 1], device_id=(x, y, c), device_id_type=MESH).wait_recv()
        for cp in copies:
            cp.wait_send()
        acc = gath[0]
        for d in range(1, 8):
            acc = acc + gath[d]
        out_ref[...] = acc

    vm = pl.BlockSpec(memory_space=pltpu.VMEM)
    return pl.pallas_call(
        body, name="allreduce_small", in_specs=[vm], out_specs=vm,
        out_shape=jax.ShapeDtypeStruct((8, D_MODEL), F32),
        scratch_shapes=[pltpu.VMEM((8, 8, D_MODEL), F32), pltpu.SemaphoreType.DMA((7,)), pltpu.SemaphoreType.DMA((7,))],
    )(block)


def _adamw(w, g, m, v, name):
    rows, cols = w.shape
    tr = 128 if rows % 128 == 0 else rows

    def body(w_ref, g_ref, m_ref, v_ref, d_ref, nm_ref, nv_ref):
        gv = g_ref[...]
        nm = ADAM_B1 * m_ref[...] + (1.0 - ADAM_B1) * gv
        nv = ADAM_B2 * v_ref[...] + (1.0 - ADAM_B2) * (gv * gv)
        m_hat = nm / (1.0 - ADAM_B1 ** ADAM_STEP)
        v_hat = nv / (1.0 - ADAM_B2 ** ADAM_STEP)
        d_ref[...] = -ADAM_LR * (m_hat / (jnp.sqrt(v_hat) + ADAM_EPS) + ADAM_WD * w_ref[...])
        nm_ref[...] = nm
        nv_ref[...] = nv

    spec = pl.BlockSpec((tr, cols), lambda i: (i, 0))
    shape = jax.ShapeDtypeStruct((rows, cols), F32)
    return pl.pallas_call(
        body, name=name, grid=(rows // tr,), in_specs=[spec] * 4, out_specs=[spec] * 3, out_shape=[shape] * 3,
        compiler_params=_params(("parallel",)),
    )(w, g, m, v)


def _pack_w_in(w):
    return jnp.concatenate([w[:, :BA_END], jnp.zeros((D_MODEL, SEG_GA - BA_END), w.dtype), w[:, BA_END:]], axis=1)


def kernel(x, norm_w, w_in, conv_w, a_log, dt_bias, gdn_norm_w, w_up_a, w_up_b, w_out, final_norm_w, loss_target, m_norm_w, m_w_in, m_conv_w, m_a_log, m_dt_bias, m_gdn_norm_w, m_w_up_a, m_w_up_b, m_w_out, m_final_norm_w, v_norm_w, v_w_in, v_conv_w, v_a_log, v_dt_bias, v_gdn_norm_w, v_w_up_a, v_w_up_b, v_w_out, v_final_norm_w):
    conv_hi = conv_w[0].astype(MXU_DTYPE)
    conv_lo = (conv_w[0] - conv_hi.astype(F32)).astype(MXU_DTYPE)
    slab = _pack_slab(w_in[0].astype(MXU_DTYPE), w_up_a[0].astype(MXU_DTYPE), w_up_b[0].astype(MXU_DTYPE),
                      w_out[0].astype(MXU_DTYPE), conv_hi, conv_lo)
    own_shard = 2 * lax.axis_index("x") + lax.axis_index("y")
    slabs = lax.dynamic_update_slice(_gather_weights(slab.reshape(2, HALF_ROWS, 128)).reshape(SHARDS, SLAB_ROWS, 128),
                                     slab[None], (own_shard, 0, 0))
    parts = [_unpack_slab(slabs[s]) for s in range(SHARDS)]
    split = BA_END - (SHARDS - 1) * W_IN_SHARD
    wp = jnp.concatenate([p[0] for p in parts[:-1]] + [parts[-1][0][:, :split], jnp.zeros((D_MODEL, SEG_GA - BA_END), MXU_DTYPE),
                                                       parts[-1][0][:, split:]], axis=1)
    w_up_a_full = jnp.concatenate([p[1] for p in parts], axis=1)
    w_up_b_full = jnp.concatenate([p[2] for p in parts], axis=1)
    w_out_full = jnp.concatenate([p[3] for p in parts], axis=0)
    conv_full = jnp.concatenate([p[4].astype(F32) + p[5].astype(F32) for p in parts], axis=1)

    g = _local_step(x[0], loss_target[0], norm_w, wp, conv_full, a_log, dt_bias, gdn_norm_w,
                    w_up_a_full, w_up_b_full, w_out_full, final_norm_w[None])

    d_wp = g["w_in"]
    d_w_in = [d_wp[:, s * W_IN_SHARD:(s + 1) * W_IN_SHARD] for s in range(SHARDS - 1)]
    d_w_in.append(jnp.concatenate([d_wp[:, (SHARDS - 1) * W_IN_SHARD:BA_END], d_wp[:, SEG_GA:]], axis=1))
    zero_conv = jnp.zeros((GDN_CONV, CONV_SHARD), F32)
    grads = jnp.stack([
        _pack_slab(d_w_in[s], g["w_up_a"][:, s * 256:(s + 1) * 256],
                   g["w_up_b"][:, s * 256:(s + 1) * 256], g["w_out"][s * 256:(s + 1) * 256],
                   g["conv_w"][:, s * CONV_SHARD:(s + 1) * CONV_SHARD], zero_conv) for s in range(SHARDS)])
    grads = grads.reshape(SHARDS, 2, HALF_ROWS, 128)
    from_sibling = _exchange_halves(grads)
    pairs = _pair_sum(grads, from_sibling)
    from_chips = _scatter_pairs(pairs)
    total = _share_total(_chip_sum(pairs, from_chips)).reshape(SLAB_ROWS, 128)
    g_w_in, g_w_up_a, g_w_up_b, g_w_out, g_conv, _ = _unpack_slab(total)

    row2 = jnp.concatenate([g["gdn_norm_w"], g["a_log"], g["dt_bias"], g["loss"][0:1, 0:1],
                            jnp.zeros((1, D_MODEL - HEAD_DIM - 2 * HEADS - 1), F32)], axis=1)
    small = _allreduce_small(jnp.concatenate([g["norm_w"], g["final_norm_w"], row2, jnp.zeros((5, D_MODEL), F32)], axis=0))
    g_norm, g_final = small[0:1], small[1]
    g_gnw, g_alog, g_dt = small[2:3, 0:HEAD_DIM], small[2:3, HEAD_DIM:HEAD_DIM + HEADS], small[2:3, HEAD_DIM + HEADS:HEAD_DIM + 2 * HEADS]
    loss = small[2, HEAD_DIM + 2 * HEADS]

    names = ["norm_w", "w_in", "conv_w", "a_log", "dt_bias", "gdn_norm_w", "w_up_a", "w_up_b", "w_out", "final_norm_w"]
    weights = dict(zip(names, (norm_w, w_in, conv_w, a_log, dt_bias, gdn_norm_w, w_up_a, w_up_b, w_out, final_norm_w)))
    ms = dict(zip(names, (m_norm_w, m_w_in, m_conv_w, m_a_log, m_dt_bias, m_gdn_norm_w, m_w_up_a, m_w_up_b, m_w_out, m_final_norm_w)))
    vs = dict(zip(names, (v_norm_w, v_w_in, v_conv_w, v_a_log, v_dt_bias, v_gdn_norm_w, v_w_up_a, v_w_up_b, v_w_out, v_final_norm_w)))
    grads2d = dict(norm_w=g_norm, w_in=g_w_in, conv_w=g_conv, a_log=g_alog, dt_bias=g_dt, gdn_norm_w=g_gnw,
                   w_up_a=g_w_up_a, w_up_b=g_w_up_b, w_out=g_w_out, final_norm_w=g_final[None])
    grad_out, delta, new_m, new_v = [], [], [], []
    for n in names:
        shape = weights[n].shape
        two_d = grads2d[n].shape
        d, nm, nv = _adamw(weights[n].reshape(two_d), grads2d[n], ms[n].reshape(two_d), vs[n].reshape(two_d), f"adamw_{n}")
        grad_out.append(grads2d[n].reshape(shape))
        delta.append(d.reshape(shape))
        new_m.append(nm.reshape(shape))
        new_v.append(nv.reshape(shape))
    return (loss, g["grad_x"][None], *grad_out, *delta, *new_m, *new_v)
```

```python
import functools

import jax
import jax.numpy as jnp
from jax import lax
from jax.experimental import pallas as pl
from jax.experimental.pallas import tpu as pltpu

F32 = jnp.float32
MXU_DTYPE = jnp.bfloat16
HIGHEST = lax.Precision.HIGHEST

D_MODEL = 1024
HEADS = 8
HEAD_DIM = 64
WIDTH = HEADS * HEAD_DIM
NORM_EPS = 1e-6
ROPE_THETA = 10000.0
ATT_BLOCK = 128
DILATIONS = (1, 4, 16)
GDN_CHUNK = 64
GDN_CONV = 4
IN_WIDTH = 9232
SEG_A, SEG_ZA, SEG_B, SEG_ZB, SEG_BA, SEG_GA, SEG_GB, PACKED_WIDTH = 0, 4608, 5120, 6656, 7168, 7680, 8704, 9728
BA_END = 7184
VMEM_LIMIT = 56 * 1024 * 1024

ADAM_LR, ADAM_B1, ADAM_B2, ADAM_EPS, ADAM_WD, ADAM_STEP = 0.001, 0.9, 0.999, 1e-08, 0.01, 10

_NN = (((1,), (0,)), ((), ()))
_NT = (((1,), (1,)), ((), ()))
_TN = (((0,), (0,)), ((), ()))


def _params(sem):
    return pltpu.CompilerParams(dimension_semantics=sem, vmem_limit_bytes=VMEM_LIMIT)


def _mxu(a, b, dims):
    return lax.dot_general(a.astype(MXU_DTYPE), b.astype(MXU_DTYPE), dims, preferred_element_type=F32)


def _sigmoid(x):
    return 1.0 / (1.0 + jnp.exp(-x))


def _softplus(x):
    return jnp.maximum(x, 0.0) + jnp.log(1.0 + jnp.exp(-jnp.abs(x)))


def _iota(shape, axis):
    return lax.broadcasted_iota(jnp.int32, shape, axis)


def _matmul(a, b, mode, name, out_dtype=F32, tm=512, tn=512, tk=512):
    if mode == "nn":
        (m, k), (k2, n) = a.shape, b.shape
    elif mode == "nt":
        (m, k), (n, k2) = a.shape, b.shape
    else:
        (k, m), (k2, n) = a.shape, b.shape
    assert k == k2
    tm, tn, tk = min(tm, m), min(tn, n), min(tk, k)
    assert m % tm == 0 and n % tn == 0 and k % tk == 0
    nk = k // tk
    dims = {"nn": _NN, "nt": _NT, "tn": _TN}[mode]

    def body(a_ref, b_ref, o_ref, acc_ref):
        kk = pl.program_id(2)

        @pl.when(kk == 0)
        def _():
            acc_ref[...] = jnp.zeros_like(acc_ref)

        acc_ref[...] += _mxu(a_ref[...], b_ref[...], dims)

        @pl.when(kk == nk - 1)
        def _():
            o_ref[...] = acc_ref[...].astype(o_ref.dtype)

    a_spec = pl.BlockSpec((tk, tm), lambda i, j, kk: (kk, i)) if mode == "tn" else pl.BlockSpec((tm, tk), lambda i, j, kk: (i, kk))
    b_spec = pl.BlockSpec((tn, tk), lambda i, j, kk: (j, kk)) if mode == "nt" else pl.BlockSpec((tk, tn), lambda i, j, kk: (kk, j))
    return pl.pallas_call(
        body, name=name, grid=(m // tm, n // tn, nk), in_specs=[a_spec, b_spec],
        out_specs=pl.BlockSpec((tm, tn), lambda i, j, kk: (i, j)),
        out_shape=jax.ShapeDtypeStruct((m, n), out_dtype),
        scratch_shapes=[pltpu.VMEM((tm, tn), F32)],
        compiler_params=_params(("parallel", "parallel", "arbitrary")),
    )(a, b)


def _norm_proj(x, norm_w, wp, tm=1024, tn=512):
    t = x.shape[0]

    def body(x_ref, nw_ref, w_ref, proj_ref, h_ref):
        @pl.when(pl.program_id(1) == 0)
        def _():
            xf = x_ref[...]
            r = lax.rsqrt(jnp.mean(xf * xf, axis=-1, keepdims=True) + NORM_EPS)
            h_ref[...] = (xf * r * nw_ref[...]).astype(h_ref.dtype)

        proj_ref[...] = jnp.dot(h_ref[...], w_ref[...], preferred_element_type=F32)

    return pl.pallas_call(
        body, name="norm_proj", grid=(t // tm, PACKED_WIDTH // tn),
        in_specs=[pl.BlockSpec((tm, D_MODEL), lambda i, j: (i, 0)),
                  pl.BlockSpec((1, D_MODEL), lambda i, j: (0, 0)),
                  pl.BlockSpec((D_MODEL, tn), lambda i, j: (0, j))],
        out_specs=[pl.BlockSpec((tm, tn), lambda i, j: (i, j)),
                   pl.BlockSpec((tm, D_MODEL), lambda i, j: (i, 0))],
        out_shape=[jax.ShapeDtypeStruct((t, PACKED_WIDTH), F32), jax.ShapeDtypeStruct((t, D_MODEL), MXU_DTYPE)],
        compiler_params=_params(("parallel", "arbitrary")),
    )(x, norm_w, wp)


def _rope_tables(t):
    lane = jnp.arange(128)
    inv_freq = ROPE_THETA ** (-jnp.arange(0, HEAD_DIM, 2, dtype=F32) / HEAD_DIM)
    ang = jnp.arange(t, dtype=F32)[:, None] * inv_freq[None, :]
    ang = jnp.concatenate([ang, ang, ang, ang], axis=-1)
    first_half = (lane % HEAD_DIM) < HEAD_DIM // 2
    cos, sin = jnp.cos(ang), jnp.sin(ang)
    return cos, jnp.where(first_half, -sin, 0.0), jnp.where(first_half, 0.0, sin)


def _rope_block(x, cos, sin_lo, sin_hi, sign):
    outs = []
    for c in range(8):
        xc = x[:, c * 128:(c + 1) * 128]
        rot = pltpu.roll(xc, 96, 1) * sin_lo + pltpu.roll(xc, 32, 1) * sin_hi
        outs.append(xc * cos + sign * rot)
    outs.append(x[:, 2 * WIDTH:])
    return jnp.concatenate(outs, axis=1)


def _tile_scratch(tm, cols):
    return pltpu.VMEM((cols // 128, tm, 128), F32)


def _store_tile(scr, y):
    for c in range(scr.shape[0]):
        scr[c] = y[:, c * 128:(c + 1) * 128]


def _load_tile(scr):
    return jnp.concatenate([scr[c] for c in range(scr.shape[0])], axis=1)


def _to_strided_view(scr, o_ref, d):
    n, tm, _ = scr.shape
    for r in range(d):
        for c in range(n):
            o_ref[:, (r * n + c) * 128:(r * n + c + 1) * 128] = scr[c, pl.ds(r, tm // d, stride=d), :].astype(o_ref.dtype)


def _from_strided_view(i_ref, scr, d):
    n, tm, _ = scr.shape
    for r in range(d):
        for c in range(n):
            scr[c, pl.ds(r, tm // d, stride=d), :] = i_ref[:, (r * n + c) * 128:(r * n + c + 1) * 128].astype(F32)


def _strided_spec(tm, d, cols):
    return pl.BlockSpec((tm // d, d * cols), lambda i: (i, 0))


def _rope_fwd(proj, tables, tm=256):
    t = proj.shape[0]
    cols = 3 * WIDTH

    def body(x_ref, c_ref, sl_ref, sh_ref, o0, o1, o2, scr):
        for g, (d, o_ref) in enumerate(zip(DILATIONS, (o0, o1, o2))):
            y = _rope_block(x_ref[:, g * cols:(g + 1) * cols], c_ref[...], sl_ref[...], sh_ref[...], 1.0)
            if d == 1:
                o_ref[...] = y.astype(o_ref.dtype)
            else:
                _store_tile(scr, y)
                _to_strided_view(scr, o_ref, d)

    tab = pl.BlockSpec((tm, 128), lambda i: (i, 0))
    return pl.pallas_call(
        body, name="rope_fwd", grid=(t // tm,),
        in_specs=[pl.BlockSpec((tm, 3 * cols), lambda i: (i, 0)), tab, tab, tab],
        out_specs=[_strided_spec(tm, d, cols) for d in DILATIONS],
        out_shape=[jax.ShapeDtypeStruct((t // d, d * cols), MXU_DTYPE) for d in DILATIONS],
        scratch_shapes=[_tile_scratch(tm, cols)],
        compiler_params=_params(("parallel",)),
    )(proj, *tables)


def _rope_bwd(dproj, dqkvs, tables, tm=256):
    t = dproj.shape[0]
    cols = 3 * WIDTH

    def body(dp_ref, i0, i1, i2, c_ref, sl_ref, sh_ref, o_ref, scr):
        for g, (d, i_ref) in enumerate(zip(DILATIONS, (i0, i1, i2))):
            if d == 1:
                x = i_ref[...]
            else:
                _from_strided_view(i_ref, scr, d)
                x = _load_tile(scr)
            y = _rope_block(x, c_ref[...], sl_ref[...], sh_ref[...], -1.0)
            o_ref[:, g * cols:(g + 1) * cols] = y.astype(o_ref.dtype)

    tab = pl.BlockSpec((tm, 128), lambda i: (i, 0))
    return pl.pallas_call(
        body, name="rope_bwd", grid=(t // tm,),
        in_specs=[pl.BlockSpec(memory_space=pl.ANY)] + [_strided_spec(tm, d, cols) for d in DILATIONS] + [tab, tab, tab],
        out_specs=pl.BlockSpec((tm, 3 * cols), lambda i: (i, 0)),
        out_shape=jax.ShapeDtypeStruct((t, PACKED_WIDTH), MXU_DTYPE),
        scratch_shapes=[_tile_scratch(tm, cols)],
        input_output_aliases={0: 0},
        compiler_params=_params(("parallel",)),
    )(dproj, *dqkvs, *tables)


def _att_masks():
    qi = _iota((ATT_BLOCK, ATT_BLOCK), 0)
    kj = _iota((ATT_BLOCK, ATT_BLOCK), 1)
    return kj <= qi, kj >= qi


def _att_fwd(qkv, d, name):
    rows = qkv.shape[0]
    nb = rows // ATT_BLOCK
    scale = HEAD_DIM ** -0.5

    def body(q_ref, kc_ref, kp_ref, vc_ref, vp_ref, o_ref, lse_ref):
        has_prev = pl.program_id(1) > 0
        m_cur, m_prev = _att_masks()
        m_prev = m_prev & has_prev
        hs = range(HEADS)
        sls = [slice(h * HEAD_DIM, (h + 1) * HEAD_DIM) for h in hs]
        qs = [q_ref[:, sl] for sl in sls]
        s_c = [jnp.where(m_cur, _mxu(qs[h], kc_ref[:, sls[h]], _NT) * scale, -jnp.inf) for h in hs]
        s_p = [jnp.where(m_prev, _mxu(qs[h], kp_ref[:, sls[h]], _NT) * scale, -jnp.inf) for h in hs]
        m = [jnp.maximum(jnp.max(s_c[h], axis=1, keepdims=True), jnp.max(s_p[h], axis=1, keepdims=True)) for h in hs]
        p_c = [jnp.exp(s_c[h] - m[h]) for h in hs]
        p_p = [jnp.exp(s_p[h] - m[h]) for h in hs]
        den = [jnp.sum(p_c[h], axis=1, keepdims=True) + jnp.sum(p_p[h], axis=1, keepdims=True) for h in hs]
        o = [_mxu(p_c[h], vc_ref[:, sls[h]], _NN) + _mxu(p_p[h], vp_ref[:, sls[h]], _NN) for h in hs]
        for h in hs:
            o_ref[:, sls[h]] = o[h] / den[h]
            lse_ref[:, sls[h]] = jnp.broadcast_to(m[h] + jnp.log(den[h]), (ATT_BLOCK, HEAD_DIM))

    def cur(c):
        return pl.BlockSpec((ATT_BLOCK, WIDTH), lambda r, i: (i, 3 * r + c))

    def prev(c):
        return pl.BlockSpec((ATT_BLOCK, WIDTH), lambda r, i: (jnp.maximum(i - 1, 0), 3 * r + c))

    out = pl.BlockSpec((ATT_BLOCK, WIDTH), lambda r, i: (i, r))
    return pl.pallas_call(
        body, name=name, grid=(d, nb), in_specs=[cur(0), cur(1), prev(1), cur(2), prev(2)],
        out_specs=[out, out],
        out_shape=[jax.ShapeDtypeStruct((rows, d * WIDTH), F32)] * 2,
        compiler_params=_params(("parallel", "arbitrary")),
    )(qkv, qkv, qkv, qkv, qkv)


def _att_bwd(qkv, do, lse, delta, d, name):
    rows = qkv.shape[0]
    nb = rows // ATT_BLOCK
    scale = HEAD_DIM ** -0.5

    def body(q_ref, qn_ref, kc_ref, kp_ref, vc_ref, vp_ref, do_ref, don_ref, l_ref, ln_ref, dl_ref, dln_ref, o_ref):
        i = pl.program_id(1)
        m_cur, m_prev = _att_masks()
        m_p = m_prev & (i > 0)
        m_n = m_prev & (i < nb - 1)

        hs = range(HEADS)
        sls = [slice(h * HEAD_DIM, (h + 1) * HEAD_DIM) for h in hs]
        col = [slice(h * HEAD_DIM, h * HEAD_DIM + 1) for h in hs]

        def probs(q_r, k_r, lse_r, mask):
            s = [_mxu(q_r[:, sls[h]], k_r[:, sls[h]], _NT) for h in hs]
            return [jnp.where(mask, jnp.exp(s[h] * scale - lse_r[:, col[h]]), 0.0) for h in hs]

        def dscores(p, do_r, v_r, dl_r):
            dp = [_mxu(do_r[:, sls[h]], v_r[:, sls[h]], _NT) for h in hs]
            return [(p[h] * (dp[h] - dl_r[:, col[h]])).astype(MXU_DTYPE) for h in hs]

        p = probs(q_ref, kc_ref, l_ref, m_cur)
        ds = dscores(p, do_ref, vc_ref, dl_ref)
        dq = [_mxu(ds[h], kc_ref[:, sls[h]], _NN) for h in hs]
        dk = [_mxu(ds[h], q_ref[:, sls[h]], _TN) for h in hs]
        dv = [_mxu(p[h], do_ref[:, sls[h]], _TN) for h in hs]
        p = probs(q_ref, kp_ref, l_ref, m_p)
        ds = dscores(p, do_ref, vp_ref, dl_ref)
        dq = [dq[h] + _mxu(ds[h], kp_ref[:, sls[h]], _NN) for h in hs]
        p = probs(qn_ref, kc_ref, ln_ref, m_n)
        ds = dscores(p, don_ref, vc_ref, dln_ref)
        dk = [dk[h] + _mxu(ds[h], qn_ref[:, sls[h]], _TN) for h in hs]
        dv = [dv[h] + _mxu(p[h], don_ref[:, sls[h]], _TN) for h in hs]
        for h in hs:
            o_ref[:, sls[h]] = dq[h] * scale
            o_ref[:, WIDTH + h * HEAD_DIM:WIDTH + (h + 1) * HEAD_DIM] = dk[h] * scale
            o_ref[:, 2 * WIDTH + h * HEAD_DIM:2 * WIDTH + (h + 1) * HEAD_DIM] = dv[h]

    def qkv_spec(c, shift):
        def idx(r, i):
            return (jnp.clip(i + shift, 0, nb - 1), 3 * r + c)
        return pl.BlockSpec((ATT_BLOCK, WIDTH), idx)

    def tok_spec(shift):
        def idx(r, i):
            return (jnp.clip(i + shift, 0, nb - 1), r)
        return pl.BlockSpec((ATT_BLOCK, WIDTH), idx)

    return pl.pallas_call(
        body, name=name, grid=(d, nb),
        in_specs=[qkv_spec(0, 0), qkv_spec(0, 1), qkv_spec(1, 0), qkv_spec(1, -1), qkv_spec(2, 0), qkv_spec(2, -1),
                  tok_spec(0), tok_spec(1), tok_spec(0), tok_spec(1), tok_spec(0), tok_spec(1)],
        out_specs=pl.BlockSpec((ATT_BLOCK, 3 * WIDTH), lambda r, i: (i, r)),
        out_shape=jax.ShapeDtypeStruct((rows, d * 3 * WIDTH), F32),
        compiler_params=_params(("parallel", "arbitrary")),
    )(qkv, qkv, qkv, qkv, qkv, qkv, do, do, lse, lse, delta, delta)


def _att_merge(os_, lses, proj, tm=256):
    t = proj.shape[0]

    def body(o0, o1, o2, l0, l1, l2, z_ref, oz_ref, o_ref, t0, t1, t2, s_o1, s_l1, s_o2, s_l2, s_t):
        _from_strided_view(o1, s_o1, DILATIONS[1])
        _from_strided_view(l1, s_l1, DILATIONS[1])
        _from_strided_view(o2, s_o2, DILATIONS[2])
        _from_strided_view(l2, s_l2, DILATIONS[2])
        a, b, c = l0[...], _load_tile(s_l1), _load_tile(s_l2)
        m = jnp.maximum(jnp.maximum(a, b), c)
        wa, wb, wc = jnp.exp(a - m), jnp.exp(b - m), jnp.exp(c - m)
        den = wa + wb + wc
        o = (wa * o0[...] + wb * _load_tile(s_o1) + wc * _load_tile(s_o2)) / den
        z = z_ref[...]
        o_ref[...] = o
        oz_ref[...] = (o * z * _sigmoid(z)).astype(oz_ref.dtype)
        total = m + jnp.log(den)
        t0[...] = total
        _store_tile(s_t, total)
        _to_strided_view(s_t, t1, DILATIONS[1])
        _to_strided_view(s_t, t2, DILATIONS[2])

    tok = pl.BlockSpec((tm, WIDTH), lambda i: (i, 0))
    views = [_strided_spec(tm, d, WIDTH) for d in DILATIONS]
    view_shapes = [jax.ShapeDtypeStruct((t // d, d * WIDTH), F32) for d in DILATIONS]
    return pl.pallas_call(
        body, name="att_merge", grid=(t // tm,),
        in_specs=views + views + [pl.BlockSpec((tm, WIDTH), lambda i: (i, SEG_ZA // WIDTH))],
        out_specs=[tok, tok] + views,
        out_shape=[jax.ShapeDtypeStruct((t, WIDTH), MXU_DTYPE), jax.ShapeDtypeStruct((t, WIDTH), F32)] + view_shapes,
        scratch_shapes=[_tile_scratch(tm, WIDTH)] * 5,
        compiler_params=_params(("parallel",)),
    )(*os_, *lses, proj)


def _att_merge_bwd(dproj, d_oz, o, proj, tm=256):
    t = proj.shape[0]

    def body(dp_ref, doz_ref, o_ref, z_ref, dz_ref, do0, do1, do2, dl0, dl1, dl2, s_do, s_dl):
        z, ov, g = z_ref[...], o_ref[...], doz_ref[...]
        sg = _sigmoid(z)
        do = g * z * sg
        dz_ref[...] = (g * ov * sg * (1.0 + z * (1.0 - sg))).astype(dz_ref.dtype)
        do0[...] = do.astype(do0.dtype)
        _store_tile(s_do, do)
        prod = do * ov
        for h in range(HEADS):
            sl = slice(h * HEAD_DIM, (h + 1) * HEAD_DIM)
            half = slice((h % 2) * HEAD_DIM, (h % 2 + 1) * HEAD_DIM)
            s_dl[h // 2, :, half] = jnp.broadcast_to(jnp.sum(prod[:, sl], axis=1, keepdims=True), (tm, HEAD_DIM))
        dl0[...] = _load_tile(s_dl)
        for d, do_v, dl_v in ((DILATIONS[1], do1, dl1), (DILATIONS[2], do2, dl2)):
            _to_strided_view(s_do, do_v, d)
            _to_strided_view(s_dl, dl_v, d)

    tok = pl.BlockSpec((tm, WIDTH), lambda i: (i, 0))
    seg = pl.BlockSpec((tm, WIDTH), lambda i: (i, SEG_ZA // WIDTH))
    views = [_strided_spec(tm, d, WIDTH) for d in DILATIONS]
    return pl.pallas_call(
        body, name="att_merge_bwd", grid=(t // tm,),
        in_specs=[pl.BlockSpec(memory_space=pl.ANY), tok, tok, seg],
        out_specs=[seg] + views + views,
        out_shape=[jax.ShapeDtypeStruct((t, PACKED_WIDTH), MXU_DTYPE)]
        + [jax.ShapeDtypeStruct((t // d, d * WIDTH), MXU_DTYPE) for d in DILATIONS]
        + [jax.ShapeDtypeStruct((t // d, d * WIDTH), F32) for d in DILATIONS],
        scratch_shapes=[_tile_scratch(tm, WIDTH)] * 2,
        input_output_aliases={0: 0},
        compiler_params=_params(("parallel",)),
    )(dproj, d_oz, o, proj)


def _shift_down(x, halo, s):
    if s == 0:
        return x
    xs = pltpu.roll(x, s, 0)
    head = jnp.where(_iota((8, x.shape[1]), 0) < s, pltpu.roll(halo, s, 0), xs[0:8])
    return jnp.concatenate([head, xs[8:]], axis=0)


def _shift_up(x, nxt, s):
    if s == 0:
        return x
    n = x.shape[0]
    xs = pltpu.roll(x, n - s, 0)
    tail = jnp.where(_iota((8, x.shape[1]), 0) >= 8 - s, pltpu.roll(nxt, 8 - s, 0), xs[n - 8:])
    return jnp.concatenate([xs[:n - 8], tail], axis=0)


def _conv_fwd(proj, conv_w, tm=256):
    t = proj.shape[0]
    cb = SEG_B // WIDTH

    def body(x_ref, halo_ref, w_ref, c_ref):
        halo = jnp.where(pl.program_id(0) > 0, halo_ref[...], 0.0)
        x = x_ref[...]
        w = w_ref[...]
        acc = jnp.zeros((tm, WIDTH), F32)
        for j in range(GDN_CONV):
            acc += _shift_down(x, halo, GDN_CONV - 1 - j) * w[j:j + 1, :]
        c_ref[...] = acc

    return pl.pallas_call(
        body, name="conv_fwd", grid=(t // tm, 3),
        in_specs=[pl.BlockSpec((tm, WIDTH), lambda i, c: (i, cb + c)),
                  pl.BlockSpec((8, WIDTH), lambda i, c: (jnp.maximum(i * (tm // 8) - 1, 0), cb + c)),
                  pl.BlockSpec((GDN_CONV, WIDTH), lambda i, c: (0, c))],
        out_specs=pl.BlockSpec((tm, WIDTH), lambda i, c: (i, c)),
        out_shape=jax.ShapeDtypeStruct((t, 3 * WIDTH), F32),
        compiler_params=_params(("parallel", "parallel")),
    )(proj, proj, conv_w)


def _conv_bwd(dproj, dc, proj, conv_w, tm=256):
    t = proj.shape[0]
    cb = SEG_B // WIDTH
    nt = t // tm

    def body(dp_ref, dc_ref, dcn_ref, x_ref, halo_ref, w_ref, dx_ref, dw_ref):
        i = pl.program_id(1)
        w = w_ref[...]
        dcn = jnp.where(i < nt - 1, dcn_ref[...], 0.0)
        dcv = dc_ref[...]
        acc = jnp.zeros((tm, WIDTH), F32)
        for j in range(GDN_CONV):
            acc += _shift_up(dcv, dcn, GDN_CONV - 1 - j) * w[j:j + 1, :]
        dx_ref[...] = acc.astype(dx_ref.dtype)
        halo = jnp.where(i > 0, halo_ref[...], 0.0)
        x = x_ref[...]
        row8 = _iota((8, WIDTH), 0)
        part = jnp.zeros((8, WIDTH), F32)
        for j in range(GDN_CONV):
            s = jnp.sum(dcv * _shift_down(x, halo, GDN_CONV - 1 - j), axis=0, keepdims=True)
            part += jnp.where(row8 == j, s, 0.0)

        @pl.when(i == 0)
        def _():
            dw_ref[...] = jnp.zeros_like(dw_ref)

        dw_ref[...] += part

    return pl.pallas_call(
        body, name="conv_bwd", grid=(3, nt),
        in_specs=[pl.BlockSpec(memory_space=pl.ANY),
                  pl.BlockSpec((tm, WIDTH), lambda c, i: (i, c)),
                  pl.BlockSpec((8, WIDTH), lambda c, i: (jnp.minimum((i + 1) * (tm // 8), t // 8 - 1), c)),
                  pl.BlockSpec((tm, WIDTH), lambda c, i: (i, cb + c)),
                  pl.BlockSpec((8, WIDTH), lambda c, i: (jnp.maximum(i * (tm // 8) - 1, 0), cb + c)),
                  pl.BlockSpec((GDN_CONV, WIDTH), lambda c, i: (0, c))],
        out_specs=[pl.BlockSpec((tm, WIDTH), lambda c, i: (i, cb + c)),
                   pl.BlockSpec((8, WIDTH), lambda c, i: (0, c))],
        out_shape=[jax.ShapeDtypeStruct((t, PACKED_WIDTH), MXU_DTYPE), jax.ShapeDtypeStruct((8, 3 * WIDTH), F32)],
        input_output_aliases={0: 0},
        compiler_params=_params(("parallel", "arbitrary")),
    )(dproj, dc, dc, proj, proj, conv_w)


def _chunk_matrices(tm):
    r, c = _iota((tm, tm), 0), _iota((tm, tm), 1)
    same = (r // GDN_CHUNK) == (c // GDN_CHUNK)
    return jnp.where(same & (c <= r), 1.0, 0.0), jnp.where(same, 1.0, 0.0)


def _gdn_gates(ba, a_log, dt_bias):
    al = ba + dt_bias
    return _sigmoid(ba), -jnp.exp(a_log) * _softplus(al), _sigmoid(al)


def _head_lane_eye():
    return jnp.where(_iota((HEADS, 128), 1) == _iota((HEADS, 128), 0) + HEADS, 1.0, 0.0)


def _gdn_prep(conv, proj, a_log, dt_bias, tm=256):
    t = proj.shape[0]
    nc = tm // GDN_CHUNK

    def body(c_ref, ba_ref, al_ref, dt_ref, q_ref, k_ref, v_ref, b_ref, g_ref, gl_ref, grow_ref):
        c = c_ref[...]
        a = c * _sigmoid(c)
        beta, g, _ = _gdn_gates(ba_ref[:, 0:128], al_ref[...], dt_ref[...])
        lmat, cmat = _chunk_matrices(tm)
        gc = jnp.dot(lmat, g, precision=HIGHEST, preferred_element_type=F32)
        gl = jnp.dot(cmat, g, precision=HIGHEST, preferred_element_type=F32)
        grow = lax.dot_general(_head_lane_eye(), gc, _NT, precision=HIGHEST, preferred_element_type=F32)
        for h in range(HEADS):
            sl = slice(h * HEAD_DIM, (h + 1) * HEAD_DIM)
            qh, kh, vh = a[:, sl], a[:, WIDTH + h * HEAD_DIM:WIDTH + (h + 1) * HEAD_DIM], a[:, 2 * WIDTH + h * HEAD_DIM:2 * WIDTH + (h + 1) * HEAD_DIM]
            rq = lax.rsqrt(jnp.sum(qh * qh, axis=1, keepdims=True) + NORM_EPS)
            rk = lax.rsqrt(jnp.sum(kh * kh, axis=1, keepdims=True) + NORM_EPS)
            q_ref[h] = qh * (rq * HEAD_DIM ** -0.5)
            k_ref[h] = kh * rk
            v_ref[h] = vh
            b_ref[h] = jnp.broadcast_to(beta[:, h:h + 1], (tm, HEAD_DIM))
            g_ref[h] = jnp.broadcast_to(gc[:, HEADS + h:HEADS + h + 1], (tm, HEAD_DIM))
            gl_ref[h] = jnp.broadcast_to(gl[:, HEADS + h:HEADS + h + 1], (tm, HEAD_DIM))
            for cc in range(nc):
                grow_ref[h, cc] = grow[h:h + 1, cc * GDN_CHUNK:(cc + 1) * GDN_CHUNK]

    hm = pl.BlockSpec((HEADS, tm, HEAD_DIM), lambda i: (0, i, 0))
    small = pl.BlockSpec((1, 128), lambda i: (0, 0))
    hm_shape = jax.ShapeDtypeStruct((HEADS, t, HEAD_DIM), F32)
    return pl.pallas_call(
        body, name="gdn_prep", grid=(t // tm,),
        in_specs=[pl.BlockSpec((tm, 3 * WIDTH), lambda i: (i, 0)),
                  pl.BlockSpec((tm, WIDTH), lambda i: (i, SEG_BA // WIDTH)), small, small],
        out_specs=[hm] * 6 + [pl.BlockSpec((HEADS, nc, 1, GDN_CHUNK), lambda i: (0, i, 0, 0))],
        out_shape=[hm_shape] * 6 + [jax.ShapeDtypeStruct((HEADS, t // GDN_CHUNK, 1, GDN_CHUNK), F32)],
        compiler_params=_params(("parallel",)),
    )(conv, proj, a_log, dt_bias)


def _gdn_prep_bwd(dproj, conv, proj, a_log, dt_bias, dq, dk, dv, db, dg, dgl, dgrow, tm=256):
    t = proj.shape[0]
    nc = tm // GDN_CHUNK

    def body(dp_ref, c_ref, ba_ref, al_ref, dt_ref, dq_ref, dk_ref, dv_ref, db_ref, dg_ref, dgl_ref, dgrow_ref,
             dba_ref, dc_ref, small_ref, row_scr):
        c = c_ref[...]
        sg = _sigmoid(c)
        a = c * sg
        beta, g, sig_al = _gdn_gates(ba_ref[:, 0:128], al_ref[...], dt_ref[...])
        lane = _iota((1, 128), 1)
        d_beta = jnp.zeros((tm, 128), F32)
        d_gc = jnp.zeros((tm, 128), F32)
        d_gl = jnp.zeros((tm, 128), F32)
        for h in range(HEADS):
            for cc in range(nc):
                row_scr[h:h + 1, cc * GDN_CHUNK:(cc + 1) * GDN_CHUNK] = dgrow_ref[h, cc]
            d_beta += jnp.sum(db_ref[h], axis=1, keepdims=True) * jnp.where(lane == h, 1.0, 0.0)
            d_gc += jnp.sum(dg_ref[h], axis=1, keepdims=True) * jnp.where(lane == HEADS + h, 1.0, 0.0)
            d_gl += jnp.sum(dgl_ref[h], axis=1, keepdims=True) * jnp.where(lane == HEADS + h, 1.0, 0.0)
            qs, ks, vs = (slice(h * HEAD_DIM, (h + 1) * HEAD_DIM), slice(WIDTH + h * HEAD_DIM, WIDTH + (h + 1) * HEAD_DIM),
                          slice(2 * WIDTH + h * HEAD_DIM, 2 * WIDTH + (h + 1) * HEAD_DIM))
            qh, kh = a[:, qs], a[:, ks]
            rq = lax.rsqrt(jnp.sum(qh * qh, axis=1, keepdims=True) + NORM_EPS)
            rk = lax.rsqrt(jnp.sum(kh * kh, axis=1, keepdims=True) + NORM_EPS)
            gq, gk = dq_ref[h] * HEAD_DIM ** -0.5, dk_ref[h]
            da_q = rq * gq - qh * (rq * rq * rq) * jnp.sum(gq * qh, axis=1, keepdims=True)
            da_k = rk * gk - kh * (rk * rk * rk) * jnp.sum(gk * kh, axis=1, keepdims=True)
            dsilu = lambda s, x: s * (1.0 + x * (1.0 - s))
            dc_ref[:, qs] = da_q * dsilu(sg[:, qs], c[:, qs])
            dc_ref[:, ks] = da_k * dsilu(sg[:, ks], c[:, ks])
            dc_ref[:, vs] = dv_ref[h] * dsilu(sg[:, vs], c[:, vs])
        d_gc += lax.dot_general(row_scr[...], _head_lane_eye(), _TN, precision=HIGHEST, preferred_element_type=F32)
        lmat, cmat = _chunk_matrices(tm)
        d_g = (lax.dot_general(lmat, d_gc, _TN, precision=HIGHEST, preferred_element_type=F32)
               + lax.dot_general(cmat, d_gl, _TN, precision=HIGHEST, preferred_element_type=F32))
        d_al = d_g * (-jnp.exp(al_ref[...])) * sig_al
        d_bl = d_beta * beta * (1.0 - beta)
        dba_ref[...] = jnp.concatenate([d_bl + d_al, jnp.zeros((tm, WIDTH - 128), F32)], axis=1).astype(dba_ref.dtype)
        row8 = _iota((8, 128), 0)
        part = (jnp.where(row8 == 0, jnp.sum(d_g * g, axis=0, keepdims=True), 0.0)
                + jnp.where(row8 == 1, jnp.sum(d_al, axis=0, keepdims=True), 0.0))

        @pl.when(pl.program_id(0) == 0)
        def _():
            small_ref[...] = jnp.zeros_like(small_ref)

        small_ref[...] += part

    hm = pl.BlockSpec((HEADS, tm, HEAD_DIM), lambda i: (0, i, 0))
    small = pl.BlockSpec((1, 128), lambda i: (0, 0))
    seg = pl.BlockSpec((tm, WIDTH), lambda i: (i, SEG_BA // WIDTH))
    return pl.pallas_call(
        body, name="gdn_prep_bwd", grid=(t // tm,),
        in_specs=[pl.BlockSpec(memory_space=pl.ANY), pl.BlockSpec((tm, 3 * WIDTH), lambda i: (i, 0)), seg, small, small]
        + [hm] * 6 + [pl.BlockSpec((HEADS, nc, 1, GDN_CHUNK), lambda i: (0, i, 0, 0))],
        out_specs=[seg, pl.BlockSpec((tm, 3 * WIDTH), lambda i: (i, 0)), pl.BlockSpec((8, 128), lambda i: (0, 0))],
        out_shape=[jax.ShapeDtypeStruct((t, PACKED_WIDTH), MXU_DTYPE), jax.ShapeDtypeStruct((t, 3 * WIDTH), F32),
                   jax.ShapeDtypeStruct((8, 128), F32)],
        scratch_shapes=[pltpu.VMEM((HEADS, tm), F32)],
        input_output_aliases={0: 0},
        compiler_params=_params(("arbitrary",)),
    )(dproj, conv, proj, a_log, dt_bias, dq, dk, dv, db, dg, dgl, dgrow)


_BNN = (((2,), (1,)), ((0,), (0,)))
_BNT = (((2,), (2,)), ((0,), (0,)))
_BTN = (((1,), (1,)), ((0,), (0,)))


@jax.custom_vjp
def _MM_NN(a, b):
    return _mxu(a, b, _BNN)


@jax.custom_vjp
def _MM_NT(a, b):
    return _mxu(a, b, _BNT)


@jax.custom_vjp
def _MM_TN(a, b):
    return _mxu(a, b, _BTN)


_MM_NN.defvjp(lambda a, b: (_mxu(a, b, _BNN), (a, b)), lambda r, g: (_mxu(g, r[1], _BNT), _mxu(r[0], g, _BTN)))
_MM_NT.defvjp(lambda a, b: (_mxu(a, b, _BNT), (a, b)), lambda r, g: (_mxu(g, r[1], _BNN), _mxu(g, r[0], _BTN)))
_MM_TN.defvjp(lambda a, b: (_mxu(a, b, _BTN), (a, b)), lambda r, g: (_mxu(r[1], g, _BNT), _mxu(r[0], g, _BNN)))


def _split(a):
    hi = a.astype(MXU_DTYPE)
    return hi, (a - hi.astype(F32)).astype(MXU_DTYPE)


def _dot3(a, b, dims):
    (ah, al), (bh, bl) = a, b
    (ca,), (cb,) = dims[0]
    return lax.dot_general(jnp.concatenate([ah, ah, al], axis=ca), jnp.concatenate([bh, bl, bh], axis=cb), dims,
                           preferred_element_type=F32)


def _unit_lower_inverse(a):
    c = GDN_CHUNK
    eye = jnp.where(_iota((c, c), 0) == _iota((c, c), 1), 1.0, 0.0)
    x = eye - a
    p = a
    for _ in range(5):
        ps = _split(p)
        p = _dot3(ps, ps, _BNN)
        x = x + _dot3(_split(x), _split(p), _BNN)
    return x


@jax.custom_vjp
def _SAVED_INVERSE(a, t_inv):
    return t_inv


def _saved_inverse_bwd(t_inv, g):
    ts = _split(t_inv)
    return -_dot3(ts, _split(_dot3(_split(g), ts, _BNT)), _BTN), jnp.zeros_like(t_inv)


_SAVED_INVERSE.defvjp(lambda a, t_inv: (t_inv, t_inv), _saved_inverse_bwd)


def _gdn_chunk(q, k, v, beta, g1, g2, gl, state, t_inv=None):
    c = GDN_CHUNK
    if t_inv is None:
        _mm_nn, _mm_nt, _mm_tn = (functools.partial(_mxu, dims=dd) for dd in (_BNN, _BNT, _BTN))
    else:
        _mm_nn, _mm_nt, _mm_tn = _MM_NN, _MM_NT, _MM_TN
    row, col = _iota((c, c), 0), _iota((c, c), 1)
    incl, strict = row >= col, row > col
    decay = jnp.where(incl, jnp.exp(jnp.where(incl, g1 - g2, 0.0)), 0.0)
    eg = jnp.exp(g1)
    kb = k * beta
    a = _mm_nt(kb, k) * jnp.where(strict, decay, 0.0)
    inv = _unit_lower_inverse(a) if t_inv is None else _SAVED_INVERSE(a, t_inv)
    u = _mm_nn(inv, v * beta)
    w = _mm_nn(inv, kb * eg)
    attn = _mm_nt(q, k) * decay
    v_new = u - _mm_nn(w, state)
    o = _mm_nn(q * eg, state) + _mm_nn(attn, v_new)
    new_state = state * jnp.exp(gl) + _mm_tn(k * jnp.exp(gl - g1), v_new)
    return (o, new_state, inv) if t_inv is None else (o, new_state)


def _gdn_fwd(q, k, v, beta, g, gl, grow, cpb=2):
    t = q.shape[1]
    rows = cpb * GDN_CHUNK

    def body(q_ref, k_ref, v_ref, b_ref, g_ref, gl_ref, grow_ref, o_ref, st_ref, inv_ref, state):
        @pl.when(pl.program_id(0) == 0)
        def _():
            state[...] = jnp.zeros_like(state)

        s = state[...]
        for cc in range(cpb):
            sl = slice(cc * GDN_CHUNK, (cc + 1) * GDN_CHUNK)
            st_ref[:, cc] = s
            g2 = jnp.broadcast_to(grow_ref[:, cc], (HEADS, GDN_CHUNK, GDN_CHUNK))
            o, s, inv = _gdn_chunk(q_ref[:, sl, :], k_ref[:, sl, :], v_ref[:, sl, :], b_ref[:, sl, :], g_ref[:, sl, :], g2,
                                   gl_ref[:, sl, :], s)
            o_ref[:, sl, :] = o
            inv_ref[:, cc] = inv
        state[...] = s

    hm = pl.BlockSpec((HEADS, rows, HEAD_DIM), lambda i: (0, i, 0))
    per_chunk = pl.BlockSpec((HEADS, cpb, GDN_CHUNK, HEAD_DIM), lambda i: (0, i, 0, 0))
    chunk_shape = jax.ShapeDtypeStruct((HEADS, t // GDN_CHUNK, GDN_CHUNK, HEAD_DIM), F32)
    return pl.pallas_call(
        body, name="gdn_fwd", grid=(t // rows,),
        in_specs=[hm] * 6 + [pl.BlockSpec((HEADS, cpb, 1, GDN_CHUNK), lambda i: (0, i, 0, 0))],
        out_specs=[hm, per_chunk, per_chunk],
        out_shape=[jax.ShapeDtypeStruct((HEADS, t, HEAD_DIM), F32), chunk_shape, chunk_shape],
        scratch_shapes=[pltpu.VMEM((HEADS, GDN_CHUNK, HEAD_DIM), F32)],
        compiler_params=_params(("arbitrary",)),
    )(q, k, v, beta, g, gl, grow)


def _gdn_bwd(q, k, v, beta, g, gl, grow, states, invs, do, cpb=1):
    t = q.shape[1]
    rows = cpb * GDN_CHUNK
    nsteps = t // rows

    def body(q_ref, k_ref, v_ref, b_ref, g_ref, gl_ref, grow_ref, st_ref, inv_ref, do_ref,
             dq_ref, dk_ref, dv_ref, db_ref, dg_ref, dgl_ref, dgrow_ref, dstate):
        @pl.when(pl.program_id(0) == 0)
        def _():
            dstate[...] = jnp.zeros_like(dstate)

        ds = dstate[...]
        for cc in reversed(range(cpb)):
            sl = slice(cc * GDN_CHUNK, (cc + 1) * GDN_CHUNK)
            g2 = jnp.broadcast_to(grow_ref[:, cc], (HEADS, GDN_CHUNK, GDN_CHUNK))
            _, vjp = jax.vjp(_gdn_chunk, q_ref[:, sl, :], k_ref[:, sl, :], v_ref[:, sl, :], b_ref[:, sl, :],
                             g_ref[:, sl, :], g2, gl_ref[:, sl, :], st_ref[:, cc], inv_ref[:, cc])
            gq, gk, gv, gb, gg1, gg2, ggl, ds, _ = vjp((do_ref[:, sl, :], ds))
            dq_ref[:, sl, :] = gq
            dk_ref[:, sl, :] = gk
            dv_ref[:, sl, :] = gv
            db_ref[:, sl, :] = gb
            dg_ref[:, sl, :] = gg1
            dgl_ref[:, sl, :] = ggl
            dgrow_ref[:, cc] = jnp.sum(gg2, axis=1, keepdims=True)
        dstate[...] = ds

    hm = pl.BlockSpec((HEADS, rows, HEAD_DIM), lambda i: (0, nsteps - 1 - i, 0))
    rowspec = pl.BlockSpec((HEADS, cpb, 1, GDN_CHUNK), lambda i: (0, nsteps - 1 - i, 0, 0))
    per_chunk = pl.BlockSpec((HEADS, cpb, GDN_CHUNK, HEAD_DIM), lambda i: (0, nsteps - 1 - i, 0, 0))
    hm_shape = jax.ShapeDtypeStruct((HEADS, t, HEAD_DIM), F32)
    return pl.pallas_call(
        body, name="gdn_bwd", grid=(nsteps,),
        in_specs=[hm] * 6 + [rowspec, per_chunk, per_chunk, hm],
        out_specs=[hm] * 6 + [rowspec],
        out_shape=[hm_shape] * 6 + [jax.ShapeDtypeStruct((HEADS, t // GDN_CHUNK, 1, GDN_CHUNK), F32)],
        scratch_shapes=[pltpu.VMEM((HEADS, GDN_CHUNK, HEAD_DIM), F32)],
        compiler_params=_params(("arbitrary",)),
    )(q, k, v, beta, g, gl, grow, states, invs, do)


def _gdn_out(o_hm, gdn_norm_w, proj, tm=256):
    t = proj.shape[0]

    def body(o_ref, w_ref, z_ref, oz_ref):
        z = z_ref[...]
        gate = z * _sigmoid(z)
        w = w_ref[...]
        for h in range(HEADS):
            sl = slice(h * HEAD_DIM, (h + 1) * HEAD_DIM)
            o = o_ref[h]
            r = lax.rsqrt(jnp.mean(o * o, axis=1, keepdims=True) + NORM_EPS)
            oz_ref[:, sl] = (o * r * w * gate[:, sl]).astype(oz_ref.dtype)

    tok = pl.BlockSpec((tm, WIDTH), lambda i: (i, 0))
    return pl.pallas_call(
        body, name="gdn_out", grid=(t // tm,),
        in_specs=[pl.BlockSpec((HEADS, tm, HEAD_DIM), lambda i: (0, i, 0)), pl.BlockSpec((1, HEAD_DIM), lambda i: (0, 0)),
                  pl.BlockSpec((tm, WIDTH), lambda i: (i, SEG_ZB // WIDTH))],
        out_specs=tok, out_shape=jax.ShapeDtypeStruct((t, WIDTH), MXU_DTYPE),
        compiler_params=_params(("parallel",)),
    )(o_hm, gdn_norm_w, proj)


def _gdn_out_bwd(dproj, d_oz, o_hm, gdn_norm_w, proj, tm=256):
    t = proj.shape[0]

    def body(dp_ref, doz_ref, o_ref, w_ref, z_ref, dz_ref, do_ref, dw_ref):
        z = z_ref[...]
        sg = _sigmoid(z)
        gate = z * sg
        dgate = sg * (1.0 + z * (1.0 - sg))
        w = w_ref[...]
        dw = jnp.zeros((1, HEAD_DIM), F32)
        for h in range(HEADS):
            sl = slice(h * HEAD_DIM, (h + 1) * HEAD_DIM)
            o = o_ref[h]
            g = doz_ref[:, sl]
            r = lax.rsqrt(jnp.mean(o * o, axis=1, keepdims=True) + NORM_EPS)
            on = o * r * w
            dz_ref[:, sl] = (g * on * dgate[:, sl]).astype(dz_ref.dtype)
            dn = g * gate[:, sl]
            dw += jnp.sum(dn * o * r, axis=0, keepdims=True)
            dnw = dn * w
            do_ref[h] = r * dnw - o * (r * r * r) * jnp.mean(dnw * o, axis=1, keepdims=True)

        @pl.when(pl.program_id(0) == 0)
        def _():
            dw_ref[...] = jnp.zeros_like(dw_ref)

        dw_ref[...] += jnp.concatenate([dw, jnp.zeros((7, HEAD_DIM), F32)], axis=0)

    tok = pl.BlockSpec((tm, WIDTH), lambda i: (i, 0))
    seg = pl.BlockSpec((tm, WIDTH), lambda i: (i, SEG_ZB // WIDTH))
    hm = pl.BlockSpec((HEADS, tm, HEAD_DIM), lambda i: (0, i, 0))
    return pl.pallas_call(
        body, name="gdn_out_bwd", grid=(t // tm,),
        in_specs=[pl.BlockSpec(memory_space=pl.ANY), tok, hm, pl.BlockSpec((1, HEAD_DIM), lambda i: (0, 0)), seg],
        out_specs=[seg, hm, pl.BlockSpec((8, HEAD_DIM), lambda i: (0, 0))],
        out_shape=[jax.ShapeDtypeStruct((t, PACKED_WIDTH), MXU_DTYPE), jax.ShapeDtypeStruct((HEADS, t, HEAD_DIM), F32),
                   jax.ShapeDtypeStruct((8, HEAD_DIM), F32)],
        input_output_aliases={0: 0},
        compiler_params=_params(("arbitrary",)),
    )(dproj, d_oz, o_hm, gdn_norm_w, proj)


def _merge(y_a, y_b, proj, tm=256):
    t = proj.shape[0]

    def body(ya_ref, yb_ref, ga_ref, gb_ref, m_ref):
        m_ref[...] = (_sigmoid(ga_ref[...]) * ya_ref[...] + _sigmoid(gb_ref[...]) * yb_ref[...]).astype(m_ref.dtype)

    half = pl.BlockSpec((tm, WIDTH), lambda i, c: (i, c))
    return pl.pallas_call(
        body, name="merge", grid=(t // tm, 2),
        in_specs=[half, half, pl.BlockSpec((tm, WIDTH), lambda i, c: (i, SEG_GA // WIDTH + c)),
                  pl.BlockSpec((tm, WIDTH), lambda i, c: (i, SEG_GB // WIDTH + c))],
        out_specs=half, out_shape=jax.ShapeDtypeStruct((t, D_MODEL), MXU_DTYPE),
        compiler_params=_params(("parallel", "parallel")),
    )(y_a, y_b, proj, proj)


def _merge_bwd(dproj, d_m, y, proj, seg, name, tm=256):
    t = proj.shape[0]

    def body(*refs):
        dm_ref, y_ref, g_ref, dg_ref, dy_ref = refs[-5:]
        dm = dm_ref[...]
        s = _sigmoid(g_ref[...])
        dy_ref[...] = (dm * s).astype(dy_ref.dtype)
        dg_ref[...] = (dm * y_ref[...] * s * (1.0 - s)).astype(dg_ref.dtype)

    half = pl.BlockSpec((tm, WIDTH), lambda i, c: (i, c))
    gate = pl.BlockSpec((tm, WIDTH), lambda i, c: (i, seg // WIDTH + c))
    specs, args, aliases = [half, half, gate], [d_m, y, proj], {}
    if dproj is not None:
        specs, args, aliases = [pl.BlockSpec(memory_space=pl.ANY)] + specs, [dproj] + args, {0: 0}
    return pl.pallas_call(
        body, name=name, grid=(t // tm, 2), in_specs=specs, out_specs=[gate, half],
        out_shape=[jax.ShapeDtypeStruct((t, PACKED_WIDTH), MXU_DTYPE), jax.ShapeDtypeStruct((t, D_MODEL), MXU_DTYPE)],
        input_output_aliases=aliases,
        compiler_params=_params(("parallel", "parallel")),
    )(*args)


def _tail(x, mo, final_w, target, tm=256):
    t = x.shape[0]

    def body(x_ref, mo_ref, w_ref, t_ref, dxm_ref, dx_ref, loss_ref, dw_ref):
        x2 = x_ref[...] + mo_ref[...]
        w = w_ref[...]
        r = lax.rsqrt(jnp.mean(x2 * x2, axis=-1, keepdims=True) + NORM_EPS)
        xn = x2 * r
        err = xn * w - t_ref[...]
        dy = err * (1.0 / D_MODEL)
        dyw = dy * w
        dx2 = r * dyw - x2 * (r * r * r) * jnp.mean(dyw * x2, axis=-1, keepdims=True)
        dx_ref[...] = dx2
        dxm_ref[...] = dx2.astype(dxm_ref.dtype)
        loss = 0.5 * jnp.sum(jnp.sum(err * err, axis=-1, keepdims=True) * (1.0 / D_MODEL), axis=0, keepdims=True)
        onehot = jnp.where((_iota((8, 128), 0) == 0) & (_iota((8, 128), 1) == 0), 1.0, 0.0)

        @pl.when(pl.program_id(0) == 0)
        def _():
            loss_ref[...] = jnp.zeros_like(loss_ref)
            dw_ref[...] = jnp.zeros_like(dw_ref)

        loss_ref[...] += loss * onehot
        dw_ref[...] += jnp.where(_iota((8, D_MODEL), 0) == 0, jnp.sum(dy * xn, axis=0, keepdims=True), 0.0)

    tok = pl.BlockSpec((tm, D_MODEL), lambda i: (i, 0))
    return pl.pallas_call(
        body, name="tail", grid=(t // tm,),
        in_specs=[tok, tok, pl.BlockSpec((1, D_MODEL), lambda i: (0, 0)), tok],
        out_specs=[tok, tok, pl.BlockSpec((8, 128), lambda i: (0, 0)), pl.BlockSpec((8, D_MODEL), lambda i: (0, 0))],
        out_shape=[jax.ShapeDtypeStruct((t, D_MODEL), MXU_DTYPE), jax.ShapeDtypeStruct((t, D_MODEL), F32),
                   jax.ShapeDtypeStruct((8, 128), F32), jax.ShapeDtypeStruct((8, D_MODEL), F32)],
        compiler_params=_params(("arbitrary",)),
    )(x, mo, final_w, target)


def _norm_bwd(x, norm_w, dh, dx2, tm=256):
    t = x.shape[0]

    def body(x_ref, w_ref, dh_ref, dx2_ref, dx_ref, dw_ref):
        xf, w, dh_ = x_ref[...], w_ref[...], dh_ref[...]
        r = lax.rsqrt(jnp.mean(xf * xf, axis=-1, keepdims=True) + NORM_EPS)
        dhw = dh_ * w
        dx_ref[...] = dx2_ref[...] + r * dhw - xf * (r * r * r) * jnp.mean(dhw * xf, axis=-1, keepdims=True)

        @pl.when(pl.program_id(0) == 0)
        def _():
            dw_ref[...] = jnp.zeros_like(dw_ref)

        dw_ref[...] += jnp.where(_iota((8, D_MODEL), 0) == 0, jnp.sum(dh_ * xf * r, axis=0, keepdims=True), 0.0)

    tok = pl.BlockSpec((tm, D_MODEL), lambda i: (i, 0))
    return pl.pallas_call(
        body, name="norm_bwd", grid=(t // tm,),
        in_specs=[tok, pl.BlockSpec((1, D_MODEL), lambda i: (0, 0)), tok, tok],
        out_specs=[tok, pl.BlockSpec((8, D_MODEL), lambda i: (0, 0))],
        out_shape=[jax.ShapeDtypeStruct((t, D_MODEL), F32), jax.ShapeDtypeStruct((8, D_MODEL), F32)],
        compiler_params=_params(("arbitrary",)),
    )(x, norm_w, dh, dx2)


def _local_step(x, target, norm_w, wp, conv_w, a_log, dt_bias, gdn_norm_w, w_up_a, w_up_b, w_out, final_w):
    t = x.shape[0]
    tables = _rope_tables(t)
    a_log = jnp.pad(a_log, ((0, 0), (HEADS, 128 - 2 * HEADS)))
    dt_bias = jnp.pad(dt_bias, ((0, 0), (HEADS, 128 - 2 * HEADS)))

    proj, h = _norm_proj(x, norm_w, wp)
    qkvs = _rope_fwd(proj, tables)
    outs, lses = zip(*[_att_fwd(qkvs[gi], d, f"att_fwd{gi}") for gi, d in enumerate(DILATIONS)])
    oz_a, o_a, *lse_views = _att_merge(outs, lses, proj)
    conv = _conv_fwd(proj, conv_w)
    gq, gk, gv, gb, gg, ggl, grow = _gdn_prep(conv, proj, a_log, dt_bias)
    o_b, states, invs = _gdn_fwd(gq, gk, gv, gb, gg, ggl, grow)
    oz_b = _gdn_out(o_b, gdn_norm_w, proj)
    big = dict(tm=1024, tn=1024, tk=1024)
    y_a = _matmul(oz_a, w_up_a, "nn", "up_a", **big)
    y_b = _matmul(oz_b, w_up_b, "nn", "up_b", **big)
    merged = _merge(y_a, y_b, proj)
    mo = _matmul(merged, w_out, "nn", "out_proj", **big)
    dx2_m, dx2, loss_blk, d_final = _tail(x, mo, final_w, target)

    d_wout = _matmul(merged, dx2_m, "tn", "d_w_out", **big)
    d_m = _matmul(dx2_m, w_out, "nt", "d_merged", **big)
    dproj, dy_a = _merge_bwd(None, d_m, y_a, proj, SEG_GA, "merge_bwd_a")
    dproj, dy_b = _merge_bwd(dproj, d_m, y_b, proj, SEG_GB, "merge_bwd_b")
    d_wua = _matmul(oz_a, dy_a, "tn", "d_w_up_a", **big)
    d_wub = _matmul(oz_b, dy_b, "tn", "d_w_up_b", **big)
    d_oz_a = _matmul(dy_a, w_up_a, "nt", "d_oz_a", **big)
    d_oz_b = _matmul(dy_b, w_up_b, "nt", "d_oz_b", **big)
    dproj, *views = _att_merge_bwd(dproj, d_oz_a, o_a, proj)
    do_views, delta_views = views[:3], views[3:]
    dqkvs = [_att_bwd(qkvs[gi], do_views[gi], lse_views[gi], delta_views[gi], d, f"att_bwd{gi}")
             for gi, d in enumerate(DILATIONS)]
    dproj = _rope_bwd(dproj, dqkvs, tables)
    dproj, do_b, d_gnw = _gdn_out_bwd(dproj, d_oz_b, o_b, gdn_norm_w, proj)
    dgq, dgk, dgv, dgb, dgg, dggl, dgrow = _gdn_bwd(gq, gk, gv, gb, gg, ggl, grow, states, invs, do_b)
    dproj, dconv, d_small = _gdn_prep_bwd(dproj, conv, proj, a_log, dt_bias, dgq, dgk, dgv, dgb, dgg, dggl, dgrow)
    dproj, d_convw = _conv_bwd(dproj, dconv, proj, conv_w)
    d_wp = _matmul(h, dproj, "tn", "d_w_in", tm=1024, tn=512, tk=1024)
    dh = _matmul(dproj, wp, "nt", "d_h", tm=1024, tn=1024, tk=512)
    grad_x, d_norm = _norm_bwd(x, norm_w, dh, dx2)
    return dict(loss=loss_blk, grad_x=grad_x, norm_w=d_norm[0:1], w_in=d_wp, conv_w=d_convw[0:GDN_CONV],
                a_log=d_small[0:1, HEADS:2 * HEADS], dt_bias=d_small[1:2, HEADS:2 * HEADS], gdn_norm_w=d_gnw[0:1],
                w_up_a=d_wua, w_up_b=d_wub,
                w_out=d_wout, final_norm_w=d_final[0:1])


SHARDS = 4
W_IN_SHARD = IN_WIDTH // SHARDS
ROWS_UP = WIDTH * (D_MODEL // SHARDS) // 128
ROWS_OUT = (D_MODEL // SHARDS) * D_MODEL // 128
CONV_SHARD = 3 * WIDTH // SHARDS
ROWS_CONV = 16
SLAB_ROWS = 2 * ROWS_UP + ROWS_OUT + 2 * ROWS_CONV
HALF_ROWS = SLAB_ROWS // 2
BIG_HALF = (D_MODEL // 2, W_IN_SHARD)
SMALL_HALF = (HALF_ROWS, 128)
MESH = pl.DeviceIdType.MESH
ANY = pl.BlockSpec(memory_space=pl.ANY)


def _pad_rows(a, rows):
    return jnp.pad(a, ((0, rows - a.shape[0]), (0, 0)))


def _pack_slab(w_up_a, w_up_b, w_out, conv, conv_lo):
    parts = [w_up_a.reshape(ROWS_UP, 128), w_up_b.reshape(ROWS_UP, 128), w_out.reshape(ROWS_OUT, 128),
             _pad_rows(conv.reshape(-1, 128), ROWS_CONV), _pad_rows(conv_lo.reshape(-1, 128), ROWS_CONV)]
    return jnp.concatenate(parts, axis=0).reshape(2, *SMALL_HALF)


def _unpack_slab(slab):
    slab = slab.reshape(SLAB_ROWS, 128)
    r0 = 0
    out = []
    for rows, shape in ((ROWS_UP, (WIDTH, D_MODEL // SHARDS)), (ROWS_UP, (WIDTH, D_MODEL // SHARDS)),
                        (ROWS_OUT, (D_MODEL // SHARDS, D_MODEL)), (ROWS_CONV, None), (ROWS_CONV, None)):
        part = slab[r0:r0 + rows]
        out.append(part[:GDN_CONV * CONV_SHARD // 128].reshape(GDN_CONV, CONV_SHARD) if shape is None else part.reshape(shape))
        r0 += rows
    return out


def _mesh_position():
    x, y, c = lax.axis_index("x"), lax.axis_index("y"), lax.axis_index("c")
    return x, y, c, [(1 - x, y), (x, 1 - y), (1 - x, 1 - y)]


def _gather_weights(shards):
    n = len(shards)

    def body(*refs):
        in_refs, out_refs, (send_sems, recv_sems) = refs[:n], refs[n:2 * n], refs[2 * n:]
        x, y, c, chips = _mesh_position()

        def half(a, chip, which):
            return out_refs[a].at[2 * chip[0] + chip[1], which]

        def copy(k, src, dst, to):
            return pltpu.make_async_remote_copy(src_ref=src, dst_ref=dst, send_sem=send_sems.at[k], recv_sem=recv_sems.at[k],
                                                device_id=to, device_id_type=MESH)

        pairs = [(a, j, chip) for a in range(n) for j, chip in enumerate(chips)]
        first = [copy(6 * a + j, in_refs[a].at[c], half(a, (x, y), c), (*chip, c)) for a, j, chip in pairs]
        for cp in first:
            cp.start()
        passed = [copy(6 * a + 3 + j, half(a, chip, c), half(a, chip, c), (x, y, 1 - c)) for a, j, chip in pairs]
        for i, (a, j, chip) in enumerate(pairs):
            copy(6 * a + j, half(a, chip, c), half(a, chip, c), (x, y, c)).wait_recv()
            passed[i].start()
        for a, j, chip in pairs:
            copy(6 * a + 3 + j, half(a, chip, 1 - c), half(a, chip, 1 - c), (x, y, c)).wait_recv()
        for cp in first + passed:
            cp.wait_send()

    return pl.pallas_call(
        body, name="gather_weights", in_specs=[ANY] * n, out_specs=[ANY] * n,
        out_shape=[jax.ShapeDtypeStruct((SHARDS, *s.shape), s.dtype) for s in shards],
        scratch_shapes=[pltpu.SemaphoreType.DMA((6 * n,)), pltpu.SemaphoreType.DMA((6 * n,))],
    )(*shards)


def _exchange_halves(grads):
    n = len(grads)

    def body(*refs):
        g_refs, out_refs, (send_sems, recv_sems) = refs[:n], refs[n:2 * n], refs[2 * n:]
        x, y, c, _ = _mesh_position()
        copies = [pltpu.make_async_remote_copy(src_ref=g_refs[a].at[s, 1 - c], dst_ref=out_refs[a].at[s],
                                               send_sem=send_sems.at[SHARDS * a + s], recv_sem=recv_sems.at[SHARDS * a + s],
                                               device_id=(x, y, 1 - c), device_id_type=MESH)
                  for a in range(n) for s in range(SHARDS)]
        for cp in copies:
            cp.start()
        for cp in copies:
            cp.wait()

    return pl.pallas_call(
        body, name="exchange_halves", in_specs=[ANY] * n, out_specs=[ANY] * n,
        out_shape=[jax.ShapeDtypeStruct((SHARDS, *g.shape[2:]), F32) for g in grads],
        scratch_shapes=[pltpu.SemaphoreType.DMA((SHARDS * n,)), pltpu.SemaphoreType.DMA((SHARDS * n,))],
    )(*grads)


def _pair_sum(grads, recv, blk, name):
    _, _, rows, cols = grads.shape

    def body(c_ref, g_ref, r_ref, o_ref):
        o_ref[...] = (g_ref[0] + r_ref[...]).astype(o_ref.dtype)

    spec = pl.BlockSpec((1, blk, cols), lambda s, i, c_ref: (s, i, 0))
    return pl.pallas_call(
        body, name=name,
        grid_spec=pltpu.PrefetchScalarGridSpec(
            num_scalar_prefetch=1, grid=(SHARDS, rows // blk),
            in_specs=[pl.BlockSpec((1, 1, blk, cols), lambda s, i, c_ref: (s, c_ref[0], i, 0)), spec],
            out_specs=spec),
        out_shape=jax.ShapeDtypeStruct((SHARDS, rows, cols), MXU_DTYPE),
        compiler_params=_params(("parallel", "parallel")),
    )(lax.axis_index("c").astype(jnp.int32).reshape(1), grads, recv)


def _scatter_pairs(pairs):
    n = len(pairs)

    def body(*refs):
        p_refs, out_refs, (send_sems, recv_sems) = refs[:n], refs[n:2 * n], refs[2 * n:]
        x, y, c, chips = _mesh_position()
        copies = [pltpu.make_async_remote_copy(src_ref=p_refs[a].at[2 * chip[0] + chip[1]], dst_ref=out_refs[a].at[j],
                                               send_sem=send_sems.at[3 * a + j], recv_sem=recv_sems.at[3 * a + j],
                                               device_id=(*chip, c), device_id_type=MESH)
                  for a in range(n) for j, chip in enumerate(chips)]
        for cp in copies:
            cp.start()
        for cp in copies:
            cp.wait()

    return pl.pallas_call(
        body, name="scatter_pairs", in_specs=[ANY] * n, out_specs=[ANY] * n,
        out_shape=[jax.ShapeDtypeStruct((3, *p.shape[1:]), p.dtype) for p in pairs],
        scratch_shapes=[pltpu.SemaphoreType.DMA((3 * n,)), pltpu.SemaphoreType.DMA((3 * n,))],
    )(*pairs)


def _chip_sum(pairs, recv, blk, name):
    _, rows, cols = pairs.shape

    def body(pos_ref, p_ref, r_ref, o_ref):
        o_ref[0] = ((p_ref[0].astype(F32) + r_ref[0].astype(F32)) + r_ref[1].astype(F32)) + r_ref[2].astype(F32)

    pos = jnp.stack([2 * lax.axis_index("x") + lax.axis_index("y"), lax.axis_index("c")]).astype(jnp.int32)
    return pl.pallas_call(
        body, name=name,
        grid_spec=pltpu.PrefetchScalarGridSpec(
            num_scalar_prefetch=1, grid=(rows // blk,),
            in_specs=[pl.BlockSpec((1, blk, cols), lambda i, pos_ref: (pos_ref[0], i, 0)),
                      pl.BlockSpec((3, blk, cols), lambda i, pos_ref: (0, i, 0))],
            out_specs=pl.BlockSpec((1, blk, cols), lambda i, pos_ref: (pos_ref[1], i, 0))),
        out_shape=jax.ShapeDtypeStruct((2, rows, cols), F32),
        compiler_params=_params(("parallel",)),
    )(pos, pairs, recv)


def _share_total(totals):
    n = len(totals)

    def body(*refs):
        t_refs, out_refs, (send_sems, recv_sems) = refs[:n], refs[n:2 * n], refs[2 * n:]
        x, y, c, _ = _mesh_position()
        copies = [pltpu.make_async_remote_copy(src_ref=t_refs[a].at[c], dst_ref=out_refs[a].at[c], send_sem=send_sems.at[a],
                                               recv_sem=recv_sems.at[a], device_id=(x, y, 1 - c), device_id_type=MESH)
                  for a in range(n)]
        for cp in copies:
            cp.start()
        for a in range(n):
            other = out_refs[a].at[1 - c]
            pltpu.make_async_remote_copy(src_ref=other, dst_ref=other, send_sem=send_sems.at[a], recv_sem=recv_sems.at[a],
                                         device_id=(x, y, c), device_id_type=MESH).wait_recv()
        for cp in copies:
            cp.wait_send()

    return pl.pallas_call(
        body, name="share_total", in_specs=[ANY] * n, out_specs=[ANY] * n,
        out_shape=[jax.ShapeDtypeStruct(t.shape, F32) for t in totals],
        scratch_shapes=[pltpu.SemaphoreType.DMA((n,)), pltpu.SemaphoreType.DMA((n,))],
        input_output_aliases={a: a for a in range(n)},
    )(*totals)


def _allreduce_small(block):
    def body(b_ref, out_ref, gath, send_sems, recv_sems):
        x, y, c, _ = _mesh_position()
        me = 4 * x + 2 * y + c
        gath[me] = b_ref[...]
        copies = []
        for k in range(1, 8):
            peer = (x ^ (k >> 2), y ^ ((k >> 1) & 1), c ^ (k & 1))
            copies.append(pltpu.make_async_remote_copy(src_ref=b_ref, dst_ref=gath.at[me], send_sem=send_sems.at[k - 1],
                                                       recv_sem=recv_sems.at[k - 1], device_id=peer, device_id_type=MESH))
        for cp in copies:
            cp.start()
        for k in range(1, 8):
            src = 4 * (x ^ (k >> 2)) + 2 * (y ^ ((k >> 1) & 1)) + (c ^ (k & 1))
            pltpu.make_async_remote_copy(src_ref=b_ref, dst_ref=gath.at[src], send_sem=send_sems.at[k - 1],
                                         recv_sem=recv_sems.at[k - 1], device_id=(x, y, c), device_id_type=MESH).wait_recv()
        for cp in copies:
            cp.wait_send()
        acc = gath[0]
        for d in range(1, 8):
            acc = acc + gath[d]
        out_ref[...] = acc

    vm = pl.BlockSpec(memory_space=pltpu.VMEM)
    return pl.pallas_call(
        body, name="allreduce_small", in_specs=[vm], out_specs=vm,
        out_shape=jax.ShapeDtypeStruct((8, D_MODEL), F32),
        scratch_shapes=[pltpu.VMEM((8, 8, D_MODEL), F32), pltpu.SemaphoreType.DMA((7,)), pltpu.SemaphoreType.DMA((7,))],
    )(block)


def _adamw(w, g, m, v, name):
    rows, cols = w.shape
    tr = 128 if rows % 128 == 0 else rows

    def body(w_ref, g_ref, m_ref, v_ref, d_ref, nm_ref, nv_ref):
        gv = g_ref[...]
        nm = ADAM_B1 * m_ref[...] + (1.0 - ADAM_B1) * gv
        nv = ADAM_B2 * v_ref[...] + (1.0 - ADAM_B2) * (gv * gv)
        m_hat = nm / (1.0 - ADAM_B1 ** ADAM_STEP)
        v_hat = nv / (1.0 - ADAM_B2 ** ADAM_STEP)
        d_ref[...] = -ADAM_LR * (m_hat / (jnp.sqrt(v_hat) + ADAM_EPS) + ADAM_WD * w_ref[...])
        nm_ref[...] = nm
        nv_ref[...] = nv

    spec = pl.BlockSpec((tr, cols), lambda i: (i, 0))
    shape = jax.ShapeDtypeStruct((rows, cols), F32)
    return pl.pallas_call(
        body, name=name, grid=(rows // tr,), in_specs=[spec] * 4, out_specs=[spec] * 3, out_shape=[shape] * 3,
        compiler_params=_params(("parallel",)),
    )(w, g, m, v)


def kernel(x, norm_w, w_in, conv_w, a_log, dt_bias, gdn_norm_w, w_up_a, w_up_b, w_out, final_norm_w, loss_target, m_norm_w, m_w_in, m_conv_w, m_a_log, m_dt_bias, m_gdn_norm_w, m_w_up_a, m_w_up_b, m_w_out, m_final_norm_w, v_norm_w, v_w_in, v_conv_w, v_a_log, v_dt_bias, v_gdn_norm_w, v_w_up_a, v_w_up_b, v_w_out, v_final_norm_w):
    conv_hi = conv_w[0].astype(MXU_DTYPE)
    conv_lo = (conv_w[0] - conv_hi.astype(F32)).astype(MXU_DTYPE)
    big = w_in[0].astype(MXU_DTYPE).reshape(2, *BIG_HALF)
    slab = _pack_slab(w_up_a[0].astype(MXU_DTYPE), w_up_b[0].astype(MXU_DTYPE), w_out[0].astype(MXU_DTYPE), conv_hi, conv_lo)
    own_shard = 2 * lax.axis_index("x") + lax.axis_index("y")
    bigs, slabs = _gather_weights([big, slab])
    bigs = lax.dynamic_update_slice(bigs, big[None], (own_shard, 0, 0, 0)).reshape(SHARDS, D_MODEL, W_IN_SHARD)
    slabs = lax.dynamic_update_slice(slabs, slab[None], (own_shard, 0, 0, 0))
    parts = [_unpack_slab(slabs[s]) for s in range(SHARDS)]
    split = BA_END - (SHARDS - 1) * W_IN_SHARD
    wp = jnp.concatenate([bigs[s] for s in range(SHARDS - 1)]
                         + [bigs[-1][:, :split], jnp.zeros((D_MODEL, SEG_GA - BA_END), MXU_DTYPE), bigs[-1][:, split:]], axis=1)
    w_up_a_full = jnp.concatenate([p[0] for p in parts], axis=1)
    w_up_b_full = jnp.concatenate([p[1] for p in parts], axis=1)
    w_out_full = jnp.concatenate([p[2] for p in parts], axis=0)
    conv_full = jnp.concatenate([p[3].astype(F32) + p[4].astype(F32) for p in parts], axis=1)

    g = _local_step(x[0], loss_target[0], norm_w, wp, conv_full, a_log, dt_bias, gdn_norm_w,
                    w_up_a_full, w_up_b_full, w_out_full, final_norm_w[None])

    d_wp = g["w_in"]
    d_w_in = [d_wp[:, s * W_IN_SHARD:(s + 1) * W_IN_SHARD] for s in range(SHARDS - 1)]
    d_w_in.append(jnp.concatenate([d_wp[:, (SHARDS - 1) * W_IN_SHARD:BA_END], d_wp[:, SEG_GA:]], axis=1))
    zero_conv = jnp.zeros((GDN_CONV, CONV_SHARD), F32)
    grads = [jnp.stack(d_w_in).reshape(SHARDS, 2, *BIG_HALF),
             jnp.stack([_pack_slab(g["w_up_a"][:, s * 256:(s + 1) * 256], g["w_up_b"][:, s * 256:(s + 1) * 256],
                                   g["w_out"][s * 256:(s + 1) * 256], g["conv_w"][:, s * CONV_SHARD:(s + 1) * CONV_SHARD],
                                   zero_conv) for s in range(SHARDS)])]
    blocks, tags = (128, HALF_ROWS), ("w_in", "slab")
    from_sibling = _exchange_halves(grads)
    pairs = [_pair_sum(gr, fs, blk, f"pair_sum_{tag}") for gr, fs, blk, tag in zip(grads, from_sibling, blocks, tags)]
    from_chips = _scatter_pairs(pairs)
    total_big, total_slab = _share_total([_chip_sum(p, fc, blk, f"chip_sum_{tag}")
                                          for p, fc, blk, tag in zip(pairs, from_chips, blocks, tags)])
    g_w_in = total_big.reshape(D_MODEL, W_IN_SHARD)
    g_w_up_a, g_w_up_b, g_w_out, g_conv, _ = _unpack_slab(total_slab)

    row2 = jnp.concatenate([g["gdn_norm_w"], g["a_log"], g["dt_bias"], g["loss"][0:1, 0:1],
                            jnp.zeros((1, D_MODEL - HEAD_DIM - 2 * HEADS - 1), F32)], axis=1)
    small = _allreduce_small(jnp.concatenate([g["norm_w"], g["final_norm_w"], row2, jnp.zeros((5, D_MODEL), F32)], axis=0))
    g_norm, g_final = small[0:1], small[1]
    g_gnw, g_alog, g_dt = small[2:3, 0:HEAD_DIM], small[2:3, HEAD_DIM:HEAD_DIM + HEADS], small[2:3, HEAD_DIM + HEADS:HEAD_DIM + 2 * HEADS]
    loss = small[2, HEAD_DIM + 2 * HEADS]

    names = ["norm_w", "w_in", "conv_w", "a_log", "dt_bias", "gdn_norm_w", "w_up_a", "w_up_b", "w_out", "final_norm_w"]
    weights = dict(zip(names, (norm_w, w_in, conv_w, a_log, dt_bias, gdn_norm_w, w_up_a, w_up_b, w_out, final_norm_w)))
    ms = dict(zip(names, (m_norm_w, m_w_in, m_conv_w, m_a_log, m_dt_bias, m_gdn_norm_w, m_w_up_a, m_w_up_b, m_w_out, m_final_norm_w)))
    vs = dict(zip(names, (v_norm_w, v_w_in, v_conv_w, v_a_log, v_dt_bias, v_gdn_norm_w, v_w_up_a, v_w_up_b, v_w_out, v_final_norm_w)))
    grads2d = dict(norm_w=g_norm, w_in=g_w_in, conv_w=g_conv, a_log=g_alog, dt_bias=g_dt, gdn_norm_w=g_gnw,
                   w_up_a=g_w_up_a, w_up_b=g_w_up_b, w_out=g_w_out, final_norm_w=g_final[None])
    grad_out, delta, new_m, new_v = [], [], [], []
    for n in names:
        shape = weights[n].shape
        two_d = grads2d[n].shape
        d, nm, nv = _adamw(weights[n].reshape(two_d), grads2d[n], ms[n].reshape(two_d), vs[n].reshape(two_d), f"adamw_{n}")
        grad_out.append(grads2d[n].reshape(shape))
        delta.append(d.reshape(shape))
        new_m.append(nm.reshape(shape))
        new_v.append(nv.reshape(shape))
    return (loss, g["grad_x"][None], *grad_out, *delta, *new_m, *new_v)
```

```python
import functools

import jax
import jax.numpy as jnp
from jax import lax
from jax.experimental import pallas as pl
from jax.experimental.pallas import tpu as pltpu

F32 = jnp.float32
MXU_DTYPE = jnp.bfloat16
HIGHEST = lax.Precision.HIGHEST

D_MODEL = 1024
HEADS = 8
HEAD_DIM = 64
WIDTH = HEADS * HEAD_DIM
NORM_EPS = 1e-6
ROPE_THETA = 10000.0
ATT_BLOCK = 128
DILATIONS = (1, 4, 16)
GDN_CHUNK = 64
GDN_CONV = 4
IN_WIDTH = 9232
SEG_A, SEG_ZA, SEG_B, SEG_ZB, SEG_BA, SEG_GA, SEG_GB, PACKED_WIDTH = 0, 4608, 5120, 6656, 7168, 7680, 8704, 9728
BA_END = 7184
VMEM_LIMIT = 56 * 1024 * 1024

ADAM_LR, ADAM_B1, ADAM_B2, ADAM_EPS, ADAM_WD, ADAM_STEP = 0.001, 0.9, 0.999, 1e-08, 0.01, 10

_NN = (((1,), (0,)), ((), ()))
_NT = (((1,), (1,)), ((), ()))
_TN = (((0,), (0,)), ((), ()))


def _params(sem):
    return pltpu.CompilerParams(dimension_semantics=sem, vmem_limit_bytes=VMEM_LIMIT)


def _mxu(a, b, dims):
    return lax.dot_general(a.astype(MXU_DTYPE), b.astype(MXU_DTYPE), dims, preferred_element_type=F32)


def _sigmoid(x):
    return 1.0 / (1.0 + jnp.exp(-x))


def _softplus(x):
    return jnp.maximum(x, 0.0) + jnp.log(1.0 + jnp.exp(-jnp.abs(x)))


def _iota(shape, axis):
    return lax.broadcasted_iota(jnp.int32, shape, axis)


def _matmul(a, b, mode, name, out_dtype=F32, tm=512, tn=512, tk=512):
    if mode == "nn":
        (m, k), (k2, n) = a.shape, b.shape
    elif mode == "nt":
        (m, k), (n, k2) = a.shape, b.shape
    else:
        (k, m), (k2, n) = a.shape, b.shape
    assert k == k2
    tm, tn, tk = min(tm, m), min(tn, n), min(tk, k)
    assert m % tm == 0 and n % tn == 0 and k % tk == 0
    nk = k // tk
    dims = {"nn": _NN, "nt": _NT, "tn": _TN}[mode]

    assert out_dtype == F32

    def body(a_ref, b_ref, o_ref):
        kk = pl.program_id(2)
        part = _mxu(a_ref[...], b_ref[...], dims)

        @pl.when(kk == 0)
        def _():
            o_ref[...] = part

        @pl.when(kk > 0)
        def _():
            o_ref[...] += part

    a_spec = pl.BlockSpec((tk, tm), lambda i, j, kk: (kk, i)) if mode == "tn" else pl.BlockSpec((tm, tk), lambda i, j, kk: (i, kk))
    b_spec = pl.BlockSpec((tn, tk), lambda i, j, kk: (j, kk)) if mode == "nt" else pl.BlockSpec((tk, tn), lambda i, j, kk: (kk, j))
    return pl.pallas_call(
        body, name=name, grid=(m // tm, n // tn, nk), in_specs=[a_spec, b_spec],
        out_specs=pl.BlockSpec((tm, tn), lambda i, j, kk: (i, j)),
        out_shape=jax.ShapeDtypeStruct((m, n), out_dtype),
        compiler_params=_params(("parallel", "parallel", "arbitrary")),
    )(a, b)


def _norm_proj(x, norm_w, wp, tm=512, tn=PACKED_WIDTH // 4):
    t = x.shape[0]
    tm = min(tm, t)

    def body(x_ref, nw_ref, w_ref, proj_ref, ht_ref, h_scr):
        @pl.when(pl.program_id(1) == 0)
        def _():
            xf = x_ref[...]
            r = lax.rsqrt(jnp.mean(xf * xf, axis=-1, keepdims=True) + NORM_EPS)
            h = xf * r * nw_ref[...]
            h_scr[...] = h.astype(h_scr.dtype)
            ht_ref[...] = h.T.astype(ht_ref.dtype)

        proj_ref[...] = jnp.dot(h_scr[...], w_ref[...], preferred_element_type=F32)

    return pl.pallas_call(
        body, name="norm_proj", grid=(t // tm, PACKED_WIDTH // tn),
        in_specs=[pl.BlockSpec((tm, D_MODEL), lambda i, j: (i, 0)),
                  pl.BlockSpec((1, D_MODEL), lambda i, j: (0, 0)),
                  pl.BlockSpec((D_MODEL, tn), lambda i, j: (0, j))],
        out_specs=[pl.BlockSpec((tm, tn), lambda i, j: (i, j)),
                   pl.BlockSpec((D_MODEL, tm), lambda i, j: (0, i))],
        out_shape=[jax.ShapeDtypeStruct((t, PACKED_WIDTH), F32), jax.ShapeDtypeStruct((D_MODEL, t), MXU_DTYPE)],
        scratch_shapes=[pltpu.VMEM((tm, D_MODEL), MXU_DTYPE)],
        compiler_params=_params(("parallel", "arbitrary")),
    )(x, norm_w, wp)


def _rope_tables(t):
    lane = jnp.arange(128)
    inv_freq = ROPE_THETA ** (-jnp.arange(0, HEAD_DIM, 2, dtype=F32) / HEAD_DIM)
    ang = jnp.arange(t, dtype=F32)[:, None] * inv_freq[None, :]
    ang = jnp.concatenate([ang, ang, ang, ang], axis=-1)
    first_half = (lane % HEAD_DIM) < HEAD_DIM // 2
    cos, sin = jnp.cos(ang), jnp.sin(ang)
    return cos, jnp.where(first_half, -sin, 0.0), jnp.where(first_half, 0.0, sin)


def _rope_block(x, cos, sin_lo, sin_hi, sign):
    outs = []
    for c in range(8):
        xc = x[:, c * 128:(c + 1) * 128]
        rot = pltpu.roll(xc, 96, 1) * sin_lo + pltpu.roll(xc, 32, 1) * sin_hi
        outs.append(xc * cos + sign * rot)
    outs.append(x[:, 2 * WIDTH:])
    return jnp.concatenate(outs, axis=1)


def _tile_scratch(tm, cols):
    return pltpu.VMEM((cols // 128, tm, 128), F32)


def _store_tile(scr, y):
    for c in range(scr.shape[0]):
        scr[c] = y[:, c * 128:(c + 1) * 128]


def _load_tile(scr):
    return jnp.concatenate([scr[c] for c in range(scr.shape[0])], axis=1)


def _to_strided_view(scr, o_ref, d):
    n, tm, _ = scr.shape
    for r in range(d):
        for c in range(n):
            o_ref[:, (r * n + c) * 128:(r * n + c + 1) * 128] = scr[c, pl.ds(r, tm // d, stride=d), :].astype(o_ref.dtype)


def _from_strided_view(i_ref, scr, d):
    n, tm, _ = scr.shape
    for r in range(d):
        for c in range(n):
            scr[c, pl.ds(r, tm // d, stride=d), :] = i_ref[:, (r * n + c) * 128:(r * n + c + 1) * 128].astype(F32)


def _strided_spec(tm, d, cols):
    return pl.BlockSpec((tm // d, d * cols), lambda i: (i, 0))


def _rope_fwd(proj, tables, tm=256):
    t = proj.shape[0]
    cols = 3 * WIDTH

    def body(x_ref, c_ref, sl_ref, sh_ref, o0, o1, o2, scr):
        for g, (d, o_ref) in enumerate(zip(DILATIONS, (o0, o1, o2))):
            y = _rope_block(x_ref[:, g * cols:(g + 1) * cols], c_ref[...], sl_ref[...], sh_ref[...], 1.0)
            if d == 1:
                o_ref[...] = y.astype(o_ref.dtype)
            else:
                _store_tile(scr, y)
                _to_strided_view(scr, o_ref, d)

    tab = pl.BlockSpec((tm, 128), lambda i: (i, 0))
    return pl.pallas_call(
        body, name="rope_fwd", grid=(t // tm,),
        in_specs=[pl.BlockSpec((tm, 3 * cols), lambda i: (i, 0)), tab, tab, tab],
        out_specs=[_strided_spec(tm, d, cols) for d in DILATIONS],
        out_shape=[jax.ShapeDtypeStruct((t // d, d * cols), MXU_DTYPE) for d in DILATIONS],
        scratch_shapes=[_tile_scratch(tm, cols)],
        compiler_params=_params(("parallel",)),
    )(proj, *tables)


def _rope_bwd(dproj, dqkvs, tables, tm=256):
    t = dproj.shape[0]
    cols = 3 * WIDTH

    def body(dp_ref, i0, i1, i2, c_ref, sl_ref, sh_ref, o_ref, scr):
        for g, (d, i_ref) in enumerate(zip(DILATIONS, (i0, i1, i2))):
            if d == 1:
                x = i_ref[...]
            else:
                _from_strided_view(i_ref, scr, d)
                x = _load_tile(scr)
            y = _rope_block(x, c_ref[...], sl_ref[...], sh_ref[...], -1.0)
            o_ref[:, g * cols:(g + 1) * cols] = y.astype(o_ref.dtype)

    tab = pl.BlockSpec((tm, 128), lambda i: (i, 0))
    return pl.pallas_call(
        body, name="rope_bwd", grid=(t // tm,),
        in_specs=[pl.BlockSpec(memory_space=pl.ANY)] + [_strided_spec(tm, d, cols) for d in DILATIONS] + [tab, tab, tab],
        out_specs=pl.BlockSpec((tm, 3 * cols), lambda i: (i, 0)),
        out_shape=jax.ShapeDtypeStruct((t, PACKED_WIDTH), MXU_DTYPE),
        scratch_shapes=[_tile_scratch(tm, cols)],
        input_output_aliases={0: 0},
        compiler_params=_params(("parallel",)),
    )(dproj, *dqkvs, *tables)


def _att_masks():
    qi = _iota((ATT_BLOCK, ATT_BLOCK), 0)
    kj = _iota((ATT_BLOCK, ATT_BLOCK), 1)
    return kj <= qi, kj >= qi


def _att_fwd(qkv, d, name):
    rows = qkv.shape[0]
    nb = rows // ATT_BLOCK
    scale = HEAD_DIM ** -0.5

    def body(q_ref, kc_ref, kp_ref, vc_ref, vp_ref, o_ref, lse_ref):
        has_prev = pl.program_id(1) > 0
        m_cur, m_prev = _att_masks()
        m_prev = m_prev & has_prev
        hs = range(HEADS)
        sls = [slice(h * HEAD_DIM, (h + 1) * HEAD_DIM) for h in hs]
        qs = [q_ref[:, sl] for sl in sls]
        s_c = [jnp.where(m_cur, _mxu(qs[h], kc_ref[:, sls[h]], _NT) * scale, -jnp.inf) for h in hs]
        s_p = [jnp.where(m_prev, _mxu(qs[h], kp_ref[:, sls[h]], _NT) * scale, -jnp.inf) for h in hs]
        m = [jnp.maximum(jnp.max(s_c[h], axis=1, keepdims=True), jnp.max(s_p[h], axis=1, keepdims=True)) for h in hs]
        p_c = [jnp.exp(s_c[h] - m[h]) for h in hs]
        p_p = [jnp.exp(s_p[h] - m[h]) for h in hs]
        den = [jnp.sum(p_c[h], axis=1, keepdims=True) + jnp.sum(p_p[h], axis=1, keepdims=True) for h in hs]
        o = [_mxu(p_c[h], vc_ref[:, sls[h]], _NN) + _mxu(p_p[h], vp_ref[:, sls[h]], _NN) for h in hs]
        for h in hs:
            o_ref[:, sls[h]] = o[h] / den[h]
            lse_ref[:, sls[h]] = jnp.broadcast_to(m[h] + jnp.log(den[h]), (ATT_BLOCK, HEAD_DIM))

    def cur(c):
        return pl.BlockSpec((ATT_BLOCK, WIDTH), lambda r, i: (i, 3 * r + c))

    def prev(c):
        return pl.BlockSpec((ATT_BLOCK, WIDTH), lambda r, i: (jnp.maximum(i - 1, 0), 3 * r + c))

    out = pl.BlockSpec((ATT_BLOCK, WIDTH), lambda r, i: (i, r))
    return pl.pallas_call(
        body, name=name, grid=(d, nb), in_specs=[cur(0), cur(1), prev(1), cur(2), prev(2)],
        out_specs=[out, out],
        out_shape=[jax.ShapeDtypeStruct((rows, d * WIDTH), F32)] * 2,
        compiler_params=_params(("parallel", "arbitrary")),
    )(qkv, qkv, qkv, qkv, qkv)


def _att_bwd(qkv, do, lse, delta, d, name):
    rows = qkv.shape[0]
    nb = rows // ATT_BLOCK
    scale = HEAD_DIM ** -0.5

    def body(q_ref, qn_ref, kc_ref, kp_ref, vc_ref, vp_ref, do_ref, don_ref, l_ref, ln_ref, dl_ref, dln_ref, o_ref):
        i = pl.program_id(1)
        m_cur, m_prev = _att_masks()
        m_p = m_prev & (i > 0)
        m_n = m_prev & (i < nb - 1)

        hs = range(HEADS)
        sls = [slice(h * HEAD_DIM, (h + 1) * HEAD_DIM) for h in hs]
        col = [slice(h * HEAD_DIM, h * HEAD_DIM + 1) for h in hs]

        def probs(q_r, k_r, lse_r, mask):
            s = [_mxu(q_r[:, sls[h]], k_r[:, sls[h]], _NT) for h in hs]
            return [jnp.where(mask, jnp.exp(s[h] * scale - lse_r[:, col[h]]), 0.0) for h in hs]

        def dscores(p, do_r, v_r, dl_r):
            dp = [_mxu(do_r[:, sls[h]], v_r[:, sls[h]], _NT) for h in hs]
            return [(p[h] * (dp[h] - dl_r[:, col[h]])).astype(MXU_DTYPE) for h in hs]

        p = probs(q_ref, kc_ref, l_ref, m_cur)
        ds = dscores(p, do_ref, vc_ref, dl_ref)
        dq = [_mxu(ds[h], kc_ref[:, sls[h]], _NN) for h in hs]
        dk = [_mxu(ds[h], q_ref[:, sls[h]], _TN) for h in hs]
        dv = [_mxu(p[h], do_ref[:, sls[h]], _TN) for h in hs]
        p = probs(q_ref, kp_ref, l_ref, m_p)
        ds = dscores(p, do_ref, vp_ref, dl_ref)
        dq = [dq[h] + _mxu(ds[h], kp_ref[:, sls[h]], _NN) for h in hs]
        p = probs(qn_ref, kc_ref, ln_ref, m_n)
        ds = dscores(p, don_ref, vc_ref, dln_ref)
        dk = [dk[h] + _mxu(ds[h], qn_ref[:, sls[h]], _TN) for h in hs]
        dv = [dv[h] + _mxu(p[h], don_ref[:, sls[h]], _TN) for h in hs]
        for h in hs:
            o_ref[:, sls[h]] = dq[h] * scale
            o_ref[:, WIDTH + h * HEAD_DIM:WIDTH + (h + 1) * HEAD_DIM] = dk[h] * scale
            o_ref[:, 2 * WIDTH + h * HEAD_DIM:2 * WIDTH + (h + 1) * HEAD_DIM] = dv[h]

    def qkv_spec(c, shift):
        def idx(r, i):
            return (jnp.clip(i + shift, 0, nb - 1), 3 * r + c)
        return pl.BlockSpec((ATT_BLOCK, WIDTH), idx)

    def tok_spec(shift):
        def idx(r, i):
            return (jnp.clip(i + shift, 0, nb - 1), r)
        return pl.BlockSpec((ATT_BLOCK, WIDTH), idx)

    return pl.pallas_call(
        body, name=name, grid=(d, nb),
        in_specs=[qkv_spec(0, 0), qkv_spec(0, 1), qkv_spec(1, 0), qkv_spec(1, -1), qkv_spec(2, 0), qkv_spec(2, -1),
                  tok_spec(0), tok_spec(1), tok_spec(0), tok_spec(1), tok_spec(0), tok_spec(1)],
        out_specs=pl.BlockSpec((ATT_BLOCK, 3 * WIDTH), lambda r, i: (i, r)),
        out_shape=jax.ShapeDtypeStruct((rows, d * 3 * WIDTH), F32),
        compiler_params=_params(("parallel", "arbitrary")),
    )(qkv, qkv, qkv, qkv, qkv, qkv, do, do, lse, lse, delta, delta)


def _att_merge(os_, lses, proj, tm=256):
    t = proj.shape[0]

    def body(o0, o1, o2, l0, l1, l2, z_ref, oz_ref, o_ref, t0, t1, t2, s_o1, s_l1, s_o2, s_l2, s_t):
        _from_strided_view(o1, s_o1, DILATIONS[1])
        _from_strided_view(l1, s_l1, DILATIONS[1])
        _from_strided_view(o2, s_o2, DILATIONS[2])
        _from_strided_view(l2, s_l2, DILATIONS[2])
        a, b, c = l0[...], _load_tile(s_l1), _load_tile(s_l2)
        m = jnp.maximum(jnp.maximum(a, b), c)
        wa, wb, wc = jnp.exp(a - m), jnp.exp(b - m), jnp.exp(c - m)
        den = wa + wb + wc
        o = (wa * o0[...] + wb * _load_tile(s_o1) + wc * _load_tile(s_o2)) / den
        z = z_ref[...]
        o_ref[...] = o
        oz_ref[...] = (o * z * _sigmoid(z)).astype(oz_ref.dtype)
        total = m + jnp.log(den)
        t0[...] = total
        _store_tile(s_t, total)
        _to_strided_view(s_t, t1, DILATIONS[1])
        _to_strided_view(s_t, t2, DILATIONS[2])

    tok = pl.BlockSpec((tm, WIDTH), lambda i: (i, 0))
    views = [_strided_spec(tm, d, WIDTH) for d in DILATIONS]
    view_shapes = [jax.ShapeDtypeStruct((t // d, d * WIDTH), F32) for d in DILATIONS]
    return pl.pallas_call(
        body, name="att_merge", grid=(t // tm,),
        in_specs=views + views + [pl.BlockSpec((tm, WIDTH), lambda i: (i, SEG_ZA // WIDTH))],
        out_specs=[tok, tok] + views,
        out_shape=[jax.ShapeDtypeStruct((t, WIDTH), MXU_DTYPE), jax.ShapeDtypeStruct((t, WIDTH), F32)] + view_shapes,
        scratch_shapes=[_tile_scratch(tm, WIDTH)] * 5,
        compiler_params=_params(("parallel",)),
    )(*os_, *lses, proj)


def _att_merge_bwd(dproj, d_oz, o, proj, tm=256):
    t = proj.shape[0]

    def body(dp_ref, doz_ref, o_ref, z_ref, dz_ref, do0, do1, do2, dl0, dl1, dl2, s_do, s_dl):
        z, ov, g = z_ref[...], o_ref[...], doz_ref[...]
        sg = _sigmoid(z)
        do = g * z * sg
        dz_ref[...] = (g * ov * sg * (1.0 + z * (1.0 - sg))).astype(dz_ref.dtype)
        do0[...] = do.astype(do0.dtype)
        _store_tile(s_do, do)
        prod = do * ov
        for h in range(HEADS):
            sl = slice(h * HEAD_DIM, (h + 1) * HEAD_DIM)
            half = slice((h % 2) * HEAD_DIM, (h % 2 + 1) * HEAD_DIM)
            s_dl[h // 2, :, half] = jnp.broadcast_to(jnp.sum(prod[:, sl], axis=1, keepdims=True), (tm, HEAD_DIM))
        dl0[...] = _load_tile(s_dl)
        for d, do_v, dl_v in ((DILATIONS[1], do1, dl1), (DILATIONS[2], do2, dl2)):
            _to_strided_view(s_do, do_v, d)
            _to_strided_view(s_dl, dl_v, d)

    tok = pl.BlockSpec((tm, WIDTH), lambda i: (i, 0))
    seg = pl.BlockSpec((tm, WIDTH), lambda i: (i, SEG_ZA // WIDTH))
    views = [_strided_spec(tm, d, WIDTH) for d in DILATIONS]
    return pl.pallas_call(
        body, name="att_merge_bwd", grid=(t // tm,),
        in_specs=[pl.BlockSpec(memory_space=pl.ANY), tok, tok, seg],
        out_specs=[seg] + views + views,
        out_shape=[jax.ShapeDtypeStruct((t, PACKED_WIDTH), MXU_DTYPE)]
        + [jax.ShapeDtypeStruct((t // d, d * WIDTH), MXU_DTYPE) for d in DILATIONS]
        + [jax.ShapeDtypeStruct((t // d, d * WIDTH), F32) for d in DILATIONS],
        scratch_shapes=[_tile_scratch(tm, WIDTH)] * 2,
        input_output_aliases={0: 0},
        compiler_params=_params(("parallel",)),
    )(dproj, d_oz, o, proj)


def _shift_down(x, halo, s):
    if s == 0:
        return x
    xs = pltpu.roll(x, s, 0)
    head = jnp.where(_iota((8, x.shape[1]), 0) < s, pltpu.roll(halo, s, 0), xs[0:8])
    return jnp.concatenate([head, xs[8:]], axis=0)


def _shift_up(x, nxt, s):
    if s == 0:
        return x
    n = x.shape[0]
    xs = pltpu.roll(x, n - s, 0)
    tail = jnp.where(_iota((8, x.shape[1]), 0) >= 8 - s, pltpu.roll(nxt, 8 - s, 0), xs[n - 8:])
    return jnp.concatenate([xs[:n - 8], tail], axis=0)


def _conv_fwd(proj, conv_w, tm=256):
    t = proj.shape[0]
    cb = SEG_B // WIDTH

    def body(x_ref, halo_ref, w_ref, c_ref):
        halo = jnp.where(pl.program_id(0) > 0, halo_ref[...], 0.0)
        x = x_ref[...]
        w = w_ref[...]
        acc = jnp.zeros((tm, WIDTH), F32)
        for j in range(GDN_CONV):
            acc += _shift_down(x, halo, GDN_CONV - 1 - j) * w[j:j + 1, :]
        c_ref[...] = acc

    return pl.pallas_call(
        body, name="conv_fwd", grid=(t // tm, 3),
        in_specs=[pl.BlockSpec((tm, WIDTH), lambda i, c: (i, cb + c)),
                  pl.BlockSpec((8, WIDTH), lambda i, c: (jnp.maximum(i * (tm // 8) - 1, 0), cb + c)),
                  pl.BlockSpec((GDN_CONV, WIDTH), lambda i, c: (0, c))],
        out_specs=pl.BlockSpec((tm, WIDTH), lambda i, c: (i, c)),
        out_shape=jax.ShapeDtypeStruct((t, 3 * WIDTH), F32),
        compiler_params=_params(("parallel", "parallel")),
    )(proj, proj, conv_w)


def _conv_bwd(dproj, dc, proj, conv_w, tm=256):
    t = proj.shape[0]
    cb = SEG_B // WIDTH
    nt = t // tm

    def body(dp_ref, dc_ref, dcn_ref, x_ref, halo_ref, w_ref, dx_ref, dw_ref):
        i = pl.program_id(1)
        w = w_ref[...]
        dcn = jnp.where(i < nt - 1, dcn_ref[...], 0.0)
        dcv = dc_ref[...]
        acc = jnp.zeros((tm, WIDTH), F32)
        for j in range(GDN_CONV):
            acc += _shift_up(dcv, dcn, GDN_CONV - 1 - j) * w[j:j + 1, :]
        dx_ref[...] = acc.astype(dx_ref.dtype)
        halo = jnp.where(i > 0, halo_ref[...], 0.0)
        x = x_ref[...]
        row8 = _iota((8, WIDTH), 0)
        part = jnp.zeros((8, WIDTH), F32)
        for j in range(GDN_CONV):
            s = jnp.sum(dcv * _shift_down(x, halo, GDN_CONV - 1 - j), axis=0, keepdims=True)
            part += jnp.where(row8 == j, s, 0.0)

        @pl.when(i == 0)
        def _():
            dw_ref[...] = jnp.zeros_like(dw_ref)

        dw_ref[...] += part

    return pl.pallas_call(
        body, name="conv_bwd", grid=(3, nt),
        in_specs=[pl.BlockSpec(memory_space=pl.ANY),
                  pl.BlockSpec((tm, WIDTH), lambda c, i: (i, c)),
                  pl.BlockSpec((8, WIDTH), lambda c, i: (jnp.minimum((i + 1) * (tm // 8), t // 8 - 1), c)),
                  pl.BlockSpec((tm, WIDTH), lambda c, i: (i, cb + c)),
                  pl.BlockSpec((8, WIDTH), lambda c, i: (jnp.maximum(i * (tm // 8) - 1, 0), cb + c)),
                  pl.BlockSpec((GDN_CONV, WIDTH), lambda c, i: (0, c))],
        out_specs=[pl.BlockSpec((tm, WIDTH), lambda c, i: (i, cb + c)),
                   pl.BlockSpec((8, WIDTH), lambda c, i: (0, c))],
        out_shape=[jax.ShapeDtypeStruct((t, PACKED_WIDTH), MXU_DTYPE), jax.ShapeDtypeStruct((8, 3 * WIDTH), F32)],
        input_output_aliases={0: 0},
        compiler_params=_params(("parallel", "arbitrary")),
    )(dproj, dc, dc, proj, proj, conv_w)


def _chunk_matrices(tm):
    r, c = _iota((tm, tm), 0), _iota((tm, tm), 1)
    same = (r // GDN_CHUNK) == (c // GDN_CHUNK)
    return jnp.where(same & (c <= r), 1.0, 0.0), jnp.where(same, 1.0, 0.0)


def _gdn_gates(ba, a_log, dt_bias):
    al = ba + dt_bias
    return _sigmoid(ba), -jnp.exp(a_log) * _softplus(al), _sigmoid(al)


def _head_lane_eye():
    return jnp.where(_iota((HEADS, 128), 1) == _iota((HEADS, 128), 0) + HEADS, 1.0, 0.0)


def _gdn_prep(conv, proj, a_log, dt_bias, tm=256):
    t = proj.shape[0]
    nc = tm // GDN_CHUNK

    def body(c_ref, ba_ref, al_ref, dt_ref, q_ref, k_ref, v_ref, b_ref, g_ref, gl_ref, grow_ref):
        c = c_ref[...]
        a = c * _sigmoid(c)
        beta, g, _ = _gdn_gates(ba_ref[:, 0:128], al_ref[...], dt_ref[...])
        lmat, cmat = _chunk_matrices(tm)
        gc = jnp.dot(lmat, g, precision=HIGHEST, preferred_element_type=F32)
        gl = jnp.dot(cmat, g, precision=HIGHEST, preferred_element_type=F32)
        grow = lax.dot_general(_head_lane_eye(), gc, _NT, precision=HIGHEST, preferred_element_type=F32)
        for h in range(HEADS):
            sl = slice(h * HEAD_DIM, (h + 1) * HEAD_DIM)
            qh, kh, vh = a[:, sl], a[:, WIDTH + h * HEAD_DIM:WIDTH + (h + 1) * HEAD_DIM], a[:, 2 * WIDTH + h * HEAD_DIM:2 * WIDTH + (h + 1) * HEAD_DIM]
            rq = lax.rsqrt(jnp.sum(qh * qh, axis=1, keepdims=True) + NORM_EPS)
            rk = lax.rsqrt(jnp.sum(kh * kh, axis=1, keepdims=True) + NORM_EPS)
            q_ref[h] = qh * (rq * HEAD_DIM ** -0.5)
            k_ref[h] = kh * rk
            v_ref[h] = vh
            b_ref[h] = jnp.broadcast_to(beta[:, h:h + 1], (tm, HEAD_DIM))
            g_ref[h] = jnp.broadcast_to(gc[:, HEADS + h:HEADS + h + 1], (tm, HEAD_DIM))
            gl_ref[h] = jnp.broadcast_to(gl[:, HEADS + h:HEADS + h + 1], (tm, HEAD_DIM))
            for cc in range(nc):
                grow_ref[h, cc] = grow[h:h + 1, cc * GDN_CHUNK:(cc + 1) * GDN_CHUNK]

    hm = pl.BlockSpec((HEADS, tm, HEAD_DIM), lambda i: (0, i, 0))
    small = pl.BlockSpec((1, 128), lambda i: (0, 0))
    hm_shape = jax.ShapeDtypeStruct((HEADS, t, HEAD_DIM), F32)
    return pl.pallas_call(
        body, name="gdn_prep", grid=(t // tm,),
        in_specs=[pl.BlockSpec((tm, 3 * WIDTH), lambda i: (i, 0)),
                  pl.BlockSpec((tm, WIDTH), lambda i: (i, SEG_BA // WIDTH)), small, small],
        out_specs=[hm] * 6 + [pl.BlockSpec((HEADS, nc, 1, GDN_CHUNK), lambda i: (0, i, 0, 0))],
        out_shape=[hm_shape] * 6 + [jax.ShapeDtypeStruct((HEADS, t // GDN_CHUNK, 1, GDN_CHUNK), F32)],
        compiler_params=_params(("parallel",)),
    )(conv, proj, a_log, dt_bias)


def _gdn_prep_bwd(dproj, conv, proj, a_log, dt_bias, dq, dk, dv, db, dg, dgl, dgrow, tm=256):
    t = proj.shape[0]
    nc = tm // GDN_CHUNK

    def body(dp_ref, c_ref, ba_ref, al_ref, dt_ref, dq_ref, dk_ref, dv_ref, db_ref, dg_ref, dgl_ref, dgrow_ref,
             dba_ref, dc_ref, small_ref, row_scr):
        c = c_ref[...]
        sg = _sigmoid(c)
        a = c * sg
        beta, g, sig_al = _gdn_gates(ba_ref[:, 0:128], al_ref[...], dt_ref[...])
        lane = _iota((1, 128), 1)
        d_beta = jnp.zeros((tm, 128), F32)
        d_gc = jnp.zeros((tm, 128), F32)
        d_gl = jnp.zeros((tm, 128), F32)
        for h in range(HEADS):
            for cc in range(nc):
                row_scr[h:h + 1, cc * GDN_CHUNK:(cc + 1) * GDN_CHUNK] = dgrow_ref[h, cc]
            d_beta += jnp.sum(db_ref[h], axis=1, keepdims=True) * jnp.where(lane == h, 1.0, 0.0)
            d_gc += jnp.sum(dg_ref[h], axis=1, keepdims=True) * jnp.where(lane == HEADS + h, 1.0, 0.0)
            d_gl += jnp.sum(dgl_ref[h], axis=1, keepdims=True) * jnp.where(lane == HEADS + h, 1.0, 0.0)
            qs, ks, vs = (slice(h * HEAD_DIM, (h + 1) * HEAD_DIM), slice(WIDTH + h * HEAD_DIM, WIDTH + (h + 1) * HEAD_DIM),
                          slice(2 * WIDTH + h * HEAD_DIM, 2 * WIDTH + (h + 1) * HEAD_DIM))
            qh, kh = a[:, qs], a[:, ks]
            rq = lax.rsqrt(jnp.sum(qh * qh, axis=1, keepdims=True) + NORM_EPS)
            rk = lax.rsqrt(jnp.sum(kh * kh, axis=1, keepdims=True) + NORM_EPS)
            gq, gk = dq_ref[h] * HEAD_DIM ** -0.5, dk_ref[h]
            da_q = rq * gq - qh * (rq * rq * rq) * jnp.sum(gq * qh, axis=1, keepdims=True)
            da_k = rk * gk - kh * (rk * rk * rk) * jnp.sum(gk * kh, axis=1, keepdims=True)
            dsilu = lambda s, x: s * (1.0 + x * (1.0 - s))
            dc_ref[:, qs] = da_q * dsilu(sg[:, qs], c[:, qs])
            dc_ref[:, ks] = da_k * dsilu(sg[:, ks], c[:, ks])
            dc_ref[:, vs] = dv_ref[h] * dsilu(sg[:, vs], c[:, vs])
        d_gc += lax.dot_general(row_scr[...], _head_lane_eye(), _TN, precision=HIGHEST, preferred_element_type=F32)
        lmat, cmat = _chunk_matrices(tm)
        d_g = (lax.dot_general(lmat, d_gc, _TN, precision=HIGHEST, preferred_element_type=F32)
               + lax.dot_general(cmat, d_gl, _TN, precision=HIGHEST, preferred_element_type=F32))
        d_al = d_g * (-jnp.exp(al_ref[...])) * sig_al
        d_bl = d_beta * beta * (1.0 - beta)
        dba_ref[...] = jnp.concatenate([d_bl + d_al, jnp.zeros((tm, WIDTH - 128), F32)], axis=1).astype(dba_ref.dtype)
        row8 = _iota((8, 128), 0)
        part = (jnp.where(row8 == 0, jnp.sum(d_g * g, axis=0, keepdims=True), 0.0)
                + jnp.where(row8 == 1, jnp.sum(d_al, axis=0, keepdims=True), 0.0))

        @pl.when(pl.program_id(0) == 0)
        def _():
            small_ref[...] = jnp.zeros_like(small_ref)

        small_ref[...] += part

    hm = pl.BlockSpec((HEADS, tm, HEAD_DIM), lambda i: (0, i, 0))
    small = pl.BlockSpec((1, 128), lambda i: (0, 0))
    seg = pl.BlockSpec((tm, WIDTH), lambda i: (i, SEG_BA // WIDTH))
    return pl.pallas_call(
        body, name="gdn_prep_bwd", grid=(t // tm,),
        in_specs=[pl.BlockSpec(memory_space=pl.ANY), pl.BlockSpec((tm, 3 * WIDTH), lambda i: (i, 0)), seg, small, small]
        + [hm] * 6 + [pl.BlockSpec((HEADS, nc, 1, GDN_CHUNK), lambda i: (0, i, 0, 0))],
        out_specs=[seg, pl.BlockSpec((tm, 3 * WIDTH), lambda i: (i, 0)), pl.BlockSpec((8, 128), lambda i: (0, 0))],
        out_shape=[jax.ShapeDtypeStruct((t, PACKED_WIDTH), MXU_DTYPE), jax.ShapeDtypeStruct((t, 3 * WIDTH), F32),
                   jax.ShapeDtypeStruct((8, 128), F32)],
        scratch_shapes=[pltpu.VMEM((HEADS, tm), F32)],
        input_output_aliases={0: 0},
        compiler_params=_params(("arbitrary",)),
    )(dproj, conv, proj, a_log, dt_bias, dq, dk, dv, db, dg, dgl, dgrow)


_BNN = (((2,), (1,)), ((0,), (0,)))
_BNT = (((2,), (2,)), ((0,), (0,)))
_BTN = (((1,), (1,)), ((0,), (0,)))


@jax.custom_vjp
def _MM_NN(a, b):
    return _mxu(a, b, _BNN)


@jax.custom_vjp
def _MM_NT(a, b):
    return _mxu(a, b, _BNT)


@jax.custom_vjp
def _MM_TN(a, b):
    return _mxu(a, b, _BTN)


_MM_NN.defvjp(lambda a, b: (_mxu(a, b, _BNN), (a, b)), lambda r, g: (_mxu(g, r[1], _BNT), _mxu(r[0], g, _BTN)))
_MM_NT.defvjp(lambda a, b: (_mxu(a, b, _BNT), (a, b)), lambda r, g: (_mxu(g, r[1], _BNN), _mxu(g, r[0], _BTN)))
_MM_TN.defvjp(lambda a, b: (_mxu(a, b, _BTN), (a, b)), lambda r, g: (_mxu(r[1], g, _BNT), _mxu(r[0], g, _BNN)))


def _split(a):
    hi = a.astype(MXU_DTYPE)
    return hi, (a - hi.astype(F32)).astype(MXU_DTYPE)


def _dot3(a, b, dims):
    (ah, al), (bh, bl) = a, b
    (ca,), (cb,) = dims[0]
    return lax.dot_general(jnp.concatenate([ah, ah, al], axis=ca), jnp.concatenate([bh, bl, bh], axis=cb), dims,
                           preferred_element_type=F32)


def _unit_lower_inverse(a):
    c = GDN_CHUNK
    eye = jnp.where(_iota((c, c), 0) == _iota((c, c), 1), 1.0, 0.0)
    x = eye - a
    p = a
    for _ in range(5):
        ps = _split(p)
        p = _dot3(ps, ps, _BNN)
        x = x + _dot3(_split(x), _split(p), _BNN)
    return x


@jax.custom_vjp
def _SAVED_INVERSE(a, t_inv):
    return t_inv


def _saved_inverse_bwd(t_inv, g):
    ts = _split(t_inv)
    return -_dot3(ts, _split(_dot3(_split(g), ts, _BNT)), _BTN), jnp.zeros_like(t_inv)


_SAVED_INVERSE.defvjp(lambda a, t_inv: (t_inv, t_inv), _saved_inverse_bwd)


def _gdn_chunk(q, k, v, beta, g1, g2, gl, state, t_inv=None):
    c = GDN_CHUNK
    if t_inv is None:
        _mm_nn, _mm_nt, _mm_tn = (functools.partial(_mxu, dims=dd) for dd in (_BNN, _BNT, _BTN))
    else:
        _mm_nn, _mm_nt, _mm_tn = _MM_NN, _MM_NT, _MM_TN
    row, col = _iota((c, c), 0), _iota((c, c), 1)
    incl, strict = row >= col, row > col
    decay = jnp.where(incl, jnp.exp(jnp.where(incl, g1 - g2, 0.0)), 0.0)
    eg = jnp.exp(g1)
    kb = k * beta
    a = _mm_nt(kb, k) * jnp.where(strict, decay, 0.0)
    inv = _unit_lower_inverse(a) if t_inv is None else _SAVED_INVERSE(a, t_inv)
    u = _mm_nn(inv, v * beta)
    w = _mm_nn(inv, kb * eg)
    attn = _mm_nt(q, k) * decay
    v_new = u - _mm_nn(w, state)
    o = _mm_nn(q * eg, state) + _mm_nn(attn, v_new)
    new_state = state * jnp.exp(gl) + _mm_tn(k * jnp.exp(gl - g1), v_new)
    return (o, new_state, inv) if t_inv is None else (o, new_state)


def _gdn_fwd(q, k, v, beta, g, gl, grow, cpb=2):
    t = q.shape[1]
    rows = cpb * GDN_CHUNK

    def body(q_ref, k_ref, v_ref, b_ref, g_ref, gl_ref, grow_ref, o_ref, st_ref, inv_ref, state):
        @pl.when(pl.program_id(0) == 0)
        def _():
            state[...] = jnp.zeros_like(state)

        s = state[...]
        for cc in range(cpb):
            sl = slice(cc * GDN_CHUNK, (cc + 1) * GDN_CHUNK)
            st_ref[:, cc] = s
            g2 = jnp.broadcast_to(grow_ref[:, cc], (HEADS, GDN_CHUNK, GDN_CHUNK))
            o, s, inv = _gdn_chunk(q_ref[:, sl, :], k_ref[:, sl, :], v_ref[:, sl, :], b_ref[:, sl, :], g_ref[:, sl, :], g2,
                                   gl_ref[:, sl, :], s)
            o_ref[:, sl, :] = o
            inv_ref[:, cc] = inv
        state[...] = s

    hm = pl.BlockSpec((HEADS, rows, HEAD_DIM), lambda i: (0, i, 0))
    per_chunk = pl.BlockSpec((HEADS, cpb, GDN_CHUNK, HEAD_DIM), lambda i: (0, i, 0, 0))
    chunk_shape = jax.ShapeDtypeStruct((HEADS, t // GDN_CHUNK, GDN_CHUNK, HEAD_DIM), F32)
    return pl.pallas_call(
        body, name="gdn_fwd", grid=(t // rows,),
        in_specs=[hm] * 6 + [pl.BlockSpec((HEADS, cpb, 1, GDN_CHUNK), lambda i: (0, i, 0, 0))],
        out_specs=[hm, per_chunk, per_chunk],
        out_shape=[jax.ShapeDtypeStruct((HEADS, t, HEAD_DIM), F32), chunk_shape, chunk_shape],
        scratch_shapes=[pltpu.VMEM((HEADS, GDN_CHUNK, HEAD_DIM), F32)],
        compiler_params=_params(("arbitrary",)),
    )(q, k, v, beta, g, gl, grow)


def _gdn_bwd(q, k, v, beta, g, gl, grow, states, invs, do, cpb=1):
    t = q.shape[1]
    rows = cpb * GDN_CHUNK
    nsteps = t // rows

    def body(q_ref, k_ref, v_ref, b_ref, g_ref, gl_ref, grow_ref, st_ref, inv_ref, do_ref,
             dq_ref, dk_ref, dv_ref, db_ref, dg_ref, dgl_ref, dgrow_ref, dstate):
        @pl.when(pl.program_id(0) == 0)
        def _():
            dstate[...] = jnp.zeros_like(dstate)

        ds = dstate[...]
        for cc in reversed(range(cpb)):
            sl = slice(cc * GDN_CHUNK, (cc + 1) * GDN_CHUNK)
            g2 = jnp.broadcast_to(grow_ref[:, cc], (HEADS, GDN_CHUNK, GDN_CHUNK))
            _, vjp = jax.vjp(_gdn_chunk, q_ref[:, sl, :], k_ref[:, sl, :], v_ref[:, sl, :], b_ref[:, sl, :],
                             g_ref[:, sl, :], g2, gl_ref[:, sl, :], st_ref[:, cc], inv_ref[:, cc])
            gq, gk, gv, gb, gg1, gg2, ggl, ds, _ = vjp((do_ref[:, sl, :], ds))
            dq_ref[:, sl, :] = gq
            dk_ref[:, sl, :] = gk
            dv_ref[:, sl, :] = gv
            db_ref[:, sl, :] = gb
            dg_ref[:, sl, :] = gg1
            dgl_ref[:, sl, :] = ggl
            dgrow_ref[:, cc] = jnp.sum(gg2, axis=1, keepdims=True)
        dstate[...] = ds

    hm = pl.BlockSpec((HEADS, rows, HEAD_DIM), lambda i: (0, nsteps - 1 - i, 0))
    rowspec = pl.BlockSpec((HEADS, cpb, 1, GDN_CHUNK), lambda i: (0, nsteps - 1 - i, 0, 0))
    per_chunk = pl.BlockSpec((HEADS, cpb, GDN_CHUNK, HEAD_DIM), lambda i: (0, nsteps - 1 - i, 0, 0))
    hm_shape = jax.ShapeDtypeStruct((HEADS, t, HEAD_DIM), F32)
    return pl.pallas_call(
        body, name="gdn_bwd", grid=(nsteps,),
        in_specs=[hm] * 6 + [rowspec, per_chunk, per_chunk, hm],
        out_specs=[hm] * 6 + [rowspec],
        out_shape=[hm_shape] * 6 + [jax.ShapeDtypeStruct((HEADS, t // GDN_CHUNK, 1, GDN_CHUNK), F32)],
        scratch_shapes=[pltpu.VMEM((HEADS, GDN_CHUNK, HEAD_DIM), F32)],
        compiler_params=_params(("arbitrary",)),
    )(q, k, v, beta, g, gl, grow, states, invs, do)


def _gdn_out(o_hm, gdn_norm_w, proj, tm=256):
    t = proj.shape[0]

    def body(o_ref, w_ref, z_ref, oz_ref):
        z = z_ref[...]
        gate = z * _sigmoid(z)
        w = w_ref[...]
        for h in range(HEADS):
            sl = slice(h * HEAD_DIM, (h + 1) * HEAD_DIM)
            o = o_ref[h]
            r = lax.rsqrt(jnp.mean(o * o, axis=1, keepdims=True) + NORM_EPS)
            oz_ref[:, sl] = (o * r * w * gate[:, sl]).astype(oz_ref.dtype)

    tok = pl.BlockSpec((tm, WIDTH), lambda i: (i, 0))
    return pl.pallas_call(
        body, name="gdn_out", grid=(t // tm,),
        in_specs=[pl.BlockSpec((HEADS, tm, HEAD_DIM), lambda i: (0, i, 0)), pl.BlockSpec((1, HEAD_DIM), lambda i: (0, 0)),
                  pl.BlockSpec((tm, WIDTH), lambda i: (i, SEG_ZB // WIDTH))],
        out_specs=tok, out_shape=jax.ShapeDtypeStruct((t, WIDTH), MXU_DTYPE),
        compiler_params=_params(("parallel",)),
    )(o_hm, gdn_norm_w, proj)


def _gdn_out_bwd(dproj, d_oz, o_hm, gdn_norm_w, proj, tm=256):
    t = proj.shape[0]

    def body(dp_ref, doz_ref, o_ref, w_ref, z_ref, dz_ref, do_ref, dw_ref):
        z = z_ref[...]
        sg = _sigmoid(z)
        gate = z * sg
        dgate = sg * (1.0 + z * (1.0 - sg))
        w = w_ref[...]
        dw = jnp.zeros((1, HEAD_DIM), F32)
        for h in range(HEADS):
            sl = slice(h * HEAD_DIM, (h + 1) * HEAD_DIM)
            o = o_ref[h]
            g = doz_ref[:, sl]
            r = lax.rsqrt(jnp.mean(o * o, axis=1, keepdims=True) + NORM_EPS)
            on = o * r * w
            dz_ref[:, sl] = (g * on * dgate[:, sl]).astype(dz_ref.dtype)
            dn = g * gate[:, sl]
            dw += jnp.sum(dn * o * r, axis=0, keepdims=True)
            dnw = dn * w
            do_ref[h] = r * dnw - o * (r * r * r) * jnp.mean(dnw * o, axis=1, keepdims=True)

        @pl.when(pl.program_id(0) == 0)
        def _():
            dw_ref[...] = jnp.zeros_like(dw_ref)

        dw_ref[...] += jnp.concatenate([dw, jnp.zeros((7, HEAD_DIM), F32)], axis=0)

    tok = pl.BlockSpec((tm, WIDTH), lambda i: (i, 0))
    seg = pl.BlockSpec((tm, WIDTH), lambda i: (i, SEG_ZB // WIDTH))
    hm = pl.BlockSpec((HEADS, tm, HEAD_DIM), lambda i: (0, i, 0))
    return pl.pallas_call(
        body, name="gdn_out_bwd", grid=(t // tm,),
        in_specs=[pl.BlockSpec(memory_space=pl.ANY), tok, hm, pl.BlockSpec((1, HEAD_DIM), lambda i: (0, 0)), seg],
        out_specs=[seg, hm, pl.BlockSpec((8, HEAD_DIM), lambda i: (0, 0))],
        out_shape=[jax.ShapeDtypeStruct((t, PACKED_WIDTH), MXU_DTYPE), jax.ShapeDtypeStruct((HEADS, t, HEAD_DIM), F32),
                   jax.ShapeDtypeStruct((8, HEAD_DIM), F32)],
        input_output_aliases={0: 0},
        compiler_params=_params(("arbitrary",)),
    )(dproj, d_oz, o_hm, gdn_norm_w, proj)


def _merge(y_a, y_b, proj, tm=256):
    t = proj.shape[0]

    def body(ya_ref, yb_ref, ga_ref, gb_ref, m_ref):
        m_ref[...] = (_sigmoid(ga_ref[...]) * ya_ref[...] + _sigmoid(gb_ref[...]) * yb_ref[...]).astype(m_ref.dtype)

    half = pl.BlockSpec((tm, WIDTH), lambda i, c: (i, c))
    return pl.pallas_call(
        body, name="merge", grid=(t // tm, 2),
        in_specs=[half, half, pl.BlockSpec((tm, WIDTH), lambda i, c: (i, SEG_GA // WIDTH + c)),
                  pl.BlockSpec((tm, WIDTH), lambda i, c: (i, SEG_GB // WIDTH + c))],
        out_specs=half, out_shape=jax.ShapeDtypeStruct((t, D_MODEL), MXU_DTYPE),
        compiler_params=_params(("parallel", "parallel")),
    )(y_a, y_b, proj, proj)


def _merge_bwd(dproj, d_m, y, proj, seg, name, tm=256):
    t = proj.shape[0]

    def body(*refs):
        dm_ref, y_ref, g_ref, dg_ref, dy_ref = refs[-5:]
        dm = dm_ref[...]
        s = _sigmoid(g_ref[...])
        dy_ref[...] = (dm * s).astype(dy_ref.dtype)
        dg_ref[...] = (dm * y_ref[...] * s * (1.0 - s)).astype(dg_ref.dtype)

    half = pl.BlockSpec((tm, WIDTH), lambda i, c: (i, c))
    gate = pl.BlockSpec((tm, WIDTH), lambda i, c: (i, seg // WIDTH + c))
    specs, args, aliases = [half, half, gate], [d_m, y, proj], {}
    if dproj is not None:
        specs, args, aliases = [pl.BlockSpec(memory_space=pl.ANY)] + specs, [dproj] + args, {0: 0}
    return pl.pallas_call(
        body, name=name, grid=(t // tm, 2), in_specs=specs, out_specs=[gate, half],
        out_shape=[jax.ShapeDtypeStruct((t, PACKED_WIDTH), MXU_DTYPE), jax.ShapeDtypeStruct((t, D_MODEL), MXU_DTYPE)],
        input_output_aliases=aliases,
        compiler_params=_params(("parallel", "parallel")),
    )(*args)


def _tail(x, mo, final_w, target, tm=256):
    t = x.shape[0]

    def body(x_ref, mo_ref, w_ref, t_ref, dxm_ref, dx_ref, loss_ref, dw_ref):
        x2 = x_ref[...] + mo_ref[...]
        w = w_ref[...]
        r = lax.rsqrt(jnp.mean(x2 * x2, axis=-1, keepdims=True) + NORM_EPS)
        xn = x2 * r
        err = xn * w - t_ref[...]
        dy = err * (1.0 / D_MODEL)
        dyw = dy * w
        dx2 = r * dyw - x2 * (r * r * r) * jnp.mean(dyw * x2, axis=-1, keepdims=True)
        dx_ref[...] = dx2
        dxm_ref[...] = dx2.astype(dxm_ref.dtype)
        loss = 0.5 * jnp.sum(jnp.sum(err * err, axis=-1, keepdims=True) * (1.0 / D_MODEL), axis=0, keepdims=True)
        onehot = jnp.where((_iota((8, 128), 0) == 0) & (_iota((8, 128), 1) == 0), 1.0, 0.0)

        @pl.when(pl.program_id(0) == 0)
        def _():
            loss_ref[...] = jnp.zeros_like(loss_ref)
            dw_ref[...] = jnp.zeros_like(dw_ref)

        loss_ref[...] += loss * onehot
        dw_ref[...] += jnp.where(_iota((8, D_MODEL), 0) == 0, jnp.sum(dy * xn, axis=0, keepdims=True), 0.0)

    tok = pl.BlockSpec((tm, D_MODEL), lambda i: (i, 0))
    return pl.pallas_call(
        body, name="tail", grid=(t // tm,),
        in_specs=[tok, tok, pl.BlockSpec((1, D_MODEL), lambda i: (0, 0)), tok],
        out_specs=[tok, tok, pl.BlockSpec((8, 128), lambda i: (0, 0)), pl.BlockSpec((8, D_MODEL), lambda i: (0, 0))],
        out_shape=[jax.ShapeDtypeStruct((t, D_MODEL), MXU_DTYPE), jax.ShapeDtypeStruct((t, D_MODEL), F32),
                   jax.ShapeDtypeStruct((8, 128), F32), jax.ShapeDtypeStruct((8, D_MODEL), F32)],
        compiler_params=_params(("arbitrary",)),
    )(x, mo, final_w, target)


def _norm_bwd(x, norm_w, dh, dx2, tm=256):
    t = x.shape[0]

    def body(x_ref, w_ref, dh_ref, dx2_ref, dx_ref, dw_ref):
        xf, w, dh_ = x_ref[...], w_ref[...], dh_ref[...]
        r = lax.rsqrt(jnp.mean(xf * xf, axis=-1, keepdims=True) + NORM_EPS)
        dhw = dh_ * w
        dx_ref[...] = dx2_ref[...] + r * dhw - xf * (r * r * r) * jnp.mean(dhw * xf, axis=-1, keepdims=True)

        @pl.when(pl.program_id(0) == 0)
        def _():
            dw_ref[...] = jnp.zeros_like(dw_ref)

        dw_ref[...] += jnp.where(_iota((8, D_MODEL), 0) == 0, jnp.sum(dh_ * xf * r, axis=0, keepdims=True), 0.0)

    tok = pl.BlockSpec((tm, D_MODEL), lambda i: (i, 0))
    return pl.pallas_call(
        body, name="norm_bwd", grid=(t // tm,),
        in_specs=[tok, pl.BlockSpec((1, D_MODEL), lambda i: (0, 0)), tok, tok],
        out_specs=[tok, pl.BlockSpec((8, D_MODEL), lambda i: (0, 0))],
        out_shape=[jax.ShapeDtypeStruct((t, D_MODEL), F32), jax.ShapeDtypeStruct((8, D_MODEL), F32)],
        compiler_params=_params(("arbitrary",)),
    )(x, norm_w, dh, dx2)


def _local_step(x, target, norm_w, wp, conv_w, a_log, dt_bias, gdn_norm_w, w_up_a, w_up_b, w_out, final_w):
    t = x.shape[0]
    tables = _rope_tables(t)
    a_log = jnp.pad(a_log, ((0, 0), (HEADS, 128 - 2 * HEADS)))
    dt_bias = jnp.pad(dt_bias, ((0, 0), (HEADS, 128 - 2 * HEADS)))

    proj, h_t = _norm_proj(x, norm_w, wp)
    qkvs = _rope_fwd(proj, tables)
    outs, lses = zip(*[_att_fwd(qkvs[gi], d, f"att_fwd{gi}") for gi, d in enumerate(DILATIONS)])
    oz_a, o_a, *lse_views = _att_merge(outs, lses, proj)
    conv = _conv_fwd(proj, conv_w)
    gq, gk, gv, gb, gg, ggl, grow = _gdn_prep(conv, proj, a_log, dt_bias)
    o_b, states, invs = _gdn_fwd(gq, gk, gv, gb, gg, ggl, grow)
    oz_b = _gdn_out(o_b, gdn_norm_w, proj)
    big = dict(tm=1024, tn=1024, tk=1024)
    y_a = _matmul(oz_a, w_up_a, "nn", "up_a", **big)
    y_b = _matmul(oz_b, w_up_b, "nn", "up_b", **big)
    merged = _merge(y_a, y_b, proj)
    mo = _matmul(merged, w_out, "nn", "out_proj", **big)
    dx2_m, dx2, loss_blk, d_final = _tail(x, mo, final_w, target)

    d_wout = _matmul(merged, dx2_m, "tn", "d_w_out", **big)
    d_m = _matmul(dx2_m, w_out, "nt", "d_merged", **big)
    dproj, dy_a = _merge_bwd(None, d_m, y_a, proj, SEG_GA, "merge_bwd_a")
    dproj, dy_b = _merge_bwd(dproj, d_m, y_b, proj, SEG_GB, "merge_bwd_b")
    d_wua = _matmul(oz_a, dy_a, "tn", "d_w_up_a", **big)
    d_wub = _matmul(oz_b, dy_b, "tn", "d_w_up_b", **big)
    d_oz_a = _matmul(dy_a, w_up_a, "nt", "d_oz_a", **big)
    d_oz_b = _matmul(dy_b, w_up_b, "nt", "d_oz_b", **big)
    dproj, *views = _att_merge_bwd(dproj, d_oz_a, o_a, proj)
    do_views, delta_views = views[:3], views[3:]
    dqkvs = [_att_bwd(qkvs[gi], do_views[gi], lse_views[gi], delta_views[gi], d, f"att_bwd{gi}")
             for gi, d in enumerate(DILATIONS)]
    dproj = _rope_bwd(dproj, dqkvs, tables)
    dproj, do_b, d_gnw = _gdn_out_bwd(dproj, d_oz_b, o_b, gdn_norm_w, proj)
    dgq, dgk, dgv, dgb, dgg, dggl, dgrow = _gdn_bwd(gq, gk, gv, gb, gg, ggl, grow, states, invs, do_b)
    dproj, dconv, d_small = _gdn_prep_bwd(dproj, conv, proj, a_log, dt_bias, dgq, dgk, dgv, dgb, dgg, dggl, dgrow)
    dproj, d_convw = _conv_bwd(dproj, dconv, proj, conv_w)
    d_wp = _matmul(h_t, dproj, "nn", "d_w_in", tm=1024, tn=512, tk=2048)
    dh = _matmul(dproj, wp, "nt", "d_h", tm=1024, tn=1024, tk=PACKED_WIDTH // 4)
    grad_x, d_norm = _norm_bwd(x, norm_w, dh, dx2)
    return dict(loss=loss_blk, grad_x=grad_x, norm_w=d_norm[0:1], w_in=d_wp, conv_w=d_convw[0:GDN_CONV],
                a_log=d_small[0:1, HEADS:2 * HEADS], dt_bias=d_small[1:2, HEADS:2 * HEADS], gdn_norm_w=d_gnw[0:1],
                w_up_a=d_wua, w_up_b=d_wub,
                w_out=d_wout, final_norm_w=d_final[0:1])


SHARDS = 4
W_IN_SHARD = IN_WIDTH // SHARDS
ROWS_UP = WIDTH * (D_MODEL // SHARDS) // 128
ROWS_OUT = (D_MODEL // SHARDS) * D_MODEL // 128
CONV_SHARD = 3 * WIDTH // SHARDS
ROWS_CONV = 16
SLAB_ROWS = 2 * ROWS_UP + ROWS_OUT + 2 * ROWS_CONV
HALF_ROWS = SLAB_ROWS // 2
BIG_HALF = (D_MODEL // 2, W_IN_SHARD)
SMALL_HALF = (HALF_ROWS, 128)
MESH = pl.DeviceIdType.MESH
ANY = pl.BlockSpec(memory_space=pl.ANY)


def _pad_rows(a, rows):
    return jnp.pad(a, ((0, rows - a.shape[0]), (0, 0)))


def _pack_slab(w_up_a, w_up_b, w_out, conv, conv_lo):
    parts = [w_up_a.reshape(ROWS_UP, 128), w_up_b.reshape(ROWS_UP, 128), w_out.reshape(ROWS_OUT, 128),
             _pad_rows(conv.reshape(-1, 128), ROWS_CONV), _pad_rows(conv_lo.reshape(-1, 128), ROWS_CONV)]
    return jnp.concatenate(parts, axis=0).reshape(2, *SMALL_HALF)


def _unpack_slab(slab):
    slab = slab.reshape(SLAB_ROWS, 128)
    r0 = 0
    out = []
    for rows, shape in ((ROWS_UP, (WIDTH, D_MODEL // SHARDS)), (ROWS_UP, (WIDTH, D_MODEL // SHARDS)),
                        (ROWS_OUT, (D_MODEL // SHARDS, D_MODEL)), (ROWS_CONV, None), (ROWS_CONV, None)):
        part = slab[r0:r0 + rows]
        out.append(part[:GDN_CONV * CONV_SHARD // 128].reshape(GDN_CONV, CONV_SHARD) if shape is None else part.reshape(shape))
        r0 += rows
    return out


def _mesh_position():
    x, y, c = lax.axis_index("x"), lax.axis_index("y"), lax.axis_index("c")
    return x, y, c, [(1 - x, y), (x, 1 - y), (1 - x, 1 - y)]


def _gather_weights(shards):
    n = len(shards)

    def body(*refs):
        in_refs, out_refs, (send_sems, recv_sems) = refs[:n], refs[n:2 * n], refs[2 * n:]
        x, y, c, chips = _mesh_position()

        def half(a, chip, which):
            return out_refs[a].at[2 * chip[0] + chip[1], which]

        def copy(k, src, dst, to):
            return pltpu.make_async_remote_copy(src_ref=src, dst_ref=dst, send_sem=send_sems.at[k], recv_sem=recv_sems.at[k],
                                                device_id=to, device_id_type=MESH)

        pairs = [(a, j, chip) for a in range(n) for j, chip in enumerate(chips)]
        first = [copy(6 * a + j, in_refs[a].at[c], half(a, (x, y), c), (*chip, c)) for a, j, chip in pairs]
        for cp in first:
            cp.start()
        passed = [copy(6 * a + 3 + j, half(a, chip, c), half(a, chip, c), (x, y, 1 - c)) for a, j, chip in pairs]
        for i, (a, j, chip) in enumerate(pairs):
            copy(6 * a + j, half(a, chip, c), half(a, chip, c), (x, y, c)).wait_recv()
            passed[i].start()
        for a, j, chip in pairs:
            copy(6 * a + 3 + j, half(a, chip, 1 - c), half(a, chip, 1 - c), (x, y, c)).wait_recv()
        for cp in first + passed:
            cp.wait_send()

    return pl.pallas_call(
        body, name="gather_weights", in_specs=[ANY] * n, out_specs=[ANY] * n,
        out_shape=[jax.ShapeDtypeStruct((SHARDS, *s.shape), s.dtype) for s in shards],
        scratch_shapes=[pltpu.SemaphoreType.DMA((6 * n,)), pltpu.SemaphoreType.DMA((6 * n,))],
    )(*shards)


def _exchange_halves(grads):
    n = len(grads)

    def body(*refs):
        g_refs, out_refs, (send_sems, recv_sems) = refs[:n], refs[n:2 * n], refs[2 * n:]
        x, y, c, _ = _mesh_position()
        copies = [pltpu.make_async_remote_copy(src_ref=g_refs[a].at[s, 1 - c], dst_ref=out_refs[a].at[s],
                                               send_sem=send_sems.at[SHARDS * a + s], recv_sem=recv_sems.at[SHARDS * a + s],
                                               device_id=(x, y, 1 - c), device_id_type=MESH)
                  for a in range(n) for s in range(SHARDS)]
        for cp in copies:
            cp.start()
        for cp in copies:
            cp.wait()

    return pl.pallas_call(
        body, name="exchange_halves", in_specs=[ANY] * n, out_specs=[ANY] * n,
        out_shape=[jax.ShapeDtypeStruct((SHARDS, *g.shape[2:]), F32) for g in grads],
        scratch_shapes=[pltpu.SemaphoreType.DMA((SHARDS * n,)), pltpu.SemaphoreType.DMA((SHARDS * n,))],
    )(*grads)


def _pair_sum(grads, recv, blk, name):
    _, _, rows, cols = grads.shape

    def body(c_ref, g_ref, r_ref, o_ref):
        o_ref[...] = (g_ref[0] + r_ref[...]).astype(o_ref.dtype)

    spec = pl.BlockSpec((1, blk, cols), lambda s, i, c_ref: (s, i, 0))
    return pl.pallas_call(
        body, name=name,
        grid_spec=pltpu.PrefetchScalarGridSpec(
            num_scalar_prefetch=1, grid=(SHARDS, rows // blk),
            in_specs=[pl.BlockSpec((1, 1, blk, cols), lambda s, i, c_ref: (s, c_ref[0], i, 0)), spec],
            out_specs=spec),
        out_shape=jax.ShapeDtypeStruct((SHARDS, rows, cols), MXU_DTYPE),
        compiler_params=_params(("parallel", "parallel")),
    )(lax.axis_index("c").astype(jnp.int32).reshape(1), grads, recv)


def _scatter_pairs(pairs):
    n = len(pairs)

    def body(*refs):
        p_refs, out_refs, (send_sems, recv_sems) = refs[:n], refs[n:2 * n], refs[2 * n:]
        x, y, c, chips = _mesh_position()
        copies = [pltpu.make_async_remote_copy(src_ref=p_refs[a].at[2 * chip[0] + chip[1]], dst_ref=out_refs[a].at[j],
                                               send_sem=send_sems.at[3 * a + j], recv_sem=recv_sems.at[3 * a + j],
                                               device_id=(*chip, c), device_id_type=MESH)
                  for a in range(n) for j, chip in enumerate(chips)]
        for cp in copies:
            cp.start()
        for cp in copies:
            cp.wait()

    return pl.pallas_call(
        body, name="scatter_pairs", in_specs=[ANY] * n, out_specs=[ANY] * n,
        out_shape=[jax.ShapeDtypeStruct((3, *p.shape[1:]), p.dtype) for p in pairs],
        scratch_shapes=[pltpu.SemaphoreType.DMA((3 * n,)), pltpu.SemaphoreType.DMA((3 * n,))],
    )(*pairs)


def _chip_sum(pairs, recv, blk, name):
    _, rows, cols = pairs.shape

    def body(pos_ref, p_ref, r_ref, o_ref):
        o_ref[0] = ((p_ref[0].astype(F32) + r_ref[0].astype(F32)) + r_ref[1].astype(F32)) + r_ref[2].astype(F32)

    pos = jnp.stack([2 * lax.axis_index("x") + lax.axis_index("y"), lax.axis_index("c")]).astype(jnp.int32)
    return pl.pallas_call(
        body, name=name,
        grid_spec=pltpu.PrefetchScalarGridSpec(
            num_scalar_prefetch=1, grid=(rows // blk,),
            in_specs=[pl.BlockSpec((1, blk, cols), lambda i, pos_ref: (pos_ref[0], i, 0)),
                      pl.BlockSpec((3, blk, cols), lambda i, pos_ref: (0, i, 0))],
            out_specs=pl.BlockSpec((1, blk, cols), lambda i, pos_ref: (pos_ref[1], i, 0))),
        out_shape=jax.ShapeDtypeStruct((2, rows, cols), F32),
        compiler_params=_params(("parallel",)),
    )(pos, pairs, recv)


def _share_total(totals):
    n = len(totals)

    def body(*refs):
        t_refs, out_refs, (send_sems, recv_sems) = refs[:n], refs[n:2 * n], refs[2 * n:]
        x, y, c, _ = _mesh_position()
        copies = [pltpu.make_async_remote_copy(src_ref=t_refs[a].at[c], dst_ref=out_refs[a].at[c], send_sem=send_sems.at[a],
                                               recv_sem=recv_sems.at[a], device_id=(x, y, 1 - c), device_id_type=MESH)
                  for a in range(n)]
        for cp in copies:
            cp.start()
        for a in range(n):
            other = out_refs[a].at[1 - c]
            pltpu.make_async_remote_copy(src_ref=other, dst_ref=other, send_sem=send_sems.at[a], recv_sem=recv_sems.at[a],
                                         device_id=(x, y, c), device_id_type=MESH).wait_recv()
        for cp in copies:
            cp.wait_send()

    return pl.pallas_call(
        body, name="share_total", in_specs=[ANY] * n, out_specs=[ANY] * n,
        out_shape=[jax.ShapeDtypeStruct(t.shape, F32) for t in totals],
        scratch_shapes=[pltpu.SemaphoreType.DMA((n,)), pltpu.SemaphoreType.DMA((n,))],
        input_output_aliases={a: a for a in range(n)},
    )(*totals)


def _allreduce_small(block):
    def body(b_ref, out_ref, gath, send_sems, recv_sems):
        x, y, c, _ = _mesh_position()
        me = 4 * x + 2 * y + c
        gath[me] = b_ref[...]
        copies = []
        for k in range(1, 8):
            peer = (x ^ (k >> 2), y ^ ((k >> 1) & 1), c ^ (k & 1))
            copies.append(pltpu.make_async_remote_copy(src_ref=b_ref, dst_ref=gath.at[me], send_sem=send_sems.at[k - 1],
                                                       recv_sem=recv_sems.at[k - 1], device_id=peer, device_id_type=MESH))
        for cp in copies:
            cp.start()
        for k in range(1, 8):
            src = 4 * (x ^ (k >> 2)) + 2 * (y ^ ((k >> 1) & 1)) + (c ^ (k & 1))
            pltpu.make_async_remote_copy(src_ref=b_ref, dst_ref=gath.at[src], send_sem=send_sems.at[k - 1],
                                         recv_sem=recv_sems.at[k - 1], device_id=(x, y, c), device_id_type=MESH).wait_recv()
        for cp in copies:
            cp.wait_send()
        acc = gath[0]
        for d in range(1, 8):
            acc = acc + gath[d]
        out_ref[...] = acc

    vm = pl.BlockSpec(memory_space=pltpu.VMEM)
    return pl.pallas_call(
        body, name="allreduce_small", in_specs=[vm], out_specs=vm,
        out_shape=jax.ShapeDtypeStruct((8, D_MODEL), F32),
        scratch_shapes=[pltpu.VMEM((8, 8, D_MODEL), F32), pltpu.SemaphoreType.DMA((7,)), pltpu.SemaphoreType.DMA((7,))],
    )(block)


def _adamw(w, g, m, v, name):
    rows, cols = w.shape
    tr = 128 if rows % 128 == 0 else rows

    def body(w_ref, g_ref, m_ref, v_ref, d_ref, nm_ref, nv_ref):
        gv = g_ref[...]
        nm = ADAM_B1 * m_ref[...] + (1.0 - ADAM_B1) * gv
        nv = ADAM_B2 * v_ref[...] + (1.0 - ADAM_B2) * (gv * gv)
        m_hat = nm / (1.0 - ADAM_B1 ** ADAM_STEP)
        v_hat = nv / (1.0 - ADAM_B2 ** ADAM_STEP)
        d_ref[...] = -ADAM_LR * (m_hat / (jnp.sqrt(v_hat) + ADAM_EPS) + ADAM_WD * w_ref[...])
        nm_ref[...] = nm
        nv_ref[...] = nv

    spec = pl.BlockSpec((tr, cols), lambda i: (i, 0))
    shape = jax.ShapeDtypeStruct((rows, cols), F32)
    return pl.pallas_call(
        body, name=name, grid=(rows // tr,), in_specs=[spec] * 4, out_specs=[spec] * 3, out_shape=[shape] * 3,
        compiler_params=_params(("parallel",)),
    )(w, g, m, v)


def kernel(x, norm_w, w_in, conv_w, a_log, dt_bias, gdn_norm_w, w_up_a, w_up_b, w_out, final_norm_w, loss_target, m_norm_w, m_w_in, m_conv_w, m_a_log, m_dt_bias, m_gdn_norm_w, m_w_up_a, m_w_up_b, m_w_out, m_final_norm_w, v_norm_w, v_w_in, v_conv_w, v_a_log, v_dt_bias, v_gdn_norm_w, v_w_up_a, v_w_up_b, v_w_out, v_final_norm_w):
    conv_hi = conv_w[0].astype(MXU_DTYPE)
    conv_lo = (conv_w[0] - conv_hi.astype(F32)).astype(MXU_DTYPE)
    big = w_in[0].astype(MXU_DTYPE).reshape(2, *BIG_HALF)
    slab = _pack_slab(w_up_a[0].astype(MXU_DTYPE), w_up_b[0].astype(MXU_DTYPE), w_out[0].astype(MXU_DTYPE), conv_hi, conv_lo)
    own_shard = 2 * lax.axis_index("x") + lax.axis_index("y")
    bigs, slabs = _gather_weights([big, slab])
    bigs = lax.dynamic_update_slice(bigs, big[None], (own_shard, 0, 0, 0)).reshape(SHARDS, D_MODEL, W_IN_SHARD)
    slabs = lax.dynamic_update_slice(slabs, slab[None], (own_shard, 0, 0, 0))
    parts = [_unpack_slab(slabs[s]) for s in range(SHARDS)]
    split = BA_END - (SHARDS - 1) * W_IN_SHARD
    wp = jnp.concatenate([bigs[s] for s in range(SHARDS - 1)]
                         + [bigs[-1][:, :split], jnp.zeros((D_MODEL, SEG_GA - BA_END), MXU_DTYPE), bigs[-1][:, split:]], axis=1)
    w_up_a_full = jnp.concatenate([p[0] for p in parts], axis=1)
    w_up_b_full = jnp.concatenate([p[1] for p in parts], axis=1)
    w_out_full = jnp.concatenate([p[2] for p in parts], axis=0)
    conv_full = jnp.concatenate([p[3].astype(F32) + p[4].astype(F32) for p in parts], axis=1)

    g = _local_step(x[0], loss_target[0], norm_w, wp, conv_full, a_log, dt_bias, gdn_norm_w,
                    w_up_a_full, w_up_b_full, w_out_full, final_norm_w[None])

    d_wp = g["w_in"]
    d_w_in = [d_wp[:, s * W_IN_SHARD:(s + 1) * W_IN_SHARD] for s in range(SHARDS - 1)]
    d_w_in.append(jnp.concatenate([d_wp[:, (SHARDS - 1) * W_IN_SHARD:BA_END], d_wp[:, SEG_GA:]], axis=1))
    zero_conv = jnp.zeros((GDN_CONV, CONV_SHARD), F32)
    grads = [jnp.stack(d_w_in).reshape(SHARDS, 2, *BIG_HALF),
             jnp.stack([_pack_slab(g["w_up_a"][:, s * 256:(s + 1) * 256], g["w_up_b"][:, s * 256:(s + 1) * 256],
                                   g["w_out"][s * 256:(s + 1) * 256], g["conv_w"][:, s * CONV_SHARD:(s + 1) * CONV_SHARD],
                                   zero_conv) for s in range(SHARDS)])]
    blocks, tags = (128, HALF_ROWS), ("w_in", "slab")
    from_sibling = _exchange_halves(grads)
    pairs = [_pair_sum(gr, fs, blk, f"pair_sum_{tag}") for gr, fs, blk, tag in zip(grads, from_sibling, blocks, tags)]
    from_chips = _scatter_pairs(pairs)
    total_big, total_slab = _share_total([_chip_sum(p, fc, blk, f"chip_sum_{tag}")
                                          for p, fc, blk, tag in zip(pairs, from_chips, blocks, tags)])
    g_w_in = total_big.reshape(D_MODEL, W_IN_SHARD)
    g_w_up_a, g_w_up_b, g_w_out, g_conv, _ = _unpack_slab(total_slab)

    row2 = jnp.concatenate([g["gdn_norm_w"], g["a_log"], g["dt_bias"], g["loss"][0:1, 0:1],
                            jnp.zeros((1, D_MODEL - HEAD_DIM - 2 * HEADS - 1), F32)], axis=1)
    small = _allreduce_small(jnp.concatenate([g["norm_w"], g["final_norm_w"], row2, jnp.zeros((5, D_MODEL), F32)], axis=0))
    g_norm, g_final = small[0:1], small[1]
    g_gnw, g_alog, g_dt = small[2:3, 0:HEAD_DIM], small[2:3, HEAD_DIM:HEAD_DIM + HEADS], small[2:3, HEAD_DIM + HEADS:HEAD_DIM + 2 * HEADS]
    loss = small[2, HEAD_DIM + 2 * HEADS]

    names = ["norm_w", "w_in", "conv_w", "a_log", "dt_bias", "gdn_norm_w", "w_up_a", "w_up_b", "w_out", "final_norm_w"]
    weights = dict(zip(names, (norm_w, w_in, conv_w, a_log, dt_bias, gdn_norm_w, w_up_a, w_up_b, w_out, final_norm_w)))
    ms = dict(zip(names, (m_norm_w, m_w_in, m_conv_w, m_a_log, m_dt_bias, m_gdn_norm_w, m_w_up_a, m_w_up_b, m_w_out, m_final_norm_w)))
    vs = dict(zip(names, (v_norm_w, v_w_in, v_conv_w, v_a_log, v_dt_bias, v_gdn_norm_w, v_w_up_a, v_w_up_b, v_w_out, v_final_norm_w)))
    grads2d = dict(norm_w=g_norm, w_in=g_w_in, conv_w=g_conv, a_log=g_alog, dt_bias=g_dt, gdn_norm_w=g_gnw,
                   w_up_a=g_w_up_a, w_up_b=g_w_up_b, w_out=g_w_out, final_norm_w=g_final[None])
    grad_out, delta, new_m, new_v = [], [], [], []
    for n in names:
        shape = weights[n].shape
        two_d = grads2d[n].shape
        d, nm, nv = _adamw(weights[n].reshape(two_d), grads2d[n], ms[n].reshape(two_d), vs[n].reshape(two_d), f"adamw_{n}")
        grad_out.append(grads2d[n].reshape(shape))
        delta.append(d.reshape(shape))
        new_m.append(nm.reshape(shape))
        new_v.append(nv.reshape(shape))
    return (loss, g["grad_x"][None], *grad_out, *delta, *new_m, *new_v)
```

```python
import functools

import jax
import jax.numpy as jnp
from jax import lax
from jax.experimental import pallas as pl
from jax.experimental.pallas import tpu as pltpu

F32 = jnp.float32
MXU_DTYPE = jnp.bfloat16
HIGHEST = lax.Precision.HIGHEST

D_MODEL = 1024
HEADS = 8
HEAD_DIM = 64
WIDTH = HEADS * HEAD_DIM
NORM_EPS = 1e-6
ROPE_THETA = 10000.0
ATT_BLOCK = 128
DILATIONS = (1, 4, 16)
GDN_CHUNK = 64
GDN_CONV = 4
IN_WIDTH = 9232
SEG_A, SEG_ZA, SEG_B, SEG_ZB, SEG_BA, SEG_GA, SEG_GB, PACKED_WIDTH = 0, 4608, 5120, 6656, 7168, 7680, 8704, 9728
BA_END = 7184
VMEM_LIMIT = 56 * 1024 * 1024

ADAM_LR, ADAM_B1, ADAM_B2, ADAM_EPS, ADAM_WD, ADAM_STEP = 0.001, 0.9, 0.999, 1e-08, 0.01, 10

_NN = (((1,), (0,)), ((), ()))
_NT = (((1,), (1,)), ((), ()))
_TN = (((0,), (0,)), ((), ()))


def _params(sem):
    return pltpu.CompilerParams(dimension_semantics=sem, vmem_limit_bytes=VMEM_LIMIT)


def _mxu(a, b, dims):
    return lax.dot_general(a.astype(MXU_DTYPE), b.astype(MXU_DTYPE), dims, preferred_element_type=F32)


def _sigmoid(x):
    return 1.0 / (1.0 + jnp.exp(-x))


def _softplus(x):
    return jnp.maximum(x, 0.0) + jnp.log(1.0 + jnp.exp(-jnp.abs(x)))


def _iota(shape, axis):
    return lax.broadcasted_iota(jnp.int32, shape, axis)


def _matmul(a, b, mode, name, out_dtype=F32, tm=512, tn=512, tk=512, after=None):
    if mode == "nn":
        (m, k), (k2, n) = a.shape, b.shape
    elif mode == "nt":
        (m, k), (n, k2) = a.shape, b.shape
    else:
        (k, m), (k2, n) = a.shape, b.shape
    assert k == k2
    tm, tn, tk = min(tm, m), min(tn, n), min(tk, k)
    assert m % tm == 0 and n % tn == 0 and k % tk == 0
    nk = k // tk
    dims = {"nn": _NN, "nt": _NT, "tn": _TN}[mode]

    assert out_dtype == F32

    def body(a_ref, b_ref, *rest):
        o_ref = rest[-1]
        kk = pl.program_id(2)
        part = _mxu(a_ref[...], b_ref[...], dims)

        @pl.when(kk == 0)
        def _():
            o_ref[...] = part

        @pl.when(kk > 0)
        def _():
            o_ref[...] += part

    a_spec = pl.BlockSpec((tk, tm), lambda i, j, kk: (kk, i)) if mode == "tn" else pl.BlockSpec((tm, tk), lambda i, j, kk: (i, kk))
    b_spec = pl.BlockSpec((tn, tk), lambda i, j, kk: (j, kk)) if mode == "nt" else pl.BlockSpec((tk, tn), lambda i, j, kk: (kk, j))
    extra_specs, extra_args = ([], []) if after is None else ([pl.BlockSpec(memory_space=pl.ANY)], [after])
    return pl.pallas_call(
        body, name=name, grid=(m // tm, n // tn, nk), in_specs=[a_spec, b_spec] + extra_specs,
        out_specs=pl.BlockSpec((tm, tn), lambda i, j, kk: (i, j)),
        out_shape=jax.ShapeDtypeStruct((m, n), out_dtype),
        compiler_params=_params(("parallel", "parallel", "arbitrary")),
    )(a, b, *extra_args)


def _norm_proj(x, norm_w, wp, tm=1024, tn=PACKED_WIDTH // 4):
    t = x.shape[0]
    tm = min(tm, t)

    def body(x_ref, nw_ref, w_ref, proj_ref, ht_ref, h_scr):
        @pl.when(pl.program_id(1) == 0)
        def _():
            xf = x_ref[...]
            r = lax.rsqrt(jnp.mean(xf * xf, axis=-1, keepdims=True) + NORM_EPS)
            h = xf * r * nw_ref[...]
            h_scr[...] = h.astype(h_scr.dtype)
            ht_ref[...] = h.T.astype(ht_ref.dtype)

        proj_ref[...] = jnp.dot(h_scr[...], w_ref[...], preferred_element_type=F32)

    return pl.pallas_call(
        body, name="norm_proj", grid=(t // tm, PACKED_WIDTH // tn),
        in_specs=[pl.BlockSpec((tm, D_MODEL), lambda i, j: (i, 0)),
                  pl.BlockSpec((1, D_MODEL), lambda i, j: (0, 0)),
                  pl.BlockSpec((D_MODEL, tn), lambda i, j: (0, j))],
        out_specs=[pl.BlockSpec((tm, tn), lambda i, j: (i, j)),
                   pl.BlockSpec((D_MODEL, tm), lambda i, j: (0, i))],
        out_shape=[jax.ShapeDtypeStruct((t, PACKED_WIDTH), F32), jax.ShapeDtypeStruct((D_MODEL, t), MXU_DTYPE)],
        scratch_shapes=[pltpu.VMEM((tm, D_MODEL), MXU_DTYPE)],
        compiler_params=_params(("parallel", "arbitrary")),
    )(x, norm_w, wp)


def _rope_tables(t):
    lane = jnp.arange(128)
    inv_freq = ROPE_THETA ** (-jnp.arange(0, HEAD_DIM, 2, dtype=F32) / HEAD_DIM)
    ang = jnp.arange(t, dtype=F32)[:, None] * inv_freq[None, :]
    ang = jnp.concatenate([ang, ang, ang, ang], axis=-1)
    first_half = (lane % HEAD_DIM) < HEAD_DIM // 2
    cos, sin = jnp.cos(ang), jnp.sin(ang)
    return cos, jnp.where(first_half, -sin, 0.0), jnp.where(first_half, 0.0, sin)


def _rope_block(x, cos, sin_lo, sin_hi, sign):
    outs = []
    for c in range(8):
        xc = x[:, c * 128:(c + 1) * 128]
        rot = pltpu.roll(xc, 96, 1) * sin_lo + pltpu.roll(xc, 32, 1) * sin_hi
        outs.append(xc * cos + sign * rot)
    outs.append(x[:, 2 * WIDTH:])
    return jnp.concatenate(outs, axis=1)


def _tile_scratch(tm, cols):
    return pltpu.VMEM((cols // 128, tm, 128), F32)


def _store_tile(scr, y):
    for c in range(scr.shape[0]):
        scr[c] = y[:, c * 128:(c + 1) * 128]


def _load_tile(scr):
    return jnp.concatenate([scr[c] for c in range(scr.shape[0])], axis=1)


def _to_strided_view(scr, o_ref, d):
    n, tm, _ = scr.shape
    for r in range(d):
        for c in range(n):
            o_ref[:, (r * n + c) * 128:(r * n + c + 1) * 128] = scr[c, pl.ds(r, tm // d, stride=d), :].astype(o_ref.dtype)


def _from_strided_view(i_ref, scr, d):
    n, tm, _ = scr.shape
    for r in range(d):
        for c in range(n):
            scr[c, pl.ds(r, tm // d, stride=d), :] = i_ref[:, (r * n + c) * 128:(r * n + c + 1) * 128].astype(F32)


def _strided_spec(tm, d, cols):
    return pl.BlockSpec((tm // d, d * cols), lambda i: (i, 0))


def _rope_fwd(proj, tables, tm=256):
    t = proj.shape[0]
    cols = 3 * WIDTH

    def body(x_ref, c_ref, sl_ref, sh_ref, o0, o1, o2, scr):
        for g, (d, o_ref) in enumerate(zip(DILATIONS, (o0, o1, o2))):
            y = _rope_block(x_ref[:, g * cols:(g + 1) * cols], c_ref[...], sl_ref[...], sh_ref[...], 1.0)
            if d == 1:
                o_ref[...] = y.astype(o_ref.dtype)
            else:
                _store_tile(scr, y)
                _to_strided_view(scr, o_ref, d)

    tab = pl.BlockSpec((tm, 128), lambda i: (i, 0))
    return pl.pallas_call(
        body, name="rope_fwd", grid=(t // tm,),
        in_specs=[pl.BlockSpec((tm, 3 * cols), lambda i: (i, 0)), tab, tab, tab],
        out_specs=[_strided_spec(tm, d, cols) for d in DILATIONS],
        out_shape=[jax.ShapeDtypeStruct((t // d, d * cols), MXU_DTYPE) for d in DILATIONS],
        scratch_shapes=[_tile_scratch(tm, cols)],
        compiler_params=_params(("parallel",)),
    )(proj, *tables)


def _rope_bwd(dproj, dqkvs, tables, tm=256):
    t = dproj.shape[0]
    cols = 3 * WIDTH

    def body(dp_ref, i0, i1, i2, c_ref, sl_ref, sh_ref, o_ref, scr):
        for g, (d, i_ref) in enumerate(zip(DILATIONS, (i0, i1, i2))):
            if d == 1:
                x = i_ref[...]
            else:
                _from_strided_view(i_ref, scr, d)
                x = _load_tile(scr)
            y = _rope_block(x, c_ref[...], sl_ref[...], sh_ref[...], -1.0)
            o_ref[:, g * cols:(g + 1) * cols] = y.astype(o_ref.dtype)

    tab = pl.BlockSpec((tm, 128), lambda i: (i, 0))
    return pl.pallas_call(
        body, name="rope_bwd", grid=(t // tm,),
        in_specs=[pl.BlockSpec(memory_space=pl.ANY)] + [_strided_spec(tm, d, cols) for d in DILATIONS] + [tab, tab, tab],
        out_specs=pl.BlockSpec((tm, 3 * cols), lambda i: (i, 0)),
        out_shape=jax.ShapeDtypeStruct((t, PACKED_WIDTH), MXU_DTYPE),
        scratch_shapes=[_tile_scratch(tm, cols)],
        input_output_aliases={0: 0},
        compiler_params=_params(("parallel",)),
    )(dproj, *dqkvs, *tables)


def _att_masks():
    qi = _iota((ATT_BLOCK, ATT_BLOCK), 0)
    kj = _iota((ATT_BLOCK, ATT_BLOCK), 1)
    return kj <= qi, kj >= qi


def _att_fwd(qkv, d, name):
    rows = qkv.shape[0]
    nb = rows // ATT_BLOCK
    scale = HEAD_DIM ** -0.5

    def body(q_ref, kc_ref, kp_ref, vc_ref, vp_ref, o_ref, lse_ref):
        has_prev = pl.program_id(1) > 0
        m_cur, m_prev = _att_masks()
        m_prev = m_prev & has_prev
        hs = range(HEADS)
        sls = [slice(h * HEAD_DIM, (h + 1) * HEAD_DIM) for h in hs]
        qs = [q_ref[:, sl] for sl in sls]
        s_c = [jnp.where(m_cur, _mxu(qs[h], kc_ref[:, sls[h]], _NT) * scale, -jnp.inf) for h in hs]
        s_p = [jnp.where(m_prev, _mxu(qs[h], kp_ref[:, sls[h]], _NT) * scale, -jnp.inf) for h in hs]
        m = [jnp.maximum(jnp.max(s_c[h], axis=1, keepdims=True), jnp.max(s_p[h], axis=1, keepdims=True)) for h in hs]
        p_c = [jnp.exp(s_c[h] - m[h]) for h in hs]
        p_p = [jnp.exp(s_p[h] - m[h]) for h in hs]
        den = [jnp.sum(p_c[h], axis=1, keepdims=True) + jnp.sum(p_p[h], axis=1, keepdims=True) for h in hs]
        o = [_mxu(p_c[h], vc_ref[:, sls[h]], _NN) + _mxu(p_p[h], vp_ref[:, sls[h]], _NN) for h in hs]
        for h in hs:
            o_ref[:, sls[h]] = o[h] / den[h]
            lse_ref[:, sls[h]] = jnp.broadcast_to(m[h] + jnp.log(den[h]), (ATT_BLOCK, HEAD_DIM))

    def cur(c):
        return pl.BlockSpec((ATT_BLOCK, WIDTH), lambda r, i: (i, 3 * r + c))

    def prev(c):
        return pl.BlockSpec((ATT_BLOCK, WIDTH), lambda r, i: (jnp.maximum(i - 1, 0), 3 * r + c))

    out = pl.BlockSpec((ATT_BLOCK, WIDTH), lambda r, i: (i, r))
    return pl.pallas_call(
        body, name=name, grid=(d, nb), in_specs=[cur(0), cur(1), prev(1), cur(2), prev(2)],
        out_specs=[out, out],
        out_shape=[jax.ShapeDtypeStruct((rows, d * WIDTH), F32)] * 2,
        compiler_params=_params(("parallel", "arbitrary")),
    )(qkv, qkv, qkv, qkv, qkv)


def _att_bwd(qkv, do, lse, delta, d, name):
    rows = qkv.shape[0]
    nb = rows // ATT_BLOCK
    scale = HEAD_DIM ** -0.5

    def body(q_ref, qn_ref, kc_ref, kp_ref, vc_ref, vp_ref, do_ref, don_ref, l_ref, ln_ref, dl_ref, dln_ref, o_ref):
        i = pl.program_id(1)
        m_cur, m_prev = _att_masks()
        m_p = m_prev & (i > 0)
        m_n = m_prev & (i < nb - 1)

        hs = range(HEADS)
        sls = [slice(h * HEAD_DIM, (h + 1) * HEAD_DIM) for h in hs]
        col = [slice(h * HEAD_DIM, h * HEAD_DIM + 1) for h in hs]

        def probs(q_r, k_r, lse_r, mask):
            s = [_mxu(q_r[:, sls[h]], k_r[:, sls[h]], _NT) for h in hs]
            return [jnp.where(mask, jnp.exp(s[h] * scale - lse_r[:, col[h]]), 0.0) for h in hs]

        def dscores(p, do_r, v_r, dl_r):
            dp = [_mxu(do_r[:, sls[h]], v_r[:, sls[h]], _NT) for h in hs]
            return [(p[h] * (dp[h] - dl_r[:, col[h]])).astype(MXU_DTYPE) for h in hs]

        p = probs(q_ref, kc_ref, l_ref, m_cur)
        ds = dscores(p, do_ref, vc_ref, dl_ref)
        dq = [_mxu(ds[h], kc_ref[:, sls[h]], _NN) for h in hs]
        dk = [_mxu(ds[h], q_ref[:, sls[h]], _TN) for h in hs]
        dv = [_mxu(p[h], do_ref[:, sls[h]], _TN) for h in hs]
        p = probs(q_ref, kp_ref, l_ref, m_p)
        ds = dscores(p, do_ref, vp_ref, dl_ref)
        dq = [dq[h] + _mxu(ds[h], kp_ref[:, sls[h]], _NN) for h in hs]
        p = probs(qn_ref, kc_ref, ln_ref, m_n)
        ds = dscores(p, don_ref, vc_ref, dln_ref)
        dk = [dk[h] + _mxu(ds[h], qn_ref[:, sls[h]], _TN) for h in hs]
        dv = [dv[h] + _mxu(p[h], don_ref[:, sls[h]], _TN) for h in hs]
        for h in hs:
            o_ref[:, sls[h]] = dq[h] * scale
            o_ref[:, WIDTH + h * HEAD_DIM:WIDTH + (h + 1) * HEAD_DIM] = dk[h] * scale
            o_ref[:, 2 * WIDTH + h * HEAD_DIM:2 * WIDTH + (h + 1) * HEAD_DIM] = dv[h]

    def qkv_spec(c, shift):
        def idx(r, i):
            return (jnp.clip(i + shift, 0, nb - 1), 3 * r + c)
        return pl.BlockSpec((ATT_BLOCK, WIDTH), idx)

    def tok_spec(shift):
        def idx(r, i):
            return (jnp.clip(i + shift, 0, nb - 1), r)
        return pl.BlockSpec((ATT_BLOCK, WIDTH), idx)

    return pl.pallas_call(
        body, name=name, grid=(d, nb),
        in_specs=[qkv_spec(0, 0), qkv_spec(0, 1), qkv_spec(1, 0), qkv_spec(1, -1), qkv_spec(2, 0), qkv_spec(2, -1),
                  tok_spec(0), tok_spec(1), tok_spec(0), tok_spec(1), tok_spec(0), tok_spec(1)],
        out_specs=pl.BlockSpec((ATT_BLOCK, 3 * WIDTH), lambda r, i: (i, r)),
        out_shape=jax.ShapeDtypeStruct((rows, d * 3 * WIDTH), F32),
        compiler_params=_params(("parallel", "arbitrary")),
    )(qkv, qkv, qkv, qkv, qkv, qkv, do, do, lse, lse, delta, delta)


def _att_merge(os_, lses, proj, tm=256):
    t = proj.shape[0]

    def body(o0, o1, o2, l0, l1, l2, z_ref, oz_ref, o_ref, t0, t1, t2, s_o1, s_l1, s_o2, s_l2, s_t):
        _from_strided_view(o1, s_o1, DILATIONS[1])
        _from_strided_view(l1, s_l1, DILATIONS[1])
        _from_strided_view(o2, s_o2, DILATIONS[2])
        _from_strided_view(l2, s_l2, DILATIONS[2])
        a, b, c = l0[...], _load_tile(s_l1), _load_tile(s_l2)
        m = jnp.maximum(jnp.maximum(a, b), c)
        wa, wb, wc = jnp.exp(a - m), jnp.exp(b - m), jnp.exp(c - m)
        den = wa + wb + wc
        o = (wa * o0[...] + wb * _load_tile(s_o1) + wc * _load_tile(s_o2)) / den
        z = z_ref[...]
        o_ref[...] = o
        oz_ref[...] = (o * z * _sigmoid(z)).astype(oz_ref.dtype)
        total = m + jnp.log(den)
        t0[...] = total
        _store_tile(s_t, total)
        _to_strided_view(s_t, t1, DILATIONS[1])
        _to_strided_view(s_t, t2, DILATIONS[2])

    tok = pl.BlockSpec((tm, WIDTH), lambda i: (i, 0))
    views = [_strided_spec(tm, d, WIDTH) for d in DILATIONS]
    view_shapes = [jax.ShapeDtypeStruct((t // d, d * WIDTH), F32) for d in DILATIONS]
    return pl.pallas_call(
        body, name="att_merge", grid=(t // tm,),
        in_specs=views + views + [pl.BlockSpec((tm, WIDTH), lambda i: (i, SEG_ZA // WIDTH))],
        out_specs=[tok, tok] + views,
        out_shape=[jax.ShapeDtypeStruct((t, WIDTH), MXU_DTYPE), jax.ShapeDtypeStruct((t, WIDTH), F32)] + view_shapes,
        scratch_shapes=[_tile_scratch(tm, WIDTH)] * 5,
        compiler_params=_params(("parallel",)),
    )(*os_, *lses, proj)


def _att_merge_bwd(dproj, d_oz, o, proj, tm=256):
    t = proj.shape[0]

    def body(dp_ref, doz_ref, o_ref, z_ref, dz_ref, do0, do1, do2, dl0, dl1, dl2, s_do, s_dl):
        z, ov, g = z_ref[...], o_ref[...], doz_ref[...]
        sg = _sigmoid(z)
        do = g * z * sg
        dz_ref[...] = (g * ov * sg * (1.0 + z * (1.0 - sg))).astype(dz_ref.dtype)
        do0[...] = do.astype(do0.dtype)
        _store_tile(s_do, do)
        prod = do * ov
        for h in range(HEADS):
            sl = slice(h * HEAD_DIM, (h + 1) * HEAD_DIM)
            half = slice((h % 2) * HEAD_DIM, (h % 2 + 1) * HEAD_DIM)
            s_dl[h // 2, :, half] = jnp.broadcast_to(jnp.sum(prod[:, sl], axis=1, keepdims=True), (tm, HEAD_DIM))
        dl0[...] = _load_tile(s_dl)
        for d, do_v, dl_v in ((DILATIONS[1], do1, dl1), (DILATIONS[2], do2, dl2)):
            _to_strided_view(s_do, do_v, d)
            _to_strided_view(s_dl, dl_v, d)

    tok = pl.BlockSpec((tm, WIDTH), lambda i: (i, 0))
    seg = pl.BlockSpec((tm, WIDTH), lambda i: (i, SEG_ZA // WIDTH))
    views = [_strided_spec(tm, d, WIDTH) for d in DILATIONS]
    return pl.pallas_call(
        body, name="att_merge_bwd", grid=(t // tm,),
        in_specs=[pl.BlockSpec(memory_space=pl.ANY), tok, tok, seg],
        out_specs=[seg] + views + views,
        out_shape=[jax.ShapeDtypeStruct((t, PACKED_WIDTH), MXU_DTYPE)]
        + [jax.ShapeDtypeStruct((t // d, d * WIDTH), MXU_DTYPE) for d in DILATIONS]
        + [jax.ShapeDtypeStruct((t // d, d * WIDTH), F32) for d in DILATIONS],
        scratch_shapes=[_tile_scratch(tm, WIDTH)] * 2,
        input_output_aliases={0: 0},
        compiler_params=_params(("parallel",)),
    )(dproj, d_oz, o, proj)


def _shift_down(x, halo, s):
    if s == 0:
        return x
    xs = pltpu.roll(x, s, 0)
    head = jnp.where(_iota((8, x.shape[1]), 0) < s, pltpu.roll(halo, s, 0), xs[0:8])
    return jnp.concatenate([head, xs[8:]], axis=0)


def _shift_up(x, nxt, s):
    if s == 0:
        return x
    n = x.shape[0]
    xs = pltpu.roll(x, n - s, 0)
    tail = jnp.where(_iota((8, x.shape[1]), 0) >= 8 - s, pltpu.roll(nxt, 8 - s, 0), xs[n - 8:])
    return jnp.concatenate([xs[:n - 8], tail], axis=0)


def _conv_fwd(proj, conv_w, tm=256):
    t = proj.shape[0]
    cb = SEG_B // WIDTH

    def body(x_ref, halo_ref, w_ref, c_ref):
        halo = jnp.where(pl.program_id(0) > 0, halo_ref[...], 0.0)
        x = x_ref[...]
        w = w_ref[...]
        acc = jnp.zeros((tm, WIDTH), F32)
        for j in range(GDN_CONV):
            acc += _shift_down(x, halo, GDN_CONV - 1 - j) * w[j:j + 1, :]
        c_ref[...] = acc

    return pl.pallas_call(
        body, name="conv_fwd", grid=(t // tm, 3),
        in_specs=[pl.BlockSpec((tm, WIDTH), lambda i, c: (i, cb + c)),
                  pl.BlockSpec((8, WIDTH), lambda i, c: (jnp.maximum(i * (tm // 8) - 1, 0), cb + c)),
                  pl.BlockSpec((GDN_CONV, WIDTH), lambda i, c: (0, c))],
        out_specs=pl.BlockSpec((tm, WIDTH), lambda i, c: (i, c)),
        out_shape=jax.ShapeDtypeStruct((t, 3 * WIDTH), F32),
        compiler_params=_params(("parallel", "parallel")),
    )(proj, proj, conv_w)


def _conv_bwd(dproj, dc, proj, conv_w, tm=256):
    t = proj.shape[0]
    cb = SEG_B // WIDTH
    nt = t // tm

    def body(dp_ref, dc_ref, dcn_ref, x_ref, halo_ref, w_ref, dx_ref, dw_ref):
        i = pl.program_id(1)
        w = w_ref[...]
        dcn = jnp.where(i < nt - 1, dcn_ref[...], 0.0)
        dcv = dc_ref[...]
        acc = jnp.zeros((tm, WIDTH), F32)
        for j in range(GDN_CONV):
            acc += _shift_up(dcv, dcn, GDN_CONV - 1 - j) * w[j:j + 1, :]
        dx_ref[...] = acc.astype(dx_ref.dtype)
        halo = jnp.where(i > 0, halo_ref[...], 0.0)
        x = x_ref[...]
        row8 = _iota((8, WIDTH), 0)
        part = jnp.zeros((8, WIDTH), F32)
        for j in range(GDN_CONV):
            s = jnp.sum(dcv * _shift_down(x, halo, GDN_CONV - 1 - j), axis=0, keepdims=True)
            part += jnp.where(row8 == j, s, 0.0)

        @pl.when(i == 0)
        def _():
            dw_ref[...] = jnp.zeros_like(dw_ref)

        dw_ref[...] += part

    return pl.pallas_call(
        body, name="conv_bwd", grid=(3, nt),
        in_specs=[pl.BlockSpec(memory_space=pl.ANY),
                  pl.BlockSpec((tm, WIDTH), lambda c, i: (i, c)),
                  pl.BlockSpec((8, WIDTH), lambda c, i: (jnp.minimum((i + 1) * (tm // 8), t // 8 - 1), c)),
                  pl.BlockSpec((tm, WIDTH), lambda c, i: (i, cb + c)),
                  pl.BlockSpec((8, WIDTH), lambda c, i: (jnp.maximum(i * (tm // 8) - 1, 0), cb + c)),
                  pl.BlockSpec((GDN_CONV, WIDTH), lambda c, i: (0, c))],
        out_specs=[pl.BlockSpec((tm, WIDTH), lambda c, i: (i, cb + c)),
                   pl.BlockSpec((8, WIDTH), lambda c, i: (0, c))],
        out_shape=[jax.ShapeDtypeStruct((t, PACKED_WIDTH), MXU_DTYPE), jax.ShapeDtypeStruct((8, 3 * WIDTH), F32)],
        input_output_aliases={0: 0},
        compiler_params=_params(("parallel", "arbitrary")),
    )(dproj, dc, dc, proj, proj, conv_w)


def _chunk_matrices(tm):
    r, c = _iota((tm, tm), 0), _iota((tm, tm), 1)
    same = (r // GDN_CHUNK) == (c // GDN_CHUNK)
    return jnp.where(same & (c <= r), 1.0, 0.0), jnp.where(same, 1.0, 0.0)


def _gdn_gates(ba, a_log, dt_bias):
    al = ba + dt_bias
    return _sigmoid(ba), -jnp.exp(a_log) * _softplus(al), _sigmoid(al)


def _head_lane_eye():
    return jnp.where(_iota((HEADS, 128), 1) == _iota((HEADS, 128), 0) + HEADS, 1.0, 0.0)


def _gdn_prep(conv, proj, a_log, dt_bias, tm=256):
    t = proj.shape[0]
    nc = tm // GDN_CHUNK

    def body(c_ref, ba_ref, al_ref, dt_ref, q_ref, k_ref, v_ref, b_ref, g_ref, gl_ref, grow_ref):
        c = c_ref[...]
        a = c * _sigmoid(c)
        beta, g, _ = _gdn_gates(ba_ref[:, 0:128], al_ref[...], dt_ref[...])
        lmat, cmat = _chunk_matrices(tm)
        gc = jnp.dot(lmat, g, precision=HIGHEST, preferred_element_type=F32)
        gl = jnp.dot(cmat, g, precision=HIGHEST, preferred_element_type=F32)
        grow = lax.dot_general(_head_lane_eye(), gc, _NT, precision=HIGHEST, preferred_element_type=F32)
        for h in range(HEADS):
            sl = slice(h * HEAD_DIM, (h + 1) * HEAD_DIM)
            qh, kh, vh = a[:, sl], a[:, WIDTH + h * HEAD_DIM:WIDTH + (h + 1) * HEAD_DIM], a[:, 2 * WIDTH + h * HEAD_DIM:2 * WIDTH + (h + 1) * HEAD_DIM]
            rq = lax.rsqrt(jnp.sum(qh * qh, axis=1, keepdims=True) + NORM_EPS)
            rk = lax.rsqrt(jnp.sum(kh * kh, axis=1, keepdims=True) + NORM_EPS)
            q_ref[h] = qh * (rq * HEAD_DIM ** -0.5)
            k_ref[h] = kh * rk
            v_ref[h] = vh
            b_ref[h] = jnp.broadcast_to(beta[:, h:h + 1], (tm, HEAD_DIM))
            g_ref[h] = jnp.broadcast_to(gc[:, HEADS + h:HEADS + h + 1], (tm, HEAD_DIM))
            gl_ref[h] = jnp.broadcast_to(gl[:, HEADS + h:HEADS + h + 1], (tm, HEAD_DIM))
            for cc in range(nc):
                grow_ref[h, cc] = grow[h:h + 1, cc * GDN_CHUNK:(cc + 1) * GDN_CHUNK]

    hm = pl.BlockSpec((HEADS, tm, HEAD_DIM), lambda i: (0, i, 0))
    small = pl.BlockSpec((1, 128), lambda i: (0, 0))
    hm_shape = jax.ShapeDtypeStruct((HEADS, t, HEAD_DIM), F32)
    return pl.pallas_call(
        body, name="gdn_prep", grid=(t // tm,),
        in_specs=[pl.BlockSpec((tm, 3 * WIDTH), lambda i: (i, 0)),
                  pl.BlockSpec((tm, WIDTH), lambda i: (i, SEG_BA // WIDTH)), small, small],
        out_specs=[hm] * 6 + [pl.BlockSpec((HEADS, nc, 1, GDN_CHUNK), lambda i: (0, i, 0, 0))],
        out_shape=[hm_shape] * 6 + [jax.ShapeDtypeStruct((HEADS, t // GDN_CHUNK, 1, GDN_CHUNK), F32)],
        compiler_params=_params(("parallel",)),
    )(conv, proj, a_log, dt_bias)


def _gdn_prep_bwd(dproj, conv, proj, a_log, dt_bias, dq, dk, dv, db, dg, dgl, dgrow, tm=256):
    t = proj.shape[0]
    nc = tm // GDN_CHUNK

    def body(dp_ref, c_ref, ba_ref, al_ref, dt_ref, dq_ref, dk_ref, dv_ref, db_ref, dg_ref, dgl_ref, dgrow_ref,
             dba_ref, dc_ref, small_ref, row_scr):
        c = c_ref[...]
        sg = _sigmoid(c)
        a = c * sg
        beta, g, sig_al = _gdn_gates(ba_ref[:, 0:128], al_ref[...], dt_ref[...])
        lane = _iota((1, 128), 1)
        d_beta = jnp.zeros((tm, 128), F32)
        d_gc = jnp.zeros((tm, 128), F32)
        d_gl = jnp.zeros((tm, 128), F32)
        for h in range(HEADS):
            for cc in range(nc):
                row_scr[h:h + 1, cc * GDN_CHUNK:(cc + 1) * GDN_CHUNK] = dgrow_ref[h, cc]
            d_beta += jnp.sum(db_ref[h], axis=1, keepdims=True) * jnp.where(lane == h, 1.0, 0.0)
            d_gc += jnp.sum(dg_ref[h], axis=1, keepdims=True) * jnp.where(lane == HEADS + h, 1.0, 0.0)
            d_gl += jnp.sum(dgl_ref[h], axis=1, keepdims=True) * jnp.where(lane == HEADS + h, 1.0, 0.0)
            qs, ks, vs = (slice(h * HEAD_DIM, (h + 1) * HEAD_DIM), slice(WIDTH + h * HEAD_DIM, WIDTH + (h + 1) * HEAD_DIM),
                          slice(2 * WIDTH + h * HEAD_DIM, 2 * WIDTH + (h + 1) * HEAD_DIM))
            qh, kh = a[:, qs], a[:, ks]
            rq = lax.rsqrt(jnp.sum(qh * qh, axis=1, keepdims=True) + NORM_EPS)
            rk = lax.rsqrt(jnp.sum(kh * kh, axis=1, keepdims=True) + NORM_EPS)
            gq, gk = dq_ref[h] * HEAD_DIM ** -0.5, dk_ref[h]
            da_q = rq * gq - qh * (rq * rq * rq) * jnp.sum(gq * qh, axis=1, keepdims=True)
            da_k = rk * gk - kh * (rk * rk * rk) * jnp.sum(gk * kh, axis=1, keepdims=True)
            dsilu = lambda s, x: s * (1.0 + x * (1.0 - s))
            dc_ref[:, qs] = da_q * dsilu(sg[:, qs], c[:, qs])
            dc_ref[:, ks] = da_k * dsilu(sg[:, ks], c[:, ks])
            dc_ref[:, vs] = dv_ref[h] * dsilu(sg[:, vs], c[:, vs])
        d_gc += lax.dot_general(row_scr[...], _head_lane_eye(), _TN, precision=HIGHEST, preferred_element_type=F32)
        lmat, cmat = _chunk_matrices(tm)
        d_g = (lax.dot_general(lmat, d_gc, _TN, precision=HIGHEST, preferred_element_type=F32)
               + lax.dot_general(cmat, d_gl, _TN, precision=HIGHEST, preferred_element_type=F32))
        d_al = d_g * (-jnp.exp(al_ref[...])) * sig_al
        d_bl = d_beta * beta * (1.0 - beta)
        dba_ref[...] = jnp.concatenate([d_bl + d_al, jnp.zeros((tm, WIDTH - 128), F32)], axis=1).astype(dba_ref.dtype)
        row8 = _iota((8, 128), 0)
        part = (jnp.where(row8 == 0, jnp.sum(d_g * g, axis=0, keepdims=True), 0.0)
                + jnp.where(row8 == 1, jnp.sum(d_al, axis=0, keepdims=True), 0.0))

        @pl.when(pl.program_id(0) == 0)
        def _():
            small_ref[...] = jnp.zeros_like(small_ref)

        small_ref[...] += part

    hm = pl.BlockSpec((HEADS, tm, HEAD_DIM), lambda i: (0, i, 0))
    small = pl.BlockSpec((1, 128), lambda i: (0, 0))
    seg = pl.BlockSpec((tm, WIDTH), lambda i: (i, SEG_BA // WIDTH))
    return pl.pallas_call(
        body, name="gdn_prep_bwd", grid=(t // tm,),
        in_specs=[pl.BlockSpec(memory_space=pl.ANY), pl.BlockSpec((tm, 3 * WIDTH), lambda i: (i, 0)), seg, small, small]
        + [hm] * 6 + [pl.BlockSpec((HEADS, nc, 1, GDN_CHUNK), lambda i: (0, i, 0, 0))],
        out_specs=[seg, pl.BlockSpec((tm, 3 * WIDTH), lambda i: (i, 0)), pl.BlockSpec((8, 128), lambda i: (0, 0))],
        out_shape=[jax.ShapeDtypeStruct((t, PACKED_WIDTH), MXU_DTYPE), jax.ShapeDtypeStruct((t, 3 * WIDTH), F32),
                   jax.ShapeDtypeStruct((8, 128), F32)],
        scratch_shapes=[pltpu.VMEM((HEADS, tm), F32)],
        input_output_aliases={0: 0},
        compiler_params=_params(("arbitrary",)),
    )(dproj, conv, proj, a_log, dt_bias, dq, dk, dv, db, dg, dgl, dgrow)


_BNN = (((2,), (1,)), ((0,), (0,)))
_BNT = (((2,), (2,)), ((0,), (0,)))
_BTN = (((1,), (1,)), ((0,), (0,)))


@jax.custom_vjp
def _MM_NN(a, b):
    return _mxu(a, b, _BNN)


@jax.custom_vjp
def _MM_NT(a, b):
    return _mxu(a, b, _BNT)


@jax.custom_vjp
def _MM_TN(a, b):
    return _mxu(a, b, _BTN)


_MM_NN.defvjp(lambda a, b: (_mxu(a, b, _BNN), (a, b)), lambda r, g: (_mxu(g, r[1], _BNT), _mxu(r[0], g, _BTN)))
_MM_NT.defvjp(lambda a, b: (_mxu(a, b, _BNT), (a, b)), lambda r, g: (_mxu(g, r[1], _BNN), _mxu(g, r[0], _BTN)))
_MM_TN.defvjp(lambda a, b: (_mxu(a, b, _BTN), (a, b)), lambda r, g: (_mxu(r[1], g, _BNT), _mxu(r[0], g, _BNN)))


def _split(a):
    hi = a.astype(MXU_DTYPE)
    return hi, (a - hi.astype(F32)).astype(MXU_DTYPE)


def _dot3(a, b, dims):
    (ah, al), (bh, bl) = a, b
    (ca,), (cb,) = dims[0]
    return lax.dot_general(jnp.concatenate([ah, ah, al], axis=ca), jnp.concatenate([bh, bl, bh], axis=cb), dims,
                           preferred_element_type=F32)


def _unit_lower_inverse(a):
    c = GDN_CHUNK
    eye = jnp.where(_iota((c, c), 0) == _iota((c, c), 1), 1.0, 0.0)
    x = eye - a
    p = a
    for _ in range(5):
        ps = _split(p)
        p = _dot3(ps, ps, _BNN)
        x = x + _dot3(_split(x), _split(p), _BNN)
    return x


@jax.custom_vjp
def _SAVED_INVERSE(a, t_inv):
    return t_inv


def _saved_inverse_bwd(t_inv, g):
    ts = _split(t_inv)
    return -_dot3(ts, _split(_dot3(_split(g), ts, _BNT)), _BTN), jnp.zeros_like(t_inv)


_SAVED_INVERSE.defvjp(lambda a, t_inv: (t_inv, t_inv), _saved_inverse_bwd)


def _gdn_chunk(q, k, v, beta, g1, g2, gl, state, t_inv=None):
    c = GDN_CHUNK
    if t_inv is None:
        _mm_nn, _mm_nt, _mm_tn = (functools.partial(_mxu, dims=dd) for dd in (_BNN, _BNT, _BTN))
    else:
        _mm_nn, _mm_nt, _mm_tn = _MM_NN, _MM_NT, _MM_TN
    row, col = _iota((c, c), 0), _iota((c, c), 1)
    incl, strict = row >= col, row > col
    decay = jnp.where(incl, jnp.exp(jnp.where(incl, g1 - g2, 0.0)), 0.0)
    eg = jnp.exp(g1)
    kb = k * beta
    a = _mm_nt(kb, k) * jnp.where(strict, decay, 0.0)
    inv = _unit_lower_inverse(a) if t_inv is None else _SAVED_INVERSE(a, t_inv)
    u = _mm_nn(inv, v * beta)
    w = _mm_nn(inv, kb * eg)
    attn = _mm_nt(q, k) * decay
    v_new = u - _mm_nn(w, state)
    o = _mm_nn(q * eg, state) + _mm_nn(attn, v_new)
    new_state = state * jnp.exp(gl) + _mm_tn(k * jnp.exp(gl - g1), v_new)
    return (o, new_state, inv) if t_inv is None else (o, new_state)


def _gdn_fwd(q, k, v, beta, g, gl, grow, cpb=2):
    t = q.shape[1]
    rows = cpb * GDN_CHUNK

    def body(q_ref, k_ref, v_ref, b_ref, g_ref, gl_ref, grow_ref, o_ref, st_ref, inv_ref, state):
        @pl.when(pl.program_id(0) == 0)
        def _():
            state[...] = jnp.zeros_like(state)

        s = state[...]
        for cc in range(cpb):
            sl = slice(cc * GDN_CHUNK, (cc + 1) * GDN_CHUNK)
            st_ref[:, cc] = s
            g2 = jnp.broadcast_to(grow_ref[:, cc], (HEADS, GDN_CHUNK, GDN_CHUNK))
            o, s, inv = _gdn_chunk(q_ref[:, sl, :], k_ref[:, sl, :], v_ref[:, sl, :], b_ref[:, sl, :], g_ref[:, sl, :], g2,
                                   gl_ref[:, sl, :], s)
            o_ref[:, sl, :] = o
            inv_ref[:, cc] = inv
        state[...] = s

    hm = pl.BlockSpec((HEADS, rows, HEAD_DIM), lambda i: (0, i, 0))
    per_chunk = pl.BlockSpec((HEADS, cpb, GDN_CHUNK, HEAD_DIM), lambda i: (0, i, 0, 0))
    chunk_shape = jax.ShapeDtypeStruct((HEADS, t // GDN_CHUNK, GDN_CHUNK, HEAD_DIM), F32)
    return pl.pallas_call(
        body, name="gdn_fwd", grid=(t // rows,),
        in_specs=[hm] * 6 + [pl.BlockSpec((HEADS, cpb, 1, GDN_CHUNK), lambda i: (0, i, 0, 0))],
        out_specs=[hm, per_chunk, per_chunk],
        out_shape=[jax.ShapeDtypeStruct((HEADS, t, HEAD_DIM), F32), chunk_shape, chunk_shape],
        scratch_shapes=[pltpu.VMEM((HEADS, GDN_CHUNK, HEAD_DIM), F32)],
        compiler_params=_params(("arbitrary",)),
    )(q, k, v, beta, g, gl, grow)


def _gdn_bwd(q, k, v, beta, g, gl, grow, states, invs, do, cpb=1):
    t = q.shape[1]
    rows = cpb * GDN_CHUNK
    nsteps = t // rows

    def body(q_ref, k_ref, v_ref, b_ref, g_ref, gl_ref, grow_ref, st_ref, inv_ref, do_ref,
             dq_ref, dk_ref, dv_ref, db_ref, dg_ref, dgl_ref, dgrow_ref, dstate):
        @pl.when(pl.program_id(0) == 0)
        def _():
            dstate[...] = jnp.zeros_like(dstate)

        ds = dstate[...]
        for cc in reversed(range(cpb)):
            sl = slice(cc * GDN_CHUNK, (cc + 1) * GDN_CHUNK)
            g2 = jnp.broadcast_to(grow_ref[:, cc], (HEADS, GDN_CHUNK, GDN_CHUNK))
            _, vjp = jax.vjp(_gdn_chunk, q_ref[:, sl, :], k_ref[:, sl, :], v_ref[:, sl, :], b_ref[:, sl, :],
                             g_ref[:, sl, :], g2, gl_ref[:, sl, :], st_ref[:, cc], inv_ref[:, cc])
            gq, gk, gv, gb, gg1, gg2, ggl, ds, _ = vjp((do_ref[:, sl, :], ds))
            dq_ref[:, sl, :] = gq
            dk_ref[:, sl, :] = gk
            dv_ref[:, sl, :] = gv
            db_ref[:, sl, :] = gb
            dg_ref[:, sl, :] = gg1
            dgl_ref[:, sl, :] = ggl
            dgrow_ref[:, cc] = jnp.sum(gg2, axis=1, keepdims=True)
        dstate[...] = ds

    hm = pl.BlockSpec((HEADS, rows, HEAD_DIM), lambda i: (0, nsteps - 1 - i, 0))
    rowspec = pl.BlockSpec((HEADS, cpb, 1, GDN_CHUNK), lambda i: (0, nsteps - 1 - i, 0, 0))
    per_chunk = pl.BlockSpec((HEADS, cpb, GDN_CHUNK, HEAD_DIM), lambda i: (0, nsteps - 1 - i, 0, 0))
    hm_shape = jax.ShapeDtypeStruct((HEADS, t, HEAD_DIM), F32)
    return pl.pallas_call(
        body, name="gdn_bwd", grid=(nsteps,),
        in_specs=[hm] * 6 + [rowspec, per_chunk, per_chunk, hm],
        out_specs=[hm] * 6 + [rowspec],
        out_shape=[hm_shape] * 6 + [jax.ShapeDtypeStruct((HEADS, t // GDN_CHUNK, 1, GDN_CHUNK), F32)],
        scratch_shapes=[pltpu.VMEM((HEADS, GDN_CHUNK, HEAD_DIM), F32)],
        compiler_params=_params(("arbitrary",)),
    )(q, k, v, beta, g, gl, grow, states, invs, do)


def _gdn_out(o_hm, gdn_norm_w, proj, tm=256):
    t = proj.shape[0]

    def body(o_ref, w_ref, z_ref, oz_ref):
        z = z_ref[...]
        gate = z * _sigmoid(z)
        w = w_ref[...]
        for h in range(HEADS):
            sl = slice(h * HEAD_DIM, (h + 1) * HEAD_DIM)
            o = o_ref[h]
            r = lax.rsqrt(jnp.mean(o * o, axis=1, keepdims=True) + NORM_EPS)
            oz_ref[:, sl] = (o * r * w * gate[:, sl]).astype(oz_ref.dtype)

    tok = pl.BlockSpec((tm, WIDTH), lambda i: (i, 0))
    return pl.pallas_call(
        body, name="gdn_out", grid=(t // tm,),
        in_specs=[pl.BlockSpec((HEADS, tm, HEAD_DIM), lambda i: (0, i, 0)), pl.BlockSpec((1, HEAD_DIM), lambda i: (0, 0)),
                  pl.BlockSpec((tm, WIDTH), lambda i: (i, SEG_ZB // WIDTH))],
        out_specs=tok, out_shape=jax.ShapeDtypeStruct((t, WIDTH), MXU_DTYPE),
        compiler_params=_params(("parallel",)),
    )(o_hm, gdn_norm_w, proj)


def _gdn_out_bwd(dproj, d_oz, o_hm, gdn_norm_w, proj, tm=256):
    t = proj.shape[0]

    def body(dp_ref, doz_ref, o_ref, w_ref, z_ref, dz_ref, do_ref, dw_ref):
        z = z_ref[...]
        sg = _sigmoid(z)
        gate = z * sg
        dgate = sg * (1.0 + z * (1.0 - sg))
        w = w_ref[...]
        dw = jnp.zeros((1, HEAD_DIM), F32)
        for h in range(HEADS):
            sl = slice(h * HEAD_DIM, (h + 1) * HEAD_DIM)
            o = o_ref[h]
            g = doz_ref[:, sl]
            r = lax.rsqrt(jnp.mean(o * o, axis=1, keepdims=True) + NORM_EPS)
            on = o * r * w
            dz_ref[:, sl] = (g * on * dgate[:, sl]).astype(dz_ref.dtype)
            dn = g * gate[:, sl]
            dw += jnp.sum(dn * o * r, axis=0, keepdims=True)
            dnw = dn * w
            do_ref[h] = r * dnw - o * (r * r * r) * jnp.mean(dnw * o, axis=1, keepdims=True)

        @pl.when(pl.program_id(0) == 0)
        def _():
            dw_ref[...] = jnp.zeros_like(dw_ref)

        dw_ref[...] += jnp.concatenate([dw, jnp.zeros((7, HEAD_DIM), F32)], axis=0)

    tok = pl.BlockSpec((tm, WIDTH), lambda i: (i, 0))
    seg = pl.BlockSpec((tm, WIDTH), lambda i: (i, SEG_ZB // WIDTH))
    hm = pl.BlockSpec((HEADS, tm, HEAD_DIM), lambda i: (0, i, 0))
    return pl.pallas_call(
        body, name="gdn_out_bwd", grid=(t // tm,),
        in_specs=[pl.BlockSpec(memory_space=pl.ANY), tok, hm, pl.BlockSpec((1, HEAD_DIM), lambda i: (0, 0)), seg],
        out_specs=[seg, hm, pl.BlockSpec((8, HEAD_DIM), lambda i: (0, 0))],
        out_shape=[jax.ShapeDtypeStruct((t, PACKED_WIDTH), MXU_DTYPE), jax.ShapeDtypeStruct((HEADS, t, HEAD_DIM), F32),
                   jax.ShapeDtypeStruct((8, HEAD_DIM), F32)],
        input_output_aliases={0: 0},
        compiler_params=_params(("arbitrary",)),
    )(dproj, d_oz, o_hm, gdn_norm_w, proj)


def _merge(y_a, y_b, proj, tm=256):
    t = proj.shape[0]

    def body(ya_ref, yb_ref, ga_ref, gb_ref, m_ref):
        m_ref[...] = (_sigmoid(ga_ref[...]) * ya_ref[...] + _sigmoid(gb_ref[...]) * yb_ref[...]).astype(m_ref.dtype)

    half = pl.BlockSpec((tm, WIDTH), lambda i, c: (i, c))
    return pl.pallas_call(
        body, name="merge", grid=(t // tm, 2),
        in_specs=[half, half, pl.BlockSpec((tm, WIDTH), lambda i, c: (i, SEG_GA // WIDTH + c)),
                  pl.BlockSpec((tm, WIDTH), lambda i, c: (i, SEG_GB // WIDTH + c))],
        out_specs=half, out_shape=jax.ShapeDtypeStruct((t, D_MODEL), MXU_DTYPE),
        compiler_params=_params(("parallel", "parallel")),
    )(y_a, y_b, proj, proj)


def _merge_bwd(dproj, d_m, y, proj, seg, name, tm=256):
    t = proj.shape[0]

    def body(*refs):
        dm_ref, y_ref, g_ref, dg_ref, dy_ref = refs[-5:]
        dm = dm_ref[...]
        s = _sigmoid(g_ref[...])
        dy_ref[...] = (dm * s).astype(dy_ref.dtype)
        dg_ref[...] = (dm * y_ref[...] * s * (1.0 - s)).astype(dg_ref.dtype)

    half = pl.BlockSpec((tm, WIDTH), lambda i, c: (i, c))
    gate = pl.BlockSpec((tm, WIDTH), lambda i, c: (i, seg // WIDTH + c))
    specs, args, aliases = [half, half, gate], [d_m, y, proj], {}
    if dproj is not None:
        specs, args, aliases = [pl.BlockSpec(memory_space=pl.ANY)] + specs, [dproj] + args, {0: 0}
    return pl.pallas_call(
        body, name=name, grid=(t // tm, 2), in_specs=specs, out_specs=[gate, half],
        out_shape=[jax.ShapeDtypeStruct((t, PACKED_WIDTH), MXU_DTYPE), jax.ShapeDtypeStruct((t, D_MODEL), MXU_DTYPE)],
        input_output_aliases=aliases,
        compiler_params=_params(("parallel", "parallel")),
    )(*args)


def _tail(x, mo, final_w, target, tm=256):
    t = x.shape[0]

    def body(x_ref, mo_ref, w_ref, t_ref, dxm_ref, dx_ref, loss_ref, dw_ref):
        x2 = x_ref[...] + mo_ref[...]
        w = w_ref[...]
        r = lax.rsqrt(jnp.mean(x2 * x2, axis=-1, keepdims=True) + NORM_EPS)
        xn = x2 * r
        err = xn * w - t_ref[...]
        dy = err * (1.0 / D_MODEL)
        dyw = dy * w
        dx2 = r * dyw - x2 * (r * r * r) * jnp.mean(dyw * x2, axis=-1, keepdims=True)
        dx_ref[...] = dx2
        dxm_ref[...] = dx2.astype(dxm_ref.dtype)
        loss = 0.5 * jnp.sum(jnp.sum(err * err, axis=-1, keepdims=True) * (1.0 / D_MODEL), axis=0, keepdims=True)
        onehot = jnp.where((_iota((8, 128), 0) == 0) & (_iota((8, 128), 1) == 0), 1.0, 0.0)

        @pl.when(pl.program_id(0) == 0)
        def _():
            loss_ref[...] = jnp.zeros_like(loss_ref)
            dw_ref[...] = jnp.zeros_like(dw_ref)

        loss_ref[...] += loss * onehot
        dw_ref[...] += jnp.where(_iota((8, D_MODEL), 0) == 0, jnp.sum(dy * xn, axis=0, keepdims=True), 0.0)

    tok = pl.BlockSpec((tm, D_MODEL), lambda i: (i, 0))
    return pl.pallas_call(
        body, name="tail", grid=(t // tm,),
        in_specs=[tok, tok, pl.BlockSpec((1, D_MODEL), lambda i: (0, 0)), tok],
        out_specs=[tok, tok, pl.BlockSpec((8, 128), lambda i: (0, 0)), pl.BlockSpec((8, D_MODEL), lambda i: (0, 0))],
        out_shape=[jax.ShapeDtypeStruct((t, D_MODEL), MXU_DTYPE), jax.ShapeDtypeStruct((t, D_MODEL), F32),
                   jax.ShapeDtypeStruct((8, 128), F32), jax.ShapeDtypeStruct((8, D_MODEL), F32)],
        compiler_params=_params(("arbitrary",)),
    )(x, mo, final_w, target)


def _norm_bwd(x, norm_w, dh, dx2, tm=256):
    t = x.shape[0]

    def body(x_ref, w_ref, dh_ref, dx2_ref, dx_ref, dw_ref):
        xf, w, dh_ = x_ref[...], w_ref[...], dh_ref[...]
        r = lax.rsqrt(jnp.mean(xf * xf, axis=-1, keepdims=True) + NORM_EPS)
        dhw = dh_ * w
        dx_ref[...] = dx2_ref[...] + r * dhw - xf * (r * r * r) * jnp.mean(dhw * xf, axis=-1, keepdims=True)

        @pl.when(pl.program_id(0) == 0)
        def _():
            dw_ref[...] = jnp.zeros_like(dw_ref)

        dw_ref[...] += jnp.where(_iota((8, D_MODEL), 0) == 0, jnp.sum(dh_ * xf * r, axis=0, keepdims=True), 0.0)

    tok = pl.BlockSpec((tm, D_MODEL), lambda i: (i, 0))
    return pl.pallas_call(
        body, name="norm_bwd", grid=(t // tm,),
        in_specs=[tok, pl.BlockSpec((1, D_MODEL), lambda i: (0, 0)), tok, tok],
        out_specs=[tok, pl.BlockSpec((8, D_MODEL), lambda i: (0, 0))],
        out_shape=[jax.ShapeDtypeStruct((t, D_MODEL), F32), jax.ShapeDtypeStruct((8, D_MODEL), F32)],
        compiler_params=_params(("arbitrary",)),
    )(x, norm_w, dh, dx2)


def _local_step(x, target, norm_w, wp, conv_w, a_log, dt_bias, gdn_norm_w, w_up_a, w_up_b, w_out, final_w, start_reduce):
    t = x.shape[0]
    tables = _rope_tables(t)
    a_log = jnp.pad(a_log, ((0, 0), (HEADS, 128 - 2 * HEADS)))
    dt_bias = jnp.pad(dt_bias, ((0, 0), (HEADS, 128 - 2 * HEADS)))

    proj, h_t = _norm_proj(x, norm_w, wp)
    qkvs = _rope_fwd(proj, tables)
    outs, lses = zip(*[_att_fwd(qkvs[gi], d, f"att_fwd{gi}") for gi, d in enumerate(DILATIONS)])
    oz_a, o_a, *lse_views = _att_merge(outs, lses, proj)
    conv = _conv_fwd(proj, conv_w)
    gq, gk, gv, gb, gg, ggl, grow = _gdn_prep(conv, proj, a_log, dt_bias)
    o_b, states, invs = _gdn_fwd(gq, gk, gv, gb, gg, ggl, grow)
    oz_b = _gdn_out(o_b, gdn_norm_w, proj)
    big = dict(tm=1024, tn=1024, tk=1024)
    y_a = _matmul(oz_a, w_up_a, "nn", "up_a", **big)
    y_b = _matmul(oz_b, w_up_b, "nn", "up_b", **big)
    merged = _merge(y_a, y_b, proj)
    mo = _matmul(merged, w_out, "nn", "out_proj", **big)
    dx2_m, dx2, loss_blk, d_final = _tail(x, mo, final_w, target)

    d_wout = _matmul(merged, dx2_m, "tn", "d_w_out", **big)
    d_m = _matmul(dx2_m, w_out, "nt", "d_merged", **big)
    dproj, dy_a = _merge_bwd(None, d_m, y_a, proj, SEG_GA, "merge_bwd_a")
    dproj, dy_b = _merge_bwd(dproj, d_m, y_b, proj, SEG_GB, "merge_bwd_b")
    d_wua = _matmul(oz_a, dy_a, "tn", "d_w_up_a", **big)
    d_wub = _matmul(oz_b, dy_b, "tn", "d_w_up_b", **big)
    d_oz_a = _matmul(dy_a, w_up_a, "nt", "d_oz_a", **big)
    d_oz_b = _matmul(dy_b, w_up_b, "nt", "d_oz_b", **big)
    dproj, *views = _att_merge_bwd(dproj, d_oz_a, o_a, proj)
    do_views, delta_views = views[:3], views[3:]
    dqkvs = [_att_bwd(qkvs[gi], do_views[gi], lse_views[gi], delta_views[gi], d, f"att_bwd{gi}")
             for gi, d in enumerate(DILATIONS)]
    dproj = _rope_bwd(dproj, dqkvs, tables)
    dproj, do_b, d_gnw = _gdn_out_bwd(dproj, d_oz_b, o_b, gdn_norm_w, proj)
    dgq, dgk, dgv, dgb, dgg, dggl, dgrow = _gdn_bwd(gq, gk, gv, gb, gg, ggl, grow, states, invs, do_b)
    dproj, dconv, d_small = _gdn_prep_bwd(dproj, conv, proj, a_log, dt_bias, dgq, dgk, dgv, dgb, dgg, dggl, dgrow)
    dproj, d_convw = _conv_bwd(dproj, dconv, proj, conv_w)
    d_wp = _matmul(h_t, dproj, "nn", "d_w_in", tm=1024, tn=PACKED_WIDTH // 4, tk=1024)
    in_flight, token = start_reduce(d_wp, d_wua, d_wub, d_wout, d_convw[0:GDN_CONV])
    dh = _matmul(dproj, wp, "nt", "d_h", tm=1024, tn=1024, tk=PACKED_WIDTH // 4, after=token)
    grad_x, d_norm = _norm_bwd(x, norm_w, dh, dx2)
    return dict(loss=loss_blk, grad_x=grad_x, norm_w=d_norm[0:1], in_flight=in_flight,
                a_log=d_small[0:1, HEADS:2 * HEADS], dt_bias=d_small[1:2, HEADS:2 * HEADS], gdn_norm_w=d_gnw[0:1],
                final_norm_w=d_final[0:1])


SHARDS = 4
W_IN_SHARD = IN_WIDTH // SHARDS
ROWS_UP = WIDTH * (D_MODEL // SHARDS) // 128
ROWS_OUT = (D_MODEL // SHARDS) * D_MODEL // 128
CONV_SHARD = 3 * WIDTH // SHARDS
ROWS_CONV = 16
SLAB_ROWS = 2 * ROWS_UP + ROWS_OUT + 2 * ROWS_CONV
HALF_ROWS = SLAB_ROWS // 2
BIG_HALF = (D_MODEL // 2, W_IN_SHARD)
SMALL_HALF = (HALF_ROWS, 128)
MESH = pl.DeviceIdType.MESH
ANY = pl.BlockSpec(memory_space=pl.ANY)


def _pad_rows(a, rows):
    return jnp.pad(a, ((0, rows - a.shape[0]), (0, 0)))


def _pack_slab(w_up_a, w_up_b, w_out, conv, conv_lo):
    parts = [w_up_a.reshape(ROWS_UP, 128), w_up_b.reshape(ROWS_UP, 128), w_out.reshape(ROWS_OUT, 128),
             _pad_rows(conv.reshape(-1, 128), ROWS_CONV), _pad_rows(conv_lo.reshape(-1, 128), ROWS_CONV)]
    return jnp.concatenate(parts, axis=0).reshape(2, *SMALL_HALF)


def _unpack_slab(slab):
    slab = slab.reshape(SLAB_ROWS, 128)
    r0 = 0
    out = []
    for rows, shape in ((ROWS_UP, (WIDTH, D_MODEL // SHARDS)), (ROWS_UP, (WIDTH, D_MODEL // SHARDS)),
                        (ROWS_OUT, (D_MODEL // SHARDS, D_MODEL)), (ROWS_CONV, None), (ROWS_CONV, None)):
        part = slab[r0:r0 + rows]
        out.append(part[:GDN_CONV * CONV_SHARD // 128].reshape(GDN_CONV, CONV_SHARD) if shape is None else part.reshape(shape))
        r0 += rows
    return out


def _mesh_position():
    x, y, c = lax.axis_index("x"), lax.axis_index("y"), lax.axis_index("c")
    return x, y, c, [(1 - x, y), (x, 1 - y), (1 - x, 1 - y)]


def _gather_weights(shards):
    n = len(shards)

    def body(*refs):
        in_refs, out_refs, (send_sems, recv_sems) = refs[:n], refs[n:2 * n], refs[2 * n:]
        x, y, c, chips = _mesh_position()

        def half(a, chip, which):
            return out_refs[a].at[2 * chip[0] + chip[1], which]

        def copy(k, src, dst, to):
            return pltpu.make_async_remote_copy(src_ref=src, dst_ref=dst, send_sem=send_sems.at[k], recv_sem=recv_sems.at[k],
                                                device_id=to, device_id_type=MESH)

        pairs = [(a, j, chip) for a in range(n) for j, chip in enumerate(chips)]
        first = [copy(6 * a + j, in_refs[a].at[c], half(a, (x, y), c), (*chip, c)) for a, j, chip in pairs]
        for cp in first:
            cp.start()
        passed = [copy(6 * a + 3 + j, half(a, chip, c), half(a, chip, c), (x, y, 1 - c)) for a, j, chip in pairs]
        for i, (a, j, chip) in enumerate(pairs):
            copy(6 * a + j, half(a, chip, c), half(a, chip, c), (x, y, c)).wait_recv()
            passed[i].start()
        for a, j, chip in pairs:
            copy(6 * a + 3 + j, half(a, chip, 1 - c), half(a, chip, 1 - c), (x, y, c)).wait_recv()
        for cp in first + passed:
            cp.wait_send()

    return pl.pallas_call(
        body, name="gather_weights", in_specs=[ANY] * n, out_specs=[ANY] * n,
        out_shape=[jax.ShapeDtypeStruct((SHARDS, *s.shape), s.dtype) for s in shards],
        scratch_shapes=[pltpu.SemaphoreType.DMA((6 * n,)), pltpu.SemaphoreType.DMA((6 * n,))],
    )(*shards)


def _exchange_halves(grads):
    n = len(grads)

    def body(*refs):
        g_refs, out_refs, (send_sems, recv_sems) = refs[:n], refs[n:2 * n], refs[2 * n:]
        x, y, c, _ = _mesh_position()
        copies = [pltpu.make_async_remote_copy(src_ref=g_refs[a].at[s, 1 - c], dst_ref=out_refs[a].at[s],
                                               send_sem=send_sems.at[SHARDS * a + s], recv_sem=recv_sems.at[SHARDS * a + s],
                                               device_id=(x, y, 1 - c), device_id_type=MESH)
                  for a in range(n) for s in range(SHARDS)]
        for cp in copies:
            cp.start()
        for cp in copies:
            cp.wait()

    return pl.pallas_call(
        body, name="exchange_halves", in_specs=[ANY] * n, out_specs=[ANY] * n,
        out_shape=[jax.ShapeDtypeStruct((SHARDS, *g.shape[2:]), F32) for g in grads],
        scratch_shapes=[pltpu.SemaphoreType.DMA((SHARDS * n,)), pltpu.SemaphoreType.DMA((SHARDS * n,))],
    )(*grads)


def _pair_sum(grads, recv, blk, name):
    _, _, rows, cols = grads.shape

    def body(c_ref, g_ref, r_ref, o_ref):
        o_ref[...] = (g_ref[0] + r_ref[...]).astype(o_ref.dtype)

    spec = pl.BlockSpec((1, blk, cols), lambda s, i, c_ref: (s, i, 0))
    return pl.pallas_call(
        body, name=name,
        grid_spec=pltpu.PrefetchScalarGridSpec(
            num_scalar_prefetch=1, grid=(SHARDS, rows // blk),
            in_specs=[pl.BlockSpec((1, 1, blk, cols), lambda s, i, c_ref: (s, c_ref[0], i, 0)), spec],
            out_specs=spec),
        out_shape=jax.ShapeDtypeStruct((SHARDS, rows, cols), MXU_DTYPE),
        compiler_params=_params(("parallel", "parallel")),
    )(lax.axis_index("c").astype(jnp.int32).reshape(1), grads, recv)


_HBM = pl.BlockSpec(memory_space=pltpu.HBM)
_SEM = pl.BlockSpec(memory_space=pltpu.SEMAPHORE)
_DATAFLOW = pltpu.SideEffectType.DATAFLOW_SIDE_EFFECTING


def _scatter_copies(p_refs, l_refs, send_sems, recv_sems):
    x, y, c, chips = _mesh_position()
    return [pltpu.make_async_remote_copy(src_ref=p_refs[a].at[2 * chip[0] + chip[1]], dst_ref=l_refs[a].at[j],
                                         send_sem=send_sems.at[3 * a + j], recv_sem=recv_sems.at[3 * a + j],
                                         device_id=(*chip, c), device_id_type=MESH)
            for a in range(len(p_refs)) for j, chip in enumerate(chips)]


def _scatter_start(pairs):
    n = len(pairs)
    lands = [lax.empty((3, *p.shape[1:]), p.dtype) for p in pairs]

    def body(*refs):
        p_refs, l_refs, send_sems, recv_sems, token = refs[:n], refs[n:2 * n], refs[2 * n], refs[2 * n + 1], refs[-1]
        for cp in _scatter_copies(p_refs, l_refs, send_sems, recv_sems):
            cp.start()
        token[...] = jnp.zeros_like(token)

    operands = [pltpu.with_memory_space_constraint(a, pltpu.HBM) for a in (*pairs, *lands)]
    return pl.pallas_call(
        body, name="scatter_start", in_specs=[_HBM] * (2 * n),
        out_shape=(pltpu.SemaphoreType.DMA((3 * n,)), pltpu.SemaphoreType.DMA((3 * n,)),
                   *[pltpu.HBM(a.shape, a.dtype) for a in operands], jax.ShapeDtypeStruct((8, 128), F32)),
        out_specs=(_SEM, _SEM, *[_HBM] * (2 * n), pl.BlockSpec(memory_space=pltpu.VMEM)),
        input_output_aliases={i: 2 + i for i in range(2 * n)},
        compiler_params=pltpu.CompilerParams(has_side_effects=_DATAFLOW),
    )(*operands)


def _scatter_wait(send_sems, recv_sems, passed, after):
    n = len(passed) // 2

    def body(*refs):
        p_refs, l_refs, send_s, recv_s = refs[:n], refs[n:2 * n], refs[2 * n], refs[2 * n + 1]
        for cp in _scatter_copies(p_refs, l_refs, send_s, recv_s):
            cp.wait_send()
            cp.wait_recv()

    return pl.pallas_call(
        body, name="scatter_wait", in_specs=[_HBM] * (2 * n) + [_SEM, _SEM, ANY],
        out_shape=[pltpu.HBM(a.shape, a.dtype) for a in passed], out_specs=[_HBM] * (2 * n),
        input_output_aliases={i: i for i in range(2 * n)},
        compiler_params=pltpu.CompilerParams(has_side_effects=_DATAFLOW),
    )(*passed, send_sems, recv_sems, after)


def _chip_sum(pairs, recv, blk, name):
    _, rows, cols = pairs.shape

    def body(pos_ref, p_ref, r_ref, o_ref):
        o_ref[0] = ((p_ref[0].astype(F32) + r_ref[0].astype(F32)) + r_ref[1].astype(F32)) + r_ref[2].astype(F32)

    pos = jnp.stack([2 * lax.axis_index("x") + lax.axis_index("y"), lax.axis_index("c")]).astype(jnp.int32)
    return pl.pallas_call(
        body, name=name,
        grid_spec=pltpu.PrefetchScalarGridSpec(
            num_scalar_prefetch=1, grid=(rows // blk,),
            in_specs=[pl.BlockSpec((1, blk, cols), lambda i, pos_ref: (pos_ref[0], i, 0)),
                      pl.BlockSpec((3, blk, cols), lambda i, pos_ref: (0, i, 0))],
            out_specs=pl.BlockSpec((1, blk, cols), lambda i, pos_ref: (pos_ref[1], i, 0))),
        out_shape=jax.ShapeDtypeStruct((2, rows, cols), F32),
        compiler_params=_params(("parallel",)),
    )(pos, pairs, recv)


def _share_total(totals):
    n = len(totals)

    def body(*refs):
        t_refs, out_refs, (send_sems, recv_sems) = refs[:n], refs[n:2 * n], refs[2 * n:]
        x, y, c, _ = _mesh_position()
        copies = [pltpu.make_async_remote_copy(src_ref=t_refs[a].at[c], dst_ref=out_refs[a].at[c], send_sem=send_sems.at[a],
                                               recv_sem=recv_sems.at[a], device_id=(x, y, 1 - c), device_id_type=MESH)
                  for a in range(n)]
        for cp in copies:
            cp.start()
        for a in range(n):
            other = out_refs[a].at[1 - c]
            pltpu.make_async_remote_copy(src_ref=other, dst_ref=other, send_sem=send_sems.at[a], recv_sem=recv_sems.at[a],
                                         device_id=(x, y, c), device_id_type=MESH).wait_recv()
        for cp in copies:
            cp.wait_send()

    return pl.pallas_call(
        body, name="share_total", in_specs=[ANY] * n, out_specs=[ANY] * n,
        out_shape=[jax.ShapeDtypeStruct(t.shape, F32) for t in totals],
        scratch_shapes=[pltpu.SemaphoreType.DMA((n,)), pltpu.SemaphoreType.DMA((n,))],
        input_output_aliases={a: a for a in range(n)},
    )(*totals)


def _allreduce_small(block):
    def body(b_ref, out_ref, gath, send_sems, recv_sems):
        x, y, c, _ = _mesh_position()
        me = 4 * x + 2 * y + c
        gath[me] = b_ref[...]
        copies = []
        for k in range(1, 8):
            peer = (x ^ (k >> 2), y ^ ((k >> 1) & 1), c ^ (k & 1))
            copies.append(pltpu.make_async_remote_copy(src_ref=b_ref, dst_ref=gath.at[me], send_sem=send_sems.at[k - 1],
                                                       recv_sem=recv_sems.at[k - 1], device_id=peer, device_id_type=MESH))
        for cp in copies:
            cp.start()
        for k in range(1, 8):
            src = 4 * (x ^ (k >> 2)) + 2 * (y ^ ((k >> 1) & 1)) + (c ^ (k & 1))
            pltpu.make_async_remote_copy(src_ref=b_ref, dst_ref=gath.at[src], send_sem=send_sems.at[k - 1],
                                         recv_sem=recv_sems.at[k - 1], device_id=(x, y, c), device_id_type=MESH).wait_recv()
        for cp in copies:
            cp.wait_send()
        acc = gath[0]
        for d in range(1, 8):
            acc = acc + gath[d]
        out_ref[...] = acc

    vm = pl.BlockSpec(memory_space=pltpu.VMEM)
    return pl.pallas_call(
        body, name="allreduce_small", in_specs=[vm], out_specs=vm,
        out_shape=jax.ShapeDtypeStruct((8, D_MODEL), F32),
        scratch_shapes=[pltpu.VMEM((8, 8, D_MODEL), F32), pltpu.SemaphoreType.DMA((7,)), pltpu.SemaphoreType.DMA((7,))],
    )(block)


def _adamw(w, g, m, v, name):
    rows, cols = w.shape
    tr = 128 if rows % 128 == 0 else rows

    def body(w_ref, g_ref, m_ref, v_ref, d_ref, nm_ref, nv_ref):
        gv = g_ref[...]
        nm = ADAM_B1 * m_ref[...] + (1.0 - ADAM_B1) * gv
        nv = ADAM_B2 * v_ref[...] + (1.0 - ADAM_B2) * (gv * gv)
        m_hat = nm / (1.0 - ADAM_B1 ** ADAM_STEP)
        v_hat = nv / (1.0 - ADAM_B2 ** ADAM_STEP)
        d_ref[...] = -ADAM_LR * (m_hat / (jnp.sqrt(v_hat) + ADAM_EPS) + ADAM_WD * w_ref[...])
        nm_ref[...] = nm
        nv_ref[...] = nv

    spec = pl.BlockSpec((tr, cols), lambda i: (i, 0))
    shape = jax.ShapeDtypeStruct((rows, cols), F32)
    return pl.pallas_call(
        body, name=name, grid=(rows // tr,), in_specs=[spec] * 4, out_specs=[spec] * 3, out_shape=[shape] * 3,
        compiler_params=_params(("parallel",)),
    )(w, g, m, v)


def kernel(x, norm_w, w_in, conv_w, a_log, dt_bias, gdn_norm_w, w_up_a, w_up_b, w_out, final_norm_w, loss_target, m_norm_w, m_w_in, m_conv_w, m_a_log, m_dt_bias, m_gdn_norm_w, m_w_up_a, m_w_up_b, m_w_out, m_final_norm_w, v_norm_w, v_w_in, v_conv_w, v_a_log, v_dt_bias, v_gdn_norm_w, v_w_up_a, v_w_up_b, v_w_out, v_final_norm_w):
    conv_hi = conv_w[0].astype(MXU_DTYPE)
    conv_lo = (conv_w[0] - conv_hi.astype(F32)).astype(MXU_DTYPE)
    big = w_in[0].astype(MXU_DTYPE).reshape(2, *BIG_HALF)
    slab = _pack_slab(w_up_a[0].astype(MXU_DTYPE), w_up_b[0].astype(MXU_DTYPE), w_out[0].astype(MXU_DTYPE), conv_hi, conv_lo)
    own_shard = 2 * lax.axis_index("x") + lax.axis_index("y")
    bigs, slabs = _gather_weights([big, slab])
    bigs = lax.dynamic_update_slice(bigs, big[None], (own_shard, 0, 0, 0)).reshape(SHARDS, D_MODEL, W_IN_SHARD)
    slabs = lax.dynamic_update_slice(slabs, slab[None], (own_shard, 0, 0, 0))
    parts = [_unpack_slab(slabs[s]) for s in range(SHARDS)]
    split = BA_END - (SHARDS - 1) * W_IN_SHARD
    wp = jnp.concatenate([bigs[s] for s in range(SHARDS - 1)]
                         + [bigs[-1][:, :split], jnp.zeros((D_MODEL, SEG_GA - BA_END), MXU_DTYPE), bigs[-1][:, split:]], axis=1)
    w_up_a_full = jnp.concatenate([p[0] for p in parts], axis=1)
    w_up_b_full = jnp.concatenate([p[1] for p in parts], axis=1)
    w_out_full = jnp.concatenate([p[2] for p in parts], axis=0)
    conv_full = jnp.concatenate([p[3].astype(F32) + p[4].astype(F32) for p in parts], axis=1)

    blocks, tags = (128, HALF_ROWS), ("w_in", "slab")

    def start_reduce(d_wp, d_w_up_a, d_w_up_b, d_w_out, d_conv_w):
        d_w_in = [d_wp[:, s * W_IN_SHARD:(s + 1) * W_IN_SHARD] for s in range(SHARDS - 1)]
        d_w_in.append(jnp.concatenate([d_wp[:, (SHARDS - 1) * W_IN_SHARD:BA_END], d_wp[:, SEG_GA:]], axis=1))
        zero_conv = jnp.zeros((GDN_CONV, CONV_SHARD), F32)
        grads = [jnp.stack(d_w_in).reshape(SHARDS, 2, *BIG_HALF),
                 jnp.stack([_pack_slab(d_w_up_a[:, s * 256:(s + 1) * 256], d_w_up_b[:, s * 256:(s + 1) * 256],
                                       d_w_out[s * 256:(s + 1) * 256], d_conv_w[:, s * CONV_SHARD:(s + 1) * CONV_SHARD],
                                       zero_conv) for s in range(SHARDS)])]
        from_sibling = _exchange_halves(grads)
        pairs = [_pair_sum(gr, fs, blk, f"pair_sum_{tag}") for gr, fs, blk, tag in zip(grads, from_sibling, blocks, tags)]
        *in_flight, token = _scatter_start(pairs)
        return in_flight, token

    g = _local_step(x[0], loss_target[0], norm_w, wp, conv_full, a_log, dt_bias, gdn_norm_w,
                    w_up_a_full, w_up_b_full, w_out_full, final_norm_w[None], start_reduce)

    send_sems, recv_sems, *passed = g["in_flight"]
    arrived = _scatter_wait(send_sems, recv_sems, passed, after=g["grad_x"])
    pairs, from_chips = arrived[:2], arrived[2:]
    total_big, total_slab = _share_total([_chip_sum(p, fc, blk, f"chip_sum_{tag}")
                                          for p, fc, blk, tag in zip(pairs, from_chips, blocks, tags)])
    g_w_in = total_big.reshape(D_MODEL, W_IN_SHARD)
    g_w_up_a, g_w_up_b, g_w_out, g_conv, _ = _unpack_slab(total_slab)

    row2 = jnp.concatenate([g["gdn_norm_w"], g["a_log"], g["dt_bias"], g["loss"][0:1, 0:1],
                            jnp.zeros((1, D_MODEL - HEAD_DIM - 2 * HEADS - 1), F32)], axis=1)
    small = _allreduce_small(jnp.concatenate([g["norm_w"], g["final_norm_w"], row2, jnp.zeros((5, D_MODEL), F32)], axis=0))
    g_norm, g_final = small[0:1], small[1]
    g_gnw, g_alog, g_dt = small[2:3, 0:HEAD_DIM], small[2:3, HEAD_DIM:HEAD_DIM + HEADS], small[2:3, HEAD_DIM + HEADS:HEAD_DIM + 2 * HEADS]
    loss = small[2, HEAD_DIM + 2 * HEADS]

    names = ["norm_w", "w_in", "conv_w", "a_log", "dt_bias", "gdn_norm_w", "w_up_a", "w_up_b", "w_out", "final_norm_w"]
    weights = dict(zip(names, (norm_w, w_in, conv_w, a_log, dt_bias, gdn_norm_w, w_up_a, w_up_b, w_out, final_norm_w)))
    ms = dict(zip(names, (m_norm_w, m_w_in, m_conv_w, m_a_log, m_dt_bias, m_gdn_norm_w, m_w_up_a, m_w_up_b, m_w_out, m_final_norm_w)))
    vs = dict(zip(names, (v_norm_w, v_w_in, v_conv_w, v_a_log, v_dt_bias, v_gdn_norm_w, v_w_up_a, v_w_up_b, v_w_out, v_final_norm_w)))
    grads2d = dict(norm_w=g_norm, w_in=g_w_in, conv_w=g_conv, a_log=g_alog, dt_bias=g_dt, gdn_norm_w=g_gnw,
                   w_up_a=g_w_up_a, w_up_b=g_w_up_b, w_out=g_w_out, final_norm_w=g_final[None])
    grad_out, delta, new_m, new_v = [], [], [], []
    for n in names:
        shape = weights[n].shape
        two_d = grads2d[n].shape
        d, nm, nv = _adamw(weights[n].reshape(two_d), grads2d[n], ms[n].reshape(two_d), vs[n].reshape(two_d), f"adamw_{n}")
        grad_out.append(grads2d[n].reshape(shape))
        delta.append(d.reshape(shape))
        new_m.append(nm.reshape(shape))
        new_v.append(nv.reshape(shape))
    return (loss, g["grad_x"][None], *grad_out, *delta, *new_m, *new_v)
```

```python
import functools

import jax
import jax.numpy as jnp
from jax import lax
from jax.experimental import pallas as pl
from jax.experimental.pallas import tpu as pltpu

F32 = jnp.float32
MXU_DTYPE = jnp.bfloat16
HIGHEST = lax.Precision.HIGHEST

D_MODEL = 1024
HEADS = 8
HEAD_DIM = 64
WIDTH = HEADS * HEAD_DIM
NORM_EPS = 1e-6
ROPE_THETA = 10000.0
ATT_BLOCK = 128
DILATIONS = (1, 4, 16)
GDN_CHUNK = 64
GDN_CONV = 4
IN_WIDTH = 9232
SEG_A, SEG_ZA, SEG_B, SEG_ZB, SEG_BA, SEG_GA, SEG_GB, PACKED_WIDTH = 0, 4608, 5120, 6656, 7168, 7680, 8704, 9728
BA_END = 7184
VMEM_LIMIT = 56 * 1024 * 1024

ADAM_LR, ADAM_B1, ADAM_B2, ADAM_EPS, ADAM_WD, ADAM_STEP = 0.001, 0.9, 0.999, 1e-08, 0.01, 10

_NN = (((1,), (0,)), ((), ()))
_NT = (((1,), (1,)), ((), ()))
_TN = (((0,), (0,)), ((), ()))


def _params(sem):
    return pltpu.CompilerParams(dimension_semantics=sem, vmem_limit_bytes=VMEM_LIMIT)


def _mxu(a, b, dims):
    return lax.dot_general(a.astype(MXU_DTYPE), b.astype(MXU_DTYPE), dims, preferred_element_type=F32)


def _sigmoid(x):
    return 1.0 / (1.0 + jnp.exp(-x))


def _softplus(x):
    return jnp.maximum(x, 0.0) + jnp.log(1.0 + jnp.exp(-jnp.abs(x)))


def _iota(shape, axis):
    return lax.broadcasted_iota(jnp.int32, shape, axis)


def _matmul(a, b, mode, name, out_dtype=F32, tm=512, tn=512, tk=512, after=None):
    if mode == "nn":
        (m, k), (k2, n) = a.shape, b.shape
    elif mode == "nt":
        (m, k), (n, k2) = a.shape, b.shape
    else:
        (k, m), (k2, n) = a.shape, b.shape
    assert k == k2
    tm, tn, tk = min(tm, m), min(tn, n), min(tk, k)
    assert m % tm == 0 and n % tn == 0 and k % tk == 0
    nk = k // tk
    dims = {"nn": _NN, "nt": _NT, "tn": _TN}[mode]

    assert out_dtype == F32

    def body(a_ref, b_ref, *rest):
        o_ref = rest[-1]
        kk = pl.program_id(2)
        part = _mxu(a_ref[...], b_ref[...], dims)

        @pl.when(kk == 0)
        def _():
            o_ref[...] = part

        @pl.when(kk > 0)
        def _():
            o_ref[...] += part

    a_spec = pl.BlockSpec((tk, tm), lambda i, j, kk: (kk, i)) if mode == "tn" else pl.BlockSpec((tm, tk), lambda i, j, kk: (i, kk))
    b_spec = pl.BlockSpec((tn, tk), lambda i, j, kk: (j, kk)) if mode == "nt" else pl.BlockSpec((tk, tn), lambda i, j, kk: (kk, j))
    extra_specs, extra_args = ([], []) if after is None else ([pl.BlockSpec(memory_space=pl.ANY)], [after])
    return pl.pallas_call(
        body, name=name, grid=(m // tm, n // tn, nk), in_specs=[a_spec, b_spec] + extra_specs,
        out_specs=pl.BlockSpec((tm, tn), lambda i, j, kk: (i, j)),
        out_shape=jax.ShapeDtypeStruct((m, n), out_dtype),
        compiler_params=_params(("parallel", "parallel", "arbitrary")),
    )(a, b, *extra_args)


def _norm_proj(x, norm_w, wp, tm=1024, tn=PACKED_WIDTH // 4):
    t = x.shape[0]
    tm = min(tm, t)

    def body(x_ref, nw_ref, w_ref, proj_ref, ht_ref, h_scr):
        @pl.when(pl.program_id(1) == 0)
        def _():
            xf = x_ref[...]
            r = lax.rsqrt(jnp.mean(xf * xf, axis=-1, keepdims=True) + NORM_EPS)
            h = xf * r * nw_ref[...]
            h_scr[...] = h.astype(h_scr.dtype)
            ht_ref[...] = h.T.astype(ht_ref.dtype)

        proj_ref[...] = jnp.dot(h_scr[...], w_ref[...], preferred_element_type=F32)

    return pl.pallas_call(
        body, name="norm_proj", grid=(t // tm, PACKED_WIDTH // tn),
        in_specs=[pl.BlockSpec((tm, D_MODEL), lambda i, j: (i, 0)),
                  pl.BlockSpec((1, D_MODEL), lambda i, j: (0, 0)),
                  pl.BlockSpec((D_MODEL, tn), lambda i, j: (0, j))],
        out_specs=[pl.BlockSpec((tm, tn), lambda i, j: (i, j)),
                   pl.BlockSpec((D_MODEL, tm), lambda i, j: (0, i))],
        out_shape=[jax.ShapeDtypeStruct((t, PACKED_WIDTH), F32), jax.ShapeDtypeStruct((D_MODEL, t), MXU_DTYPE)],
        scratch_shapes=[pltpu.VMEM((tm, D_MODEL), MXU_DTYPE)],
        compiler_params=_params(("parallel", "arbitrary")),
    )(x, norm_w, wp)


def _rope_tables(t):
    lane = jnp.arange(128)
    inv_freq = ROPE_THETA ** (-jnp.arange(0, HEAD_DIM, 2, dtype=F32) / HEAD_DIM)
    ang = jnp.arange(t, dtype=F32)[:, None] * inv_freq[None, :]
    ang = jnp.concatenate([ang, ang, ang, ang], axis=-1)
    first_half = (lane % HEAD_DIM) < HEAD_DIM // 2
    cos, sin = jnp.cos(ang), jnp.sin(ang)
    return cos, jnp.where(first_half, -sin, 0.0), jnp.where(first_half, 0.0, sin)


def _rope_block(x, cos, sin_lo, sin_hi, sign):
    outs = []
    for c in range(8):
        xc = x[:, c * 128:(c + 1) * 128]
        rot = pltpu.roll(xc, 96, 1) * sin_lo + pltpu.roll(xc, 32, 1) * sin_hi
        outs.append(xc * cos + sign * rot)
    outs.append(x[:, 2 * WIDTH:])
    return jnp.concatenate(outs, axis=1)


def _tile_scratch(tm, cols):
    return pltpu.VMEM((cols // 128, tm, 128), F32)


def _store_tile(scr, y):
    for c in range(scr.shape[0]):
        scr[c] = y[:, c * 128:(c + 1) * 128]


def _load_tile(scr):
    return jnp.concatenate([scr[c] for c in range(scr.shape[0])], axis=1)


def _to_strided_view(scr, o_ref, d):
    n, tm, _ = scr.shape
    for r in range(d):
        for c in range(n):
            o_ref[:, (r * n + c) * 128:(r * n + c + 1) * 128] = scr[c, pl.ds(r, tm // d, stride=d), :].astype(o_ref.dtype)


def _from_strided_view(i_ref, scr, d):
    n, tm, _ = scr.shape
    for r in range(d):
        for c in range(n):
            scr[c, pl.ds(r, tm // d, stride=d), :] = i_ref[:, (r * n + c) * 128:(r * n + c + 1) * 128].astype(F32)


def _strided_spec(tm, d, cols):
    return pl.BlockSpec((tm // d, d * cols), lambda i: (i, 0))


def _rope_fwd(proj, tables, tm=256):
    t = proj.shape[0]
    cols = 3 * WIDTH

    def body(x_ref, c_ref, sl_ref, sh_ref, o0, o1, o2, scr):
        for g, (d, o_ref) in enumerate(zip(DILATIONS, (o0, o1, o2))):
            y = _rope_block(x_ref[:, g * cols:(g + 1) * cols], c_ref[...], sl_ref[...], sh_ref[...], 1.0)
            if d == 1:
                o_ref[...] = y.astype(o_ref.dtype)
            else:
                _store_tile(scr, y)
                _to_strided_view(scr, o_ref, d)

    tab = pl.BlockSpec((tm, 128), lambda i: (i, 0))
    return pl.pallas_call(
        body, name="rope_fwd", grid=(t // tm,),
        in_specs=[pl.BlockSpec((tm, 3 * cols), lambda i: (i, 0)), tab, tab, tab],
        out_specs=[_strided_spec(tm, d, cols) for d in DILATIONS],
        out_shape=[jax.ShapeDtypeStruct((t // d, d * cols), MXU_DTYPE) for d in DILATIONS],
        scratch_shapes=[_tile_scratch(tm, cols)],
        compiler_params=_params(("parallel",)),
    )(proj, *tables)


def _rope_bwd(dproj, dqkvs, tables, tm=256):
    t = dproj.shape[0]
    cols = 3 * WIDTH

    def body(dp_ref, i0, i1, i2, c_ref, sl_ref, sh_ref, o_ref, scr):
        for g, (d, i_ref) in enumerate(zip(DILATIONS, (i0, i1, i2))):
            if d == 1:
                x = i_ref[...]
            else:
                _from_strided_view(i_ref, scr, d)
                x = _load_tile(scr)
            y = _rope_block(x, c_ref[...], sl_ref[...], sh_ref[...], -1.0)
            o_ref[:, g * cols:(g + 1) * cols] = y.astype(o_ref.dtype)

    tab = pl.BlockSpec((tm, 128), lambda i: (i, 0))
    return pl.pallas_call(
        body, name="rope_bwd", grid=(t // tm,),
        in_specs=[pl.BlockSpec(memory_space=pl.ANY)] + [_strided_spec(tm, d, cols) for d in DILATIONS] + [tab, tab, tab],
        out_specs=pl.BlockSpec((tm, 3 * cols), lambda i: (i, 0)),
        out_shape=jax.ShapeDtypeStruct((t, PACKED_WIDTH), MXU_DTYPE),
        scratch_shapes=[_tile_scratch(tm, cols)],
        input_output_aliases={0: 0},
        compiler_params=_params(("parallel",)),
    )(dproj, *dqkvs, *tables)


def _att_masks():
    qi = _iota((ATT_BLOCK, ATT_BLOCK), 0)
    kj = _iota((ATT_BLOCK, ATT_BLOCK), 1)
    return kj <= qi, kj >= qi


def _att_fwd(qkv, d, name):
    rows = qkv.shape[0]
    nb = rows // ATT_BLOCK
    scale = HEAD_DIM ** -0.5

    def body(q_ref, kc_ref, kp_ref, vc_ref, vp_ref, o_ref, lse_ref):
        has_prev = pl.program_id(1) > 0
        m_cur, m_prev = _att_masks()
        m_prev = m_prev & has_prev
        hs = range(HEADS)
        sls = [slice(h * HEAD_DIM, (h + 1) * HEAD_DIM) for h in hs]
        qs = [q_ref[:, sl] for sl in sls]
        s_c = [jnp.where(m_cur, _mxu(qs[h], kc_ref[:, sls[h]], _NT) * scale, -jnp.inf) for h in hs]
        s_p = [jnp.where(m_prev, _mxu(qs[h], kp_ref[:, sls[h]], _NT) * scale, -jnp.inf) for h in hs]
        m = [jnp.maximum(jnp.max(s_c[h], axis=1, keepdims=True), jnp.max(s_p[h], axis=1, keepdims=True)) for h in hs]
        p_c = [jnp.exp(s_c[h] - m[h]) for h in hs]
        p_p = [jnp.exp(s_p[h] - m[h]) for h in hs]
        den = [jnp.sum(p_c[h], axis=1, keepdims=True) + jnp.sum(p_p[h], axis=1, keepdims=True) for h in hs]
        o = [_mxu(p_c[h], vc_ref[:, sls[h]], _NN) + _mxu(p_p[h], vp_ref[:, sls[h]], _NN) for h in hs]
        for h in hs:
            o_ref[:, sls[h]] = o[h] / den[h]
            lse_ref[:, sls[h]] = jnp.broadcast_to(m[h] + jnp.log(den[h]), (ATT_BLOCK, HEAD_DIM))

    def cur(c):
        return pl.BlockSpec((ATT_BLOCK, WIDTH), lambda r, i: (i, 3 * r + c))

    def prev(c):
        return pl.BlockSpec((ATT_BLOCK, WIDTH), lambda r, i: (jnp.maximum(i - 1, 0), 3 * r + c))

    out = pl.BlockSpec((ATT_BLOCK, WIDTH), lambda r, i: (i, r))
    return pl.pallas_call(
        body, name=name, grid=(d, nb), in_specs=[cur(0), cur(1), prev(1), cur(2), prev(2)],
        out_specs=[out, out],
        out_shape=[jax.ShapeDtypeStruct((rows, d * WIDTH), F32)] * 2,
        compiler_params=_params(("parallel", "arbitrary")),
    )(qkv, qkv, qkv, qkv, qkv)


def _att_bwd(qkv, do, lse, delta, d, name):
    rows = qkv.shape[0]
    nb = rows // ATT_BLOCK
    scale = HEAD_DIM ** -0.5

    def body(q_ref, qn_ref, kc_ref, kp_ref, vc_ref, vp_ref, do_ref, don_ref, l_ref, ln_ref, dl_ref, dln_ref, o_ref):
        i = pl.program_id(1)
        m_cur, m_prev = _att_masks()
        m_p = m_prev & (i > 0)
        m_n = m_prev & (i < nb - 1)

        hs = range(HEADS)
        sls = [slice(h * HEAD_DIM, (h + 1) * HEAD_DIM) for h in hs]
        col = [slice(h * HEAD_DIM, h * HEAD_DIM + 1) for h in hs]

        def probs(q_r, k_r, lse_r, mask):
            s = [_mxu(q_r[:, sls[h]], k_r[:, sls[h]], _NT) for h in hs]
            return [jnp.where(mask, jnp.exp(s[h] * scale - lse_r[:, col[h]]), 0.0) for h in hs]

        def dscores(p, do_r, v_r, dl_r):
            dp = [_mxu(do_r[:, sls[h]], v_r[:, sls[h]], _NT) for h in hs]
            return [(p[h] * (dp[h] - dl_r[:, col[h]])).astype(MXU_DTYPE) for h in hs]

        p = probs(q_ref, kc_ref, l_ref, m_cur)
        ds = dscores(p, do_ref, vc_ref, dl_ref)
        dq = [_mxu(ds[h], kc_ref[:, sls[h]], _NN) for h in hs]
        dk = [_mxu(ds[h], q_ref[:, sls[h]], _TN) for h in hs]
        dv = [_mxu(p[h], do_ref[:, sls[h]], _TN) for h in hs]
        p = probs(q_ref, kp_ref, l_ref, m_p)
        ds = dscores(p, do_ref, vp_ref, dl_ref)
        dq = [dq[h] + _mxu(ds[h], kp_ref[:, sls[h]], _NN) for h in hs]
        p = probs(qn_ref, kc_ref, ln_ref, m_n)
        ds = dscores(p, don_ref, vc_ref, dln_ref)
        dk = [dk[h] + _mxu(ds[h], qn_ref[:, sls[h]], _TN) for h in hs]
        dv = [dv[h] + _mxu(p[h], don_ref[:, sls[h]], _TN) for h in hs]
        for h in hs:
            o_ref[:, sls[h]] = dq[h] * scale
            o_ref[:, WIDTH + h * HEAD_DIM:WIDTH + (h + 1) * HEAD_DIM] = dk[h] * scale
            o_ref[:, 2 * WIDTH + h * HEAD_DIM:2 * WIDTH + (h + 1) * HEAD_DIM] = dv[h]

    def qkv_spec(c, shift):
        def idx(r, i):
            return (jnp.clip(i + shift, 0, nb - 1), 3 * r + c)
        return pl.BlockSpec((ATT_BLOCK, WIDTH), idx)

    def tok_spec(shift):
        def idx(r, i):
            return (jnp.clip(i + shift, 0, nb - 1), r)
        return pl.BlockSpec((ATT_BLOCK, WIDTH), idx)

    return pl.pallas_call(
        body, name=name, grid=(d, nb),
        in_specs=[qkv_spec(0, 0), qkv_spec(0, 1), qkv_spec(1, 0), qkv_spec(1, -1), qkv_spec(2, 0), qkv_spec(2, -1),
                  tok_spec(0), tok_spec(1), tok_spec(0), tok_spec(1), tok_spec(0), tok_spec(1)],
        out_specs=pl.BlockSpec((ATT_BLOCK, 3 * WIDTH), lambda r, i: (i, r)),
        out_shape=jax.ShapeDtypeStruct((rows, d * 3 * WIDTH), F32),
        compiler_params=_params(("parallel", "arbitrary")),
    )(qkv, qkv, qkv, qkv, qkv, qkv, do, do, lse, lse, delta, delta)


def _att_merge(os_, lses, proj, tm=256):
    t = proj.shape[0]

    def body(o0, o1, o2, l0, l1, l2, z_ref, oz_ref, o_ref, t0, t1, t2, s_o1, s_l1, s_o2, s_l2, s_t):
        _from_strided_view(o1, s_o1, DILATIONS[1])
        _from_strided_view(l1, s_l1, DILATIONS[1])
        _from_strided_view(o2, s_o2, DILATIONS[2])
        _from_strided_view(l2, s_l2, DILATIONS[2])
        a, b, c = l0[...], _load_tile(s_l1), _load_tile(s_l2)
        m = jnp.maximum(jnp.maximum(a, b), c)
        wa, wb, wc = jnp.exp(a - m), jnp.exp(b - m), jnp.exp(c - m)
        den = wa + wb + wc
        o = (wa * o0[...] + wb * _load_tile(s_o1) + wc * _load_tile(s_o2)) / den
        z = z_ref[...]
        o_ref[...] = o
        oz_ref[...] = (o * z * _sigmoid(z)).astype(oz_ref.dtype)
        total = m + jnp.log(den)
        t0[...] = total
        _store_tile(s_t, total)
        _to_strided_view(s_t, t1, DILATIONS[1])
        _to_strided_view(s_t, t2, DILATIONS[2])

    tok = pl.BlockSpec((tm, WIDTH), lambda i: (i, 0))
    views = [_strided_spec(tm, d, WIDTH) for d in DILATIONS]
    view_shapes = [jax.ShapeDtypeStruct((t // d, d * WIDTH), F32) for d in DILATIONS]
    return pl.pallas_call(
        body, name="att_merge", grid=(t // tm,),
        in_specs=views + views + [pl.BlockSpec((tm, WIDTH), lambda i: (i, SEG_ZA // WIDTH))],
        out_specs=[tok, tok] + views,
        out_shape=[jax.ShapeDtypeStruct((t, WIDTH), MXU_DTYPE), jax.ShapeDtypeStruct((t, WIDTH), F32)] + view_shapes,
        scratch_shapes=[_tile_scratch(tm, WIDTH)] * 5,
        compiler_params=_params(("parallel",)),
    )(*os_, *lses, proj)


def _att_merge_bwd(dproj, d_oz, o, proj, tm=256):
    t = proj.shape[0]

    def body(dp_ref, doz_ref, o_ref, z_ref, dz_ref, do0, do1, do2, dl0, dl1, dl2, s_do, s_dl):
        z, ov, g = z_ref[...], o_ref[...], doz_ref[...]
        sg = _sigmoid(z)
        do = g * z * sg
        dz_ref[...] = (g * ov * sg * (1.0 + z * (1.0 - sg))).astype(dz_ref.dtype)
        do0[...] = do.astype(do0.dtype)
        _store_tile(s_do, do)
        prod = do * ov
        for p in range(HEADS // 2):
            s_dl[p] = _pair_sum_lanes(prod[:, p * 128:(p + 1) * 128], _pair_ones())
        dl0[...] = _load_tile(s_dl)
        for d, do_v, dl_v in ((DILATIONS[1], do1, dl1), (DILATIONS[2], do2, dl2)):
            _to_strided_view(s_do, do_v, d)
            _to_strided_view(s_dl, dl_v, d)

    tok = pl.BlockSpec((tm, WIDTH), lambda i: (i, 0))
    seg = pl.BlockSpec((tm, WIDTH), lambda i: (i, SEG_ZA // WIDTH))
    views = [_strided_spec(tm, d, WIDTH) for d in DILATIONS]
    return pl.pallas_call(
        body, name="att_merge_bwd", grid=(t // tm,),
        in_specs=[pl.BlockSpec(memory_space=pl.ANY), tok, tok, seg],
        out_specs=[seg] + views + views,
        out_shape=[jax.ShapeDtypeStruct((t, PACKED_WIDTH), MXU_DTYPE)]
        + [jax.ShapeDtypeStruct((t // d, d * WIDTH), MXU_DTYPE) for d in DILATIONS]
        + [jax.ShapeDtypeStruct((t // d, d * WIDTH), F32) for d in DILATIONS],
        scratch_shapes=[_tile_scratch(tm, WIDTH)] * 2,
        input_output_aliases={0: 0},
        compiler_params=_params(("parallel",)),
    )(dproj, d_oz, o, proj)


def _shift_down(x, halo, s):
    if s == 0:
        return x
    xs = pltpu.roll(x, s, 0)
    head = jnp.where(_iota((8, x.shape[1]), 0) < s, pltpu.roll(halo, s, 0), xs[0:8])
    return jnp.concatenate([head, xs[8:]], axis=0)


def _shift_up(x, nxt, s):
    if s == 0:
        return x
    n = x.shape[0]
    xs = pltpu.roll(x, n - s, 0)
    tail = jnp.where(_iota((8, x.shape[1]), 0) >= 8 - s, pltpu.roll(nxt, 8 - s, 0), xs[n - 8:])
    return jnp.concatenate([xs[:n - 8], tail], axis=0)


def _conv_fwd(proj, conv_w, tm=256):
    t = proj.shape[0]
    cb = SEG_B // WIDTH

    def body(x_ref, halo_ref, w_ref, c_ref):
        halo = jnp.where(pl.program_id(0) > 0, halo_ref[...], 0.0)
        x = x_ref[...]
        w = w_ref[...]
        acc = jnp.zeros((tm, WIDTH), F32)
        for j in range(GDN_CONV):
            acc += _shift_down(x, halo, GDN_CONV - 1 - j) * w[j:j + 1, :]
        c_ref[...] = acc

    return pl.pallas_call(
        body, name="conv_fwd", grid=(t // tm, 3),
        in_specs=[pl.BlockSpec((tm, WIDTH), lambda i, c: (i, cb + c)),
                  pl.BlockSpec((8, WIDTH), lambda i, c: (jnp.maximum(i * (tm // 8) - 1, 0), cb + c)),
                  pl.BlockSpec((GDN_CONV, WIDTH), lambda i, c: (0, c))],
        out_specs=pl.BlockSpec((tm, WIDTH), lambda i, c: (i, c)),
        out_shape=jax.ShapeDtypeStruct((t, 3 * WIDTH), F32),
        compiler_params=_params(("parallel", "parallel")),
    )(proj, proj, conv_w)


def _conv_bwd(dproj, dc, proj, conv_w, tm=256):
    t = proj.shape[0]
    cb = SEG_B // WIDTH
    nt = t // tm

    def body(dp_ref, dc_ref, dcn_ref, x_ref, halo_ref, w_ref, dx_ref, dw_ref):
        i = pl.program_id(1)
        w = w_ref[...]
        dcn = jnp.where(i < nt - 1, dcn_ref[...], 0.0)
        dcv = dc_ref[...]
        acc = jnp.zeros((tm, WIDTH), F32)
        for j in range(GDN_CONV):
            acc += _shift_up(dcv, dcn, GDN_CONV - 1 - j) * w[j:j + 1, :]
        dx_ref[...] = acc.astype(dx_ref.dtype)
        halo = jnp.where(i > 0, halo_ref[...], 0.0)
        x = x_ref[...]
        row8 = _iota((8, WIDTH), 0)
        part = jnp.zeros((8, WIDTH), F32)
        for j in range(GDN_CONV):
            s = jnp.sum(dcv * _shift_down(x, halo, GDN_CONV - 1 - j), axis=0, keepdims=True)
            part += jnp.where(row8 == j, s, 0.0)

        @pl.when(i == 0)
        def _():
            dw_ref[...] = jnp.zeros_like(dw_ref)

        dw_ref[...] += part

    return pl.pallas_call(
        body, name="conv_bwd", grid=(3, nt),
        in_specs=[pl.BlockSpec(memory_space=pl.ANY),
                  pl.BlockSpec((tm, WIDTH), lambda c, i: (i, c)),
                  pl.BlockSpec((8, WIDTH), lambda c, i: (jnp.minimum((i + 1) * (tm // 8), t // 8 - 1), c)),
                  pl.BlockSpec((tm, WIDTH), lambda c, i: (i, cb + c)),
                  pl.BlockSpec((8, WIDTH), lambda c, i: (jnp.maximum(i * (tm // 8) - 1, 0), cb + c)),
                  pl.BlockSpec((GDN_CONV, WIDTH), lambda c, i: (0, c))],
        out_specs=[pl.BlockSpec((tm, WIDTH), lambda c, i: (i, cb + c)),
                   pl.BlockSpec((8, WIDTH), lambda c, i: (0, c))],
        out_shape=[jax.ShapeDtypeStruct((t, PACKED_WIDTH), MXU_DTYPE), jax.ShapeDtypeStruct((8, 3 * WIDTH), F32)],
        input_output_aliases={0: 0},
        compiler_params=_params(("parallel", "arbitrary")),
    )(dproj, dc, dc, proj, proj, conv_w)


def _chunk_matrices(tm):
    r, c = _iota((tm, tm), 0), _iota((tm, tm), 1)
    same = (r // GDN_CHUNK) == (c // GDN_CHUNK)
    return jnp.where(same & (c <= r), 1.0, 0.0), jnp.where(same, 1.0, 0.0)


def _gdn_gates(ba, a_log, dt_bias):
    al = ba + dt_bias
    return _sigmoid(ba), -jnp.exp(a_log) * _softplus(al), _sigmoid(al)


def _head_lane_eye():
    return jnp.where(_iota((HEADS, 128), 1) == _iota((HEADS, 128), 0) + HEADS, 1.0, 0.0)


def _pair_ones():
    return jnp.where(_iota((128, 128), 0) // HEAD_DIM == _iota((128, 128), 1) // HEAD_DIM, 1.0, 0.0).astype(MXU_DTYPE)


def _pair_to_lanes(p, base):
    return jnp.where(_iota((128, 128), 1) == base + 2 * p + _iota((128, 128), 0) // HEAD_DIM, 1.0, 0.0).astype(MXU_DTYPE)


def _pair_sum_lanes(x, sel):
    hi, lo = _split(x)
    return jnp.dot(hi, sel, preferred_element_type=F32) + jnp.dot(lo, sel, preferred_element_type=F32)


def _gdn_prep(conv, proj, a_log, dt_bias, tm=256):
    t = proj.shape[0]
    nc = tm // GDN_CHUNK

    def body(c_ref, ba_ref, al_ref, dt_ref, q_ref, k_ref, v_ref, b_ref, g_ref, gl_ref, grow_ref):
        beta, g, _ = _gdn_gates(ba_ref[:, 0:128], al_ref[...], dt_ref[...])
        lmat, cmat = _chunk_matrices(tm)
        gc = jnp.dot(lmat, g, precision=HIGHEST, preferred_element_type=F32)
        gl = jnp.dot(cmat, g, precision=HIGHEST, preferred_element_type=F32)
        grow = lax.dot_general(_head_lane_eye(), gc, _NT, precision=HIGHEST, preferred_element_type=F32)
        ones = _pair_ones()
        for p in range(HEADS // 2):
            for seg, ref, scale in ((0, q_ref, HEAD_DIM ** -0.5), (1, k_ref, 1.0), (2, v_ref, None)):
                c = c_ref[:, seg * WIDTH + p * 128:seg * WIDTH + (p + 1) * 128]
                a = c * _sigmoid(c)
                if scale is not None:
                    a = a * (lax.rsqrt(_pair_sum_lanes(a * a, ones) + NORM_EPS) * scale)
                ref[2 * p] = a[:, :HEAD_DIM]
                ref[2 * p + 1] = a[:, HEAD_DIM:]
        for h in range(HEADS):
            b_ref[h] = jnp.broadcast_to(beta[:, h:h + 1], (tm, HEAD_DIM))
            g_ref[h] = jnp.broadcast_to(gc[:, HEADS + h:HEADS + h + 1], (tm, HEAD_DIM))
            gl_ref[h] = jnp.broadcast_to(gl[:, HEADS + h:HEADS + h + 1], (tm, HEAD_DIM))
            for cc in range(nc):
                grow_ref[h, cc] = grow[h:h + 1, cc * GDN_CHUNK:(cc + 1) * GDN_CHUNK]

    hm = pl.BlockSpec((HEADS, tm, HEAD_DIM), lambda i: (0, i, 0))
    small = pl.BlockSpec((1, 128), lambda i: (0, 0))
    hm_shape = jax.ShapeDtypeStruct((HEADS, t, HEAD_DIM), F32)
    return pl.pallas_call(
        body, name="gdn_prep", grid=(t // tm,),
        in_specs=[pl.BlockSpec((tm, 3 * WIDTH), lambda i: (i, 0)),
                  pl.BlockSpec((tm, WIDTH), lambda i: (i, SEG_BA // WIDTH)), small, small],
        out_specs=[hm] * 6 + [pl.BlockSpec((HEADS, nc, 1, GDN_CHUNK), lambda i: (0, i, 0, 0))],
        out_shape=[hm_shape] * 6 + [jax.ShapeDtypeStruct((HEADS, t // GDN_CHUNK, 1, GDN_CHUNK), F32)],
        compiler_params=_params(("parallel",)),
    )(conv, proj, a_log, dt_bias)


def _gdn_prep_bwd(dproj, conv, proj, a_log, dt_bias, dq, dk, dv, db, dg, dgl, dgrow, tm=256):
    t = proj.shape[0]
    nc = tm // GDN_CHUNK

    def body(dp_ref, c_ref, ba_ref, al_ref, dt_ref, dq_ref, dk_ref, dv_ref, db_ref, dg_ref, dgl_ref, dgrow_ref,
             dba_ref, dc_ref, small_ref, row_scr, pair_scr):
        beta, g, sig_al = _gdn_gates(ba_ref[:, 0:128], al_ref[...], dt_ref[...])
        d_beta = jnp.zeros((tm, 128), F32)
        d_gc = jnp.zeros((tm, 128), F32)
        d_gl = jnp.zeros((tm, 128), F32)
        ones = _pair_ones()

        def pair(ref, p):
            pair_scr[:, :HEAD_DIM] = ref[2 * p]
            pair_scr[:, HEAD_DIM:] = ref[2 * p + 1]
            return pair_scr[...]

        for h in range(HEADS):
            for cc in range(nc):
                row_scr[h:h + 1, cc * GDN_CHUNK:(cc + 1) * GDN_CHUNK] = dgrow_ref[h, cc]
        for p in range(HEADS // 2):
            d_beta += _pair_sum_lanes(pair(db_ref, p), _pair_to_lanes(p, 0))
            d_gc += _pair_sum_lanes(pair(dg_ref, p), _pair_to_lanes(p, HEADS))
            d_gl += _pair_sum_lanes(pair(dgl_ref, p), _pair_to_lanes(p, HEADS))
            for seg, ref, scale in ((0, dq_ref, HEAD_DIM ** -0.5), (1, dk_ref, 1.0), (2, dv_ref, None)):
                cols = slice(seg * WIDTH + p * 128, seg * WIDTH + (p + 1) * 128)
                c = c_ref[:, cols]
                sg = _sigmoid(c)
                da = pair(ref, p)
                if scale is not None:
                    a = c * sg
                    r = lax.rsqrt(_pair_sum_lanes(a * a, ones) + NORM_EPS)
                    da = da * scale
                    da = r * da - a * (r * r * r) * _pair_sum_lanes(da * a, ones)
                dc_ref[:, cols] = da * (sg * (1.0 + c * (1.0 - sg)))
        d_gc += lax.dot_general(row_scr[...], _head_lane_eye(), _TN, precision=HIGHEST, preferred_element_type=F32)
        lmat, cmat = _chunk_matrices(tm)
        d_g = (lax.dot_general(lmat, d_gc, _TN, precision=HIGHEST, preferred_element_type=F32)
               + lax.dot_general(cmat, d_gl, _TN, precision=HIGHEST, preferred_element_type=F32))
        d_al = d_g * (-jnp.exp(al_ref[...])) * sig_al
        d_bl = d_beta * beta * (1.0 - beta)
        dba_ref[...] = jnp.concatenate([d_bl + d_al, jnp.zeros((tm, WIDTH - 128), F32)], axis=1).astype(dba_ref.dtype)
        row8 = _iota((8, 128), 0)
        part = (jnp.where(row8 == 0, jnp.sum(d_g * g, axis=0, keepdims=True), 0.0)
                + jnp.where(row8 == 1, jnp.sum(d_al, axis=0, keepdims=True), 0.0))

        @pl.when(pl.program_id(0) == 0)
        def _():
            small_ref[...] = jnp.zeros_like(small_ref)

        small_ref[...] += part

    hm = pl.BlockSpec((HEADS, tm, HEAD_DIM), lambda i: (0, i, 0))
    small = pl.BlockSpec((1, 128), lambda i: (0, 0))
    seg = pl.BlockSpec((tm, WIDTH), lambda i: (i, SEG_BA // WIDTH))
    return pl.pallas_call(
        body, name="gdn_prep_bwd", grid=(t // tm,),
        in_specs=[pl.BlockSpec(memory_space=pl.ANY), pl.BlockSpec((tm, 3 * WIDTH), lambda i: (i, 0)), seg, small, small]
        + [hm] * 6 + [pl.BlockSpec((HEADS, nc, 1, GDN_CHUNK), lambda i: (0, i, 0, 0))],
        out_specs=[seg, pl.BlockSpec((tm, 3 * WIDTH), lambda i: (i, 0)), pl.BlockSpec((8, 128), lambda i: (0, 0))],
        out_shape=[jax.ShapeDtypeStruct((t, PACKED_WIDTH), MXU_DTYPE), jax.ShapeDtypeStruct((t, 3 * WIDTH), F32),
                   jax.ShapeDtypeStruct((8, 128), F32)],
        scratch_shapes=[pltpu.VMEM((HEADS, tm), F32), pltpu.VMEM((tm, 128), F32)],
        input_output_aliases={0: 0},
        compiler_params=_params(("arbitrary",)),
    )(dproj, conv, proj, a_log, dt_bias, dq, dk, dv, db, dg, dgl, dgrow)


_BNN = (((2,), (1,)), ((0,), (0,)))
_BNT = (((2,), (2,)), ((0,), (0,)))
_BTN = (((1,), (1,)), ((0,), (0,)))


@jax.custom_vjp
def _MM_NN(a, b):
    return _mxu(a, b, _BNN)


@jax.custom_vjp
def _MM_NT(a, b):
    return _mxu(a, b, _BNT)


@jax.custom_vjp
def _MM_TN(a, b):
    return _mxu(a, b, _BTN)


_MM_NN.defvjp(lambda a, b: (_mxu(a, b, _BNN), (a, b)), lambda r, g: (_mxu(g, r[1], _BNT), _mxu(r[0], g, _BTN)))
_MM_NT.defvjp(lambda a, b: (_mxu(a, b, _BNT), (a, b)), lambda r, g: (_mxu(g, r[1], _BNN), _mxu(g, r[0], _BTN)))
_MM_TN.defvjp(lambda a, b: (_mxu(a, b, _BTN), (a, b)), lambda r, g: (_mxu(r[1], g, _BNT), _mxu(r[0], g, _BNN)))


def _split(a):
    hi = a.astype(MXU_DTYPE)
    return hi, (a - hi.astype(F32)).astype(MXU_DTYPE)


def _dot3(a, b, dims):
    (ah, al), (bh, bl) = a, b
    (ca,), (cb,) = dims[0]
    return lax.dot_general(jnp.concatenate([ah, ah, al], axis=ca), jnp.concatenate([bh, bl, bh], axis=cb), dims,
                           preferred_element_type=F32)


def _unit_lower_inverse(a):
    c = GDN_CHUNK
    eye = jnp.where(_iota((c, c), 0) == _iota((c, c), 1), 1.0, 0.0)
    x = eye - a
    p = a
    for _ in range(5):
        ps = _split(p)
        p = _dot3(ps, ps, _BNN)
        x = x + _dot3(_split(x), _split(p), _BNN)
    return x


@jax.custom_vjp
def _SAVED_INVERSE(a, t_inv):
    return t_inv


def _saved_inverse_bwd(t_inv, g):
    ts = _split(t_inv)
    return -_dot3(ts, _split(_dot3(_split(g), ts, _BNT)), _BTN), jnp.zeros_like(t_inv)


_SAVED_INVERSE.defvjp(lambda a, t_inv: (t_inv, t_inv), _saved_inverse_bwd)


def _gdn_chunk(q, k, v, beta, g1, g2, gl, state, t_inv=None):
    c = GDN_CHUNK
    if t_inv is None:
        _mm_nn, _mm_nt, _mm_tn = (functools.partial(_mxu, dims=dd) for dd in (_BNN, _BNT, _BTN))
    else:
        _mm_nn, _mm_nt, _mm_tn = _MM_NN, _MM_NT, _MM_TN
    row, col = _iota((c, c), 0), _iota((c, c), 1)
    incl, strict = row >= col, row > col
    decay = jnp.where(incl, jnp.exp(jnp.where(incl, g1 - g2, 0.0)), 0.0)
    eg = jnp.exp(g1)
    kb = k * beta
    a = _mm_nt(kb, k) * jnp.where(strict, decay, 0.0)
    inv = _unit_lower_inverse(a) if t_inv is None else _SAVED_INVERSE(a, t_inv)
    u = _mm_nn(inv, v * beta)
    w = _mm_nn(inv, kb * eg)
    attn = _mm_nt(q, k) * decay
    v_new = u - _mm_nn(w, state)
    o = _mm_nn(q * eg, state) + _mm_nn(attn, v_new)
    new_state = state * jnp.exp(gl) + _mm_tn(k * jnp.exp(gl - g1), v_new)
    return (o, new_state, inv) if t_inv is None else (o, new_state)


def _gdn_fwd(q, k, v, beta, g, gl, grow, cpb=2):
    t = q.shape[1]
    rows = cpb * GDN_CHUNK

    def body(q_ref, k_ref, v_ref, b_ref, g_ref, gl_ref, grow_ref, o_ref, st_ref, inv_ref, state):
        @pl.when(pl.program_id(0) == 0)
        def _():
            state[...] = jnp.zeros_like(state)

        s = state[...]
        for cc in range(cpb):
            sl = slice(cc * GDN_CHUNK, (cc + 1) * GDN_CHUNK)
            st_ref[:, cc] = s
            g2 = jnp.broadcast_to(grow_ref[:, cc], (HEADS, GDN_CHUNK, GDN_CHUNK))
            o, s, inv = _gdn_chunk(q_ref[:, sl, :], k_ref[:, sl, :], v_ref[:, sl, :], b_ref[:, sl, :], g_ref[:, sl, :], g2,
                                   gl_ref[:, sl, :], s)
            o_ref[:, sl, :] = o
            inv_ref[:, cc] = inv
        state[...] = s

    hm = pl.BlockSpec((HEADS, rows, HEAD_DIM), lambda i: (0, i, 0))
    per_chunk = pl.BlockSpec((HEADS, cpb, GDN_CHUNK, HEAD_DIM), lambda i: (0, i, 0, 0))
    chunk_shape = jax.ShapeDtypeStruct((HEADS, t // GDN_CHUNK, GDN_CHUNK, HEAD_DIM), F32)
    return pl.pallas_call(
        body, name="gdn_fwd", grid=(t // rows,),
        in_specs=[hm] * 6 + [pl.BlockSpec((HEADS, cpb, 1, GDN_CHUNK), lambda i: (0, i, 0, 0))],
        out_specs=[hm, per_chunk, per_chunk],
        out_shape=[jax.ShapeDtypeStruct((HEADS, t, HEAD_DIM), F32), chunk_shape, chunk_shape],
        scratch_shapes=[pltpu.VMEM((HEADS, GDN_CHUNK, HEAD_DIM), F32)],
        compiler_params=_params(("arbitrary",)),
    )(q, k, v, beta, g, gl, grow)


def _gdn_bwd(q, k, v, beta, g, gl, grow, states, invs, do, cpb=1):
    t = q.shape[1]
    rows = cpb * GDN_CHUNK
    nsteps = t // rows

    def body(q_ref, k_ref, v_ref, b_ref, g_ref, gl_ref, grow_ref, st_ref, inv_ref, do_ref,
             dq_ref, dk_ref, dv_ref, db_ref, dg_ref, dgl_ref, dgrow_ref, dstate):
        @pl.when(pl.program_id(0) == 0)
        def _():
            dstate[...] = jnp.zeros_like(dstate)

        ds = dstate[...]
        for cc in reversed(range(cpb)):
            sl = slice(cc * GDN_CHUNK, (cc + 1) * GDN_CHUNK)
            g2 = jnp.broadcast_to(grow_ref[:, cc], (HEADS, GDN_CHUNK, GDN_CHUNK))
            _, vjp = jax.vjp(_gdn_chunk, q_ref[:, sl, :], k_ref[:, sl, :], v_ref[:, sl, :], b_ref[:, sl, :],
                             g_ref[:, sl, :], g2, gl_ref[:, sl, :], st_ref[:, cc], inv_ref[:, cc])
            gq, gk, gv, gb, gg1, gg2, ggl, ds, _ = vjp((do_ref[:, sl, :], ds))
            dq_ref[:, sl, :] = gq
            dk_ref[:, sl, :] = gk
            dv_ref[:, sl, :] = gv
            db_ref[:, sl, :] = gb
            dg_ref[:, sl, :] = gg1
            dgl_ref[:, sl, :] = ggl
            dgrow_ref[:, cc] = jnp.sum(gg2, axis=1, keepdims=True)
        dstate[...] = ds

    hm = pl.BlockSpec((HEADS, rows, HEAD_DIM), lambda i: (0, nsteps - 1 - i, 0))
    rowspec = pl.BlockSpec((HEADS, cpb, 1, GDN_CHUNK), lambda i: (0, nsteps - 1 - i, 0, 0))
    per_chunk = pl.BlockSpec((HEADS, cpb, GDN_CHUNK, HEAD_DIM), lambda i: (0, nsteps - 1 - i, 0, 0))
    hm_shape = jax.ShapeDtypeStruct((HEADS, t, HEAD_DIM), F32)
    return pl.pallas_call(
        body, name="gdn_bwd", grid=(nsteps,),
        in_specs=[hm] * 6 + [rowspec, per_chunk, per_chunk, hm],
        out_specs=[hm] * 6 + [rowspec],
        out_shape=[hm_shape] * 6 + [jax.ShapeDtypeStruct((HEADS, t // GDN_CHUNK, 1, GDN_CHUNK), F32)],
        scratch_shapes=[pltpu.VMEM((HEADS, GDN_CHUNK, HEAD_DIM), F32)],
        compiler_params=_params(("arbitrary",)),
    )(q, k, v, beta, g, gl, grow, states, invs, do)


def _gdn_out(o_hm, gdn_norm_w, proj, tm=256):
    t = proj.shape[0]

    def body(o_ref, w_ref, z_ref, oz_ref, pair_scr):
        w = w_ref[...]
        ones = _pair_ones()
        for p in range(HEADS // 2):
            cols = slice(p * 128, (p + 1) * 128)
            pair_scr[:, :HEAD_DIM] = o_ref[2 * p]
            pair_scr[:, HEAD_DIM:] = o_ref[2 * p + 1]
            o = pair_scr[...]
            z = z_ref[:, cols]
            r = lax.rsqrt(_pair_sum_lanes(o * o, ones) * (1.0 / HEAD_DIM) + NORM_EPS)
            oz_ref[:, cols] = (o * r * w * (z * _sigmoid(z))).astype(oz_ref.dtype)

    tok = pl.BlockSpec((tm, WIDTH), lambda i: (i, 0))
    return pl.pallas_call(
        body, name="gdn_out", grid=(t // tm,),
        in_specs=[pl.BlockSpec((HEADS, tm, HEAD_DIM), lambda i: (0, i, 0)), pl.BlockSpec((1, 128), lambda i: (0, 0)),
                  pl.BlockSpec((tm, WIDTH), lambda i: (i, SEG_ZB // WIDTH))],
        out_specs=tok, out_shape=jax.ShapeDtypeStruct((t, WIDTH), MXU_DTYPE),
        scratch_shapes=[pltpu.VMEM((tm, 128), F32)],
        compiler_params=_params(("parallel",)),
    )(o_hm, jnp.tile(gdn_norm_w, (1, 2)), proj)


def _gdn_out_bwd(dproj, d_oz, o_hm, gdn_norm_w, proj, tm=256):
    t = proj.shape[0]

    def body(dp_ref, doz_ref, o_ref, w_ref, z_ref, dz_ref, do_ref, dw_ref, pair_scr):
        w = w_ref[...]
        ones = _pair_ones()
        dw = jnp.zeros((1, 128), F32)
        for p in range(HEADS // 2):
            cols = slice(p * 128, (p + 1) * 128)
            pair_scr[:, :HEAD_DIM] = o_ref[2 * p]
            pair_scr[:, HEAD_DIM:] = o_ref[2 * p + 1]
            o = pair_scr[...]
            z, g = z_ref[:, cols], doz_ref[:, cols]
            sg = _sigmoid(z)
            r = lax.rsqrt(_pair_sum_lanes(o * o, ones) * (1.0 / HEAD_DIM) + NORM_EPS)
            dz_ref[:, cols] = (g * (o * r * w) * (sg * (1.0 + z * (1.0 - sg)))).astype(dz_ref.dtype)
            dn = g * (z * sg)
            dw += jnp.sum(dn * o * r, axis=0, keepdims=True)
            dnw = dn * w
            do = r * dnw - o * (r * r * r) * (_pair_sum_lanes(dnw * o, ones) * (1.0 / HEAD_DIM))
            do_ref[2 * p] = do[:, :HEAD_DIM]
            do_ref[2 * p + 1] = do[:, HEAD_DIM:]

        @pl.when(pl.program_id(0) == 0)
        def _():
            dw_ref[...] = jnp.zeros_like(dw_ref)

        dw_ref[...] += jnp.where(_iota((8, 128), 0) == 0, dw, 0.0)

    tok = pl.BlockSpec((tm, WIDTH), lambda i: (i, 0))
    seg = pl.BlockSpec((tm, WIDTH), lambda i: (i, SEG_ZB // WIDTH))
    hm = pl.BlockSpec((HEADS, tm, HEAD_DIM), lambda i: (0, i, 0))
    dz, do, dw = pl.pallas_call(
        body, name="gdn_out_bwd", grid=(t // tm,),
        in_specs=[pl.BlockSpec(memory_space=pl.ANY), tok, hm, pl.BlockSpec((1, 128), lambda i: (0, 0)), seg],
        out_specs=[seg, hm, pl.BlockSpec((8, 128), lambda i: (0, 0))],
        out_shape=[jax.ShapeDtypeStruct((t, PACKED_WIDTH), MXU_DTYPE), jax.ShapeDtypeStruct((HEADS, t, HEAD_DIM), F32),
                   jax.ShapeDtypeStruct((8, 128), F32)],
        scratch_shapes=[pltpu.VMEM((tm, 128), F32)],
        input_output_aliases={0: 0},
        compiler_params=_params(("arbitrary",)),
    )(dproj, d_oz, o_hm, jnp.tile(gdn_norm_w, (1, 2)), proj)
    return dz, do, dw[:, :HEAD_DIM] + dw[:, HEAD_DIM:]


def _merge(y_a, y_b, proj, tm=256):
    t = proj.shape[0]

    def body(ya_ref, yb_ref, ga_ref, gb_ref, m_ref):
        m_ref[...] = (_sigmoid(ga_ref[...]) * ya_ref[...] + _sigmoid(gb_ref[...]) * yb_ref[...]).astype(m_ref.dtype)

    half = pl.BlockSpec((tm, WIDTH), lambda i, c: (i, c))
    return pl.pallas_call(
        body, name="merge", grid=(t // tm, 2),
        in_specs=[half, half, pl.BlockSpec((tm, WIDTH), lambda i, c: (i, SEG_GA // WIDTH + c)),
                  pl.BlockSpec((tm, WIDTH), lambda i, c: (i, SEG_GB // WIDTH + c))],
        out_specs=half, out_shape=jax.ShapeDtypeStruct((t, D_MODEL), MXU_DTYPE),
        compiler_params=_params(("parallel", "parallel")),
    )(y_a, y_b, proj, proj)


def _merge_bwd(dproj, d_m, y, proj, seg, name, tm=256):
    t = proj.shape[0]

    def body(*refs):
        dm_ref, y_ref, g_ref, dg_ref, dy_ref = refs[-5:]
        dm = dm_ref[...]
        s = _sigmoid(g_ref[...])
        dy_ref[...] = (dm * s).astype(dy_ref.dtype)
        dg_ref[...] = (dm * y_ref[...] * s * (1.0 - s)).astype(dg_ref.dtype)

    half = pl.BlockSpec((tm, WIDTH), lambda i, c: (i, c))
    gate = pl.BlockSpec((tm, WIDTH), lambda i, c: (i, seg // WIDTH + c))
    specs, args, aliases = [half, half, gate], [d_m, y, proj], {}
    if dproj is not None:
        specs, args, aliases = [pl.BlockSpec(memory_space=pl.ANY)] + specs, [dproj] + args, {0: 0}
    return pl.pallas_call(
        body, name=name, grid=(t // tm, 2), in_specs=specs, out_specs=[gate, half],
        out_shape=[jax.ShapeDtypeStruct((t, PACKED_WIDTH), MXU_DTYPE), jax.ShapeDtypeStruct((t, D_MODEL), MXU_DTYPE)],
        input_output_aliases=aliases,
        compiler_params=_params(("parallel", "parallel")),
    )(*args)


def _tail(x, mo, final_w, target, tm=256):
    t = x.shape[0]

    def body(x_ref, mo_ref, w_ref, t_ref, dxm_ref, dx_ref, loss_ref, dw_ref):
        x2 = x_ref[...] + mo_ref[...]
        w = w_ref[...]
        r = lax.rsqrt(jnp.mean(x2 * x2, axis=-1, keepdims=True) + NORM_EPS)
        xn = x2 * r
        err = xn * w - t_ref[...]
        dy = err * (1.0 / D_MODEL)
        dyw = dy * w
        dx2 = r * dyw - x2 * (r * r * r) * jnp.mean(dyw * x2, axis=-1, keepdims=True)
        dx_ref[...] = dx2
        dxm_ref[...] = dx2.astype(dxm_ref.dtype)
        loss = 0.5 * jnp.sum(jnp.sum(err * err, axis=-1, keepdims=True) * (1.0 / D_MODEL), axis=0, keepdims=True)
        onehot = jnp.where((_iota((8, 128), 0) == 0) & (_iota((8, 128), 1) == 0), 1.0, 0.0)

        @pl.when(pl.program_id(0) == 0)
        def _():
            loss_ref[...] = jnp.zeros_like(loss_ref)
            dw_ref[...] = jnp.zeros_like(dw_ref)

        loss_ref[...] += loss * onehot
        dw_ref[...] += jnp.where(_iota((8, D_MODEL), 0) == 0, jnp.sum(dy * xn, axis=0, keepdims=True), 0.0)

    tok = pl.BlockSpec((tm, D_MODEL), lambda i: (i, 0))
    return pl.pallas_call(
        body, name="tail", grid=(t // tm,),
        in_specs=[tok, tok, pl.BlockSpec((1, D_MODEL), lambda i: (0, 0)), tok],
        out_specs=[tok, tok, pl.BlockSpec((8, 128), lambda i: (0, 0)), pl.BlockSpec((8, D_MODEL), lambda i: (0, 0))],
        out_shape=[jax.ShapeDtypeStruct((t, D_MODEL), MXU_DTYPE), jax.ShapeDtypeStruct((t, D_MODEL), F32),
                   jax.ShapeDtypeStruct((8, 128), F32), jax.ShapeDtypeStruct((8, D_MODEL), F32)],
        compiler_params=_params(("arbitrary",)),
    )(x, mo, final_w, target)


def _norm_bwd(x, norm_w, dh, dx2, tm=256):
    t = x.shape[0]

    def body(x_ref, w_ref, dh_ref, dx2_ref, dx_ref, dw_ref):
        xf, w, dh_ = x_ref[...], w_ref[...], dh_ref[...]
        r = lax.rsqrt(jnp.mean(xf * xf, axis=-1, keepdims=True) + NORM_EPS)
        dhw = dh_ * w
        dx_ref[...] = dx2_ref[...] + r * dhw - xf * (r * r * r) * jnp.mean(dhw * xf, axis=-1, keepdims=True)

        @pl.when(pl.program_id(0) == 0)
        def _():
            dw_ref[...] = jnp.zeros_like(dw_ref)

        dw_ref[...] += jnp.where(_iota((8, D_MODEL), 0) == 0, jnp.sum(dh_ * xf * r, axis=0, keepdims=True), 0.0)

    tok = pl.BlockSpec((tm, D_MODEL), lambda i: (i, 0))
    return pl.pallas_call(
        body, name="norm_bwd", grid=(t // tm,),
        in_specs=[tok, pl.BlockSpec((1, D_MODEL), lambda i: (0, 0)), tok, tok],
        out_specs=[tok, pl.BlockSpec((8, D_MODEL), lambda i: (0, 0))],
        out_shape=[jax.ShapeDtypeStruct((t, D_MODEL), F32), jax.ShapeDtypeStruct((8, D_MODEL), F32)],
        compiler_params=_params(("arbitrary",)),
    )(x, norm_w, dh, dx2)


def _local_step(x, target, norm_w, wp, conv_w, a_log, dt_bias, gdn_norm_w, w_up_a, w_up_b, w_out, final_w, start_reduce):
    t = x.shape[0]
    tables = _rope_tables(t)
    a_log = jnp.pad(a_log, ((0, 0), (HEADS, 128 - 2 * HEADS)))
    dt_bias = jnp.pad(dt_bias, ((0, 0), (HEADS, 128 - 2 * HEADS)))

    proj, h_t = _norm_proj(x, norm_w, wp)
    qkvs = _rope_fwd(proj, tables)
    outs, lses = zip(*[_att_fwd(qkvs[gi], d, f"att_fwd{gi}") for gi, d in enumerate(DILATIONS)])
    oz_a, o_a, *lse_views = _att_merge(outs, lses, proj)
    conv = _conv_fwd(proj, conv_w)
    gq, gk, gv, gb, gg, ggl, grow = _gdn_prep(conv, proj, a_log, dt_bias)
    o_b, states, invs = _gdn_fwd(gq, gk, gv, gb, gg, ggl, grow)
    oz_b = _gdn_out(o_b, gdn_norm_w, proj)
    big = dict(tm=1024, tn=1024, tk=1024)
    y_a = _matmul(oz_a, w_up_a, "nn", "up_a", **big)
    y_b = _matmul(oz_b, w_up_b, "nn", "up_b", **big)
    merged = _merge(y_a, y_b, proj)
    mo = _matmul(merged, w_out, "nn", "out_proj", **big)
    dx2_m, dx2, loss_blk, d_final = _tail(x, mo, final_w, target)

    d_wout = _matmul(merged, dx2_m, "tn", "d_w_out", **big)
    d_m = _matmul(dx2_m, w_out, "nt", "d_merged", **big)
    dproj, dy_a = _merge_bwd(None, d_m, y_a, proj, SEG_GA, "merge_bwd_a")
    dproj, dy_b = _merge_bwd(dproj, d_m, y_b, proj, SEG_GB, "merge_bwd_b")
    d_wua = _matmul(oz_a, dy_a, "tn", "d_w_up_a", **big)
    d_wub = _matmul(oz_b, dy_b, "tn", "d_w_up_b", **big)
    d_oz_a = _matmul(dy_a, w_up_a, "nt", "d_oz_a", **big)
    d_oz_b = _matmul(dy_b, w_up_b, "nt", "d_oz_b", **big)
    dproj, *views = _att_merge_bwd(dproj, d_oz_a, o_a, proj)
    do_views, delta_views = views[:3], views[3:]
    dqkvs = [_att_bwd(qkvs[gi], do_views[gi], lse_views[gi], delta_views[gi], d, f"att_bwd{gi}")
             for gi, d in enumerate(DILATIONS)]
    dproj = _rope_bwd(dproj, dqkvs, tables)
    dproj, do_b, d_gnw = _gdn_out_bwd(dproj, d_oz_b, o_b, gdn_norm_w, proj)
    dgq, dgk, dgv, dgb, dgg, dggl, dgrow = _gdn_bwd(gq, gk, gv, gb, gg, ggl, grow, states, invs, do_b)
    dproj, dconv, d_small = _gdn_prep_bwd(dproj, conv, proj, a_log, dt_bias, dgq, dgk, dgv, dgb, dgg, dggl, dgrow)
    dproj, d_convw = _conv_bwd(dproj, dconv, proj, conv_w)
    d_wp = _matmul(h_t, dproj, "nn", "d_w_in", tm=1024, tn=PACKED_WIDTH // 4, tk=1024)
    in_flight, token = start_reduce(d_wp, d_wua, d_wub, d_wout, d_convw[0:GDN_CONV])
    dh = _matmul(dproj, wp, "nt", "d_h", tm=1024, tn=1024, tk=PACKED_WIDTH // 4, after=token)
    grad_x, d_norm = _norm_bwd(x, norm_w, dh, dx2)
    return dict(loss=loss_blk, grad_x=grad_x, norm_w=d_norm[0:1], in_flight=in_flight,
                a_log=d_small[0:1, HEADS:2 * HEADS], dt_bias=d_small[1:2, HEADS:2 * HEADS], gdn_norm_w=d_gnw[0:1],
                final_norm_w=d_final[0:1])


SHARDS = 4
W_IN_SHARD = IN_WIDTH // SHARDS
ROWS_UP = WIDTH * (D_MODEL // SHARDS) // 128
ROWS_OUT = (D_MODEL // SHARDS) * D_MODEL // 128
CONV_SHARD = 3 * WIDTH // SHARDS
ROWS_CONV = 16
SLAB_ROWS = 2 * ROWS_UP + ROWS_OUT + 2 * ROWS_CONV
HALF_ROWS = SLAB_ROWS // 2
BIG_HALF = (D_MODEL // 2, W_IN_SHARD)
SMALL_HALF = (HALF_ROWS, 128)
MESH = pl.DeviceIdType.MESH
ANY = pl.BlockSpec(memory_space=pl.ANY)


def _pad_rows(a, rows):
    return jnp.pad(a, ((0, rows - a.shape[0]), (0, 0)))


def _pack_slab(w_up_a, w_up_b, w_out, conv, conv_lo):
    parts = [w_up_a.reshape(ROWS_UP, 128), w_up_b.reshape(ROWS_UP, 128), w_out.reshape(ROWS_OUT, 128),
             _pad_rows(conv.reshape(-1, 128), ROWS_CONV), _pad_rows(conv_lo.reshape(-1, 128), ROWS_CONV)]
    return jnp.concatenate(parts, axis=0).reshape(2, *SMALL_HALF)


def _unpack_slab(slab):
    slab = slab.reshape(SLAB_ROWS, 128)
    r0 = 0
    out = []
    for rows, shape in ((ROWS_UP, (WIDTH, D_MODEL // SHARDS)), (ROWS_UP, (WIDTH, D_MODEL // SHARDS)),
                        (ROWS_OUT, (D_MODEL // SHARDS, D_MODEL)), (ROWS_CONV, None), (ROWS_CONV, None)):
        part = slab[r0:r0 + rows]
        out.append(part[:GDN_CONV * CONV_SHARD // 128].reshape(GDN_CONV, CONV_SHARD) if shape is None else part.reshape(shape))
        r0 += rows
    return out


def _mesh_position():
    x, y, c = lax.axis_index("x"), lax.axis_index("y"), lax.axis_index("c")
    return x, y, c, [(1 - x, y), (x, 1 - y), (1 - x, 1 - y)]


def _gather_weights(shards):
    n = len(shards)

    def body(*refs):
        in_refs, out_refs, (send_sems, recv_sems) = refs[:n], refs[n:2 * n], refs[2 * n:]
        x, y, c, chips = _mesh_position()

        def half(a, chip, which):
            return out_refs[a].at[2 * chip[0] + chip[1], which]

        def copy(k, src, dst, to):
            return pltpu.make_async_remote_copy(src_ref=src, dst_ref=dst, send_sem=send_sems.at[k], recv_sem=recv_sems.at[k],
                                                device_id=to, device_id_type=MESH)

        pairs = [(a, j, chip) for a in range(n) for j, chip in enumerate(chips)]
        first = [copy(6 * a + j, in_refs[a].at[c], half(a, (x, y), c), (*chip, c)) for a, j, chip in pairs]
        for cp in first:
            cp.start()
        passed = [copy(6 * a + 3 + j, half(a, chip, c), half(a, chip, c), (x, y, 1 - c)) for a, j, chip in pairs]
        for i, (a, j, chip) in enumerate(pairs):
            copy(6 * a + j, half(a, chip, c), half(a, chip, c), (x, y, c)).wait_recv()
            passed[i].start()
        for a, j, chip in pairs:
            copy(6 * a + 3 + j, half(a, chip, 1 - c), half(a, chip, 1 - c), (x, y, c)).wait_recv()
        for cp in first + passed:
            cp.wait_send()

    return pl.pallas_call(
        body, name="gather_weights", in_specs=[ANY] * n, out_specs=[ANY] * n,
        out_shape=[jax.ShapeDtypeStruct((SHARDS, *s.shape), s.dtype) for s in shards],
        scratch_shapes=[pltpu.SemaphoreType.DMA((6 * n,)), pltpu.SemaphoreType.DMA((6 * n,))],
    )(*shards)


def _exchange_halves(grads):
    n = len(grads)

    def body(*refs):
        g_refs, out_refs, (send_sems, recv_sems) = refs[:n], refs[n:2 * n], refs[2 * n:]
        x, y, c, _ = _mesh_position()
        copies = [pltpu.make_async_remote_copy(src_ref=g_refs[a].at[s, 1 - c], dst_ref=out_refs[a].at[s],
                                               send_sem=send_sems.at[SHARDS * a + s], recv_sem=recv_sems.at[SHARDS * a + s],
                                               device_id=(x, y, 1 - c), device_id_type=MESH)
                  for a in range(n) for s in range(SHARDS)]
        for cp in copies:
            cp.start()
        for cp in copies:
            cp.wait()

    return pl.pallas_call(
        body, name="exchange_halves", in_specs=[ANY] * n, out_specs=[ANY] * n,
        out_shape=[jax.ShapeDtypeStruct((SHARDS, *g.shape[2:]), F32) for g in grads],
        scratch_shapes=[pltpu.SemaphoreType.DMA((SHARDS * n,)), pltpu.SemaphoreType.DMA((SHARDS * n,))],
    )(*grads)


def _pair_sum(grads, recv, blk, name):
    _, _, rows, cols = grads.shape

    def body(c_ref, g_ref, r_ref, o_ref):
        o_ref[...] = (g_ref[0] + r_ref[...]).astype(o_ref.dtype)

    spec = pl.BlockSpec((1, blk, cols), lambda s, i, c_ref: (s, i, 0))
    return pl.pallas_call(
        body, name=name,
        grid_spec=pltpu.PrefetchScalarGridSpec(
            num_scalar_prefetch=1, grid=(SHARDS, rows // blk),
            in_specs=[pl.BlockSpec((1, 1, blk, cols), lambda s, i, c_ref: (s, c_ref[0], i, 0)), spec],
            out_specs=spec),
        out_shape=jax.ShapeDtypeStruct((SHARDS, rows, cols), MXU_DTYPE),
        compiler_params=_params(("parallel", "parallel")),
    )(lax.axis_index("c").astype(jnp.int32).reshape(1), grads, recv)


_HBM = pl.BlockSpec(memory_space=pltpu.HBM)
_SEM = pl.BlockSpec(memory_space=pltpu.SEMAPHORE)
_DATAFLOW = pltpu.SideEffectType.DATAFLOW_SIDE_EFFECTING


def _scatter_copies(p_refs, l_refs, send_sems, recv_sems):
    x, y, c, chips = _mesh_position()
    return [pltpu.make_async_remote_copy(src_ref=p_refs[a].at[2 * chip[0] + chip[1]], dst_ref=l_refs[a].at[j],
                                         send_sem=send_sems.at[3 * a + j], recv_sem=recv_sems.at[3 * a + j],
                                         device_id=(*chip, c), device_id_type=MESH)
            for a in range(len(p_refs)) for j, chip in enumerate(chips)]


def _scatter_start(pairs):
    n = len(pairs)
    lands = [lax.empty((3, *p.shape[1:]), p.dtype) for p in pairs]

    def body(*refs):
        p_refs, l_refs, send_sems, recv_sems, token = refs[:n], refs[n:2 * n], refs[2 * n], refs[2 * n + 1], refs[-1]
        for cp in _scatter_copies(p_refs, l_refs, send_sems, recv_sems):
            cp.start()
        token[...] = jnp.zeros_like(token)

    operands = [pltpu.with_memory_space_constraint(a, pltpu.HBM) for a in (*pairs, *lands)]
    return pl.pallas_call(
        body, name="scatter_start", in_specs=[_HBM] * (2 * n),
        out_shape=(pltpu.SemaphoreType.DMA((3 * n,)), pltpu.SemaphoreType.DMA((3 * n,)),
                   *[pltpu.HBM(a.shape, a.dtype) for a in operands], jax.ShapeDtypeStruct((8, 128), F32)),
        out_specs=(_SEM, _SEM, *[_HBM] * (2 * n), pl.BlockSpec(memory_space=pltpu.VMEM)),
        input_output_aliases={i: 2 + i for i in range(2 * n)},
        compiler_params=pltpu.CompilerParams(has_side_effects=_DATAFLOW),
    )(*operands)


def _scatter_wait(send_sems, recv_sems, passed, after):
    n = len(passed) // 2

    def body(*refs):
        p_refs, l_refs, send_s, recv_s = refs[:n], refs[n:2 * n], refs[2 * n], refs[2 * n + 1]
        for cp in _scatter_copies(p_refs, l_refs, send_s, recv_s):
            cp.wait_send()
            cp.wait_recv()

    return pl.pallas_call(
        body, name="scatter_wait", in_specs=[_HBM] * (2 * n) + [_SEM, _SEM, ANY],
        out_shape=[pltpu.HBM(a.shape, a.dtype) for a in passed], out_specs=[_HBM] * (2 * n),
        input_output_aliases={i: i for i in range(2 * n)},
        compiler_params=pltpu.CompilerParams(has_side_effects=_DATAFLOW),
    )(*passed, send_sems, recv_sems, after)


def _chip_sum(pairs, recv, blk, name):
    _, rows, cols = pairs.shape

    def body(pos_ref, p_ref, r_ref, o_ref):
        o_ref[0] = ((p_ref[0].astype(F32) + r_ref[0].astype(F32)) + r_ref[1].astype(F32)) + r_ref[2].astype(F32)

    pos = jnp.stack([2 * lax.axis_index("x") + lax.axis_index("y"), lax.axis_index("c")]).astype(jnp.int32)
    return pl.pallas_call(
        body, name=name,
        grid_spec=pltpu.PrefetchScalarGridSpec(
            num_scalar_prefetch=1, grid=(rows // blk,),
            in_specs=[pl.BlockSpec((1, blk, cols), lambda i, pos_ref: (pos_ref[0], i, 0)),
                      pl.BlockSpec((3, blk, cols), lambda i, pos_ref: (0, i, 0))],
            out_specs=pl.BlockSpec((1, blk, cols), lambda i, pos_ref: (pos_ref[1], i, 0))),
        out_shape=jax.ShapeDtypeStruct((2, rows, cols), F32),
        compiler_params=_params(("parallel",)),
    )(pos, pairs, recv)


def _share_total(totals):
    n = len(totals)

    def body(*refs):
        t_refs, out_refs, (send_sems, recv_sems) = refs[:n], refs[n:2 * n], refs[2 * n:]
        x, y, c, _ = _mesh_position()
        copies = [pltpu.make_async_remote_copy(src_ref=t_refs[a].at[c], dst_ref=out_refs[a].at[c], send_sem=send_sems.at[a],
                                               recv_sem=recv_sems.at[a], device_id=(x, y, 1 - c), device_id_type=MESH)
                  for a in range(n)]
        for cp in copies:
            cp.start()
        for a in range(n):
            other = out_refs[a].at[1 - c]
            pltpu.make_async_remote_copy(src_ref=other, dst_ref=other, send_sem=send_sems.at[a], recv_sem=recv_sems.at[a],
                                         device_id=(x, y, c), device_id_type=MESH).wait_recv()
        for cp in copies:
            cp.wait_send()

    return pl.pallas_call(
        body, name="share_total", in_specs=[ANY] * n, out_specs=[ANY] * n,
        out_shape=[jax.ShapeDtypeStruct(t.shape, F32) for t in totals],
        scratch_shapes=[pltpu.SemaphoreType.DMA((n,)), pltpu.SemaphoreType.DMA((n,))],
        input_output_aliases={a: a for a in range(n)},
    )(*totals)


def _allreduce_small(block):
    def body(b_ref, out_ref, gath, send_sems, recv_sems):
        x, y, c, _ = _mesh_position()
        me = 4 * x + 2 * y + c
        gath[me] = b_ref[...]
        copies = []
        for k in range(1, 8):
            peer = (x ^ (k >> 2), y ^ ((k >> 1) & 1), c ^ (k & 1))
            copies.append(pltpu.make_async_remote_copy(src_ref=b_ref, dst_ref=gath.at[me], send_sem=send_sems.at[k - 1],
                                                       recv_sem=recv_sems.at[k - 1], device_id=peer, device_id_type=MESH))
        for cp in copies:
            cp.start()
        for k in range(1, 8):
            src = 4 * (x ^ (k >> 2)) + 2 * (y ^ ((k >> 1) & 1)) + (c ^ (k & 1))
            pltpu.make_async_remote_copy(src_ref=b_ref, dst_ref=gath.at[src], send_sem=send_sems.at[k - 1],
                                         recv_sem=recv_sems.at[k - 1], device_id=(x, y, c), device_id_type=MESH).wait_recv()
        for cp in copies:
            cp.wait_send()
        acc = gath[0]
        for d in range(1, 8):
            acc = acc + gath[d]
        out_ref[...] = acc

    vm = pl.BlockSpec(memory_space=pltpu.VMEM)
    return pl.pallas_call(
        body, name="allreduce_small", in_specs=[vm], out_specs=vm,
        out_shape=jax.ShapeDtypeStruct((8, D_MODEL), F32),
        scratch_shapes=[pltpu.VMEM((8, 8, D_MODEL), F32), pltpu.SemaphoreType.DMA((7,)), pltpu.SemaphoreType.DMA((7,))],
    )(block)


def _adamw(w, g, m, v, name):
    rows, cols = w.shape
    tr = 128 if rows % 128 == 0 else rows

    def body(w_ref, g_ref, m_ref, v_ref, d_ref, nm_ref, nv_ref):
        gv = g_ref[...]
        nm = ADAM_B1 * m_ref[...] + (1.0 - ADAM_B1) * gv
        nv = ADAM_B2 * v_ref[...] + (1.0 - ADAM_B2) * (gv * gv)
        m_hat = nm / (1.0 - ADAM_B1 ** ADAM_STEP)
        v_hat = nv / (1.0 - ADAM_B2 ** ADAM_STEP)
        d_ref[...] = -ADAM_LR * (m_hat / (jnp.sqrt(v_hat) + ADAM_EPS) + ADAM_WD * w_ref[...])
        nm_ref[...] = nm
        nv_ref[...] = nv

    spec = pl.BlockSpec((tr, cols), lambda i: (i, 0))
    shape = jax.ShapeDtypeStruct((rows, cols), F32)
    return pl.pallas_call(
        body, name=name, grid=(rows // tr,), in_specs=[spec] * 4, out_specs=[spec] * 3, out_shape=[shape] * 3,
        compiler_params=_params(("parallel",)),
    )(w, g, m, v)


def kernel(x, norm_w, w_in, conv_w, a_log, dt_bias, gdn_norm_w, w_up_a, w_up_b, w_out, final_norm_w, loss_target, m_norm_w, m_w_in, m_conv_w, m_a_log, m_dt_bias, m_gdn_norm_w, m_w_up_a, m_w_up_b, m_w_out, m_final_norm_w, v_norm_w, v_w_in, v_conv_w, v_a_log, v_dt_bias, v_gdn_norm_w, v_w_up_a, v_w_up_b, v_w_out, v_final_norm_w):
    conv_hi = conv_w[0].astype(MXU_DTYPE)
    conv_lo = (conv_w[0] - conv_hi.astype(F32)).astype(MXU_DTYPE)
    big = w_in[0].astype(MXU_DTYPE).reshape(2, *BIG_HALF)
    slab = _pack_slab(w_up_a[0].astype(MXU_DTYPE), w_up_b[0].astype(MXU_DTYPE), w_out[0].astype(MXU_DTYPE), conv_hi, conv_lo)
    own_shard = 2 * lax.axis_index("x") + lax.axis_index("y")
    bigs, slabs = _gather_weights([big, slab])
    bigs = lax.dynamic_update_slice(bigs, big[None], (own_shard, 0, 0, 0)).reshape(SHARDS, D_MODEL, W_IN_SHARD)
    slabs = lax.dynamic_update_slice(slabs, slab[None], (own_shard, 0, 0, 0))
    parts = [_unpack_slab(slabs[s]) for s in range(SHARDS)]
    split = BA_END - (SHARDS - 1) * W_IN_SHARD
    wp = jnp.concatenate([bigs[s] for s in range(SHARDS - 1)]
                         + [bigs[-1][:, :split], jnp.zeros((D_MODEL, SEG_GA - BA_END), MXU_DTYPE), bigs[-1][:, split:]], axis=1)
    w_up_a_full = jnp.concatenate([p[0] for p in parts], axis=1)
    w_up_b_full = jnp.concatenate([p[1] for p in parts], axis=1)
    w_out_full = jnp.concatenate([p[2] for p in parts], axis=0)
    conv_full = jnp.concatenate([p[3].astype(F32) + p[4].astype(F32) for p in parts], axis=1)

    blocks, tags = (128, HALF_ROWS), ("w_in", "slab")

    def start_reduce(d_wp, d_w_up_a, d_w_up_b, d_w_out, d_conv_w):
        d_w_in = [d_wp[:, s * W_IN_SHARD:(s + 1) * W_IN_SHARD] for s in range(SHARDS - 1)]
        d_w_in.append(jnp.concatenate([d_wp[:, (SHARDS - 1) * W_IN_SHARD:BA_END], d_wp[:, SEG_GA:]], axis=1))
        zero_conv = jnp.zeros((GDN_CONV, CONV_SHARD), F32)
        grads = [jnp.stack(d_w_in).reshape(SHARDS, 2, *BIG_HALF),
                 jnp.stack([_pack_slab(d_w_up_a[:, s * 256:(s + 1) * 256], d_w_up_b[:, s * 256:(s + 1) * 256],
                                       d_w_out[s * 256:(s + 1) * 256], d_conv_w[:, s * CONV_SHARD:(s + 1) * CONV_SHARD],
                                       zero_conv) for s in range(SHARDS)])]
        from_sibling = _exchange_halves(grads)
        pairs = [_pair_sum(gr, fs, blk, f"pair_sum_{tag}") for gr, fs, blk, tag in zip(grads, from_sibling, blocks, tags)]
        *in_flight, token = _scatter_start(pairs)
        return in_flight, token

    g = _local_step(x[0], loss_target[0], norm_w, wp, conv_full, a_log, dt_bias, gdn_norm_w,
                    w_up_a_full, w_up_b_full, w_out_full, final_norm_w[None], start_reduce)

    send_sems, recv_sems, *passed = g["in_flight"]
    arrived = _scatter_wait(send_sems, recv_sems, passed, after=g["grad_x"])
    pairs, from_chips = arrived[:2], arrived[2:]
    total_big, total_slab = _share_total([_chip_sum(p, fc, blk, f"chip_sum_{tag}")
                                          for p, fc, blk, tag in zip(pairs, from_chips, blocks, tags)])
    g_w_in = total_big.reshape(D_MODEL, W_IN_SHARD)
    g_w_up_a, g_w_up_b, g_w_out, g_conv, _ = _unpack_slab(total_slab)

    row2 = jnp.concatenate([g["gdn_norm_w"], g["a_log"], g["dt_bias"], g["loss"][0:1, 0:1],
                            jnp.zeros((1, D_MODEL - HEAD_DIM - 2 * HEADS - 1), F32)], axis=1)
    small = _allreduce_small(jnp.concatenate([g["norm_w"], g["final_norm_w"], row2, jnp.zeros((5, D_MODEL), F32)], axis=0))
    g_norm, g_final = small[0:1], small[1]
    g_gnw, g_alog, g_dt = small[2:3, 0:HEAD_DIM], small[2:3, HEAD_DIM:HEAD_DIM + HEADS], small[2:3, HEAD_DIM + HEADS:HEAD_DIM + 2 * HEADS]
    loss = small[2, HEAD_DIM + 2 * HEADS]

    names = ["norm_w", "w_in", "conv_w", "a_log", "dt_bias", "gdn_norm_w", "w_up_a", "w_up_b", "w_out", "final_norm_w"]
    weights = dict(zip(names, (norm_w, w_in, conv_w, a_log, dt_bias, gdn_norm_w, w_up_a, w_up_b, w_out, final_norm_w)))
    ms = dict(zip(names, (m_norm_w, m_w_in, m_conv_w, m_a_log, m_dt_bias, m_gdn_norm_w, m_w_up_a, m_w_up_b, m_w_out, m_final_norm_w)))
    vs = dict(zip(names, (v_norm_w, v_w_in, v_conv_w, v_a_log, v_dt_bias, v_gdn_norm_w, v_w_up_a, v_w_up_b, v_w_out, v_final_norm_w)))
    grads2d = dict(norm_w=g_norm, w_in=g_w_in, conv_w=g_conv, a_log=g_alog, dt_bias=g_dt, gdn_norm_w=g_gnw,
                   w_up_a=g_w_up_a, w_up_b=g_w_up_b, w_out=g_w_out, final_norm_w=g_final[None])
    grad_out, delta, new_m, new_v = [], [], [], []
    for n in names:
        shape = weights[n].shape
        two_d = grads2d[n].shape
        d, nm, nv = _adamw(weights[n].reshape(two_d), grads2d[n], ms[n].reshape(two_d), vs[n].reshape(two_d), f"adamw_{n}")
        grad_out.append(grads2d[n].reshape(shape))
        delta.append(d.reshape(shape))
        new_m.append(nm.reshape(shape))
        new_v.append(nv.reshape(shape))
    return (loss, g["grad_x"][None], *grad_out, *delta, *new_m, *new_v)
```

```python
import functools

import jax
import jax.numpy as jnp
from jax import lax
from jax.experimental import pallas as pl
from jax.experimental.pallas import tpu as pltpu

F32 = jnp.float32
MXU_DTYPE = jnp.bfloat16
HIGHEST = lax.Precision.HIGHEST

D_MODEL = 1024
HEADS = 8
HEAD_DIM = 64
WIDTH = HEADS * HEAD_DIM
NORM_EPS = 1e-6
ROPE_THETA = 10000.0
ATT_BLOCK = 128
DILATIONS = (1, 4, 16)
GDN_CHUNK = 64
GDN_CONV = 4
IN_WIDTH = 9232
SEG_A, SEG_ZA, SEG_B, SEG_ZB, SEG_BA, SEG_GA, SEG_GB, PACKED_WIDTH = 0, 4608, 5120, 6656, 7168, 7680, 8704, 9728
BA_END = 7184
VMEM_LIMIT = 56 * 1024 * 1024

ADAM_LR, ADAM_B1, ADAM_B2, ADAM_EPS, ADAM_WD, ADAM_STEP = 0.001, 0.9, 0.999, 1e-08, 0.01, 10

_NN = (((1,), (0,)), ((), ()))
_NT = (((1,), (1,)), ((), ()))
_TN = (((0,), (0,)), ((), ()))


def _params(sem):
    return pltpu.CompilerParams(dimension_semantics=sem, vmem_limit_bytes=VMEM_LIMIT)


def _mxu(a, b, dims):
    return lax.dot_general(a.astype(MXU_DTYPE), b.astype(MXU_DTYPE), dims, preferred_element_type=F32)


def _sigmoid(x):
    return 1.0 / (1.0 + jnp.exp(-x))


def _softplus(x):
    return jnp.maximum(x, 0.0) + jnp.log(1.0 + jnp.exp(-jnp.abs(x)))


def _iota(shape, axis):
    return lax.broadcasted_iota(jnp.int32, shape, axis)


def _matmul(a, b, mode, name, out_dtype=F32, tm=512, tn=512, tk=512, after=None):
    if mode == "nn":
        (m, k), (k2, n) = a.shape, b.shape
    elif mode == "nt":
        (m, k), (n, k2) = a.shape, b.shape
    else:
        (k, m), (k2, n) = a.shape, b.shape
    assert k == k2
    tm, tn, tk = min(tm, m), min(tn, n), min(tk, k)
    assert m % tm == 0 and n % tn == 0 and k % tk == 0
    nk = k // tk
    dims = {"nn": _NN, "nt": _NT, "tn": _TN}[mode]

    assert out_dtype == F32

    def body(a_ref, b_ref, *rest):
        o_ref = rest[-1]
        kk = pl.program_id(2)
        part = _mxu(a_ref[...], b_ref[...], dims)

        @pl.when(kk == 0)
        def _():
            o_ref[...] = part

        @pl.when(kk > 0)
        def _():
            o_ref[...] += part

    a_spec = pl.BlockSpec((tk, tm), lambda i, j, kk: (kk, i)) if mode == "tn" else pl.BlockSpec((tm, tk), lambda i, j, kk: (i, kk))
    b_spec = pl.BlockSpec((tn, tk), lambda i, j, kk: (j, kk)) if mode == "nt" else pl.BlockSpec((tk, tn), lambda i, j, kk: (kk, j))
    extra_specs, extra_args = ([], []) if after is None else ([pl.BlockSpec(memory_space=pl.ANY)], [after])
    return pl.pallas_call(
        body, name=name, grid=(m // tm, n // tn, nk), in_specs=[a_spec, b_spec] + extra_specs,
        out_specs=pl.BlockSpec((tm, tn), lambda i, j, kk: (i, j)),
        out_shape=jax.ShapeDtypeStruct((m, n), out_dtype),
        compiler_params=_params(("parallel", "parallel", "arbitrary")),
    )(a, b, *extra_args)


def _norm_proj(x, norm_w, wp, tm=1024, tn=PACKED_WIDTH // 4):
    t = x.shape[0]
    tm = min(tm, t)

    def body(x_ref, nw_ref, w_ref, proj_ref, ht_ref, h_scr):
        @pl.when(pl.program_id(1) == 0)
        def _():
            xf = x_ref[...]
            r = lax.rsqrt(jnp.mean(xf * xf, axis=-1, keepdims=True) + NORM_EPS)
            h = xf * r * nw_ref[...]
            h_scr[...] = h.astype(h_scr.dtype)
            ht_ref[...] = h.T.astype(ht_ref.dtype)

        proj_ref[...] = jnp.dot(h_scr[...], w_ref[...], preferred_element_type=F32)

    return pl.pallas_call(
        body, name="norm_proj", grid=(t // tm, PACKED_WIDTH // tn),
        in_specs=[pl.BlockSpec((tm, D_MODEL), lambda i, j: (i, 0)),
                  pl.BlockSpec((1, D_MODEL), lambda i, j: (0, 0)),
                  pl.BlockSpec((D_MODEL, tn), lambda i, j: (0, j))],
        out_specs=[pl.BlockSpec((tm, tn), lambda i, j: (i, j)),
                   pl.BlockSpec((D_MODEL, tm), lambda i, j: (0, i))],
        out_shape=[jax.ShapeDtypeStruct((t, PACKED_WIDTH), F32), jax.ShapeDtypeStruct((D_MODEL, t), MXU_DTYPE)],
        scratch_shapes=[pltpu.VMEM((tm, D_MODEL), MXU_DTYPE)],
        compiler_params=_params(("parallel", "arbitrary")),
    )(x, norm_w, wp)


def _rope_tables(t):
    lane = jnp.arange(128)
    inv_freq = ROPE_THETA ** (-jnp.arange(0, HEAD_DIM, 2, dtype=F32) / HEAD_DIM)
    ang = jnp.arange(t, dtype=F32)[:, None] * inv_freq[None, :]
    ang = jnp.concatenate([ang, ang, ang, ang], axis=-1)
    first_half = (lane % HEAD_DIM) < HEAD_DIM // 2
    cos, sin = jnp.cos(ang), jnp.sin(ang)
    return cos, jnp.where(first_half, -sin, 0.0), jnp.where(first_half, 0.0, sin)


def _rope_block(x, cos, sin_lo, sin_hi, sign):
    outs = []
    for c in range(8):
        xc = x[:, c * 128:(c + 1) * 128]
        rot = pltpu.roll(xc, 96, 1) * sin_lo + pltpu.roll(xc, 32, 1) * sin_hi
        outs.append(xc * cos + sign * rot)
    outs.append(x[:, 2 * WIDTH:])
    return jnp.concatenate(outs, axis=1)


def _tile_scratch(tm, cols):
    return pltpu.VMEM((cols // 128, tm, 128), F32)


def _store_tile(scr, y):
    for c in range(scr.shape[0]):
        scr[c] = y[:, c * 128:(c + 1) * 128]


def _load_tile(scr):
    return jnp.concatenate([scr[c] for c in range(scr.shape[0])], axis=1)


def _to_strided_view(scr, o_ref, d):
    n, tm, _ = scr.shape
    for r in range(d):
        for c in range(n):
            o_ref[:, (r * n + c) * 128:(r * n + c + 1) * 128] = scr[c, pl.ds(r, tm // d, stride=d), :].astype(o_ref.dtype)


def _from_strided_view(i_ref, scr, d):
    n, tm, _ = scr.shape
    for r in range(d):
        for c in range(n):
            scr[c, pl.ds(r, tm // d, stride=d), :] = i_ref[:, (r * n + c) * 128:(r * n + c + 1) * 128].astype(F32)


def _strided_spec(tm, d, cols):
    return pl.BlockSpec((tm // d, d * cols), lambda i: (i, 0))


def _rope_fwd(proj, tables, tm=256):
    t = proj.shape[0]
    cols = 3 * WIDTH

    def body(x_ref, c_ref, sl_ref, sh_ref, o0, o1, o2, scr):
        for g, (d, o_ref) in enumerate(zip(DILATIONS, (o0, o1, o2))):
            y = _rope_block(x_ref[:, g * cols:(g + 1) * cols], c_ref[...], sl_ref[...], sh_ref[...], 1.0)
            if d == 1:
                o_ref[...] = y.astype(o_ref.dtype)
            else:
                _store_tile(scr, y)
                _to_strided_view(scr, o_ref, d)

    tab = pl.BlockSpec((tm, 128), lambda i: (i, 0))
    return pl.pallas_call(
        body, name="rope_fwd", grid=(t // tm,),
        in_specs=[pl.BlockSpec((tm, 3 * cols), lambda i: (i, 0)), tab, tab, tab],
        out_specs=[_strided_spec(tm, d, cols) for d in DILATIONS],
        out_shape=[jax.ShapeDtypeStruct((t // d, d * cols), MXU_DTYPE) for d in DILATIONS],
        scratch_shapes=[_tile_scratch(tm, cols)],
        compiler_params=_params(("parallel",)),
    )(proj, *tables)


def _rope_bwd(dproj, dqkvs, tables, tm=256):
    t = dproj.shape[0]
    cols = 3 * WIDTH

    def body(dp_ref, i0, i1, i2, c_ref, sl_ref, sh_ref, o_ref, scr):
        for g, (d, i_ref) in enumerate(zip(DILATIONS, (i0, i1, i2))):
            if d == 1:
                x = i_ref[...]
            else:
                _from_strided_view(i_ref, scr, d)
                x = _load_tile(scr)
            y = _rope_block(x, c_ref[...], sl_ref[...], sh_ref[...], -1.0)
            o_ref[:, g * cols:(g + 1) * cols] = y.astype(o_ref.dtype)

    tab = pl.BlockSpec((tm, 128), lambda i: (i, 0))
    return pl.pallas_call(
        body, name="rope_bwd", grid=(t // tm,),
        in_specs=[pl.BlockSpec(memory_space=pl.ANY)] + [_strided_spec(tm, d, cols) for d in DILATIONS] + [tab, tab, tab],
        out_specs=pl.BlockSpec((tm, 3 * cols), lambda i: (i, 0)),
        out_shape=jax.ShapeDtypeStruct((t, PACKED_WIDTH), MXU_DTYPE),
        scratch_shapes=[_tile_scratch(tm, cols)],
        input_output_aliases={0: 0},
        compiler_params=_params(("parallel",)),
    )(dproj, *dqkvs, *tables)


def _att_masks():
    qi = _iota((ATT_BLOCK, ATT_BLOCK), 0)
    kj = _iota((ATT_BLOCK, ATT_BLOCK), 1)
    return kj <= qi, kj >= qi


def _att_fwd(qkv, d, name):
    rows = qkv.shape[0]
    nb = rows // ATT_BLOCK
    scale = HEAD_DIM ** -0.5

    def body(q_ref, kc_ref, kp_ref, vc_ref, vp_ref, o_ref, lse_ref):
        has_prev = pl.program_id(1) > 0
        m_cur, m_prev = _att_masks()
        m_prev = m_prev & has_prev
        hs = range(HEADS)
        sls = [slice(h * HEAD_DIM, (h + 1) * HEAD_DIM) for h in hs]
        qs = [q_ref[:, sl] for sl in sls]
        s_c = [jnp.where(m_cur, _mxu(qs[h], kc_ref[:, sls[h]], _NT) * scale, -jnp.inf) for h in hs]
        s_p = [jnp.where(m_prev, _mxu(qs[h], kp_ref[:, sls[h]], _NT) * scale, -jnp.inf) for h in hs]
        m = [jnp.maximum(jnp.max(s_c[h], axis=1, keepdims=True), jnp.max(s_p[h], axis=1, keepdims=True)) for h in hs]
        p_c = [jnp.exp(s_c[h] - m[h]) for h in hs]
        p_p = [jnp.exp(s_p[h] - m[h]) for h in hs]
        den = [jnp.sum(p_c[h], axis=1, keepdims=True) + jnp.sum(p_p[h], axis=1, keepdims=True) for h in hs]
        o = [_mxu(p_c[h], vc_ref[:, sls[h]], _NN) + _mxu(p_p[h], vp_ref[:, sls[h]], _NN) for h in hs]
        for h in hs:
            o_ref[:, sls[h]] = o[h] / den[h]
            lse_ref[:, sls[h]] = jnp.broadcast_to(m[h] + jnp.log(den[h]), (ATT_BLOCK, HEAD_DIM))

    def cur(c):
        return pl.BlockSpec((ATT_BLOCK, WIDTH), lambda r, i: (i, 3 * r + c))

    def prev(c):
        return pl.BlockSpec((ATT_BLOCK, WIDTH), lambda r, i: (jnp.maximum(i - 1, 0), 3 * r + c))

    out = pl.BlockSpec((ATT_BLOCK, WIDTH), lambda r, i: (i, r))
    return pl.pallas_call(
        body, name=name, grid=(d, nb), in_specs=[cur(0), cur(1), prev(1), cur(2), prev(2)],
        out_specs=[out, out],
        out_shape=[jax.ShapeDtypeStruct((rows, d * WIDTH), F32)] * 2,
        compiler_params=_params(("parallel", "arbitrary")),
    )(qkv, qkv, qkv, qkv, qkv)


def _att_bwd(qkv, do, lse, delta, d, name):
    rows = qkv.shape[0]
    nb = rows // ATT_BLOCK
    scale = HEAD_DIM ** -0.5

    def body(q_ref, qn_ref, kc_ref, kp_ref, vc_ref, vp_ref, do_ref, don_ref, l_ref, ln_ref, dl_ref, dln_ref, o_ref):
        i = pl.program_id(1)
        m_cur, m_prev = _att_masks()
        m_p = m_prev & (i > 0)
        m_n = m_prev & (i < nb - 1)

        hs = range(HEADS)
        sls = [slice(h * HEAD_DIM, (h + 1) * HEAD_DIM) for h in hs]
        col = [slice(h * HEAD_DIM, h * HEAD_DIM + 1) for h in hs]

        def probs(q_r, k_r, lse_r, mask):
            s = [_mxu(q_r[:, sls[h]], k_r[:, sls[h]], _NT) for h in hs]
            return [jnp.where(mask, jnp.exp(s[h] * scale - lse_r[:, col[h]]), 0.0) for h in hs]

        def dscores(p, do_r, v_r, dl_r):
            dp = [_mxu(do_r[:, sls[h]], v_r[:, sls[h]], _NT) for h in hs]
            return [(p[h] * (dp[h] - dl_r[:, col[h]])).astype(MXU_DTYPE) for h in hs]

        p = probs(q_ref, kc_ref, l_ref, m_cur)
        ds = dscores(p, do_ref, vc_ref, dl_ref)
        dq = [_mxu(ds[h], kc_ref[:, sls[h]], _NN) for h in hs]
        dk = [_mxu(ds[h], q_ref[:, sls[h]], _TN) for h in hs]
        dv = [_mxu(p[h], do_ref[:, sls[h]], _TN) for h in hs]
        p = probs(q_ref, kp_ref, l_ref, m_p)
        ds = dscores(p, do_ref, vp_ref, dl_ref)
        dq = [dq[h] + _mxu(ds[h], kp_ref[:, sls[h]], _NN) for h in hs]
        p = probs(qn_ref, kc_ref, ln_ref, m_n)
        ds = dscores(p, don_ref, vc_ref, dln_ref)
        dk = [dk[h] + _mxu(ds[h], qn_ref[:, sls[h]], _TN) for h in hs]
        dv = [dv[h] + _mxu(p[h], don_ref[:, sls[h]], _TN) for h in hs]
        for h in hs:
            o_ref[:, sls[h]] = dq[h] * scale
            o_ref[:, WIDTH + h * HEAD_DIM:WIDTH + (h + 1) * HEAD_DIM] = dk[h] * scale
            o_ref[:, 2 * WIDTH + h * HEAD_DIM:2 * WIDTH + (h + 1) * HEAD_DIM] = dv[h]

    def qkv_spec(c, shift):
        def idx(r, i):
            return (jnp.clip(i + shift, 0, nb - 1), 3 * r + c)
        return pl.BlockSpec((ATT_BLOCK, WIDTH), idx)

    def tok_spec(shift):
        def idx(r, i):
            return (jnp.clip(i + shift, 0, nb - 1), r)
        return pl.BlockSpec((ATT_BLOCK, WIDTH), idx)

    return pl.pallas_call(
        body, name=name, grid=(d, nb),
        in_specs=[qkv_spec(0, 0), qkv_spec(0, 1), qkv_spec(1, 0), qkv_spec(1, -1), qkv_spec(2, 0), qkv_spec(2, -1),
                  tok_spec(0), tok_spec(1), tok_spec(0), tok_spec(1), tok_spec(0), tok_spec(1)],
        out_specs=pl.BlockSpec((ATT_BLOCK, 3 * WIDTH), lambda r, i: (i, r)),
        out_shape=jax.ShapeDtypeStruct((rows, d * 3 * WIDTH), F32),
        compiler_params=_params(("parallel", "arbitrary")),
    )(qkv, qkv, qkv, qkv, qkv, qkv, do, do, lse, lse, delta, delta)


def _att_merge(os_, lses, proj, tm=256):
    t = proj.shape[0]

    def body(o0, o1, o2, l0, l1, l2, z_ref, oz_ref, o_ref, t0, t1, t2, s_o1, s_l1, s_o2, s_l2, s_t):
        _from_strided_view(o1, s_o1, DILATIONS[1])
        _from_strided_view(l1, s_l1, DILATIONS[1])
        _from_strided_view(o2, s_o2, DILATIONS[2])
        _from_strided_view(l2, s_l2, DILATIONS[2])
        a, b, c = l0[...], _load_tile(s_l1), _load_tile(s_l2)
        m = jnp.maximum(jnp.maximum(a, b), c)
        wa, wb, wc = jnp.exp(a - m), jnp.exp(b - m), jnp.exp(c - m)
        den = wa + wb + wc
        o = (wa * o0[...] + wb * _load_tile(s_o1) + wc * _load_tile(s_o2)) / den
        z = z_ref[...]
        o_ref[...] = o
        oz_ref[...] = (o * z * _sigmoid(z)).astype(oz_ref.dtype)
        total = m + jnp.log(den)
        t0[...] = total
        _store_tile(s_t, total)
        _to_strided_view(s_t, t1, DILATIONS[1])
        _to_strided_view(s_t, t2, DILATIONS[2])

    tok = pl.BlockSpec((tm, WIDTH), lambda i: (i, 0))
    views = [_strided_spec(tm, d, WIDTH) for d in DILATIONS]
    view_shapes = [jax.ShapeDtypeStruct((t // d, d * WIDTH), F32) for d in DILATIONS]
    return pl.pallas_call(
        body, name="att_merge", grid=(t // tm,),
        in_specs=views + views + [pl.BlockSpec((tm, WIDTH), lambda i: (i, SEG_ZA // WIDTH))],
        out_specs=[tok, tok] + views,
        out_shape=[jax.ShapeDtypeStruct((t, WIDTH), MXU_DTYPE), jax.ShapeDtypeStruct((t, WIDTH), F32)] + view_shapes,
        scratch_shapes=[_tile_scratch(tm, WIDTH)] * 5,
        compiler_params=_params(("parallel",)),
    )(*os_, *lses, proj)


def _att_merge_bwd(dproj, d_oz, o, proj, tm=256):
    t = proj.shape[0]

    def body(dp_ref, doz_ref, o_ref, z_ref, dz_ref, do0, do1, do2, dl0, dl1, dl2, s_do, s_dl):
        z, ov, g = z_ref[...], o_ref[...], doz_ref[...]
        sg = _sigmoid(z)
        do = g * z * sg
        dz_ref[...] = (g * ov * sg * (1.0 + z * (1.0 - sg))).astype(dz_ref.dtype)
        do0[...] = do.astype(do0.dtype)
        _store_tile(s_do, do)
        prod = do * ov
        for p in range(HEADS // 2):
            s_dl[p] = _pair_sum_lanes(prod[:, p * 128:(p + 1) * 128], _pair_ones())
        dl0[...] = _load_tile(s_dl)
        for d, do_v, dl_v in ((DILATIONS[1], do1, dl1), (DILATIONS[2], do2, dl2)):
            _to_strided_view(s_do, do_v, d)
            _to_strided_view(s_dl, dl_v, d)

    tok = pl.BlockSpec((tm, WIDTH), lambda i: (i, 0))
    seg = pl.BlockSpec((tm, WIDTH), lambda i: (i, SEG_ZA // WIDTH))
    views = [_strided_spec(tm, d, WIDTH) for d in DILATIONS]
    return pl.pallas_call(
        body, name="att_merge_bwd", grid=(t // tm,),
        in_specs=[pl.BlockSpec(memory_space=pl.ANY), tok, tok, seg],
        out_specs=[seg] + views + views,
        out_shape=[jax.ShapeDtypeStruct((t, PACKED_WIDTH), MXU_DTYPE)]
        + [jax.ShapeDtypeStruct((t // d, d * WIDTH), MXU_DTYPE) for d in DILATIONS]
        + [jax.ShapeDtypeStruct((t // d, d * WIDTH), F32) for d in DILATIONS],
        scratch_shapes=[_tile_scratch(tm, WIDTH)] * 2,
        input_output_aliases={0: 0},
        compiler_params=_params(("parallel",)),
    )(dproj, d_oz, o, proj)


def _shift_down(x, halo, s):
    if s == 0:
        return x
    xs = pltpu.roll(x, s, 0)
    head = jnp.where(_iota((8, x.shape[1]), 0) < s, pltpu.roll(halo, s, 0), xs[0:8])
    return jnp.concatenate([head, xs[8:]], axis=0)


def _shift_up(x, nxt, s):
    if s == 0:
        return x
    n = x.shape[0]
    xs = pltpu.roll(x, n - s, 0)
    tail = jnp.where(_iota((8, x.shape[1]), 0) >= 8 - s, pltpu.roll(nxt, 8 - s, 0), xs[n - 8:])
    return jnp.concatenate([xs[:n - 8], tail], axis=0)


def _conv_fwd(proj, conv_w, tm=256):
    t = proj.shape[0]
    cb = SEG_B // WIDTH

    def body(x_ref, halo_ref, w_ref, c_ref):
        halo = jnp.where(pl.program_id(0) > 0, halo_ref[...], 0.0)
        x = x_ref[...]
        w = w_ref[...]
        acc = jnp.zeros((tm, WIDTH), F32)
        for j in range(GDN_CONV):
            acc += _shift_down(x, halo, GDN_CONV - 1 - j) * w[j:j + 1, :]
        c_ref[...] = acc

    return pl.pallas_call(
        body, name="conv_fwd", grid=(t // tm, 3),
        in_specs=[pl.BlockSpec((tm, WIDTH), lambda i, c: (i, cb + c)),
                  pl.BlockSpec((8, WIDTH), lambda i, c: (jnp.maximum(i * (tm // 8) - 1, 0), cb + c)),
                  pl.BlockSpec((GDN_CONV, WIDTH), lambda i, c: (0, c))],
        out_specs=pl.BlockSpec((tm, WIDTH), lambda i, c: (i, c)),
        out_shape=jax.ShapeDtypeStruct((t, 3 * WIDTH), F32),
        compiler_params=_params(("parallel", "parallel")),
    )(proj, proj, conv_w)


def _conv_bwd(dproj, dc, proj, conv_w, tm=256):
    t = proj.shape[0]
    cb = SEG_B // WIDTH
    nt = t // tm

    def body(dp_ref, dc_ref, dcn_ref, x_ref, halo_ref, w_ref, dx_ref, dw_ref):
        i = pl.program_id(1)
        w = w_ref[...]
        dcn = jnp.where(i < nt - 1, dcn_ref[...], 0.0)
        dcv = dc_ref[...]
        acc = jnp.zeros((tm, WIDTH), F32)
        for j in range(GDN_CONV):
            acc += _shift_up(dcv, dcn, GDN_CONV - 1 - j) * w[j:j + 1, :]
        dx_ref[...] = acc.astype(dx_ref.dtype)
        halo = jnp.where(i > 0, halo_ref[...], 0.0)
        x = x_ref[...]
        row8 = _iota((8, WIDTH), 0)
        part = jnp.zeros((8, WIDTH), F32)
        for j in range(GDN_CONV):
            s = jnp.sum(dcv * _shift_down(x, halo, GDN_CONV - 1 - j), axis=0, keepdims=True)
            part += jnp.where(row8 == j, s, 0.0)

        @pl.when(i == 0)
        def _():
            dw_ref[...] = jnp.zeros_like(dw_ref)

        dw_ref[...] += part

    return pl.pallas_call(
        body, name="conv_bwd", grid=(3, nt),
        in_specs=[pl.BlockSpec(memory_space=pl.ANY),
                  pl.BlockSpec((tm, WIDTH), lambda c, i: (i, c)),
                  pl.BlockSpec((8, WIDTH), lambda c, i: (jnp.minimum((i + 1) * (tm // 8), t // 8 - 1), c)),
                  pl.BlockSpec((tm, WIDTH), lambda c, i: (i, cb + c)),
                  pl.BlockSpec((8, WIDTH), lambda c, i: (jnp.maximum(i * (tm // 8) - 1, 0), cb + c)),
                  pl.BlockSpec((GDN_CONV, WIDTH), lambda c, i: (0, c))],
        out_specs=[pl.BlockSpec((tm, WIDTH), lambda c, i: (i, cb + c)),
                   pl.BlockSpec((8, WIDTH), lambda c, i: (0, c))],
        out_shape=[jax.ShapeDtypeStruct((t, PACKED_WIDTH), MXU_DTYPE), jax.ShapeDtypeStruct((8, 3 * WIDTH), F32)],
        input_output_aliases={0: 0},
        compiler_params=_params(("parallel", "arbitrary")),
    )(dproj, dc, dc, proj, proj, conv_w)


def _chunk_matrices(tm):
    r, c = _iota((tm, tm), 0), _iota((tm, tm), 1)
    same = (r // GDN_CHUNK) == (c // GDN_CHUNK)
    return jnp.where(same & (c <= r), 1.0, 0.0), jnp.where(same, 1.0, 0.0)


def _gdn_gates(ba, a_log, dt_bias):
    al = ba + dt_bias
    return _sigmoid(ba), -jnp.exp(a_log) * _softplus(al), _sigmoid(al)


SCAN_HEADS = (0, 2, 4, 6, 1, 3, 5, 7)


def _head_lane_eye():
    return jnp.where(_iota((HEADS, 128), 1) == _iota((HEADS, 128), 0) + HEADS, 1.0, 0.0)


def _pair_ones():
    return jnp.where(_iota((128, 128), 0) // HEAD_DIM == _iota((128, 128), 1) // HEAD_DIM, 1.0, 0.0).astype(MXU_DTYPE)


def _pair_to_lanes(p, base):
    return jnp.where(_iota((128, 128), 1) == base + 2 * p + _iota((128, 128), 0) // HEAD_DIM, 1.0, 0.0).astype(MXU_DTYPE)


def _pair_sum_lanes(x, sel):
    hi, lo = _split(x)
    return jnp.dot(hi, sel, preferred_element_type=F32) + jnp.dot(lo, sel, preferred_element_type=F32)


def _gdn_prep(conv, proj, a_log, dt_bias, tm=256):
    t = proj.shape[0]
    nc = tm // GDN_CHUNK

    def body(c_ref, ba_ref, al_ref, dt_ref, q_ref, k_ref, v_ref, b_ref, g_ref, gl_ref, grow_ref):
        beta, g, _ = _gdn_gates(ba_ref[:, 0:128], al_ref[...], dt_ref[...])
        lmat, cmat = _chunk_matrices(tm)
        gc = jnp.dot(lmat, g, precision=HIGHEST, preferred_element_type=F32)
        gl = jnp.dot(cmat, g, precision=HIGHEST, preferred_element_type=F32)
        grow = lax.dot_general(_head_lane_eye(), gc, _NT, precision=HIGHEST, preferred_element_type=F32)
        ones = _pair_ones()
        first = _iota((1, 128), 1) < HEAD_DIM

        def spread(x, base, p):
            return jnp.where(first, x[:, base + 2 * p:base + 2 * p + 1], x[:, base + 2 * p + 1:base + 2 * p + 2])

        for p in range(HEADS // 2):
            for seg, ref, scale in ((0, q_ref, HEAD_DIM ** -0.5), (1, k_ref, 1.0), (2, v_ref, None)):
                c = c_ref[:, seg * WIDTH + p * 128:seg * WIDTH + (p + 1) * 128]
                a = c * _sigmoid(c)
                if scale is not None:
                    a = a * (lax.rsqrt(_pair_sum_lanes(a * a, ones) + NORM_EPS) * scale)
                ref[p] = a
            b_ref[p] = spread(beta, 0, p)
            g_ref[p] = spread(gc, HEADS, p)
            gl_ref[p] = spread(gl, HEADS, p)
        for pos, h in enumerate(SCAN_HEADS):
            for cc in range(nc):
                grow_ref[pos, cc] = grow[h:h + 1, cc * GDN_CHUNK:(cc + 1) * GDN_CHUNK]

    hm = pl.BlockSpec((HEADS // 2, tm, 128), lambda i: (0, i, 0))
    small = pl.BlockSpec((1, 128), lambda i: (0, 0))
    hm_shape = jax.ShapeDtypeStruct((HEADS // 2, t, 128), F32)
    return pl.pallas_call(
        body, name="gdn_prep", grid=(t // tm,),
        in_specs=[pl.BlockSpec((tm, 3 * WIDTH), lambda i: (i, 0)),
                  pl.BlockSpec((tm, WIDTH), lambda i: (i, SEG_BA // WIDTH)), small, small],
        out_specs=[hm] * 6 + [pl.BlockSpec((HEADS, nc, 1, GDN_CHUNK), lambda i: (0, i, 0, 0))],
        out_shape=[hm_shape] * 6 + [jax.ShapeDtypeStruct((HEADS, t // GDN_CHUNK, 1, GDN_CHUNK), F32)],
        compiler_params=_params(("parallel",)),
    )(conv, proj, a_log, dt_bias)


def _gdn_prep_bwd(dproj, conv, proj, a_log, dt_bias, dq, dk, dv, db, dg, dgl, dgrow, tm=256):
    t = proj.shape[0]
    nc = tm // GDN_CHUNK

    def body(dp_ref, c_ref, ba_ref, al_ref, dt_ref, dq_ref, dk_ref, dv_ref, db_ref, dg_ref, dgl_ref, dgrow_ref,
             dba_ref, dc_ref, small_ref, row_scr):
        beta, g, sig_al = _gdn_gates(ba_ref[:, 0:128], al_ref[...], dt_ref[...])
        d_beta = jnp.zeros((tm, 128), F32)
        d_gc = jnp.zeros((tm, 128), F32)
        d_gl = jnp.zeros((tm, 128), F32)
        ones = _pair_ones()
        for pos, h in enumerate(SCAN_HEADS):
            for cc in range(nc):
                row_scr[h:h + 1, cc * GDN_CHUNK:(cc + 1) * GDN_CHUNK] = dgrow_ref[pos, cc]
        for p in range(HEADS // 2):
            d_beta += _pair_sum_lanes(db_ref[p], _pair_to_lanes(p, 0))
            d_gc += _pair_sum_lanes(dg_ref[p], _pair_to_lanes(p, HEADS))
            d_gl += _pair_sum_lanes(dgl_ref[p], _pair_to_lanes(p, HEADS))
            for seg, ref, scale in ((0, dq_ref, HEAD_DIM ** -0.5), (1, dk_ref, 1.0), (2, dv_ref, None)):
                cols = slice(seg * WIDTH + p * 128, seg * WIDTH + (p + 1) * 128)
                c = c_ref[:, cols]
                sg = _sigmoid(c)
                da = ref[p]
                if scale is not None:
                    a = c * sg
                    r = lax.rsqrt(_pair_sum_lanes(a * a, ones) + NORM_EPS)
                    da = da * scale
                    da = r * da - a * (r * r * r) * _pair_sum_lanes(da * a, ones)
                dc_ref[:, cols] = da * (sg * (1.0 + c * (1.0 - sg)))
        d_gc += lax.dot_general(row_scr[...], _head_lane_eye(), _TN, precision=HIGHEST, preferred_element_type=F32)
        lmat, cmat = _chunk_matrices(tm)
        d_g = (lax.dot_general(lmat, d_gc, _TN, precision=HIGHEST, preferred_element_type=F32)
               + lax.dot_general(cmat, d_gl, _TN, precision=HIGHEST, preferred_element_type=F32))
        d_al = d_g * (-jnp.exp(al_ref[...])) * sig_al
        d_bl = d_beta * beta * (1.0 - beta)
        dba_ref[...] = jnp.concatenate([d_bl + d_al, jnp.zeros((tm, WIDTH - 128), F32)], axis=1).astype(dba_ref.dtype)
        row8 = _iota((8, 128), 0)
        part = (jnp.where(row8 == 0, jnp.sum(d_g * g, axis=0, keepdims=True), 0.0)
                + jnp.where(row8 == 1, jnp.sum(d_al, axis=0, keepdims=True), 0.0))

        @pl.when(pl.program_id(0) == 0)
        def _():
            small_ref[...] = jnp.zeros_like(small_ref)

        small_ref[...] += part

    hm = pl.BlockSpec((HEADS // 2, tm, 128), lambda i: (0, i, 0))
    small = pl.BlockSpec((1, 128), lambda i: (0, 0))
    seg = pl.BlockSpec((tm, WIDTH), lambda i: (i, SEG_BA // WIDTH))
    return pl.pallas_call(
        body, name="gdn_prep_bwd", grid=(t // tm,),
        in_specs=[pl.BlockSpec(memory_space=pl.ANY), pl.BlockSpec((tm, 3 * WIDTH), lambda i: (i, 0)), seg, small, small]
        + [hm] * 6 + [pl.BlockSpec((HEADS, nc, 1, GDN_CHUNK), lambda i: (0, i, 0, 0))],
        out_specs=[seg, pl.BlockSpec((tm, 3 * WIDTH), lambda i: (i, 0)), pl.BlockSpec((8, 128), lambda i: (0, 0))],
        out_shape=[jax.ShapeDtypeStruct((t, PACKED_WIDTH), MXU_DTYPE), jax.ShapeDtypeStruct((t, 3 * WIDTH), F32),
                   jax.ShapeDtypeStruct((8, 128), F32)],
        scratch_shapes=[pltpu.VMEM((HEADS, tm), F32)],
        input_output_aliases={0: 0},
        compiler_params=_params(("arbitrary",)),
    )(dproj, conv, proj, a_log, dt_bias, dq, dk, dv, db, dg, dgl, dgrow)


_BNN = (((2,), (1,)), ((0,), (0,)))
_BNT = (((2,), (2,)), ((0,), (0,)))
_BTN = (((1,), (1,)), ((0,), (0,)))


@jax.custom_vjp
def _MM_NN(a, b):
    return _mxu(a, b, _BNN)


@jax.custom_vjp
def _MM_NT(a, b):
    return _mxu(a, b, _BNT)


@jax.custom_vjp
def _MM_TN(a, b):
    return _mxu(a, b, _BTN)


_MM_NN.defvjp(lambda a, b: (_mxu(a, b, _BNN), (a, b)), lambda r, g: (_mxu(g, r[1], _BNT), _mxu(r[0], g, _BTN)))
_MM_NT.defvjp(lambda a, b: (_mxu(a, b, _BNT), (a, b)), lambda r, g: (_mxu(g, r[1], _BNN), _mxu(g, r[0], _BTN)))
_MM_TN.defvjp(lambda a, b: (_mxu(a, b, _BTN), (a, b)), lambda r, g: (_mxu(r[1], g, _BNT), _mxu(r[0], g, _BNN)))


def _split(a):
    hi = a.astype(MXU_DTYPE)
    return hi, (a - hi.astype(F32)).astype(MXU_DTYPE)


def _dot3(a, b, dims):
    (ah, al), (bh, bl) = a, b
    (ca,), (cb,) = dims[0]
    return lax.dot_general(jnp.concatenate([ah, ah, al], axis=ca), jnp.concatenate([bh, bl, bh], axis=cb), dims,
                           preferred_element_type=F32)


def _unit_lower_inverse(a):
    c = GDN_CHUNK
    eye = jnp.where(_iota((c, c), 0) == _iota((c, c), 1), 1.0, 0.0)
    x = eye - a
    p = a
    for _ in range(5):
        ps = _split(p)
        p = _dot3(ps, ps, _BNN)
        x = x + _dot3(_split(x), _split(p), _BNN)
    return x


@jax.custom_vjp
def _SAVED_INVERSE(a, t_inv):
    return t_inv


def _saved_inverse_bwd(t_inv, g):
    ts = _split(t_inv)
    return -_dot3(ts, _split(_dot3(_split(g), ts, _BNT)), _BTN), jnp.zeros_like(t_inv)


_SAVED_INVERSE.defvjp(lambda a, t_inv: (t_inv, t_inv), _saved_inverse_bwd)


def _gdn_chunk(q, k, v, beta, g1, g2, gl, state, t_inv=None):
    c = GDN_CHUNK
    if t_inv is None:
        _mm_nn, _mm_nt, _mm_tn = (functools.partial(_mxu, dims=dd) for dd in (_BNN, _BNT, _BTN))
    else:
        _mm_nn, _mm_nt, _mm_tn = _MM_NN, _MM_NT, _MM_TN
    row, col = _iota((c, c), 0), _iota((c, c), 1)
    incl, strict = row >= col, row > col
    decay = jnp.where(incl, jnp.exp(jnp.where(incl, g1 - g2, 0.0)), 0.0)
    eg = jnp.exp(g1)
    kb = k * beta
    a = _mm_nt(kb, k) * jnp.where(strict, decay, 0.0)
    inv = _unit_lower_inverse(a) if t_inv is None else _SAVED_INVERSE(a, t_inv)
    u = _mm_nn(inv, v * beta)
    w = _mm_nn(inv, kb * eg)
    attn = _mm_nt(q, k) * decay
    v_new = u - _mm_nn(w, state)
    o = _mm_nn(q * eg, state) + _mm_nn(attn, v_new)
    new_state = state * jnp.exp(gl) + _mm_tn(k * jnp.exp(gl - g1), v_new)
    return (o, new_state, inv) if t_inv is None else (o, new_state)


def _unpair(x):
    return jnp.concatenate([x[..., :HEAD_DIM], x[..., HEAD_DIM:]], axis=0)


def _gdn_fwd(q, k, v, beta, g, gl, grow, cpb=2):
    t = q.shape[1]
    rows = cpb * GDN_CHUNK
    lo, hi = slice(0, HEAD_DIM), slice(HEAD_DIM, 2 * HEAD_DIM)

    def body(q_ref, k_ref, v_ref, b_ref, g_ref, gl_ref, grow_ref, o_ref, st_ref, inv_ref, state):
        @pl.when(pl.program_id(0) == 0)
        def _():
            state[...] = jnp.zeros_like(state)

        s = state[...]
        for cc in range(cpb):
            sl = slice(cc * GDN_CHUNK, (cc + 1) * GDN_CHUNK)
            st_ref[:, cc, :, lo], st_ref[:, cc, :, hi] = s[:HEADS // 2], s[HEADS // 2:]
            g2 = jnp.broadcast_to(grow_ref[:, cc], (HEADS, GDN_CHUNK, GDN_CHUNK))
            o, s, inv = _gdn_chunk(*[_unpair(r[:, sl, :]) for r in (q_ref, k_ref, v_ref, b_ref, g_ref)], g2,
                                   _unpair(gl_ref[:, sl, :]), s)
            o_ref[:, sl, lo], o_ref[:, sl, hi] = o[:HEADS // 2], o[HEADS // 2:]
            inv_ref[:, cc, :, lo], inv_ref[:, cc, :, hi] = inv[:HEADS // 2], inv[HEADS // 2:]
        state[...] = s

    hm = pl.BlockSpec((HEADS // 2, rows, 128), lambda i: (0, i, 0))
    per_chunk = pl.BlockSpec((HEADS // 2, cpb, GDN_CHUNK, 128), lambda i: (0, i, 0, 0))
    chunk_shape = jax.ShapeDtypeStruct((HEADS // 2, t // GDN_CHUNK, GDN_CHUNK, 128), F32)
    return pl.pallas_call(
        body, name="gdn_fwd", grid=(t // rows,),
        in_specs=[hm] * 6 + [pl.BlockSpec((HEADS, cpb, 1, GDN_CHUNK), lambda i: (0, i, 0, 0))],
        out_specs=[hm, per_chunk, per_chunk],
        out_shape=[jax.ShapeDtypeStruct((HEADS // 2, t, 128), F32), chunk_shape, chunk_shape],
        scratch_shapes=[pltpu.VMEM((HEADS, GDN_CHUNK, HEAD_DIM), F32)],
        compiler_params=_params(("arbitrary",)),
    )(q, k, v, beta, g, gl, grow)


def _gdn_bwd(q, k, v, beta, g, gl, grow, states, invs, do, cpb=1):
    t = q.shape[1]
    rows = cpb * GDN_CHUNK
    nsteps = t // rows
    lo, hi = slice(0, HEAD_DIM), slice(HEAD_DIM, 2 * HEAD_DIM)

    def body(q_ref, k_ref, v_ref, b_ref, g_ref, gl_ref, grow_ref, st_ref, inv_ref, do_ref,
             dq_ref, dk_ref, dv_ref, db_ref, dg_ref, dgl_ref, dgrow_ref, dstate):
        @pl.when(pl.program_id(0) == 0)
        def _():
            dstate[...] = jnp.zeros_like(dstate)

        ds = dstate[...]
        for cc in reversed(range(cpb)):
            sl = slice(cc * GDN_CHUNK, (cc + 1) * GDN_CHUNK)
            g2 = jnp.broadcast_to(grow_ref[:, cc], (HEADS, GDN_CHUNK, GDN_CHUNK))
            _, vjp = jax.vjp(_gdn_chunk, *[_unpair(r[:, sl, :]) for r in (q_ref, k_ref, v_ref, b_ref, g_ref)], g2,
                             _unpair(gl_ref[:, sl, :]), _unpair(st_ref[:, cc]), _unpair(inv_ref[:, cc]))
            gq, gk, gv, gb, gg1, gg2, ggl, ds, _ = vjp((_unpair(do_ref[:, sl, :]), ds))
            for ref, val in ((dq_ref, gq), (dk_ref, gk), (dv_ref, gv), (db_ref, gb), (dg_ref, gg1), (dgl_ref, ggl)):
                ref[:, sl, lo], ref[:, sl, hi] = val[:HEADS // 2], val[HEADS // 2:]
            dgrow_ref[:, cc] = jnp.sum(gg2, axis=1, keepdims=True)
        dstate[...] = ds

    hm = pl.BlockSpec((HEADS // 2, rows, 128), lambda i: (0, nsteps - 1 - i, 0))
    rowspec = pl.BlockSpec((HEADS, cpb, 1, GDN_CHUNK), lambda i: (0, nsteps - 1 - i, 0, 0))
    per_chunk = pl.BlockSpec((HEADS // 2, cpb, GDN_CHUNK, 128), lambda i: (0, nsteps - 1 - i, 0, 0))
    hm_shape = jax.ShapeDtypeStruct((HEADS // 2, t, 128), F32)
    return pl.pallas_call(
        body, name="gdn_bwd", grid=(nsteps,),
        in_specs=[hm] * 6 + [rowspec, per_chunk, per_chunk, hm],
        out_specs=[hm] * 6 + [rowspec],
        out_shape=[hm_shape] * 6 + [jax.ShapeDtypeStruct((HEADS, t // GDN_CHUNK, 1, GDN_CHUNK), F32)],
        scratch_shapes=[pltpu.VMEM((HEADS, GDN_CHUNK, HEAD_DIM), F32)],
        compiler_params=_params(("arbitrary",)),
    )(q, k, v, beta, g, gl, grow, states, invs, do)


def _gdn_out(o_hm, gdn_norm_w, proj, tm=256):
    t = proj.shape[0]

    def body(o_ref, w_ref, z_ref, oz_ref):
        w = w_ref[...]
        ones = _pair_ones()
        for p in range(HEADS // 2):
            cols = slice(p * 128, (p + 1) * 128)
            o = o_ref[p]
            z = z_ref[:, cols]
            r = lax.rsqrt(_pair_sum_lanes(o * o, ones) * (1.0 / HEAD_DIM) + NORM_EPS)
            oz_ref[:, cols] = (o * r * w * (z * _sigmoid(z))).astype(oz_ref.dtype)

    tok = pl.BlockSpec((tm, WIDTH), lambda i: (i, 0))
    return pl.pallas_call(
        body, name="gdn_out", grid=(t // tm,),
        in_specs=[pl.BlockSpec((HEADS // 2, tm, 128), lambda i: (0, i, 0)), pl.BlockSpec((1, 128), lambda i: (0, 0)),
                  pl.BlockSpec((tm, WIDTH), lambda i: (i, SEG_ZB // WIDTH))],
        out_specs=tok, out_shape=jax.ShapeDtypeStruct((t, WIDTH), MXU_DTYPE),
        compiler_params=_params(("parallel",)),
    )(o_hm, jnp.tile(gdn_norm_w, (1, 2)), proj)


def _gdn_out_bwd(dproj, d_oz, o_hm, gdn_norm_w, proj, tm=256):
    t = proj.shape[0]

    def body(dp_ref, doz_ref, o_ref, w_ref, z_ref, dz_ref, do_ref, dw_ref):
        w = w_ref[...]
        ones = _pair_ones()
        dw = jnp.zeros((1, 128), F32)
        for p in range(HEADS // 2):
            cols = slice(p * 128, (p + 1) * 128)
            o = o_ref[p]
            z, g = z_ref[:, cols], doz_ref[:, cols]
            sg = _sigmoid(z)
            r = lax.rsqrt(_pair_sum_lanes(o * o, ones) * (1.0 / HEAD_DIM) + NORM_EPS)
            dz_ref[:, cols] = (g * (o * r * w) * (sg * (1.0 + z * (1.0 - sg)))).astype(dz_ref.dtype)
            dn = g * (z * sg)
            dw += jnp.sum(dn * o * r, axis=0, keepdims=True)
            dnw = dn * w
            do_ref[p] = r * dnw - o * (r * r * r) * (_pair_sum_lanes(dnw * o, ones) * (1.0 / HEAD_DIM))

        @pl.when(pl.program_id(0) == 0)
        def _():
            dw_ref[...] = jnp.zeros_like(dw_ref)

        dw_ref[...] += jnp.where(_iota((8, 128), 0) == 0, dw, 0.0)

    tok = pl.BlockSpec((tm, WIDTH), lambda i: (i, 0))
    seg = pl.BlockSpec((tm, WIDTH), lambda i: (i, SEG_ZB // WIDTH))
    hm = pl.BlockSpec((HEADS // 2, tm, 128), lambda i: (0, i, 0))
    dz, do, dw = pl.pallas_call(
        body, name="gdn_out_bwd", grid=(t // tm,),
        in_specs=[pl.BlockSpec(memory_space=pl.ANY), tok, hm, pl.BlockSpec((1, 128), lambda i: (0, 0)), seg],
        out_specs=[seg, hm, pl.BlockSpec((8, 128), lambda i: (0, 0))],
        out_shape=[jax.ShapeDtypeStruct((t, PACKED_WIDTH), MXU_DTYPE), jax.ShapeDtypeStruct((HEADS // 2, t, 128), F32),
                   jax.ShapeDtypeStruct((8, 128), F32)],
        input_output_aliases={0: 0},
        compiler_params=_params(("arbitrary",)),
    )(dproj, d_oz, o_hm, jnp.tile(gdn_norm_w, (1, 2)), proj)
    return dz, do, dw[:, :HEAD_DIM] + dw[:, HEAD_DIM:]


def _merge(y_a, y_b, proj, tm=256):
    t = proj.shape[0]

    def body(ya_ref, yb_ref, ga_ref, gb_ref, m_ref):
        m_ref[...] = (_sigmoid(ga_ref[...]) * ya_ref[...] + _sigmoid(gb_ref[...]) * yb_ref[...]).astype(m_ref.dtype)

    half = pl.BlockSpec((tm, WIDTH), lambda i, c: (i, c))
    return pl.pallas_call(
        body, name="merge", grid=(t // tm, 2),
        in_specs=[half, half, pl.BlockSpec((tm, WIDTH), lambda i, c: (i, SEG_GA // WIDTH + c)),
                  pl.BlockSpec((tm, WIDTH), lambda i, c: (i, SEG_GB // WIDTH + c))],
        out_specs=half, out_shape=jax.ShapeDtypeStruct((t, D_MODEL), MXU_DTYPE),
        compiler_params=_params(("parallel", "parallel")),
    )(y_a, y_b, proj, proj)


def _merge_bwd(dproj, d_m, y, proj, seg, name, tm=256):
    t = proj.shape[0]

    def body(*refs):
        dm_ref, y_ref, g_ref, dg_ref, dy_ref = refs[-5:]
        dm = dm_ref[...]
        s = _sigmoid(g_ref[...])
        dy_ref[...] = (dm * s).astype(dy_ref.dtype)
        dg_ref[...] = (dm * y_ref[...] * s * (1.0 - s)).astype(dg_ref.dtype)

    half = pl.BlockSpec((tm, WIDTH), lambda i, c: (i, c))
    gate = pl.BlockSpec((tm, WIDTH), lambda i, c: (i, seg // WIDTH + c))
    specs, args, aliases = [half, half, gate], [d_m, y, proj], {}
    if dproj is not None:
        specs, args, aliases = [pl.BlockSpec(memory_space=pl.ANY)] + specs, [dproj] + args, {0: 0}
    return pl.pallas_call(
        body, name=name, grid=(t // tm, 2), in_specs=specs, out_specs=[gate, half],
        out_shape=[jax.ShapeDtypeStruct((t, PACKED_WIDTH), MXU_DTYPE), jax.ShapeDtypeStruct((t, D_MODEL), MXU_DTYPE)],
        input_output_aliases=aliases,
        compiler_params=_params(("parallel", "parallel")),
    )(*args)


def _tail(x, mo, final_w, target, tm=256):
    t = x.shape[0]

    def body(x_ref, mo_ref, w_ref, t_ref, dxm_ref, dx_ref, loss_ref, dw_ref):
        x2 = x_ref[...] + mo_ref[...]
        w = w_ref[...]
        r = lax.rsqrt(jnp.mean(x2 * x2, axis=-1, keepdims=True) + NORM_EPS)
        xn = x2 * r
        err = xn * w - t_ref[...]
        dy = err * (1.0 / D_MODEL)
        dyw = dy * w
        dx2 = r * dyw - x2 * (r * r * r) * jnp.mean(dyw * x2, axis=-1, keepdims=True)
        dx_ref[...] = dx2
        dxm_ref[...] = dx2.astype(dxm_ref.dtype)
        loss = 0.5 * jnp.sum(jnp.sum(err * err, axis=-1, keepdims=True) * (1.0 / D_MODEL), axis=0, keepdims=True)
        onehot = jnp.where((_iota((8, 128), 0) == 0) & (_iota((8, 128), 1) == 0), 1.0, 0.0)

        @pl.when(pl.program_id(0) == 0)
        def _():
            loss_ref[...] = jnp.zeros_like(loss_ref)
            dw_ref[...] = jnp.zeros_like(dw_ref)

        loss_ref[...] += loss * onehot
        dw_ref[...] += jnp.where(_iota((8, D_MODEL), 0) == 0, jnp.sum(dy * xn, axis=0, keepdims=True), 0.0)

    tok = pl.BlockSpec((tm, D_MODEL), lambda i: (i, 0))
    return pl.pallas_call(
        body, name="tail", grid=(t // tm,),
        in_specs=[tok, tok, pl.BlockSpec((1, D_MODEL), lambda i: (0, 0)), tok],
        out_specs=[tok, tok, pl.BlockSpec((8, 128), lambda i: (0, 0)), pl.BlockSpec((8, D_MODEL), lambda i: (0, 0))],
        out_shape=[jax.ShapeDtypeStruct((t, D_MODEL), MXU_DTYPE), jax.ShapeDtypeStruct((t, D_MODEL), F32),
                   jax.ShapeDtypeStruct((8, 128), F32), jax.ShapeDtypeStruct((8, D_MODEL), F32)],
        compiler_params=_params(("arbitrary",)),
    )(x, mo, final_w, target)


def _norm_bwd(x, norm_w, dh, dx2, tm=256):
    t = x.shape[0]

    def body(x_ref, w_ref, dh_ref, dx2_ref, dx_ref, dw_ref):
        xf, w, dh_ = x_ref[...], w_ref[...], dh_ref[...]
        r = lax.rsqrt(jnp.mean(xf * xf, axis=-1, keepdims=True) + NORM_EPS)
        dhw = dh_ * w
        dx_ref[...] = dx2_ref[...] + r * dhw - xf * (r * r * r) * jnp.mean(dhw * xf, axis=-1, keepdims=True)

        @pl.when(pl.program_id(0) == 0)
        def _():
            dw_ref[...] = jnp.zeros_like(dw_ref)

        dw_ref[...] += jnp.where(_iota((8, D_MODEL), 0) == 0, jnp.sum(dh_ * xf * r, axis=0, keepdims=True), 0.0)

    tok = pl.BlockSpec((tm, D_MODEL), lambda i: (i, 0))
    return pl.pallas_call(
        body, name="norm_bwd", grid=(t // tm,),
        in_specs=[tok, pl.BlockSpec((1, D_MODEL), lambda i: (0, 0)), tok, tok],
        out_specs=[tok, pl.BlockSpec((8, D_MODEL), lambda i: (0, 0))],
        out_shape=[jax.ShapeDtypeStruct((t, D_MODEL), F32), jax.ShapeDtypeStruct((8, D_MODEL), F32)],
        compiler_params=_params(("arbitrary",)),
    )(x, norm_w, dh, dx2)


def _local_step(x, target, norm_w, wp, conv_w, a_log, dt_bias, gdn_norm_w, w_up_a, w_up_b, w_out, final_w, start_reduce):
    t = x.shape[0]
    tables = _rope_tables(t)
    a_log = jnp.pad(a_log, ((0, 0), (HEADS, 128 - 2 * HEADS)))
    dt_bias = jnp.pad(dt_bias, ((0, 0), (HEADS, 128 - 2 * HEADS)))

    proj, h_t = _norm_proj(x, norm_w, wp)
    qkvs = _rope_fwd(proj, tables)
    outs, lses = zip(*[_att_fwd(qkvs[gi], d, f"att_fwd{gi}") for gi, d in enumerate(DILATIONS)])
    oz_a, o_a, *lse_views = _att_merge(outs, lses, proj)
    conv = _conv_fwd(proj, conv_w)
    gq, gk, gv, gb, gg, ggl, grow = _gdn_prep(conv, proj, a_log, dt_bias)
    o_b, states, invs = _gdn_fwd(gq, gk, gv, gb, gg, ggl, grow)
    oz_b = _gdn_out(o_b, gdn_norm_w, proj)
    big = dict(tm=1024, tn=1024, tk=1024)
    y_a = _matmul(oz_a, w_up_a, "nn", "up_a", **big)
    y_b = _matmul(oz_b, w_up_b, "nn", "up_b", **big)
    merged = _merge(y_a, y_b, proj)
    mo = _matmul(merged, w_out, "nn", "out_proj", **big)
    dx2_m, dx2, loss_blk, d_final = _tail(x, mo, final_w, target)

    d_wout = _matmul(merged, dx2_m, "tn", "d_w_out", **big)
    d_m = _matmul(dx2_m, w_out, "nt", "d_merged", **big)
    dproj, dy_a = _merge_bwd(None, d_m, y_a, proj, SEG_GA, "merge_bwd_a")
    dproj, dy_b = _merge_bwd(dproj, d_m, y_b, proj, SEG_GB, "merge_bwd_b")
    d_wua = _matmul(oz_a, dy_a, "tn", "d_w_up_a", **big)
    d_wub = _matmul(oz_b, dy_b, "tn", "d_w_up_b", **big)
    d_oz_a = _matmul(dy_a, w_up_a, "nt", "d_oz_a", **big)
    d_oz_b = _matmul(dy_b, w_up_b, "nt", "d_oz_b", **big)
    dproj, *views = _att_merge_bwd(dproj, d_oz_a, o_a, proj)
    do_views, delta_views = views[:3], views[3:]
    dqkvs = [_att_bwd(qkvs[gi], do_views[gi], lse_views[gi], delta_views[gi], d, f"att_bwd{gi}")
             for gi, d in enumerate(DILATIONS)]
    dproj = _rope_bwd(dproj, dqkvs, tables)
    dproj, do_b, d_gnw = _gdn_out_bwd(dproj, d_oz_b, o_b, gdn_norm_w, proj)
    dgq, dgk, dgv, dgb, dgg, dggl, dgrow = _gdn_bwd(gq, gk, gv, gb, gg, ggl, grow, states, invs, do_b)
    dproj, dconv, d_small = _gdn_prep_bwd(dproj, conv, proj, a_log, dt_bias, dgq, dgk, dgv, dgb, dgg, dggl, dgrow)
    dproj, d_convw = _conv_bwd(dproj, dconv, proj, conv_w)
    d_wp = _matmul(h_t, dproj, "nn", "d_w_in", tm=1024, tn=PACKED_WIDTH // 4, tk=1024)
    in_flight, token = start_reduce(d_wp, d_wua, d_wub, d_wout, d_convw[0:GDN_CONV])
    dh = _matmul(dproj, wp, "nt", "d_h", tm=1024, tn=1024, tk=PACKED_WIDTH // 4, after=token)
    grad_x, d_norm = _norm_bwd(x, norm_w, dh, dx2)
    return dict(loss=loss_blk, grad_x=grad_x, norm_w=d_norm[0:1], in_flight=in_flight,
                a_log=d_small[0:1, HEADS:2 * HEADS], dt_bias=d_small[1:2, HEADS:2 * HEADS], gdn_norm_w=d_gnw[0:1],
                final_norm_w=d_final[0:1])


SHARDS = 4
W_IN_SHARD = IN_WIDTH // SHARDS
ROWS_UP = WIDTH * (D_MODEL // SHARDS) // 128
ROWS_OUT = (D_MODEL // SHARDS) * D_MODEL // 128
CONV_SHARD = 3 * WIDTH // SHARDS
ROWS_CONV = 16
SLAB_ROWS = 2 * ROWS_UP + ROWS_OUT + 2 * ROWS_CONV
HALF_ROWS = SLAB_ROWS // 2
BIG_HALF = (D_MODEL // 2, W_IN_SHARD)
SMALL_HALF = (HALF_ROWS, 128)
MESH = pl.DeviceIdType.MESH
ANY = pl.BlockSpec(memory_space=pl.ANY)


def _pad_rows(a, rows):
    return jnp.pad(a, ((0, rows - a.shape[0]), (0, 0)))


def _pack_slab(w_up_a, w_up_b, w_out, conv, conv_lo):
    parts = [w_up_a.reshape(ROWS_UP, 128), w_up_b.reshape(ROWS_UP, 128), w_out.reshape(ROWS_OUT, 128),
             _pad_rows(conv.reshape(-1, 128), ROWS_CONV), _pad_rows(conv_lo.reshape(-1, 128), ROWS_CONV)]
    return jnp.concatenate(parts, axis=0).reshape(2, *SMALL_HALF)


def _unpack_slab(slab):
    slab = slab.reshape(SLAB_ROWS, 128)
    r0 = 0
    out = []
    for rows, shape in ((ROWS_UP, (WIDTH, D_MODEL // SHARDS)), (ROWS_UP, (WIDTH, D_MODEL // SHARDS)),
                        (ROWS_OUT, (D_MODEL // SHARDS, D_MODEL)), (ROWS_CONV, None), (ROWS_CONV, None)):
        part = slab[r0:r0 + rows]
        out.append(part[:GDN_CONV * CONV_SHARD // 128].reshape(GDN_CONV, CONV_SHARD) if shape is None else part.reshape(shape))
        r0 += rows
    return out


def _mesh_position():
    x, y, c = lax.axis_index("x"), lax.axis_index("y"), lax.axis_index("c")
    return x, y, c, [(1 - x, y), (x, 1 - y), (1 - x, 1 - y)]


def _gather_weights(shards):
    n = len(shards)

    def body(*refs):
        in_refs, out_refs, (send_sems, recv_sems) = refs[:n], refs[n:2 * n], refs[2 * n:]
        x, y, c, chips = _mesh_position()

        def half(a, chip, which):
            return out_refs[a].at[2 * chip[0] + chip[1], which]

        def copy(k, src, dst, to):
            return pltpu.make_async_remote_copy(src_ref=src, dst_ref=dst, send_sem=send_sems.at[k], recv_sem=recv_sems.at[k],
                                                device_id=to, device_id_type=MESH)

        pairs = [(a, j, chip) for a in range(n) for j, chip in enumerate(chips)]
        first = [copy(6 * a + j, in_refs[a].at[c], half(a, (x, y), c), (*chip, c)) for a, j, chip in pairs]
        for cp in first:
            cp.start()
        passed = [copy(6 * a + 3 + j, half(a, chip, c), half(a, chip, c), (x, y, 1 - c)) for a, j, chip in pairs]
        for i, (a, j, chip) in enumerate(pairs):
            copy(6 * a + j, half(a, chip, c), half(a, chip, c), (x, y, c)).wait_recv()
            passed[i].start()
        for a, j, chip in pairs:
            copy(6 * a + 3 + j, half(a, chip, 1 - c), half(a, chip, 1 - c), (x, y, c)).wait_recv()
        for cp in first + passed:
            cp.wait_send()

    return pl.pallas_call(
        body, name="gather_weights", in_specs=[ANY] * n, out_specs=[ANY] * n,
        out_shape=[jax.ShapeDtypeStruct((SHARDS, *s.shape), s.dtype) for s in shards],
        scratch_shapes=[pltpu.SemaphoreType.DMA((6 * n,)), pltpu.SemaphoreType.DMA((6 * n,))],
    )(*shards)


def _exchange_halves(grads):
    n = len(grads)

    def body(*refs):
        g_refs, out_refs, (send_sems, recv_sems) = refs[:n], refs[n:2 * n], refs[2 * n:]
        x, y, c, _ = _mesh_position()
        copies = [pltpu.make_async_remote_copy(src_ref=g_refs[a].at[s, 1 - c], dst_ref=out_refs[a].at[s],
                                               send_sem=send_sems.at[SHARDS * a + s], recv_sem=recv_sems.at[SHARDS * a + s],
                                               device_id=(x, y, 1 - c), device_id_type=MESH)
                  for a in range(n) for s in range(SHARDS)]
        for cp in copies:
            cp.start()
        for cp in copies:
            cp.wait()

    return pl.pallas_call(
        body, name="exchange_halves", in_specs=[ANY] * n, out_specs=[ANY] * n,
        out_shape=[jax.ShapeDtypeStruct((SHARDS, *g.shape[2:]), F32) for g in grads],
        scratch_shapes=[pltpu.SemaphoreType.DMA((SHARDS * n,)), pltpu.SemaphoreType.DMA((SHARDS * n,))],
    )(*grads)


def _pair_sum(grads, recv, blk, name):
    _, _, rows, cols = grads.shape

    def body(c_ref, g_ref, r_ref, o_ref):
        o_ref[...] = (g_ref[0] + r_ref[...]).astype(o_ref.dtype)

    spec = pl.BlockSpec((1, blk, cols), lambda s, i, c_ref: (s, i, 0))
    return pl.pallas_call(
        body, name=name,
        grid_spec=pltpu.PrefetchScalarGridSpec(
            num_scalar_prefetch=1, grid=(SHARDS, rows // blk),
            in_specs=[pl.BlockSpec((1, 1, blk, cols), lambda s, i, c_ref: (s, c_ref[0], i, 0)), spec],
            out_specs=spec),
        out_shape=jax.ShapeDtypeStruct((SHARDS, rows, cols), MXU_DTYPE),
        compiler_params=_params(("parallel", "parallel")),
    )(lax.axis_index("c").astype(jnp.int32).reshape(1), grads, recv)


_HBM = pl.BlockSpec(memory_space=pltpu.HBM)
_SEM = pl.BlockSpec(memory_space=pltpu.SEMAPHORE)
_DATAFLOW = pltpu.SideEffectType.DATAFLOW_SIDE_EFFECTING


def _scatter_copies(p_refs, l_refs, send_sems, recv_sems):
    x, y, c, chips = _mesh_position()
    return [pltpu.make_async_remote_copy(src_ref=p_refs[a].at[2 * chip[0] + chip[1]], dst_ref=l_refs[a].at[j],
                                         send_sem=send_sems.at[3 * a + j], recv_sem=recv_sems.at[3 * a + j],
                                         device_id=(*chip, c), device_id_type=MESH)
            for a in range(len(p_refs)) for j, chip in enumerate(chips)]


def _scatter_start(pairs):
    n = len(pairs)
    lands = [lax.empty((3, *p.shape[1:]), p.dtype) for p in pairs]

    def body(*refs):
        p_refs, l_refs, send_sems, recv_sems, token = refs[:n], refs[n:2 * n], refs[2 * n], refs[2 * n + 1], refs[-1]
        for cp in _scatter_copies(p_refs, l_refs, send_sems, recv_sems):
            cp.start()
        token[...] = jnp.zeros_like(token)

    operands = [pltpu.with_memory_space_constraint(a, pltpu.HBM) for a in (*pairs, *lands)]
    return pl.pallas_call(
        body, name="scatter_start", in_specs=[_HBM] * (2 * n),
        out_shape=(pltpu.SemaphoreType.DMA((3 * n,)), pltpu.SemaphoreType.DMA((3 * n,)),
                   *[pltpu.HBM(a.shape, a.dtype) for a in operands], jax.ShapeDtypeStruct((8, 128), F32)),
        out_specs=(_SEM, _SEM, *[_HBM] * (2 * n), pl.BlockSpec(memory_space=pltpu.VMEM)),
        input_output_aliases={i: 2 + i for i in range(2 * n)},
        compiler_params=pltpu.CompilerParams(has_side_effects=_DATAFLOW),
    )(*operands)


def _scatter_wait(send_sems, recv_sems, passed, after):
    n = len(passed) // 2

    def body(*refs):
        p_refs, l_refs, send_s, recv_s = refs[:n], refs[n:2 * n], refs[2 * n], refs[2 * n + 1]
        for cp in _scatter_copies(p_refs, l_refs, send_s, recv_s):
            cp.wait_send()
            cp.wait_recv()

    return pl.pallas_call(
        body, name="scatter_wait", in_specs=[_HBM] * (2 * n) + [_SEM, _SEM, ANY],
        out_shape=[pltpu.HBM(a.shape, a.dtype) for a in passed], out_specs=[_HBM] * (2 * n),
        input_output_aliases={i: i for i in range(2 * n)},
        compiler_params=pltpu.CompilerParams(has_side_effects=_DATAFLOW),
    )(*passed, send_sems, recv_sems, after)


def _chip_sum(pairs, recv, blk, name):
    _, rows, cols = pairs.shape

    def body(pos_ref, p_ref, r_ref, o_ref):
        o_ref[0] = ((p_ref[0].astype(F32) + r_ref[0].astype(F32)) + r_ref[1].astype(F32)) + r_ref[2].astype(F32)

    pos = jnp.stack([2 * lax.axis_index("x") + lax.axis_index("y"), lax.axis_index("c")]).astype(jnp.int32)
    return pl.pallas_call(
        body, name=name,
        grid_spec=pltpu.PrefetchScalarGridSpec(
            num_scalar_prefetch=1, grid=(rows // blk,),
            in_specs=[pl.BlockSpec((1, blk, cols), lambda i, pos_ref: (pos_ref[0], i, 0)),
                      pl.BlockSpec((3, blk, cols), lambda i, pos_ref: (0, i, 0))],
            out_specs=pl.BlockSpec((1, blk, cols), lambda i, pos_ref: (pos_ref[1], i, 0))),
        out_shape=jax.ShapeDtypeStruct((2, rows, cols), F32),
        compiler_params=_params(("parallel",)),
    )(pos, pairs, recv)


def _share_total(totals):
    n = len(totals)

    def body(*refs):
        t_refs, out_refs, (send_sems, recv_sems) = refs[:n], refs[n:2 * n], refs[2 * n:]
        x, y, c, _ = _mesh_position()
        copies = [pltpu.make_async_remote_copy(src_ref=t_refs[a].at[c], dst_ref=out_refs[a].at[c], send_sem=send_sems.at[a],
                                               recv_sem=recv_sems.at[a], device_id=(x, y, 1 - c), device_id_type=MESH)
                  for a in range(n)]
        for cp in copies:
            cp.start()
        for a in range(n):
            other = out_refs[a].at[1 - c]
            pltpu.make_async_remote_copy(src_ref=other, dst_ref=other, send_sem=send_sems.at[a], recv_sem=recv_sems.at[a],
                                         device_id=(x, y, c), device_id_type=MESH).wait_recv()
        for cp in copies:
            cp.wait_send()

    return pl.pallas_call(
        body, name="share_total", in_specs=[ANY] * n, out_specs=[ANY] * n,
        out_shape=[jax.ShapeDtypeStruct(t.shape, F32) for t in totals],
        scratch_shapes=[pltpu.SemaphoreType.DMA((n,)), pltpu.SemaphoreType.DMA((n,))],
        input_output_aliases={a: a for a in range(n)},
    )(*totals)


def _allreduce_small(block):
    def body(b_ref, out_ref, gath, send_sems, recv_sems):
        x, y, c, _ = _mesh_position()
        me = 4 * x + 2 * y + c
        gath[me] = b_ref[...]
        copies = []
        for k in range(1, 8):
            peer = (x ^ (k >> 2), y ^ ((k >> 1) & 1), c ^ (k & 1))
            copies.append(pltpu.make_async_remote_copy(src_ref=b_ref, dst_ref=gath.at[me], send_sem=send_sems.at[k - 1],
                                                       recv_sem=recv_sems.at[k - 1], device_id=peer, device_id_type=MESH))
        for cp in copies:
            cp.start()
        for k in range(1, 8):
            src = 4 * (x ^ (k >> 2)) + 2 * (y ^ ((k >> 1) & 1)) + (c ^ (k & 1))
            pltpu.make_async_remote_copy(src_ref=b_ref, dst_ref=gath.at[src], send_sem=send_sems.at[k - 1],
                                         recv_sem=recv_sems.at[k - 1], device_id=(x, y, c), device_id_type=MESH).wait_recv()
        for cp in copies:
            cp.wait_send()
        acc = gath[0]
        for d in range(1, 8):
            acc = acc + gath[d]
        out_ref[...] = acc

    vm = pl.BlockSpec(memory_space=pltpu.VMEM)
    return pl.pallas_call(
        body, name="allreduce_small", in_specs=[vm], out_specs=vm,
        out_shape=jax.ShapeDtypeStruct((8, D_MODEL), F32),
        scratch_shapes=[pltpu.VMEM((8, 8, D_MODEL), F32), pltpu.SemaphoreType.DMA((7,)), pltpu.SemaphoreType.DMA((7,))],
    )(block)


def _adamw(w, g, m, v, name):
    rows, cols = w.shape
    tr = 128 if rows % 128 == 0 else rows

    def body(w_ref, g_ref, m_ref, v_ref, d_ref, nm_ref, nv_ref):
        gv = g_ref[...]
        nm = ADAM_B1 * m_ref[...] + (1.0 - ADAM_B1) * gv
        nv = ADAM_B2 * v_ref[...] + (1.0 - ADAM_B2) * (gv * gv)
        m_hat = nm / (1.0 - ADAM_B1 ** ADAM_STEP)
        v_hat = nv / (1.0 - ADAM_B2 ** ADAM_STEP)
        d_ref[...] = -ADAM_LR * (m_hat / (jnp.sqrt(v_hat) + ADAM_EPS) + ADAM_WD * w_ref[...])
        nm_ref[...] = nm
        nv_ref[...] = nv

    spec = pl.BlockSpec((tr, cols), lambda i: (i, 0))
    shape = jax.ShapeDtypeStruct((rows, cols), F32)
    return pl.pallas_call(
        body, name=name, grid=(rows // tr,), in_specs=[spec] * 4, out_specs=[spec] * 3, out_shape=[shape] * 3,
        compiler_params=_params(("parallel",)),
    )(w, g, m, v)


def kernel(x, norm_w, w_in, conv_w, a_log, dt_bias, gdn_norm_w, w_up_a, w_up_b, w_out, final_norm_w, loss_target, m_norm_w, m_w_in, m_conv_w, m_a_log, m_dt_bias, m_gdn_norm_w, m_w_up_a, m_w_up_b, m_w_out, m_final_norm_w, v_norm_w, v_w_in, v_conv_w, v_a_log, v_dt_bias, v_gdn_norm_w, v_w_up_a, v_w_up_b, v_w_out, v_final_norm_w):
    conv_hi = conv_w[0].astype(MXU_DTYPE)
    conv_lo = (conv_w[0] - conv_hi.astype(F32)).astype(MXU_DTYPE)
    big = w_in[0].astype(MXU_DTYPE).reshape(2, *BIG_HALF)
    slab = _pack_slab(w_up_a[0].astype(MXU_DTYPE), w_up_b[0].astype(MXU_DTYPE), w_out[0].astype(MXU_DTYPE), conv_hi, conv_lo)
    own_shard = 2 * lax.axis_index("x") + lax.axis_index("y")
    bigs, slabs = _gather_weights([big, slab])
    bigs = lax.dynamic_update_slice(bigs, big[None], (own_shard, 0, 0, 0)).reshape(SHARDS, D_MODEL, W_IN_SHARD)
    slabs = lax.dynamic_update_slice(slabs, slab[None], (own_shard, 0, 0, 0))
    parts = [_unpack_slab(slabs[s]) for s in range(SHARDS)]
    split = BA_END - (SHARDS - 1) * W_IN_SHARD
    wp = jnp.concatenate([bigs[s] for s in range(SHARDS - 1)]
                         + [bigs[-1][:, :split], jnp.zeros((D_MODEL, SEG_GA - BA_END), MXU_DTYPE), bigs[-1][:, split:]], axis=1)
    w_up_a_full = jnp.concatenate([p[0] for p in parts], axis=1)
    w_up_b_full = jnp.concatenate([p[1] for p in parts], axis=1)
    w_out_full = jnp.concatenate([p[2] for p in parts], axis=0)
    conv_full = jnp.concatenate([p[3].astype(F32) + p[4].astype(F32) for p in parts], axis=1)

    blocks, tags = (128, HALF_ROWS), ("w_in", "slab")

    def start_reduce(d_wp, d_w_up_a, d_w_up_b, d_w_out, d_conv_w):
        d_w_in = [d_wp[:, s * W_IN_SHARD:(s + 1) * W_IN_SHARD] for s in range(SHARDS - 1)]
        d_w_in.append(jnp.concatenate([d_wp[:, (SHARDS - 1) * W_IN_SHARD:BA_END], d_wp[:, SEG_GA:]], axis=1))
        zero_conv = jnp.zeros((GDN_CONV, CONV_SHARD), F32)
        grads = [jnp.stack(d_w_in).reshape(SHARDS, 2, *BIG_HALF),
                 jnp.stack([_pack_slab(d_w_up_a[:, s * 256:(s + 1) * 256], d_w_up_b[:, s * 256:(s + 1) * 256],
                                       d_w_out[s * 256:(s + 1) * 256], d_conv_w[:, s * CONV_SHARD:(s + 1) * CONV_SHARD],
                                       zero_conv) for s in range(SHARDS)])]
        from_sibling = _exchange_halves(grads)
        pairs = [_pair_sum(gr, fs, blk, f"pair_sum_{tag}") for gr, fs, blk, tag in zip(grads, from_sibling, blocks, tags)]
        *in_flight, token = _scatter_start(pairs)
        return in_flight, token

    g = _local_step(x[0], loss_target[0], norm_w, wp, conv_full, a_log, dt_bias, gdn_norm_w,
                    w_up_a_full, w_up_b_full, w_out_full, final_norm_w[None], start_reduce)

    send_sems, recv_sems, *passed = g["in_flight"]
    arrived = _scatter_wait(send_sems, recv_sems, passed, after=g["grad_x"])
    pairs, from_chips = arrived[:2], arrived[2:]
    total_big, total_slab = _share_total([_chip_sum(p, fc, blk, f"chip_sum_{tag}")
                                          for p, fc, blk, tag in zip(pairs, from_chips, blocks, tags)])
    g_w_in = total_big.reshape(D_MODEL, W_IN_SHARD)
    g_w_up_a, g_w_up_b, g_w_out, g_conv, _ = _unpack_slab(total_slab)

    row2 = jnp.concatenate([g["gdn_norm_w"], g["a_log"], g["dt_bias"], g["loss"][0:1, 0:1],
                            jnp.zeros((1, D_MODEL - HEAD_DIM - 2 * HEADS - 1), F32)], axis=1)
    small = _allreduce_small(jnp.concatenate([g["norm_w"], g["final_norm_w"], row2, jnp.zeros((5, D_MODEL), F32)], axis=0))
    g_norm, g_final = small[0:1], small[1]
    g_gnw, g_alog, g_dt = small[2:3, 0:HEAD_DIM], small[2:3, HEAD_DIM:HEAD_DIM + HEADS], small[2:3, HEAD_DIM + HEADS:HEAD_DIM + 2 * HEADS]
    loss = small[2, HEAD_DIM + 2 * HEADS]

    names = ["norm_w", "w_in", "conv_w", "a_log", "dt_bias", "gdn_norm_w", "w_up_a", "w_up_b", "w_out", "final_norm_w"]
    weights = dict(zip(names, (norm_w, w_in, conv_w, a_log, dt_bias, gdn_norm_w, w_up_a, w_up_b, w_out, final_norm_w)))
    ms = dict(zip(names, (m_norm_w, m_w_in, m_conv_w, m_a_log, m_dt_bias, m_gdn_norm_w, m_w_up_a, m_w_up_b, m_w_out, m_final_norm_w)))
    vs = dict(zip(names, (v_norm_w, v_w_in, v_conv_w, v_a_log, v_dt_bias, v_gdn_norm_w, v_w_up_a, v_w_up_b, v_w_out, v_final_norm_w)))
    grads2d = dict(norm_w=g_norm, w_in=g_w_in, conv_w=g_conv, a_log=g_alog, dt_bias=g_dt, gdn_norm_w=g_gnw,
                   w_up_a=g_w_up_a, w_up_b=g_w_up_b, w_out=g_w_out, final_norm_w=g_final[None])
    grad_out, delta, new_m, new_v = [], [], [], []
    for n in names:
        shape = weights[n].shape
        two_d = grads2d[n].shape
        d, nm, nv = _adamw(weights[n].reshape(two_d), grads2d[n], ms[n].reshape(two_d), vs[n].reshape(two_d), f"adamw_{n}")
        grad_out.append(grads2d[n].reshape(shape))
        delta.append(d.reshape(shape))
        new_m.append(nm.reshape(shape))
        new_v.append(nv.reshape(shape))
    return (loss, g["grad_x"][None], *grad_out, *delta, *new_m, *new_v)
```

```python
import functools

import jax
import jax.numpy as jnp
from jax import lax
from jax.experimental import pallas as pl
from jax.experimental.pallas import tpu as pltpu

F32 = jnp.float32
MXU_DTYPE = jnp.bfloat16
HIGHEST = lax.Precision.HIGHEST

D_MODEL = 1024
HEADS = 8
HEAD_DIM = 64
WIDTH = HEADS * HEAD_DIM
NORM_EPS = 1e-6
ROPE_THETA = 10000.0
ATT_BLOCK = 128
DILATIONS = (1, 4, 16)
GDN_CHUNK = 64
GDN_CONV = 4
IN_WIDTH = 9232
SEG_A, SEG_ZA, SEG_B, SEG_ZB, SEG_BA, SEG_GA, SEG_GB, PACKED_WIDTH = 0, 4608, 5120, 6656, 7168, 7680, 8704, 9728
BA_END = 7184
VMEM_LIMIT = 56 * 1024 * 1024

ADAM_LR, ADAM_B1, ADAM_B2, ADAM_EPS, ADAM_WD, ADAM_STEP = 0.001, 0.9, 0.999, 1e-08, 0.01, 10

_NN = (((1,), (0,)), ((), ()))
_NT = (((1,), (1,)), ((), ()))
_TN = (((0,), (0,)), ((), ()))


def _params(sem):
    return pltpu.CompilerParams(dimension_semantics=sem, vmem_limit_bytes=VMEM_LIMIT)


def _mxu(a, b, dims):
    return lax.dot_general(a.astype(MXU_DTYPE), b.astype(MXU_DTYPE), dims, preferred_element_type=F32)


def _sigmoid(x):
    return 1.0 / (1.0 + jnp.exp(-x))


def _softplus(x):
    return jnp.maximum(x, 0.0) + jnp.log(1.0 + jnp.exp(-jnp.abs(x)))


def _iota(shape, axis):
    return lax.broadcasted_iota(jnp.int32, shape, axis)


def _matmul(a, b, mode, name, out_dtype=F32, tm=512, tn=512, tk=512, after=None):
    if mode == "nn":
        (m, k), (k2, n) = a.shape, b.shape
    elif mode == "nt":
        (m, k), (n, k2) = a.shape, b.shape
    else:
        (k, m), (k2, n) = a.shape, b.shape
    assert k == k2
    tm, tn, tk = min(tm, m), min(tn, n), min(tk, k)
    assert m % tm == 0 and n % tn == 0 and k % tk == 0
    nk = k // tk
    dims = {"nn": _NN, "nt": _NT, "tn": _TN}[mode]

    assert out_dtype == F32

    def body(a_ref, b_ref, *rest):
        o_ref = rest[-1]
        kk = pl.program_id(2)
        part = _mxu(a_ref[...], b_ref[...], dims)

        @pl.when(kk == 0)
        def _():
            o_ref[...] = part

        @pl.when(kk > 0)
        def _():
            o_ref[...] += part

    a_spec = pl.BlockSpec((tk, tm), lambda i, j, kk: (kk, i)) if mode == "tn" else pl.BlockSpec((tm, tk), lambda i, j, kk: (i, kk))
    b_spec = pl.BlockSpec((tn, tk), lambda i, j, kk: (j, kk)) if mode == "nt" else pl.BlockSpec((tk, tn), lambda i, j, kk: (kk, j))
    extra_specs, extra_args = ([], []) if after is None else ([pl.BlockSpec(memory_space=pl.ANY)], [after])
    return pl.pallas_call(
        body, name=name, grid=(m // tm, n // tn, nk), in_specs=[a_spec, b_spec] + extra_specs,
        out_specs=pl.BlockSpec((tm, tn), lambda i, j, kk: (i, j)),
        out_shape=jax.ShapeDtypeStruct((m, n), out_dtype),
        compiler_params=_params(("parallel", "parallel", "arbitrary")),
    )(a, b, *extra_args)


def _norm_proj(x, norm_w, wp, tm=1024, tn=PACKED_WIDTH // 4):
    t = x.shape[0]
    tm = min(tm, t)

    def body(x_ref, nw_ref, w_ref, proj_ref, ht_ref, h_scr):
        @pl.when(pl.program_id(1) == 0)
        def _():
            xf = x_ref[...]
            r = lax.rsqrt(jnp.mean(xf * xf, axis=-1, keepdims=True) + NORM_EPS)
            h = xf * r * nw_ref[...]
            h_scr[...] = h.astype(h_scr.dtype)
            ht_ref[...] = h.T.astype(ht_ref.dtype)

        proj_ref[...] = jnp.dot(h_scr[...], w_ref[...], preferred_element_type=F32)

    return pl.pallas_call(
        body, name="norm_proj", grid=(t // tm, PACKED_WIDTH // tn),
        in_specs=[pl.BlockSpec((tm, D_MODEL), lambda i, j: (i, 0)),
                  pl.BlockSpec((1, D_MODEL), lambda i, j: (0, 0)),
                  pl.BlockSpec((D_MODEL, tn), lambda i, j: (0, j))],
        out_specs=[pl.BlockSpec((tm, tn), lambda i, j: (i, j)),
                   pl.BlockSpec((D_MODEL, tm), lambda i, j: (0, i))],
        out_shape=[jax.ShapeDtypeStruct((t, PACKED_WIDTH), F32), jax.ShapeDtypeStruct((D_MODEL, t), MXU_DTYPE)],
        scratch_shapes=[pltpu.VMEM((tm, D_MODEL), MXU_DTYPE)],
        compiler_params=_params(("parallel", "arbitrary")),
    )(x, norm_w, wp)


def _rope_tables(t):
    lane = jnp.arange(128)
    inv_freq = ROPE_THETA ** (-jnp.arange(0, HEAD_DIM, 2, dtype=F32) / HEAD_DIM)
    ang = jnp.arange(t, dtype=F32)[:, None] * inv_freq[None, :]
    ang = jnp.concatenate([ang, ang, ang, ang], axis=-1)
    first_half = (lane % HEAD_DIM) < HEAD_DIM // 2
    cos, sin = jnp.cos(ang), jnp.sin(ang)
    return cos, jnp.where(first_half, -sin, 0.0), jnp.where(first_half, 0.0, sin)


def _rope_block(x, cos, sin_lo, sin_hi, sign):
    outs = []
    for c in range(8):
        xc = x[:, c * 128:(c + 1) * 128]
        rot = pltpu.roll(xc, 96, 1) * sin_lo + pltpu.roll(xc, 32, 1) * sin_hi
        outs.append(xc * cos + sign * rot)
    outs.append(x[:, 2 * WIDTH:])
    return jnp.concatenate(outs, axis=1)


def _tile_scratch(tm, cols):
    return pltpu.VMEM((cols // 128, tm, 128), F32)


def _store_tile(scr, y):
    for c in range(scr.shape[0]):
        scr[c] = y[:, c * 128:(c + 1) * 128]


def _load_tile(scr):
    return jnp.concatenate([scr[c] for c in range(scr.shape[0])], axis=1)


def _to_strided_view(scr, o_ref, d):
    n, tm, _ = scr.shape
    for r in range(d):
        for c in range(n):
            o_ref[:, (r * n + c) * 128:(r * n + c + 1) * 128] = scr[c, pl.ds(r, tm // d, stride=d), :].astype(o_ref.dtype)


def _from_strided_view(i_ref, scr, d):
    n, tm, _ = scr.shape
    for r in range(d):
        for c in range(n):
            scr[c, pl.ds(r, tm // d, stride=d), :] = i_ref[:, (r * n + c) * 128:(r * n + c + 1) * 128].astype(F32)


def _strided_spec(tm, d, cols):
    return pl.BlockSpec((tm // d, d * cols), lambda i: (i, 0))


def _rope_fwd(proj, tables, tm=256):
    t = proj.shape[0]
    cols = 3 * WIDTH

    def body(x_ref, c_ref, sl_ref, sh_ref, o0, o1, o2, scr):
        for g, (d, o_ref) in enumerate(zip(DILATIONS, (o0, o1, o2))):
            y = _rope_block(x_ref[:, g * cols:(g + 1) * cols], c_ref[...], sl_ref[...], sh_ref[...], 1.0)
            if d == 1:
                o_ref[...] = y.astype(o_ref.dtype)
            else:
                _store_tile(scr, y)
                _to_strided_view(scr, o_ref, d)

    tab = pl.BlockSpec((tm, 128), lambda i: (i, 0))
    return pl.pallas_call(
        body, name="rope_fwd", grid=(t // tm,),
        in_specs=[pl.BlockSpec((tm, 3 * cols), lambda i: (i, 0)), tab, tab, tab],
        out_specs=[_strided_spec(tm, d, cols) for d in DILATIONS],
        out_shape=[jax.ShapeDtypeStruct((t // d, d * cols), MXU_DTYPE) for d in DILATIONS],
        scratch_shapes=[_tile_scratch(tm, cols)],
        compiler_params=_params(("parallel",)),
    )(proj, *tables)


def _rope_bwd(dproj, dqkvs, tables, tm=256):
    t = dproj.shape[0]
    cols = 3 * WIDTH

    def body(dp_ref, i0, i1, i2, c_ref, sl_ref, sh_ref, o_ref, scr):
        for g, (d, i_ref) in enumerate(zip(DILATIONS, (i0, i1, i2))):
            if d == 1:
                x = i_ref[...]
            else:
                _from_strided_view(i_ref, scr, d)
                x = _load_tile(scr)
            y = _rope_block(x, c_ref[...], sl_ref[...], sh_ref[...], -1.0)
            o_ref[:, g * cols:(g + 1) * cols] = y.astype(o_ref.dtype)

    tab = pl.BlockSpec((tm, 128), lambda i: (i, 0))
    return pl.pallas_call(
        body, name="rope_bwd", grid=(t // tm,),
        in_specs=[pl.BlockSpec(memory_space=pl.ANY)] + [_strided_spec(tm, d, cols) for d in DILATIONS] + [tab, tab, tab],
        out_specs=pl.BlockSpec((tm, 3 * cols), lambda i: (i, 0)),
        out_shape=jax.ShapeDtypeStruct((t, PACKED_WIDTH), MXU_DTYPE),
        scratch_shapes=[_tile_scratch(tm, cols)],
        input_output_aliases={0: 0},
        compiler_params=_params(("parallel",)),
    )(dproj, *dqkvs, *tables)


def _att_masks():
    qi = _iota((ATT_BLOCK, ATT_BLOCK), 0)
    kj = _iota((ATT_BLOCK, ATT_BLOCK), 1)
    return kj <= qi, kj >= qi


def _att_fwd(qkv, d, name):
    rows = qkv.shape[0]
    nb = rows // ATT_BLOCK
    scale = HEAD_DIM ** -0.5

    def body(q_ref, kc_ref, kp_ref, vc_ref, vp_ref, o_ref, lse_ref):
        has_prev = pl.program_id(1) > 0
        m_cur, m_prev = _att_masks()
        m_prev = m_prev & has_prev
        hs = range(HEADS)
        sls = [slice(h * HEAD_DIM, (h + 1) * HEAD_DIM) for h in hs]
        qs = [q_ref[:, sl] for sl in sls]
        s_c = [jnp.where(m_cur, _mxu(qs[h], kc_ref[:, sls[h]], _NT) * scale, -jnp.inf) for h in hs]
        s_p = [jnp.where(m_prev, _mxu(qs[h], kp_ref[:, sls[h]], _NT) * scale, -jnp.inf) for h in hs]
        m = [jnp.maximum(jnp.max(s_c[h], axis=1, keepdims=True), jnp.max(s_p[h], axis=1, keepdims=True)) for h in hs]
        p_c = [jnp.exp(s_c[h] - m[h]) for h in hs]
        p_p = [jnp.exp(s_p[h] - m[h]) for h in hs]
        den = [jnp.sum(p_c[h], axis=1, keepdims=True) + jnp.sum(p_p[h], axis=1, keepdims=True) for h in hs]
        o = [_mxu(p_c[h], vc_ref[:, sls[h]], _NN) + _mxu(p_p[h], vp_ref[:, sls[h]], _NN) for h in hs]
        for h in hs:
            o_ref[:, sls[h]] = o[h] / den[h]
            lse_ref[:, sls[h]] = jnp.broadcast_to(m[h] + jnp.log(den[h]), (ATT_BLOCK, HEAD_DIM))

    def cur(c):
        return pl.BlockSpec((ATT_BLOCK, WIDTH), lambda r, i: (i, 3 * r + c))

    def prev(c):
        return pl.BlockSpec((ATT_BLOCK, WIDTH), lambda r, i: (jnp.maximum(i - 1, 0), 3 * r + c))

    out = pl.BlockSpec((ATT_BLOCK, WIDTH), lambda r, i: (i, r))
    return pl.pallas_call(
        body, name=name, grid=(d, nb), in_specs=[cur(0), cur(1), prev(1), cur(2), prev(2)],
        out_specs=[out, out],
        out_shape=[jax.ShapeDtypeStruct((rows, d * WIDTH), F32)] * 2,
        compiler_params=_params(("parallel", "arbitrary")),
    )(qkv, qkv, qkv, qkv, qkv)


def _att_bwd(qkv, do, lse, delta, d, name):
    rows = qkv.shape[0]
    nb = rows // ATT_BLOCK
    scale = HEAD_DIM ** -0.5

    def body(q_ref, qn_ref, kc_ref, kp_ref, vc_ref, vp_ref, do_ref, don_ref, l_ref, ln_ref, dl_ref, dln_ref, o_ref):
        i = pl.program_id(1)
        m_cur, m_prev = _att_masks()
        m_p = m_prev & (i > 0)
        m_n = m_prev & (i < nb - 1)

        hs = range(HEADS)
        sls = [slice(h * HEAD_DIM, (h + 1) * HEAD_DIM) for h in hs]
        col = [slice(h * HEAD_DIM, h * HEAD_DIM + 1) for h in hs]

        def probs(q_r, k_r, lse_r, mask):
            s = [_mxu(q_r[:, sls[h]], k_r[:, sls[h]], _NT) for h in hs]
            return [jnp.where(mask, jnp.exp(s[h] * scale - lse_r[:, col[h]]), 0.0) for h in hs]

        def dscores(p, do_r, v_r, dl_r):
            dp = [_mxu(do_r[:, sls[h]], v_r[:, sls[h]], _NT) for h in hs]
            return [(p[h] * (dp[h] - dl_r[:, col[h]])).astype(MXU_DTYPE) for h in hs]

        p = probs(q_ref, kc_ref, l_ref, m_cur)
        ds = dscores(p, do_ref, vc_ref, dl_ref)
        dq = [_mxu(ds[h], kc_ref[:, sls[h]], _NN) for h in hs]
        dk = [_mxu(ds[h], q_ref[:, sls[h]], _TN) for h in hs]
        dv = [_mxu(p[h], do_ref[:, sls[h]], _TN) for h in hs]
        p = probs(q_ref, kp_ref, l_ref, m_p)
        ds = dscores(p, do_ref, vp_ref, dl_ref)
        dq = [dq[h] + _mxu(ds[h], kp_ref[:, sls[h]], _NN) for h in hs]
        p = probs(qn_ref, kc_ref, ln_ref, m_n)
        ds = dscores(p, don_ref, vc_ref, dln_ref)
        dk = [dk[h] + _mxu(ds[h], qn_ref[:, sls[h]], _TN) for h in hs]
        dv = [dv[h] + _mxu(p[h], don_ref[:, sls[h]], _TN) for h in hs]
        for h in hs:
            o_ref[:, sls[h]] = dq[h] * scale
            o_ref[:, WIDTH + h * HEAD_DIM:WIDTH + (h + 1) * HEAD_DIM] = dk[h] * scale
            o_ref[:, 2 * WIDTH + h * HEAD_DIM:2 * WIDTH + (h + 1) * HEAD_DIM] = dv[h]

    def qkv_spec(c, shift):
        def idx(r, i):
            return (jnp.clip(i + shift, 0, nb - 1), 3 * r + c)
        return pl.BlockSpec((ATT_BLOCK, WIDTH), idx)

    def tok_spec(shift):
        def idx(r, i):
            return (jnp.clip(i + shift, 0, nb - 1), r)
        return pl.BlockSpec((ATT_BLOCK, WIDTH), idx)

    return pl.pallas_call(
        body, name=name, grid=(d, nb),
        in_specs=[qkv_spec(0, 0), qkv_spec(0, 1), qkv_spec(1, 0), qkv_spec(1, -1), qkv_spec(2, 0), qkv_spec(2, -1),
                  tok_spec(0), tok_spec(1), tok_spec(0), tok_spec(1), tok_spec(0), tok_spec(1)],
        out_specs=pl.BlockSpec((ATT_BLOCK, 3 * WIDTH), lambda r, i: (i, r)),
        out_shape=jax.ShapeDtypeStruct((rows, d * 3 * WIDTH), F32),
        compiler_params=_params(("parallel", "arbitrary")),
    )(qkv, qkv, qkv, qkv, qkv, qkv, do, do, lse, lse, delta, delta)


def _att_merge(os_, lses, proj, tm=256):
    t = proj.shape[0]

    def body(o0, o1, o2, l0, l1, l2, z_ref, oz_ref, o_ref, t0, t1, t2, s_o1, s_l1, s_o2, s_l2, s_t):
        _from_strided_view(o1, s_o1, DILATIONS[1])
        _from_strided_view(l1, s_l1, DILATIONS[1])
        _from_strided_view(o2, s_o2, DILATIONS[2])
        _from_strided_view(l2, s_l2, DILATIONS[2])
        a, b, c = l0[...], _load_tile(s_l1), _load_tile(s_l2)
        m = jnp.maximum(jnp.maximum(a, b), c)
        wa, wb, wc = jnp.exp(a - m), jnp.exp(b - m), jnp.exp(c - m)
        den = wa + wb + wc
        o = (wa * o0[...] + wb * _load_tile(s_o1) + wc * _load_tile(s_o2)) / den
        z = z_ref[...]
        o_ref[...] = o
        oz_ref[...] = (o * z * _sigmoid(z)).astype(oz_ref.dtype)
        total = m + jnp.log(den)
        t0[...] = total
        _store_tile(s_t, total)
        _to_strided_view(s_t, t1, DILATIONS[1])
        _to_strided_view(s_t, t2, DILATIONS[2])

    tok = pl.BlockSpec((tm, WIDTH), lambda i: (i, 0))
    views = [_strided_spec(tm, d, WIDTH) for d in DILATIONS]
    view_shapes = [jax.ShapeDtypeStruct((t // d, d * WIDTH), F32) for d in DILATIONS]
    return pl.pallas_call(
        body, name="att_merge", grid=(t // tm,),
        in_specs=views + views + [pl.BlockSpec((tm, WIDTH), lambda i: (i, SEG_ZA // WIDTH))],
        out_specs=[tok, tok] + views,
        out_shape=[jax.ShapeDtypeStruct((t, WIDTH), MXU_DTYPE), jax.ShapeDtypeStruct((t, WIDTH), F32)] + view_shapes,
        scratch_shapes=[_tile_scratch(tm, WIDTH)] * 5,
        compiler_params=_params(("parallel",)),
    )(*os_, *lses, proj)


def _att_merge_bwd(dproj, d_oz, o, proj, tm=256):
    t = proj.shape[0]

    def body(dp_ref, doz_ref, o_ref, z_ref, dz_ref, do0, do1, do2, dl0, dl1, dl2, s_do, s_dl):
        z, ov, g = z_ref[...], o_ref[...], doz_ref[...]
        sg = _sigmoid(z)
        do = g * z * sg
        dz_ref[...] = (g * ov * sg * (1.0 + z * (1.0 - sg))).astype(dz_ref.dtype)
        do0[...] = do.astype(do0.dtype)
        _store_tile(s_do, do)
        prod = do * ov
        for p in range(HEADS // 2):
            s_dl[p] = _pair_sum_lanes(prod[:, p * 128:(p + 1) * 128], _pair_ones())
        dl0[...] = _load_tile(s_dl)
        for d, do_v, dl_v in ((DILATIONS[1], do1, dl1), (DILATIONS[2], do2, dl2)):
            _to_strided_view(s_do, do_v, d)
            _to_strided_view(s_dl, dl_v, d)

    tok = pl.BlockSpec((tm, WIDTH), lambda i: (i, 0))
    seg = pl.BlockSpec((tm, WIDTH), lambda i: (i, SEG_ZA // WIDTH))
    views = [_strided_spec(tm, d, WIDTH) for d in DILATIONS]
    return pl.pallas_call(
        body, name="att_merge_bwd", grid=(t // tm,),
        in_specs=[pl.BlockSpec(memory_space=pl.ANY), tok, tok, seg],
        out_specs=[seg] + views + views,
        out_shape=[jax.ShapeDtypeStruct((t, PACKED_WIDTH), MXU_DTYPE)]
        + [jax.ShapeDtypeStruct((t // d, d * WIDTH), MXU_DTYPE) for d in DILATIONS]
        + [jax.ShapeDtypeStruct((t // d, d * WIDTH), F32) for d in DILATIONS],
        scratch_shapes=[_tile_scratch(tm, WIDTH)] * 2,
        input_output_aliases={0: 0},
        compiler_params=_params(("parallel",)),
    )(dproj, d_oz, o, proj)


def _shift_down(x, halo, s):
    if s == 0:
        return x
    xs = pltpu.roll(x, s, 0)
    head = jnp.where(_iota((8, x.shape[1]), 0) < s, pltpu.roll(halo, s, 0), xs[0:8])
    return jnp.concatenate([head, xs[8:]], axis=0)


def _shift_up(x, nxt, s):
    if s == 0:
        return x
    n = x.shape[0]
    xs = pltpu.roll(x, n - s, 0)
    tail = jnp.where(_iota((8, x.shape[1]), 0) >= 8 - s, pltpu.roll(nxt, 8 - s, 0), xs[n - 8:])
    return jnp.concatenate([xs[:n - 8], tail], axis=0)


def _conv_fwd(proj, conv_w, tm=1024):
    t = proj.shape[0]
    cb = SEG_B // WIDTH

    def body(x_ref, halo_ref, w_ref, c_ref):
        halo = jnp.where(pl.program_id(0) > 0, halo_ref[...], 0.0)
        x = x_ref[...]
        w = w_ref[...]
        acc = jnp.zeros((tm, WIDTH), F32)
        for j in range(GDN_CONV):
            acc += _shift_down(x, halo, GDN_CONV - 1 - j) * w[j:j + 1, :]
        c_ref[...] = acc

    return pl.pallas_call(
        body, name="conv_fwd", grid=(t // tm, 3),
        in_specs=[pl.BlockSpec((tm, WIDTH), lambda i, c: (i, cb + c)),
                  pl.BlockSpec((8, WIDTH), lambda i, c: (jnp.maximum(i * (tm // 8) - 1, 0), cb + c)),
                  pl.BlockSpec((GDN_CONV, WIDTH), lambda i, c: (0, c))],
        out_specs=pl.BlockSpec((tm, WIDTH), lambda i, c: (i, c)),
        out_shape=jax.ShapeDtypeStruct((t, 3 * WIDTH), F32),
        compiler_params=_params(("parallel", "parallel")),
    )(proj, proj, conv_w)


def _conv_bwd(dproj, dc, proj, conv_w, tm=1024):
    t = proj.shape[0]
    cb = SEG_B // WIDTH
    nt = t // tm

    def body(dp_ref, dc_ref, dcn_ref, x_ref, halo_ref, w_ref, dx_ref, dw_ref):
        i = pl.program_id(1)
        w = w_ref[...]
        dcn = jnp.where(i < nt - 1, dcn_ref[...], 0.0)
        dcv = dc_ref[...]
        acc = jnp.zeros((tm, WIDTH), F32)
        for j in range(GDN_CONV):
            acc += _shift_up(dcv, dcn, GDN_CONV - 1 - j) * w[j:j + 1, :]
        dx_ref[...] = acc.astype(dx_ref.dtype)
        halo = jnp.where(i > 0, halo_ref[...], 0.0)
        x = x_ref[...]
        row8 = _iota((8, WIDTH), 0)
        part = jnp.zeros((8, WIDTH), F32)
        for j in range(GDN_CONV):
            s = jnp.sum(dcv * _shift_down(x, halo, GDN_CONV - 1 - j), axis=0, keepdims=True)
            part += jnp.where(row8 == j, s, 0.0)

        @pl.when(i == 0)
        def _():
            dw_ref[...] = jnp.zeros_like(dw_ref)

        dw_ref[...] += part

    return pl.pallas_call(
        body, name="conv_bwd", grid=(3, nt),
        in_specs=[pl.BlockSpec(memory_space=pl.ANY),
                  pl.BlockSpec((tm, WIDTH), lambda c, i: (i, c)),
                  pl.BlockSpec((8, WIDTH), lambda c, i: (jnp.minimum((i + 1) * (tm // 8), t // 8 - 1), c)),
                  pl.BlockSpec((tm, WIDTH), lambda c, i: (i, cb + c)),
                  pl.BlockSpec((8, WIDTH), lambda c, i: (jnp.maximum(i * (tm // 8) - 1, 0), cb + c)),
                  pl.BlockSpec((GDN_CONV, WIDTH), lambda c, i: (0, c))],
        out_specs=[pl.BlockSpec((tm, WIDTH), lambda c, i: (i, cb + c)),
                   pl.BlockSpec((8, WIDTH), lambda c, i: (0, c))],
        out_shape=[jax.ShapeDtypeStruct((t, PACKED_WIDTH), MXU_DTYPE), jax.ShapeDtypeStruct((8, 3 * WIDTH), F32)],
        input_output_aliases={0: 0},
        compiler_params=_params(("parallel", "arbitrary")),
    )(dproj, dc, dc, proj, proj, conv_w)


def _chunk_matrices(tm):
    r, c = _iota((tm, tm), 0), _iota((tm, tm), 1)
    same = (r // GDN_CHUNK) == (c // GDN_CHUNK)
    return jnp.where(same & (c <= r), 1.0, 0.0), jnp.where(same, 1.0, 0.0)


def _gdn_gates(ba, a_log, dt_bias):
    al = ba + dt_bias
    return _sigmoid(ba), -jnp.exp(a_log) * _softplus(al), _sigmoid(al)


SCAN_HEADS = (0, 2, 4, 6, 1, 3, 5, 7)


def _head_lane_eye():
    return jnp.where(_iota((HEADS, 128), 1) == _iota((HEADS, 128), 0) + HEADS, 1.0, 0.0)


def _pair_ones():
    return jnp.where(_iota((128, 128), 0) // HEAD_DIM == _iota((128, 128), 1) // HEAD_DIM, 1.0, 0.0).astype(MXU_DTYPE)


def _pair_to_lanes(p, base):
    return jnp.where(_iota((128, 128), 1) == base + 2 * p + _iota((128, 128), 0) // HEAD_DIM, 1.0, 0.0).astype(MXU_DTYPE)


def _pair_sum_lanes(x, sel):
    hi, lo = _split(x)
    return jnp.dot(hi, sel, preferred_element_type=F32) + jnp.dot(lo, sel, preferred_element_type=F32)


def _gdn_prep(conv, proj, a_log, dt_bias, tm=256):
    t = proj.shape[0]
    nc = tm // GDN_CHUNK

    def body(c_ref, ba_ref, al_ref, dt_ref, q_ref, k_ref, v_ref, b_ref, g_ref, gl_ref, grow_ref):
        beta, g, _ = _gdn_gates(ba_ref[:, 0:128], al_ref[...], dt_ref[...])
        lmat, cmat = _chunk_matrices(tm)
        gc = jnp.dot(lmat, g, precision=HIGHEST, preferred_element_type=F32)
        gl = jnp.dot(cmat, g, precision=HIGHEST, preferred_element_type=F32)
        grow = lax.dot_general(_head_lane_eye(), gc, _NT, precision=HIGHEST, preferred_element_type=F32)
        ones = _pair_ones()
        first = _iota((1, 128), 1) < HEAD_DIM

        def spread(x, base, p):
            return jnp.where(first, x[:, base + 2 * p:base + 2 * p + 1], x[:, base + 2 * p + 1:base + 2 * p + 2])

        for p in range(HEADS // 2):
            for seg, ref, scale in ((0, q_ref, HEAD_DIM ** -0.5), (1, k_ref, 1.0), (2, v_ref, None)):
                c = c_ref[:, seg * WIDTH + p * 128:seg * WIDTH + (p + 1) * 128]
                a = c * _sigmoid(c)
                if scale is not None:
                    a = a * (lax.rsqrt(_pair_sum_lanes(a * a, ones) + NORM_EPS) * scale)
                ref[p] = a
            b_ref[p] = spread(beta, 0, p)
            g_ref[p] = spread(gc, HEADS, p)
            gl_ref[p] = spread(gl, HEADS, p)
        for pos, h in enumerate(SCAN_HEADS):
            for cc in range(nc):
                grow_ref[pos, cc] = grow[h:h + 1, cc * GDN_CHUNK:(cc + 1) * GDN_CHUNK]

    hm = pl.BlockSpec((HEADS // 2, tm, 128), lambda i: (0, i, 0))
    small = pl.BlockSpec((1, 128), lambda i: (0, 0))
    hm_shape = jax.ShapeDtypeStruct((HEADS // 2, t, 128), F32)
    return pl.pallas_call(
        body, name="gdn_prep", grid=(t // tm,),
        in_specs=[pl.BlockSpec((tm, 3 * WIDTH), lambda i: (i, 0)),
                  pl.BlockSpec((tm, WIDTH), lambda i: (i, SEG_BA // WIDTH)), small, small],
        out_specs=[hm] * 6 + [pl.BlockSpec((HEADS, nc, 1, GDN_CHUNK), lambda i: (0, i, 0, 0))],
        out_shape=[hm_shape] * 6 + [jax.ShapeDtypeStruct((HEADS, t // GDN_CHUNK, 1, GDN_CHUNK), F32)],
        compiler_params=_params(("parallel",)),
    )(conv, proj, a_log, dt_bias)


def _gdn_prep_bwd(dproj, conv, proj, a_log, dt_bias, dq, dk, dv, db, dg, dgl, dgrow, tm=256):
    t = proj.shape[0]
    nc = tm // GDN_CHUNK

    def body(dp_ref, c_ref, ba_ref, al_ref, dt_ref, dq_ref, dk_ref, dv_ref, db_ref, dg_ref, dgl_ref, dgrow_ref,
             dba_ref, dc_ref, small_ref, row_scr):
        beta, g, sig_al = _gdn_gates(ba_ref[:, 0:128], al_ref[...], dt_ref[...])
        d_beta = jnp.zeros((tm, 128), F32)
        d_gc = jnp.zeros((tm, 128), F32)
        d_gl = jnp.zeros((tm, 128), F32)
        ones = _pair_ones()
        for pos, h in enumerate(SCAN_HEADS):
            for cc in range(nc):
                row_scr[h:h + 1, cc * GDN_CHUNK:(cc + 1) * GDN_CHUNK] = dgrow_ref[pos, cc]
        for p in range(HEADS // 2):
            d_beta += _pair_sum_lanes(db_ref[p], _pair_to_lanes(p, 0))
            d_gc += _pair_sum_lanes(dg_ref[p], _pair_to_lanes(p, HEADS))
            d_gl += _pair_sum_lanes(dgl_ref[p], _pair_to_lanes(p, HEADS))
            for seg, ref, scale in ((0, dq_ref, HEAD_DIM ** -0.5), (1, dk_ref, 1.0), (2, dv_ref, None)):
                cols = slice(seg * WIDTH + p * 128, seg * WIDTH + (p + 1) * 128)
                c = c_ref[:, cols]
                sg = _sigmoid(c)
                da = ref[p]
                if scale is not None:
                    a = c * sg
                    r = lax.rsqrt(_pair_sum_lanes(a * a, ones) + NORM_EPS)
                    da = da * scale
                    da = r * da - a * (r * r * r) * _pair_sum_lanes(da * a, ones)
                dc_ref[:, cols] = da * (sg * (1.0 + c * (1.0 - sg)))
        d_gc += lax.dot_general(row_scr[...], _head_lane_eye(), _TN, precision=HIGHEST, preferred_element_type=F32)
        lmat, cmat = _chunk_matrices(tm)
        d_g = (lax.dot_general(lmat, d_gc, _TN, precision=HIGHEST, preferred_element_type=F32)
               + lax.dot_general(cmat, d_gl, _TN, precision=HIGHEST, preferred_element_type=F32))
        d_al = d_g * (-jnp.exp(al_ref[...])) * sig_al
        d_bl = d_beta * beta * (1.0 - beta)
        dba_ref[...] = jnp.concatenate([d_bl + d_al, jnp.zeros((tm, WIDTH - 128), F32)], axis=1).astype(dba_ref.dtype)
        row8 = _iota((8, 128), 0)
        part = (jnp.where(row8 == 0, jnp.sum(d_g * g, axis=0, keepdims=True), 0.0)
                + jnp.where(row8 == 1, jnp.sum(d_al, axis=0, keepdims=True), 0.0))

        @pl.when(pl.program_id(0) == 0)
        def _():
            small_ref[...] = jnp.zeros_like(small_ref)

        small_ref[...] += part

    hm = pl.BlockSpec((HEADS // 2, tm, 128), lambda i: (0, i, 0))
    small = pl.BlockSpec((1, 128), lambda i: (0, 0))
    seg = pl.BlockSpec((tm, WIDTH), lambda i: (i, SEG_BA // WIDTH))
    return pl.pallas_call(
        body, name="gdn_prep_bwd", grid=(t // tm,),
        in_specs=[pl.BlockSpec(memory_space=pl.ANY), pl.BlockSpec((tm, 3 * WIDTH), lambda i: (i, 0)), seg, small, small]
        + [hm] * 6 + [pl.BlockSpec((HEADS, nc, 1, GDN_CHUNK), lambda i: (0, i, 0, 0))],
        out_specs=[seg, pl.BlockSpec((tm, 3 * WIDTH), lambda i: (i, 0)), pl.BlockSpec((8, 128), lambda i: (0, 0))],
        out_shape=[jax.ShapeDtypeStruct((t, PACKED_WIDTH), MXU_DTYPE), jax.ShapeDtypeStruct((t, 3 * WIDTH), F32),
                   jax.ShapeDtypeStruct((8, 128), F32)],
        scratch_shapes=[pltpu.VMEM((HEADS, tm), F32)],
        input_output_aliases={0: 0},
        compiler_params=_params(("arbitrary",)),
    )(dproj, conv, proj, a_log, dt_bias, dq, dk, dv, db, dg, dgl, dgrow)


_BNN = (((2,), (1,)), ((0,), (0,)))
_BNT = (((2,), (2,)), ((0,), (0,)))
_BTN = (((1,), (1,)), ((0,), (0,)))


@jax.custom_vjp
def _MM_NN(a, b):
    return _mxu(a, b, _BNN)


@jax.custom_vjp
def _MM_NT(a, b):
    return _mxu(a, b, _BNT)


@jax.custom_vjp
def _MM_TN(a, b):
    return _mxu(a, b, _BTN)


_MM_NN.defvjp(lambda a, b: (_mxu(a, b, _BNN), (a, b)), lambda r, g: (_mxu(g, r[1], _BNT), _mxu(r[0], g, _BTN)))
_MM_NT.defvjp(lambda a, b: (_mxu(a, b, _BNT), (a, b)), lambda r, g: (_mxu(g, r[1], _BNN), _mxu(g, r[0], _BTN)))
_MM_TN.defvjp(lambda a, b: (_mxu(a, b, _BTN), (a, b)), lambda r, g: (_mxu(r[1], g, _BNT), _mxu(r[0], g, _BNN)))


def _split(a):
    hi = a.astype(MXU_DTYPE)
    return hi, (a - hi.astype(F32)).astype(MXU_DTYPE)


def _dot3(a, b, dims):
    (ah, al), (bh, bl) = a, b
    (ca,), (cb,) = dims[0]
    return lax.dot_general(jnp.concatenate([ah, ah, al], axis=ca), jnp.concatenate([bh, bl, bh], axis=cb), dims,
                           preferred_element_type=F32)


def _unit_lower_inverse(a):
    c = GDN_CHUNK
    eye = jnp.where(_iota((c, c), 0) == _iota((c, c), 1), 1.0, 0.0)
    x = eye - a
    p = a
    for _ in range(5):
        ps = _split(p)
        p = _dot3(ps, ps, _BNN)
        x = x + _dot3(_split(x), _split(p), _BNN)
    return x


@jax.custom_vjp
def _SAVED_INVERSE(a, t_inv):
    return t_inv


def _saved_inverse_bwd(t_inv, g):
    ts = _split(t_inv)
    return -_dot3(ts, _split(_dot3(_split(g), ts, _BNT)), _BTN), jnp.zeros_like(t_inv)


_SAVED_INVERSE.defvjp(lambda a, t_inv: (t_inv, t_inv), _saved_inverse_bwd)


def _gdn_chunk(q, k, v, beta, g1, g2, gl, state, t_inv=None):
    c = GDN_CHUNK
    if t_inv is None:
        _mm_nn, _mm_nt, _mm_tn = (functools.partial(_mxu, dims=dd) for dd in (_BNN, _BNT, _BTN))
    else:
        _mm_nn, _mm_nt, _mm_tn = _MM_NN, _MM_NT, _MM_TN
    row, col = _iota((c, c), 0), _iota((c, c), 1)
    incl, strict = row >= col, row > col
    decay = jnp.where(incl, jnp.exp(jnp.where(incl, g1 - g2, 0.0)), 0.0)
    eg = jnp.exp(g1)
    kb = k * beta
    a = _mm_nt(kb, k) * jnp.where(strict, decay, 0.0)
    inv = _unit_lower_inverse(a) if t_inv is None else _SAVED_INVERSE(a, t_inv)
    u = _mm_nn(inv, v * beta)
    w = _mm_nn(inv, kb * eg)
    attn = _mm_nt(q, k) * decay
    v_new = u - _mm_nn(w, state)
    o = _mm_nn(q * eg, state) + _mm_nn(attn, v_new)
    new_state = state * jnp.exp(gl) + _mm_tn(k * jnp.exp(gl - g1), v_new)
    return (o, new_state, inv) if t_inv is None else (o, new_state)


def _unpair(x):
    return jnp.concatenate([x[..., :HEAD_DIM], x[..., HEAD_DIM:]], axis=0)


def _gdn_fwd(q, k, v, beta, g, gl, grow, cpb=4):
    t = q.shape[1]
    rows = cpb * GDN_CHUNK
    lo, hi = slice(0, HEAD_DIM), slice(HEAD_DIM, 2 * HEAD_DIM)

    def body(q_ref, k_ref, v_ref, b_ref, g_ref, gl_ref, grow_ref, o_ref, st_ref, inv_ref, state):
        @pl.when(pl.program_id(0) == 0)
        def _():
            state[...] = jnp.zeros_like(state)

        s = state[...]
        for cc in range(cpb):
            sl = slice(cc * GDN_CHUNK, (cc + 1) * GDN_CHUNK)
            st_ref[:, cc, :, lo], st_ref[:, cc, :, hi] = s[:HEADS // 2], s[HEADS // 2:]
            g2 = jnp.broadcast_to(grow_ref[:, cc], (HEADS, GDN_CHUNK, GDN_CHUNK))
            o, s, inv = _gdn_chunk(*[_unpair(r[:, sl, :]) for r in (q_ref, k_ref, v_ref, b_ref, g_ref)], g2,
                                   _unpair(gl_ref[:, sl, :]), s)
            o_ref[:, sl, lo], o_ref[:, sl, hi] = o[:HEADS // 2], o[HEADS // 2:]
            inv_ref[:, cc, :, lo], inv_ref[:, cc, :, hi] = inv[:HEADS // 2], inv[HEADS // 2:]
        state[...] = s

    hm = pl.BlockSpec((HEADS // 2, rows, 128), lambda i: (0, i, 0))
    per_chunk = pl.BlockSpec((HEADS // 2, cpb, GDN_CHUNK, 128), lambda i: (0, i, 0, 0))
    chunk_shape = jax.ShapeDtypeStruct((HEADS // 2, t // GDN_CHUNK, GDN_CHUNK, 128), F32)
    return pl.pallas_call(
        body, name="gdn_fwd", grid=(t // rows,),
        in_specs=[hm] * 6 + [pl.BlockSpec((HEADS, cpb, 1, GDN_CHUNK), lambda i: (0, i, 0, 0))],
        out_specs=[hm, per_chunk, per_chunk],
        out_shape=[jax.ShapeDtypeStruct((HEADS // 2, t, 128), F32), chunk_shape, chunk_shape],
        scratch_shapes=[pltpu.VMEM((HEADS, GDN_CHUNK, HEAD_DIM), F32)],
        compiler_params=_params(("arbitrary",)),
    )(q, k, v, beta, g, gl, grow)


def _gdn_bwd(q, k, v, beta, g, gl, grow, states, invs, do, cpb=2):
    t = q.shape[1]
    rows = cpb * GDN_CHUNK
    nsteps = t // rows
    lo, hi = slice(0, HEAD_DIM), slice(HEAD_DIM, 2 * HEAD_DIM)

    def body(q_ref, k_ref, v_ref, b_ref, g_ref, gl_ref, grow_ref, st_ref, inv_ref, do_ref,
             dq_ref, dk_ref, dv_ref, db_ref, dg_ref, dgl_ref, dgrow_ref, dstate):
        @pl.when(pl.program_id(0) == 0)
        def _():
            dstate[...] = jnp.zeros_like(dstate)

        ds = dstate[...]
        for cc in reversed(range(cpb)):
            sl = slice(cc * GDN_CHUNK, (cc + 1) * GDN_CHUNK)
            g2 = jnp.broadcast_to(grow_ref[:, cc], (HEADS, GDN_CHUNK, GDN_CHUNK))
            _, vjp = jax.vjp(_gdn_chunk, *[_unpair(r[:, sl, :]) for r in (q_ref, k_ref, v_ref, b_ref, g_ref)], g2,
                             _unpair(gl_ref[:, sl, :]), _unpair(st_ref[:, cc]), _unpair(inv_ref[:, cc]))
            gq, gk, gv, gb, gg1, gg2, ggl, ds, _ = vjp((_unpair(do_ref[:, sl, :]), ds))
            for ref, val in ((dq_ref, gq), (dk_ref, gk), (dv_ref, gv), (db_ref, gb), (dg_ref, gg1), (dgl_ref, ggl)):
                ref[:, sl, lo], ref[:, sl, hi] = val[:HEADS // 2], val[HEADS // 2:]
            dgrow_ref[:, cc] = jnp.sum(gg2, axis=1, keepdims=True)
        dstate[...] = ds

    hm = pl.BlockSpec((HEADS // 2, rows, 128), lambda i: (0, nsteps - 1 - i, 0))
    rowspec = pl.BlockSpec((HEADS, cpb, 1, GDN_CHUNK), lambda i: (0, nsteps - 1 - i, 0, 0))
    per_chunk = pl.BlockSpec((HEADS // 2, cpb, GDN_CHUNK, 128), lambda i: (0, nsteps - 1 - i, 0, 0))
    hm_shape = jax.ShapeDtypeStruct((HEADS // 2, t, 128), F32)
    return pl.pallas_call(
        body, name="gdn_bwd", grid=(nsteps,),
        in_specs=[hm] * 6 + [rowspec, per_chunk, per_chunk, hm],
        out_specs=[hm] * 6 + [rowspec],
        out_shape=[hm_shape] * 6 + [jax.ShapeDtypeStruct((HEADS, t // GDN_CHUNK, 1, GDN_CHUNK), F32)],
        scratch_shapes=[pltpu.VMEM((HEADS, GDN_CHUNK, HEAD_DIM), F32)],
        compiler_params=_params(("arbitrary",)),
    )(q, k, v, beta, g, gl, grow, states, invs, do)


def _gdn_out(o_hm, gdn_norm_w, proj, tm=512):
    t = proj.shape[0]

    def body(o_ref, w_ref, z_ref, oz_ref):
        w = w_ref[...]
        ones = _pair_ones()
        for p in range(HEADS // 2):
            cols = slice(p * 128, (p + 1) * 128)
            o = o_ref[p]
            z = z_ref[:, cols]
            r = lax.rsqrt(_pair_sum_lanes(o * o, ones) * (1.0 / HEAD_DIM) + NORM_EPS)
            oz_ref[:, cols] = (o * r * w * (z * _sigmoid(z))).astype(oz_ref.dtype)

    tok = pl.BlockSpec((tm, WIDTH), lambda i: (i, 0))
    return pl.pallas_call(
        body, name="gdn_out", grid=(t // tm,),
        in_specs=[pl.BlockSpec((HEADS // 2, tm, 128), lambda i: (0, i, 0)), pl.BlockSpec((1, 128), lambda i: (0, 0)),
                  pl.BlockSpec((tm, WIDTH), lambda i: (i, SEG_ZB // WIDTH))],
        out_specs=tok, out_shape=jax.ShapeDtypeStruct((t, WIDTH), MXU_DTYPE),
        compiler_params=_params(("parallel",)),
    )(o_hm, jnp.tile(gdn_norm_w, (1, 2)), proj)


def _gdn_out_bwd(dproj, d_oz, o_hm, gdn_norm_w, proj, tm=512):
    t = proj.shape[0]

    def body(dp_ref, doz_ref, o_ref, w_ref, z_ref, dz_ref, do_ref, dw_ref):
        w = w_ref[...]
        ones = _pair_ones()
        dw = jnp.zeros((1, 128), F32)
        for p in range(HEADS // 2):
            cols = slice(p * 128, (p + 1) * 128)
            o = o_ref[p]
            z, g = z_ref[:, cols], doz_ref[:, cols]
            sg = _sigmoid(z)
            r = lax.rsqrt(_pair_sum_lanes(o * o, ones) * (1.0 / HEAD_DIM) + NORM_EPS)
            dz_ref[:, cols] = (g * (o * r * w) * (sg * (1.0 + z * (1.0 - sg)))).astype(dz_ref.dtype)
            dn = g * (z * sg)
            dw += jnp.sum(dn * o * r, axis=0, keepdims=True)
            dnw = dn * w
            do_ref[p] = r * dnw - o * (r * r * r) * (_pair_sum_lanes(dnw * o, ones) * (1.0 / HEAD_DIM))

        @pl.when(pl.program_id(0) == 0)
        def _():
            dw_ref[...] = jnp.zeros_like(dw_ref)

        dw_ref[...] += jnp.where(_iota((8, 128), 0) == 0, dw, 0.0)

    tok = pl.BlockSpec((tm, WIDTH), lambda i: (i, 0))
    seg = pl.BlockSpec((tm, WIDTH), lambda i: (i, SEG_ZB // WIDTH))
    hm = pl.BlockSpec((HEADS // 2, tm, 128), lambda i: (0, i, 0))
    dz, do, dw = pl.pallas_call(
        body, name="gdn_out_bwd", grid=(t // tm,),
        in_specs=[pl.BlockSpec(memory_space=pl.ANY), tok, hm, pl.BlockSpec((1, 128), lambda i: (0, 0)), seg],
        out_specs=[seg, hm, pl.BlockSpec((8, 128), lambda i: (0, 0))],
        out_shape=[jax.ShapeDtypeStruct((t, PACKED_WIDTH), MXU_DTYPE), jax.ShapeDtypeStruct((HEADS // 2, t, 128), F32),
                   jax.ShapeDtypeStruct((8, 128), F32)],
        input_output_aliases={0: 0},
        compiler_params=_params(("arbitrary",)),
    )(dproj, d_oz, o_hm, jnp.tile(gdn_norm_w, (1, 2)), proj)
    return dz, do, dw[:, :HEAD_DIM] + dw[:, HEAD_DIM:]


def _merge(y_a, y_b, proj, tm=1024):
    t = proj.shape[0]

    def body(ya_ref, yb_ref, ga_ref, gb_ref, m_ref):
        m_ref[...] = (_sigmoid(ga_ref[...]) * ya_ref[...] + _sigmoid(gb_ref[...]) * yb_ref[...]).astype(m_ref.dtype)

    half = pl.BlockSpec((tm, WIDTH), lambda i, c: (i, c))
    return pl.pallas_call(
        body, name="merge", grid=(t // tm, 2),
        in_specs=[half, half, pl.BlockSpec((tm, WIDTH), lambda i, c: (i, SEG_GA // WIDTH + c)),
                  pl.BlockSpec((tm, WIDTH), lambda i, c: (i, SEG_GB // WIDTH + c))],
        out_specs=half, out_shape=jax.ShapeDtypeStruct((t, D_MODEL), MXU_DTYPE),
        compiler_params=_params(("parallel", "parallel")),
    )(y_a, y_b, proj, proj)


def _merge_bwd(dproj, d_m, y, proj, seg, name, tm=1024):
    t = proj.shape[0]

    def body(*refs):
        dm_ref, y_ref, g_ref, dg_ref, dy_ref = refs[-5:]
        dm = dm_ref[...]
        s = _sigmoid(g_ref[...])
        dy_ref[...] = (dm * s).astype(dy_ref.dtype)
        dg_ref[...] = (dm * y_ref[...] * s * (1.0 - s)).astype(dg_ref.dtype)

    half = pl.BlockSpec((tm, WIDTH), lambda i, c: (i, c))
    gate = pl.BlockSpec((tm, WIDTH), lambda i, c: (i, seg // WIDTH + c))
    specs, args, aliases = [half, half, gate], [d_m, y, proj], {}
    if dproj is not None:
        specs, args, aliases = [pl.BlockSpec(memory_space=pl.ANY)] + specs, [dproj] + args, {0: 0}
    return pl.pallas_call(
        body, name=name, grid=(t // tm, 2), in_specs=specs, out_specs=[gate, half],
        out_shape=[jax.ShapeDtypeStruct((t, PACKED_WIDTH), MXU_DTYPE), jax.ShapeDtypeStruct((t, D_MODEL), MXU_DTYPE)],
        input_output_aliases=aliases,
        compiler_params=_params(("parallel", "parallel")),
    )(*args)


def _tail(x, mo, final_w, target, tm=512):
    t = x.shape[0]

    def body(x_ref, mo_ref, w_ref, t_ref, dxm_ref, dx_ref, loss_ref, dw_ref):
        x2 = x_ref[...] + mo_ref[...]
        w = w_ref[...]
        r = lax.rsqrt(jnp.mean(x2 * x2, axis=-1, keepdims=True) + NORM_EPS)
        xn = x2 * r
        err = xn * w - t_ref[...]
        dy = err * (1.0 / D_MODEL)
        dyw = dy * w
        dx2 = r * dyw - x2 * (r * r * r) * jnp.mean(dyw * x2, axis=-1, keepdims=True)
        dx_ref[...] = dx2
        dxm_ref[...] = dx2.astype(dxm_ref.dtype)
        loss = 0.5 * jnp.sum(jnp.sum(err * err, axis=-1, keepdims=True) * (1.0 / D_MODEL), axis=0, keepdims=True)
        onehot = jnp.where((_iota((8, 128), 0) == 0) & (_iota((8, 128), 1) == 0), 1.0, 0.0)

        @pl.when(pl.program_id(0) == 0)
        def _():
            loss_ref[...] = jnp.zeros_like(loss_ref)
            dw_ref[...] = jnp.zeros_like(dw_ref)

        loss_ref[...] += loss * onehot
        dw_ref[...] += jnp.where(_iota((8, D_MODEL), 0) == 0, jnp.sum(dy * xn, axis=0, keepdims=True), 0.0)

    tok = pl.BlockSpec((tm, D_MODEL), lambda i: (i, 0))
    return pl.pallas_call(
        body, name="tail", grid=(t // tm,),
        in_specs=[tok, tok, pl.BlockSpec((1, D_MODEL), lambda i: (0, 0)), tok],
        out_specs=[tok, tok, pl.BlockSpec((8, 128), lambda i: (0, 0)), pl.BlockSpec((8, D_MODEL), lambda i: (0, 0))],
        out_shape=[jax.ShapeDtypeStruct((t, D_MODEL), MXU_DTYPE), jax.ShapeDtypeStruct((t, D_MODEL), F32),
                   jax.ShapeDtypeStruct((8, 128), F32), jax.ShapeDtypeStruct((8, D_MODEL), F32)],
        compiler_params=_params(("arbitrary",)),
    )(x, mo, final_w, target)


def _norm_bwd(x, norm_w, dh, dx2, tm=512):
    t = x.shape[0]

    def body(x_ref, w_ref, dh_ref, dx2_ref, dx_ref, dw_ref):
        xf, w, dh_ = x_ref[...], w_ref[...], dh_ref[...]
        r = lax.rsqrt(jnp.mean(xf * xf, axis=-1, keepdims=True) + NORM_EPS)
        dhw = dh_ * w
        dx_ref[...] = dx2_ref[...] + r * dhw - xf * (r * r * r) * jnp.mean(dhw * xf, axis=-1, keepdims=True)

        @pl.when(pl.program_id(0) == 0)
        def _():
            dw_ref[...] = jnp.zeros_like(dw_ref)

        dw_ref[...] += jnp.where(_iota((8, D_MODEL), 0) == 0, jnp.sum(dh_ * xf * r, axis=0, keepdims=True), 0.0)

    tok = pl.BlockSpec((tm, D_MODEL), lambda i: (i, 0))
    return pl.pallas_call(
        body, name="norm_bwd", grid=(t // tm,),
        in_specs=[tok, pl.BlockSpec((1, D_MODEL), lambda i: (0, 0)), tok, tok],
        out_specs=[tok, pl.BlockSpec((8, D_MODEL), lambda i: (0, 0))],
        out_shape=[jax.ShapeDtypeStruct((t, D_MODEL), F32), jax.ShapeDtypeStruct((8, D_MODEL), F32)],
        compiler_params=_params(("arbitrary",)),
    )(x, norm_w, dh, dx2)


def _local_step(x, target, norm_w, wp, conv_w, a_log, dt_bias, gdn_norm_w, w_up_a, w_up_b, w_out, final_w, start_reduce):
    t = x.shape[0]
    tables = _rope_tables(t)
    a_log = jnp.pad(a_log, ((0, 0), (HEADS, 128 - 2 * HEADS)))
    dt_bias = jnp.pad(dt_bias, ((0, 0), (HEADS, 128 - 2 * HEADS)))

    proj, h_t = _norm_proj(x, norm_w, wp)
    qkvs = _rope_fwd(proj, tables)
    outs, lses = zip(*[_att_fwd(qkvs[gi], d, f"att_fwd{gi}") for gi, d in enumerate(DILATIONS)])
    oz_a, o_a, *lse_views = _att_merge(outs, lses, proj)
    conv = _conv_fwd(proj, conv_w)
    gq, gk, gv, gb, gg, ggl, grow = _gdn_prep(conv, proj, a_log, dt_bias)
    o_b, states, invs = _gdn_fwd(gq, gk, gv, gb, gg, ggl, grow)
    oz_b = _gdn_out(o_b, gdn_norm_w, proj)
    big = dict(tm=1024, tn=1024, tk=1024)
    y_a = _matmul(oz_a, w_up_a, "nn", "up_a", **big)
    y_b = _matmul(oz_b, w_up_b, "nn", "up_b", **big)
    merged = _merge(y_a, y_b, proj)
    mo = _matmul(merged, w_out, "nn", "out_proj", **big)
    dx2_m, dx2, loss_blk, d_final = _tail(x, mo, final_w, target)

    d_wout = _matmul(merged, dx2_m, "tn", "d_w_out", **big)
    d_m = _matmul(dx2_m, w_out, "nt", "d_merged", **big)
    dproj, dy_a = _merge_bwd(None, d_m, y_a, proj, SEG_GA, "merge_bwd_a")
    dproj, dy_b = _merge_bwd(dproj, d_m, y_b, proj, SEG_GB, "merge_bwd_b")
    d_wua = _matmul(oz_a, dy_a, "tn", "d_w_up_a", **big)
    d_wub = _matmul(oz_b, dy_b, "tn", "d_w_up_b", **big)
    d_oz_a = _matmul(dy_a, w_up_a, "nt", "d_oz_a", **big)
    d_oz_b = _matmul(dy_b, w_up_b, "nt", "d_oz_b", **big)
    dproj, *views = _att_merge_bwd(dproj, d_oz_a, o_a, proj)
    do_views, delta_views = views[:3], views[3:]
    dqkvs = [_att_bwd(qkvs[gi], do_views[gi], lse_views[gi], delta_views[gi], d, f"att_bwd{gi}")
             for gi, d in enumerate(DILATIONS)]
    dproj = _rope_bwd(dproj, dqkvs, tables)
    dproj, do_b, d_gnw = _gdn_out_bwd(dproj, d_oz_b, o_b, gdn_norm_w, proj)
    dgq, dgk, dgv, dgb, dgg, dggl, dgrow = _gdn_bwd(gq, gk, gv, gb, gg, ggl, grow, states, invs, do_b)
    dproj, dconv, d_small = _gdn_prep_bwd(dproj, conv, proj, a_log, dt_bias, dgq, dgk, dgv, dgb, dgg, dggl, dgrow)
    dproj, d_convw = _conv_bwd(dproj, dconv, proj, conv_w)
    d_wp = _matmul(h_t, dproj, "nn", "d_w_in", tm=1024, tn=PACKED_WIDTH // 4, tk=1024)
    in_flight, token = start_reduce(d_wp, d_wua, d_wub, d_wout, d_convw[0:GDN_CONV])
    dh = _matmul(dproj, wp, "nt", "d_h", tm=1024, tn=1024, tk=PACKED_WIDTH // 4, after=token)
    grad_x, d_norm = _norm_bwd(x, norm_w, dh, dx2)
    return dict(loss=loss_blk, grad_x=grad_x, norm_w=d_norm[0:1], in_flight=in_flight,
                a_log=d_small[0:1, HEADS:2 * HEADS], dt_bias=d_small[1:2, HEADS:2 * HEADS], gdn_norm_w=d_gnw[0:1],
                final_norm_w=d_final[0:1])


SHARDS = 4
W_IN_SHARD = IN_WIDTH // SHARDS
ROWS_UP = WIDTH * (D_MODEL // SHARDS) // 128
ROWS_OUT = (D_MODEL // SHARDS) * D_MODEL // 128
CONV_SHARD = 3 * WIDTH // SHARDS
ROWS_CONV = 16
SLAB_ROWS = 2 * ROWS_UP + ROWS_OUT + 2 * ROWS_CONV
HALF_ROWS = SLAB_ROWS // 2
BIG_HALF = (D_MODEL // 2, W_IN_SHARD)
SMALL_HALF = (HALF_ROWS, 128)
MESH = pl.DeviceIdType.MESH
ANY = pl.BlockSpec(memory_space=pl.ANY)


def _pad_rows(a, rows):
    return jnp.pad(a, ((0, rows - a.shape[0]), (0, 0)))


def _pack_slab(w_up_a, w_up_b, w_out, conv, conv_lo):
    parts = [w_up_a.reshape(ROWS_UP, 128), w_up_b.reshape(ROWS_UP, 128), w_out.reshape(ROWS_OUT, 128),
             _pad_rows(conv.reshape(-1, 128), ROWS_CONV), _pad_rows(conv_lo.reshape(-1, 128), ROWS_CONV)]
    return jnp.concatenate(parts, axis=0).reshape(2, *SMALL_HALF)


def _unpack_slab(slab):
    slab = slab.reshape(SLAB_ROWS, 128)
    r0 = 0
    out = []
    for rows, shape in ((ROWS_UP, (WIDTH, D_MODEL // SHARDS)), (ROWS_UP, (WIDTH, D_MODEL // SHARDS)),
                        (ROWS_OUT, (D_MODEL // SHARDS, D_MODEL)), (ROWS_CONV, None), (ROWS_CONV, None)):
        part = slab[r0:r0 + rows]
        out.append(part[:GDN_CONV * CONV_SHARD // 128].reshape(GDN_CONV, CONV_SHARD) if shape is None else part.reshape(shape))
        r0 += rows
    return out


def _mesh_position():
    x, y, c = lax.axis_index("x"), lax.axis_index("y"), lax.axis_index("c")
    return x, y, c, [(1 - x, y), (x, 1 - y), (1 - x, 1 - y)]


def _gather_weights(shards):
    n = len(shards)

    def body(*refs):
        in_refs, out_refs, (send_sems, recv_sems) = refs[:n], refs[n:2 * n], refs[2 * n:]
        x, y, c, chips = _mesh_position()

        def half(a, chip, which):
            return out_refs[a].at[2 * chip[0] + chip[1], which]

        def copy(k, src, dst, to):
            return pltpu.make_async_remote_copy(src_ref=src, dst_ref=dst, send_sem=send_sems.at[k], recv_sem=recv_sems.at[k],
                                                device_id=to, device_id_type=MESH)

        pairs = [(a, j, chip) for a in range(n) for j, chip in enumerate(chips)]
        first = [copy(6 * a + j, in_refs[a].at[c], half(a, (x, y), c), (*chip, c)) for a, j, chip in pairs]
        for cp in first:
            cp.start()
        passed = [copy(6 * a + 3 + j, half(a, chip, c), half(a, chip, c), (x, y, 1 - c)) for a, j, chip in pairs]
        for i, (a, j, chip) in enumerate(pairs):
            copy(6 * a + j, half(a, chip, c), half(a, chip, c), (x, y, c)).wait_recv()
            passed[i].start()
        for a, j, chip in pairs:
            copy(6 * a + 3 + j, half(a, chip, 1 - c), half(a, chip, 1 - c), (x, y, c)).wait_recv()
        for cp in first + passed:
            cp.wait_send()

    return pl.pallas_call(
        body, name="gather_weights", in_specs=[ANY] * n, out_specs=[ANY] * n,
        out_shape=[jax.ShapeDtypeStruct((SHARDS, *s.shape), s.dtype) for s in shards],
        scratch_shapes=[pltpu.SemaphoreType.DMA((6 * n,)), pltpu.SemaphoreType.DMA((6 * n,))],
    )(*shards)


def _exchange_halves(grads):
    n = len(grads)

    def body(*refs):
        g_refs, out_refs, (send_sems, recv_sems) = refs[:n], refs[n:2 * n], refs[2 * n:]
        x, y, c, _ = _mesh_position()
        copies = [pltpu.make_async_remote_copy(src_ref=g_refs[a].at[s, 1 - c], dst_ref=out_refs[a].at[s],
                                               send_sem=send_sems.at[SHARDS * a + s], recv_sem=recv_sems.at[SHARDS * a + s],
                                               device_id=(x, y, 1 - c), device_id_type=MESH)
                  for a in range(n) for s in range(SHARDS)]
        for cp in copies:
            cp.start()
        for cp in copies:
            cp.wait()

    return pl.pallas_call(
        body, name="exchange_halves", in_specs=[ANY] * n, out_specs=[ANY] * n,
        out_shape=[jax.ShapeDtypeStruct((SHARDS, *g.shape[2:]), F32) for g in grads],
        scratch_shapes=[pltpu.SemaphoreType.DMA((SHARDS * n,)), pltpu.SemaphoreType.DMA((SHARDS * n,))],
    )(*grads)


def _pair_sum(grads, recv, blk, name):
    _, _, rows, cols = grads.shape

    def body(c_ref, g_ref, r_ref, o_ref):
        o_ref[...] = (g_ref[0] + r_ref[...]).astype(o_ref.dtype)

    spec = pl.BlockSpec((1, blk, cols), lambda s, i, c_ref: (s, i, 0))
    return pl.pallas_call(
        body, name=name,
        grid_spec=pltpu.PrefetchScalarGridSpec(
            num_scalar_prefetch=1, grid=(SHARDS, rows // blk),
            in_specs=[pl.BlockSpec((1, 1, blk, cols), lambda s, i, c_ref: (s, c_ref[0], i, 0)), spec],
            out_specs=spec),
        out_shape=jax.ShapeDtypeStruct((SHARDS, rows, cols), MXU_DTYPE),
        compiler_params=_params(("parallel", "parallel")),
    )(lax.axis_index("c").astype(jnp.int32).reshape(1), grads, recv)


_HBM = pl.BlockSpec(memory_space=pltpu.HBM)
_SEM = pl.BlockSpec(memory_space=pltpu.SEMAPHORE)
_DATAFLOW = pltpu.SideEffectType.DATAFLOW_SIDE_EFFECTING


def _scatter_copies(p_refs, l_refs, send_sems, recv_sems):
    x, y, c, chips = _mesh_position()
    return [pltpu.make_async_remote_copy(src_ref=p_refs[a].at[2 * chip[0] + chip[1]], dst_ref=l_refs[a].at[j],
                                         send_sem=send_sems.at[3 * a + j], recv_sem=recv_sems.at[3 * a + j],
                                         device_id=(*chip, c), device_id_type=MESH)
            for a in range(len(p_refs)) for j, chip in enumerate(chips)]


def _scatter_start(pairs):
    n = len(pairs)
    lands = [lax.empty((3, *p.shape[1:]), p.dtype) for p in pairs]

    def body(*refs):
        p_refs, l_refs, send_sems, recv_sems, token = refs[:n], refs[n:2 * n], refs[2 * n], refs[2 * n + 1], refs[-1]
        for cp in _scatter_copies(p_refs, l_refs, send_sems, recv_sems):
            cp.start()
        token[...] = jnp.zeros_like(token)

    operands = [pltpu.with_memory_space_constraint(a, pltpu.HBM) for a in (*pairs, *lands)]
    return pl.pallas_call(
        body, name="scatter_start", in_specs=[_HBM] * (2 * n),
        out_shape=(pltpu.SemaphoreType.DMA((3 * n,)), pltpu.SemaphoreType.DMA((3 * n,)),
                   *[pltpu.HBM(a.shape, a.dtype) for a in operands], jax.ShapeDtypeStruct((8, 128), F32)),
        out_specs=(_SEM, _SEM, *[_HBM] * (2 * n), pl.BlockSpec(memory_space=pltpu.VMEM)),
        input_output_aliases={i: 2 + i for i in range(2 * n)},
        compiler_params=pltpu.CompilerParams(has_side_effects=_DATAFLOW),
    )(*operands)


def _scatter_wait(send_sems, recv_sems, passed, after):
    n = len(passed) // 2

    def body(*refs):
        p_refs, l_refs, send_s, recv_s = refs[:n], refs[n:2 * n], refs[2 * n], refs[2 * n + 1]
        for cp in _scatter_copies(p_refs, l_refs, send_s, recv_s):
            cp.wait_send()
            cp.wait_recv()

    return pl.pallas_call(
        body, name="scatter_wait", in_specs=[_HBM] * (2 * n) + [_SEM, _SEM, ANY],
        out_shape=[pltpu.HBM(a.shape, a.dtype) for a in passed], out_specs=[_HBM] * (2 * n),
        input_output_aliases={i: i for i in range(2 * n)},
        compiler_params=pltpu.CompilerParams(has_side_effects=_DATAFLOW),
    )(*passed, send_sems, recv_sems, after)


def _chip_sum(pairs, recv, blk, name):
    _, rows, cols = pairs.shape

    def body(pos_ref, p_ref, r_ref, o_ref):
        o_ref[0] = ((p_ref[0].astype(F32) + r_ref[0].astype(F32)) + r_ref[1].astype(F32)) + r_ref[2].astype(F32)

    pos = jnp.stack([2 * lax.axis_index("x") + lax.axis_index("y"), lax.axis_index("c")]).astype(jnp.int32)
    return pl.pallas_call(
        body, name=name,
        grid_spec=pltpu.PrefetchScalarGridSpec(
            num_scalar_prefetch=1, grid=(rows // blk,),
            in_specs=[pl.BlockSpec((1, blk, cols), lambda i, pos_ref: (pos_ref[0], i, 0)),
                      pl.BlockSpec((3, blk, cols), lambda i, pos_ref: (0, i, 0))],
            out_specs=pl.BlockSpec((1, blk, cols), lambda i, pos_ref: (pos_ref[1], i, 0))),
        out_shape=jax.ShapeDtypeStruct((2, rows, cols), F32),
        compiler_params=_params(("parallel",)),
    )(pos, pairs, recv)


def _share_total(totals):
    n = len(totals)

    def body(*refs):
        t_refs, out_refs, (send_sems, recv_sems) = refs[:n], refs[n:2 * n], refs[2 * n:]
        x, y, c, _ = _mesh_position()
        copies = [pltpu.make_async_remote_copy(src_ref=t_refs[a].at[c], dst_ref=out_refs[a].at[c], send_sem=send_sems.at[a],
                                               recv_sem=recv_sems.at[a], device_id=(x, y, 1 - c), device_id_type=MESH)
                  for a in range(n)]
        for cp in copies:
            cp.start()
        for a in range(n):
            other = out_refs[a].at[1 - c]
            pltpu.make_async_remote_copy(src_ref=other, dst_ref=other, send_sem=send_sems.at[a], recv_sem=recv_sems.at[a],
                                         device_id=(x, y, c), device_id_type=MESH).wait_recv()
        for cp in copies:
            cp.wait_send()

    return pl.pallas_call(
        body, name="share_total", in_specs=[ANY] * n, out_specs=[ANY] * n,
        out_shape=[jax.ShapeDtypeStruct(t.shape, F32) for t in totals],
        scratch_shapes=[pltpu.SemaphoreType.DMA((n,)), pltpu.SemaphoreType.DMA((n,))],
        input_output_aliases={a: a for a in range(n)},
    )(*totals)


def _allreduce_small(block):
    def body(b_ref, out_ref, gath, send_sems, recv_sems):
        x, y, c, _ = _mesh_position()
        me = 4 * x + 2 * y + c
        gath[me] = b_ref[...]
        copies = []
        for k in range(1, 8):
            peer = (x ^ (k >> 2), y ^ ((k >> 1) & 1), c ^ (k & 1))
            copies.append(pltpu.make_async_remote_copy(src_ref=b_ref, dst_ref=gath.at[me], send_sem=send_sems.at[k - 1],
                                                       recv_sem=recv_sems.at[k - 1], device_id=peer, device_id_type=MESH))
        for cp in copies:
            cp.start()
        for k in range(1, 8):
            src = 4 * (x ^ (k >> 2)) + 2 * (y ^ ((k >> 1) & 1)) + (c ^ (k & 1))
            pltpu.make_async_remote_copy(src_ref=b_ref, dst_ref=gath.at[src], send_sem=send_sems.at[k - 1],
                                         recv_sem=recv_sems.at[k - 1], device_id=(x, y, c), device_id_type=MESH).wait_recv()
        for cp in copies:
            cp.wait_send()
        acc = gath[0]
        for d in range(1, 8):
            acc = acc + gath[d]
        out_ref[...] = acc

    vm = pl.BlockSpec(memory_space=pltpu.VMEM)
    return pl.pallas_call(
        body, name="allreduce_small", in_specs=[vm], out_specs=vm,
        out_shape=jax.ShapeDtypeStruct((8, D_MODEL), F32),
        scratch_shapes=[pltpu.VMEM((8, 8, D_MODEL), F32), pltpu.SemaphoreType.DMA((7,)), pltpu.SemaphoreType.DMA((7,))],
    )(block)


def _adamw(w, g, m, v, name):
    rows, cols = w.shape
    tr = 128 if rows % 128 == 0 else rows

    def body(w_ref, g_ref, m_ref, v_ref, d_ref, nm_ref, nv_ref):
        gv = g_ref[...]
        nm = ADAM_B1 * m_ref[...] + (1.0 - ADAM_B1) * gv
        nv = ADAM_B2 * v_ref[...] + (1.0 - ADAM_B2) * (gv * gv)
        m_hat = nm / (1.0 - ADAM_B1 ** ADAM_STEP)
        v_hat = nv / (1.0 - ADAM_B2 ** ADAM_STEP)
        d_ref[...] = -ADAM_LR * (m_hat / (jnp.sqrt(v_hat) + ADAM_EPS) + ADAM_WD * w_ref[...])
        nm_ref[...] = nm
        nv_ref[...] = nv

    spec = pl.BlockSpec((tr, cols), lambda i: (i, 0))
    shape = jax.ShapeDtypeStruct((rows, cols), F32)
    return pl.pallas_call(
        body, name=name, grid=(rows // tr,), in_specs=[spec] * 4, out_specs=[spec] * 3, out_shape=[shape] * 3,
        compiler_params=_params(("parallel",)),
    )(w, g, m, v)


def kernel(x, norm_w, w_in, conv_w, a_log, dt_bias, gdn_norm_w, w_up_a, w_up_b, w_out, final_norm_w, loss_target, m_norm_w, m_w_in, m_conv_w, m_a_log, m_dt_bias, m_gdn_norm_w, m_w_up_a, m_w_up_b, m_w_out, m_final_norm_w, v_norm_w, v_w_in, v_conv_w, v_a_log, v_dt_bias, v_gdn_norm_w, v_w_up_a, v_w_up_b, v_w_out, v_final_norm_w):
    conv_hi = conv_w[0].astype(MXU_DTYPE)
    conv_lo = (conv_w[0] - conv_hi.astype(F32)).astype(MXU_DTYPE)
    big = w_in[0].astype(MXU_DTYPE).reshape(2, *BIG_HALF)
    slab = _pack_slab(w_up_a[0].astype(MXU_DTYPE), w_up_b[0].astype(MXU_DTYPE), w_out[0].astype(MXU_DTYPE), conv_hi, conv_lo)
    own_shard = 2 * lax.axis_index("x") + lax.axis_index("y")
    bigs, slabs = _gather_weights([big, slab])
    bigs = lax.dynamic_update_slice(bigs, big[None], (own_shard, 0, 0, 0)).reshape(SHARDS, D_MODEL, W_IN_SHARD)
    slabs = lax.dynamic_update_slice(slabs, slab[None], (own_shard, 0, 0, 0))
    parts = [_unpack_slab(slabs[s]) for s in range(SHARDS)]
    split = BA_END - (SHARDS - 1) * W_IN_SHARD
    wp = jnp.concatenate([bigs[s] for s in range(SHARDS - 1)]
                         + [bigs[-1][:, :split], jnp.zeros((D_MODEL, SEG_GA - BA_END), MXU_DTYPE), bigs[-1][:, split:]], axis=1)
    w_up_a_full = jnp.concatenate([p[0] for p in parts], axis=1)
    w_up_b_full = jnp.concatenate([p[1] for p in parts], axis=1)
    w_out_full = jnp.concatenate([p[2] for p in parts], axis=0)
    conv_full = jnp.concatenate([p[3].astype(F32) + p[4].astype(F32) for p in parts], axis=1)

    blocks, tags = (128, HALF_ROWS), ("w_in", "slab")

    def start_reduce(d_wp, d_w_up_a, d_w_up_b, d_w_out, d_conv_w):
        d_w_in = [d_wp[:, s * W_IN_SHARD:(s + 1) * W_IN_SHARD] for s in range(SHARDS - 1)]
        d_w_in.append(jnp.concatenate([d_wp[:, (SHARDS - 1) * W_IN_SHARD:BA_END], d_wp[:, SEG_GA:]], axis=1))
        zero_conv = jnp.zeros((GDN_CONV, CONV_SHARD), F32)
        grads = [jnp.stack(d_w_in).reshape(SHARDS, 2, *BIG_HALF),
                 jnp.stack([_pack_slab(d_w_up_a[:, s * 256:(s + 1) * 256], d_w_up_b[:, s * 256:(s + 1) * 256],
                                       d_w_out[s * 256:(s + 1) * 256], d_conv_w[:, s * CONV_SHARD:(s + 1) * CONV_SHARD],
                                       zero_conv) for s in range(SHARDS)])]
        from_sibling = _exchange_halves(grads)
        pairs = [_pair_sum(gr, fs, blk, f"pair_sum_{tag}") for gr, fs, blk, tag in zip(grads, from_sibling, blocks, tags)]
        *in_flight, token = _scatter_start(pairs)
        return in_flight, token

    g = _local_step(x[0], loss_target[0], norm_w, wp, conv_full, a_log, dt_bias, gdn_norm_w,
                    w_up_a_full, w_up_b_full, w_out_full, final_norm_w[None], start_reduce)

    send_sems, recv_sems, *passed = g["in_flight"]
    arrived = _scatter_wait(send_sems, recv_sems, passed, after=g["grad_x"])
    pairs, from_chips = arrived[:2], arrived[2:]
    total_big, total_slab = _share_total([_chip_sum(p, fc, blk, f"chip_sum_{tag}")
                                          for p, fc, blk, tag in zip(pairs, from_chips, blocks, tags)])
    g_w_in = total_big.reshape(D_MODEL, W_IN_SHARD)
    g_w_up_a, g_w_up_b, g_w_out, g_conv, _ = _unpack_slab(total_slab)

    row2 = jnp.concatenate([g["gdn_norm_w"], g["a_log"], g["dt_bias"], g["loss"][0:1, 0:1],
                            jnp.zeros((1, D_MODEL - HEAD_DIM - 2 * HEADS - 1), F32)], axis=1)
    small = _allreduce_small(jnp.concatenate([g["norm_w"], g["final_norm_w"], row2, jnp.zeros((5, D_MODEL), F32)], axis=0))
    g_norm, g_final = small[0:1], small[1]
    g_gnw, g_alog, g_dt = small[2:3, 0:HEAD_DIM], small[2:3, HEAD_DIM:HEAD_DIM + HEADS], small[2:3, HEAD_DIM + HEADS:HEAD_DIM + 2 * HEADS]
    loss = small[2, HEAD_DIM + 2 * HEADS]

    names = ["norm_w", "w_in", "conv_w", "a_log", "dt_bias", "gdn_norm_w", "w_up_a", "w_up_b", "w_out", "final_norm_w"]
    weights = dict(zip(names, (norm_w, w_in, conv_w, a_log, dt_bias, gdn_norm_w, w_up_a, w_up_b, w_out, final_norm_w)))
    ms = dict(zip(names, (m_norm_w, m_w_in, m_conv_w, m_a_log, m_dt_bias, m_gdn_norm_w, m_w_up_a, m_w_up_b, m_w_out, m_final_norm_w)))
    vs = dict(zip(names, (v_norm_w, v_w_in, v_conv_w, v_a_log, v_dt_bias, v_gdn_norm_w, v_w_up_a, v_w_up_b, v_w_out, v_final_norm_w)))
    grads2d = dict(norm_w=g_norm, w_in=g_w_in, conv_w=g_conv, a_log=g_alog, dt_bias=g_dt, gdn_norm_w=g_gnw,
                   w_up_a=g_w_up_a, w_up_b=g_w_up_b, w_out=g_w_out, final_norm_w=g_final[None])
    grad_out, delta, new_m, new_v = [], [], [], []
    for n in names:
        shape = weights[n].shape
        two_d = grads2d[n].shape
        d, nm, nv = _adamw(weights[n].reshape(two_d), grads2d[n], ms[n].reshape(two_d), vs[n].reshape(two_d), f"adamw_{n}")
        grad_out.append(grads2d[n].reshape(shape))
        delta.append(d.reshape(shape))
        new_m.append(nm.reshape(shape))
        new_v.append(nv.reshape(shape))
    return (loss, g["grad_x"][None], *grad_out, *delta, *new_m, *new_v)
```

```python
import functools

import jax
import jax.numpy as jnp
from jax import lax
from jax.experimental import pallas as pl
from jax.experimental.pallas import tpu as pltpu

F32 = jnp.float32
MXU_DTYPE = jnp.bfloat16
HIGHEST = lax.Precision.HIGHEST

D_MODEL = 1024
HEADS = 8
HEAD_DIM = 64
WIDTH = HEADS * HEAD_DIM
NORM_EPS = 1e-6
ROPE_THETA = 10000.0
ATT_BLOCK = 128
DILATIONS = (1, 4, 16)
GDN_CHUNK = 64
GDN_CONV = 4
IN_WIDTH = 9232
SEG_A, SEG_ZA, SEG_B, SEG_ZB, SEG_BA, SEG_GA, SEG_GB, PACKED_WIDTH = 0, 4608, 5120, 6656, 7168, 7680, 8704, 9728
BA_END = 7184
VMEM_LIMIT = 56 * 1024 * 1024

ADAM_LR, ADAM_B1, ADAM_B2, ADAM_EPS, ADAM_WD, ADAM_STEP = 0.001, 0.9, 0.999, 1e-08, 0.01, 10

_NN = (((1,), (0,)), ((), ()))
_NT = (((1,), (1,)), ((), ()))
_TN = (((0,), (0,)), ((), ()))


def _params(sem):
    return pltpu.CompilerParams(dimension_semantics=sem, vmem_limit_bytes=VMEM_LIMIT)


def _mxu(a, b, dims):
    return lax.dot_general(a.astype(MXU_DTYPE), b.astype(MXU_DTYPE), dims, preferred_element_type=F32)


def _sigmoid(x):
    return 1.0 / (1.0 + jnp.exp(-x))


def _softplus(x):
    return jnp.maximum(x, 0.0) + jnp.log(1.0 + jnp.exp(-jnp.abs(x)))


def _iota(shape, axis):
    return lax.broadcasted_iota(jnp.int32, shape, axis)


def _matmul(a, b, mode, name, out_dtype=F32, tm=512, tn=512, tk=512, after=None):
    if mode == "nn":
        (m, k), (k2, n) = a.shape, b.shape
    elif mode == "nt":
        (m, k), (n, k2) = a.shape, b.shape
    else:
        (k, m), (k2, n) = a.shape, b.shape
    assert k == k2
    tm, tn, tk = min(tm, m), min(tn, n), min(tk, k)
    assert m % tm == 0 and n % tn == 0 and k % tk == 0
    nk = k // tk
    dims = {"nn": _NN, "nt": _NT, "tn": _TN}[mode]

    assert out_dtype == F32

    def body(a_ref, b_ref, *rest):
        o_ref = rest[-1]
        kk = pl.program_id(2)
        part = _mxu(a_ref[...], b_ref[...], dims)

        @pl.when(kk == 0)
        def _():
            o_ref[...] = part

        @pl.when(kk > 0)
        def _():
            o_ref[...] += part

    a_spec = pl.BlockSpec((tk, tm), lambda i, j, kk: (kk, i)) if mode == "tn" else pl.BlockSpec((tm, tk), lambda i, j, kk: (i, kk))
    b_spec = pl.BlockSpec((tn, tk), lambda i, j, kk: (j, kk)) if mode == "nt" else pl.BlockSpec((tk, tn), lambda i, j, kk: (kk, j))
    extra_specs, extra_args = ([], []) if after is None else ([pl.BlockSpec(memory_space=pl.ANY)], [after])
    return pl.pallas_call(
        body, name=name, grid=(m // tm, n // tn, nk), in_specs=[a_spec, b_spec] + extra_specs,
        out_specs=pl.BlockSpec((tm, tn), lambda i, j, kk: (i, j)),
        out_shape=jax.ShapeDtypeStruct((m, n), out_dtype),
        compiler_params=_params(("parallel", "parallel", "arbitrary")),
    )(a, b, *extra_args)


def _norm_proj(x, norm_w, wp, tm=1024, tn=PACKED_WIDTH // 4):
    t = x.shape[0]
    tm = min(tm, t)

    def body(x_ref, nw_ref, w_ref, proj_ref, ht_ref, h_scr):
        @pl.when(pl.program_id(1) == 0)
        def _():
            xf = x_ref[...]
            r = lax.rsqrt(jnp.mean(xf * xf, axis=-1, keepdims=True) + NORM_EPS)
            h = xf * r * nw_ref[...]
            h_scr[...] = h.astype(h_scr.dtype)
            ht_ref[...] = h.T.astype(ht_ref.dtype)

        proj_ref[...] = jnp.dot(h_scr[...], w_ref[...], preferred_element_type=F32)

    return pl.pallas_call(
        body, name="norm_proj", grid=(t // tm, PACKED_WIDTH // tn),
        in_specs=[pl.BlockSpec((tm, D_MODEL), lambda i, j: (i, 0)),
                  pl.BlockSpec((1, D_MODEL), lambda i, j: (0, 0)),
                  pl.BlockSpec((D_MODEL, tn), lambda i, j: (0, j))],
        out_specs=[pl.BlockSpec((tm, tn), lambda i, j: (i, j)),
                   pl.BlockSpec((D_MODEL, tm), lambda i, j: (0, i))],
        out_shape=[jax.ShapeDtypeStruct((t, PACKED_WIDTH), F32), jax.ShapeDtypeStruct((D_MODEL, t), MXU_DTYPE)],
        scratch_shapes=[pltpu.VMEM((tm, D_MODEL), MXU_DTYPE)],
        compiler_params=_params(("parallel", "arbitrary")),
    )(x, norm_w, wp)


def _rope_tables(t):
    lane = jnp.arange(128)
    inv_freq = ROPE_THETA ** (-jnp.arange(0, HEAD_DIM, 2, dtype=F32) / HEAD_DIM)
    ang = jnp.arange(t, dtype=F32)[:, None] * inv_freq[None, :]
    ang = jnp.concatenate([ang, ang, ang, ang], axis=-1)
    first_half = (lane % HEAD_DIM) < HEAD_DIM // 2
    cos, sin = jnp.cos(ang), jnp.sin(ang)
    return cos, jnp.where(first_half, -sin, 0.0), jnp.where(first_half, 0.0, sin)


def _rope_cols(x, cos, sin_lo, sin_hi, sign):
    outs = []
    for c in range(x.shape[1] // 128):
        xc = x[:, c * 128:(c + 1) * 128]
        rot = pltpu.roll(xc, 96, 1) * sin_lo + pltpu.roll(xc, 32, 1) * sin_hi
        outs.append(xc * cos + sign * rot)
    return jnp.concatenate(outs, axis=1)


def _rope_block(x, cos, sin_lo, sin_hi, sign):
    return jnp.concatenate([_rope_cols(x[:, :2 * WIDTH], cos, sin_lo, sin_hi, sign), x[:, 2 * WIDTH:]], axis=1)


def _tile_scratch(tm, cols):
    return pltpu.VMEM((cols // 128, tm, 128), F32)


def _store_tile(scr, y):
    for c in range(scr.shape[0]):
        scr[c] = y[:, c * 128:(c + 1) * 128]


def _load_tile(scr):
    return jnp.concatenate([scr[c] for c in range(scr.shape[0])], axis=1)


def _to_strided_view(scr, o_ref, d, width=None, col0=0):
    n, tm, _ = scr.shape
    width = n * 128 if width is None else width
    for r in range(d):
        for c in range(n):
            at = r * width + col0 + c * 128
            o_ref[:, at:at + 128] = scr[c, pl.ds(r, tm // d, stride=d), :].astype(o_ref.dtype)


def _from_strided_view(i_ref, scr, d):
    n, tm, _ = scr.shape
    for r in range(d):
        for c in range(n):
            scr[c, pl.ds(r, tm // d, stride=d), :] = i_ref[:, (r * n + c) * 128:(r * n + c + 1) * 128].astype(F32)


def _strided_spec(tm, d, cols):
    return pl.BlockSpec((tm // d, d * cols), lambda i: (i, 0))


def _rope_fwd(proj, tables, tm=256):
    t = proj.shape[0]
    cols = 3 * WIDTH

    def body(x_ref, c_ref, sl_ref, sh_ref, o0, o1, o2, scr):
        for g, (d, o_ref) in enumerate(zip(DILATIONS, (o0, o1, o2))):
            y = _rope_block(x_ref[:, g * cols:(g + 1) * cols], c_ref[...], sl_ref[...], sh_ref[...], 1.0)
            if d == 1:
                o_ref[...] = y.astype(o_ref.dtype)
            else:
                _store_tile(scr, y)
                _to_strided_view(scr, o_ref, d)

    tab = pl.BlockSpec((tm, 128), lambda i: (i, 0))
    return pl.pallas_call(
        body, name="rope_fwd", grid=(t // tm,),
        in_specs=[pl.BlockSpec((tm, 3 * cols), lambda i: (i, 0)), tab, tab, tab],
        out_specs=[_strided_spec(tm, d, cols) for d in DILATIONS],
        out_shape=[jax.ShapeDtypeStruct((t // d, d * cols), MXU_DTYPE) for d in DILATIONS],
        scratch_shapes=[_tile_scratch(tm, cols)],
        compiler_params=_params(("parallel",)),
    )(proj, *tables)


def _rope_bwd(dproj, dqs, dkvs, tables, tm=256):
    t = dproj.shape[0]
    cols = 3 * WIDTH

    def body(dp_ref, q0, q1, q2, kv0, kv1, kv2, c_ref, sl_ref, sh_ref, o_ref, scr_q, scr_kv):
        for g, (d, q_ref, kv_ref) in enumerate(zip(DILATIONS, (q0, q1, q2), (kv0, kv1, kv2))):
            if d == 1:
                x = jnp.concatenate([q_ref[...], kv_ref[...]], axis=1)
            else:
                _from_strided_view(q_ref, scr_q, d)
                _from_strided_view(kv_ref, scr_kv, d)
                x = jnp.concatenate([_load_tile(scr_q), _load_tile(scr_kv)], axis=1)
            y = _rope_block(x, c_ref[...], sl_ref[...], sh_ref[...], -1.0)
            o_ref[:, g * cols:(g + 1) * cols] = y.astype(o_ref.dtype)

    tab = pl.BlockSpec((tm, 128), lambda i: (i, 0))
    return pl.pallas_call(
        body, name="rope_bwd", grid=(t // tm,),
        in_specs=[pl.BlockSpec(memory_space=pl.ANY)] + [_strided_spec(tm, d, WIDTH) for d in DILATIONS]
        + [_strided_spec(tm, d, 2 * WIDTH) for d in DILATIONS] + [tab, tab, tab],
        out_specs=pl.BlockSpec((tm, 3 * cols), lambda i: (i, 0)),
        out_shape=jax.ShapeDtypeStruct((t, PACKED_WIDTH), MXU_DTYPE),
        scratch_shapes=[_tile_scratch(tm, WIDTH), _tile_scratch(tm, 2 * WIDTH)],
        input_output_aliases={0: 0},
        compiler_params=_params(("parallel",)),
    )(dproj, *dqs, *dkvs, *tables)


def _att_masks():
    qi = _iota((ATT_BLOCK, ATT_BLOCK), 0)
    kj = _iota((ATT_BLOCK, ATT_BLOCK), 1)
    return kj <= qi, kj >= qi


def _att_fwd(qkv, d, name):
    rows = qkv.shape[0]
    nb = rows // ATT_BLOCK
    scale = HEAD_DIM ** -0.5

    def body(q_ref, kc_ref, kp_ref, vc_ref, vp_ref, o_ref, lse_ref):
        has_prev = pl.program_id(1) > 0
        m_cur, m_prev = _att_masks()
        m_prev = m_prev & has_prev
        hs = range(HEADS)
        sls = [slice(h * HEAD_DIM, (h + 1) * HEAD_DIM) for h in hs]
        qs = [q_ref[:, sl] for sl in sls]
        s_c = [jnp.where(m_cur, _mxu(qs[h], kc_ref[:, sls[h]], _NT) * scale, -jnp.inf) for h in hs]
        s_p = [jnp.where(m_prev, _mxu(qs[h], kp_ref[:, sls[h]], _NT) * scale, -jnp.inf) for h in hs]
        m = [jnp.maximum(jnp.max(s_c[h], axis=1, keepdims=True), jnp.max(s_p[h], axis=1, keepdims=True)) for h in hs]
        p_c = [jnp.exp(s_c[h] - m[h]) for h in hs]
        p_p = [jnp.exp(s_p[h] - m[h]) for h in hs]
        den = [jnp.sum(p_c[h], axis=1, keepdims=True) + jnp.sum(p_p[h], axis=1, keepdims=True) for h in hs]
        o = [_mxu(p_c[h], vc_ref[:, sls[h]], _NN) + _mxu(p_p[h], vp_ref[:, sls[h]], _NN) for h in hs]
        for h in hs:
            o_ref[:, sls[h]] = o[h] / den[h]
            lse_ref[:, sls[h]] = jnp.broadcast_to(m[h] + jnp.log(den[h]), (ATT_BLOCK, HEAD_DIM))

    def cur(c):
        return pl.BlockSpec((ATT_BLOCK, WIDTH), lambda r, i: (i, 3 * r + c))

    def prev(c):
        return pl.BlockSpec((ATT_BLOCK, WIDTH), lambda r, i: (jnp.maximum(i - 1, 0), 3 * r + c))

    out = pl.BlockSpec((ATT_BLOCK, WIDTH), lambda r, i: (i, r))
    return pl.pallas_call(
        body, name=name, grid=(d, nb), in_specs=[cur(0), cur(1), prev(1), cur(2), prev(2)],
        out_specs=[out, out],
        out_shape=[jax.ShapeDtypeStruct((rows, d * WIDTH), F32)] * 2,
        compiler_params=_params(("parallel", "arbitrary")),
    )(qkv, qkv, qkv, qkv, qkv)


def _att_bwd(qkv, do, lse, delta, d, name):
    rows = qkv.shape[0]
    nb = rows // ATT_BLOCK
    scale = HEAD_DIM ** -0.5

    def body(q_ref, kc_ref, kp_ref, vc_ref, vp_ref, do_ref, l_ref, dl_ref, dq_ref, dkv_ref, own):
        i = pl.program_id(1)
        m_cur, m_prev = _att_masks()
        m_prev = m_prev & (i > 0)

        hs = range(HEADS)
        sls = [slice(h * HEAD_DIM, (h + 1) * HEAD_DIM) for h in hs]
        col = [slice(h * HEAD_DIM, h * HEAD_DIM + 1) for h in hs]

        def probs(k_r, mask):
            s = [_mxu(q_ref[:, sls[h]], k_r[:, sls[h]], _NT) for h in hs]
            return [jnp.where(mask, jnp.exp(s[h] * scale - l_ref[:, col[h]]), 0.0) for h in hs]

        def dscores(p, v_r):
            dp = [_mxu(do_ref[:, sls[h]], v_r[:, sls[h]], _NT) for h in hs]
            return [(p[h] * (dp[h] - dl_ref[:, col[h]])).astype(MXU_DTYPE) for h in hs]

        @pl.when(i == 0)
        def _():
            own[...] = jnp.zeros_like(own)

        @pl.when(i < nb)
        def _():
            p_c = probs(kc_ref, m_cur)
            ds_c = dscores(p_c, vc_ref)
            p_p = probs(kp_ref, m_prev)
            ds_p = dscores(p_p, vp_ref)
            dq = [_mxu(ds_c[h], kc_ref[:, sls[h]], _NN) + _mxu(ds_p[h], kp_ref[:, sls[h]], _NN) for h in hs]
            dk_p = [_mxu(ds_p[h], q_ref[:, sls[h]], _TN) for h in hs]
            dv_p = [_mxu(p_p[h], do_ref[:, sls[h]], _TN) for h in hs]
            dk_c = [_mxu(ds_c[h], q_ref[:, sls[h]], _TN) for h in hs]
            dv_c = [_mxu(p_c[h], do_ref[:, sls[h]], _TN) for h in hs]
            for h in hs:
                vs = slice(WIDTH + h * HEAD_DIM, WIDTH + (h + 1) * HEAD_DIM)
                dq_ref[:, sls[h]] = dq[h] * scale
                dkv_ref[:, sls[h]] = own[:, sls[h]] + dk_p[h] * scale
                dkv_ref[:, vs] = own[:, vs] + dv_p[h]
                own[:, sls[h]] = dk_c[h] * scale
                own[:, vs] = dv_c[h]

        @pl.when(i == nb)
        def _():
            dkv_ref[...] = own[...]

    def qkv_spec(c, shift):
        return pl.BlockSpec((ATT_BLOCK, WIDTH), lambda r, i: (jnp.clip(i + shift, 0, nb - 1), 3 * r + c))

    tok = pl.BlockSpec((ATT_BLOCK, WIDTH), lambda r, i: (jnp.minimum(i, nb - 1), r))
    return pl.pallas_call(
        body, name=name, grid=(d, nb + 1),
        in_specs=[qkv_spec(0, 0), qkv_spec(1, 0), qkv_spec(1, -1), qkv_spec(2, 0), qkv_spec(2, -1), tok, tok, tok],
        out_specs=[tok, pl.BlockSpec((ATT_BLOCK, 2 * WIDTH), lambda r, i: (jnp.maximum(i - 1, 0), r))],
        out_shape=[jax.ShapeDtypeStruct((rows, d * WIDTH), F32), jax.ShapeDtypeStruct((rows, d * 2 * WIDTH), F32)],
        scratch_shapes=[pltpu.VMEM((ATT_BLOCK, 2 * WIDTH), F32)],
        compiler_params=_params(("parallel", "arbitrary")),
    )(qkv, qkv, qkv, qkv, qkv, do, lse, delta)


def _att_merge(os_, lses, proj, tm=256):
    t = proj.shape[0]

    def body(o0, o1, o2, l0, l1, l2, z_ref, oz_ref, o_ref, t0, t1, t2, s_o1, s_l1, s_o2, s_l2, s_t):
        _from_strided_view(o1, s_o1, DILATIONS[1])
        _from_strided_view(l1, s_l1, DILATIONS[1])
        _from_strided_view(o2, s_o2, DILATIONS[2])
        _from_strided_view(l2, s_l2, DILATIONS[2])
        a, b, c = l0[...], _load_tile(s_l1), _load_tile(s_l2)
        m = jnp.maximum(jnp.maximum(a, b), c)
        wa, wb, wc = jnp.exp(a - m), jnp.exp(b - m), jnp.exp(c - m)
        den = wa + wb + wc
        o = (wa * o0[...] + wb * _load_tile(s_o1) + wc * _load_tile(s_o2)) / den
        z = z_ref[...]
        o_ref[...] = o
        oz_ref[...] = (o * z * _sigmoid(z)).astype(oz_ref.dtype)
        total = m + jnp.log(den)
        t0[...] = total
        _store_tile(s_t, total)
        _to_strided_view(s_t, t1, DILATIONS[1])
        _to_strided_view(s_t, t2, DILATIONS[2])

    tok = pl.BlockSpec((tm, WIDTH), lambda i: (i, 0))
    views = [_strided_spec(tm, d, WIDTH) for d in DILATIONS]
    view_shapes = [jax.ShapeDtypeStruct((t // d, d * WIDTH), F32) for d in DILATIONS]
    return pl.pallas_call(
        body, name="att_merge", grid=(t // tm,),
        in_specs=views + views + [pl.BlockSpec((tm, WIDTH), lambda i: (i, SEG_ZA // WIDTH))],
        out_specs=[tok, tok] + views,
        out_shape=[jax.ShapeDtypeStruct((t, WIDTH), MXU_DTYPE), jax.ShapeDtypeStruct((t, WIDTH), F32)] + view_shapes,
        scratch_shapes=[_tile_scratch(tm, WIDTH)] * 5,
        compiler_params=_params(("parallel",)),
    )(*os_, *lses, proj)


def _att_merge_bwd(dproj, d_oz, o, proj, tm=256):
    t = proj.shape[0]

    def body(dp_ref, doz_ref, o_ref, z_ref, dz_ref, do0, do1, do2, dl0, dl1, dl2, s_do, s_dl):
        z, ov, g = z_ref[...], o_ref[...], doz_ref[...]
        sg = _sigmoid(z)
        do = g * z * sg
        dz_ref[...] = (g * ov * sg * (1.0 + z * (1.0 - sg))).astype(dz_ref.dtype)
        do0[...] = do.astype(do0.dtype)
        _store_tile(s_do, do)
        prod = do * ov
        for p in range(HEADS // 2):
            s_dl[p] = _pair_sum_lanes(prod[:, p * 128:(p + 1) * 128], _pair_ones())
        dl0[...] = _load_tile(s_dl)
        for d, do_v, dl_v in ((DILATIONS[1], do1, dl1), (DILATIONS[2], do2, dl2)):
            _to_strided_view(s_do, do_v, d)
            _to_strided_view(s_dl, dl_v, d)

    tok = pl.BlockSpec((tm, WIDTH), lambda i: (i, 0))
    seg = pl.BlockSpec((tm, WIDTH), lambda i: (i, SEG_ZA // WIDTH))
    views = [_strided_spec(tm, d, WIDTH) for d in DILATIONS]
    return pl.pallas_call(
        body, name="att_merge_bwd", grid=(t // tm,),
        in_specs=[pl.BlockSpec(memory_space=pl.ANY), tok, tok, seg],
        out_specs=[seg] + views + views,
        out_shape=[jax.ShapeDtypeStruct((t, PACKED_WIDTH), MXU_DTYPE)]
        + [jax.ShapeDtypeStruct((t // d, d * WIDTH), MXU_DTYPE) for d in DILATIONS]
        + [jax.ShapeDtypeStruct((t // d, d * WIDTH), F32) for d in DILATIONS],
        scratch_shapes=[_tile_scratch(tm, WIDTH)] * 2,
        input_output_aliases={0: 0},
        compiler_params=_params(("parallel",)),
    )(dproj, d_oz, o, proj)


def _shift_down(x, halo, s):
    if s == 0:
        return x
    xs = pltpu.roll(x, s, 0)
    head = jnp.where(_iota((8, x.shape[1]), 0) < s, pltpu.roll(halo, s, 0), xs[0:8])
    return jnp.concatenate([head, xs[8:]], axis=0)


def _shift_up(x, nxt, s):
    if s == 0:
        return x
    n = x.shape[0]
    xs = pltpu.roll(x, n - s, 0)
    tail = jnp.where(_iota((8, x.shape[1]), 0) >= 8 - s, pltpu.roll(nxt, 8 - s, 0), xs[n - 8:])
    return jnp.concatenate([xs[:n - 8], tail], axis=0)


def _conv_fwd(proj, conv_w, tm=1024):
    t = proj.shape[0]
    cb = SEG_B // WIDTH

    def body(x_ref, halo_ref, w_ref, c_ref):
        halo = jnp.where(pl.program_id(0) > 0, halo_ref[...], 0.0)
        x = x_ref[...]
        w = w_ref[...]
        acc = jnp.zeros((tm, WIDTH), F32)
        for j in range(GDN_CONV):
            acc += _shift_down(x, halo, GDN_CONV - 1 - j) * w[j:j + 1, :]
        c_ref[...] = acc

    return pl.pallas_call(
        body, name="conv_fwd", grid=(t // tm, 3),
        in_specs=[pl.BlockSpec((tm, WIDTH), lambda i, c: (i, cb + c)),
                  pl.BlockSpec((8, WIDTH), lambda i, c: (jnp.maximum(i * (tm // 8) - 1, 0), cb + c)),
                  pl.BlockSpec((GDN_CONV, WIDTH), lambda i, c: (0, c))],
        out_specs=pl.BlockSpec((tm, WIDTH), lambda i, c: (i, c)),
        out_shape=jax.ShapeDtypeStruct((t, 3 * WIDTH), F32),
        compiler_params=_params(("parallel", "parallel")),
    )(proj, proj, conv_w)


def _conv_bwd(dproj, dc, proj, conv_w, tm=1024):
    t = proj.shape[0]
    cb = SEG_B // WIDTH
    nt = t // tm

    def body(dp_ref, dc_ref, dcn_ref, x_ref, halo_ref, w_ref, dx_ref, dw_ref):
        i = pl.program_id(1)
        w = w_ref[...]
        dcn = jnp.where(i < nt - 1, dcn_ref[...], 0.0)
        dcv = dc_ref[...]
        acc = jnp.zeros((tm, WIDTH), F32)
        for j in range(GDN_CONV):
            acc += _shift_up(dcv, dcn, GDN_CONV - 1 - j) * w[j:j + 1, :]
        dx_ref[...] = acc.astype(dx_ref.dtype)
        halo = jnp.where(i > 0, halo_ref[...], 0.0)
        x = x_ref[...]
        row8 = _iota((8, WIDTH), 0)
        part = jnp.zeros((8, WIDTH), F32)
        for j in range(GDN_CONV):
            s = jnp.sum(dcv * _shift_down(x, halo, GDN_CONV - 1 - j), axis=0, keepdims=True)
            part += jnp.where(row8 == j, s, 0.0)

        @pl.when(i == 0)
        def _():
            dw_ref[...] = jnp.zeros_like(dw_ref)

        dw_ref[...] += part

    return pl.pallas_call(
        body, name="conv_bwd", grid=(3, nt),
        in_specs=[pl.BlockSpec(memory_space=pl.ANY),
                  pl.BlockSpec((tm, WIDTH), lambda c, i: (i, c)),
                  pl.BlockSpec((8, WIDTH), lambda c, i: (jnp.minimum((i + 1) * (tm // 8), t // 8 - 1), c)),
                  pl.BlockSpec((tm, WIDTH), lambda c, i: (i, cb + c)),
                  pl.BlockSpec((8, WIDTH), lambda c, i: (jnp.maximum(i * (tm // 8) - 1, 0), cb + c)),
                  pl.BlockSpec((GDN_CONV, WIDTH), lambda c, i: (0, c))],
        out_specs=[pl.BlockSpec((tm, WIDTH), lambda c, i: (i, cb + c)),
                   pl.BlockSpec((8, WIDTH), lambda c, i: (0, c))],
        out_shape=[jax.ShapeDtypeStruct((t, PACKED_WIDTH), MXU_DTYPE), jax.ShapeDtypeStruct((8, 3 * WIDTH), F32)],
        input_output_aliases={0: 0},
        compiler_params=_params(("parallel", "arbitrary")),
    )(dproj, dc, dc, proj, proj, conv_w)


def _chunk_matrices(tm):
    r, c = _iota((tm, tm), 0), _iota((tm, tm), 1)
    same = (r // GDN_CHUNK) == (c // GDN_CHUNK)
    return jnp.where(same & (c <= r), 1.0, 0.0), jnp.where(same, 1.0, 0.0)


def _gdn_gates(ba, a_log, dt_bias):
    al = ba + dt_bias
    return _sigmoid(ba), -jnp.exp(a_log) * _softplus(al), _sigmoid(al)


SCAN_HEADS = (0, 2, 4, 6, 1, 3, 5, 7)


def _head_lane_eye():
    return jnp.where(_iota((HEADS, 128), 1) == _iota((HEADS, 128), 0) + HEADS, 1.0, 0.0)


def _pair_ones():
    return jnp.where(_iota((128, 128), 0) // HEAD_DIM == _iota((128, 128), 1) // HEAD_DIM, 1.0, 0.0).astype(MXU_DTYPE)


def _pair_to_lanes(p, base):
    return jnp.where(_iota((128, 128), 1) == base + 2 * p + _iota((128, 128), 0) // HEAD_DIM, 1.0, 0.0).astype(MXU_DTYPE)


def _pair_sum_lanes(x, sel):
    hi, lo = _split(x)
    return jnp.dot(hi, sel, preferred_element_type=F32) + jnp.dot(lo, sel, preferred_element_type=F32)


def _gdn_prep(conv, proj, a_log, dt_bias, tm=256):
    t = proj.shape[0]
    nc = tm // GDN_CHUNK

    def body(c_ref, ba_ref, al_ref, dt_ref, q_ref, k_ref, v_ref, b_ref, g_ref, gl_ref, grow_ref):
        beta, g, _ = _gdn_gates(ba_ref[:, 0:128], al_ref[...], dt_ref[...])
        lmat, cmat = _chunk_matrices(tm)
        gc = jnp.dot(lmat, g, precision=HIGHEST, preferred_element_type=F32)
        gl = jnp.dot(cmat, g, precision=HIGHEST, preferred_element_type=F32)
        grow = lax.dot_general(_head_lane_eye(), gc, _NT, precision=HIGHEST, preferred_element_type=F32)
        ones = _pair_ones()
        first = _iota((1, 128), 1) < HEAD_DIM

        def spread(x, base, p):
            return jnp.where(first, x[:, base + 2 * p:base + 2 * p + 1], x[:, base + 2 * p + 1:base + 2 * p + 2])

        for p in range(HEADS // 2):
            for seg, ref, scale in ((0, q_ref, HEAD_DIM ** -0.5), (1, k_ref, 1.0), (2, v_ref, None)):
                c = c_ref[:, seg * WIDTH + p * 128:seg * WIDTH + (p + 1) * 128]
                a = c * _sigmoid(c)
                if scale is not None:
                    a = a * (lax.rsqrt(_pair_sum_lanes(a * a, ones) + NORM_EPS) * scale)
                ref[p] = a
            b_ref[p] = spread(beta, 0, p)
            g_ref[p] = spread(gc, HEADS, p)
            gl_ref[p] = spread(gl, HEADS, p)
        for pos, h in enumerate(SCAN_HEADS):
            for cc in range(nc):
                grow_ref[pos, cc] = grow[h:h + 1, cc * GDN_CHUNK:(cc + 1) * GDN_CHUNK]

    hm = pl.BlockSpec((HEADS // 2, tm, 128), lambda i: (0, i, 0))
    small = pl.BlockSpec((1, 128), lambda i: (0, 0))
    hm_shape = jax.ShapeDtypeStruct((HEADS // 2, t, 128), F32)
    return pl.pallas_call(
        body, name="gdn_prep", grid=(t // tm,),
        in_specs=[pl.BlockSpec((tm, 3 * WIDTH), lambda i: (i, 0)),
                  pl.BlockSpec((tm, WIDTH), lambda i: (i, SEG_BA // WIDTH)), small, small],
        out_specs=[hm] * 6 + [pl.BlockSpec((HEADS, nc, 1, GDN_CHUNK), lambda i: (0, i, 0, 0))],
        out_shape=[hm_shape] * 6 + [jax.ShapeDtypeStruct((HEADS, t // GDN_CHUNK, 1, GDN_CHUNK), F32)],
        compiler_params=_params(("parallel",)),
    )(conv, proj, a_log, dt_bias)


def _gdn_prep_bwd(dproj, conv, proj, a_log, dt_bias, dq, dk, dv, db, dg, dgl, dgrow, tm=256):
    t = proj.shape[0]
    nc = tm // GDN_CHUNK

    def body(dp_ref, c_ref, ba_ref, al_ref, dt_ref, dq_ref, dk_ref, dv_ref, db_ref, dg_ref, dgl_ref, dgrow_ref,
             dba_ref, dc_ref, small_ref, row_scr):
        beta, g, sig_al = _gdn_gates(ba_ref[:, 0:128], al_ref[...], dt_ref[...])
        d_beta = jnp.zeros((tm, 128), F32)
        d_gc = jnp.zeros((tm, 128), F32)
        d_gl = jnp.zeros((tm, 128), F32)
        ones = _pair_ones()
        for pos, h in enumerate(SCAN_HEADS):
            for cc in range(nc):
                row_scr[h:h + 1, cc * GDN_CHUNK:(cc + 1) * GDN_CHUNK] = dgrow_ref[pos, cc]
        for p in range(HEADS // 2):
            d_beta += _pair_sum_lanes(db_ref[p], _pair_to_lanes(p, 0))
            d_gc += _pair_sum_lanes(dg_ref[p], _pair_to_lanes(p, HEADS))
            d_gl += _pair_sum_lanes(dgl_ref[p], _pair_to_lanes(p, HEADS))
            for seg, ref, scale in ((0, dq_ref, HEAD_DIM ** -0.5), (1, dk_ref, 1.0), (2, dv_ref, None)):
                cols = slice(seg * WIDTH + p * 128, seg * WIDTH + (p + 1) * 128)
                c = c_ref[:, cols]
                sg = _sigmoid(c)
                da = ref[p]
                if scale is not None:
                    a = c * sg
                    r = lax.rsqrt(_pair_sum_lanes(a * a, ones) + NORM_EPS)
                    da = da * scale
                    da = r * da - a * (r * r * r) * _pair_sum_lanes(da * a, ones)
                dc_ref[:, cols] = da * (sg * (1.0 + c * (1.0 - sg)))
        d_gc += lax.dot_general(row_scr[...], _head_lane_eye(), _TN, precision=HIGHEST, preferred_element_type=F32)
        lmat, cmat = _chunk_matrices(tm)
        d_g = (lax.dot_general(lmat, d_gc, _TN, precision=HIGHEST, preferred_element_type=F32)
               + lax.dot_general(cmat, d_gl, _TN, precision=HIGHEST, preferred_element_type=F32))
        d_al = d_g * (-jnp.exp(al_ref[...])) * sig_al
        d_bl = d_beta * beta * (1.0 - beta)
        dba_ref[...] = jnp.concatenate([d_bl + d_al, jnp.zeros((tm, WIDTH - 128), F32)], axis=1).astype(dba_ref.dtype)
        row8 = _iota((8, 128), 0)
        part = (jnp.where(row8 == 0, jnp.sum(d_g * g, axis=0, keepdims=True), 0.0)
                + jnp.where(row8 == 1, jnp.sum(d_al, axis=0, keepdims=True), 0.0))

        @pl.when(pl.program_id(0) == 0)
        def _():
            small_ref[...] = jnp.zeros_like(small_ref)

        small_ref[...] += part

    hm = pl.BlockSpec((HEADS // 2, tm, 128), lambda i: (0, i, 0))
    small = pl.BlockSpec((1, 128), lambda i: (0, 0))
    seg = pl.BlockSpec((tm, WIDTH), lambda i: (i, SEG_BA // WIDTH))
    return pl.pallas_call(
        body, name="gdn_prep_bwd", grid=(t // tm,),
        in_specs=[pl.BlockSpec(memory_space=pl.ANY), pl.BlockSpec((tm, 3 * WIDTH), lambda i: (i, 0)), seg, small, small]
        + [hm] * 6 + [pl.BlockSpec((HEADS, nc, 1, GDN_CHUNK), lambda i: (0, i, 0, 0))],
        out_specs=[seg, pl.BlockSpec((tm, 3 * WIDTH), lambda i: (i, 0)), pl.BlockSpec((8, 128), lambda i: (0, 0))],
        out_shape=[jax.ShapeDtypeStruct((t, PACKED_WIDTH), MXU_DTYPE), jax.ShapeDtypeStruct((t, 3 * WIDTH), F32),
                   jax.ShapeDtypeStruct((8, 128), F32)],
        scratch_shapes=[pltpu.VMEM((HEADS, tm), F32)],
        input_output_aliases={0: 0},
        compiler_params=_params(("arbitrary",)),
    )(dproj, conv, proj, a_log, dt_bias, dq, dk, dv, db, dg, dgl, dgrow)


_BNN = (((2,), (1,)), ((0,), (0,)))
_BNT = (((2,), (2,)), ((0,), (0,)))
_BTN = (((1,), (1,)), ((0,), (0,)))


@jax.custom_vjp
def _MM_NN(a, b):
    return _mxu(a, b, _BNN)


@jax.custom_vjp
def _MM_NT(a, b):
    return _mxu(a, b, _BNT)


@jax.custom_vjp
def _MM_TN(a, b):
    return _mxu(a, b, _BTN)


_MM_NN.defvjp(lambda a, b: (_mxu(a, b, _BNN), (a, b)), lambda r, g: (_mxu(g, r[1], _BNT), _mxu(r[0], g, _BTN)))
_MM_NT.defvjp(lambda a, b: (_mxu(a, b, _BNT), (a, b)), lambda r, g: (_mxu(g, r[1], _BNN), _mxu(g, r[0], _BTN)))
_MM_TN.defvjp(lambda a, b: (_mxu(a, b, _BTN), (a, b)), lambda r, g: (_mxu(r[1], g, _BNT), _mxu(r[0], g, _BNN)))


def _split(a):
    hi = a.astype(MXU_DTYPE)
    return hi, (a - hi.astype(F32)).astype(MXU_DTYPE)


def _dot3(a, b, dims):
    (ah, al), (bh, bl) = a, b
    (ca,), (cb,) = dims[0]
    return lax.dot_general(jnp.concatenate([ah, ah, al], axis=ca), jnp.concatenate([bh, bl, bh], axis=cb), dims,
                           preferred_element_type=F32)


def _unit_lower_inverse(a):
    c = GDN_CHUNK
    eye = jnp.where(_iota((c, c), 0) == _iota((c, c), 1), 1.0, 0.0)
    x = eye - a
    p = a
    for _ in range(5):
        ps = _split(p)
        p = _dot3(ps, ps, _BNN)
        x = x + _dot3(_split(x), _split(p), _BNN)
    return x


@jax.custom_vjp
def _SAVED_INVERSE(a, t_inv):
    return t_inv


def _saved_inverse_bwd(t_inv, g):
    ts = _split(t_inv)
    return -_dot3(ts, _split(_dot3(_split(g), ts, _BNT)), _BTN), jnp.zeros_like(t_inv)


_SAVED_INVERSE.defvjp(lambda a, t_inv: (t_inv, t_inv), _saved_inverse_bwd)


def _gdn_chunk(q, k, v, beta, g1, g2, gl, state, t_inv=None):
    c = GDN_CHUNK
    if t_inv is None:
        _mm_nn, _mm_nt, _mm_tn = (functools.partial(_mxu, dims=dd) for dd in (_BNN, _BNT, _BTN))
    else:
        _mm_nn, _mm_nt, _mm_tn = _MM_NN, _MM_NT, _MM_TN
    row, col = _iota((c, c), 0), _iota((c, c), 1)
    incl, strict = row >= col, row > col
    decay = jnp.where(incl, jnp.exp(jnp.where(incl, g1 - g2, 0.0)), 0.0)
    eg = jnp.exp(g1)
    kb = k * beta
    a = _mm_nt(kb, k) * jnp.where(strict, decay, 0.0)
    inv = _unit_lower_inverse(a) if t_inv is None else _SAVED_INVERSE(a, t_inv)
    u = _mm_nn(inv, v * beta)
    w = _mm_nn(inv, kb * eg)
    attn = _mm_nt(q, k) * decay
    v_new = u - _mm_nn(w, state)
    o = _mm_nn(q * eg, state) + _mm_nn(attn, v_new)
    new_state = state * jnp.exp(gl) + _mm_tn(k * jnp.exp(gl - g1), v_new)
    return (o, new_state, inv) if t_inv is None else (o, new_state)


def _unpair(x):
    return jnp.concatenate([x[..., :HEAD_DIM], x[..., HEAD_DIM:]], axis=0)


def _gdn_fwd(q, k, v, beta, g, gl, grow, cpb=4):
    t = q.shape[1]
    rows = cpb * GDN_CHUNK
    lo, hi = slice(0, HEAD_DIM), slice(HEAD_DIM, 2 * HEAD_DIM)

    def body(q_ref, k_ref, v_ref, b_ref, g_ref, gl_ref, grow_ref, o_ref, st_ref, inv_ref, state):
        @pl.when(pl.program_id(0) == 0)
        def _():
            state[...] = jnp.zeros_like(state)

        s = state[...]
        for cc in range(cpb):
            sl = slice(cc * GDN_CHUNK, (cc + 1) * GDN_CHUNK)
            st_ref[:, cc, :, lo], st_ref[:, cc, :, hi] = s[:HEADS // 2], s[HEADS // 2:]
            g2 = jnp.broadcast_to(grow_ref[:, cc], (HEADS, GDN_CHUNK, GDN_CHUNK))
            o, s, inv = _gdn_chunk(*[_unpair(r[:, sl, :]) for r in (q_ref, k_ref, v_ref, b_ref, g_ref)], g2,
                                   _unpair(gl_ref[:, sl, :]), s)
            o_ref[:, sl, lo], o_ref[:, sl, hi] = o[:HEADS // 2], o[HEADS // 2:]
            inv_ref[:, cc, :, lo], inv_ref[:, cc, :, hi] = inv[:HEADS // 2], inv[HEADS // 2:]
        state[...] = s

    hm = pl.BlockSpec((HEADS // 2, rows, 128), lambda i: (0, i, 0))
    per_chunk = pl.BlockSpec((HEADS // 2, cpb, GDN_CHUNK, 128), lambda i: (0, i, 0, 0))
    chunk_shape = jax.ShapeDtypeStruct((HEADS // 2, t // GDN_CHUNK, GDN_CHUNK, 128), F32)
    return pl.pallas_call(
        body, name="gdn_fwd", grid=(t // rows,),
        in_specs=[hm] * 6 + [pl.BlockSpec((HEADS, cpb, 1, GDN_CHUNK), lambda i: (0, i, 0, 0))],
        out_specs=[hm, per_chunk, per_chunk],
        out_shape=[jax.ShapeDtypeStruct((HEADS // 2, t, 128), F32), chunk_shape, chunk_shape],
        scratch_shapes=[pltpu.VMEM((HEADS, GDN_CHUNK, HEAD_DIM), F32)],
        compiler_params=_params(("arbitrary",)),
    )(q, k, v, beta, g, gl, grow)


def _gdn_bwd(q, k, v, beta, g, gl, grow, states, invs, do, cpb=2):
    t = q.shape[1]
    rows = cpb * GDN_CHUNK
    nsteps = t // rows
    lo, hi = slice(0, HEAD_DIM), slice(HEAD_DIM, 2 * HEAD_DIM)

    def body(q_ref, k_ref, v_ref, b_ref, g_ref, gl_ref, grow_ref, st_ref, inv_ref, do_ref,
             dq_ref, dk_ref, dv_ref, db_ref, dg_ref, dgl_ref, dgrow_ref, dstate):
        @pl.when(pl.program_id(0) == 0)
        def _():
            dstate[...] = jnp.zeros_like(dstate)

        ds = dstate[...]
        for cc in reversed(range(cpb)):
            sl = slice(cc * GDN_CHUNK, (cc + 1) * GDN_CHUNK)
            g2 = jnp.broadcast_to(grow_ref[:, cc], (HEADS, GDN_CHUNK, GDN_CHUNK))
            _, vjp = jax.vjp(_gdn_chunk, *[_unpair(r[:, sl, :]) for r in (q_ref, k_ref, v_ref, b_ref, g_ref)], g2,
                             _unpair(gl_ref[:, sl, :]), _unpair(st_ref[:, cc]), _unpair(inv_ref[:, cc]))
            gq, gk, gv, gb, gg1, gg2, ggl, ds, _ = vjp((_unpair(do_ref[:, sl, :]), ds))
            for ref, val in ((dq_ref, gq), (dk_ref, gk), (dv_ref, gv), (db_ref, gb), (dg_ref, gg1), (dgl_ref, ggl)):
                ref[:, sl, lo], ref[:, sl, hi] = val[:HEADS // 2], val[HEADS // 2:]
            dgrow_ref[:, cc] = jnp.sum(gg2, axis=1, keepdims=True)
        dstate[...] = ds

    hm = pl.BlockSpec((HEADS // 2, rows, 128), lambda i: (0, nsteps - 1 - i, 0))
    rowspec = pl.BlockSpec((HEADS, cpb, 1, GDN_CHUNK), lambda i: (0, nsteps - 1 - i, 0, 0))
    per_chunk = pl.BlockSpec((HEADS // 2, cpb, GDN_CHUNK, 128), lambda i: (0, nsteps - 1 - i, 0, 0))
    hm_shape = jax.ShapeDtypeStruct((HEADS // 2, t, 128), F32)
    return pl.pallas_call(
        body, name="gdn_bwd", grid=(nsteps,),
        in_specs=[hm] * 6 + [rowspec, per_chunk, per_chunk, hm],
        out_specs=[hm] * 6 + [rowspec],
        out_shape=[hm_shape] * 6 + [jax.ShapeDtypeStruct((HEADS, t // GDN_CHUNK, 1, GDN_CHUNK), F32)],
        scratch_shapes=[pltpu.VMEM((HEADS, GDN_CHUNK, HEAD_DIM), F32)],
        compiler_params=_params(("arbitrary",)),
    )(q, k, v, beta, g, gl, grow, states, invs, do)


def _gdn_out(o_hm, gdn_norm_w, proj, tm=512):
    t = proj.shape[0]

    def body(o_ref, w_ref, z_ref, oz_ref):
        w = w_ref[...]
        ones = _pair_ones()
        for p in range(HEADS // 2):
            cols = slice(p * 128, (p + 1) * 128)
            o = o_ref[p]
            z = z_ref[:, cols]
            r = lax.rsqrt(_pair_sum_lanes(o * o, ones) * (1.0 / HEAD_DIM) + NORM_EPS)
            oz_ref[:, cols] = (o * r * w * (z * _sigmoid(z))).astype(oz_ref.dtype)

    tok = pl.BlockSpec((tm, WIDTH), lambda i: (i, 0))
    return pl.pallas_call(
        body, name="gdn_out", grid=(t // tm,),
        in_specs=[pl.BlockSpec((HEADS // 2, tm, 128), lambda i: (0, i, 0)), pl.BlockSpec((1, 128), lambda i: (0, 0)),
                  pl.BlockSpec((tm, WIDTH), lambda i: (i, SEG_ZB // WIDTH))],
        out_specs=tok, out_shape=jax.ShapeDtypeStruct((t, WIDTH), MXU_DTYPE),
        compiler_params=_params(("parallel",)),
    )(o_hm, jnp.tile(gdn_norm_w, (1, 2)), proj)


def _gdn_out_bwd(dproj, d_oz, o_hm, gdn_norm_w, proj, tm=512):
    t = proj.shape[0]

    def body(dp_ref, doz_ref, o_ref, w_ref, z_ref, dz_ref, do_ref, dw_ref):
        w = w_ref[...]
        ones = _pair_ones()
        dw = jnp.zeros((1, 128), F32)
        for p in range(HEADS // 2):
            cols = slice(p * 128, (p + 1) * 128)
            o = o_ref[p]
            z, g = z_ref[:, cols], doz_ref[:, cols]
            sg = _sigmoid(z)
            r = lax.rsqrt(_pair_sum_lanes(o * o, ones) * (1.0 / HEAD_DIM) + NORM_EPS)
            dz_ref[:, cols] = (g * (o * r * w) * (sg * (1.0 + z * (1.0 - sg)))).astype(dz_ref.dtype)
            dn = g * (z * sg)
            dw += jnp.sum(dn * o * r, axis=0, keepdims=True)
            dnw = dn * w
            do_ref[p] = r * dnw - o * (r * r * r) * (_pair_sum_lanes(dnw * o, ones) * (1.0 / HEAD_DIM))

        @pl.when(pl.program_id(0) == 0)
        def _():
            dw_ref[...] = jnp.zeros_like(dw_ref)

        dw_ref[...] += jnp.where(_iota((8, 128), 0) == 0, dw, 0.0)

    tok = pl.BlockSpec((tm, WIDTH), lambda i: (i, 0))
    seg = pl.BlockSpec((tm, WIDTH), lambda i: (i, SEG_ZB // WIDTH))
    hm = pl.BlockSpec((HEADS // 2, tm, 128), lambda i: (0, i, 0))
    dz, do, dw = pl.pallas_call(
        body, name="gdn_out_bwd", grid=(t // tm,),
        in_specs=[pl.BlockSpec(memory_space=pl.ANY), tok, hm, pl.BlockSpec((1, 128), lambda i: (0, 0)), seg],
        out_specs=[seg, hm, pl.BlockSpec((8, 128), lambda i: (0, 0))],
        out_shape=[jax.ShapeDtypeStruct((t, PACKED_WIDTH), MXU_DTYPE), jax.ShapeDtypeStruct((HEADS // 2, t, 128), F32),
                   jax.ShapeDtypeStruct((8, 128), F32)],
        input_output_aliases={0: 0},
        compiler_params=_params(("arbitrary",)),
    )(dproj, d_oz, o_hm, jnp.tile(gdn_norm_w, (1, 2)), proj)
    return dz, do, dw[:, :HEAD_DIM] + dw[:, HEAD_DIM:]


def _merge(y_a, y_b, proj, tm=1024):
    t = proj.shape[0]

    def body(ya_ref, yb_ref, ga_ref, gb_ref, m_ref):
        m_ref[...] = (_sigmoid(ga_ref[...]) * ya_ref[...] + _sigmoid(gb_ref[...]) * yb_ref[...]).astype(m_ref.dtype)

    half = pl.BlockSpec((tm, WIDTH), lambda i, c: (i, c))
    return pl.pallas_call(
        body, name="merge", grid=(t // tm, 2),
        in_specs=[half, half, pl.BlockSpec((tm, WIDTH), lambda i, c: (i, SEG_GA // WIDTH + c)),
                  pl.BlockSpec((tm, WIDTH), lambda i, c: (i, SEG_GB // WIDTH + c))],
        out_specs=half, out_shape=jax.ShapeDtypeStruct((t, D_MODEL), MXU_DTYPE),
        compiler_params=_params(("parallel", "parallel")),
    )(y_a, y_b, proj, proj)


def _merge_bwd(dproj, d_m, y, proj, seg, name, tm=1024):
    t = proj.shape[0]

    def body(*refs):
        dm_ref, y_ref, g_ref, dg_ref, dy_ref = refs[-5:]
        dm = dm_ref[...]
        s = _sigmoid(g_ref[...])
        dy_ref[...] = (dm * s).astype(dy_ref.dtype)
        dg_ref[...] = (dm * y_ref[...] * s * (1.0 - s)).astype(dg_ref.dtype)

    half = pl.BlockSpec((tm, WIDTH), lambda i, c: (i, c))
    gate = pl.BlockSpec((tm, WIDTH), lambda i, c: (i, seg // WIDTH + c))
    specs, args, aliases = [half, half, gate], [d_m, y, proj], {}
    if dproj is not None:
        specs, args, aliases = [pl.BlockSpec(memory_space=pl.ANY)] + specs, [dproj] + args, {0: 0}
    return pl.pallas_call(
        body, name=name, grid=(t // tm, 2), in_specs=specs, out_specs=[gate, half],
        out_shape=[jax.ShapeDtypeStruct((t, PACKED_WIDTH), MXU_DTYPE), jax.ShapeDtypeStruct((t, D_MODEL), MXU_DTYPE)],
        input_output_aliases=aliases,
        compiler_params=_params(("parallel", "parallel")),
    )(*args)


def _tail(x, mo, final_w, target, tm=512):
    t = x.shape[0]

    def body(x_ref, mo_ref, w_ref, t_ref, dxm_ref, dx_ref, loss_ref, dw_ref):
        x2 = x_ref[...] + mo_ref[...]
        w = w_ref[...]
        r = lax.rsqrt(jnp.mean(x2 * x2, axis=-1, keepdims=True) + NORM_EPS)
        xn = x2 * r
        err = xn * w - t_ref[...]
        dy = err * (1.0 / D_MODEL)
        dyw = dy * w
        dx2 = r * dyw - x2 * (r * r * r) * jnp.mean(dyw * x2, axis=-1, keepdims=True)
        dx_ref[...] = dx2
        dxm_ref[...] = dx2.astype(dxm_ref.dtype)
        loss = 0.5 * jnp.sum(jnp.sum(err * err, axis=-1, keepdims=True) * (1.0 / D_MODEL), axis=0, keepdims=True)
        onehot = jnp.where((_iota((8, 128), 0) == 0) & (_iota((8, 128), 1) == 0), 1.0, 0.0)

        @pl.when(pl.program_id(0) == 0)
        def _():
            loss_ref[...] = jnp.zeros_like(loss_ref)
            dw_ref[...] = jnp.zeros_like(dw_ref)

        loss_ref[...] += loss * onehot
        dw_ref[...] += jnp.where(_iota((8, D_MODEL), 0) == 0, jnp.sum(dy * xn, axis=0, keepdims=True), 0.0)

    tok = pl.BlockSpec((tm, D_MODEL), lambda i: (i, 0))
    return pl.pallas_call(
        body, name="tail", grid=(t // tm,),
        in_specs=[tok, tok, pl.BlockSpec((1, D_MODEL), lambda i: (0, 0)), tok],
        out_specs=[tok, tok, pl.BlockSpec((8, 128), lambda i: (0, 0)), pl.BlockSpec((8, D_MODEL), lambda i: (0, 0))],
        out_shape=[jax.ShapeDtypeStruct((t, D_MODEL), MXU_DTYPE), jax.ShapeDtypeStruct((t, D_MODEL), F32),
                   jax.ShapeDtypeStruct((8, 128), F32), jax.ShapeDtypeStruct((8, D_MODEL), F32)],
        compiler_params=_params(("arbitrary",)),
    )(x, mo, final_w, target)


def _norm_bwd(x, norm_w, dh, dx2, tm=512):
    t = x.shape[0]

    def body(x_ref, w_ref, dh_ref, dx2_ref, dx_ref, dw_ref):
        xf, w, dh_ = x_ref[...], w_ref[...], dh_ref[...]
        r = lax.rsqrt(jnp.mean(xf * xf, axis=-1, keepdims=True) + NORM_EPS)
        dhw = dh_ * w
        dx_ref[...] = dx2_ref[...] + r * dhw - xf * (r * r * r) * jnp.mean(dhw * xf, axis=-1, keepdims=True)

        @pl.when(pl.program_id(0) == 0)
        def _():
            dw_ref[...] = jnp.zeros_like(dw_ref)

        dw_ref[...] += jnp.where(_iota((8, D_MODEL), 0) == 0, jnp.sum(dh_ * xf * r, axis=0, keepdims=True), 0.0)

    tok = pl.BlockSpec((tm, D_MODEL), lambda i: (i, 0))
    return pl.pallas_call(
        body, name="norm_bwd", grid=(t // tm,),
        in_specs=[tok, pl.BlockSpec((1, D_MODEL), lambda i: (0, 0)), tok, tok],
        out_specs=[tok, pl.BlockSpec((8, D_MODEL), lambda i: (0, 0))],
        out_shape=[jax.ShapeDtypeStruct((t, D_MODEL), F32), jax.ShapeDtypeStruct((8, D_MODEL), F32)],
        compiler_params=_params(("arbitrary",)),
    )(x, norm_w, dh, dx2)


def _local_step(x, target, norm_w, wp, conv_w, a_log, dt_bias, gdn_norm_w, w_up_a, w_up_b, w_out, final_w, start_reduce):
    t = x.shape[0]
    tables = _rope_tables(t)
    a_log = jnp.pad(a_log, ((0, 0), (HEADS, 128 - 2 * HEADS)))
    dt_bias = jnp.pad(dt_bias, ((0, 0), (HEADS, 128 - 2 * HEADS)))

    proj, h_t = _norm_proj(x, norm_w, wp)
    qkvs = _rope_fwd(proj, tables)
    outs, lses = zip(*[_att_fwd(qkvs[gi], d, f"att_fwd{gi}") for gi, d in enumerate(DILATIONS)])
    oz_a, o_a, *lse_views = _att_merge(outs, lses, proj)
    conv = _conv_fwd(proj, conv_w)
    gq, gk, gv, gb, gg, ggl, grow = _gdn_prep(conv, proj, a_log, dt_bias)
    o_b, states, invs = _gdn_fwd(gq, gk, gv, gb, gg, ggl, grow)
    oz_b = _gdn_out(o_b, gdn_norm_w, proj)
    big = dict(tm=1024, tn=1024, tk=1024)
    y_a = _matmul(oz_a, w_up_a, "nn", "up_a", **big)
    y_b = _matmul(oz_b, w_up_b, "nn", "up_b", **big)
    merged = _merge(y_a, y_b, proj)
    mo = _matmul(merged, w_out, "nn", "out_proj", **big)
    dx2_m, dx2, loss_blk, d_final = _tail(x, mo, final_w, target)

    d_wout = _matmul(merged, dx2_m, "tn", "d_w_out", **big)
    d_m = _matmul(dx2_m, w_out, "nt", "d_merged", **big)
    dproj, dy_a = _merge_bwd(None, d_m, y_a, proj, SEG_GA, "merge_bwd_a")
    dproj, dy_b = _merge_bwd(dproj, d_m, y_b, proj, SEG_GB, "merge_bwd_b")
    d_wua = _matmul(oz_a, dy_a, "tn", "d_w_up_a", **big)
    d_wub = _matmul(oz_b, dy_b, "tn", "d_w_up_b", **big)
    d_oz_a = _matmul(dy_a, w_up_a, "nt", "d_oz_a", **big)
    d_oz_b = _matmul(dy_b, w_up_b, "nt", "d_oz_b", **big)
    dproj, *views = _att_merge_bwd(dproj, d_oz_a, o_a, proj)
    do_views, delta_views = views[:3], views[3:]
    dqs, dkvs = zip(*[_att_bwd(qkvs[gi], do_views[gi], lse_views[gi], delta_views[gi], d, f"att_bwd{gi}")
                      for gi, d in enumerate(DILATIONS)])
    dproj = _rope_bwd(dproj, dqs, dkvs, tables)
    dproj, do_b, d_gnw = _gdn_out_bwd(dproj, d_oz_b, o_b, gdn_norm_w, proj)
    dgq, dgk, dgv, dgb, dgg, dggl, dgrow = _gdn_bwd(gq, gk, gv, gb, gg, ggl, grow, states, invs, do_b)
    dproj, dconv, d_small = _gdn_prep_bwd(dproj, conv, proj, a_log, dt_bias, dgq, dgk, dgv, dgb, dgg, dggl, dgrow)
    dproj, d_convw = _conv_bwd(dproj, dconv, proj, conv_w)
    d_wp = _matmul(h_t, dproj, "nn", "d_w_in", tm=1024, tn=PACKED_WIDTH // 4, tk=1024)
    in_flight, token = start_reduce(d_wp, d_wua, d_wub, d_wout, d_convw[0:GDN_CONV])
    dh = _matmul(dproj, wp, "nt", "d_h", tm=1024, tn=1024, tk=PACKED_WIDTH // 4, after=token)
    grad_x, d_norm = _norm_bwd(x, norm_w, dh, dx2)
    return dict(loss=loss_blk, grad_x=grad_x, norm_w=d_norm[0:1], in_flight=in_flight,
                a_log=d_small[0:1, HEADS:2 * HEADS], dt_bias=d_small[1:2, HEADS:2 * HEADS], gdn_norm_w=d_gnw[0:1],
                final_norm_w=d_final[0:1])


SHARDS = 4
W_IN_SHARD = IN_WIDTH // SHARDS
ROWS_UP = WIDTH * (D_MODEL // SHARDS) // 128
ROWS_OUT = (D_MODEL // SHARDS) * D_MODEL // 128
CONV_SHARD = 3 * WIDTH // SHARDS
ROWS_CONV = 16
SLAB_ROWS = 2 * ROWS_UP + ROWS_OUT + 2 * ROWS_CONV
HALF_ROWS = SLAB_ROWS // 2
BIG_HALF = (D_MODEL // 2, W_IN_SHARD)
SMALL_HALF = (HALF_ROWS, 128)
MESH = pl.DeviceIdType.MESH
ANY = pl.BlockSpec(memory_space=pl.ANY)


def _pad_rows(a, rows):
    return jnp.pad(a, ((0, rows - a.shape[0]), (0, 0)))


def _pack_slab(w_up_a, w_up_b, w_out, conv, conv_lo):
    parts = [w_up_a.reshape(ROWS_UP, 128), w_up_b.reshape(ROWS_UP, 128), w_out.reshape(ROWS_OUT, 128),
             _pad_rows(conv.reshape(-1, 128), ROWS_CONV), _pad_rows(conv_lo.reshape(-1, 128), ROWS_CONV)]
    return jnp.concatenate(parts, axis=0).reshape(2, *SMALL_HALF)


def _unpack_slab(slab):
    slab = slab.reshape(SLAB_ROWS, 128)
    r0 = 0
    out = []
    for rows, shape in ((ROWS_UP, (WIDTH, D_MODEL // SHARDS)), (ROWS_UP, (WIDTH, D_MODEL // SHARDS)),
                        (ROWS_OUT, (D_MODEL // SHARDS, D_MODEL)), (ROWS_CONV, None), (ROWS_CONV, None)):
        part = slab[r0:r0 + rows]
        out.append(part[:GDN_CONV * CONV_SHARD // 128].reshape(GDN_CONV, CONV_SHARD) if shape is None else part.reshape(shape))
        r0 += rows
    return out


def _mesh_position():
    x, y, c = lax.axis_index("x"), lax.axis_index("y"), lax.axis_index("c")
    return x, y, c, [(1 - x, y), (x, 1 - y), (1 - x, 1 - y)]


def _gather_weights(shards):
    n = len(shards)

    def body(*refs):
        in_refs, out_refs, (send_sems, recv_sems) = refs[:n], refs[n:2 * n], refs[2 * n:]
        x, y, c, chips = _mesh_position()

        def half(a, chip, which):
            return out_refs[a].at[2 * chip[0] + chip[1], which]

        def copy(k, src, dst, to):
            return pltpu.make_async_remote_copy(src_ref=src, dst_ref=dst, send_sem=send_sems.at[k], recv_sem=recv_sems.at[k],
                                                device_id=to, device_id_type=MESH)

        pairs = [(a, j, chip) for a in range(n) for j, chip in enumerate(chips)]
        first = [copy(6 * a + j, in_refs[a].at[c], half(a, (x, y), c), (*chip, c)) for a, j, chip in pairs]
        for cp in first:
            cp.start()
        passed = [copy(6 * a + 3 + j, half(a, chip, c), half(a, chip, c), (x, y, 1 - c)) for a, j, chip in pairs]
        for i, (a, j, chip) in enumerate(pairs):
            copy(6 * a + j, half(a, chip, c), half(a, chip, c), (x, y, c)).wait_recv()
            passed[i].start()
        for a, j, chip in pairs:
            copy(6 * a + 3 + j, half(a, chip, 1 - c), half(a, chip, 1 - c), (x, y, c)).wait_recv()
        for cp in first + passed:
            cp.wait_send()

    return pl.pallas_call(
        body, name="gather_weights", in_specs=[ANY] * n, out_specs=[ANY] * n,
        out_shape=[jax.ShapeDtypeStruct((SHARDS, *s.shape), s.dtype) for s in shards],
        scratch_shapes=[pltpu.SemaphoreType.DMA((6 * n,)), pltpu.SemaphoreType.DMA((6 * n,))],
    )(*shards)


def _exchange_halves(grads):
    n = len(grads)

    def body(*refs):
        g_refs, out_refs, (send_sems, recv_sems) = refs[:n], refs[n:2 * n], refs[2 * n:]
        x, y, c, _ = _mesh_position()
        copies = [pltpu.make_async_remote_copy(src_ref=g_refs[a].at[s, 1 - c], dst_ref=out_refs[a].at[s],
                                               send_sem=send_sems.at[SHARDS * a + s], recv_sem=recv_sems.at[SHARDS * a + s],
                                               device_id=(x, y, 1 - c), device_id_type=MESH)
                  for a in range(n) for s in range(SHARDS)]
        for cp in copies:
            cp.start()
        for cp in copies:
            cp.wait()

    return pl.pallas_call(
        body, name="exchange_halves", in_specs=[ANY] * n, out_specs=[ANY] * n,
        out_shape=[jax.ShapeDtypeStruct((SHARDS, *g.shape[2:]), F32) for g in grads],
        scratch_shapes=[pltpu.SemaphoreType.DMA((SHARDS * n,)), pltpu.SemaphoreType.DMA((SHARDS * n,))],
    )(*grads)


def _pair_sum(grads, recv, blk, name):
    _, _, rows, cols = grads.shape

    def body(c_ref, g_ref, r_ref, o_ref):
        o_ref[...] = (g_ref[0] + r_ref[...]).astype(o_ref.dtype)

    spec = pl.BlockSpec((1, blk, cols), lambda s, i, c_ref: (s, i, 0))
    return pl.pallas_call(
        body, name=name,
        grid_spec=pltpu.PrefetchScalarGridSpec(
            num_scalar_prefetch=1, grid=(SHARDS, rows // blk),
            in_specs=[pl.BlockSpec((1, 1, blk, cols), lambda s, i, c_ref: (s, c_ref[0], i, 0)), spec],
            out_specs=spec),
        out_shape=jax.ShapeDtypeStruct((SHARDS, rows, cols), MXU_DTYPE),
        compiler_params=_params(("parallel", "parallel")),
    )(lax.axis_index("c").astype(jnp.int32).reshape(1), grads, recv)


_HBM = pl.BlockSpec(memory_space=pltpu.HBM)
_SEM = pl.BlockSpec(memory_space=pltpu.SEMAPHORE)
_DATAFLOW = pltpu.SideEffectType.DATAFLOW_SIDE_EFFECTING


def _scatter_copies(p_refs, l_refs, send_sems, recv_sems):
    x, y, c, chips = _mesh_position()
    return [pltpu.make_async_remote_copy(src_ref=p_refs[a].at[2 * chip[0] + chip[1]], dst_ref=l_refs[a].at[j],
                                         send_sem=send_sems.at[3 * a + j], recv_sem=recv_sems.at[3 * a + j],
                                         device_id=(*chip, c), device_id_type=MESH)
            for a in range(len(p_refs)) for j, chip in enumerate(chips)]


def _scatter_start(pairs):
    n = len(pairs)
    lands = [lax.empty((3, *p.shape[1:]), p.dtype) for p in pairs]

    def body(*refs):
        p_refs, l_refs, send_sems, recv_sems, token = refs[:n], refs[n:2 * n], refs[2 * n], refs[2 * n + 1], refs[-1]
        for cp in _scatter_copies(p_refs, l_refs, send_sems, recv_sems):
            cp.start()
        token[...] = jnp.zeros_like(token)

    operands = [pltpu.with_memory_space_constraint(a, pltpu.HBM) for a in (*pairs, *lands)]
    return pl.pallas_call(
        body, name="scatter_start", in_specs=[_HBM] * (2 * n),
        out_shape=(pltpu.SemaphoreType.DMA((3 * n,)), pltpu.SemaphoreType.DMA((3 * n,)),
                   *[pltpu.HBM(a.shape, a.dtype) for a in operands], jax.ShapeDtypeStruct((8, 128), F32)),
        out_specs=(_SEM, _SEM, *[_HBM] * (2 * n), pl.BlockSpec(memory_space=pltpu.VMEM)),
        input_output_aliases={i: 2 + i for i in range(2 * n)},
        compiler_params=pltpu.CompilerParams(has_side_effects=_DATAFLOW),
    )(*operands)


def _scatter_wait(send_sems, recv_sems, passed, after):
    n = len(passed) // 2

    def body(*refs):
        p_refs, l_refs, send_s, recv_s = refs[:n], refs[n:2 * n], refs[2 * n], refs[2 * n + 1]
        for cp in _scatter_copies(p_refs, l_refs, send_s, recv_s):
            cp.wait_send()
            cp.wait_recv()

    return pl.pallas_call(
        body, name="scatter_wait", in_specs=[_HBM] * (2 * n) + [_SEM, _SEM, ANY],
        out_shape=[pltpu.HBM(a.shape, a.dtype) for a in passed], out_specs=[_HBM] * (2 * n),
        input_output_aliases={i: i for i in range(2 * n)},
        compiler_params=pltpu.CompilerParams(has_side_effects=_DATAFLOW),
    )(*passed, send_sems, recv_sems, after)


def _chip_sum(pairs, recv, blk, name):
    _, rows, cols = pairs.shape

    def body(pos_ref, p_ref, r_ref, o_ref):
        o_ref[0] = ((p_ref[0].astype(F32) + r_ref[0].astype(F32)) + r_ref[1].astype(F32)) + r_ref[2].astype(F32)

    pos = jnp.stack([2 * lax.axis_index("x") + lax.axis_index("y"), lax.axis_index("c")]).astype(jnp.int32)
    return pl.pallas_call(
        body, name=name,
        grid_spec=pltpu.PrefetchScalarGridSpec(
            num_scalar_prefetch=1, grid=(rows // blk,),
            in_specs=[pl.BlockSpec((1, blk, cols), lambda i, pos_ref: (pos_ref[0], i, 0)),
                      pl.BlockSpec((3, blk, cols), lambda i, pos_ref: (0, i, 0))],
            out_specs=pl.BlockSpec((1, blk, cols), lambda i, pos_ref: (pos_ref[1], i, 0))),
        out_shape=jax.ShapeDtypeStruct((2, rows, cols), F32),
        compiler_params=_params(("parallel",)),
    )(pos, pairs, recv)


def _share_total(totals):
    n = len(totals)

    def body(*refs):
        t_refs, out_refs, (send_sems, recv_sems) = refs[:n], refs[n:2 * n], refs[2 * n:]
        x, y, c, _ = _mesh_position()
        copies = [pltpu.make_async_remote_copy(src_ref=t_refs[a].at[c], dst_ref=out_refs[a].at[c], send_sem=send_sems.at[a],
                                               recv_sem=recv_sems.at[a], device_id=(x, y, 1 - c), device_id_type=MESH)
                  for a in range(n)]
        for cp in copies:
            cp.start()
        for a in range(n):
            other = out_refs[a].at[1 - c]
            pltpu.make_async_remote_copy(src_ref=other, dst_ref=other, send_sem=send_sems.at[a], recv_sem=recv_sems.at[a],
                                         device_id=(x, y, c), device_id_type=MESH).wait_recv()
        for cp in copies:
            cp.wait_send()

    return pl.pallas_call(
        body, name="share_total", in_specs=[ANY] * n, out_specs=[ANY] * n,
        out_shape=[jax.ShapeDtypeStruct(t.shape, F32) for t in totals],
        scratch_shapes=[pltpu.SemaphoreType.DMA((n,)), pltpu.SemaphoreType.DMA((n,))],
        input_output_aliases={a: a for a in range(n)},
    )(*totals)


def _allreduce_small(block):
    def body(b_ref, out_ref, gath, send_sems, recv_sems):
        x, y, c, _ = _mesh_position()
        me = 4 * x + 2 * y + c
        gath[me] = b_ref[...]
        copies = []
        for k in range(1, 8):
            peer = (x ^ (k >> 2), y ^ ((k >> 1) & 1), c ^ (k & 1))
            copies.append(pltpu.make_async_remote_copy(src_ref=b_ref, dst_ref=gath.at[me], send_sem=send_sems.at[k - 1],
                                                       recv_sem=recv_sems.at[k - 1], device_id=peer, device_id_type=MESH))
        for cp in copies:
            cp.start()
        for k in range(1, 8):
            src = 4 * (x ^ (k >> 2)) + 2 * (y ^ ((k >> 1) & 1)) + (c ^ (k & 1))
            pltpu.make_async_remote_copy(src_ref=b_ref, dst_ref=gath.at[src], send_sem=send_sems.at[k - 1],
                                         recv_sem=recv_sems.at[k - 1], device_id=(x, y, c), device_id_type=MESH).wait_recv()
        for cp in copies:
            cp.wait_send()
        acc = gath[0]
        for d in range(1, 8):
            acc = acc + gath[d]
        out_ref[...] = acc

    vm = pl.BlockSpec(memory_space=pltpu.VMEM)
    return pl.pallas_call(
        body, name="allreduce_small", in_specs=[vm], out_specs=vm,
        out_shape=jax.ShapeDtypeStruct((8, D_MODEL), F32),
        scratch_shapes=[pltpu.VMEM((8, 8, D_MODEL), F32), pltpu.SemaphoreType.DMA((7,)), pltpu.SemaphoreType.DMA((7,))],
    )(block)


def _adamw(w, g, m, v, name):
    rows, cols = w.shape
    tr = 128 if rows % 128 == 0 else rows

    def body(w_ref, g_ref, m_ref, v_ref, d_ref, nm_ref, nv_ref):
        gv = g_ref[...]
        nm = ADAM_B1 * m_ref[...] + (1.0 - ADAM_B1) * gv
        nv = ADAM_B2 * v_ref[...] + (1.0 - ADAM_B2) * (gv * gv)
        m_hat = nm / (1.0 - ADAM_B1 ** ADAM_STEP)
        v_hat = nv / (1.0 - ADAM_B2 ** ADAM_STEP)
        d_ref[...] = -ADAM_LR * (m_hat / (jnp.sqrt(v_hat) + ADAM_EPS) + ADAM_WD * w_ref[...])
        nm_ref[...] = nm
        nv_ref[...] = nv

    spec = pl.BlockSpec((tr, cols), lambda i: (i, 0))
    shape = jax.ShapeDtypeStruct((rows, cols), F32)
    return pl.pallas_call(
        body, name=name, grid=(rows // tr,), in_specs=[spec] * 4, out_specs=[spec] * 3, out_shape=[shape] * 3,
        compiler_params=_params(("parallel",)),
    )(w, g, m, v)


def kernel(x, norm_w, w_in, conv_w, a_log, dt_bias, gdn_norm_w, w_up_a, w_up_b, w_out, final_norm_w, loss_target, m_norm_w, m_w_in, m_conv_w, m_a_log, m_dt_bias, m_gdn_norm_w, m_w_up_a, m_w_up_b, m_w_out, m_final_norm_w, v_norm_w, v_w_in, v_conv_w, v_a_log, v_dt_bias, v_gdn_norm_w, v_w_up_a, v_w_up_b, v_w_out, v_final_norm_w):
    conv_hi = conv_w[0].astype(MXU_DTYPE)
    conv_lo = (conv_w[0] - conv_hi.astype(F32)).astype(MXU_DTYPE)
    big = w_in[0].astype(MXU_DTYPE).reshape(2, *BIG_HALF)
    slab = _pack_slab(w_up_a[0].astype(MXU_DTYPE), w_up_b[0].astype(MXU_DTYPE), w_out[0].astype(MXU_DTYPE), conv_hi, conv_lo)
    own_shard = 2 * lax.axis_index("x") + lax.axis_index("y")
    bigs, slabs = _gather_weights([big, slab])
    bigs = lax.dynamic_update_slice(bigs, big[None], (own_shard, 0, 0, 0)).reshape(SHARDS, D_MODEL, W_IN_SHARD)
    slabs = lax.dynamic_update_slice(slabs, slab[None], (own_shard, 0, 0, 0))
    parts = [_unpack_slab(slabs[s]) for s in range(SHARDS)]
    split = BA_END - (SHARDS - 1) * W_IN_SHARD
    wp = jnp.concatenate([bigs[s] for s in range(SHARDS - 1)]
                         + [bigs[-1][:, :split], jnp.zeros((D_MODEL, SEG_GA - BA_END), MXU_DTYPE), bigs[-1][:, split:]], axis=1)
    w_up_a_full = jnp.concatenate([p[0] for p in parts], axis=1)
    w_up_b_full = jnp.concatenate([p[1] for p in parts], axis=1)
    w_out_full = jnp.concatenate([p[2] for p in parts], axis=0)
    conv_full = jnp.concatenate([p[3].astype(F32) + p[4].astype(F32) for p in parts], axis=1)

    blocks, tags = (128, HALF_ROWS), ("w_in", "slab")

    def start_reduce(d_wp, d_w_up_a, d_w_up_b, d_w_out, d_conv_w):
        d_w_in = [d_wp[:, s * W_IN_SHARD:(s + 1) * W_IN_SHARD] for s in range(SHARDS - 1)]
        d_w_in.append(jnp.concatenate([d_wp[:, (SHARDS - 1) * W_IN_SHARD:BA_END], d_wp[:, SEG_GA:]], axis=1))
        zero_conv = jnp.zeros((GDN_CONV, CONV_SHARD), F32)
        grads = [jnp.stack(d_w_in).reshape(SHARDS, 2, *BIG_HALF),
                 jnp.stack([_pack_slab(d_w_up_a[:, s * 256:(s + 1) * 256], d_w_up_b[:, s * 256:(s + 1) * 256],
                                       d_w_out[s * 256:(s + 1) * 256], d_conv_w[:, s * CONV_SHARD:(s + 1) * CONV_SHARD],
                                       zero_conv) for s in range(SHARDS)])]
        from_sibling = _exchange_halves(grads)
        pairs = [_pair_sum(gr, fs, blk, f"pair_sum_{tag}") for gr, fs, blk, tag in zip(grads, from_sibling, blocks, tags)]
        *in_flight, token = _scatter_start(pairs)
        return in_flight, token

    g = _local_step(x[0], loss_target[0], norm_w, wp, conv_full, a_log, dt_bias, gdn_norm_w,
                    w_up_a_full, w_up_b_full, w_out_full, final_norm_w[None], start_reduce)

    send_sems, recv_sems, *passed = g["in_flight"]
    arrived = _scatter_wait(send_sems, recv_sems, passed, after=g["grad_x"])
    pairs, from_chips = arrived[:2], arrived[2:]
    total_big, total_slab = _share_total([_chip_sum(p, fc, blk, f"chip_sum_{tag}")
                                          for p, fc, blk, tag in zip(pairs, from_chips, blocks, tags)])
    g_w_in = total_big.reshape(D_MODEL, W_IN_SHARD)
    g_w_up_a, g_w_up_b, g_w_out, g_conv, _ = _unpack_slab(total_slab)

    row2 = jnp.concatenate([g["gdn_norm_w"], g["a_log"], g["dt_bias"], g["loss"][0:1, 0:1],
                            jnp.zeros((1, D_MODEL - HEAD_DIM - 2 * HEADS - 1), F32)], axis=1)
    small = _allreduce_small(jnp.concatenate([g["norm_w"], g["final_norm_w"], row2, jnp.zeros((5, D_MODEL), F32)], axis=0))
    g_norm, g_final = small[0:1], small[1]
    g_gnw, g_alog, g_dt = small[2:3, 0:HEAD_DIM], small[2:3, HEAD_DIM:HEAD_DIM + HEADS], small[2:3, HEAD_DIM + HEADS:HEAD_DIM + 2 * HEADS]
    loss = small[2, HEAD_DIM + 2 * HEADS]

    names = ["norm_w", "w_in", "conv_w", "a_log", "dt_bias", "gdn_norm_w", "w_up_a", "w_up_b", "w_out", "final_norm_w"]
    weights = dict(zip(names, (norm_w, w_in, conv_w, a_log, dt_bias, gdn_norm_w, w_up_a, w_up_b, w_out, final_norm_w)))
    ms = dict(zip(names, (m_norm_w, m_w_in, m_conv_w, m_a_log, m_dt_bias, m_gdn_norm_w, m_w_up_a, m_w_up_b, m_w_out, m_final_norm_w)))
    vs = dict(zip(names, (v_norm_w, v_w_in, v_conv_w, v_a_log, v_dt_bias, v_gdn_norm_w, v_w_up_a, v_w_up_b, v_w_out, v_final_norm_w)))
    grads2d = dict(norm_w=g_norm, w_in=g_w_in, conv_w=g_conv, a_log=g_alog, dt_bias=g_dt, gdn_norm_w=g_gnw,
                   w_up_a=g_w_up_a, w_up_b=g_w_up_b, w_out=g_w_out, final_norm_w=g_final[None])
    grad_out, delta, new_m, new_v = [], [], [], []
    for n in names:
        shape = weights[n].shape
        two_d = grads2d[n].shape
        d, nm, nv = _adamw(weights[n].reshape(two_d), grads2d[n], ms[n].reshape(two_d), vs[n].reshape(two_d), f"adamw_{n}")
        grad_out.append(grads2d[n].reshape(shape))
        delta.append(d.reshape(shape))
        new_m.append(nm.reshape(shape))
        new_v.append(nv.reshape(shape))
    return (loss, g["grad_x"][None], *grad_out, *delta, *new_m, *new_v)
```

```python
import functools

import jax
import jax.numpy as jnp
from jax import lax
from jax.experimental import pallas as pl
from jax.experimental.pallas import tpu as pltpu

F32 = jnp.float32
MXU_DTYPE = jnp.bfloat16
HIGHEST = lax.Precision.HIGHEST

D_MODEL = 1024
HEADS = 8
HEAD_DIM = 64
WIDTH = HEADS * HEAD_DIM
NORM_EPS = 1e-6
ROPE_THETA = 10000.0
ATT_BLOCK = 128
DILATIONS = (1, 4, 16)
GDN_CHUNK = 64
GDN_CONV = 4
IN_WIDTH = 9232
SEG_A, SEG_ZA, SEG_B, SEG_ZB, SEG_BA, SEG_GA, SEG_GB, PACKED_WIDTH = 0, 4608, 5120, 6656, 7168, 7680, 8704, 9728
BA_END = 7184
VMEM_LIMIT = 56 * 1024 * 1024

ADAM_LR, ADAM_B1, ADAM_B2, ADAM_EPS, ADAM_WD, ADAM_STEP = 0.001, 0.9, 0.999, 1e-08, 0.01, 10

_NN = (((1,), (0,)), ((), ()))
_NT = (((1,), (1,)), ((), ()))
_TN = (((0,), (0,)), ((), ()))


def _params(sem):
    return pltpu.CompilerParams(dimension_semantics=sem, vmem_limit_bytes=VMEM_LIMIT)


def _mxu(a, b, dims):
    return lax.dot_general(a.astype(MXU_DTYPE), b.astype(MXU_DTYPE), dims, preferred_element_type=F32)


def _sigmoid(x):
    return 1.0 / (1.0 + jnp.exp(-x))


def _softplus(x):
    return jnp.maximum(x, 0.0) + jnp.log(1.0 + jnp.exp(-jnp.abs(x)))


def _iota(shape, axis):
    return lax.broadcasted_iota(jnp.int32, shape, axis)


def _matmul(a, b, mode, name, out_dtype=F32, tm=512, tn=512, tk=512):
    if mode == "nn":
        (m, k), (k2, n) = a.shape, b.shape
    elif mode == "nt":
        (m, k), (n, k2) = a.shape, b.shape
    else:
        (k, m), (k2, n) = a.shape, b.shape
    assert k == k2
    tm, tn, tk = min(tm, m), min(tn, n), min(tk, k)
    assert m % tm == 0 and n % tn == 0 and k % tk == 0
    nk = k // tk
    dims = {"nn": _NN, "nt": _NT, "tn": _TN}[mode]

    assert out_dtype == F32

    def body(a_ref, b_ref, o_ref):
        kk = pl.program_id(2)
        part = _mxu(a_ref[...], b_ref[...], dims)

        @pl.when(kk == 0)
        def _():
            o_ref[...] = part

        @pl.when(kk > 0)
        def _():
            o_ref[...] += part

    a_spec = pl.BlockSpec((tk, tm), lambda i, j, kk: (kk, i)) if mode == "tn" else pl.BlockSpec((tm, tk), lambda i, j, kk: (i, kk))
    b_spec = pl.BlockSpec((tn, tk), lambda i, j, kk: (j, kk)) if mode == "nt" else pl.BlockSpec((tk, tn), lambda i, j, kk: (kk, j))
    return pl.pallas_call(
        body, name=name, grid=(m // tm, n // tn, nk), in_specs=[a_spec, b_spec],
        out_specs=pl.BlockSpec((tm, tn), lambda i, j, kk: (i, j)),
        out_shape=jax.ShapeDtypeStruct((m, n), out_dtype),
        compiler_params=_params(("parallel", "parallel", "arbitrary")),
    )(a, b)


def _norm_proj(x, norm_w, wp, tm=1024, tn=PACKED_WIDTH // 4):
    t = x.shape[0]
    tm = min(tm, t)

    def body(x_ref, nw_ref, w_ref, proj_ref, ht_ref, h_scr):
        @pl.when(pl.program_id(1) == 0)
        def _():
            xf = x_ref[...]
            r = lax.rsqrt(jnp.mean(xf * xf, axis=-1, keepdims=True) + NORM_EPS)
            h = xf * r * nw_ref[...]
            h_scr[...] = h.astype(h_scr.dtype)
            ht_ref[...] = h.T.astype(ht_ref.dtype)

        proj_ref[...] = jnp.dot(h_scr[...], w_ref[...], preferred_element_type=F32)

    return pl.pallas_call(
        body, name="norm_proj", grid=(t // tm, PACKED_WIDTH // tn),
        in_specs=[pl.BlockSpec((tm, D_MODEL), lambda i, j: (i, 0)),
                  pl.BlockSpec((1, D_MODEL), lambda i, j: (0, 0)),
                  pl.BlockSpec((D_MODEL, tn), lambda i, j: (0, j))],
        out_specs=[pl.BlockSpec((tm, tn), lambda i, j: (i, j)),
                   pl.BlockSpec((D_MODEL, tm), lambda i, j: (0, i))],
        out_shape=[jax.ShapeDtypeStruct((t, PACKED_WIDTH), F32), jax.ShapeDtypeStruct((D_MODEL, t), MXU_DTYPE)],
        scratch_shapes=[pltpu.VMEM((tm, D_MODEL), MXU_DTYPE)],
        compiler_params=_params(("parallel", "arbitrary")),
    )(x, norm_w, wp)


def _rope_tables(t):
    lane = jnp.arange(128)
    inv_freq = ROPE_THETA ** (-jnp.arange(0, HEAD_DIM, 2, dtype=F32) / HEAD_DIM)
    ang = jnp.arange(t, dtype=F32)[:, None] * inv_freq[None, :]
    ang = jnp.concatenate([ang, ang, ang, ang], axis=-1)
    first_half = (lane % HEAD_DIM) < HEAD_DIM // 2
    cos, sin = jnp.cos(ang), jnp.sin(ang)
    return cos, jnp.where(first_half, -sin, 0.0), jnp.where(first_half, 0.0, sin)


def _rope_cols(x, cos, sin_lo, sin_hi, sign):
    outs = []
    for c in range(x.shape[1] // 128):
        xc = x[:, c * 128:(c + 1) * 128]
        rot = pltpu.roll(xc, 96, 1) * sin_lo + pltpu.roll(xc, 32, 1) * sin_hi
        outs.append(xc * cos + sign * rot)
    return jnp.concatenate(outs, axis=1)


def _rope_block(x, cos, sin_lo, sin_hi, sign):
    return jnp.concatenate([_rope_cols(x[:, :2 * WIDTH], cos, sin_lo, sin_hi, sign), x[:, 2 * WIDTH:]], axis=1)


def _tile_scratch(tm, cols):
    return pltpu.VMEM((cols // 128, tm, 128), F32)


def _store_tile(scr, y):
    for c in range(scr.shape[0]):
        scr[c] = y[:, c * 128:(c + 1) * 128]


def _load_tile(scr):
    return jnp.concatenate([scr[c] for c in range(scr.shape[0])], axis=1)


def _to_strided_view(scr, o_ref, d, width=None, col0=0):
    n, tm, _ = scr.shape
    width = n * 128 if width is None else width
    for r in range(d):
        for c in range(n):
            at = r * width + col0 + c * 128
            o_ref[:, at:at + 128] = scr[c, pl.ds(r, tm // d, stride=d), :].astype(o_ref.dtype)


def _from_strided_view(i_ref, scr, d):
    n, tm, _ = scr.shape
    for r in range(d):
        for c in range(n):
            scr[c, pl.ds(r, tm // d, stride=d), :] = i_ref[:, (r * n + c) * 128:(r * n + c + 1) * 128].astype(F32)


def _strided_spec(tm, d, cols):
    return pl.BlockSpec((tm // d, d * cols), lambda i: (i, 0))


def _rope_fwd(proj, tables, tm=256):
    t = proj.shape[0]
    cols = 3 * WIDTH

    def body(x_ref, c_ref, sl_ref, sh_ref, o0, o1, o2, scr):
        for g, (d, o_ref) in enumerate(zip(DILATIONS, (o0, o1, o2))):
            y = _rope_block(x_ref[:, g * cols:(g + 1) * cols], c_ref[...], sl_ref[...], sh_ref[...], 1.0)
            if d == 1:
                o_ref[...] = y.astype(o_ref.dtype)
            else:
                _store_tile(scr, y)
                _to_strided_view(scr, o_ref, d)

    tab = pl.BlockSpec((tm, 128), lambda i: (i, 0))
    return pl.pallas_call(
        body, name="rope_fwd", grid=(t // tm,),
        in_specs=[pl.BlockSpec((tm, 3 * cols), lambda i: (i, 0)), tab, tab, tab],
        out_specs=[_strided_spec(tm, d, cols) for d in DILATIONS],
        out_shape=[jax.ShapeDtypeStruct((t // d, d * cols), MXU_DTYPE) for d in DILATIONS],
        scratch_shapes=[_tile_scratch(tm, cols)],
        compiler_params=_params(("parallel",)),
    )(proj, *tables)


def _rope_bwd(dproj, dqs, dkvs, tables, tm=256):
    t = dproj.shape[0]
    cols = 3 * WIDTH

    def body(dp_ref, q0, q1, q2, kv0, kv1, kv2, c_ref, sl_ref, sh_ref, o_ref, scr_q, scr_kv):
        for g, (d, q_ref, kv_ref) in enumerate(zip(DILATIONS, (q0, q1, q2), (kv0, kv1, kv2))):
            if d == 1:
                x = jnp.concatenate([q_ref[...], kv_ref[...]], axis=1)
            else:
                _from_strided_view(q_ref, scr_q, d)
                _from_strided_view(kv_ref, scr_kv, d)
                x = jnp.concatenate([_load_tile(scr_q), _load_tile(scr_kv)], axis=1)
            y = _rope_block(x, c_ref[...], sl_ref[...], sh_ref[...], -1.0)
            o_ref[:, g * cols:(g + 1) * cols] = y.astype(o_ref.dtype)

    tab = pl.BlockSpec((tm, 128), lambda i: (i, 0))
    return pl.pallas_call(
        body, name="rope_bwd", grid=(t // tm,),
        in_specs=[pl.BlockSpec(memory_space=pl.ANY)] + [_strided_spec(tm, d, WIDTH) for d in DILATIONS]
        + [_strided_spec(tm, d, 2 * WIDTH) for d in DILATIONS] + [tab, tab, tab],
        out_specs=pl.BlockSpec((tm, 3 * cols), lambda i: (i, 0)),
        out_shape=jax.ShapeDtypeStruct((t, PACKED_WIDTH), MXU_DTYPE),
        scratch_shapes=[_tile_scratch(tm, WIDTH), _tile_scratch(tm, 2 * WIDTH)],
        input_output_aliases={0: 0},
        compiler_params=_params(("parallel",)),
    )(dproj, *dqs, *dkvs, *tables)


def _att_masks():
    qi = _iota((ATT_BLOCK, ATT_BLOCK), 0)
    kj = _iota((ATT_BLOCK, ATT_BLOCK), 1)
    return kj <= qi, kj >= qi


def _att_fwd(qkv, d, name):
    rows = qkv.shape[0]
    nb = rows // ATT_BLOCK
    scale = HEAD_DIM ** -0.5

    def body(q_ref, kc_ref, kp_ref, vc_ref, vp_ref, o_ref, lse_ref):
        has_prev = pl.program_id(1) > 0
        m_cur, m_prev = _att_masks()
        m_prev = m_prev & has_prev
        hs = range(HEADS)
        sls = [slice(h * HEAD_DIM, (h + 1) * HEAD_DIM) for h in hs]
        qs = [q_ref[:, sl] for sl in sls]
        s_c = [jnp.where(m_cur, _mxu(qs[h], kc_ref[:, sls[h]], _NT) * scale, -jnp.inf) for h in hs]
        s_p = [jnp.where(m_prev, _mxu(qs[h], kp_ref[:, sls[h]], _NT) * scale, -jnp.inf) for h in hs]
        m = [jnp.maximum(jnp.max(s_c[h], axis=1, keepdims=True), jnp.max(s_p[h], axis=1, keepdims=True)) for h in hs]
        p_c = [jnp.exp(s_c[h] - m[h]) for h in hs]
        p_p = [jnp.exp(s_p[h] - m[h]) for h in hs]
        den = [jnp.sum(p_c[h], axis=1, keepdims=True) + jnp.sum(p_p[h], axis=1, keepdims=True) for h in hs]
        o = [_mxu(p_c[h], vc_ref[:, sls[h]], _NN) + _mxu(p_p[h], vp_ref[:, sls[h]], _NN) for h in hs]
        for h in hs:
            o_ref[:, sls[h]] = o[h] / den[h]
            lse_ref[:, sls[h]] = jnp.broadcast_to(m[h] + jnp.log(den[h]), (ATT_BLOCK, HEAD_DIM))

    def cur(c):
        return pl.BlockSpec((ATT_BLOCK, WIDTH), lambda r, i: (i, 3 * r + c))

    def prev(c):
        return pl.BlockSpec((ATT_BLOCK, WIDTH), lambda r, i: (jnp.maximum(i - 1, 0), 3 * r + c))

    out = pl.BlockSpec((ATT_BLOCK, WIDTH), lambda r, i: (i, r))
    return pl.pallas_call(
        body, name=name, grid=(d, nb), in_specs=[cur(0), cur(1), prev(1), cur(2), prev(2)],
        out_specs=[out, out],
        out_shape=[jax.ShapeDtypeStruct((rows, d * WIDTH), F32)] * 2,
        compiler_params=_params(("parallel", "arbitrary")),
    )(qkv, qkv, qkv, qkv, qkv)


def _att_bwd(qkv, do, lse, delta, d, name):
    rows = qkv.shape[0]
    nb = rows // ATT_BLOCK
    scale = HEAD_DIM ** -0.5

    def body(q_ref, kc_ref, kp_ref, vc_ref, vp_ref, do_ref, l_ref, dl_ref, dq_ref, dkv_ref, own):
        i = pl.program_id(1)
        m_cur, m_prev = _att_masks()
        m_prev = m_prev & (i > 0)

        hs = range(HEADS)
        sls = [slice(h * HEAD_DIM, (h + 1) * HEAD_DIM) for h in hs]
        col = [slice(h * HEAD_DIM, h * HEAD_DIM + 1) for h in hs]

        def probs(k_r, mask):
            s = [_mxu(q_ref[:, sls[h]], k_r[:, sls[h]], _NT) for h in hs]
            return [jnp.where(mask, jnp.exp(s[h] * scale - l_ref[:, col[h]]), 0.0) for h in hs]

        def dscores(p, v_r):
            dp = [_mxu(do_ref[:, sls[h]], v_r[:, sls[h]], _NT) for h in hs]
            return [(p[h] * (dp[h] - dl_ref[:, col[h]])).astype(MXU_DTYPE) for h in hs]

        @pl.when(i == 0)
        def _():
            own[...] = jnp.zeros_like(own)

        @pl.when(i < nb)
        def _():
            p_c = probs(kc_ref, m_cur)
            ds_c = dscores(p_c, vc_ref)
            p_p = probs(kp_ref, m_prev)
            ds_p = dscores(p_p, vp_ref)
            dq = [_mxu(ds_c[h], kc_ref[:, sls[h]], _NN) + _mxu(ds_p[h], kp_ref[:, sls[h]], _NN) for h in hs]
            dk_p = [_mxu(ds_p[h], q_ref[:, sls[h]], _TN) for h in hs]
            dv_p = [_mxu(p_p[h], do_ref[:, sls[h]], _TN) for h in hs]
            dk_c = [_mxu(ds_c[h], q_ref[:, sls[h]], _TN) for h in hs]
            dv_c = [_mxu(p_c[h], do_ref[:, sls[h]], _TN) for h in hs]
            for h in hs:
                vs = slice(WIDTH + h * HEAD_DIM, WIDTH + (h + 1) * HEAD_DIM)
                dq_ref[:, sls[h]] = dq[h] * scale
                dkv_ref[:, sls[h]] = own[:, sls[h]] + dk_p[h] * scale
                dkv_ref[:, vs] = own[:, vs] + dv_p[h]
                own[:, sls[h]] = dk_c[h] * scale
                own[:, vs] = dv_c[h]

        @pl.when(i == nb)
        def _():
            dkv_ref[...] = own[...]

    def qkv_spec(c, shift):
        return pl.BlockSpec((ATT_BLOCK, WIDTH), lambda r, i: (jnp.clip(i + shift, 0, nb - 1), 3 * r + c))

    tok = pl.BlockSpec((ATT_BLOCK, WIDTH), lambda r, i: (jnp.minimum(i, nb - 1), r))
    return pl.pallas_call(
        body, name=name, grid=(d, nb + 1),
        in_specs=[qkv_spec(0, 0), qkv_spec(1, 0), qkv_spec(1, -1), qkv_spec(2, 0), qkv_spec(2, -1), tok, tok, tok],
        out_specs=[tok, pl.BlockSpec((ATT_BLOCK, 2 * WIDTH), lambda r, i: (jnp.maximum(i - 1, 0), r))],
        out_shape=[jax.ShapeDtypeStruct((rows, d * WIDTH), F32), jax.ShapeDtypeStruct((rows, d * 2 * WIDTH), F32)],
        scratch_shapes=[pltpu.VMEM((ATT_BLOCK, 2 * WIDTH), F32)],
        compiler_params=_params(("parallel", "arbitrary")),
    )(qkv, qkv, qkv, qkv, qkv, do, lse, delta)


def _att_merge(os_, lses, proj, tm=256):
    t = proj.shape[0]

    def body(o0, o1, o2, l0, l1, l2, z_ref, oz_ref, o_ref, t0, t1, t2, s_o1, s_l1, s_o2, s_l2, s_t):
        _from_strided_view(o1, s_o1, DILATIONS[1])
        _from_strided_view(l1, s_l1, DILATIONS[1])
        _from_strided_view(o2, s_o2, DILATIONS[2])
        _from_strided_view(l2, s_l2, DILATIONS[2])
        a, b, c = l0[...], _load_tile(s_l1), _load_tile(s_l2)
        m = jnp.maximum(jnp.maximum(a, b), c)
        wa, wb, wc = jnp.exp(a - m), jnp.exp(b - m), jnp.exp(c - m)
        den = wa + wb + wc
        o = (wa * o0[...] + wb * _load_tile(s_o1) + wc * _load_tile(s_o2)) / den
        z = z_ref[...]
        o_ref[...] = o
        oz_ref[...] = (o * z * _sigmoid(z)).astype(oz_ref.dtype)
        total = m + jnp.log(den)
        t0[...] = total
        _store_tile(s_t, total)
        _to_strided_view(s_t, t1, DILATIONS[1])
        _to_strided_view(s_t, t2, DILATIONS[2])

    tok = pl.BlockSpec((tm, WIDTH), lambda i: (i, 0))
    views = [_strided_spec(tm, d, WIDTH) for d in DILATIONS]
    view_shapes = [jax.ShapeDtypeStruct((t // d, d * WIDTH), F32) for d in DILATIONS]
    return pl.pallas_call(
        body, name="att_merge", grid=(t // tm,),
        in_specs=views + views + [pl.BlockSpec((tm, WIDTH), lambda i: (i, SEG_ZA // WIDTH))],
        out_specs=[tok, tok] + views,
        out_shape=[jax.ShapeDtypeStruct((t, WIDTH), MXU_DTYPE), jax.ShapeDtypeStruct((t, WIDTH), F32)] + view_shapes,
        scratch_shapes=[_tile_scratch(tm, WIDTH)] * 5,
        compiler_params=_params(("parallel",)),
    )(*os_, *lses, proj)


def _att_merge_bwd(dproj, d_oz, o, proj, tm=256):
    t = proj.shape[0]

    def body(dp_ref, doz_ref, o_ref, z_ref, dz_ref, do0, do1, do2, dl0, dl1, dl2, s_do, s_dl):
        z, ov, g = z_ref[...], o_ref[...], doz_ref[...]
        sg = _sigmoid(z)
        do = g * z * sg
        dz_ref[...] = (g * ov * sg * (1.0 + z * (1.0 - sg))).astype(dz_ref.dtype)
        do0[...] = do.astype(do0.dtype)
        _store_tile(s_do, do)
        prod = do * ov
        for p in range(HEADS // 2):
            s_dl[p] = _pair_sum_lanes(prod[:, p * 128:(p + 1) * 128], _pair_ones())
        dl0[...] = _load_tile(s_dl)
        for d, do_v, dl_v in ((DILATIONS[1], do1, dl1), (DILATIONS[2], do2, dl2)):
            _to_strided_view(s_do, do_v, d)
            _to_strided_view(s_dl, dl_v, d)

    tok = pl.BlockSpec((tm, WIDTH), lambda i: (i, 0))
    seg = pl.BlockSpec((tm, WIDTH), lambda i: (i, SEG_ZA // WIDTH))
    views = [_strided_spec(tm, d, WIDTH) for d in DILATIONS]
    return pl.pallas_call(
        body, name="att_merge_bwd", grid=(t // tm,),
        in_specs=[pl.BlockSpec(memory_space=pl.ANY), tok, tok, seg],
        out_specs=[seg] + views + views,
        out_shape=[jax.ShapeDtypeStruct((t, PACKED_WIDTH), MXU_DTYPE)]
        + [jax.ShapeDtypeStruct((t // d, d * WIDTH), MXU_DTYPE) for d in DILATIONS]
        + [jax.ShapeDtypeStruct((t // d, d * WIDTH), F32) for d in DILATIONS],
        scratch_shapes=[_tile_scratch(tm, WIDTH)] * 2,
        input_output_aliases={0: 0},
        compiler_params=_params(("parallel",)),
    )(dproj, d_oz, o, proj)


def _shift_down(x, halo, s):
    if s == 0:
        return x
    xs = pltpu.roll(x, s, 0)
    head = jnp.where(_iota((8, x.shape[1]), 0) < s, pltpu.roll(halo, s, 0), xs[0:8])
    return jnp.concatenate([head, xs[8:]], axis=0)


def _shift_up(x, nxt, s):
    if s == 0:
        return x
    n = x.shape[0]
    xs = pltpu.roll(x, n - s, 0)
    tail = jnp.where(_iota((8, x.shape[1]), 0) >= 8 - s, pltpu.roll(nxt, 8 - s, 0), xs[n - 8:])
    return jnp.concatenate([xs[:n - 8], tail], axis=0)


def _conv_fwd(proj, conv_w, tm=1024):
    t = proj.shape[0]
    cb = SEG_B // WIDTH

    def body(x_ref, halo_ref, w_ref, c_ref):
        halo = jnp.where(pl.program_id(0) > 0, halo_ref[...], 0.0)
        x = x_ref[...]
        w = w_ref[...]
        acc = jnp.zeros((tm, WIDTH), F32)
        for j in range(GDN_CONV):
            acc += _shift_down(x, halo, GDN_CONV - 1 - j) * w[j:j + 1, :]
        c_ref[...] = acc

    return pl.pallas_call(
        body, name="conv_fwd", grid=(t // tm, 3),
        in_specs=[pl.BlockSpec((tm, WIDTH), lambda i, c: (i, cb + c)),
                  pl.BlockSpec((8, WIDTH), lambda i, c: (jnp.maximum(i * (tm // 8) - 1, 0), cb + c)),
                  pl.BlockSpec((GDN_CONV, WIDTH), lambda i, c: (0, c))],
        out_specs=pl.BlockSpec((tm, WIDTH), lambda i, c: (i, c)),
        out_shape=jax.ShapeDtypeStruct((t, 3 * WIDTH), F32),
        compiler_params=_params(("parallel", "parallel")),
    )(proj, proj, conv_w)


def _conv_bwd(dproj, dc, proj, conv_w, tm=1024):
    t = proj.shape[0]
    cb = SEG_B // WIDTH
    nt = t // tm

    def body(dp_ref, dc_ref, dcn_ref, x_ref, halo_ref, w_ref, dx_ref, dw_ref):
        i = pl.program_id(1)
        w = w_ref[...]
        dcn = jnp.where(i < nt - 1, dcn_ref[...], 0.0)
        dcv = dc_ref[...]
        acc = jnp.zeros((tm, WIDTH), F32)
        for j in range(GDN_CONV):
            acc += _shift_up(dcv, dcn, GDN_CONV - 1 - j) * w[j:j + 1, :]
        dx_ref[...] = acc.astype(dx_ref.dtype)
        halo = jnp.where(i > 0, halo_ref[...], 0.0)
        x = x_ref[...]
        row8 = _iota((8, WIDTH), 0)
        part = jnp.zeros((8, WIDTH), F32)
        for j in range(GDN_CONV):
            s = jnp.sum(dcv * _shift_down(x, halo, GDN_CONV - 1 - j), axis=0, keepdims=True)
            part += jnp.where(row8 == j, s, 0.0)

        @pl.when(i == 0)
        def _():
            dw_ref[...] = jnp.zeros_like(dw_ref)

        dw_ref[...] += part

    return pl.pallas_call(
        body, name="conv_bwd", grid=(3, nt),
        in_specs=[pl.BlockSpec(memory_space=pl.ANY),
                  pl.BlockSpec((tm, WIDTH), lambda c, i: (i, c)),
                  pl.BlockSpec((8, WIDTH), lambda c, i: (jnp.minimum((i + 1) * (tm // 8), t // 8 - 1), c)),
                  pl.BlockSpec((tm, WIDTH), lambda c, i: (i, cb + c)),
                  pl.BlockSpec((8, WIDTH), lambda c, i: (jnp.maximum(i * (tm // 8) - 1, 0), cb + c)),
                  pl.BlockSpec((GDN_CONV, WIDTH), lambda c, i: (0, c))],
        out_specs=[pl.BlockSpec((tm, WIDTH), lambda c, i: (i, cb + c)),
                   pl.BlockSpec((8, WIDTH), lambda c, i: (0, c))],
        out_shape=[jax.ShapeDtypeStruct((t, PACKED_WIDTH), MXU_DTYPE), jax.ShapeDtypeStruct((8, 3 * WIDTH), F32)],
        input_output_aliases={0: 0},
        compiler_params=_params(("parallel", "arbitrary")),
    )(dproj, dc, dc, proj, proj, conv_w)


def _chunk_matrices(tm):
    r, c = _iota((tm, tm), 0), _iota((tm, tm), 1)
    same = (r // GDN_CHUNK) == (c // GDN_CHUNK)
    return jnp.where(same & (c <= r), 1.0, 0.0), jnp.where(same, 1.0, 0.0)


def _gdn_gates(ba, a_log, dt_bias):
    al = ba + dt_bias
    return _sigmoid(ba), -jnp.exp(a_log) * _softplus(al), _sigmoid(al)


SCAN_HEADS = (0, 2, 4, 6, 1, 3, 5, 7)


def _head_lane_eye():
    return jnp.where(_iota((HEADS, 128), 1) == _iota((HEADS, 128), 0) + HEADS, 1.0, 0.0)


def _pair_ones():
    return jnp.where(_iota((128, 128), 0) // HEAD_DIM == _iota((128, 128), 1) // HEAD_DIM, 1.0, 0.0).astype(MXU_DTYPE)


def _pair_to_lanes(p, base):
    return jnp.where(_iota((128, 128), 1) == base + 2 * p + _iota((128, 128), 0) // HEAD_DIM, 1.0, 0.0).astype(MXU_DTYPE)


def _pair_sum_lanes(x, sel):
    hi, lo = _split(x)
    return jnp.dot(hi, sel, preferred_element_type=F32) + jnp.dot(lo, sel, preferred_element_type=F32)


def _gdn_prep(conv, proj, a_log, dt_bias, tm=256):
    t = proj.shape[0]
    nc = tm // GDN_CHUNK

    def body(c_ref, ba_ref, al_ref, dt_ref, q_ref, k_ref, v_ref, b_ref, g_ref, gl_ref, grow_ref):
        beta, g, _ = _gdn_gates(ba_ref[:, 0:128], al_ref[...], dt_ref[...])
        lmat, cmat = _chunk_matrices(tm)
        gc = jnp.dot(lmat, g, precision=HIGHEST, preferred_element_type=F32)
        gl = jnp.dot(cmat, g, precision=HIGHEST, preferred_element_type=F32)
        grow = lax.dot_general(_head_lane_eye(), gc, _NT, precision=HIGHEST, preferred_element_type=F32)
        ones = _pair_ones()
        first = _iota((1, 128), 1) < HEAD_DIM

        def spread(x, base, p):
            return jnp.where(first, x[:, base + 2 * p:base + 2 * p + 1], x[:, base + 2 * p + 1:base + 2 * p + 2])

        for p in range(HEADS // 2):
            for seg, ref, scale in ((0, q_ref, HEAD_DIM ** -0.5), (1, k_ref, 1.0), (2, v_ref, None)):
                c = c_ref[:, seg * WIDTH + p * 128:seg * WIDTH + (p + 1) * 128]
                a = c * _sigmoid(c)
                if scale is not None:
                    a = a * (lax.rsqrt(_pair_sum_lanes(a * a, ones) + NORM_EPS) * scale)
                ref[p] = a
            b_ref[p] = spread(beta, 0, p)
            g_ref[p] = spread(gc, HEADS, p)
            gl_ref[p] = spread(gl, HEADS, p)
        for pos, h in enumerate(SCAN_HEADS):
            for cc in range(nc):
                grow_ref[pos, cc] = grow[h:h + 1, cc * GDN_CHUNK:(cc + 1) * GDN_CHUNK]

    hm = pl.BlockSpec((HEADS // 2, tm, 128), lambda i: (0, i, 0))
    small = pl.BlockSpec((1, 128), lambda i: (0, 0))
    hm_shape = jax.ShapeDtypeStruct((HEADS // 2, t, 128), F32)
    return pl.pallas_call(
        body, name="gdn_prep", grid=(t // tm,),
        in_specs=[pl.BlockSpec((tm, 3 * WIDTH), lambda i: (i, 0)),
                  pl.BlockSpec((tm, WIDTH), lambda i: (i, SEG_BA // WIDTH)), small, small],
        out_specs=[hm] * 6 + [pl.BlockSpec((HEADS, nc, 1, GDN_CHUNK), lambda i: (0, i, 0, 0))],
        out_shape=[hm_shape] * 6 + [jax.ShapeDtypeStruct((HEADS, t // GDN_CHUNK, 1, GDN_CHUNK), F32)],
        compiler_params=_params(("parallel",)),
    )(conv, proj, a_log, dt_bias)


def _gdn_prep_bwd(dproj, conv, proj, a_log, dt_bias, dq, dk, dv, db, dg, dgl, dgrow, tm=256):
    t = proj.shape[0]
    nc = tm // GDN_CHUNK

    def body(dp_ref, c_ref, ba_ref, al_ref, dt_ref, dq_ref, dk_ref, dv_ref, db_ref, dg_ref, dgl_ref, dgrow_ref,
             dba_ref, dc_ref, small_ref, row_scr):
        beta, g, sig_al = _gdn_gates(ba_ref[:, 0:128], al_ref[...], dt_ref[...])
        d_beta = jnp.zeros((tm, 128), F32)
        d_gc = jnp.zeros((tm, 128), F32)
        d_gl = jnp.zeros((tm, 128), F32)
        ones = _pair_ones()
        for pos, h in enumerate(SCAN_HEADS):
            for cc in range(nc):
                row_scr[h:h + 1, cc * GDN_CHUNK:(cc + 1) * GDN_CHUNK] = dgrow_ref[pos, cc]
        for p in range(HEADS // 2):
            d_beta += _pair_sum_lanes(db_ref[p], _pair_to_lanes(p, 0))
            d_gc += _pair_sum_lanes(dg_ref[p], _pair_to_lanes(p, HEADS))
            d_gl += _pair_sum_lanes(dgl_ref[p], _pair_to_lanes(p, HEADS))
            for seg, ref, scale in ((0, dq_ref, HEAD_DIM ** -0.5), (1, dk_ref, 1.0), (2, dv_ref, None)):
                cols = slice(seg * WIDTH + p * 128, seg * WIDTH + (p + 1) * 128)
                c = c_ref[:, cols]
                sg = _sigmoid(c)
                da = ref[p]
                if scale is not None:
                    a = c * sg
                    r = lax.rsqrt(_pair_sum_lanes(a * a, ones) + NORM_EPS)
                    da = da * scale
                    da = r * da - a * (r * r * r) * _pair_sum_lanes(da * a, ones)
                dc_ref[:, cols] = da * (sg * (1.0 + c * (1.0 - sg)))
        d_gc += lax.dot_general(row_scr[...], _head_lane_eye(), _TN, precision=HIGHEST, preferred_element_type=F32)
        lmat, cmat = _chunk_matrices(tm)
        d_g = (lax.dot_general(lmat, d_gc, _TN, precision=HIGHEST, preferred_element_type=F32)
               + lax.dot_general(cmat, d_gl, _TN, precision=HIGHEST, preferred_element_type=F32))
        d_al = d_g * (-jnp.exp(al_ref[...])) * sig_al
        d_bl = d_beta * beta * (1.0 - beta)
        dba_ref[...] = jnp.concatenate([d_bl + d_al, jnp.zeros((tm, WIDTH - 128), F32)], axis=1).astype(dba_ref.dtype)
        row8 = _iota((8, 128), 0)
        part = (jnp.where(row8 == 0, jnp.sum(d_g * g, axis=0, keepdims=True), 0.0)
                + jnp.where(row8 == 1, jnp.sum(d_al, axis=0, keepdims=True), 0.0))

        @pl.when(pl.program_id(0) == 0)
        def _():
            small_ref[...] = jnp.zeros_like(small_ref)

        small_ref[...] += part

    hm = pl.BlockSpec((HEADS // 2, tm, 128), lambda i: (0, i, 0))
    small = pl.BlockSpec((1, 128), lambda i: (0, 0))
    seg = pl.BlockSpec((tm, WIDTH), lambda i: (i, SEG_BA // WIDTH))
    return pl.pallas_call(
        body, name="gdn_prep_bwd", grid=(t // tm,),
        in_specs=[pl.BlockSpec(memory_space=pl.ANY), pl.BlockSpec((tm, 3 * WIDTH), lambda i: (i, 0)), seg, small, small]
        + [hm] * 6 + [pl.BlockSpec((HEADS, nc, 1, GDN_CHUNK), lambda i: (0, i, 0, 0))],
        out_specs=[seg, pl.BlockSpec((tm, 3 * WIDTH), lambda i: (i, 0)), pl.BlockSpec((8, 128), lambda i: (0, 0))],
        out_shape=[jax.ShapeDtypeStruct((t, PACKED_WIDTH), MXU_DTYPE), jax.ShapeDtypeStruct((t, 3 * WIDTH), F32),
                   jax.ShapeDtypeStruct((8, 128), F32)],
        scratch_shapes=[pltpu.VMEM((HEADS, tm), F32)],
        input_output_aliases={0: 0},
        compiler_params=_params(("arbitrary",)),
    )(dproj, conv, proj, a_log, dt_bias, dq, dk, dv, db, dg, dgl, dgrow)


_BNN = (((2,), (1,)), ((0,), (0,)))
_BNT = (((2,), (2,)), ((0,), (0,)))
_BTN = (((1,), (1,)), ((0,), (0,)))


@jax.custom_vjp
def _MM_NN(a, b):
    return _mxu(a, b, _BNN)


@jax.custom_vjp
def _MM_NT(a, b):
    return _mxu(a, b, _BNT)


@jax.custom_vjp
def _MM_TN(a, b):
    return _mxu(a, b, _BTN)


_MM_NN.defvjp(lambda a, b: (_mxu(a, b, _BNN), (a, b)), lambda r, g: (_mxu(g, r[1], _BNT), _mxu(r[0], g, _BTN)))
_MM_NT.defvjp(lambda a, b: (_mxu(a, b, _BNT), (a, b)), lambda r, g: (_mxu(g, r[1], _BNN), _mxu(g, r[0], _BTN)))
_MM_TN.defvjp(lambda a, b: (_mxu(a, b, _BTN), (a, b)), lambda r, g: (_mxu(r[1], g, _BNT), _mxu(r[0], g, _BNN)))


def _split(a):
    hi = a.astype(MXU_DTYPE)
    return hi, (a - hi.astype(F32)).astype(MXU_DTYPE)


def _dot3(a, b, dims):
    (ah, al), (bh, bl) = a, b
    (ca,), (cb,) = dims[0]
    return lax.dot_general(jnp.concatenate([ah, ah, al], axis=ca), jnp.concatenate([bh, bl, bh], axis=cb), dims,
                           preferred_element_type=F32)


def _unit_lower_inverse(a):
    c = GDN_CHUNK
    eye = jnp.where(_iota((c, c), 0) == _iota((c, c), 1), 1.0, 0.0)
    x = eye - a
    p = a
    for _ in range(5):
        ps = _split(p)
        p = _dot3(ps, ps, _BNN)
        x = x + _dot3(_split(x), _split(p), _BNN)
    return x


@jax.custom_vjp
def _SAVED_INVERSE(a, t_inv):
    return t_inv


def _saved_inverse_bwd(t_inv, g):
    ts = _split(t_inv)
    return -_dot3(ts, _split(_dot3(_split(g), ts, _BNT)), _BTN), jnp.zeros_like(t_inv)


_SAVED_INVERSE.defvjp(lambda a, t_inv: (t_inv, t_inv), _saved_inverse_bwd)


def _gdn_chunk(q, k, v, beta, g1, g2, gl, state, t_inv=None):
    c = GDN_CHUNK
    if t_inv is None:
        _mm_nn, _mm_nt, _mm_tn = (functools.partial(_mxu, dims=dd) for dd in (_BNN, _BNT, _BTN))
    else:
        _mm_nn, _mm_nt, _mm_tn = _MM_NN, _MM_NT, _MM_TN
    row, col = _iota((c, c), 0), _iota((c, c), 1)
    incl, strict = row >= col, row > col
    decay = jnp.where(incl, jnp.exp(jnp.where(incl, g1 - g2, 0.0)), 0.0)
    eg = jnp.exp(g1)
    kb = k * beta
    a = _mm_nt(kb, k) * jnp.where(strict, decay, 0.0)
    inv = _unit_lower_inverse(a) if t_inv is None else _SAVED_INVERSE(a, t_inv)
    u = _mm_nn(inv, v * beta)
    w = _mm_nn(inv, kb * eg)
    attn = _mm_nt(q, k) * decay
    v_new = u - _mm_nn(w, state)
    o = _mm_nn(q * eg, state) + _mm_nn(attn, v_new)
    new_state = state * jnp.exp(gl) + _mm_tn(k * jnp.exp(gl - g1), v_new)
    return (o, new_state, inv) if t_inv is None else (o, new_state)


def _unpair(x):
    return jnp.concatenate([x[..., :HEAD_DIM], x[..., HEAD_DIM:]], axis=0)


def _gdn_fwd(q, k, v, beta, g, gl, grow, cpb=4):
    t = q.shape[1]
    rows = cpb * GDN_CHUNK
    lo, hi = slice(0, HEAD_DIM), slice(HEAD_DIM, 2 * HEAD_DIM)

    def body(q_ref, k_ref, v_ref, b_ref, g_ref, gl_ref, grow_ref, o_ref, st_ref, inv_ref, state):
        @pl.when(pl.program_id(0) == 0)
        def _():
            state[...] = jnp.zeros_like(state)

        s = state[...]
        for cc in range(cpb):
            sl = slice(cc * GDN_CHUNK, (cc + 1) * GDN_CHUNK)
            st_ref[:, cc, :, lo], st_ref[:, cc, :, hi] = s[:HEADS // 2], s[HEADS // 2:]
            g2 = jnp.broadcast_to(grow_ref[:, cc], (HEADS, GDN_CHUNK, GDN_CHUNK))
            o, s, inv = _gdn_chunk(*[_unpair(r[:, sl, :]) for r in (q_ref, k_ref, v_ref, b_ref, g_ref)], g2,
                                   _unpair(gl_ref[:, sl, :]), s)
            o_ref[:, sl, lo], o_ref[:, sl, hi] = o[:HEADS // 2], o[HEADS // 2:]
            inv_ref[:, cc, :, lo], inv_ref[:, cc, :, hi] = inv[:HEADS // 2], inv[HEADS // 2:]
        state[...] = s

    hm = pl.BlockSpec((HEADS // 2, rows, 128), lambda i: (0, i, 0))
    per_chunk = pl.BlockSpec((HEADS // 2, cpb, GDN_CHUNK, 128), lambda i: (0, i, 0, 0))
    chunk_shape = jax.ShapeDtypeStruct((HEADS // 2, t // GDN_CHUNK, GDN_CHUNK, 128), F32)
    return pl.pallas_call(
        body, name="gdn_fwd", grid=(t // rows,),
        in_specs=[hm] * 6 + [pl.BlockSpec((HEADS, cpb, 1, GDN_CHUNK), lambda i: (0, i, 0, 0))],
        out_specs=[hm, per_chunk, per_chunk],
        out_shape=[jax.ShapeDtypeStruct((HEADS // 2, t, 128), F32), chunk_shape, chunk_shape],
        scratch_shapes=[pltpu.VMEM((HEADS, GDN_CHUNK, HEAD_DIM), F32)],
        compiler_params=_params(("arbitrary",)),
    )(q, k, v, beta, g, gl, grow)


def _gdn_bwd(q, k, v, beta, g, gl, grow, states, invs, do, cpb=2):
    t = q.shape[1]
    rows = cpb * GDN_CHUNK
    nsteps = t // rows
    lo, hi = slice(0, HEAD_DIM), slice(HEAD_DIM, 2 * HEAD_DIM)

    def body(q_ref, k_ref, v_ref, b_ref, g_ref, gl_ref, grow_ref, st_ref, inv_ref, do_ref,
             dq_ref, dk_ref, dv_ref, db_ref, dg_ref, dgl_ref, dgrow_ref, dstate):
        @pl.when(pl.program_id(0) == 0)
        def _():
            dstate[...] = jnp.zeros_like(dstate)

        ds = dstate[...]
        for cc in reversed(range(cpb)):
            sl = slice(cc * GDN_CHUNK, (cc + 1) * GDN_CHUNK)
            g2 = jnp.broadcast_to(grow_ref[:, cc], (HEADS, GDN_CHUNK, GDN_CHUNK))
            _, vjp = jax.vjp(_gdn_chunk, *[_unpair(r[:, sl, :]) for r in (q_ref, k_ref, v_ref, b_ref, g_ref)], g2,
                             _unpair(gl_ref[:, sl, :]), _unpair(st_ref[:, cc]), _unpair(inv_ref[:, cc]))
            gq, gk, gv, gb, gg1, gg2, ggl, ds, _ = vjp((_unpair(do_ref[:, sl, :]), ds))
            for ref, val in ((dq_ref, gq), (dk_ref, gk), (dv_ref, gv), (db_ref, gb), (dg_ref, gg1), (dgl_ref, ggl)):
                ref[:, sl, lo], ref[:, sl, hi] = val[:HEADS // 2], val[HEADS // 2:]
            dgrow_ref[:, cc] = jnp.sum(gg2, axis=1, keepdims=True)
        dstate[...] = ds

    hm = pl.BlockSpec((HEADS // 2, rows, 128), lambda i: (0, nsteps - 1 - i, 0))
    rowspec = pl.BlockSpec((HEADS, cpb, 1, GDN_CHUNK), lambda i: (0, nsteps - 1 - i, 0, 0))
    per_chunk = pl.BlockSpec((HEADS // 2, cpb, GDN_CHUNK, 128), lambda i: (0, nsteps - 1 - i, 0, 0))
    hm_shape = jax.ShapeDtypeStruct((HEADS // 2, t, 128), F32)
    return pl.pallas_call(
        body, name="gdn_bwd", grid=(nsteps,),
        in_specs=[hm] * 6 + [rowspec, per_chunk, per_chunk, hm],
        out_specs=[hm] * 6 + [rowspec],
        out_shape=[hm_shape] * 6 + [jax.ShapeDtypeStruct((HEADS, t // GDN_CHUNK, 1, GDN_CHUNK), F32)],
        scratch_shapes=[pltpu.VMEM((HEADS, GDN_CHUNK, HEAD_DIM), F32)],
        compiler_params=_params(("arbitrary",)),
    )(q, k, v, beta, g, gl, grow, states, invs, do)


def _gdn_out(o_hm, gdn_norm_w, proj, tm=512):
    t = proj.shape[0]

    def body(o_ref, w_ref, z_ref, oz_ref):
        w = w_ref[...]
        ones = _pair_ones()
        for p in range(HEADS // 2):
            cols = slice(p * 128, (p + 1) * 128)
            o = o_ref[p]
            z = z_ref[:, cols]
            r = lax.rsqrt(_pair_sum_lanes(o * o, ones) * (1.0 / HEAD_DIM) + NORM_EPS)
            oz_ref[:, cols] = (o * r * w * (z * _sigmoid(z))).astype(oz_ref.dtype)

    tok = pl.BlockSpec((tm, WIDTH), lambda i: (i, 0))
    return pl.pallas_call(
        body, name="gdn_out", grid=(t // tm,),
        in_specs=[pl.BlockSpec((HEADS // 2, tm, 128), lambda i: (0, i, 0)), pl.BlockSpec((1, 128), lambda i: (0, 0)),
                  pl.BlockSpec((tm, WIDTH), lambda i: (i, SEG_ZB // WIDTH))],
        out_specs=tok, out_shape=jax.ShapeDtypeStruct((t, WIDTH), MXU_DTYPE),
        compiler_params=_params(("parallel",)),
    )(o_hm, jnp.tile(gdn_norm_w, (1, 2)), proj)


def _gdn_out_bwd(dproj, d_oz, o_hm, gdn_norm_w, proj, tm=512):
    t = proj.shape[0]

    def body(dp_ref, doz_ref, o_ref, w_ref, z_ref, dz_ref, do_ref, dw_ref):
        w = w_ref[...]
        ones = _pair_ones()
        dw = jnp.zeros((1, 128), F32)
        for p in range(HEADS // 2):
            cols = slice(p * 128, (p + 1) * 128)
            o = o_ref[p]
            z, g = z_ref[:, cols], doz_ref[:, cols]
            sg = _sigmoid(z)
            r = lax.rsqrt(_pair_sum_lanes(o * o, ones) * (1.0 / HEAD_DIM) + NORM_EPS)
            dz_ref[:, cols] = (g * (o * r * w) * (sg * (1.0 + z * (1.0 - sg)))).astype(dz_ref.dtype)
            dn = g * (z * sg)
            dw += jnp.sum(dn * o * r, axis=0, keepdims=True)
            dnw = dn * w
            do_ref[p] = r * dnw - o * (r * r * r) * (_pair_sum_lanes(dnw * o, ones) * (1.0 / HEAD_DIM))

        @pl.when(pl.program_id(0) == 0)
        def _():
            dw_ref[...] = jnp.zeros_like(dw_ref)

        dw_ref[...] += jnp.where(_iota((8, 128), 0) == 0, dw, 0.0)

    tok = pl.BlockSpec((tm, WIDTH), lambda i: (i, 0))
    seg = pl.BlockSpec((tm, WIDTH), lambda i: (i, SEG_ZB // WIDTH))
    hm = pl.BlockSpec((HEADS // 2, tm, 128), lambda i: (0, i, 0))
    dz, do, dw = pl.pallas_call(
        body, name="gdn_out_bwd", grid=(t // tm,),
        in_specs=[pl.BlockSpec(memory_space=pl.ANY), tok, hm, pl.BlockSpec((1, 128), lambda i: (0, 0)), seg],
        out_specs=[seg, hm, pl.BlockSpec((8, 128), lambda i: (0, 0))],
        out_shape=[jax.ShapeDtypeStruct((t, PACKED_WIDTH), MXU_DTYPE), jax.ShapeDtypeStruct((HEADS // 2, t, 128), F32),
                   jax.ShapeDtypeStruct((8, 128), F32)],
        input_output_aliases={0: 0},
        compiler_params=_params(("arbitrary",)),
    )(dproj, d_oz, o_hm, jnp.tile(gdn_norm_w, (1, 2)), proj)
    return dz, do, dw[:, :HEAD_DIM] + dw[:, HEAD_DIM:]


def _merge(y_a, y_b, proj, tm=1024):
    t = proj.shape[0]

    def body(ya_ref, yb_ref, ga_ref, gb_ref, m_ref):
        m_ref[...] = (_sigmoid(ga_ref[...]) * ya_ref[...] + _sigmoid(gb_ref[...]) * yb_ref[...]).astype(m_ref.dtype)

    half = pl.BlockSpec((tm, WIDTH), lambda i, c: (i, c))
    return pl.pallas_call(
        body, name="merge", grid=(t // tm, 2),
        in_specs=[half, half, pl.BlockSpec((tm, WIDTH), lambda i, c: (i, SEG_GA // WIDTH + c)),
                  pl.BlockSpec((tm, WIDTH), lambda i, c: (i, SEG_GB // WIDTH + c))],
        out_specs=half, out_shape=jax.ShapeDtypeStruct((t, D_MODEL), MXU_DTYPE),
        compiler_params=_params(("parallel", "parallel")),
    )(y_a, y_b, proj, proj)


def _merge_bwd(dproj, d_m, y, proj, seg, name, tm=1024):
    t = proj.shape[0]

    def body(*refs):
        dm_ref, y_ref, g_ref, dg_ref, dy_ref = refs[-5:]
        dm = dm_ref[...]
        s = _sigmoid(g_ref[...])
        dy_ref[...] = (dm * s).astype(dy_ref.dtype)
        dg_ref[...] = (dm * y_ref[...] * s * (1.0 - s)).astype(dg_ref.dtype)

    half = pl.BlockSpec((tm, WIDTH), lambda i, c: (i, c))
    gate = pl.BlockSpec((tm, WIDTH), lambda i, c: (i, seg // WIDTH + c))
    specs, args, aliases = [half, half, gate], [d_m, y, proj], {}
    if dproj is not None:
        specs, args, aliases = [pl.BlockSpec(memory_space=pl.ANY)] + specs, [dproj] + args, {0: 0}
    return pl.pallas_call(
        body, name=name, grid=(t // tm, 2), in_specs=specs, out_specs=[gate, half],
        out_shape=[jax.ShapeDtypeStruct((t, PACKED_WIDTH), MXU_DTYPE), jax.ShapeDtypeStruct((t, D_MODEL), MXU_DTYPE)],
        input_output_aliases=aliases,
        compiler_params=_params(("parallel", "parallel")),
    )(*args)


def _out_tail(merged, w_out, x, final_w, target, tm=1024):
    t = x.shape[0]
    tm = min(tm, t)

    def body(m_ref, wo_ref, x_ref, w_ref, t_ref, dxm_ref, dx_ref, loss_ref, dw_ref):
        x2 = x_ref[...] + jnp.dot(m_ref[...], wo_ref[...], preferred_element_type=F32)
        w = w_ref[...]
        r = lax.rsqrt(jnp.mean(x2 * x2, axis=-1, keepdims=True) + NORM_EPS)
        xn = x2 * r
        err = xn * w - t_ref[...]
        dy = err * (1.0 / D_MODEL)
        dyw = dy * w
        dx2 = r * dyw - x2 * (r * r * r) * jnp.mean(dyw * x2, axis=-1, keepdims=True)
        dx_ref[...] = dx2
        dxm_ref[...] = dx2.astype(dxm_ref.dtype)
        loss = 0.5 * jnp.sum(jnp.sum(err * err, axis=-1, keepdims=True) * (1.0 / D_MODEL), axis=0, keepdims=True)
        onehot = jnp.where((_iota((8, 128), 0) == 0) & (_iota((8, 128), 1) == 0), 1.0, 0.0)

        @pl.when(pl.program_id(0) == 0)
        def _():
            loss_ref[...] = jnp.zeros_like(loss_ref)
            dw_ref[...] = jnp.zeros_like(dw_ref)

        loss_ref[...] += loss * onehot
        dw_ref[...] += jnp.where(_iota((8, D_MODEL), 0) == 0, jnp.sum(dy * xn, axis=0, keepdims=True), 0.0)

    tok = pl.BlockSpec((tm, D_MODEL), lambda i: (i, 0))
    full = pl.BlockSpec((D_MODEL, D_MODEL), lambda i: (0, 0))
    return pl.pallas_call(
        body, name="out_tail", grid=(t // tm,),
        in_specs=[tok, full, tok, pl.BlockSpec((1, D_MODEL), lambda i: (0, 0)), tok],
        out_specs=[tok, tok, pl.BlockSpec((8, 128), lambda i: (0, 0)), pl.BlockSpec((8, D_MODEL), lambda i: (0, 0))],
        out_shape=[jax.ShapeDtypeStruct((t, D_MODEL), MXU_DTYPE), jax.ShapeDtypeStruct((t, D_MODEL), F32),
                   jax.ShapeDtypeStruct((8, 128), F32), jax.ShapeDtypeStruct((8, D_MODEL), F32)],
        compiler_params=_params(("arbitrary",)),
    )(merged, w_out, x, final_w, target)


def _dh_norm_bwd(dproj, wp, x, norm_w, dx2, after, tm=1024, tk=PACKED_WIDTH // 4):
    t = x.shape[0]
    tm = min(tm, t)
    nk = PACKED_WIDTH // tk

    def body(dp_ref, wp_ref, x_ref, w_ref, dx2_ref, after_ref, dx_ref, dw_ref, acc):
        kk = pl.program_id(1)
        part = _mxu(dp_ref[...], wp_ref[...], _NT)

        @pl.when(kk == 0)
        def _():
            acc[...] = part

        @pl.when(kk > 0)
        def _():
            acc[...] += part

        @pl.when((kk == 0) & (pl.program_id(0) == 0))
        def _():
            dw_ref[...] = jnp.zeros_like(dw_ref)

        @pl.when(kk == nk - 1)
        def _():
            xf, w, dh_ = x_ref[...], w_ref[...], acc[...]
            r = lax.rsqrt(jnp.mean(xf * xf, axis=-1, keepdims=True) + NORM_EPS)
            dhw = dh_ * w
            dx_ref[...] = dx2_ref[...] + r * dhw - xf * (r * r * r) * jnp.mean(dhw * xf, axis=-1, keepdims=True)
            dw_ref[...] += jnp.where(_iota((8, D_MODEL), 0) == 0, jnp.sum(dh_ * xf * r, axis=0, keepdims=True), 0.0)

    tok = pl.BlockSpec((tm, D_MODEL), lambda i, kk: (i, 0))
    return pl.pallas_call(
        body, name="dh_norm_bwd", grid=(t // tm, nk),
        in_specs=[pl.BlockSpec((tm, tk), lambda i, kk: (i, kk)), pl.BlockSpec((D_MODEL, tk), lambda i, kk: (0, kk)),
                  tok, pl.BlockSpec((1, D_MODEL), lambda i, kk: (0, 0)), tok, pl.BlockSpec(memory_space=pl.ANY)],
        out_specs=[tok, pl.BlockSpec((8, D_MODEL), lambda i, kk: (0, 0))],
        out_shape=[jax.ShapeDtypeStruct((t, D_MODEL), F32), jax.ShapeDtypeStruct((8, D_MODEL), F32)],
        scratch_shapes=[pltpu.VMEM((tm, D_MODEL), F32)],
        compiler_params=_params(("arbitrary", "arbitrary")),
    )(dproj, wp, x, norm_w, dx2, after)


def _local_step(x, target, norm_w, wp, conv_w, a_log, dt_bias, gdn_norm_w, w_up_a, w_up_b, w_out, final_w, start_reduce):
    t = x.shape[0]
    tables = _rope_tables(t)
    a_log = jnp.pad(a_log, ((0, 0), (HEADS, 128 - 2 * HEADS)))
    dt_bias = jnp.pad(dt_bias, ((0, 0), (HEADS, 128 - 2 * HEADS)))

    proj, h_t = _norm_proj(x, norm_w, wp)
    qkvs = _rope_fwd(proj, tables)
    outs, lses = zip(*[_att_fwd(qkvs[gi], d, f"att_fwd{gi}") for gi, d in enumerate(DILATIONS)])
    oz_a, o_a, *lse_views = _att_merge(outs, lses, proj)
    conv = _conv_fwd(proj, conv_w)
    gq, gk, gv, gb, gg, ggl, grow = _gdn_prep(conv, proj, a_log, dt_bias)
    o_b, states, invs = _gdn_fwd(gq, gk, gv, gb, gg, ggl, grow)
    oz_b = _gdn_out(o_b, gdn_norm_w, proj)
    big = dict(tm=1024, tn=1024, tk=1024)
    y_a = _matmul(oz_a, w_up_a, "nn", "up_a", **big)
    y_b = _matmul(oz_b, w_up_b, "nn", "up_b", **big)
    merged = _merge(y_a, y_b, proj)
    dx2_m, dx2, loss_blk, d_final = _out_tail(merged, w_out, x, final_w, target)

    d_wout = _matmul(merged, dx2_m, "tn", "d_w_out", **big)
    d_m = _matmul(dx2_m, w_out, "nt", "d_merged", **big)
    dproj, dy_a = _merge_bwd(None, d_m, y_a, proj, SEG_GA, "merge_bwd_a")
    dproj, dy_b = _merge_bwd(dproj, d_m, y_b, proj, SEG_GB, "merge_bwd_b")
    d_wua = _matmul(oz_a, dy_a, "tn", "d_w_up_a", **big)
    d_wub = _matmul(oz_b, dy_b, "tn", "d_w_up_b", **big)
    d_oz_a = _matmul(dy_a, w_up_a, "nt", "d_oz_a", **big)
    d_oz_b = _matmul(dy_b, w_up_b, "nt", "d_oz_b", **big)
    dproj, *views = _att_merge_bwd(dproj, d_oz_a, o_a, proj)
    do_views, delta_views = views[:3], views[3:]
    dqs, dkvs = zip(*[_att_bwd(qkvs[gi], do_views[gi], lse_views[gi], delta_views[gi], d, f"att_bwd{gi}")
                      for gi, d in enumerate(DILATIONS)])
    dproj = _rope_bwd(dproj, dqs, dkvs, tables)
    dproj, do_b, d_gnw = _gdn_out_bwd(dproj, d_oz_b, o_b, gdn_norm_w, proj)
    dgq, dgk, dgv, dgb, dgg, dggl, dgrow = _gdn_bwd(gq, gk, gv, gb, gg, ggl, grow, states, invs, do_b)
    dproj, dconv, d_small = _gdn_prep_bwd(dproj, conv, proj, a_log, dt_bias, dgq, dgk, dgv, dgb, dgg, dggl, dgrow)
    dproj, d_convw = _conv_bwd(dproj, dconv, proj, conv_w)
    d_wp = _matmul(h_t, dproj, "nn", "d_w_in", tm=1024, tn=PACKED_WIDTH // 4, tk=1024)
    in_flight, token = start_reduce(d_wp, d_wua, d_wub, d_wout, d_convw[0:GDN_CONV])
    grad_x, d_norm = _dh_norm_bwd(dproj, wp, x, norm_w, dx2, token)
    return dict(loss=loss_blk, grad_x=grad_x, norm_w=d_norm[0:1], in_flight=in_flight,
                a_log=d_small[0:1, HEADS:2 * HEADS], dt_bias=d_small[1:2, HEADS:2 * HEADS], gdn_norm_w=d_gnw[0:1],
                final_norm_w=d_final[0:1])


SHARDS = 4
W_IN_SHARD = IN_WIDTH // SHARDS
ROWS_UP = WIDTH * (D_MODEL // SHARDS) // 128
ROWS_OUT = (D_MODEL // SHARDS) * D_MODEL // 128
CONV_SHARD = 3 * WIDTH // SHARDS
ROWS_CONV = 16
SLAB_ROWS = 2 * ROWS_UP + ROWS_OUT + 2 * ROWS_CONV
HALF_ROWS = SLAB_ROWS // 2
BIG_HALF = (D_MODEL // 2, W_IN_SHARD)
SMALL_HALF = (HALF_ROWS, 128)
MESH = pl.DeviceIdType.MESH
ANY = pl.BlockSpec(memory_space=pl.ANY)


def _pad_rows(a, rows):
    return jnp.pad(a, ((0, rows - a.shape[0]), (0, 0)))


def _pack_slab(w_up_a, w_up_b, w_out, conv, conv_lo):
    parts = [w_up_a.reshape(ROWS_UP, 128), w_up_b.reshape(ROWS_UP, 128), w_out.reshape(ROWS_OUT, 128),
             _pad_rows(conv.reshape(-1, 128), ROWS_CONV), _pad_rows(conv_lo.reshape(-1, 128), ROWS_CONV)]
    return jnp.concatenate(parts, axis=0).reshape(2, *SMALL_HALF)


def _unpack_slab(slab):
    slab = slab.reshape(SLAB_ROWS, 128)
    r0 = 0
    out = []
    for rows, shape in ((ROWS_UP, (WIDTH, D_MODEL // SHARDS)), (ROWS_UP, (WIDTH, D_MODEL // SHARDS)),
                        (ROWS_OUT, (D_MODEL // SHARDS, D_MODEL)), (ROWS_CONV, None), (ROWS_CONV, None)):
        part = slab[r0:r0 + rows]
        out.append(part[:GDN_CONV * CONV_SHARD // 128].reshape(GDN_CONV, CONV_SHARD) if shape is None else part.reshape(shape))
        r0 += rows
    return out


def _mesh_position():
    x, y, c = lax.axis_index("x"), lax.axis_index("y"), lax.axis_index("c")
    return x, y, c, [(1 - x, y), (x, 1 - y), (1 - x, 1 - y)]


def _gather_weights(shards):
    n = len(shards)

    def body(*refs):
        in_refs, out_refs, (send_sems, recv_sems) = refs[:n], refs[n:2 * n], refs[2 * n:]
        x, y, c, chips = _mesh_position()

        def half(a, chip, which):
            return out_refs[a].at[2 * chip[0] + chip[1], which]

        def copy(k, src, dst, to):
            return pltpu.make_async_remote_copy(src_ref=src, dst_ref=dst, send_sem=send_sems.at[k], recv_sem=recv_sems.at[k],
                                                device_id=to, device_id_type=MESH)

        pairs = [(a, j, chip) for a in range(n) for j, chip in enumerate(chips)]
        first = [copy(6 * a + j, in_refs[a].at[c], half(a, (x, y), c), (*chip, c)) for a, j, chip in pairs]
        for cp in first:
            cp.start()
        passed = [copy(6 * a + 3 + j, half(a, chip, c), half(a, chip, c), (x, y, 1 - c)) for a, j, chip in pairs]
        for i, (a, j, chip) in enumerate(pairs):
            copy(6 * a + j, half(a, chip, c), half(a, chip, c), (x, y, c)).wait_recv()
            passed[i].start()
        for a, j, chip in pairs:
            copy(6 * a + 3 + j, half(a, chip, 1 - c), half(a, chip, 1 - c), (x, y, c)).wait_recv()
        for cp in first + passed:
            cp.wait_send()

    return pl.pallas_call(
        body, name="gather_weights", in_specs=[ANY] * n, out_specs=[ANY] * n,
        out_shape=[jax.ShapeDtypeStruct((SHARDS, *s.shape), s.dtype) for s in shards],
        scratch_shapes=[pltpu.SemaphoreType.DMA((6 * n,)), pltpu.SemaphoreType.DMA((6 * n,))],
    )(*shards)


def _exchange_halves(grads):
    n = len(grads)

    def body(*refs):
        g_refs, out_refs, (send_sems, recv_sems) = refs[:n], refs[n:2 * n], refs[2 * n:]
        x, y, c, _ = _mesh_position()
        copies = [pltpu.make_async_remote_copy(src_ref=g_refs[a].at[s, 1 - c], dst_ref=out_refs[a].at[s],
                                               send_sem=send_sems.at[SHARDS * a + s], recv_sem=recv_sems.at[SHARDS * a + s],
                                               device_id=(x, y, 1 - c), device_id_type=MESH)
                  for a in range(n) for s in range(SHARDS)]
        for cp in copies:
            cp.start()
        for cp in copies:
            cp.wait()

    return pl.pallas_call(
        body, name="exchange_halves", in_specs=[ANY] * n, out_specs=[ANY] * n,
        out_shape=[jax.ShapeDtypeStruct((SHARDS, *g.shape[2:]), F32) for g in grads],
        scratch_shapes=[pltpu.SemaphoreType.DMA((SHARDS * n,)), pltpu.SemaphoreType.DMA((SHARDS * n,))],
    )(*grads)


def _pair_sum(grads, recv, blk, name):
    _, _, rows, cols = grads.shape

    def body(c_ref, g_ref, r_ref, o_ref):
        o_ref[...] = (g_ref[0] + r_ref[...]).astype(o_ref.dtype)

    spec = pl.BlockSpec((1, blk, cols), lambda s, i, c_ref: (s, i, 0))
    return pl.pallas_call(
        body, name=name,
        grid_spec=pltpu.PrefetchScalarGridSpec(
            num_scalar_prefetch=1, grid=(SHARDS, rows // blk),
            in_specs=[pl.BlockSpec((1, 1, blk, cols), lambda s, i, c_ref: (s, c_ref[0], i, 0)), spec],
            out_specs=spec),
        out_shape=jax.ShapeDtypeStruct((SHARDS, rows, cols), MXU_DTYPE),
        compiler_params=_params(("parallel", "parallel")),
    )(lax.axis_index("c").astype(jnp.int32).reshape(1), grads, recv)


_HBM = pl.BlockSpec(memory_space=pltpu.HBM)
_SEM = pl.BlockSpec(memory_space=pltpu.SEMAPHORE)
_DATAFLOW = pltpu.SideEffectType.DATAFLOW_SIDE_EFFECTING


def _scatter_copies(p_refs, l_refs, send_sems, recv_sems):
    x, y, c, chips = _mesh_position()
    return [pltpu.make_async_remote_copy(src_ref=p_refs[a].at[2 * chip[0] + chip[1]], dst_ref=l_refs[a].at[j],
                                         send_sem=send_sems.at[3 * a + j], recv_sem=recv_sems.at[3 * a + j],
                                         device_id=(*chip, c), device_id_type=MESH)
            for a in range(len(p_refs)) for j, chip in enumerate(chips)]


def _scatter_start(pairs):
    n = len(pairs)
    lands = [lax.empty((3, *p.shape[1:]), p.dtype) for p in pairs]

    def body(*refs):
        p_refs, l_refs, send_sems, recv_sems, token = refs[:n], refs[n:2 * n], refs[2 * n], refs[2 * n + 1], refs[-1]
        for cp in _scatter_copies(p_refs, l_refs, send_sems, recv_sems):
            cp.start()
        token[...] = jnp.zeros_like(token)

    operands = [pltpu.with_memory_space_constraint(a, pltpu.HBM) for a in (*pairs, *lands)]
    return pl.pallas_call(
        body, name="scatter_start", in_specs=[_HBM] * (2 * n),
        out_shape=(pltpu.SemaphoreType.DMA((3 * n,)), pltpu.SemaphoreType.DMA((3 * n,)),
                   *[pltpu.HBM(a.shape, a.dtype) for a in operands], jax.ShapeDtypeStruct((8, 128), F32)),
        out_specs=(_SEM, _SEM, *[_HBM] * (2 * n), pl.BlockSpec(memory_space=pltpu.VMEM)),
        input_output_aliases={i: 2 + i for i in range(2 * n)},
        compiler_params=pltpu.CompilerParams(has_side_effects=_DATAFLOW),
    )(*operands)


def _scatter_wait(send_sems, recv_sems, passed, after):
    n = len(passed) // 2

    def body(*refs):
        p_refs, l_refs, send_s, recv_s = refs[:n], refs[n:2 * n], refs[2 * n], refs[2 * n + 1]
        for cp in _scatter_copies(p_refs, l_refs, send_s, recv_s):
            cp.wait_send()
            cp.wait_recv()

    return pl.pallas_call(
        body, name="scatter_wait", in_specs=[_HBM] * (2 * n) + [_SEM, _SEM, ANY],
        out_shape=[pltpu.HBM(a.shape, a.dtype) for a in passed], out_specs=[_HBM] * (2 * n),
        input_output_aliases={i: i for i in range(2 * n)},
        compiler_params=pltpu.CompilerParams(has_side_effects=_DATAFLOW),
    )(*passed, send_sems, recv_sems, after)


def _chip_sum(pairs, recv, blk, name):
    _, rows, cols = pairs.shape

    def body(pos_ref, p_ref, r_ref, o_ref):
        o_ref[0] = ((p_ref[0].astype(F32) + r_ref[0].astype(F32)) + r_ref[1].astype(F32)) + r_ref[2].astype(F32)

    pos = jnp.stack([2 * lax.axis_index("x") + lax.axis_index("y"), lax.axis_index("c")]).astype(jnp.int32)
    return pl.pallas_call(
        body, name=name,
        grid_spec=pltpu.PrefetchScalarGridSpec(
            num_scalar_prefetch=1, grid=(rows // blk,),
            in_specs=[pl.BlockSpec((1, blk, cols), lambda i, pos_ref: (pos_ref[0], i, 0)),
                      pl.BlockSpec((3, blk, cols), lambda i, pos_ref: (0, i, 0))],
            out_specs=pl.BlockSpec((1, blk, cols), lambda i, pos_ref: (pos_ref[1], i, 0))),
        out_shape=jax.ShapeDtypeStruct((2, rows, cols), F32),
        compiler_params=_params(("parallel",)),
    )(pos, pairs, recv)


def _share_total(totals):
    n = len(totals)

    def body(*refs):
        t_refs, out_refs, (send_sems, recv_sems) = refs[:n], refs[n:2 * n], refs[2 * n:]
        x, y, c, _ = _mesh_position()
        copies = [pltpu.make_async_remote_copy(src_ref=t_refs[a].at[c], dst_ref=out_refs[a].at[c], send_sem=send_sems.at[a],
                                               recv_sem=recv_sems.at[a], device_id=(x, y, 1 - c), device_id_type=MESH)
                  for a in range(n)]
        for cp in copies:
            cp.start()
        for a in range(n):
            other = out_refs[a].at[1 - c]
            pltpu.make_async_remote_copy(src_ref=other, dst_ref=other, send_sem=send_sems.at[a], recv_sem=recv_sems.at[a],
                                         device_id=(x, y, c), device_id_type=MESH).wait_recv()
        for cp in copies:
            cp.wait_send()

    return pl.pallas_call(
        body, name="share_total", in_specs=[ANY] * n, out_specs=[ANY] * n,
        out_shape=[jax.ShapeDtypeStruct(t.shape, F32) for t in totals],
        scratch_shapes=[pltpu.SemaphoreType.DMA((n,)), pltpu.SemaphoreType.DMA((n,))],
        input_output_aliases={a: a for a in range(n)},
    )(*totals)


def _allreduce_small(block):
    def body(b_ref, out_ref, gath, send_sems, recv_sems):
        x, y, c, _ = _mesh_position()
        me = 4 * x + 2 * y + c
        gath[me] = b_ref[...]
        copies = []
        for k in range(1, 8):
            peer = (x ^ (k >> 2), y ^ ((k >> 1) & 1), c ^ (k & 1))
            copies.append(pltpu.make_async_remote_copy(src_ref=b_ref, dst_ref=gath.at[me], send_sem=send_sems.at[k - 1],
                                                       recv_sem=recv_sems.at[k - 1], device_id=peer, device_id_type=MESH))
        for cp in copies:
            cp.start()
        for k in range(1, 8):
            src = 4 * (x ^ (k >> 2)) + 2 * (y ^ ((k >> 1) & 1)) + (c ^ (k & 1))
            pltpu.make_async_remote_copy(src_ref=b_ref, dst_ref=gath.at[src], send_sem=send_sems.at[k - 1],
                                         recv_sem=recv_sems.at[k - 1], device_id=(x, y, c), device_id_type=MESH).wait_recv()
        for cp in copies:
            cp.wait_send()
        acc = gath[0]
        for d in range(1, 8):
            acc = acc + gath[d]
        out_ref[...] = acc

    vm = pl.BlockSpec(memory_space=pltpu.VMEM)
    return pl.pallas_call(
        body, name="allreduce_small", in_specs=[vm], out_specs=vm,
        out_shape=jax.ShapeDtypeStruct((8, D_MODEL), F32),
        scratch_shapes=[pltpu.VMEM((8, 8, D_MODEL), F32), pltpu.SemaphoreType.DMA((7,)), pltpu.SemaphoreType.DMA((7,))],
    )(block)


def _adamw(w, g, m, v, name):
    rows, cols = w.shape
    tr = 128 if rows % 128 == 0 else rows

    def body(w_ref, g_ref, m_ref, v_ref, d_ref, nm_ref, nv_ref):
        gv = g_ref[...]
        nm = ADAM_B1 * m_ref[...] + (1.0 - ADAM_B1) * gv
        nv = ADAM_B2 * v_ref[...] + (1.0 - ADAM_B2) * (gv * gv)
        m_hat = nm / (1.0 - ADAM_B1 ** ADAM_STEP)
        v_hat = nv / (1.0 - ADAM_B2 ** ADAM_STEP)
        d_ref[...] = -ADAM_LR * (m_hat / (jnp.sqrt(v_hat) + ADAM_EPS) + ADAM_WD * w_ref[...])
        nm_ref[...] = nm
        nv_ref[...] = nv

    spec = pl.BlockSpec((tr, cols), lambda i: (i, 0))
    shape = jax.ShapeDtypeStruct((rows, cols), F32)
    return pl.pallas_call(
        body, name=name, grid=(rows // tr,), in_specs=[spec] * 4, out_specs=[spec] * 3, out_shape=[shape] * 3,
        compiler_params=_params(("parallel",)),
    )(w, g, m, v)


def kernel(x, norm_w, w_in, conv_w, a_log, dt_bias, gdn_norm_w, w_up_a, w_up_b, w_out, final_norm_w, loss_target, m_norm_w, m_w_in, m_conv_w, m_a_log, m_dt_bias, m_gdn_norm_w, m_w_up_a, m_w_up_b, m_w_out, m_final_norm_w, v_norm_w, v_w_in, v_conv_w, v_a_log, v_dt_bias, v_gdn_norm_w, v_w_up_a, v_w_up_b, v_w_out, v_final_norm_w):
    conv_hi = conv_w[0].astype(MXU_DTYPE)
    conv_lo = (conv_w[0] - conv_hi.astype(F32)).astype(MXU_DTYPE)
    big = w_in[0].astype(MXU_DTYPE).reshape(2, *BIG_HALF)
    slab = _pack_slab(w_up_a[0].astype(MXU_DTYPE), w_up_b[0].astype(MXU_DTYPE), w_out[0].astype(MXU_DTYPE), conv_hi, conv_lo)
    own_shard = 2 * lax.axis_index("x") + lax.axis_index("y")
    bigs, slabs = _gather_weights([big, slab])
    bigs = lax.dynamic_update_slice(bigs, big[None], (own_shard, 0, 0, 0)).reshape(SHARDS, D_MODEL, W_IN_SHARD)
    slabs = lax.dynamic_update_slice(slabs, slab[None], (own_shard, 0, 0, 0))
    parts = [_unpack_slab(slabs[s]) for s in range(SHARDS)]
    split = BA_END - (SHARDS - 1) * W_IN_SHARD
    wp = jnp.concatenate([bigs[s] for s in range(SHARDS - 1)]
                         + [bigs[-1][:, :split], jnp.zeros((D_MODEL, SEG_GA - BA_END), MXU_DTYPE), bigs[-1][:, split:]], axis=1)
    w_up_a_full = jnp.concatenate([p[0] for p in parts], axis=1)
    w_up_b_full = jnp.concatenate([p[1] for p in parts], axis=1)
    w_out_full = jnp.concatenate([p[2] for p in parts], axis=0)
    conv_full = jnp.concatenate([p[3].astype(F32) + p[4].astype(F32) for p in parts], axis=1)

    blocks, tags = (128, HALF_ROWS), ("w_in", "slab")

    def start_reduce(d_wp, d_w_up_a, d_w_up_b, d_w_out, d_conv_w):
        d_w_in = [d_wp[:, s * W_IN_SHARD:(s + 1) * W_IN_SHARD] for s in range(SHARDS - 1)]
        d_w_in.append(jnp.concatenate([d_wp[:, (SHARDS - 1) * W_IN_SHARD:BA_END], d_wp[:, SEG_GA:]], axis=1))
        zero_conv = jnp.zeros((GDN_CONV, CONV_SHARD), F32)
        grads = [jnp.stack(d_w_in).reshape(SHARDS, 2, *BIG_HALF),
                 jnp.stack([_pack_slab(d_w_up_a[:, s * 256:(s + 1) * 256], d_w_up_b[:, s * 256:(s + 1) * 256],
                                       d_w_out[s * 256:(s + 1) * 256], d_conv_w[:, s * CONV_SHARD:(s + 1) * CONV_SHARD],
                                       zero_conv) for s in range(SHARDS)])]
        from_sibling = _exchange_halves(grads)
        pairs = [_pair_sum(gr, fs, blk, f"pair_sum_{tag}") for gr, fs, blk, tag in zip(grads, from_sibling, blocks, tags)]
        *in_flight, token = _scatter_start(pairs)
        return in_flight, token

    g = _local_step(x[0], loss_target[0], norm_w, wp, conv_full, a_log, dt_bias, gdn_norm_w,
                    w_up_a_full, w_up_b_full, w_out_full, final_norm_w[None], start_reduce)

    send_sems, recv_sems, *passed = g["in_flight"]
    arrived = _scatter_wait(send_sems, recv_sems, passed, after=g["grad_x"])
    pairs, from_chips = arrived[:2], arrived[2:]
    total_big, total_slab = _share_total([_chip_sum(p, fc, blk, f"chip_sum_{tag}")
                                          for p, fc, blk, tag in zip(pairs, from_chips, blocks, tags)])
    g_w_in = total_big.reshape(D_MODEL, W_IN_SHARD)
    g_w_up_a, g_w_up_b, g_w_out, g_conv, _ = _unpack_slab(total_slab)

    row2 = jnp.concatenate([g["gdn_norm_w"], g["a_log"], g["dt_bias"], g["loss"][0:1, 0:1],
                            jnp.zeros((1, D_MODEL - HEAD_DIM - 2 * HEADS - 1), F32)], axis=1)
    small = _allreduce_small(jnp.concatenate([g["norm_w"], g["final_norm_w"], row2, jnp.zeros((5, D_MODEL), F32)], axis=0))
    g_norm, g_final = small[0:1], small[1]
    g_gnw, g_alog, g_dt = small[2:3, 0:HEAD_DIM], small[2:3, HEAD_DIM:HEAD_DIM + HEADS], small[2:3, HEAD_DIM + HEADS:HEAD_DIM + 2 * HEADS]
    loss = small[2, HEAD_DIM + 2 * HEADS]

    names = ["norm_w", "w_in", "conv_w", "a_log", "dt_bias", "gdn_norm_w", "w_up_a", "w_up_b", "w_out", "final_norm_w"]
    weights = dict(zip(names, (norm_w, w_in, conv_w, a_log, dt_bias, gdn_norm_w, w_up_a, w_up_b, w_out, final_norm_w)))
    ms = dict(zip(names, (m_norm_w, m_w_in, m_conv_w, m_a_log, m_dt_bias, m_gdn_norm_w, m_w_up_a, m_w_up_b, m_w_out, m_final_norm_w)))
    vs = dict(zip(names, (v_norm_w, v_w_in, v_conv_w, v_a_log, v_dt_bias, v_gdn_norm_w, v_w_up_a, v_w_up_b, v_w_out, v_final_norm_w)))
    grads2d = dict(norm_w=g_norm, w_in=g_w_in, conv_w=g_conv, a_log=g_alog, dt_bias=g_dt, gdn_norm_w=g_gnw,
                   w_up_a=g_w_up_a, w_up_b=g_w_up_b, w_out=g_w_out, final_norm_w=g_final[None])
    grad_out, delta, new_m, new_v = [], [], [], []
    for n in names:
        shape = weights[n].shape
        two_d = grads2d[n].shape
        d, nm, nv = _adamw(weights[n].reshape(two_d), grads2d[n], ms[n].reshape(two_d), vs[n].reshape(two_d), f"adamw_{n}")
        grad_out.append(grads2d[n].reshape(shape))
        delta.append(d.reshape(shape))
        new_m.append(nm.reshape(shape))
        new_v.append(nv.reshape(shape))
    return (loss, g["grad_x"][None], *grad_out, *delta, *new_m, *new_v)
```

```python
import functools

import jax
import jax.numpy as jnp
from jax import lax
from jax.experimental import pallas as pl
from jax.experimental.pallas import tpu as pltpu

F32 = jnp.float32
MXU_DTYPE = jnp.bfloat16
HIGHEST = lax.Precision.HIGHEST

D_MODEL = 1024
HEADS = 8
HEAD_DIM = 64
WIDTH = HEADS * HEAD_DIM
NORM_EPS = 1e-6
ROPE_THETA = 10000.0
ATT_BLOCK = 128
DILATIONS = (1, 4, 16)
GDN_CHUNK = 64
GDN_CONV = 4
IN_WIDTH = 9232
SEG_A, SEG_ZA, SEG_B, SEG_ZB, SEG_BA, SEG_GA, SEG_GB, PACKED_WIDTH = 0, 4608, 5120, 6656, 7168, 7680, 8704, 9728
BA_END = 7184
VMEM_LIMIT = 56 * 1024 * 1024

ADAM_LR, ADAM_B1, ADAM_B2, ADAM_EPS, ADAM_WD, ADAM_STEP = 0.001, 0.9, 0.999, 1e-08, 0.01, 10

_NN = (((1,), (0,)), ((), ()))
_NT = (((1,), (1,)), ((), ()))
_TN = (((0,), (0,)), ((), ()))


def _params(sem):
    return pltpu.CompilerParams(dimension_semantics=sem, vmem_limit_bytes=VMEM_LIMIT)


def _mxu(a, b, dims):
    return lax.dot_general(a.astype(MXU_DTYPE), b.astype(MXU_DTYPE), dims, preferred_element_type=F32)


def _sigmoid(x):
    return 1.0 / (1.0 + jnp.exp(-x))


def _softplus(x):
    return jnp.maximum(x, 0.0) + jnp.log(1.0 + jnp.exp(-jnp.abs(x)))


def _iota(shape, axis):
    return lax.broadcasted_iota(jnp.int32, shape, axis)


def _matmul(a, b, mode, name, out_dtype=F32, tm=512, tn=512, tk=512):
    if mode == "nn":
        (m, k), (k2, n) = a.shape, b.shape
    elif mode == "nt":
        (m, k), (n, k2) = a.shape, b.shape
    else:
        (k, m), (k2, n) = a.shape, b.shape
    assert k == k2
    tm, tn, tk = min(tm, m), min(tn, n), min(tk, k)
    assert m % tm == 0 and n % tn == 0 and k % tk == 0
    nk = k // tk
    dims = {"nn": _NN, "nt": _NT, "tn": _TN}[mode]

    assert out_dtype == F32

    def body(a_ref, b_ref, o_ref):
        kk = pl.program_id(2)
        part = _mxu(a_ref[...], b_ref[...], dims)

        @pl.when(kk == 0)
        def _():
            o_ref[...] = part

        @pl.when(kk > 0)
        def _():
            o_ref[...] += part

    a_spec = pl.BlockSpec((tk, tm), lambda i, j, kk: (kk, i)) if mode == "tn" else pl.BlockSpec((tm, tk), lambda i, j, kk: (i, kk))
    b_spec = pl.BlockSpec((tn, tk), lambda i, j, kk: (j, kk)) if mode == "nt" else pl.BlockSpec((tk, tn), lambda i, j, kk: (kk, j))
    return pl.pallas_call(
        body, name=name, grid=(m // tm, n // tn, nk), in_specs=[a_spec, b_spec],
        out_specs=pl.BlockSpec((tm, tn), lambda i, j, kk: (i, j)),
        out_shape=jax.ShapeDtypeStruct((m, n), out_dtype),
        compiler_params=_params(("parallel", "parallel", "arbitrary")),
    )(a, b)


def _norm_proj(x, norm_w, wp, tm=1024, tn=PACKED_WIDTH // 4):
    t = x.shape[0]
    tm = min(tm, t)

    def body(x_ref, nw_ref, w_ref, proj_ref, ht_ref, h_scr):
        @pl.when(pl.program_id(1) == 0)
        def _():
            xf = x_ref[...]
            r = lax.rsqrt(jnp.mean(xf * xf, axis=-1, keepdims=True) + NORM_EPS)
            h = xf * r * nw_ref[...]
            h_scr[...] = h.astype(h_scr.dtype)
            ht_ref[...] = h.T.astype(ht_ref.dtype)

        proj_ref[...] = jnp.dot(h_scr[...], w_ref[...], preferred_element_type=F32)

    return pl.pallas_call(
        body, name="norm_proj", grid=(t // tm, PACKED_WIDTH // tn),
        in_specs=[pl.BlockSpec((tm, D_MODEL), lambda i, j: (i, 0)),
                  pl.BlockSpec((1, D_MODEL), lambda i, j: (0, 0)),
                  pl.BlockSpec((D_MODEL, tn), lambda i, j: (0, j))],
        out_specs=[pl.BlockSpec((tm, tn), lambda i, j: (i, j)),
                   pl.BlockSpec((D_MODEL, tm), lambda i, j: (0, i))],
        out_shape=[jax.ShapeDtypeStruct((t, PACKED_WIDTH), F32), jax.ShapeDtypeStruct((D_MODEL, t), MXU_DTYPE)],
        scratch_shapes=[pltpu.VMEM((tm, D_MODEL), MXU_DTYPE)],
        compiler_params=_params(("parallel", "arbitrary")),
    )(x, norm_w, wp)


def _rope_tables(t):
    lane = jnp.arange(128)
    inv_freq = ROPE_THETA ** (-jnp.arange(0, HEAD_DIM, 2, dtype=F32) / HEAD_DIM)
    freq = jnp.concatenate([inv_freq] * 4)
    coarse = (jnp.arange(t // 128, dtype=F32) * 128.0)[:, None] * freq[None, :]
    fine = jnp.arange(128, dtype=F32)[:, None] * freq[None, :]
    ca, sa, cb, sb = jnp.cos(coarse)[:, None, :], jnp.sin(coarse)[:, None, :], jnp.cos(fine)[None], jnp.sin(fine)[None]
    cos = (ca * cb - sa * sb).reshape(t, 128)
    sin = (sa * cb + ca * sb).reshape(t, 128)
    first_half = (lane % HEAD_DIM) < HEAD_DIM // 2
    return cos, jnp.where(first_half, -sin, 0.0), jnp.where(first_half, 0.0, sin)


def _rope_cols(x, cos, sin_lo, sin_hi, sign):
    outs = []
    for c in range(x.shape[1] // 128):
        xc = x[:, c * 128:(c + 1) * 128]
        rot = pltpu.roll(xc, 96, 1) * sin_lo + pltpu.roll(xc, 32, 1) * sin_hi
        outs.append(xc * cos + sign * rot)
    return jnp.concatenate(outs, axis=1)


def _rope_block(x, cos, sin_lo, sin_hi, sign):
    return jnp.concatenate([_rope_cols(x[:, :2 * WIDTH], cos, sin_lo, sin_hi, sign), x[:, 2 * WIDTH:]], axis=1)


def _tile_scratch(tm, cols):
    return pltpu.VMEM((cols // 128, tm, 128), F32)


def _store_tile(scr, y):
    for c in range(scr.shape[0]):
        scr[c] = y[:, c * 128:(c + 1) * 128]


def _load_tile(scr):
    return jnp.concatenate([scr[c] for c in range(scr.shape[0])], axis=1)


def _to_strided_view(scr, o_ref, d, width=None, col0=0):
    n, tm, _ = scr.shape
    width = n * 128 if width is None else width
    for r in range(d):
        for c in range(n):
            at = r * width + col0 + c * 128
            o_ref[:, at:at + 128] = scr[c, pl.ds(r, tm // d, stride=d), :].astype(o_ref.dtype)


def _from_strided_view(i_ref, scr, d):
    n, tm, _ = scr.shape
    for r in range(d):
        for c in range(n):
            scr[c, pl.ds(r, tm // d, stride=d), :] = i_ref[:, (r * n + c) * 128:(r * n + c + 1) * 128].astype(F32)


def _strided_spec(tm, d, cols):
    return pl.BlockSpec((tm // d, d * cols), lambda i: (i, 0))


def _rope_fwd(proj, tables, tm=256):
    t = proj.shape[0]
    cols = 3 * WIDTH

    def body(x_ref, c_ref, sl_ref, sh_ref, o0, o1, o2, scr):
        for g, (d, o_ref) in enumerate(zip(DILATIONS, (o0, o1, o2))):
            y = _rope_block(x_ref[:, g * cols:(g + 1) * cols], c_ref[...], sl_ref[...], sh_ref[...], 1.0)
            if d == 1:
                o_ref[...] = y.astype(o_ref.dtype)
            else:
                _store_tile(scr, y)
                _to_strided_view(scr, o_ref, d)

    tab = pl.BlockSpec((tm, 128), lambda i: (i, 0))
    return pl.pallas_call(
        body, name="rope_fwd", grid=(t // tm,),
        in_specs=[pl.BlockSpec((tm, 3 * cols), lambda i: (i, 0)), tab, tab, tab],
        out_specs=[_strided_spec(tm, d, cols) for d in DILATIONS],
        out_shape=[jax.ShapeDtypeStruct((t // d, d * cols), MXU_DTYPE) for d in DILATIONS],
        scratch_shapes=[_tile_scratch(tm, cols)],
        compiler_params=_params(("parallel",)),
    )(proj, *tables)


def _rope_bwd(dproj, dqs, dkvs, tables, tm=256):
    t = dproj.shape[0]
    cols = 3 * WIDTH

    def body(dp_ref, q0, q1, q2, kv0, kv1, kv2, c_ref, sl_ref, sh_ref, o_ref, scr_q, scr_kv):
        for g, (d, q_ref, kv_ref) in enumerate(zip(DILATIONS, (q0, q1, q2), (kv0, kv1, kv2))):
            if d == 1:
                x = jnp.concatenate([q_ref[...], kv_ref[...]], axis=1)
            else:
                _from_strided_view(q_ref, scr_q, d)
                _from_strided_view(kv_ref, scr_kv, d)
                x = jnp.concatenate([_load_tile(scr_q), _load_tile(scr_kv)], axis=1)
            y = _rope_block(x, c_ref[...], sl_ref[...], sh_ref[...], -1.0)
            o_ref[:, g * cols:(g + 1) * cols] = y.astype(o_ref.dtype)

    tab = pl.BlockSpec((tm, 128), lambda i: (i, 0))
    return pl.pallas_call(
        body, name="rope_bwd", grid=(t // tm,),
        in_specs=[pl.BlockSpec(memory_space=pl.ANY)] + [_strided_spec(tm, d, WIDTH) for d in DILATIONS]
        + [_strided_spec(tm, d, 2 * WIDTH) for d in DILATIONS] + [tab, tab, tab],
        out_specs=pl.BlockSpec((tm, 3 * cols), lambda i: (i, 0)),
        out_shape=jax.ShapeDtypeStruct((t, PACKED_WIDTH), MXU_DTYPE),
        scratch_shapes=[_tile_scratch(tm, WIDTH), _tile_scratch(tm, 2 * WIDTH)],
        input_output_aliases={0: 0},
        compiler_params=_params(("parallel",)),
    )(dproj, *dqs, *dkvs, *tables)


def _att_masks():
    qi = _iota((ATT_BLOCK, ATT_BLOCK), 0)
    kj = _iota((ATT_BLOCK, ATT_BLOCK), 1)
    return kj <= qi, kj >= qi


def _att_fwd(qkv, d, name):
    rows = qkv.shape[0]
    nb = rows // ATT_BLOCK
    scale = HEAD_DIM ** -0.5

    def body(q_ref, kc_ref, kp_ref, vc_ref, vp_ref, o_ref, lse_ref):
        has_prev = pl.program_id(1) > 0
        m_cur, m_prev = _att_masks()
        m_prev = m_prev & has_prev
        hs = range(HEADS)
        sls = [slice(h * HEAD_DIM, (h + 1) * HEAD_DIM) for h in hs]
        qs = [q_ref[:, sl] for sl in sls]
        s_c = [jnp.where(m_cur, _mxu(qs[h], kc_ref[:, sls[h]], _NT) * scale, -jnp.inf) for h in hs]
        s_p = [jnp.where(m_prev, _mxu(qs[h], kp_ref[:, sls[h]], _NT) * scale, -jnp.inf) for h in hs]
        m = [jnp.max(jnp.maximum(s_c[h], s_p[h]), axis=1, keepdims=True) for h in hs]
        p_c = [jnp.exp(s_c[h] - m[h]) for h in hs]
        p_p = [jnp.exp(s_p[h] - m[h]) for h in hs]
        den = [jnp.sum(p_c[h] + p_p[h], axis=1, keepdims=True) for h in hs]
        o = [_mxu(p_c[h], vc_ref[:, sls[h]], _NN) + _mxu(p_p[h], vp_ref[:, sls[h]], _NN) for h in hs]
        for h in hs:
            o_ref[:, sls[h]] = o[h] / den[h]
            lse_ref[:, sls[h]] = jnp.broadcast_to(m[h] + jnp.log(den[h]), (ATT_BLOCK, HEAD_DIM))

    def cur(c):
        return pl.BlockSpec((ATT_BLOCK, WIDTH), lambda r, i: (i, 3 * r + c))

    def prev(c):
        return pl.BlockSpec((ATT_BLOCK, WIDTH), lambda r, i: (jnp.maximum(i - 1, 0), 3 * r + c))

    out = pl.BlockSpec((ATT_BLOCK, WIDTH), lambda r, i: (i, r))
    return pl.pallas_call(
        body, name=name, grid=(d, nb), in_specs=[cur(0), cur(1), prev(1), cur(2), prev(2)],
        out_specs=[out, out],
        out_shape=[jax.ShapeDtypeStruct((rows, d * WIDTH), F32)] * 2,
        compiler_params=_params(("parallel", "arbitrary")),
    )(qkv, qkv, qkv, qkv, qkv)


def _att_bwd(qkv, do, lse, delta, d, name):
    rows = qkv.shape[0]
    nb = rows // ATT_BLOCK
    scale = HEAD_DIM ** -0.5

    def body(q_ref, kc_ref, kp_ref, vc_ref, vp_ref, do_ref, l_ref, dl_ref, dq_ref, dkv_ref, own):
        i = pl.program_id(1)
        m_cur, m_prev = _att_masks()
        m_prev = m_prev & (i > 0)

        hs = range(HEADS)
        sls = [slice(h * HEAD_DIM, (h + 1) * HEAD_DIM) for h in hs]
        col = [slice(h * HEAD_DIM, h * HEAD_DIM + 1) for h in hs]

        def probs(k_r, mask):
            s = [_mxu(q_ref[:, sls[h]], k_r[:, sls[h]], _NT) for h in hs]
            return [jnp.where(mask, jnp.exp(s[h] * scale - l_ref[:, col[h]]), 0.0) for h in hs]

        def dscores(p, v_r):
            dp = [_mxu(do_ref[:, sls[h]], v_r[:, sls[h]], _NT) for h in hs]
            return [(p[h] * (dp[h] - dl_ref[:, col[h]])).astype(MXU_DTYPE) for h in hs]

        @pl.when(i == 0)
        def _():
            own[...] = jnp.zeros_like(own)

        @pl.when(i < nb)
        def _():
            p_c = probs(kc_ref, m_cur)
            ds_c = dscores(p_c, vc_ref)
            p_p = probs(kp_ref, m_prev)
            ds_p = dscores(p_p, vp_ref)
            dq = [_mxu(ds_c[h], kc_ref[:, sls[h]], _NN) + _mxu(ds_p[h], kp_ref[:, sls[h]], _NN) for h in hs]
            dk_p = [_mxu(ds_p[h], q_ref[:, sls[h]], _TN) for h in hs]
            dv_p = [_mxu(p_p[h], do_ref[:, sls[h]], _TN) for h in hs]
            dk_c = [_mxu(ds_c[h], q_ref[:, sls[h]], _TN) for h in hs]
            dv_c = [_mxu(p_c[h], do_ref[:, sls[h]], _TN) for h in hs]
            for h in hs:
                vs = slice(WIDTH + h * HEAD_DIM, WIDTH + (h + 1) * HEAD_DIM)
                dq_ref[:, sls[h]] = dq[h] * scale
                dkv_ref[:, sls[h]] = own[:, sls[h]] + dk_p[h] * scale
                dkv_ref[:, vs] = own[:, vs] + dv_p[h]
                own[:, sls[h]] = dk_c[h] * scale
                own[:, vs] = dv_c[h]

        @pl.when(i == nb)
        def _():
            dkv_ref[...] = own[...]

    def qkv_spec(c, shift):
        return pl.BlockSpec((ATT_BLOCK, WIDTH), lambda r, i: (jnp.clip(i + shift, 0, nb - 1), 3 * r + c))

    tok = pl.BlockSpec((ATT_BLOCK, WIDTH), lambda r, i: (jnp.minimum(i, nb - 1), r))
    return pl.pallas_call(
        body, name=name, grid=(d, nb + 1),
        in_specs=[qkv_spec(0, 0), qkv_spec(1, 0), qkv_spec(1, -1), qkv_spec(2, 0), qkv_spec(2, -1), tok, tok, tok],
        out_specs=[tok, pl.BlockSpec((ATT_BLOCK, 2 * WIDTH), lambda r, i: (jnp.maximum(i - 1, 0), r))],
        out_shape=[jax.ShapeDtypeStruct((rows, d * WIDTH), F32), jax.ShapeDtypeStruct((rows, d * 2 * WIDTH), F32)],
        scratch_shapes=[pltpu.VMEM((ATT_BLOCK, 2 * WIDTH), F32)],
        compiler_params=_params(("parallel", "arbitrary")),
    )(qkv, qkv, qkv, qkv, qkv, do, lse, delta)


def _att_merge(os_, lses, proj, tm=256):
    t = proj.shape[0]

    def body(o0, o1, o2, l0, l1, l2, z_ref, oz_ref, o_ref, t0, t1, t2, s_o1, s_l1, s_o2, s_l2, s_t):
        _from_strided_view(o1, s_o1, DILATIONS[1])
        _from_strided_view(l1, s_l1, DILATIONS[1])
        _from_strided_view(o2, s_o2, DILATIONS[2])
        _from_strided_view(l2, s_l2, DILATIONS[2])
        a, b, c = l0[...], _load_tile(s_l1), _load_tile(s_l2)
        m = jnp.maximum(jnp.maximum(a, b), c)
        wa, wb, wc = jnp.exp(a - m), jnp.exp(b - m), jnp.exp(c - m)
        den = wa + wb + wc
        o = (wa * o0[...] + wb * _load_tile(s_o1) + wc * _load_tile(s_o2)) / den
        z = z_ref[...]
        o_ref[...] = o
        oz_ref[...] = (o * z * _sigmoid(z)).astype(oz_ref.dtype)
        total = m + jnp.log(den)
        t0[...] = total
        _store_tile(s_t, total)
        _to_strided_view(s_t, t1, DILATIONS[1])
        _to_strided_view(s_t, t2, DILATIONS[2])

    tok = pl.BlockSpec((tm, WIDTH), lambda i: (i, 0))
    views = [_strided_spec(tm, d, WIDTH) for d in DILATIONS]
    view_shapes = [jax.ShapeDtypeStruct((t // d, d * WIDTH), F32) for d in DILATIONS]
    return pl.pallas_call(
        body, name="att_merge", grid=(t // tm,),
        in_specs=views + views + [pl.BlockSpec((tm, WIDTH), lambda i: (i, SEG_ZA // WIDTH))],
        out_specs=[tok, tok] + views,
        out_shape=[jax.ShapeDtypeStruct((t, WIDTH), MXU_DTYPE), jax.ShapeDtypeStruct((t, WIDTH), F32)] + view_shapes,
        scratch_shapes=[_tile_scratch(tm, WIDTH)] * 5,
        compiler_params=_params(("parallel",)),
    )(*os_, *lses, proj)


def _att_merge_bwd(dproj, d_oz, o, proj, tm=256):
    t = proj.shape[0]

    def body(dp_ref, doz_ref, o_ref, z_ref, dz_ref, do0, do1, do2, dl0, dl1, dl2, s_do, s_dl):
        z, ov, g = z_ref[...], o_ref[...], doz_ref[...]
        sg = _sigmoid(z)
        do = g * z * sg
        dz_ref[...] = (g * ov * sg * (1.0 + z * (1.0 - sg))).astype(dz_ref.dtype)
        do0[...] = do.astype(do0.dtype)
        _store_tile(s_do, do)
        prod = do * ov
        for p in range(HEADS // 2):
            s_dl[p] = _pair_sum_lanes(prod[:, p * 128:(p + 1) * 128], _pair_ones())
        dl0[...] = _load_tile(s_dl)
        for d, do_v, dl_v in ((DILATIONS[1], do1, dl1), (DILATIONS[2], do2, dl2)):
            _to_strided_view(s_do, do_v, d)
            _to_strided_view(s_dl, dl_v, d)

    tok = pl.BlockSpec((tm, WIDTH), lambda i: (i, 0))
    seg = pl.BlockSpec((tm, WIDTH), lambda i: (i, SEG_ZA // WIDTH))
    views = [_strided_spec(tm, d, WIDTH) for d in DILATIONS]
    return pl.pallas_call(
        body, name="att_merge_bwd", grid=(t // tm,),
        in_specs=[pl.BlockSpec(memory_space=pl.ANY), tok, tok, seg],
        out_specs=[seg] + views + views,
        out_shape=[jax.ShapeDtypeStruct((t, PACKED_WIDTH), MXU_DTYPE)]
        + [jax.ShapeDtypeStruct((t // d, d * WIDTH), MXU_DTYPE) for d in DILATIONS]
        + [jax.ShapeDtypeStruct((t // d, d * WIDTH), F32) for d in DILATIONS],
        scratch_shapes=[_tile_scratch(tm, WIDTH)] * 2,
        input_output_aliases={0: 0},
        compiler_params=_params(("parallel",)),
    )(dproj, d_oz, o, proj)


def _shift_down(x, halo, s):
    if s == 0:
        return x
    xs = pltpu.roll(x, s, 0)
    head = jnp.where(_iota((8, x.shape[1]), 0) < s, pltpu.roll(halo, s, 0), xs[0:8])
    return jnp.concatenate([head, xs[8:]], axis=0)


def _shift_up(x, nxt, s):
    if s == 0:
        return x
    n = x.shape[0]
    xs = pltpu.roll(x, n - s, 0)
    tail = jnp.where(_iota((8, x.shape[1]), 0) >= 8 - s, pltpu.roll(nxt, 8 - s, 0), xs[n - 8:])
    return jnp.concatenate([xs[:n - 8], tail], axis=0)


def _conv_fwd(proj, conv_w, tm=1024):
    t = proj.shape[0]
    cb = SEG_B // WIDTH

    def body(x_ref, halo_ref, w_ref, c_ref):
        halo = jnp.where(pl.program_id(0) > 0, halo_ref[...], 0.0)
        x = x_ref[...]
        w = w_ref[...]
        acc = jnp.zeros((tm, WIDTH), F32)
        for j in range(GDN_CONV):
            acc += _shift_down(x, halo, GDN_CONV - 1 - j) * w[j:j + 1, :]
        c_ref[...] = acc

    return pl.pallas_call(
        body, name="conv_fwd", grid=(t // tm, 3),
        in_specs=[pl.BlockSpec((tm, WIDTH), lambda i, c: (i, cb + c)),
                  pl.BlockSpec((8, WIDTH), lambda i, c: (jnp.maximum(i * (tm // 8) - 1, 0), cb + c)),
                  pl.BlockSpec((GDN_CONV, WIDTH), lambda i, c: (0, c))],
        out_specs=pl.BlockSpec((tm, WIDTH), lambda i, c: (i, c)),
        out_shape=jax.ShapeDtypeStruct((t, 3 * WIDTH), F32),
        compiler_params=_params(("parallel", "parallel")),
    )(proj, proj, conv_w)


def _conv_bwd(dproj, dc, proj, conv_w, tm=1024):
    t = proj.shape[0]
    cb = SEG_B // WIDTH
    nt = t // tm

    def body(dp_ref, dc_ref, dcn_ref, x_ref, halo_ref, w_ref, dx_ref, dw_ref):
        i = pl.program_id(1)
        w = w_ref[...]
        dcn = jnp.where(i < nt - 1, dcn_ref[...], 0.0)
        dcv = dc_ref[...]
        acc = jnp.zeros((tm, WIDTH), F32)
        for j in range(GDN_CONV):
            acc += _shift_up(dcv, dcn, GDN_CONV - 1 - j) * w[j:j + 1, :]
        dx_ref[...] = acc.astype(dx_ref.dtype)
        halo = jnp.where(i > 0, halo_ref[...], 0.0)
        x = x_ref[...]
        row8 = _iota((8, WIDTH), 0)
        part = jnp.zeros((8, WIDTH), F32)
        for j in range(GDN_CONV):
            s = jnp.sum(dcv * _shift_down(x, halo, GDN_CONV - 1 - j), axis=0, keepdims=True)
            part += jnp.where(row8 == j, s, 0.0)

        @pl.when(i == 0)
        def _():
            dw_ref[...] = jnp.zeros_like(dw_ref)

        dw_ref[...] += part

    return pl.pallas_call(
        body, name="conv_bwd", grid=(3, nt),
        in_specs=[pl.BlockSpec(memory_space=pl.ANY),
                  pl.BlockSpec((tm, WIDTH), lambda c, i: (i, c)),
                  pl.BlockSpec((8, WIDTH), lambda c, i: (jnp.minimum((i + 1) * (tm // 8), t // 8 - 1), c)),
                  pl.BlockSpec((tm, WIDTH), lambda c, i: (i, cb + c)),
                  pl.BlockSpec((8, WIDTH), lambda c, i: (jnp.maximum(i * (tm // 8) - 1, 0), cb + c)),
                  pl.BlockSpec((GDN_CONV, WIDTH), lambda c, i: (0, c))],
        out_specs=[pl.BlockSpec((tm, WIDTH), lambda c, i: (i, cb + c)),
                   pl.BlockSpec((8, WIDTH), lambda c, i: (0, c))],
        out_shape=[jax.ShapeDtypeStruct((t, PACKED_WIDTH), MXU_DTYPE), jax.ShapeDtypeStruct((8, 3 * WIDTH), F32)],
        input_output_aliases={0: 0},
        compiler_params=_params(("parallel", "arbitrary")),
    )(dproj, dc, dc, proj, proj, conv_w)


def _chunk_matrices(tm):
    r, c = _iota((tm, tm), 0), _iota((tm, tm), 1)
    same = (r // GDN_CHUNK) == (c // GDN_CHUNK)
    return jnp.where(same & (c <= r), 1.0, 0.0), jnp.where(same, 1.0, 0.0)


def _gdn_gates(ba, a_log, dt_bias):
    al = ba + dt_bias
    return _sigmoid(ba), -jnp.exp(a_log) * _softplus(al), _sigmoid(al)


SCAN_HEADS = (0, 2, 4, 6, 1, 3, 5, 7)


def _head_lane_eye():
    return jnp.where(_iota((HEADS, 128), 1) == _iota((HEADS, 128), 0) + HEADS, 1.0, 0.0)


def _pair_ones():
    return jnp.where(_iota((128, 128), 0) // HEAD_DIM == _iota((128, 128), 1) // HEAD_DIM, 1.0, 0.0).astype(MXU_DTYPE)


def _pair_to_lanes(p, base):
    return jnp.where(_iota((128, 128), 1) == base + 2 * p + _iota((128, 128), 0) // HEAD_DIM, 1.0, 0.0).astype(MXU_DTYPE)


def _pair_sum_lanes(x, sel):
    hi, lo = _split(x)
    return jnp.dot(hi, sel, preferred_element_type=F32) + jnp.dot(lo, sel, preferred_element_type=F32)


def _gdn_prep(conv, proj, a_log, dt_bias, tm=256):
    t = proj.shape[0]
    nc = tm // GDN_CHUNK

    def body(c_ref, ba_ref, al_ref, dt_ref, q_ref, k_ref, v_ref, b_ref, g_ref, gl_ref, grow_ref):
        beta, g, _ = _gdn_gates(ba_ref[:, 0:128], al_ref[...], dt_ref[...])
        lmat, cmat = _chunk_matrices(tm)
        gc = jnp.dot(lmat, g, precision=HIGHEST, preferred_element_type=F32)
        gl = jnp.dot(cmat, g, precision=HIGHEST, preferred_element_type=F32)
        grow = lax.dot_general(_head_lane_eye(), gc, _NT, precision=HIGHEST, preferred_element_type=F32)
        ones = _pair_ones()
        first = _iota((1, 128), 1) < HEAD_DIM

        def spread(x, base, p):
            return jnp.where(first, x[:, base + 2 * p:base + 2 * p + 1], x[:, base + 2 * p + 1:base + 2 * p + 2])

        for p in range(HEADS // 2):
            for seg, ref, scale in ((0, q_ref, HEAD_DIM ** -0.5), (1, k_ref, 1.0), (2, v_ref, None)):
                c = c_ref[:, seg * WIDTH + p * 128:seg * WIDTH + (p + 1) * 128]
                a = c * _sigmoid(c)
                if scale is not None:
                    a = a * (lax.rsqrt(_pair_sum_lanes(a * a, ones) + NORM_EPS) * scale)
                ref[p] = a
            b_ref[p] = spread(beta, 0, p)
            g_ref[p] = spread(gc, HEADS, p)
            gl_ref[p] = spread(gl, HEADS, p)
        for pos, h in enumerate(SCAN_HEADS):
            for cc in range(nc):
                grow_ref[pos, cc] = grow[h:h + 1, cc * GDN_CHUNK:(cc + 1) * GDN_CHUNK]

    hm = pl.BlockSpec((HEADS // 2, tm, 128), lambda i: (0, i, 0))
    small = pl.BlockSpec((1, 128), lambda i: (0, 0))
    hm_shape = jax.ShapeDtypeStruct((HEADS // 2, t, 128), F32)
    return pl.pallas_call(
        body, name="gdn_prep", grid=(t // tm,),
        in_specs=[pl.BlockSpec((tm, 3 * WIDTH), lambda i: (i, 0)),
                  pl.BlockSpec((tm, WIDTH), lambda i: (i, SEG_BA // WIDTH)), small, small],
        out_specs=[hm] * 6 + [pl.BlockSpec((HEADS, nc, 1, GDN_CHUNK), lambda i: (0, i, 0, 0))],
        out_shape=[hm_shape] * 6 + [jax.ShapeDtypeStruct((HEADS, t // GDN_CHUNK, 1, GDN_CHUNK), F32)],
        compiler_params=_params(("parallel",)),
    )(conv, proj, a_log, dt_bias)


def _gdn_prep_bwd(dproj, conv, proj, a_log, dt_bias, dq, dk, dv, db, dg, dgl, dgrow, tm=256):
    t = proj.shape[0]
    nc = tm // GDN_CHUNK

    def body(dp_ref, c_ref, ba_ref, al_ref, dt_ref, dq_ref, dk_ref, dv_ref, db_ref, dg_ref, dgl_ref, dgrow_ref,
             dba_ref, dc_ref, small_ref, row_scr):
        beta, g, sig_al = _gdn_gates(ba_ref[:, 0:128], al_ref[...], dt_ref[...])
        d_beta = jnp.zeros((tm, 128), F32)
        d_gc = jnp.zeros((tm, 128), F32)
        d_gl = jnp.zeros((tm, 128), F32)
        ones = _pair_ones()
        for pos, h in enumerate(SCAN_HEADS):
            for cc in range(nc):
                row_scr[h:h + 1, cc * GDN_CHUNK:(cc + 1) * GDN_CHUNK] = dgrow_ref[pos, cc]
        for p in range(HEADS // 2):
            d_beta += _pair_sum_lanes(db_ref[p], _pair_to_lanes(p, 0))
            d_gc += _pair_sum_lanes(dg_ref[p], _pair_to_lanes(p, HEADS))
            d_gl += _pair_sum_lanes(dgl_ref[p], _pair_to_lanes(p, HEADS))
            for seg, ref, scale in ((0, dq_ref, HEAD_DIM ** -0.5), (1, dk_ref, 1.0), (2, dv_ref, None)):
                cols = slice(seg * WIDTH + p * 128, seg * WIDTH + (p + 1) * 128)
                c = c_ref[:, cols]
                sg = _sigmoid(c)
                da = ref[p]
                if scale is not None:
                    a = c * sg
                    r = lax.rsqrt(_pair_sum_lanes(a * a, ones) + NORM_EPS)
                    da = da * scale
                    da = r * da - a * (r * r * r) * _pair_sum_lanes(da * a, ones)
                dc_ref[:, cols] = da * (sg * (1.0 + c * (1.0 - sg)))
        d_gc += lax.dot_general(row_scr[...], _head_lane_eye(), _TN, precision=HIGHEST, preferred_element_type=F32)
        lmat, cmat = _chunk_matrices(tm)
        d_g = (lax.dot_general(lmat, d_gc, _TN, precision=HIGHEST, preferred_element_type=F32)
               + lax.dot_general(cmat, d_gl, _TN, precision=HIGHEST, preferred_element_type=F32))
        d_al = d_g * (-jnp.exp(al_ref[...])) * sig_al
        d_bl = d_beta * beta * (1.0 - beta)
        dba_ref[...] = jnp.concatenate([d_bl + d_al, jnp.zeros((tm, WIDTH - 128), F32)], axis=1).astype(dba_ref.dtype)
        row8 = _iota((8, 128), 0)
        part = (jnp.where(row8 == 0, jnp.sum(d_g * g, axis=0, keepdims=True), 0.0)
                + jnp.where(row8 == 1, jnp.sum(d_al, axis=0, keepdims=True), 0.0))

        @pl.when(pl.program_id(0) == 0)
        def _():
            small_ref[...] = jnp.zeros_like(small_ref)

        small_ref[...] += part

    hm = pl.BlockSpec((HEADS // 2, tm, 128), lambda i: (0, i, 0))
    small = pl.BlockSpec((1, 128), lambda i: (0, 0))
    seg = pl.BlockSpec((tm, WIDTH), lambda i: (i, SEG_BA // WIDTH))
    return pl.pallas_call(
        body, name="gdn_prep_bwd", grid=(t // tm,),
        in_specs=[pl.BlockSpec(memory_space=pl.ANY), pl.BlockSpec((tm, 3 * WIDTH), lambda i: (i, 0)), seg, small, small]
        + [hm] * 6 + [pl.BlockSpec((HEADS, nc, 1, GDN_CHUNK), lambda i: (0, i, 0, 0))],
        out_specs=[seg, pl.BlockSpec((tm, 3 * WIDTH), lambda i: (i, 0)), pl.BlockSpec((8, 128), lambda i: (0, 0))],
        out_shape=[jax.ShapeDtypeStruct((t, PACKED_WIDTH), MXU_DTYPE), jax.ShapeDtypeStruct((t, 3 * WIDTH), F32),
                   jax.ShapeDtypeStruct((8, 128), F32)],
        scratch_shapes=[pltpu.VMEM((HEADS, tm), F32)],
        input_output_aliases={0: 0},
        compiler_params=_params(("arbitrary",)),
    )(dproj, conv, proj, a_log, dt_bias, dq, dk, dv, db, dg, dgl, dgrow)


_BNN = (((2,), (1,)), ((0,), (0,)))
_BNT = (((2,), (2,)), ((0,), (0,)))
_BTN = (((1,), (1,)), ((0,), (0,)))


@jax.custom_vjp
def _MM_NN(a, b):
    return _mxu(a, b, _BNN)


@jax.custom_vjp
def _MM_NT(a, b):
    return _mxu(a, b, _BNT)


@jax.custom_vjp
def _MM_TN(a, b):
    return _mxu(a, b, _BTN)


_MM_NN.defvjp(lambda a, b: (_mxu(a, b, _BNN), (a, b)), lambda r, g: (_mxu(g, r[1], _BNT), _mxu(r[0], g, _BTN)))
_MM_NT.defvjp(lambda a, b: (_mxu(a, b, _BNT), (a, b)), lambda r, g: (_mxu(g, r[1], _BNN), _mxu(g, r[0], _BTN)))
_MM_TN.defvjp(lambda a, b: (_mxu(a, b, _BTN), (a, b)), lambda r, g: (_mxu(r[1], g, _BNT), _mxu(r[0], g, _BNN)))


def _split(a):
    hi = a.astype(MXU_DTYPE)
    return hi, (a - hi.astype(F32)).astype(MXU_DTYPE)


def _dot3(a, b, dims):
    (ah, al), (bh, bl) = a, b
    (ca,), (cb,) = dims[0]
    return lax.dot_general(jnp.concatenate([ah, ah, al], axis=ca), jnp.concatenate([bh, bl, bh], axis=cb), dims,
                           preferred_element_type=F32)


def _unit_lower_inverse(a):
    c = GDN_CHUNK
    eye = jnp.where(_iota((c, c), 0) == _iota((c, c), 1), 1.0, 0.0)
    x = eye - a
    p = a
    for _ in range(5):
        ps = _split(p)
        p = _dot3(ps, ps, _BNN)
        x = x + _dot3(_split(x), _split(p), _BNN)
    return x


@jax.custom_vjp
def _SAVED_INVERSE(a, t_inv):
    return t_inv


def _saved_inverse_bwd(t_inv, g):
    ts = _split(t_inv)
    return -_dot3(ts, _split(_dot3(_split(g), ts, _BNT)), _BTN), jnp.zeros_like(t_inv)


_SAVED_INVERSE.defvjp(lambda a, t_inv: (t_inv, t_inv), _saved_inverse_bwd)


def _gdn_chunk(q, k, v, beta, g1, g2, gl, state, t_inv=None):
    c = GDN_CHUNK
    if t_inv is None:
        _mm_nn, _mm_nt, _mm_tn = (functools.partial(_mxu, dims=dd) for dd in (_BNN, _BNT, _BTN))
    else:
        _mm_nn, _mm_nt, _mm_tn = _MM_NN, _MM_NT, _MM_TN
    row, col = _iota((c, c), 0), _iota((c, c), 1)
    incl, strict = row >= col, row > col
    decay = jnp.where(incl, jnp.exp(jnp.where(incl, g1 - g2, 0.0)), 0.0)
    eg = jnp.exp(g1)
    kb = k * beta
    a = _mm_nt(kb, k) * jnp.where(strict, decay, 0.0)
    inv = _unit_lower_inverse(a) if t_inv is None else _SAVED_INVERSE(a, t_inv)
    u = _mm_nn(inv, v * beta)
    w = _mm_nn(inv, kb * eg)
    attn = _mm_nt(q, k) * decay
    v_new = u - _mm_nn(w, state)
    o = _mm_nn(q * eg, state) + _mm_nn(attn, v_new)
    new_state = state * jnp.exp(gl) + _mm_tn(k * jnp.exp(gl - g1), v_new)
    return (o, new_state, inv) if t_inv is None else (o, new_state)


def _unpair(x):
    return jnp.concatenate([x[..., :HEAD_DIM], x[..., HEAD_DIM:]], axis=0)


def _gdn_fwd(q, k, v, beta, g, gl, grow, cpb=4):
    t = q.shape[1]
    rows = cpb * GDN_CHUNK
    lo, hi = slice(0, HEAD_DIM), slice(HEAD_DIM, 2 * HEAD_DIM)

    def body(q_ref, k_ref, v_ref, b_ref, g_ref, gl_ref, grow_ref, o_ref, st_ref, inv_ref, state):
        @pl.when(pl.program_id(0) == 0)
        def _():
            state[...] = jnp.zeros_like(state)

        s = state[...]
        for cc in range(cpb):
            sl = slice(cc * GDN_CHUNK, (cc + 1) * GDN_CHUNK)
            st_ref[:, cc, :, lo], st_ref[:, cc, :, hi] = s[:HEADS // 2], s[HEADS // 2:]
            g2 = jnp.broadcast_to(grow_ref[:, cc], (HEADS, GDN_CHUNK, GDN_CHUNK))
            o, s, inv = _gdn_chunk(*[_unpair(r[:, sl, :]) for r in (q_ref, k_ref, v_ref, b_ref, g_ref)], g2,
                                   _unpair(gl_ref[:, sl, :]), s)
            o_ref[:, sl, lo], o_ref[:, sl, hi] = o[:HEADS // 2], o[HEADS // 2:]
            inv_ref[:, cc, :, lo], inv_ref[:, cc, :, hi] = inv[:HEADS // 2], inv[HEADS // 2:]
        state[...] = s

    hm = pl.BlockSpec((HEADS // 2, rows, 128), lambda i: (0, i, 0))
    per_chunk = pl.BlockSpec((HEADS // 2, cpb, GDN_CHUNK, 128), lambda i: (0, i, 0, 0))
    chunk_shape = jax.ShapeDtypeStruct((HEADS // 2, t // GDN_CHUNK, GDN_CHUNK, 128), F32)
    return pl.pallas_call(
        body, name="gdn_fwd", grid=(t // rows,),
        in_specs=[hm] * 6 + [pl.BlockSpec((HEADS, cpb, 1, GDN_CHUNK), lambda i: (0, i, 0, 0))],
        out_specs=[hm, per_chunk, per_chunk],
        out_shape=[jax.ShapeDtypeStruct((HEADS // 2, t, 128), F32), chunk_shape, chunk_shape],
        scratch_shapes=[pltpu.VMEM((HEADS, GDN_CHUNK, HEAD_DIM), F32)],
        compiler_params=_params(("arbitrary",)),
    )(q, k, v, beta, g, gl, grow)


def _gdn_bwd(q, k, v, beta, g, gl, grow, states, invs, do, cpb=2):
    t = q.shape[1]
    rows = cpb * GDN_CHUNK
    nsteps = t // rows
    lo, hi = slice(0, HEAD_DIM), slice(HEAD_DIM, 2 * HEAD_DIM)

    def body(q_ref, k_ref, v_ref, b_ref, g_ref, gl_ref, grow_ref, st_ref, inv_ref, do_ref,
             dq_ref, dk_ref, dv_ref, db_ref, dg_ref, dgl_ref, dgrow_ref, dstate):
        @pl.when(pl.program_id(0) == 0)
        def _():
            dstate[...] = jnp.zeros_like(dstate)

        ds = dstate[...]
        for cc in reversed(range(cpb)):
            sl = slice(cc * GDN_CHUNK, (cc + 1) * GDN_CHUNK)
            g2 = jnp.broadcast_to(grow_ref[:, cc], (HEADS, GDN_CHUNK, GDN_CHUNK))
            _, vjp = jax.vjp(_gdn_chunk, *[_unpair(r[:, sl, :]) for r in (q_ref, k_ref, v_ref, b_ref, g_ref)], g2,
                             _unpair(gl_ref[:, sl, :]), _unpair(st_ref[:, cc]), _unpair(inv_ref[:, cc]))
            gq, gk, gv, gb, gg1, gg2, ggl, ds, _ = vjp((_unpair(do_ref[:, sl, :]), ds))
            for ref, val in ((dq_ref, gq), (dk_ref, gk), (dv_ref, gv), (db_ref, gb), (dg_ref, gg1), (dgl_ref, ggl)):
                ref[:, sl, lo], ref[:, sl, hi] = val[:HEADS // 2], val[HEADS // 2:]
            dgrow_ref[:, cc] = jnp.sum(gg2, axis=1, keepdims=True)
        dstate[...] = ds

    hm = pl.BlockSpec((HEADS // 2, rows, 128), lambda i: (0, nsteps - 1 - i, 0))
    rowspec = pl.BlockSpec((HEADS, cpb, 1, GDN_CHUNK), lambda i: (0, nsteps - 1 - i, 0, 0))
    per_chunk = pl.BlockSpec((HEADS // 2, cpb, GDN_CHUNK, 128), lambda i: (0, nsteps - 1 - i, 0, 0))
    hm_shape = jax.ShapeDtypeStruct((HEADS // 2, t, 128), F32)
    return pl.pallas_call(
        body, name="gdn_bwd", grid=(nsteps,),
        in_specs=[hm] * 6 + [rowspec, per_chunk, per_chunk, hm],
        out_specs=[hm] * 6 + [rowspec],
        out_shape=[hm_shape] * 6 + [jax.ShapeDtypeStruct((HEADS, t // GDN_CHUNK, 1, GDN_CHUNK), F32)],
        scratch_shapes=[pltpu.VMEM((HEADS, GDN_CHUNK, HEAD_DIM), F32)],
        compiler_params=_params(("arbitrary",)),
    )(q, k, v, beta, g, gl, grow, states, invs, do)


def _gdn_out(o_hm, gdn_norm_w, proj, tm=512):
    t = proj.shape[0]

    def body(o_ref, w_ref, z_ref, oz_ref):
        w = w_ref[...]
        ones = _pair_ones()
        for p in range(HEADS // 2):
            cols = slice(p * 128, (p + 1) * 128)
            o = o_ref[p]
            z = z_ref[:, cols]
            r = lax.rsqrt(_pair_sum_lanes(o * o, ones) * (1.0 / HEAD_DIM) + NORM_EPS)
            oz_ref[:, cols] = (o * r * w * (z * _sigmoid(z))).astype(oz_ref.dtype)

    tok = pl.BlockSpec((tm, WIDTH), lambda i: (i, 0))
    return pl.pallas_call(
        body, name="gdn_out", grid=(t // tm,),
        in_specs=[pl.BlockSpec((HEADS // 2, tm, 128), lambda i: (0, i, 0)), pl.BlockSpec((1, 128), lambda i: (0, 0)),
                  pl.BlockSpec((tm, WIDTH), lambda i: (i, SEG_ZB // WIDTH))],
        out_specs=tok, out_shape=jax.ShapeDtypeStruct((t, WIDTH), MXU_DTYPE),
        compiler_params=_params(("parallel",)),
    )(o_hm, jnp.tile(gdn_norm_w, (1, 2)), proj)


def _gdn_out_bwd(dproj, d_oz, o_hm, gdn_norm_w, proj, tm=512):
    t = proj.shape[0]

    def body(dp_ref, doz_ref, o_ref, w_ref, z_ref, dz_ref, do_ref, dw_ref):
        w = w_ref[...]
        ones = _pair_ones()
        dw = jnp.zeros((1, 128), F32)
        for p in range(HEADS // 2):
            cols = slice(p * 128, (p + 1) * 128)
            o = o_ref[p]
            z, g = z_ref[:, cols], doz_ref[:, cols]
            sg = _sigmoid(z)
            r = lax.rsqrt(_pair_sum_lanes(o * o, ones) * (1.0 / HEAD_DIM) + NORM_EPS)
            dz_ref[:, cols] = (g * (o * r * w) * (sg * (1.0 + z * (1.0 - sg)))).astype(dz_ref.dtype)
            dn = g * (z * sg)
            dw += jnp.sum(dn * o * r, axis=0, keepdims=True)
            dnw = dn * w
            do_ref[p] = r * dnw - o * (r * r * r) * (_pair_sum_lanes(dnw * o, ones) * (1.0 / HEAD_DIM))

        @pl.when(pl.program_id(0) == 0)
        def _():
            dw_ref[...] = jnp.zeros_like(dw_ref)

        dw_ref[...] += jnp.where(_iota((8, 128), 0) == 0, dw, 0.0)

    tok = pl.BlockSpec((tm, WIDTH), lambda i: (i, 0))
    seg = pl.BlockSpec((tm, WIDTH), lambda i: (i, SEG_ZB // WIDTH))
    hm = pl.BlockSpec((HEADS // 2, tm, 128), lambda i: (0, i, 0))
    dz, do, dw = pl.pallas_call(
        body, name="gdn_out_bwd", grid=(t // tm,),
        in_specs=[pl.BlockSpec(memory_space=pl.ANY), tok, hm, pl.BlockSpec((1, 128), lambda i: (0, 0)), seg],
        out_specs=[seg, hm, pl.BlockSpec((8, 128), lambda i: (0, 0))],
        out_shape=[jax.ShapeDtypeStruct((t, PACKED_WIDTH), MXU_DTYPE), jax.ShapeDtypeStruct((HEADS // 2, t, 128), F32),
                   jax.ShapeDtypeStruct((8, 128), F32)],
        input_output_aliases={0: 0},
        compiler_params=_params(("arbitrary",)),
    )(dproj, d_oz, o_hm, jnp.tile(gdn_norm_w, (1, 2)), proj)
    return dz, do, dw[:, :HEAD_DIM] + dw[:, HEAD_DIM:]


def _merge(y_a, y_b, proj, tm=1024):
    t = proj.shape[0]

    def body(ya_ref, yb_ref, ga_ref, gb_ref, m_ref):
        m_ref[...] = (_sigmoid(ga_ref[...]) * ya_ref[...] + _sigmoid(gb_ref[...]) * yb_ref[...]).astype(m_ref.dtype)

    half = pl.BlockSpec((tm, WIDTH), lambda i, c: (i, c))
    return pl.pallas_call(
        body, name="merge", grid=(t // tm, 2),
        in_specs=[half, half, pl.BlockSpec((tm, WIDTH), lambda i, c: (i, SEG_GA // WIDTH + c)),
                  pl.BlockSpec((tm, WIDTH), lambda i, c: (i, SEG_GB // WIDTH + c))],
        out_specs=half, out_shape=jax.ShapeDtypeStruct((t, D_MODEL), MXU_DTYPE),
        compiler_params=_params(("parallel", "parallel")),
    )(y_a, y_b, proj, proj)


def _merge_bwd(dproj, d_m, y, proj, seg, name, tm=1024):
    t = proj.shape[0]

    def body(*refs):
        dm_ref, y_ref, g_ref, dg_ref, dy_ref = refs[-5:]
        dm = dm_ref[...]
        s = _sigmoid(g_ref[...])
        dy_ref[...] = (dm * s).astype(dy_ref.dtype)
        dg_ref[...] = (dm * y_ref[...] * s * (1.0 - s)).astype(dg_ref.dtype)

    half = pl.BlockSpec((tm, WIDTH), lambda i, c: (i, c))
    gate = pl.BlockSpec((tm, WIDTH), lambda i, c: (i, seg // WIDTH + c))
    specs, args, aliases = [half, half, gate], [d_m, y, proj], {}
    if dproj is not None:
        specs, args, aliases = [pl.BlockSpec(memory_space=pl.ANY)] + specs, [dproj] + args, {0: 0}
    return pl.pallas_call(
        body, name=name, grid=(t // tm, 2), in_specs=specs, out_specs=[gate, half],
        out_shape=[jax.ShapeDtypeStruct((t, PACKED_WIDTH), MXU_DTYPE), jax.ShapeDtypeStruct((t, D_MODEL), MXU_DTYPE)],
        input_output_aliases=aliases,
        compiler_params=_params(("parallel", "parallel")),
    )(*args)


def _out_tail(merged, w_out, x, final_w, target, tm=1024):
    t = x.shape[0]
    tm = min(tm, t)

    def body(m_ref, wo_ref, x_ref, w_ref, t_ref, dxm_ref, dx_ref, loss_ref, dw_ref):
        x2 = x_ref[...] + jnp.dot(m_ref[...], wo_ref[...], preferred_element_type=F32)
        w = w_ref[...]
        r = lax.rsqrt(jnp.mean(x2 * x2, axis=-1, keepdims=True) + NORM_EPS)
        xn = x2 * r
        err = xn * w - t_ref[...]
        dy = err * (1.0 / D_MODEL)
        dyw = dy * w
        dx2 = r * dyw - x2 * (r * r * r) * jnp.mean(dyw * x2, axis=-1, keepdims=True)
        dx_ref[...] = dx2
        dxm_ref[...] = dx2.astype(dxm_ref.dtype)
        loss = 0.5 * jnp.sum(jnp.sum(err * err, axis=-1, keepdims=True) * (1.0 / D_MODEL), axis=0, keepdims=True)
        onehot = jnp.where((_iota((8, 128), 0) == 0) & (_iota((8, 128), 1) == 0), 1.0, 0.0)

        @pl.when(pl.program_id(0) == 0)
        def _():
            loss_ref[...] = jnp.zeros_like(loss_ref)
            dw_ref[...] = jnp.zeros_like(dw_ref)

        loss_ref[...] += loss * onehot
        dw_ref[...] += jnp.where(_iota((8, D_MODEL), 0) == 0, jnp.sum(dy * xn, axis=0, keepdims=True), 0.0)

    tok = pl.BlockSpec((tm, D_MODEL), lambda i: (i, 0))
    full = pl.BlockSpec((D_MODEL, D_MODEL), lambda i: (0, 0))
    return pl.pallas_call(
        body, name="out_tail", grid=(t // tm,),
        in_specs=[tok, full, tok, pl.BlockSpec((1, D_MODEL), lambda i: (0, 0)), tok],
        out_specs=[tok, tok, pl.BlockSpec((8, 128), lambda i: (0, 0)), pl.BlockSpec((8, D_MODEL), lambda i: (0, 0))],
        out_shape=[jax.ShapeDtypeStruct((t, D_MODEL), MXU_DTYPE), jax.ShapeDtypeStruct((t, D_MODEL), F32),
                   jax.ShapeDtypeStruct((8, 128), F32), jax.ShapeDtypeStruct((8, D_MODEL), F32)],
        compiler_params=_params(("arbitrary",)),
    )(merged, w_out, x, final_w, target)


def _dh_norm_bwd(dproj, wp, x, norm_w, dx2, after, tm=1024, tk=PACKED_WIDTH // 4):
    t = x.shape[0]
    tm = min(tm, t)
    nk = PACKED_WIDTH // tk

    def body(dp_ref, wp_ref, x_ref, w_ref, dx2_ref, after_ref, dx_ref, dw_ref, acc):
        kk = pl.program_id(1)
        part = _mxu(dp_ref[...], wp_ref[...], _NT)

        @pl.when(kk == 0)
        def _():
            acc[...] = part

        @pl.when(kk > 0)
        def _():
            acc[...] += part

        @pl.when((kk == 0) & (pl.program_id(0) == 0))
        def _():
            dw_ref[...] = jnp.zeros_like(dw_ref)

        @pl.when(kk == nk - 1)
        def _():
            xf, w, dh_ = x_ref[...], w_ref[...], acc[...]
            r = lax.rsqrt(jnp.mean(xf * xf, axis=-1, keepdims=True) + NORM_EPS)
            dhw = dh_ * w
            dx_ref[...] = dx2_ref[...] + r * dhw - xf * (r * r * r) * jnp.mean(dhw * xf, axis=-1, keepdims=True)
            dw_ref[...] += jnp.where(_iota((8, D_MODEL), 0) == 0, jnp.sum(dh_ * xf * r, axis=0, keepdims=True), 0.0)

    tok = pl.BlockSpec((tm, D_MODEL), lambda i, kk: (i, 0))
    return pl.pallas_call(
        body, name="dh_norm_bwd", grid=(t // tm, nk),
        in_specs=[pl.BlockSpec((tm, tk), lambda i, kk: (i, kk)), pl.BlockSpec((D_MODEL, tk), lambda i, kk: (0, kk)),
                  tok, pl.BlockSpec((1, D_MODEL), lambda i, kk: (0, 0)), tok, pl.BlockSpec(memory_space=pl.ANY)],
        out_specs=[tok, pl.BlockSpec((8, D_MODEL), lambda i, kk: (0, 0))],
        out_shape=[jax.ShapeDtypeStruct((t, D_MODEL), F32), jax.ShapeDtypeStruct((8, D_MODEL), F32)],
        scratch_shapes=[pltpu.VMEM((tm, D_MODEL), F32)],
        compiler_params=_params(("arbitrary", "arbitrary")),
    )(dproj, wp, x, norm_w, dx2, after)


def _local_step(x, target, norm_w, wp, conv_w, a_log, dt_bias, gdn_norm_w, w_up_a, w_up_b, w_out, final_w, start_reduce):
    t = x.shape[0]
    tables = _rope_tables(t)
    a_log = jnp.pad(a_log, ((0, 0), (HEADS, 128 - 2 * HEADS)))
    dt_bias = jnp.pad(dt_bias, ((0, 0), (HEADS, 128 - 2 * HEADS)))

    proj, h_t = _norm_proj(x, norm_w, wp)
    qkvs = _rope_fwd(proj, tables)
    outs, lses = zip(*[_att_fwd(qkvs[gi], d, f"att_fwd{gi}") for gi, d in enumerate(DILATIONS)])
    oz_a, o_a, *lse_views = _att_merge(outs, lses, proj)
    conv = _conv_fwd(proj, conv_w)
    gq, gk, gv, gb, gg, ggl, grow = _gdn_prep(conv, proj, a_log, dt_bias)
    o_b, states, invs = _gdn_fwd(gq, gk, gv, gb, gg, ggl, grow)
    oz_b = _gdn_out(o_b, gdn_norm_w, proj)
    big = dict(tm=1024, tn=1024, tk=1024)
    y_a = _matmul(oz_a, w_up_a, "nn", "up_a", **big)
    y_b = _matmul(oz_b, w_up_b, "nn", "up_b", **big)
    merged = _merge(y_a, y_b, proj)
    dx2_m, dx2, loss_blk, d_final = _out_tail(merged, w_out, x, final_w, target)

    d_wout = _matmul(merged, dx2_m, "tn", "d_w_out", **big)
    d_m = _matmul(dx2_m, w_out, "nt", "d_merged", **big)
    dproj, dy_a = _merge_bwd(None, d_m, y_a, proj, SEG_GA, "merge_bwd_a")
    dproj, dy_b = _merge_bwd(dproj, d_m, y_b, proj, SEG_GB, "merge_bwd_b")
    d_wua = _matmul(oz_a, dy_a, "tn", "d_w_up_a", **big)
    d_wub = _matmul(oz_b, dy_b, "tn", "d_w_up_b", **big)
    d_oz_a = _matmul(dy_a, w_up_a, "nt", "d_oz_a", **big)
    d_oz_b = _matmul(dy_b, w_up_b, "nt", "d_oz_b", **big)
    dproj, *views = _att_merge_bwd(dproj, d_oz_a, o_a, proj)
    do_views, delta_views = views[:3], views[3:]
    dqs, dkvs = zip(*[_att_bwd(qkvs[gi], do_views[gi], lse_views[gi], delta_views[gi], d, f"att_bwd{gi}")
                      for gi, d in enumerate(DILATIONS)])
    dproj = _rope_bwd(dproj, dqs, dkvs, tables)
    dproj, do_b, d_gnw = _gdn_out_bwd(dproj, d_oz_b, o_b, gdn_norm_w, proj)
    dgq, dgk, dgv, dgb, dgg, dggl, dgrow = _gdn_bwd(gq, gk, gv, gb, gg, ggl, grow, states, invs, do_b)
    dproj, dconv, d_small = _gdn_prep_bwd(dproj, conv, proj, a_log, dt_bias, dgq, dgk, dgv, dgb, dgg, dggl, dgrow)
    dproj, d_convw = _conv_bwd(dproj, dconv, proj, conv_w)
    d_wp = _matmul(h_t, dproj, "nn", "d_w_in", tm=1024, tn=PACKED_WIDTH // 4, tk=1024)
    in_flight, token = start_reduce(d_wp, d_wua, d_wub, d_wout, d_convw[0:GDN_CONV])
    grad_x, d_norm = _dh_norm_bwd(dproj, wp, x, norm_w, dx2, token)
    return dict(loss=loss_blk, grad_x=grad_x, norm_w=d_norm[0:1], in_flight=in_flight,
                a_log=d_small[0:1, HEADS:2 * HEADS], dt_bias=d_small[1:2, HEADS:2 * HEADS], gdn_norm_w=d_gnw[0:1],
                final_norm_w=d_final[0:1])


SHARDS = 4
W_IN_SHARD = IN_WIDTH // SHARDS
ROWS_UP = WIDTH * (D_MODEL // SHARDS) // 128
ROWS_OUT = (D_MODEL // SHARDS) * D_MODEL // 128
CONV_SHARD = 3 * WIDTH // SHARDS
ROWS_CONV = 16
SLAB_ROWS = 2 * ROWS_UP + ROWS_OUT + 2 * ROWS_CONV
HALF_ROWS = SLAB_ROWS // 2
BIG_HALF = (D_MODEL // 2, W_IN_SHARD)
SMALL_HALF = (HALF_ROWS, 128)
MESH = pl.DeviceIdType.MESH
ANY = pl.BlockSpec(memory_space=pl.ANY)


def _pad_rows(a, rows):
    return jnp.pad(a, ((0, rows - a.shape[0]), (0, 0)))


def _pack_slab(w_up_a, w_up_b, w_out, conv, conv_lo):
    parts = [w_up_a.reshape(ROWS_UP, 128), w_up_b.reshape(ROWS_UP, 128), w_out.reshape(ROWS_OUT, 128),
             _pad_rows(conv.reshape(-1, 128), ROWS_CONV), _pad_rows(conv_lo.reshape(-1, 128), ROWS_CONV)]
    return jnp.concatenate(parts, axis=0).reshape(2, *SMALL_HALF)


def _unpack_slab(slab):
    slab = slab.reshape(SLAB_ROWS, 128)
    r0 = 0
    out = []
    for rows, shape in ((ROWS_UP, (WIDTH, D_MODEL // SHARDS)), (ROWS_UP, (WIDTH, D_MODEL // SHARDS)),
                        (ROWS_OUT, (D_MODEL // SHARDS, D_MODEL)), (ROWS_CONV, None), (ROWS_CONV, None)):
        part = slab[r0:r0 + rows]
        out.append(part[:GDN_CONV * CONV_SHARD // 128].reshape(GDN_CONV, CONV_SHARD) if shape is None else part.reshape(shape))
        r0 += rows
    return out


def _mesh_position():
    x, y, c = lax.axis_index("x"), lax.axis_index("y"), lax.axis_index("c")
    return x, y, c, [(1 - x, y), (x, 1 - y), (1 - x, 1 - y)]


def _gather_weights(shards):
    n = len(shards)

    def body(*refs):
        in_refs, out_refs, (send_sems, recv_sems) = refs[:n], refs[n:2 * n], refs[2 * n:]
        x, y, c, chips = _mesh_position()

        def half(a, chip, which):
            return out_refs[a].at[2 * chip[0] + chip[1], which]

        def copy(k, src, dst, to):
            return pltpu.make_async_remote_copy(src_ref=src, dst_ref=dst, send_sem=send_sems.at[k], recv_sem=recv_sems.at[k],
                                                device_id=to, device_id_type=MESH)

        pairs = [(a, j, chip) for a in range(n) for j, chip in enumerate(chips)]
        first = [copy(6 * a + j, in_refs[a].at[c], half(a, (x, y), c), (*chip, c)) for a, j, chip in pairs]
        for cp in first:
            cp.start()
        passed = [copy(6 * a + 3 + j, half(a, chip, c), half(a, chip, c), (x, y, 1 - c)) for a, j, chip in pairs]
        for i, (a, j, chip) in enumerate(pairs):
            copy(6 * a + j, half(a, chip, c), half(a, chip, c), (x, y, c)).wait_recv()
            passed[i].start()
        for a, j, chip in pairs:
            copy(6 * a + 3 + j, half(a, chip, 1 - c), half(a, chip, 1 - c), (x, y, c)).wait_recv()
        for cp in first + passed:
            cp.wait_send()

    return pl.pallas_call(
        body, name="gather_weights", in_specs=[ANY] * n, out_specs=[ANY] * n,
        out_shape=[jax.ShapeDtypeStruct((SHARDS, *s.shape), s.dtype) for s in shards],
        scratch_shapes=[pltpu.SemaphoreType.DMA((6 * n,)), pltpu.SemaphoreType.DMA((6 * n,))],
    )(*shards)


def _exchange_halves(grads):
    n = len(grads)

    def body(*refs):
        g_refs, out_refs, (send_sems, recv_sems) = refs[:n], refs[n:2 * n], refs[2 * n:]
        x, y, c, _ = _mesh_position()
        copies = [pltpu.make_async_remote_copy(src_ref=g_refs[a].at[s, 1 - c], dst_ref=out_refs[a].at[s],
                                               send_sem=send_sems.at[SHARDS * a + s], recv_sem=recv_sems.at[SHARDS * a + s],
                                               device_id=(x, y, 1 - c), device_id_type=MESH)
                  for a in range(n) for s in range(SHARDS)]
        for cp in copies:
            cp.start()
        for cp in copies:
            cp.wait()

    return pl.pallas_call(
        body, name="exchange_halves", in_specs=[ANY] * n, out_specs=[ANY] * n,
        out_shape=[jax.ShapeDtypeStruct((SHARDS, *g.shape[2:]), F32) for g in grads],
        scratch_shapes=[pltpu.SemaphoreType.DMA((SHARDS * n,)), pltpu.SemaphoreType.DMA((SHARDS * n,))],
    )(*grads)


def _pair_sum(grads, recv, blk, name):
    _, _, rows, cols = grads.shape

    def body(c_ref, g_ref, r_ref, o_ref):
        o_ref[...] = (g_ref[0] + r_ref[...]).astype(o_ref.dtype)

    spec = pl.BlockSpec((1, blk, cols), lambda s, i, c_ref: (s, i, 0))
    return pl.pallas_call(
        body, name=name,
        grid_spec=pltpu.PrefetchScalarGridSpec(
            num_scalar_prefetch=1, grid=(SHARDS, rows // blk),
            in_specs=[pl.BlockSpec((1, 1, blk, cols), lambda s, i, c_ref: (s, c_ref[0], i, 0)), spec],
            out_specs=spec),
        out_shape=jax.ShapeDtypeStruct((SHARDS, rows, cols), MXU_DTYPE),
        compiler_params=_params(("parallel", "parallel")),
    )(lax.axis_index("c").astype(jnp.int32).reshape(1), grads, recv)


_HBM = pl.BlockSpec(memory_space=pltpu.HBM)
_SEM = pl.BlockSpec(memory_space=pltpu.SEMAPHORE)
_DATAFLOW = pltpu.SideEffectType.DATAFLOW_SIDE_EFFECTING


def _scatter_copies(p_refs, l_refs, send_sems, recv_sems):
    x, y, c, chips = _mesh_position()
    return [pltpu.make_async_remote_copy(src_ref=p_refs[a].at[2 * chip[0] + chip[1]], dst_ref=l_refs[a].at[j],
                                         send_sem=send_sems.at[3 * a + j], recv_sem=recv_sems.at[3 * a + j],
                                         device_id=(*chip, c), device_id_type=MESH)
            for a in range(len(p_refs)) for j, chip in enumerate(chips)]


def _scatter_start(pairs):
    n = len(pairs)
    lands = [lax.empty((3, *p.shape[1:]), p.dtype) for p in pairs]

    def body(*refs):
        p_refs, l_refs, send_sems, recv_sems, token = refs[:n], refs[n:2 * n], refs[2 * n], refs[2 * n + 1], refs[-1]
        for cp in _scatter_copies(p_refs, l_refs, send_sems, recv_sems):
            cp.start()
        token[...] = jnp.zeros_like(token)

    operands = [pltpu.with_memory_space_constraint(a, pltpu.HBM) for a in (*pairs, *lands)]
    return pl.pallas_call(
        body, name="scatter_start", in_specs=[_HBM] * (2 * n),
        out_shape=(pltpu.SemaphoreType.DMA((3 * n,)), pltpu.SemaphoreType.DMA((3 * n,)),
                   *[pltpu.HBM(a.shape, a.dtype) for a in operands], jax.ShapeDtypeStruct((8, 128), F32)),
        out_specs=(_SEM, _SEM, *[_HBM] * (2 * n), pl.BlockSpec(memory_space=pltpu.VMEM)),
        input_output_aliases={i: 2 + i for i in range(2 * n)},
        compiler_params=pltpu.CompilerParams(has_side_effects=_DATAFLOW),
    )(*operands)


def _scatter_wait(send_sems, recv_sems, passed, after):
    n = len(passed) // 2

    def body(*refs):
        p_refs, l_refs, send_s, recv_s = refs[:n], refs[n:2 * n], refs[2 * n], refs[2 * n + 1]
        for cp in _scatter_copies(p_refs, l_refs, send_s, recv_s):
            cp.wait_send()
            cp.wait_recv()

    return pl.pallas_call(
        body, name="scatter_wait", in_specs=[_HBM] * (2 * n) + [_SEM, _SEM, ANY],
        out_shape=[pltpu.HBM(a.shape, a.dtype) for a in passed], out_specs=[_HBM] * (2 * n),
        input_output_aliases={i: i for i in range(2 * n)},
        compiler_params=pltpu.CompilerParams(has_side_effects=_DATAFLOW),
    )(*passed, send_sems, recv_sems, after)


def _chip_sum(pairs, recv, blk, name):
    _, rows, cols = pairs.shape

    def body(pos_ref, p_ref, r_ref, o_ref):
        o_ref[0] = ((p_ref[0].astype(F32) + r_ref[0].astype(F32)) + r_ref[1].astype(F32)) + r_ref[2].astype(F32)

    pos = jnp.stack([2 * lax.axis_index("x") + lax.axis_index("y"), lax.axis_index("c")]).astype(jnp.int32)
    return pl.pallas_call(
        body, name=name,
        grid_spec=pltpu.PrefetchScalarGridSpec(
            num_scalar_prefetch=1, grid=(rows // blk,),
            in_specs=[pl.BlockSpec((1, blk, cols), lambda i, pos_ref: (pos_ref[0], i, 0)),
                      pl.BlockSpec((3, blk, cols), lambda i, pos_ref: (0, i, 0))],
            out_specs=pl.BlockSpec((1, blk, cols), lambda i, pos_ref: (pos_ref[1], i, 0))),
        out_shape=jax.ShapeDtypeStruct((2, rows, cols), F32),
        compiler_params=_params(("parallel",)),
    )(pos, pairs, recv)


def _share_total(totals):
    n = len(totals)

    def body(*refs):
        t_refs, out_refs, (send_sems, recv_sems) = refs[:n], refs[n:2 * n], refs[2 * n:]
        x, y, c, _ = _mesh_position()
        copies = [pltpu.make_async_remote_copy(src_ref=t_refs[a].at[c], dst_ref=out_refs[a].at[c], send_sem=send_sems.at[a],
                                               recv_sem=recv_sems.at[a], device_id=(x, y, 1 - c), device_id_type=MESH)
                  for a in range(n)]
        for cp in copies:
            cp.start()
        for a in range(n):
            other = out_refs[a].at[1 - c]
            pltpu.make_async_remote_copy(src_ref=other, dst_ref=other, send_sem=send_sems.at[a], recv_sem=recv_sems.at[a],
                                         device_id=(x, y, c), device_id_type=MESH).wait_recv()
        for cp in copies:
            cp.wait_send()

    return pl.pallas_call(
        body, name="share_total", in_specs=[ANY] * n, out_specs=[ANY] * n,
        out_shape=[jax.ShapeDtypeStruct(t.shape, F32) for t in totals],
        scratch_shapes=[pltpu.SemaphoreType.DMA((n,)), pltpu.SemaphoreType.DMA((n,))],
        input_output_aliases={a: a for a in range(n)},
    )(*totals)


def _allreduce_small(block):
    def body(b_ref, out_ref, gath, send_sems, recv_sems):
        x, y, c, _ = _mesh_position()
        me = 4 * x + 2 * y + c
        gath[me] = b_ref[...]
        copies = []
        for k in range(1, 8):
            peer = (x ^ (k >> 2), y ^ ((k >> 1) & 1), c ^ (k & 1))
            copies.append(pltpu.make_async_remote_copy(src_ref=b_ref, dst_ref=gath.at[me], send_sem=send_sems.at[k - 1],
                                                       recv_sem=recv_sems.at[k - 1], device_id=peer, device_id_type=MESH))
        for cp in copies:
            cp.start()
        for k in range(1, 8):
            src = 4 * (x ^ (k >> 2)) + 2 * (y ^ ((k >> 1) & 1)) + (c ^ (k & 1))
            pltpu.make_async_remote_copy(src_ref=b_ref, dst_ref=gath.at[src], send_sem=send_sems.at[k - 1],
                                         recv_sem=recv_sems.at[k - 1], device_id=(x, y, c), device_id_type=MESH).wait_recv()
        for cp in copies:
            cp.wait_send()
        acc = gath[0]
        for d in range(1, 8):
            acc = acc + gath[d]
        out_ref[...] = acc

    vm = pl.BlockSpec(memory_space=pltpu.VMEM)
    return pl.pallas_call(
        body, name="allreduce_small", in_specs=[vm], out_specs=vm,
        out_shape=jax.ShapeDtypeStruct((8, D_MODEL), F32),
        scratch_shapes=[pltpu.VMEM((8, 8, D_MODEL), F32), pltpu.SemaphoreType.DMA((7,)), pltpu.SemaphoreType.DMA((7,))],
    )(block)


def _adamw(w, g, m, v, name):
    rows, cols = w.shape
    tr = 128 if rows % 128 == 0 else rows

    def body(w_ref, g_ref, m_ref, v_ref, d_ref, nm_ref, nv_ref):
        gv = g_ref[...]
        nm = ADAM_B1 * m_ref[...] + (1.0 - ADAM_B1) * gv
        nv = ADAM_B2 * v_ref[...] + (1.0 - ADAM_B2) * (gv * gv)
        m_hat = nm / (1.0 - ADAM_B1 ** ADAM_STEP)
        v_hat = nv / (1.0 - ADAM_B2 ** ADAM_STEP)
        d_ref[...] = -ADAM_LR * (m_hat / (jnp.sqrt(v_hat) + ADAM_EPS) + ADAM_WD * w_ref[...])
        nm_ref[...] = nm
        nv_ref[...] = nv

    spec = pl.BlockSpec((tr, cols), lambda i: (i, 0))
    shape = jax.ShapeDtypeStruct((rows, cols), F32)
    return pl.pallas_call(
        body, name=name, grid=(rows // tr,), in_specs=[spec] * 4, out_specs=[spec] * 3, out_shape=[shape] * 3,
        compiler_params=_params(("parallel",)),
    )(w, g, m, v)


def kernel(x, norm_w, w_in, conv_w, a_log, dt_bias, gdn_norm_w, w_up_a, w_up_b, w_out, final_norm_w, loss_target, m_norm_w, m_w_in, m_conv_w, m_a_log, m_dt_bias, m_gdn_norm_w, m_w_up_a, m_w_up_b, m_w_out, m_final_norm_w, v_norm_w, v_w_in, v_conv_w, v_a_log, v_dt_bias, v_gdn_norm_w, v_w_up_a, v_w_up_b, v_w_out, v_final_norm_w):
    conv_hi = conv_w[0].astype(MXU_DTYPE)
    conv_lo = (conv_w[0] - conv_hi.astype(F32)).astype(MXU_DTYPE)
    big = w_in[0].astype(MXU_DTYPE).reshape(2, *BIG_HALF)
    slab = _pack_slab(w_up_a[0].astype(MXU_DTYPE), w_up_b[0].astype(MXU_DTYPE), w_out[0].astype(MXU_DTYPE), conv_hi, conv_lo)
    own_shard = 2 * lax.axis_index("x") + lax.axis_index("y")
    bigs, slabs = _gather_weights([big, slab])
    bigs = lax.dynamic_update_slice(bigs, big[None], (own_shard, 0, 0, 0)).reshape(SHARDS, D_MODEL, W_IN_SHARD)
    slabs = lax.dynamic_update_slice(slabs, slab[None], (own_shard, 0, 0, 0))
    parts = [_unpack_slab(slabs[s]) for s in range(SHARDS)]
    split = BA_END - (SHARDS - 1) * W_IN_SHARD
    wp = jnp.concatenate([bigs[s] for s in range(SHARDS - 1)]
                         + [bigs[-1][:, :split], jnp.zeros((D_MODEL, SEG_GA - BA_END), MXU_DTYPE), bigs[-1][:, split:]], axis=1)
    w_up_a_full = jnp.concatenate([p[0] for p in parts], axis=1)
    w_up_b_full = jnp.concatenate([p[1] for p in parts], axis=1)
    w_out_full = jnp.concatenate([p[2] for p in parts], axis=0)
    conv_full = jnp.concatenate([p[3].astype(F32) + p[4].astype(F32) for p in parts], axis=1)

    blocks, tags = (128, HALF_ROWS), ("w_in", "slab")

    def start_reduce(d_wp, d_w_up_a, d_w_up_b, d_w_out, d_conv_w):
        d_w_in = [d_wp[:, s * W_IN_SHARD:(s + 1) * W_IN_SHARD] for s in range(SHARDS - 1)]
        d_w_in.append(jnp.concatenate([d_wp[:, (SHARDS - 1) * W_IN_SHARD:BA_END], d_wp[:, SEG_GA:]], axis=1))
        zero_conv = jnp.zeros((GDN_CONV, CONV_SHARD), F32)
        grads = [jnp.stack(d_w_in).reshape(SHARDS, 2, *BIG_HALF),
                 jnp.stack([_pack_slab(d_w_up_a[:, s * 256:(s + 1) * 256], d_w_up_b[:, s * 256:(s + 1) * 256],
                                       d_w_out[s * 256:(s + 1) * 256], d_conv_w[:, s * CONV_SHARD:(s + 1) * CONV_SHARD],
                                       zero_conv) for s in range(SHARDS)])]
        from_sibling = _exchange_halves(grads)
        pairs = [_pair_sum(gr, fs, blk, f"pair_sum_{tag}") for gr, fs, blk, tag in zip(grads, from_sibling, blocks, tags)]
        *in_flight, token = _scatter_start(pairs)
        return in_flight, token

    g = _local_step(x[0], loss_target[0], norm_w, wp, conv_full, a_log, dt_bias, gdn_norm_w,
                    w_up_a_full, w_up_b_full, w_out_full, final_norm_w[None], start_reduce)

    send_sems, recv_sems, *passed = g["in_flight"]
    arrived = _scatter_wait(send_sems, recv_sems, passed, after=g["grad_x"])
    pairs, from_chips = arrived[:2], arrived[2:]
    total_big, total_slab = _share_total([_chip_sum(p, fc, blk, f"chip_sum_{tag}")
                                          for p, fc, blk, tag in zip(pairs, from_chips, blocks, tags)])
    g_w_in = total_big.reshape(D_MODEL, W_IN_SHARD)
    g_w_up_a, g_w_up_b, g_w_out, g_conv, _ = _unpack_slab(total_slab)

    row2 = jnp.concatenate([g["gdn_norm_w"], g["a_log"], g["dt_bias"], g["loss"][0:1, 0:1],
                            jnp.zeros((1, D_MODEL - HEAD_DIM - 2 * HEADS - 1), F32)], axis=1)
    small = _allreduce_small(jnp.concatenate([g["norm_w"], g["final_norm_w"], row2, jnp.zeros((5, D_MODEL), F32)], axis=0))
    g_norm, g_final = small[0:1], small[1]
    g_gnw, g_alog, g_dt = small[2:3, 0:HEAD_DIM], small[2:3, HEAD_DIM:HEAD_DIM + HEADS], small[2:3, HEAD_DIM + HEADS:HEAD_DIM + 2 * HEADS]
    loss = small[2, HEAD_DIM + 2 * HEADS]

    names = ["norm_w", "w_in", "conv_w", "a_log", "dt_bias", "gdn_norm_w", "w_up_a", "w_up_b", "w_out", "final_norm_w"]
    weights = dict(zip(names, (norm_w, w_in, conv_w, a_log, dt_bias, gdn_norm_w, w_up_a, w_up_b, w_out, final_norm_w)))
    ms = dict(zip(names, (m_norm_w, m_w_in, m_conv_w, m_a_log, m_dt_bias, m_gdn_norm_w, m_w_up_a, m_w_up_b, m_w_out, m_final_norm_w)))
    vs = dict(zip(names, (v_norm_w, v_w_in, v_conv_w, v_a_log, v_dt_bias, v_gdn_norm_w, v_w_up_a, v_w_up_b, v_w_out, v_final_norm_w)))
    grads2d = dict(norm_w=g_norm, w_in=g_w_in, conv_w=g_conv, a_log=g_alog, dt_bias=g_dt, gdn_norm_w=g_gnw,
                   w_up_a=g_w_up_a, w_up_b=g_w_up_b, w_out=g_w_out, final_norm_w=g_final[None])
    grad_out, delta, new_m, new_v = [], [], [], []
    for n in names:
        shape = weights[n].shape
        two_d = grads2d[n].shape
        d, nm, nv = _adamw(weights[n].reshape(two_d), grads2d[n], ms[n].reshape(two_d), vs[n].reshape(two_d), f"adamw_{n}")
        grad_out.append(grads2d[n].reshape(shape))
        delta.append(d.reshape(shape))
        new_m.append(nm.reshape(shape))
        new_v.append(nv.reshape(shape))
    return (loss, g["grad_x"][None], *grad_out, *delta, *new_m, *new_v)
```

```python
import functools

import jax
import jax.numpy as jnp
from jax import lax
from jax.experimental import pallas as pl
from jax.experimental.pallas import tpu as pltpu

F32 = jnp.float32
MXU_DTYPE = jnp.bfloat16
HIGHEST = lax.Precision.HIGHEST

D_MODEL = 1024
HEADS = 8
HEAD_DIM = 64
WIDTH = HEADS * HEAD_DIM
NORM_EPS = 1e-6
ROPE_THETA = 10000.0
ATT_BLOCK = 128
DILATIONS = (1, 4, 16)
GDN_CHUNK = 64
GDN_CONV = 4
IN_WIDTH = 9232
SEG_A, SEG_ZA, SEG_B, SEG_ZB, SEG_BA, SEG_GA, SEG_GB, PACKED_WIDTH = 0, 4608, 5120, 6656, 7168, 7680, 8704, 9728
BA_END = 7184
VMEM_LIMIT = 56 * 1024 * 1024

ADAM_LR, ADAM_B1, ADAM_B2, ADAM_EPS, ADAM_WD, ADAM_STEP = 0.001, 0.9, 0.999, 1e-08, 0.01, 10

_NN = (((1,), (0,)), ((), ()))
_NT = (((1,), (1,)), ((), ()))
_TN = (((0,), (0,)), ((), ()))


def _params(sem):
    return pltpu.CompilerParams(dimension_semantics=sem, vmem_limit_bytes=VMEM_LIMIT)


def _mxu(a, b, dims):
    return lax.dot_general(a.astype(MXU_DTYPE), b.astype(MXU_DTYPE), dims, preferred_element_type=F32)


def _sigmoid(x):
    return 1.0 / (1.0 + jnp.exp(-x))


def _softplus(x):
    return jnp.maximum(x, 0.0) + jnp.log(1.0 + jnp.exp(-jnp.abs(x)))


def _iota(shape, axis):
    return lax.broadcasted_iota(jnp.int32, shape, axis)


def _matmul(a, b, mode, name, out_dtype=F32, tm=512, tn=512, tk=512):
    if mode == "nn":
        (m, k), (k2, n) = a.shape, b.shape
    elif mode == "nt":
        (m, k), (n, k2) = a.shape, b.shape
    else:
        (k, m), (k2, n) = a.shape, b.shape
    assert k == k2
    tm, tn, tk = min(tm, m), min(tn, n), min(tk, k)
    assert m % tm == 0 and n % tn == 0 and k % tk == 0
    nk = k // tk
    dims = {"nn": _NN, "nt": _NT, "tn": _TN}[mode]

    assert out_dtype == F32

    def body(a_ref, b_ref, o_ref):
        kk = pl.program_id(2)
        part = _mxu(a_ref[...], b_ref[...], dims)

        @pl.when(kk == 0)
        def _():
            o_ref[...] = part

        @pl.when(kk > 0)
        def _():
            o_ref[...] += part

    a_spec = pl.BlockSpec((tk, tm), lambda i, j, kk: (kk, i)) if mode == "tn" else pl.BlockSpec((tm, tk), lambda i, j, kk: (i, kk))
    b_spec = pl.BlockSpec((tn, tk), lambda i, j, kk: (j, kk)) if mode == "nt" else pl.BlockSpec((tk, tn), lambda i, j, kk: (kk, j))
    return pl.pallas_call(
        body, name=name, grid=(m // tm, n // tn, nk), in_specs=[a_spec, b_spec],
        out_specs=pl.BlockSpec((tm, tn), lambda i, j, kk: (i, j)),
        out_shape=jax.ShapeDtypeStruct((m, n), out_dtype),
        compiler_params=_params(("parallel", "parallel", "arbitrary")),
    )(a, b)


def _norm_proj(x, norm_w, wp, tm=1024, tn=PACKED_WIDTH // 4):
    t = x.shape[0]
    tm = min(tm, t)

    def body(x_ref, nw_ref, w_ref, proj_ref, ht_ref, h_scr):
        @pl.when(pl.program_id(1) == 0)
        def _():
            xf = x_ref[...]
            r = lax.rsqrt(jnp.mean(xf * xf, axis=-1, keepdims=True) + NORM_EPS)
            h = xf * r * nw_ref[...]
            h_scr[...] = h.astype(h_scr.dtype)
            ht_ref[...] = h.T.astype(ht_ref.dtype)

        proj_ref[...] = jnp.dot(h_scr[...], w_ref[...], preferred_element_type=F32)

    return pl.pallas_call(
        body, name="norm_proj", grid=(t // tm, PACKED_WIDTH // tn),
        in_specs=[pl.BlockSpec((tm, D_MODEL), lambda i, j: (i, 0)),
                  pl.BlockSpec((1, D_MODEL), lambda i, j: (0, 0)),
                  pl.BlockSpec((D_MODEL, tn), lambda i, j: (0, j))],
        out_specs=[pl.BlockSpec((tm, tn), lambda i, j: (i, j)),
                   pl.BlockSpec((D_MODEL, tm), lambda i, j: (0, i))],
        out_shape=[jax.ShapeDtypeStruct((t, PACKED_WIDTH), F32), jax.ShapeDtypeStruct((D_MODEL, t), MXU_DTYPE)],
        scratch_shapes=[pltpu.VMEM((tm, D_MODEL), MXU_DTYPE)],
        compiler_params=_params(("parallel", "arbitrary")),
    )(x, norm_w, wp)


def _rope_tables(t):
    lane = jnp.arange(128)
    inv_freq = ROPE_THETA ** (-jnp.arange(0, HEAD_DIM, 2, dtype=F32) / HEAD_DIM)
    freq = jnp.concatenate([inv_freq] * 4)
    coarse = (jnp.arange(t // 128, dtype=F32) * 128.0)[:, None] * freq[None, :]
    fine = jnp.arange(128, dtype=F32)[:, None] * freq[None, :]
    ca, sa, cb, sb = jnp.cos(coarse)[:, None, :], jnp.sin(coarse)[:, None, :], jnp.cos(fine)[None], jnp.sin(fine)[None]
    cos = (ca * cb - sa * sb).reshape(t, 128)
    sin = (sa * cb + ca * sb).reshape(t, 128)
    first_half = (lane % HEAD_DIM) < HEAD_DIM // 2
    return cos, jnp.where(first_half, -sin, 0.0), jnp.where(first_half, 0.0, sin)


def _rope_cols(x, cos, sin_lo, sin_hi, sign):
    outs = []
    for c in range(x.shape[1] // 128):
        xc = x[:, c * 128:(c + 1) * 128]
        rot = pltpu.roll(xc, 96, 1) * sin_lo + pltpu.roll(xc, 32, 1) * sin_hi
        outs.append(xc * cos + sign * rot)
    return jnp.concatenate(outs, axis=1)


def _rope_block(x, cos, sin_lo, sin_hi, sign):
    return jnp.concatenate([_rope_cols(x[:, :2 * WIDTH], cos, sin_lo, sin_hi, sign), x[:, 2 * WIDTH:]], axis=1)


def _tile_scratch(tm, cols):
    return pltpu.VMEM((cols // 128, tm, 128), F32)


def _store_tile(scr, y):
    for c in range(scr.shape[0]):
        scr[c] = y[:, c * 128:(c + 1) * 128]


def _load_tile(scr):
    return jnp.concatenate([scr[c] for c in range(scr.shape[0])], axis=1)


def _to_strided_view(scr, o_ref, d, width=None, col0=0):
    n, tm, _ = scr.shape
    width = n * 128 if width is None else width
    for r in range(d):
        for c in range(n):
            at = r * width + col0 + c * 128
            o_ref[:, at:at + 128] = scr[c, pl.ds(r, tm // d, stride=d), :].astype(o_ref.dtype)


def _from_strided_view(i_ref, scr, d):
    n, tm, _ = scr.shape
    for r in range(d):
        for c in range(n):
            scr[c, pl.ds(r, tm // d, stride=d), :] = i_ref[:, (r * n + c) * 128:(r * n + c + 1) * 128].astype(F32)


def _strided_spec(tm, d, cols):
    return pl.BlockSpec((tm // d, d * cols), lambda i: (i, 0))


def _rope_fwd(proj, tables, tm=256):
    t = proj.shape[0]
    cols = 3 * WIDTH

    def body(x_ref, c_ref, sl_ref, sh_ref, o0, o1, o2, scr):
        for g, (d, o_ref) in enumerate(zip(DILATIONS, (o0, o1, o2))):
            y = _rope_block(x_ref[:, g * cols:(g + 1) * cols], c_ref[...], sl_ref[...], sh_ref[...], 1.0)
            if d == 1:
                o_ref[...] = y.astype(o_ref.dtype)
            else:
                _store_tile(scr, y)
                _to_strided_view(scr, o_ref, d)

    tab = pl.BlockSpec((tm, 128), lambda i: (i, 0))
    return pl.pallas_call(
        body, name="rope_fwd", grid=(t // tm,),
        in_specs=[pl.BlockSpec((tm, 3 * cols), lambda i: (i, 0)), tab, tab, tab],
        out_specs=[_strided_spec(tm, d, cols) for d in DILATIONS],
        out_shape=[jax.ShapeDtypeStruct((t // d, d * cols), MXU_DTYPE) for d in DILATIONS],
        scratch_shapes=[_tile_scratch(tm, cols)],
        compiler_params=_params(("parallel",)),
    )(proj, *tables)


def _rope_bwd(dproj, dqs, dkvs, tables, tm=256):
    t = dproj.shape[0]
    cols = 3 * WIDTH

    def body(dp_ref, q0, q1, q2, kv0, kv1, kv2, c_ref, sl_ref, sh_ref, o_ref, scr_q, scr_kv):
        for g, (d, q_ref, kv_ref) in enumerate(zip(DILATIONS, (q0, q1, q2), (kv0, kv1, kv2))):
            if d == 1:
                x = jnp.concatenate([q_ref[...], kv_ref[...]], axis=1)
            else:
                _from_strided_view(q_ref, scr_q, d)
                _from_strided_view(kv_ref, scr_kv, d)
                x = jnp.concatenate([_load_tile(scr_q), _load_tile(scr_kv)], axis=1)
            y = _rope_block(x, c_ref[...], sl_ref[...], sh_ref[...], -1.0)
            o_ref[:, g * cols:(g + 1) * cols] = y.astype(o_ref.dtype)

    tab = pl.BlockSpec((tm, 128), lambda i: (i, 0))
    return pl.pallas_call(
        body, name="rope_bwd", grid=(t // tm,),
        in_specs=[pl.BlockSpec(memory_space=pl.ANY)] + [_strided_spec(tm, d, WIDTH) for d in DILATIONS]
        + [_strided_spec(tm, d, 2 * WIDTH) for d in DILATIONS] + [tab, tab, tab],
        out_specs=pl.BlockSpec((tm, 3 * cols), lambda i: (i, 0)),
        out_shape=jax.ShapeDtypeStruct((t, PACKED_WIDTH), MXU_DTYPE),
        scratch_shapes=[_tile_scratch(tm, WIDTH), _tile_scratch(tm, 2 * WIDTH)],
        input_output_aliases={0: 0},
        compiler_params=_params(("parallel",)),
    )(dproj, *dqs, *dkvs, *tables)


def _att_masks():
    qi = _iota((ATT_BLOCK, ATT_BLOCK), 0)
    kj = _iota((ATT_BLOCK, ATT_BLOCK), 1)
    return kj <= qi, kj >= qi


def _att_fwd(qkv, d, name):
    rows = qkv.shape[0]
    nb = rows // ATT_BLOCK
    scale = HEAD_DIM ** -0.5

    def body(q_ref, kc_ref, kp_ref, vc_ref, vp_ref, o_ref, lse_ref):
        has_prev = pl.program_id(1) > 0
        m_cur, m_prev = _att_masks()
        m_prev = m_prev & has_prev
        hs = range(HEADS)
        sls = [slice(h * HEAD_DIM, (h + 1) * HEAD_DIM) for h in hs]
        qs = [q_ref[:, sl] for sl in sls]
        s_c = [jnp.where(m_cur, _mxu(qs[h], kc_ref[:, sls[h]], _NT) * scale, -jnp.inf) for h in hs]
        s_p = [jnp.where(m_prev, _mxu(qs[h], kp_ref[:, sls[h]], _NT) * scale, -jnp.inf) for h in hs]
        m = [jnp.max(jnp.maximum(s_c[h], s_p[h]), axis=1, keepdims=True) for h in hs]
        p_c = [jnp.exp(s_c[h] - m[h]) for h in hs]
        p_p = [jnp.exp(s_p[h] - m[h]) for h in hs]
        den = [jnp.sum(p_c[h] + p_p[h], axis=1, keepdims=True) for h in hs]
        o = [_mxu(p_c[h], vc_ref[:, sls[h]], _NN) + _mxu(p_p[h], vp_ref[:, sls[h]], _NN) for h in hs]
        lane = _iota((1, 128), 1)
        lse = jnp.zeros((ATT_BLOCK, 128), F32)
        for h in hs:
            o_ref[:, sls[h]] = o[h] / den[h]
            lse = jnp.where(lane == h, m[h] + jnp.log(den[h]), lse)
        lse_ref[...] = lse

    def cur(c):
        return pl.BlockSpec((ATT_BLOCK, WIDTH), lambda r, i: (i, 3 * r + c))

    def prev(c):
        return pl.BlockSpec((ATT_BLOCK, WIDTH), lambda r, i: (jnp.maximum(i - 1, 0), 3 * r + c))

    return pl.pallas_call(
        body, name=name, grid=(d, nb), in_specs=[cur(0), cur(1), prev(1), cur(2), prev(2)],
        out_specs=[pl.BlockSpec((ATT_BLOCK, WIDTH), lambda r, i: (i, r)), pl.BlockSpec((ATT_BLOCK, 128), lambda r, i: (i, r))],
        out_shape=[jax.ShapeDtypeStruct((rows, d * WIDTH), F32), jax.ShapeDtypeStruct((rows, d * 128), F32)],
        compiler_params=_params(("parallel", "arbitrary")),
    )(qkv, qkv, qkv, qkv, qkv)


def _att_bwd(qkv, do, lse, delta, d, name):
    rows = qkv.shape[0]
    nb = rows // ATT_BLOCK
    scale = HEAD_DIM ** -0.5

    def body(q_ref, kc_ref, kp_ref, vc_ref, vp_ref, do_ref, l_ref, dl_ref, dq_ref, dkv_ref, own):
        i = pl.program_id(1)
        m_cur, m_prev = _att_masks()
        m_prev = m_prev & (i > 0)

        hs = range(HEADS)
        sls = [slice(h * HEAD_DIM, (h + 1) * HEAD_DIM) for h in hs]
        col = [slice(h, h + 1) for h in hs]

        def probs(k_r, mask):
            s = [_mxu(q_ref[:, sls[h]], k_r[:, sls[h]], _NT) for h in hs]
            return [jnp.where(mask, jnp.exp(s[h] * scale - l_ref[:, col[h]]), 0.0) for h in hs]

        def dscores(p, v_r):
            dp = [_mxu(do_ref[:, sls[h]], v_r[:, sls[h]], _NT) for h in hs]
            return [(p[h] * (dp[h] - dl_ref[:, col[h]])).astype(MXU_DTYPE) for h in hs]

        @pl.when(i == 0)
        def _():
            own[...] = jnp.zeros_like(own)

        @pl.when(i < nb)
        def _():
            p_c = probs(kc_ref, m_cur)
            ds_c = dscores(p_c, vc_ref)
            p_p = probs(kp_ref, m_prev)
            ds_p = dscores(p_p, vp_ref)
            dq = [_mxu(ds_c[h], kc_ref[:, sls[h]], _NN) + _mxu(ds_p[h], kp_ref[:, sls[h]], _NN) for h in hs]
            dk_p = [_mxu(ds_p[h], q_ref[:, sls[h]], _TN) for h in hs]
            dv_p = [_mxu(p_p[h], do_ref[:, sls[h]], _TN) for h in hs]
            dk_c = [_mxu(ds_c[h], q_ref[:, sls[h]], _TN) for h in hs]
            dv_c = [_mxu(p_c[h], do_ref[:, sls[h]], _TN) for h in hs]
            for h in hs:
                vs = slice(WIDTH + h * HEAD_DIM, WIDTH + (h + 1) * HEAD_DIM)
                dq_ref[:, sls[h]] = dq[h] * scale
                dkv_ref[:, sls[h]] = own[:, sls[h]] + dk_p[h] * scale
                dkv_ref[:, vs] = own[:, vs] + dv_p[h]
                own[:, sls[h]] = dk_c[h] * scale
                own[:, vs] = dv_c[h]

        @pl.when(i == nb)
        def _():
            dkv_ref[...] = own[...]

    def qkv_spec(c, shift):
        return pl.BlockSpec((ATT_BLOCK, WIDTH), lambda r, i: (jnp.clip(i + shift, 0, nb - 1), 3 * r + c))

    tok = pl.BlockSpec((ATT_BLOCK, WIDTH), lambda r, i: (jnp.minimum(i, nb - 1), r))
    per_head = pl.BlockSpec((ATT_BLOCK, 128), lambda r, i: (jnp.minimum(i, nb - 1), r))
    return pl.pallas_call(
        body, name=name, grid=(d, nb + 1),
        in_specs=[qkv_spec(0, 0), qkv_spec(1, 0), qkv_spec(1, -1), qkv_spec(2, 0), qkv_spec(2, -1), tok, per_head, per_head],
        out_specs=[tok, pl.BlockSpec((ATT_BLOCK, 2 * WIDTH), lambda r, i: (jnp.maximum(i - 1, 0), r))],
        out_shape=[jax.ShapeDtypeStruct((rows, d * WIDTH), F32), jax.ShapeDtypeStruct((rows, d * 2 * WIDTH), F32)],
        scratch_shapes=[pltpu.VMEM((ATT_BLOCK, 2 * WIDTH), F32)],
        compiler_params=_params(("parallel", "arbitrary")),
    )(qkv, qkv, qkv, qkv, qkv, do, lse, delta)


def _att_merge(os_, lses, proj, tm=256):
    t = proj.shape[0]

    def body(o0, o1, o2, l0, l1, l2, z_ref, oz_ref, o_ref, t0, t1, t2, s_o1, s_o2, s_l1, s_l2, s_t):
        _from_strided_view(o1, s_o1, DILATIONS[1])
        _from_strided_view(o2, s_o2, DILATIONS[2])
        _from_strided_view(l1, s_l1, DILATIONS[1])
        _from_strided_view(l2, s_l2, DILATIONS[2])
        a, b, c = l0[...], s_l1[0], s_l2[0]
        m = jnp.maximum(jnp.maximum(a, b), c)
        wa, wb, wc = jnp.exp(a - m), jnp.exp(b - m), jnp.exp(c - m)
        den = wa + wb + wc
        total = m + jnp.log(den)
        t0[...] = total
        s_t[0] = total
        _to_strided_view(s_t, t1, DILATIONS[1])
        _to_strided_view(s_t, t2, DILATIONS[2])
        spread = jnp.where(_iota((128, WIDTH), 1) // HEAD_DIM == _iota((128, WIDTH), 0), 1.0, 0.0).astype(MXU_DTYPE)
        ra, rb, rc = (_pair_sum_lanes(w / den, spread) for w in (wa, wb, wc))
        o = ra * o0[...] + rb * _load_tile(s_o1) + rc * _load_tile(s_o2)
        z = z_ref[...]
        o_ref[...] = o
        oz_ref[...] = (o * z * _sigmoid(z)).astype(oz_ref.dtype)

    tok = pl.BlockSpec((tm, WIDTH), lambda i: (i, 0))
    views = [_strided_spec(tm, d, WIDTH) for d in DILATIONS]
    per_head = [_strided_spec(tm, d, 128) for d in DILATIONS]
    return pl.pallas_call(
        body, name="att_merge", grid=(t // tm,),
        in_specs=views + per_head + [pl.BlockSpec((tm, WIDTH), lambda i: (i, SEG_ZA // WIDTH))],
        out_specs=[tok, tok] + per_head,
        out_shape=[jax.ShapeDtypeStruct((t, WIDTH), MXU_DTYPE), jax.ShapeDtypeStruct((t, WIDTH), F32)]
        + [jax.ShapeDtypeStruct((t // d, d * 128), F32) for d in DILATIONS],
        scratch_shapes=[_tile_scratch(tm, WIDTH)] * 2 + [_tile_scratch(tm, 128)] * 3,
        compiler_params=_params(("parallel",)),
    )(*os_, *lses, proj)


def _att_merge_bwd(dproj, d_oz, o, proj, tm=256):
    t = proj.shape[0]

    def body(dp_ref, doz_ref, o_ref, z_ref, dz_ref, do0, do1, do2, dl0, dl1, dl2, s_do, s_dl):
        z, ov, g = z_ref[...], o_ref[...], doz_ref[...]
        sg = _sigmoid(z)
        do = g * z * sg
        dz_ref[...] = (g * ov * sg * (1.0 + z * (1.0 - sg))).astype(dz_ref.dtype)
        do0[...] = do.astype(do0.dtype)
        _store_tile(s_do, do)
        prod = do * ov
        delta = jnp.zeros((tm, 128), F32)
        for p in range(HEADS // 2):
            delta += _pair_sum_lanes(prod[:, p * 128:(p + 1) * 128], _pair_to_lanes(p, 0))
        dl0[...] = delta
        s_dl[0] = delta
        for d, do_v, dl_v in ((DILATIONS[1], do1, dl1), (DILATIONS[2], do2, dl2)):
            _to_strided_view(s_do, do_v, d)
            _to_strided_view(s_dl, dl_v, d)

    tok = pl.BlockSpec((tm, WIDTH), lambda i: (i, 0))
    seg = pl.BlockSpec((tm, WIDTH), lambda i: (i, SEG_ZA // WIDTH))
    views = [_strided_spec(tm, d, WIDTH) for d in DILATIONS]
    per_head = [_strided_spec(tm, d, 128) for d in DILATIONS]
    return pl.pallas_call(
        body, name="att_merge_bwd", grid=(t // tm,),
        in_specs=[pl.BlockSpec(memory_space=pl.ANY), tok, tok, seg],
        out_specs=[seg] + views + per_head,
        out_shape=[jax.ShapeDtypeStruct((t, PACKED_WIDTH), MXU_DTYPE)]
        + [jax.ShapeDtypeStruct((t // d, d * WIDTH), MXU_DTYPE) for d in DILATIONS]
        + [jax.ShapeDtypeStruct((t // d, d * 128), F32) for d in DILATIONS],
        scratch_shapes=[_tile_scratch(tm, WIDTH), _tile_scratch(tm, 128)],
        input_output_aliases={0: 0},
        compiler_params=_params(("parallel",)),
    )(dproj, d_oz, o, proj)


def _shift_down(x, halo, s):
    if s == 0:
        return x
    xs = pltpu.roll(x, s, 0)
    head = jnp.where(_iota((8, x.shape[1]), 0) < s, pltpu.roll(halo, s, 0), xs[0:8])
    return jnp.concatenate([head, xs[8:]], axis=0)


def _shift_up(x, nxt, s):
    if s == 0:
        return x
    n = x.shape[0]
    xs = pltpu.roll(x, n - s, 0)
    tail = jnp.where(_iota((8, x.shape[1]), 0) >= 8 - s, pltpu.roll(nxt, 8 - s, 0), xs[n - 8:])
    return jnp.concatenate([xs[:n - 8], tail], axis=0)


def _conv_fwd(proj, conv_w, tm=1024):
    t = proj.shape[0]
    cb = SEG_B // WIDTH

    def body(x_ref, halo_ref, w_ref, c_ref):
        halo = jnp.where(pl.program_id(0) > 0, halo_ref[...], 0.0)
        x = x_ref[...]
        w = w_ref[...]
        acc = jnp.zeros((tm, WIDTH), F32)
        for j in range(GDN_CONV):
            acc += _shift_down(x, halo, GDN_CONV - 1 - j) * w[j:j + 1, :]
        c_ref[...] = acc

    return pl.pallas_call(
        body, name="conv_fwd", grid=(t // tm, 3),
        in_specs=[pl.BlockSpec((tm, WIDTH), lambda i, c: (i, cb + c)),
                  pl.BlockSpec((8, WIDTH), lambda i, c: (jnp.maximum(i * (tm // 8) - 1, 0), cb + c)),
                  pl.BlockSpec((GDN_CONV, WIDTH), lambda i, c: (0, c))],
        out_specs=pl.BlockSpec((tm, WIDTH), lambda i, c: (i, c)),
        out_shape=jax.ShapeDtypeStruct((t, 3 * WIDTH), F32),
        compiler_params=_params(("parallel", "parallel")),
    )(proj, proj, conv_w)


def _conv_bwd(dproj, dc, proj, conv_w, tm=1024):
    t = proj.shape[0]
    cb = SEG_B // WIDTH
    nt = t // tm

    def body(dp_ref, dc_ref, dcn_ref, x_ref, halo_ref, w_ref, dx_ref, dw_ref):
        i = pl.program_id(1)
        w = w_ref[...]
        dcn = jnp.where(i < nt - 1, dcn_ref[...], 0.0)
        dcv = dc_ref[...]
        acc = jnp.zeros((tm, WIDTH), F32)
        for j in range(GDN_CONV):
            acc += _shift_up(dcv, dcn, GDN_CONV - 1 - j) * w[j:j + 1, :]
        dx_ref[...] = acc.astype(dx_ref.dtype)
        halo = jnp.where(i > 0, halo_ref[...], 0.0)
        x = x_ref[...]
        row8 = _iota((8, WIDTH), 0)
        part = jnp.zeros((8, WIDTH), F32)
        for j in range(GDN_CONV):
            s = jnp.sum(dcv * _shift_down(x, halo, GDN_CONV - 1 - j), axis=0, keepdims=True)
            part += jnp.where(row8 == j, s, 0.0)

        @pl.when(i == 0)
        def _():
            dw_ref[...] = jnp.zeros_like(dw_ref)

        dw_ref[...] += part

    return pl.pallas_call(
        body, name="conv_bwd", grid=(3, nt),
        in_specs=[pl.BlockSpec(memory_space=pl.ANY),
                  pl.BlockSpec((tm, WIDTH), lambda c, i: (i, c)),
                  pl.BlockSpec((8, WIDTH), lambda c, i: (jnp.minimum((i + 1) * (tm // 8), t // 8 - 1), c)),
                  pl.BlockSpec((tm, WIDTH), lambda c, i: (i, cb + c)),
                  pl.BlockSpec((8, WIDTH), lambda c, i: (jnp.maximum(i * (tm // 8) - 1, 0), cb + c)),
                  pl.BlockSpec((GDN_CONV, WIDTH), lambda c, i: (0, c))],
        out_specs=[pl.BlockSpec((tm, WIDTH), lambda c, i: (i, cb + c)),
                   pl.BlockSpec((8, WIDTH), lambda c, i: (0, c))],
        out_shape=[jax.ShapeDtypeStruct((t, PACKED_WIDTH), MXU_DTYPE), jax.ShapeDtypeStruct((8, 3 * WIDTH), F32)],
        input_output_aliases={0: 0},
        compiler_params=_params(("parallel", "arbitrary")),
    )(dproj, dc, dc, proj, proj, conv_w)


def _chunk_matrices(tm):
    r, c = _iota((tm, tm), 0), _iota((tm, tm), 1)
    same = (r // GDN_CHUNK) == (c // GDN_CHUNK)
    return jnp.where(same & (c <= r), 1.0, 0.0), jnp.where(same, 1.0, 0.0)


def _gdn_gates(ba, a_log, dt_bias):
    al = ba + dt_bias
    return _sigmoid(ba), -jnp.exp(a_log) * _softplus(al), _sigmoid(al)


SCAN_HEADS = (0, 2, 4, 6, 1, 3, 5, 7)


def _head_lane_eye():
    return jnp.where(_iota((HEADS, 128), 1) == _iota((HEADS, 128), 0) + HEADS, 1.0, 0.0)


def _pair_ones():
    return jnp.where(_iota((128, 128), 0) // HEAD_DIM == _iota((128, 128), 1) // HEAD_DIM, 1.0, 0.0).astype(MXU_DTYPE)


def _pair_to_lanes(p, base):
    return jnp.where(_iota((128, 128), 1) == base + 2 * p + _iota((128, 128), 0) // HEAD_DIM, 1.0, 0.0).astype(MXU_DTYPE)


def _pair_sum_lanes(x, sel):
    hi, lo = _split(x)
    return jnp.dot(hi, sel, preferred_element_type=F32) + jnp.dot(lo, sel, preferred_element_type=F32)


def _gdn_prep(conv, proj, a_log, dt_bias, tm=256):
    t = proj.shape[0]
    nc = tm // GDN_CHUNK

    def body(c_ref, ba_ref, al_ref, dt_ref, q_ref, k_ref, v_ref, b_ref, g_ref, gl_ref, grow_ref):
        beta, g, _ = _gdn_gates(ba_ref[:, 0:128], al_ref[...], dt_ref[...])
        lmat, cmat = _chunk_matrices(tm)
        gc = jnp.dot(lmat, g, precision=HIGHEST, preferred_element_type=F32)
        gl = jnp.dot(cmat, g, precision=HIGHEST, preferred_element_type=F32)
        grow = lax.dot_general(_head_lane_eye(), gc, _NT, precision=HIGHEST, preferred_element_type=F32)
        ones = _pair_ones()
        first = _iota((1, 128), 1) < HEAD_DIM

        def spread(x, base, p):
            return jnp.where(first, x[:, base + 2 * p:base + 2 * p + 1], x[:, base + 2 * p + 1:base + 2 * p + 2])

        for p in range(HEADS // 2):
            for seg, ref, scale in ((0, q_ref, HEAD_DIM ** -0.5), (1, k_ref, 1.0), (2, v_ref, None)):
                c = c_ref[:, seg * WIDTH + p * 128:seg * WIDTH + (p + 1) * 128]
                a = c * _sigmoid(c)
                if scale is not None:
                    a = a * (lax.rsqrt(_pair_sum_lanes(a * a, ones) + NORM_EPS) * scale)
                ref[p] = a
            b_ref[p] = spread(beta, 0, p)
            g_ref[p] = spread(gc, HEADS, p)
            gl_ref[p] = spread(gl, HEADS, p)
        for pos, h in enumerate(SCAN_HEADS):
            for cc in range(nc):
                grow_ref[pos, cc] = grow[h:h + 1, cc * GDN_CHUNK:(cc + 1) * GDN_CHUNK]

    hm = pl.BlockSpec((HEADS // 2, tm, 128), lambda i: (0, i, 0))
    small = pl.BlockSpec((1, 128), lambda i: (0, 0))
    hm_shape = jax.ShapeDtypeStruct((HEADS // 2, t, 128), F32)
    return pl.pallas_call(
        body, name="gdn_prep", grid=(t // tm,),
        in_specs=[pl.BlockSpec((tm, 3 * WIDTH), lambda i: (i, 0)),
                  pl.BlockSpec((tm, WIDTH), lambda i: (i, SEG_BA // WIDTH)), small, small],
        out_specs=[hm] * 6 + [pl.BlockSpec((HEADS, nc, 1, GDN_CHUNK), lambda i: (0, i, 0, 0))],
        out_shape=[hm_shape] * 6 + [jax.ShapeDtypeStruct((HEADS, t // GDN_CHUNK, 1, GDN_CHUNK), F32)],
        compiler_params=_params(("parallel",)),
    )(conv, proj, a_log, dt_bias)


def _gdn_prep_bwd(dproj, conv, proj, a_log, dt_bias, dq, dk, dv, db, dg, dgl, dgrow, tm=256):
    t = proj.shape[0]
    nc = tm // GDN_CHUNK

    def body(dp_ref, c_ref, ba_ref, al_ref, dt_ref, dq_ref, dk_ref, dv_ref, db_ref, dg_ref, dgl_ref, dgrow_ref,
             dba_ref, dc_ref, small_ref, row_scr):
        beta, g, sig_al = _gdn_gates(ba_ref[:, 0:128], al_ref[...], dt_ref[...])
        d_beta = jnp.zeros((tm, 128), F32)
        d_gc = jnp.zeros((tm, 128), F32)
        d_gl = jnp.zeros((tm, 128), F32)
        ones = _pair_ones()
        for pos, h in enumerate(SCAN_HEADS):
            for cc in range(nc):
                row_scr[h:h + 1, cc * GDN_CHUNK:(cc + 1) * GDN_CHUNK] = dgrow_ref[pos, cc]
        for p in range(HEADS // 2):
            d_beta += _pair_sum_lanes(db_ref[p], _pair_to_lanes(p, 0))
            d_gc += _pair_sum_lanes(dg_ref[p], _pair_to_lanes(p, HEADS))
            d_gl += _pair_sum_lanes(dgl_ref[p], _pair_to_lanes(p, HEADS))
            for seg, ref, scale in ((0, dq_ref, HEAD_DIM ** -0.5), (1, dk_ref, 1.0), (2, dv_ref, None)):
                cols = slice(seg * WIDTH + p * 128, seg * WIDTH + (p + 1) * 128)
                c = c_ref[:, cols]
                sg = _sigmoid(c)
                da = ref[p]
                if scale is not None:
                    a = c * sg
                    r = lax.rsqrt(_pair_sum_lanes(a * a, ones) + NORM_EPS)
                    da = da * scale
                    da = r * da - a * (r * r * r) * _pair_sum_lanes(da * a, ones)
                dc_ref[:, cols] = da * (sg * (1.0 + c * (1.0 - sg)))
        d_gc += lax.dot_general(row_scr[...], _head_lane_eye(), _TN, precision=HIGHEST, preferred_element_type=F32)
        lmat, cmat = _chunk_matrices(tm)
        d_g = (lax.dot_general(lmat, d_gc, _TN, precision=HIGHEST, preferred_element_type=F32)
               + lax.dot_general(cmat, d_gl, _TN, precision=HIGHEST, preferred_element_type=F32))
        d_al = d_g * (-jnp.exp(al_ref[...])) * sig_al
        d_bl = d_beta * beta * (1.0 - beta)
        dba_ref[...] = jnp.concatenate([d_bl + d_al, jnp.zeros((tm, WIDTH - 128), F32)], axis=1).astype(dba_ref.dtype)
        row8 = _iota((8, 128), 0)
        part = (jnp.where(row8 == 0, jnp.sum(d_g * g, axis=0, keepdims=True), 0.0)
                + jnp.where(row8 == 1, jnp.sum(d_al, axis=0, keepdims=True), 0.0))

        @pl.when(pl.program_id(0) == 0)
        def _():
            small_ref[...] = jnp.zeros_like(small_ref)

        small_ref[...] += part

    hm = pl.BlockSpec((HEADS // 2, tm, 128), lambda i: (0, i, 0))
    small = pl.BlockSpec((1, 128), lambda i: (0, 0))
    seg = pl.BlockSpec((tm, WIDTH), lambda i: (i, SEG_BA // WIDTH))
    return pl.pallas_call(
        body, name="gdn_prep_bwd", grid=(t // tm,),
        in_specs=[pl.BlockSpec(memory_space=pl.ANY), pl.BlockSpec((tm, 3 * WIDTH), lambda i: (i, 0)), seg, small, small]
        + [hm] * 6 + [pl.BlockSpec((HEADS, nc, 1, GDN_CHUNK), lambda i: (0, i, 0, 0))],
        out_specs=[seg, pl.BlockSpec((tm, 3 * WIDTH), lambda i: (i, 0)), pl.BlockSpec((8, 128), lambda i: (0, 0))],
        out_shape=[jax.ShapeDtypeStruct((t, PACKED_WIDTH), MXU_DTYPE), jax.ShapeDtypeStruct((t, 3 * WIDTH), F32),
                   jax.ShapeDtypeStruct((8, 128), F32)],
        scratch_shapes=[pltpu.VMEM((HEADS, tm), F32)],
        input_output_aliases={0: 0},
        compiler_params=_params(("arbitrary",)),
    )(dproj, conv, proj, a_log, dt_bias, dq, dk, dv, db, dg, dgl, dgrow)


_BNN = (((2,), (1,)), ((0,), (0,)))
_BNT = (((2,), (2,)), ((0,), (0,)))
_BTN = (((1,), (1,)), ((0,), (0,)))


@jax.custom_vjp
def _MM_NN(a, b):
    return _mxu(a, b, _BNN)


@jax.custom_vjp
def _MM_NT(a, b):
    return _mxu(a, b, _BNT)


@jax.custom_vjp
def _MM_TN(a, b):
    return _mxu(a, b, _BTN)


_MM_NN.defvjp(lambda a, b: (_mxu(a, b, _BNN), (a, b)), lambda r, g: (_mxu(g, r[1], _BNT), _mxu(r[0], g, _BTN)))
_MM_NT.defvjp(lambda a, b: (_mxu(a, b, _BNT), (a, b)), lambda r, g: (_mxu(g, r[1], _BNN), _mxu(g, r[0], _BTN)))
_MM_TN.defvjp(lambda a, b: (_mxu(a, b, _BTN), (a, b)), lambda r, g: (_mxu(r[1], g, _BNT), _mxu(r[0], g, _BNN)))


def _split(a):
    hi = a.astype(MXU_DTYPE)
    return hi, (a - hi.astype(F32)).astype(MXU_DTYPE)


def _dot3(a, b, dims):
    (ah, al), (bh, bl) = a, b
    (ca,), (cb,) = dims[0]
    return lax.dot_general(jnp.concatenate([ah, ah, al], axis=ca), jnp.concatenate([bh, bl, bh], axis=cb), dims,
                           preferred_element_type=F32)


def _unit_lower_inverse(a):
    c = GDN_CHUNK
    eye = jnp.where(_iota((c, c), 0) == _iota((c, c), 1), 1.0, 0.0)
    x = eye - a
    p = a
    for _ in range(5):
        ps = _split(p)
        p = _dot3(ps, ps, _BNN)
        x = x + _dot3(_split(x), _split(p), _BNN)
    return x


@jax.custom_vjp
def _SAVED_INVERSE(a, t_inv):
    return t_inv


def _saved_inverse_bwd(t_inv, g):
    ts = _split(t_inv)
    return -_dot3(ts, _split(_dot3(_split(g), ts, _BNT)), _BTN), jnp.zeros_like(t_inv)


_SAVED_INVERSE.defvjp(lambda a, t_inv: (t_inv, t_inv), _saved_inverse_bwd)


def _gdn_chunk(q, k, v, beta, g1, g2, gl, state, t_inv=None):
    c = GDN_CHUNK
    if t_inv is None:
        _mm_nn, _mm_nt, _mm_tn = (functools.partial(_mxu, dims=dd) for dd in (_BNN, _BNT, _BTN))
    else:
        _mm_nn, _mm_nt, _mm_tn = _MM_NN, _MM_NT, _MM_TN
    row, col = _iota((c, c), 0), _iota((c, c), 1)
    incl, strict = row >= col, row > col
    decay = jnp.where(incl, jnp.exp(jnp.where(incl, g1 - g2, 0.0)), 0.0)
    eg = jnp.exp(g1)
    kb = k * beta
    a = _mm_nt(kb, k) * jnp.where(strict, decay, 0.0)
    inv = _unit_lower_inverse(a) if t_inv is None else _SAVED_INVERSE(a, t_inv)
    u = _mm_nn(inv, v * beta)
    w = _mm_nn(inv, kb * eg)
    attn = _mm_nt(q, k) * decay
    v_new = u - _mm_nn(w, state)
    o = _mm_nn(q * eg, state) + _mm_nn(attn, v_new)
    new_state = state * jnp.exp(gl) + _mm_tn(k * jnp.exp(gl - g1), v_new)
    return (o, new_state, inv) if t_inv is None else (o, new_state)


def _unpair(x):
    return jnp.concatenate([x[..., :HEAD_DIM], x[..., HEAD_DIM:]], axis=0)


def _gdn_fwd(q, k, v, beta, g, gl, grow, cpb=4):
    t = q.shape[1]
    rows = cpb * GDN_CHUNK
    lo, hi = slice(0, HEAD_DIM), slice(HEAD_DIM, 2 * HEAD_DIM)

    def body(q_ref, k_ref, v_ref, b_ref, g_ref, gl_ref, grow_ref, o_ref, st_ref, inv_ref, state):
        @pl.when(pl.program_id(0) == 0)
        def _():
            state[...] = jnp.zeros_like(state)

        s = state[...]
        for cc in range(cpb):
            sl = slice(cc * GDN_CHUNK, (cc + 1) * GDN_CHUNK)
            st_ref[:, cc, :, lo], st_ref[:, cc, :, hi] = s[:HEADS // 2], s[HEADS // 2:]
            g2 = jnp.broadcast_to(grow_ref[:, cc], (HEADS, GDN_CHUNK, GDN_CHUNK))
            o, s, inv = _gdn_chunk(*[_unpair(r[:, sl, :]) for r in (q_ref, k_ref, v_ref, b_ref, g_ref)], g2,
                                   _unpair(gl_ref[:, sl, :]), s)
            o_ref[:, sl, lo], o_ref[:, sl, hi] = o[:HEADS // 2], o[HEADS // 2:]
            inv_ref[:, cc, :, lo], inv_ref[:, cc, :, hi] = inv[:HEADS // 2], inv[HEADS // 2:]
        state[...] = s

    hm = pl.BlockSpec((HEADS // 2, rows, 128), lambda i: (0, i, 0))
    per_chunk = pl.BlockSpec((HEADS // 2, cpb, GDN_CHUNK, 128), lambda i: (0, i, 0, 0))
    chunk_shape = jax.ShapeDtypeStruct((HEADS // 2, t // GDN_CHUNK, GDN_CHUNK, 128), F32)
    return pl.pallas_call(
        body, name="gdn_fwd", grid=(t // rows,),
        in_specs=[hm] * 6 + [pl.BlockSpec((HEADS, cpb, 1, GDN_CHUNK), lambda i: (0, i, 0, 0))],
        out_specs=[hm, per_chunk, per_chunk],
        out_shape=[jax.ShapeDtypeStruct((HEADS // 2, t, 128), F32), chunk_shape, chunk_shape],
        scratch_shapes=[pltpu.VMEM((HEADS, GDN_CHUNK, HEAD_DIM), F32)],
        compiler_params=_params(("arbitrary",)),
    )(q, k, v, beta, g, gl, grow)


def _gdn_bwd(q, k, v, beta, g, gl, grow, states, invs, do, cpb=2):
    t = q.shape[1]
    rows = cpb * GDN_CHUNK
    nsteps = t // rows
    lo, hi = slice(0, HEAD_DIM), slice(HEAD_DIM, 2 * HEAD_DIM)

    def body(q_ref, k_ref, v_ref, b_ref, g_ref, gl_ref, grow_ref, st_ref, inv_ref, do_ref,
             dq_ref, dk_ref, dv_ref, db_ref, dg_ref, dgl_ref, dgrow_ref, dstate):
        @pl.when(pl.program_id(0) == 0)
        def _():
            dstate[...] = jnp.zeros_like(dstate)

        ds = dstate[...]
        for cc in reversed(range(cpb)):
            sl = slice(cc * GDN_CHUNK, (cc + 1) * GDN_CHUNK)
            g2 = jnp.broadcast_to(grow_ref[:, cc], (HEADS, GDN_CHUNK, GDN_CHUNK))
            _, vjp = jax.vjp(_gdn_chunk, *[_unpair(r[:, sl, :]) for r in (q_ref, k_ref, v_ref, b_ref, g_ref)], g2,
                             _unpair(gl_ref[:, sl, :]), _unpair(st_ref[:, cc]), _unpair(inv_ref[:, cc]))
            gq, gk, gv, gb, gg1, gg2, ggl, ds, _ = vjp((_unpair(do_ref[:, sl, :]), ds))
            for ref, val in ((dq_ref, gq), (dk_ref, gk), (dv_ref, gv), (db_ref, gb), (dg_ref, gg1), (dgl_ref, ggl)):
                ref[:, sl, lo], ref[:, sl, hi] = val[:HEADS // 2], val[HEADS // 2:]
            dgrow_ref[:, cc] = jnp.sum(gg2, axis=1, keepdims=True)
        dstate[...] = ds

    hm = pl.BlockSpec((HEADS // 2, rows, 128), lambda i: (0, nsteps - 1 - i, 0))
    rowspec = pl.BlockSpec((HEADS, cpb, 1, GDN_CHUNK), lambda i: (0, nsteps - 1 - i, 0, 0))
    per_chunk = pl.BlockSpec((HEADS // 2, cpb, GDN_CHUNK, 128), lambda i: (0, nsteps - 1 - i, 0, 0))
    hm_shape = jax.ShapeDtypeStruct((HEADS // 2, t, 128), F32)
    return pl.pallas_call(
        body, name="gdn_bwd", grid=(nsteps,),
        in_specs=[hm] * 6 + [rowspec, per_chunk, per_chunk, hm],
        out_specs=[hm] * 6 + [rowspec],
        out_shape=[hm_shape] * 6 + [jax.ShapeDtypeStruct((HEADS, t // GDN_CHUNK, 1, GDN_CHUNK), F32)],
        scratch_shapes=[pltpu.VMEM((HEADS, GDN_CHUNK, HEAD_DIM), F32)],
        compiler_params=_params(("arbitrary",)),
    )(q, k, v, beta, g, gl, grow, states, invs, do)


def _gdn_out(o_hm, gdn_norm_w, proj, tm=512):
    t = proj.shape[0]

    def body(o_ref, w_ref, z_ref, oz_ref):
        w = w_ref[...]
        ones = _pair_ones()
        for p in range(HEADS // 2):
            cols = slice(p * 128, (p + 1) * 128)
            o = o_ref[p]
            z = z_ref[:, cols]
            r = lax.rsqrt(_pair_sum_lanes(o * o, ones) * (1.0 / HEAD_DIM) + NORM_EPS)
            oz_ref[:, cols] = (o * r * w * (z * _sigmoid(z))).astype(oz_ref.dtype)

    tok = pl.BlockSpec((tm, WIDTH), lambda i: (i, 0))
    return pl.pallas_call(
        body, name="gdn_out", grid=(t // tm,),
        in_specs=[pl.BlockSpec((HEADS // 2, tm, 128), lambda i: (0, i, 0)), pl.BlockSpec((1, 128), lambda i: (0, 0)),
                  pl.BlockSpec((tm, WIDTH), lambda i: (i, SEG_ZB // WIDTH))],
        out_specs=tok, out_shape=jax.ShapeDtypeStruct((t, WIDTH), MXU_DTYPE),
        compiler_params=_params(("parallel",)),
    )(o_hm, jnp.tile(gdn_norm_w, (1, 2)), proj)


def _gdn_out_bwd(dproj, d_oz, o_hm, gdn_norm_w, proj, tm=512):
    t = proj.shape[0]

    def body(dp_ref, doz_ref, o_ref, w_ref, z_ref, dz_ref, do_ref, dw_ref):
        w = w_ref[...]
        ones = _pair_ones()
        dw = jnp.zeros((1, 128), F32)
        for p in range(HEADS // 2):
            cols = slice(p * 128, (p + 1) * 128)
            o = o_ref[p]
            z, g = z_ref[:, cols], doz_ref[:, cols]
            sg = _sigmoid(z)
            r = lax.rsqrt(_pair_sum_lanes(o * o, ones) * (1.0 / HEAD_DIM) + NORM_EPS)
            dz_ref[:, cols] = (g * (o * r * w) * (sg * (1.0 + z * (1.0 - sg)))).astype(dz_ref.dtype)
            dn = g * (z * sg)
            dw += jnp.sum(dn * o * r, axis=0, keepdims=True)
            dnw = dn * w
            do_ref[p] = r * dnw - o * (r * r * r) * (_pair_sum_lanes(dnw * o, ones) * (1.0 / HEAD_DIM))

        @pl.when(pl.program_id(0) == 0)
        def _():
            dw_ref[...] = jnp.zeros_like(dw_ref)

        dw_ref[...] += jnp.where(_iota((8, 128), 0) == 0, dw, 0.0)

    tok = pl.BlockSpec((tm, WIDTH), lambda i: (i, 0))
    seg = pl.BlockSpec((tm, WIDTH), lambda i: (i, SEG_ZB // WIDTH))
    hm = pl.BlockSpec((HEADS // 2, tm, 128), lambda i: (0, i, 0))
    dz, do, dw = pl.pallas_call(
        body, name="gdn_out_bwd", grid=(t // tm,),
        in_specs=[pl.BlockSpec(memory_space=pl.ANY), tok, hm, pl.BlockSpec((1, 128), lambda i: (0, 0)), seg],
        out_specs=[seg, hm, pl.BlockSpec((8, 128), lambda i: (0, 0))],
        out_shape=[jax.ShapeDtypeStruct((t, PACKED_WIDTH), MXU_DTYPE), jax.ShapeDtypeStruct((HEADS // 2, t, 128), F32),
                   jax.ShapeDtypeStruct((8, 128), F32)],
        input_output_aliases={0: 0},
        compiler_params=_params(("arbitrary",)),
    )(dproj, d_oz, o_hm, jnp.tile(gdn_norm_w, (1, 2)), proj)
    return dz, do, dw[:, :HEAD_DIM] + dw[:, HEAD_DIM:]


def _merge(y_a, y_b, proj, tm=1024):
    t = proj.shape[0]

    def body(ya_ref, yb_ref, ga_ref, gb_ref, m_ref):
        m_ref[...] = (_sigmoid(ga_ref[...]) * ya_ref[...] + _sigmoid(gb_ref[...]) * yb_ref[...]).astype(m_ref.dtype)

    half = pl.BlockSpec((tm, WIDTH), lambda i, c: (i, c))
    return pl.pallas_call(
        body, name="merge", grid=(t // tm, 2),
        in_specs=[half, half, pl.BlockSpec((tm, WIDTH), lambda i, c: (i, SEG_GA // WIDTH + c)),
                  pl.BlockSpec((tm, WIDTH), lambda i, c: (i, SEG_GB // WIDTH + c))],
        out_specs=half, out_shape=jax.ShapeDtypeStruct((t, D_MODEL), MXU_DTYPE),
        compiler_params=_params(("parallel", "parallel")),
    )(y_a, y_b, proj, proj)


def _merge_bwd(dproj, d_m, y, proj, seg, name, tm=1024):
    t = proj.shape[0]

    def body(*refs):
        dm_ref, y_ref, g_ref, dg_ref, dy_ref = refs[-5:]
        dm = dm_ref[...]
        s = _sigmoid(g_ref[...])
        dy_ref[...] = (dm * s).astype(dy_ref.dtype)
        dg_ref[...] = (dm * y_ref[...] * s * (1.0 - s)).astype(dg_ref.dtype)

    half = pl.BlockSpec((tm, WIDTH), lambda i, c: (i, c))
    gate = pl.BlockSpec((tm, WIDTH), lambda i, c: (i, seg // WIDTH + c))
    specs, args, aliases = [half, half, gate], [d_m, y, proj], {}
    if dproj is not None:
        specs, args, aliases = [pl.BlockSpec(memory_space=pl.ANY)] + specs, [dproj] + args, {0: 0}
    return pl.pallas_call(
        body, name=name, grid=(t // tm, 2), in_specs=specs, out_specs=[gate, half],
        out_shape=[jax.ShapeDtypeStruct((t, PACKED_WIDTH), MXU_DTYPE), jax.ShapeDtypeStruct((t, D_MODEL), MXU_DTYPE)],
        input_output_aliases=aliases,
        compiler_params=_params(("parallel", "parallel")),
    )(*args)


def _out_tail(merged, w_out, x, final_w, target, tm=1024):
    t = x.shape[0]
    tm = min(tm, t)

    def body(m_ref, wo_ref, x_ref, w_ref, t_ref, dxm_ref, dx_ref, loss_ref, dw_ref):
        x2 = x_ref[...] + jnp.dot(m_ref[...], wo_ref[...], preferred_element_type=F32)
        w = w_ref[...]
        r = lax.rsqrt(jnp.mean(x2 * x2, axis=-1, keepdims=True) + NORM_EPS)
        xn = x2 * r
        err = xn * w - t_ref[...]
        dy = err * (1.0 / D_MODEL)
        dyw = dy * w
        dx2 = r * dyw - x2 * (r * r * r) * jnp.mean(dyw * x2, axis=-1, keepdims=True)
        dx_ref[...] = dx2
        dxm_ref[...] = dx2.astype(dxm_ref.dtype)
        loss = 0.5 * jnp.sum(jnp.sum(err * err, axis=-1, keepdims=True) * (1.0 / D_MODEL), axis=0, keepdims=True)
        onehot = jnp.where((_iota((8, 128), 0) == 0) & (_iota((8, 128), 1) == 0), 1.0, 0.0)

        @pl.when(pl.program_id(0) == 0)
        def _():
            loss_ref[...] = jnp.zeros_like(loss_ref)
            dw_ref[...] = jnp.zeros_like(dw_ref)

        loss_ref[...] += loss * onehot
        dw_ref[...] += jnp.where(_iota((8, D_MODEL), 0) == 0, jnp.sum(dy * xn, axis=0, keepdims=True), 0.0)

    tok = pl.BlockSpec((tm, D_MODEL), lambda i: (i, 0))
    full = pl.BlockSpec((D_MODEL, D_MODEL), lambda i: (0, 0))
    return pl.pallas_call(
        body, name="out_tail", grid=(t // tm,),
        in_specs=[tok, full, tok, pl.BlockSpec((1, D_MODEL), lambda i: (0, 0)), tok],
        out_specs=[tok, tok, pl.BlockSpec((8, 128), lambda i: (0, 0)), pl.BlockSpec((8, D_MODEL), lambda i: (0, 0))],
        out_shape=[jax.ShapeDtypeStruct((t, D_MODEL), MXU_DTYPE), jax.ShapeDtypeStruct((t, D_MODEL), F32),
                   jax.ShapeDtypeStruct((8, 128), F32), jax.ShapeDtypeStruct((8, D_MODEL), F32)],
        compiler_params=_params(("arbitrary",)),
    )(merged, w_out, x, final_w, target)


def _dh_norm_bwd(dproj, wp, x, norm_w, dx2, after, tm=1024, tk=PACKED_WIDTH // 4):
    t = x.shape[0]
    tm = min(tm, t)
    nk = PACKED_WIDTH // tk

    def body(dp_ref, wp_ref, x_ref, w_ref, dx2_ref, after_ref, dx_ref, dw_ref, acc):
        kk = pl.program_id(1)
        part = _mxu(dp_ref[...], wp_ref[...], _NT)

        @pl.when(kk == 0)
        def _():
            acc[...] = part

        @pl.when(kk > 0)
        def _():
            acc[...] += part

        @pl.when((kk == 0) & (pl.program_id(0) == 0))
        def _():
            dw_ref[...] = jnp.zeros_like(dw_ref)

        @pl.when(kk == nk - 1)
        def _():
            xf, w, dh_ = x_ref[...], w_ref[...], acc[...]
            r = lax.rsqrt(jnp.mean(xf * xf, axis=-1, keepdims=True) + NORM_EPS)
            dhw = dh_ * w
            dx_ref[...] = dx2_ref[...] + r * dhw - xf * (r * r * r) * jnp.mean(dhw * xf, axis=-1, keepdims=True)
            dw_ref[...] += jnp.where(_iota((8, D_MODEL), 0) == 0, jnp.sum(dh_ * xf * r, axis=0, keepdims=True), 0.0)

    tok = pl.BlockSpec((tm, D_MODEL), lambda i, kk: (i, 0))
    return pl.pallas_call(
        body, name="dh_norm_bwd", grid=(t // tm, nk),
        in_specs=[pl.BlockSpec((tm, tk), lambda i, kk: (i, kk)), pl.BlockSpec((D_MODEL, tk), lambda i, kk: (0, kk)),
                  tok, pl.BlockSpec((1, D_MODEL), lambda i, kk: (0, 0)), tok, pl.BlockSpec(memory_space=pl.ANY)],
        out_specs=[tok, pl.BlockSpec((8, D_MODEL), lambda i, kk: (0, 0))],
        out_shape=[jax.ShapeDtypeStruct((t, D_MODEL), F32), jax.ShapeDtypeStruct((8, D_MODEL), F32)],
        scratch_shapes=[pltpu.VMEM((tm, D_MODEL), F32)],
        compiler_params=_params(("arbitrary", "arbitrary")),
    )(dproj, wp, x, norm_w, dx2, after)


def _local_step(x, target, norm_w, wp, conv_w, a_log, dt_bias, gdn_norm_w, w_up_a, w_up_b, w_out, final_w, start_reduce):
    t = x.shape[0]
    tables = _rope_tables(t)
    a_log = jnp.pad(a_log, ((0, 0), (HEADS, 128 - 2 * HEADS)))
    dt_bias = jnp.pad(dt_bias, ((0, 0), (HEADS, 128 - 2 * HEADS)))

    proj, h_t = _norm_proj(x, norm_w, wp)
    qkvs = _rope_fwd(proj, tables)
    outs, lses = zip(*[_att_fwd(qkvs[gi], d, f"att_fwd{gi}") for gi, d in enumerate(DILATIONS)])
    oz_a, o_a, *lse_views = _att_merge(outs, lses, proj)
    conv = _conv_fwd(proj, conv_w)
    gq, gk, gv, gb, gg, ggl, grow = _gdn_prep(conv, proj, a_log, dt_bias)
    o_b, states, invs = _gdn_fwd(gq, gk, gv, gb, gg, ggl, grow)
    oz_b = _gdn_out(o_b, gdn_norm_w, proj)
    big = dict(tm=1024, tn=1024, tk=1024)
    y_a = _matmul(oz_a, w_up_a, "nn", "up_a", **big)
    y_b = _matmul(oz_b, w_up_b, "nn", "up_b", **big)
    merged = _merge(y_a, y_b, proj)
    dx2_m, dx2, loss_blk, d_final = _out_tail(merged, w_out, x, final_w, target)

    d_wout = _matmul(merged, dx2_m, "tn", "d_w_out", **big)
    d_m = _matmul(dx2_m, w_out, "nt", "d_merged", **big)
    dproj, dy_a = _merge_bwd(None, d_m, y_a, proj, SEG_GA, "merge_bwd_a")
    dproj, dy_b = _merge_bwd(dproj, d_m, y_b, proj, SEG_GB, "merge_bwd_b")
    d_wua = _matmul(oz_a, dy_a, "tn", "d_w_up_a", **big)
    d_wub = _matmul(oz_b, dy_b, "tn", "d_w_up_b", **big)
    d_oz_a = _matmul(dy_a, w_up_a, "nt", "d_oz_a", **big)
    d_oz_b = _matmul(dy_b, w_up_b, "nt", "d_oz_b", **big)
    dproj, *views = _att_merge_bwd(dproj, d_oz_a, o_a, proj)
    do_views, delta_views = views[:3], views[3:]
    dqs, dkvs = zip(*[_att_bwd(qkvs[gi], do_views[gi], lse_views[gi], delta_views[gi], d, f"att_bwd{gi}")
                      for gi, d in enumerate(DILATIONS)])
    dproj = _rope_bwd(dproj, dqs, dkvs, tables)
    dproj, do_b, d_gnw = _gdn_out_bwd(dproj, d_oz_b, o_b, gdn_norm_w, proj)
    dgq, dgk, dgv, dgb, dgg, dggl, dgrow = _gdn_bwd(gq, gk, gv, gb, gg, ggl, grow, states, invs, do_b)
    dproj, dconv, d_small = _gdn_prep_bwd(dproj, conv, proj, a_log, dt_bias, dgq, dgk, dgv, dgb, dgg, dggl, dgrow)
    dproj, d_convw = _conv_bwd(dproj, dconv, proj, conv_w)
    d_wp = _matmul(h_t, dproj, "nn", "d_w_in", tm=1024, tn=PACKED_WIDTH // 4, tk=1024)
    in_flight, token = start_reduce(d_wp, d_wua, d_wub, d_wout, d_convw[0:GDN_CONV])
    grad_x, d_norm = _dh_norm_bwd(dproj, wp, x, norm_w, dx2, token)
    return dict(loss=loss_blk, grad_x=grad_x, norm_w=d_norm[0:1], in_flight=in_flight,
                a_log=d_small[0:1, HEADS:2 * HEADS], dt_bias=d_small[1:2, HEADS:2 * HEADS], gdn_norm_w=d_gnw[0:1],
                final_norm_w=d_final[0:1])


SHARDS = 4
W_IN_SHARD = IN_WIDTH // SHARDS
ROWS_UP = WIDTH * (D_MODEL // SHARDS) // 128
ROWS_OUT = (D_MODEL // SHARDS) * D_MODEL // 128
CONV_SHARD = 3 * WIDTH // SHARDS
ROWS_CONV = 16
SLAB_ROWS = 2 * ROWS_UP + ROWS_OUT + 2 * ROWS_CONV
HALF_ROWS = SLAB_ROWS // 2
BIG_HALF = (D_MODEL // 2, W_IN_SHARD)
SMALL_HALF = (HALF_ROWS, 128)
MESH = pl.DeviceIdType.MESH
ANY = pl.BlockSpec(memory_space=pl.ANY)


def _pad_rows(a, rows):
    return jnp.pad(a, ((0, rows - a.shape[0]), (0, 0)))


def _pack_slab(w_up_a, w_up_b, w_out, conv, conv_lo):
    parts = [w_up_a.reshape(ROWS_UP, 128), w_up_b.reshape(ROWS_UP, 128), w_out.reshape(ROWS_OUT, 128),
             _pad_rows(conv.reshape(-1, 128), ROWS_CONV), _pad_rows(conv_lo.reshape(-1, 128), ROWS_CONV)]
    return jnp.concatenate(parts, axis=0).reshape(2, *SMALL_HALF)


def _unpack_slab(slab):
    slab = slab.reshape(SLAB_ROWS, 128)
    r0 = 0
    out = []
    for rows, shape in ((ROWS_UP, (WIDTH, D_MODEL // SHARDS)), (ROWS_UP, (WIDTH, D_MODEL // SHARDS)),
                        (ROWS_OUT, (D_MODEL // SHARDS, D_MODEL)), (ROWS_CONV, None), (ROWS_CONV, None)):
        part = slab[r0:r0 + rows]
        out.append(part[:GDN_CONV * CONV_SHARD // 128].reshape(GDN_CONV, CONV_SHARD) if shape is None else part.reshape(shape))
        r0 += rows
    return out


def _mesh_position():
    x, y, c = lax.axis_index("x"), lax.axis_index("y"), lax.axis_index("c")
    return x, y, c, [(1 - x, y), (x, 1 - y), (1 - x, 1 - y)]


def _gather_weights(shards):
    n = len(shards)

    def body(*refs):
        in_refs, out_refs, (send_sems, recv_sems) = refs[:n], refs[n:2 * n], refs[2 * n:]
        x, y, c, chips = _mesh_position()

        def half(a, chip, which):
            return out_refs[a].at[2 * chip[0] + chip[1], which]

        def copy(k, src, dst, to):
            return pltpu.make_async_remote_copy(src_ref=src, dst_ref=dst, send_sem=send_sems.at[k], recv_sem=recv_sems.at[k],
                                                device_id=to, device_id_type=MESH)

        pairs = [(a, j, chip) for a in range(n) for j, chip in enumerate(chips)]
        first = [copy(6 * a + j, in_refs[a].at[c], half(a, (x, y), c), (*chip, c)) for a, j, chip in pairs]
        for cp in first:
            cp.start()
        passed = [copy(6 * a + 3 + j, half(a, chip, c), half(a, chip, c), (x, y, 1 - c)) for a, j, chip in pairs]
        for i, (a, j, chip) in enumerate(pairs):
            copy(6 * a + j, half(a, chip, c), half(a, chip, c), (x, y, c)).wait_recv()
            passed[i].start()
        for a, j, chip in pairs:
            copy(6 * a + 3 + j, half(a, chip, 1 - c), half(a, chip, 1 - c), (x, y, c)).wait_recv()
        for cp in first + passed:
            cp.wait_send()

    return pl.pallas_call(
        body, name="gather_weights", in_specs=[ANY] * n, out_specs=[ANY] * n,
        out_shape=[jax.ShapeDtypeStruct((SHARDS, *s.shape), s.dtype) for s in shards],
        scratch_shapes=[pltpu.SemaphoreType.DMA((6 * n,)), pltpu.SemaphoreType.DMA((6 * n,))],
    )(*shards)


def _exchange_halves(grads):
    n = len(grads)

    def body(*refs):
        g_refs, out_refs, (send_sems, recv_sems) = refs[:n], refs[n:2 * n], refs[2 * n:]
        x, y, c, _ = _mesh_position()
        copies = [pltpu.make_async_remote_copy(src_ref=g_refs[a].at[s, 1 - c], dst_ref=out_refs[a].at[s],
                                               send_sem=send_sems.at[SHARDS * a + s], recv_sem=recv_sems.at[SHARDS * a + s],
                                               device_id=(x, y, 1 - c), device_id_type=MESH)
                  for a in range(n) for s in range(SHARDS)]
        for cp in copies:
            cp.start()
        for cp in copies:
            cp.wait()

    return pl.pallas_call(
        body, name="exchange_halves", in_specs=[ANY] * n, out_specs=[ANY] * n,
        out_shape=[jax.ShapeDtypeStruct((SHARDS, *g.shape[2:]), F32) for g in grads],
        scratch_shapes=[pltpu.SemaphoreType.DMA((SHARDS * n,)), pltpu.SemaphoreType.DMA((SHARDS * n,))],
    )(*grads)


def _pair_sum(grads, recv, blk, name):
    _, _, rows, cols = grads.shape

    def body(c_ref, g_ref, r_ref, o_ref):
        o_ref[...] = (g_ref[0] + r_ref[...]).astype(o_ref.dtype)

    spec = pl.BlockSpec((1, blk, cols), lambda s, i, c_ref: (s, i, 0))
    return pl.pallas_call(
        body, name=name,
        grid_spec=pltpu.PrefetchScalarGridSpec(
            num_scalar_prefetch=1, grid=(SHARDS, rows // blk),
            in_specs=[pl.BlockSpec((1, 1, blk, cols), lambda s, i, c_ref: (s, c_ref[0], i, 0)), spec],
            out_specs=spec),
        out_shape=jax.ShapeDtypeStruct((SHARDS, rows, cols), MXU_DTYPE),
        compiler_params=_params(("parallel", "parallel")),
    )(lax.axis_index("c").astype(jnp.int32).reshape(1), grads, recv)


_HBM = pl.BlockSpec(memory_space=pltpu.HBM)
_SEM = pl.BlockSpec(memory_space=pltpu.SEMAPHORE)
_DATAFLOW = pltpu.SideEffectType.DATAFLOW_SIDE_EFFECTING


def _scatter_copies(p_refs, l_refs, send_sems, recv_sems):
    x, y, c, chips = _mesh_position()
    return [pltpu.make_async_remote_copy(src_ref=p_refs[a].at[2 * chip[0] + chip[1]], dst_ref=l_refs[a].at[j],
                                         send_sem=send_sems.at[3 * a + j], recv_sem=recv_sems.at[3 * a + j],
                                         device_id=(*chip, c), device_id_type=MESH)
            for a in range(len(p_refs)) for j, chip in enumerate(chips)]


def _scatter_start(pairs):
    n = len(pairs)
    lands = [lax.empty((3, *p.shape[1:]), p.dtype) for p in pairs]

    def body(*refs):
        p_refs, l_refs, send_sems, recv_sems, token = refs[:n], refs[n:2 * n], refs[2 * n], refs[2 * n + 1], refs[-1]
        for cp in _scatter_copies(p_refs, l_refs, send_sems, recv_sems):
            cp.start()
        token[...] = jnp.zeros_like(token)

    operands = [pltpu.with_memory_space_constraint(a, pltpu.HBM) for a in (*pairs, *lands)]
    return pl.pallas_call(
        body, name="scatter_start", in_specs=[_HBM] * (2 * n),
        out_shape=(pltpu.SemaphoreType.DMA((3 * n,)), pltpu.SemaphoreType.DMA((3 * n,)),
                   *[pltpu.HBM(a.shape, a.dtype) for a in operands], jax.ShapeDtypeStruct((8, 128), F32)),
        out_specs=(_SEM, _SEM, *[_HBM] * (2 * n), pl.BlockSpec(memory_space=pltpu.VMEM)),
        input_output_aliases={i: 2 + i for i in range(2 * n)},
        compiler_params=pltpu.CompilerParams(has_side_effects=_DATAFLOW),
    )(*operands)


def _scatter_wait(send_sems, recv_sems, passed, after):
    n = len(passed) // 2

    def body(*refs):
        p_refs, l_refs, send_s, recv_s = refs[:n], refs[n:2 * n], refs[2 * n], refs[2 * n + 1]
        for cp in _scatter_copies(p_refs, l_refs, send_s, recv_s):
            cp.wait_send()
            cp.wait_recv()

    return pl.pallas_call(
        body, name="scatter_wait", in_specs=[_HBM] * (2 * n) + [_SEM, _SEM, ANY],
        out_shape=[pltpu.HBM(a.shape, a.dtype) for a in passed], out_specs=[_HBM] * (2 * n),
        input_output_aliases={i: i for i in range(2 * n)},
        compiler_params=pltpu.CompilerParams(has_side_effects=_DATAFLOW),
    )(*passed, send_sems, recv_sems, after)


def _chip_sum(pairs, recv, blk, name):
    _, rows, cols = pairs.shape

    def body(pos_ref, p_ref, r_ref, o_ref):
        o_ref[0] = ((p_ref[0].astype(F32) + r_ref[0].astype(F32)) + r_ref[1].astype(F32)) + r_ref[2].astype(F32)

    pos = jnp.stack([2 * lax.axis_index("x") + lax.axis_index("y"), lax.axis_index("c")]).astype(jnp.int32)
    return pl.pallas_call(
        body, name=name,
        grid_spec=pltpu.PrefetchScalarGridSpec(
            num_scalar_prefetch=1, grid=(rows // blk,),
            in_specs=[pl.BlockSpec((1, blk, cols), lambda i, pos_ref: (pos_ref[0], i, 0)),
                      pl.BlockSpec((3, blk, cols), lambda i, pos_ref: (0, i, 0))],
            out_specs=pl.BlockSpec((1, blk, cols), lambda i, pos_ref: (pos_ref[1], i, 0))),
        out_shape=jax.ShapeDtypeStruct((2, rows, cols), F32),
        compiler_params=_params(("parallel",)),
    )(pos, pairs, recv)


def _share_total(totals):
    n = len(totals)

    def body(*refs):
        t_refs, out_refs, (send_sems, recv_sems) = refs[:n], refs[n:2 * n], refs[2 * n:]
        x, y, c, _ = _mesh_position()
        copies = [pltpu.make_async_remote_copy(src_ref=t_refs[a].at[c], dst_ref=out_refs[a].at[c], send_sem=send_sems.at[a],
                                               recv_sem=recv_sems.at[a], device_id=(x, y, 1 - c), device_id_type=MESH)
                  for a in range(n)]
        for cp in copies:
            cp.start()
        for a in range(n):
            other = out_refs[a].at[1 - c]
            pltpu.make_async_remote_copy(src_ref=other, dst_ref=other, send_sem=send_sems.at[a], recv_sem=recv_sems.at[a],
                                         device_id=(x, y, c), device_id_type=MESH).wait_recv()
        for cp in copies:
            cp.wait_send()

    return pl.pallas_call(
        body, name="share_total", in_specs=[ANY] * n, out_specs=[ANY] * n,
        out_shape=[jax.ShapeDtypeStruct(t.shape, F32) for t in totals],
        scratch_shapes=[pltpu.SemaphoreType.DMA((n,)), pltpu.SemaphoreType.DMA((n,))],
        input_output_aliases={a: a for a in range(n)},
    )(*totals)


def _allreduce_small(block):
    def body(b_ref, out_ref, gath, send_sems, recv_sems):
        x, y, c, _ = _mesh_position()
        me = 4 * x + 2 * y + c
        gath[me] = b_ref[...]
        copies = []
        for k in range(1, 8):
            peer = (x ^ (k >> 2), y ^ ((k >> 1) & 1), c ^ (k & 1))
            copies.append(pltpu.make_async_remote_copy(src_ref=b_ref, dst_ref=gath.at[me], send_sem=send_sems.at[k - 1],
                                                       recv_sem=recv_sems.at[k - 1], device_id=peer, device_id_type=MESH))
        for cp in copies:
            cp.start()
        for k in range(1, 8):
            src = 4 * (x ^ (k >> 2)) + 2 * (y ^ ((k >> 1) & 1)) + (c ^ (k & 1))
            pltpu.make_async_remote_copy(src_ref=b_ref, dst_ref=gath.at[src], send_sem=send_sems.at[k - 1],
                                         recv_sem=recv_sems.at[k - 1], device_id=(x, y, c), device_id_type=MESH).wait_recv()
        for cp in copies:
            cp.wait_send()
        acc = gath[0]
        for d in range(1, 8):
            acc = acc + gath[d]
        out_ref[...] = acc

    vm = pl.BlockSpec(memory_space=pltpu.VMEM)
    return pl.pallas_call(
        body, name="allreduce_small", in_specs=[vm], out_specs=vm,
        out_shape=jax.ShapeDtypeStruct((8, D_MODEL), F32),
        scratch_shapes=[pltpu.VMEM((8, 8, D_MODEL), F32), pltpu.SemaphoreType.DMA((7,)), pltpu.SemaphoreType.DMA((7,))],
    )(block)


def _adamw(w, g, m, v, name):
    rows, cols = w.shape
    tr = 128 if rows % 128 == 0 else rows

    def body(w_ref, g_ref, m_ref, v_ref, d_ref, nm_ref, nv_ref):
        gv = g_ref[...]
        nm = ADAM_B1 * m_ref[...] + (1.0 - ADAM_B1) * gv
        nv = ADAM_B2 * v_ref[...] + (1.0 - ADAM_B2) * (gv * gv)
        m_hat = nm / (1.0 - ADAM_B1 ** ADAM_STEP)
        v_hat = nv / (1.0 - ADAM_B2 ** ADAM_STEP)
        d_ref[...] = -ADAM_LR * (m_hat / (jnp.sqrt(v_hat) + ADAM_EPS) + ADAM_WD * w_ref[...])
        nm_ref[...] = nm
        nv_ref[...] = nv

    spec = pl.BlockSpec((tr, cols), lambda i: (i, 0))
    shape = jax.ShapeDtypeStruct((rows, cols), F32)
    return pl.pallas_call(
        body, name=name, grid=(rows // tr,), in_specs=[spec] * 4, out_specs=[spec] * 3, out_shape=[shape] * 3,
        compiler_params=_params(("parallel",)),
    )(w, g, m, v)


def kernel(x, norm_w, w_in, conv_w, a_log, dt_bias, gdn_norm_w, w_up_a, w_up_b, w_out, final_norm_w, loss_target, m_norm_w, m_w_in, m_conv_w, m_a_log, m_dt_bias, m_gdn_norm_w, m_w_up_a, m_w_up_b, m_w_out, m_final_norm_w, v_norm_w, v_w_in, v_conv_w, v_a_log, v_dt_bias, v_gdn_norm_w, v_w_up_a, v_w_up_b, v_w_out, v_final_norm_w):
    conv_hi = conv_w[0].astype(MXU_DTYPE)
    conv_lo = (conv_w[0] - conv_hi.astype(F32)).astype(MXU_DTYPE)
    big = w_in[0].astype(MXU_DTYPE).reshape(2, *BIG_HALF)
    slab = _pack_slab(w_up_a[0].astype(MXU_DTYPE), w_up_b[0].astype(MXU_DTYPE), w_out[0].astype(MXU_DTYPE), conv_hi, conv_lo)
    own_shard = 2 * lax.axis_index("x") + lax.axis_index("y")
    bigs, slabs = _gather_weights([big, slab])
    bigs = lax.dynamic_update_slice(bigs, big[None], (own_shard, 0, 0, 0)).reshape(SHARDS, D_MODEL, W_IN_SHARD)
    slabs = lax.dynamic_update_slice(slabs, slab[None], (own_shard, 0, 0, 0))
    parts = [_unpack_slab(slabs[s]) for s in range(SHARDS)]
    split = BA_END - (SHARDS - 1) * W_IN_SHARD
    wp = jnp.concatenate([bigs[s] for s in range(SHARDS - 1)]
                         + [bigs[-1][:, :split], jnp.zeros((D_MODEL, SEG_GA - BA_END), MXU_DTYPE), bigs[-1][:, split:]], axis=1)
    w_up_a_full = jnp.concatenate([p[0] for p in parts], axis=1)
    w_up_b_full = jnp.concatenate([p[1] for p in parts], axis=1)
    w_out_full = jnp.concatenate([p[2] for p in parts], axis=0)
    conv_full = jnp.concatenate([p[3].astype(F32) + p[4].astype(F32) for p in parts], axis=1)

    blocks, tags = (128, HALF_ROWS), ("w_in", "slab")

    def start_reduce(d_wp, d_w_up_a, d_w_up_b, d_w_out, d_conv_w):
        d_w_in = [d_wp[:, s * W_IN_SHARD:(s + 1) * W_IN_SHARD] for s in range(SHARDS - 1)]
        d_w_in.append(jnp.concatenate([d_wp[:, (SHARDS - 1) * W_IN_SHARD:BA_END], d_wp[:, SEG_GA:]], axis=1))
        zero_conv = jnp.zeros((GDN_CONV, CONV_SHARD), F32)
        grads = [jnp.stack(d_w_in).reshape(SHARDS, 2, *BIG_HALF),
                 jnp.stack([_pack_slab(d_w_up_a[:, s * 256:(s + 1) * 256], d_w_up_b[:, s * 256:(s + 1) * 256],
                                       d_w_out[s * 256:(s + 1) * 256], d_conv_w[:, s * CONV_SHARD:(s + 1) * CONV_SHARD],
                                       zero_conv) for s in range(SHARDS)])]
        from_sibling = _exchange_halves(grads)
        pairs = [_pair_sum(gr, fs, blk, f"pair_sum_{tag}") for gr, fs, blk, tag in zip(grads, from_sibling, blocks, tags)]
        *in_flight, token = _scatter_start(pairs)
        return in_flight, token

    g = _local_step(x[0], loss_target[0], norm_w, wp, conv_full, a_log, dt_bias, gdn_norm_w,
                    w_up_a_full, w_up_b_full, w_out_full, final_norm_w[None], start_reduce)

    send_sems, recv_sems, *passed = g["in_flight"]
    arrived = _scatter_wait(send_sems, recv_sems, passed, after=g["grad_x"])
    pairs, from_chips = arrived[:2], arrived[2:]
    total_big, total_slab = _share_total([_chip_sum(p, fc, blk, f"chip_sum_{tag}")
                                          for p, fc, blk, tag in zip(pairs, from_chips, blocks, tags)])
    g_w_in = total_big.reshape(D_MODEL, W_IN_SHARD)
    g_w_up_a, g_w_up_b, g_w_out, g_conv, _ = _unpack_slab(total_slab)

    row2 = jnp.concatenate([g["gdn_norm_w"], g["a_log"], g["dt_bias"], g["loss"][0:1, 0:1],
                            jnp.zeros((1, D_MODEL - HEAD_DIM - 2 * HEADS - 1), F32)], axis=1)
    small = _allreduce_small(jnp.concatenate([g["norm_w"], g["final_norm_w"], row2, jnp.zeros((5, D_MODEL), F32)], axis=0))
    g_norm, g_final = small[0:1], small[1]
    g_gnw, g_alog, g_dt = small[2:3, 0:HEAD_DIM], small[2:3, HEAD_DIM:HEAD_DIM + HEADS], small[2:3, HEAD_DIM + HEADS:HEAD_DIM + 2 * HEADS]
    loss = small[2, HEAD_DIM + 2 * HEADS]

    names = ["norm_w", "w_in", "conv_w", "a_log", "dt_bias", "gdn_norm_w", "w_up_a", "w_up_b", "w_out", "final_norm_w"]
    weights = dict(zip(names, (norm_w, w_in, conv_w, a_log, dt_bias, gdn_norm_w, w_up_a, w_up_b, w_out, final_norm_w)))
    ms = dict(zip(names, (m_norm_w, m_w_in, m_conv_w, m_a_log, m_dt_bias, m_gdn_norm_w, m_w_up_a, m_w_up_b, m_w_out, m_final_norm_w)))
    vs = dict(zip(names, (v_norm_w, v_w_in, v_conv_w, v_a_log, v_dt_bias, v_gdn_norm_w, v_w_up_a, v_w_up_b, v_w_out, v_final_norm_w)))
    grads2d = dict(norm_w=g_norm, w_in=g_w_in, conv_w=g_conv, a_log=g_alog, dt_bias=g_dt, gdn_norm_w=g_gnw,
                   w_up_a=g_w_up_a, w_up_b=g_w_up_b, w_out=g_w_out, final_norm_w=g_final[None])
    grad_out, delta, new_m, new_v = [], [], [], []
    for n in names:
        shape = weights[n].shape
        two_d = grads2d[n].shape
        d, nm, nv = _adamw(weights[n].reshape(two_d), grads2d[n], ms[n].reshape(two_d), vs[n].reshape(two_d), f"adamw_{n}")
        grad_out.append(grads2d[n].reshape(shape))
        delta.append(d.reshape(shape))
        new_m.append(nm.reshape(shape))
        new_v.append(nv.reshape(shape))
    return (loss, g["grad_x"][None], *grad_out, *delta, *new_m, *new_v)
```

```python
import functools

import jax
import jax.numpy as jnp
from jax import lax
from jax.experimental import pallas as pl
from jax.experimental.pallas import tpu as pltpu

F32 = jnp.float32
MXU_DTYPE = jnp.bfloat16
HIGHEST = lax.Precision.HIGHEST

D_MODEL = 1024
HEADS = 8
HEAD_DIM = 64
WIDTH = HEADS * HEAD_DIM
NORM_EPS = 1e-6
ROPE_THETA = 10000.0
ATT_BLOCK = 128
DILATIONS = (1, 4, 16)
GDN_CHUNK = 64
GDN_CONV = 4
IN_WIDTH = 9232
SEG_A, SEG_ZA, SEG_B, SEG_ZB, SEG_BA, SEG_GA, SEG_GB, PACKED_WIDTH = 0, 4608, 5120, 6656, 7168, 7680, 8704, 9728
BA_END = 7184
VMEM_LIMIT = 56 * 1024 * 1024

ADAM_LR, ADAM_B1, ADAM_B2, ADAM_EPS, ADAM_WD, ADAM_STEP = 0.001, 0.9, 0.999, 1e-08, 0.01, 10

_NN = (((1,), (0,)), ((), ()))
_NT = (((1,), (1,)), ((), ()))
_TN = (((0,), (0,)), ((), ()))


def _params(sem):
    return pltpu.CompilerParams(dimension_semantics=sem, vmem_limit_bytes=VMEM_LIMIT)


def _mxu(a, b, dims):
    return lax.dot_general(a.astype(MXU_DTYPE), b.astype(MXU_DTYPE), dims, preferred_element_type=F32)


def _sigmoid(x):
    return 1.0 / (1.0 + jnp.exp(-x))


def _softplus(x):
    return jnp.maximum(x, 0.0) + jnp.log(1.0 + jnp.exp(-jnp.abs(x)))


def _iota(shape, axis):
    return lax.broadcasted_iota(jnp.int32, shape, axis)


def _matmul(a, b, mode, name, out_dtype=F32, tm=512, tn=512, tk=512):
    if mode == "nn":
        (m, k), (k2, n) = a.shape, b.shape
    elif mode == "nt":
        (m, k), (n, k2) = a.shape, b.shape
    else:
        (k, m), (k2, n) = a.shape, b.shape
    assert k == k2
    tm, tn, tk = min(tm, m), min(tn, n), min(tk, k)
    assert m % tm == 0 and n % tn == 0 and k % tk == 0
    nk = k // tk
    dims = {"nn": _NN, "nt": _NT, "tn": _TN}[mode]

    assert out_dtype == F32

    def body(a_ref, b_ref, o_ref):
        kk = pl.program_id(2)
        part = _mxu(a_ref[...], b_ref[...], dims)

        @pl.when(kk == 0)
        def _():
            o_ref[...] = part

        @pl.when(kk > 0)
        def _():
            o_ref[...] += part

    a_spec = pl.BlockSpec((tk, tm), lambda i, j, kk: (kk, i)) if mode == "tn" else pl.BlockSpec((tm, tk), lambda i, j, kk: (i, kk))
    b_spec = pl.BlockSpec((tn, tk), lambda i, j, kk: (j, kk)) if mode == "nt" else pl.BlockSpec((tk, tn), lambda i, j, kk: (kk, j))
    return pl.pallas_call(
        body, name=name, grid=(m // tm, n // tn, nk), in_specs=[a_spec, b_spec],
        out_specs=pl.BlockSpec((tm, tn), lambda i, j, kk: (i, j)),
        out_shape=jax.ShapeDtypeStruct((m, n), out_dtype),
        compiler_params=_params(("parallel", "parallel", "arbitrary")),
    )(a, b)


def _norm_proj(x, norm_w, wp, tm=1024, tn=PACKED_WIDTH // 4):
    t = x.shape[0]
    tm = min(tm, t)

    def body(x_ref, nw_ref, w_ref, proj_ref, ht_ref, h_scr):
        @pl.when(pl.program_id(1) == 0)
        def _():
            xf = x_ref[...]
            r = lax.rsqrt(jnp.mean(xf * xf, axis=-1, keepdims=True) + NORM_EPS)
            h = xf * r * nw_ref[...]
            h_scr[...] = h.astype(h_scr.dtype)
            ht_ref[...] = h.T.astype(ht_ref.dtype)

        proj_ref[...] = jnp.dot(h_scr[...], w_ref[...], preferred_element_type=F32)

    return pl.pallas_call(
        body, name="norm_proj", grid=(t // tm, PACKED_WIDTH // tn),
        in_specs=[pl.BlockSpec((tm, D_MODEL), lambda i, j: (i, 0)),
                  pl.BlockSpec((1, D_MODEL), lambda i, j: (0, 0)),
                  pl.BlockSpec((D_MODEL, tn), lambda i, j: (0, j))],
        out_specs=[pl.BlockSpec((tm, tn), lambda i, j: (i, j)),
                   pl.BlockSpec((D_MODEL, tm), lambda i, j: (0, i))],
        out_shape=[jax.ShapeDtypeStruct((t, PACKED_WIDTH), F32), jax.ShapeDtypeStruct((D_MODEL, t), MXU_DTYPE)],
        scratch_shapes=[pltpu.VMEM((tm, D_MODEL), MXU_DTYPE)],
        compiler_params=_params(("parallel", "arbitrary")),
    )(x, norm_w, wp)


def _rope_tables(t):
    lane = jnp.arange(128)
    inv_freq = ROPE_THETA ** (-jnp.arange(0, HEAD_DIM, 2, dtype=F32) / HEAD_DIM)
    freq = jnp.concatenate([inv_freq] * 4)
    coarse = (jnp.arange(t // 128, dtype=F32) * 128.0)[:, None] * freq[None, :]
    fine = jnp.arange(128, dtype=F32)[:, None] * freq[None, :]
    ca, sa, cb, sb = jnp.cos(coarse)[:, None, :], jnp.sin(coarse)[:, None, :], jnp.cos(fine)[None], jnp.sin(fine)[None]
    cos = (ca * cb - sa * sb).reshape(t, 128)
    sin = (sa * cb + ca * sb).reshape(t, 128)
    first_half = (lane % HEAD_DIM) < HEAD_DIM // 2
    return cos, jnp.where(first_half, -sin, 0.0), jnp.where(first_half, 0.0, sin)


def _rope_cols(x, cos, sin_lo, sin_hi, sign):
    outs = []
    for c in range(x.shape[1] // 128):
        xc = x[:, c * 128:(c + 1) * 128]
        rot = pltpu.roll(xc, 96, 1) * sin_lo + pltpu.roll(xc, 32, 1) * sin_hi
        outs.append(xc * cos + sign * rot)
    return jnp.concatenate(outs, axis=1)


def _rope_block(x, cos, sin_lo, sin_hi, sign):
    return jnp.concatenate([_rope_cols(x[:, :2 * WIDTH], cos, sin_lo, sin_hi, sign), x[:, 2 * WIDTH:]], axis=1)


def _tile_scratch(tm, cols):
    return pltpu.VMEM((cols // 128, tm, 128), F32)


def _store_tile(scr, y):
    for c in range(scr.shape[0]):
        scr[c] = y[:, c * 128:(c + 1) * 128]


def _load_tile(scr):
    return jnp.concatenate([scr[c] for c in range(scr.shape[0])], axis=1)


def _to_strided_view(scr, o_ref, d, width=None, col0=0):
    n, tm, _ = scr.shape
    width = n * 128 if width is None else width
    for r in range(d):
        for c in range(n):
            at = r * width + col0 + c * 128
            o_ref[:, at:at + 128] = scr[c, pl.ds(r, tm // d, stride=d), :].astype(o_ref.dtype)


def _from_strided_view(i_ref, scr, d):
    n, tm, _ = scr.shape
    for r in range(d):
        for c in range(n):
            scr[c, pl.ds(r, tm // d, stride=d), :] = i_ref[:, (r * n + c) * 128:(r * n + c + 1) * 128].astype(F32)


def _strided_spec(tm, d, cols):
    return pl.BlockSpec((tm // d, d * cols), lambda i: (i, 0))


def _rope_fwd(proj, tables, tm=256):
    t = proj.shape[0]
    cols = 3 * WIDTH

    def body(x_ref, c_ref, sl_ref, sh_ref, o0, o1, o2, scr):
        for g, (d, o_ref) in enumerate(zip(DILATIONS, (o0, o1, o2))):
            y = _rope_block(x_ref[:, g * cols:(g + 1) * cols], c_ref[...], sl_ref[...], sh_ref[...], 1.0)
            if d == 1:
                o_ref[...] = y.astype(o_ref.dtype)
            else:
                _store_tile(scr, y)
                _to_strided_view(scr, o_ref, d)

    tab = pl.BlockSpec((tm, 128), lambda i: (i, 0))
    return pl.pallas_call(
        body, name="rope_fwd", grid=(t // tm,),
        in_specs=[pl.BlockSpec((tm, 3 * cols), lambda i: (i, 0)), tab, tab, tab],
        out_specs=[_strided_spec(tm, d, cols) for d in DILATIONS],
        out_shape=[jax.ShapeDtypeStruct((t // d, d * cols), MXU_DTYPE) for d in DILATIONS],
        scratch_shapes=[_tile_scratch(tm, cols)],
        compiler_params=_params(("parallel",)),
    )(proj, *tables)


def _rope_bwd(dproj, dqs, dkvs, tables, tm=256):
    t = dproj.shape[0]
    cols = 3 * WIDTH

    def body(dp_ref, q0, q1, q2, kv0, kv1, kv2, c_ref, sl_ref, sh_ref, o_ref, scr_q, scr_kv):
        for g, (d, q_ref, kv_ref) in enumerate(zip(DILATIONS, (q0, q1, q2), (kv0, kv1, kv2))):
            if d == 1:
                x = jnp.concatenate([q_ref[...], kv_ref[...]], axis=1)
            else:
                _from_strided_view(q_ref, scr_q, d)
                _from_strided_view(kv_ref, scr_kv, d)
                x = jnp.concatenate([_load_tile(scr_q), _load_tile(scr_kv)], axis=1)
            y = _rope_block(x, c_ref[...], sl_ref[...], sh_ref[...], -1.0)
            o_ref[:, g * cols:(g + 1) * cols] = y.astype(o_ref.dtype)

    tab = pl.BlockSpec((tm, 128), lambda i: (i, 0))
    return pl.pallas_call(
        body, name="rope_bwd", grid=(t // tm,),
        in_specs=[pl.BlockSpec(memory_space=pl.ANY)] + [_strided_spec(tm, d, WIDTH) for d in DILATIONS]
        + [_strided_spec(tm, d, 2 * WIDTH) for d in DILATIONS] + [tab, tab, tab],
        out_specs=pl.BlockSpec((tm, 3 * cols), lambda i: (i, 0)),
        out_shape=jax.ShapeDtypeStruct((t, PACKED_WIDTH), MXU_DTYPE),
        scratch_shapes=[_tile_scratch(tm, WIDTH), _tile_scratch(tm, 2 * WIDTH)],
        input_output_aliases={0: 0},
        compiler_params=_params(("parallel",)),
    )(dproj, *dqs, *dkvs, *tables)


def _att_masks():
    qi = _iota((ATT_BLOCK, ATT_BLOCK), 0)
    kj = _iota((ATT_BLOCK, ATT_BLOCK), 1)
    return kj <= qi, kj >= qi


def _att_fwd(qkv, d, name):
    rows = qkv.shape[0]
    nb = rows // ATT_BLOCK
    scale = HEAD_DIM ** -0.5

    def body(q_ref, kc_ref, kp_ref, vc_ref, vp_ref, o_ref, lse_ref):
        has_prev = pl.program_id(1) > 0
        m_cur, m_prev = _att_masks()
        m_prev = m_prev & has_prev
        hs = range(HEADS)
        sls = [slice(h * HEAD_DIM, (h + 1) * HEAD_DIM) for h in hs]
        qs = [q_ref[:, sl] for sl in sls]
        s_c = [jnp.where(m_cur, _mxu(qs[h], kc_ref[:, sls[h]], _NT) * scale, -jnp.inf) for h in hs]
        s_p = [jnp.where(m_prev, _mxu(qs[h], kp_ref[:, sls[h]], _NT) * scale, -jnp.inf) for h in hs]
        m = [jnp.max(jnp.maximum(s_c[h], s_p[h]), axis=1, keepdims=True) for h in hs]
        p_c = [jnp.exp(s_c[h] - m[h]) for h in hs]
        p_p = [jnp.exp(s_p[h] - m[h]) for h in hs]
        den = [jnp.sum(p_c[h] + p_p[h], axis=1, keepdims=True) for h in hs]
        o = [_mxu(p_c[h], vc_ref[:, sls[h]], _NN) + _mxu(p_p[h], vp_ref[:, sls[h]], _NN) for h in hs]
        lane = _iota((1, 128), 1)
        lse = jnp.zeros((ATT_BLOCK, 128), F32)
        for h in hs:
            o_ref[:, sls[h]] = o[h] / den[h]
            lse = jnp.where(lane == h, m[h] + jnp.log(den[h]), lse)
        lse_ref[...] = lse

    def cur(c):
        return pl.BlockSpec((ATT_BLOCK, WIDTH), lambda r, i: (i, 3 * r + c))

    def prev(c):
        return pl.BlockSpec((ATT_BLOCK, WIDTH), lambda r, i: (jnp.maximum(i - 1, 0), 3 * r + c))

    return pl.pallas_call(
        body, name=name, grid=(d, nb), in_specs=[cur(0), cur(1), prev(1), cur(2), prev(2)],
        out_specs=[pl.BlockSpec((ATT_BLOCK, WIDTH), lambda r, i: (i, r)), pl.BlockSpec((ATT_BLOCK, 128), lambda r, i: (i, r))],
        out_shape=[jax.ShapeDtypeStruct((rows, d * WIDTH), F32), jax.ShapeDtypeStruct((rows, d * 128), F32)],
        compiler_params=_params(("parallel", "arbitrary")),
    )(qkv, qkv, qkv, qkv, qkv)


def _att_bwd(qkv, do, lse, delta, d, name):
    rows = qkv.shape[0]
    nb = rows // ATT_BLOCK
    scale = HEAD_DIM ** -0.5

    def body(q_ref, kc_ref, kp_ref, vc_ref, vp_ref, do_ref, l_ref, dl_ref, dq_ref, dkv_ref, own):
        i = pl.program_id(1)
        m_cur, m_prev = _att_masks()
        m_prev = m_prev & (i > 0)

        hs = range(HEADS)
        sls = [slice(h * HEAD_DIM, (h + 1) * HEAD_DIM) for h in hs]
        col = [slice(h, h + 1) for h in hs]

        def probs(k_r, mask):
            s = [_mxu(q_ref[:, sls[h]], k_r[:, sls[h]], _NT) for h in hs]
            return [jnp.where(mask, jnp.exp(s[h] * scale - l_ref[:, col[h]]), 0.0) for h in hs]

        def dscores(p, v_r):
            dp = [_mxu(do_ref[:, sls[h]], v_r[:, sls[h]], _NT) for h in hs]
            return [(p[h] * (dp[h] - dl_ref[:, col[h]])).astype(MXU_DTYPE) for h in hs]

        @pl.when(i == 0)
        def _():
            own[...] = jnp.zeros_like(own)

        @pl.when(i < nb)
        def _():
            p_c = probs(kc_ref, m_cur)
            ds_c = dscores(p_c, vc_ref)
            p_p = probs(kp_ref, m_prev)
            ds_p = dscores(p_p, vp_ref)
            dq = [_mxu(ds_c[h], kc_ref[:, sls[h]], _NN) + _mxu(ds_p[h], kp_ref[:, sls[h]], _NN) for h in hs]
            dk_p = [_mxu(ds_p[h], q_ref[:, sls[h]], _TN) for h in hs]
            dv_p = [_mxu(p_p[h], do_ref[:, sls[h]], _TN) for h in hs]
            dk_c = [_mxu(ds_c[h], q_ref[:, sls[h]], _TN) for h in hs]
            dv_c = [_mxu(p_c[h], do_ref[:, sls[h]], _TN) for h in hs]
            for h in hs:
                vs = slice(WIDTH + h * HEAD_DIM, WIDTH + (h + 1) * HEAD_DIM)
                dq_ref[:, sls[h]] = dq[h] * scale
                dkv_ref[:, sls[h]] = own[:, sls[h]] + dk_p[h] * scale
                dkv_ref[:, vs] = own[:, vs] + dv_p[h]
                own[:, sls[h]] = dk_c[h] * scale
                own[:, vs] = dv_c[h]

        @pl.when(i == nb)
        def _():
            dkv_ref[...] = own[...]

    def qkv_spec(c, shift):
        return pl.BlockSpec((ATT_BLOCK, WIDTH), lambda r, i: (jnp.clip(i + shift, 0, nb - 1), 3 * r + c))

    tok = pl.BlockSpec((ATT_BLOCK, WIDTH), lambda r, i: (jnp.minimum(i, nb - 1), r))
    per_head = pl.BlockSpec((ATT_BLOCK, 128), lambda r, i: (jnp.minimum(i, nb - 1), r))
    return pl.pallas_call(
        body, name=name, grid=(d, nb + 1),
        in_specs=[qkv_spec(0, 0), qkv_spec(1, 0), qkv_spec(1, -1), qkv_spec(2, 0), qkv_spec(2, -1), tok, per_head, per_head],
        out_specs=[tok, pl.BlockSpec((ATT_BLOCK, 2 * WIDTH), lambda r, i: (jnp.maximum(i - 1, 0), r))],
        out_shape=[jax.ShapeDtypeStruct((rows, d * WIDTH), F32), jax.ShapeDtypeStruct((rows, d * 2 * WIDTH), F32)],
        scratch_shapes=[pltpu.VMEM((ATT_BLOCK, 2 * WIDTH), F32)],
        compiler_params=_params(("parallel", "arbitrary")),
    )(qkv, qkv, qkv, qkv, qkv, do, lse, delta)


def _att_merge(os_, lses, proj, tm=256):
    t = proj.shape[0]

    def body(o0, o1, o2, l0, l1, l2, z_ref, oz_ref, o_ref, t0, t1, t2, s_o1, s_o2, s_l1, s_l2, s_t):
        _from_strided_view(o1, s_o1, DILATIONS[1])
        _from_strided_view(o2, s_o2, DILATIONS[2])
        _from_strided_view(l1, s_l1, DILATIONS[1])
        _from_strided_view(l2, s_l2, DILATIONS[2])
        a, b, c = l0[...], s_l1[0], s_l2[0]
        m = jnp.maximum(jnp.maximum(a, b), c)
        wa, wb, wc = jnp.exp(a - m), jnp.exp(b - m), jnp.exp(c - m)
        den = wa + wb + wc
        total = m + jnp.log(den)
        t0[...] = total
        s_t[0] = total
        _to_strided_view(s_t, t1, DILATIONS[1])
        _to_strided_view(s_t, t2, DILATIONS[2])
        spread = jnp.where(_iota((128, WIDTH), 1) // HEAD_DIM == _iota((128, WIDTH), 0), 1.0, 0.0).astype(MXU_DTYPE)
        ra, rb, rc = (_pair_sum_lanes(w / den, spread) for w in (wa, wb, wc))
        o = ra * o0[...] + rb * _load_tile(s_o1) + rc * _load_tile(s_o2)
        z = z_ref[...]
        o_ref[...] = o
        oz_ref[...] = (o * z * _sigmoid(z)).astype(oz_ref.dtype)

    tok = pl.BlockSpec((tm, WIDTH), lambda i: (i, 0))
    views = [_strided_spec(tm, d, WIDTH) for d in DILATIONS]
    per_head = [_strided_spec(tm, d, 128) for d in DILATIONS]
    return pl.pallas_call(
        body, name="att_merge", grid=(t // tm,),
        in_specs=views + per_head + [pl.BlockSpec((tm, WIDTH), lambda i: (i, SEG_ZA // WIDTH))],
        out_specs=[tok, tok] + per_head,
        out_shape=[jax.ShapeDtypeStruct((t, WIDTH), MXU_DTYPE), jax.ShapeDtypeStruct((t, WIDTH), F32)]
        + [jax.ShapeDtypeStruct((t // d, d * 128), F32) for d in DILATIONS],
        scratch_shapes=[_tile_scratch(tm, WIDTH)] * 2 + [_tile_scratch(tm, 128)] * 3,
        compiler_params=_params(("parallel",)),
    )(*os_, *lses, proj)


def _att_merge_bwd(dproj, d_oz, o, proj, tm=256):
    t = proj.shape[0]

    def body(dp_ref, doz_ref, o_ref, z_ref, dz_ref, do0, do1, do2, dl0, dl1, dl2, s_do, s_dl):
        z, ov, g = z_ref[...], o_ref[...], doz_ref[...]
        sg = _sigmoid(z)
        do = g * z * sg
        dz_ref[...] = (g * ov * sg * (1.0 + z * (1.0 - sg))).astype(dz_ref.dtype)
        do0[...] = do.astype(do0.dtype)
        _store_tile(s_do, do)
        prod = do * ov
        delta = jnp.zeros((tm, 128), F32)
        for p in range(HEADS // 2):
            delta += _pair_sum_lanes(prod[:, p * 128:(p + 1) * 128], _pair_to_lanes(p, 0))
        dl0[...] = delta
        s_dl[0] = delta
        for d, do_v, dl_v in ((DILATIONS[1], do1, dl1), (DILATIONS[2], do2, dl2)):
            _to_strided_view(s_do, do_v, d)
            _to_strided_view(s_dl, dl_v, d)

    tok = pl.BlockSpec((tm, WIDTH), lambda i: (i, 0))
    seg = pl.BlockSpec((tm, WIDTH), lambda i: (i, SEG_ZA // WIDTH))
    views = [_strided_spec(tm, d, WIDTH) for d in DILATIONS]
    per_head = [_strided_spec(tm, d, 128) for d in DILATIONS]
    return pl.pallas_call(
        body, name="att_merge_bwd", grid=(t // tm,),
        in_specs=[pl.BlockSpec(memory_space=pl.ANY), tok, tok, seg],
        out_specs=[seg] + views + per_head,
        out_shape=[jax.ShapeDtypeStruct((t, PACKED_WIDTH), MXU_DTYPE)]
        + [jax.ShapeDtypeStruct((t // d, d * WIDTH), MXU_DTYPE) for d in DILATIONS]
        + [jax.ShapeDtypeStruct((t // d, d * 128), F32) for d in DILATIONS],
        scratch_shapes=[_tile_scratch(tm, WIDTH), _tile_scratch(tm, 128)],
        input_output_aliases={0: 0},
        compiler_params=_params(("parallel",)),
    )(dproj, d_oz, o, proj)


def _shift_down(x, halo, s):
    if s == 0:
        return x
    xs = pltpu.roll(x, s, 0)
    head = jnp.where(_iota((8, x.shape[1]), 0) < s, pltpu.roll(halo, s, 0), xs[0:8])
    return jnp.concatenate([head, xs[8:]], axis=0)


def _shift_up(x, nxt, s):
    if s == 0:
        return x
    n = x.shape[0]
    xs = pltpu.roll(x, n - s, 0)
    tail = jnp.where(_iota((8, x.shape[1]), 0) >= 8 - s, pltpu.roll(nxt, 8 - s, 0), xs[n - 8:])
    return jnp.concatenate([xs[:n - 8], tail], axis=0)


def _conv_fwd(proj, conv_w, tm=1024):
    t = proj.shape[0]
    cb = SEG_B // WIDTH

    def body(x_ref, halo_ref, w_ref, c_ref):
        halo = jnp.where(pl.program_id(0) > 0, halo_ref[...], 0.0)
        x = x_ref[...]
        w = w_ref[...]
        acc = jnp.zeros((tm, WIDTH), F32)
        for j in range(GDN_CONV):
            acc += _shift_down(x, halo, GDN_CONV - 1 - j) * w[j:j + 1, :]
        c_ref[...] = acc

    return pl.pallas_call(
        body, name="conv_fwd", grid=(t // tm, 3),
        in_specs=[pl.BlockSpec((tm, WIDTH), lambda i, c: (i, cb + c)),
                  pl.BlockSpec((8, WIDTH), lambda i, c: (jnp.maximum(i * (tm // 8) - 1, 0), cb + c)),
                  pl.BlockSpec((GDN_CONV, WIDTH), lambda i, c: (0, c))],
        out_specs=pl.BlockSpec((tm, WIDTH), lambda i, c: (i, c)),
        out_shape=jax.ShapeDtypeStruct((t, 3 * WIDTH), F32),
        compiler_params=_params(("parallel", "parallel")),
    )(proj, proj, conv_w)


def _conv_bwd(dproj, dc, proj, conv_w, tm=1024):
    t = proj.shape[0]
    cb = SEG_B // WIDTH
    nt = t // tm

    def body(dp_ref, dc_ref, dcn_ref, x_ref, halo_ref, w_ref, dx_ref, dw_ref):
        i = pl.program_id(1)
        w = w_ref[...]
        dcn = jnp.where(i < nt - 1, dcn_ref[...], 0.0)
        dcv = dc_ref[...]
        acc = jnp.zeros((tm, WIDTH), F32)
        for j in range(GDN_CONV):
            acc += _shift_up(dcv, dcn, GDN_CONV - 1 - j) * w[j:j + 1, :]
        dx_ref[...] = acc.astype(dx_ref.dtype)
        halo = jnp.where(i > 0, halo_ref[...], 0.0)
        x = x_ref[...]
        row8 = _iota((8, WIDTH), 0)
        part = jnp.zeros((8, WIDTH), F32)
        for j in range(GDN_CONV):
            s = jnp.sum(dcv * _shift_down(x, halo, GDN_CONV - 1 - j), axis=0, keepdims=True)
            part += jnp.where(row8 == j, s, 0.0)

        @pl.when(i == 0)
        def _():
            dw_ref[...] = jnp.zeros_like(dw_ref)

        dw_ref[...] += part

    return pl.pallas_call(
        body, name="conv_bwd", grid=(3, nt),
        in_specs=[pl.BlockSpec(memory_space=pl.ANY),
                  pl.BlockSpec((tm, WIDTH), lambda c, i: (i, c)),
                  pl.BlockSpec((8, WIDTH), lambda c, i: (jnp.minimum((i + 1) * (tm // 8), t // 8 - 1), c)),
                  pl.BlockSpec((tm, WIDTH), lambda c, i: (i, cb + c)),
                  pl.BlockSpec((8, WIDTH), lambda c, i: (jnp.maximum(i * (tm // 8) - 1, 0), cb + c)),
                  pl.BlockSpec((GDN_CONV, WIDTH), lambda c, i: (0, c))],
        out_specs=[pl.BlockSpec((tm, WIDTH), lambda c, i: (i, cb + c)),
                   pl.BlockSpec((8, WIDTH), lambda c, i: (0, c))],
        out_shape=[jax.ShapeDtypeStruct((t, PACKED_WIDTH), MXU_DTYPE), jax.ShapeDtypeStruct((8, 3 * WIDTH), F32)],
        input_output_aliases={0: 0},
        compiler_params=_params(("parallel", "arbitrary")),
    )(dproj, dc, dc, proj, proj, conv_w)


def _chunk_matrices(tm):
    r, c = _iota((tm, tm), 0), _iota((tm, tm), 1)
    same = (r // GDN_CHUNK) == (c // GDN_CHUNK)
    return jnp.where(same & (c <= r), 1.0, 0.0), jnp.where(same, 1.0, 0.0)


def _gdn_gates(ba, a_log, dt_bias):
    al = ba + dt_bias
    return _sigmoid(ba), -jnp.exp(a_log) * _softplus(al), _sigmoid(al)


SCAN_HEADS = (0, 2, 4, 6, 1, 3, 5, 7)


def _head_lane_eye():
    return jnp.where(_iota((HEADS, 128), 1) == _iota((HEADS, 128), 0) + HEADS, 1.0, 0.0)


def _pair_ones():
    return jnp.where(_iota((128, 128), 0) // HEAD_DIM == _iota((128, 128), 1) // HEAD_DIM, 1.0, 0.0).astype(MXU_DTYPE)


def _pair_to_lanes(p, base):
    return jnp.where(_iota((128, 128), 1) == base + 2 * p + _iota((128, 128), 0) // HEAD_DIM, 1.0, 0.0).astype(MXU_DTYPE)


def _pair_sum_lanes(x, sel):
    hi, lo = _split(x)
    return jnp.dot(hi, sel, preferred_element_type=F32) + jnp.dot(lo, sel, preferred_element_type=F32)


def _gdn_prep(conv, proj, a_log, dt_bias, tm=256):
    t = proj.shape[0]
    nc = tm // GDN_CHUNK

    def body(c_ref, ba_ref, al_ref, dt_ref, q_ref, k_ref, v_ref, b_ref, g_ref, gl_ref, grow_ref):
        beta, g, _ = _gdn_gates(ba_ref[:, 0:128], al_ref[...], dt_ref[...])
        lmat, cmat = _chunk_matrices(tm)
        gc = jnp.dot(lmat, g, precision=HIGHEST, preferred_element_type=F32)
        gl = jnp.dot(cmat, g, precision=HIGHEST, preferred_element_type=F32)
        grow = lax.dot_general(_head_lane_eye(), gc, _NT, precision=HIGHEST, preferred_element_type=F32)
        ones = _pair_ones()
        first = _iota((1, 128), 1) < HEAD_DIM

        def spread(x, base, p):
            return jnp.where(first, x[:, base + 2 * p:base + 2 * p + 1], x[:, base + 2 * p + 1:base + 2 * p + 2])

        for p in range(HEADS // 2):
            for seg, ref, scale in ((0, q_ref, HEAD_DIM ** -0.5), (1, k_ref, 1.0), (2, v_ref, None)):
                c = c_ref[:, seg * WIDTH + p * 128:seg * WIDTH + (p + 1) * 128]
                a = c * _sigmoid(c)
                if scale is not None:
                    a = a * (lax.rsqrt(_pair_sum_lanes(a * a, ones) + NORM_EPS) * scale)
                ref[p] = a
            b_ref[p] = spread(beta, 0, p)
            g_ref[p] = spread(gc, HEADS, p)
            gl_ref[p] = spread(gl, HEADS, p)
        for pos, h in enumerate(SCAN_HEADS):
            for cc in range(nc):
                grow_ref[pos, cc] = grow[h:h + 1, cc * GDN_CHUNK:(cc + 1) * GDN_CHUNK]

    hm = pl.BlockSpec((HEADS // 2, tm, 128), lambda i: (0, i, 0))
    small = pl.BlockSpec((1, 128), lambda i: (0, 0))
    hm_shape = jax.ShapeDtypeStruct((HEADS // 2, t, 128), F32)
    return pl.pallas_call(
        body, name="gdn_prep", grid=(t // tm,),
        in_specs=[pl.BlockSpec((tm, 3 * WIDTH), lambda i: (i, 0)),
                  pl.BlockSpec((tm, WIDTH), lambda i: (i, SEG_BA // WIDTH)), small, small],
        out_specs=[hm] * 6 + [pl.BlockSpec((HEADS, nc, 1, GDN_CHUNK), lambda i: (0, i, 0, 0))],
        out_shape=[hm_shape] * 6 + [jax.ShapeDtypeStruct((HEADS, t // GDN_CHUNK, 1, GDN_CHUNK), F32)],
        compiler_params=_params(("parallel",)),
    )(conv, proj, a_log, dt_bias)


def _gdn_prep_bwd(dproj, conv, proj, a_log, dt_bias, dq, dk, dv, db, dg, dgl, dgrow, tm=256):
    t = proj.shape[0]
    nc = tm // GDN_CHUNK

    def body(dp_ref, c_ref, ba_ref, al_ref, dt_ref, dq_ref, dk_ref, dv_ref, db_ref, dg_ref, dgl_ref, dgrow_ref,
             dba_ref, dc_ref, small_ref, row_scr):
        beta, g, sig_al = _gdn_gates(ba_ref[:, 0:128], al_ref[...], dt_ref[...])
        d_beta = jnp.zeros((tm, 128), F32)
        d_gc = jnp.zeros((tm, 128), F32)
        d_gl = jnp.zeros((tm, 128), F32)
        ones = _pair_ones()
        for pos, h in enumerate(SCAN_HEADS):
            for cc in range(nc):
                row_scr[h:h + 1, cc * GDN_CHUNK:(cc + 1) * GDN_CHUNK] = dgrow_ref[pos, cc]
        for p in range(HEADS // 2):
            d_beta += _pair_sum_lanes(db_ref[p], _pair_to_lanes(p, 0))
            d_gc += _pair_sum_lanes(dg_ref[p], _pair_to_lanes(p, HEADS))
            d_gl += _pair_sum_lanes(dgl_ref[p], _pair_to_lanes(p, HEADS))
            for seg, ref, scale in ((0, dq_ref, HEAD_DIM ** -0.5), (1, dk_ref, 1.0), (2, dv_ref, None)):
                cols = slice(seg * WIDTH + p * 128, seg * WIDTH + (p + 1) * 128)
                c = c_ref[:, cols]
                sg = _sigmoid(c)
                da = ref[p]
                if scale is not None:
                    a = c * sg
                    r = lax.rsqrt(_pair_sum_lanes(a * a, ones) + NORM_EPS)
                    da = da * scale
                    da = r * da - a * (r * r * r) * _pair_sum_lanes(da * a, ones)
                dc_ref[:, cols] = da * (sg * (1.0 + c * (1.0 - sg)))
        d_gc += lax.dot_general(row_scr[...], _head_lane_eye(), _TN, precision=HIGHEST, preferred_element_type=F32)
        lmat, cmat = _chunk_matrices(tm)
        d_g = (lax.dot_general(lmat, d_gc, _TN, precision=HIGHEST, preferred_element_type=F32)
               + lax.dot_general(cmat, d_gl, _TN, precision=HIGHEST, preferred_element_type=F32))
        d_al = d_g * (-jnp.exp(al_ref[...])) * sig_al
        d_bl = d_beta * beta * (1.0 - beta)
        dba_ref[...] = jnp.concatenate([d_bl + d_al, jnp.zeros((tm, WIDTH - 128), F32)], axis=1).astype(dba_ref.dtype)
        row8 = _iota((8, 128), 0)
        part = (jnp.where(row8 == 0, jnp.sum(d_g * g, axis=0, keepdims=True), 0.0)
                + jnp.where(row8 == 1, jnp.sum(d_al, axis=0, keepdims=True), 0.0))

        @pl.when(pl.program_id(0) == 0)
        def _():
            small_ref[...] = jnp.zeros_like(small_ref)

        small_ref[...] += part

    hm = pl.BlockSpec((HEADS // 2, tm, 128), lambda i: (0, i, 0))
    small = pl.BlockSpec((1, 128), lambda i: (0, 0))
    seg = pl.BlockSpec((tm, WIDTH), lambda i: (i, SEG_BA // WIDTH))
    return pl.pallas_call(
        body, name="gdn_prep_bwd", grid=(t // tm,),
        in_specs=[pl.BlockSpec(memory_space=pl.ANY), pl.BlockSpec((tm, 3 * WIDTH), lambda i: (i, 0)), seg, small, small]
        + [hm] * 6 + [pl.BlockSpec((HEADS, nc, 1, GDN_CHUNK), lambda i: (0, i, 0, 0))],
        out_specs=[seg, pl.BlockSpec((tm, 3 * WIDTH), lambda i: (i, 0)), pl.BlockSpec((8, 128), lambda i: (0, 0))],
        out_shape=[jax.ShapeDtypeStruct((t, PACKED_WIDTH), MXU_DTYPE), jax.ShapeDtypeStruct((t, 3 * WIDTH), F32),
                   jax.ShapeDtypeStruct((8, 128), F32)],
        scratch_shapes=[pltpu.VMEM((HEADS, tm), F32)],
        input_output_aliases={0: 0},
        compiler_params=_params(("arbitrary",)),
    )(dproj, conv, proj, a_log, dt_bias, dq, dk, dv, db, dg, dgl, dgrow)


_BNN = (((2,), (1,)), ((0,), (0,)))
_BNT = (((2,), (2,)), ((0,), (0,)))
_BTN = (((1,), (1,)), ((0,), (0,)))


@jax.custom_vjp
def _MM_NN(a, b):
    return _mxu(a, b, _BNN)


@jax.custom_vjp
def _MM_NT(a, b):
    return _mxu(a, b, _BNT)


@jax.custom_vjp
def _MM_TN(a, b):
    return _mxu(a, b, _BTN)


_MM_NN.defvjp(lambda a, b: (_mxu(a, b, _BNN), (a, b)), lambda r, g: (_mxu(g, r[1], _BNT), _mxu(r[0], g, _BTN)))
_MM_NT.defvjp(lambda a, b: (_mxu(a, b, _BNT), (a, b)), lambda r, g: (_mxu(g, r[1], _BNN), _mxu(g, r[0], _BTN)))
_MM_TN.defvjp(lambda a, b: (_mxu(a, b, _BTN), (a, b)), lambda r, g: (_mxu(r[1], g, _BNT), _mxu(r[0], g, _BNN)))


def _split(a):
    hi = a.astype(MXU_DTYPE)
    return hi, (a - hi.astype(F32)).astype(MXU_DTYPE)


def _dot3(a, b, dims):
    (ah, al), (bh, bl) = a, b
    (ca,), (cb,) = dims[0]
    return lax.dot_general(jnp.concatenate([ah, ah, al], axis=ca), jnp.concatenate([bh, bl, bh], axis=cb), dims,
                           preferred_element_type=F32)


def _unit_lower_inverse(a):
    c = GDN_CHUNK
    eye = jnp.where(_iota((c, c), 0) == _iota((c, c), 1), 1.0, 0.0)
    x = eye - a
    p = a
    for _ in range(5):
        ps = _split(p)
        p = _dot3(ps, ps, _BNN)
        x = x + _dot3(_split(x), _split(p), _BNN)
    return x


@jax.custom_vjp
def _SAVED_INVERSE(a, t_inv):
    return t_inv


def _saved_inverse_bwd(t_inv, g):
    ts = _split(t_inv)
    return -_dot3(ts, _split(_dot3(_split(g), ts, _BNT)), _BTN), jnp.zeros_like(t_inv)


_SAVED_INVERSE.defvjp(lambda a, t_inv: (t_inv, t_inv), _saved_inverse_bwd)


def _gdn_chunk(q, k, v, beta, g1, g2, gl, state, t_inv=None):
    c = GDN_CHUNK
    if t_inv is None:
        _mm_nn, _mm_nt, _mm_tn = (functools.partial(_mxu, dims=dd) for dd in (_BNN, _BNT, _BTN))
    else:
        _mm_nn, _mm_nt, _mm_tn = _MM_NN, _MM_NT, _MM_TN
    row, col = _iota((c, c), 0), _iota((c, c), 1)
    incl, strict = row >= col, row > col
    decay = jnp.where(incl, jnp.exp(jnp.where(incl, g1 - g2, 0.0)), 0.0)
    eg = jnp.exp(g1)
    kb = k * beta
    a = _mm_nt(kb, k) * jnp.where(strict, decay, 0.0)
    inv = _unit_lower_inverse(a) if t_inv is None else _SAVED_INVERSE(a, t_inv)
    u = _mm_nn(inv, v * beta)
    w = _mm_nn(inv, kb * eg)
    attn = _mm_nt(q, k) * decay
    v_new = u - _mm_nn(w, state)
    o = _mm_nn(q * eg, state) + _mm_nn(attn, v_new)
    new_state = state * jnp.exp(gl) + _mm_tn(k * jnp.exp(gl - g1), v_new)
    return (o, new_state, inv) if t_inv is None else (o, new_state)


def _unpair(x):
    return jnp.concatenate([x[..., :HEAD_DIM], x[..., HEAD_DIM:]], axis=0)


def _gdn_fwd(q, k, v, beta, g, gl, grow, cpb=4):
    t = q.shape[1]
    rows = cpb * GDN_CHUNK
    lo, hi = slice(0, HEAD_DIM), slice(HEAD_DIM, 2 * HEAD_DIM)

    def body(q_ref, k_ref, v_ref, b_ref, g_ref, gl_ref, grow_ref, o_ref, st_ref, inv_ref, state):
        @pl.when(pl.program_id(0) == 0)
        def _():
            state[...] = jnp.zeros_like(state)

        s = state[...]
        for cc in range(cpb):
            sl = slice(cc * GDN_CHUNK, (cc + 1) * GDN_CHUNK)
            st_ref[:, cc, :, lo], st_ref[:, cc, :, hi] = s[:HEADS // 2], s[HEADS // 2:]
            g2 = jnp.broadcast_to(grow_ref[:, cc], (HEADS, GDN_CHUNK, GDN_CHUNK))
            o, s, inv = _gdn_chunk(*[_unpair(r[:, sl, :]) for r in (q_ref, k_ref, v_ref, b_ref, g_ref)], g2,
                                   _unpair(gl_ref[:, sl, :]), s)
            o_ref[:, sl, lo], o_ref[:, sl, hi] = o[:HEADS // 2], o[HEADS // 2:]
            inv_ref[:, cc, :, lo], inv_ref[:, cc, :, hi] = inv[:HEADS // 2], inv[HEADS // 2:]
        state[...] = s

    hm = pl.BlockSpec((HEADS // 2, rows, 128), lambda i: (0, i, 0))
    per_chunk = pl.BlockSpec((HEADS // 2, cpb, GDN_CHUNK, 128), lambda i: (0, i, 0, 0))
    chunk_shape = jax.ShapeDtypeStruct((HEADS // 2, t // GDN_CHUNK, GDN_CHUNK, 128), F32)
    return pl.pallas_call(
        body, name="gdn_fwd", grid=(t // rows,),
        in_specs=[hm] * 6 + [pl.BlockSpec((HEADS, cpb, 1, GDN_CHUNK), lambda i: (0, i, 0, 0))],
        out_specs=[hm, per_chunk, per_chunk],
        out_shape=[jax.ShapeDtypeStruct((HEADS // 2, t, 128), F32), chunk_shape, chunk_shape],
        scratch_shapes=[pltpu.VMEM((HEADS, GDN_CHUNK, HEAD_DIM), F32)],
        compiler_params=_params(("arbitrary",)),
    )(q, k, v, beta, g, gl, grow)


def _gdn_bwd(q, k, v, beta, g, gl, grow, states, invs, do, cpb=2):
    t = q.shape[1]
    rows = cpb * GDN_CHUNK
    nsteps = t // rows
    lo, hi = slice(0, HEAD_DIM), slice(HEAD_DIM, 2 * HEAD_DIM)

    def body(q_ref, k_ref, v_ref, b_ref, g_ref, gl_ref, grow_ref, st_ref, inv_ref, do_ref,
             dq_ref, dk_ref, dv_ref, db_ref, dg_ref, dgl_ref, dgrow_ref, dstate):
        @pl.when(pl.program_id(0) == 0)
        def _():
            dstate[...] = jnp.zeros_like(dstate)

        ds = dstate[...]
        for cc in reversed(range(cpb)):
            sl = slice(cc * GDN_CHUNK, (cc + 1) * GDN_CHUNK)
            g2 = jnp.broadcast_to(grow_ref[:, cc], (HEADS, GDN_CHUNK, GDN_CHUNK))
            _, vjp = jax.vjp(_gdn_chunk, *[_unpair(r[:, sl, :]) for r in (q_ref, k_ref, v_ref, b_ref, g_ref)], g2,
                             _unpair(gl_ref[:, sl, :]), _unpair(st_ref[:, cc]), _unpair(inv_ref[:, cc]))
            gq, gk, gv, gb, gg1, gg2, ggl, ds, _ = vjp((_unpair(do_ref[:, sl, :]), ds))
            for ref, val in ((dq_ref, gq), (dk_ref, gk), (dv_ref, gv), (db_ref, gb), (dg_ref, gg1), (dgl_ref, ggl)):
                ref[:, sl, lo], ref[:, sl, hi] = val[:HEADS // 2], val[HEADS // 2:]
            dgrow_ref[:, cc] = jnp.sum(gg2, axis=1, keepdims=True)
        dstate[...] = ds

    hm = pl.BlockSpec((HEADS // 2, rows, 128), lambda i: (0, nsteps - 1 - i, 0))
    rowspec = pl.BlockSpec((HEADS, cpb, 1, GDN_CHUNK), lambda i: (0, nsteps - 1 - i, 0, 0))
    per_chunk = pl.BlockSpec((HEADS // 2, cpb, GDN_CHUNK, 128), lambda i: (0, nsteps - 1 - i, 0, 0))
    hm_shape = jax.ShapeDtypeStruct((HEADS // 2, t, 128), F32)
    return pl.pallas_call(
        body, name="gdn_bwd", grid=(nsteps,),
        in_specs=[hm] * 6 + [rowspec, per_chunk, per_chunk, hm],
        out_specs=[hm] * 6 + [rowspec],
        out_shape=[hm_shape] * 6 + [jax.ShapeDtypeStruct((HEADS, t // GDN_CHUNK, 1, GDN_CHUNK), F32)],
        scratch_shapes=[pltpu.VMEM((HEADS, GDN_CHUNK, HEAD_DIM), F32)],
        compiler_params=_params(("arbitrary",)),
    )(q, k, v, beta, g, gl, grow, states, invs, do)


def _gdn_out(o_hm, gdn_norm_w, proj, tm=512):
    t = proj.shape[0]

    def body(o_ref, w_ref, z_ref, oz_ref):
        w = w_ref[...]
        ones = _pair_ones()
        for p in range(HEADS // 2):
            cols = slice(p * 128, (p + 1) * 128)
            o = o_ref[p]
            z = z_ref[:, cols]
            r = lax.rsqrt(_pair_sum_lanes(o * o, ones) * (1.0 / HEAD_DIM) + NORM_EPS)
            oz_ref[:, cols] = (o * r * w * (z * _sigmoid(z))).astype(oz_ref.dtype)

    tok = pl.BlockSpec((tm, WIDTH), lambda i: (i, 0))
    return pl.pallas_call(
        body, name="gdn_out", grid=(t // tm,),
        in_specs=[pl.BlockSpec((HEADS // 2, tm, 128), lambda i: (0, i, 0)), pl.BlockSpec((1, 128), lambda i: (0, 0)),
                  pl.BlockSpec((tm, WIDTH), lambda i: (i, SEG_ZB // WIDTH))],
        out_specs=tok, out_shape=jax.ShapeDtypeStruct((t, WIDTH), MXU_DTYPE),
        compiler_params=_params(("parallel",)),
    )(o_hm, jnp.tile(gdn_norm_w, (1, 2)), proj)


def _gdn_out_bwd(dproj, d_oz, o_hm, gdn_norm_w, proj, tm=512):
    t = proj.shape[0]

    def body(dp_ref, doz_ref, o_ref, w_ref, z_ref, dz_ref, do_ref, dw_ref):
        w = w_ref[...]
        ones = _pair_ones()
        dw = jnp.zeros((1, 128), F32)
        for p in range(HEADS // 2):
            cols = slice(p * 128, (p + 1) * 128)
            o = o_ref[p]
            z, g = z_ref[:, cols], doz_ref[:, cols]
            sg = _sigmoid(z)
            r = lax.rsqrt(_pair_sum_lanes(o * o, ones) * (1.0 / HEAD_DIM) + NORM_EPS)
            dz_ref[:, cols] = (g * (o * r * w) * (sg * (1.0 + z * (1.0 - sg)))).astype(dz_ref.dtype)
            dn = g * (z * sg)
            dw += jnp.sum(dn * o * r, axis=0, keepdims=True)
            dnw = dn * w
            do_ref[p] = r * dnw - o * (r * r * r) * (_pair_sum_lanes(dnw * o, ones) * (1.0 / HEAD_DIM))

        @pl.when(pl.program_id(0) == 0)
        def _():
            dw_ref[...] = jnp.zeros_like(dw_ref)

        dw_ref[...] += jnp.where(_iota((8, 128), 0) == 0, dw, 0.0)

    tok = pl.BlockSpec((tm, WIDTH), lambda i: (i, 0))
    seg = pl.BlockSpec((tm, WIDTH), lambda i: (i, SEG_ZB // WIDTH))
    hm = pl.BlockSpec((HEADS // 2, tm, 128), lambda i: (0, i, 0))
    dz, do, dw = pl.pallas_call(
        body, name="gdn_out_bwd", grid=(t // tm,),
        in_specs=[pl.BlockSpec(memory_space=pl.ANY), tok, hm, pl.BlockSpec((1, 128), lambda i: (0, 0)), seg],
        out_specs=[seg, hm, pl.BlockSpec((8, 128), lambda i: (0, 0))],
        out_shape=[jax.ShapeDtypeStruct((t, PACKED_WIDTH), MXU_DTYPE), jax.ShapeDtypeStruct((HEADS // 2, t, 128), F32),
                   jax.ShapeDtypeStruct((8, 128), F32)],
        input_output_aliases={0: 0},
        compiler_params=_params(("arbitrary",)),
    )(dproj, d_oz, o_hm, jnp.tile(gdn_norm_w, (1, 2)), proj)
    return dz, do, dw[:, :HEAD_DIM] + dw[:, HEAD_DIM:]


def _merge(y_a, y_b, proj, tm=1024):
    t = proj.shape[0]

    def body(ya_ref, yb_ref, ga_ref, gb_ref, m_ref):
        m_ref[...] = (_sigmoid(ga_ref[...]) * ya_ref[...] + _sigmoid(gb_ref[...]) * yb_ref[...]).astype(m_ref.dtype)

    half = pl.BlockSpec((tm, WIDTH), lambda i, c: (i, c))
    return pl.pallas_call(
        body, name="merge", grid=(t // tm, 2),
        in_specs=[half, half, pl.BlockSpec((tm, WIDTH), lambda i, c: (i, SEG_GA // WIDTH + c)),
                  pl.BlockSpec((tm, WIDTH), lambda i, c: (i, SEG_GB // WIDTH + c))],
        out_specs=half, out_shape=jax.ShapeDtypeStruct((t, D_MODEL), MXU_DTYPE),
        compiler_params=_params(("parallel", "parallel")),
    )(y_a, y_b, proj, proj)


def _merge_bwd(dproj, d_m, y, proj, seg, name, tm=1024):
    t = proj.shape[0]

    def body(*refs):
        dm_ref, y_ref, g_ref, dg_ref, dy_ref = refs[-5:]
        dm = dm_ref[...]
        s = _sigmoid(g_ref[...])
        dy_ref[...] = (dm * s).astype(dy_ref.dtype)
        dg_ref[...] = (dm * y_ref[...] * s * (1.0 - s)).astype(dg_ref.dtype)

    half = pl.BlockSpec((tm, WIDTH), lambda i, c: (i, c))
    gate = pl.BlockSpec((tm, WIDTH), lambda i, c: (i, seg // WIDTH + c))
    specs, args, aliases = [half, half, gate], [d_m, y, proj], {}
    if dproj is not None:
        specs, args, aliases = [pl.BlockSpec(memory_space=pl.ANY)] + specs, [dproj] + args, {0: 0}
    return pl.pallas_call(
        body, name=name, grid=(t // tm, 2), in_specs=specs, out_specs=[gate, half],
        out_shape=[jax.ShapeDtypeStruct((t, PACKED_WIDTH), MXU_DTYPE), jax.ShapeDtypeStruct((t, D_MODEL), MXU_DTYPE)],
        input_output_aliases=aliases,
        compiler_params=_params(("parallel", "parallel")),
    )(*args)


def _out_tail(merged, w_out, x, final_w, target, tm=1024):
    t = x.shape[0]
    tm = min(tm, t)

    def body(m_ref, wo_ref, x_ref, w_ref, t_ref, dxm_ref, dx_ref, loss_ref, dw_ref):
        x2 = x_ref[...] + jnp.dot(m_ref[...], wo_ref[...], preferred_element_type=F32)
        w = w_ref[...]
        r = lax.rsqrt(jnp.mean(x2 * x2, axis=-1, keepdims=True) + NORM_EPS)
        xn = x2 * r
        err = xn * w - t_ref[...]
        dy = err * (1.0 / D_MODEL)
        dyw = dy * w
        dx2 = r * dyw - x2 * (r * r * r) * jnp.mean(dyw * x2, axis=-1, keepdims=True)
        dx_ref[...] = dx2
        dxm_ref[...] = dx2.astype(dxm_ref.dtype)
        loss = 0.5 * jnp.sum(jnp.sum(err * err, axis=-1, keepdims=True) * (1.0 / D_MODEL), axis=0, keepdims=True)
        onehot = jnp.where((_iota((8, 128), 0) == 0) & (_iota((8, 128), 1) == 0), 1.0, 0.0)

        @pl.when(pl.program_id(0) == 0)
        def _():
            loss_ref[...] = jnp.zeros_like(loss_ref)
            dw_ref[...] = jnp.zeros_like(dw_ref)

        loss_ref[...] += loss * onehot
        dw_ref[...] += jnp.where(_iota((8, D_MODEL), 0) == 0, jnp.sum(dy * xn, axis=0, keepdims=True), 0.0)

    tok = pl.BlockSpec((tm, D_MODEL), lambda i: (i, 0))
    full = pl.BlockSpec((D_MODEL, D_MODEL), lambda i: (0, 0))
    return pl.pallas_call(
        body, name="out_tail", grid=(t // tm,),
        in_specs=[tok, full, tok, pl.BlockSpec((1, D_MODEL), lambda i: (0, 0)), tok],
        out_specs=[tok, tok, pl.BlockSpec((8, 128), lambda i: (0, 0)), pl.BlockSpec((8, D_MODEL), lambda i: (0, 0))],
        out_shape=[jax.ShapeDtypeStruct((t, D_MODEL), MXU_DTYPE), jax.ShapeDtypeStruct((t, D_MODEL), F32),
                   jax.ShapeDtypeStruct((8, 128), F32), jax.ShapeDtypeStruct((8, D_MODEL), F32)],
        compiler_params=_params(("arbitrary",)),
    )(merged, w_out, x, final_w, target)


def _dh_norm_bwd(dproj, wp, x, norm_w, dx2, after, blocks, name, carry=None, tm=1024, tk=PACKED_WIDTH // 4):
    t = x.shape[0]
    nk = PACKED_WIDTH // tk
    first, end = blocks

    def body(*refs):
        dp_ref, wp_ref, x_ref, w_ref, dx2_ref = refs[:5]
        dx_ref, dw_ref, acc = refs[-3:]
        kk = pl.program_id(1)
        part = _mxu(dp_ref[...], wp_ref[...], _NT)

        @pl.when(kk == 0)
        def _():
            acc[...] = part

        @pl.when(kk > 0)
        def _():
            acc[...] += part

        @pl.when((kk == 0) & (pl.program_id(0) == 0))
        def _():
            dw_ref[...] = jnp.zeros_like(dw_ref) if carry is None else refs[-4][...]

        @pl.when(kk == nk - 1)
        def _():
            xf, w, dh_ = x_ref[...], w_ref[...], acc[...]
            r = lax.rsqrt(jnp.mean(xf * xf, axis=-1, keepdims=True) + NORM_EPS)
            dhw = dh_ * w
            dx_ref[...] = dx2_ref[...] + r * dhw - xf * (r * r * r) * jnp.mean(dhw * xf, axis=-1, keepdims=True)
            dw_ref[...] += jnp.where(_iota((8, D_MODEL), 0) == 0, jnp.sum(dh_ * xf * r, axis=0, keepdims=True), 0.0)

    tok = pl.BlockSpec((tm, D_MODEL), lambda i, kk: (first + i, 0))
    small = pl.BlockSpec((8, D_MODEL), lambda i, kk: (0, 0))
    specs = [pl.BlockSpec((tm, tk), lambda i, kk: (first + i, kk)), pl.BlockSpec((D_MODEL, tk), lambda i, kk: (0, kk)),
             tok, pl.BlockSpec((1, D_MODEL), lambda i, kk: (0, 0)), tok, pl.BlockSpec(memory_space=pl.ANY)]
    args = [dproj, wp, x, norm_w, dx2, after]
    aliases = {}
    if carry is not None:
        specs += [pl.BlockSpec(memory_space=pl.ANY), small]
        args += list(carry)
        aliases = {len(args) - 2: 0}
    return pl.pallas_call(
        body, name=name, grid=(end - first, nk), in_specs=specs, out_specs=[tok, small],
        out_shape=[jax.ShapeDtypeStruct((t, D_MODEL), F32), jax.ShapeDtypeStruct((8, D_MODEL), F32)],
        scratch_shapes=[pltpu.VMEM((tm, D_MODEL), F32)],
        input_output_aliases=aliases,
        compiler_params=_params(("arbitrary", "arbitrary")),
    )(*args)


def _local_step(x, target, norm_w, wp, conv_w, a_log, dt_bias, gdn_norm_w, w_up_a, w_up_b, w_out, final_w,
                start_reduce, continue_reduce):
    t = x.shape[0]
    tables = _rope_tables(t)
    a_log = jnp.pad(a_log, ((0, 0), (HEADS, 128 - 2 * HEADS)))
    dt_bias = jnp.pad(dt_bias, ((0, 0), (HEADS, 128 - 2 * HEADS)))

    proj, h_t = _norm_proj(x, norm_w, wp)
    qkvs = _rope_fwd(proj, tables)
    outs, lses = zip(*[_att_fwd(qkvs[gi], d, f"att_fwd{gi}") for gi, d in enumerate(DILATIONS)])
    oz_a, o_a, *lse_views = _att_merge(outs, lses, proj)
    conv = _conv_fwd(proj, conv_w)
    gq, gk, gv, gb, gg, ggl, grow = _gdn_prep(conv, proj, a_log, dt_bias)
    o_b, states, invs = _gdn_fwd(gq, gk, gv, gb, gg, ggl, grow)
    oz_b = _gdn_out(o_b, gdn_norm_w, proj)
    big = dict(tm=1024, tn=1024, tk=1024)
    y_a = _matmul(oz_a, w_up_a, "nn", "up_a", **big)
    y_b = _matmul(oz_b, w_up_b, "nn", "up_b", **big)
    merged = _merge(y_a, y_b, proj)
    dx2_m, dx2, loss_blk, d_final = _out_tail(merged, w_out, x, final_w, target)

    d_wout = _matmul(merged, dx2_m, "tn", "d_w_out", **big)
    d_m = _matmul(dx2_m, w_out, "nt", "d_merged", **big)
    dproj, dy_a = _merge_bwd(None, d_m, y_a, proj, SEG_GA, "merge_bwd_a")
    dproj, dy_b = _merge_bwd(dproj, d_m, y_b, proj, SEG_GB, "merge_bwd_b")
    d_wua = _matmul(oz_a, dy_a, "tn", "d_w_up_a", **big)
    d_wub = _matmul(oz_b, dy_b, "tn", "d_w_up_b", **big)
    d_oz_a = _matmul(dy_a, w_up_a, "nt", "d_oz_a", **big)
    d_oz_b = _matmul(dy_b, w_up_b, "nt", "d_oz_b", **big)
    dproj, *views = _att_merge_bwd(dproj, d_oz_a, o_a, proj)
    do_views, delta_views = views[:3], views[3:]
    dqs, dkvs = zip(*[_att_bwd(qkvs[gi], do_views[gi], lse_views[gi], delta_views[gi], d, f"att_bwd{gi}")
                      for gi, d in enumerate(DILATIONS)])
    dproj = _rope_bwd(dproj, dqs, dkvs, tables)
    dproj, do_b, d_gnw = _gdn_out_bwd(dproj, d_oz_b, o_b, gdn_norm_w, proj)
    dgq, dgk, dgv, dgb, dgg, dggl, dgrow = _gdn_bwd(gq, gk, gv, gb, gg, ggl, grow, states, invs, do_b)
    dproj, dconv, d_small = _gdn_prep_bwd(dproj, conv, proj, a_log, dt_bias, dgq, dgk, dgv, dgb, dgg, dggl, dgrow)
    dproj, d_convw = _conv_bwd(dproj, dconv, proj, conv_w)
    d_wp = _matmul(h_t, dproj, "nn", "d_w_in", tm=1024, tn=PACKED_WIDTH // 4, tk=1024)
    in_flight, token = start_reduce(d_wp, d_wua, d_wub, d_wout, d_convw[0:GDN_CONV])
    nblk = t // 1024
    cut = max(1, nblk // 4)
    part = _dh_norm_bwd(dproj, wp, x, norm_w, dx2, token, (0, cut), "dh_norm_bwd_a")
    in_flight, token = continue_reduce(in_flight, part[0])
    grad_x, d_norm = _dh_norm_bwd(dproj, wp, x, norm_w, dx2, token, (cut, nblk), "dh_norm_bwd_b", carry=part)
    return dict(loss=loss_blk, grad_x=grad_x, norm_w=d_norm[0:1], in_flight=in_flight,
                a_log=d_small[0:1, HEADS:2 * HEADS], dt_bias=d_small[1:2, HEADS:2 * HEADS], gdn_norm_w=d_gnw[0:1],
                final_norm_w=d_final[0:1])


SHARDS = 4
W_IN_SHARD = IN_WIDTH // SHARDS
ROWS_UP = WIDTH * (D_MODEL // SHARDS) // 128
ROWS_OUT = (D_MODEL // SHARDS) * D_MODEL // 128
CONV_SHARD = 3 * WIDTH // SHARDS
ROWS_CONV = 16
SLAB_ROWS = 2 * ROWS_UP + ROWS_OUT + 2 * ROWS_CONV
HALF_ROWS = SLAB_ROWS // 2
BIG_HALF = (D_MODEL // 2, W_IN_SHARD)
SMALL_HALF = (HALF_ROWS, 128)
MESH = pl.DeviceIdType.MESH
ANY = pl.BlockSpec(memory_space=pl.ANY)


def _pad_rows(a, rows):
    return jnp.pad(a, ((0, rows - a.shape[0]), (0, 0)))


def _pack_slab(w_up_a, w_up_b, w_out, conv, conv_lo):
    parts = [w_up_a.reshape(ROWS_UP, 128), w_up_b.reshape(ROWS_UP, 128), w_out.reshape(ROWS_OUT, 128),
             _pad_rows(conv.reshape(-1, 128), ROWS_CONV), _pad_rows(conv_lo.reshape(-1, 128), ROWS_CONV)]
    return jnp.concatenate(parts, axis=0).reshape(2, *SMALL_HALF)


def _unpack_slab(slab):
    slab = slab.reshape(SLAB_ROWS, 128)
    r0 = 0
    out = []
    for rows, shape in ((ROWS_UP, (WIDTH, D_MODEL // SHARDS)), (ROWS_UP, (WIDTH, D_MODEL // SHARDS)),
                        (ROWS_OUT, (D_MODEL // SHARDS, D_MODEL)), (ROWS_CONV, None), (ROWS_CONV, None)):
        part = slab[r0:r0 + rows]
        out.append(part[:GDN_CONV * CONV_SHARD // 128].reshape(GDN_CONV, CONV_SHARD) if shape is None else part.reshape(shape))
        r0 += rows
    return out


def _mesh_position():
    x, y, c = lax.axis_index("x"), lax.axis_index("y"), lax.axis_index("c")
    return x, y, c, [(1 - x, y), (x, 1 - y), (1 - x, 1 - y)]


def _gather_weights(shards):
    n = len(shards)

    def body(*refs):
        in_refs, out_refs, (send_sems, recv_sems) = refs[:n], refs[n:2 * n], refs[2 * n:]
        x, y, c, chips = _mesh_position()

        def half(a, chip, which):
            return out_refs[a].at[2 * chip[0] + chip[1], which]

        def copy(k, src, dst, to):
            return pltpu.make_async_remote_copy(src_ref=src, dst_ref=dst, send_sem=send_sems.at[k], recv_sem=recv_sems.at[k],
                                                device_id=to, device_id_type=MESH)

        pairs = [(a, j, chip) for a in range(n) for j, chip in enumerate(chips)]
        first = [copy(6 * a + j, in_refs[a].at[c], half(a, (x, y), c), (*chip, c)) for a, j, chip in pairs]
        for cp in first:
            cp.start()
        passed = [copy(6 * a + 3 + j, half(a, chip, c), half(a, chip, c), (x, y, 1 - c)) for a, j, chip in pairs]
        for i, (a, j, chip) in enumerate(pairs):
            copy(6 * a + j, half(a, chip, c), half(a, chip, c), (x, y, c)).wait_recv()
            passed[i].start()
        for a, j, chip in pairs:
            copy(6 * a + 3 + j, half(a, chip, 1 - c), half(a, chip, 1 - c), (x, y, c)).wait_recv()
        for cp in first + passed:
            cp.wait_send()

    return pl.pallas_call(
        body, name="gather_weights", in_specs=[ANY] * n, out_specs=[ANY] * n,
        out_shape=[jax.ShapeDtypeStruct((SHARDS, *s.shape), s.dtype) for s in shards],
        scratch_shapes=[pltpu.SemaphoreType.DMA((6 * n,)), pltpu.SemaphoreType.DMA((6 * n,))],
    )(*shards)


def _pair_sum(grads, recv, blk, name):
    _, _, rows, cols = grads.shape

    def body(c_ref, g_ref, r_ref, o_ref):
        o_ref[...] = (g_ref[0] + r_ref[...]).astype(o_ref.dtype)

    spec = pl.BlockSpec((1, blk, cols), lambda s, i, c_ref: (s, i, 0))
    return pl.pallas_call(
        body, name=name,
        grid_spec=pltpu.PrefetchScalarGridSpec(
            num_scalar_prefetch=1, grid=(SHARDS, rows // blk),
            in_specs=[pl.BlockSpec((1, 1, blk, cols), lambda s, i, c_ref: (s, c_ref[0], i, 0)), spec],
            out_specs=spec),
        out_shape=jax.ShapeDtypeStruct((SHARDS, rows, cols), MXU_DTYPE),
        compiler_params=_params(("parallel", "parallel")),
    )(lax.axis_index("c").astype(jnp.int32).reshape(1), grads, recv)


_HBM = pl.BlockSpec(memory_space=pltpu.HBM)
_SEM = pl.BlockSpec(memory_space=pltpu.SEMAPHORE)
_DATAFLOW = pltpu.SideEffectType.DATAFLOW_SIDE_EFFECTING


def _scatter_copies(p_refs, l_refs, send_sems, recv_sems):
    x, y, c, chips = _mesh_position()
    return [pltpu.make_async_remote_copy(src_ref=p_refs[a].at[2 * chip[0] + chip[1]], dst_ref=l_refs[a].at[j],
                                         send_sem=send_sems.at[3 * a + j], recv_sem=recv_sems.at[3 * a + j],
                                         device_id=(*chip, c), device_id_type=MESH)
            for a in range(len(p_refs)) for j, chip in enumerate(chips)]


def _exchange_copies(g_refs, l_refs, send_sems, recv_sems):
    x, y, c, _ = _mesh_position()
    return [pltpu.make_async_remote_copy(src_ref=g_refs[a].at[s, 1 - c], dst_ref=l_refs[a].at[s],
                                         send_sem=send_sems.at[SHARDS * a + s], recv_sem=recv_sems.at[SHARDS * a + s],
                                         device_id=(x, y, 1 - c), device_id_type=MESH)
            for a in range(len(g_refs)) for s in range(SHARDS)]


def _copies_start(name, copies, count, sources, land_shapes):
    n = len(sources)
    lands = [lax.empty(shape, src.dtype) for shape, src in zip(land_shapes, sources)]

    def body(*refs):
        s_refs, l_refs, send_sems, recv_sems, token = refs[:n], refs[n:2 * n], refs[2 * n], refs[2 * n + 1], refs[-1]
        for cp in copies(s_refs, l_refs, send_sems, recv_sems):
            cp.start()
        token[...] = jnp.zeros_like(token)

    operands = [pltpu.with_memory_space_constraint(a, pltpu.HBM) for a in (*sources, *lands)]
    return pl.pallas_call(
        body, name=name, in_specs=[_HBM] * (2 * n),
        out_shape=(pltpu.SemaphoreType.DMA((count,)), pltpu.SemaphoreType.DMA((count,)),
                   *[pltpu.HBM(a.shape, a.dtype) for a in operands], jax.ShapeDtypeStruct((8, 128), F32)),
        out_specs=(_SEM, _SEM, *[_HBM] * (2 * n), pl.BlockSpec(memory_space=pltpu.VMEM)),
        input_output_aliases={i: 2 + i for i in range(2 * n)},
        compiler_params=pltpu.CompilerParams(has_side_effects=_DATAFLOW),
    )(*operands)


def _copies_wait(name, copies, send_sems, recv_sems, passed, after):
    n = len(passed) // 2

    def body(*refs):
        s_refs, l_refs, send_s, recv_s = refs[:n], refs[n:2 * n], refs[2 * n], refs[2 * n + 1]
        for cp in copies(s_refs, l_refs, send_s, recv_s):
            cp.wait_send()
            cp.wait_recv()

    return pl.pallas_call(
        body, name=name, in_specs=[_HBM] * (2 * n) + [_SEM, _SEM, ANY],
        out_shape=[pltpu.HBM(a.shape, a.dtype) for a in passed], out_specs=[_HBM] * (2 * n),
        input_output_aliases={i: i for i in range(2 * n)},
        compiler_params=pltpu.CompilerParams(has_side_effects=_DATAFLOW),
    )(*passed, send_sems, recv_sems, after)


def _chip_sum(pairs, recv, blk, name):
    _, rows, cols = pairs.shape

    def body(pos_ref, p_ref, r_ref, o_ref):
        o_ref[0] = ((p_ref[0].astype(F32) + r_ref[0].astype(F32)) + r_ref[1].astype(F32)) + r_ref[2].astype(F32)

    pos = jnp.stack([2 * lax.axis_index("x") + lax.axis_index("y"), lax.axis_index("c")]).astype(jnp.int32)
    return pl.pallas_call(
        body, name=name,
        grid_spec=pltpu.PrefetchScalarGridSpec(
            num_scalar_prefetch=1, grid=(rows // blk,),
            in_specs=[pl.BlockSpec((1, blk, cols), lambda i, pos_ref: (pos_ref[0], i, 0)),
                      pl.BlockSpec((3, blk, cols), lambda i, pos_ref: (0, i, 0))],
            out_specs=pl.BlockSpec((1, blk, cols), lambda i, pos_ref: (pos_ref[1], i, 0))),
        out_shape=jax.ShapeDtypeStruct((2, rows, cols), F32),
        compiler_params=_params(("parallel",)),
    )(pos, pairs, recv)


def _share_total(totals):
    n = len(totals)

    def body(*refs):
        t_refs, out_refs, (send_sems, recv_sems) = refs[:n], refs[n:2 * n], refs[2 * n:]
        x, y, c, _ = _mesh_position()
        copies = [pltpu.make_async_remote_copy(src_ref=t_refs[a].at[c], dst_ref=out_refs[a].at[c], send_sem=send_sems.at[a],
                                               recv_sem=recv_sems.at[a], device_id=(x, y, 1 - c), device_id_type=MESH)
                  for a in range(n)]
        for cp in copies:
            cp.start()
        for a in range(n):
            other = out_refs[a].at[1 - c]
            pltpu.make_async_remote_copy(src_ref=other, dst_ref=other, send_sem=send_sems.at[a], recv_sem=recv_sems.at[a],
                                         device_id=(x, y, c), device_id_type=MESH).wait_recv()
        for cp in copies:
            cp.wait_send()

    return pl.pallas_call(
        body, name="share_total", in_specs=[ANY] * n, out_specs=[ANY] * n,
        out_shape=[jax.ShapeDtypeStruct(t.shape, F32) for t in totals],
        scratch_shapes=[pltpu.SemaphoreType.DMA((n,)), pltpu.SemaphoreType.DMA((n,))],
        input_output_aliases={a: a for a in range(n)},
    )(*totals)


def _allreduce_small(block):
    def body(b_ref, out_ref, gath, send_sems, recv_sems):
        x, y, c, _ = _mesh_position()
        me = 4 * x + 2 * y + c
        gath[me] = b_ref[...]
        copies = []
        for k in range(1, 8):
            peer = (x ^ (k >> 2), y ^ ((k >> 1) & 1), c ^ (k & 1))
            copies.append(pltpu.make_async_remote_copy(src_ref=b_ref, dst_ref=gath.at[me], send_sem=send_sems.at[k - 1],
                                                       recv_sem=recv_sems.at[k - 1], device_id=peer, device_id_type=MESH))
        for cp in copies:
            cp.start()
        for k in range(1, 8):
            src = 4 * (x ^ (k >> 2)) + 2 * (y ^ ((k >> 1) & 1)) + (c ^ (k & 1))
            pltpu.make_async_remote_copy(src_ref=b_ref, dst_ref=gath.at[src], send_sem=send_sems.at[k - 1],
                                         recv_sem=recv_sems.at[k - 1], device_id=(x, y, c), device_id_type=MESH).wait_recv()
        for cp in copies:
            cp.wait_send()
        acc = gath[0]
        for d in range(1, 8):
            acc = acc + gath[d]
        out_ref[...] = acc

    vm = pl.BlockSpec(memory_space=pltpu.VMEM)
    return pl.pallas_call(
        body, name="allreduce_small", in_specs=[vm], out_specs=vm,
        out_shape=jax.ShapeDtypeStruct((8, D_MODEL), F32),
        scratch_shapes=[pltpu.VMEM((8, 8, D_MODEL), F32), pltpu.SemaphoreType.DMA((7,)), pltpu.SemaphoreType.DMA((7,))],
    )(block)


def _adamw(w, g, m, v, name):
    rows, cols = w.shape
    tr = 128 if rows % 128 == 0 else rows

    def body(w_ref, g_ref, m_ref, v_ref, d_ref, nm_ref, nv_ref):
        gv = g_ref[...]
        nm = ADAM_B1 * m_ref[...] + (1.0 - ADAM_B1) * gv
        nv = ADAM_B2 * v_ref[...] + (1.0 - ADAM_B2) * (gv * gv)
        m_hat = nm / (1.0 - ADAM_B1 ** ADAM_STEP)
        v_hat = nv / (1.0 - ADAM_B2 ** ADAM_STEP)
        d_ref[...] = -ADAM_LR * (m_hat / (jnp.sqrt(v_hat) + ADAM_EPS) + ADAM_WD * w_ref[...])
        nm_ref[...] = nm
        nv_ref[...] = nv

    spec = pl.BlockSpec((tr, cols), lambda i: (i, 0))
    shape = jax.ShapeDtypeStruct((rows, cols), F32)
    return pl.pallas_call(
        body, name=name, grid=(rows // tr,), in_specs=[spec] * 4, out_specs=[spec] * 3, out_shape=[shape] * 3,
        compiler_params=_params(("parallel",)),
    )(w, g, m, v)


def kernel(x, norm_w, w_in, conv_w, a_log, dt_bias, gdn_norm_w, w_up_a, w_up_b, w_out, final_norm_w, loss_target, m_norm_w, m_w_in, m_conv_w, m_a_log, m_dt_bias, m_gdn_norm_w, m_w_up_a, m_w_up_b, m_w_out, m_final_norm_w, v_norm_w, v_w_in, v_conv_w, v_a_log, v_dt_bias, v_gdn_norm_w, v_w_up_a, v_w_up_b, v_w_out, v_final_norm_w):
    conv_hi = conv_w[0].astype(MXU_DTYPE)
    conv_lo = (conv_w[0] - conv_hi.astype(F32)).astype(MXU_DTYPE)
    big = w_in[0].astype(MXU_DTYPE).reshape(2, *BIG_HALF)
    slab = _pack_slab(w_up_a[0].astype(MXU_DTYPE), w_up_b[0].astype(MXU_DTYPE), w_out[0].astype(MXU_DTYPE), conv_hi, conv_lo)
    own_shard = 2 * lax.axis_index("x") + lax.axis_index("y")
    bigs, slabs = _gather_weights([big, slab])
    bigs = lax.dynamic_update_slice(bigs, big[None], (own_shard, 0, 0, 0)).reshape(SHARDS, D_MODEL, W_IN_SHARD)
    slabs = lax.dynamic_update_slice(slabs, slab[None], (own_shard, 0, 0, 0))
    parts = [_unpack_slab(slabs[s]) for s in range(SHARDS)]
    split = BA_END - (SHARDS - 1) * W_IN_SHARD
    wp = jnp.concatenate([bigs[s] for s in range(SHARDS - 1)]
                         + [bigs[-1][:, :split], jnp.zeros((D_MODEL, SEG_GA - BA_END), MXU_DTYPE), bigs[-1][:, split:]], axis=1)
    w_up_a_full = jnp.concatenate([p[0] for p in parts], axis=1)
    w_up_b_full = jnp.concatenate([p[1] for p in parts], axis=1)
    w_out_full = jnp.concatenate([p[2] for p in parts], axis=0)
    conv_full = jnp.concatenate([p[3].astype(F32) + p[4].astype(F32) for p in parts], axis=1)

    blocks, tags = (128, HALF_ROWS), ("w_in", "slab")

    def start_reduce(d_wp, d_w_up_a, d_w_up_b, d_w_out, d_conv_w):
        d_w_in = [d_wp[:, s * W_IN_SHARD:(s + 1) * W_IN_SHARD] for s in range(SHARDS - 1)]
        d_w_in.append(jnp.concatenate([d_wp[:, (SHARDS - 1) * W_IN_SHARD:BA_END], d_wp[:, SEG_GA:]], axis=1))
        zero_conv = jnp.zeros((GDN_CONV, CONV_SHARD), F32)
        grads = [jnp.stack(d_w_in).reshape(SHARDS, 2, *BIG_HALF),
                 jnp.stack([_pack_slab(d_w_up_a[:, s * 256:(s + 1) * 256], d_w_up_b[:, s * 256:(s + 1) * 256],
                                       d_w_out[s * 256:(s + 1) * 256], d_conv_w[:, s * CONV_SHARD:(s + 1) * CONV_SHARD],
                                       zero_conv) for s in range(SHARDS)])]
        *in_flight, token = _copies_start("exchange_start", _exchange_copies, 2 * SHARDS, grads,
                                          [(SHARDS, *gr.shape[2:]) for gr in grads])
        return in_flight, token

    def continue_reduce(in_flight, after):
        send_sems, recv_sems, *passed = in_flight
        arrived = _copies_wait("exchange_wait", _exchange_copies, send_sems, recv_sems, passed, after)
        grads, from_sibling = arrived[:2], arrived[2:]
        pairs = [_pair_sum(gr, fs, blk, f"pair_sum_{tag}") for gr, fs, blk, tag in zip(grads, from_sibling, blocks, tags)]
        *in_flight, token = _copies_start("scatter_start", _scatter_copies, 2 * 3, pairs, [(3, *p.shape[1:]) for p in pairs])
        return in_flight, token

    g = _local_step(x[0], loss_target[0], norm_w, wp, conv_full, a_log, dt_bias, gdn_norm_w,
                    w_up_a_full, w_up_b_full, w_out_full, final_norm_w[None], start_reduce, continue_reduce)

    send_sems, recv_sems, *passed = g["in_flight"]
    arrived = _copies_wait("scatter_wait", _scatter_copies, send_sems, recv_sems, passed, g["grad_x"])
    pairs, from_chips = arrived[:2], arrived[2:]
    total_big, total_slab = _share_total([_chip_sum(p, fc, blk, f"chip_sum_{tag}")
                                          for p, fc, blk, tag in zip(pairs, from_chips, blocks, tags)])
    g_w_in = total_big.reshape(D_MODEL, W_IN_SHARD)
    g_w_up_a, g_w_up_b, g_w_out, g_conv, _ = _unpack_slab(total_slab)

    row2 = jnp.concatenate([g["gdn_norm_w"], g["a_log"], g["dt_bias"], g["loss"][0:1, 0:1],
                            jnp.zeros((1, D_MODEL - HEAD_DIM - 2 * HEADS - 1), F32)], axis=1)
    small = _allreduce_small(jnp.concatenate([g["norm_w"], g["final_norm_w"], row2, jnp.zeros((5, D_MODEL), F32)], axis=0))
    g_norm, g_final = small[0:1], small[1]
    g_gnw, g_alog, g_dt = small[2:3, 0:HEAD_DIM], small[2:3, HEAD_DIM:HEAD_DIM + HEADS], small[2:3, HEAD_DIM + HEADS:HEAD_DIM + 2 * HEADS]
    loss = small[2, HEAD_DIM + 2 * HEADS]

    names = ["norm_w", "w_in", "conv_w", "a_log", "dt_bias", "gdn_norm_w", "w_up_a", "w_up_b", "w_out", "final_norm_w"]
    weights = dict(zip(names, (norm_w, w_in, conv_w, a_log, dt_bias, gdn_norm_w, w_up_a, w_up_b, w_out, final_norm_w)))
    ms = dict(zip(names, (m_norm_w, m_w_in, m_conv_w, m_a_log, m_dt_bias, m_gdn_norm_w, m_w_up_a, m_w_up_b, m_w_out, m_final_norm_w)))
    vs = dict(zip(names, (v_norm_w, v_w_in, v_conv_w, v_a_log, v_dt_bias, v_gdn_norm_w, v_w_up_a, v_w_up_b, v_w_out, v_final_norm_w)))
    grads2d = dict(norm_w=g_norm, w_in=g_w_in, conv_w=g_conv, a_log=g_alog, dt_bias=g_dt, gdn_norm_w=g_gnw,
                   w_up_a=g_w_up_a, w_up_b=g_w_up_b, w_out=g_w_out, final_norm_w=g_final[None])
    grad_out, delta, new_m, new_v = [], [], [], []
    for n in names:
        shape = weights[n].shape
        two_d = grads2d[n].shape
        d, nm, nv = _adamw(weights[n].reshape(two_d), grads2d[n], ms[n].reshape(two_d), vs[n].reshape(two_d), f"adamw_{n}")
        grad_out.append(grads2d[n].reshape(shape))
        delta.append(d.reshape(shape))
        new_m.append(nm.reshape(shape))
        new_v.append(nv.reshape(shape))
    return (loss, g["grad_x"][None], *grad_out, *delta, *new_m, *new_v)
```

```python
import functools

import jax
import jax.numpy as jnp
from jax import lax
from jax.experimental import pallas as pl
from jax.experimental.pallas import tpu as pltpu

F32 = jnp.float32
MXU_DTYPE = jnp.bfloat16
HIGHEST = lax.Precision.HIGHEST

D_MODEL = 1024
HEADS = 8
HEAD_DIM = 64
WIDTH = HEADS * HEAD_DIM
NORM_EPS = 1e-6
ROPE_THETA = 10000.0
ATT_BLOCK = 128
DILATIONS = (1, 4, 16)
GDN_CHUNK = 64
GDN_CONV = 4
IN_WIDTH = 9232
SEG_A, SEG_ZA, SEG_B, SEG_ZB, SEG_BA, SEG_GA, SEG_GB, PACKED_WIDTH = 0, 4608, 5120, 6656, 7168, 7680, 8704, 9728
BA_END = 7184
VMEM_LIMIT = 56 * 1024 * 1024

ADAM_LR, ADAM_B1, ADAM_B2, ADAM_EPS, ADAM_WD, ADAM_STEP = 0.001, 0.9, 0.999, 1e-08, 0.01, 10

_NN = (((1,), (0,)), ((), ()))
_NT = (((1,), (1,)), ((), ()))
_TN = (((0,), (0,)), ((), ()))


def _params(sem):
    return pltpu.CompilerParams(dimension_semantics=sem, vmem_limit_bytes=VMEM_LIMIT)


def _mxu(a, b, dims):
    return lax.dot_general(a.astype(MXU_DTYPE), b.astype(MXU_DTYPE), dims, preferred_element_type=F32)


def _sigmoid(x):
    return 1.0 / (1.0 + jnp.exp(-x))


def _softplus(x):
    return jnp.maximum(x, 0.0) + jnp.log(1.0 + jnp.exp(-jnp.abs(x)))


def _iota(shape, axis):
    return lax.broadcasted_iota(jnp.int32, shape, axis)


def _matmul(a, b, mode, name, out_dtype=F32, tm=512, tn=512, tk=512):
    if mode == "nn":
        (m, k), (k2, n) = a.shape, b.shape
    elif mode == "nt":
        (m, k), (n, k2) = a.shape, b.shape
    else:
        (k, m), (k2, n) = a.shape, b.shape
    assert k == k2
    tm, tn, tk = min(tm, m), min(tn, n), min(tk, k)
    assert m % tm == 0 and n % tn == 0 and k % tk == 0
    nk = k // tk
    dims = {"nn": _NN, "nt": _NT, "tn": _TN}[mode]

    assert out_dtype == F32

    def body(a_ref, b_ref, o_ref):
        kk = pl.program_id(2)
        part = _mxu(a_ref[...], b_ref[...], dims)

        @pl.when(kk == 0)
        def _():
            o_ref[...] = part

        @pl.when(kk > 0)
        def _():
            o_ref[...] += part

    a_spec = pl.BlockSpec((tk, tm), lambda i, j, kk: (kk, i)) if mode == "tn" else pl.BlockSpec((tm, tk), lambda i, j, kk: (i, kk))
    b_spec = pl.BlockSpec((tn, tk), lambda i, j, kk: (j, kk)) if mode == "nt" else pl.BlockSpec((tk, tn), lambda i, j, kk: (kk, j))
    return pl.pallas_call(
        body, name=name, grid=(m // tm, n // tn, nk), in_specs=[a_spec, b_spec],
        out_specs=pl.BlockSpec((tm, tn), lambda i, j, kk: (i, j)),
        out_shape=jax.ShapeDtypeStruct((m, n), out_dtype),
        compiler_params=_params(("parallel", "parallel", "arbitrary")),
    )(a, b)


def _norm_proj(x, norm_w, wp, tm=1024, tn=PACKED_WIDTH // 4):
    t = x.shape[0]
    tm = min(tm, t)

    def body(x_ref, nw_ref, w_ref, proj_ref, ht_ref, h_scr):
        @pl.when(pl.program_id(1) == 0)
        def _():
            xf = x_ref[...]
            r = lax.rsqrt(jnp.mean(xf * xf, axis=-1, keepdims=True) + NORM_EPS)
            h = xf * r * nw_ref[...]
            h_scr[...] = h.astype(h_scr.dtype)
            ht_ref[...] = h.T.astype(ht_ref.dtype)

        proj_ref[...] = jnp.dot(h_scr[...], w_ref[...], preferred_element_type=F32)

    return pl.pallas_call(
        body, name="norm_proj", grid=(t // tm, PACKED_WIDTH // tn),
        in_specs=[pl.BlockSpec((tm, D_MODEL), lambda i, j: (i, 0)),
                  pl.BlockSpec((1, D_MODEL), lambda i, j: (0, 0)),
                  pl.BlockSpec((D_MODEL, tn), lambda i, j: (0, j))],
        out_specs=[pl.BlockSpec((tm, tn), lambda i, j: (i, j)),
                   pl.BlockSpec((D_MODEL, tm), lambda i, j: (0, i))],
        out_shape=[jax.ShapeDtypeStruct((t, PACKED_WIDTH), F32), jax.ShapeDtypeStruct((D_MODEL, t), MXU_DTYPE)],
        scratch_shapes=[pltpu.VMEM((tm, D_MODEL), MXU_DTYPE)],
        compiler_params=_params(("parallel", "arbitrary")),
    )(x, norm_w, wp)


def _rope_tables(t):
    lane = jnp.arange(128)
    inv_freq = ROPE_THETA ** (-jnp.arange(0, HEAD_DIM, 2, dtype=F32) / HEAD_DIM)
    freq = jnp.concatenate([inv_freq] * 4)
    coarse = (jnp.arange(t // 128, dtype=F32) * 128.0)[:, None] * freq[None, :]
    fine = jnp.arange(128, dtype=F32)[:, None] * freq[None, :]
    ca, sa, cb, sb = jnp.cos(coarse)[:, None, :], jnp.sin(coarse)[:, None, :], jnp.cos(fine)[None], jnp.sin(fine)[None]
    cos = (ca * cb - sa * sb).reshape(t, 128)
    sin = (sa * cb + ca * sb).reshape(t, 128)
    first_half = (lane % HEAD_DIM) < HEAD_DIM // 2
    return cos, jnp.where(first_half, -sin, 0.0), jnp.where(first_half, 0.0, sin)


def _rope_cols(x, cos, sin_lo, sin_hi, sign):
    outs = []
    for c in range(x.shape[1] // 128):
        xc = x[:, c * 128:(c + 1) * 128]
        rot = pltpu.roll(xc, 96, 1) * sin_lo + pltpu.roll(xc, 32, 1) * sin_hi
        outs.append(xc * cos + sign * rot)
    return jnp.concatenate(outs, axis=1)


def _rope_block(x, cos, sin_lo, sin_hi, sign):
    return jnp.concatenate([_rope_cols(x[:, :2 * WIDTH], cos, sin_lo, sin_hi, sign), x[:, 2 * WIDTH:]], axis=1)


def _tile_scratch(tm, cols):
    return pltpu.VMEM((cols // 128, tm, 128), F32)


def _store_tile(scr, y):
    for c in range(scr.shape[0]):
        scr[c] = y[:, c * 128:(c + 1) * 128]


def _load_tile(scr):
    return jnp.concatenate([scr[c] for c in range(scr.shape[0])], axis=1)


def _to_strided_view(scr, o_ref, d, width=None, col0=0):
    n, tm, _ = scr.shape
    width = n * 128 if width is None else width
    for r in range(d):
        for c in range(n):
            at = r * width + col0 + c * 128
            o_ref[:, at:at + 128] = scr[c, pl.ds(r, tm // d, stride=d), :].astype(o_ref.dtype)


def _from_strided_view(i_ref, scr, d):
    n, tm, _ = scr.shape
    for r in range(d):
        for c in range(n):
            scr[c, pl.ds(r, tm // d, stride=d), :] = i_ref[:, (r * n + c) * 128:(r * n + c + 1) * 128].astype(F32)


def _strided_spec(tm, d, cols):
    return pl.BlockSpec((tm // d, d * cols), lambda i: (i, 0))


def _rope_fwd(proj, tables, tm=512):
    t = proj.shape[0]
    cols = 3 * WIDTH

    def body(x_ref, c_ref, sl_ref, sh_ref, o0, o1, o2, scr):
        for g, (d, o_ref) in enumerate(zip(DILATIONS, (o0, o1, o2))):
            y = _rope_block(x_ref[:, g * cols:(g + 1) * cols], c_ref[...], sl_ref[...], sh_ref[...], 1.0)
            if d == 1:
                o_ref[...] = y.astype(o_ref.dtype)
            else:
                _store_tile(scr, y)
                _to_strided_view(scr, o_ref, d)

    tab = pl.BlockSpec((tm, 128), lambda i: (i, 0))
    return pl.pallas_call(
        body, name="rope_fwd", grid=(t // tm,),
        in_specs=[pl.BlockSpec((tm, 3 * cols), lambda i: (i, 0)), tab, tab, tab],
        out_specs=[_strided_spec(tm, d, cols) for d in DILATIONS],
        out_shape=[jax.ShapeDtypeStruct((t // d, d * cols), MXU_DTYPE) for d in DILATIONS],
        scratch_shapes=[_tile_scratch(tm, cols)],
        compiler_params=_params(("parallel",)),
    )(proj, *tables)


def _rope_bwd(dproj, dqs, dkvs, tables, tm=512):
    t = dproj.shape[0]
    cols = 3 * WIDTH

    def body(dp_ref, q0, q1, q2, kv0, kv1, kv2, c_ref, sl_ref, sh_ref, o_ref, scr_q, scr_kv):
        for g, (d, q_ref, kv_ref) in enumerate(zip(DILATIONS, (q0, q1, q2), (kv0, kv1, kv2))):
            if d == 1:
                x = jnp.concatenate([q_ref[...], kv_ref[...]], axis=1)
            else:
                _from_strided_view(q_ref, scr_q, d)
                _from_strided_view(kv_ref, scr_kv, d)
                x = jnp.concatenate([_load_tile(scr_q), _load_tile(scr_kv)], axis=1)
            y = _rope_block(x, c_ref[...], sl_ref[...], sh_ref[...], -1.0)
            o_ref[:, g * cols:(g + 1) * cols] = y.astype(o_ref.dtype)

    tab = pl.BlockSpec((tm, 128), lambda i: (i, 0))
    return pl.pallas_call(
        body, name="rope_bwd", grid=(t // tm,),
        in_specs=[pl.BlockSpec(memory_space=pl.ANY)] + [_strided_spec(tm, d, WIDTH) for d in DILATIONS]
        + [_strided_spec(tm, d, 2 * WIDTH) for d in DILATIONS] + [tab, tab, tab],
        out_specs=pl.BlockSpec((tm, 3 * cols), lambda i: (i, 0)),
        out_shape=jax.ShapeDtypeStruct((t, PACKED_WIDTH), MXU_DTYPE),
        scratch_shapes=[_tile_scratch(tm, WIDTH), _tile_scratch(tm, 2 * WIDTH)],
        input_output_aliases={0: 0},
        compiler_params=_params(("parallel",)),
    )(dproj, *dqs, *dkvs, *tables)


def _att_masks():
    qi = _iota((ATT_BLOCK, ATT_BLOCK), 0)
    kj = _iota((ATT_BLOCK, ATT_BLOCK), 1)
    return kj <= qi, kj >= qi


def _att_fwd(qkv, d, name):
    rows = qkv.shape[0]
    nb = rows // ATT_BLOCK
    scale = HEAD_DIM ** -0.5

    def body(q_ref, kc_ref, kp_ref, vc_ref, vp_ref, o_ref, lse_ref):
        has_prev = pl.program_id(1) > 0
        m_cur, m_prev = _att_masks()
        m_prev = m_prev & has_prev
        hs = range(HEADS)
        sls = [slice(h * HEAD_DIM, (h + 1) * HEAD_DIM) for h in hs]
        qs = [q_ref[:, sl] for sl in sls]
        s_c = [jnp.where(m_cur, _mxu(qs[h], kc_ref[:, sls[h]], _NT) * scale, -jnp.inf) for h in hs]
        s_p = [jnp.where(m_prev, _mxu(qs[h], kp_ref[:, sls[h]], _NT) * scale, -jnp.inf) for h in hs]
        m = [jnp.max(jnp.maximum(s_c[h], s_p[h]), axis=1, keepdims=True) for h in hs]
        p_c = [jnp.exp(s_c[h] - m[h]) for h in hs]
        p_p = [jnp.exp(s_p[h] - m[h]) for h in hs]
        den = [jnp.sum(p_c[h] + p_p[h], axis=1, keepdims=True) for h in hs]
        o = [_mxu(p_c[h], vc_ref[:, sls[h]], _NN) + _mxu(p_p[h], vp_ref[:, sls[h]], _NN) for h in hs]
        lane = _iota((1, 128), 1)
        lse = jnp.zeros((ATT_BLOCK, 128), F32)
        for h in hs:
            o_ref[:, sls[h]] = o[h] / den[h]
            lse = jnp.where(lane == h, m[h] + jnp.log(den[h]), lse)
        lse_ref[...] = lse

    def cur(c):
        return pl.BlockSpec((ATT_BLOCK, WIDTH), lambda r, i: (i, 3 * r + c))

    def prev(c):
        return pl.BlockSpec((ATT_BLOCK, WIDTH), lambda r, i: (jnp.maximum(i - 1, 0), 3 * r + c))

    return pl.pallas_call(
        body, name=name, grid=(d, nb), in_specs=[cur(0), cur(1), prev(1), cur(2), prev(2)],
        out_specs=[pl.BlockSpec((ATT_BLOCK, WIDTH), lambda r, i: (i, r)), pl.BlockSpec((ATT_BLOCK, 128), lambda r, i: (i, r))],
        out_shape=[jax.ShapeDtypeStruct((rows, d * WIDTH), F32), jax.ShapeDtypeStruct((rows, d * 128), F32)],
        compiler_params=_params(("parallel", "arbitrary")),
    )(qkv, qkv, qkv, qkv, qkv)


def _att_bwd(qkv, do, lse, delta, d, name):
    rows = qkv.shape[0]
    nb = rows // ATT_BLOCK
    scale = HEAD_DIM ** -0.5

    def body(q_ref, kc_ref, kp_ref, vc_ref, vp_ref, do_ref, l_ref, dl_ref, dq_ref, dkv_ref, own):
        i = pl.program_id(1)
        m_cur, m_prev = _att_masks()
        m_prev = m_prev & (i > 0)

        hs = range(HEADS)
        sls = [slice(h * HEAD_DIM, (h + 1) * HEAD_DIM) for h in hs]
        col = [slice(h, h + 1) for h in hs]

        def probs(k_r, mask):
            s = [_mxu(q_ref[:, sls[h]], k_r[:, sls[h]], _NT) for h in hs]
            return [jnp.where(mask, jnp.exp(s[h] * scale - l_ref[:, col[h]]), 0.0) for h in hs]

        def dscores(p, v_r):
            dp = [_mxu(do_ref[:, sls[h]], v_r[:, sls[h]], _NT) for h in hs]
            return [(p[h] * (dp[h] - dl_ref[:, col[h]])).astype(MXU_DTYPE) for h in hs]

        @pl.when(i == 0)
        def _():
            own[...] = jnp.zeros_like(own)

        @pl.when(i < nb)
        def _():
            p_c = probs(kc_ref, m_cur)
            ds_c = dscores(p_c, vc_ref)
            p_p = probs(kp_ref, m_prev)
            ds_p = dscores(p_p, vp_ref)
            dq = [_mxu(ds_c[h], kc_ref[:, sls[h]], _NN) + _mxu(ds_p[h], kp_ref[:, sls[h]], _NN) for h in hs]
            dk_p = [_mxu(ds_p[h], q_ref[:, sls[h]], _TN) for h in hs]
            dv_p = [_mxu(p_p[h], do_ref[:, sls[h]], _TN) for h in hs]
            dk_c = [_mxu(ds_c[h], q_ref[:, sls[h]], _TN) for h in hs]
            dv_c = [_mxu(p_c[h], do_ref[:, sls[h]], _TN) for h in hs]
            for h in hs:
                vs = slice(WIDTH + h * HEAD_DIM, WIDTH + (h + 1) * HEAD_DIM)
                dq_ref[:, sls[h]] = dq[h] * scale
                dkv_ref[:, sls[h]] = own[:, sls[h]] + dk_p[h] * scale
                dkv_ref[:, vs] = own[:, vs] + dv_p[h]
                own[:, sls[h]] = dk_c[h] * scale
                own[:, vs] = dv_c[h]

        @pl.when(i == nb)
        def _():
            dkv_ref[...] = own[...]

    def qkv_spec(c, shift):
        return pl.BlockSpec((ATT_BLOCK, WIDTH), lambda r, i: (jnp.clip(i + shift, 0, nb - 1), 3 * r + c))

    tok = pl.BlockSpec((ATT_BLOCK, WIDTH), lambda r, i: (jnp.minimum(i, nb - 1), r))
    per_head = pl.BlockSpec((ATT_BLOCK, 128), lambda r, i: (jnp.minimum(i, nb - 1), r))
    return pl.pallas_call(
        body, name=name, grid=(d, nb + 1),
        in_specs=[qkv_spec(0, 0), qkv_spec(1, 0), qkv_spec(1, -1), qkv_spec(2, 0), qkv_spec(2, -1), tok, per_head, per_head],
        out_specs=[tok, pl.BlockSpec((ATT_BLOCK, 2 * WIDTH), lambda r, i: (jnp.maximum(i - 1, 0), r))],
        out_shape=[jax.ShapeDtypeStruct((rows, d * WIDTH), F32), jax.ShapeDtypeStruct((rows, d * 2 * WIDTH), F32)],
        scratch_shapes=[pltpu.VMEM((ATT_BLOCK, 2 * WIDTH), F32)],
        compiler_params=_params(("parallel", "arbitrary")),
    )(qkv, qkv, qkv, qkv, qkv, do, lse, delta)


def _att_merge(os_, lses, proj, tm=512):
    t = proj.shape[0]

    def body(o0, o1, o2, l0, l1, l2, z_ref, oz_ref, o_ref, t0, t1, t2, s_o1, s_o2, s_l1, s_l2, s_t):
        _from_strided_view(o1, s_o1, DILATIONS[1])
        _from_strided_view(o2, s_o2, DILATIONS[2])
        _from_strided_view(l1, s_l1, DILATIONS[1])
        _from_strided_view(l2, s_l2, DILATIONS[2])
        a, b, c = l0[...], s_l1[0], s_l2[0]
        m = jnp.maximum(jnp.maximum(a, b), c)
        wa, wb, wc = jnp.exp(a - m), jnp.exp(b - m), jnp.exp(c - m)
        den = wa + wb + wc
        total = m + jnp.log(den)
        t0[...] = total
        s_t[0] = total
        _to_strided_view(s_t, t1, DILATIONS[1])
        _to_strided_view(s_t, t2, DILATIONS[2])
        spread = jnp.where(_iota((128, WIDTH), 1) // HEAD_DIM == _iota((128, WIDTH), 0), 1.0, 0.0).astype(MXU_DTYPE)
        ra, rb, rc = (_pair_sum_lanes(w / den, spread) for w in (wa, wb, wc))
        o = ra * o0[...] + rb * _load_tile(s_o1) + rc * _load_tile(s_o2)
        z = z_ref[...]
        o_ref[...] = o
        oz_ref[...] = (o * z * _sigmoid(z)).astype(oz_ref.dtype)

    tok = pl.BlockSpec((tm, WIDTH), lambda i: (i, 0))
    views = [_strided_spec(tm, d, WIDTH) for d in DILATIONS]
    per_head = [_strided_spec(tm, d, 128) for d in DILATIONS]
    return pl.pallas_call(
        body, name="att_merge", grid=(t // tm,),
        in_specs=views + per_head + [pl.BlockSpec((tm, WIDTH), lambda i: (i, SEG_ZA // WIDTH))],
        out_specs=[tok, tok] + per_head,
        out_shape=[jax.ShapeDtypeStruct((t, WIDTH), MXU_DTYPE), jax.ShapeDtypeStruct((t, WIDTH), F32)]
        + [jax.ShapeDtypeStruct((t // d, d * 128), F32) for d in DILATIONS],
        scratch_shapes=[_tile_scratch(tm, WIDTH)] * 2 + [_tile_scratch(tm, 128)] * 3,
        compiler_params=_params(("parallel",)),
    )(*os_, *lses, proj)


def _att_merge_bwd(dproj, d_oz, o, proj, tm=512):
    t = proj.shape[0]

    def body(dp_ref, doz_ref, o_ref, z_ref, dz_ref, do0, do1, do2, dl0, dl1, dl2, s_do, s_dl):
        z, ov, g = z_ref[...], o_ref[...], doz_ref[...]
        sg = _sigmoid(z)
        do = g * z * sg
        dz_ref[...] = (g * ov * sg * (1.0 + z * (1.0 - sg))).astype(dz_ref.dtype)
        do0[...] = do.astype(do0.dtype)
        _store_tile(s_do, do)
        prod = do * ov
        delta = jnp.zeros((tm, 128), F32)
        for p in range(HEADS // 2):
            delta += _pair_sum_lanes(prod[:, p * 128:(p + 1) * 128], _pair_to_lanes(p, 0))
        dl0[...] = delta
        s_dl[0] = delta
        for d, do_v, dl_v in ((DILATIONS[1], do1, dl1), (DILATIONS[2], do2, dl2)):
            _to_strided_view(s_do, do_v, d)
            _to_strided_view(s_dl, dl_v, d)

    tok = pl.BlockSpec((tm, WIDTH), lambda i: (i, 0))
    seg = pl.BlockSpec((tm, WIDTH), lambda i: (i, SEG_ZA // WIDTH))
    views = [_strided_spec(tm, d, WIDTH) for d in DILATIONS]
    per_head = [_strided_spec(tm, d, 128) for d in DILATIONS]
    return pl.pallas_call(
        body, name="att_merge_bwd", grid=(t // tm,),
        in_specs=[pl.BlockSpec(memory_space=pl.ANY), tok, tok, seg],
        out_specs=[seg] + views + per_head,
        out_shape=[jax.ShapeDtypeStruct((t, PACKED_WIDTH), MXU_DTYPE)]
        + [jax.ShapeDtypeStruct((t // d, d * WIDTH), MXU_DTYPE) for d in DILATIONS]
        + [jax.ShapeDtypeStruct((t // d, d * 128), F32) for d in DILATIONS],
        scratch_shapes=[_tile_scratch(tm, WIDTH), _tile_scratch(tm, 128)],
        input_output_aliases={0: 0},
        compiler_params=_params(("parallel",)),
    )(dproj, d_oz, o, proj)


def _shift_down(x, halo, s):
    if s == 0:
        return x
    xs = pltpu.roll(x, s, 0)
    head = jnp.where(_iota((8, x.shape[1]), 0) < s, pltpu.roll(halo, s, 0), xs[0:8])
    return jnp.concatenate([head, xs[8:]], axis=0)


def _shift_up(x, nxt, s):
    if s == 0:
        return x
    n = x.shape[0]
    xs = pltpu.roll(x, n - s, 0)
    tail = jnp.where(_iota((8, x.shape[1]), 0) >= 8 - s, pltpu.roll(nxt, 8 - s, 0), xs[n - 8:])
    return jnp.concatenate([xs[:n - 8], tail], axis=0)


def _conv_fwd(proj, conv_w, tm=1024):
    t = proj.shape[0]
    cb = SEG_B // WIDTH

    def body(x_ref, halo_ref, w_ref, c_ref):
        halo = jnp.where(pl.program_id(0) > 0, halo_ref[...], 0.0)
        x = x_ref[...]
        w = w_ref[...]
        acc = jnp.zeros((tm, WIDTH), F32)
        for j in range(GDN_CONV):
            acc += _shift_down(x, halo, GDN_CONV - 1 - j) * w[j:j + 1, :]
        c_ref[...] = acc

    return pl.pallas_call(
        body, name="conv_fwd", grid=(t // tm, 3),
        in_specs=[pl.BlockSpec((tm, WIDTH), lambda i, c: (i, cb + c)),
                  pl.BlockSpec((8, WIDTH), lambda i, c: (jnp.maximum(i * (tm // 8) - 1, 0), cb + c)),
                  pl.BlockSpec((GDN_CONV, WIDTH), lambda i, c: (0, c))],
        out_specs=pl.BlockSpec((tm, WIDTH), lambda i, c: (i, c)),
        out_shape=jax.ShapeDtypeStruct((t, 3 * WIDTH), F32),
        compiler_params=_params(("parallel", "parallel")),
    )(proj, proj, conv_w)


def _conv_bwd(dproj, dc, proj, conv_w, tm=1024):
    t = proj.shape[0]
    cb = SEG_B // WIDTH
    nt = t // tm

    def body(dp_ref, dc_ref, dcn_ref, x_ref, halo_ref, w_ref, dx_ref, dw_ref):
        i = pl.program_id(1)
        w = w_ref[...]
        dcn = jnp.where(i < nt - 1, dcn_ref[...], 0.0)
        dcv = dc_ref[...]
        acc = jnp.zeros((tm, WIDTH), F32)
        for j in range(GDN_CONV):
            acc += _shift_up(dcv, dcn, GDN_CONV - 1 - j) * w[j:j + 1, :]
        dx_ref[...] = acc.astype(dx_ref.dtype)
        halo = jnp.where(i > 0, halo_ref[...], 0.0)
        x = x_ref[...]
        row8 = _iota((8, WIDTH), 0)
        part = jnp.zeros((8, WIDTH), F32)
        for j in range(GDN_CONV):
            s = jnp.sum(dcv * _shift_down(x, halo, GDN_CONV - 1 - j), axis=0, keepdims=True)
            part += jnp.where(row8 == j, s, 0.0)

        @pl.when(i == 0)
        def _():
            dw_ref[...] = jnp.zeros_like(dw_ref)

        dw_ref[...] += part

    return pl.pallas_call(
        body, name="conv_bwd", grid=(3, nt),
        in_specs=[pl.BlockSpec(memory_space=pl.ANY),
                  pl.BlockSpec((tm, WIDTH), lambda c, i: (i, c)),
                  pl.BlockSpec((8, WIDTH), lambda c, i: (jnp.minimum((i + 1) * (tm // 8), t // 8 - 1), c)),
                  pl.BlockSpec((tm, WIDTH), lambda c, i: (i, cb + c)),
                  pl.BlockSpec((8, WIDTH), lambda c, i: (jnp.maximum(i * (tm // 8) - 1, 0), cb + c)),
                  pl.BlockSpec((GDN_CONV, WIDTH), lambda c, i: (0, c))],
        out_specs=[pl.BlockSpec((tm, WIDTH), lambda c, i: (i, cb + c)),
                   pl.BlockSpec((8, WIDTH), lambda c, i: (0, c))],
        out_shape=[jax.ShapeDtypeStruct((t, PACKED_WIDTH), MXU_DTYPE), jax.ShapeDtypeStruct((8, 3 * WIDTH), F32)],
        input_output_aliases={0: 0},
        compiler_params=_params(("parallel", "arbitrary")),
    )(dproj, dc, dc, proj, proj, conv_w)


def _chunk_matrices(tm):
    r, c = _iota((tm, tm), 0), _iota((tm, tm), 1)
    same = (r // GDN_CHUNK) == (c // GDN_CHUNK)
    return jnp.where(same & (c <= r), 1.0, 0.0), jnp.where(same, 1.0, 0.0)


def _gdn_gates(ba, a_log, dt_bias):
    al = ba + dt_bias
    return _sigmoid(ba), -jnp.exp(a_log) * _softplus(al), _sigmoid(al)


SCAN_HEADS = (0, 2, 4, 6, 1, 3, 5, 7)


def _head_lane_eye():
    return jnp.where(_iota((HEADS, 128), 1) == _iota((HEADS, 128), 0) + HEADS, 1.0, 0.0)


def _pair_ones():
    return jnp.where(_iota((128, 128), 0) // HEAD_DIM == _iota((128, 128), 1) // HEAD_DIM, 1.0, 0.0).astype(MXU_DTYPE)


def _pair_to_lanes(p, base):
    return jnp.where(_iota((128, 128), 1) == base + 2 * p + _iota((128, 128), 0) // HEAD_DIM, 1.0, 0.0).astype(MXU_DTYPE)


def _pair_sum_lanes(x, sel):
    hi, lo = _split(x)
    return jnp.dot(hi, sel, preferred_element_type=F32) + jnp.dot(lo, sel, preferred_element_type=F32)


def _gdn_prep(conv, proj, a_log, dt_bias, tm=256):
    t = proj.shape[0]
    nc = tm // GDN_CHUNK

    def body(c_ref, ba_ref, al_ref, dt_ref, q_ref, k_ref, v_ref, b_ref, g_ref, gl_ref, grow_ref):
        beta, g, _ = _gdn_gates(ba_ref[:, 0:128], al_ref[...], dt_ref[...])
        lmat, cmat = _chunk_matrices(tm)
        gc = jnp.dot(lmat, g, precision=HIGHEST, preferred_element_type=F32)
        gl = jnp.dot(cmat, g, precision=HIGHEST, preferred_element_type=F32)
        grow = lax.dot_general(_head_lane_eye(), gc, _NT, precision=HIGHEST, preferred_element_type=F32)
        ones = _pair_ones()
        first = _iota((1, 128), 1) < HEAD_DIM

        def spread(x, base, p):
            return jnp.where(first, x[:, base + 2 * p:base + 2 * p + 1], x[:, base + 2 * p + 1:base + 2 * p + 2])

        for p in range(HEADS // 2):
            for seg, ref, scale in ((0, q_ref, HEAD_DIM ** -0.5), (1, k_ref, 1.0), (2, v_ref, None)):
                c = c_ref[:, seg * WIDTH + p * 128:seg * WIDTH + (p + 1) * 128]
                a = c * _sigmoid(c)
                if scale is not None:
                    a = a * (lax.rsqrt(_pair_sum_lanes(a * a, ones) + NORM_EPS) * scale)
                ref[p] = a
            b_ref[p] = spread(beta, 0, p)
            g_ref[p] = spread(gc, HEADS, p)
            gl_ref[p] = spread(gl, HEADS, p)
        for pos, h in enumerate(SCAN_HEADS):
            for cc in range(nc):
                grow_ref[pos, cc] = grow[h:h + 1, cc * GDN_CHUNK:(cc + 1) * GDN_CHUNK]

    hm = pl.BlockSpec((HEADS // 2, tm, 128), lambda i: (0, i, 0))
    small = pl.BlockSpec((1, 128), lambda i: (0, 0))
    hm_shape = jax.ShapeDtypeStruct((HEADS // 2, t, 128), F32)
    return pl.pallas_call(
        body, name="gdn_prep", grid=(t // tm,),
        in_specs=[pl.BlockSpec((tm, 3 * WIDTH), lambda i: (i, 0)),
                  pl.BlockSpec((tm, WIDTH), lambda i: (i, SEG_BA // WIDTH)), small, small],
        out_specs=[hm] * 6 + [pl.BlockSpec((HEADS, nc, 1, GDN_CHUNK), lambda i: (0, i, 0, 0))],
        out_shape=[hm_shape] * 6 + [jax.ShapeDtypeStruct((HEADS, t // GDN_CHUNK, 1, GDN_CHUNK), F32)],
        compiler_params=_params(("parallel",)),
    )(conv, proj, a_log, dt_bias)


def _gdn_prep_bwd(dproj, conv, proj, a_log, dt_bias, dq, dk, dv, db, dg, dgl, dgrow, tm=256):
    t = proj.shape[0]
    nc = tm // GDN_CHUNK

    def body(dp_ref, c_ref, ba_ref, al_ref, dt_ref, dq_ref, dk_ref, dv_ref, db_ref, dg_ref, dgl_ref, dgrow_ref,
             dba_ref, dc_ref, small_ref, row_scr):
        beta, g, sig_al = _gdn_gates(ba_ref[:, 0:128], al_ref[...], dt_ref[...])
        d_beta = jnp.zeros((tm, 128), F32)
        d_gc = jnp.zeros((tm, 128), F32)
        d_gl = jnp.zeros((tm, 128), F32)
        ones = _pair_ones()
        for pos, h in enumerate(SCAN_HEADS):
            for cc in range(nc):
                row_scr[h:h + 1, cc * GDN_CHUNK:(cc + 1) * GDN_CHUNK] = dgrow_ref[pos, cc]
        for p in range(HEADS // 2):
            d_beta += _pair_sum_lanes(db_ref[p], _pair_to_lanes(p, 0))
            d_gc += _pair_sum_lanes(dg_ref[p], _pair_to_lanes(p, HEADS))
            d_gl += _pair_sum_lanes(dgl_ref[p], _pair_to_lanes(p, HEADS))
            for seg, ref, scale in ((0, dq_ref, HEAD_DIM ** -0.5), (1, dk_ref, 1.0), (2, dv_ref, None)):
                cols = slice(seg * WIDTH + p * 128, seg * WIDTH + (p + 1) * 128)
                c = c_ref[:, cols]
                sg = _sigmoid(c)
                da = ref[p]
                if scale is not None:
                    a = c * sg
                    r = lax.rsqrt(_pair_sum_lanes(a * a, ones) + NORM_EPS)
                    da = da * scale
                    da = r * da - a * (r * r * r) * _pair_sum_lanes(da * a, ones)
                dc_ref[:, cols] = da * (sg * (1.0 + c * (1.0 - sg)))
        d_gc += lax.dot_general(row_scr[...], _head_lane_eye(), _TN, precision=HIGHEST, preferred_element_type=F32)
        lmat, cmat = _chunk_matrices(tm)
        d_g = (lax.dot_general(lmat, d_gc, _TN, precision=HIGHEST, preferred_element_type=F32)
               + lax.dot_general(cmat, d_gl, _TN, precision=HIGHEST, preferred_element_type=F32))
        d_al = d_g * (-jnp.exp(al_ref[...])) * sig_al
        d_bl = d_beta * beta * (1.0 - beta)
        dba_ref[...] = jnp.concatenate([d_bl + d_al, jnp.zeros((tm, WIDTH - 128), F32)], axis=1).astype(dba_ref.dtype)
        row8 = _iota((8, 128), 0)
        part = (jnp.where(row8 == 0, jnp.sum(d_g * g, axis=0, keepdims=True), 0.0)
                + jnp.where(row8 == 1, jnp.sum(d_al, axis=0, keepdims=True), 0.0))

        @pl.when(pl.program_id(0) == 0)
        def _():
            small_ref[...] = jnp.zeros_like(small_ref)

        small_ref[...] += part

    hm = pl.BlockSpec((HEADS // 2, tm, 128), lambda i: (0, i, 0))
    small = pl.BlockSpec((1, 128), lambda i: (0, 0))
    seg = pl.BlockSpec((tm, WIDTH), lambda i: (i, SEG_BA // WIDTH))
    return pl.pallas_call(
        body, name="gdn_prep_bwd", grid=(t // tm,),
        in_specs=[pl.BlockSpec(memory_space=pl.ANY), pl.BlockSpec((tm, 3 * WIDTH), lambda i: (i, 0)), seg, small, small]
        + [hm] * 6 + [pl.BlockSpec((HEADS, nc, 1, GDN_CHUNK), lambda i: (0, i, 0, 0))],
        out_specs=[seg, pl.BlockSpec((tm, 3 * WIDTH), lambda i: (i, 0)), pl.BlockSpec((8, 128), lambda i: (0, 0))],
        out_shape=[jax.ShapeDtypeStruct((t, PACKED_WIDTH), MXU_DTYPE), jax.ShapeDtypeStruct((t, 3 * WIDTH), F32),
                   jax.ShapeDtypeStruct((8, 128), F32)],
        scratch_shapes=[pltpu.VMEM((HEADS, tm), F32)],
        input_output_aliases={0: 0},
        compiler_params=_params(("arbitrary",)),
    )(dproj, conv, proj, a_log, dt_bias, dq, dk, dv, db, dg, dgl, dgrow)


_BNN = (((2,), (1,)), ((0,), (0,)))
_BNT = (((2,), (2,)), ((0,), (0,)))
_BTN = (((1,), (1,)), ((0,), (0,)))


@jax.custom_vjp
def _MM_NN(a, b):
    return _mxu(a, b, _BNN)


@jax.custom_vjp
def _MM_NT(a, b):
    return _mxu(a, b, _BNT)


@jax.custom_vjp
def _MM_TN(a, b):
    return _mxu(a, b, _BTN)


_MM_NN.defvjp(lambda a, b: (_mxu(a, b, _BNN), (a, b)), lambda r, g: (_mxu(g, r[1], _BNT), _mxu(r[0], g, _BTN)))
_MM_NT.defvjp(lambda a, b: (_mxu(a, b, _BNT), (a, b)), lambda r, g: (_mxu(g, r[1], _BNN), _mxu(g, r[0], _BTN)))
_MM_TN.defvjp(lambda a, b: (_mxu(a, b, _BTN), (a, b)), lambda r, g: (_mxu(r[1], g, _BNT), _mxu(r[0], g, _BNN)))


def _split(a):
    hi = a.astype(MXU_DTYPE)
    return hi, (a - hi.astype(F32)).astype(MXU_DTYPE)


def _dot3(a, b, dims):
    (ah, al), (bh, bl) = a, b
    (ca,), (cb,) = dims[0]
    return lax.dot_general(jnp.concatenate([ah, ah, al], axis=ca), jnp.concatenate([bh, bl, bh], axis=cb), dims,
                           preferred_element_type=F32)


def _unit_lower_inverse(a):
    c = GDN_CHUNK
    eye = jnp.where(_iota((c, c), 0) == _iota((c, c), 1), 1.0, 0.0)
    x = eye - a
    p = a
    for _ in range(5):
        ps = _split(p)
        p = _dot3(ps, ps, _BNN)
        x = x + _dot3(_split(x), _split(p), _BNN)
    return x


@jax.custom_vjp
def _SAVED_INVERSE(a, t_inv):
    return t_inv


def _saved_inverse_bwd(t_inv, g):
    ts = _split(t_inv)
    return -_dot3(ts, _split(_dot3(_split(g), ts, _BNT)), _BTN), jnp.zeros_like(t_inv)


_SAVED_INVERSE.defvjp(lambda a, t_inv: (t_inv, t_inv), _saved_inverse_bwd)


def _gdn_chunk(q, k, v, beta, g1, g2, gl, state, t_inv=None):
    c = GDN_CHUNK
    if t_inv is None:
        _mm_nn, _mm_nt, _mm_tn = (functools.partial(_mxu, dims=dd) for dd in (_BNN, _BNT, _BTN))
    else:
        _mm_nn, _mm_nt, _mm_tn = _MM_NN, _MM_NT, _MM_TN
    row, col = _iota((c, c), 0), _iota((c, c), 1)
    incl, strict = row >= col, row > col
    decay = jnp.where(incl, jnp.exp(jnp.where(incl, g1 - g2, 0.0)), 0.0)
    eg = jnp.exp(g1)
    kb = k * beta
    a = _mm_nt(kb, k) * jnp.where(strict, decay, 0.0)
    inv = _unit_lower_inverse(a) if t_inv is None else _SAVED_INVERSE(a, t_inv)
    u = _mm_nn(inv, v * beta)
    w = _mm_nn(inv, kb * eg)
    attn = _mm_nt(q, k) * decay
    v_new = u - _mm_nn(w, state)
    o = _mm_nn(q * eg, state) + _mm_nn(attn, v_new)
    new_state = state * jnp.exp(gl) + _mm_tn(k * jnp.exp(gl - g1), v_new)
    return (o, new_state, inv) if t_inv is None else (o, new_state)


def _unpair(x):
    return jnp.concatenate([x[..., :HEAD_DIM], x[..., HEAD_DIM:]], axis=0)


def _gdn_fwd(q, k, v, beta, g, gl, grow, cpb=4):
    t = q.shape[1]
    rows = cpb * GDN_CHUNK
    lo, hi = slice(0, HEAD_DIM), slice(HEAD_DIM, 2 * HEAD_DIM)

    def body(q_ref, k_ref, v_ref, b_ref, g_ref, gl_ref, grow_ref, o_ref, st_ref, inv_ref, state):
        @pl.when(pl.program_id(0) == 0)
        def _():
            state[...] = jnp.zeros_like(state)

        s = state[...]
        for cc in range(cpb):
            sl = slice(cc * GDN_CHUNK, (cc + 1) * GDN_CHUNK)
            st_ref[:, cc, :, lo], st_ref[:, cc, :, hi] = s[:HEADS // 2], s[HEADS // 2:]
            g2 = jnp.broadcast_to(grow_ref[:, cc], (HEADS, GDN_CHUNK, GDN_CHUNK))
            o, s, inv = _gdn_chunk(*[_unpair(r[:, sl, :]) for r in (q_ref, k_ref, v_ref, b_ref, g_ref)], g2,
                                   _unpair(gl_ref[:, sl, :]), s)
            o_ref[:, sl, lo], o_ref[:, sl, hi] = o[:HEADS // 2], o[HEADS // 2:]
            inv_ref[:, cc, :, lo], inv_ref[:, cc, :, hi] = inv[:HEADS // 2], inv[HEADS // 2:]
        state[...] = s

    hm = pl.BlockSpec((HEADS // 2, rows, 128), lambda i: (0, i, 0))
    per_chunk = pl.BlockSpec((HEADS // 2, cpb, GDN_CHUNK, 128), lambda i: (0, i, 0, 0))
    chunk_shape = jax.ShapeDtypeStruct((HEADS // 2, t // GDN_CHUNK, GDN_CHUNK, 128), F32)
    return pl.pallas_call(
        body, name="gdn_fwd", grid=(t // rows,),
        in_specs=[hm] * 6 + [pl.BlockSpec((HEADS, cpb, 1, GDN_CHUNK), lambda i: (0, i, 0, 0))],
        out_specs=[hm, per_chunk, per_chunk],
        out_shape=[jax.ShapeDtypeStruct((HEADS // 2, t, 128), F32), chunk_shape, chunk_shape],
        scratch_shapes=[pltpu.VMEM((HEADS, GDN_CHUNK, HEAD_DIM), F32)],
        compiler_params=_params(("arbitrary",)),
    )(q, k, v, beta, g, gl, grow)


def _gdn_bwd(q, k, v, beta, g, gl, grow, states, invs, do, cpb=2):
    t = q.shape[1]
    rows = cpb * GDN_CHUNK
    nsteps = t // rows
    lo, hi = slice(0, HEAD_DIM), slice(HEAD_DIM, 2 * HEAD_DIM)

    def body(q_ref, k_ref, v_ref, b_ref, g_ref, gl_ref, grow_ref, st_ref, inv_ref, do_ref,
             dq_ref, dk_ref, dv_ref, db_ref, dg_ref, dgl_ref, dgrow_ref, dstate):
        @pl.when(pl.program_id(0) == 0)
        def _():
            dstate[...] = jnp.zeros_like(dstate)

        ds = dstate[...]
        for cc in reversed(range(cpb)):
            sl = slice(cc * GDN_CHUNK, (cc + 1) * GDN_CHUNK)
            g2 = jnp.broadcast_to(grow_ref[:, cc], (HEADS, GDN_CHUNK, GDN_CHUNK))
            _, vjp = jax.vjp(_gdn_chunk, *[_unpair(r[:, sl, :]) for r in (q_ref, k_ref, v_ref, b_ref, g_ref)], g2,
                             _unpair(gl_ref[:, sl, :]), _unpair(st_ref[:, cc]), _unpair(inv_ref[:, cc]))
            gq, gk, gv, gb, gg1, gg2, ggl, ds, _ = vjp((_unpair(do_ref[:, sl, :]), ds))
            for ref, val in ((dq_ref, gq), (dk_ref, gk), (dv_ref, gv), (db_ref, gb), (dg_ref, gg1), (dgl_ref, ggl)):
                ref[:, sl, lo], ref[:, sl, hi] = val[:HEADS // 2], val[HEADS // 2:]
            dgrow_ref[:, cc] = jnp.sum(gg2, axis=1, keepdims=True)
        dstate[...] = ds

    hm = pl.BlockSpec((HEADS // 2, rows, 128), lambda i: (0, nsteps - 1 - i, 0))
    rowspec = pl.BlockSpec((HEADS, cpb, 1, GDN_CHUNK), lambda i: (0, nsteps - 1 - i, 0, 0))
    per_chunk = pl.BlockSpec((HEADS // 2, cpb, GDN_CHUNK, 128), lambda i: (0, nsteps - 1 - i, 0, 0))
    hm_shape = jax.ShapeDtypeStruct((HEADS // 2, t, 128), F32)
    return pl.pallas_call(
        body, name="gdn_bwd", grid=(nsteps,),
        in_specs=[hm] * 6 + [rowspec, per_chunk, per_chunk, hm],
        out_specs=[hm] * 6 + [rowspec],
        out_shape=[hm_shape] * 6 + [jax.ShapeDtypeStruct((HEADS, t // GDN_CHUNK, 1, GDN_CHUNK), F32)],
        scratch_shapes=[pltpu.VMEM((HEADS, GDN_CHUNK, HEAD_DIM), F32)],
        compiler_params=_params(("arbitrary",)),
    )(q, k, v, beta, g, gl, grow, states, invs, do)


def _gdn_out(o_hm, gdn_norm_w, proj, tm=512):
    t = proj.shape[0]

    def body(o_ref, w_ref, z_ref, oz_ref):
        w = w_ref[...]
        ones = _pair_ones()
        for p in range(HEADS // 2):
            cols = slice(p * 128, (p + 1) * 128)
            o = o_ref[p]
            z = z_ref[:, cols]
            r = lax.rsqrt(_pair_sum_lanes(o * o, ones) * (1.0 / HEAD_DIM) + NORM_EPS)
            oz_ref[:, cols] = (o * r * w * (z * _sigmoid(z))).astype(oz_ref.dtype)

    tok = pl.BlockSpec((tm, WIDTH), lambda i: (i, 0))
    return pl.pallas_call(
        body, name="gdn_out", grid=(t // tm,),
        in_specs=[pl.BlockSpec((HEADS // 2, tm, 128), lambda i: (0, i, 0)), pl.BlockSpec((1, 128), lambda i: (0, 0)),
                  pl.BlockSpec((tm, WIDTH), lambda i: (i, SEG_ZB // WIDTH))],
        out_specs=tok, out_shape=jax.ShapeDtypeStruct((t, WIDTH), MXU_DTYPE),
        compiler_params=_params(("parallel",)),
    )(o_hm, jnp.tile(gdn_norm_w, (1, 2)), proj)


def _gdn_out_bwd(dproj, d_oz, o_hm, gdn_norm_w, proj, tm=512):
    t = proj.shape[0]

    def body(dp_ref, doz_ref, o_ref, w_ref, z_ref, dz_ref, do_ref, dw_ref):
        w = w_ref[...]
        ones = _pair_ones()
        dw = jnp.zeros((1, 128), F32)
        for p in range(HEADS // 2):
            cols = slice(p * 128, (p + 1) * 128)
            o = o_ref[p]
            z, g = z_ref[:, cols], doz_ref[:, cols]
            sg = _sigmoid(z)
            r = lax.rsqrt(_pair_sum_lanes(o * o, ones) * (1.0 / HEAD_DIM) + NORM_EPS)
            dz_ref[:, cols] = (g * (o * r * w) * (sg * (1.0 + z * (1.0 - sg)))).astype(dz_ref.dtype)
            dn = g * (z * sg)
            dw += jnp.sum(dn * o * r, axis=0, keepdims=True)
            dnw = dn * w
            do_ref[p] = r * dnw - o * (r * r * r) * (_pair_sum_lanes(dnw * o, ones) * (1.0 / HEAD_DIM))

        @pl.when(pl.program_id(0) == 0)
        def _():
            dw_ref[...] = jnp.zeros_like(dw_ref)

        dw_ref[...] += jnp.where(_iota((8, 128), 0) == 0, dw, 0.0)

    tok = pl.BlockSpec((tm, WIDTH), lambda i: (i, 0))
    seg = pl.BlockSpec((tm, WIDTH), lambda i: (i, SEG_ZB // WIDTH))
    hm = pl.BlockSpec((HEADS // 2, tm, 128), lambda i: (0, i, 0))
    dz, do, dw = pl.pallas_call(
        body, name="gdn_out_bwd", grid=(t // tm,),
        in_specs=[pl.BlockSpec(memory_space=pl.ANY), tok, hm, pl.BlockSpec((1, 128), lambda i: (0, 0)), seg],
        out_specs=[seg, hm, pl.BlockSpec((8, 128), lambda i: (0, 0))],
        out_shape=[jax.ShapeDtypeStruct((t, PACKED_WIDTH), MXU_DTYPE), jax.ShapeDtypeStruct((HEADS // 2, t, 128), F32),
                   jax.ShapeDtypeStruct((8, 128), F32)],
        input_output_aliases={0: 0},
        compiler_params=_params(("arbitrary",)),
    )(dproj, d_oz, o_hm, jnp.tile(gdn_norm_w, (1, 2)), proj)
    return dz, do, dw[:, :HEAD_DIM] + dw[:, HEAD_DIM:]


def _merge(y_a, y_b, proj, tm=1024):
    t = proj.shape[0]

    def body(ya_ref, yb_ref, ga_ref, gb_ref, m_ref):
        m_ref[...] = (_sigmoid(ga_ref[...]) * ya_ref[...] + _sigmoid(gb_ref[...]) * yb_ref[...]).astype(m_ref.dtype)

    half = pl.BlockSpec((tm, WIDTH), lambda i, c: (i, c))
    return pl.pallas_call(
        body, name="merge", grid=(t // tm, 2),
        in_specs=[half, half, pl.BlockSpec((tm, WIDTH), lambda i, c: (i, SEG_GA // WIDTH + c)),
                  pl.BlockSpec((tm, WIDTH), lambda i, c: (i, SEG_GB // WIDTH + c))],
        out_specs=half, out_shape=jax.ShapeDtypeStruct((t, D_MODEL), MXU_DTYPE),
        compiler_params=_params(("parallel", "parallel")),
    )(y_a, y_b, proj, proj)


def _merge_bwd(dproj, d_m, y, proj, seg, name, tm=1024):
    t = proj.shape[0]

    def body(*refs):
        dm_ref, y_ref, g_ref, dg_ref, dy_ref = refs[-5:]
        dm = dm_ref[...]
        s = _sigmoid(g_ref[...])
        dy_ref[...] = (dm * s).astype(dy_ref.dtype)
        dg_ref[...] = (dm * y_ref[...] * s * (1.0 - s)).astype(dg_ref.dtype)

    half = pl.BlockSpec((tm, WIDTH), lambda i, c: (i, c))
    gate = pl.BlockSpec((tm, WIDTH), lambda i, c: (i, seg // WIDTH + c))
    specs, args, aliases = [half, half, gate], [d_m, y, proj], {}
    if dproj is not None:
        specs, args, aliases = [pl.BlockSpec(memory_space=pl.ANY)] + specs, [dproj] + args, {0: 0}
    return pl.pallas_call(
        body, name=name, grid=(t // tm, 2), in_specs=specs, out_specs=[gate, half],
        out_shape=[jax.ShapeDtypeStruct((t, PACKED_WIDTH), MXU_DTYPE), jax.ShapeDtypeStruct((t, D_MODEL), MXU_DTYPE)],
        input_output_aliases=aliases,
        compiler_params=_params(("parallel", "parallel")),
    )(*args)


def _out_tail(merged, w_out, x, final_w, target, tm=1024):
    t = x.shape[0]
    tm = min(tm, t)

    def body(m_ref, wo_ref, x_ref, w_ref, t_ref, dxm_ref, dx_ref, loss_ref, dw_ref):
        x2 = x_ref[...] + jnp.dot(m_ref[...], wo_ref[...], preferred_element_type=F32)
        w = w_ref[...]
        r = lax.rsqrt(jnp.mean(x2 * x2, axis=-1, keepdims=True) + NORM_EPS)
        xn = x2 * r
        err = xn * w - t_ref[...]
        dy = err * (1.0 / D_MODEL)
        dyw = dy * w
        dx2 = r * dyw - x2 * (r * r * r) * jnp.mean(dyw * x2, axis=-1, keepdims=True)
        dx_ref[...] = dx2
        dxm_ref[...] = dx2.astype(dxm_ref.dtype)
        loss = 0.5 * jnp.sum(jnp.sum(err * err, axis=-1, keepdims=True) * (1.0 / D_MODEL), axis=0, keepdims=True)
        onehot = jnp.where((_iota((8, 128), 0) == 0) & (_iota((8, 128), 1) == 0), 1.0, 0.0)

        @pl.when(pl.program_id(0) == 0)
        def _():
            loss_ref[...] = jnp.zeros_like(loss_ref)
            dw_ref[...] = jnp.zeros_like(dw_ref)

        loss_ref[...] += loss * onehot
        dw_ref[...] += jnp.where(_iota((8, D_MODEL), 0) == 0, jnp.sum(dy * xn, axis=0, keepdims=True), 0.0)

    tok = pl.BlockSpec((tm, D_MODEL), lambda i: (i, 0))
    full = pl.BlockSpec((D_MODEL, D_MODEL), lambda i: (0, 0))
    return pl.pallas_call(
        body, name="out_tail", grid=(t // tm,),
        in_specs=[tok, full, tok, pl.BlockSpec((1, D_MODEL), lambda i: (0, 0)), tok],
        out_specs=[tok, tok, pl.BlockSpec((8, 128), lambda i: (0, 0)), pl.BlockSpec((8, D_MODEL), lambda i: (0, 0))],
        out_shape=[jax.ShapeDtypeStruct((t, D_MODEL), MXU_DTYPE), jax.ShapeDtypeStruct((t, D_MODEL), F32),
                   jax.ShapeDtypeStruct((8, 128), F32), jax.ShapeDtypeStruct((8, D_MODEL), F32)],
        compiler_params=_params(("arbitrary",)),
    )(merged, w_out, x, final_w, target)


def _dh_norm_bwd(dproj, wp, x, norm_w, dx2, after, blocks, name, carry=None, tm=1024, tk=PACKED_WIDTH // 4):
    t = x.shape[0]
    nk = PACKED_WIDTH // tk
    first, end = blocks

    def body(*refs):
        dp_ref, wp_ref, x_ref, w_ref, dx2_ref = refs[:5]
        dx_ref, dw_ref, acc = refs[-3:]
        kk = pl.program_id(1)
        part = _mxu(dp_ref[...], wp_ref[...], _NT)

        @pl.when(kk == 0)
        def _():
            acc[...] = part

        @pl.when(kk > 0)
        def _():
            acc[...] += part

        @pl.when((kk == 0) & (pl.program_id(0) == 0))
        def _():
            dw_ref[...] = jnp.zeros_like(dw_ref) if carry is None else refs[-4][...]

        @pl.when(kk == nk - 1)
        def _():
            xf, w, dh_ = x_ref[...], w_ref[...], acc[...]
            r = lax.rsqrt(jnp.mean(xf * xf, axis=-1, keepdims=True) + NORM_EPS)
            dhw = dh_ * w
            dx_ref[...] = dx2_ref[...] + r * dhw - xf * (r * r * r) * jnp.mean(dhw * xf, axis=-1, keepdims=True)
            dw_ref[...] += jnp.where(_iota((8, D_MODEL), 0) == 0, jnp.sum(dh_ * xf * r, axis=0, keepdims=True), 0.0)

    tok = pl.BlockSpec((tm, D_MODEL), lambda i, kk: (first + i, 0))
    small = pl.BlockSpec((8, D_MODEL), lambda i, kk: (0, 0))
    specs = [pl.BlockSpec((tm, tk), lambda i, kk: (first + i, kk)), pl.BlockSpec((D_MODEL, tk), lambda i, kk: (0, kk)),
             tok, pl.BlockSpec((1, D_MODEL), lambda i, kk: (0, 0)), tok, pl.BlockSpec(memory_space=pl.ANY)]
    args = [dproj, wp, x, norm_w, dx2, after]
    aliases = {}
    if carry is not None:
        specs += [pl.BlockSpec(memory_space=pl.ANY), small]
        args += list(carry)
        aliases = {len(args) - 2: 0}
    return pl.pallas_call(
        body, name=name, grid=(end - first, nk), in_specs=specs, out_specs=[tok, small],
        out_shape=[jax.ShapeDtypeStruct((t, D_MODEL), F32), jax.ShapeDtypeStruct((8, D_MODEL), F32)],
        scratch_shapes=[pltpu.VMEM((tm, D_MODEL), F32)],
        input_output_aliases=aliases,
        compiler_params=_params(("arbitrary", "arbitrary")),
    )(*args)


def _local_step(x, target, norm_w, wp, conv_w, a_log, dt_bias, gdn_norm_w, w_up_a, w_up_b, w_out, final_w,
                start_reduce, continue_reduce):
    t = x.shape[0]
    tables = _rope_tables(t)
    a_log = jnp.pad(a_log, ((0, 0), (HEADS, 128 - 2 * HEADS)))
    dt_bias = jnp.pad(dt_bias, ((0, 0), (HEADS, 128 - 2 * HEADS)))

    proj, h_t = _norm_proj(x, norm_w, wp)
    qkvs = _rope_fwd(proj, tables)
    outs, lses = zip(*[_att_fwd(qkvs[gi], d, f"att_fwd{gi}") for gi, d in enumerate(DILATIONS)])
    oz_a, o_a, *lse_views = _att_merge(outs, lses, proj)
    conv = _conv_fwd(proj, conv_w)
    gq, gk, gv, gb, gg, ggl, grow = _gdn_prep(conv, proj, a_log, dt_bias)
    o_b, states, invs = _gdn_fwd(gq, gk, gv, gb, gg, ggl, grow)
    oz_b = _gdn_out(o_b, gdn_norm_w, proj)
    big = dict(tm=1024, tn=1024, tk=1024)
    y_a = _matmul(oz_a, w_up_a, "nn", "up_a", **big)
    y_b = _matmul(oz_b, w_up_b, "nn", "up_b", **big)
    merged = _merge(y_a, y_b, proj)
    dx2_m, dx2, loss_blk, d_final = _out_tail(merged, w_out, x, final_w, target)

    d_wout = _matmul(merged, dx2_m, "tn", "d_w_out", **big)
    d_m = _matmul(dx2_m, w_out, "nt", "d_merged", **big)
    dproj, dy_a = _merge_bwd(None, d_m, y_a, proj, SEG_GA, "merge_bwd_a")
    dproj, dy_b = _merge_bwd(dproj, d_m, y_b, proj, SEG_GB, "merge_bwd_b")
    d_wua = _matmul(oz_a, dy_a, "tn", "d_w_up_a", **big)
    d_wub = _matmul(oz_b, dy_b, "tn", "d_w_up_b", **big)
    d_oz_a = _matmul(dy_a, w_up_a, "nt", "d_oz_a", **big)
    d_oz_b = _matmul(dy_b, w_up_b, "nt", "d_oz_b", **big)
    dproj, *views = _att_merge_bwd(dproj, d_oz_a, o_a, proj)
    do_views, delta_views = views[:3], views[3:]
    dqs, dkvs = zip(*[_att_bwd(qkvs[gi], do_views[gi], lse_views[gi], delta_views[gi], d, f"att_bwd{gi}")
                      for gi, d in enumerate(DILATIONS)])
    dproj = _rope_bwd(dproj, dqs, dkvs, tables)
    dproj, do_b, d_gnw = _gdn_out_bwd(dproj, d_oz_b, o_b, gdn_norm_w, proj)
    dgq, dgk, dgv, dgb, dgg, dggl, dgrow = _gdn_bwd(gq, gk, gv, gb, gg, ggl, grow, states, invs, do_b)
    dproj, dconv, d_small = _gdn_prep_bwd(dproj, conv, proj, a_log, dt_bias, dgq, dgk, dgv, dgb, dgg, dggl, dgrow)
    dproj, d_convw = _conv_bwd(dproj, dconv, proj, conv_w)
    d_wp = _matmul(h_t, dproj, "nn", "d_w_in", tm=1024, tn=PACKED_WIDTH // 4, tk=1024)
    in_flight, token = start_reduce(d_wp, d_wua, d_wub, d_wout, d_convw[0:GDN_CONV])
    nblk = t // 1024
    cut = max(1, nblk // 4)
    part = _dh_norm_bwd(dproj, wp, x, norm_w, dx2, token, (0, cut), "dh_norm_bwd_a")
    in_flight, token = continue_reduce(in_flight, part[0])
    grad_x, d_norm = _dh_norm_bwd(dproj, wp, x, norm_w, dx2, token, (cut, nblk), "dh_norm_bwd_b", carry=part)
    return dict(loss=loss_blk, grad_x=grad_x, norm_w=d_norm[0:1], in_flight=in_flight,
                a_log=d_small[0:1, HEADS:2 * HEADS], dt_bias=d_small[1:2, HEADS:2 * HEADS], gdn_norm_w=d_gnw[0:1],
                final_norm_w=d_final[0:1])


SHARDS = 4
W_IN_SHARD = IN_WIDTH // SHARDS
ROWS_UP = WIDTH * (D_MODEL // SHARDS) // 128
ROWS_OUT = (D_MODEL // SHARDS) * D_MODEL // 128
CONV_SHARD = 3 * WIDTH // SHARDS
ROWS_CONV = 16
SLAB_ROWS = 2 * ROWS_UP + ROWS_OUT + 2 * ROWS_CONV
HALF_ROWS = SLAB_ROWS // 2
BIG_HALF = (D_MODEL // 2, W_IN_SHARD)
SMALL_HALF = (HALF_ROWS, 128)
MESH = pl.DeviceIdType.MESH
ANY = pl.BlockSpec(memory_space=pl.ANY)


def _pad_rows(a, rows):
    return jnp.pad(a, ((0, rows - a.shape[0]), (0, 0)))


def _pack_slab(w_up_a, w_up_b, w_out, conv, conv_lo):
    parts = [w_up_a.reshape(ROWS_UP, 128), w_up_b.reshape(ROWS_UP, 128), w_out.reshape(ROWS_OUT, 128),
             _pad_rows(conv.reshape(-1, 128), ROWS_CONV), _pad_rows(conv_lo.reshape(-1, 128), ROWS_CONV)]
    return jnp.concatenate(parts, axis=0).reshape(2, *SMALL_HALF)


def _unpack_slab(slab):
    slab = slab.reshape(SLAB_ROWS, 128)
    r0 = 0
    out = []
    for rows, shape in ((ROWS_UP, (WIDTH, D_MODEL // SHARDS)), (ROWS_UP, (WIDTH, D_MODEL // SHARDS)),
                        (ROWS_OUT, (D_MODEL // SHARDS, D_MODEL)), (ROWS_CONV, None), (ROWS_CONV, None)):
        part = slab[r0:r0 + rows]
        out.append(part[:GDN_CONV * CONV_SHARD // 128].reshape(GDN_CONV, CONV_SHARD) if shape is None else part.reshape(shape))
        r0 += rows
    return out


def _mesh_position():
    x, y, c = lax.axis_index("x"), lax.axis_index("y"), lax.axis_index("c")
    return x, y, c, [(1 - x, y), (x, 1 - y), (1 - x, 1 - y)]


def _gather_weights(shards):
    n = len(shards)

    def body(*refs):
        in_refs, out_refs, (send_sems, recv_sems) = refs[:n], refs[n:2 * n], refs[2 * n:]
        x, y, c, chips = _mesh_position()

        def half(a, chip, which):
            return out_refs[a].at[2 * chip[0] + chip[1], which]

        def copy(k, src, dst, to):
            return pltpu.make_async_remote_copy(src_ref=src, dst_ref=dst, send_sem=send_sems.at[k], recv_sem=recv_sems.at[k],
                                                device_id=to, device_id_type=MESH)

        pairs = [(a, j, chip) for a in range(n) for j, chip in enumerate(chips)]
        first = [copy(6 * a + j, in_refs[a].at[c], half(a, (x, y), c), (*chip, c)) for a, j, chip in pairs]
        for cp in first:
            cp.start()
        passed = [copy(6 * a + 3 + j, half(a, chip, c), half(a, chip, c), (x, y, 1 - c)) for a, j, chip in pairs]
        for i, (a, j, chip) in enumerate(pairs):
            copy(6 * a + j, half(a, chip, c), half(a, chip, c), (x, y, c)).wait_recv()
            passed[i].start()
        for a, j, chip in pairs:
            copy(6 * a + 3 + j, half(a, chip, 1 - c), half(a, chip, 1 - c), (x, y, c)).wait_recv()
        for cp in first + passed:
            cp.wait_send()

    return pl.pallas_call(
        body, name="gather_weights", in_specs=[ANY] * n, out_specs=[ANY] * n,
        out_shape=[jax.ShapeDtypeStruct((SHARDS, *s.shape), s.dtype) for s in shards],
        scratch_shapes=[pltpu.SemaphoreType.DMA((6 * n,)), pltpu.SemaphoreType.DMA((6 * n,))],
    )(*shards)


def _pair_sum(grads, recv, blk, name):
    _, _, rows, cols = grads.shape

    def body(c_ref, g_ref, r_ref, o_ref):
        o_ref[...] = (g_ref[0] + r_ref[...]).astype(o_ref.dtype)

    spec = pl.BlockSpec((1, blk, cols), lambda s, i, c_ref: (s, i, 0))
    return pl.pallas_call(
        body, name=name,
        grid_spec=pltpu.PrefetchScalarGridSpec(
            num_scalar_prefetch=1, grid=(SHARDS, rows // blk),
            in_specs=[pl.BlockSpec((1, 1, blk, cols), lambda s, i, c_ref: (s, c_ref[0], i, 0)), spec],
            out_specs=spec),
        out_shape=jax.ShapeDtypeStruct((SHARDS, rows, cols), MXU_DTYPE),
        compiler_params=_params(("parallel", "parallel")),
    )(lax.axis_index("c").astype(jnp.int32).reshape(1), grads, recv)


_HBM = pl.BlockSpec(memory_space=pltpu.HBM)
_SEM = pl.BlockSpec(memory_space=pltpu.SEMAPHORE)
_DATAFLOW = pltpu.SideEffectType.DATAFLOW_SIDE_EFFECTING


def _scatter_copies(p_refs, l_refs, send_sems, recv_sems):
    x, y, c, chips = _mesh_position()
    return [pltpu.make_async_remote_copy(src_ref=p_refs[a].at[2 * chip[0] + chip[1]], dst_ref=l_refs[a].at[j],
                                         send_sem=send_sems.at[3 * a + j], recv_sem=recv_sems.at[3 * a + j],
                                         device_id=(*chip, c), device_id_type=MESH)
            for a in range(len(p_refs)) for j, chip in enumerate(chips)]


def _exchange_copies(g_refs, l_refs, send_sems, recv_sems):
    x, y, c, _ = _mesh_position()
    return [pltpu.make_async_remote_copy(src_ref=g_refs[a].at[s, 1 - c], dst_ref=l_refs[a].at[s],
                                         send_sem=send_sems.at[SHARDS * a + s], recv_sem=recv_sems.at[SHARDS * a + s],
                                         device_id=(x, y, 1 - c), device_id_type=MESH)
            for a in range(len(g_refs)) for s in range(SHARDS)]


def _copies_start(name, copies, count, sources, land_shapes):
    n = len(sources)
    lands = [lax.empty(shape, src.dtype) for shape, src in zip(land_shapes, sources)]

    def body(*refs):
        s_refs, l_refs, send_sems, recv_sems, token = refs[:n], refs[n:2 * n], refs[2 * n], refs[2 * n + 1], refs[-1]
        for cp in copies(s_refs, l_refs, send_sems, recv_sems):
            cp.start()
        token[...] = jnp.zeros_like(token)

    operands = [pltpu.with_memory_space_constraint(a, pltpu.HBM) for a in (*sources, *lands)]
    return pl.pallas_call(
        body, name=name, in_specs=[_HBM] * (2 * n),
        out_shape=(pltpu.SemaphoreType.DMA((count,)), pltpu.SemaphoreType.DMA((count,)),
                   *[pltpu.HBM(a.shape, a.dtype) for a in operands], jax.ShapeDtypeStruct((8, 128), F32)),
        out_specs=(_SEM, _SEM, *[_HBM] * (2 * n), pl.BlockSpec(memory_space=pltpu.VMEM)),
        input_output_aliases={i: 2 + i for i in range(2 * n)},
        compiler_params=pltpu.CompilerParams(has_side_effects=_DATAFLOW),
    )(*operands)


def _copies_wait(name, copies, send_sems, recv_sems, passed, after):
    n = len(passed) // 2

    def body(*refs):
        s_refs, l_refs, send_s, recv_s = refs[:n], refs[n:2 * n], refs[2 * n], refs[2 * n + 1]
        for cp in copies(s_refs, l_refs, send_s, recv_s):
            cp.wait_send()
            cp.wait_recv()

    return pl.pallas_call(
        body, name=name, in_specs=[_HBM] * (2 * n) + [_SEM, _SEM, ANY],
        out_shape=[pltpu.HBM(a.shape, a.dtype) for a in passed], out_specs=[_HBM] * (2 * n),
        input_output_aliases={i: i for i in range(2 * n)},
        compiler_params=pltpu.CompilerParams(has_side_effects=_DATAFLOW),
    )(*passed, send_sems, recv_sems, after)


def _chip_sum(pairs, recv, blk, name):
    _, rows, cols = pairs.shape

    def body(pos_ref, p_ref, r_ref, o_ref):
        o_ref[0] = ((p_ref[0].astype(F32) + r_ref[0].astype(F32)) + r_ref[1].astype(F32)) + r_ref[2].astype(F32)

    pos = jnp.stack([2 * lax.axis_index("x") + lax.axis_index("y"), lax.axis_index("c")]).astype(jnp.int32)
    return pl.pallas_call(
        body, name=name,
        grid_spec=pltpu.PrefetchScalarGridSpec(
            num_scalar_prefetch=1, grid=(rows // blk,),
            in_specs=[pl.BlockSpec((1, blk, cols), lambda i, pos_ref: (pos_ref[0], i, 0)),
                      pl.BlockSpec((3, blk, cols), lambda i, pos_ref: (0, i, 0))],
            out_specs=pl.BlockSpec((1, blk, cols), lambda i, pos_ref: (pos_ref[1], i, 0))),
        out_shape=jax.ShapeDtypeStruct((2, rows, cols), F32),
        compiler_params=_params(("parallel",)),
    )(pos, pairs, recv)


def _share_total(totals):
    n = len(totals)

    def body(*refs):
        t_refs, out_refs, (send_sems, recv_sems) = refs[:n], refs[n:2 * n], refs[2 * n:]
        x, y, c, _ = _mesh_position()
        copies = [pltpu.make_async_remote_copy(src_ref=t_refs[a].at[c], dst_ref=out_refs[a].at[c], send_sem=send_sems.at[a],
                                               recv_sem=recv_sems.at[a], device_id=(x, y, 1 - c), device_id_type=MESH)
                  for a in range(n)]
        for cp in copies:
            cp.start()
        for a in range(n):
            other = out_refs[a].at[1 - c]
            pltpu.make_async_remote_copy(src_ref=other, dst_ref=other, send_sem=send_sems.at[a], recv_sem=recv_sems.at[a],
                                         device_id=(x, y, c), device_id_type=MESH).wait_recv()
        for cp in copies:
            cp.wait_send()

    return pl.pallas_call(
        body, name="share_total", in_specs=[ANY] * n, out_specs=[ANY] * n,
        out_shape=[jax.ShapeDtypeStruct(t.shape, F32) for t in totals],
        scratch_shapes=[pltpu.SemaphoreType.DMA((n,)), pltpu.SemaphoreType.DMA((n,))],
        input_output_aliases={a: a for a in range(n)},
    )(*totals)


def _allreduce_small(block):
    def body(b_ref, out_ref, gath, send_sems, recv_sems):
        x, y, c, _ = _mesh_position()
        me = 4 * x + 2 * y + c
        gath[me] = b_ref[...]
        copies = []
        for k in range(1, 8):
            peer = (x ^ (k >> 2), y ^ ((k >> 1) & 1), c ^ (k & 1))
            copies.append(pltpu.make_async_remote_copy(src_ref=b_ref, dst_ref=gath.at[me], send_sem=send_sems.at[k - 1],
                                                       recv_sem=recv_sems.at[k - 1], device_id=peer, device_id_type=MESH))
        for cp in copies:
            cp.start()
        for k in range(1, 8):
            src = 4 * (x ^ (k >> 2)) + 2 * (y ^ ((k >> 1) & 1)) + (c ^ (k & 1))
            pltpu.make_async_remote_copy(src_ref=b_ref, dst_ref=gath.at[src], send_sem=send_sems.at[k - 1],
                                         recv_sem=recv_sems.at[k - 1], device_id=(x, y, c), device_id_type=MESH).wait_recv()
        for cp in copies:
            cp.wait_send()
        acc = gath[0]
        for d in range(1, 8):
            acc = acc + gath[d]
        out_ref[...] = acc

    vm = pl.BlockSpec(memory_space=pltpu.VMEM)
    return pl.pallas_call(
        body, name="allreduce_small", in_specs=[vm], out_specs=vm,
        out_shape=jax.ShapeDtypeStruct((8, D_MODEL), F32),
        scratch_shapes=[pltpu.VMEM((8, 8, D_MODEL), F32), pltpu.SemaphoreType.DMA((7,)), pltpu.SemaphoreType.DMA((7,))],
    )(block)


def _adamw(w, g, m, v, name):
    rows, cols = w.shape
    tr = 128 if rows % 128 == 0 else rows

    def body(w_ref, g_ref, m_ref, v_ref, d_ref, nm_ref, nv_ref):
        gv = g_ref[...]
        nm = ADAM_B1 * m_ref[...] + (1.0 - ADAM_B1) * gv
        nv = ADAM_B2 * v_ref[...] + (1.0 - ADAM_B2) * (gv * gv)
        m_hat = nm / (1.0 - ADAM_B1 ** ADAM_STEP)
        v_hat = nv / (1.0 - ADAM_B2 ** ADAM_STEP)
        d_ref[...] = -ADAM_LR * (m_hat / (jnp.sqrt(v_hat) + ADAM_EPS) + ADAM_WD * w_ref[...])
        nm_ref[...] = nm
        nv_ref[...] = nv

    spec = pl.BlockSpec((tr, cols), lambda i: (i, 0))
    shape = jax.ShapeDtypeStruct((rows, cols), F32)
    return pl.pallas_call(
        body, name=name, grid=(rows // tr,), in_specs=[spec] * 4, out_specs=[spec] * 3, out_shape=[shape] * 3,
        compiler_params=_params(("parallel",)),
    )(w, g, m, v)


def kernel(x, norm_w, w_in, conv_w, a_log, dt_bias, gdn_norm_w, w_up_a, w_up_b, w_out, final_norm_w, loss_target, m_norm_w, m_w_in, m_conv_w, m_a_log, m_dt_bias, m_gdn_norm_w, m_w_up_a, m_w_up_b, m_w_out, m_final_norm_w, v_norm_w, v_w_in, v_conv_w, v_a_log, v_dt_bias, v_gdn_norm_w, v_w_up_a, v_w_up_b, v_w_out, v_final_norm_w):
    conv_hi = conv_w[0].astype(MXU_DTYPE)
    conv_lo = (conv_w[0] - conv_hi.astype(F32)).astype(MXU_DTYPE)
    big = w_in[0].astype(MXU_DTYPE).reshape(2, *BIG_HALF)
    slab = _pack_slab(w_up_a[0].astype(MXU_DTYPE), w_up_b[0].astype(MXU_DTYPE), w_out[0].astype(MXU_DTYPE), conv_hi, conv_lo)
    own_shard = 2 * lax.axis_index("x") + lax.axis_index("y")
    bigs, slabs = _gather_weights([big, slab])
    bigs = lax.dynamic_update_slice(bigs, big[None], (own_shard, 0, 0, 0)).reshape(SHARDS, D_MODEL, W_IN_SHARD)
    slabs = lax.dynamic_update_slice(slabs, slab[None], (own_shard, 0, 0, 0))
    parts = [_unpack_slab(slabs[s]) for s in range(SHARDS)]
    split = BA_END - (SHARDS - 1) * W_IN_SHARD
    wp = jnp.concatenate([bigs[s] for s in range(SHARDS - 1)]
                         + [bigs[-1][:, :split], jnp.zeros((D_MODEL, SEG_GA - BA_END), MXU_DTYPE), bigs[-1][:, split:]], axis=1)
    w_up_a_full = jnp.concatenate([p[0] for p in parts], axis=1)
    w_up_b_full = jnp.concatenate([p[1] for p in parts], axis=1)
    w_out_full = jnp.concatenate([p[2] for p in parts], axis=0)
    conv_full = jnp.concatenate([p[3].astype(F32) + p[4].astype(F32) for p in parts], axis=1)

    blocks, tags = (128, HALF_ROWS), ("w_in", "slab")

    def start_reduce(d_wp, d_w_up_a, d_w_up_b, d_w_out, d_conv_w):
        d_w_in = [d_wp[:, s * W_IN_SHARD:(s + 1) * W_IN_SHARD] for s in range(SHARDS - 1)]
        d_w_in.append(jnp.concatenate([d_wp[:, (SHARDS - 1) * W_IN_SHARD:BA_END], d_wp[:, SEG_GA:]], axis=1))
        zero_conv = jnp.zeros((GDN_CONV, CONV_SHARD), F32)
        grads = [jnp.stack(d_w_in).reshape(SHARDS, 2, *BIG_HALF),
                 jnp.stack([_pack_slab(d_w_up_a[:, s * 256:(s + 1) * 256], d_w_up_b[:, s * 256:(s + 1) * 256],
                                       d_w_out[s * 256:(s + 1) * 256], d_conv_w[:, s * CONV_SHARD:(s + 1) * CONV_SHARD],
                                       zero_conv) for s in range(SHARDS)])]
        *in_flight, token = _copies_start("exchange_start", _exchange_copies, 2 * SHARDS, grads,
                                          [(SHARDS, *gr.shape[2:]) for gr in grads])
        return in_flight, token

    def continue_reduce(in_flight, after):
        send_sems, recv_sems, *passed = in_flight
        arrived = _copies_wait("exchange_wait", _exchange_copies, send_sems, recv_sems, passed, after)
        grads, from_sibling = arrived[:2], arrived[2:]
        pairs = [_pair_sum(gr, fs, blk, f"pair_sum_{tag}") for gr, fs, blk, tag in zip(grads, from_sibling, blocks, tags)]
        *in_flight, token = _copies_start("scatter_start", _scatter_copies, 2 * 3, pairs, [(3, *p.shape[1:]) for p in pairs])
        return in_flight, token

    g = _local_step(x[0], loss_target[0], norm_w, wp, conv_full, a_log, dt_bias, gdn_norm_w,
                    w_up_a_full, w_up_b_full, w_out_full, final_norm_w[None], start_reduce, continue_reduce)

    send_sems, recv_sems, *passed = g["in_flight"]
    arrived = _copies_wait("scatter_wait", _scatter_copies, send_sems, recv_sems, passed, g["grad_x"])
    pairs, from_chips = arrived[:2], arrived[2:]
    total_big, total_slab = _share_total([_chip_sum(p, fc, blk, f"chip_sum_{tag}")
                                          for p, fc, blk, tag in zip(pairs, from_chips, blocks, tags)])
    g_w_in = total_big.reshape(D_MODEL, W_IN_SHARD)
    g_w_up_a, g_w_up_b, g_w_out, g_conv, _ = _unpack_slab(total_slab)

    row2 = jnp.concatenate([g["gdn_norm_w"], g["a_log"], g["dt_bias"], g["loss"][0:1, 0:1],
                            jnp.zeros((1, D_MODEL - HEAD_DIM - 2 * HEADS - 1), F32)], axis=1)
    small = _allreduce_small(jnp.concatenate([g["norm_w"], g["final_norm_w"], row2, jnp.zeros((5, D_MODEL), F32)], axis=0))
    g_norm, g_final = small[0:1], small[1]
    g_gnw, g_alog, g_dt = small[2:3, 0:HEAD_DIM], small[2:3, HEAD_DIM:HEAD_DIM + HEADS], small[2:3, HEAD_DIM + HEADS:HEAD_DIM + 2 * HEADS]
    loss = small[2, HEAD_DIM + 2 * HEADS]

    names = ["norm_w", "w_in", "conv_w", "a_log", "dt_bias", "gdn_norm_w", "w_up_a", "w_up_b", "w_out", "final_norm_w"]
    weights = dict(zip(names, (norm_w, w_in, conv_w, a_log, dt_bias, gdn_norm_w, w_up_a, w_up_b, w_out, final_norm_w)))
    ms = dict(zip(names, (m_norm_w, m_w_in, m_conv_w, m_a_log, m_dt_bias, m_gdn_norm_w, m_w_up_a, m_w_up_b, m_w_out, m_final_norm_w)))
    vs = dict(zip(names, (v_norm_w, v_w_in, v_conv_w, v_a_log, v_dt_bias, v_gdn_norm_w, v_w_up_a, v_w_up_b, v_w_out, v_final_norm_w)))
    grads2d = dict(norm_w=g_norm, w_in=g_w_in, conv_w=g_conv, a_log=g_alog, dt_bias=g_dt, gdn_norm_w=g_gnw,
                   w_up_a=g_w_up_a, w_up_b=g_w_up_b, w_out=g_w_out, final_norm_w=g_final[None])
    grad_out, delta, new_m, new_v = [], [], [], []
    for n in names:
        shape = weights[n].shape
        two_d = grads2d[n].shape
        d, nm, nv = _adamw(weights[n].reshape(two_d), grads2d[n], ms[n].reshape(two_d), vs[n].reshape(two_d), f"adamw_{n}")
        grad_out.append(grads2d[n].reshape(shape))
        delta.append(d.reshape(shape))
        new_m.append(nm.reshape(shape))
        new_v.append(nv.reshape(shape))
    return (loss, g["grad_x"][None], *grad_out, *delta, *new_m, *new_v)
```

```python
import functools

import jax
import jax.numpy as jnp
from jax import lax
from jax.experimental import pallas as pl
from jax.experimental.pallas import tpu as pltpu

F32 = jnp.float32
MXU_DTYPE = jnp.bfloat16
HIGHEST = lax.Precision.HIGHEST

D_MODEL = 1024
HEADS = 8
HEAD_DIM = 64
WIDTH = HEADS * HEAD_DIM
NORM_EPS = 1e-6
ROPE_THETA = 10000.0
ATT_BLOCK = 128
DILATIONS = (1, 4, 16)
GDN_CHUNK = 64
GDN_CONV = 4
IN_WIDTH = 9232
SEG_A, SEG_ZA, SEG_B, SEG_ZB, SEG_BA, SEG_GA, SEG_GB, PACKED_WIDTH = 0, 4608, 5120, 6656, 7168, 7680, 8704, 9728
BA_END = 7184
VMEM_LIMIT = 56 * 1024 * 1024

ADAM_LR, ADAM_B1, ADAM_B2, ADAM_EPS, ADAM_WD, ADAM_STEP = 0.001, 0.9, 0.999, 1e-08, 0.01, 10

_NN = (((1,), (0,)), ((), ()))
_NT = (((1,), (1,)), ((), ()))
_TN = (((0,), (0,)), ((), ()))


def _params(sem):
    return pltpu.CompilerParams(dimension_semantics=sem, vmem_limit_bytes=VMEM_LIMIT)


def _mxu(a, b, dims):
    return lax.dot_general(a.astype(MXU_DTYPE), b.astype(MXU_DTYPE), dims, preferred_element_type=F32)


def _sigmoid(x):
    return 1.0 / (1.0 + jnp.exp(-x))


def _softplus(x):
    return jnp.maximum(x, 0.0) + jnp.log(1.0 + jnp.exp(-jnp.abs(x)))


def _iota(shape, axis):
    return lax.broadcasted_iota(jnp.int32, shape, axis)


def _matmul(a, b, mode, name, out_dtype=F32, tm=512, tn=512, tk=512):
    if mode == "nn":
        (m, k), (k2, n) = a.shape, b.shape
    elif mode == "nt":
        (m, k), (n, k2) = a.shape, b.shape
    else:
        (k, m), (k2, n) = a.shape, b.shape
    assert k == k2
    tm, tn, tk = min(tm, m), min(tn, n), min(tk, k)
    assert m % tm == 0 and n % tn == 0 and k % tk == 0
    nk = k // tk
    dims = {"nn": _NN, "nt": _NT, "tn": _TN}[mode]

    assert out_dtype == F32

    def body(a_ref, b_ref, o_ref):
        kk = pl.program_id(2)
        part = _mxu(a_ref[...], b_ref[...], dims)

        @pl.when(kk == 0)
        def _():
            o_ref[...] = part

        @pl.when(kk > 0)
        def _():
            o_ref[...] += part

    a_spec = pl.BlockSpec((tk, tm), lambda i, j, kk: (kk, i)) if mode == "tn" else pl.BlockSpec((tm, tk), lambda i, j, kk: (i, kk))
    b_spec = pl.BlockSpec((tn, tk), lambda i, j, kk: (j, kk)) if mode == "nt" else pl.BlockSpec((tk, tn), lambda i, j, kk: (kk, j))
    return pl.pallas_call(
        body, name=name, grid=(m // tm, n // tn, nk), in_specs=[a_spec, b_spec],
        out_specs=pl.BlockSpec((tm, tn), lambda i, j, kk: (i, j)),
        out_shape=jax.ShapeDtypeStruct((m, n), out_dtype),
        compiler_params=_params(("parallel", "parallel", "arbitrary")),
    )(a, b)


def _norm_proj(x, norm_w, wp, tm=1024, tn=PACKED_WIDTH // 4):
    t = x.shape[0]
    tm = min(tm, t)

    def body(x_ref, nw_ref, w_ref, proj_ref, ht_ref, h_scr):
        @pl.when(pl.program_id(1) == 0)
        def _():
            xf = x_ref[...]
            r = lax.rsqrt(jnp.mean(xf * xf, axis=-1, keepdims=True) + NORM_EPS)
            h = xf * r * nw_ref[...]
            h_scr[...] = h.astype(h_scr.dtype)
            ht_ref[...] = h.T.astype(ht_ref.dtype)

        proj_ref[...] = jnp.dot(h_scr[...], w_ref[...], preferred_element_type=F32)

    return pl.pallas_call(
        body, name="norm_proj", grid=(t // tm, PACKED_WIDTH // tn),
        in_specs=[pl.BlockSpec((tm, D_MODEL), lambda i, j: (i, 0)),
                  pl.BlockSpec((1, D_MODEL), lambda i, j: (0, 0)),
                  pl.BlockSpec((D_MODEL, tn), lambda i, j: (0, j))],
        out_specs=[pl.BlockSpec((tm, tn), lambda i, j: (i, j)),
                   pl.BlockSpec((D_MODEL, tm), lambda i, j: (0, i))],
        out_shape=[jax.ShapeDtypeStruct((t, PACKED_WIDTH), F32), jax.ShapeDtypeStruct((D_MODEL, t), MXU_DTYPE)],
        scratch_shapes=[pltpu.VMEM((tm, D_MODEL), MXU_DTYPE)],
        compiler_params=_params(("parallel", "arbitrary")),
    )(x, norm_w, wp)


def _rope_tables(t):
    lane = jnp.arange(128)
    inv_freq = ROPE_THETA ** (-jnp.arange(0, HEAD_DIM, 2, dtype=F32) / HEAD_DIM)
    freq = jnp.concatenate([inv_freq] * 4)
    coarse = (jnp.arange(t // 128, dtype=F32) * 128.0)[:, None] * freq[None, :]
    fine = jnp.arange(128, dtype=F32)[:, None] * freq[None, :]
    ca, sa, cb, sb = jnp.cos(coarse)[:, None, :], jnp.sin(coarse)[:, None, :], jnp.cos(fine)[None], jnp.sin(fine)[None]
    cos = (ca * cb - sa * sb).reshape(t, 128)
    sin = (sa * cb + ca * sb).reshape(t, 128)
    first_half = (lane % HEAD_DIM) < HEAD_DIM // 2
    return cos, jnp.where(first_half, -sin, 0.0), jnp.where(first_half, 0.0, sin)


def _rope_cols(x, cos, sin_lo, sin_hi, sign):
    outs = []
    for c in range(x.shape[1] // 128):
        xc = x[:, c * 128:(c + 1) * 128]
        rot = pltpu.roll(xc, 96, 1) * sin_lo + pltpu.roll(xc, 32, 1) * sin_hi
        outs.append(xc * cos + sign * rot)
    return jnp.concatenate(outs, axis=1)


def _rope_block(x, cos, sin_lo, sin_hi, sign):
    return jnp.concatenate([_rope_cols(x[:, :2 * WIDTH], cos, sin_lo, sin_hi, sign), x[:, 2 * WIDTH:]], axis=1)


def _tile_scratch(tm, cols):
    return pltpu.VMEM((cols // 128, tm, 128), F32)


def _store_tile(scr, y):
    for c in range(scr.shape[0]):
        scr[c] = y[:, c * 128:(c + 1) * 128]


def _load_tile(scr):
    return jnp.concatenate([scr[c] for c in range(scr.shape[0])], axis=1)


def _to_strided_view(scr, o_ref, d, width=None, col0=0):
    n, tm, _ = scr.shape
    width = n * 128 if width is None else width
    for r in range(d):
        for c in range(n):
            at = r * width + col0 + c * 128
            o_ref[:, at:at + 128] = scr[c, pl.ds(r, tm // d, stride=d), :].astype(o_ref.dtype)


def _from_strided_view(i_ref, scr, d):
    n, tm, _ = scr.shape
    for r in range(d):
        for c in range(n):
            scr[c, pl.ds(r, tm // d, stride=d), :] = i_ref[:, (r * n + c) * 128:(r * n + c + 1) * 128].astype(F32)


def _strided_spec(tm, d, cols):
    return pl.BlockSpec((tm // d, d * cols), lambda i: (i, 0))


def _rope_fwd(proj, tables, tm=512):
    t = proj.shape[0]
    cols = 3 * WIDTH

    def body(x_ref, c_ref, sl_ref, sh_ref, o0, o1, o2, scr):
        for g, (d, o_ref) in enumerate(zip(DILATIONS, (o0, o1, o2))):
            y = _rope_block(x_ref[:, g * cols:(g + 1) * cols], c_ref[...], sl_ref[...], sh_ref[...], 1.0)
            if d == 1:
                o_ref[...] = y.astype(o_ref.dtype)
            else:
                _store_tile(scr, y)
                _to_strided_view(scr, o_ref, d)

    tab = pl.BlockSpec((tm, 128), lambda i: (i, 0))
    return pl.pallas_call(
        body, name="rope_fwd", grid=(t // tm,),
        in_specs=[pl.BlockSpec((tm, 3 * cols), lambda i: (i, 0)), tab, tab, tab],
        out_specs=[_strided_spec(tm, d, cols) for d in DILATIONS],
        out_shape=[jax.ShapeDtypeStruct((t // d, d * cols), MXU_DTYPE) for d in DILATIONS],
        scratch_shapes=[_tile_scratch(tm, cols)],
        compiler_params=_params(("parallel",)),
    )(proj, *tables)


def _rope_bwd(dproj, dqs, dkvs, tables, tm=512):
    t = dproj.shape[0]
    cols = 3 * WIDTH

    def body(dp_ref, q0, q1, q2, kv0, kv1, kv2, c_ref, sl_ref, sh_ref, o_ref, scr_q, scr_kv):
        for g, (d, q_ref, kv_ref) in enumerate(zip(DILATIONS, (q0, q1, q2), (kv0, kv1, kv2))):
            if d == 1:
                x = jnp.concatenate([q_ref[...], kv_ref[...]], axis=1)
            else:
                _from_strided_view(q_ref, scr_q, d)
                _from_strided_view(kv_ref, scr_kv, d)
                x = jnp.concatenate([_load_tile(scr_q), _load_tile(scr_kv)], axis=1)
            y = _rope_block(x, c_ref[...], sl_ref[...], sh_ref[...], -1.0)
            o_ref[:, g * cols:(g + 1) * cols] = y.astype(o_ref.dtype)

    tab = pl.BlockSpec((tm, 128), lambda i: (i, 0))
    return pl.pallas_call(
        body, name="rope_bwd", grid=(t // tm,),
        in_specs=[pl.BlockSpec(memory_space=pl.ANY)] + [_strided_spec(tm, d, WIDTH) for d in DILATIONS]
        + [_strided_spec(tm, d, 2 * WIDTH) for d in DILATIONS] + [tab, tab, tab],
        out_specs=pl.BlockSpec((tm, 3 * cols), lambda i: (i, 0)),
        out_shape=jax.ShapeDtypeStruct((t, PACKED_WIDTH), MXU_DTYPE),
        scratch_shapes=[_tile_scratch(tm, WIDTH), _tile_scratch(tm, 2 * WIDTH)],
        input_output_aliases={0: 0},
        compiler_params=_params(("parallel",)),
    )(dproj, *dqs, *dkvs, *tables)


def _att_masks():
    qi = _iota((ATT_BLOCK, ATT_BLOCK), 0)
    kj = _iota((ATT_BLOCK, ATT_BLOCK), 1)
    return kj <= qi, kj >= qi


def _att_fwd(qkv, d, name):
    rows = qkv.shape[0]
    nb = rows // ATT_BLOCK
    scale = HEAD_DIM ** -0.5

    def body(q_ref, kc_ref, kp_ref, vc_ref, vp_ref, o_ref, lse_ref):
        has_prev = pl.program_id(1) > 0
        m_cur, m_prev = _att_masks()
        m_prev = m_prev & has_prev
        hs = range(HEADS)
        sls = [slice(h * HEAD_DIM, (h + 1) * HEAD_DIM) for h in hs]
        qs = [q_ref[:, sl] for sl in sls]
        s_c = [jnp.where(m_cur, _mxu(qs[h], kc_ref[:, sls[h]], _NT) * scale, -jnp.inf) for h in hs]
        s_p = [jnp.where(m_prev, _mxu(qs[h], kp_ref[:, sls[h]], _NT) * scale, -jnp.inf) for h in hs]
        m = [jnp.max(jnp.maximum(s_c[h], s_p[h]), axis=1, keepdims=True) for h in hs]
        p_c = [jnp.exp(s_c[h] - m[h]) for h in hs]
        p_p = [jnp.exp(s_p[h] - m[h]) for h in hs]
        den = [jnp.sum(p_c[h] + p_p[h], axis=1, keepdims=True) for h in hs]
        o = [_mxu(p_c[h], vc_ref[:, sls[h]], _NN) + _mxu(p_p[h], vp_ref[:, sls[h]], _NN) for h in hs]
        lane = _iota((1, 128), 1)
        lse = jnp.zeros((ATT_BLOCK, 128), F32)
        for h in hs:
            o_ref[:, sls[h]] = o[h] / den[h]
            lse = jnp.where(lane == h, m[h] + jnp.log(den[h]), lse)
        lse_ref[...] = lse

    def cur(c):
        return pl.BlockSpec((ATT_BLOCK, WIDTH), lambda r, i: (i, 3 * r + c))

    def prev(c):
        return pl.BlockSpec((ATT_BLOCK, WIDTH), lambda r, i: (jnp.maximum(i - 1, 0), 3 * r + c))

    return pl.pallas_call(
        body, name=name, grid=(d, nb), in_specs=[cur(0), cur(1), prev(1), cur(2), prev(2)],
        out_specs=[pl.BlockSpec((ATT_BLOCK, WIDTH), lambda r, i: (i, r)), pl.BlockSpec((ATT_BLOCK, 128), lambda r, i: (i, r))],
        out_shape=[jax.ShapeDtypeStruct((rows, d * WIDTH), F32), jax.ShapeDtypeStruct((rows, d * 128), F32)],
        compiler_params=_params(("parallel", "arbitrary")),
    )(qkv, qkv, qkv, qkv, qkv)


def _att_bwd(qkv, do, lse, delta, d, name):
    rows = qkv.shape[0]
    nb = rows // ATT_BLOCK
    scale = HEAD_DIM ** -0.5

    def body(q_ref, kc_ref, kp_ref, vc_ref, vp_ref, do_ref, l_ref, dl_ref, dq_ref, dkv_ref, own):
        i = pl.program_id(1)
        m_cur, m_prev = _att_masks()
        m_prev = m_prev & (i > 0)

        hs = range(HEADS)
        sls = [slice(h * HEAD_DIM, (h + 1) * HEAD_DIM) for h in hs]
        col = [slice(h, h + 1) for h in hs]

        def probs(k_r, mask):
            s = [_mxu(q_ref[:, sls[h]], k_r[:, sls[h]], _NT) for h in hs]
            return [jnp.where(mask, jnp.exp(s[h] * scale - l_ref[:, col[h]]), 0.0) for h in hs]

        def dscores(p, v_r):
            dp = [_mxu(do_ref[:, sls[h]], v_r[:, sls[h]], _NT) for h in hs]
            return [(p[h] * (dp[h] - dl_ref[:, col[h]])).astype(MXU_DTYPE) for h in hs]

        @pl.when(i == 0)
        def _():
            own[...] = jnp.zeros_like(own)

        @pl.when(i < nb)
        def _():
            p_c = probs(kc_ref, m_cur)
            ds_c = dscores(p_c, vc_ref)
            p_p = probs(kp_ref, m_prev)
            ds_p = dscores(p_p, vp_ref)
            dq = [_mxu(ds_c[h], kc_ref[:, sls[h]], _NN) + _mxu(ds_p[h], kp_ref[:, sls[h]], _NN) for h in hs]
            dk_p = [_mxu(ds_p[h], q_ref[:, sls[h]], _TN) for h in hs]
            dv_p = [_mxu(p_p[h], do_ref[:, sls[h]], _TN) for h in hs]
            dk_c = [_mxu(ds_c[h], q_ref[:, sls[h]], _TN) for h in hs]
            dv_c = [_mxu(p_c[h], do_ref[:, sls[h]], _TN) for h in hs]
            for h in hs:
                vs = slice(WIDTH + h * HEAD_DIM, WIDTH + (h + 1) * HEAD_DIM)
                dq_ref[:, sls[h]] = dq[h] * scale
                dkv_ref[:, sls[h]] = own[:, sls[h]] + dk_p[h] * scale
                dkv_ref[:, vs] = own[:, vs] + dv_p[h]
                own[:, sls[h]] = dk_c[h] * scale
                own[:, vs] = dv_c[h]

        @pl.when(i == nb)
        def _():
            dkv_ref[...] = own[...]

    def qkv_spec(c, shift):
        return pl.BlockSpec((ATT_BLOCK, WIDTH), lambda r, i: (jnp.clip(i + shift, 0, nb - 1), 3 * r + c))

    tok = pl.BlockSpec((ATT_BLOCK, WIDTH), lambda r, i: (jnp.minimum(i, nb - 1), r))
    per_head = pl.BlockSpec((ATT_BLOCK, 128), lambda r, i: (jnp.minimum(i, nb - 1), r))
    return pl.pallas_call(
        body, name=name, grid=(d, nb + 1),
        in_specs=[qkv_spec(0, 0), qkv_spec(1, 0), qkv_spec(1, -1), qkv_spec(2, 0), qkv_spec(2, -1), tok, per_head, per_head],
        out_specs=[tok, pl.BlockSpec((ATT_BLOCK, 2 * WIDTH), lambda r, i: (jnp.maximum(i - 1, 0), r))],
        out_shape=[jax.ShapeDtypeStruct((rows, d * WIDTH), F32), jax.ShapeDtypeStruct((rows, d * 2 * WIDTH), F32)],
        scratch_shapes=[pltpu.VMEM((ATT_BLOCK, 2 * WIDTH), F32)],
        compiler_params=_params(("parallel", "arbitrary")),
    )(qkv, qkv, qkv, qkv, qkv, do, lse, delta)


def _att_merge(os_, lses, proj, tm=1024):
    t = proj.shape[0]

    def body(o0, o1, o2, l0, l1, l2, z_ref, oz_ref, o_ref, t0, t1, t2, s_o1, s_o2, s_l1, s_l2, s_t):
        _from_strided_view(o1, s_o1, DILATIONS[1])
        _from_strided_view(o2, s_o2, DILATIONS[2])
        _from_strided_view(l1, s_l1, DILATIONS[1])
        _from_strided_view(l2, s_l2, DILATIONS[2])
        a, b, c = l0[...], s_l1[0], s_l2[0]
        m = jnp.maximum(jnp.maximum(a, b), c)
        wa, wb, wc = jnp.exp(a - m), jnp.exp(b - m), jnp.exp(c - m)
        den = wa + wb + wc
        total = m + jnp.log(den)
        t0[...] = total
        s_t[0] = total
        _to_strided_view(s_t, t1, DILATIONS[1])
        _to_strided_view(s_t, t2, DILATIONS[2])
        spread = jnp.where(_iota((128, WIDTH), 1) // HEAD_DIM == _iota((128, WIDTH), 0), 1.0, 0.0).astype(MXU_DTYPE)
        ra, rb, rc = (_pair_sum_lanes(w / den, spread) for w in (wa, wb, wc))
        o = ra * o0[...] + rb * _load_tile(s_o1) + rc * _load_tile(s_o2)
        z = z_ref[...]
        o_ref[...] = o
        oz_ref[...] = (o * z * _sigmoid(z)).astype(oz_ref.dtype)

    tok = pl.BlockSpec((tm, WIDTH), lambda i: (i, 0))
    views = [_strided_spec(tm, d, WIDTH) for d in DILATIONS]
    per_head = [_strided_spec(tm, d, 128) for d in DILATIONS]
    return pl.pallas_call(
        body, name="att_merge", grid=(t // tm,),
        in_specs=views + per_head + [pl.BlockSpec((tm, WIDTH), lambda i: (i, SEG_ZA // WIDTH))],
        out_specs=[tok, tok] + per_head,
        out_shape=[jax.ShapeDtypeStruct((t, WIDTH), MXU_DTYPE), jax.ShapeDtypeStruct((t, WIDTH), F32)]
        + [jax.ShapeDtypeStruct((t // d, d * 128), F32) for d in DILATIONS],
        scratch_shapes=[_tile_scratch(tm, WIDTH)] * 2 + [_tile_scratch(tm, 128)] * 3,
        compiler_params=_params(("parallel",)),
    )(*os_, *lses, proj)


def _att_merge_bwd(dproj, d_oz, o, proj, tm=1024):
    t = proj.shape[0]

    def body(dp_ref, doz_ref, o_ref, z_ref, dz_ref, do0, do1, do2, dl0, dl1, dl2, s_do, s_dl):
        z, ov, g = z_ref[...], o_ref[...], doz_ref[...]
        sg = _sigmoid(z)
        do = g * z * sg
        dz_ref[...] = (g * ov * sg * (1.0 + z * (1.0 - sg))).astype(dz_ref.dtype)
        do0[...] = do.astype(do0.dtype)
        _store_tile(s_do, do)
        prod = do * ov
        delta = jnp.zeros((tm, 128), F32)
        for p in range(HEADS // 2):
            delta += _pair_sum_lanes(prod[:, p * 128:(p + 1) * 128], _pair_to_lanes(p, 0))
        dl0[...] = delta
        s_dl[0] = delta
        for d, do_v, dl_v in ((DILATIONS[1], do1, dl1), (DILATIONS[2], do2, dl2)):
            _to_strided_view(s_do, do_v, d)
            _to_strided_view(s_dl, dl_v, d)

    tok = pl.BlockSpec((tm, WIDTH), lambda i: (i, 0))
    seg = pl.BlockSpec((tm, WIDTH), lambda i: (i, SEG_ZA // WIDTH))
    views = [_strided_spec(tm, d, WIDTH) for d in DILATIONS]
    per_head = [_strided_spec(tm, d, 128) for d in DILATIONS]
    return pl.pallas_call(
        body, name="att_merge_bwd", grid=(t // tm,),
        in_specs=[pl.BlockSpec(memory_space=pl.ANY), tok, tok, seg],
        out_specs=[seg] + views + per_head,
        out_shape=[jax.ShapeDtypeStruct((t, PACKED_WIDTH), MXU_DTYPE)]
        + [jax.ShapeDtypeStruct((t // d, d * WIDTH), MXU_DTYPE) for d in DILATIONS]
        + [jax.ShapeDtypeStruct((t // d, d * 128), F32) for d in DILATIONS],
        scratch_shapes=[_tile_scratch(tm, WIDTH), _tile_scratch(tm, 128)],
        input_output_aliases={0: 0},
        compiler_params=_params(("parallel",)),
    )(dproj, d_oz, o, proj)


def _shift_down(x, halo, s):
    if s == 0:
        return x
    xs = pltpu.roll(x, s, 0)
    head = jnp.where(_iota((8, x.shape[1]), 0) < s, pltpu.roll(halo, s, 0), xs[0:8])
    return jnp.concatenate([head, xs[8:]], axis=0)


def _shift_up(x, nxt, s):
    if s == 0:
        return x
    n = x.shape[0]
    xs = pltpu.roll(x, n - s, 0)
    tail = jnp.where(_iota((8, x.shape[1]), 0) >= 8 - s, pltpu.roll(nxt, 8 - s, 0), xs[n - 8:])
    return jnp.concatenate([xs[:n - 8], tail], axis=0)


def _conv_fwd(proj, conv_w, tm=1024):
    t = proj.shape[0]
    cb = SEG_B // WIDTH

    def body(x_ref, halo_ref, w_ref, c_ref):
        halo = jnp.where(pl.program_id(0) > 0, halo_ref[...], 0.0)
        x = x_ref[...]
        w = w_ref[...]
        acc = jnp.zeros((tm, WIDTH), F32)
        for j in range(GDN_CONV):
            acc += _shift_down(x, halo, GDN_CONV - 1 - j) * w[j:j + 1, :]
        c_ref[...] = acc

    return pl.pallas_call(
        body, name="conv_fwd", grid=(t // tm, 3),
        in_specs=[pl.BlockSpec((tm, WIDTH), lambda i, c: (i, cb + c)),
                  pl.BlockSpec((8, WIDTH), lambda i, c: (jnp.maximum(i * (tm // 8) - 1, 0), cb + c)),
                  pl.BlockSpec((GDN_CONV, WIDTH), lambda i, c: (0, c))],
        out_specs=pl.BlockSpec((tm, WIDTH), lambda i, c: (i, c)),
        out_shape=jax.ShapeDtypeStruct((t, 3 * WIDTH), F32),
        compiler_params=_params(("parallel", "parallel")),
    )(proj, proj, conv_w)


def _conv_bwd(dproj, dc, proj, conv_w, tm=1024):
    t = proj.shape[0]
    cb = SEG_B // WIDTH
    nt = t // tm

    def body(dp_ref, dc_ref, dcn_ref, x_ref, halo_ref, w_ref, dx_ref, dw_ref):
        i = pl.program_id(1)
        w = w_ref[...]
        dcn = jnp.where(i < nt - 1, dcn_ref[...], 0.0)
        dcv = dc_ref[...]
        acc = jnp.zeros((tm, WIDTH), F32)
        for j in range(GDN_CONV):
            acc += _shift_up(dcv, dcn, GDN_CONV - 1 - j) * w[j:j + 1, :]
        dx_ref[...] = acc.astype(dx_ref.dtype)
        halo = jnp.where(i > 0, halo_ref[...], 0.0)
        x = x_ref[...]
        row8 = _iota((8, WIDTH), 0)
        part = jnp.zeros((8, WIDTH), F32)
        for j in range(GDN_CONV):
            s = jnp.sum(dcv * _shift_down(x, halo, GDN_CONV - 1 - j), axis=0, keepdims=True)
            part += jnp.where(row8 == j, s, 0.0)

        @pl.when(i == 0)
        def _():
            dw_ref[...] = jnp.zeros_like(dw_ref)

        dw_ref[...] += part

    return pl.pallas_call(
        body, name="conv_bwd", grid=(3, nt),
        in_specs=[pl.BlockSpec(memory_space=pl.ANY),
                  pl.BlockSpec((tm, WIDTH), lambda c, i: (i, c)),
                  pl.BlockSpec((8, WIDTH), lambda c, i: (jnp.minimum((i + 1) * (tm // 8), t // 8 - 1), c)),
                  pl.BlockSpec((tm, WIDTH), lambda c, i: (i, cb + c)),
                  pl.BlockSpec((8, WIDTH), lambda c, i: (jnp.maximum(i * (tm // 8) - 1, 0), cb + c)),
                  pl.BlockSpec((GDN_CONV, WIDTH), lambda c, i: (0, c))],
        out_specs=[pl.BlockSpec((tm, WIDTH), lambda c, i: (i, cb + c)),
                   pl.BlockSpec((8, WIDTH), lambda c, i: (0, c))],
        out_shape=[jax.ShapeDtypeStruct((t, PACKED_WIDTH), MXU_DTYPE), jax.ShapeDtypeStruct((8, 3 * WIDTH), F32)],
        input_output_aliases={0: 0},
        compiler_params=_params(("parallel", "arbitrary")),
    )(dproj, dc, dc, proj, proj, conv_w)


def _chunk_matrices(tm):
    r, c = _iota((tm, tm), 0), _iota((tm, tm), 1)
    same = (r // GDN_CHUNK) == (c // GDN_CHUNK)
    return jnp.where(same & (c <= r), 1.0, 0.0), jnp.where(same, 1.0, 0.0)


def _gdn_gates(ba, a_log, dt_bias):
    al = ba + dt_bias
    return _sigmoid(ba), -jnp.exp(a_log) * _softplus(al), _sigmoid(al)


SCAN_HEADS = (0, 2, 4, 6, 1, 3, 5, 7)


def _head_lane_eye():
    return jnp.where(_iota((HEADS, 128), 1) == _iota((HEADS, 128), 0) + HEADS, 1.0, 0.0)


def _pair_ones():
    return jnp.where(_iota((128, 128), 0) // HEAD_DIM == _iota((128, 128), 1) // HEAD_DIM, 1.0, 0.0).astype(MXU_DTYPE)


def _pair_to_lanes(p, base):
    return jnp.where(_iota((128, 128), 1) == base + 2 * p + _iota((128, 128), 0) // HEAD_DIM, 1.0, 0.0).astype(MXU_DTYPE)


def _pair_sum_lanes(x, sel):
    hi, lo = _split(x)
    return jnp.dot(hi, sel, preferred_element_type=F32) + jnp.dot(lo, sel, preferred_element_type=F32)


def _gdn_prep(conv, proj, a_log, dt_bias, tm=512):
    t = proj.shape[0]
    nc = tm // GDN_CHUNK

    def body(c_ref, ba_ref, al_ref, dt_ref, q_ref, k_ref, v_ref, b_ref, g_ref, gl_ref, grow_ref):
        beta, g, _ = _gdn_gates(ba_ref[:, 0:128], al_ref[...], dt_ref[...])
        lmat, cmat = _chunk_matrices(tm)
        gc = jnp.dot(lmat, g, precision=HIGHEST, preferred_element_type=F32)
        gl = jnp.dot(cmat, g, precision=HIGHEST, preferred_element_type=F32)
        grow = lax.dot_general(_head_lane_eye(), gc, _NT, precision=HIGHEST, preferred_element_type=F32)
        ones = _pair_ones()
        first = _iota((1, 128), 1) < HEAD_DIM

        def spread(x, base, p):
            return jnp.where(first, x[:, base + 2 * p:base + 2 * p + 1], x[:, base + 2 * p + 1:base + 2 * p + 2])

        for p in range(HEADS // 2):
            for seg, ref, scale in ((0, q_ref, HEAD_DIM ** -0.5), (1, k_ref, 1.0), (2, v_ref, None)):
                c = c_ref[:, seg * WIDTH + p * 128:seg * WIDTH + (p + 1) * 128]
                a = c * _sigmoid(c)
                if scale is not None:
                    a = a * (lax.rsqrt(_pair_sum_lanes(a * a, ones) + NORM_EPS) * scale)
                ref[p] = a
            b_ref[p] = spread(beta, 0, p)
            g_ref[p] = spread(gc, HEADS, p)
            gl_ref[p] = spread(gl, HEADS, p)
        for pos, h in enumerate(SCAN_HEADS):
            for cc in range(nc):
                grow_ref[pos, cc] = grow[h:h + 1, cc * GDN_CHUNK:(cc + 1) * GDN_CHUNK]

    hm = pl.BlockSpec((HEADS // 2, tm, 128), lambda i: (0, i, 0))
    small = pl.BlockSpec((1, 128), lambda i: (0, 0))
    hm_shape = jax.ShapeDtypeStruct((HEADS // 2, t, 128), F32)
    return pl.pallas_call(
        body, name="gdn_prep", grid=(t // tm,),
        in_specs=[pl.BlockSpec((tm, 3 * WIDTH), lambda i: (i, 0)),
                  pl.BlockSpec((tm, WIDTH), lambda i: (i, SEG_BA // WIDTH)), small, small],
        out_specs=[hm] * 6 + [pl.BlockSpec((HEADS, nc, 1, GDN_CHUNK), lambda i: (0, i, 0, 0))],
        out_shape=[hm_shape] * 6 + [jax.ShapeDtypeStruct((HEADS, t // GDN_CHUNK, 1, GDN_CHUNK), F32)],
        compiler_params=_params(("parallel",)),
    )(conv, proj, a_log, dt_bias)


def _gdn_prep_bwd(dproj, conv, proj, a_log, dt_bias, dq, dk, dv, db, dg, dgl, dgrow, tm=512):
    t = proj.shape[0]
    nc = tm // GDN_CHUNK

    def body(dp_ref, c_ref, ba_ref, al_ref, dt_ref, dq_ref, dk_ref, dv_ref, db_ref, dg_ref, dgl_ref, dgrow_ref,
             dba_ref, dc_ref, small_ref, row_scr):
        beta, g, sig_al = _gdn_gates(ba_ref[:, 0:128], al_ref[...], dt_ref[...])
        d_beta = jnp.zeros((tm, 128), F32)
        d_gc = jnp.zeros((tm, 128), F32)
        d_gl = jnp.zeros((tm, 128), F32)
        ones = _pair_ones()
        for pos, h in enumerate(SCAN_HEADS):
            for cc in range(nc):
                row_scr[h:h + 1, cc * GDN_CHUNK:(cc + 1) * GDN_CHUNK] = dgrow_ref[pos, cc]
        for p in range(HEADS // 2):
            d_beta += _pair_sum_lanes(db_ref[p], _pair_to_lanes(p, 0))
            d_gc += _pair_sum_lanes(dg_ref[p], _pair_to_lanes(p, HEADS))
            d_gl += _pair_sum_lanes(dgl_ref[p], _pair_to_lanes(p, HEADS))
            for seg, ref, scale in ((0, dq_ref, HEAD_DIM ** -0.5), (1, dk_ref, 1.0), (2, dv_ref, None)):
                cols = slice(seg * WIDTH + p * 128, seg * WIDTH + (p + 1) * 128)
                c = c_ref[:, cols]
                sg = _sigmoid(c)
                da = ref[p]
                if scale is not None:
                    a = c * sg
                    r = lax.rsqrt(_pair_sum_lanes(a * a, ones) + NORM_EPS)
                    da = da * scale
                    da = r * da - a * (r * r * r) * _pair_sum_lanes(da * a, ones)
                dc_ref[:, cols] = da * (sg * (1.0 + c * (1.0 - sg)))
        d_gc += lax.dot_general(row_scr[...], _head_lane_eye(), _TN, precision=HIGHEST, preferred_element_type=F32)
        lmat, cmat = _chunk_matrices(tm)
        d_g = (lax.dot_general(lmat, d_gc, _TN, precision=HIGHEST, preferred_element_type=F32)
               + lax.dot_general(cmat, d_gl, _TN, precision=HIGHEST, preferred_element_type=F32))
        d_al = d_g * (-jnp.exp(al_ref[...])) * sig_al
        d_bl = d_beta * beta * (1.0 - beta)
        dba_ref[...] = jnp.concatenate([d_bl + d_al, jnp.zeros((tm, WIDTH - 128), F32)], axis=1).astype(dba_ref.dtype)
        row8 = _iota((8, 128), 0)
        part = (jnp.where(row8 == 0, jnp.sum(d_g * g, axis=0, keepdims=True), 0.0)
                + jnp.where(row8 == 1, jnp.sum(d_al, axis=0, keepdims=True), 0.0))

        @pl.when(pl.program_id(0) == 0)
        def _():
            small_ref[...] = jnp.zeros_like(small_ref)

        small_ref[...] += part

    hm = pl.BlockSpec((HEADS // 2, tm, 128), lambda i: (0, i, 0))
    small = pl.BlockSpec((1, 128), lambda i: (0, 0))
    seg = pl.BlockSpec((tm, WIDTH), lambda i: (i, SEG_BA // WIDTH))
    return pl.pallas_call(
        body, name="gdn_prep_bwd", grid=(t // tm,),
        in_specs=[pl.BlockSpec(memory_space=pl.ANY), pl.BlockSpec((tm, 3 * WIDTH), lambda i: (i, 0)), seg, small, small]
        + [hm] * 6 + [pl.BlockSpec((HEADS, nc, 1, GDN_CHUNK), lambda i: (0, i, 0, 0))],
        out_specs=[seg, pl.BlockSpec((tm, 3 * WIDTH), lambda i: (i, 0)), pl.BlockSpec((8, 128), lambda i: (0, 0))],
        out_shape=[jax.ShapeDtypeStruct((t, PACKED_WIDTH), MXU_DTYPE), jax.ShapeDtypeStruct((t, 3 * WIDTH), F32),
                   jax.ShapeDtypeStruct((8, 128), F32)],
        scratch_shapes=[pltpu.VMEM((HEADS, tm), F32)],
        input_output_aliases={0: 0},
        compiler_params=_params(("arbitrary",)),
    )(dproj, conv, proj, a_log, dt_bias, dq, dk, dv, db, dg, dgl, dgrow)


_BNN = (((2,), (1,)), ((0,), (0,)))
_BNT = (((2,), (2,)), ((0,), (0,)))
_BTN = (((1,), (1,)), ((0,), (0,)))


@jax.custom_vjp
def _MM_NN(a, b):
    return _mxu(a, b, _BNN)


@jax.custom_vjp
def _MM_NT(a, b):
    return _mxu(a, b, _BNT)


@jax.custom_vjp
def _MM_TN(a, b):
    return _mxu(a, b, _BTN)


_MM_NN.defvjp(lambda a, b: (_mxu(a, b, _BNN), (a, b)), lambda r, g: (_mxu(g, r[1], _BNT), _mxu(r[0], g, _BTN)))
_MM_NT.defvjp(lambda a, b: (_mxu(a, b, _BNT), (a, b)), lambda r, g: (_mxu(g, r[1], _BNN), _mxu(g, r[0], _BTN)))
_MM_TN.defvjp(lambda a, b: (_mxu(a, b, _BTN), (a, b)), lambda r, g: (_mxu(r[1], g, _BNT), _mxu(r[0], g, _BNN)))


def _split(a):
    hi = a.astype(MXU_DTYPE)
    return hi, (a - hi.astype(F32)).astype(MXU_DTYPE)


def _dot3(a, b, dims):
    (ah, al), (bh, bl) = a, b
    (ca,), (cb,) = dims[0]
    return lax.dot_general(jnp.concatenate([ah, ah, al], axis=ca), jnp.concatenate([bh, bl, bh], axis=cb), dims,
                           preferred_element_type=F32)


def _unit_lower_inverse(a):
    c = GDN_CHUNK
    eye = jnp.where(_iota((c, c), 0) == _iota((c, c), 1), 1.0, 0.0)
    x = eye - a
    p = a
    for _ in range(5):
        ps = _split(p)
        p = _dot3(ps, ps, _BNN)
        x = x + _dot3(_split(x), _split(p), _BNN)
    return x


@jax.custom_vjp
def _SAVED_INVERSE(a, t_inv):
    return t_inv


def _saved_inverse_bwd(t_inv, g):
    ts = _split(t_inv)
    return -_dot3(ts, _split(_dot3(_split(g), ts, _BNT)), _BTN), jnp.zeros_like(t_inv)


_SAVED_INVERSE.defvjp(lambda a, t_inv: (t_inv, t_inv), _saved_inverse_bwd)


def _gdn_chunk(q, k, v, beta, g1, g2, gl, state, t_inv=None):
    c = GDN_CHUNK
    if t_inv is None:
        _mm_nn, _mm_nt, _mm_tn = (functools.partial(_mxu, dims=dd) for dd in (_BNN, _BNT, _BTN))
    else:
        _mm_nn, _mm_nt, _mm_tn = _MM_NN, _MM_NT, _MM_TN
    row, col = _iota((c, c), 0), _iota((c, c), 1)
    incl, strict = row >= col, row > col
    decay = jnp.where(incl, jnp.exp(jnp.where(incl, g1 - g2, 0.0)), 0.0)
    eg = jnp.exp(g1)
    kb = k * beta
    a = _mm_nt(kb, k) * jnp.where(strict, decay, 0.0)
    inv = _unit_lower_inverse(a) if t_inv is None else _SAVED_INVERSE(a, t_inv)
    u = _mm_nn(inv, v * beta)
    w = _mm_nn(inv, kb * eg)
    attn = _mm_nt(q, k) * decay
    v_new = u - _mm_nn(w, state)
    o = _mm_nn(q * eg, state) + _mm_nn(attn, v_new)
    new_state = state * jnp.exp(gl) + _mm_tn(k * jnp.exp(gl - g1), v_new)
    return (o, new_state, inv) if t_inv is None else (o, new_state)


def _unpair(x):
    return jnp.concatenate([x[..., :HEAD_DIM], x[..., HEAD_DIM:]], axis=0)


def _gdn_fwd(q, k, v, beta, g, gl, grow, cpb=4):
    t = q.shape[1]
    rows = cpb * GDN_CHUNK
    lo, hi = slice(0, HEAD_DIM), slice(HEAD_DIM, 2 * HEAD_DIM)

    def body(q_ref, k_ref, v_ref, b_ref, g_ref, gl_ref, grow_ref, o_ref, st_ref, inv_ref, state):
        @pl.when(pl.program_id(0) == 0)
        def _():
            state[...] = jnp.zeros_like(state)

        s = state[...]
        for cc in range(cpb):
            sl = slice(cc * GDN_CHUNK, (cc + 1) * GDN_CHUNK)
            st_ref[:, cc, :, lo], st_ref[:, cc, :, hi] = s[:HEADS // 2], s[HEADS // 2:]
            g2 = jnp.broadcast_to(grow_ref[:, cc], (HEADS, GDN_CHUNK, GDN_CHUNK))
            o, s, inv = _gdn_chunk(*[_unpair(r[:, sl, :]) for r in (q_ref, k_ref, v_ref, b_ref, g_ref)], g2,
                                   _unpair(gl_ref[:, sl, :]), s)
            o_ref[:, sl, lo], o_ref[:, sl, hi] = o[:HEADS // 2], o[HEADS // 2:]
            inv_ref[:, cc, :, lo], inv_ref[:, cc, :, hi] = inv[:HEADS // 2], inv[HEADS // 2:]
        state[...] = s

    hm = pl.BlockSpec((HEADS // 2, rows, 128), lambda i: (0, i, 0))
    per_chunk = pl.BlockSpec((HEADS // 2, cpb, GDN_CHUNK, 128), lambda i: (0, i, 0, 0))
    chunk_shape = jax.ShapeDtypeStruct((HEADS // 2, t // GDN_CHUNK, GDN_CHUNK, 128), F32)
    return pl.pallas_call(
        body, name="gdn_fwd", grid=(t // rows,),
        in_specs=[hm] * 6 + [pl.BlockSpec((HEADS, cpb, 1, GDN_CHUNK), lambda i: (0, i, 0, 0))],
        out_specs=[hm, per_chunk, per_chunk],
        out_shape=[jax.ShapeDtypeStruct((HEADS // 2, t, 128), F32), chunk_shape, chunk_shape],
        scratch_shapes=[pltpu.VMEM((HEADS, GDN_CHUNK, HEAD_DIM), F32)],
        compiler_params=_params(("arbitrary",)),
    )(q, k, v, beta, g, gl, grow)


def _gdn_bwd(q, k, v, beta, g, gl, grow, states, invs, do, cpb=2):
    t = q.shape[1]
    rows = cpb * GDN_CHUNK
    nsteps = t // rows
    lo, hi = slice(0, HEAD_DIM), slice(HEAD_DIM, 2 * HEAD_DIM)

    def body(q_ref, k_ref, v_ref, b_ref, g_ref, gl_ref, grow_ref, st_ref, inv_ref, do_ref,
             dq_ref, dk_ref, dv_ref, db_ref, dg_ref, dgl_ref, dgrow_ref, dstate):
        @pl.when(pl.program_id(0) == 0)
        def _():
            dstate[...] = jnp.zeros_like(dstate)

        ds = dstate[...]
        for cc in reversed(range(cpb)):
            sl = slice(cc * GDN_CHUNK, (cc + 1) * GDN_CHUNK)
            g2 = jnp.broadcast_to(grow_ref[:, cc], (HEADS, GDN_CHUNK, GDN_CHUNK))
            _, vjp = jax.vjp(_gdn_chunk, *[_unpair(r[:, sl, :]) for r in (q_ref, k_ref, v_ref, b_ref, g_ref)], g2,
                             _unpair(gl_ref[:, sl, :]), _unpair(st_ref[:, cc]), _unpair(inv_ref[:, cc]))
            gq, gk, gv, gb, gg1, gg2, ggl, ds, _ = vjp((_unpair(do_ref[:, sl, :]), ds))
            for ref, val in ((dq_ref, gq), (dk_ref, gk), (dv_ref, gv), (db_ref, gb), (dg_ref, gg1), (dgl_ref, ggl)):
                ref[:, sl, lo], ref[:, sl, hi] = val[:HEADS // 2], val[HEADS // 2:]
            dgrow_ref[:, cc] = jnp.sum(gg2, axis=1, keepdims=True)
        dstate[...] = ds

    hm = pl.BlockSpec((HEADS // 2, rows, 128), lambda i: (0, nsteps - 1 - i, 0))
    rowspec = pl.BlockSpec((HEADS, cpb, 1, GDN_CHUNK), lambda i: (0, nsteps - 1 - i, 0, 0))
    per_chunk = pl.BlockSpec((HEADS // 2, cpb, GDN_CHUNK, 128), lambda i: (0, nsteps - 1 - i, 0, 0))
    hm_shape = jax.ShapeDtypeStruct((HEADS // 2, t, 128), F32)
    return pl.pallas_call(
        body, name="gdn_bwd", grid=(nsteps,),
        in_specs=[hm] * 6 + [rowspec, per_chunk, per_chunk, hm],
        out_specs=[hm] * 6 + [rowspec],
        out_shape=[hm_shape] * 6 + [jax.ShapeDtypeStruct((HEADS, t // GDN_CHUNK, 1, GDN_CHUNK), F32)],
        scratch_shapes=[pltpu.VMEM((HEADS, GDN_CHUNK, HEAD_DIM), F32)],
        compiler_params=_params(("arbitrary",)),
    )(q, k, v, beta, g, gl, grow, states, invs, do)


def _gdn_out(o_hm, gdn_norm_w, proj, tm=1024):
    t = proj.shape[0]

    def body(o_ref, w_ref, z_ref, oz_ref):
        w = w_ref[...]
        ones = _pair_ones()
        for p in range(HEADS // 2):
            cols = slice(p * 128, (p + 1) * 128)
            o = o_ref[p]
            z = z_ref[:, cols]
            r = lax.rsqrt(_pair_sum_lanes(o * o, ones) * (1.0 / HEAD_DIM) + NORM_EPS)
            oz_ref[:, cols] = (o * r * w * (z * _sigmoid(z))).astype(oz_ref.dtype)

    tok = pl.BlockSpec((tm, WIDTH), lambda i: (i, 0))
    return pl.pallas_call(
        body, name="gdn_out", grid=(t // tm,),
        in_specs=[pl.BlockSpec((HEADS // 2, tm, 128), lambda i: (0, i, 0)), pl.BlockSpec((1, 128), lambda i: (0, 0)),
                  pl.BlockSpec((tm, WIDTH), lambda i: (i, SEG_ZB // WIDTH))],
        out_specs=tok, out_shape=jax.ShapeDtypeStruct((t, WIDTH), MXU_DTYPE),
        compiler_params=_params(("parallel",)),
    )(o_hm, jnp.tile(gdn_norm_w, (1, 2)), proj)


def _gdn_out_bwd(dproj, d_oz, o_hm, gdn_norm_w, proj, tm=1024):
    t = proj.shape[0]

    def body(dp_ref, doz_ref, o_ref, w_ref, z_ref, dz_ref, do_ref, dw_ref):
        w = w_ref[...]
        ones = _pair_ones()
        dw = jnp.zeros((1, 128), F32)
        for p in range(HEADS // 2):
            cols = slice(p * 128, (p + 1) * 128)
            o = o_ref[p]
            z, g = z_ref[:, cols], doz_ref[:, cols]
            sg = _sigmoid(z)
            r = lax.rsqrt(_pair_sum_lanes(o * o, ones) * (1.0 / HEAD_DIM) + NORM_EPS)
            dz_ref[:, cols] = (g * (o * r * w) * (sg * (1.0 + z * (1.0 - sg)))).astype(dz_ref.dtype)
            dn = g * (z * sg)
            dw += jnp.sum(dn * o * r, axis=0, keepdims=True)
            dnw = dn * w
            do_ref[p] = r * dnw - o * (r * r * r) * (_pair_sum_lanes(dnw * o, ones) * (1.0 / HEAD_DIM))

        @pl.when(pl.program_id(0) == 0)
        def _():
            dw_ref[...] = jnp.zeros_like(dw_ref)

        dw_ref[...] += jnp.where(_iota((8, 128), 0) == 0, dw, 0.0)

    tok = pl.BlockSpec((tm, WIDTH), lambda i: (i, 0))
    seg = pl.BlockSpec((tm, WIDTH), lambda i: (i, SEG_ZB // WIDTH))
    hm = pl.BlockSpec((HEADS // 2, tm, 128), lambda i: (0, i, 0))
    dz, do, dw = pl.pallas_call(
        body, name="gdn_out_bwd", grid=(t // tm,),
        in_specs=[pl.BlockSpec(memory_space=pl.ANY), tok, hm, pl.BlockSpec((1, 128), lambda i: (0, 0)), seg],
        out_specs=[seg, hm, pl.BlockSpec((8, 128), lambda i: (0, 0))],
        out_shape=[jax.ShapeDtypeStruct((t, PACKED_WIDTH), MXU_DTYPE), jax.ShapeDtypeStruct((HEADS // 2, t, 128), F32),
                   jax.ShapeDtypeStruct((8, 128), F32)],
        input_output_aliases={0: 0},
        compiler_params=_params(("arbitrary",)),
    )(dproj, d_oz, o_hm, jnp.tile(gdn_norm_w, (1, 2)), proj)
    return dz, do, dw[:, :HEAD_DIM] + dw[:, HEAD_DIM:]


def _merge(y_a, y_b, proj, tm=1024):
    t = proj.shape[0]

    def body(ya_ref, yb_ref, ga_ref, gb_ref, m_ref):
        m_ref[...] = (_sigmoid(ga_ref[...]) * ya_ref[...] + _sigmoid(gb_ref[...]) * yb_ref[...]).astype(m_ref.dtype)

    half = pl.BlockSpec((tm, WIDTH), lambda i, c: (i, c))
    return pl.pallas_call(
        body, name="merge", grid=(t // tm, 2),
        in_specs=[half, half, pl.BlockSpec((tm, WIDTH), lambda i, c: (i, SEG_GA // WIDTH + c)),
                  pl.BlockSpec((tm, WIDTH), lambda i, c: (i, SEG_GB // WIDTH + c))],
        out_specs=half, out_shape=jax.ShapeDtypeStruct((t, D_MODEL), MXU_DTYPE),
        compiler_params=_params(("parallel", "parallel")),
    )(y_a, y_b, proj, proj)


def _merge_bwd(dproj, d_m, y, proj, seg, name, tm=1024):
    t = proj.shape[0]

    def body(*refs):
        dm_ref, y_ref, g_ref, dg_ref, dy_ref = refs[-5:]
        dm = dm_ref[...]
        s = _sigmoid(g_ref[...])
        dy_ref[...] = (dm * s).astype(dy_ref.dtype)
        dg_ref[...] = (dm * y_ref[...] * s * (1.0 - s)).astype(dg_ref.dtype)

    half = pl.BlockSpec((tm, WIDTH), lambda i, c: (i, c))
    gate = pl.BlockSpec((tm, WIDTH), lambda i, c: (i, seg // WIDTH + c))
    specs, args, aliases = [half, half, gate], [d_m, y, proj], {}
    if dproj is not None:
        specs, args, aliases = [pl.BlockSpec(memory_space=pl.ANY)] + specs, [dproj] + args, {0: 0}
    return pl.pallas_call(
        body, name=name, grid=(t // tm, 2), in_specs=specs, out_specs=[gate, half],
        out_shape=[jax.ShapeDtypeStruct((t, PACKED_WIDTH), MXU_DTYPE), jax.ShapeDtypeStruct((t, D_MODEL), MXU_DTYPE)],
        input_output_aliases=aliases,
        compiler_params=_params(("parallel", "parallel")),
    )(*args)


def _out_tail(merged, w_out, x, final_w, target, tm=1024):
    t = x.shape[0]
    tm = min(tm, t)

    def body(m_ref, wo_ref, x_ref, w_ref, t_ref, dxm_ref, dx_ref, loss_ref, dw_ref):
        x2 = x_ref[...] + jnp.dot(m_ref[...], wo_ref[...], preferred_element_type=F32)
        w = w_ref[...]
        r = lax.rsqrt(jnp.mean(x2 * x2, axis=-1, keepdims=True) + NORM_EPS)
        xn = x2 * r
        err = xn * w - t_ref[...]
        dy = err * (1.0 / D_MODEL)
        dyw = dy * w
        dx2 = r * dyw - x2 * (r * r * r) * jnp.mean(dyw * x2, axis=-1, keepdims=True)
        dx_ref[...] = dx2
        dxm_ref[...] = dx2.astype(dxm_ref.dtype)
        loss = 0.5 * jnp.sum(jnp.sum(err * err, axis=-1, keepdims=True) * (1.0 / D_MODEL), axis=0, keepdims=True)
        onehot = jnp.where((_iota((8, 128), 0) == 0) & (_iota((8, 128), 1) == 0), 1.0, 0.0)

        @pl.when(pl.program_id(0) == 0)
        def _():
            loss_ref[...] = jnp.zeros_like(loss_ref)
            dw_ref[...] = jnp.zeros_like(dw_ref)

        loss_ref[...] += loss * onehot
        dw_ref[...] += jnp.where(_iota((8, D_MODEL), 0) == 0, jnp.sum(dy * xn, axis=0, keepdims=True), 0.0)

    tok = pl.BlockSpec((tm, D_MODEL), lambda i: (i, 0))
    full = pl.BlockSpec((D_MODEL, D_MODEL), lambda i: (0, 0))
    return pl.pallas_call(
        body, name="out_tail", grid=(t // tm,),
        in_specs=[tok, full, tok, pl.BlockSpec((1, D_MODEL), lambda i: (0, 0)), tok],
        out_specs=[tok, tok, pl.BlockSpec((8, 128), lambda i: (0, 0)), pl.BlockSpec((8, D_MODEL), lambda i: (0, 0))],
        out_shape=[jax.ShapeDtypeStruct((t, D_MODEL), MXU_DTYPE), jax.ShapeDtypeStruct((t, D_MODEL), F32),
                   jax.ShapeDtypeStruct((8, 128), F32), jax.ShapeDtypeStruct((8, D_MODEL), F32)],
        compiler_params=_params(("arbitrary",)),
    )(merged, w_out, x, final_w, target)


def _dh_norm_bwd(dproj, wp, x, norm_w, dx2, after, blocks, name, carry=None, tm=1024, tk=PACKED_WIDTH // 4):
    t = x.shape[0]
    nk = PACKED_WIDTH // tk
    first, end = blocks

    def body(*refs):
        dp_ref, wp_ref, x_ref, w_ref, dx2_ref = refs[:5]
        dx_ref, dw_ref, acc = refs[-3:]
        kk = pl.program_id(1)
        part = _mxu(dp_ref[...], wp_ref[...], _NT)

        @pl.when(kk == 0)
        def _():
            acc[...] = part

        @pl.when(kk > 0)
        def _():
            acc[...] += part

        @pl.when((kk == 0) & (pl.program_id(0) == 0))
        def _():
            dw_ref[...] = jnp.zeros_like(dw_ref) if carry is None else refs[-4][...]

        @pl.when(kk == nk - 1)
        def _():
            xf, w, dh_ = x_ref[...], w_ref[...], acc[...]
            r = lax.rsqrt(jnp.mean(xf * xf, axis=-1, keepdims=True) + NORM_EPS)
            dhw = dh_ * w
            dx_ref[...] = dx2_ref[...] + r * dhw - xf * (r * r * r) * jnp.mean(dhw * xf, axis=-1, keepdims=True)
            dw_ref[...] += jnp.where(_iota((8, D_MODEL), 0) == 0, jnp.sum(dh_ * xf * r, axis=0, keepdims=True), 0.0)

    tok = pl.BlockSpec((tm, D_MODEL), lambda i, kk: (first + i, 0))
    small = pl.BlockSpec((8, D_MODEL), lambda i, kk: (0, 0))
    specs = [pl.BlockSpec((tm, tk), lambda i, kk: (first + i, kk)), pl.BlockSpec((D_MODEL, tk), lambda i, kk: (0, kk)),
             tok, pl.BlockSpec((1, D_MODEL), lambda i, kk: (0, 0)), tok, pl.BlockSpec(memory_space=pl.ANY)]
    args = [dproj, wp, x, norm_w, dx2, after]
    aliases = {}
    if carry is not None:
        specs += [pl.BlockSpec(memory_space=pl.ANY), small]
        args += list(carry)
        aliases = {len(args) - 2: 0}
    return pl.pallas_call(
        body, name=name, grid=(end - first, nk), in_specs=specs, out_specs=[tok, small],
        out_shape=[jax.ShapeDtypeStruct((t, D_MODEL), F32), jax.ShapeDtypeStruct((8, D_MODEL), F32)],
        scratch_shapes=[pltpu.VMEM((tm, D_MODEL), F32)],
        input_output_aliases=aliases,
        compiler_params=_params(("arbitrary", "arbitrary")),
    )(*args)


def _local_step(x, target, norm_w, wp, conv_w, a_log, dt_bias, gdn_norm_w, w_up_a, w_up_b, w_out, final_w,
                start_reduce, continue_reduce):
    t = x.shape[0]
    tables = _rope_tables(t)
    a_log = jnp.pad(a_log, ((0, 0), (HEADS, 128 - 2 * HEADS)))
    dt_bias = jnp.pad(dt_bias, ((0, 0), (HEADS, 128 - 2 * HEADS)))

    proj, h_t = _norm_proj(x, norm_w, wp)
    qkvs = _rope_fwd(proj, tables)
    outs, lses = zip(*[_att_fwd(qkvs[gi], d, f"att_fwd{gi}") for gi, d in enumerate(DILATIONS)])
    oz_a, o_a, *lse_views = _att_merge(outs, lses, proj)
    conv = _conv_fwd(proj, conv_w)
    gq, gk, gv, gb, gg, ggl, grow = _gdn_prep(conv, proj, a_log, dt_bias)
    o_b, states, invs = _gdn_fwd(gq, gk, gv, gb, gg, ggl, grow)
    oz_b = _gdn_out(o_b, gdn_norm_w, proj)
    big = dict(tm=1024, tn=1024, tk=1024)
    y_a = _matmul(oz_a, w_up_a, "nn", "up_a", **big)
    y_b = _matmul(oz_b, w_up_b, "nn", "up_b", **big)
    merged = _merge(y_a, y_b, proj)
    dx2_m, dx2, loss_blk, d_final = _out_tail(merged, w_out, x, final_w, target)

    d_wout = _matmul(merged, dx2_m, "tn", "d_w_out", **big)
    d_m = _matmul(dx2_m, w_out, "nt", "d_merged", **big)
    dproj, dy_a = _merge_bwd(None, d_m, y_a, proj, SEG_GA, "merge_bwd_a")
    dproj, dy_b = _merge_bwd(dproj, d_m, y_b, proj, SEG_GB, "merge_bwd_b")
    d_wua = _matmul(oz_a, dy_a, "tn", "d_w_up_a", **big)
    d_wub = _matmul(oz_b, dy_b, "tn", "d_w_up_b", **big)
    d_oz_a = _matmul(dy_a, w_up_a, "nt", "d_oz_a", **big)
    d_oz_b = _matmul(dy_b, w_up_b, "nt", "d_oz_b", **big)
    dproj, *views = _att_merge_bwd(dproj, d_oz_a, o_a, proj)
    do_views, delta_views = views[:3], views[3:]
    dqs, dkvs = zip(*[_att_bwd(qkvs[gi], do_views[gi], lse_views[gi], delta_views[gi], d, f"att_bwd{gi}")
                      for gi, d in enumerate(DILATIONS)])
    dproj = _rope_bwd(dproj, dqs, dkvs, tables)
    dproj, do_b, d_gnw = _gdn_out_bwd(dproj, d_oz_b, o_b, gdn_norm_w, proj)
    dgq, dgk, dgv, dgb, dgg, dggl, dgrow = _gdn_bwd(gq, gk, gv, gb, gg, ggl, grow, states, invs, do_b)
    dproj, dconv, d_small = _gdn_prep_bwd(dproj, conv, proj, a_log, dt_bias, dgq, dgk, dgv, dgb, dgg, dggl, dgrow)
    dproj, d_convw = _conv_bwd(dproj, dconv, proj, conv_w)
    d_wp = _matmul(h_t, dproj, "nn", "d_w_in", tm=1024, tn=PACKED_WIDTH // 4, tk=1024)
    in_flight, token = start_reduce(d_wp, d_wua, d_wub, d_wout, d_convw[0:GDN_CONV])
    nblk = t // 1024
    cut = max(1, nblk // 4)
    part = _dh_norm_bwd(dproj, wp, x, norm_w, dx2, token, (0, cut), "dh_norm_bwd_a")
    in_flight, token = continue_reduce(in_flight, part[0])
    grad_x, d_norm = _dh_norm_bwd(dproj, wp, x, norm_w, dx2, token, (cut, nblk), "dh_norm_bwd_b", carry=part)
    return dict(loss=loss_blk, grad_x=grad_x, norm_w=d_norm[0:1], in_flight=in_flight,
                a_log=d_small[0:1, HEADS:2 * HEADS], dt_bias=d_small[1:2, HEADS:2 * HEADS], gdn_norm_w=d_gnw[0:1],
                final_norm_w=d_final[0:1])


SHARDS = 4
W_IN_SHARD = IN_WIDTH // SHARDS
ROWS_UP = WIDTH * (D_MODEL // SHARDS) // 128
ROWS_OUT = (D_MODEL // SHARDS) * D_MODEL // 128
CONV_SHARD = 3 * WIDTH // SHARDS
ROWS_CONV = 16
SLAB_ROWS = 2 * ROWS_UP + ROWS_OUT + 2 * ROWS_CONV
HALF_ROWS = SLAB_ROWS // 2
BIG_HALF = (D_MODEL // 2, W_IN_SHARD)
SMALL_HALF = (HALF_ROWS, 128)
MESH = pl.DeviceIdType.MESH
ANY = pl.BlockSpec(memory_space=pl.ANY)


def _pad_rows(a, rows):
    return jnp.pad(a, ((0, rows - a.shape[0]), (0, 0)))


def _pack_slab(w_up_a, w_up_b, w_out, conv, conv_lo):
    parts = [w_up_a.reshape(ROWS_UP, 128), w_up_b.reshape(ROWS_UP, 128), w_out.reshape(ROWS_OUT, 128),
             _pad_rows(conv.reshape(-1, 128), ROWS_CONV), _pad_rows(conv_lo.reshape(-1, 128), ROWS_CONV)]
    return jnp.concatenate(parts, axis=0).reshape(2, *SMALL_HALF)


def _unpack_slab(slab):
    slab = slab.reshape(SLAB_ROWS, 128)
    r0 = 0
    out = []
    for rows, shape in ((ROWS_UP, (WIDTH, D_MODEL // SHARDS)), (ROWS_UP, (WIDTH, D_MODEL // SHARDS)),
                        (ROWS_OUT, (D_MODEL // SHARDS, D_MODEL)), (ROWS_CONV, None), (ROWS_CONV, None)):
        part = slab[r0:r0 + rows]
        out.append(part[:GDN_CONV * CONV_SHARD // 128].reshape(GDN_CONV, CONV_SHARD) if shape is None else part.reshape(shape))
        r0 += rows
    return out


def _mesh_position():
    x, y, c = lax.axis_index("x"), lax.axis_index("y"), lax.axis_index("c")
    return x, y, c, [(1 - x, y), (x, 1 - y), (1 - x, 1 - y)]


def _gather_weights(shards):
    n = len(shards)

    def body(*refs):
        in_refs, out_refs, (send_sems, recv_sems) = refs[:n], refs[n:2 * n], refs[2 * n:]
        x, y, c, chips = _mesh_position()

        def half(a, chip, which):
            return out_refs[a].at[2 * chip[0] + chip[1], which]

        def copy(k, src, dst, to):
            return pltpu.make_async_remote_copy(src_ref=src, dst_ref=dst, send_sem=send_sems.at[k], recv_sem=recv_sems.at[k],
                                                device_id=to, device_id_type=MESH)

        pairs = [(a, j, chip) for a in range(n) for j, chip in enumerate(chips)]
        first = [copy(6 * a + j, in_refs[a].at[c], half(a, (x, y), c), (*chip, c)) for a, j, chip in pairs]
        for cp in first:
            cp.start()
        passed = [copy(6 * a + 3 + j, half(a, chip, c), half(a, chip, c), (x, y, 1 - c)) for a, j, chip in pairs]
        for i, (a, j, chip) in enumerate(pairs):
            copy(6 * a + j, half(a, chip, c), half(a, chip, c), (x, y, c)).wait_recv()
            passed[i].start()
        for a, j, chip in pairs:
            copy(6 * a + 3 + j, half(a, chip, 1 - c), half(a, chip, 1 - c), (x, y, c)).wait_recv()
        for cp in first + passed:
            cp.wait_send()

    return pl.pallas_call(
        body, name="gather_weights", in_specs=[ANY] * n, out_specs=[ANY] * n,
        out_shape=[jax.ShapeDtypeStruct((SHARDS, *s.shape), s.dtype) for s in shards],
        scratch_shapes=[pltpu.SemaphoreType.DMA((6 * n,)), pltpu.SemaphoreType.DMA((6 * n,))],
    )(*shards)


def _pair_sum(grads, recv, blk, name):
    _, _, rows, cols = grads.shape

    def body(c_ref, g_ref, r_ref, o_ref):
        o_ref[...] = (g_ref[0] + r_ref[...]).astype(o_ref.dtype)

    spec = pl.BlockSpec((1, blk, cols), lambda s, i, c_ref: (s, i, 0))
    return pl.pallas_call(
        body, name=name,
        grid_spec=pltpu.PrefetchScalarGridSpec(
            num_scalar_prefetch=1, grid=(SHARDS, rows // blk),
            in_specs=[pl.BlockSpec((1, 1, blk, cols), lambda s, i, c_ref: (s, c_ref[0], i, 0)), spec],
            out_specs=spec),
        out_shape=jax.ShapeDtypeStruct((SHARDS, rows, cols), MXU_DTYPE),
        compiler_params=_params(("parallel", "parallel")),
    )(lax.axis_index("c").astype(jnp.int32).reshape(1), grads, recv)


_HBM = pl.BlockSpec(memory_space=pltpu.HBM)
_SEM = pl.BlockSpec(memory_space=pltpu.SEMAPHORE)
_DATAFLOW = pltpu.SideEffectType.DATAFLOW_SIDE_EFFECTING


def _scatter_copies(p_refs, l_refs, send_sems, recv_sems):
    x, y, c, chips = _mesh_position()
    return [pltpu.make_async_remote_copy(src_ref=p_refs[a].at[2 * chip[0] + chip[1]], dst_ref=l_refs[a].at[j],
                                         send_sem=send_sems.at[3 * a + j], recv_sem=recv_sems.at[3 * a + j],
                                         device_id=(*chip, c), device_id_type=MESH)
            for a in range(len(p_refs)) for j, chip in enumerate(chips)]


def _exchange_copies(g_refs, l_refs, send_sems, recv_sems):
    x, y, c, _ = _mesh_position()
    return [pltpu.make_async_remote_copy(src_ref=g_refs[a].at[s, 1 - c], dst_ref=l_refs[a].at[s],
                                         send_sem=send_sems.at[SHARDS * a + s], recv_sem=recv_sems.at[SHARDS * a + s],
                                         device_id=(x, y, 1 - c), device_id_type=MESH)
            for a in range(len(g_refs)) for s in range(SHARDS)]


def _copies_start(name, copies, count, sources, land_shapes):
    n = len(sources)
    lands = [lax.empty(shape, src.dtype) for shape, src in zip(land_shapes, sources)]

    def body(*refs):
        s_refs, l_refs, send_sems, recv_sems, token = refs[:n], refs[n:2 * n], refs[2 * n], refs[2 * n + 1], refs[-1]
        for cp in copies(s_refs, l_refs, send_sems, recv_sems):
            cp.start()
        token[...] = jnp.zeros_like(token)

    operands = [pltpu.with_memory_space_constraint(a, pltpu.HBM) for a in (*sources, *lands)]
    return pl.pallas_call(
        body, name=name, in_specs=[_HBM] * (2 * n),
        out_shape=(pltpu.SemaphoreType.DMA((count,)), pltpu.SemaphoreType.DMA((count,)),
                   *[pltpu.HBM(a.shape, a.dtype) for a in operands], jax.ShapeDtypeStruct((8, 128), F32)),
        out_specs=(_SEM, _SEM, *[_HBM] * (2 * n), pl.BlockSpec(memory_space=pltpu.VMEM)),
        input_output_aliases={i: 2 + i for i in range(2 * n)},
        compiler_params=pltpu.CompilerParams(has_side_effects=_DATAFLOW),
    )(*operands)


def _copies_wait(name, copies, send_sems, recv_sems, passed, after):
    n = len(passed) // 2

    def body(*refs):
        s_refs, l_refs, send_s, recv_s = refs[:n], refs[n:2 * n], refs[2 * n], refs[2 * n + 1]
        for cp in copies(s_refs, l_refs, send_s, recv_s):
            cp.wait_send()
            cp.wait_recv()

    return pl.pallas_call(
        body, name=name, in_specs=[_HBM] * (2 * n) + [_SEM, _SEM, ANY],
        out_shape=[pltpu.HBM(a.shape, a.dtype) for a in passed], out_specs=[_HBM] * (2 * n),
        input_output_aliases={i: i for i in range(2 * n)},
        compiler_params=pltpu.CompilerParams(has_side_effects=_DATAFLOW),
    )(*passed, send_sems, recv_sems, after)


def _chip_sum(pairs, recv, blk, name):
    _, rows, cols = pairs.shape

    def body(pos_ref, p_ref, r_ref, o_ref):
        o_ref[0] = ((p_ref[0].astype(F32) + r_ref[0].astype(F32)) + r_ref[1].astype(F32)) + r_ref[2].astype(F32)

    pos = jnp.stack([2 * lax.axis_index("x") + lax.axis_index("y"), lax.axis_index("c")]).astype(jnp.int32)
    return pl.pallas_call(
        body, name=name,
        grid_spec=pltpu.PrefetchScalarGridSpec(
            num_scalar_prefetch=1, grid=(rows // blk,),
            in_specs=[pl.BlockSpec((1, blk, cols), lambda i, pos_ref: (pos_ref[0], i, 0)),
                      pl.BlockSpec((3, blk, cols), lambda i, pos_ref: (0, i, 0))],
            out_specs=pl.BlockSpec((1, blk, cols), lambda i, pos_ref: (pos_ref[1], i, 0))),
        out_shape=jax.ShapeDtypeStruct((2, rows, cols), F32),
        compiler_params=_params(("parallel",)),
    )(pos, pairs, recv)


def _share_total(totals):
    n = len(totals)

    def body(*refs):
        t_refs, out_refs, (send_sems, recv_sems) = refs[:n], refs[n:2 * n], refs[2 * n:]
        x, y, c, _ = _mesh_position()
        copies = [pltpu.make_async_remote_copy(src_ref=t_refs[a].at[c], dst_ref=out_refs[a].at[c], send_sem=send_sems.at[a],
                                               recv_sem=recv_sems.at[a], device_id=(x, y, 1 - c), device_id_type=MESH)
                  for a in range(n)]
        for cp in copies:
            cp.start()
        for a in range(n):
            other = out_refs[a].at[1 - c]
            pltpu.make_async_remote_copy(src_ref=other, dst_ref=other, send_sem=send_sems.at[a], recv_sem=recv_sems.at[a],
                                         device_id=(x, y, c), device_id_type=MESH).wait_recv()
        for cp in copies:
            cp.wait_send()

    return pl.pallas_call(
        body, name="share_total", in_specs=[ANY] * n, out_specs=[ANY] * n,
        out_shape=[jax.ShapeDtypeStruct(t.shape, F32) for t in totals],
        scratch_shapes=[pltpu.SemaphoreType.DMA((n,)), pltpu.SemaphoreType.DMA((n,))],
        input_output_aliases={a: a for a in range(n)},
    )(*totals)


def _allreduce_small(block):
    def body(b_ref, out_ref, gath, send_sems, recv_sems):
        x, y, c, _ = _mesh_position()
        me = 4 * x + 2 * y + c
        gath[me] = b_ref[...]
        copies = []
        for k in range(1, 8):
            peer = (x ^ (k >> 2), y ^ ((k >> 1) & 1), c ^ (k & 1))
            copies.append(pltpu.make_async_remote_copy(src_ref=b_ref, dst_ref=gath.at[me], send_sem=send_sems.at[k - 1],
                                                       recv_sem=recv_sems.at[k - 1], device_id=peer, device_id_type=MESH))
        for cp in copies:
            cp.start()
        for k in range(1, 8):
            src = 4 * (x ^ (k >> 2)) + 2 * (y ^ ((k >> 1) & 1)) + (c ^ (k & 1))
            pltpu.make_async_remote_copy(src_ref=b_ref, dst_ref=gath.at[src], send_sem=send_sems.at[k - 1],
                                         recv_sem=recv_sems.at[k - 1], device_id=(x, y, c), device_id_type=MESH).wait_recv()
        for cp in copies:
            cp.wait_send()
        acc = gath[0]
        for d in range(1, 8):
            acc = acc + gath[d]
        out_ref[...] = acc

    vm = pl.BlockSpec(memory_space=pltpu.VMEM)
    return pl.pallas_call(
        body, name="allreduce_small", in_specs=[vm], out_specs=vm,
        out_shape=jax.ShapeDtypeStruct((8, D_MODEL), F32),
        scratch_shapes=[pltpu.VMEM((8, 8, D_MODEL), F32), pltpu.SemaphoreType.DMA((7,)), pltpu.SemaphoreType.DMA((7,))],
    )(block)


def _adamw(w, g, m, v, name):
    rows, cols = w.shape
    tr = 128 if rows % 128 == 0 else rows

    def body(w_ref, g_ref, m_ref, v_ref, d_ref, nm_ref, nv_ref):
        gv = g_ref[...]
        nm = ADAM_B1 * m_ref[...] + (1.0 - ADAM_B1) * gv
        nv = ADAM_B2 * v_ref[...] + (1.0 - ADAM_B2) * (gv * gv)
        m_hat = nm / (1.0 - ADAM_B1 ** ADAM_STEP)
        v_hat = nv / (1.0 - ADAM_B2 ** ADAM_STEP)
        d_ref[...] = -ADAM_LR * (m_hat / (jnp.sqrt(v_hat) + ADAM_EPS) + ADAM_WD * w_ref[...])
        nm_ref[...] = nm
        nv_ref[...] = nv

    spec = pl.BlockSpec((tr, cols), lambda i: (i, 0))
    shape = jax.ShapeDtypeStruct((rows, cols), F32)
    return pl.pallas_call(
        body, name=name, grid=(rows // tr,), in_specs=[spec] * 4, out_specs=[spec] * 3, out_shape=[shape] * 3,
        compiler_params=_params(("parallel",)),
    )(w, g, m, v)


def kernel(x, norm_w, w_in, conv_w, a_log, dt_bias, gdn_norm_w, w_up_a, w_up_b, w_out, final_norm_w, loss_target, m_norm_w, m_w_in, m_conv_w, m_a_log, m_dt_bias, m_gdn_norm_w, m_w_up_a, m_w_up_b, m_w_out, m_final_norm_w, v_norm_w, v_w_in, v_conv_w, v_a_log, v_dt_bias, v_gdn_norm_w, v_w_up_a, v_w_up_b, v_w_out, v_final_norm_w):
    conv_hi = conv_w[0].astype(MXU_DTYPE)
    conv_lo = (conv_w[0] - conv_hi.astype(F32)).astype(MXU_DTYPE)
    big = w_in[0].astype(MXU_DTYPE).reshape(2, *BIG_HALF)
    slab = _pack_slab(w_up_a[0].astype(MXU_DTYPE), w_up_b[0].astype(MXU_DTYPE), w_out[0].astype(MXU_DTYPE), conv_hi, conv_lo)
    own_shard = 2 * lax.axis_index("x") + lax.axis_index("y")
    bigs, slabs = _gather_weights([big, slab])
    bigs = lax.dynamic_update_slice(bigs, big[None], (own_shard, 0, 0, 0)).reshape(SHARDS, D_MODEL, W_IN_SHARD)
    slabs = lax.dynamic_update_slice(slabs, slab[None], (own_shard, 0, 0, 0))
    parts = [_unpack_slab(slabs[s]) for s in range(SHARDS)]
    split = BA_END - (SHARDS - 1) * W_IN_SHARD
    wp = jnp.concatenate([bigs[s] for s in range(SHARDS - 1)]
                         + [bigs[-1][:, :split], jnp.zeros((D_MODEL, SEG_GA - BA_END), MXU_DTYPE), bigs[-1][:, split:]], axis=1)
    w_up_a_full = jnp.concatenate([p[0] for p in parts], axis=1)
    w_up_b_full = jnp.concatenate([p[1] for p in parts], axis=1)
    w_out_full = jnp.concatenate([p[2] for p in parts], axis=0)
    conv_full = jnp.concatenate([p[3].astype(F32) + p[4].astype(F32) for p in parts], axis=1)

    blocks, tags = (128, HALF_ROWS), ("w_in", "slab")

    def start_reduce(d_wp, d_w_up_a, d_w_up_b, d_w_out, d_conv_w):
        d_w_in = [d_wp[:, s * W_IN_SHARD:(s + 1) * W_IN_SHARD] for s in range(SHARDS - 1)]
        d_w_in.append(jnp.concatenate([d_wp[:, (SHARDS - 1) * W_IN_SHARD:BA_END], d_wp[:, SEG_GA:]], axis=1))
        zero_conv = jnp.zeros((GDN_CONV, CONV_SHARD), F32)
        grads = [jnp.stack(d_w_in).reshape(SHARDS, 2, *BIG_HALF),
                 jnp.stack([_pack_slab(d_w_up_a[:, s * 256:(s + 1) * 256], d_w_up_b[:, s * 256:(s + 1) * 256],
                                       d_w_out[s * 256:(s + 1) * 256], d_conv_w[:, s * CONV_SHARD:(s + 1) * CONV_SHARD],
                                       zero_conv) for s in range(SHARDS)])]
        *in_flight, token = _copies_start("exchange_start", _exchange_copies, 2 * SHARDS, grads,
                                          [(SHARDS, *gr.shape[2:]) for gr in grads])
        return in_flight, token

    def continue_reduce(in_flight, after):
        send_sems, recv_sems, *passed = in_flight
        arrived = _copies_wait("exchange_wait", _exchange_copies, send_sems, recv_sems, passed, after)
        grads, from_sibling = arrived[:2], arrived[2:]
        pairs = [_pair_sum(gr, fs, blk, f"pair_sum_{tag}") for gr, fs, blk, tag in zip(grads, from_sibling, blocks, tags)]
        *in_flight, token = _copies_start("scatter_start", _scatter_copies, 2 * 3, pairs, [(3, *p.shape[1:]) for p in pairs])
        return in_flight, token

    g = _local_step(x[0], loss_target[0], norm_w, wp, conv_full, a_log, dt_bias, gdn_norm_w,
                    w_up_a_full, w_up_b_full, w_out_full, final_norm_w[None], start_reduce, continue_reduce)

    send_sems, recv_sems, *passed = g["in_flight"]
    arrived = _copies_wait("scatter_wait", _scatter_copies, send_sems, recv_sems, passed, g["grad_x"])
    pairs, from_chips = arrived[:2], arrived[2:]
    total_big, total_slab = _share_total([_chip_sum(p, fc, blk, f"chip_sum_{tag}")
                                          for p, fc, blk, tag in zip(pairs, from_chips, blocks, tags)])
    g_w_in = total_big.reshape(D_MODEL, W_IN_SHARD)
    g_w_up_a, g_w_up_b, g_w_out, g_conv, _ = _unpack_slab(total_slab)

    row2 = jnp.concatenate([g["gdn_norm_w"], g["a_log"], g["dt_bias"], g["loss"][0:1, 0:1],
                            jnp.zeros((1, D_MODEL - HEAD_DIM - 2 * HEADS - 1), F32)], axis=1)
    small = _allreduce_small(jnp.concatenate([g["norm_w"], g["final_norm_w"], row2, jnp.zeros((5, D_MODEL), F32)], axis=0))
    g_norm, g_final = small[0:1], small[1]
    g_gnw, g_alog, g_dt = small[2:3, 0:HEAD_DIM], small[2:3, HEAD_DIM:HEAD_DIM + HEADS], small[2:3, HEAD_DIM + HEADS:HEAD_DIM + 2 * HEADS]
    loss = small[2, HEAD_DIM + 2 * HEADS]

    names = ["norm_w", "w_in", "conv_w", "a_log", "dt_bias", "gdn_norm_w", "w_up_a", "w_up_b", "w_out", "final_norm_w"]
    weights = dict(zip(names, (norm_w, w_in, conv_w, a_log, dt_bias, gdn_norm_w, w_up_a, w_up_b, w_out, final_norm_w)))
    ms = dict(zip(names, (m_norm_w, m_w_in, m_conv_w, m_a_log, m_dt_bias, m_gdn_norm_w, m_w_up_a, m_w_up_b, m_w_out, m_final_norm_w)))
    vs = dict(zip(names, (v_norm_w, v_w_in, v_conv_w, v_a_log, v_dt_bias, v_gdn_norm_w, v_w_up_a, v_w_up_b, v_w_out, v_final_norm_w)))
    grads2d = dict(norm_w=g_norm, w_in=g_w_in, conv_w=g_conv, a_log=g_alog, dt_bias=g_dt, gdn_norm_w=g_gnw,
                   w_up_a=g_w_up_a, w_up_b=g_w_up_b, w_out=g_w_out, final_norm_w=g_final[None])
    grad_out, delta, new_m, new_v = [], [], [], []
    for n in names:
        shape = weights[n].shape
        two_d = grads2d[n].shape
        d, nm, nv = _adamw(weights[n].reshape(two_d), grads2d[n], ms[n].reshape(two_d), vs[n].reshape(two_d), f"adamw_{n}")
        grad_out.append(grads2d[n].reshape(shape))
        delta.append(d.reshape(shape))
        new_m.append(nm.reshape(shape))
        new_v.append(nv.reshape(shape))
    return (loss, g["grad_x"][None], *grad_out, *delta, *new_m, *new_v)
```

```python
import functools

import jax
import jax.numpy as jnp
from jax import lax
from jax.experimental import pallas as pl
from jax.experimental.pallas import tpu as pltpu

F32 = jnp.float32
MXU_DTYPE = jnp.bfloat16
HIGHEST = lax.Precision.HIGHEST

D_MODEL = 1024
HEADS = 8
HEAD_DIM = 64
WIDTH = HEADS * HEAD_DIM
NORM_EPS = 1e-6
ROPE_THETA = 10000.0
ATT_BLOCK = 128
DILATIONS = (1, 4, 16)
GDN_CHUNK = 64
GDN_CONV = 4
IN_WIDTH = 9232
SEG_A, SEG_ZA, SEG_B, SEG_ZB, SEG_BA, SEG_GA, SEG_GB, PACKED_WIDTH = 0, 4608, 5120, 6656, 7168, 7680, 8704, 9728
BA_END = 7184
VMEM_LIMIT = 56 * 1024 * 1024

ADAM_LR, ADAM_B1, ADAM_B2, ADAM_EPS, ADAM_WD, ADAM_STEP = 0.001, 0.9, 0.999, 1e-08, 0.01, 10

_NN = (((1,), (0,)), ((), ()))
_NT = (((1,), (1,)), ((), ()))
_TN = (((0,), (0,)), ((), ()))


def _params(sem):
    return pltpu.CompilerParams(dimension_semantics=sem, vmem_limit_bytes=VMEM_LIMIT)


def _mxu(a, b, dims):
    return lax.dot_general(a.astype(MXU_DTYPE), b.astype(MXU_DTYPE), dims, preferred_element_type=F32)


def _sigmoid(x):
    return 1.0 / (1.0 + jnp.exp(-x))


def _softplus(x):
    return jnp.maximum(x, 0.0) + jnp.log(1.0 + jnp.exp(-jnp.abs(x)))


def _iota(shape, axis):
    return lax.broadcasted_iota(jnp.int32, shape, axis)


def _matmul(a, b, mode, name, out_dtype=F32, tm=512, tn=512, tk=512):
    if mode == "nn":
        (m, k), (k2, n) = a.shape, b.shape
    elif mode == "nt":
        (m, k), (n, k2) = a.shape, b.shape
    else:
        (k, m), (k2, n) = a.shape, b.shape
    assert k == k2
    tm, tn, tk = min(tm, m), min(tn, n), min(tk, k)
    assert m % tm == 0 and n % tn == 0 and k % tk == 0
    nk = k // tk
    dims = {"nn": _NN, "nt": _NT, "tn": _TN}[mode]

    assert out_dtype == F32

    def body(a_ref, b_ref, o_ref):
        kk = pl.program_id(2)
        part = _mxu(a_ref[...], b_ref[...], dims)

        @pl.when(kk == 0)
        def _():
            o_ref[...] = part

        @pl.when(kk > 0)
        def _():
            o_ref[...] += part

    a_spec = pl.BlockSpec((tk, tm), lambda i, j, kk: (kk, i)) if mode == "tn" else pl.BlockSpec((tm, tk), lambda i, j, kk: (i, kk))
    b_spec = pl.BlockSpec((tn, tk), lambda i, j, kk: (j, kk)) if mode == "nt" else pl.BlockSpec((tk, tn), lambda i, j, kk: (kk, j))
    return pl.pallas_call(
        body, name=name, grid=(m // tm, n // tn, nk), in_specs=[a_spec, b_spec],
        out_specs=pl.BlockSpec((tm, tn), lambda i, j, kk: (i, j)),
        out_shape=jax.ShapeDtypeStruct((m, n), out_dtype),
        compiler_params=_params(("parallel", "parallel", "arbitrary")),
    )(a, b)


def _norm_proj(x, norm_w, wp, tm=1024, tn=PACKED_WIDTH // 4):
    t = x.shape[0]
    tm = min(tm, t)

    def body(x_ref, nw_ref, w_ref, proj_ref, ht_ref, h_scr):
        @pl.when(pl.program_id(1) == 0)
        def _():
            xf = x_ref[...]
            r = lax.rsqrt(jnp.mean(xf * xf, axis=-1, keepdims=True) + NORM_EPS)
            h = xf * r * nw_ref[...]
            h_scr[...] = h.astype(h_scr.dtype)
            ht_ref[...] = h.T.astype(ht_ref.dtype)

        proj_ref[...] = jnp.dot(h_scr[...], w_ref[...], preferred_element_type=F32)

    return pl.pallas_call(
        body, name="norm_proj", grid=(t // tm, PACKED_WIDTH // tn),
        in_specs=[pl.BlockSpec((tm, D_MODEL), lambda i, j: (i, 0)),
                  pl.BlockSpec((1, D_MODEL), lambda i, j: (0, 0)),
                  pl.BlockSpec((D_MODEL, tn), lambda i, j: (0, j))],
        out_specs=[pl.BlockSpec((tm, tn), lambda i, j: (i, j)),
                   pl.BlockSpec((D_MODEL, tm), lambda i, j: (0, i))],
        out_shape=[jax.ShapeDtypeStruct((t, PACKED_WIDTH), F32), jax.ShapeDtypeStruct((D_MODEL, t), MXU_DTYPE)],
        scratch_shapes=[pltpu.VMEM((tm, D_MODEL), MXU_DTYPE)],
        compiler_params=_params(("parallel", "arbitrary")),
    )(x, norm_w, wp)


def _rope_tables(t):
    lane = jnp.arange(128)
    inv_freq = ROPE_THETA ** (-jnp.arange(0, HEAD_DIM, 2, dtype=F32) / HEAD_DIM)
    freq = jnp.concatenate([inv_freq] * 4)
    coarse = (jnp.arange(t // 128, dtype=F32) * 128.0)[:, None] * freq[None, :]
    fine = jnp.arange(128, dtype=F32)[:, None] * freq[None, :]
    ca, sa, cb, sb = jnp.cos(coarse)[:, None, :], jnp.sin(coarse)[:, None, :], jnp.cos(fine)[None], jnp.sin(fine)[None]
    cos = (ca * cb - sa * sb).reshape(t, 128)
    sin = (sa * cb + ca * sb).reshape(t, 128)
    first_half = (lane % HEAD_DIM) < HEAD_DIM // 2
    return cos, jnp.where(first_half, -sin, 0.0), jnp.where(first_half, 0.0, sin)


def _rope_cols(x, cos, sin_lo, sin_hi, sign):
    outs = []
    for c in range(x.shape[1] // 128):
        xc = x[:, c * 128:(c + 1) * 128]
        rot = pltpu.roll(xc, 96, 1) * sin_lo + pltpu.roll(xc, 32, 1) * sin_hi
        outs.append(xc * cos + sign * rot)
    return jnp.concatenate(outs, axis=1)


def _rope_block(x, cos, sin_lo, sin_hi, sign, q_scale=None):
    q = _rope_cols(x[:, :WIDTH], cos, sin_lo, sin_hi, sign)
    k = _rope_cols(x[:, WIDTH:2 * WIDTH], cos, sin_lo, sin_hi, sign)
    return jnp.concatenate([q if q_scale is None else q * q_scale, k, x[:, 2 * WIDTH:]], axis=1)


def _tile_scratch(tm, cols):
    return pltpu.VMEM((cols // 128, tm, 128), F32)


def _store_tile(scr, y):
    for c in range(scr.shape[0]):
        scr[c] = y[:, c * 128:(c + 1) * 128]


def _load_tile(scr):
    return jnp.concatenate([scr[c] for c in range(scr.shape[0])], axis=1)


def _to_strided_view(scr, o_ref, d, width=None, col0=0):
    n, tm, _ = scr.shape
    width = n * 128 if width is None else width
    for r in range(d):
        for c in range(n):
            at = r * width + col0 + c * 128
            o_ref[:, at:at + 128] = scr[c, pl.ds(r, tm // d, stride=d), :].astype(o_ref.dtype)


def _from_strided_view(i_ref, scr, d):
    n, tm, _ = scr.shape
    for r in range(d):
        for c in range(n):
            scr[c, pl.ds(r, tm // d, stride=d), :] = i_ref[:, (r * n + c) * 128:(r * n + c + 1) * 128].astype(F32)


def _strided_spec(tm, d, cols):
    return pl.BlockSpec((tm // d, d * cols), lambda i: (i, 0))


def _rope_fwd(proj, tables, tm=512):
    t = proj.shape[0]
    cols = 3 * WIDTH

    def body(x_ref, c_ref, sl_ref, sh_ref, o0, o1, o2, scr):
        for g, (d, o_ref) in enumerate(zip(DILATIONS, (o0, o1, o2))):
            y = _rope_block(x_ref[:, g * cols:(g + 1) * cols], c_ref[...], sl_ref[...], sh_ref[...], 1.0, HEAD_DIM ** -0.5)
            if d == 1:
                o_ref[...] = y.astype(o_ref.dtype)
            else:
                _store_tile(scr, y)
                _to_strided_view(scr, o_ref, d)

    tab = pl.BlockSpec((tm, 128), lambda i: (i, 0))
    return pl.pallas_call(
        body, name="rope_fwd", grid=(t // tm,),
        in_specs=[pl.BlockSpec((tm, 3 * cols), lambda i: (i, 0)), tab, tab, tab],
        out_specs=[_strided_spec(tm, d, cols) for d in DILATIONS],
        out_shape=[jax.ShapeDtypeStruct((t // d, d * cols), MXU_DTYPE) for d in DILATIONS],
        scratch_shapes=[_tile_scratch(tm, cols)],
        compiler_params=_params(("parallel",)),
    )(proj, *tables)


def _rope_bwd(dproj, dqs, dkvs, tables, tm=512):
    t = dproj.shape[0]
    cols = 3 * WIDTH

    def body(dp_ref, q0, q1, q2, kv0, kv1, kv2, c_ref, sl_ref, sh_ref, o_ref, scr_q, scr_kv):
        for g, (d, q_ref, kv_ref) in enumerate(zip(DILATIONS, (q0, q1, q2), (kv0, kv1, kv2))):
            if d == 1:
                x = jnp.concatenate([q_ref[...], kv_ref[...]], axis=1)
            else:
                _from_strided_view(q_ref, scr_q, d)
                _from_strided_view(kv_ref, scr_kv, d)
                x = jnp.concatenate([_load_tile(scr_q), _load_tile(scr_kv)], axis=1)
            y = _rope_block(x, c_ref[...], sl_ref[...], sh_ref[...], -1.0)
            o_ref[:, g * cols:(g + 1) * cols] = y.astype(o_ref.dtype)

    tab = pl.BlockSpec((tm, 128), lambda i: (i, 0))
    return pl.pallas_call(
        body, name="rope_bwd", grid=(t // tm,),
        in_specs=[pl.BlockSpec(memory_space=pl.ANY)] + [_strided_spec(tm, d, WIDTH) for d in DILATIONS]
        + [_strided_spec(tm, d, 2 * WIDTH) for d in DILATIONS] + [tab, tab, tab],
        out_specs=pl.BlockSpec((tm, 3 * cols), lambda i: (i, 0)),
        out_shape=jax.ShapeDtypeStruct((t, PACKED_WIDTH), MXU_DTYPE),
        scratch_shapes=[_tile_scratch(tm, WIDTH), _tile_scratch(tm, 2 * WIDTH)],
        input_output_aliases={0: 0},
        compiler_params=_params(("parallel",)),
    )(dproj, *dqs, *dkvs, *tables)


def _att_masks():
    qi = _iota((ATT_BLOCK, ATT_BLOCK), 0)
    kj = _iota((ATT_BLOCK, ATT_BLOCK), 1)
    return kj <= qi, kj >= qi


def _att_fwd(qkv, d, name):
    rows = qkv.shape[0]
    nb = rows // ATT_BLOCK

    def body(q_ref, kc_ref, kp_ref, vc_ref, vp_ref, o_ref, lse_ref):
        has_prev = pl.program_id(1) > 0
        m_cur, m_prev = _att_masks()
        m_prev = m_prev & has_prev
        hs = range(HEADS)
        sls = [slice(h * HEAD_DIM, (h + 1) * HEAD_DIM) for h in hs]
        qs = [q_ref[:, sl] for sl in sls]
        s_c = [jnp.where(m_cur, _mxu(qs[h], kc_ref[:, sls[h]], _NT), -jnp.inf) for h in hs]
        s_p = [jnp.where(m_prev, _mxu(qs[h], kp_ref[:, sls[h]], _NT), -jnp.inf) for h in hs]
        m = [jnp.max(jnp.maximum(s_c[h], s_p[h]), axis=1, keepdims=True) for h in hs]
        p_c = [jnp.exp(s_c[h] - m[h]) for h in hs]
        p_p = [jnp.exp(s_p[h] - m[h]) for h in hs]
        den = [jnp.sum(p_c[h] + p_p[h], axis=1, keepdims=True) for h in hs]
        o = [_mxu(p_c[h], vc_ref[:, sls[h]], _NN) + _mxu(p_p[h], vp_ref[:, sls[h]], _NN) for h in hs]
        lane = _iota((1, 128), 1)
        lse = jnp.zeros((ATT_BLOCK, 128), F32)
        for h in hs:
            o_ref[:, sls[h]] = o[h] / den[h]
            lse = jnp.where(lane == h, m[h] + jnp.log(den[h]), lse)
        lse_ref[...] = lse

    def cur(c):
        return pl.BlockSpec((ATT_BLOCK, WIDTH), lambda r, i: (i, 3 * r + c))

    def prev(c):
        return pl.BlockSpec((ATT_BLOCK, WIDTH), lambda r, i: (jnp.maximum(i - 1, 0), 3 * r + c))

    return pl.pallas_call(
        body, name=name, grid=(d, nb), in_specs=[cur(0), cur(1), prev(1), cur(2), prev(2)],
        out_specs=[pl.BlockSpec((ATT_BLOCK, WIDTH), lambda r, i: (i, r)), pl.BlockSpec((ATT_BLOCK, 128), lambda r, i: (i, r))],
        out_shape=[jax.ShapeDtypeStruct((rows, d * WIDTH), F32), jax.ShapeDtypeStruct((rows, d * 128), F32)],
        compiler_params=_params(("parallel", "arbitrary")),
    )(qkv, qkv, qkv, qkv, qkv)


def _att_bwd(qkv, do, lse, delta, d, name):
    rows = qkv.shape[0]
    nb = rows // ATT_BLOCK
    scale = HEAD_DIM ** -0.5

    def body(q_ref, kc_ref, kp_ref, vc_ref, vp_ref, do_ref, l_ref, dl_ref, dq_ref, dkv_ref, own):
        i = pl.program_id(1)
        m_cur, m_prev = _att_masks()
        m_prev = m_prev & (i > 0)

        hs = range(HEADS)
        sls = [slice(h * HEAD_DIM, (h + 1) * HEAD_DIM) for h in hs]
        col = [slice(h, h + 1) for h in hs]

        def probs(k_r, mask):
            s = [_mxu(q_ref[:, sls[h]], k_r[:, sls[h]], _NT) for h in hs]
            return [jnp.where(mask, jnp.exp(s[h] - l_ref[:, col[h]]), 0.0) for h in hs]

        def dscores(p, v_r):
            dp = [_mxu(do_ref[:, sls[h]], v_r[:, sls[h]], _NT) for h in hs]
            return [(p[h] * (dp[h] - dl_ref[:, col[h]])).astype(MXU_DTYPE) for h in hs]

        @pl.when(i == 0)
        def _():
            own[...] = jnp.zeros_like(own)

        @pl.when(i < nb)
        def _():
            p_c = probs(kc_ref, m_cur)
            ds_c = dscores(p_c, vc_ref)
            p_p = probs(kp_ref, m_prev)
            ds_p = dscores(p_p, vp_ref)
            dq = [_mxu(ds_c[h], kc_ref[:, sls[h]], _NN) + _mxu(ds_p[h], kp_ref[:, sls[h]], _NN) for h in hs]
            dk_p = [_mxu(ds_p[h], q_ref[:, sls[h]], _TN) for h in hs]
            dv_p = [_mxu(p_p[h], do_ref[:, sls[h]], _TN) for h in hs]
            dk_c = [_mxu(ds_c[h], q_ref[:, sls[h]], _TN) for h in hs]
            dv_c = [_mxu(p_c[h], do_ref[:, sls[h]], _TN) for h in hs]
            for h in hs:
                vs = slice(WIDTH + h * HEAD_DIM, WIDTH + (h + 1) * HEAD_DIM)
                dq_ref[:, sls[h]] = dq[h] * scale
                dkv_ref[:, sls[h]] = own[:, sls[h]] + dk_p[h]
                dkv_ref[:, vs] = own[:, vs] + dv_p[h]
                own[:, sls[h]] = dk_c[h]
                own[:, vs] = dv_c[h]

        @pl.when(i == nb)
        def _():
            dkv_ref[...] = own[...]

    def qkv_spec(c, shift):
        return pl.BlockSpec((ATT_BLOCK, WIDTH), lambda r, i: (jnp.clip(i + shift, 0, nb - 1), 3 * r + c))

    tok = pl.BlockSpec((ATT_BLOCK, WIDTH), lambda r, i: (jnp.minimum(i, nb - 1), r))
    per_head = pl.BlockSpec((ATT_BLOCK, 128), lambda r, i: (jnp.minimum(i, nb - 1), r))
    return pl.pallas_call(
        body, name=name, grid=(d, nb + 1),
        in_specs=[qkv_spec(0, 0), qkv_spec(1, 0), qkv_spec(1, -1), qkv_spec(2, 0), qkv_spec(2, -1), tok, per_head, per_head],
        out_specs=[tok, pl.BlockSpec((ATT_BLOCK, 2 * WIDTH), lambda r, i: (jnp.maximum(i - 1, 0), r))],
        out_shape=[jax.ShapeDtypeStruct((rows, d * WIDTH), F32), jax.ShapeDtypeStruct((rows, d * 2 * WIDTH), F32)],
        scratch_shapes=[pltpu.VMEM((ATT_BLOCK, 2 * WIDTH), F32)],
        compiler_params=_params(("parallel", "arbitrary")),
    )(qkv, qkv, qkv, qkv, qkv, do, lse, delta)


def _att_merge(os_, lses, proj, tm=1024):
    t = proj.shape[0]

    def body(o0, o1, o2, l0, l1, l2, z_ref, oz_ref, o_ref, t0, t1, t2, s_o1, s_o2, s_l1, s_l2, s_t):
        _from_strided_view(o1, s_o1, DILATIONS[1])
        _from_strided_view(o2, s_o2, DILATIONS[2])
        _from_strided_view(l1, s_l1, DILATIONS[1])
        _from_strided_view(l2, s_l2, DILATIONS[2])
        a, b, c = l0[...], s_l1[0], s_l2[0]
        m = jnp.maximum(jnp.maximum(a, b), c)
        wa, wb, wc = jnp.exp(a - m), jnp.exp(b - m), jnp.exp(c - m)
        den = wa + wb + wc
        total = m + jnp.log(den)
        t0[...] = total
        s_t[0] = total
        _to_strided_view(s_t, t1, DILATIONS[1])
        _to_strided_view(s_t, t2, DILATIONS[2])
        spread = jnp.where(_iota((128, WIDTH), 1) // HEAD_DIM == _iota((128, WIDTH), 0), 1.0, 0.0).astype(MXU_DTYPE)
        ra, rb, rc = (_pair_sum_lanes(w / den, spread) for w in (wa, wb, wc))
        o = ra * o0[...] + rb * _load_tile(s_o1) + rc * _load_tile(s_o2)
        z = z_ref[...]
        o_ref[...] = o
        oz_ref[...] = (o * z * _sigmoid(z)).astype(oz_ref.dtype)

    tok = pl.BlockSpec((tm, WIDTH), lambda i: (i, 0))
    views = [_strided_spec(tm, d, WIDTH) for d in DILATIONS]
    per_head = [_strided_spec(tm, d, 128) for d in DILATIONS]
    return pl.pallas_call(
        body, name="att_merge", grid=(t // tm,),
        in_specs=views + per_head + [pl.BlockSpec((tm, WIDTH), lambda i: (i, SEG_ZA // WIDTH))],
        out_specs=[tok, tok] + per_head,
        out_shape=[jax.ShapeDtypeStruct((t, WIDTH), MXU_DTYPE), jax.ShapeDtypeStruct((t, WIDTH), F32)]
        + [jax.ShapeDtypeStruct((t // d, d * 128), F32) for d in DILATIONS],
        scratch_shapes=[_tile_scratch(tm, WIDTH)] * 2 + [_tile_scratch(tm, 128)] * 3,
        compiler_params=_params(("parallel",)),
    )(*os_, *lses, proj)


def _att_merge_bwd(dproj, d_oz, o, proj, tm=1024):
    t = proj.shape[0]

    def body(dp_ref, doz_ref, o_ref, z_ref, dz_ref, do0, do1, do2, dl0, dl1, dl2, s_do, s_dl):
        z, ov, g = z_ref[...], o_ref[...], doz_ref[...]
        sg = _sigmoid(z)
        do = g * z * sg
        dz_ref[...] = (g * ov * sg * (1.0 + z * (1.0 - sg))).astype(dz_ref.dtype)
        do0[...] = do.astype(do0.dtype)
        _store_tile(s_do, do)
        prod = do * ov
        delta = jnp.zeros((tm, 128), F32)
        for p in range(HEADS // 2):
            delta += _pair_sum_lanes(prod[:, p * 128:(p + 1) * 128], _pair_to_lanes(p, 0))
        dl0[...] = delta
        s_dl[0] = delta
        for d, do_v, dl_v in ((DILATIONS[1], do1, dl1), (DILATIONS[2], do2, dl2)):
            _to_strided_view(s_do, do_v, d)
            _to_strided_view(s_dl, dl_v, d)

    tok = pl.BlockSpec((tm, WIDTH), lambda i: (i, 0))
    seg = pl.BlockSpec((tm, WIDTH), lambda i: (i, SEG_ZA // WIDTH))
    views = [_strided_spec(tm, d, WIDTH) for d in DILATIONS]
    per_head = [_strided_spec(tm, d, 128) for d in DILATIONS]
    return pl.pallas_call(
        body, name="att_merge_bwd", grid=(t // tm,),
        in_specs=[pl.BlockSpec(memory_space=pl.ANY), tok, tok, seg],
        out_specs=[seg] + views + per_head,
        out_shape=[jax.ShapeDtypeStruct((t, PACKED_WIDTH), MXU_DTYPE)]
        + [jax.ShapeDtypeStruct((t // d, d * WIDTH), MXU_DTYPE) for d in DILATIONS]
        + [jax.ShapeDtypeStruct((t // d, d * 128), F32) for d in DILATIONS],
        scratch_shapes=[_tile_scratch(tm, WIDTH), _tile_scratch(tm, 128)],
        input_output_aliases={0: 0},
        compiler_params=_params(("parallel",)),
    )(dproj, d_oz, o, proj)


def _shift_down(x, halo, s):
    if s == 0:
        return x
    xs = pltpu.roll(x, s, 0)
    head = jnp.where(_iota((8, x.shape[1]), 0) < s, pltpu.roll(halo, s, 0), xs[0:8])
    return jnp.concatenate([head, xs[8:]], axis=0)


def _shift_up(x, nxt, s):
    if s == 0:
        return x
    n = x.shape[0]
    xs = pltpu.roll(x, n - s, 0)
    tail = jnp.where(_iota((8, x.shape[1]), 0) >= 8 - s, pltpu.roll(nxt, 8 - s, 0), xs[n - 8:])
    return jnp.concatenate([xs[:n - 8], tail], axis=0)


def _conv_fwd(proj, conv_w, tm=1024):
    t = proj.shape[0]
    cb = SEG_B // WIDTH

    def body(x_ref, halo_ref, w_ref, c_ref):
        halo = jnp.where(pl.program_id(0) > 0, halo_ref[...], 0.0)
        x = x_ref[...]
        w = w_ref[...]
        acc = jnp.zeros((tm, WIDTH), F32)
        for j in range(GDN_CONV):
            acc += _shift_down(x, halo, GDN_CONV - 1 - j) * w[j:j + 1, :]
        c_ref[...] = acc

    return pl.pallas_call(
        body, name="conv_fwd", grid=(t // tm, 3),
        in_specs=[pl.BlockSpec((tm, WIDTH), lambda i, c: (i, cb + c)),
                  pl.BlockSpec((8, WIDTH), lambda i, c: (jnp.maximum(i * (tm // 8) - 1, 0), cb + c)),
                  pl.BlockSpec((GDN_CONV, WIDTH), lambda i, c: (0, c))],
        out_specs=pl.BlockSpec((tm, WIDTH), lambda i, c: (i, c)),
        out_shape=jax.ShapeDtypeStruct((t, 3 * WIDTH), F32),
        compiler_params=_params(("parallel", "parallel")),
    )(proj, proj, conv_w)


def _conv_bwd(dproj, dc, proj, conv_w, tm=1024):
    t = proj.shape[0]
    cb = SEG_B // WIDTH
    nt = t // tm

    def body(dp_ref, dc_ref, dcn_ref, x_ref, halo_ref, w_ref, dx_ref, dw_ref):
        i = pl.program_id(1)
        w = w_ref[...]
        dcn = jnp.where(i < nt - 1, dcn_ref[...], 0.0)
        dcv = dc_ref[...]
        acc = jnp.zeros((tm, WIDTH), F32)
        for j in range(GDN_CONV):
            acc += _shift_up(dcv, dcn, GDN_CONV - 1 - j) * w[j:j + 1, :]
        dx_ref[...] = acc.astype(dx_ref.dtype)
        halo = jnp.where(i > 0, halo_ref[...], 0.0)
        x = x_ref[...]
        row8 = _iota((8, WIDTH), 0)
        part = jnp.zeros((8, WIDTH), F32)
        for j in range(GDN_CONV):
            s = jnp.sum(dcv * _shift_down(x, halo, GDN_CONV - 1 - j), axis=0, keepdims=True)
            part += jnp.where(row8 == j, s, 0.0)

        @pl.when(i == 0)
        def _():
            dw_ref[...] = jnp.zeros_like(dw_ref)

        dw_ref[...] += part

    return pl.pallas_call(
        body, name="conv_bwd", grid=(3, nt),
        in_specs=[pl.BlockSpec(memory_space=pl.ANY),
                  pl.BlockSpec((tm, WIDTH), lambda c, i: (i, c)),
                  pl.BlockSpec((8, WIDTH), lambda c, i: (jnp.minimum((i + 1) * (tm // 8), t // 8 - 1), c)),
                  pl.BlockSpec((tm, WIDTH), lambda c, i: (i, cb + c)),
                  pl.BlockSpec((8, WIDTH), lambda c, i: (jnp.maximum(i * (tm // 8) - 1, 0), cb + c)),
                  pl.BlockSpec((GDN_CONV, WIDTH), lambda c, i: (0, c))],
        out_specs=[pl.BlockSpec((tm, WIDTH), lambda c, i: (i, cb + c)),
                   pl.BlockSpec((8, WIDTH), lambda c, i: (0, c))],
        out_shape=[jax.ShapeDtypeStruct((t, PACKED_WIDTH), MXU_DTYPE), jax.ShapeDtypeStruct((8, 3 * WIDTH), F32)],
        input_output_aliases={0: 0},
        compiler_params=_params(("parallel", "arbitrary")),
    )(dproj, dc, dc, proj, proj, conv_w)


def _chunk_matrices(tm):
    r, c = _iota((tm, tm), 0), _iota((tm, tm), 1)
    same = (r // GDN_CHUNK) == (c // GDN_CHUNK)
    return jnp.where(same & (c <= r), 1.0, 0.0), jnp.where(same, 1.0, 0.0)


def _gdn_gates(ba, a_log, dt_bias):
    al = ba + dt_bias
    return _sigmoid(ba), -jnp.exp(a_log) * _softplus(al), _sigmoid(al)


SCAN_HEADS = (0, 2, 4, 6, 1, 3, 5, 7)


def _head_lane_eye():
    return jnp.where(_iota((HEADS, 128), 1) == _iota((HEADS, 128), 0) + HEADS, 1.0, 0.0)


def _pair_ones():
    return jnp.where(_iota((128, 128), 0) // HEAD_DIM == _iota((128, 128), 1) // HEAD_DIM, 1.0, 0.0).astype(MXU_DTYPE)


def _pair_to_lanes(p, base):
    return jnp.where(_iota((128, 128), 1) == base + 2 * p + _iota((128, 128), 0) // HEAD_DIM, 1.0, 0.0).astype(MXU_DTYPE)


def _pair_sum_lanes(x, sel):
    hi, lo = _split(x)
    return jnp.dot(hi, sel, preferred_element_type=F32) + jnp.dot(lo, sel, preferred_element_type=F32)


def _gdn_prep(conv, proj, a_log, dt_bias, tm=512):
    t = proj.shape[0]
    nc = tm // GDN_CHUNK

    def body(c_ref, ba_ref, al_ref, dt_ref, q_ref, k_ref, v_ref, b_ref, g_ref, gl_ref, grow_ref):
        beta, g, _ = _gdn_gates(ba_ref[:, 0:128], al_ref[...], dt_ref[...])
        lmat, cmat = _chunk_matrices(tm)
        gc = jnp.dot(lmat, g, precision=HIGHEST, preferred_element_type=F32)
        gl = jnp.dot(cmat, g, precision=HIGHEST, preferred_element_type=F32)
        grow = lax.dot_general(_head_lane_eye(), gc, _NT, precision=HIGHEST, preferred_element_type=F32)
        ones = _pair_ones()
        first = _iota((1, 128), 1) < HEAD_DIM

        def spread(x, base, p):
            return jnp.where(first, x[:, base + 2 * p:base + 2 * p + 1], x[:, base + 2 * p + 1:base + 2 * p + 2])

        for p in range(HEADS // 2):
            for seg, ref, scale in ((0, q_ref, HEAD_DIM ** -0.5), (1, k_ref, 1.0), (2, v_ref, None)):
                c = c_ref[:, seg * WIDTH + p * 128:seg * WIDTH + (p + 1) * 128]
                a = c * _sigmoid(c)
                if scale is not None:
                    a = a * (lax.rsqrt(_pair_sum_lanes(a * a, ones) + NORM_EPS) * scale)
                ref[p] = a
            b_ref[p] = spread(beta, 0, p)
            g_ref[p] = spread(gc, HEADS, p)
            gl_ref[p] = spread(gl, HEADS, p)
        for pos, h in enumerate(SCAN_HEADS):
            for cc in range(nc):
                grow_ref[pos, cc] = grow[h:h + 1, cc * GDN_CHUNK:(cc + 1) * GDN_CHUNK]

    hm = pl.BlockSpec((HEADS // 2, tm, 128), lambda i: (0, i, 0))
    small = pl.BlockSpec((1, 128), lambda i: (0, 0))
    hm_shape = jax.ShapeDtypeStruct((HEADS // 2, t, 128), F32)
    return pl.pallas_call(
        body, name="gdn_prep", grid=(t // tm,),
        in_specs=[pl.BlockSpec((tm, 3 * WIDTH), lambda i: (i, 0)),
                  pl.BlockSpec((tm, WIDTH), lambda i: (i, SEG_BA // WIDTH)), small, small],
        out_specs=[hm] * 6 + [pl.BlockSpec((HEADS, nc, 1, GDN_CHUNK), lambda i: (0, i, 0, 0))],
        out_shape=[hm_shape] * 6 + [jax.ShapeDtypeStruct((HEADS, t // GDN_CHUNK, 1, GDN_CHUNK), F32)],
        compiler_params=_params(("parallel",)),
    )(conv, proj, a_log, dt_bias)


def _gdn_prep_bwd(dproj, conv, proj, a_log, dt_bias, dq, dk, dv, db, dg, dgl, dgrow, tm=512):
    t = proj.shape[0]
    nc = tm // GDN_CHUNK

    def body(dp_ref, c_ref, ba_ref, al_ref, dt_ref, dq_ref, dk_ref, dv_ref, db_ref, dg_ref, dgl_ref, dgrow_ref,
             dba_ref, dc_ref, small_ref, row_scr):
        beta, g, sig_al = _gdn_gates(ba_ref[:, 0:128], al_ref[...], dt_ref[...])
        d_beta = jnp.zeros((tm, 128), F32)
        d_gc = jnp.zeros((tm, 128), F32)
        d_gl = jnp.zeros((tm, 128), F32)
        ones = _pair_ones()
        for pos, h in enumerate(SCAN_HEADS):
            for cc in range(nc):
                row_scr[h:h + 1, cc * GDN_CHUNK:(cc + 1) * GDN_CHUNK] = dgrow_ref[pos, cc]
        for p in range(HEADS // 2):
            d_beta += _pair_sum_lanes(db_ref[p], _pair_to_lanes(p, 0))
            d_gc += _pair_sum_lanes(dg_ref[p], _pair_to_lanes(p, HEADS))
            d_gl += _pair_sum_lanes(dgl_ref[p], _pair_to_lanes(p, HEADS))
            for seg, ref, scale in ((0, dq_ref, HEAD_DIM ** -0.5), (1, dk_ref, 1.0), (2, dv_ref, None)):
                cols = slice(seg * WIDTH + p * 128, seg * WIDTH + (p + 1) * 128)
                c = c_ref[:, cols]
                sg = _sigmoid(c)
                da = ref[p]
                if scale is not None:
                    a = c * sg
                    r = lax.rsqrt(_pair_sum_lanes(a * a, ones) + NORM_EPS)
                    da = da * scale
                    da = r * da - a * (r * r * r) * _pair_sum_lanes(da * a, ones)
                dc_ref[:, cols] = da * (sg * (1.0 + c * (1.0 - sg)))
        d_gc += lax.dot_general(row_scr[...], _head_lane_eye(), _TN, precision=HIGHEST, preferred_element_type=F32)
        lmat, cmat = _chunk_matrices(tm)
        d_g = (lax.dot_general(lmat, d_gc, _TN, precision=HIGHEST, preferred_element_type=F32)
               + lax.dot_general(cmat, d_gl, _TN, precision=HIGHEST, preferred_element_type=F32))
        d_al = d_g * (-jnp.exp(al_ref[...])) * sig_al
        d_bl = d_beta * beta * (1.0 - beta)
        dba_ref[...] = jnp.concatenate([d_bl + d_al, jnp.zeros((tm, WIDTH - 128), F32)], axis=1).astype(dba_ref.dtype)
        row8 = _iota((8, 128), 0)
        part = (jnp.where(row8 == 0, jnp.sum(d_g * g, axis=0, keepdims=True), 0.0)
                + jnp.where(row8 == 1, jnp.sum(d_al, axis=0, keepdims=True), 0.0))

        @pl.when(pl.program_id(0) == 0)
        def _():
            small_ref[...] = jnp.zeros_like(small_ref)

        small_ref[...] += part

    hm = pl.BlockSpec((HEADS // 2, tm, 128), lambda i: (0, i, 0))
    small = pl.BlockSpec((1, 128), lambda i: (0, 0))
    seg = pl.BlockSpec((tm, WIDTH), lambda i: (i, SEG_BA // WIDTH))
    return pl.pallas_call(
        body, name="gdn_prep_bwd", grid=(t // tm,),
        in_specs=[pl.BlockSpec(memory_space=pl.ANY), pl.BlockSpec((tm, 3 * WIDTH), lambda i: (i, 0)), seg, small, small]
        + [hm] * 6 + [pl.BlockSpec((HEADS, nc, 1, GDN_CHUNK), lambda i: (0, i, 0, 0))],
        out_specs=[seg, pl.BlockSpec((tm, 3 * WIDTH), lambda i: (i, 0)), pl.BlockSpec((8, 128), lambda i: (0, 0))],
        out_shape=[jax.ShapeDtypeStruct((t, PACKED_WIDTH), MXU_DTYPE), jax.ShapeDtypeStruct((t, 3 * WIDTH), F32),
                   jax.ShapeDtypeStruct((8, 128), F32)],
        scratch_shapes=[pltpu.VMEM((HEADS, tm), F32)],
        input_output_aliases={0: 0},
        compiler_params=_params(("arbitrary",)),
    )(dproj, conv, proj, a_log, dt_bias, dq, dk, dv, db, dg, dgl, dgrow)


_BNN = (((2,), (1,)), ((0,), (0,)))
_BNT = (((2,), (2,)), ((0,), (0,)))
_BTN = (((1,), (1,)), ((0,), (0,)))


@jax.custom_vjp
def _MM_NN(a, b):
    return _mxu(a, b, _BNN)


@jax.custom_vjp
def _MM_NT(a, b):
    return _mxu(a, b, _BNT)


@jax.custom_vjp
def _MM_TN(a, b):
    return _mxu(a, b, _BTN)


_MM_NN.defvjp(lambda a, b: (_mxu(a, b, _BNN), (a, b)), lambda r, g: (_mxu(g, r[1], _BNT), _mxu(r[0], g, _BTN)))
_MM_NT.defvjp(lambda a, b: (_mxu(a, b, _BNT), (a, b)), lambda r, g: (_mxu(g, r[1], _BNN), _mxu(g, r[0], _BTN)))
_MM_TN.defvjp(lambda a, b: (_mxu(a, b, _BTN), (a, b)), lambda r, g: (_mxu(r[1], g, _BNT), _mxu(r[0], g, _BNN)))


def _split(a):
    hi = a.astype(MXU_DTYPE)
    return hi, (a - hi.astype(F32)).astype(MXU_DTYPE)


def _dot3(a, b, dims):
    (ah, al), (bh, bl) = a, b
    (ca,), (cb,) = dims[0]
    return lax.dot_general(jnp.concatenate([ah, ah, al], axis=ca), jnp.concatenate([bh, bl, bh], axis=cb), dims,
                           preferred_element_type=F32)


def _unit_lower_inverse(a):
    c = GDN_CHUNK
    eye = jnp.where(_iota((c, c), 0) == _iota((c, c), 1), 1.0, 0.0)
    x = eye - a
    p = a
    for _ in range(5):
        ps = _split(p)
        p = _dot3(ps, ps, _BNN)
        x = x + _dot3(_split(x), _split(p), _BNN)
    return x


@jax.custom_vjp
def _SAVED_INVERSE(a, t_inv):
    return t_inv


def _saved_inverse_bwd(t_inv, g):
    ts = _split(t_inv)
    return -_dot3(ts, _split(_dot3(_split(g), ts, _BNT)), _BTN), jnp.zeros_like(t_inv)


_SAVED_INVERSE.defvjp(lambda a, t_inv: (t_inv, t_inv), _saved_inverse_bwd)


def _gdn_chunk(q, k, v, beta, g1, g2, gl, state, t_inv=None):
    c = GDN_CHUNK
    if t_inv is None:
        _mm_nn, _mm_nt, _mm_tn = (functools.partial(_mxu, dims=dd) for dd in (_BNN, _BNT, _BTN))
    else:
        _mm_nn, _mm_nt, _mm_tn = _MM_NN, _MM_NT, _MM_TN
    row, col = _iota((c, c), 0), _iota((c, c), 1)
    incl, strict = row >= col, row > col
    decay = jnp.where(incl, jnp.exp(jnp.where(incl, g1 - g2, 0.0)), 0.0)
    eg = jnp.exp(g1)
    kb = k * beta
    a = _mm_nt(kb, k) * jnp.where(strict, decay, 0.0)
    inv = _unit_lower_inverse(a) if t_inv is None else _SAVED_INVERSE(a, t_inv)
    u = _mm_nn(inv, v * beta)
    w = _mm_nn(inv, kb * eg)
    attn = _mm_nt(q, k) * decay
    v_new = u - _mm_nn(w, state)
    o = _mm_nn(q * eg, state) + _mm_nn(attn, v_new)
    new_state = state * jnp.exp(gl) + _mm_tn(k * jnp.exp(gl - g1), v_new)
    return (o, new_state, inv) if t_inv is None else (o, new_state)


def _unpair(x):
    return jnp.concatenate([x[..., :HEAD_DIM], x[..., HEAD_DIM:]], axis=0)


def _gdn_fwd(q, k, v, beta, g, gl, grow, cpb=8):
    t = q.shape[1]
    rows = cpb * GDN_CHUNK
    lo, hi = slice(0, HEAD_DIM), slice(HEAD_DIM, 2 * HEAD_DIM)

    def body(q_ref, k_ref, v_ref, b_ref, g_ref, gl_ref, grow_ref, o_ref, st_ref, inv_ref, state):
        @pl.when(pl.program_id(0) == 0)
        def _():
            state[...] = jnp.zeros_like(state)

        s = state[...]
        for cc in range(cpb):
            sl = slice(cc * GDN_CHUNK, (cc + 1) * GDN_CHUNK)
            st_ref[:, cc, :, lo], st_ref[:, cc, :, hi] = s[:HEADS // 2], s[HEADS // 2:]
            g2 = jnp.broadcast_to(grow_ref[:, cc], (HEADS, GDN_CHUNK, GDN_CHUNK))
            o, s, inv = _gdn_chunk(*[_unpair(r[:, sl, :]) for r in (q_ref, k_ref, v_ref, b_ref, g_ref)], g2,
                                   _unpair(gl_ref[:, sl, :]), s)
            o_ref[:, sl, lo], o_ref[:, sl, hi] = o[:HEADS // 2], o[HEADS // 2:]
            inv_ref[:, cc, :, lo], inv_ref[:, cc, :, hi] = inv[:HEADS // 2], inv[HEADS // 2:]
        state[...] = s

    hm = pl.BlockSpec((HEADS // 2, rows, 128), lambda i: (0, i, 0))
    per_chunk = pl.BlockSpec((HEADS // 2, cpb, GDN_CHUNK, 128), lambda i: (0, i, 0, 0))
    chunk_shape = jax.ShapeDtypeStruct((HEADS // 2, t // GDN_CHUNK, GDN_CHUNK, 128), F32)
    return pl.pallas_call(
        body, name="gdn_fwd", grid=(t // rows,),
        in_specs=[hm] * 6 + [pl.BlockSpec((HEADS, cpb, 1, GDN_CHUNK), lambda i: (0, i, 0, 0))],
        out_specs=[hm, per_chunk, per_chunk],
        out_shape=[jax.ShapeDtypeStruct((HEADS // 2, t, 128), F32), chunk_shape, chunk_shape],
        scratch_shapes=[pltpu.VMEM((HEADS, GDN_CHUNK, HEAD_DIM), F32)],
        compiler_params=_params(("arbitrary",)),
    )(q, k, v, beta, g, gl, grow)


def _gdn_bwd(q, k, v, beta, g, gl, grow, states, invs, do, cpb=4):
    t = q.shape[1]
    rows = cpb * GDN_CHUNK
    nsteps = t // rows
    lo, hi = slice(0, HEAD_DIM), slice(HEAD_DIM, 2 * HEAD_DIM)

    def body(q_ref, k_ref, v_ref, b_ref, g_ref, gl_ref, grow_ref, st_ref, inv_ref, do_ref,
             dq_ref, dk_ref, dv_ref, db_ref, dg_ref, dgl_ref, dgrow_ref, dstate):
        @pl.when(pl.program_id(0) == 0)
        def _():
            dstate[...] = jnp.zeros_like(dstate)

        ds = dstate[...]
        for cc in reversed(range(cpb)):
            sl = slice(cc * GDN_CHUNK, (cc + 1) * GDN_CHUNK)
            g2 = jnp.broadcast_to(grow_ref[:, cc], (HEADS, GDN_CHUNK, GDN_CHUNK))
            _, vjp = jax.vjp(_gdn_chunk, *[_unpair(r[:, sl, :]) for r in (q_ref, k_ref, v_ref, b_ref, g_ref)], g2,
                             _unpair(gl_ref[:, sl, :]), _unpair(st_ref[:, cc]), _unpair(inv_ref[:, cc]))
            gq, gk, gv, gb, gg1, gg2, ggl, ds, _ = vjp((_unpair(do_ref[:, sl, :]), ds))
            for ref, val in ((dq_ref, gq), (dk_ref, gk), (dv_ref, gv), (db_ref, gb), (dg_ref, gg1), (dgl_ref, ggl)):
                ref[:, sl, lo], ref[:, sl, hi] = val[:HEADS // 2], val[HEADS // 2:]
            dgrow_ref[:, cc] = jnp.sum(gg2, axis=1, keepdims=True)
        dstate[...] = ds

    hm = pl.BlockSpec((HEADS // 2, rows, 128), lambda i: (0, nsteps - 1 - i, 0))
    rowspec = pl.BlockSpec((HEADS, cpb, 1, GDN_CHUNK), lambda i: (0, nsteps - 1 - i, 0, 0))
    per_chunk = pl.BlockSpec((HEADS // 2, cpb, GDN_CHUNK, 128), lambda i: (0, nsteps - 1 - i, 0, 0))
    hm_shape = jax.ShapeDtypeStruct((HEADS // 2, t, 128), F32)
    return pl.pallas_call(
        body, name="gdn_bwd", grid=(nsteps,),
        in_specs=[hm] * 6 + [rowspec, per_chunk, per_chunk, hm],
        out_specs=[hm] * 6 + [rowspec],
        out_shape=[hm_shape] * 6 + [jax.ShapeDtypeStruct((HEADS, t // GDN_CHUNK, 1, GDN_CHUNK), F32)],
        scratch_shapes=[pltpu.VMEM((HEADS, GDN_CHUNK, HEAD_DIM), F32)],
        compiler_params=_params(("arbitrary",)),
    )(q, k, v, beta, g, gl, grow, states, invs, do)


def _gdn_out(o_hm, gdn_norm_w, proj, tm=1024):
    t = proj.shape[0]

    def body(o_ref, w_ref, z_ref, oz_ref):
        w = w_ref[...]
        ones = _pair_ones()
        for p in range(HEADS // 2):
            cols = slice(p * 128, (p + 1) * 128)
            o = o_ref[p]
            z = z_ref[:, cols]
            r = lax.rsqrt(_pair_sum_lanes(o * o, ones) * (1.0 / HEAD_DIM) + NORM_EPS)
            oz_ref[:, cols] = (o * r * w * (z * _sigmoid(z))).astype(oz_ref.dtype)

    tok = pl.BlockSpec((tm, WIDTH), lambda i: (i, 0))
    return pl.pallas_call(
        body, name="gdn_out", grid=(t // tm,),
        in_specs=[pl.BlockSpec((HEADS // 2, tm, 128), lambda i: (0, i, 0)), pl.BlockSpec((1, 128), lambda i: (0, 0)),
                  pl.BlockSpec((tm, WIDTH), lambda i: (i, SEG_ZB // WIDTH))],
        out_specs=tok, out_shape=jax.ShapeDtypeStruct((t, WIDTH), MXU_DTYPE),
        compiler_params=_params(("parallel",)),
    )(o_hm, jnp.tile(gdn_norm_w, (1, 2)), proj)


def _gdn_out_bwd(dproj, d_oz, o_hm, gdn_norm_w, proj, tm=1024):
    t = proj.shape[0]

    def body(dp_ref, doz_ref, o_ref, w_ref, z_ref, dz_ref, do_ref, dw_ref):
        w = w_ref[...]
        ones = _pair_ones()
        dw = jnp.zeros((1, 128), F32)
        for p in range(HEADS // 2):
            cols = slice(p * 128, (p + 1) * 128)
            o = o_ref[p]
            z, g = z_ref[:, cols], doz_ref[:, cols]
            sg = _sigmoid(z)
            r = lax.rsqrt(_pair_sum_lanes(o * o, ones) * (1.0 / HEAD_DIM) + NORM_EPS)
            dz_ref[:, cols] = (g * (o * r * w) * (sg * (1.0 + z * (1.0 - sg)))).astype(dz_ref.dtype)
            dn = g * (z * sg)
            dw += jnp.sum(dn * o * r, axis=0, keepdims=True)
            dnw = dn * w
            do_ref[p] = r * dnw - o * (r * r * r) * (_pair_sum_lanes(dnw * o, ones) * (1.0 / HEAD_DIM))

        @pl.when(pl.program_id(0) == 0)
        def _():
            dw_ref[...] = jnp.zeros_like(dw_ref)

        dw_ref[...] += jnp.where(_iota((8, 128), 0) == 0, dw, 0.0)

    tok = pl.BlockSpec((tm, WIDTH), lambda i: (i, 0))
    seg = pl.BlockSpec((tm, WIDTH), lambda i: (i, SEG_ZB // WIDTH))
    hm = pl.BlockSpec((HEADS // 2, tm, 128), lambda i: (0, i, 0))
    dz, do, dw = pl.pallas_call(
        body, name="gdn_out_bwd", grid=(t // tm,),
        in_specs=[pl.BlockSpec(memory_space=pl.ANY), tok, hm, pl.BlockSpec((1, 128), lambda i: (0, 0)), seg],
        out_specs=[seg, hm, pl.BlockSpec((8, 128), lambda i: (0, 0))],
        out_shape=[jax.ShapeDtypeStruct((t, PACKED_WIDTH), MXU_DTYPE), jax.ShapeDtypeStruct((HEADS // 2, t, 128), F32),
                   jax.ShapeDtypeStruct((8, 128), F32)],
        input_output_aliases={0: 0},
        compiler_params=_params(("arbitrary",)),
    )(dproj, d_oz, o_hm, jnp.tile(gdn_norm_w, (1, 2)), proj)
    return dz, do, dw[:, :HEAD_DIM] + dw[:, HEAD_DIM:]


def _merge(y_a, y_b, proj, tm=1024):
    t = proj.shape[0]

    def body(ya_ref, yb_ref, ga_ref, gb_ref, m_ref):
        m_ref[...] = (_sigmoid(ga_ref[...]) * ya_ref[...] + _sigmoid(gb_ref[...]) * yb_ref[...]).astype(m_ref.dtype)

    half = pl.BlockSpec((tm, WIDTH), lambda i, c: (i, c))
    return pl.pallas_call(
        body, name="merge", grid=(t // tm, 2),
        in_specs=[half, half, pl.BlockSpec((tm, WIDTH), lambda i, c: (i, SEG_GA // WIDTH + c)),
                  pl.BlockSpec((tm, WIDTH), lambda i, c: (i, SEG_GB // WIDTH + c))],
        out_specs=half, out_shape=jax.ShapeDtypeStruct((t, D_MODEL), MXU_DTYPE),
        compiler_params=_params(("parallel", "parallel")),
    )(y_a, y_b, proj, proj)


def _merge_bwd(dproj, d_m, y, proj, seg, name, tm=1024):
    t = proj.shape[0]

    def body(*refs):
        dm_ref, y_ref, g_ref, dg_ref, dy_ref = refs[-5:]
        dm = dm_ref[...]
        s = _sigmoid(g_ref[...])
        dy_ref[...] = (dm * s).astype(dy_ref.dtype)
        dg_ref[...] = (dm * y_ref[...] * s * (1.0 - s)).astype(dg_ref.dtype)

    half = pl.BlockSpec((tm, WIDTH), lambda i, c: (i, c))
    gate = pl.BlockSpec((tm, WIDTH), lambda i, c: (i, seg // WIDTH + c))
    specs, args, aliases = [half, half, gate], [d_m, y, proj], {}
    if dproj is not None:
        specs, args, aliases = [pl.BlockSpec(memory_space=pl.ANY)] + specs, [dproj] + args, {0: 0}
    return pl.pallas_call(
        body, name=name, grid=(t // tm, 2), in_specs=specs, out_specs=[gate, half],
        out_shape=[jax.ShapeDtypeStruct((t, PACKED_WIDTH), MXU_DTYPE), jax.ShapeDtypeStruct((t, D_MODEL), MXU_DTYPE)],
        input_output_aliases=aliases,
        compiler_params=_params(("parallel", "parallel")),
    )(*args)


def _out_tail(merged, w_out, x, final_w, target, tm=1024):
    t = x.shape[0]
    tm = min(tm, t)

    def body(m_ref, wo_ref, x_ref, w_ref, t_ref, dxm_ref, dx_ref, loss_ref, dw_ref):
        x2 = x_ref[...] + jnp.dot(m_ref[...], wo_ref[...], preferred_element_type=F32)
        w = w_ref[...]
        r = lax.rsqrt(jnp.mean(x2 * x2, axis=-1, keepdims=True) + NORM_EPS)
        xn = x2 * r
        err = xn * w - t_ref[...]
        dy = err * (1.0 / D_MODEL)
        dyw = dy * w
        dx2 = r * dyw - x2 * (r * r * r) * jnp.mean(dyw * x2, axis=-1, keepdims=True)
        dx_ref[...] = dx2
        dxm_ref[...] = dx2.astype(dxm_ref.dtype)
        loss = 0.5 * jnp.sum(jnp.sum(err * err, axis=-1, keepdims=True) * (1.0 / D_MODEL), axis=0, keepdims=True)
        onehot = jnp.where((_iota((8, 128), 0) == 0) & (_iota((8, 128), 1) == 0), 1.0, 0.0)

        @pl.when(pl.program_id(0) == 0)
        def _():
            loss_ref[...] = jnp.zeros_like(loss_ref)
            dw_ref[...] = jnp.zeros_like(dw_ref)

        loss_ref[...] += loss * onehot
        dw_ref[...] += jnp.where(_iota((8, D_MODEL), 0) == 0, jnp.sum(dy * xn, axis=0, keepdims=True), 0.0)

    tok = pl.BlockSpec((tm, D_MODEL), lambda i: (i, 0))
    full = pl.BlockSpec((D_MODEL, D_MODEL), lambda i: (0, 0))
    return pl.pallas_call(
        body, name="out_tail", grid=(t // tm,),
        in_specs=[tok, full, tok, pl.BlockSpec((1, D_MODEL), lambda i: (0, 0)), tok],
        out_specs=[tok, tok, pl.BlockSpec((8, 128), lambda i: (0, 0)), pl.BlockSpec((8, D_MODEL), lambda i: (0, 0))],
        out_shape=[jax.ShapeDtypeStruct((t, D_MODEL), MXU_DTYPE), jax.ShapeDtypeStruct((t, D_MODEL), F32),
                   jax.ShapeDtypeStruct((8, 128), F32), jax.ShapeDtypeStruct((8, D_MODEL), F32)],
        compiler_params=_params(("arbitrary",)),
    )(merged, w_out, x, final_w, target)


def _dh_norm_bwd(dproj, wp, x, norm_w, dx2, after, blocks, name, carry=None, tm=1024, tk=PACKED_WIDTH // 4):
    t = x.shape[0]
    nk = PACKED_WIDTH // tk
    first, end = blocks

    def body(*refs):
        dp_ref, wp_ref, x_ref, w_ref, dx2_ref = refs[:5]
        dx_ref, dw_ref, acc = refs[-3:]
        kk = pl.program_id(1)
        part = _mxu(dp_ref[...], wp_ref[...], _NT)

        @pl.when(kk == 0)
        def _():
            acc[...] = part

        @pl.when(kk > 0)
        def _():
            acc[...] += part

        @pl.when((kk == 0) & (pl.program_id(0) == 0))
        def _():
            dw_ref[...] = jnp.zeros_like(dw_ref) if carry is None else refs[-4][...]

        @pl.when(kk == nk - 1)
        def _():
            xf, w, dh_ = x_ref[...], w_ref[...], acc[...]
            r = lax.rsqrt(jnp.mean(xf * xf, axis=-1, keepdims=True) + NORM_EPS)
            dhw = dh_ * w
            dx_ref[...] = dx2_ref[...] + r * dhw - xf * (r * r * r) * jnp.mean(dhw * xf, axis=-1, keepdims=True)
            dw_ref[...] += jnp.where(_iota((8, D_MODEL), 0) == 0, jnp.sum(dh_ * xf * r, axis=0, keepdims=True), 0.0)

    tok = pl.BlockSpec((tm, D_MODEL), lambda i, kk: (first + i, 0))
    small = pl.BlockSpec((8, D_MODEL), lambda i, kk: (0, 0))
    specs = [pl.BlockSpec((tm, tk), lambda i, kk: (first + i, kk)), pl.BlockSpec((D_MODEL, tk), lambda i, kk: (0, kk)),
             tok, pl.BlockSpec((1, D_MODEL), lambda i, kk: (0, 0)), tok, pl.BlockSpec(memory_space=pl.ANY)]
    args = [dproj, wp, x, norm_w, dx2, after]
    aliases = {}
    if carry is not None:
        specs += [pl.BlockSpec(memory_space=pl.ANY), small]
        args += list(carry)
        aliases = {len(args) - 2: 0}
    return pl.pallas_call(
        body, name=name, grid=(end - first, nk), in_specs=specs, out_specs=[tok, small],
        out_shape=[jax.ShapeDtypeStruct((t, D_MODEL), F32), jax.ShapeDtypeStruct((8, D_MODEL), F32)],
        scratch_shapes=[pltpu.VMEM((tm, D_MODEL), F32)],
        input_output_aliases=aliases,
        compiler_params=_params(("arbitrary", "arbitrary")),
    )(*args)


def _local_step(x, target, norm_w, wp, conv_w, a_log, dt_bias, gdn_norm_w, w_up_a, w_up_b, w_out, final_w,
                start_reduce, continue_reduce):
    t = x.shape[0]
    tables = _rope_tables(t)
    a_log = jnp.pad(a_log, ((0, 0), (HEADS, 128 - 2 * HEADS)))
    dt_bias = jnp.pad(dt_bias, ((0, 0), (HEADS, 128 - 2 * HEADS)))

    proj, h_t = _norm_proj(x, norm_w, wp)
    qkvs = _rope_fwd(proj, tables)
    outs, lses = zip(*[_att_fwd(qkvs[gi], d, f"att_fwd{gi}") for gi, d in enumerate(DILATIONS)])
    oz_a, o_a, *lse_views = _att_merge(outs, lses, proj)
    conv = _conv_fwd(proj, conv_w)
    gq, gk, gv, gb, gg, ggl, grow = _gdn_prep(conv, proj, a_log, dt_bias)
    o_b, states, invs = _gdn_fwd(gq, gk, gv, gb, gg, ggl, grow)
    oz_b = _gdn_out(o_b, gdn_norm_w, proj)
    big = dict(tm=1024, tn=1024, tk=1024)
    y_a = _matmul(oz_a, w_up_a, "nn", "up_a", **big)
    y_b = _matmul(oz_b, w_up_b, "nn", "up_b", **big)
    merged = _merge(y_a, y_b, proj)
    dx2_m, dx2, loss_blk, d_final = _out_tail(merged, w_out, x, final_w, target)

    d_wout = _matmul(merged, dx2_m, "tn", "d_w_out", **big)
    d_m = _matmul(dx2_m, w_out, "nt", "d_merged", **big)
    dproj, dy_a = _merge_bwd(None, d_m, y_a, proj, SEG_GA, "merge_bwd_a")
    dproj, dy_b = _merge_bwd(dproj, d_m, y_b, proj, SEG_GB, "merge_bwd_b")
    d_wua = _matmul(oz_a, dy_a, "tn", "d_w_up_a", **big)
    d_wub = _matmul(oz_b, dy_b, "tn", "d_w_up_b", **big)
    d_oz_a = _matmul(dy_a, w_up_a, "nt", "d_oz_a", **big)
    d_oz_b = _matmul(dy_b, w_up_b, "nt", "d_oz_b", **big)
    dproj, *views = _att_merge_bwd(dproj, d_oz_a, o_a, proj)
    do_views, delta_views = views[:3], views[3:]
    dqs, dkvs = zip(*[_att_bwd(qkvs[gi], do_views[gi], lse_views[gi], delta_views[gi], d, f"att_bwd{gi}")
                      for gi, d in enumerate(DILATIONS)])
    dproj = _rope_bwd(dproj, dqs, dkvs, tables)
    dproj, do_b, d_gnw = _gdn_out_bwd(dproj, d_oz_b, o_b, gdn_norm_w, proj)
    dgq, dgk, dgv, dgb, dgg, dggl, dgrow = _gdn_bwd(gq, gk, gv, gb, gg, ggl, grow, states, invs, do_b)
    dproj, dconv, d_small = _gdn_prep_bwd(dproj, conv, proj, a_log, dt_bias, dgq, dgk, dgv, dgb, dgg, dggl, dgrow)
    dproj, d_convw = _conv_bwd(dproj, dconv, proj, conv_w)
    d_wp = _matmul(h_t, dproj, "nn", "d_w_in", tm=1024, tn=PACKED_WIDTH // 4, tk=1024)
    in_flight, token = start_reduce(d_wp, d_wua, d_wub, d_wout, d_convw[0:GDN_CONV])
    nblk = t // 1024
    cut = max(1, nblk // 4)
    part = _dh_norm_bwd(dproj, wp, x, norm_w, dx2, token, (0, cut), "dh_norm_bwd_a")
    in_flight, token = continue_reduce(in_flight, part[0])
    grad_x, d_norm = _dh_norm_bwd(dproj, wp, x, norm_w, dx2, token, (cut, nblk), "dh_norm_bwd_b", carry=part)
    return dict(loss=loss_blk, grad_x=grad_x, norm_w=d_norm[0:1], in_flight=in_flight,
                a_log=d_small[0:1, HEADS:2 * HEADS], dt_bias=d_small[1:2, HEADS:2 * HEADS], gdn_norm_w=d_gnw[0:1],
                final_norm_w=d_final[0:1])


SHARDS = 4
W_IN_SHARD = IN_WIDTH // SHARDS
ROWS_UP = WIDTH * (D_MODEL // SHARDS) // 128
ROWS_OUT = (D_MODEL // SHARDS) * D_MODEL // 128
CONV_SHARD = 3 * WIDTH // SHARDS
ROWS_CONV = 16
SLAB_ROWS = 2 * ROWS_UP + ROWS_OUT + 2 * ROWS_CONV
HALF_ROWS = SLAB_ROWS // 2
BIG_HALF = (D_MODEL // 2, W_IN_SHARD)
SMALL_HALF = (HALF_ROWS, 128)
MESH = pl.DeviceIdType.MESH
ANY = pl.BlockSpec(memory_space=pl.ANY)


def _pad_rows(a, rows):
    return jnp.pad(a, ((0, rows - a.shape[0]), (0, 0)))


def _pack_slab(w_up_a, w_up_b, w_out, conv, conv_lo):
    parts = [w_up_a.reshape(ROWS_UP, 128), w_up_b.reshape(ROWS_UP, 128), w_out.reshape(ROWS_OUT, 128),
             _pad_rows(conv.reshape(-1, 128), ROWS_CONV), _pad_rows(conv_lo.reshape(-1, 128), ROWS_CONV)]
    return jnp.concatenate(parts, axis=0).reshape(2, *SMALL_HALF)


def _unpack_slab(slab):
    slab = slab.reshape(SLAB_ROWS, 128)
    r0 = 0
    out = []
    for rows, shape in ((ROWS_UP, (WIDTH, D_MODEL // SHARDS)), (ROWS_UP, (WIDTH, D_MODEL // SHARDS)),
                        (ROWS_OUT, (D_MODEL // SHARDS, D_MODEL)), (ROWS_CONV, None), (ROWS_CONV, None)):
        part = slab[r0:r0 + rows]
        out.append(part[:GDN_CONV * CONV_SHARD // 128].reshape(GDN_CONV, CONV_SHARD) if shape is None else part.reshape(shape))
        r0 += rows
    return out


def _mesh_position():
    x, y, c = lax.axis_index("x"), lax.axis_index("y"), lax.axis_index("c")
    return x, y, c, [(1 - x, y), (x, 1 - y), (1 - x, 1 - y)]


def _gather_weights(shards):
    n = len(shards)

    def body(*refs):
        in_refs, out_refs, (send_sems, recv_sems) = refs[:n], refs[n:2 * n], refs[2 * n:]
        x, y, c, chips = _mesh_position()

        def half(a, chip, which):
            return out_refs[a].at[2 * chip[0] + chip[1], which]

        def copy(k, src, dst, to):
            return pltpu.make_async_remote_copy(src_ref=src, dst_ref=dst, send_sem=send_sems.at[k], recv_sem=recv_sems.at[k],
                                                device_id=to, device_id_type=MESH)

        pairs = [(a, j, chip) for a in range(n) for j, chip in enumerate(chips)]
        first = [copy(6 * a + j, in_refs[a].at[c], half(a, (x, y), c), (*chip, c)) for a, j, chip in pairs]
        for cp in first:
            cp.start()
        passed = [copy(6 * a + 3 + j, half(a, chip, c), half(a, chip, c), (x, y, 1 - c)) for a, j, chip in pairs]
        for i, (a, j, chip) in enumerate(pairs):
            copy(6 * a + j, half(a, chip, c), half(a, chip, c), (x, y, c)).wait_recv()
            passed[i].start()
        for a, j, chip in pairs:
            copy(6 * a + 3 + j, half(a, chip, 1 - c), half(a, chip, 1 - c), (x, y, c)).wait_recv()
        for cp in first + passed:
            cp.wait_send()

    return pl.pallas_call(
        body, name="gather_weights", in_specs=[ANY] * n, out_specs=[ANY] * n,
        out_shape=[jax.ShapeDtypeStruct((SHARDS, *s.shape), s.dtype) for s in shards],
        scratch_shapes=[pltpu.SemaphoreType.DMA((6 * n,)), pltpu.SemaphoreType.DMA((6 * n,))],
    )(*shards)


def _pair_sum(grads, recv, blk, name):
    _, _, rows, cols = grads.shape

    def body(c_ref, g_ref, r_ref, o_ref):
        o_ref[...] = (g_ref[0] + r_ref[...]).astype(o_ref.dtype)

    spec = pl.BlockSpec((1, blk, cols), lambda s, i, c_ref: (s, i, 0))
    return pl.pallas_call(
        body, name=name,
        grid_spec=pltpu.PrefetchScalarGridSpec(
            num_scalar_prefetch=1, grid=(SHARDS, rows // blk),
            in_specs=[pl.BlockSpec((1, 1, blk, cols), lambda s, i, c_ref: (s, c_ref[0], i, 0)), spec],
            out_specs=spec),
        out_shape=jax.ShapeDtypeStruct((SHARDS, rows, cols), MXU_DTYPE),
        compiler_params=_params(("parallel", "parallel")),
    )(lax.axis_index("c").astype(jnp.int32).reshape(1), grads, recv)


_HBM = pl.BlockSpec(memory_space=pltpu.HBM)
_SEM = pl.BlockSpec(memory_space=pltpu.SEMAPHORE)
_DATAFLOW = pltpu.SideEffectType.DATAFLOW_SIDE_EFFECTING


def _scatter_copies(p_refs, l_refs, send_sems, recv_sems):
    x, y, c, chips = _mesh_position()
    return [pltpu.make_async_remote_copy(src_ref=p_refs[a].at[2 * chip[0] + chip[1]], dst_ref=l_refs[a].at[j],
                                         send_sem=send_sems.at[3 * a + j], recv_sem=recv_sems.at[3 * a + j],
                                         device_id=(*chip, c), device_id_type=MESH)
            for a in range(len(p_refs)) for j, chip in enumerate(chips)]


def _exchange_copies(g_refs, l_refs, send_sems, recv_sems):
    x, y, c, _ = _mesh_position()
    return [pltpu.make_async_remote_copy(src_ref=g_refs[a].at[s, 1 - c], dst_ref=l_refs[a].at[s],
                                         send_sem=send_sems.at[SHARDS * a + s], recv_sem=recv_sems.at[SHARDS * a + s],
                                         device_id=(x, y, 1 - c), device_id_type=MESH)
            for a in range(len(g_refs)) for s in range(SHARDS)]


def _copies_start(name, copies, count, sources, land_shapes):
    n = len(sources)
    lands = [lax.empty(shape, src.dtype) for shape, src in zip(land_shapes, sources)]

    def body(*refs):
        s_refs, l_refs, send_sems, recv_sems, token = refs[:n], refs[n:2 * n], refs[2 * n], refs[2 * n + 1], refs[-1]
        for cp in copies(s_refs, l_refs, send_sems, recv_sems):
            cp.start()
        token[...] = jnp.zeros_like(token)

    operands = [pltpu.with_memory_space_constraint(a, pltpu.HBM) for a in (*sources, *lands)]
    return pl.pallas_call(
        body, name=name, in_specs=[_HBM] * (2 * n),
        out_shape=(pltpu.SemaphoreType.DMA((count,)), pltpu.SemaphoreType.DMA((count,)),
                   *[pltpu.HBM(a.shape, a.dtype) for a in operands], jax.ShapeDtypeStruct((8, 128), F32)),
        out_specs=(_SEM, _SEM, *[_HBM] * (2 * n), pl.BlockSpec(memory_space=pltpu.VMEM)),
        input_output_aliases={i: 2 + i for i in range(2 * n)},
        compiler_params=pltpu.CompilerParams(has_side_effects=_DATAFLOW),
    )(*operands)


def _copies_wait(name, copies, send_sems, recv_sems, passed, after):
    n = len(passed) // 2

    def body(*refs):
        s_refs, l_refs, send_s, recv_s = refs[:n], refs[n:2 * n], refs[2 * n], refs[2 * n + 1]
        for cp in copies(s_refs, l_refs, send_s, recv_s):
            cp.wait_send()
            cp.wait_recv()

    return pl.pallas_call(
        body, name=name, in_specs=[_HBM] * (2 * n) + [_SEM, _SEM, ANY],
        out_shape=[pltpu.HBM(a.shape, a.dtype) for a in passed], out_specs=[_HBM] * (2 * n),
        input_output_aliases={i: i for i in range(2 * n)},
        compiler_params=pltpu.CompilerParams(has_side_effects=_DATAFLOW),
    )(*passed, send_sems, recv_sems, after)


def _chip_sum(pairs, recv, blk, name):
    _, rows, cols = pairs.shape

    def body(pos_ref, p_ref, r_ref, o_ref):
        o_ref[0] = ((p_ref[0].astype(F32) + r_ref[0].astype(F32)) + r_ref[1].astype(F32)) + r_ref[2].astype(F32)

    pos = jnp.stack([2 * lax.axis_index("x") + lax.axis_index("y"), lax.axis_index("c")]).astype(jnp.int32)
    return pl.pallas_call(
        body, name=name,
        grid_spec=pltpu.PrefetchScalarGridSpec(
            num_scalar_prefetch=1, grid=(rows // blk,),
            in_specs=[pl.BlockSpec((1, blk, cols), lambda i, pos_ref: (pos_ref[0], i, 0)),
                      pl.BlockSpec((3, blk, cols), lambda i, pos_ref: (0, i, 0))],
            out_specs=pl.BlockSpec((1, blk, cols), lambda i, pos_ref: (pos_ref[1], i, 0))),
        out_shape=jax.ShapeDtypeStruct((2, rows, cols), F32),
        compiler_params=_params(("parallel",)),
    )(pos, pairs, recv)


def _share_total(totals):
    n = len(totals)

    def body(*refs):
        t_refs, out_refs, (send_sems, recv_sems) = refs[:n], refs[n:2 * n], refs[2 * n:]
        x, y, c, _ = _mesh_position()
        copies = [pltpu.make_async_remote_copy(src_ref=t_refs[a].at[c], dst_ref=out_refs[a].at[c], send_sem=send_sems.at[a],
                                               recv_sem=recv_sems.at[a], device_id=(x, y, 1 - c), device_id_type=MESH)
                  for a in range(n)]
        for cp in copies:
            cp.start()
        for a in range(n):
            other = out_refs[a].at[1 - c]
            pltpu.make_async_remote_copy(src_ref=other, dst_ref=other, send_sem=send_sems.at[a], recv_sem=recv_sems.at[a],
                                         device_id=(x, y, c), device_id_type=MESH).wait_recv()
        for cp in copies:
            cp.wait_send()

    return pl.pallas_call(
        body, name="share_total", in_specs=[ANY] * n, out_specs=[ANY] * n,
        out_shape=[jax.ShapeDtypeStruct(t.shape, F32) for t in totals],
        scratch_shapes=[pltpu.SemaphoreType.DMA((n,)), pltpu.SemaphoreType.DMA((n,))],
        input_output_aliases={a: a for a in range(n)},
    )(*totals)


def _allreduce_small(block):
    def body(b_ref, out_ref, gath, send_sems, recv_sems):
        x, y, c, _ = _mesh_position()
        me = 4 * x + 2 * y + c
        gath[me] = b_ref[...]
        copies = []
        for k in range(1, 8):
            peer = (x ^ (k >> 2), y ^ ((k >> 1) & 1), c ^ (k & 1))
            copies.append(pltpu.make_async_remote_copy(src_ref=b_ref, dst_ref=gath.at[me], send_sem=send_sems.at[k - 1],
                                                       recv_sem=recv_sems.at[k - 1], device_id=peer, device_id_type=MESH))
        for cp in copies:
            cp.start()
        for k in range(1, 8):
            src = 4 * (x ^ (k >> 2)) + 2 * (y ^ ((k >> 1) & 1)) + (c ^ (k & 1))
            pltpu.make_async_remote_copy(src_ref=b_ref, dst_ref=gath.at[src], send_sem=send_sems.at[k - 1],
                                         recv_sem=recv_sems.at[k - 1], device_id=(x, y, c), device_id_type=MESH).wait_recv()
        for cp in copies:
            cp.wait_send()
        acc = gath[0]
        for d in range(1, 8):
            acc = acc + gath[d]
        out_ref[...] = acc

    vm = pl.BlockSpec(memory_space=pltpu.VMEM)
    return pl.pallas_call(
        body, name="allreduce_small", in_specs=[vm], out_specs=vm,
        out_shape=jax.ShapeDtypeStruct((8, D_MODEL), F32),
        scratch_shapes=[pltpu.VMEM((8, 8, D_MODEL), F32), pltpu.SemaphoreType.DMA((7,)), pltpu.SemaphoreType.DMA((7,))],
    )(block)


def _adamw(w, g, m, v, name):
    rows, cols = w.shape
    tr = 128 if rows % 128 == 0 else rows

    def body(w_ref, g_ref, m_ref, v_ref, d_ref, nm_ref, nv_ref):
        gv = g_ref[...]
        nm = ADAM_B1 * m_ref[...] + (1.0 - ADAM_B1) * gv
        nv = ADAM_B2 * v_ref[...] + (1.0 - ADAM_B2) * (gv * gv)
        m_hat = nm / (1.0 - ADAM_B1 ** ADAM_STEP)
        v_hat = nv / (1.0 - ADAM_B2 ** ADAM_STEP)
        d_ref[...] = -ADAM_LR * (m_hat / (jnp.sqrt(v_hat) + ADAM_EPS) + ADAM_WD * w_ref[...])
        nm_ref[...] = nm
        nv_ref[...] = nv

    spec = pl.BlockSpec((tr, cols), lambda i: (i, 0))
    shape = jax.ShapeDtypeStruct((rows, cols), F32)
    return pl.pallas_call(
        body, name=name, grid=(rows // tr,), in_specs=[spec] * 4, out_specs=[spec] * 3, out_shape=[shape] * 3,
        compiler_params=_params(("parallel",)),
    )(w, g, m, v)


def kernel(x, norm_w, w_in, conv_w, a_log, dt_bias, gdn_norm_w, w_up_a, w_up_b, w_out, final_norm_w, loss_target, m_norm_w, m_w_in, m_conv_w, m_a_log, m_dt_bias, m_gdn_norm_w, m_w_up_a, m_w_up_b, m_w_out, m_final_norm_w, v_norm_w, v_w_in, v_conv_w, v_a_log, v_dt_bias, v_gdn_norm_w, v_w_up_a, v_w_up_b, v_w_out, v_final_norm_w):
    conv_hi = conv_w[0].astype(MXU_DTYPE)
    conv_lo = (conv_w[0] - conv_hi.astype(F32)).astype(MXU_DTYPE)
    big = w_in[0].astype(MXU_DTYPE).reshape(2, *BIG_HALF)
    slab = _pack_slab(w_up_a[0].astype(MXU_DTYPE), w_up_b[0].astype(MXU_DTYPE), w_out[0].astype(MXU_DTYPE), conv_hi, conv_lo)
    own_shard = 2 * lax.axis_index("x") + lax.axis_index("y")
    bigs, slabs = _gather_weights([big, slab])
    bigs = lax.dynamic_update_slice(bigs, big[None], (own_shard, 0, 0, 0)).reshape(SHARDS, D_MODEL, W_IN_SHARD)
    slabs = lax.dynamic_update_slice(slabs, slab[None], (own_shard, 0, 0, 0))
    parts = [_unpack_slab(slabs[s]) for s in range(SHARDS)]
    split = BA_END - (SHARDS - 1) * W_IN_SHARD
    wp = jnp.concatenate([bigs[s] for s in range(SHARDS - 1)]
                         + [bigs[-1][:, :split], jnp.zeros((D_MODEL, SEG_GA - BA_END), MXU_DTYPE), bigs[-1][:, split:]], axis=1)
    w_up_a_full = jnp.concatenate([p[0] for p in parts], axis=1)
    w_up_b_full = jnp.concatenate([p[1] for p in parts], axis=1)
    w_out_full = jnp.concatenate([p[2] for p in parts], axis=0)
    conv_full = jnp.concatenate([p[3].astype(F32) + p[4].astype(F32) for p in parts], axis=1)

    blocks, tags = (128, HALF_ROWS), ("w_in", "slab")

    def start_reduce(d_wp, d_w_up_a, d_w_up_b, d_w_out, d_conv_w):
        d_w_in = [d_wp[:, s * W_IN_SHARD:(s + 1) * W_IN_SHARD] for s in range(SHARDS - 1)]
        d_w_in.append(jnp.concatenate([d_wp[:, (SHARDS - 1) * W_IN_SHARD:BA_END], d_wp[:, SEG_GA:]], axis=1))
        zero_conv = jnp.zeros((GDN_CONV, CONV_SHARD), F32)
        grads = [jnp.stack(d_w_in).reshape(SHARDS, 2, *BIG_HALF),
                 jnp.stack([_pack_slab(d_w_up_a[:, s * 256:(s + 1) * 256], d_w_up_b[:, s * 256:(s + 1) * 256],
                                       d_w_out[s * 256:(s + 1) * 256], d_conv_w[:, s * CONV_SHARD:(s + 1) * CONV_SHARD],
                                       zero_conv) for s in range(SHARDS)])]
        *in_flight, token = _copies_start("exchange_start", _exchange_copies, 2 * SHARDS, grads,
                                          [(SHARDS, *gr.shape[2:]) for gr in grads])
        return in_flight, token

    def continue_reduce(in_flight, after):
        send_sems, recv_sems, *passed = in_flight
        arrived = _copies_wait("exchange_wait", _exchange_copies, send_sems, recv_sems, passed, after)
        grads, from_sibling = arrived[:2], arrived[2:]
        pairs = [_pair_sum(gr, fs, blk, f"pair_sum_{tag}") for gr, fs, blk, tag in zip(grads, from_sibling, blocks, tags)]
        *in_flight, token = _copies_start("scatter_start", _scatter_copies, 2 * 3, pairs, [(3, *p.shape[1:]) for p in pairs])
        return in_flight, token

    g = _local_step(x[0], loss_target[0], norm_w, wp, conv_full, a_log, dt_bias, gdn_norm_w,
                    w_up_a_full, w_up_b_full, w_out_full, final_norm_w[None], start_reduce, continue_reduce)

    send_sems, recv_sems, *passed = g["in_flight"]
    arrived = _copies_wait("scatter_wait", _scatter_copies, send_sems, recv_sems, passed, g["grad_x"])
    pairs, from_chips = arrived[:2], arrived[2:]
    total_big, total_slab = _share_total([_chip_sum(p, fc, blk, f"chip_sum_{tag}")
                                          for p, fc, blk, tag in zip(pairs, from_chips, blocks, tags)])
    g_w_in = total_big.reshape(D_MODEL, W_IN_SHARD)
    g_w_up_a, g_w_up_b, g_w_out, g_conv, _ = _unpack_slab(total_slab)

    row2 = jnp.concatenate([g["gdn_norm_w"], g["a_log"], g["dt_bias"], g["loss"][0:1, 0:1],
                            jnp.zeros((1, D_MODEL - HEAD_DIM - 2 * HEADS - 1), F32)], axis=1)
    small = _allreduce_small(jnp.concatenate([g["norm_w"], g["final_norm_w"], row2, jnp.zeros((5, D_MODEL), F32)], axis=0))
    g_norm, g_final = small[0:1], small[1]
    g_gnw, g_alog, g_dt = small[2:3, 0:HEAD_DIM], small[2:3, HEAD_DIM:HEAD_DIM + HEADS], small[2:3, HEAD_DIM + HEADS:HEAD_DIM + 2 * HEADS]
    loss = small[2, HEAD_DIM + 2 * HEADS]

    names = ["norm_w", "w_in", "conv_w", "a_log", "dt_bias", "gdn_norm_w", "w_up_a", "w_up_b", "w_out", "final_norm_w"]
    weights = dict(zip(names, (norm_w, w_in, conv_w, a_log, dt_bias, gdn_norm_w, w_up_a, w_up_b, w_out, final_norm_w)))
    ms = dict(zip(names, (m_norm_w, m_w_in, m_conv_w, m_a_log, m_dt_bias, m_gdn_norm_w, m_w_up_a, m_w_up_b, m_w_out, m_final_norm_w)))
    vs = dict(zip(names, (v_norm_w, v_w_in, v_conv_w, v_a_log, v_dt_bias, v_gdn_norm_w, v_w_up_a, v_w_up_b, v_w_out, v_final_norm_w)))
    grads2d = dict(norm_w=g_norm, w_in=g_w_in, conv_w=g_conv, a_log=g_alog, dt_bias=g_dt, gdn_norm_w=g_gnw,
                   w_up_a=g_w_up_a, w_up_b=g_w_up_b, w_out=g_w_out, final_norm_w=g_final[None])
    grad_out, delta, new_m, new_v = [], [], [], []
    for n in names:
        shape = weights[n].shape
        two_d = grads2d[n].shape
        d, nm, nv = _adamw(weights[n].reshape(two_d), grads2d[n], ms[n].reshape(two_d), vs[n].reshape(two_d), f"adamw_{n}")
        grad_out.append(grads2d[n].reshape(shape))
        delta.append(d.reshape(shape))
        new_m.append(nm.reshape(shape))
        new_v.append(nv.reshape(shape))
    return (loss, g["grad_x"][None], *grad_out, *delta, *new_m, *new_v)
```

```python
import functools

import jax
import jax.numpy as jnp
from jax import lax
from jax.experimental import pallas as pl
from jax.experimental.pallas import tpu as pltpu

F32 = jnp.float32
MXU_DTYPE = jnp.bfloat16
HIGHEST = lax.Precision.HIGHEST

D_MODEL = 1024
HEADS = 8
HEAD_DIM = 64
WIDTH = HEADS * HEAD_DIM
NORM_EPS = 1e-6
ROPE_THETA = 10000.0
ATT_BLOCK = 128
DILATIONS = (1, 4, 16)
GDN_CHUNK = 64
GDN_CONV = 4
IN_WIDTH = 9232
SEG_A, SEG_ZA, SEG_B, SEG_ZB, SEG_BA, SEG_GA, SEG_GB, PACKED_WIDTH = 0, 4608, 5120, 6656, 7168, 7680, 8704, 9728
BA_END = 7184
VMEM_LIMIT = 56 * 1024 * 1024

ADAM_LR, ADAM_B1, ADAM_B2, ADAM_EPS, ADAM_WD, ADAM_STEP = 0.001, 0.9, 0.999, 1e-08, 0.01, 10

_NN = (((1,), (0,)), ((), ()))
_NT = (((1,), (1,)), ((), ()))
_TN = (((0,), (0,)), ((), ()))


def _params(sem):
    return pltpu.CompilerParams(dimension_semantics=sem, vmem_limit_bytes=VMEM_LIMIT)


def _mxu(a, b, dims):
    return lax.dot_general(a.astype(MXU_DTYPE), b.astype(MXU_DTYPE), dims, preferred_element_type=F32)


def _sigmoid(x):
    return 1.0 / (1.0 + jnp.exp(-x))


def _softplus(x):
    return jnp.maximum(x, 0.0) + jnp.log(1.0 + jnp.exp(-jnp.abs(x)))


def _iota(shape, axis):
    return lax.broadcasted_iota(jnp.int32, shape, axis)


def _matmul(a, b, mode, name, out_dtype=F32, tm=512, tn=512, tk=512):
    if mode == "nn":
        (m, k), (k2, n) = a.shape, b.shape
    elif mode == "nt":
        (m, k), (n, k2) = a.shape, b.shape
    else:
        (k, m), (k2, n) = a.shape, b.shape
    assert k == k2
    tm, tn, tk = min(tm, m), min(tn, n), min(tk, k)
    assert m % tm == 0 and n % tn == 0 and k % tk == 0
    nk = k // tk
    dims = {"nn": _NN, "nt": _NT, "tn": _TN}[mode]

    assert out_dtype == F32

    def body(a_ref, b_ref, o_ref):
        kk = pl.program_id(2)
        part = _mxu(a_ref[...], b_ref[...], dims)

        @pl.when(kk == 0)
        def _():
            o_ref[...] = part

        @pl.when(kk > 0)
        def _():
            o_ref[...] += part

    a_spec = pl.BlockSpec((tk, tm), lambda i, j, kk: (kk, i)) if mode == "tn" else pl.BlockSpec((tm, tk), lambda i, j, kk: (i, kk))
    b_spec = pl.BlockSpec((tn, tk), lambda i, j, kk: (j, kk)) if mode == "nt" else pl.BlockSpec((tk, tn), lambda i, j, kk: (kk, j))
    return pl.pallas_call(
        body, name=name, grid=(m // tm, n // tn, nk), in_specs=[a_spec, b_spec],
        out_specs=pl.BlockSpec((tm, tn), lambda i, j, kk: (i, j)),
        out_shape=jax.ShapeDtypeStruct((m, n), out_dtype),
        compiler_params=_params(("parallel", "parallel", "arbitrary")),
    )(a, b)


def _norm_proj(x, norm_w, wp, tm=1024, tn=PACKED_WIDTH // 4):
    t = x.shape[0]
    tm = min(tm, t)

    def body(x_ref, nw_ref, w_ref, proj_ref, ht_ref, h_scr):
        @pl.when(pl.program_id(1) == 0)
        def _():
            xf = x_ref[...]
            r = lax.rsqrt(jnp.mean(xf * xf, axis=-1, keepdims=True) + NORM_EPS)
            h = xf * r * nw_ref[...]
            h_scr[...] = h.astype(h_scr.dtype)
            ht_ref[...] = h.T.astype(ht_ref.dtype)

        proj_ref[...] = jnp.dot(h_scr[...], w_ref[...], preferred_element_type=F32)

    return pl.pallas_call(
        body, name="norm_proj", grid=(t // tm, PACKED_WIDTH // tn),
        in_specs=[pl.BlockSpec((tm, D_MODEL), lambda i, j: (i, 0)),
                  pl.BlockSpec((1, D_MODEL), lambda i, j: (0, 0)),
                  pl.BlockSpec((D_MODEL, tn), lambda i, j: (0, j))],
        out_specs=[pl.BlockSpec((tm, tn), lambda i, j: (i, j)),
                   pl.BlockSpec((D_MODEL, tm), lambda i, j: (0, i))],
        out_shape=[jax.ShapeDtypeStruct((t, PACKED_WIDTH), F32), jax.ShapeDtypeStruct((D_MODEL, t), MXU_DTYPE)],
        scratch_shapes=[pltpu.VMEM((tm, D_MODEL), MXU_DTYPE)],
        compiler_params=_params(("parallel", "arbitrary")),
    )(x, norm_w, wp)


def _rope_tables(t):
    lane = jnp.arange(128)
    inv_freq = ROPE_THETA ** (-jnp.arange(0, HEAD_DIM, 2, dtype=F32) / HEAD_DIM)
    freq = jnp.concatenate([inv_freq] * 4)
    coarse = (jnp.arange(t // 128, dtype=F32) * 128.0)[:, None] * freq[None, :]
    fine = jnp.arange(128, dtype=F32)[:, None] * freq[None, :]
    ca, sa, cb, sb = jnp.cos(coarse)[:, None, :], jnp.sin(coarse)[:, None, :], jnp.cos(fine)[None], jnp.sin(fine)[None]
    cos = (ca * cb - sa * sb).reshape(t, 128)
    sin = (sa * cb + ca * sb).reshape(t, 128)
    first_half = (lane % HEAD_DIM) < HEAD_DIM // 2
    return cos, jnp.where(first_half, -sin, 0.0), jnp.where(first_half, 0.0, sin)


def _rope_cols(x, cos, sin_lo, sin_hi, sign):
    outs = []
    for c in range(x.shape[1] // 128):
        xc = x[:, c * 128:(c + 1) * 128]
        rot = pltpu.roll(xc, 96, 1) * sin_lo + pltpu.roll(xc, 32, 1) * sin_hi
        outs.append(xc * cos + sign * rot)
    return jnp.concatenate(outs, axis=1)


def _rope_block(x, cos, sin_lo, sin_hi, sign, q_scale=None):
    q = _rope_cols(x[:, :WIDTH], cos, sin_lo, sin_hi, sign)
    k = _rope_cols(x[:, WIDTH:2 * WIDTH], cos, sin_lo, sin_hi, sign)
    return jnp.concatenate([q if q_scale is None else q * q_scale, k, x[:, 2 * WIDTH:]], axis=1)


def _tile_scratch(tm, cols):
    return pltpu.VMEM((cols // 128, tm, 128), F32)


def _store_tile(scr, y):
    for c in range(scr.shape[0]):
        scr[c] = y[:, c * 128:(c + 1) * 128]


def _load_tile(scr):
    return jnp.concatenate([scr[c] for c in range(scr.shape[0])], axis=1)


def _to_strided_view(scr, o_ref, d, width=None, col0=0):
    n, tm, _ = scr.shape
    width = n * 128 if width is None else width
    for r in range(d):
        for c in range(n):
            at = r * width + col0 + c * 128
            o_ref[:, at:at + 128] = scr[c, pl.ds(r, tm // d, stride=d), :].astype(o_ref.dtype)


def _from_strided_view(i_ref, scr, d):
    n, tm, _ = scr.shape
    for r in range(d):
        for c in range(n):
            scr[c, pl.ds(r, tm // d, stride=d), :] = i_ref[:, (r * n + c) * 128:(r * n + c + 1) * 128].astype(F32)


def _strided_spec(tm, d, cols):
    return pl.BlockSpec((tm // d, d * cols), lambda i: (i, 0))


def _rope_fwd(proj, tables, tm=512):
    t = proj.shape[0]
    cols = 3 * WIDTH

    def body(x_ref, c_ref, sl_ref, sh_ref, o0, o1, o2, scr):
        for g, (d, o_ref) in enumerate(zip(DILATIONS, (o0, o1, o2))):
            y = _rope_block(x_ref[:, g * cols:(g + 1) * cols], c_ref[...], sl_ref[...], sh_ref[...], 1.0, HEAD_DIM ** -0.5)
            if d == 1:
                o_ref[...] = y.astype(o_ref.dtype)
            else:
                _store_tile(scr, y)
                _to_strided_view(scr, o_ref, d)

    tab = pl.BlockSpec((tm, 128), lambda i: (i, 0))
    return pl.pallas_call(
        body, name="rope_fwd", grid=(t // tm,),
        in_specs=[pl.BlockSpec((tm, 3 * cols), lambda i: (i, 0)), tab, tab, tab],
        out_specs=[_strided_spec(tm, d, cols) for d in DILATIONS],
        out_shape=[jax.ShapeDtypeStruct((t // d, d * cols), MXU_DTYPE) for d in DILATIONS],
        scratch_shapes=[_tile_scratch(tm, cols)],
        compiler_params=_params(("parallel",)),
    )(proj, *tables)


def _rope_bwd(dproj, dqs, dkvs, tables, tm=512):
    t = dproj.shape[0]
    cols = 3 * WIDTH

    def body(dp_ref, q0, q1, q2, kv0, kv1, kv2, c_ref, sl_ref, sh_ref, o_ref, scr_q, scr_kv):
        for g, (d, q_ref, kv_ref) in enumerate(zip(DILATIONS, (q0, q1, q2), (kv0, kv1, kv2))):
            if d == 1:
                x = jnp.concatenate([q_ref[...], kv_ref[...]], axis=1)
            else:
                _from_strided_view(q_ref, scr_q, d)
                _from_strided_view(kv_ref, scr_kv, d)
                x = jnp.concatenate([_load_tile(scr_q), _load_tile(scr_kv)], axis=1)
            y = _rope_block(x, c_ref[...], sl_ref[...], sh_ref[...], -1.0)
            o_ref[:, g * cols:(g + 1) * cols] = y.astype(o_ref.dtype)

    tab = pl.BlockSpec((tm, 128), lambda i: (i, 0))
    return pl.pallas_call(
        body, name="rope_bwd", grid=(t // tm,),
        in_specs=[pl.BlockSpec(memory_space=pl.ANY)] + [_strided_spec(tm, d, WIDTH) for d in DILATIONS]
        + [_strided_spec(tm, d, 2 * WIDTH) for d in DILATIONS] + [tab, tab, tab],
        out_specs=pl.BlockSpec((tm, 3 * cols), lambda i: (i, 0)),
        out_shape=jax.ShapeDtypeStruct((t, PACKED_WIDTH), MXU_DTYPE),
        scratch_shapes=[_tile_scratch(tm, WIDTH), _tile_scratch(tm, 2 * WIDTH)],
        input_output_aliases={0: 0},
        compiler_params=_params(("parallel",)),
    )(dproj, *dqs, *dkvs, *tables)


def _att_masks():
    qi = _iota((ATT_BLOCK, ATT_BLOCK), 0)
    kj = _iota((ATT_BLOCK, ATT_BLOCK), 1)
    return kj <= qi, kj >= qi


def _att_fwd(qkv, d, name):
    rows = qkv.shape[0]
    nb = rows // ATT_BLOCK

    def body(q_ref, kc_ref, kp_ref, vc_ref, vp_ref, o_ref, lse_ref):
        has_prev = pl.program_id(1) > 0
        m_cur, m_prev = _att_masks()
        m_prev = m_prev & has_prev
        hs = range(HEADS)
        sls = [slice(h * HEAD_DIM, (h + 1) * HEAD_DIM) for h in hs]
        qs = [q_ref[:, sl] for sl in sls]
        s_c = [jnp.where(m_cur, _mxu(qs[h], kc_ref[:, sls[h]], _NT), -jnp.inf) for h in hs]
        s_p = [jnp.where(m_prev, _mxu(qs[h], kp_ref[:, sls[h]], _NT), -jnp.inf) for h in hs]
        m = [jnp.max(jnp.maximum(s_c[h], s_p[h]), axis=1, keepdims=True) for h in hs]
        p_c = [jnp.exp(s_c[h] - m[h]) for h in hs]
        p_p = [jnp.exp(s_p[h] - m[h]) for h in hs]
        den = [jnp.sum(p_c[h] + p_p[h], axis=1, keepdims=True) for h in hs]
        o = [_mxu(p_c[h], vc_ref[:, sls[h]], _NN) + _mxu(p_p[h], vp_ref[:, sls[h]], _NN) for h in hs]
        lane = _iota((1, 128), 1)
        lse = jnp.zeros((ATT_BLOCK, 128), F32)
        for h in hs:
            o_ref[:, sls[h]] = o[h] / den[h]
            lse = jnp.where(lane == h, m[h] + jnp.log(den[h]), lse)
        lse_ref[...] = lse

    def cur(c):
        return pl.BlockSpec((ATT_BLOCK, WIDTH), lambda r, i: (i, 3 * r + c))

    def prev(c):
        return pl.BlockSpec((ATT_BLOCK, WIDTH), lambda r, i: (jnp.maximum(i - 1, 0), 3 * r + c))

    return pl.pallas_call(
        body, name=name, grid=(d, nb), in_specs=[cur(0), cur(1), prev(1), cur(2), prev(2)],
        out_specs=[pl.BlockSpec((ATT_BLOCK, WIDTH), lambda r, i: (i, r)), pl.BlockSpec((ATT_BLOCK, 128), lambda r, i: (i, r))],
        out_shape=[jax.ShapeDtypeStruct((rows, d * WIDTH), F32), jax.ShapeDtypeStruct((rows, d * 128), F32)],
        compiler_params=_params(("parallel", "arbitrary")),
    )(qkv, qkv, qkv, qkv, qkv)


def _att_bwd(qkv, do, lse, delta, d, name):
    rows = qkv.shape[0]
    nb = rows // ATT_BLOCK
    scale = HEAD_DIM ** -0.5

    def body(q_ref, kc_ref, kp_ref, vc_ref, vp_ref, do_ref, l_ref, dl_ref, dq_ref, dkv_ref, own):
        i = pl.program_id(1)
        m_cur, m_prev = _att_masks()
        m_prev = m_prev & (i > 0)

        hs = range(HEADS)
        sls = [slice(h * HEAD_DIM, (h + 1) * HEAD_DIM) for h in hs]
        col = [slice(h, h + 1) for h in hs]

        def probs(k_r, mask):
            s = [_mxu(q_ref[:, sls[h]], k_r[:, sls[h]], _NT) for h in hs]
            return [jnp.where(mask, jnp.exp(s[h] - l_ref[:, col[h]]), 0.0) for h in hs]

        def dscores(p, v_r):
            dp = [_mxu(do_ref[:, sls[h]], v_r[:, sls[h]], _NT) for h in hs]
            return [(p[h] * (dp[h] - dl_ref[:, col[h]])).astype(MXU_DTYPE) for h in hs]

        @pl.when(i == 0)
        def _():
            own[...] = jnp.zeros_like(own)

        @pl.when(i < nb)
        def _():
            p_c = probs(kc_ref, m_cur)
            ds_c = dscores(p_c, vc_ref)
            p_p = probs(kp_ref, m_prev)
            ds_p = dscores(p_p, vp_ref)
            dq = [_mxu(ds_c[h], kc_ref[:, sls[h]], _NN) + _mxu(ds_p[h], kp_ref[:, sls[h]], _NN) for h in hs]
            dk_p = [_mxu(ds_p[h], q_ref[:, sls[h]], _TN) for h in hs]
            dv_p = [_mxu(p_p[h], do_ref[:, sls[h]], _TN) for h in hs]
            dk_c = [_mxu(ds_c[h], q_ref[:, sls[h]], _TN) for h in hs]
            dv_c = [_mxu(p_c[h], do_ref[:, sls[h]], _TN) for h in hs]
            for h in hs:
                vs = slice(WIDTH + h * HEAD_DIM, WIDTH + (h + 1) * HEAD_DIM)
                dq_ref[:, sls[h]] = dq[h] * scale
                dkv_ref[:, sls[h]] = own[:, sls[h]] + dk_p[h]
                dkv_ref[:, vs] = own[:, vs] + dv_p[h]
                own[:, sls[h]] = dk_c[h]
                own[:, vs] = dv_c[h]

        @pl.when(i == nb)
        def _():
            dkv_ref[...] = own[...]

    def qkv_spec(c, shift):
        return pl.BlockSpec((ATT_BLOCK, WIDTH), lambda r, i: (jnp.clip(i + shift, 0, nb - 1), 3 * r + c))

    tok = pl.BlockSpec((ATT_BLOCK, WIDTH), lambda r, i: (jnp.minimum(i, nb - 1), r))
    per_head = pl.BlockSpec((ATT_BLOCK, 128), lambda r, i: (jnp.minimum(i, nb - 1), r))
    return pl.pallas_call(
        body, name=name, grid=(d, nb + 1),
        in_specs=[qkv_spec(0, 0), qkv_spec(1, 0), qkv_spec(1, -1), qkv_spec(2, 0), qkv_spec(2, -1), tok, per_head, per_head],
        out_specs=[tok, pl.BlockSpec((ATT_BLOCK, 2 * WIDTH), lambda r, i: (jnp.maximum(i - 1, 0), r))],
        out_shape=[jax.ShapeDtypeStruct((rows, d * WIDTH), F32), jax.ShapeDtypeStruct((rows, d * 2 * WIDTH), F32)],
        scratch_shapes=[pltpu.VMEM((ATT_BLOCK, 2 * WIDTH), F32)],
        compiler_params=_params(("parallel", "arbitrary")),
    )(qkv, qkv, qkv, qkv, qkv, do, lse, delta)


def _att_merge(os_, lses, proj, tm=1024):
    t = proj.shape[0]

    def body(o0, o1, o2, l0, l1, l2, z_ref, oz_ref, o_ref, t0, t1, t2, s_o1, s_o2, s_l1, s_l2, s_t):
        _from_strided_view(o1, s_o1, DILATIONS[1])
        _from_strided_view(o2, s_o2, DILATIONS[2])
        _from_strided_view(l1, s_l1, DILATIONS[1])
        _from_strided_view(l2, s_l2, DILATIONS[2])
        a, b, c = l0[...], s_l1[0], s_l2[0]
        m = jnp.maximum(jnp.maximum(a, b), c)
        wa, wb, wc = jnp.exp(a - m), jnp.exp(b - m), jnp.exp(c - m)
        den = wa + wb + wc
        total = m + jnp.log(den)
        t0[...] = total
        s_t[0] = total
        _to_strided_view(s_t, t1, DILATIONS[1])
        _to_strided_view(s_t, t2, DILATIONS[2])
        spread = jnp.where(_iota((128, WIDTH), 1) // HEAD_DIM == _iota((128, WIDTH), 0), 1.0, 0.0).astype(MXU_DTYPE)
        ra, rb, rc = (_pair_sum_lanes(w / den, spread) for w in (wa, wb, wc))
        o = ra * o0[...] + rb * _load_tile(s_o1) + rc * _load_tile(s_o2)
        z = z_ref[...]
        o_ref[...] = o
        oz_ref[...] = (o * z * _sigmoid(z)).astype(oz_ref.dtype)

    tok = pl.BlockSpec((tm, WIDTH), lambda i: (i, 0))
    views = [_strided_spec(tm, d, WIDTH) for d in DILATIONS]
    per_head = [_strided_spec(tm, d, 128) for d in DILATIONS]
    return pl.pallas_call(
        body, name="att_merge", grid=(t // tm,),
        in_specs=views + per_head + [pl.BlockSpec((tm, WIDTH), lambda i: (i, SEG_ZA // WIDTH))],
        out_specs=[tok, tok] + per_head,
        out_shape=[jax.ShapeDtypeStruct((t, WIDTH), MXU_DTYPE), jax.ShapeDtypeStruct((t, WIDTH), F32)]
        + [jax.ShapeDtypeStruct((t // d, d * 128), F32) for d in DILATIONS],
        scratch_shapes=[_tile_scratch(tm, WIDTH)] * 2 + [_tile_scratch(tm, 128)] * 3,
        compiler_params=_params(("parallel",)),
    )(*os_, *lses, proj)


def _att_merge_bwd(dproj, d_oz, o, proj, tm=1024):
    t = proj.shape[0]

    def body(dp_ref, doz_ref, o_ref, z_ref, dz_ref, do0, do1, do2, dl0, dl1, dl2, s_do, s_dl):
        z, ov, g = z_ref[...], o_ref[...], doz_ref[...]
        sg = _sigmoid(z)
        do = g * z * sg
        dz_ref[...] = (g * ov * sg * (1.0 + z * (1.0 - sg))).astype(dz_ref.dtype)
        do0[...] = do.astype(do0.dtype)
        _store_tile(s_do, do)
        prod = do * ov
        delta = jnp.zeros((tm, 128), F32)
        for p in range(HEADS // 2):
            delta += _pair_sum_lanes(prod[:, p * 128:(p + 1) * 128], _pair_to_lanes(p, 0))
        dl0[...] = delta
        s_dl[0] = delta
        for d, do_v, dl_v in ((DILATIONS[1], do1, dl1), (DILATIONS[2], do2, dl2)):
            _to_strided_view(s_do, do_v, d)
            _to_strided_view(s_dl, dl_v, d)

    tok = pl.BlockSpec((tm, WIDTH), lambda i: (i, 0))
    seg = pl.BlockSpec((tm, WIDTH), lambda i: (i, SEG_ZA // WIDTH))
    views = [_strided_spec(tm, d, WIDTH) for d in DILATIONS]
    per_head = [_strided_spec(tm, d, 128) for d in DILATIONS]
    return pl.pallas_call(
        body, name="att_merge_bwd", grid=(t // tm,),
        in_specs=[pl.BlockSpec(memory_space=pl.ANY), tok, tok, seg],
        out_specs=[seg] + views + per_head,
        out_shape=[jax.ShapeDtypeStruct((t, PACKED_WIDTH), MXU_DTYPE)]
        + [jax.ShapeDtypeStruct((t // d, d * WIDTH), MXU_DTYPE) for d in DILATIONS]
        + [jax.ShapeDtypeStruct((t // d, d * 128), F32) for d in DILATIONS],
        scratch_shapes=[_tile_scratch(tm, WIDTH), _tile_scratch(tm, 128)],
        input_output_aliases={0: 0},
        compiler_params=_params(("parallel",)),
    )(dproj, d_oz, o, proj)


def _shift_down(x, halo, s):
    if s == 0:
        return x
    xs = pltpu.roll(x, s, 0)
    head = jnp.where(_iota((8, x.shape[1]), 0) < s, pltpu.roll(halo, s, 0), xs[0:8])
    return jnp.concatenate([head, xs[8:]], axis=0)


def _shift_up(x, nxt, s):
    if s == 0:
        return x
    n = x.shape[0]
    xs = pltpu.roll(x, n - s, 0)
    tail = jnp.where(_iota((8, x.shape[1]), 0) >= 8 - s, pltpu.roll(nxt, 8 - s, 0), xs[n - 8:])
    return jnp.concatenate([xs[:n - 8], tail], axis=0)


def _conv_fwd(proj, conv_w, tm=1024):
    t = proj.shape[0]
    cb = SEG_B // WIDTH

    def body(x_ref, halo_ref, w_ref, c_ref):
        halo = jnp.where(pl.program_id(0) > 0, halo_ref[...], 0.0)
        x = x_ref[...]
        w = w_ref[...]
        acc = jnp.zeros((tm, WIDTH), F32)
        for j in range(GDN_CONV):
            acc += _shift_down(x, halo, GDN_CONV - 1 - j) * w[j:j + 1, :]
        c_ref[...] = acc

    return pl.pallas_call(
        body, name="conv_fwd", grid=(t // tm, 3),
        in_specs=[pl.BlockSpec((tm, WIDTH), lambda i, c: (i, cb + c)),
                  pl.BlockSpec((8, WIDTH), lambda i, c: (jnp.maximum(i * (tm // 8) - 1, 0), cb + c)),
                  pl.BlockSpec((GDN_CONV, WIDTH), lambda i, c: (0, c))],
        out_specs=pl.BlockSpec((tm, WIDTH), lambda i, c: (i, c)),
        out_shape=jax.ShapeDtypeStruct((t, 3 * WIDTH), F32),
        compiler_params=_params(("parallel", "parallel")),
    )(proj, proj, conv_w)


def _conv_bwd(dproj, dc, proj, conv_w, tm=1024):
    t = proj.shape[0]
    cb = SEG_B // WIDTH
    nt = t // tm

    def body(dp_ref, dc_ref, dcn_ref, x_ref, halo_ref, w_ref, dx_ref, dw_ref):
        i = pl.program_id(1)
        w = w_ref[...]
        dcn = jnp.where(i < nt - 1, dcn_ref[...], 0.0)
        dcv = dc_ref[...]
        acc = jnp.zeros((tm, WIDTH), F32)
        for j in range(GDN_CONV):
            acc += _shift_up(dcv, dcn, GDN_CONV - 1 - j) * w[j:j + 1, :]
        dx_ref[...] = acc.astype(dx_ref.dtype)
        halo = jnp.where(i > 0, halo_ref[...], 0.0)
        x = x_ref[...]
        row8 = _iota((8, WIDTH), 0)
        part = jnp.zeros((8, WIDTH), F32)
        for j in range(GDN_CONV):
            s = jnp.sum(dcv * _shift_down(x, halo, GDN_CONV - 1 - j), axis=0, keepdims=True)
            part += jnp.where(row8 == j, s, 0.0)

        @pl.when(i == 0)
        def _():
            dw_ref[...] = jnp.zeros_like(dw_ref)

        dw_ref[...] += part

    return pl.pallas_call(
        body, name="conv_bwd", grid=(3, nt),
        in_specs=[pl.BlockSpec(memory_space=pl.ANY),
                  pl.BlockSpec((tm, WIDTH), lambda c, i: (i, c)),
                  pl.BlockSpec((8, WIDTH), lambda c, i: (jnp.minimum((i + 1) * (tm // 8), t // 8 - 1), c)),
                  pl.BlockSpec((tm, WIDTH), lambda c, i: (i, cb + c)),
                  pl.BlockSpec((8, WIDTH), lambda c, i: (jnp.maximum(i * (tm // 8) - 1, 0), cb + c)),
                  pl.BlockSpec((GDN_CONV, WIDTH), lambda c, i: (0, c))],
        out_specs=[pl.BlockSpec((tm, WIDTH), lambda c, i: (i, cb + c)),
                   pl.BlockSpec((8, WIDTH), lambda c, i: (0, c))],
        out_shape=[jax.ShapeDtypeStruct((t, PACKED_WIDTH), MXU_DTYPE), jax.ShapeDtypeStruct((8, 3 * WIDTH), F32)],
        input_output_aliases={0: 0},
        compiler_params=_params(("parallel", "arbitrary")),
    )(dproj, dc, dc, proj, proj, conv_w)


def _chunk_matrices(tm):
    r, c = _iota((tm, tm), 0), _iota((tm, tm), 1)
    same = (r // GDN_CHUNK) == (c // GDN_CHUNK)
    return jnp.where(same & (c <= r), 1.0, 0.0), jnp.where(same, 1.0, 0.0)


def _gdn_gates(ba, a_log, dt_bias):
    al = ba + dt_bias
    return _sigmoid(ba), -jnp.exp(a_log) * _softplus(al), _sigmoid(al)


SCAN_HEADS = (0, 2, 4, 6, 1, 3, 5, 7)


def _head_lane_eye():
    return jnp.where(_iota((HEADS, 128), 1) == _iota((HEADS, 128), 0) + HEADS, 1.0, 0.0)


def _pair_ones():
    return jnp.where(_iota((128, 128), 0) // HEAD_DIM == _iota((128, 128), 1) // HEAD_DIM, 1.0, 0.0).astype(MXU_DTYPE)


def _pair_to_lanes(p, base):
    return jnp.where(_iota((128, 128), 1) == base + 2 * p + _iota((128, 128), 0) // HEAD_DIM, 1.0, 0.0).astype(MXU_DTYPE)


def _pair_sum_lanes(x, sel):
    hi, lo = _split(x)
    return jnp.dot(hi, sel, preferred_element_type=F32) + jnp.dot(lo, sel, preferred_element_type=F32)


def _gdn_prep(conv, proj, a_log, dt_bias, tm=512):
    t = proj.shape[0]
    nc = tm // GDN_CHUNK

    def body(c_ref, ba_ref, al_ref, dt_ref, q_ref, k_ref, v_ref, b_ref, g_ref, gl_ref, grow_ref):
        beta, g, _ = _gdn_gates(ba_ref[:, 0:128], al_ref[...], dt_ref[...])
        lmat, cmat = _chunk_matrices(tm)
        gc = jnp.dot(lmat, g, precision=HIGHEST, preferred_element_type=F32)
        gl = jnp.dot(cmat, g, precision=HIGHEST, preferred_element_type=F32)
        grow = lax.dot_general(_head_lane_eye(), gc, _NT, precision=HIGHEST, preferred_element_type=F32)
        ones = _pair_ones()
        first = _iota((1, 128), 1) < HEAD_DIM

        def spread(x, base, p):
            return jnp.where(first, x[:, base + 2 * p:base + 2 * p + 1], x[:, base + 2 * p + 1:base + 2 * p + 2])

        for p in range(HEADS // 2):
            for seg, ref, scale in ((0, q_ref, HEAD_DIM ** -0.5), (1, k_ref, 1.0), (2, v_ref, None)):
                c = c_ref[:, seg * WIDTH + p * 128:seg * WIDTH + (p + 1) * 128]
                a = c * _sigmoid(c)
                if scale is not None:
                    a = a * (lax.rsqrt(_pair_sum_lanes(a * a, ones) + NORM_EPS) * scale)
                ref[p] = a
            b_ref[p] = spread(beta, 0, p)
            g_ref[p] = spread(gc, HEADS, p)
            gl_ref[p] = spread(gl, HEADS, p)
        for pos, h in enumerate(SCAN_HEADS):
            for cc in range(nc):
                grow_ref[pos, cc] = grow[h:h + 1, cc * GDN_CHUNK:(cc + 1) * GDN_CHUNK]

    hm = pl.BlockSpec((HEADS // 2, tm, 128), lambda i: (0, i, 0))
    small = pl.BlockSpec((1, 128), lambda i: (0, 0))
    hm_shape = jax.ShapeDtypeStruct((HEADS // 2, t, 128), F32)
    return pl.pallas_call(
        body, name="gdn_prep", grid=(t // tm,),
        in_specs=[pl.BlockSpec((tm, 3 * WIDTH), lambda i: (i, 0)),
                  pl.BlockSpec((tm, WIDTH), lambda i: (i, SEG_BA // WIDTH)), small, small],
        out_specs=[hm] * 6 + [pl.BlockSpec((HEADS, nc, 1, GDN_CHUNK), lambda i: (0, i, 0, 0))],
        out_shape=[hm_shape] * 6 + [jax.ShapeDtypeStruct((HEADS, t // GDN_CHUNK, 1, GDN_CHUNK), F32)],
        compiler_params=_params(("parallel",)),
    )(conv, proj, a_log, dt_bias)


def _gdn_prep_bwd(dproj, conv, proj, a_log, dt_bias, dq, dk, dv, db, dg, dgl, dgrow, tm=512):
    t = proj.shape[0]
    nc = tm // GDN_CHUNK

    def body(dp_ref, c_ref, ba_ref, al_ref, dt_ref, dq_ref, dk_ref, dv_ref, db_ref, dg_ref, dgl_ref, dgrow_ref,
             dba_ref, dc_ref, small_ref, row_scr):
        beta, g, sig_al = _gdn_gates(ba_ref[:, 0:128], al_ref[...], dt_ref[...])
        d_beta = jnp.zeros((tm, 128), F32)
        d_gc = jnp.zeros((tm, 128), F32)
        d_gl = jnp.zeros((tm, 128), F32)
        ones = _pair_ones()
        for pos, h in enumerate(SCAN_HEADS):
            for cc in range(nc):
                row_scr[h:h + 1, cc * GDN_CHUNK:(cc + 1) * GDN_CHUNK] = dgrow_ref[pos, cc]
        for p in range(HEADS // 2):
            d_beta += _pair_sum_lanes(db_ref[p], _pair_to_lanes(p, 0))
            d_gc += _pair_sum_lanes(dg_ref[p], _pair_to_lanes(p, HEADS))
            d_gl += _pair_sum_lanes(dgl_ref[p], _pair_to_lanes(p, HEADS))
            for seg, ref, scale in ((0, dq_ref, HEAD_DIM ** -0.5), (1, dk_ref, 1.0), (2, dv_ref, None)):
                cols = slice(seg * WIDTH + p * 128, seg * WIDTH + (p + 1) * 128)
                c = c_ref[:, cols]
                sg = _sigmoid(c)
                da = ref[p]
                if scale is not None:
                    a = c * sg
                    r = lax.rsqrt(_pair_sum_lanes(a * a, ones) + NORM_EPS)
                    da = da * scale
                    da = r * da - a * (r * r * r) * _pair_sum_lanes(da * a, ones)
                dc_ref[:, cols] = da * (sg * (1.0 + c * (1.0 - sg)))
        d_gc += lax.dot_general(row_scr[...], _head_lane_eye(), _TN, precision=HIGHEST, preferred_element_type=F32)
        lmat, cmat = _chunk_matrices(tm)
        d_g = (lax.dot_general(lmat, d_gc, _TN, precision=HIGHEST, preferred_element_type=F32)
               + lax.dot_general(cmat, d_gl, _TN, precision=HIGHEST, preferred_element_type=F32))
        d_al = d_g * (-jnp.exp(al_ref[...])) * sig_al
        d_bl = d_beta * beta * (1.0 - beta)
        dba_ref[...] = jnp.concatenate([d_bl + d_al, jnp.zeros((tm, WIDTH - 128), F32)], axis=1).astype(dba_ref.dtype)
        row8 = _iota((8, 128), 0)
        part = (jnp.where(row8 == 0, jnp.sum(d_g * g, axis=0, keepdims=True), 0.0)
                + jnp.where(row8 == 1, jnp.sum(d_al, axis=0, keepdims=True), 0.0))

        @pl.when(pl.program_id(0) == 0)
        def _():
            small_ref[...] = jnp.zeros_like(small_ref)

        small_ref[...] += part

    hm = pl.BlockSpec((HEADS // 2, tm, 128), lambda i: (0, i, 0))
    small = pl.BlockSpec((1, 128), lambda i: (0, 0))
    seg = pl.BlockSpec((tm, WIDTH), lambda i: (i, SEG_BA // WIDTH))
    return pl.pallas_call(
        body, name="gdn_prep_bwd", grid=(t // tm,),
        in_specs=[pl.BlockSpec(memory_space=pl.ANY), pl.BlockSpec((tm, 3 * WIDTH), lambda i: (i, 0)), seg, small, small]
        + [hm] * 6 + [pl.BlockSpec((HEADS, nc, 1, GDN_CHUNK), lambda i: (0, i, 0, 0))],
        out_specs=[seg, pl.BlockSpec((tm, 3 * WIDTH), lambda i: (i, 0)), pl.BlockSpec((8, 128), lambda i: (0, 0))],
        out_shape=[jax.ShapeDtypeStruct((t, PACKED_WIDTH), MXU_DTYPE), jax.ShapeDtypeStruct((t, 3 * WIDTH), F32),
                   jax.ShapeDtypeStruct((8, 128), F32)],
        scratch_shapes=[pltpu.VMEM((HEADS, tm), F32)],
        input_output_aliases={0: 0},
        compiler_params=_params(("arbitrary",)),
    )(dproj, conv, proj, a_log, dt_bias, dq, dk, dv, db, dg, dgl, dgrow)


_BNN = (((2,), (1,)), ((0,), (0,)))
_BNT = (((2,), (2,)), ((0,), (0,)))
_BTN = (((1,), (1,)), ((0,), (0,)))


@jax.custom_vjp
def _MM_NN(a, b):
    return _mxu(a, b, _BNN)


@jax.custom_vjp
def _MM_NT(a, b):
    return _mxu(a, b, _BNT)


@jax.custom_vjp
def _MM_TN(a, b):
    return _mxu(a, b, _BTN)


_MM_NN.defvjp(lambda a, b: (_mxu(a, b, _BNN), (a, b)), lambda r, g: (_mxu(g, r[1], _BNT), _mxu(r[0], g, _BTN)))
_MM_NT.defvjp(lambda a, b: (_mxu(a, b, _BNT), (a, b)), lambda r, g: (_mxu(g, r[1], _BNN), _mxu(g, r[0], _BTN)))
_MM_TN.defvjp(lambda a, b: (_mxu(a, b, _BTN), (a, b)), lambda r, g: (_mxu(r[1], g, _BNT), _mxu(r[0], g, _BNN)))


def _split(a):
    hi = a.astype(MXU_DTYPE)
    return hi, (a - hi.astype(F32)).astype(MXU_DTYPE)


def _dot3(a, b, dims):
    (ah, al), (bh, bl) = a, b
    (ca,), (cb,) = dims[0]
    return lax.dot_general(jnp.concatenate([ah, ah, al], axis=ca), jnp.concatenate([bh, bl, bh], axis=cb), dims,
                           preferred_element_type=F32)


def _unit_lower_inverse(a):
    c = GDN_CHUNK
    eye = jnp.where(_iota((c, c), 0) == _iota((c, c), 1), 1.0, 0.0)
    x = eye - a
    p = a
    for _ in range(5):
        ps = _split(p)
        p = _dot3(ps, ps, _BNN)
        x = x + _dot3(_split(x), _split(p), _BNN)
    return x


@jax.custom_vjp
def _SAVED_INVERSE(a, t_inv):
    return t_inv


def _saved_inverse_bwd(t_inv, g):
    ts = _split(t_inv)
    return -_dot3(ts, _split(_dot3(_split(g), ts, _BNT)), _BTN), jnp.zeros_like(t_inv)


_SAVED_INVERSE.defvjp(lambda a, t_inv: (t_inv, t_inv), _saved_inverse_bwd)


def _gdn_chunk(q, k, v, beta, g1, g2, gl, state, t_inv=None):
    c = GDN_CHUNK
    if t_inv is None:
        _mm_nn, _mm_nt, _mm_tn = (functools.partial(_mxu, dims=dd) for dd in (_BNN, _BNT, _BTN))
    else:
        _mm_nn, _mm_nt, _mm_tn = _MM_NN, _MM_NT, _MM_TN
    row, col = _iota((c, c), 0), _iota((c, c), 1)
    incl, strict = row >= col, row > col
    decay = jnp.where(incl, jnp.exp(jnp.where(incl, g1 - g2, 0.0)), 0.0)
    eg = jnp.exp(g1)
    kb = k * beta
    a = _mm_nt(kb, k) * jnp.where(strict, decay, 0.0)
    inv = _unit_lower_inverse(a) if t_inv is None else _SAVED_INVERSE(a, t_inv)
    u = _mm_nn(inv, v * beta)
    w = _mm_nn(inv, kb * eg)
    attn = _mm_nt(q, k) * decay
    v_new = u - _mm_nn(w, state)
    o = _mm_nn(q * eg, state) + _mm_nn(attn, v_new)
    new_state = state * jnp.exp(gl) + _mm_tn(k * jnp.exp(gl - g1), v_new)
    return (o, new_state, inv) if t_inv is None else (o, new_state)


def _unpair(x):
    return jnp.concatenate([x[..., :HEAD_DIM], x[..., HEAD_DIM:]], axis=0)


def _gdn_fwd(q, k, v, beta, g, gl, grow, cpb=4):
    t = q.shape[1]
    rows = cpb * GDN_CHUNK
    lo, hi = slice(0, HEAD_DIM), slice(HEAD_DIM, 2 * HEAD_DIM)

    def body(q_ref, k_ref, v_ref, b_ref, g_ref, gl_ref, grow_ref, o_ref, st_ref, inv_ref, state):
        @pl.when(pl.program_id(0) == 0)
        def _():
            state[...] = jnp.zeros_like(state)

        s = state[...]
        for cc in range(cpb):
            sl = slice(cc * GDN_CHUNK, (cc + 1) * GDN_CHUNK)
            st_ref[:, cc, :, lo], st_ref[:, cc, :, hi] = s[:HEADS // 2], s[HEADS // 2:]
            g2 = jnp.broadcast_to(grow_ref[:, cc], (HEADS, GDN_CHUNK, GDN_CHUNK))
            o, s, inv = _gdn_chunk(*[_unpair(r[:, sl, :]) for r in (q_ref, k_ref, v_ref, b_ref, g_ref)], g2,
                                   _unpair(gl_ref[:, sl, :]), s)
            o_ref[:, sl, lo], o_ref[:, sl, hi] = o[:HEADS // 2], o[HEADS // 2:]
            inv_ref[:, cc, :, lo], inv_ref[:, cc, :, hi] = inv[:HEADS // 2], inv[HEADS // 2:]
        state[...] = s

    hm = pl.BlockSpec((HEADS // 2, rows, 128), lambda i: (0, i, 0))
    per_chunk = pl.BlockSpec((HEADS // 2, cpb, GDN_CHUNK, 128), lambda i: (0, i, 0, 0))
    chunk_shape = jax.ShapeDtypeStruct((HEADS // 2, t // GDN_CHUNK, GDN_CHUNK, 128), F32)
    return pl.pallas_call(
        body, name="gdn_fwd", grid=(t // rows,),
        in_specs=[hm] * 6 + [pl.BlockSpec((HEADS, cpb, 1, GDN_CHUNK), lambda i: (0, i, 0, 0))],
        out_specs=[hm, per_chunk, per_chunk],
        out_shape=[jax.ShapeDtypeStruct((HEADS // 2, t, 128), F32), chunk_shape, chunk_shape],
        scratch_shapes=[pltpu.VMEM((HEADS, GDN_CHUNK, HEAD_DIM), F32)],
        compiler_params=_params(("arbitrary",)),
    )(q, k, v, beta, g, gl, grow)


def _gdn_bwd(q, k, v, beta, g, gl, grow, states, invs, do, cpb=4):
    t = q.shape[1]
    rows = cpb * GDN_CHUNK
    nsteps = t // rows
    lo, hi = slice(0, HEAD_DIM), slice(HEAD_DIM, 2 * HEAD_DIM)

    def body(q_ref, k_ref, v_ref, b_ref, g_ref, gl_ref, grow_ref, st_ref, inv_ref, do_ref,
             dq_ref, dk_ref, dv_ref, db_ref, dg_ref, dgl_ref, dgrow_ref, dstate):
        @pl.when(pl.program_id(0) == 0)
        def _():
            dstate[...] = jnp.zeros_like(dstate)

        ds = dstate[...]
        for cc in reversed(range(cpb)):
            sl = slice(cc * GDN_CHUNK, (cc + 1) * GDN_CHUNK)
            g2 = jnp.broadcast_to(grow_ref[:, cc], (HEADS, GDN_CHUNK, GDN_CHUNK))
            _, vjp = jax.vjp(_gdn_chunk, *[_unpair(r[:, sl, :]) for r in (q_ref, k_ref, v_ref, b_ref, g_ref)], g2,
                             _unpair(gl_ref[:, sl, :]), _unpair(st_ref[:, cc]), _unpair(inv_ref[:, cc]))
            gq, gk, gv, gb, gg1, gg2, ggl, ds, _ = vjp((_unpair(do_ref[:, sl, :]), ds))
            for ref, val in ((dq_ref, gq), (dk_ref, gk), (dv_ref, gv), (db_ref, gb), (dg_ref, gg1), (dgl_ref, ggl)):
                ref[:, sl, lo], ref[:, sl, hi] = val[:HEADS // 2], val[HEADS // 2:]
            dgrow_ref[:, cc] = jnp.sum(gg2, axis=1, keepdims=True)
        dstate[...] = ds

    hm = pl.BlockSpec((HEADS // 2, rows, 128), lambda i: (0, nsteps - 1 - i, 0))
    rowspec = pl.BlockSpec((HEADS, cpb, 1, GDN_CHUNK), lambda i: (0, nsteps - 1 - i, 0, 0))
    per_chunk = pl.BlockSpec((HEADS // 2, cpb, GDN_CHUNK, 128), lambda i: (0, nsteps - 1 - i, 0, 0))
    hm_shape = jax.ShapeDtypeStruct((HEADS // 2, t, 128), F32)
    return pl.pallas_call(
        body, name="gdn_bwd", grid=(nsteps,),
        in_specs=[hm] * 6 + [rowspec, per_chunk, per_chunk, hm],
        out_specs=[hm] * 6 + [rowspec],
        out_shape=[hm_shape] * 6 + [jax.ShapeDtypeStruct((HEADS, t // GDN_CHUNK, 1, GDN_CHUNK), F32)],
        scratch_shapes=[pltpu.VMEM((HEADS, GDN_CHUNK, HEAD_DIM), F32)],
        compiler_params=_params(("arbitrary",)),
    )(q, k, v, beta, g, gl, grow, states, invs, do)


def _gdn_out(o_hm, gdn_norm_w, proj, tm=1024):
    t = proj.shape[0]

    def body(o_ref, w_ref, z_ref, oz_ref):
        w = w_ref[...]
        ones = _pair_ones()
        for p in range(HEADS // 2):
            cols = slice(p * 128, (p + 1) * 128)
            o = o_ref[p]
            z = z_ref[:, cols]
            r = lax.rsqrt(_pair_sum_lanes(o * o, ones) * (1.0 / HEAD_DIM) + NORM_EPS)
            oz_ref[:, cols] = (o * r * w * (z * _sigmoid(z))).astype(oz_ref.dtype)

    tok = pl.BlockSpec((tm, WIDTH), lambda i: (i, 0))
    return pl.pallas_call(
        body, name="gdn_out", grid=(t // tm,),
        in_specs=[pl.BlockSpec((HEADS // 2, tm, 128), lambda i: (0, i, 0)), pl.BlockSpec((1, 128), lambda i: (0, 0)),
                  pl.BlockSpec((tm, WIDTH), lambda i: (i, SEG_ZB // WIDTH))],
        out_specs=tok, out_shape=jax.ShapeDtypeStruct((t, WIDTH), MXU_DTYPE),
        compiler_params=_params(("parallel",)),
    )(o_hm, jnp.tile(gdn_norm_w, (1, 2)), proj)


def _gdn_out_bwd(dproj, d_oz, o_hm, gdn_norm_w, proj, tm=1024):
    t = proj.shape[0]

    def body(dp_ref, doz_ref, o_ref, w_ref, z_ref, dz_ref, do_ref, dw_ref):
        w = w_ref[...]
        ones = _pair_ones()
        dw = jnp.zeros((1, 128), F32)
        for p in range(HEADS // 2):
            cols = slice(p * 128, (p + 1) * 128)
            o = o_ref[p]
            z, g = z_ref[:, cols], doz_ref[:, cols]
            sg = _sigmoid(z)
            r = lax.rsqrt(_pair_sum_lanes(o * o, ones) * (1.0 / HEAD_DIM) + NORM_EPS)
            dz_ref[:, cols] = (g * (o * r * w) * (sg * (1.0 + z * (1.0 - sg)))).astype(dz_ref.dtype)
            dn = g * (z * sg)
            dw += jnp.sum(dn * o * r, axis=0, keepdims=True)
            dnw = dn * w
            do_ref[p] = r * dnw - o * (r * r * r) * (_pair_sum_lanes(dnw * o, ones) * (1.0 / HEAD_DIM))

        @pl.when(pl.program_id(0) == 0)
        def _():
            dw_ref[...] = jnp.zeros_like(dw_ref)

        dw_ref[...] += jnp.where(_iota((8, 128), 0) == 0, dw, 0.0)

    tok = pl.BlockSpec((tm, WIDTH), lambda i: (i, 0))
    seg = pl.BlockSpec((tm, WIDTH), lambda i: (i, SEG_ZB // WIDTH))
    hm = pl.BlockSpec((HEADS // 2, tm, 128), lambda i: (0, i, 0))
    dz, do, dw = pl.pallas_call(
        body, name="gdn_out_bwd", grid=(t // tm,),
        in_specs=[pl.BlockSpec(memory_space=pl.ANY), tok, hm, pl.BlockSpec((1, 128), lambda i: (0, 0)), seg],
        out_specs=[seg, hm, pl.BlockSpec((8, 128), lambda i: (0, 0))],
        out_shape=[jax.ShapeDtypeStruct((t, PACKED_WIDTH), MXU_DTYPE), jax.ShapeDtypeStruct((HEADS // 2, t, 128), F32),
                   jax.ShapeDtypeStruct((8, 128), F32)],
        input_output_aliases={0: 0},
        compiler_params=_params(("arbitrary",)),
    )(dproj, d_oz, o_hm, jnp.tile(gdn_norm_w, (1, 2)), proj)
    return dz, do, dw[:, :HEAD_DIM] + dw[:, HEAD_DIM:]


def _up_merge(oz_a, oz_b, w_up_a, w_up_b, proj, tm=512):
    t = proj.shape[0]

    def body(oa_ref, ob_ref, wa_ref, wb_ref, ga0, ga1, gb0, gb1, ya_ref, yb_ref, m_ref):
        ya = jnp.dot(oa_ref[...], wa_ref[...], preferred_element_type=F32)
        yb = jnp.dot(ob_ref[...], wb_ref[...], preferred_element_type=F32)
        ya_ref[...] = ya
        yb_ref[...] = yb
        for c, (ga, gb) in enumerate(((ga0, gb0), (ga1, gb1))):
            cols = slice(c * WIDTH, (c + 1) * WIDTH)
            m_ref[:, cols] = (_sigmoid(ga[...]) * ya[:, cols] + _sigmoid(gb[...]) * yb[:, cols]).astype(m_ref.dtype)

    def gate(seg, c):
        return pl.BlockSpec((tm, WIDTH), lambda i: (i, seg // WIDTH + c))

    tok_in = pl.BlockSpec((tm, WIDTH), lambda i: (i, 0))
    weight = pl.BlockSpec((WIDTH, D_MODEL), lambda i: (0, 0))
    tok = pl.BlockSpec((tm, D_MODEL), lambda i: (i, 0))
    return pl.pallas_call(
        body, name="up_merge", grid=(t // tm,),
        in_specs=[tok_in, tok_in, weight, weight, gate(SEG_GA, 0), gate(SEG_GA, 1), gate(SEG_GB, 0), gate(SEG_GB, 1)],
        out_specs=[tok, tok, tok],
        out_shape=[jax.ShapeDtypeStruct((t, D_MODEL), F32), jax.ShapeDtypeStruct((t, D_MODEL), F32),
                   jax.ShapeDtypeStruct((t, D_MODEL), MXU_DTYPE)],
        compiler_params=_params(("parallel",)),
    )(oz_a, oz_b, w_up_a, w_up_b, proj, proj, proj, proj)


def _merge_bwd(dproj, d_m, y, proj, seg, name, tm=1024):
    t = proj.shape[0]

    def body(*refs):
        dm_ref, y_ref, g_ref, dg_ref, dy_ref = refs[-5:]
        dm = dm_ref[...]
        s = _sigmoid(g_ref[...])
        dy_ref[...] = (dm * s).astype(dy_ref.dtype)
        dg_ref[...] = (dm * y_ref[...] * s * (1.0 - s)).astype(dg_ref.dtype)

    half = pl.BlockSpec((tm, WIDTH), lambda i, c: (i, c))
    gate = pl.BlockSpec((tm, WIDTH), lambda i, c: (i, seg // WIDTH + c))
    specs, args, aliases = [half, half, gate], [d_m, y, proj], {}
    if dproj is not None:
        specs, args, aliases = [pl.BlockSpec(memory_space=pl.ANY)] + specs, [dproj] + args, {0: 0}
    return pl.pallas_call(
        body, name=name, grid=(t // tm, 2), in_specs=specs, out_specs=[gate, half],
        out_shape=[jax.ShapeDtypeStruct((t, PACKED_WIDTH), MXU_DTYPE), jax.ShapeDtypeStruct((t, D_MODEL), MXU_DTYPE)],
        input_output_aliases=aliases,
        compiler_params=_params(("parallel", "parallel")),
    )(*args)


def _out_tail(merged, w_out, x, final_w, target, tm=1024):
    t = x.shape[0]
    tm = min(tm, t)

    def body(m_ref, wo_ref, x_ref, w_ref, t_ref, dxm_ref, dx_ref, loss_ref, dw_ref):
        x2 = x_ref[...] + jnp.dot(m_ref[...], wo_ref[...], preferred_element_type=F32)
        w = w_ref[...]
        r = lax.rsqrt(jnp.mean(x2 * x2, axis=-1, keepdims=True) + NORM_EPS)
        xn = x2 * r
        err = xn * w - t_ref[...]
        dy = err * (1.0 / D_MODEL)
        dyw = dy * w
        dx2 = r * dyw - x2 * (r * r * r) * jnp.mean(dyw * x2, axis=-1, keepdims=True)
        dx_ref[...] = dx2
        dxm_ref[...] = dx2.astype(dxm_ref.dtype)
        loss = 0.5 * jnp.sum(jnp.sum(err * err, axis=-1, keepdims=True) * (1.0 / D_MODEL), axis=0, keepdims=True)
        onehot = jnp.where((_iota((8, 128), 0) == 0) & (_iota((8, 128), 1) == 0), 1.0, 0.0)

        @pl.when(pl.program_id(0) == 0)
        def _():
            loss_ref[...] = jnp.zeros_like(loss_ref)
            dw_ref[...] = jnp.zeros_like(dw_ref)

        loss_ref[...] += loss * onehot
        dw_ref[...] += jnp.where(_iota((8, D_MODEL), 0) == 0, jnp.sum(dy * xn, axis=0, keepdims=True), 0.0)

    tok = pl.BlockSpec((tm, D_MODEL), lambda i: (i, 0))
    full = pl.BlockSpec((D_MODEL, D_MODEL), lambda i: (0, 0))
    return pl.pallas_call(
        body, name="out_tail", grid=(t // tm,),
        in_specs=[tok, full, tok, pl.BlockSpec((1, D_MODEL), lambda i: (0, 0)), tok],
        out_specs=[tok, tok, pl.BlockSpec((8, 128), lambda i: (0, 0)), pl.BlockSpec((8, D_MODEL), lambda i: (0, 0))],
        out_shape=[jax.ShapeDtypeStruct((t, D_MODEL), MXU_DTYPE), jax.ShapeDtypeStruct((t, D_MODEL), F32),
                   jax.ShapeDtypeStruct((8, 128), F32), jax.ShapeDtypeStruct((8, D_MODEL), F32)],
        compiler_params=_params(("arbitrary",)),
    )(merged, w_out, x, final_w, target)


def _dh_norm_bwd(dproj, wp, x, norm_w, dx2, after, blocks, name, carry=None, tm=1024, tk=PACKED_WIDTH // 4):
    t = x.shape[0]
    nk = PACKED_WIDTH // tk
    first, end = blocks

    def body(*refs):
        dp_ref, wp_ref, x_ref, w_ref, dx2_ref = refs[:5]
        dx_ref, dw_ref, acc = refs[-3:]
        kk = pl.program_id(1)
        part = _mxu(dp_ref[...], wp_ref[...], _NT)

        @pl.when(kk == 0)
        def _():
            acc[...] = part

        @pl.when(kk > 0)
        def _():
            acc[...] += part

        @pl.when((kk == 0) & (pl.program_id(0) == 0))
        def _():
            dw_ref[...] = jnp.zeros_like(dw_ref) if carry is None else refs[-4][...]

        @pl.when(kk == nk - 1)
        def _():
            xf, w, dh_ = x_ref[...], w_ref[...], acc[...]
            r = lax.rsqrt(jnp.mean(xf * xf, axis=-1, keepdims=True) + NORM_EPS)
            dhw = dh_ * w
            dx_ref[...] = dx2_ref[...] + r * dhw - xf * (r * r * r) * jnp.mean(dhw * xf, axis=-1, keepdims=True)
            dw_ref[...] += jnp.where(_iota((8, D_MODEL), 0) == 0, jnp.sum(dh_ * xf * r, axis=0, keepdims=True), 0.0)

    tok = pl.BlockSpec((tm, D_MODEL), lambda i, kk: (first + i, 0))
    small = pl.BlockSpec((8, D_MODEL), lambda i, kk: (0, 0))
    specs = [pl.BlockSpec((tm, tk), lambda i, kk: (first + i, kk)), pl.BlockSpec((D_MODEL, tk), lambda i, kk: (0, kk)),
             tok, pl.BlockSpec((1, D_MODEL), lambda i, kk: (0, 0)), tok, pl.BlockSpec(memory_space=pl.ANY)]
    args = [dproj, wp, x, norm_w, dx2, after]
    aliases = {}
    if carry is not None:
        specs += [pl.BlockSpec(memory_space=pl.ANY), small]
        args += list(carry)
        aliases = {len(args) - 2: 0}
    return pl.pallas_call(
        body, name=name, grid=(end - first, nk), in_specs=specs, out_specs=[tok, small],
        out_shape=[jax.ShapeDtypeStruct((t, D_MODEL), F32), jax.ShapeDtypeStruct((8, D_MODEL), F32)],
        scratch_shapes=[pltpu.VMEM((tm, D_MODEL), F32)],
        input_output_aliases=aliases,
        compiler_params=_params(("arbitrary", "arbitrary")),
    )(*args)


def _local_step(x, target, norm_w, wp, conv_w, a_log, dt_bias, gdn_norm_w, w_up_a, w_up_b, w_out, final_w,
                start_reduce, continue_reduce):
    t = x.shape[0]
    tables = _rope_tables(t)
    a_log = jnp.pad(a_log, ((0, 0), (HEADS, 128 - 2 * HEADS)))
    dt_bias = jnp.pad(dt_bias, ((0, 0), (HEADS, 128 - 2 * HEADS)))

    proj, h_t = _norm_proj(x, norm_w, wp)
    qkvs = _rope_fwd(proj, tables)
    outs, lses = zip(*[_att_fwd(qkvs[gi], d, f"att_fwd{gi}") for gi, d in enumerate(DILATIONS)])
    oz_a, o_a, *lse_views = _att_merge(outs, lses, proj)
    conv = _conv_fwd(proj, conv_w)
    gq, gk, gv, gb, gg, ggl, grow = _gdn_prep(conv, proj, a_log, dt_bias)
    o_b, states, invs = _gdn_fwd(gq, gk, gv, gb, gg, ggl, grow)
    oz_b = _gdn_out(o_b, gdn_norm_w, proj)
    big = dict(tm=1024, tn=1024, tk=1024)
    y_a, y_b, merged = _up_merge(oz_a, oz_b, w_up_a, w_up_b, proj)
    dx2_m, dx2, loss_blk, d_final = _out_tail(merged, w_out, x, final_w, target)

    d_wout = _matmul(merged, dx2_m, "tn", "d_w_out", **big)
    d_m = _matmul(dx2_m, w_out, "nt", "d_merged", **big)
    dproj, dy_a = _merge_bwd(None, d_m, y_a, proj, SEG_GA, "merge_bwd_a")
    dproj, dy_b = _merge_bwd(dproj, d_m, y_b, proj, SEG_GB, "merge_bwd_b")
    d_wua = _matmul(oz_a, dy_a, "tn", "d_w_up_a", **big)
    d_wub = _matmul(oz_b, dy_b, "tn", "d_w_up_b", **big)
    d_oz_a = _matmul(dy_a, w_up_a, "nt", "d_oz_a", **big)
    d_oz_b = _matmul(dy_b, w_up_b, "nt", "d_oz_b", **big)
    dproj, *views = _att_merge_bwd(dproj, d_oz_a, o_a, proj)
    do_views, delta_views = views[:3], views[3:]
    dqs, dkvs = zip(*[_att_bwd(qkvs[gi], do_views[gi], lse_views[gi], delta_views[gi], d, f"att_bwd{gi}")
                      for gi, d in enumerate(DILATIONS)])
    dproj = _rope_bwd(dproj, dqs, dkvs, tables)
    dproj, do_b, d_gnw = _gdn_out_bwd(dproj, d_oz_b, o_b, gdn_norm_w, proj)
    dgq, dgk, dgv, dgb, dgg, dggl, dgrow = _gdn_bwd(gq, gk, gv, gb, gg, ggl, grow, states, invs, do_b)
    dproj, dconv, d_small = _gdn_prep_bwd(dproj, conv, proj, a_log, dt_bias, dgq, dgk, dgv, dgb, dgg, dggl, dgrow)
    dproj, d_convw = _conv_bwd(dproj, dconv, proj, conv_w)
    d_wp = _matmul(h_t, dproj, "nn", "d_w_in", tm=1024, tn=PACKED_WIDTH // 4, tk=1024)
    in_flight, token = start_reduce(d_wp, d_wua, d_wub, d_wout, d_convw[0:GDN_CONV])
    nblk = t // 1024
    cut = max(1, nblk // 4)
    part = _dh_norm_bwd(dproj, wp, x, norm_w, dx2, token, (0, cut), "dh_norm_bwd_a")
    in_flight, token = continue_reduce(in_flight, part[0])
    grad_x, d_norm = _dh_norm_bwd(dproj, wp, x, norm_w, dx2, token, (cut, nblk), "dh_norm_bwd_b", carry=part)
    return dict(loss=loss_blk, grad_x=grad_x, norm_w=d_norm[0:1], in_flight=in_flight,
                a_log=d_small[0:1, HEADS:2 * HEADS], dt_bias=d_small[1:2, HEADS:2 * HEADS], gdn_norm_w=d_gnw[0:1],
                final_norm_w=d_final[0:1])


SHARDS = 4
W_IN_SHARD = IN_WIDTH // SHARDS
ROWS_UP = WIDTH * (D_MODEL // SHARDS) // 128
ROWS_OUT = (D_MODEL // SHARDS) * D_MODEL // 128
CONV_SHARD = 3 * WIDTH // SHARDS
ROWS_CONV = 16
SLAB_ROWS = 2 * ROWS_UP + ROWS_OUT + 2 * ROWS_CONV
HALF_ROWS = SLAB_ROWS // 2
BIG_HALF = (D_MODEL // 2, W_IN_SHARD)
SMALL_HALF = (HALF_ROWS, 128)
MESH = pl.DeviceIdType.MESH
ANY = pl.BlockSpec(memory_space=pl.ANY)


def _pad_rows(a, rows):
    return jnp.pad(a, ((0, rows - a.shape[0]), (0, 0)))


def _pack_slab(w_up_a, w_up_b, w_out, conv, conv_lo):
    parts = [w_up_a.reshape(ROWS_UP, 128), w_up_b.reshape(ROWS_UP, 128), w_out.reshape(ROWS_OUT, 128),
             _pad_rows(conv.reshape(-1, 128), ROWS_CONV), _pad_rows(conv_lo.reshape(-1, 128), ROWS_CONV)]
    return jnp.concatenate(parts, axis=0).reshape(2, *SMALL_HALF)


def _unpack_slab(slab):
    slab = slab.reshape(SLAB_ROWS, 128)
    r0 = 0
    out = []
    for rows, shape in ((ROWS_UP, (WIDTH, D_MODEL // SHARDS)), (ROWS_UP, (WIDTH, D_MODEL // SHARDS)),
                        (ROWS_OUT, (D_MODEL // SHARDS, D_MODEL)), (ROWS_CONV, None), (ROWS_CONV, None)):
        part = slab[r0:r0 + rows]
        out.append(part[:GDN_CONV * CONV_SHARD // 128].reshape(GDN_CONV, CONV_SHARD) if shape is None else part.reshape(shape))
        r0 += rows
    return out


def _mesh_position():
    x, y, c = lax.axis_index("x"), lax.axis_index("y"), lax.axis_index("c")
    return x, y, c, [(1 - x, y), (x, 1 - y), (1 - x, 1 - y)]


def _gather_weights(shards):
    n = len(shards)

    def body(*refs):
        in_refs, out_refs, (send_sems, recv_sems) = refs[:n], refs[n:2 * n], refs[2 * n:]
        x, y, c, chips = _mesh_position()

        def half(a, chip, which):
            return out_refs[a].at[2 * chip[0] + chip[1], which]

        def copy(k, src, dst, to):
            return pltpu.make_async_remote_copy(src_ref=src, dst_ref=dst, send_sem=send_sems.at[k], recv_sem=recv_sems.at[k],
                                                device_id=to, device_id_type=MESH)

        pairs = [(a, j, chip) for a in range(n) for j, chip in enumerate(chips)]
        first = [copy(6 * a + j, in_refs[a].at[c], half(a, (x, y), c), (*chip, c)) for a, j, chip in pairs]
        for cp in first:
            cp.start()
        passed = [copy(6 * a + 3 + j, half(a, chip, c), half(a, chip, c), (x, y, 1 - c)) for a, j, chip in pairs]
        for i, (a, j, chip) in enumerate(pairs):
            copy(6 * a + j, half(a, chip, c), half(a, chip, c), (x, y, c)).wait_recv()
            passed[i].start()
        for a, j, chip in pairs:
            copy(6 * a + 3 + j, half(a, chip, 1 - c), half(a, chip, 1 - c), (x, y, c)).wait_recv()
        for cp in first + passed:
            cp.wait_send()

    return pl.pallas_call(
        body, name="gather_weights", in_specs=[ANY] * n, out_specs=[ANY] * n,
        out_shape=[jax.ShapeDtypeStruct((SHARDS, *s.shape), s.dtype) for s in shards],
        scratch_shapes=[pltpu.SemaphoreType.DMA((6 * n,)), pltpu.SemaphoreType.DMA((6 * n,))],
    )(*shards)


def _pair_sum(grads, recv, blk, name):
    _, _, rows, cols = grads.shape

    def body(c_ref, g_ref, r_ref, o_ref):
        o_ref[...] = (g_ref[0] + r_ref[...]).astype(o_ref.dtype)

    spec = pl.BlockSpec((1, blk, cols), lambda s, i, c_ref: (s, i, 0))
    return pl.pallas_call(
        body, name=name,
        grid_spec=pltpu.PrefetchScalarGridSpec(
            num_scalar_prefetch=1, grid=(SHARDS, rows // blk),
            in_specs=[pl.BlockSpec((1, 1, blk, cols), lambda s, i, c_ref: (s, c_ref[0], i, 0)), spec],
            out_specs=spec),
        out_shape=jax.ShapeDtypeStruct((SHARDS, rows, cols), MXU_DTYPE),
        compiler_params=_params(("parallel", "parallel")),
    )(lax.axis_index("c").astype(jnp.int32).reshape(1), grads, recv)


_HBM = pl.BlockSpec(memory_space=pltpu.HBM)
_SEM = pl.BlockSpec(memory_space=pltpu.SEMAPHORE)
_DATAFLOW = pltpu.SideEffectType.DATAFLOW_SIDE_EFFECTING


def _scatter_copies(p_refs, l_refs, send_sems, recv_sems):
    x, y, c, chips = _mesh_position()
    return [pltpu.make_async_remote_copy(src_ref=p_refs[a].at[2 * chip[0] + chip[1]], dst_ref=l_refs[a].at[j],
                                         send_sem=send_sems.at[3 * a + j], recv_sem=recv_sems.at[3 * a + j],
                                         device_id=(*chip, c), device_id_type=MESH)
            for a in range(len(p_refs)) for j, chip in enumerate(chips)]


def _exchange_copies(g_refs, l_refs, send_sems, recv_sems):
    x, y, c, _ = _mesh_position()
    return [pltpu.make_async_remote_copy(src_ref=g_refs[a].at[s, 1 - c], dst_ref=l_refs[a].at[s],
                                         send_sem=send_sems.at[SHARDS * a + s], recv_sem=recv_sems.at[SHARDS * a + s],
                                         device_id=(x, y, 1 - c), device_id_type=MESH)
            for a in range(len(g_refs)) for s in range(SHARDS)]


def _copies_start(name, copies, count, sources, land_shapes):
    n = len(sources)
    lands = [lax.empty(shape, src.dtype) for shape, src in zip(land_shapes, sources)]

    def body(*refs):
        s_refs, l_refs, send_sems, recv_sems, token = refs[:n], refs[n:2 * n], refs[2 * n], refs[2 * n + 1], refs[-1]
        for cp in copies(s_refs, l_refs, send_sems, recv_sems):
            cp.start()
        token[...] = jnp.zeros_like(token)

    operands = [pltpu.with_memory_space_constraint(a, pltpu.HBM) for a in (*sources, *lands)]
    return pl.pallas_call(
        body, name=name, in_specs=[_HBM] * (2 * n),
        out_shape=(pltpu.SemaphoreType.DMA((count,)), pltpu.SemaphoreType.DMA((count,)),
                   *[pltpu.HBM(a.shape, a.dtype) for a in operands], jax.ShapeDtypeStruct((8, 128), F32)),
        out_specs=(_SEM, _SEM, *[_HBM] * (2 * n), pl.BlockSpec(memory_space=pltpu.VMEM)),
        input_output_aliases={i: 2 + i for i in range(2 * n)},
        compiler_params=pltpu.CompilerParams(has_side_effects=_DATAFLOW),
    )(*operands)


def _copies_wait(name, copies, send_sems, recv_sems, passed, after):
    n = len(passed) // 2

    def body(*refs):
        s_refs, l_refs, send_s, recv_s = refs[:n], refs[n:2 * n], refs[2 * n], refs[2 * n + 1]
        for cp in copies(s_refs, l_refs, send_s, recv_s):
            cp.wait_send()
            cp.wait_recv()

    return pl.pallas_call(
        body, name=name, in_specs=[_HBM] * (2 * n) + [_SEM, _SEM, ANY],
        out_shape=[pltpu.HBM(a.shape, a.dtype) for a in passed], out_specs=[_HBM] * (2 * n),
        input_output_aliases={i: i for i in range(2 * n)},
        compiler_params=pltpu.CompilerParams(has_side_effects=_DATAFLOW),
    )(*passed, send_sems, recv_sems, after)


def _chip_sum(pairs, recv, blk, name):
    _, rows, cols = pairs.shape

    def body(pos_ref, p_ref, r_ref, o_ref):
        o_ref[0] = ((p_ref[0].astype(F32) + r_ref[0].astype(F32)) + r_ref[1].astype(F32)) + r_ref[2].astype(F32)

    pos = jnp.stack([2 * lax.axis_index("x") + lax.axis_index("y"), lax.axis_index("c")]).astype(jnp.int32)
    return pl.pallas_call(
        body, name=name,
        grid_spec=pltpu.PrefetchScalarGridSpec(
            num_scalar_prefetch=1, grid=(rows // blk,),
            in_specs=[pl.BlockSpec((1, blk, cols), lambda i, pos_ref: (pos_ref[0], i, 0)),
                      pl.BlockSpec((3, blk, cols), lambda i, pos_ref: (0, i, 0))],
            out_specs=pl.BlockSpec((1, blk, cols), lambda i, pos_ref: (pos_ref[1], i, 0))),
        out_shape=jax.ShapeDtypeStruct((2, rows, cols), F32),
        compiler_params=_params(("parallel",)),
    )(pos, pairs, recv)


def _share_total(totals):
    n = len(totals)

    def body(*refs):
        t_refs, out_refs, (send_sems, recv_sems) = refs[:n], refs[n:2 * n], refs[2 * n:]
        x, y, c, _ = _mesh_position()
        copies = [pltpu.make_async_remote_copy(src_ref=t_refs[a].at[c], dst_ref=out_refs[a].at[c], send_sem=send_sems.at[a],
                                               recv_sem=recv_sems.at[a], device_id=(x, y, 1 - c), device_id_type=MESH)
                  for a in range(n)]
        for cp in copies:
            cp.start()
        for a in range(n):
            other = out_refs[a].at[1 - c]
            pltpu.make_async_remote_copy(src_ref=other, dst_ref=other, send_sem=send_sems.at[a], recv_sem=recv_sems.at[a],
                                         device_id=(x, y, c), device_id_type=MESH).wait_recv()
        for cp in copies:
            cp.wait_send()

    return pl.pallas_call(
        body, name="share_total", in_specs=[ANY] * n, out_specs=[ANY] * n,
        out_shape=[jax.ShapeDtypeStruct(t.shape, F32) for t in totals],
        scratch_shapes=[pltpu.SemaphoreType.DMA((n,)), pltpu.SemaphoreType.DMA((n,))],
        input_output_aliases={a: a for a in range(n)},
    )(*totals)


def _allreduce_small(block):
    def body(b_ref, out_ref, gath, send_sems, recv_sems):
        x, y, c, _ = _mesh_position()
        me = 4 * x + 2 * y + c
        gath[me] = b_ref[...]
        copies = []
        for k in range(1, 8):
            peer = (x ^ (k >> 2), y ^ ((k >> 1) & 1), c ^ (k & 1))
            copies.append(pltpu.make_async_remote_copy(src_ref=b_ref, dst_ref=gath.at[me], send_sem=send_sems.at[k - 1],
                                                       recv_sem=recv_sems.at[k - 1], device_id=peer, device_id_type=MESH))
        for cp in copies:
            cp.start()
        for k in range(1, 8):
            src = 4 * (x ^ (k >> 2)) + 2 * (y ^ ((k >> 1) & 1)) + (c ^ (k & 1))
            pltpu.make_async_remote_copy(src_ref=b_ref, dst_ref=gath.at[src], send_sem=send_sems.at[k - 1],
                                         recv_sem=recv_sems.at[k - 1], device_id=(x, y, c), device_id_type=MESH).wait_recv()
        for cp in copies:
            cp.wait_send()
        acc = gath[0]
        for d in range(1, 8):
            acc = acc + gath[d]
        out_ref[...] = acc

    vm = pl.BlockSpec(memory_space=pltpu.VMEM)
    return pl.pallas_call(
        body, name="allreduce_small", in_specs=[vm], out_specs=vm,
        out_shape=jax.ShapeDtypeStruct((8, D_MODEL), F32),
        scratch_shapes=[pltpu.VMEM((8, 8, D_MODEL), F32), pltpu.SemaphoreType.DMA((7,)), pltpu.SemaphoreType.DMA((7,))],
    )(block)


def _adamw(w, g, m, v, name):
    rows, cols = w.shape
    tr = 128 if rows % 128 == 0 else rows

    def body(w_ref, g_ref, m_ref, v_ref, d_ref, nm_ref, nv_ref):
        gv = g_ref[...]
        nm = ADAM_B1 * m_ref[...] + (1.0 - ADAM_B1) * gv
        nv = ADAM_B2 * v_ref[...] + (1.0 - ADAM_B2) * (gv * gv)
        m_hat = nm / (1.0 - ADAM_B1 ** ADAM_STEP)
        v_hat = nv / (1.0 - ADAM_B2 ** ADAM_STEP)
        d_ref[...] = -ADAM_LR * (m_hat / (jnp.sqrt(v_hat) + ADAM_EPS) + ADAM_WD * w_ref[...])
        nm_ref[...] = nm
        nv_ref[...] = nv

    spec = pl.BlockSpec((tr, cols), lambda i: (i, 0))
    shape = jax.ShapeDtypeStruct((rows, cols), F32)
    return pl.pallas_call(
        body, name=name, grid=(rows // tr,), in_specs=[spec] * 4, out_specs=[spec] * 3, out_shape=[shape] * 3,
        compiler_params=_params(("parallel",)),
    )(w, g, m, v)


def kernel(x, norm_w, w_in, conv_w, a_log, dt_bias, gdn_norm_w, w_up_a, w_up_b, w_out, final_norm_w, loss_target, m_norm_w, m_w_in, m_conv_w, m_a_log, m_dt_bias, m_gdn_norm_w, m_w_up_a, m_w_up_b, m_w_out, m_final_norm_w, v_norm_w, v_w_in, v_conv_w, v_a_log, v_dt_bias, v_gdn_norm_w, v_w_up_a, v_w_up_b, v_w_out, v_final_norm_w):
    conv_hi = conv_w[0].astype(MXU_DTYPE)
    conv_lo = (conv_w[0] - conv_hi.astype(F32)).astype(MXU_DTYPE)
    big = w_in[0].astype(MXU_DTYPE).reshape(2, *BIG_HALF)
    slab = _pack_slab(w_up_a[0].astype(MXU_DTYPE), w_up_b[0].astype(MXU_DTYPE), w_out[0].astype(MXU_DTYPE), conv_hi, conv_lo)
    own_shard = 2 * lax.axis_index("x") + lax.axis_index("y")
    bigs, slabs = _gather_weights([big, slab])
    bigs = lax.dynamic_update_slice(bigs, big[None], (own_shard, 0, 0, 0)).reshape(SHARDS, D_MODEL, W_IN_SHARD)
    slabs = lax.dynamic_update_slice(slabs, slab[None], (own_shard, 0, 0, 0))
    parts = [_unpack_slab(slabs[s]) for s in range(SHARDS)]
    split = BA_END - (SHARDS - 1) * W_IN_SHARD
    wp = jnp.concatenate([bigs[s] for s in range(SHARDS - 1)]
                         + [bigs[-1][:, :split], jnp.zeros((D_MODEL, SEG_GA - BA_END), MXU_DTYPE), bigs[-1][:, split:]], axis=1)
    w_up_a_full = jnp.concatenate([p[0] for p in parts], axis=1)
    w_up_b_full = jnp.concatenate([p[1] for p in parts], axis=1)
    w_out_full = jnp.concatenate([p[2] for p in parts], axis=0)
    conv_full = jnp.concatenate([p[3].astype(F32) + p[4].astype(F32) for p in parts], axis=1)

    blocks, tags = (128, HALF_ROWS), ("w_in", "slab")

    def start_reduce(d_wp, d_w_up_a, d_w_up_b, d_w_out, d_conv_w):
        d_w_in = [d_wp[:, s * W_IN_SHARD:(s + 1) * W_IN_SHARD] for s in range(SHARDS - 1)]
        d_w_in.append(jnp.concatenate([d_wp[:, (SHARDS - 1) * W_IN_SHARD:BA_END], d_wp[:, SEG_GA:]], axis=1))
        zero_conv = jnp.zeros((GDN_CONV, CONV_SHARD), F32)
        grads = [jnp.stack(d_w_in).reshape(SHARDS, 2, *BIG_HALF),
                 jnp.stack([_pack_slab(d_w_up_a[:, s * 256:(s + 1) * 256], d_w_up_b[:, s * 256:(s + 1) * 256],
                                       d_w_out[s * 256:(s + 1) * 256], d_conv_w[:, s * CONV_SHARD:(s + 1) * CONV_SHARD],
                                       zero_conv) for s in range(SHARDS)])]
        *in_flight, token = _copies_start("exchange_start", _exchange_copies, 2 * SHARDS, grads,
                                          [(SHARDS, *gr.shape[2:]) for gr in grads])
        return in_flight, token

    def continue_reduce(in_flight, after):
        send_sems, recv_sems, *passed = in_flight
        arrived = _copies_wait("exchange_wait", _exchange_copies, send_sems, recv_sems, passed, after)
        grads, from_sibling = arrived[:2], arrived[2:]
        pairs = [_pair_sum(gr, fs, blk, f"pair_sum_{tag}") for gr, fs, blk, tag in zip(grads, from_sibling, blocks, tags)]
        *in_flight, token = _copies_start("scatter_start", _scatter_copies, 2 * 3, pairs, [(3, *p.shape[1:]) for p in pairs])
        return in_flight, token

    g = _local_step(x[0], loss_target[0], norm_w, wp, conv_full, a_log, dt_bias, gdn_norm_w,
                    w_up_a_full, w_up_b_full, w_out_full, final_norm_w[None], start_reduce, continue_reduce)

    send_sems, recv_sems, *passed = g["in_flight"]
    arrived = _copies_wait("scatter_wait", _scatter_copies, send_sems, recv_sems, passed, g["grad_x"])
    pairs, from_chips = arrived[:2], arrived[2:]
    total_big, total_slab = _share_total([_chip_sum(p, fc, blk, f"chip_sum_{tag}")
                                          for p, fc, blk, tag in zip(pairs, from_chips, blocks, tags)])
    g_w_in = total_big.reshape(D_MODEL, W_IN_SHARD)
    g_w_up_a, g_w_up_b, g_w_out, g_conv, _ = _unpack_slab(total_slab)

    row2 = jnp.concatenate([g["gdn_norm_w"], g["a_log"], g["dt_bias"], g["loss"][0:1, 0:1],
                            jnp.zeros((1, D_MODEL - HEAD_DIM - 2 * HEADS - 1), F32)], axis=1)
    small = _allreduce_small(jnp.concatenate([g["norm_w"], g["final_norm_w"], row2, jnp.zeros((5, D_MODEL), F32)], axis=0))
    g_norm, g_final = small[0:1], small[1]
    g_gnw, g_alog, g_dt = small[2:3, 0:HEAD_DIM], small[2:3, HEAD_DIM:HEAD_DIM + HEADS], small[2:3, HEAD_DIM + HEADS:HEAD_DIM + 2 * HEADS]
    loss = small[2, HEAD_DIM + 2 * HEADS]

    names = ["norm_w", "w_in", "conv_w", "a_log", "dt_bias", "gdn_norm_w", "w_up_a", "w_up_b", "w_out", "final_norm_w"]
    weights = dict(zip(names, (norm_w, w_in, conv_w, a_log, dt_bias, gdn_norm_w, w_up_a, w_up_b, w_out, final_norm_w)))
    ms = dict(zip(names, (m_norm_w, m_w_in, m_conv_w, m_a_log, m_dt_bias, m_gdn_norm_w, m_w_up_a, m_w_up_b, m_w_out, m_final_norm_w)))
    vs = dict(zip(names, (v_norm_w, v_w_in, v_conv_w, v_a_log, v_dt_bias, v_gdn_norm_w, v_w_up_a, v_w_up_b, v_w_out, v_final_norm_w)))
    grads2d = dict(norm_w=g_norm, w_in=g_w_in, conv_w=g_conv, a_log=g_alog, dt_bias=g_dt, gdn_norm_w=g_gnw,
                   w_up_a=g_w_up_a, w_up_b=g_w_up_b, w_out=g_w_out, final_norm_w=g_final[None])
    grad_out, delta, new_m, new_v = [], [], [], []
    for n in names:
        shape = weights[n].shape
        two_d = grads2d[n].shape
        d, nm, nv = _adamw(weights[n].reshape(two_d), grads2d[n], ms[n].reshape(two_d), vs[n].reshape(two_d), f"adamw_{n}")
        grad_out.append(grads2d[n].reshape(shape))
        delta.append(d.reshape(shape))
        new_m.append(nm.reshape(shape))
        new_v.append(nv.reshape(shape))
    return (loss, g["grad_x"][None], *grad_out, *delta, *new_m, *new_v)
```

```python
import functools

import jax
import jax.numpy as jnp
from jax import lax
from jax.experimental import pallas as pl
from jax.experimental.pallas import tpu as pltpu

F32 = jnp.float32
MXU_DTYPE = jnp.bfloat16
HIGHEST = lax.Precision.HIGHEST

D_MODEL = 1024
HEADS = 8
HEAD_DIM = 64
WIDTH = HEADS * HEAD_DIM
NORM_EPS = 1e-6
ROPE_THETA = 10000.0
ATT_BLOCK = 128
DILATIONS = (1, 4, 16)
GDN_CHUNK = 64
GDN_CONV = 4
IN_WIDTH = 9232
SEG_A, SEG_ZA, SEG_B, SEG_ZB, SEG_BA, SEG_GA, SEG_GB, PACKED_WIDTH = 0, 4608, 5120, 6656, 7168, 7680, 8704, 9728
BA_END = 7184
VMEM_LIMIT = 56 * 1024 * 1024

ADAM_LR, ADAM_B1, ADAM_B2, ADAM_EPS, ADAM_WD, ADAM_STEP = 0.001, 0.9, 0.999, 1e-08, 0.01, 10

_NN = (((1,), (0,)), ((), ()))
_NT = (((1,), (1,)), ((), ()))
_TN = (((0,), (0,)), ((), ()))


def _params(sem):
    return pltpu.CompilerParams(dimension_semantics=sem, vmem_limit_bytes=VMEM_LIMIT)


def _mxu(a, b, dims):
    return lax.dot_general(a.astype(MXU_DTYPE), b.astype(MXU_DTYPE), dims, preferred_element_type=F32)


def _sigmoid(x):
    return 1.0 / (1.0 + jnp.exp(-x))


def _softplus(x):
    return jnp.maximum(x, 0.0) + jnp.log(1.0 + jnp.exp(-jnp.abs(x)))


def _iota(shape, axis):
    return lax.broadcasted_iota(jnp.int32, shape, axis)


def _matmul(a, b, mode, name, out_dtype=F32, tm=512, tn=512, tk=512):
    if mode == "nn":
        (m, k), (k2, n) = a.shape, b.shape
    elif mode == "nt":
        (m, k), (n, k2) = a.shape, b.shape
    else:
        (k, m), (k2, n) = a.shape, b.shape
    assert k == k2
    tm, tn, tk = min(tm, m), min(tn, n), min(tk, k)
    assert m % tm == 0 and n % tn == 0 and k % tk == 0
    nk = k // tk
    dims = {"nn": _NN, "nt": _NT, "tn": _TN}[mode]

    assert out_dtype == F32

    def body(a_ref, b_ref, o_ref):
        kk = pl.program_id(2)
        part = _mxu(a_ref[...], b_ref[...], dims)

        @pl.when(kk == 0)
        def _():
            o_ref[...] = part

        @pl.when(kk > 0)
        def _():
            o_ref[...] += part

    a_spec = pl.BlockSpec((tk, tm), lambda i, j, kk: (kk, i)) if mode == "tn" else pl.BlockSpec((tm, tk), lambda i, j, kk: (i, kk))
    b_spec = pl.BlockSpec((tn, tk), lambda i, j, kk: (j, kk)) if mode == "nt" else pl.BlockSpec((tk, tn), lambda i, j, kk: (kk, j))
    return pl.pallas_call(
        body, name=name, grid=(m // tm, n // tn, nk), in_specs=[a_spec, b_spec],
        out_specs=pl.BlockSpec((tm, tn), lambda i, j, kk: (i, j)),
        out_shape=jax.ShapeDtypeStruct((m, n), out_dtype),
        compiler_params=_params(("parallel", "parallel", "arbitrary")),
    )(a, b)


def _norm_proj(x, norm_w, wp, tm=1024, tn=PACKED_WIDTH // 4):
    t = x.shape[0]
    tm = min(tm, t)

    def body(x_ref, nw_ref, w_ref, proj_ref, ht_ref, h_scr):
        @pl.when(pl.program_id(1) == 0)
        def _():
            xf = x_ref[...]
            r = lax.rsqrt(jnp.mean(xf * xf, axis=-1, keepdims=True) + NORM_EPS)
            h = xf * r * nw_ref[...]
            h_scr[...] = h.astype(h_scr.dtype)
            ht_ref[...] = h.T.astype(ht_ref.dtype)

        proj_ref[...] = jnp.dot(h_scr[...], w_ref[...], preferred_element_type=F32)

    return pl.pallas_call(
        body, name="norm_proj", grid=(t // tm, PACKED_WIDTH // tn),
        in_specs=[pl.BlockSpec((tm, D_MODEL), lambda i, j: (i, 0)),
                  pl.BlockSpec((1, D_MODEL), lambda i, j: (0, 0)),
                  pl.BlockSpec((D_MODEL, tn), lambda i, j: (0, j))],
        out_specs=[pl.BlockSpec((tm, tn), lambda i, j: (i, j)),
                   pl.BlockSpec((D_MODEL, tm), lambda i, j: (0, i))],
        out_shape=[jax.ShapeDtypeStruct((t, PACKED_WIDTH), F32), jax.ShapeDtypeStruct((D_MODEL, t), MXU_DTYPE)],
        scratch_shapes=[pltpu.VMEM((tm, D_MODEL), MXU_DTYPE)],
        compiler_params=_params(("parallel", "arbitrary")),
    )(x, norm_w, wp)


def _rope_tables(t):
    lane = jnp.arange(128)
    inv_freq = ROPE_THETA ** (-jnp.arange(0, HEAD_DIM, 2, dtype=F32) / HEAD_DIM)
    freq = jnp.concatenate([inv_freq] * 4)
    coarse = (jnp.arange(t // 128, dtype=F32) * 128.0)[:, None] * freq[None, :]
    fine = jnp.arange(128, dtype=F32)[:, None] * freq[None, :]
    ca, sa, cb, sb = jnp.cos(coarse)[:, None, :], jnp.sin(coarse)[:, None, :], jnp.cos(fine)[None], jnp.sin(fine)[None]
    cos = (ca * cb - sa * sb).reshape(t, 128)
    sin = (sa * cb + ca * sb).reshape(t, 128)
    first_half = (lane % HEAD_DIM) < HEAD_DIM // 2
    return cos, jnp.where(first_half, -sin, 0.0), jnp.where(first_half, 0.0, sin)


def _rope_cols(x, cos, sin_lo, sin_hi, sign):
    outs = []
    for c in range(x.shape[1] // 128):
        xc = x[:, c * 128:(c + 1) * 128]
        rot = pltpu.roll(xc, 96, 1) * sin_lo + pltpu.roll(xc, 32, 1) * sin_hi
        outs.append(xc * cos + sign * rot)
    return jnp.concatenate(outs, axis=1)


def _rope_block(x, cos, sin_lo, sin_hi, sign, q_scale=None):
    q = _rope_cols(x[:, :WIDTH], cos, sin_lo, sin_hi, sign)
    k = _rope_cols(x[:, WIDTH:2 * WIDTH], cos, sin_lo, sin_hi, sign)
    return jnp.concatenate([q if q_scale is None else q * q_scale, k, x[:, 2 * WIDTH:]], axis=1)


def _tile_scratch(tm, cols):
    return pltpu.VMEM((cols // 128, tm, 128), F32)


def _store_tile(scr, y):
    for c in range(scr.shape[0]):
        scr[c] = y[:, c * 128:(c + 1) * 128]


def _load_tile(scr):
    return jnp.concatenate([scr[c] for c in range(scr.shape[0])], axis=1)


def _to_strided_view(scr, o_ref, d, width=None, col0=0):
    n, tm, _ = scr.shape
    width = n * 128 if width is None else width
    for r in range(d):
        for c in range(n):
            at = r * width + col0 + c * 128
            o_ref[:, at:at + 128] = scr[c, pl.ds(r, tm // d, stride=d), :].astype(o_ref.dtype)


def _from_strided_view(i_ref, scr, d):
    n, tm, _ = scr.shape
    for r in range(d):
        for c in range(n):
            scr[c, pl.ds(r, tm // d, stride=d), :] = i_ref[:, (r * n + c) * 128:(r * n + c + 1) * 128].astype(F32)


def _strided_spec(tm, d, cols):
    return pl.BlockSpec((tm // d, d * cols), lambda i: (i, 0))


def _rope_fwd(proj, tables, tm=512):
    t = proj.shape[0]
    cols = 3 * WIDTH

    def body(x_ref, c_ref, sl_ref, sh_ref, o0, o1, o2, scr):
        for g, (d, o_ref) in enumerate(zip(DILATIONS, (o0, o1, o2))):
            y = _rope_block(x_ref[:, g * cols:(g + 1) * cols], c_ref[...], sl_ref[...], sh_ref[...], 1.0, HEAD_DIM ** -0.5)
            if d == 1:
                o_ref[...] = y.astype(o_ref.dtype)
            else:
                _store_tile(scr, y)
                _to_strided_view(scr, o_ref, d)

    tab = pl.BlockSpec((tm, 128), lambda i: (i, 0))
    return pl.pallas_call(
        body, name="rope_fwd", grid=(t // tm,),
        in_specs=[pl.BlockSpec((tm, 3 * cols), lambda i: (i, 0)), tab, tab, tab],
        out_specs=[_strided_spec(tm, d, cols) for d in DILATIONS],
        out_shape=[jax.ShapeDtypeStruct((t // d, d * cols), MXU_DTYPE) for d in DILATIONS],
        scratch_shapes=[_tile_scratch(tm, cols)],
        compiler_params=_params(("parallel",)),
    )(proj, *tables)


def _rope_bwd(dproj, dqs, dkvs, tables, tm=512):
    t = dproj.shape[0]
    cols = 3 * WIDTH

    def body(dp_ref, q0, q1, q2, kv0, kv1, kv2, c_ref, sl_ref, sh_ref, o_ref, scr_q, scr_kv):
        for g, (d, q_ref, kv_ref) in enumerate(zip(DILATIONS, (q0, q1, q2), (kv0, kv1, kv2))):
            if d == 1:
                x = jnp.concatenate([q_ref[...], kv_ref[...]], axis=1)
            else:
                _from_strided_view(q_ref, scr_q, d)
                _from_strided_view(kv_ref, scr_kv, d)
                x = jnp.concatenate([_load_tile(scr_q), _load_tile(scr_kv)], axis=1)
            y = _rope_block(x, c_ref[...], sl_ref[...], sh_ref[...], -1.0)
            o_ref[:, g * cols:(g + 1) * cols] = y.astype(o_ref.dtype)

    tab = pl.BlockSpec((tm, 128), lambda i: (i, 0))
    return pl.pallas_call(
        body, name="rope_bwd", grid=(t // tm,),
        in_specs=[pl.BlockSpec(memory_space=pl.ANY)] + [_strided_spec(tm, d, WIDTH) for d in DILATIONS]
        + [_strided_spec(tm, d, 2 * WIDTH) for d in DILATIONS] + [tab, tab, tab],
        out_specs=pl.BlockSpec((tm, 3 * cols), lambda i: (i, 0)),
        out_shape=jax.ShapeDtypeStruct((t, PACKED_WIDTH), MXU_DTYPE),
        scratch_shapes=[_tile_scratch(tm, WIDTH), _tile_scratch(tm, 2 * WIDTH)],
        input_output_aliases={0: 0},
        compiler_params=_params(("parallel",)),
    )(dproj, *dqs, *dkvs, *tables)


def _att_masks():
    qi = _iota((ATT_BLOCK, ATT_BLOCK), 0)
    kj = _iota((ATT_BLOCK, ATT_BLOCK), 1)
    return kj <= qi, kj >= qi


def _att_fwd(qkv, d, name):
    rows = qkv.shape[0]
    nb = rows // ATT_BLOCK

    def body(q_ref, kc_ref, kp_ref, vc_ref, vp_ref, o_ref, lse_ref):
        has_prev = pl.program_id(1) > 0
        m_cur, m_prev = _att_masks()
        m_prev = m_prev & has_prev
        hs = range(HEADS)
        sls = [slice(h * HEAD_DIM, (h + 1) * HEAD_DIM) for h in hs]
        qs = [q_ref[:, sl] for sl in sls]
        s_c = [jnp.where(m_cur, _mxu(qs[h], kc_ref[:, sls[h]], _NT), -jnp.inf) for h in hs]
        s_p = [jnp.where(m_prev, _mxu(qs[h], kp_ref[:, sls[h]], _NT), -jnp.inf) for h in hs]
        m = [jnp.max(jnp.maximum(s_c[h], s_p[h]), axis=1, keepdims=True) for h in hs]
        p_c = [jnp.exp(s_c[h] - m[h]) for h in hs]
        p_p = [jnp.exp(s_p[h] - m[h]) for h in hs]
        den = [jnp.sum(p_c[h] + p_p[h], axis=1, keepdims=True) for h in hs]
        o = [_mxu(p_c[h], vc_ref[:, sls[h]], _NN) + _mxu(p_p[h], vp_ref[:, sls[h]], _NN) for h in hs]
        lane = _iota((1, 128), 1)
        lse = jnp.zeros((ATT_BLOCK, 128), F32)
        for h in hs:
            o_ref[:, sls[h]] = o[h] / den[h]
            lse = jnp.where(lane == h, m[h] + jnp.log(den[h]), lse)
        lse_ref[...] = lse

    def cur(c):
        return pl.BlockSpec((ATT_BLOCK, WIDTH), lambda r, i: (i, 3 * r + c))

    def prev(c):
        return pl.BlockSpec((ATT_BLOCK, WIDTH), lambda r, i: (jnp.maximum(i - 1, 0), 3 * r + c))

    return pl.pallas_call(
        body, name=name, grid=(d, nb), in_specs=[cur(0), cur(1), prev(1), cur(2), prev(2)],
        out_specs=[pl.BlockSpec((ATT_BLOCK, WIDTH), lambda r, i: (i, r)), pl.BlockSpec((ATT_BLOCK, 128), lambda r, i: (i, r))],
        out_shape=[jax.ShapeDtypeStruct((rows, d * WIDTH), F32), jax.ShapeDtypeStruct((rows, d * 128), F32)],
        compiler_params=_params(("parallel", "arbitrary")),
    )(qkv, qkv, qkv, qkv, qkv)


def _att_bwd(qkv, do, lse, delta, d, name):
    rows = qkv.shape[0]
    nb = rows // ATT_BLOCK
    scale = HEAD_DIM ** -0.5

    def body(q_ref, kc_ref, kp_ref, vc_ref, vp_ref, do_ref, l_ref, dl_ref, dq_ref, dkv_ref, own):
        i = pl.program_id(1)
        m_cur, m_prev = _att_masks()
        m_prev = m_prev & (i > 0)

        hs = range(HEADS)
        sls = [slice(h * HEAD_DIM, (h + 1) * HEAD_DIM) for h in hs]
        col = [slice(h, h + 1) for h in hs]

        def probs(k_r, mask):
            s = [_mxu(q_ref[:, sls[h]], k_r[:, sls[h]], _NT) for h in hs]
            return [jnp.where(mask, jnp.exp(s[h] - l_ref[:, col[h]]), 0.0) for h in hs]

        def dscores(p, v_r):
            dp = [_mxu(do_ref[:, sls[h]], v_r[:, sls[h]], _NT) for h in hs]
            return [(p[h] * (dp[h] - dl_ref[:, col[h]])).astype(MXU_DTYPE) for h in hs]

        @pl.when(i == 0)
        def _():
            own[...] = jnp.zeros_like(own)

        @pl.when(i < nb)
        def _():
            p_c = probs(kc_ref, m_cur)
            ds_c = dscores(p_c, vc_ref)
            p_p = probs(kp_ref, m_prev)
            ds_p = dscores(p_p, vp_ref)
            dq = [_mxu(ds_c[h], kc_ref[:, sls[h]], _NN) + _mxu(ds_p[h], kp_ref[:, sls[h]], _NN) for h in hs]
            dk_p = [_mxu(ds_p[h], q_ref[:, sls[h]], _TN) for h in hs]
            dv_p = [_mxu(p_p[h], do_ref[:, sls[h]], _TN) for h in hs]
            dk_c = [_mxu(ds_c[h], q_ref[:, sls[h]], _TN) for h in hs]
            dv_c = [_mxu(p_c[h], do_ref[:, sls[h]], _TN) for h in hs]
            for h in hs:
                vs = slice(WIDTH + h * HEAD_DIM, WIDTH + (h + 1) * HEAD_DIM)
                dq_ref[:, sls[h]] = dq[h] * scale
                dkv_ref[:, sls[h]] = own[:, sls[h]] + dk_p[h]
                dkv_ref[:, vs] = own[:, vs] + dv_p[h]
                own[:, sls[h]] = dk_c[h]
                own[:, vs] = dv_c[h]

        @pl.when(i == nb)
        def _():
            dkv_ref[...] = own[...]

    def qkv_spec(c, shift):
        return pl.BlockSpec((ATT_BLOCK, WIDTH), lambda r, i: (jnp.clip(i + shift, 0, nb - 1), 3 * r + c))

    tok = pl.BlockSpec((ATT_BLOCK, WIDTH), lambda r, i: (jnp.minimum(i, nb - 1), r))
    per_head = pl.BlockSpec((ATT_BLOCK, 128), lambda r, i: (jnp.minimum(i, nb - 1), r))
    return pl.pallas_call(
        body, name=name, grid=(d, nb + 1),
        in_specs=[qkv_spec(0, 0), qkv_spec(1, 0), qkv_spec(1, -1), qkv_spec(2, 0), qkv_spec(2, -1), tok, per_head, per_head],
        out_specs=[tok, pl.BlockSpec((ATT_BLOCK, 2 * WIDTH), lambda r, i: (jnp.maximum(i - 1, 0), r))],
        out_shape=[jax.ShapeDtypeStruct((rows, d * WIDTH), F32), jax.ShapeDtypeStruct((rows, d * 2 * WIDTH), F32)],
        scratch_shapes=[pltpu.VMEM((ATT_BLOCK, 2 * WIDTH), F32)],
        compiler_params=_params(("parallel", "arbitrary")),
    )(qkv, qkv, qkv, qkv, qkv, do, lse, delta)


def _att_merge(os_, lses, proj, tm=1024):
    t = proj.shape[0]

    def body(o0, o1, o2, l0, l1, l2, z_ref, oz_ref, o_ref, t0, t1, t2, s_o1, s_o2, s_l1, s_l2, s_t):
        _from_strided_view(o1, s_o1, DILATIONS[1])
        _from_strided_view(o2, s_o2, DILATIONS[2])
        _from_strided_view(l1, s_l1, DILATIONS[1])
        _from_strided_view(l2, s_l2, DILATIONS[2])
        a, b, c = l0[...], s_l1[0], s_l2[0]
        m = jnp.maximum(jnp.maximum(a, b), c)
        wa, wb, wc = jnp.exp(a - m), jnp.exp(b - m), jnp.exp(c - m)
        den = wa + wb + wc
        total = m + jnp.log(den)
        t0[...] = total
        s_t[0] = total
        _to_strided_view(s_t, t1, DILATIONS[1])
        _to_strided_view(s_t, t2, DILATIONS[2])
        spread = jnp.where(_iota((128, WIDTH), 1) // HEAD_DIM == _iota((128, WIDTH), 0), 1.0, 0.0).astype(MXU_DTYPE)
        ra, rb, rc = (_pair_sum_lanes(w / den, spread) for w in (wa, wb, wc))
        o = ra * o0[...] + rb * _load_tile(s_o1) + rc * _load_tile(s_o2)
        z = z_ref[...]
        o_ref[...] = o
        oz_ref[...] = (o * z * _sigmoid(z)).astype(oz_ref.dtype)

    tok = pl.BlockSpec((tm, WIDTH), lambda i: (i, 0))
    views = [_strided_spec(tm, d, WIDTH) for d in DILATIONS]
    per_head = [_strided_spec(tm, d, 128) for d in DILATIONS]
    return pl.pallas_call(
        body, name="att_merge", grid=(t // tm,),
        in_specs=views + per_head + [pl.BlockSpec((tm, WIDTH), lambda i: (i, SEG_ZA // WIDTH))],
        out_specs=[tok, tok] + per_head,
        out_shape=[jax.ShapeDtypeStruct((t, WIDTH), MXU_DTYPE), jax.ShapeDtypeStruct((t, WIDTH), F32)]
        + [jax.ShapeDtypeStruct((t // d, d * 128), F32) for d in DILATIONS],
        scratch_shapes=[_tile_scratch(tm, WIDTH)] * 2 + [_tile_scratch(tm, 128)] * 3,
        compiler_params=_params(("parallel",)),
    )(*os_, *lses, proj)


def _att_merge_bwd(dproj, dy, w_up, o, proj, tm=1024):
    t = proj.shape[0]

    def body(dp_ref, dy_ref, wu_ref, o_ref, z_ref, dz_ref, do0, do1, do2, dl0, dl1, dl2, s_do, s_dl):
        z, ov, g = z_ref[...], o_ref[...], _mxu(dy_ref[...], wu_ref[...], _NT)
        sg = _sigmoid(z)
        do = g * z * sg
        dz_ref[...] = (g * ov * sg * (1.0 + z * (1.0 - sg))).astype(dz_ref.dtype)
        do0[...] = do.astype(do0.dtype)
        _store_tile(s_do, do)
        prod = do * ov
        delta = jnp.zeros((tm, 128), F32)
        for p in range(HEADS // 2):
            delta += _pair_sum_lanes(prod[:, p * 128:(p + 1) * 128], _pair_to_lanes(p, 0))
        dl0[...] = delta
        s_dl[0] = delta
        for d, do_v, dl_v in ((DILATIONS[1], do1, dl1), (DILATIONS[2], do2, dl2)):
            _to_strided_view(s_do, do_v, d)
            _to_strided_view(s_dl, dl_v, d)

    tok = pl.BlockSpec((tm, WIDTH), lambda i: (i, 0))
    seg = pl.BlockSpec((tm, WIDTH), lambda i: (i, SEG_ZA // WIDTH))
    views = [_strided_spec(tm, d, WIDTH) for d in DILATIONS]
    per_head = [_strided_spec(tm, d, 128) for d in DILATIONS]
    return pl.pallas_call(
        body, name="att_merge_bwd", grid=(t // tm,),
        in_specs=[pl.BlockSpec(memory_space=pl.ANY), pl.BlockSpec((tm, D_MODEL), lambda i: (i, 0)),
                  pl.BlockSpec((WIDTH, D_MODEL), lambda i: (0, 0)), tok, seg],
        out_specs=[seg] + views + per_head,
        out_shape=[jax.ShapeDtypeStruct((t, PACKED_WIDTH), MXU_DTYPE)]
        + [jax.ShapeDtypeStruct((t // d, d * WIDTH), MXU_DTYPE) for d in DILATIONS]
        + [jax.ShapeDtypeStruct((t // d, d * 128), F32) for d in DILATIONS],
        scratch_shapes=[_tile_scratch(tm, WIDTH), _tile_scratch(tm, 128)],
        input_output_aliases={0: 0},
        compiler_params=_params(("parallel",)),
    )(dproj, dy, w_up, o, proj)


def _shift_down(x, halo, s):
    if s == 0:
        return x
    xs = pltpu.roll(x, s, 0)
    head = jnp.where(_iota((8, x.shape[1]), 0) < s, pltpu.roll(halo, s, 0), xs[0:8])
    return jnp.concatenate([head, xs[8:]], axis=0)


def _shift_up(x, nxt, s):
    if s == 0:
        return x
    n = x.shape[0]
    xs = pltpu.roll(x, n - s, 0)
    tail = jnp.where(_iota((8, x.shape[1]), 0) >= 8 - s, pltpu.roll(nxt, 8 - s, 0), xs[n - 8:])
    return jnp.concatenate([xs[:n - 8], tail], axis=0)


def _conv_fwd(proj, conv_w, tm=1024):
    t = proj.shape[0]
    cb = SEG_B // WIDTH

    def body(x_ref, halo_ref, w_ref, c_ref):
        halo = jnp.where(pl.program_id(0) > 0, halo_ref[...], 0.0)
        x = x_ref[...]
        w = w_ref[...]
        acc = jnp.zeros((tm, WIDTH), F32)
        for j in range(GDN_CONV):
            acc += _shift_down(x, halo, GDN_CONV - 1 - j) * w[j:j + 1, :]
        c_ref[...] = acc

    return pl.pallas_call(
        body, name="conv_fwd", grid=(t // tm, 3),
        in_specs=[pl.BlockSpec((tm, WIDTH), lambda i, c: (i, cb + c)),
                  pl.BlockSpec((8, WIDTH), lambda i, c: (jnp.maximum(i * (tm // 8) - 1, 0), cb + c)),
                  pl.BlockSpec((GDN_CONV, WIDTH), lambda i, c: (0, c))],
        out_specs=pl.BlockSpec((tm, WIDTH), lambda i, c: (i, c)),
        out_shape=jax.ShapeDtypeStruct((t, 3 * WIDTH), F32),
        compiler_params=_params(("parallel", "parallel")),
    )(proj, proj, conv_w)


def _conv_bwd(dproj, dc, proj, conv_w, tm=1024):
    t = proj.shape[0]
    cb = SEG_B // WIDTH
    nt = t // tm

    def body(dp_ref, dc_ref, dcn_ref, x_ref, halo_ref, w_ref, dx_ref, dw_ref):
        i = pl.program_id(1)
        w = w_ref[...]
        dcn = jnp.where(i < nt - 1, dcn_ref[...], 0.0)
        dcv = dc_ref[...]
        acc = jnp.zeros((tm, WIDTH), F32)
        for j in range(GDN_CONV):
            acc += _shift_up(dcv, dcn, GDN_CONV - 1 - j) * w[j:j + 1, :]
        dx_ref[...] = acc.astype(dx_ref.dtype)
        halo = jnp.where(i > 0, halo_ref[...], 0.0)
        x = x_ref[...]
        row8 = _iota((8, WIDTH), 0)
        part = jnp.zeros((8, WIDTH), F32)
        for j in range(GDN_CONV):
            s = jnp.sum(dcv * _shift_down(x, halo, GDN_CONV - 1 - j), axis=0, keepdims=True)
            part += jnp.where(row8 == j, s, 0.0)

        @pl.when(i == 0)
        def _():
            dw_ref[...] = jnp.zeros_like(dw_ref)

        dw_ref[...] += part

    return pl.pallas_call(
        body, name="conv_bwd", grid=(3, nt),
        in_specs=[pl.BlockSpec(memory_space=pl.ANY),
                  pl.BlockSpec((tm, WIDTH), lambda c, i: (i, c)),
                  pl.BlockSpec((8, WIDTH), lambda c, i: (jnp.minimum((i + 1) * (tm // 8), t // 8 - 1), c)),
                  pl.BlockSpec((tm, WIDTH), lambda c, i: (i, cb + c)),
                  pl.BlockSpec((8, WIDTH), lambda c, i: (jnp.maximum(i * (tm // 8) - 1, 0), cb + c)),
                  pl.BlockSpec((GDN_CONV, WIDTH), lambda c, i: (0, c))],
        out_specs=[pl.BlockSpec((tm, WIDTH), lambda c, i: (i, cb + c)),
                   pl.BlockSpec((8, WIDTH), lambda c, i: (0, c))],
        out_shape=[jax.ShapeDtypeStruct((t, PACKED_WIDTH), MXU_DTYPE), jax.ShapeDtypeStruct((8, 3 * WIDTH), F32)],
        input_output_aliases={0: 0},
        compiler_params=_params(("parallel", "arbitrary")),
    )(dproj, dc, dc, proj, proj, conv_w)


def _chunk_matrices(tm):
    r, c = _iota((tm, tm), 0), _iota((tm, tm), 1)
    same = (r // GDN_CHUNK) == (c // GDN_CHUNK)
    return jnp.where(same & (c <= r), 1.0, 0.0), jnp.where(same, 1.0, 0.0)


def _gdn_gates(ba, a_log, dt_bias):
    al = ba + dt_bias
    return _sigmoid(ba), -jnp.exp(a_log) * _softplus(al), _sigmoid(al)


SCAN_HEADS = (0, 2, 4, 6, 1, 3, 5, 7)


def _head_lane_eye():
    return jnp.where(_iota((HEADS, 128), 1) == _iota((HEADS, 128), 0) + HEADS, 1.0, 0.0)


def _pair_ones():
    return jnp.where(_iota((128, 128), 0) // HEAD_DIM == _iota((128, 128), 1) // HEAD_DIM, 1.0, 0.0).astype(MXU_DTYPE)


def _pair_to_lanes(p, base):
    return jnp.where(_iota((128, 128), 1) == base + 2 * p + _iota((128, 128), 0) // HEAD_DIM, 1.0, 0.0).astype(MXU_DTYPE)


def _pair_sum_lanes(x, sel):
    hi, lo = _split(x)
    return jnp.dot(hi, sel, preferred_element_type=F32) + jnp.dot(lo, sel, preferred_element_type=F32)


def _gdn_prep(conv, proj, a_log, dt_bias, tm=512):
    t = proj.shape[0]
    nc = tm // GDN_CHUNK

    def body(c_ref, ba_ref, al_ref, dt_ref, q_ref, k_ref, v_ref, b_ref, g_ref, gl_ref, grow_ref):
        beta, g, _ = _gdn_gates(ba_ref[:, 0:128], al_ref[...], dt_ref[...])
        lmat, cmat = _chunk_matrices(tm)
        gc = jnp.dot(lmat, g, precision=HIGHEST, preferred_element_type=F32)
        gl = jnp.dot(cmat, g, precision=HIGHEST, preferred_element_type=F32)
        grow = lax.dot_general(_head_lane_eye(), gc, _NT, precision=HIGHEST, preferred_element_type=F32)
        ones = _pair_ones()
        first = _iota((1, 128), 1) < HEAD_DIM

        def spread(x, base, p):
            return jnp.where(first, x[:, base + 2 * p:base + 2 * p + 1], x[:, base + 2 * p + 1:base + 2 * p + 2])

        for p in range(HEADS // 2):
            for seg, ref, scale in ((0, q_ref, HEAD_DIM ** -0.5), (1, k_ref, 1.0), (2, v_ref, None)):
                c = c_ref[:, seg * WIDTH + p * 128:seg * WIDTH + (p + 1) * 128]
                a = c * _sigmoid(c)
                if scale is not None:
                    a = a * (lax.rsqrt(_pair_sum_lanes(a * a, ones) + NORM_EPS) * scale)
                ref[p] = a
            b_ref[p] = spread(beta, 0, p)
            g_ref[p] = spread(gc, HEADS, p)
            gl_ref[p] = spread(gl, HEADS, p)
        for pos, h in enumerate(SCAN_HEADS):
            for cc in range(nc):
                grow_ref[pos, cc] = grow[h:h + 1, cc * GDN_CHUNK:(cc + 1) * GDN_CHUNK]

    hm = pl.BlockSpec((HEADS // 2, tm, 128), lambda i: (0, i, 0))
    small = pl.BlockSpec((1, 128), lambda i: (0, 0))
    hm_shape = jax.ShapeDtypeStruct((HEADS // 2, t, 128), F32)
    return pl.pallas_call(
        body, name="gdn_prep", grid=(t // tm,),
        in_specs=[pl.BlockSpec((tm, 3 * WIDTH), lambda i: (i, 0)),
                  pl.BlockSpec((tm, WIDTH), lambda i: (i, SEG_BA // WIDTH)), small, small],
        out_specs=[hm] * 6 + [pl.BlockSpec((HEADS, nc, 1, GDN_CHUNK), lambda i: (0, i, 0, 0))],
        out_shape=[hm_shape] * 6 + [jax.ShapeDtypeStruct((HEADS, t // GDN_CHUNK, 1, GDN_CHUNK), F32)],
        compiler_params=_params(("parallel",)),
    )(conv, proj, a_log, dt_bias)


def _gdn_prep_bwd(dproj, conv, proj, a_log, dt_bias, dq, dk, dv, db, dg, dgl, dgrow, tm=512):
    t = proj.shape[0]
    nc = tm // GDN_CHUNK

    def body(dp_ref, c_ref, ba_ref, al_ref, dt_ref, dq_ref, dk_ref, dv_ref, db_ref, dg_ref, dgl_ref, dgrow_ref,
             dba_ref, dc_ref, small_ref, row_scr):
        beta, g, sig_al = _gdn_gates(ba_ref[:, 0:128], al_ref[...], dt_ref[...])
        d_beta = jnp.zeros((tm, 128), F32)
        d_gc = jnp.zeros((tm, 128), F32)
        d_gl = jnp.zeros((tm, 128), F32)
        ones = _pair_ones()
        for pos, h in enumerate(SCAN_HEADS):
            for cc in range(nc):
                row_scr[h:h + 1, cc * GDN_CHUNK:(cc + 1) * GDN_CHUNK] = dgrow_ref[pos, cc]
        for p in range(HEADS // 2):
            d_beta += _pair_sum_lanes(db_ref[p], _pair_to_lanes(p, 0))
            d_gc += _pair_sum_lanes(dg_ref[p], _pair_to_lanes(p, HEADS))
            d_gl += _pair_sum_lanes(dgl_ref[p], _pair_to_lanes(p, HEADS))
            for seg, ref, scale in ((0, dq_ref, HEAD_DIM ** -0.5), (1, dk_ref, 1.0), (2, dv_ref, None)):
                cols = slice(seg * WIDTH + p * 128, seg * WIDTH + (p + 1) * 128)
                c = c_ref[:, cols]
                sg = _sigmoid(c)
                da = ref[p]
                if scale is not None:
                    a = c * sg
                    r = lax.rsqrt(_pair_sum_lanes(a * a, ones) + NORM_EPS)
                    da = da * scale
                    da = r * da - a * (r * r * r) * _pair_sum_lanes(da * a, ones)
                dc_ref[:, cols] = da * (sg * (1.0 + c * (1.0 - sg)))
        d_gc += lax.dot_general(row_scr[...], _head_lane_eye(), _TN, precision=HIGHEST, preferred_element_type=F32)
        lmat, cmat = _chunk_matrices(tm)
        d_g = (lax.dot_general(lmat, d_gc, _TN, precision=HIGHEST, preferred_element_type=F32)
               + lax.dot_general(cmat, d_gl, _TN, precision=HIGHEST, preferred_element_type=F32))
        d_al = d_g * (-jnp.exp(al_ref[...])) * sig_al
        d_bl = d_beta * beta * (1.0 - beta)
        dba_ref[...] = jnp.concatenate([d_bl + d_al, jnp.zeros((tm, WIDTH - 128), F32)], axis=1).astype(dba_ref.dtype)
        row8 = _iota((8, 128), 0)
        part = (jnp.where(row8 == 0, jnp.sum(d_g * g, axis=0, keepdims=True), 0.0)
                + jnp.where(row8 == 1, jnp.sum(d_al, axis=0, keepdims=True), 0.0))

        @pl.when(pl.program_id(0) == 0)
        def _():
            small_ref[...] = jnp.zeros_like(small_ref)

        small_ref[...] += part

    hm = pl.BlockSpec((HEADS // 2, tm, 128), lambda i: (0, i, 0))
    small = pl.BlockSpec((1, 128), lambda i: (0, 0))
    seg = pl.BlockSpec((tm, WIDTH), lambda i: (i, SEG_BA // WIDTH))
    return pl.pallas_call(
        body, name="gdn_prep_bwd", grid=(t // tm,),
        in_specs=[pl.BlockSpec(memory_space=pl.ANY), pl.BlockSpec((tm, 3 * WIDTH), lambda i: (i, 0)), seg, small, small]
        + [hm] * 6 + [pl.BlockSpec((HEADS, nc, 1, GDN_CHUNK), lambda i: (0, i, 0, 0))],
        out_specs=[seg, pl.BlockSpec((tm, 3 * WIDTH), lambda i: (i, 0)), pl.BlockSpec((8, 128), lambda i: (0, 0))],
        out_shape=[jax.ShapeDtypeStruct((t, PACKED_WIDTH), MXU_DTYPE), jax.ShapeDtypeStruct((t, 3 * WIDTH), F32),
                   jax.ShapeDtypeStruct((8, 128), F32)],
        scratch_shapes=[pltpu.VMEM((HEADS, tm), F32)],
        input_output_aliases={0: 0},
        compiler_params=_params(("arbitrary",)),
    )(dproj, conv, proj, a_log, dt_bias, dq, dk, dv, db, dg, dgl, dgrow)


_BNN = (((2,), (1,)), ((0,), (0,)))
_BNT = (((2,), (2,)), ((0,), (0,)))
_BTN = (((1,), (1,)), ((0,), (0,)))


@jax.custom_vjp
def _MM_NN(a, b):
    return _mxu(a, b, _BNN)


@jax.custom_vjp
def _MM_NT(a, b):
    return _mxu(a, b, _BNT)


@jax.custom_vjp
def _MM_TN(a, b):
    return _mxu(a, b, _BTN)


_MM_NN.defvjp(lambda a, b: (_mxu(a, b, _BNN), (a, b)), lambda r, g: (_mxu(g, r[1], _BNT), _mxu(r[0], g, _BTN)))
_MM_NT.defvjp(lambda a, b: (_mxu(a, b, _BNT), (a, b)), lambda r, g: (_mxu(g, r[1], _BNN), _mxu(g, r[0], _BTN)))
_MM_TN.defvjp(lambda a, b: (_mxu(a, b, _BTN), (a, b)), lambda r, g: (_mxu(r[1], g, _BNT), _mxu(r[0], g, _BNN)))


def _split(a):
    hi = a.astype(MXU_DTYPE)
    return hi, (a - hi.astype(F32)).astype(MXU_DTYPE)


def _dot3(a, b, dims):
    (ah, al), (bh, bl) = a, b
    (ca,), (cb,) = dims[0]
    return lax.dot_general(jnp.concatenate([ah, ah, al], axis=ca), jnp.concatenate([bh, bl, bh], axis=cb), dims,
                           preferred_element_type=F32)


def _unit_lower_inverse(a):
    c = GDN_CHUNK
    eye = jnp.where(_iota((c, c), 0) == _iota((c, c), 1), 1.0, 0.0)
    x = eye - a
    p = a
    for _ in range(5):
        ps = _split(p)
        p = _dot3(ps, ps, _BNN)
        x = x + _dot3(_split(x), _split(p), _BNN)
    return x


@jax.custom_vjp
def _SAVED_INVERSE(a, t_inv):
    return t_inv


def _saved_inverse_bwd(t_inv, g):
    ts = _split(t_inv)
    return -_dot3(ts, _split(_dot3(_split(g), ts, _BNT)), _BTN), jnp.zeros_like(t_inv)


_SAVED_INVERSE.defvjp(lambda a, t_inv: (t_inv, t_inv), _saved_inverse_bwd)


def _gdn_chunk(q, k, v, beta, g1, g2, gl, state, t_inv=None):
    c = GDN_CHUNK
    if t_inv is None:
        _mm_nn, _mm_nt, _mm_tn = (functools.partial(_mxu, dims=dd) for dd in (_BNN, _BNT, _BTN))
    else:
        _mm_nn, _mm_nt, _mm_tn = _MM_NN, _MM_NT, _MM_TN
    row, col = _iota((c, c), 0), _iota((c, c), 1)
    incl, strict = row >= col, row > col
    decay = jnp.where(incl, jnp.exp(jnp.where(incl, g1 - g2, 0.0)), 0.0)
    eg = jnp.exp(g1)
    kb = k * beta
    a = _mm_nt(kb, k) * jnp.where(strict, decay, 0.0)
    inv = _unit_lower_inverse(a) if t_inv is None else _SAVED_INVERSE(a, t_inv)
    u = _mm_nn(inv, v * beta)
    w = _mm_nn(inv, kb * eg)
    attn = _mm_nt(q, k) * decay
    v_new = u - _mm_nn(w, state)
    o = _mm_nn(q * eg, state) + _mm_nn(attn, v_new)
    new_state = state * jnp.exp(gl) + _mm_tn(k * jnp.exp(gl - g1), v_new)
    return (o, new_state, inv) if t_inv is None else (o, new_state)


def _unpair(x):
    return jnp.concatenate([x[..., :HEAD_DIM], x[..., HEAD_DIM:]], axis=0)


def _gdn_fwd(q, k, v, beta, g, gl, grow, cpb=4):
    t = q.shape[1]
    rows = cpb * GDN_CHUNK
    lo, hi = slice(0, HEAD_DIM), slice(HEAD_DIM, 2 * HEAD_DIM)

    def body(q_ref, k_ref, v_ref, b_ref, g_ref, gl_ref, grow_ref, o_ref, st_ref, inv_ref, state):
        @pl.when(pl.program_id(0) == 0)
        def _():
            state[...] = jnp.zeros_like(state)

        s = state[...]
        for cc in range(cpb):
            sl = slice(cc * GDN_CHUNK, (cc + 1) * GDN_CHUNK)
            st_ref[:, cc, :, lo], st_ref[:, cc, :, hi] = s[:HEADS // 2], s[HEADS // 2:]
            g2 = jnp.broadcast_to(grow_ref[:, cc], (HEADS, GDN_CHUNK, GDN_CHUNK))
            o, s, inv = _gdn_chunk(*[_unpair(r[:, sl, :]) for r in (q_ref, k_ref, v_ref, b_ref, g_ref)], g2,
                                   _unpair(gl_ref[:, sl, :]), s)
            o_ref[:, sl, lo], o_ref[:, sl, hi] = o[:HEADS // 2], o[HEADS // 2:]
            inv_ref[:, cc, :, lo], inv_ref[:, cc, :, hi] = inv[:HEADS // 2], inv[HEADS // 2:]
        state[...] = s

    hm = pl.BlockSpec((HEADS // 2, rows, 128), lambda i: (0, i, 0))
    per_chunk = pl.BlockSpec((HEADS // 2, cpb, GDN_CHUNK, 128), lambda i: (0, i, 0, 0))
    chunk_shape = jax.ShapeDtypeStruct((HEADS // 2, t // GDN_CHUNK, GDN_CHUNK, 128), F32)
    return pl.pallas_call(
        body, name="gdn_fwd", grid=(t // rows,),
        in_specs=[hm] * 6 + [pl.BlockSpec((HEADS, cpb, 1, GDN_CHUNK), lambda i: (0, i, 0, 0))],
        out_specs=[hm, per_chunk, per_chunk],
        out_shape=[jax.ShapeDtypeStruct((HEADS // 2, t, 128), F32), chunk_shape, chunk_shape],
        scratch_shapes=[pltpu.VMEM((HEADS, GDN_CHUNK, HEAD_DIM), F32)],
        compiler_params=_params(("arbitrary",)),
    )(q, k, v, beta, g, gl, grow)


def _gdn_bwd(q, k, v, beta, g, gl, grow, states, invs, do, cpb=4):
    t = q.shape[1]
    rows = cpb * GDN_CHUNK
    nsteps = t // rows
    lo, hi = slice(0, HEAD_DIM), slice(HEAD_DIM, 2 * HEAD_DIM)

    def body(q_ref, k_ref, v_ref, b_ref, g_ref, gl_ref, grow_ref, st_ref, inv_ref, do_ref,
             dq_ref, dk_ref, dv_ref, db_ref, dg_ref, dgl_ref, dgrow_ref, dstate):
        @pl.when(pl.program_id(0) == 0)
        def _():
            dstate[...] = jnp.zeros_like(dstate)

        ds = dstate[...]
        for cc in reversed(range(cpb)):
            sl = slice(cc * GDN_CHUNK, (cc + 1) * GDN_CHUNK)
            g2 = jnp.broadcast_to(grow_ref[:, cc], (HEADS, GDN_CHUNK, GDN_CHUNK))
            _, vjp = jax.vjp(_gdn_chunk, *[_unpair(r[:, sl, :]) for r in (q_ref, k_ref, v_ref, b_ref, g_ref)], g2,
                             _unpair(gl_ref[:, sl, :]), _unpair(st_ref[:, cc]), _unpair(inv_ref[:, cc]))
            gq, gk, gv, gb, gg1, gg2, ggl, ds, _ = vjp((_unpair(do_ref[:, sl, :]), ds))
            for ref, val in ((dq_ref, gq), (dk_ref, gk), (dv_ref, gv), (db_ref, gb), (dg_ref, gg1), (dgl_ref, ggl)):
                ref[:, sl, lo], ref[:, sl, hi] = val[:HEADS // 2], val[HEADS // 2:]
            dgrow_ref[:, cc] = jnp.sum(gg2, axis=1, keepdims=True)
        dstate[...] = ds

    hm = pl.BlockSpec((HEADS // 2, rows, 128), lambda i: (0, nsteps - 1 - i, 0))
    rowspec = pl.BlockSpec((HEADS, cpb, 1, GDN_CHUNK), lambda i: (0, nsteps - 1 - i, 0, 0))
    per_chunk = pl.BlockSpec((HEADS // 2, cpb, GDN_CHUNK, 128), lambda i: (0, nsteps - 1 - i, 0, 0))
    hm_shape = jax.ShapeDtypeStruct((HEADS // 2, t, 128), F32)
    return pl.pallas_call(
        body, name="gdn_bwd", grid=(nsteps,),
        in_specs=[hm] * 6 + [rowspec, per_chunk, per_chunk, hm],
        out_specs=[hm] * 6 + [rowspec],
        out_shape=[hm_shape] * 6 + [jax.ShapeDtypeStruct((HEADS, t // GDN_CHUNK, 1, GDN_CHUNK), F32)],
        scratch_shapes=[pltpu.VMEM((HEADS, GDN_CHUNK, HEAD_DIM), F32)],
        compiler_params=_params(("arbitrary",)),
    )(q, k, v, beta, g, gl, grow, states, invs, do)


def _gdn_out(o_hm, gdn_norm_w, proj, tm=1024):
    t = proj.shape[0]

    def body(o_ref, w_ref, z_ref, oz_ref):
        w = w_ref[...]
        ones = _pair_ones()
        for p in range(HEADS // 2):
            cols = slice(p * 128, (p + 1) * 128)
            o = o_ref[p]
            z = z_ref[:, cols]
            r = lax.rsqrt(_pair_sum_lanes(o * o, ones) * (1.0 / HEAD_DIM) + NORM_EPS)
            oz_ref[:, cols] = (o * r * w * (z * _sigmoid(z))).astype(oz_ref.dtype)

    tok = pl.BlockSpec((tm, WIDTH), lambda i: (i, 0))
    return pl.pallas_call(
        body, name="gdn_out", grid=(t // tm,),
        in_specs=[pl.BlockSpec((HEADS // 2, tm, 128), lambda i: (0, i, 0)), pl.BlockSpec((1, 128), lambda i: (0, 0)),
                  pl.BlockSpec((tm, WIDTH), lambda i: (i, SEG_ZB // WIDTH))],
        out_specs=tok, out_shape=jax.ShapeDtypeStruct((t, WIDTH), MXU_DTYPE),
        compiler_params=_params(("parallel",)),
    )(o_hm, jnp.tile(gdn_norm_w, (1, 2)), proj)


def _gdn_out_bwd(dproj, dy, w_up, o_hm, gdn_norm_w, proj, tm=1024):
    t = proj.shape[0]

    def body(dp_ref, dy_ref, wu_ref, o_ref, w_ref, z_ref, dz_ref, do_ref, dw_ref):
        w = w_ref[...]
        ones = _pair_ones()
        dw = jnp.zeros((1, 128), F32)
        d_oz = _mxu(dy_ref[...], wu_ref[...], _NT)
        for p in range(HEADS // 2):
            cols = slice(p * 128, (p + 1) * 128)
            o = o_ref[p]
            z, g = z_ref[:, cols], d_oz[:, cols]
            sg = _sigmoid(z)
            r = lax.rsqrt(_pair_sum_lanes(o * o, ones) * (1.0 / HEAD_DIM) + NORM_EPS)
            dz_ref[:, cols] = (g * (o * r * w) * (sg * (1.0 + z * (1.0 - sg)))).astype(dz_ref.dtype)
            dn = g * (z * sg)
            dw += jnp.sum(dn * o * r, axis=0, keepdims=True)
            dnw = dn * w
            do_ref[p] = r * dnw - o * (r * r * r) * (_pair_sum_lanes(dnw * o, ones) * (1.0 / HEAD_DIM))

        @pl.when(pl.program_id(0) == 0)
        def _():
            dw_ref[...] = jnp.zeros_like(dw_ref)

        dw_ref[...] += jnp.where(_iota((8, 128), 0) == 0, dw, 0.0)

    tok = pl.BlockSpec((tm, WIDTH), lambda i: (i, 0))
    seg = pl.BlockSpec((tm, WIDTH), lambda i: (i, SEG_ZB // WIDTH))
    hm = pl.BlockSpec((HEADS // 2, tm, 128), lambda i: (0, i, 0))
    dz, do, dw = pl.pallas_call(
        body, name="gdn_out_bwd", grid=(t // tm,),
        in_specs=[pl.BlockSpec(memory_space=pl.ANY), pl.BlockSpec((tm, D_MODEL), lambda i: (i, 0)),
                  pl.BlockSpec((WIDTH, D_MODEL), lambda i: (0, 0)), hm, pl.BlockSpec((1, 128), lambda i: (0, 0)), seg],
        out_specs=[seg, hm, pl.BlockSpec((8, 128), lambda i: (0, 0))],
        out_shape=[jax.ShapeDtypeStruct((t, PACKED_WIDTH), MXU_DTYPE), jax.ShapeDtypeStruct((HEADS // 2, t, 128), F32),
                   jax.ShapeDtypeStruct((8, 128), F32)],
        input_output_aliases={0: 0},
        compiler_params=_params(("arbitrary",)),
    )(dproj, dy, w_up, o_hm, jnp.tile(gdn_norm_w, (1, 2)), proj)
    return dz, do, dw[:, :HEAD_DIM] + dw[:, HEAD_DIM:]


def _up_merge(oz_a, oz_b, w_up_a, w_up_b, proj, tm=512):
    t = proj.shape[0]

    def body(oa_ref, ob_ref, wa_ref, wb_ref, ga0, ga1, gb0, gb1, ya_ref, yb_ref, m_ref):
        ya = jnp.dot(oa_ref[...], wa_ref[...], preferred_element_type=F32)
        yb = jnp.dot(ob_ref[...], wb_ref[...], preferred_element_type=F32)
        ya_ref[...] = ya
        yb_ref[...] = yb
        for c, (ga, gb) in enumerate(((ga0, gb0), (ga1, gb1))):
            cols = slice(c * WIDTH, (c + 1) * WIDTH)
            m_ref[:, cols] = (_sigmoid(ga[...]) * ya[:, cols] + _sigmoid(gb[...]) * yb[:, cols]).astype(m_ref.dtype)

    def gate(seg, c):
        return pl.BlockSpec((tm, WIDTH), lambda i: (i, seg // WIDTH + c))

    tok_in = pl.BlockSpec((tm, WIDTH), lambda i: (i, 0))
    weight = pl.BlockSpec((WIDTH, D_MODEL), lambda i: (0, 0))
    tok = pl.BlockSpec((tm, D_MODEL), lambda i: (i, 0))
    return pl.pallas_call(
        body, name="up_merge", grid=(t // tm,),
        in_specs=[tok_in, tok_in, weight, weight, gate(SEG_GA, 0), gate(SEG_GA, 1), gate(SEG_GB, 0), gate(SEG_GB, 1)],
        out_specs=[tok, tok, tok],
        out_shape=[jax.ShapeDtypeStruct((t, D_MODEL), F32), jax.ShapeDtypeStruct((t, D_MODEL), F32),
                   jax.ShapeDtypeStruct((t, D_MODEL), MXU_DTYPE)],
        compiler_params=_params(("parallel",)),
    )(oz_a, oz_b, w_up_a, w_up_b, proj, proj, proj, proj)


def _merge_bwd(dproj, d_m, y, proj, seg, name, tm=1024):
    t = proj.shape[0]

    def body(*refs):
        dm_ref, y_ref, g_ref, dg_ref, dy_ref = refs[-5:]
        dm = dm_ref[...]
        s = _sigmoid(g_ref[...])
        dy_ref[...] = (dm * s).astype(dy_ref.dtype)
        dg_ref[...] = (dm * y_ref[...] * s * (1.0 - s)).astype(dg_ref.dtype)

    half = pl.BlockSpec((tm, WIDTH), lambda i, c: (i, c))
    gate = pl.BlockSpec((tm, WIDTH), lambda i, c: (i, seg // WIDTH + c))
    specs, args, aliases = [half, half, gate], [d_m, y, proj], {}
    if dproj is not None:
        specs, args, aliases = [pl.BlockSpec(memory_space=pl.ANY)] + specs, [dproj] + args, {0: 0}
    return pl.pallas_call(
        body, name=name, grid=(t // tm, 2), in_specs=specs, out_specs=[gate, half],
        out_shape=[jax.ShapeDtypeStruct((t, PACKED_WIDTH), MXU_DTYPE), jax.ShapeDtypeStruct((t, D_MODEL), MXU_DTYPE)],
        input_output_aliases=aliases,
        compiler_params=_params(("parallel", "parallel")),
    )(*args)


def _out_tail(merged, w_out, x, final_w, target, tm=1024):
    t = x.shape[0]
    tm = min(tm, t)

    def body(m_ref, wo_ref, x_ref, w_ref, t_ref, dxm_ref, dx_ref, loss_ref, dw_ref):
        x2 = x_ref[...] + jnp.dot(m_ref[...], wo_ref[...], preferred_element_type=F32)
        w = w_ref[...]
        r = lax.rsqrt(jnp.mean(x2 * x2, axis=-1, keepdims=True) + NORM_EPS)
        xn = x2 * r
        err = xn * w - t_ref[...]
        dy = err * (1.0 / D_MODEL)
        dyw = dy * w
        dx2 = r * dyw - x2 * (r * r * r) * jnp.mean(dyw * x2, axis=-1, keepdims=True)
        dx_ref[...] = dx2
        dxm_ref[...] = dx2.astype(dxm_ref.dtype)
        loss = 0.5 * jnp.sum(jnp.sum(err * err, axis=-1, keepdims=True) * (1.0 / D_MODEL), axis=0, keepdims=True)
        onehot = jnp.where((_iota((8, 128), 0) == 0) & (_iota((8, 128), 1) == 0), 1.0, 0.0)

        @pl.when(pl.program_id(0) == 0)
        def _():
            loss_ref[...] = jnp.zeros_like(loss_ref)
            dw_ref[...] = jnp.zeros_like(dw_ref)

        loss_ref[...] += loss * onehot
        dw_ref[...] += jnp.where(_iota((8, D_MODEL), 0) == 0, jnp.sum(dy * xn, axis=0, keepdims=True), 0.0)

    tok = pl.BlockSpec((tm, D_MODEL), lambda i: (i, 0))
    full = pl.BlockSpec((D_MODEL, D_MODEL), lambda i: (0, 0))
    return pl.pallas_call(
        body, name="out_tail", grid=(t // tm,),
        in_specs=[tok, full, tok, pl.BlockSpec((1, D_MODEL), lambda i: (0, 0)), tok],
        out_specs=[tok, tok, pl.BlockSpec((8, 128), lambda i: (0, 0)), pl.BlockSpec((8, D_MODEL), lambda i: (0, 0))],
        out_shape=[jax.ShapeDtypeStruct((t, D_MODEL), MXU_DTYPE), jax.ShapeDtypeStruct((t, D_MODEL), F32),
                   jax.ShapeDtypeStruct((8, 128), F32), jax.ShapeDtypeStruct((8, D_MODEL), F32)],
        compiler_params=_params(("arbitrary",)),
    )(merged, w_out, x, final_w, target)


def _dh_norm_bwd(dproj, wp, x, norm_w, dx2, after, blocks, name, carry=None, tm=1024, tk=PACKED_WIDTH // 4):
    t = x.shape[0]
    nk = PACKED_WIDTH // tk
    first, end = blocks

    def body(*refs):
        dp_ref, wp_ref, x_ref, w_ref, dx2_ref = refs[:5]
        dx_ref, dw_ref, acc = refs[-3:]
        kk = pl.program_id(1)
        part = _mxu(dp_ref[...], wp_ref[...], _NT)

        @pl.when(kk == 0)
        def _():
            acc[...] = part

        @pl.when(kk > 0)
        def _():
            acc[...] += part

        @pl.when((kk == 0) & (pl.program_id(0) == 0))
        def _():
            dw_ref[...] = jnp.zeros_like(dw_ref) if carry is None else refs[-4][...]

        @pl.when(kk == nk - 1)
        def _():
            xf, w, dh_ = x_ref[...], w_ref[...], acc[...]
            r = lax.rsqrt(jnp.mean(xf * xf, axis=-1, keepdims=True) + NORM_EPS)
            dhw = dh_ * w
            dx_ref[...] = dx2_ref[...] + r * dhw - xf * (r * r * r) * jnp.mean(dhw * xf, axis=-1, keepdims=True)
            dw_ref[...] += jnp.where(_iota((8, D_MODEL), 0) == 0, jnp.sum(dh_ * xf * r, axis=0, keepdims=True), 0.0)

    tok = pl.BlockSpec((tm, D_MODEL), lambda i, kk: (first + i, 0))
    small = pl.BlockSpec((8, D_MODEL), lambda i, kk: (0, 0))
    specs = [pl.BlockSpec((tm, tk), lambda i, kk: (first + i, kk)), pl.BlockSpec((D_MODEL, tk), lambda i, kk: (0, kk)),
             tok, pl.BlockSpec((1, D_MODEL), lambda i, kk: (0, 0)), tok, pl.BlockSpec(memory_space=pl.ANY)]
    args = [dproj, wp, x, norm_w, dx2, after]
    aliases = {}
    if carry is not None:
        specs += [pl.BlockSpec(memory_space=pl.ANY), small]
        args += list(carry)
        aliases = {len(args) - 2: 0}
    return pl.pallas_call(
        body, name=name, grid=(end - first, nk), in_specs=specs, out_specs=[tok, small],
        out_shape=[jax.ShapeDtypeStruct((t, D_MODEL), F32), jax.ShapeDtypeStruct((8, D_MODEL), F32)],
        scratch_shapes=[pltpu.VMEM((tm, D_MODEL), F32)],
        input_output_aliases=aliases,
        compiler_params=_params(("arbitrary", "arbitrary")),
    )(*args)


def _local_step(x, target, norm_w, wp, conv_w, a_log, dt_bias, gdn_norm_w, w_up_a, w_up_b, w_out, final_w,
                start_reduce, continue_reduce):
    t = x.shape[0]
    tables = _rope_tables(t)
    a_log = jnp.pad(a_log, ((0, 0), (HEADS, 128 - 2 * HEADS)))
    dt_bias = jnp.pad(dt_bias, ((0, 0), (HEADS, 128 - 2 * HEADS)))

    proj, h_t = _norm_proj(x, norm_w, wp)
    qkvs = _rope_fwd(proj, tables)
    outs, lses = zip(*[_att_fwd(qkvs[gi], d, f"att_fwd{gi}") for gi, d in enumerate(DILATIONS)])
    oz_a, o_a, *lse_views = _att_merge(outs, lses, proj)
    conv = _conv_fwd(proj, conv_w)
    gq, gk, gv, gb, gg, ggl, grow = _gdn_prep(conv, proj, a_log, dt_bias)
    o_b, states, invs = _gdn_fwd(gq, gk, gv, gb, gg, ggl, grow)
    oz_b = _gdn_out(o_b, gdn_norm_w, proj)
    big = dict(tm=1024, tn=1024, tk=1024)
    y_a, y_b, merged = _up_merge(oz_a, oz_b, w_up_a, w_up_b, proj)
    dx2_m, dx2, loss_blk, d_final = _out_tail(merged, w_out, x, final_w, target)

    d_wout = _matmul(merged, dx2_m, "tn", "d_w_out", **big)
    d_m = _matmul(dx2_m, w_out, "nt", "d_merged", **big)
    dproj, dy_a = _merge_bwd(None, d_m, y_a, proj, SEG_GA, "merge_bwd_a")
    dproj, dy_b = _merge_bwd(dproj, d_m, y_b, proj, SEG_GB, "merge_bwd_b")
    d_wua = _matmul(oz_a, dy_a, "tn", "d_w_up_a", **big)
    d_wub = _matmul(oz_b, dy_b, "tn", "d_w_up_b", **big)
    dproj, *views = _att_merge_bwd(dproj, dy_a, w_up_a, o_a, proj)
    do_views, delta_views = views[:3], views[3:]
    dqs, dkvs = zip(*[_att_bwd(qkvs[gi], do_views[gi], lse_views[gi], delta_views[gi], d, f"att_bwd{gi}")
                      for gi, d in enumerate(DILATIONS)])
    dproj = _rope_bwd(dproj, dqs, dkvs, tables)
    dproj, do_b, d_gnw = _gdn_out_bwd(dproj, dy_b, w_up_b, o_b, gdn_norm_w, proj)
    dgq, dgk, dgv, dgb, dgg, dggl, dgrow = _gdn_bwd(gq, gk, gv, gb, gg, ggl, grow, states, invs, do_b)
    dproj, dconv, d_small = _gdn_prep_bwd(dproj, conv, proj, a_log, dt_bias, dgq, dgk, dgv, dgb, dgg, dggl, dgrow)
    dproj, d_convw = _conv_bwd(dproj, dconv, proj, conv_w)
    d_wp = _matmul(h_t, dproj, "nn", "d_w_in", tm=1024, tn=PACKED_WIDTH // 4, tk=1024)
    in_flight, token = start_reduce(d_wp, d_wua, d_wub, d_wout, d_convw[0:GDN_CONV])
    nblk = t // 1024
    cut = max(1, nblk // 4)
    part = _dh_norm_bwd(dproj, wp, x, norm_w, dx2, token, (0, cut), "dh_norm_bwd_a")
    in_flight, token = continue_reduce(in_flight, part[0])
    grad_x, d_norm = _dh_norm_bwd(dproj, wp, x, norm_w, dx2, token, (cut, nblk), "dh_norm_bwd_b", carry=part)
    return dict(loss=loss_blk, grad_x=grad_x, norm_w=d_norm[0:1], in_flight=in_flight,
                a_log=d_small[0:1, HEADS:2 * HEADS], dt_bias=d_small[1:2, HEADS:2 * HEADS], gdn_norm_w=d_gnw[0:1],
                final_norm_w=d_final[0:1])


SHARDS = 4
W_IN_SHARD = IN_WIDTH // SHARDS
ROWS_UP = WIDTH * (D_MODEL // SHARDS) // 128
ROWS_OUT = (D_MODEL // SHARDS) * D_MODEL // 128
CONV_SHARD = 3 * WIDTH // SHARDS
ROWS_CONV = 16
SLAB_ROWS = 2 * ROWS_UP + ROWS_OUT + 2 * ROWS_CONV
HALF_ROWS = SLAB_ROWS // 2
BIG_HALF = (D_MODEL // 2, W_IN_SHARD)
SMALL_HALF = (HALF_ROWS, 128)
MESH = pl.DeviceIdType.MESH
ANY = pl.BlockSpec(memory_space=pl.ANY)


def _pad_rows(a, rows):
    return jnp.pad(a, ((0, rows - a.shape[0]), (0, 0)))


def _pack_slab(w_up_a, w_up_b, w_out, conv, conv_lo):
    parts = [w_up_a.reshape(ROWS_UP, 128), w_up_b.reshape(ROWS_UP, 128), w_out.reshape(ROWS_OUT, 128),
             _pad_rows(conv.reshape(-1, 128), ROWS_CONV), _pad_rows(conv_lo.reshape(-1, 128), ROWS_CONV)]
    return jnp.concatenate(parts, axis=0).reshape(2, *SMALL_HALF)


def _unpack_slab(slab):
    slab = slab.reshape(SLAB_ROWS, 128)
    r0 = 0
    out = []
    for rows, shape in ((ROWS_UP, (WIDTH, D_MODEL // SHARDS)), (ROWS_UP, (WIDTH, D_MODEL // SHARDS)),
                        (ROWS_OUT, (D_MODEL // SHARDS, D_MODEL)), (ROWS_CONV, None), (ROWS_CONV, None)):
        part = slab[r0:r0 + rows]
        out.append(part[:GDN_CONV * CONV_SHARD // 128].reshape(GDN_CONV, CONV_SHARD) if shape is None else part.reshape(shape))
        r0 += rows
    return out


def _mesh_position():
    x, y, c = lax.axis_index("x"), lax.axis_index("y"), lax.axis_index("c")
    return x, y, c, [(1 - x, y), (x, 1 - y), (1 - x, 1 - y)]


def _gather_weights(shards):
    n = len(shards)

    def body(*refs):
        in_refs, out_refs, (send_sems, recv_sems) = refs[:n], refs[n:2 * n], refs[2 * n:]
        x, y, c, chips = _mesh_position()

        def half(a, chip, which):
            return out_refs[a].at[2 * chip[0] + chip[1], which]

        def copy(k, src, dst, to):
            return pltpu.make_async_remote_copy(src_ref=src, dst_ref=dst, send_sem=send_sems.at[k], recv_sem=recv_sems.at[k],
                                                device_id=to, device_id_type=MESH)

        pairs = [(a, j, chip) for a in range(n) for j, chip in enumerate(chips)]
        first = [copy(6 * a + j, in_refs[a].at[c], half(a, (x, y), c), (*chip, c)) for a, j, chip in pairs]
        for cp in first:
            cp.start()
        passed = [copy(6 * a + 3 + j, half(a, chip, c), half(a, chip, c), (x, y, 1 - c)) for a, j, chip in pairs]
        for i, (a, j, chip) in enumerate(pairs):
            copy(6 * a + j, half(a, chip, c), half(a, chip, c), (x, y, c)).wait_recv()
            passed[i].start()
        for a, j, chip in pairs:
            copy(6 * a + 3 + j, half(a, chip, 1 - c), half(a, chip, 1 - c), (x, y, c)).wait_recv()
        for cp in first + passed:
            cp.wait_send()

    return pl.pallas_call(
        body, name="gather_weights", in_specs=[ANY] * n, out_specs=[ANY] * n,
        out_shape=[jax.ShapeDtypeStruct((SHARDS, *s.shape), s.dtype) for s in shards],
        scratch_shapes=[pltpu.SemaphoreType.DMA((6 * n,)), pltpu.SemaphoreType.DMA((6 * n,))],
    )(*shards)


def _pair_sum(grads, recv, blk, name):
    _, _, rows, cols = grads.shape

    def body(c_ref, g_ref, r_ref, o_ref):
        o_ref[...] = (g_ref[0] + r_ref[...]).astype(o_ref.dtype)

    spec = pl.BlockSpec((1, blk, cols), lambda s, i, c_ref: (s, i, 0))
    return pl.pallas_call(
        body, name=name,
        grid_spec=pltpu.PrefetchScalarGridSpec(
            num_scalar_prefetch=1, grid=(SHARDS, rows // blk),
            in_specs=[pl.BlockSpec((1, 1, blk, cols), lambda s, i, c_ref: (s, c_ref[0], i, 0)), spec],
            out_specs=spec),
        out_shape=jax.ShapeDtypeStruct((SHARDS, rows, cols), MXU_DTYPE),
        compiler_params=_params(("parallel", "parallel")),
    )(lax.axis_index("c").astype(jnp.int32).reshape(1), grads, recv)


_HBM = pl.BlockSpec(memory_space=pltpu.HBM)
_SEM = pl.BlockSpec(memory_space=pltpu.SEMAPHORE)
_DATAFLOW = pltpu.SideEffectType.DATAFLOW_SIDE_EFFECTING


def _scatter_copies(p_refs, l_refs, send_sems, recv_sems):
    x, y, c, chips = _mesh_position()
    return [pltpu.make_async_remote_copy(src_ref=p_refs[a].at[2 * chip[0] + chip[1]], dst_ref=l_refs[a].at[j],
                                         send_sem=send_sems.at[3 * a + j], recv_sem=recv_sems.at[3 * a + j],
                                         device_id=(*chip, c), device_id_type=MESH)
            for a in range(len(p_refs)) for j, chip in enumerate(chips)]


def _exchange_copies(g_refs, l_refs, send_sems, recv_sems):
    x, y, c, _ = _mesh_position()
    return [pltpu.make_async_remote_copy(src_ref=g_refs[a].at[s, 1 - c], dst_ref=l_refs[a].at[s],
                                         send_sem=send_sems.at[SHARDS * a + s], recv_sem=recv_sems.at[SHARDS * a + s],
                                         device_id=(x, y, 1 - c), device_id_type=MESH)
            for a in range(len(g_refs)) for s in range(SHARDS)]


def _copies_start(name, copies, count, sources, land_shapes):
    n = len(sources)
    lands = [lax.empty(shape, src.dtype) for shape, src in zip(land_shapes, sources)]

    def body(*refs):
        s_refs, l_refs, send_sems, recv_sems, token = refs[:n], refs[n:2 * n], refs[2 * n], refs[2 * n + 1], refs[-1]
        for cp in copies(s_refs, l_refs, send_sems, recv_sems):
            cp.start()
        token[...] = jnp.zeros_like(token)

    operands = [pltpu.with_memory_space_constraint(a, pltpu.HBM) for a in (*sources, *lands)]
    return pl.pallas_call(
        body, name=name, in_specs=[_HBM] * (2 * n),
        out_shape=(pltpu.SemaphoreType.DMA((count,)), pltpu.SemaphoreType.DMA((count,)),
                   *[pltpu.HBM(a.shape, a.dtype) for a in operands], jax.ShapeDtypeStruct((8, 128), F32)),
        out_specs=(_SEM, _SEM, *[_HBM] * (2 * n), pl.BlockSpec(memory_space=pltpu.VMEM)),
        input_output_aliases={i: 2 + i for i in range(2 * n)},
        compiler_params=pltpu.CompilerParams(has_side_effects=_DATAFLOW),
    )(*operands)


def _copies_wait(name, copies, send_sems, recv_sems, passed, after):
    n = len(passed) // 2

    def body(*refs):
        s_refs, l_refs, send_s, recv_s = refs[:n], refs[n:2 * n], refs[2 * n], refs[2 * n + 1]
        for cp in copies(s_refs, l_refs, send_s, recv_s):
            cp.wait_send()
            cp.wait_recv()

    return pl.pallas_call(
        body, name=name, in_specs=[_HBM] * (2 * n) + [_SEM, _SEM, ANY],
        out_shape=[pltpu.HBM(a.shape, a.dtype) for a in passed], out_specs=[_HBM] * (2 * n),
        input_output_aliases={i: i for i in range(2 * n)},
        compiler_params=pltpu.CompilerParams(has_side_effects=_DATAFLOW),
    )(*passed, send_sems, recv_sems, after)


def _chip_sum(pairs, recv, blk, name):
    _, rows, cols = pairs.shape

    def body(pos_ref, p_ref, r_ref, o_ref):
        o_ref[0] = ((p_ref[0].astype(F32) + r_ref[0].astype(F32)) + r_ref[1].astype(F32)) + r_ref[2].astype(F32)

    pos = jnp.stack([2 * lax.axis_index("x") + lax.axis_index("y"), lax.axis_index("c")]).astype(jnp.int32)
    return pl.pallas_call(
        body, name=name,
        grid_spec=pltpu.PrefetchScalarGridSpec(
            num_scalar_prefetch=1, grid=(rows // blk,),
            in_specs=[pl.BlockSpec((1, blk, cols), lambda i, pos_ref: (pos_ref[0], i, 0)),
                      pl.BlockSpec((3, blk, cols), lambda i, pos_ref: (0, i, 0))],
            out_specs=pl.BlockSpec((1, blk, cols), lambda i, pos_ref: (pos_ref[1], i, 0))),
        out_shape=jax.ShapeDtypeStruct((2, rows, cols), F32),
        compiler_params=_params(("parallel",)),
    )(pos, pairs, recv)


def _share_total(totals):
    n = len(totals)

    def body(*refs):
        t_refs, out_refs, (send_sems, recv_sems) = refs[:n], refs[n:2 * n], refs[2 * n:]
        x, y, c, _ = _mesh_position()
        copies = [pltpu.make_async_remote_copy(src_ref=t_refs[a].at[c], dst_ref=out_refs[a].at[c], send_sem=send_sems.at[a],
                                               recv_sem=recv_sems.at[a], device_id=(x, y, 1 - c), device_id_type=MESH)
                  for a in range(n)]
        for cp in copies:
            cp.start()
        for a in range(n):
            other = out_refs[a].at[1 - c]
            pltpu.make_async_remote_copy(src_ref=other, dst_ref=other, send_sem=send_sems.at[a], recv_sem=recv_sems.at[a],
                                         device_id=(x, y, c), device_id_type=MESH).wait_recv()
        for cp in copies:
            cp.wait_send()

    return pl.pallas_call(
        body, name="share_total", in_specs=[ANY] * n, out_specs=[ANY] * n,
        out_shape=[jax.ShapeDtypeStruct(t.shape, F32) for t in totals],
        scratch_shapes=[pltpu.SemaphoreType.DMA((n,)), pltpu.SemaphoreType.DMA((n,))],
        input_output_aliases={a: a for a in range(n)},
    )(*totals)


def _allreduce_small(block):
    def body(b_ref, out_ref, gath, send_sems, recv_sems):
        x, y, c, _ = _mesh_position()
        me = 4 * x + 2 * y + c
        gath[me] = b_ref[...]
        copies = []
        for k in range(1, 8):
            peer = (x ^ (k >> 2), y ^ ((k >> 1) & 1), c ^ (k & 1))
            copies.append(pltpu.make_async_remote_copy(src_ref=b_ref, dst_ref=gath.at[me], send_sem=send_sems.at[k - 1],
                                                       recv_sem=recv_sems.at[k - 1], device_id=peer, device_id_type=MESH))
        for cp in copies:
            cp.start()
        for k in range(1, 8):
            src = 4 * (x ^ (k >> 2)) + 2 * (y ^ ((k >> 1) & 1)) + (c ^ (k & 1))
            pltpu.make_async_remote_copy(src_ref=b_ref, dst_ref=gath.at[src], send_sem=send_sems.at[k - 1],
                                         recv_sem=recv_sems.at[k - 1], device_id=(x, y, c), device_id_type=MESH).wait_recv()
        for cp in copies:
            cp.wait_send()
        acc = gath[0]
        for d in range(1, 8):
            acc = acc + gath[d]
        out_ref[...] = acc

    vm = pl.BlockSpec(memory_space=pltpu.VMEM)
    return pl.pallas_call(
        body, name="allreduce_small", in_specs=[vm], out_specs=vm,
        out_shape=jax.ShapeDtypeStruct((8, D_MODEL), F32),
        scratch_shapes=[pltpu.VMEM((8, 8, D_MODEL), F32), pltpu.SemaphoreType.DMA((7,)), pltpu.SemaphoreType.DMA((7,))],
    )(block)


def _adamw(w, g, m, v, name):
    rows, cols = w.shape
    tr = 128 if rows % 128 == 0 else rows

    def body(w_ref, g_ref, m_ref, v_ref, d_ref, nm_ref, nv_ref):
        gv = g_ref[...]
        nm = ADAM_B1 * m_ref[...] + (1.0 - ADAM_B1) * gv
        nv = ADAM_B2 * v_ref[...] + (1.0 - ADAM_B2) * (gv * gv)
        m_hat = nm / (1.0 - ADAM_B1 ** ADAM_STEP)
        v_hat = nv / (1.0 - ADAM_B2 ** ADAM_STEP)
        d_ref[...] = -ADAM_LR * (m_hat / (jnp.sqrt(v_hat) + ADAM_EPS) + ADAM_WD * w_ref[...])
        nm_ref[...] = nm
        nv_ref[...] = nv

    spec = pl.BlockSpec((tr, cols), lambda i: (i, 0))
    shape = jax.ShapeDtypeStruct((rows, cols), F32)
    return pl.pallas_call(
        body, name=name, grid=(rows // tr,), in_specs=[spec] * 4, out_specs=[spec] * 3, out_shape=[shape] * 3,
        compiler_params=_params(("parallel",)),
    )(w, g, m, v)


def kernel(x, norm_w, w_in, conv_w, a_log, dt_bias, gdn_norm_w, w_up_a, w_up_b, w_out, final_norm_w, loss_target, m_norm_w, m_w_in, m_conv_w, m_a_log, m_dt_bias, m_gdn_norm_w, m_w_up_a, m_w_up_b, m_w_out, m_final_norm_w, v_norm_w, v_w_in, v_conv_w, v_a_log, v_dt_bias, v_gdn_norm_w, v_w_up_a, v_w_up_b, v_w_out, v_final_norm_w):
    conv_hi = conv_w[0].astype(MXU_DTYPE)
    conv_lo = (conv_w[0] - conv_hi.astype(F32)).astype(MXU_DTYPE)
    big = w_in[0].astype(MXU_DTYPE).reshape(2, *BIG_HALF)
    slab = _pack_slab(w_up_a[0].astype(MXU_DTYPE), w_up_b[0].astype(MXU_DTYPE), w_out[0].astype(MXU_DTYPE), conv_hi, conv_lo)
    own_shard = 2 * lax.axis_index("x") + lax.axis_index("y")
    bigs, slabs = _gather_weights([big, slab])
    bigs = lax.dynamic_update_slice(bigs, big[None], (own_shard, 0, 0, 0)).reshape(SHARDS, D_MODEL, W_IN_SHARD)
    slabs = lax.dynamic_update_slice(slabs, slab[None], (own_shard, 0, 0, 0))
    parts = [_unpack_slab(slabs[s]) for s in range(SHARDS)]
    split = BA_END - (SHARDS - 1) * W_IN_SHARD
    wp = jnp.concatenate([bigs[s] for s in range(SHARDS - 1)]
                         + [bigs[-1][:, :split], jnp.zeros((D_MODEL, SEG_GA - BA_END), MXU_DTYPE), bigs[-1][:, split:]], axis=1)
    w_up_a_full = jnp.concatenate([p[0] for p in parts], axis=1)
    w_up_b_full = jnp.concatenate([p[1] for p in parts], axis=1)
    w_out_full = jnp.concatenate([p[2] for p in parts], axis=0)
    conv_full = jnp.concatenate([p[3].astype(F32) + p[4].astype(F32) for p in parts], axis=1)

    blocks, tags = (128, HALF_ROWS), ("w_in", "slab")

    def start_reduce(d_wp, d_w_up_a, d_w_up_b, d_w_out, d_conv_w):
        d_w_in = [d_wp[:, s * W_IN_SHARD:(s + 1) * W_IN_SHARD] for s in range(SHARDS - 1)]
        d_w_in.append(jnp.concatenate([d_wp[:, (SHARDS - 1) * W_IN_SHARD:BA_END], d_wp[:, SEG_GA:]], axis=1))
        zero_conv = jnp.zeros((GDN_CONV, CONV_SHARD), F32)
        grads = [jnp.stack(d_w_in).reshape(SHARDS, 2, *BIG_HALF),
                 jnp.stack([_pack_slab(d_w_up_a[:, s * 256:(s + 1) * 256], d_w_up_b[:, s * 256:(s + 1) * 256],
                                       d_w_out[s * 256:(s + 1) * 256], d_conv_w[:, s * CONV_SHARD:(s + 1) * CONV_SHARD],
                                       zero_conv) for s in range(SHARDS)])]
        *in_flight, token = _copies_start("exchange_start", _exchange_copies, 2 * SHARDS, grads,
                                          [(SHARDS, *gr.shape[2:]) for gr in grads])
        return in_flight, token

    def continue_reduce(in_flight, after):
        send_sems, recv_sems, *passed = in_flight
        arrived = _copies_wait("exchange_wait", _exchange_copies, send_sems, recv_sems, passed, after)
        grads, from_sibling = arrived[:2], arrived[2:]
        pairs = [_pair_sum(gr, fs, blk, f"pair_sum_{tag}") for gr, fs, blk, tag in zip(grads, from_sibling, blocks, tags)]
        *in_flight, token = _copies_start("scatter_start", _scatter_copies, 2 * 3, pairs, [(3, *p.shape[1:]) for p in pairs])
        return in_flight, token

    g = _local_step(x[0], loss_target[0], norm_w, wp, conv_full, a_log, dt_bias, gdn_norm_w,
                    w_up_a_full, w_up_b_full, w_out_full, final_norm_w[None], start_reduce, continue_reduce)

    send_sems, recv_sems, *passed = g["in_flight"]
    arrived = _copies_wait("scatter_wait", _scatter_copies, send_sems, recv_sems, passed, g["grad_x"])
    pairs, from_chips = arrived[:2], arrived[2:]
    total_big, total_slab = _share_total([_chip_sum(p, fc, blk, f"chip_sum_{tag}")
                                          for p, fc, blk, tag in zip(pairs, from_chips, blocks, tags)])
    g_w_in = total_big.reshape(D_MODEL, W_IN_SHARD)
    g_w_up_a, g_w_up_b, g_w_out, g_conv, _ = _unpack_slab(total_slab)

    row2 = jnp.concatenate([g["gdn_norm_w"], g["a_log"], g["dt_bias"], g["loss"][0:1, 0:1],
                            jnp.zeros((1, D_MODEL - HEAD_DIM - 2 * HEADS - 1), F32)], axis=1)
    small = _allreduce_small(jnp.concatenate([g["norm_w"], g["final_norm_w"], row2, jnp.zeros((5, D_MODEL), F32)], axis=0))
    g_norm, g_final = small[0:1], small[1]
    g_gnw, g_alog, g_dt = small[2:3, 0:HEAD_DIM], small[2:3, HEAD_DIM:HEAD_DIM + HEADS], small[2:3, HEAD_DIM + HEADS:HEAD_DIM + 2 * HEADS]
    loss = small[2, HEAD_DIM + 2 * HEADS]

    names = ["norm_w", "w_in", "conv_w", "a_log", "dt_bias", "gdn_norm_w", "w_up_a", "w_up_b", "w_out", "final_norm_w"]
    weights = dict(zip(names, (norm_w, w_in, conv_w, a_log, dt_bias, gdn_norm_w, w_up_a, w_up_b, w_out, final_norm_w)))
    ms = dict(zip(names, (m_norm_w, m_w_in, m_conv_w, m_a_log, m_dt_bias, m_gdn_norm_w, m_w_up_a, m_w_up_b, m_w_out, m_final_norm_w)))
    vs = dict(zip(names, (v_norm_w, v_w_in, v_conv_w, v_a_log, v_dt_bias, v_gdn_norm_w, v_w_up_a, v_w_up_b, v_w_out, v_final_norm_w)))
    grads2d = dict(norm_w=g_norm, w_in=g_w_in, conv_w=g_conv, a_log=g_alog, dt_bias=g_dt, gdn_norm_w=g_gnw,
                   w_up_a=g_w_up_a, w_up_b=g_w_up_b, w_out=g_w_out, final_norm_w=g_final[None])
    grad_out, delta, new_m, new_v = [], [], [], []
    for n in names:
        shape = weights[n].shape
        two_d = grads2d[n].shape
        d, nm, nv = _adamw(weights[n].reshape(two_d), grads2d[n], ms[n].reshape(two_d), vs[n].reshape(two_d), f"adamw_{n}")
        grad_out.append(grads2d[n].reshape(shape))
        delta.append(d.reshape(shape))
        new_m.append(nm.reshape(shape))
        new_v.append(nv.reshape(shape))
    return (loss, g["grad_x"][None], *grad_out, *delta, *new_m, *new_v)
```

```python
import functools

import jax
import jax.numpy as jnp
from jax import lax
from jax.experimental import pallas as pl
from jax.experimental.pallas import tpu as pltpu

F32 = jnp.float32
MXU_DTYPE = jnp.bfloat16
HIGHEST = lax.Precision.HIGHEST

D_MODEL = 1024
HEADS = 8
HEAD_DIM = 64
WIDTH = HEADS * HEAD_DIM
NORM_EPS = 1e-6
ROPE_THETA = 10000.0
ATT_BLOCK = 128
DILATIONS = (1, 4, 16)
GDN_CHUNK = 64
GDN_CONV = 4
IN_WIDTH = 9232
SEG_A, SEG_ZA, SEG_B, SEG_ZB, SEG_BA, SEG_GA, SEG_GB, PACKED_WIDTH = 0, 4608, 5120, 6656, 7168, 7680, 8704, 9728
BA_END = 7184
VMEM_LIMIT = 56 * 1024 * 1024

ADAM_LR, ADAM_B1, ADAM_B2, ADAM_EPS, ADAM_WD, ADAM_STEP = 0.001, 0.9, 0.999, 1e-08, 0.01, 10

_NN = (((1,), (0,)), ((), ()))
_NT = (((1,), (1,)), ((), ()))
_TN = (((0,), (0,)), ((), ()))


def _params(sem):
    return pltpu.CompilerParams(dimension_semantics=sem, vmem_limit_bytes=VMEM_LIMIT)


def _mxu(a, b, dims):
    return lax.dot_general(a.astype(MXU_DTYPE), b.astype(MXU_DTYPE), dims, preferred_element_type=F32)


def _sigmoid(x):
    return 1.0 / (1.0 + jnp.exp(-x))


def _softplus(x):
    return jnp.maximum(x, 0.0) + jnp.log(1.0 + jnp.exp(-jnp.abs(x)))


def _iota(shape, axis):
    return lax.broadcasted_iota(jnp.int32, shape, axis)


def _matmul(a, b, mode, name, out_dtype=F32, tm=512, tn=512, tk=512):
    if mode == "nn":
        (m, k), (k2, n) = a.shape, b.shape
    elif mode == "nt":
        (m, k), (n, k2) = a.shape, b.shape
    else:
        (k, m), (k2, n) = a.shape, b.shape
    assert k == k2
    tm, tn, tk = min(tm, m), min(tn, n), min(tk, k)
    assert m % tm == 0 and n % tn == 0 and k % tk == 0
    nk = k // tk
    dims = {"nn": _NN, "nt": _NT, "tn": _TN}[mode]

    assert out_dtype == F32

    def body(a_ref, b_ref, o_ref):
        kk = pl.program_id(2)
        part = _mxu(a_ref[...], b_ref[...], dims)

        @pl.when(kk == 0)
        def _():
            o_ref[...] = part

        @pl.when(kk > 0)
        def _():
            o_ref[...] += part

    a_spec = pl.BlockSpec((tk, tm), lambda i, j, kk: (kk, i)) if mode == "tn" else pl.BlockSpec((tm, tk), lambda i, j, kk: (i, kk))
    b_spec = pl.BlockSpec((tn, tk), lambda i, j, kk: (j, kk)) if mode == "nt" else pl.BlockSpec((tk, tn), lambda i, j, kk: (kk, j))
    return pl.pallas_call(
        body, name=name, grid=(m // tm, n // tn, nk), in_specs=[a_spec, b_spec],
        out_specs=pl.BlockSpec((tm, tn), lambda i, j, kk: (i, j)),
        out_shape=jax.ShapeDtypeStruct((m, n), out_dtype),
        compiler_params=_params(("parallel", "parallel", "arbitrary")),
    )(a, b)


def _norm_proj(x, norm_w, wp, tm=1024, tn=PACKED_WIDTH // 4):
    t = x.shape[0]
    tm = min(tm, t)

    def body(x_ref, nw_ref, w_ref, proj_ref, ht_ref, h_scr):
        @pl.when(pl.program_id(1) == 0)
        def _():
            xf = x_ref[...]
            r = lax.rsqrt(jnp.mean(xf * xf, axis=-1, keepdims=True) + NORM_EPS)
            h = xf * r * nw_ref[...]
            h_scr[...] = h.astype(h_scr.dtype)
            ht_ref[...] = h.T.astype(ht_ref.dtype)

        proj_ref[...] = jnp.dot(h_scr[...], w_ref[...], preferred_element_type=F32)

    return pl.pallas_call(
        body, name="norm_proj", grid=(t // tm, PACKED_WIDTH // tn),
        in_specs=[pl.BlockSpec((tm, D_MODEL), lambda i, j: (i, 0)),
                  pl.BlockSpec((1, D_MODEL), lambda i, j: (0, 0)),
                  pl.BlockSpec((D_MODEL, tn), lambda i, j: (0, j))],
        out_specs=[pl.BlockSpec((tm, tn), lambda i, j: (i, j)),
                   pl.BlockSpec((D_MODEL, tm), lambda i, j: (0, i))],
        out_shape=[jax.ShapeDtypeStruct((t, PACKED_WIDTH), F32), jax.ShapeDtypeStruct((D_MODEL, t), MXU_DTYPE)],
        scratch_shapes=[pltpu.VMEM((tm, D_MODEL), MXU_DTYPE)],
        compiler_params=_params(("parallel", "arbitrary")),
    )(x, norm_w, wp)


def _rope_tables(t):
    lane = jnp.arange(128)
    inv_freq = ROPE_THETA ** (-jnp.arange(0, HEAD_DIM, 2, dtype=F32) / HEAD_DIM)
    freq = jnp.concatenate([inv_freq] * 4)
    coarse = (jnp.arange(t // 128, dtype=F32) * 128.0)[:, None] * freq[None, :]
    fine = jnp.arange(128, dtype=F32)[:, None] * freq[None, :]
    ca, sa, cb, sb = jnp.cos(coarse)[:, None, :], jnp.sin(coarse)[:, None, :], jnp.cos(fine)[None], jnp.sin(fine)[None]
    cos = (ca * cb - sa * sb).reshape(t, 128)
    sin = (sa * cb + ca * sb).reshape(t, 128)
    first_half = (lane % HEAD_DIM) < HEAD_DIM // 2
    return cos, jnp.where(first_half, -sin, 0.0), jnp.where(first_half, 0.0, sin)


def _rope_cols(x, cos, sin_lo, sin_hi, sign):
    outs = []
    for c in range(x.shape[1] // 128):
        xc = x[:, c * 128:(c + 1) * 128]
        rot = pltpu.roll(xc, 96, 1) * sin_lo + pltpu.roll(xc, 32, 1) * sin_hi
        outs.append(xc * cos + sign * rot)
    return jnp.concatenate(outs, axis=1)


def _rope_block(x, cos, sin_lo, sin_hi, sign, q_scale=None):
    q = _rope_cols(x[:, :WIDTH], cos, sin_lo, sin_hi, sign)
    k = _rope_cols(x[:, WIDTH:2 * WIDTH], cos, sin_lo, sin_hi, sign)
    return jnp.concatenate([q if q_scale is None else q * q_scale, k, x[:, 2 * WIDTH:]], axis=1)


def _tile_scratch(tm, cols):
    return pltpu.VMEM((cols // 128, tm, 128), F32)


def _store_tile(scr, y):
    for c in range(scr.shape[0]):
        scr[c] = y[:, c * 128:(c + 1) * 128]


def _load_tile(scr):
    return jnp.concatenate([scr[c] for c in range(scr.shape[0])], axis=1)


def _to_strided_view(scr, o_ref, d, width=None, col0=0):
    n, tm, _ = scr.shape
    width = n * 128 if width is None else width
    for r in range(d):
        for c in range(n):
            at = r * width + col0 + c * 128
            o_ref[:, at:at + 128] = scr[c, pl.ds(r, tm // d, stride=d), :].astype(o_ref.dtype)


def _from_strided_view(i_ref, scr, d):
    n, tm, _ = scr.shape
    for r in range(d):
        for c in range(n):
            scr[c, pl.ds(r, tm // d, stride=d), :] = i_ref[:, (r * n + c) * 128:(r * n + c + 1) * 128].astype(F32)


def _strided_spec(tm, d, cols):
    return pl.BlockSpec((tm // d, d * cols), lambda i: (i, 0))


def _rope_fwd(proj, tables, tm=512):
    t = proj.shape[0]
    cols = 3 * WIDTH

    def body(x_ref, c_ref, sl_ref, sh_ref, o0, o1, o2, scr):
        for g, (d, o_ref) in enumerate(zip(DILATIONS, (o0, o1, o2))):
            y = _rope_block(x_ref[:, g * cols:(g + 1) * cols], c_ref[...], sl_ref[...], sh_ref[...], 1.0, HEAD_DIM ** -0.5)
            if d == 1:
                o_ref[...] = y.astype(o_ref.dtype)
            else:
                _store_tile(scr, y)
                _to_strided_view(scr, o_ref, d)

    tab = pl.BlockSpec((tm, 128), lambda i: (i, 0))
    return pl.pallas_call(
        body, name="rope_fwd", grid=(t // tm,),
        in_specs=[pl.BlockSpec((tm, 3 * cols), lambda i: (i, 0)), tab, tab, tab],
        out_specs=[_strided_spec(tm, d, cols) for d in DILATIONS],
        out_shape=[jax.ShapeDtypeStruct((t // d, d * cols), MXU_DTYPE) for d in DILATIONS],
        scratch_shapes=[_tile_scratch(tm, cols)],
        compiler_params=_params(("parallel",)),
    )(proj, *tables)


def _rope_bwd(dproj, dqs, dkvs, tables, tm=512):
    t = dproj.shape[0]
    cols = 3 * WIDTH

    def body(dp_ref, q0, q1, q2, kv0, kv1, kv2, c_ref, sl_ref, sh_ref, o_ref, scr_q, scr_kv):
        for g, (d, q_ref, kv_ref) in enumerate(zip(DILATIONS, (q0, q1, q2), (kv0, kv1, kv2))):
            if d == 1:
                x = jnp.concatenate([q_ref[...], kv_ref[...]], axis=1)
            else:
                _from_strided_view(q_ref, scr_q, d)
                _from_strided_view(kv_ref, scr_kv, d)
                x = jnp.concatenate([_load_tile(scr_q), _load_tile(scr_kv)], axis=1)
            y = _rope_block(x, c_ref[...], sl_ref[...], sh_ref[...], -1.0)
            o_ref[:, g * cols:(g + 1) * cols] = y.astype(o_ref.dtype)

    tab = pl.BlockSpec((tm, 128), lambda i: (i, 0))
    return pl.pallas_call(
        body, name="rope_bwd", grid=(t // tm,),
        in_specs=[pl.BlockSpec(memory_space=pl.ANY)] + [_strided_spec(tm, d, WIDTH) for d in DILATIONS]
        + [_strided_spec(tm, d, 2 * WIDTH) for d in DILATIONS] + [tab, tab, tab],
        out_specs=pl.BlockSpec((tm, 3 * cols), lambda i: (i, 0)),
        out_shape=jax.ShapeDtypeStruct((t, PACKED_WIDTH), MXU_DTYPE),
        scratch_shapes=[_tile_scratch(tm, WIDTH), _tile_scratch(tm, 2 * WIDTH)],
        input_output_aliases={0: 0},
        compiler_params=_params(("parallel",)),
    )(dproj, *dqs, *dkvs, *tables)


def _att_masks():
    qi = _iota((ATT_BLOCK, ATT_BLOCK), 0)
    kj = _iota((ATT_BLOCK, ATT_BLOCK), 1)
    return kj <= qi, kj >= qi


def _att_fwd(qkv, d, name):
    rows = qkv.shape[0]
    nb = rows // ATT_BLOCK

    def body(q_ref, kc_ref, kp_ref, vc_ref, vp_ref, o_ref, lse_ref):
        m_cur, m_prev = _att_masks()
        hs = range(HEADS)
        sls = [slice(h * HEAD_DIM, (h + 1) * HEAD_DIM) for h in hs]

        def attend(with_prev):
            qs = [q_ref[:, sl] for sl in sls]
            s_c = [jnp.where(m_cur, _mxu(qs[h], kc_ref[:, sls[h]], _NT), -jnp.inf) for h in hs]
            if with_prev:
                s_p = [jnp.where(m_prev, _mxu(qs[h], kp_ref[:, sls[h]], _NT), -jnp.inf) for h in hs]
                m = [jnp.max(jnp.maximum(s_c[h], s_p[h]), axis=1, keepdims=True) for h in hs]
                p_c = [jnp.exp(s_c[h] - m[h]) for h in hs]
                p_p = [jnp.exp(s_p[h] - m[h]) for h in hs]
                den = [jnp.sum(p_c[h] + p_p[h], axis=1, keepdims=True) for h in hs]
                o = [_mxu(p_c[h], vc_ref[:, sls[h]], _NN) + _mxu(p_p[h], vp_ref[:, sls[h]], _NN) for h in hs]
            else:
                m = [jnp.max(s_c[h], axis=1, keepdims=True) for h in hs]
                p_c = [jnp.exp(s_c[h] - m[h]) for h in hs]
                den = [jnp.sum(p_c[h], axis=1, keepdims=True) for h in hs]
                o = [_mxu(p_c[h], vc_ref[:, sls[h]], _NN) for h in hs]
            lane = _iota((1, 128), 1)
            lse = jnp.zeros((ATT_BLOCK, 128), F32)
            for h in hs:
                o_ref[:, sls[h]] = o[h] / den[h]
                lse = jnp.where(lane == h, m[h] + jnp.log(den[h]), lse)
            lse_ref[...] = lse

        @pl.when(pl.program_id(1) > 0)
        def _():
            attend(True)

        @pl.when(pl.program_id(1) == 0)
        def _():
            attend(False)

    def cur(c):
        return pl.BlockSpec((ATT_BLOCK, WIDTH), lambda r, i: (i, 3 * r + c))

    def prev(c):
        return pl.BlockSpec((ATT_BLOCK, WIDTH), lambda r, i: (jnp.maximum(i - 1, 0), 3 * r + c))

    return pl.pallas_call(
        body, name=name, grid=(d, nb), in_specs=[cur(0), cur(1), prev(1), cur(2), prev(2)],
        out_specs=[pl.BlockSpec((ATT_BLOCK, WIDTH), lambda r, i: (i, r)), pl.BlockSpec((ATT_BLOCK, 128), lambda r, i: (i, r))],
        out_shape=[jax.ShapeDtypeStruct((rows, d * WIDTH), F32), jax.ShapeDtypeStruct((rows, d * 128), F32)],
        compiler_params=_params(("parallel", "arbitrary")),
    )(qkv, qkv, qkv, qkv, qkv)


def _att_bwd(qkv, do, lse, delta, d, name):
    rows = qkv.shape[0]
    nb = rows // ATT_BLOCK
    scale = HEAD_DIM ** -0.5

    def body(q_ref, kc_ref, kp_ref, vc_ref, vp_ref, do_ref, l_ref, dl_ref, dq_ref, dkv_ref, own):
        i = pl.program_id(1)
        m_cur, m_prev = _att_masks()

        hs = range(HEADS)
        sls = [slice(h * HEAD_DIM, (h + 1) * HEAD_DIM) for h in hs]
        col = [slice(h, h + 1) for h in hs]

        def probs(k_r, mask):
            s = [_mxu(q_ref[:, sls[h]], k_r[:, sls[h]], _NT) for h in hs]
            return [jnp.where(mask, jnp.exp(s[h] - l_ref[:, col[h]]), 0.0) for h in hs]

        def dscores(p, v_r):
            dp = [_mxu(do_ref[:, sls[h]], v_r[:, sls[h]], _NT) for h in hs]
            return [(p[h] * (dp[h] - dl_ref[:, col[h]])).astype(MXU_DTYPE) for h in hs]

        def step(with_prev):
            p_c = probs(kc_ref, m_cur)
            ds_c = dscores(p_c, vc_ref)
            dq = [_mxu(ds_c[h], kc_ref[:, sls[h]], _NN) for h in hs]
            if with_prev:
                p_p = probs(kp_ref, m_prev)
                ds_p = dscores(p_p, vp_ref)
                dq = [dq[h] + _mxu(ds_p[h], kp_ref[:, sls[h]], _NN) for h in hs]
                dk_p = [_mxu(ds_p[h], q_ref[:, sls[h]], _TN) for h in hs]
                dv_p = [_mxu(p_p[h], do_ref[:, sls[h]], _TN) for h in hs]
            dk_c = [_mxu(ds_c[h], q_ref[:, sls[h]], _TN) for h in hs]
            dv_c = [_mxu(p_c[h], do_ref[:, sls[h]], _TN) for h in hs]
            for h in hs:
                vs = slice(WIDTH + h * HEAD_DIM, WIDTH + (h + 1) * HEAD_DIM)
                dq_ref[:, sls[h]] = dq[h] * scale
                if with_prev:
                    dkv_ref[:, sls[h]] = own[:, sls[h]] + dk_p[h]
                    dkv_ref[:, vs] = own[:, vs] + dv_p[h]
                own[:, sls[h]] = dk_c[h]
                own[:, vs] = dv_c[h]

        @pl.when((i > 0) & (i < nb))
        def _():
            step(True)

        @pl.when(i == 0)
        def _():
            step(False)

        @pl.when(i == nb)
        def _():
            dkv_ref[...] = own[...]

    def qkv_spec(c, shift):
        return pl.BlockSpec((ATT_BLOCK, WIDTH), lambda r, i: (jnp.clip(i + shift, 0, nb - 1), 3 * r + c))

    tok = pl.BlockSpec((ATT_BLOCK, WIDTH), lambda r, i: (jnp.minimum(i, nb - 1), r))
    per_head = pl.BlockSpec((ATT_BLOCK, 128), lambda r, i: (jnp.minimum(i, nb - 1), r))
    return pl.pallas_call(
        body, name=name, grid=(d, nb + 1),
        in_specs=[qkv_spec(0, 0), qkv_spec(1, 0), qkv_spec(1, -1), qkv_spec(2, 0), qkv_spec(2, -1), tok, per_head, per_head],
        out_specs=[tok, pl.BlockSpec((ATT_BLOCK, 2 * WIDTH), lambda r, i: (jnp.maximum(i - 1, 0), r))],
        out_shape=[jax.ShapeDtypeStruct((rows, d * WIDTH), F32), jax.ShapeDtypeStruct((rows, d * 2 * WIDTH), F32)],
        scratch_shapes=[pltpu.VMEM((ATT_BLOCK, 2 * WIDTH), F32)],
        compiler_params=_params(("parallel", "arbitrary")),
    )(qkv, qkv, qkv, qkv, qkv, do, lse, delta)


def _att_merge(os_, lses, proj, tm=1024):
    t = proj.shape[0]

    def body(o0, o1, o2, l0, l1, l2, z_ref, oz_ref, o_ref, t0, t1, t2, s_o1, s_o2, s_l1, s_l2, s_t):
        _from_strided_view(o1, s_o1, DILATIONS[1])
        _from_strided_view(o2, s_o2, DILATIONS[2])
        _from_strided_view(l1, s_l1, DILATIONS[1])
        _from_strided_view(l2, s_l2, DILATIONS[2])
        a, b, c = l0[...], s_l1[0], s_l2[0]
        m = jnp.maximum(jnp.maximum(a, b), c)
        wa, wb, wc = jnp.exp(a - m), jnp.exp(b - m), jnp.exp(c - m)
        den = wa + wb + wc
        total = m + jnp.log(den)
        t0[...] = total
        s_t[0] = total
        _to_strided_view(s_t, t1, DILATIONS[1])
        _to_strided_view(s_t, t2, DILATIONS[2])
        spread = jnp.where(_iota((128, WIDTH), 1) // HEAD_DIM == _iota((128, WIDTH), 0), 1.0, 0.0).astype(MXU_DTYPE)
        ra, rb, rc = (_pair_sum_lanes(w / den, spread) for w in (wa, wb, wc))
        o = ra * o0[...] + rb * _load_tile(s_o1) + rc * _load_tile(s_o2)
        z = z_ref[...]
        o_ref[...] = o
        oz_ref[...] = (o * z * _sigmoid(z)).astype(oz_ref.dtype)

    tok = pl.BlockSpec((tm, WIDTH), lambda i: (i, 0))
    views = [_strided_spec(tm, d, WIDTH) for d in DILATIONS]
    per_head = [_strided_spec(tm, d, 128) for d in DILATIONS]
    return pl.pallas_call(
        body, name="att_merge", grid=(t // tm,),
        in_specs=views + per_head + [pl.BlockSpec((tm, WIDTH), lambda i: (i, SEG_ZA // WIDTH))],
        out_specs=[tok, tok] + per_head,
        out_shape=[jax.ShapeDtypeStruct((t, WIDTH), MXU_DTYPE), jax.ShapeDtypeStruct((t, WIDTH), F32)]
        + [jax.ShapeDtypeStruct((t // d, d * 128), F32) for d in DILATIONS],
        scratch_shapes=[_tile_scratch(tm, WIDTH)] * 2 + [_tile_scratch(tm, 128)] * 3,
        compiler_params=_params(("parallel",)),
    )(*os_, *lses, proj)


def _att_merge_bwd(dproj, dy, w_up, o, proj, tm=1024):
    t = proj.shape[0]

    def body(dp_ref, dy_ref, wu_ref, o_ref, z_ref, dz_ref, do0, do1, do2, dl0, dl1, dl2, s_do, s_dl):
        z, ov, g = z_ref[...], o_ref[...], _mxu(dy_ref[...], wu_ref[...], _NT)
        sg = _sigmoid(z)
        do = g * z * sg
        dz_ref[...] = (g * ov * sg * (1.0 + z * (1.0 - sg))).astype(dz_ref.dtype)
        do0[...] = do.astype(do0.dtype)
        _store_tile(s_do, do)
        prod = do * ov
        delta = jnp.zeros((tm, 128), F32)
        for p in range(HEADS // 2):
            delta += _pair_sum_lanes(prod[:, p * 128:(p + 1) * 128], _pair_to_lanes(p, 0))
        dl0[...] = delta
        s_dl[0] = delta
        for d, do_v, dl_v in ((DILATIONS[1], do1, dl1), (DILATIONS[2], do2, dl2)):
            _to_strided_view(s_do, do_v, d)
            _to_strided_view(s_dl, dl_v, d)

    tok = pl.BlockSpec((tm, WIDTH), lambda i: (i, 0))
    seg = pl.BlockSpec((tm, WIDTH), lambda i: (i, SEG_ZA // WIDTH))
    views = [_strided_spec(tm, d, WIDTH) for d in DILATIONS]
    per_head = [_strided_spec(tm, d, 128) for d in DILATIONS]
    return pl.pallas_call(
        body, name="att_merge_bwd", grid=(t // tm,),
        in_specs=[pl.BlockSpec(memory_space=pl.ANY), pl.BlockSpec((tm, D_MODEL), lambda i: (i, 0)),
                  pl.BlockSpec((WIDTH, D_MODEL), lambda i: (0, 0)), tok, seg],
        out_specs=[seg] + views + per_head,
        out_shape=[jax.ShapeDtypeStruct((t, PACKED_WIDTH), MXU_DTYPE)]
        + [jax.ShapeDtypeStruct((t // d, d * WIDTH), MXU_DTYPE) for d in DILATIONS]
        + [jax.ShapeDtypeStruct((t // d, d * 128), F32) for d in DILATIONS],
        scratch_shapes=[_tile_scratch(tm, WIDTH), _tile_scratch(tm, 128)],
        input_output_aliases={0: 0},
        compiler_params=_params(("parallel",)),
    )(dproj, dy, w_up, o, proj)


def _shift_down(x, halo, s):
    if s == 0:
        return x
    xs = pltpu.roll(x, s, 0)
    head = jnp.where(_iota((8, x.shape[1]), 0) < s, pltpu.roll(halo, s, 0), xs[0:8])
    return jnp.concatenate([head, xs[8:]], axis=0)


def _shift_up(x, nxt, s):
    if s == 0:
        return x
    n = x.shape[0]
    xs = pltpu.roll(x, n - s, 0)
    tail = jnp.where(_iota((8, x.shape[1]), 0) >= 8 - s, pltpu.roll(nxt, 8 - s, 0), xs[n - 8:])
    return jnp.concatenate([xs[:n - 8], tail], axis=0)


def _conv_fwd(proj, conv_w, tm=1024):
    t = proj.shape[0]
    cb = SEG_B // WIDTH

    def body(x_ref, halo_ref, w_ref, c_ref):
        halo = jnp.where(pl.program_id(0) > 0, halo_ref[...], 0.0)
        x = x_ref[...]
        w = w_ref[...]
        acc = jnp.zeros((tm, WIDTH), F32)
        for j in range(GDN_CONV):
            acc += _shift_down(x, halo, GDN_CONV - 1 - j) * w[j:j + 1, :]
        c_ref[...] = acc

    return pl.pallas_call(
        body, name="conv_fwd", grid=(t // tm, 3),
        in_specs=[pl.BlockSpec((tm, WIDTH), lambda i, c: (i, cb + c)),
                  pl.BlockSpec((8, WIDTH), lambda i, c: (jnp.maximum(i * (tm // 8) - 1, 0), cb + c)),
                  pl.BlockSpec((GDN_CONV, WIDTH), lambda i, c: (0, c))],
        out_specs=pl.BlockSpec((tm, WIDTH), lambda i, c: (i, c)),
        out_shape=jax.ShapeDtypeStruct((t, 3 * WIDTH), F32),
        compiler_params=_params(("parallel", "parallel")),
    )(proj, proj, conv_w)


def _conv_bwd(dproj, dc, proj, conv_w, tm=1024):
    t = proj.shape[0]
    cb = SEG_B // WIDTH
    nt = t // tm

    def body(dp_ref, dc_ref, dcn_ref, x_ref, halo_ref, w_ref, dx_ref, dw_ref):
        i = pl.program_id(1)
        w = w_ref[...]
        dcn = jnp.where(i < nt - 1, dcn_ref[...], 0.0)
        dcv = dc_ref[...]
        acc = jnp.zeros((tm, WIDTH), F32)
        for j in range(GDN_CONV):
            acc += _shift_up(dcv, dcn, GDN_CONV - 1 - j) * w[j:j + 1, :]
        dx_ref[...] = acc.astype(dx_ref.dtype)
        halo = jnp.where(i > 0, halo_ref[...], 0.0)
        x = x_ref[...]
        row8 = _iota((8, WIDTH), 0)
        part = jnp.zeros((8, WIDTH), F32)
        for j in range(GDN_CONV):
            s = jnp.sum(dcv * _shift_down(x, halo, GDN_CONV - 1 - j), axis=0, keepdims=True)
            part += jnp.where(row8 == j, s, 0.0)

        @pl.when(i == 0)
        def _():
            dw_ref[...] = jnp.zeros_like(dw_ref)

        dw_ref[...] += part

    return pl.pallas_call(
        body, name="conv_bwd", grid=(3, nt),
        in_specs=[pl.BlockSpec(memory_space=pl.ANY),
                  pl.BlockSpec((tm, WIDTH), lambda c, i: (i, c)),
                  pl.BlockSpec((8, WIDTH), lambda c, i: (jnp.minimum((i + 1) * (tm // 8), t // 8 - 1), c)),
                  pl.BlockSpec((tm, WIDTH), lambda c, i: (i, cb + c)),
                  pl.BlockSpec((8, WIDTH), lambda c, i: (jnp.maximum(i * (tm // 8) - 1, 0), cb + c)),
                  pl.BlockSpec((GDN_CONV, WIDTH), lambda c, i: (0, c))],
        out_specs=[pl.BlockSpec((tm, WIDTH), lambda c, i: (i, cb + c)),
                   pl.BlockSpec((8, WIDTH), lambda c, i: (0, c))],
        out_shape=[jax.ShapeDtypeStruct((t, PACKED_WIDTH), MXU_DTYPE), jax.ShapeDtypeStruct((8, 3 * WIDTH), F32)],
        input_output_aliases={0: 0},
        compiler_params=_params(("parallel", "arbitrary")),
    )(dproj, dc, dc, proj, proj, conv_w)


def _chunk_matrices(tm):
    r, c = _iota((tm, tm), 0), _iota((tm, tm), 1)
    same = (r // GDN_CHUNK) == (c // GDN_CHUNK)
    return jnp.where(same & (c <= r), 1.0, 0.0), jnp.where(same, 1.0, 0.0)


def _gdn_gates(ba, a_log, dt_bias):
    al = ba + dt_bias
    return _sigmoid(ba), -jnp.exp(a_log) * _softplus(al), _sigmoid(al)


SCAN_HEADS = (0, 2, 4, 6, 1, 3, 5, 7)


def _head_lane_eye():
    return jnp.where(_iota((HEADS, 128), 1) == _iota((HEADS, 128), 0) + HEADS, 1.0, 0.0)


def _pair_ones():
    return jnp.where(_iota((128, 128), 0) // HEAD_DIM == _iota((128, 128), 1) // HEAD_DIM, 1.0, 0.0).astype(MXU_DTYPE)


def _pair_to_lanes(p, base):
    return jnp.where(_iota((128, 128), 1) == base + 2 * p + _iota((128, 128), 0) // HEAD_DIM, 1.0, 0.0).astype(MXU_DTYPE)


def _pair_sum_lanes(x, sel):
    hi, lo = _split(x)
    return jnp.dot(hi, sel, preferred_element_type=F32) + jnp.dot(lo, sel, preferred_element_type=F32)


def _gdn_prep(conv, proj, a_log, dt_bias, tm=512):
    t = proj.shape[0]
    nc = tm // GDN_CHUNK

    def body(c_ref, ba_ref, al_ref, dt_ref, q_ref, k_ref, v_ref, b_ref, g_ref, gl_ref, grow_ref):
        beta, g, _ = _gdn_gates(ba_ref[:, 0:128], al_ref[...], dt_ref[...])
        lmat, cmat = _chunk_matrices(tm)
        gc = jnp.dot(lmat, g, precision=HIGHEST, preferred_element_type=F32)
        gl = jnp.dot(cmat, g, precision=HIGHEST, preferred_element_type=F32)
        grow = lax.dot_general(_head_lane_eye(), gc, _NT, precision=HIGHEST, preferred_element_type=F32)
        ones = _pair_ones()
        first = _iota((1, 128), 1) < HEAD_DIM

        def spread(x, base, p):
            return jnp.where(first, x[:, base + 2 * p:base + 2 * p + 1], x[:, base + 2 * p + 1:base + 2 * p + 2])

        for p in range(HEADS // 2):
            for seg, ref, scale in ((0, q_ref, HEAD_DIM ** -0.5), (1, k_ref, 1.0), (2, v_ref, None)):
                c = c_ref[:, seg * WIDTH + p * 128:seg * WIDTH + (p + 1) * 128]
                a = c * _sigmoid(c)
                if scale is not None:
                    a = a * (lax.rsqrt(_pair_sum_lanes(a * a, ones) + NORM_EPS) * scale)
                ref[p] = a
            b_ref[p] = spread(beta, 0, p)
            g_ref[p] = spread(gc, HEADS, p)
            gl_ref[p] = spread(gl, HEADS, p)
        for pos, h in enumerate(SCAN_HEADS):
            for cc in range(nc):
                grow_ref[pos, cc] = grow[h:h + 1, cc * GDN_CHUNK:(cc + 1) * GDN_CHUNK]

    hm = pl.BlockSpec((HEADS // 2, tm, 128), lambda i: (0, i, 0))
    small = pl.BlockSpec((1, 128), lambda i: (0, 0))
    hm_shape = jax.ShapeDtypeStruct((HEADS // 2, t, 128), F32)
    return pl.pallas_call(
        body, name="gdn_prep", grid=(t // tm,),
        in_specs=[pl.BlockSpec((tm, 3 * WIDTH), lambda i: (i, 0)),
                  pl.BlockSpec((tm, WIDTH), lambda i: (i, SEG_BA // WIDTH)), small, small],
        out_specs=[hm] * 6 + [pl.BlockSpec((HEADS, nc, 1, GDN_CHUNK), lambda i: (0, i, 0, 0))],
        out_shape=[hm_shape] * 6 + [jax.ShapeDtypeStruct((HEADS, t // GDN_CHUNK, 1, GDN_CHUNK), F32)],
        compiler_params=_params(("parallel",)),
    )(conv, proj, a_log, dt_bias)


def _gdn_prep_bwd(dproj, conv, proj, a_log, dt_bias, dq, dk, dv, db, dg, dgl, dgrow, tm=512):
    t = proj.shape[0]
    nc = tm // GDN_CHUNK

    def body(dp_ref, c_ref, ba_ref, al_ref, dt_ref, dq_ref, dk_ref, dv_ref, db_ref, dg_ref, dgl_ref, dgrow_ref,
             dba_ref, dc_ref, small_ref, row_scr):
        beta, g, sig_al = _gdn_gates(ba_ref[:, 0:128], al_ref[...], dt_ref[...])
        d_beta = jnp.zeros((tm, 128), F32)
        d_gc = jnp.zeros((tm, 128), F32)
        d_gl = jnp.zeros((tm, 128), F32)
        ones = _pair_ones()
        for pos, h in enumerate(SCAN_HEADS):
            for cc in range(nc):
                row_scr[h:h + 1, cc * GDN_CHUNK:(cc + 1) * GDN_CHUNK] = dgrow_ref[pos, cc]
        for p in range(HEADS // 2):
            d_beta += _pair_sum_lanes(db_ref[p], _pair_to_lanes(p, 0))
            d_gc += _pair_sum_lanes(dg_ref[p], _pair_to_lanes(p, HEADS))
            d_gl += _pair_sum_lanes(dgl_ref[p], _pair_to_lanes(p, HEADS))
            for seg, ref, scale in ((0, dq_ref, HEAD_DIM ** -0.5), (1, dk_ref, 1.0), (2, dv_ref, None)):
                cols = slice(seg * WIDTH + p * 128, seg * WIDTH + (p + 1) * 128)
                c = c_ref[:, cols]
                sg = _sigmoid(c)
                da = ref[p]
                if scale is not None:
                    a = c * sg
                    r = lax.rsqrt(_pair_sum_lanes(a * a, ones) + NORM_EPS)
                    da = da * scale
                    da = r * da - a * (r * r * r) * _pair_sum_lanes(da * a, ones)
                dc_ref[:, cols] = da * (sg * (1.0 + c * (1.0 - sg)))
        d_gc += lax.dot_general(row_scr[...], _head_lane_eye(), _TN, precision=HIGHEST, preferred_element_type=F32)
        lmat, cmat = _chunk_matrices(tm)
        d_g = (lax.dot_general(lmat, d_gc, _TN, precision=HIGHEST, preferred_element_type=F32)
               + lax.dot_general(cmat, d_gl, _TN, precision=HIGHEST, preferred_element_type=F32))
        d_al = d_g * (-jnp.exp(al_ref[...])) * sig_al
        d_bl = d_beta * beta * (1.0 - beta)
        dba_ref[...] = jnp.concatenate([d_bl + d_al, jnp.zeros((tm, WIDTH - 128), F32)], axis=1).astype(dba_ref.dtype)
        row8 = _iota((8, 128), 0)
        part = (jnp.where(row8 == 0, jnp.sum(d_g * g, axis=0, keepdims=True), 0.0)
                + jnp.where(row8 == 1, jnp.sum(d_al, axis=0, keepdims=True), 0.0))

        @pl.when(pl.program_id(0) == 0)
        def _():
            small_ref[...] = jnp.zeros_like(small_ref)

        small_ref[...] += part

    hm = pl.BlockSpec((HEADS // 2, tm, 128), lambda i: (0, i, 0))
    small = pl.BlockSpec((1, 128), lambda i: (0, 0))
    seg = pl.BlockSpec((tm, WIDTH), lambda i: (i, SEG_BA // WIDTH))
    return pl.pallas_call(
        body, name="gdn_prep_bwd", grid=(t // tm,),
        in_specs=[pl.BlockSpec(memory_space=pl.ANY), pl.BlockSpec((tm, 3 * WIDTH), lambda i: (i, 0)), seg, small, small]
        + [hm] * 6 + [pl.BlockSpec((HEADS, nc, 1, GDN_CHUNK), lambda i: (0, i, 0, 0))],
        out_specs=[seg, pl.BlockSpec((tm, 3 * WIDTH), lambda i: (i, 0)), pl.BlockSpec((8, 128), lambda i: (0, 0))],
        out_shape=[jax.ShapeDtypeStruct((t, PACKED_WIDTH), MXU_DTYPE), jax.ShapeDtypeStruct((t, 3 * WIDTH), F32),
                   jax.ShapeDtypeStruct((8, 128), F32)],
        scratch_shapes=[pltpu.VMEM((HEADS, tm), F32)],
        input_output_aliases={0: 0},
        compiler_params=_params(("arbitrary",)),
    )(dproj, conv, proj, a_log, dt_bias, dq, dk, dv, db, dg, dgl, dgrow)


_BNN = (((2,), (1,)), ((0,), (0,)))
_BNT = (((2,), (2,)), ((0,), (0,)))
_BTN = (((1,), (1,)), ((0,), (0,)))


@jax.custom_vjp
def _MM_NN(a, b):
    return _mxu(a, b, _BNN)


@jax.custom_vjp
def _MM_NT(a, b):
    return _mxu(a, b, _BNT)


@jax.custom_vjp
def _MM_TN(a, b):
    return _mxu(a, b, _BTN)


_MM_NN.defvjp(lambda a, b: (_mxu(a, b, _BNN), (a, b)), lambda r, g: (_mxu(g, r[1], _BNT), _mxu(r[0], g, _BTN)))
_MM_NT.defvjp(lambda a, b: (_mxu(a, b, _BNT), (a, b)), lambda r, g: (_mxu(g, r[1], _BNN), _mxu(g, r[0], _BTN)))
_MM_TN.defvjp(lambda a, b: (_mxu(a, b, _BTN), (a, b)), lambda r, g: (_mxu(r[1], g, _BNT), _mxu(r[0], g, _BNN)))


def _split(a):
    hi = a.astype(MXU_DTYPE)
    return hi, (a - hi.astype(F32)).astype(MXU_DTYPE)


def _dot3(a, b, dims):
    (ah, al), (bh, bl) = a, b
    (ca,), (cb,) = dims[0]
    return lax.dot_general(jnp.concatenate([ah, ah, al], axis=ca), jnp.concatenate([bh, bl, bh], axis=cb), dims,
                           preferred_element_type=F32)


def _unit_lower_inverse(a):
    c = GDN_CHUNK
    eye = jnp.where(_iota((c, c), 0) == _iota((c, c), 1), 1.0, 0.0)
    x = eye - a
    p = a
    for _ in range(5):
        ps = _split(p)
        p = _dot3(ps, ps, _BNN)
        x = x + _dot3(_split(x), _split(p), _BNN)
    return x


@jax.custom_vjp
def _SAVED_INVERSE(a, t_inv):
    return t_inv


def _saved_inverse_bwd(t_inv, g):
    ts = _split(t_inv)
    return -_dot3(ts, _split(_dot3(_split(g), ts, _BNT)), _BTN), jnp.zeros_like(t_inv)


_SAVED_INVERSE.defvjp(lambda a, t_inv: (t_inv, t_inv), _saved_inverse_bwd)


def _gdn_chunk(q, k, v, beta, g1, g2, gl, state, t_inv=None):
    c = GDN_CHUNK
    if t_inv is None:
        _mm_nn, _mm_nt, _mm_tn = (functools.partial(_mxu, dims=dd) for dd in (_BNN, _BNT, _BTN))
    else:
        _mm_nn, _mm_nt, _mm_tn = _MM_NN, _MM_NT, _MM_TN
    row, col = _iota((c, c), 0), _iota((c, c), 1)
    incl, strict = row >= col, row > col
    decay = jnp.where(incl, jnp.exp(jnp.where(incl, g1 - g2, 0.0)), 0.0)
    eg = jnp.exp(g1)
    kb = k * beta
    a = _mm_nt(kb, k) * jnp.where(strict, decay, 0.0)
    inv = _unit_lower_inverse(a) if t_inv is None else _SAVED_INVERSE(a, t_inv)
    u = _mm_nn(inv, v * beta)
    w = _mm_nn(inv, kb * eg)
    attn = _mm_nt(q, k) * decay
    v_new = u - _mm_nn(w, state)
    o = _mm_nn(q * eg, state) + _mm_nn(attn, v_new)
    new_state = state * jnp.exp(gl) + _mm_tn(k * jnp.exp(gl - g1), v_new)
    return (o, new_state, inv) if t_inv is None else (o, new_state)


def _unpair(x):
    return jnp.concatenate([x[..., :HEAD_DIM], x[..., HEAD_DIM:]], axis=0)


def _gdn_fwd(q, k, v, beta, g, gl, grow, cpb=4):
    t = q.shape[1]
    rows = cpb * GDN_CHUNK
    lo, hi = slice(0, HEAD_DIM), slice(HEAD_DIM, 2 * HEAD_DIM)

    def body(q_ref, k_ref, v_ref, b_ref, g_ref, gl_ref, grow_ref, o_ref, st_ref, inv_ref, state):
        @pl.when(pl.program_id(0) == 0)
        def _():
            state[...] = jnp.zeros_like(state)

        s = state[...]
        for cc in range(cpb):
            sl = slice(cc * GDN_CHUNK, (cc + 1) * GDN_CHUNK)
            st_ref[:, cc, :, lo], st_ref[:, cc, :, hi] = s[:HEADS // 2], s[HEADS // 2:]
            g2 = jnp.broadcast_to(grow_ref[:, cc], (HEADS, GDN_CHUNK, GDN_CHUNK))
            o, s, inv = _gdn_chunk(*[_unpair(r[:, sl, :]) for r in (q_ref, k_ref, v_ref, b_ref, g_ref)], g2,
                                   _unpair(gl_ref[:, sl, :]), s)
            o_ref[:, sl, lo], o_ref[:, sl, hi] = o[:HEADS // 2], o[HEADS // 2:]
            inv_ref[:, cc, :, lo], inv_ref[:, cc, :, hi] = inv[:HEADS // 2], inv[HEADS // 2:]
        state[...] = s

    hm = pl.BlockSpec((HEADS // 2, rows, 128), lambda i: (0, i, 0))
    per_chunk = pl.BlockSpec((HEADS // 2, cpb, GDN_CHUNK, 128), lambda i: (0, i, 0, 0))
    chunk_shape = jax.ShapeDtypeStruct((HEADS // 2, t // GDN_CHUNK, GDN_CHUNK, 128), F32)
    return pl.pallas_call(
        body, name="gdn_fwd", grid=(t // rows,),
        in_specs=[hm] * 6 + [pl.BlockSpec((HEADS, cpb, 1, GDN_CHUNK), lambda i: (0, i, 0, 0))],
        out_specs=[hm, per_chunk, per_chunk],
        out_shape=[jax.ShapeDtypeStruct((HEADS // 2, t, 128), F32), chunk_shape, chunk_shape],
        scratch_shapes=[pltpu.VMEM((HEADS, GDN_CHUNK, HEAD_DIM), F32)],
        compiler_params=_params(("arbitrary",)),
    )(q, k, v, beta, g, gl, grow)


def _gdn_bwd(q, k, v, beta, g, gl, grow, states, invs, do, cpb=4):
    t = q.shape[1]
    rows = cpb * GDN_CHUNK
    nsteps = t // rows
    lo, hi = slice(0, HEAD_DIM), slice(HEAD_DIM, 2 * HEAD_DIM)

    def body(q_ref, k_ref, v_ref, b_ref, g_ref, gl_ref, grow_ref, st_ref, inv_ref, do_ref,
             dq_ref, dk_ref, dv_ref, db_ref, dg_ref, dgl_ref, dgrow_ref, dstate):
        @pl.when(pl.program_id(0) == 0)
        def _():
            dstate[...] = jnp.zeros_like(dstate)

        ds = dstate[...]
        for cc in reversed(range(cpb)):
            sl = slice(cc * GDN_CHUNK, (cc + 1) * GDN_CHUNK)
            g2 = jnp.broadcast_to(grow_ref[:, cc], (HEADS, GDN_CHUNK, GDN_CHUNK))
            _, vjp = jax.vjp(_gdn_chunk, *[_unpair(r[:, sl, :]) for r in (q_ref, k_ref, v_ref, b_ref, g_ref)], g2,
                             _unpair(gl_ref[:, sl, :]), _unpair(st_ref[:, cc]), _unpair(inv_ref[:, cc]))
            gq, gk, gv, gb, gg1, gg2, ggl, ds, _ = vjp((_unpair(do_ref[:, sl, :]), ds))
            for ref, val in ((dq_ref, gq), (dk_ref, gk), (dv_ref, gv), (db_ref, gb), (dg_ref, gg1), (dgl_ref, ggl)):
                ref[:, sl, lo], ref[:, sl, hi] = val[:HEADS // 2], val[HEADS // 2:]
            dgrow_ref[:, cc] = jnp.sum(gg2, axis=1, keepdims=True)
        dstate[...] = ds

    hm = pl.BlockSpec((HEADS // 2, rows, 128), lambda i: (0, nsteps - 1 - i, 0))
    rowspec = pl.BlockSpec((HEADS, cpb, 1, GDN_CHUNK), lambda i: (0, nsteps - 1 - i, 0, 0))
    per_chunk = pl.BlockSpec((HEADS // 2, cpb, GDN_CHUNK, 128), lambda i: (0, nsteps - 1 - i, 0, 0))
    hm_shape = jax.ShapeDtypeStruct((HEADS // 2, t, 128), F32)
    return pl.pallas_call(
        body, name="gdn_bwd", grid=(nsteps,),
        in_specs=[hm] * 6 + [rowspec, per_chunk, per_chunk, hm],
        out_specs=[hm] * 6 + [rowspec],
        out_shape=[hm_shape] * 6 + [jax.ShapeDtypeStruct((HEADS, t // GDN_CHUNK, 1, GDN_CHUNK), F32)],
        scratch_shapes=[pltpu.VMEM((HEADS, GDN_CHUNK, HEAD_DIM), F32)],
        compiler_params=_params(("arbitrary",)),
    )(q, k, v, beta, g, gl, grow, states, invs, do)


def _gdn_out(o_hm, gdn_norm_w, proj, tm=1024):
    t = proj.shape[0]

    def body(o_ref, w_ref, z_ref, oz_ref):
        w = w_ref[...]
        ones = _pair_ones()
        for p in range(HEADS // 2):
            cols = slice(p * 128, (p + 1) * 128)
            o = o_ref[p]
            z = z_ref[:, cols]
            r = lax.rsqrt(_pair_sum_lanes(o * o, ones) * (1.0 / HEAD_DIM) + NORM_EPS)
            oz_ref[:, cols] = (o * r * w * (z * _sigmoid(z))).astype(oz_ref.dtype)

    tok = pl.BlockSpec((tm, WIDTH), lambda i: (i, 0))
    return pl.pallas_call(
        body, name="gdn_out", grid=(t // tm,),
        in_specs=[pl.BlockSpec((HEADS // 2, tm, 128), lambda i: (0, i, 0)), pl.BlockSpec((1, 128), lambda i: (0, 0)),
                  pl.BlockSpec((tm, WIDTH), lambda i: (i, SEG_ZB // WIDTH))],
        out_specs=tok, out_shape=jax.ShapeDtypeStruct((t, WIDTH), MXU_DTYPE),
        compiler_params=_params(("parallel",)),
    )(o_hm, jnp.tile(gdn_norm_w, (1, 2)), proj)


def _gdn_out_bwd(dproj, dy, w_up, o_hm, gdn_norm_w, proj, tm=1024):
    t = proj.shape[0]

    def body(dp_ref, dy_ref, wu_ref, o_ref, w_ref, z_ref, dz_ref, do_ref, dw_ref):
        w = w_ref[...]
        ones = _pair_ones()
        dw = jnp.zeros((1, 128), F32)
        d_oz = _mxu(dy_ref[...], wu_ref[...], _NT)
        for p in range(HEADS // 2):
            cols = slice(p * 128, (p + 1) * 128)
            o = o_ref[p]
            z, g = z_ref[:, cols], d_oz[:, cols]
            sg = _sigmoid(z)
            r = lax.rsqrt(_pair_sum_lanes(o * o, ones) * (1.0 / HEAD_DIM) + NORM_EPS)
            dz_ref[:, cols] = (g * (o * r * w) * (sg * (1.0 + z * (1.0 - sg)))).astype(dz_ref.dtype)
            dn = g * (z * sg)
            dw += jnp.sum(dn * o * r, axis=0, keepdims=True)
            dnw = dn * w
            do_ref[p] = r * dnw - o * (r * r * r) * (_pair_sum_lanes(dnw * o, ones) * (1.0 / HEAD_DIM))

        @pl.when(pl.program_id(0) == 0)
        def _():
            dw_ref[...] = jnp.zeros_like(dw_ref)

        dw_ref[...] += jnp.where(_iota((8, 128), 0) == 0, dw, 0.0)

    tok = pl.BlockSpec((tm, WIDTH), lambda i: (i, 0))
    seg = pl.BlockSpec((tm, WIDTH), lambda i: (i, SEG_ZB // WIDTH))
    hm = pl.BlockSpec((HEADS // 2, tm, 128), lambda i: (0, i, 0))
    dz, do, dw = pl.pallas_call(
        body, name="gdn_out_bwd", grid=(t // tm,),
        in_specs=[pl.BlockSpec(memory_space=pl.ANY), pl.BlockSpec((tm, D_MODEL), lambda i: (i, 0)),
                  pl.BlockSpec((WIDTH, D_MODEL), lambda i: (0, 0)), hm, pl.BlockSpec((1, 128), lambda i: (0, 0)), seg],
        out_specs=[seg, hm, pl.BlockSpec((8, 128), lambda i: (0, 0))],
        out_shape=[jax.ShapeDtypeStruct((t, PACKED_WIDTH), MXU_DTYPE), jax.ShapeDtypeStruct((HEADS // 2, t, 128), F32),
                   jax.ShapeDtypeStruct((8, 128), F32)],
        input_output_aliases={0: 0},
        compiler_params=_params(("arbitrary",)),
    )(dproj, dy, w_up, o_hm, jnp.tile(gdn_norm_w, (1, 2)), proj)
    return dz, do, dw[:, :HEAD_DIM] + dw[:, HEAD_DIM:]


def _up_merge(oz_a, oz_b, w_up_a, w_up_b, proj, tm=512):
    t = proj.shape[0]

    def body(oa_ref, ob_ref, wa_ref, wb_ref, ga0, ga1, gb0, gb1, ya_ref, yb_ref, m_ref):
        ya = jnp.dot(oa_ref[...], wa_ref[...], preferred_element_type=F32)
        yb = jnp.dot(ob_ref[...], wb_ref[...], preferred_element_type=F32)
        ya_ref[...] = ya
        yb_ref[...] = yb
        for c, (ga, gb) in enumerate(((ga0, gb0), (ga1, gb1))):
            cols = slice(c * WIDTH, (c + 1) * WIDTH)
            m_ref[:, cols] = (_sigmoid(ga[...]) * ya[:, cols] + _sigmoid(gb[...]) * yb[:, cols]).astype(m_ref.dtype)

    def gate(seg, c):
        return pl.BlockSpec((tm, WIDTH), lambda i: (i, seg // WIDTH + c))

    tok_in = pl.BlockSpec((tm, WIDTH), lambda i: (i, 0))
    weight = pl.BlockSpec((WIDTH, D_MODEL), lambda i: (0, 0))
    tok = pl.BlockSpec((tm, D_MODEL), lambda i: (i, 0))
    return pl.pallas_call(
        body, name="up_merge", grid=(t // tm,),
        in_specs=[tok_in, tok_in, weight, weight, gate(SEG_GA, 0), gate(SEG_GA, 1), gate(SEG_GB, 0), gate(SEG_GB, 1)],
        out_specs=[tok, tok, tok],
        out_shape=[jax.ShapeDtypeStruct((t, D_MODEL), F32), jax.ShapeDtypeStruct((t, D_MODEL), F32),
                   jax.ShapeDtypeStruct((t, D_MODEL), MXU_DTYPE)],
        compiler_params=_params(("parallel",)),
    )(oz_a, oz_b, w_up_a, w_up_b, proj, proj, proj, proj)


def _merge_bwd(dproj, d_m, y, proj, seg, name, tm=1024):
    t = proj.shape[0]

    def body(*refs):
        dm_ref, y_ref, g_ref, dg_ref, dy_ref = refs[-5:]
        dm = dm_ref[...]
        s = _sigmoid(g_ref[...])
        dy_ref[...] = (dm * s).astype(dy_ref.dtype)
        dg_ref[...] = (dm * y_ref[...] * s * (1.0 - s)).astype(dg_ref.dtype)

    half = pl.BlockSpec((tm, WIDTH), lambda i, c: (i, c))
    gate = pl.BlockSpec((tm, WIDTH), lambda i, c: (i, seg // WIDTH + c))
    specs, args, aliases = [half, half, gate], [d_m, y, proj], {}
    if dproj is not None:
        specs, args, aliases = [pl.BlockSpec(memory_space=pl.ANY)] + specs, [dproj] + args, {0: 0}
    return pl.pallas_call(
        body, name=name, grid=(t // tm, 2), in_specs=specs, out_specs=[gate, half],
        out_shape=[jax.ShapeDtypeStruct((t, PACKED_WIDTH), MXU_DTYPE), jax.ShapeDtypeStruct((t, D_MODEL), MXU_DTYPE)],
        input_output_aliases=aliases,
        compiler_params=_params(("parallel", "parallel")),
    )(*args)


def _out_tail(merged, w_out, x, final_w, target, tm=1024):
    t = x.shape[0]
    tm = min(tm, t)

    def body(m_ref, wo_ref, x_ref, w_ref, t_ref, dxm_ref, dx_ref, loss_ref, dw_ref):
        x2 = x_ref[...] + jnp.dot(m_ref[...], wo_ref[...], preferred_element_type=F32)
        w = w_ref[...]
        r = lax.rsqrt(jnp.mean(x2 * x2, axis=-1, keepdims=True) + NORM_EPS)
        xn = x2 * r
        err = xn * w - t_ref[...]
        dy = err * (1.0 / D_MODEL)
        dyw = dy * w
        dx2 = r * dyw - x2 * (r * r * r) * jnp.mean(dyw * x2, axis=-1, keepdims=True)
        dx_ref[...] = dx2
        dxm_ref[...] = dx2.astype(dxm_ref.dtype)
        loss = 0.5 * jnp.sum(jnp.sum(err * err, axis=-1, keepdims=True) * (1.0 / D_MODEL), axis=0, keepdims=True)
        onehot = jnp.where((_iota((8, 128), 0) == 0) & (_iota((8, 128), 1) == 0), 1.0, 0.0)

        @pl.when(pl.program_id(0) == 0)
        def _():
            loss_ref[...] = jnp.zeros_like(loss_ref)
            dw_ref[...] = jnp.zeros_like(dw_ref)

        loss_ref[...] += loss * onehot
        dw_ref[...] += jnp.where(_iota((8, D_MODEL), 0) == 0, jnp.sum(dy * xn, axis=0, keepdims=True), 0.0)

    tok = pl.BlockSpec((tm, D_MODEL), lambda i: (i, 0))
    full = pl.BlockSpec((D_MODEL, D_MODEL), lambda i: (0, 0))
    return pl.pallas_call(
        body, name="out_tail", grid=(t // tm,),
        in_specs=[tok, full, tok, pl.BlockSpec((1, D_MODEL), lambda i: (0, 0)), tok],
        out_specs=[tok, tok, pl.BlockSpec((8, 128), lambda i: (0, 0)), pl.BlockSpec((8, D_MODEL), lambda i: (0, 0))],
        out_shape=[jax.ShapeDtypeStruct((t, D_MODEL), MXU_DTYPE), jax.ShapeDtypeStruct((t, D_MODEL), F32),
                   jax.ShapeDtypeStruct((8, 128), F32), jax.ShapeDtypeStruct((8, D_MODEL), F32)],
        compiler_params=_params(("arbitrary",)),
    )(merged, w_out, x, final_w, target)


def _dh_norm_bwd(dproj, wp, x, norm_w, dx2, after, blocks, name, carry=None, tm=1024, tk=PACKED_WIDTH // 4):
    t = x.shape[0]
    nk = PACKED_WIDTH // tk
    first, end = blocks

    def body(*refs):
        dp_ref, wp_ref, x_ref, w_ref, dx2_ref = refs[:5]
        dx_ref, dw_ref, acc = refs[-3:]
        kk = pl.program_id(1)
        part = _mxu(dp_ref[...], wp_ref[...], _NT)

        @pl.when(kk == 0)
        def _():
            acc[...] = part

        @pl.when(kk > 0)
        def _():
            acc[...] += part

        @pl.when((kk == 0) & (pl.program_id(0) == 0))
        def _():
            dw_ref[...] = jnp.zeros_like(dw_ref) if carry is None else refs[-4][...]

        @pl.when(kk == nk - 1)
        def _():
            xf, w, dh_ = x_ref[...], w_ref[...], acc[...]
            r = lax.rsqrt(jnp.mean(xf * xf, axis=-1, keepdims=True) + NORM_EPS)
            dhw = dh_ * w
            dx_ref[...] = dx2_ref[...] + r * dhw - xf * (r * r * r) * jnp.mean(dhw * xf, axis=-1, keepdims=True)
            dw_ref[...] += jnp.where(_iota((8, D_MODEL), 0) == 0, jnp.sum(dh_ * xf * r, axis=0, keepdims=True), 0.0)

    tok = pl.BlockSpec((tm, D_MODEL), lambda i, kk: (first + i, 0))
    small = pl.BlockSpec((8, D_MODEL), lambda i, kk: (0, 0))
    specs = [pl.BlockSpec((tm, tk), lambda i, kk: (first + i, kk)), pl.BlockSpec((D_MODEL, tk), lambda i, kk: (0, kk)),
             tok, pl.BlockSpec((1, D_MODEL), lambda i, kk: (0, 0)), tok, pl.BlockSpec(memory_space=pl.ANY)]
    args = [dproj, wp, x, norm_w, dx2, after]
    aliases = {}
    if carry is not None:
        specs += [pl.BlockSpec(memory_space=pl.ANY), small]
        args += list(carry)
        aliases = {len(args) - 2: 0}
    return pl.pallas_call(
        body, name=name, grid=(end - first, nk), in_specs=specs, out_specs=[tok, small],
        out_shape=[jax.ShapeDtypeStruct((t, D_MODEL), F32), jax.ShapeDtypeStruct((8, D_MODEL), F32)],
        scratch_shapes=[pltpu.VMEM((tm, D_MODEL), F32)],
        input_output_aliases=aliases,
        compiler_params=_params(("arbitrary", "arbitrary")),
    )(*args)


def _local_step(x, target, norm_w, wp, conv_w, a_log, dt_bias, gdn_norm_w, w_up_a, w_up_b, w_out, final_w,
                start_reduce, continue_reduce):
    t = x.shape[0]
    tables = _rope_tables(t)
    a_log = jnp.pad(a_log, ((0, 0), (HEADS, 128 - 2 * HEADS)))
    dt_bias = jnp.pad(dt_bias, ((0, 0), (HEADS, 128 - 2 * HEADS)))

    proj, h_t = _norm_proj(x, norm_w, wp)
    qkvs = _rope_fwd(proj, tables)
    outs, lses = zip(*[_att_fwd(qkvs[gi], d, f"att_fwd{gi}") for gi, d in enumerate(DILATIONS)])
    oz_a, o_a, *lse_views = _att_merge(outs, lses, proj)
    conv = _conv_fwd(proj, conv_w)
    gq, gk, gv, gb, gg, ggl, grow = _gdn_prep(conv, proj, a_log, dt_bias)
    o_b, states, invs = _gdn_fwd(gq, gk, gv, gb, gg, ggl, grow)
    oz_b = _gdn_out(o_b, gdn_norm_w, proj)
    big = dict(tm=1024, tn=1024, tk=1024)
    y_a, y_b, merged = _up_merge(oz_a, oz_b, w_up_a, w_up_b, proj)
    dx2_m, dx2, loss_blk, d_final = _out_tail(merged, w_out, x, final_w, target)

    d_wout = _matmul(merged, dx2_m, "tn", "d_w_out", **big)
    d_m = _matmul(dx2_m, w_out, "nt", "d_merged", **big)
    dproj, dy_a = _merge_bwd(None, d_m, y_a, proj, SEG_GA, "merge_bwd_a")
    dproj, dy_b = _merge_bwd(dproj, d_m, y_b, proj, SEG_GB, "merge_bwd_b")
    d_wua = _matmul(oz_a, dy_a, "tn", "d_w_up_a", **big)
    d_wub = _matmul(oz_b, dy_b, "tn", "d_w_up_b", **big)
    dproj, *views = _att_merge_bwd(dproj, dy_a, w_up_a, o_a, proj)
    do_views, delta_views = views[:3], views[3:]
    dqs, dkvs = zip(*[_att_bwd(qkvs[gi], do_views[gi], lse_views[gi], delta_views[gi], d, f"att_bwd{gi}")
                      for gi, d in enumerate(DILATIONS)])
    dproj = _rope_bwd(dproj, dqs, dkvs, tables)
    dproj, do_b, d_gnw = _gdn_out_bwd(dproj, dy_b, w_up_b, o_b, gdn_norm_w, proj)
    dgq, dgk, dgv, dgb, dgg, dggl, dgrow = _gdn_bwd(gq, gk, gv, gb, gg, ggl, grow, states, invs, do_b)
    dproj, dconv, d_small = _gdn_prep_bwd(dproj, conv, proj, a_log, dt_bias, dgq, dgk, dgv, dgb, dgg, dggl, dgrow)
    dproj, d_convw = _conv_bwd(dproj, dconv, proj, conv_w)
    d_wp = _matmul(h_t, dproj, "nn", "d_w_in", tm=1024, tn=PACKED_WIDTH // 4, tk=1024)
    in_flight, token = start_reduce(d_wp, d_wua, d_wub, d_wout, d_convw[0:GDN_CONV])
    nblk = t // 1024
    cut = max(1, nblk // 4)
    part = _dh_norm_bwd(dproj, wp, x, norm_w, dx2, token, (0, cut), "dh_norm_bwd_a")
    in_flight, token = continue_reduce(in_flight, part[0])
    grad_x, d_norm = _dh_norm_bwd(dproj, wp, x, norm_w, dx2, token, (cut, nblk), "dh_norm_bwd_b", carry=part)
    return dict(loss=loss_blk, grad_x=grad_x, norm_w=d_norm[0:1], in_flight=in_flight,
                a_log=d_small[0:1, HEADS:2 * HEADS], dt_bias=d_small[1:2, HEADS:2 * HEADS], gdn_norm_w=d_gnw[0:1],
                final_norm_w=d_final[0:1])


SHARDS = 4
W_IN_SHARD = IN_WIDTH // SHARDS
ROWS_UP = WIDTH * (D_MODEL // SHARDS) // 128
ROWS_OUT = (D_MODEL // SHARDS) * D_MODEL // 128
CONV_SHARD = 3 * WIDTH // SHARDS
ROWS_CONV = 16
SLAB_ROWS = 2 * ROWS_UP + ROWS_OUT + 2 * ROWS_CONV
HALF_ROWS = SLAB_ROWS // 2
BIG_HALF = (D_MODEL // 2, W_IN_SHARD)
SMALL_HALF = (HALF_ROWS, 128)
MESH = pl.DeviceIdType.MESH
ANY = pl.BlockSpec(memory_space=pl.ANY)


def _pad_rows(a, rows):
    return jnp.pad(a, ((0, rows - a.shape[0]), (0, 0)))


def _pack_slab(w_up_a, w_up_b, w_out, conv, conv_lo):
    parts = [w_up_a.reshape(ROWS_UP, 128), w_up_b.reshape(ROWS_UP, 128), w_out.reshape(ROWS_OUT, 128),
             _pad_rows(conv.reshape(-1, 128), ROWS_CONV), _pad_rows(conv_lo.reshape(-1, 128), ROWS_CONV)]
    return jnp.concatenate(parts, axis=0).reshape(2, *SMALL_HALF)


def _unpack_slab(slab):
    slab = slab.reshape(SLAB_ROWS, 128)
    r0 = 0
    out = []
    for rows, shape in ((ROWS_UP, (WIDTH, D_MODEL // SHARDS)), (ROWS_UP, (WIDTH, D_MODEL // SHARDS)),
                        (ROWS_OUT, (D_MODEL // SHARDS, D_MODEL)), (ROWS_CONV, None), (ROWS_CONV, None)):
        part = slab[r0:r0 + rows]
        out.append(part[:GDN_CONV * CONV_SHARD // 128].reshape(GDN_CONV, CONV_SHARD) if shape is None else part.reshape(shape))
        r0 += rows
    return out


def _mesh_position():
    x, y, c = lax.axis_index("x"), lax.axis_index("y"), lax.axis_index("c")
    return x, y, c, [(1 - x, y), (x, 1 - y), (1 - x, 1 - y)]


def _gather_weights(shards):
    n = len(shards)

    def body(*refs):
        in_refs, out_refs, (send_sems, recv_sems) = refs[:n], refs[n:2 * n], refs[2 * n:]
        x, y, c, chips = _mesh_position()

        def half(a, chip, which):
            return out_refs[a].at[2 * chip[0] + chip[1], which]

        def copy(k, src, dst, to):
            return pltpu.make_async_remote_copy(src_ref=src, dst_ref=dst, send_sem=send_sems.at[k], recv_sem=recv_sems.at[k],
                                                device_id=to, device_id_type=MESH)

        pairs = [(a, j, chip) for a in range(n) for j, chip in enumerate(chips)]
        first = [copy(6 * a + j, in_refs[a].at[c], half(a, (x, y), c), (*chip, c)) for a, j, chip in pairs]
        for cp in first:
            cp.start()
        passed = [copy(6 * a + 3 + j, half(a, chip, c), half(a, chip, c), (x, y, 1 - c)) for a, j, chip in pairs]
        for i, (a, j, chip) in enumerate(pairs):
            copy(6 * a + j, half(a, chip, c), half(a, chip, c), (x, y, c)).wait_recv()
            passed[i].start()
        for a, j, chip in pairs:
            copy(6 * a + 3 + j, half(a, chip, 1 - c), half(a, chip, 1 - c), (x, y, c)).wait_recv()
        for cp in first + passed:
            cp.wait_send()

    return pl.pallas_call(
        body, name="gather_weights", in_specs=[ANY] * n, out_specs=[ANY] * n,
        out_shape=[jax.ShapeDtypeStruct((SHARDS, *s.shape), s.dtype) for s in shards],
        scratch_shapes=[pltpu.SemaphoreType.DMA((6 * n,)), pltpu.SemaphoreType.DMA((6 * n,))],
    )(*shards)


def _pair_sum(grads, recv, blk, name):
    _, _, rows, cols = grads.shape

    def body(c_ref, g_ref, r_ref, o_ref):
        o_ref[...] = (g_ref[0] + r_ref[...]).astype(o_ref.dtype)

    spec = pl.BlockSpec((1, blk, cols), lambda s, i, c_ref: (s, i, 0))
    return pl.pallas_call(
        body, name=name,
        grid_spec=pltpu.PrefetchScalarGridSpec(
            num_scalar_prefetch=1, grid=(SHARDS, rows // blk),
            in_specs=[pl.BlockSpec((1, 1, blk, cols), lambda s, i, c_ref: (s, c_ref[0], i, 0)), spec],
            out_specs=spec),
        out_shape=jax.ShapeDtypeStruct((SHARDS, rows, cols), MXU_DTYPE),
        compiler_params=_params(("parallel", "parallel")),
    )(lax.axis_index("c").astype(jnp.int32).reshape(1), grads, recv)


_HBM = pl.BlockSpec(memory_space=pltpu.HBM)
_SEM = pl.BlockSpec(memory_space=pltpu.SEMAPHORE)
_DATAFLOW = pltpu.SideEffectType.DATAFLOW_SIDE_EFFECTING


def _scatter_copies(p_refs, l_refs, send_sems, recv_sems):
    x, y, c, chips = _mesh_position()
    return [pltpu.make_async_remote_copy(src_ref=p_refs[a].at[2 * chip[0] + chip[1]], dst_ref=l_refs[a].at[j],
                                         send_sem=send_sems.at[3 * a + j], recv_sem=recv_sems.at[3 * a + j],
                                         device_id=(*chip, c), device_id_type=MESH)
            for a in range(len(p_refs)) for j, chip in enumerate(chips)]


def _exchange_copies(g_refs, l_refs, send_sems, recv_sems):
    x, y, c, _ = _mesh_position()
    return [pltpu.make_async_remote_copy(src_ref=g_refs[a].at[s, 1 - c], dst_ref=l_refs[a].at[s],
                                         send_sem=send_sems.at[SHARDS * a + s], recv_sem=recv_sems.at[SHARDS * a + s],
                                         device_id=(x, y, 1 - c), device_id_type=MESH)
            for a in range(len(g_refs)) for s in range(SHARDS)]


def _copies_start(name, copies, count, sources, land_shapes):
    n = len(sources)
    lands = [lax.empty(shape, src.dtype) for shape, src in zip(land_shapes, sources)]

    def body(*refs):
        s_refs, l_refs, send_sems, recv_sems, token = refs[:n], refs[n:2 * n], refs[2 * n], refs[2 * n + 1], refs[-1]
        for cp in copies(s_refs, l_refs, send_sems, recv_sems):
            cp.start()
        token[...] = jnp.zeros_like(token)

    operands = [pltpu.with_memory_space_constraint(a, pltpu.HBM) for a in (*sources, *lands)]
    return pl.pallas_call(
        body, name=name, in_specs=[_HBM] * (2 * n),
        out_shape=(pltpu.SemaphoreType.DMA((count,)), pltpu.SemaphoreType.DMA((count,)),
                   *[pltpu.HBM(a.shape, a.dtype) for a in operands], jax.ShapeDtypeStruct((8, 128), F32)),
        out_specs=(_SEM, _SEM, *[_HBM] * (2 * n), pl.BlockSpec(memory_space=pltpu.VMEM)),
        input_output_aliases={i: 2 + i for i in range(2 * n)},
        compiler_params=pltpu.CompilerParams(has_side_effects=_DATAFLOW),
    )(*operands)


def _copies_wait(name, copies, send_sems, recv_sems, passed, after):
    n = len(passed) // 2

    def body(*refs):
        s_refs, l_refs, send_s, recv_s = refs[:n], refs[n:2 * n], refs[2 * n], refs[2 * n + 1]
        for cp in copies(s_refs, l_refs, send_s, recv_s):
            cp.wait_send()
            cp.wait_recv()

    return pl.pallas_call(
        body, name=name, in_specs=[_HBM] * (2 * n) + [_SEM, _SEM, ANY],
        out_shape=[pltpu.HBM(a.shape, a.dtype) for a in passed], out_specs=[_HBM] * (2 * n),
        input_output_aliases={i: i for i in range(2 * n)},
        compiler_params=pltpu.CompilerParams(has_side_effects=_DATAFLOW),
    )(*passed, send_sems, recv_sems, after)


def _chip_sum(pairs, recv, blk, name):
    _, rows, cols = pairs.shape

    def body(pos_ref, p_ref, r_ref, o_ref):
        o_ref[0] = ((p_ref[0].astype(F32) + r_ref[0].astype(F32)) + r_ref[1].astype(F32)) + r_ref[2].astype(F32)

    pos = jnp.stack([2 * lax.axis_index("x") + lax.axis_index("y"), lax.axis_index("c")]).astype(jnp.int32)
    return pl.pallas_call(
        body, name=name,
        grid_spec=pltpu.PrefetchScalarGridSpec(
            num_scalar_prefetch=1, grid=(rows // blk,),
            in_specs=[pl.BlockSpec((1, blk, cols), lambda i, pos_ref: (pos_ref[0], i, 0)),
                      pl.BlockSpec((3, blk, cols), lambda i, pos_ref: (0, i, 0))],
            out_specs=pl.BlockSpec((1, blk, cols), lambda i, pos_ref: (pos_ref[1], i, 0))),
        out_shape=jax.ShapeDtypeStruct((2, rows, cols), F32),
        compiler_params=_params(("parallel",)),
    )(pos, pairs, recv)


def _share_total(totals):
    n = len(totals)

    def body(*refs):
        t_refs, out_refs, (send_sems, recv_sems) = refs[:n], refs[n:2 * n], refs[2 * n:]
        x, y, c, _ = _mesh_position()
        copies = [pltpu.make_async_remote_copy(src_ref=t_refs[a].at[c], dst_ref=out_refs[a].at[c], send_sem=send_sems.at[a],
                                               recv_sem=recv_sems.at[a], device_id=(x, y, 1 - c), device_id_type=MESH)
                  for a in range(n)]
        for cp in copies:
            cp.start()
        for a in range(n):
            other = out_refs[a].at[1 - c]
            pltpu.make_async_remote_copy(src_ref=other, dst_ref=other, send_sem=send_sems.at[a], recv_sem=recv_sems.at[a],
                                         device_id=(x, y, c), device_id_type=MESH).wait_recv()
        for cp in copies:
            cp.wait_send()

    return pl.pallas_call(
        body, name="share_total", in_specs=[ANY] * n, out_specs=[ANY] * n,
        out_shape=[jax.ShapeDtypeStruct(t.shape, F32) for t in totals],
        scratch_shapes=[pltpu.SemaphoreType.DMA((n,)), pltpu.SemaphoreType.DMA((n,))],
        input_output_aliases={a: a for a in range(n)},
    )(*totals)


def _allreduce_small(block):
    def body(b_ref, out_ref, gath, send_sems, recv_sems):
        x, y, c, _ = _mesh_position()
        me = 4 * x + 2 * y + c
        gath[me] = b_ref[...]
        copies = []
        for k in range(1, 8):
            peer = (x ^ (k >> 2), y ^ ((k >> 1) & 1), c ^ (k & 1))
            copies.append(pltpu.make_async_remote_copy(src_ref=b_ref, dst_ref=gath.at[me], send_sem=send_sems.at[k - 1],
                                                       recv_sem=recv_sems.at[k - 1], device_id=peer, device_id_type=MESH))
        for cp in copies:
            cp.start()
        for k in range(1, 8):
            src = 4 * (x ^ (k >> 2)) + 2 * (y ^ ((k >> 1) & 1)) + (c ^ (k & 1))
            pltpu.make_async_remote_copy(src_ref=b_ref, dst_ref=gath.at[src], send_sem=send_sems.at[k - 1],
                                         recv_sem=recv_sems.at[k - 1], device_id=(x, y, c), device_id_type=MESH).wait_recv()
        for cp in copies:
            cp.wait_send()
        acc = gath[0]
        for d in range(1, 8):
            acc = acc + gath[d]
        out_ref[...] = acc

    vm = pl.BlockSpec(memory_space=pltpu.VMEM)
    return pl.pallas_call(
        body, name="allreduce_small", in_specs=[vm], out_specs=vm,
        out_shape=jax.ShapeDtypeStruct((8, D_MODEL), F32),
        scratch_shapes=[pltpu.VMEM((8, 8, D_MODEL), F32), pltpu.SemaphoreType.DMA((7,)), pltpu.SemaphoreType.DMA((7,))],
    )(block)


def _adamw(w, g, m, v, name):
    rows, cols = w.shape
    tr = 128 if rows % 128 == 0 else rows

    def body(w_ref, g_ref, m_ref, v_ref, d_ref, nm_ref, nv_ref):
        gv = g_ref[...]
        nm = ADAM_B1 * m_ref[...] + (1.0 - ADAM_B1) * gv
        nv = ADAM_B2 * v_ref[...] + (1.0 - ADAM_B2) * (gv * gv)
        m_hat = nm / (1.0 - ADAM_B1 ** ADAM_STEP)
        v_hat = nv / (1.0 - ADAM_B2 ** ADAM_STEP)
        d_ref[...] = -ADAM_LR * (m_hat / (jnp.sqrt(v_hat) + ADAM_EPS) + ADAM_WD * w_ref[...])
        nm_ref[...] = nm
        nv_ref[...] = nv

    spec = pl.BlockSpec((tr, cols), lambda i: (i, 0))
    shape = jax.ShapeDtypeStruct((rows, cols), F32)
    return pl.pallas_call(
        body, name=name, grid=(rows // tr,), in_specs=[spec] * 4, out_specs=[spec] * 3, out_shape=[shape] * 3,
        compiler_params=_params(("parallel",)),
    )(w, g, m, v)


def kernel(x, norm_w, w_in, conv_w, a_log, dt_bias, gdn_norm_w, w_up_a, w_up_b, w_out, final_norm_w, loss_target, m_norm_w, m_w_in, m_conv_w, m_a_log, m_dt_bias, m_gdn_norm_w, m_w_up_a, m_w_up_b, m_w_out, m_final_norm_w, v_norm_w, v_w_in, v_conv_w, v_a_log, v_dt_bias, v_gdn_norm_w, v_w_up_a, v_w_up_b, v_w_out, v_final_norm_w):
    conv_hi = conv_w[0].astype(MXU_DTYPE)
    conv_lo = (conv_w[0] - conv_hi.astype(F32)).astype(MXU_DTYPE)
    big = w_in[0].astype(MXU_DTYPE).reshape(2, *BIG_HALF)
    slab = _pack_slab(w_up_a[0].astype(MXU_DTYPE), w_up_b[0].astype(MXU_DTYPE), w_out[0].astype(MXU_DTYPE), conv_hi, conv_lo)
    own_shard = 2 * lax.axis_index("x") + lax.axis_index("y")
    bigs, slabs = _gather_weights([big, slab])
    bigs = lax.dynamic_update_slice(bigs, big[None], (own_shard, 0, 0, 0)).reshape(SHARDS, D_MODEL, W_IN_SHARD)
    slabs = lax.dynamic_update_slice(slabs, slab[None], (own_shard, 0, 0, 0))
    parts = [_unpack_slab(slabs[s]) for s in range(SHARDS)]
    split = BA_END - (SHARDS - 1) * W_IN_SHARD
    wp = jnp.concatenate([bigs[s] for s in range(SHARDS - 1)]
                         + [bigs[-1][:, :split], jnp.zeros((D_MODEL, SEG_GA - BA_END), MXU_DTYPE), bigs[-1][:, split:]], axis=1)
    w_up_a_full = jnp.concatenate([p[0] for p in parts], axis=1)
    w_up_b_full = jnp.concatenate([p[1] for p in parts], axis=1)
    w_out_full = jnp.concatenate([p[2] for p in parts], axis=0)
    conv_full = jnp.concatenate([p[3].astype(F32) + p[4].astype(F32) for p in parts], axis=1)

    blocks, tags = (128, HALF_ROWS), ("w_in", "slab")

    def start_reduce(d_wp, d_w_up_a, d_w_up_b, d_w_out, d_conv_w):
        d_w_in = [d_wp[:, s * W_IN_SHARD:(s + 1) * W_IN_SHARD] for s in range(SHARDS - 1)]
        d_w_in.append(jnp.concatenate([d_wp[:, (SHARDS - 1) * W_IN_SHARD:BA_END], d_wp[:, SEG_GA:]], axis=1))
        zero_conv = jnp.zeros((GDN_CONV, CONV_SHARD), F32)
        grads = [jnp.stack(d_w_in).reshape(SHARDS, 2, *BIG_HALF),
                 jnp.stack([_pack_slab(d_w_up_a[:, s * 256:(s + 1) * 256], d_w_up_b[:, s * 256:(s + 1) * 256],
                                       d_w_out[s * 256:(s + 1) * 256], d_conv_w[:, s * CONV_SHARD:(s + 1) * CONV_SHARD],
                                       zero_conv) for s in range(SHARDS)])]
        *in_flight, token = _copies_start("exchange_start", _exchange_copies, 2 * SHARDS, grads,
                                          [(SHARDS, *gr.shape[2:]) for gr in grads])
        return in_flight, token

    def continue_reduce(in_flight, after):
        send_sems, recv_sems, *passed = in_flight
        arrived = _copies_wait("exchange_wait", _exchange_copies, send_sems, recv_sems, passed, after)
        grads, from_sibling = arrived[:2], arrived[2:]
        pairs = [_pair_sum(gr, fs, blk, f"pair_sum_{tag}") for gr, fs, blk, tag in zip(grads, from_sibling, blocks, tags)]
        *in_flight, token = _copies_start("scatter_start", _scatter_copies, 2 * 3, pairs, [(3, *p.shape[1:]) for p in pairs])
        return in_flight, token

    g = _local_step(x[0], loss_target[0], norm_w, wp, conv_full, a_log, dt_bias, gdn_norm_w,
                    w_up_a_full, w_up_b_full, w_out_full, final_norm_w[None], start_reduce, continue_reduce)

    send_sems, recv_sems, *passed = g["in_flight"]
    arrived = _copies_wait("scatter_wait", _scatter_copies, send_sems, recv_sems, passed, g["grad_x"])
    pairs, from_chips = arrived[:2], arrived[2:]
    total_big, total_slab = _share_total([_chip_sum(p, fc, blk, f"chip_sum_{tag}")
                                          for p, fc, blk, tag in zip(pairs, from_chips, blocks, tags)])
    g_w_in = total_big.reshape(D_MODEL, W_IN_SHARD)
    g_w_up_a, g_w_up_b, g_w_out, g_conv, _ = _unpack_slab(total_slab)

    row2 = jnp.concatenate([g["gdn_norm_w"], g["a_log"], g["dt_bias"], g["loss"][0:1, 0:1],
                            jnp.zeros((1, D_MODEL - HEAD_DIM - 2 * HEADS - 1), F32)], axis=1)
    small = _allreduce_small(jnp.concatenate([g["norm_w"], g["final_norm_w"], row2, jnp.zeros((5, D_MODEL), F32)], axis=0))
    g_norm, g_final = small[0:1], small[1]
    g_gnw, g_alog, g_dt = small[2:3, 0:HEAD_DIM], small[2:3, HEAD_DIM:HEAD_DIM + HEADS], small[2:3, HEAD_DIM + HEADS:HEAD_DIM + 2 * HEADS]
    loss = small[2, HEAD_DIM + 2 * HEADS]

    names = ["norm_w", "w_in", "conv_w", "a_log", "dt_bias", "gdn_norm_w", "w_up_a", "w_up_b", "w_out", "final_norm_w"]
    weights = dict(zip(names, (norm_w, w_in, conv_w, a_log, dt_bias, gdn_norm_w, w_up_a, w_up_b, w_out, final_norm_w)))
    ms = dict(zip(names, (m_norm_w, m_w_in, m_conv_w, m_a_log, m_dt_bias, m_gdn_norm_w, m_w_up_a, m_w_up_b, m_w_out, m_final_norm_w)))
    vs = dict(zip(names, (v_norm_w, v_w_in, v_conv_w, v_a_log, v_dt_bias, v_gdn_norm_w, v_w_up_a, v_w_up_b, v_w_out, v_final_norm_w)))
    grads2d = dict(norm_w=g_norm, w_in=g_w_in, conv_w=g_conv, a_log=g_alog, dt_bias=g_dt, gdn_norm_w=g_gnw,
                   w_up_a=g_w_up_a, w_up_b=g_w_up_b, w_out=g_w_out, final_norm_w=g_final[None])
    grad_out, delta, new_m, new_v = [], [], [], []
    for n in names:
        shape = weights[n].shape
        two_d = grads2d[n].shape
        d, nm, nv = _adamw(weights[n].reshape(two_d), grads2d[n], ms[n].reshape(two_d), vs[n].reshape(two_d), f"adamw_{n}")
        grad_out.append(grads2d[n].reshape(shape))
        delta.append(d.reshape(shape))
        new_m.append(nm.reshape(shape))
        new_v.append(nv.reshape(shape))
    return (loss, g["grad_x"][None], *grad_out, *delta, *new_m, *new_v)
```

```python
import functools

import jax
import jax.numpy as jnp
from jax import lax
from jax.experimental import pallas as pl
from jax.experimental.pallas import tpu as pltpu

F32 = jnp.float32
MXU_DTYPE = jnp.bfloat16
HIGHEST = lax.Precision.HIGHEST

D_MODEL = 1024
HEADS = 8
HEAD_DIM = 64
WIDTH = HEADS * HEAD_DIM
NORM_EPS = 1e-6
ROPE_THETA = 10000.0
ATT_BLOCK = 128
DILATIONS = (1, 4, 16)
GDN_CHUNK = 64
GDN_CONV = 4
IN_WIDTH = 9232
SEG_A, SEG_ZA, SEG_B, SEG_ZB, SEG_BA, SEG_GA, SEG_GB, PACKED_WIDTH = 0, 4608, 5120, 6656, 7168, 7680, 8704, 9728
BA_END = 7184
VMEM_LIMIT = 56 * 1024 * 1024

ADAM_LR, ADAM_B1, ADAM_B2, ADAM_EPS, ADAM_WD, ADAM_STEP = 0.001, 0.9, 0.999, 1e-08, 0.01, 10

_NN = (((1,), (0,)), ((), ()))
_NT = (((1,), (1,)), ((), ()))
_TN = (((0,), (0,)), ((), ()))


def _params(sem):
    return pltpu.CompilerParams(dimension_semantics=sem, vmem_limit_bytes=VMEM_LIMIT)


def _mxu(a, b, dims):
    return lax.dot_general(a.astype(MXU_DTYPE), b.astype(MXU_DTYPE), dims, preferred_element_type=F32)


def _sigmoid(x):
    return 1.0 / (1.0 + jnp.exp(-x))


def _softplus(x):
    return jnp.maximum(x, 0.0) + jnp.log(1.0 + jnp.exp(-jnp.abs(x)))


def _iota(shape, axis):
    return lax.broadcasted_iota(jnp.int32, shape, axis)


def _matmul(a, b, mode, name, out_dtype=F32, tm=512, tn=512, tk=512):
    if mode == "nn":
        (m, k), (k2, n) = a.shape, b.shape
    elif mode == "nt":
        (m, k), (n, k2) = a.shape, b.shape
    else:
        (k, m), (k2, n) = a.shape, b.shape
    assert k == k2
    tm, tn, tk = min(tm, m), min(tn, n), min(tk, k)
    assert m % tm == 0 and n % tn == 0 and k % tk == 0
    nk = k // tk
    dims = {"nn": _NN, "nt": _NT, "tn": _TN}[mode]

    assert out_dtype == F32

    def body(a_ref, b_ref, o_ref):
        kk = pl.program_id(2)
        part = _mxu(a_ref[...], b_ref[...], dims)

        @pl.when(kk == 0)
        def _():
            o_ref[...] = part

        @pl.when(kk > 0)
        def _():
            o_ref[...] += part

    a_spec = pl.BlockSpec((tk, tm), lambda i, j, kk: (kk, i)) if mode == "tn" else pl.BlockSpec((tm, tk), lambda i, j, kk: (i, kk))
    b_spec = pl.BlockSpec((tn, tk), lambda i, j, kk: (j, kk)) if mode == "nt" else pl.BlockSpec((tk, tn), lambda i, j, kk: (kk, j))
    return pl.pallas_call(
        body, name=name, grid=(m // tm, n // tn, nk), in_specs=[a_spec, b_spec],
        out_specs=pl.BlockSpec((tm, tn), lambda i, j, kk: (i, j)),
        out_shape=jax.ShapeDtypeStruct((m, n), out_dtype),
        compiler_params=_params(("parallel", "parallel", "arbitrary")),
    )(a, b)


def _norm_proj(x, norm_w, wp, tm=1024, tn=PACKED_WIDTH // 4):
    t = x.shape[0]
    tm = min(tm, t)

    def body(x_ref, nw_ref, w_ref, proj_ref, ht_ref, h_scr):
        @pl.when(pl.program_id(1) == 0)
        def _():
            xf = x_ref[...]
            r = lax.rsqrt(jnp.mean(xf * xf, axis=-1, keepdims=True) + NORM_EPS)
            h = xf * r * nw_ref[...]
            h_scr[...] = h.astype(h_scr.dtype)
            ht_ref[...] = h.T.astype(ht_ref.dtype)

        proj_ref[...] = jnp.dot(h_scr[...], w_ref[...], preferred_element_type=F32)

    return pl.pallas_call(
        body, name="norm_proj", grid=(t // tm, PACKED_WIDTH // tn),
        in_specs=[pl.BlockSpec((tm, D_MODEL), lambda i, j: (i, 0)),
                  pl.BlockSpec((1, D_MODEL), lambda i, j: (0, 0)),
                  pl.BlockSpec((D_MODEL, tn), lambda i, j: (0, j))],
        out_specs=[pl.BlockSpec((tm, tn), lambda i, j: (i, j)),
                   pl.BlockSpec((D_MODEL, tm), lambda i, j: (0, i))],
        out_shape=[jax.ShapeDtypeStruct((t, PACKED_WIDTH), F32), jax.ShapeDtypeStruct((D_MODEL, t), MXU_DTYPE)],
        scratch_shapes=[pltpu.VMEM((tm, D_MODEL), MXU_DTYPE)],
        compiler_params=_params(("parallel", "arbitrary")),
    )(x, norm_w, wp)


def _rope_tables(t):
    lane = jnp.arange(128)
    inv_freq = ROPE_THETA ** (-jnp.arange(0, HEAD_DIM, 2, dtype=F32) / HEAD_DIM)
    freq = jnp.concatenate([inv_freq] * 4)
    coarse = (jnp.arange(t // 128, dtype=F32) * 128.0)[:, None] * freq[None, :]
    fine = jnp.arange(128, dtype=F32)[:, None] * freq[None, :]
    ca, sa, cb, sb = jnp.cos(coarse)[:, None, :], jnp.sin(coarse)[:, None, :], jnp.cos(fine)[None], jnp.sin(fine)[None]
    cos = (ca * cb - sa * sb).reshape(t, 128)
    sin = (sa * cb + ca * sb).reshape(t, 128)
    first_half = (lane % HEAD_DIM) < HEAD_DIM // 2
    return cos, jnp.where(first_half, -sin, 0.0), jnp.where(first_half, 0.0, sin)


def _rope_cols(x, cos, sin_lo, sin_hi, sign):
    outs = []
    for c in range(x.shape[1] // 128):
        xc = x[:, c * 128:(c + 1) * 128]
        rot = pltpu.roll(xc, 96, 1) * sin_lo + pltpu.roll(xc, 32, 1) * sin_hi
        outs.append(xc * cos + sign * rot)
    return jnp.concatenate(outs, axis=1)


def _rope_block(x, cos, sin_lo, sin_hi, sign, q_scale=None):
    q = _rope_cols(x[:, :WIDTH], cos, sin_lo, sin_hi, sign)
    k = _rope_cols(x[:, WIDTH:2 * WIDTH], cos, sin_lo, sin_hi, sign)
    return jnp.concatenate([q if q_scale is None else q * q_scale, k, x[:, 2 * WIDTH:]], axis=1)


def _tile_scratch(tm, cols):
    return pltpu.VMEM((cols // 128, tm, 128), F32)


def _store_tile(scr, y):
    for c in range(scr.shape[0]):
        scr[c] = y[:, c * 128:(c + 1) * 128]


def _load_tile(scr):
    return jnp.concatenate([scr[c] for c in range(scr.shape[0])], axis=1)


def _to_strided_view(scr, o_ref, d, width=None, col0=0):
    n, tm, _ = scr.shape
    width = n * 128 if width is None else width
    for r in range(d):
        for c in range(n):
            at = r * width + col0 + c * 128
            o_ref[:, at:at + 128] = scr[c, pl.ds(r, tm // d, stride=d), :].astype(o_ref.dtype)


def _from_strided_view(i_ref, scr, d):
    n, tm, _ = scr.shape
    for r in range(d):
        for c in range(n):
            scr[c, pl.ds(r, tm // d, stride=d), :] = i_ref[:, (r * n + c) * 128:(r * n + c + 1) * 128].astype(F32)


def _strided_spec(tm, d, cols):
    return pl.BlockSpec((tm // d, d * cols), lambda i: (i, 0))


def _rope_fwd(proj, tables, tm=512):
    t = proj.shape[0]
    cols = 3 * WIDTH

    def body(x_ref, c_ref, sl_ref, sh_ref, o0, o1, o2, scr):
        for g, (d, o_ref) in enumerate(zip(DILATIONS, (o0, o1, o2))):
            y = _rope_block(x_ref[:, g * cols:(g + 1) * cols], c_ref[...], sl_ref[...], sh_ref[...], 1.0, HEAD_DIM ** -0.5)
            if d == 1:
                o_ref[...] = y.astype(o_ref.dtype)
            else:
                _store_tile(scr, y)
                _to_strided_view(scr, o_ref, d)

    tab = pl.BlockSpec((tm, 128), lambda i: (i, 0))
    return pl.pallas_call(
        body, name="rope_fwd", grid=(t // tm,),
        in_specs=[pl.BlockSpec((tm, 3 * cols), lambda i: (i, 0)), tab, tab, tab],
        out_specs=[_strided_spec(tm, d, cols) for d in DILATIONS],
        out_shape=[jax.ShapeDtypeStruct((t // d, d * cols), MXU_DTYPE) for d in DILATIONS],
        scratch_shapes=[_tile_scratch(tm, cols)],
        compiler_params=_params(("parallel",)),
    )(proj, *tables)


def _rope_bwd(dproj, dqs, dkvs, tables, tm=512):
    t = dproj.shape[0]
    cols = 3 * WIDTH

    def body(dp_ref, q0, q1, q2, kv0, kv1, kv2, c_ref, sl_ref, sh_ref, o_ref, scr_q, scr_kv):
        for g, (d, q_ref, kv_ref) in enumerate(zip(DILATIONS, (q0, q1, q2), (kv0, kv1, kv2))):
            if d == 1:
                x = jnp.concatenate([q_ref[...], kv_ref[...]], axis=1)
            else:
                _from_strided_view(q_ref, scr_q, d)
                _from_strided_view(kv_ref, scr_kv, d)
                x = jnp.concatenate([_load_tile(scr_q), _load_tile(scr_kv)], axis=1)
            y = _rope_block(x, c_ref[...], sl_ref[...], sh_ref[...], -1.0)
            o_ref[:, g * cols:(g + 1) * cols] = y.astype(o_ref.dtype)

    tab = pl.BlockSpec((tm, 128), lambda i: (i, 0))
    return pl.pallas_call(
        body, name="rope_bwd", grid=(t // tm,),
        in_specs=[pl.BlockSpec(memory_space=pl.ANY)] + [_strided_spec(tm, d, WIDTH) for d in DILATIONS]
        + [_strided_spec(tm, d, 2 * WIDTH) for d in DILATIONS] + [tab, tab, tab],
        out_specs=pl.BlockSpec((tm, 3 * cols), lambda i: (i, 0)),
        out_shape=jax.ShapeDtypeStruct((t, PACKED_WIDTH), MXU_DTYPE),
        scratch_shapes=[_tile_scratch(tm, WIDTH), _tile_scratch(tm, 2 * WIDTH)],
        input_output_aliases={0: 0},
        compiler_params=_params(("parallel",)),
    )(dproj, *dqs, *dkvs, *tables)


def _att_masks():
    qi = _iota((ATT_BLOCK, ATT_BLOCK), 0)
    kj = _iota((ATT_BLOCK, ATT_BLOCK), 1)
    return kj <= qi, kj >= qi


def _att_fwd(qkv, d, name):
    rows = qkv.shape[0]
    nb = rows // ATT_BLOCK

    def body(q_ref, kc_ref, kp_ref, vc_ref, vp_ref, o_ref, lse_ref):
        m_cur, m_prev = _att_masks()
        hs = range(HEADS)
        sls = [slice(h * HEAD_DIM, (h + 1) * HEAD_DIM) for h in hs]

        def attend(with_prev):
            qs = [q_ref[:, sl] for sl in sls]
            s_c = [jnp.where(m_cur, _mxu(qs[h], kc_ref[:, sls[h]], _NT), -jnp.inf) for h in hs]
            if with_prev:
                s_p = [jnp.where(m_prev, _mxu(qs[h], kp_ref[:, sls[h]], _NT), -jnp.inf) for h in hs]
                m = [jnp.max(jnp.maximum(s_c[h], s_p[h]), axis=1, keepdims=True) for h in hs]
                p_c = [jnp.exp(s_c[h] - m[h]) for h in hs]
                p_p = [jnp.exp(s_p[h] - m[h]) for h in hs]
                den = [jnp.sum(p_c[h] + p_p[h], axis=1, keepdims=True) for h in hs]
                o = [_mxu(p_c[h], vc_ref[:, sls[h]], _NN) + _mxu(p_p[h], vp_ref[:, sls[h]], _NN) for h in hs]
            else:
                m = [jnp.max(s_c[h], axis=1, keepdims=True) for h in hs]
                p_c = [jnp.exp(s_c[h] - m[h]) for h in hs]
                den = [jnp.sum(p_c[h], axis=1, keepdims=True) for h in hs]
                o = [_mxu(p_c[h], vc_ref[:, sls[h]], _NN) for h in hs]
            lane = _iota((1, 128), 1)
            lse = jnp.zeros((ATT_BLOCK, 128), F32)
            for h in hs:
                o_ref[:, sls[h]] = o[h] / den[h]
                lse = jnp.where(lane == h, m[h] + jnp.log(den[h]), lse)
            lse_ref[...] = lse

        @pl.when(pl.program_id(1) > 0)
        def _():
            attend(True)

        @pl.when(pl.program_id(1) == 0)
        def _():
            attend(False)

    def cur(c):
        return pl.BlockSpec((ATT_BLOCK, WIDTH), lambda r, i: (i, 3 * r + c))

    def prev(c):
        return pl.BlockSpec((ATT_BLOCK, WIDTH), lambda r, i: (jnp.maximum(i - 1, 0), 3 * r + c))

    return pl.pallas_call(
        body, name=name, grid=(d, nb), in_specs=[cur(0), cur(1), prev(1), cur(2), prev(2)],
        out_specs=[pl.BlockSpec((ATT_BLOCK, WIDTH), lambda r, i: (i, r)), pl.BlockSpec((ATT_BLOCK, 128), lambda r, i: (i, r))],
        out_shape=[jax.ShapeDtypeStruct((rows, d * WIDTH), F32), jax.ShapeDtypeStruct((rows, d * 128), F32)],
        compiler_params=_params(("parallel", "arbitrary")),
    )(qkv, qkv, qkv, qkv, qkv)


def _att_bwd(qkv, do, lse, delta, d, name):
    rows = qkv.shape[0]
    nb = rows // ATT_BLOCK
    total = d * nb
    scale = HEAD_DIM ** -0.5

    def body(q_ref, kc_ref, kp_ref, vc_ref, vp_ref, do_ref, l_ref, dl_ref, dq_ref, dkv_ref, own):
        n = pl.program_id(0)
        i = n % nb
        m_cur, m_prev = _att_masks()

        hs = range(HEADS)
        sls = [slice(h * HEAD_DIM, (h + 1) * HEAD_DIM) for h in hs]
        col = [slice(h, h + 1) for h in hs]

        def probs(k_r, mask):
            s = [_mxu(q_ref[:, sls[h]], k_r[:, sls[h]], _NT) for h in hs]
            return [jnp.where(mask, jnp.exp(s[h] - l_ref[:, col[h]]), 0.0) for h in hs]

        def dscores(p, v_r):
            dp = [_mxu(do_ref[:, sls[h]], v_r[:, sls[h]], _NT) for h in hs]
            return [(p[h] * (dp[h] - dl_ref[:, col[h]])).astype(MXU_DTYPE) for h in hs]

        def step(with_prev):
            p_c = probs(kc_ref, m_cur)
            ds_c = dscores(p_c, vc_ref)
            dq = [_mxu(ds_c[h], kc_ref[:, sls[h]], _NN) for h in hs]
            if with_prev:
                p_p = probs(kp_ref, m_prev)
                ds_p = dscores(p_p, vp_ref)
                dq = [dq[h] + _mxu(ds_p[h], kp_ref[:, sls[h]], _NN) for h in hs]
                dk_p = [_mxu(ds_p[h], q_ref[:, sls[h]], _TN) for h in hs]
                dv_p = [_mxu(p_p[h], do_ref[:, sls[h]], _TN) for h in hs]
            dk_c = [_mxu(ds_c[h], q_ref[:, sls[h]], _TN) for h in hs]
            dv_c = [_mxu(p_c[h], do_ref[:, sls[h]], _TN) for h in hs]
            for h in hs:
                vs = slice(WIDTH + h * HEAD_DIM, WIDTH + (h + 1) * HEAD_DIM)
                dq_ref[:, sls[h]] = dq[h] * scale
                if with_prev:
                    dkv_ref[:, sls[h]] = own[:, sls[h]] + dk_p[h]
                    dkv_ref[:, vs] = own[:, vs] + dv_p[h]
            if not with_prev:
                dkv_ref[...] = own[...]
            for h in hs:
                vs = slice(WIDTH + h * HEAD_DIM, WIDTH + (h + 1) * HEAD_DIM)
                own[:, sls[h]] = dk_c[h]
                own[:, vs] = dv_c[h]

        @pl.when(n == 0)
        def _():
            own[...] = jnp.zeros_like(own)

        @pl.when((i > 0) & (n < total))
        def _():
            step(True)

        @pl.when((i == 0) & (n < total))
        def _():
            step(False)

        @pl.when(n == total)
        def _():
            dkv_ref[...] = own[...]

    def at(n, shift):
        n = jnp.minimum(n, total - 1)
        return jnp.clip(n % nb + shift, 0, nb - 1), n // nb

    def qkv_spec(c, shift):
        def index(n):
            i, r = at(n, shift)
            return i, 3 * r + c
        return pl.BlockSpec((ATT_BLOCK, WIDTH), index)

    tok = pl.BlockSpec((ATT_BLOCK, WIDTH), lambda n: at(n, 0))
    per_head = pl.BlockSpec((ATT_BLOCK, 128), lambda n: at(n, 0))
    return pl.pallas_call(
        body, name=name, grid=(total + 1,),
        in_specs=[qkv_spec(0, 0), qkv_spec(1, 0), qkv_spec(1, -1), qkv_spec(2, 0), qkv_spec(2, -1), tok, per_head, per_head],
        out_specs=[tok, pl.BlockSpec((ATT_BLOCK, 2 * WIDTH), lambda n: at(jnp.maximum(n - 1, 0), 0))],
        out_shape=[jax.ShapeDtypeStruct((rows, d * WIDTH), F32), jax.ShapeDtypeStruct((rows, d * 2 * WIDTH), F32)],
        scratch_shapes=[pltpu.VMEM((ATT_BLOCK, 2 * WIDTH), F32)],
        compiler_params=_params(("arbitrary",)),
    )(qkv, qkv, qkv, qkv, qkv, do, lse, delta)


def _att_merge(os_, lses, proj, tm=1024):
    t = proj.shape[0]

    def body(o0, o1, o2, l0, l1, l2, z_ref, oz_ref, o_ref, t0, t1, t2, s_o1, s_o2, s_l1, s_l2, s_t):
        _from_strided_view(o1, s_o1, DILATIONS[1])
        _from_strided_view(o2, s_o2, DILATIONS[2])
        _from_strided_view(l1, s_l1, DILATIONS[1])
        _from_strided_view(l2, s_l2, DILATIONS[2])
        a, b, c = l0[...], s_l1[0], s_l2[0]
        m = jnp.maximum(jnp.maximum(a, b), c)
        wa, wb, wc = jnp.exp(a - m), jnp.exp(b - m), jnp.exp(c - m)
        den = wa + wb + wc
        total = m + jnp.log(den)
        t0[...] = total
        s_t[0] = total
        _to_strided_view(s_t, t1, DILATIONS[1])
        _to_strided_view(s_t, t2, DILATIONS[2])
        spread = jnp.where(_iota((128, WIDTH), 1) // HEAD_DIM == _iota((128, WIDTH), 0), 1.0, 0.0).astype(MXU_DTYPE)
        ra, rb, rc = (_pair_sum_lanes(w / den, spread) for w in (wa, wb, wc))
        o = ra * o0[...] + rb * _load_tile(s_o1) + rc * _load_tile(s_o2)
        z = z_ref[...]
        o_ref[...] = o
        oz_ref[...] = (o * z * _sigmoid(z)).astype(oz_ref.dtype)

    tok = pl.BlockSpec((tm, WIDTH), lambda i: (i, 0))
    views = [_strided_spec(tm, d, WIDTH) for d in DILATIONS]
    per_head = [_strided_spec(tm, d, 128) for d in DILATIONS]
    return pl.pallas_call(
        body, name="att_merge", grid=(t // tm,),
        in_specs=views + per_head + [pl.BlockSpec((tm, WIDTH), lambda i: (i, SEG_ZA // WIDTH))],
        out_specs=[tok, tok] + per_head,
        out_shape=[jax.ShapeDtypeStruct((t, WIDTH), MXU_DTYPE), jax.ShapeDtypeStruct((t, WIDTH), F32)]
        + [jax.ShapeDtypeStruct((t // d, d * 128), F32) for d in DILATIONS],
        scratch_shapes=[_tile_scratch(tm, WIDTH)] * 2 + [_tile_scratch(tm, 128)] * 3,
        compiler_params=_params(("parallel",)),
    )(*os_, *lses, proj)


def _att_merge_bwd(dproj, dy, w_up, o, proj, tm=1024):
    t = proj.shape[0]

    def body(dp_ref, dy_ref, wu_ref, o_ref, z_ref, dz_ref, do0, do1, do2, dl0, dl1, dl2, s_do, s_dl):
        z, ov, g = z_ref[...], o_ref[...], _mxu(dy_ref[...], wu_ref[...], _NT)
        sg = _sigmoid(z)
        do = g * z * sg
        dz_ref[...] = (g * ov * sg * (1.0 + z * (1.0 - sg))).astype(dz_ref.dtype)
        do0[...] = do.astype(do0.dtype)
        _store_tile(s_do, do)
        prod = do * ov
        delta = jnp.zeros((tm, 128), F32)
        for p in range(HEADS // 2):
            delta += _pair_sum_lanes(prod[:, p * 128:(p + 1) * 128], _pair_to_lanes(p, 0))
        dl0[...] = delta
        s_dl[0] = delta
        for d, do_v, dl_v in ((DILATIONS[1], do1, dl1), (DILATIONS[2], do2, dl2)):
            _to_strided_view(s_do, do_v, d)
            _to_strided_view(s_dl, dl_v, d)

    tok = pl.BlockSpec((tm, WIDTH), lambda i: (i, 0))
    seg = pl.BlockSpec((tm, WIDTH), lambda i: (i, SEG_ZA // WIDTH))
    views = [_strided_spec(tm, d, WIDTH) for d in DILATIONS]
    per_head = [_strided_spec(tm, d, 128) for d in DILATIONS]
    return pl.pallas_call(
        body, name="att_merge_bwd", grid=(t // tm,),
        in_specs=[pl.BlockSpec(memory_space=pl.ANY), pl.BlockSpec((tm, D_MODEL), lambda i: (i, 0)),
                  pl.BlockSpec((WIDTH, D_MODEL), lambda i: (0, 0)), tok, seg],
        out_specs=[seg] + views + per_head,
        out_shape=[jax.ShapeDtypeStruct((t, PACKED_WIDTH), MXU_DTYPE)]
        + [jax.ShapeDtypeStruct((t // d, d * WIDTH), MXU_DTYPE) for d in DILATIONS]
        + [jax.ShapeDtypeStruct((t // d, d * 128), F32) for d in DILATIONS],
        scratch_shapes=[_tile_scratch(tm, WIDTH), _tile_scratch(tm, 128)],
        input_output_aliases={0: 0},
        compiler_params=_params(("parallel",)),
    )(dproj, dy, w_up, o, proj)


def _shift_down(x, halo, s):
    if s == 0:
        return x
    xs = pltpu.roll(x, s, 0)
    head = jnp.where(_iota((8, x.shape[1]), 0) < s, pltpu.roll(halo, s, 0), xs[0:8])
    return jnp.concatenate([head, xs[8:]], axis=0)


def _shift_up(x, nxt, s):
    if s == 0:
        return x
    n = x.shape[0]
    xs = pltpu.roll(x, n - s, 0)
    tail = jnp.where(_iota((8, x.shape[1]), 0) >= 8 - s, pltpu.roll(nxt, 8 - s, 0), xs[n - 8:])
    return jnp.concatenate([xs[:n - 8], tail], axis=0)


def _conv_fwd(proj, conv_w, tm=1024):
    t = proj.shape[0]
    cb = SEG_B // WIDTH

    def body(x_ref, halo_ref, w_ref, c_ref):
        halo = jnp.where(pl.program_id(0) > 0, halo_ref[...], 0.0)
        x = x_ref[...]
        w = w_ref[...]
        acc = jnp.zeros((tm, WIDTH), F32)
        for j in range(GDN_CONV):
            acc += _shift_down(x, halo, GDN_CONV - 1 - j) * w[j:j + 1, :]
        c_ref[...] = acc

    return pl.pallas_call(
        body, name="conv_fwd", grid=(t // tm, 3),
        in_specs=[pl.BlockSpec((tm, WIDTH), lambda i, c: (i, cb + c)),
                  pl.BlockSpec((8, WIDTH), lambda i, c: (jnp.maximum(i * (tm // 8) - 1, 0), cb + c)),
                  pl.BlockSpec((GDN_CONV, WIDTH), lambda i, c: (0, c))],
        out_specs=pl.BlockSpec((tm, WIDTH), lambda i, c: (i, c)),
        out_shape=jax.ShapeDtypeStruct((t, 3 * WIDTH), F32),
        compiler_params=_params(("parallel", "parallel")),
    )(proj, proj, conv_w)


def _conv_bwd(dproj, dc, proj, conv_w, tm=1024):
    t = proj.shape[0]
    cb = SEG_B // WIDTH
    nt = t // tm

    def body(dp_ref, dc_ref, dcn_ref, x_ref, halo_ref, w_ref, dx_ref, dw_ref):
        i = pl.program_id(1)
        w = w_ref[...]
        dcn = jnp.where(i < nt - 1, dcn_ref[...], 0.0)
        dcv = dc_ref[...]
        acc = jnp.zeros((tm, WIDTH), F32)
        for j in range(GDN_CONV):
            acc += _shift_up(dcv, dcn, GDN_CONV - 1 - j) * w[j:j + 1, :]
        dx_ref[...] = acc.astype(dx_ref.dtype)
        halo = jnp.where(i > 0, halo_ref[...], 0.0)
        x = x_ref[...]
        row8 = _iota((8, WIDTH), 0)
        part = jnp.zeros((8, WIDTH), F32)
        for j in range(GDN_CONV):
            s = jnp.sum(dcv * _shift_down(x, halo, GDN_CONV - 1 - j), axis=0, keepdims=True)
            part += jnp.where(row8 == j, s, 0.0)

        @pl.when(i == 0)
        def _():
            dw_ref[...] = jnp.zeros_like(dw_ref)

        dw_ref[...] += part

    return pl.pallas_call(
        body, name="conv_bwd", grid=(3, nt),
        in_specs=[pl.BlockSpec(memory_space=pl.ANY),
                  pl.BlockSpec((tm, WIDTH), lambda c, i: (i, c)),
                  pl.BlockSpec((8, WIDTH), lambda c, i: (jnp.minimum((i + 1) * (tm // 8), t // 8 - 1), c)),
                  pl.BlockSpec((tm, WIDTH), lambda c, i: (i, cb + c)),
                  pl.BlockSpec((8, WIDTH), lambda c, i: (jnp.maximum(i * (tm // 8) - 1, 0), cb + c)),
                  pl.BlockSpec((GDN_CONV, WIDTH), lambda c, i: (0, c))],
        out_specs=[pl.BlockSpec((tm, WIDTH), lambda c, i: (i, cb + c)),
                   pl.BlockSpec((8, WIDTH), lambda c, i: (0, c))],
        out_shape=[jax.ShapeDtypeStruct((t, PACKED_WIDTH), MXU_DTYPE), jax.ShapeDtypeStruct((8, 3 * WIDTH), F32)],
        input_output_aliases={0: 0},
        compiler_params=_params(("parallel", "arbitrary")),
    )(dproj, dc, dc, proj, proj, conv_w)


def _chunk_matrices(tm):
    r, c = _iota((tm, tm), 0), _iota((tm, tm), 1)
    same = (r // GDN_CHUNK) == (c // GDN_CHUNK)
    return jnp.where(same & (c <= r), 1.0, 0.0), jnp.where(same, 1.0, 0.0)


def _gdn_gates(ba, a_log, dt_bias):
    al = ba + dt_bias
    return _sigmoid(ba), -jnp.exp(a_log) * _softplus(al), _sigmoid(al)


SCAN_HEADS = (0, 2, 4, 6, 1, 3, 5, 7)


def _head_lane_eye():
    return jnp.where(_iota((HEADS, 128), 1) == _iota((HEADS, 128), 0) + HEADS, 1.0, 0.0)


def _pair_ones():
    return jnp.where(_iota((128, 128), 0) // HEAD_DIM == _iota((128, 128), 1) // HEAD_DIM, 1.0, 0.0).astype(MXU_DTYPE)


def _pair_to_lanes(p, base):
    return jnp.where(_iota((128, 128), 1) == base + 2 * p + _iota((128, 128), 0) // HEAD_DIM, 1.0, 0.0).astype(MXU_DTYPE)


def _pair_sum_lanes(x, sel):
    hi, lo = _split(x)
    return jnp.dot(hi, sel, preferred_element_type=F32) + jnp.dot(lo, sel, preferred_element_type=F32)


def _gdn_prep(conv, proj, a_log, dt_bias, tm=512):
    t = proj.shape[0]
    nc = tm // GDN_CHUNK

    def body(c_ref, ba_ref, al_ref, dt_ref, q_ref, k_ref, v_ref, b_ref, g_ref, gl_ref, grow_ref):
        beta, g, _ = _gdn_gates(ba_ref[:, 0:128], al_ref[...], dt_ref[...])
        lmat, cmat = _chunk_matrices(tm)
        gc = jnp.dot(lmat, g, precision=HIGHEST, preferred_element_type=F32)
        gl = jnp.dot(cmat, g, precision=HIGHEST, preferred_element_type=F32)
        grow = lax.dot_general(_head_lane_eye(), gc, _NT, precision=HIGHEST, preferred_element_type=F32)
        ones = _pair_ones()
        first = _iota((1, 128), 1) < HEAD_DIM

        def spread(x, base, p):
            return jnp.where(first, x[:, base + 2 * p:base + 2 * p + 1], x[:, base + 2 * p + 1:base + 2 * p + 2])

        for p in range(HEADS // 2):
            for seg, ref, scale in ((0, q_ref, HEAD_DIM ** -0.5), (1, k_ref, 1.0), (2, v_ref, None)):
                c = c_ref[:, seg * WIDTH + p * 128:seg * WIDTH + (p + 1) * 128]
                a = c * _sigmoid(c)
                if scale is not None:
                    a = a * (lax.rsqrt(_pair_sum_lanes(a * a, ones) + NORM_EPS) * scale)
                ref[p] = a
            b_ref[p] = spread(beta, 0, p)
            g_ref[p] = spread(gc, HEADS, p)
            gl_ref[p] = spread(gl, HEADS, p)
        for pos, h in enumerate(SCAN_HEADS):
            for cc in range(nc):
                grow_ref[pos, cc] = grow[h:h + 1, cc * GDN_CHUNK:(cc + 1) * GDN_CHUNK]

    hm = pl.BlockSpec((HEADS // 2, tm, 128), lambda i: (0, i, 0))
    small = pl.BlockSpec((1, 128), lambda i: (0, 0))
    hm_shape = jax.ShapeDtypeStruct((HEADS // 2, t, 128), F32)
    return pl.pallas_call(
        body, name="gdn_prep", grid=(t // tm,),
        in_specs=[pl.BlockSpec((tm, 3 * WIDTH), lambda i: (i, 0)),
                  pl.BlockSpec((tm, WIDTH), lambda i: (i, SEG_BA // WIDTH)), small, small],
        out_specs=[hm] * 6 + [pl.BlockSpec((HEADS, nc, 1, GDN_CHUNK), lambda i: (0, i, 0, 0))],
        out_shape=[hm_shape] * 6 + [jax.ShapeDtypeStruct((HEADS, t // GDN_CHUNK, 1, GDN_CHUNK), F32)],
        compiler_params=_params(("parallel",)),
    )(conv, proj, a_log, dt_bias)


def _gdn_prep_bwd(dproj, conv, proj, a_log, dt_bias, dq, dk, dv, db, dg, dgl, dgrow, tm=512):
    t = proj.shape[0]
    nc = tm // GDN_CHUNK

    def body(dp_ref, c_ref, ba_ref, al_ref, dt_ref, dq_ref, dk_ref, dv_ref, db_ref, dg_ref, dgl_ref, dgrow_ref,
             dba_ref, dc_ref, small_ref, row_scr):
        beta, g, sig_al = _gdn_gates(ba_ref[:, 0:128], al_ref[...], dt_ref[...])
        d_beta = jnp.zeros((tm, 128), F32)
        d_gc = jnp.zeros((tm, 128), F32)
        d_gl = jnp.zeros((tm, 128), F32)
        ones = _pair_ones()
        for pos, h in enumerate(SCAN_HEADS):
            for cc in range(nc):
                row_scr[h:h + 1, cc * GDN_CHUNK:(cc + 1) * GDN_CHUNK] = dgrow_ref[pos, cc]
        for p in range(HEADS // 2):
            d_beta += _pair_sum_lanes(db_ref[p], _pair_to_lanes(p, 0))
            d_gc += _pair_sum_lanes(dg_ref[p], _pair_to_lanes(p, HEADS))
            d_gl += _pair_sum_lanes(dgl_ref[p], _pair_to_lanes(p, HEADS))
            for seg, ref, scale in ((0, dq_ref, HEAD_DIM ** -0.5), (1, dk_ref, 1.0), (2, dv_ref, None)):
                cols = slice(seg * WIDTH + p * 128, seg * WIDTH + (p + 1) * 128)
                c = c_ref[:, cols]
                sg = _sigmoid(c)
                da = ref[p]
                if scale is not None:
                    a = c * sg
                    r = lax.rsqrt(_pair_sum_lanes(a * a, ones) + NORM_EPS)
                    da = da * scale
                    da = r * da - a * (r * r * r) * _pair_sum_lanes(da * a, ones)
                dc_ref[:, cols] = da * (sg * (1.0 + c * (1.0 - sg)))
        d_gc += lax.dot_general(row_scr[...], _head_lane_eye(), _TN, precision=HIGHEST, preferred_element_type=F32)
        lmat, cmat = _chunk_matrices(tm)
        d_g = (lax.dot_general(lmat, d_gc, _TN, precision=HIGHEST, preferred_element_type=F32)
               + lax.dot_general(cmat, d_gl, _TN, precision=HIGHEST, preferred_element_type=F32))
        d_al = d_g * (-jnp.exp(al_ref[...])) * sig_al
        d_bl = d_beta * beta * (1.0 - beta)
        dba_ref[...] = jnp.concatenate([d_bl + d_al, jnp.zeros((tm, WIDTH - 128), F32)], axis=1).astype(dba_ref.dtype)
        row8 = _iota((8, 128), 0)
        part = (jnp.where(row8 == 0, jnp.sum(d_g * g, axis=0, keepdims=True), 0.0)
                + jnp.where(row8 == 1, jnp.sum(d_al, axis=0, keepdims=True), 0.0))

        @pl.when(pl.program_id(0) == 0)
        def _():
            small_ref[...] = jnp.zeros_like(small_ref)

        small_ref[...] += part

    hm = pl.BlockSpec((HEADS // 2, tm, 128), lambda i: (0, i, 0))
    small = pl.BlockSpec((1, 128), lambda i: (0, 0))
    seg = pl.BlockSpec((tm, WIDTH), lambda i: (i, SEG_BA // WIDTH))
    return pl.pallas_call(
        body, name="gdn_prep_bwd", grid=(t // tm,),
        in_specs=[pl.BlockSpec(memory_space=pl.ANY), pl.BlockSpec((tm, 3 * WIDTH), lambda i: (i, 0)), seg, small, small]
        + [hm] * 6 + [pl.BlockSpec((HEADS, nc, 1, GDN_CHUNK), lambda i: (0, i, 0, 0))],
        out_specs=[seg, pl.BlockSpec((tm, 3 * WIDTH), lambda i: (i, 0)), pl.BlockSpec((8, 128), lambda i: (0, 0))],
        out_shape=[jax.ShapeDtypeStruct((t, PACKED_WIDTH), MXU_DTYPE), jax.ShapeDtypeStruct((t, 3 * WIDTH), F32),
                   jax.ShapeDtypeStruct((8, 128), F32)],
        scratch_shapes=[pltpu.VMEM((HEADS, tm), F32)],
        input_output_aliases={0: 0},
        compiler_params=_params(("arbitrary",)),
    )(dproj, conv, proj, a_log, dt_bias, dq, dk, dv, db, dg, dgl, dgrow)


_BNN = (((2,), (1,)), ((0,), (0,)))
_BNT = (((2,), (2,)), ((0,), (0,)))
_BTN = (((1,), (1,)), ((0,), (0,)))


@jax.custom_vjp
def _MM_NN(a, b):
    return _mxu(a, b, _BNN)


@jax.custom_vjp
def _MM_NT(a, b):
    return _mxu(a, b, _BNT)


@jax.custom_vjp
def _MM_TN(a, b):
    return _mxu(a, b, _BTN)


_MM_NN.defvjp(lambda a, b: (_mxu(a, b, _BNN), (a, b)), lambda r, g: (_mxu(g, r[1], _BNT), _mxu(r[0], g, _BTN)))
_MM_NT.defvjp(lambda a, b: (_mxu(a, b, _BNT), (a, b)), lambda r, g: (_mxu(g, r[1], _BNN), _mxu(g, r[0], _BTN)))
_MM_TN.defvjp(lambda a, b: (_mxu(a, b, _BTN), (a, b)), lambda r, g: (_mxu(r[1], g, _BNT), _mxu(r[0], g, _BNN)))


def _split(a):
    hi = a.astype(MXU_DTYPE)
    return hi, (a - hi.astype(F32)).astype(MXU_DTYPE)


def _dot3(a, b, dims):
    (ah, al), (bh, bl) = a, b
    (ca,), (cb,) = dims[0]
    return lax.dot_general(jnp.concatenate([ah, ah, al], axis=ca), jnp.concatenate([bh, bl, bh], axis=cb), dims,
                           preferred_element_type=F32)


def _unit_lower_inverse(a):
    c = GDN_CHUNK
    eye = jnp.where(_iota((c, c), 0) == _iota((c, c), 1), 1.0, 0.0)
    x = eye - a
    p = a
    for _ in range(5):
        ps = _split(p)
        p = _dot3(ps, ps, _BNN)
        x = x + _dot3(_split(x), _split(p), _BNN)
    return x


@jax.custom_vjp
def _SAVED_INVERSE(a, t_inv):
    return t_inv


def _saved_inverse_bwd(t_inv, g):
    ts = _split(t_inv)
    return -_dot3(ts, _split(_dot3(_split(g), ts, _BNT)), _BTN), jnp.zeros_like(t_inv)


_SAVED_INVERSE.defvjp(lambda a, t_inv: (t_inv, t_inv), _saved_inverse_bwd)


def _gdn_chunk(q, k, v, beta, g1, g2, gl, state, t_inv=None):
    c = GDN_CHUNK
    if t_inv is None:
        _mm_nn, _mm_nt, _mm_tn = (functools.partial(_mxu, dims=dd) for dd in (_BNN, _BNT, _BTN))
    else:
        _mm_nn, _mm_nt, _mm_tn = _MM_NN, _MM_NT, _MM_TN
    row, col = _iota((c, c), 0), _iota((c, c), 1)
    incl, strict = row >= col, row > col
    decay = jnp.where(incl, jnp.exp(jnp.where(incl, g1 - g2, 0.0)), 0.0)
    eg = jnp.exp(g1)
    kb = k * beta
    a = _mm_nt(kb, k) * jnp.where(strict, decay, 0.0)
    inv = _unit_lower_inverse(a) if t_inv is None else _SAVED_INVERSE(a, t_inv)
    u = _mm_nn(inv, v * beta)
    w = _mm_nn(inv, kb * eg)
    attn = _mm_nt(q, k) * decay
    v_new = u - _mm_nn(w, state)
    o = _mm_nn(q * eg, state) + _mm_nn(attn, v_new)
    new_state = state * jnp.exp(gl) + _mm_tn(k * jnp.exp(gl - g1), v_new)
    return (o, new_state, inv) if t_inv is None else (o, new_state)


def _unpair(x):
    return jnp.concatenate([x[..., :HEAD_DIM], x[..., HEAD_DIM:]], axis=0)


def _gdn_fwd(q, k, v, beta, g, gl, grow, cpb=4):
    t = q.shape[1]
    rows = cpb * GDN_CHUNK
    lo, hi = slice(0, HEAD_DIM), slice(HEAD_DIM, 2 * HEAD_DIM)

    def body(q_ref, k_ref, v_ref, b_ref, g_ref, gl_ref, grow_ref, o_ref, st_ref, inv_ref, state):
        @pl.when(pl.program_id(0) == 0)
        def _():
            state[...] = jnp.zeros_like(state)

        s = state[...]
        for cc in range(cpb):
            sl = slice(cc * GDN_CHUNK, (cc + 1) * GDN_CHUNK)
            st_ref[:, cc, :, lo], st_ref[:, cc, :, hi] = s[:HEADS // 2], s[HEADS // 2:]
            g2 = jnp.broadcast_to(grow_ref[:, cc], (HEADS, GDN_CHUNK, GDN_CHUNK))
            o, s, inv = _gdn_chunk(*[_unpair(r[:, sl, :]) for r in (q_ref, k_ref, v_ref, b_ref, g_ref)], g2,
                                   _unpair(gl_ref[:, sl, :]), s)
            o_ref[:, sl, lo], o_ref[:, sl, hi] = o[:HEADS // 2], o[HEADS // 2:]
            inv_ref[:, cc, :, lo], inv_ref[:, cc, :, hi] = inv[:HEADS // 2], inv[HEADS // 2:]
        state[...] = s

    hm = pl.BlockSpec((HEADS // 2, rows, 128), lambda i: (0, i, 0))
    per_chunk = pl.BlockSpec((HEADS // 2, cpb, GDN_CHUNK, 128), lambda i: (0, i, 0, 0))
    chunk_shape = jax.ShapeDtypeStruct((HEADS // 2, t // GDN_CHUNK, GDN_CHUNK, 128), F32)
    return pl.pallas_call(
        body, name="gdn_fwd", grid=(t // rows,),
        in_specs=[hm] * 6 + [pl.BlockSpec((HEADS, cpb, 1, GDN_CHUNK), lambda i: (0, i, 0, 0))],
        out_specs=[hm, per_chunk, per_chunk],
        out_shape=[jax.ShapeDtypeStruct((HEADS // 2, t, 128), F32), chunk_shape, chunk_shape],
        scratch_shapes=[pltpu.VMEM((HEADS, GDN_CHUNK, HEAD_DIM), F32)],
        compiler_params=_params(("arbitrary",)),
    )(q, k, v, beta, g, gl, grow)


def _gdn_bwd(q, k, v, beta, g, gl, grow, states, invs, do, cpb=4):
    t = q.shape[1]
    rows = cpb * GDN_CHUNK
    nsteps = t // rows
    lo, hi = slice(0, HEAD_DIM), slice(HEAD_DIM, 2 * HEAD_DIM)

    def body(q_ref, k_ref, v_ref, b_ref, g_ref, gl_ref, grow_ref, st_ref, inv_ref, do_ref,
             dq_ref, dk_ref, dv_ref, db_ref, dg_ref, dgl_ref, dgrow_ref, dstate):
        @pl.when(pl.program_id(0) == 0)
        def _():
            dstate[...] = jnp.zeros_like(dstate)

        ds = dstate[...]
        for cc in reversed(range(cpb)):
            sl = slice(cc * GDN_CHUNK, (cc + 1) * GDN_CHUNK)
            g2 = jnp.broadcast_to(grow_ref[:, cc], (HEADS, GDN_CHUNK, GDN_CHUNK))
            _, vjp = jax.vjp(_gdn_chunk, *[_unpair(r[:, sl, :]) for r in (q_ref, k_ref, v_ref, b_ref, g_ref)], g2,
                             _unpair(gl_ref[:, sl, :]), _unpair(st_ref[:, cc]), _unpair(inv_ref[:, cc]))
            gq, gk, gv, gb, gg1, gg2, ggl, ds, _ = vjp((_unpair(do_ref[:, sl, :]), ds))
            for ref, val in ((dq_ref, gq), (dk_ref, gk), (dv_ref, gv), (db_ref, gb), (dg_ref, gg1), (dgl_ref, ggl)):
                ref[:, sl, lo], ref[:, sl, hi] = val[:HEADS // 2], val[HEADS // 2:]
            dgrow_ref[:, cc] = jnp.sum(gg2, axis=1, keepdims=True)
        dstate[...] = ds

    hm = pl.BlockSpec((HEADS // 2, rows, 128), lambda i: (0, nsteps - 1 - i, 0))
    rowspec = pl.BlockSpec((HEADS, cpb, 1, GDN_CHUNK), lambda i: (0, nsteps - 1 - i, 0, 0))
    per_chunk = pl.BlockSpec((HEADS // 2, cpb, GDN_CHUNK, 128), lambda i: (0, nsteps - 1 - i, 0, 0))
    hm_shape = jax.ShapeDtypeStruct((HEADS // 2, t, 128), F32)
    return pl.pallas_call(
        body, name="gdn_bwd", grid=(nsteps,),
        in_specs=[hm] * 6 + [rowspec, per_chunk, per_chunk, hm],
        out_specs=[hm] * 6 + [rowspec],
        out_shape=[hm_shape] * 6 + [jax.ShapeDtypeStruct((HEADS, t // GDN_CHUNK, 1, GDN_CHUNK), F32)],
        scratch_shapes=[pltpu.VMEM((HEADS, GDN_CHUNK, HEAD_DIM), F32)],
        compiler_params=_params(("arbitrary",)),
    )(q, k, v, beta, g, gl, grow, states, invs, do)


def _gdn_out(o_hm, gdn_norm_w, proj, tm=1024):
    t = proj.shape[0]

    def body(o_ref, w_ref, z_ref, oz_ref):
        w = w_ref[...]
        ones = _pair_ones()
        for p in range(HEADS // 2):
            cols = slice(p * 128, (p + 1) * 128)
            o = o_ref[p]
            z = z_ref[:, cols]
            r = lax.rsqrt(_pair_sum_lanes(o * o, ones) * (1.0 / HEAD_DIM) + NORM_EPS)
            oz_ref[:, cols] = (o * r * w * (z * _sigmoid(z))).astype(oz_ref.dtype)

    tok = pl.BlockSpec((tm, WIDTH), lambda i: (i, 0))
    return pl.pallas_call(
        body, name="gdn_out", grid=(t // tm,),
        in_specs=[pl.BlockSpec((HEADS // 2, tm, 128), lambda i: (0, i, 0)), pl.BlockSpec((1, 128), lambda i: (0, 0)),
                  pl.BlockSpec((tm, WIDTH), lambda i: (i, SEG_ZB // WIDTH))],
        out_specs=tok, out_shape=jax.ShapeDtypeStruct((t, WIDTH), MXU_DTYPE),
        compiler_params=_params(("parallel",)),
    )(o_hm, jnp.tile(gdn_norm_w, (1, 2)), proj)


def _gdn_out_bwd(dproj, dy, w_up, o_hm, gdn_norm_w, proj, tm=1024):
    t = proj.shape[0]

    def body(dp_ref, dy_ref, wu_ref, o_ref, w_ref, z_ref, dz_ref, do_ref, dw_ref):
        w = w_ref[...]
        ones = _pair_ones()
        dw = jnp.zeros((1, 128), F32)
        d_oz = _mxu(dy_ref[...], wu_ref[...], _NT)
        for p in range(HEADS // 2):
            cols = slice(p * 128, (p + 1) * 128)
            o = o_ref[p]
            z, g = z_ref[:, cols], d_oz[:, cols]
            sg = _sigmoid(z)
            r = lax.rsqrt(_pair_sum_lanes(o * o, ones) * (1.0 / HEAD_DIM) + NORM_EPS)
            dz_ref[:, cols] = (g * (o * r * w) * (sg * (1.0 + z * (1.0 - sg)))).astype(dz_ref.dtype)
            dn = g * (z * sg)
            dw += jnp.sum(dn * o * r, axis=0, keepdims=True)
            dnw = dn * w
            do_ref[p] = r * dnw - o * (r * r * r) * (_pair_sum_lanes(dnw * o, ones) * (1.0 / HEAD_DIM))

        @pl.when(pl.program_id(0) == 0)
        def _():
            dw_ref[...] = jnp.zeros_like(dw_ref)

        dw_ref[...] += jnp.where(_iota((8, 128), 0) == 0, dw, 0.0)

    tok = pl.BlockSpec((tm, WIDTH), lambda i: (i, 0))
    seg = pl.BlockSpec((tm, WIDTH), lambda i: (i, SEG_ZB // WIDTH))
    hm = pl.BlockSpec((HEADS // 2, tm, 128), lambda i: (0, i, 0))
    dz, do, dw = pl.pallas_call(
        body, name="gdn_out_bwd", grid=(t // tm,),
        in_specs=[pl.BlockSpec(memory_space=pl.ANY), pl.BlockSpec((tm, D_MODEL), lambda i: (i, 0)),
                  pl.BlockSpec((WIDTH, D_MODEL), lambda i: (0, 0)), hm, pl.BlockSpec((1, 128), lambda i: (0, 0)), seg],
        out_specs=[seg, hm, pl.BlockSpec((8, 128), lambda i: (0, 0))],
        out_shape=[jax.ShapeDtypeStruct((t, PACKED_WIDTH), MXU_DTYPE), jax.ShapeDtypeStruct((HEADS // 2, t, 128), F32),
                   jax.ShapeDtypeStruct((8, 128), F32)],
        input_output_aliases={0: 0},
        compiler_params=_params(("arbitrary",)),
    )(dproj, dy, w_up, o_hm, jnp.tile(gdn_norm_w, (1, 2)), proj)
    return dz, do, dw[:, :HEAD_DIM] + dw[:, HEAD_DIM:]


def _up_merge(oz_a, oz_b, w_up_a, w_up_b, proj, tm=512):
    t = proj.shape[0]

    def body(oa_ref, ob_ref, wa_ref, wb_ref, ga0, ga1, gb0, gb1, ya_ref, yb_ref, m_ref):
        ya = jnp.dot(oa_ref[...], wa_ref[...], preferred_element_type=F32)
        yb = jnp.dot(ob_ref[...], wb_ref[...], preferred_element_type=F32)
        ya_ref[...] = ya
        yb_ref[...] = yb
        for c, (ga, gb) in enumerate(((ga0, gb0), (ga1, gb1))):
            cols = slice(c * WIDTH, (c + 1) * WIDTH)
            m_ref[:, cols] = (_sigmoid(ga[...]) * ya[:, cols] + _sigmoid(gb[...]) * yb[:, cols]).astype(m_ref.dtype)

    def gate(seg, c):
        return pl.BlockSpec((tm, WIDTH), lambda i: (i, seg // WIDTH + c))

    tok_in = pl.BlockSpec((tm, WIDTH), lambda i: (i, 0))
    weight = pl.BlockSpec((WIDTH, D_MODEL), lambda i: (0, 0))
    tok = pl.BlockSpec((tm, D_MODEL), lambda i: (i, 0))
    return pl.pallas_call(
        body, name="up_merge", grid=(t // tm,),
        in_specs=[tok_in, tok_in, weight, weight, gate(SEG_GA, 0), gate(SEG_GA, 1), gate(SEG_GB, 0), gate(SEG_GB, 1)],
        out_specs=[tok, tok, tok],
        out_shape=[jax.ShapeDtypeStruct((t, D_MODEL), F32), jax.ShapeDtypeStruct((t, D_MODEL), F32),
                   jax.ShapeDtypeStruct((t, D_MODEL), MXU_DTYPE)],
        compiler_params=_params(("parallel",)),
    )(oz_a, oz_b, w_up_a, w_up_b, proj, proj, proj, proj)


def _merge_bwd(dproj, d_m, y, proj, seg, name, tm=1024):
    t = proj.shape[0]

    def body(*refs):
        dm_ref, y_ref, g_ref, dg_ref, dy_ref = refs[-5:]
        dm = dm_ref[...]
        s = _sigmoid(g_ref[...])
        dy_ref[...] = (dm * s).astype(dy_ref.dtype)
        dg_ref[...] = (dm * y_ref[...] * s * (1.0 - s)).astype(dg_ref.dtype)

    half = pl.BlockSpec((tm, WIDTH), lambda i, c: (i, c))
    gate = pl.BlockSpec((tm, WIDTH), lambda i, c: (i, seg // WIDTH + c))
    specs, args, aliases = [half, half, gate], [d_m, y, proj], {}
    if dproj is not None:
        specs, args, aliases = [pl.BlockSpec(memory_space=pl.ANY)] + specs, [dproj] + args, {0: 0}
    return pl.pallas_call(
        body, name=name, grid=(t // tm, 2), in_specs=specs, out_specs=[gate, half],
        out_shape=[jax.ShapeDtypeStruct((t, PACKED_WIDTH), MXU_DTYPE), jax.ShapeDtypeStruct((t, D_MODEL), MXU_DTYPE)],
        input_output_aliases=aliases,
        compiler_params=_params(("parallel", "parallel")),
    )(*args)


def _out_tail(merged, w_out, x, final_w, target, tm=1024):
    t = x.shape[0]
    tm = min(tm, t)

    def body(m_ref, wo_ref, x_ref, w_ref, t_ref, dxm_ref, dx_ref, loss_ref, dw_ref):
        x2 = x_ref[...] + jnp.dot(m_ref[...], wo_ref[...], preferred_element_type=F32)
        w = w_ref[...]
        r = lax.rsqrt(jnp.mean(x2 * x2, axis=-1, keepdims=True) + NORM_EPS)
        xn = x2 * r
        err = xn * w - t_ref[...]
        dy = err * (1.0 / D_MODEL)
        dyw = dy * w
        dx2 = r * dyw - x2 * (r * r * r) * jnp.mean(dyw * x2, axis=-1, keepdims=True)
        dx_ref[...] = dx2
        dxm_ref[...] = dx2.astype(dxm_ref.dtype)
        loss = 0.5 * jnp.sum(jnp.sum(err * err, axis=-1, keepdims=True) * (1.0 / D_MODEL), axis=0, keepdims=True)
        onehot = jnp.where((_iota((8, 128), 0) == 0) & (_iota((8, 128), 1) == 0), 1.0, 0.0)

        @pl.when(pl.program_id(0) == 0)
        def _():
            loss_ref[...] = jnp.zeros_like(loss_ref)
            dw_ref[...] = jnp.zeros_like(dw_ref)

        loss_ref[...] += loss * onehot
        dw_ref[...] += jnp.where(_iota((8, D_MODEL), 0) == 0, jnp.sum(dy * xn, axis=0, keepdims=True), 0.0)

    tok = pl.BlockSpec((tm, D_MODEL), lambda i: (i, 0))
    full = pl.BlockSpec((D_MODEL, D_MODEL), lambda i: (0, 0))
    return pl.pallas_call(
        body, name="out_tail", grid=(t // tm,),
        in_specs=[tok, full, tok, pl.BlockSpec((1, D_MODEL), lambda i: (0, 0)), tok],
        out_specs=[tok, tok, pl.BlockSpec((8, 128), lambda i: (0, 0)), pl.BlockSpec((8, D_MODEL), lambda i: (0, 0))],
        out_shape=[jax.ShapeDtypeStruct((t, D_MODEL), MXU_DTYPE), jax.ShapeDtypeStruct((t, D_MODEL), F32),
                   jax.ShapeDtypeStruct((8, 128), F32), jax.ShapeDtypeStruct((8, D_MODEL), F32)],
        compiler_params=_params(("arbitrary",)),
    )(merged, w_out, x, final_w, target)


def _dh_norm_bwd(dproj, wp, x, norm_w, dx2, after, blocks, name, carry=None, tm=1024, tk=PACKED_WIDTH // 4):
    t = x.shape[0]
    nk = PACKED_WIDTH // tk
    first, end = blocks

    def body(*refs):
        dp_ref, wp_ref, x_ref, w_ref, dx2_ref = refs[:5]
        dx_ref, dw_ref, acc = refs[-3:]
        kk = pl.program_id(1)
        part = _mxu(dp_ref[...], wp_ref[...], _NT)

        @pl.when(kk == 0)
        def _():
            acc[...] = part

        @pl.when(kk > 0)
        def _():
            acc[...] += part

        @pl.when((kk == 0) & (pl.program_id(0) == 0))
        def _():
            dw_ref[...] = jnp.zeros_like(dw_ref) if carry is None else refs[-4][...]

        @pl.when(kk == nk - 1)
        def _():
            xf, w, dh_ = x_ref[...], w_ref[...], acc[...]
            r = lax.rsqrt(jnp.mean(xf * xf, axis=-1, keepdims=True) + NORM_EPS)
            dhw = dh_ * w
            dx_ref[...] = dx2_ref[...] + r * dhw - xf * (r * r * r) * jnp.mean(dhw * xf, axis=-1, keepdims=True)
            dw_ref[...] += jnp.where(_iota((8, D_MODEL), 0) == 0, jnp.sum(dh_ * xf * r, axis=0, keepdims=True), 0.0)

    tok = pl.BlockSpec((tm, D_MODEL), lambda i, kk: (first + i, 0))
    small = pl.BlockSpec((8, D_MODEL), lambda i, kk: (0, 0))
    specs = [pl.BlockSpec((tm, tk), lambda i, kk: (first + i, kk)), pl.BlockSpec((D_MODEL, tk), lambda i, kk: (0, kk)),
             tok, pl.BlockSpec((1, D_MODEL), lambda i, kk: (0, 0)), tok, pl.BlockSpec(memory_space=pl.ANY)]
    args = [dproj, wp, x, norm_w, dx2, after]
    aliases = {}
    if carry is not None:
        specs += [pl.BlockSpec(memory_space=pl.ANY), small]
        args += list(carry)
        aliases = {len(args) - 2: 0}
    return pl.pallas_call(
        body, name=name, grid=(end - first, nk), in_specs=specs, out_specs=[tok, small],
        out_shape=[jax.ShapeDtypeStruct((t, D_MODEL), F32), jax.ShapeDtypeStruct((8, D_MODEL), F32)],
        scratch_shapes=[pltpu.VMEM((tm, D_MODEL), F32)],
        input_output_aliases=aliases,
        compiler_params=_params(("arbitrary", "arbitrary")),
    )(*args)


def _local_step(x, target, norm_w, wp, conv_w, a_log, dt_bias, gdn_norm_w, w_up_a, w_up_b, w_out, final_w,
                start_reduce, continue_reduce):
    t = x.shape[0]
    tables = _rope_tables(t)
    a_log = jnp.pad(a_log, ((0, 0), (HEADS, 128 - 2 * HEADS)))
    dt_bias = jnp.pad(dt_bias, ((0, 0), (HEADS, 128 - 2 * HEADS)))

    proj, h_t = _norm_proj(x, norm_w, wp)
    qkvs = _rope_fwd(proj, tables)
    outs, lses = zip(*[_att_fwd(qkvs[gi], d, f"att_fwd{gi}") for gi, d in enumerate(DILATIONS)])
    oz_a, o_a, *lse_views = _att_merge(outs, lses, proj)
    conv = _conv_fwd(proj, conv_w)
    gq, gk, gv, gb, gg, ggl, grow = _gdn_prep(conv, proj, a_log, dt_bias)
    o_b, states, invs = _gdn_fwd(gq, gk, gv, gb, gg, ggl, grow)
    oz_b = _gdn_out(o_b, gdn_norm_w, proj)
    big = dict(tm=1024, tn=1024, tk=1024)
    y_a, y_b, merged = _up_merge(oz_a, oz_b, w_up_a, w_up_b, proj)
    dx2_m, dx2, loss_blk, d_final = _out_tail(merged, w_out, x, final_w, target)

    d_wout = _matmul(merged, dx2_m, "tn", "d_w_out", **big)
    d_m = _matmul(dx2_m, w_out, "nt", "d_merged", **big)
    dproj, dy_a = _merge_bwd(None, d_m, y_a, proj, SEG_GA, "merge_bwd_a")
    dproj, dy_b = _merge_bwd(dproj, d_m, y_b, proj, SEG_GB, "merge_bwd_b")
    d_wua = _matmul(oz_a, dy_a, "tn", "d_w_up_a", **big)
    d_wub = _matmul(oz_b, dy_b, "tn", "d_w_up_b", **big)
    dproj, *views = _att_merge_bwd(dproj, dy_a, w_up_a, o_a, proj)
    do_views, delta_views = views[:3], views[3:]
    dqs, dkvs = zip(*[_att_bwd(qkvs[gi], do_views[gi], lse_views[gi], delta_views[gi], d, f"att_bwd{gi}")
                      for gi, d in enumerate(DILATIONS)])
    dproj = _rope_bwd(dproj, dqs, dkvs, tables)
    dproj, do_b, d_gnw = _gdn_out_bwd(dproj, dy_b, w_up_b, o_b, gdn_norm_w, proj)
    dgq, dgk, dgv, dgb, dgg, dggl, dgrow = _gdn_bwd(gq, gk, gv, gb, gg, ggl, grow, states, invs, do_b)
    dproj, dconv, d_small = _gdn_prep_bwd(dproj, conv, proj, a_log, dt_bias, dgq, dgk, dgv, dgb, dgg, dggl, dgrow)
    dproj, d_convw = _conv_bwd(dproj, dconv, proj, conv_w)
    d_wp = _matmul(h_t, dproj, "nn", "d_w_in", tm=1024, tn=PACKED_WIDTH // 4, tk=1024)
    in_flight, token = start_reduce(d_wp, d_wua, d_wub, d_wout, d_convw[0:GDN_CONV])
    nblk = t // 1024
    cut = max(1, nblk // 4)
    part = _dh_norm_bwd(dproj, wp, x, norm_w, dx2, token, (0, cut), "dh_norm_bwd_a")
    in_flight, token = continue_reduce(in_flight, part[0])
    grad_x, d_norm = _dh_norm_bwd(dproj, wp, x, norm_w, dx2, token, (cut, nblk), "dh_norm_bwd_b", carry=part)
    return dict(loss=loss_blk, grad_x=grad_x, norm_w=d_norm[0:1], in_flight=in_flight,
                a_log=d_small[0:1, HEADS:2 * HEADS], dt_bias=d_small[1:2, HEADS:2 * HEADS], gdn_norm_w=d_gnw[0:1],
                final_norm_w=d_final[0:1])


SHARDS = 4
W_IN_SHARD = IN_WIDTH // SHARDS
ROWS_UP = WIDTH * (D_MODEL // SHARDS) // 128
ROWS_OUT = (D_MODEL // SHARDS) * D_MODEL // 128
CONV_SHARD = 3 * WIDTH // SHARDS
ROWS_CONV = 16
SLAB_ROWS = 2 * ROWS_UP + ROWS_OUT + 2 * ROWS_CONV
HALF_ROWS = SLAB_ROWS // 2
BIG_HALF = (D_MODEL // 2, W_IN_SHARD)
SMALL_HALF = (HALF_ROWS, 128)
MESH = pl.DeviceIdType.MESH
ANY = pl.BlockSpec(memory_space=pl.ANY)


def _pad_rows(a, rows):
    return jnp.pad(a, ((0, rows - a.shape[0]), (0, 0)))


def _pack_slab(w_up_a, w_up_b, w_out, conv, conv_lo):
    parts = [w_up_a.reshape(ROWS_UP, 128), w_up_b.reshape(ROWS_UP, 128), w_out.reshape(ROWS_OUT, 128),
             _pad_rows(conv.reshape(-1, 128), ROWS_CONV), _pad_rows(conv_lo.reshape(-1, 128), ROWS_CONV)]
    return jnp.concatenate(parts, axis=0).reshape(2, *SMALL_HALF)


def _unpack_slab(slab):
    slab = slab.reshape(SLAB_ROWS, 128)
    r0 = 0
    out = []
    for rows, shape in ((ROWS_UP, (WIDTH, D_MODEL // SHARDS)), (ROWS_UP, (WIDTH, D_MODEL // SHARDS)),
                        (ROWS_OUT, (D_MODEL // SHARDS, D_MODEL)), (ROWS_CONV, None), (ROWS_CONV, None)):
        part = slab[r0:r0 + rows]
        out.append(part[:GDN_CONV * CONV_SHARD // 128].reshape(GDN_CONV, CONV_SHARD) if shape is None else part.reshape(shape))
        r0 += rows
    return out


def _mesh_position():
    x, y, c = lax.axis_index("x"), lax.axis_index("y"), lax.axis_index("c")
    return x, y, c, [(1 - x, y), (x, 1 - y), (1 - x, 1 - y)]


def _gather_weights(shards):
    n = len(shards)

    def body(*refs):
        in_refs, out_refs, (send_sems, recv_sems) = refs[:n], refs[n:2 * n], refs[2 * n:]
        x, y, c, chips = _mesh_position()

        def half(a, chip, which):
            return out_refs[a].at[2 * chip[0] + chip[1], which]

        def copy(k, src, dst, to):
            return pltpu.make_async_remote_copy(src_ref=src, dst_ref=dst, send_sem=send_sems.at[k], recv_sem=recv_sems.at[k],
                                                device_id=to, device_id_type=MESH)

        pairs = [(a, j, chip) for a in range(n) for j, chip in enumerate(chips)]
        first = [copy(6 * a + j, in_refs[a].at[c], half(a, (x, y), c), (*chip, c)) for a, j, chip in pairs]
        for cp in first:
            cp.start()
        passed = [copy(6 * a + 3 + j, half(a, chip, c), half(a, chip, c), (x, y, 1 - c)) for a, j, chip in pairs]
        for i, (a, j, chip) in enumerate(pairs):
            copy(6 * a + j, half(a, chip, c), half(a, chip, c), (x, y, c)).wait_recv()
            passed[i].start()
        for a, j, chip in pairs:
            copy(6 * a + 3 + j, half(a, chip, 1 - c), half(a, chip, 1 - c), (x, y, c)).wait_recv()
        for cp in first + passed:
            cp.wait_send()

    return pl.pallas_call(
        body, name="gather_weights", in_specs=[ANY] * n, out_specs=[ANY] * n,
        out_shape=[jax.ShapeDtypeStruct((SHARDS, *s.shape), s.dtype) for s in shards],
        scratch_shapes=[pltpu.SemaphoreType.DMA((6 * n,)), pltpu.SemaphoreType.DMA((6 * n,))],
    )(*shards)


def _pair_sum(grads, recv, blk, name):
    _, _, rows, cols = grads.shape

    def body(c_ref, g_ref, r_ref, o_ref):
        o_ref[...] = (g_ref[0] + r_ref[...]).astype(o_ref.dtype)

    spec = pl.BlockSpec((1, blk, cols), lambda s, i, c_ref: (s, i, 0))
    return pl.pallas_call(
        body, name=name,
        grid_spec=pltpu.PrefetchScalarGridSpec(
            num_scalar_prefetch=1, grid=(SHARDS, rows // blk),
            in_specs=[pl.BlockSpec((1, 1, blk, cols), lambda s, i, c_ref: (s, c_ref[0], i, 0)), spec],
            out_specs=spec),
        out_shape=jax.ShapeDtypeStruct((SHARDS, rows, cols), MXU_DTYPE),
        compiler_params=_params(("parallel", "parallel")),
    )(lax.axis_index("c").astype(jnp.int32).reshape(1), grads, recv)


_HBM = pl.BlockSpec(memory_space=pltpu.HBM)
_SEM = pl.BlockSpec(memory_space=pltpu.SEMAPHORE)
_DATAFLOW = pltpu.SideEffectType.DATAFLOW_SIDE_EFFECTING


def _scatter_copies(p_refs, l_refs, send_sems, recv_sems):
    x, y, c, chips = _mesh_position()
    return [pltpu.make_async_remote_copy(src_ref=p_refs[a].at[2 * chip[0] + chip[1]], dst_ref=l_refs[a].at[j],
                                         send_sem=send_sems.at[3 * a + j], recv_sem=recv_sems.at[3 * a + j],
                                         device_id=(*chip, c), device_id_type=MESH)
            for a in range(len(p_refs)) for j, chip in enumerate(chips)]


def _exchange_copies(g_refs, l_refs, send_sems, recv_sems):
    x, y, c, _ = _mesh_position()
    return [pltpu.make_async_remote_copy(src_ref=g_refs[a].at[s, 1 - c], dst_ref=l_refs[a].at[s],
                                         send_sem=send_sems.at[SHARDS * a + s], recv_sem=recv_sems.at[SHARDS * a + s],
                                         device_id=(x, y, 1 - c), device_id_type=MESH)
            for a in range(len(g_refs)) for s in range(SHARDS)]


def _copies_start(name, copies, count, sources, land_shapes):
    n = len(sources)
    lands = [lax.empty(shape, src.dtype) for shape, src in zip(land_shapes, sources)]

    def body(*refs):
        s_refs, l_refs, send_sems, recv_sems, token = refs[:n], refs[n:2 * n], refs[2 * n], refs[2 * n + 1], refs[-1]
        for cp in copies(s_refs, l_refs, send_sems, recv_sems):
            cp.start()
        token[...] = jnp.zeros_like(token)

    operands = [pltpu.with_memory_space_constraint(a, pltpu.HBM) for a in (*sources, *lands)]
    return pl.pallas_call(
        body, name=name, in_specs=[_HBM] * (2 * n),
        out_shape=(pltpu.SemaphoreType.DMA((count,)), pltpu.SemaphoreType.DMA((count,)),
                   *[pltpu.HBM(a.shape, a.dtype) for a in operands], jax.ShapeDtypeStruct((8, 128), F32)),
        out_specs=(_SEM, _SEM, *[_HBM] * (2 * n), pl.BlockSpec(memory_space=pltpu.VMEM)),
        input_output_aliases={i: 2 + i for i in range(2 * n)},
        compiler_params=pltpu.CompilerParams(has_side_effects=_DATAFLOW),
    )(*operands)


def _copies_wait(name, copies, send_sems, recv_sems, passed, after):
    n = len(passed) // 2

    def body(*refs):
        s_refs, l_refs, send_s, recv_s = refs[:n], refs[n:2 * n], refs[2 * n], refs[2 * n + 1]
        for cp in copies(s_refs, l_refs, send_s, recv_s):
            cp.wait_send()
            cp.wait_recv()

    return pl.pallas_call(
        body, name=name, in_specs=[_HBM] * (2 * n) + [_SEM, _SEM, ANY],
        out_shape=[pltpu.HBM(a.shape, a.dtype) for a in passed], out_specs=[_HBM] * (2 * n),
        input_output_aliases={i: i for i in range(2 * n)},
        compiler_params=pltpu.CompilerParams(has_side_effects=_DATAFLOW),
    )(*passed, send_sems, recv_sems, after)


def _chip_sum(pairs, recv, blk, name):
    _, rows, cols = pairs.shape

    def body(pos_ref, p_ref, r_ref, o_ref):
        o_ref[0] = ((p_ref[0].astype(F32) + r_ref[0].astype(F32)) + r_ref[1].astype(F32)) + r_ref[2].astype(F32)

    pos = jnp.stack([2 * lax.axis_index("x") + lax.axis_index("y"), lax.axis_index("c")]).astype(jnp.int32)
    return pl.pallas_call(
        body, name=name,
        grid_spec=pltpu.PrefetchScalarGridSpec(
            num_scalar_prefetch=1, grid=(rows // blk,),
            in_specs=[pl.BlockSpec((1, blk, cols), lambda i, pos_ref: (pos_ref[0], i, 0)),
                      pl.BlockSpec((3, blk, cols), lambda i, pos_ref: (0, i, 0))],
            out_specs=pl.BlockSpec((1, blk, cols), lambda i, pos_ref: (pos_ref[1], i, 0))),
        out_shape=jax.ShapeDtypeStruct((2, rows, cols), F32),
        compiler_params=_params(("parallel",)),
    )(pos, pairs, recv)


def _share_total(totals):
    n = len(totals)

    def body(*refs):
        t_refs, out_refs, (send_sems, recv_sems) = refs[:n], refs[n:2 * n], refs[2 * n:]
        x, y, c, _ = _mesh_position()
        copies = [pltpu.make_async_remote_copy(src_ref=t_refs[a].at[c], dst_ref=out_refs[a].at[c], send_sem=send_sems.at[a],
                                               recv_sem=recv_sems.at[a], device_id=(x, y, 1 - c), device_id_type=MESH)
                  for a in range(n)]
        for cp in copies:
            cp.start()
        for a in range(n):
            other = out_refs[a].at[1 - c]
            pltpu.make_async_remote_copy(src_ref=other, dst_ref=other, send_sem=send_sems.at[a], recv_sem=recv_sems.at[a],
                                         device_id=(x, y, c), device_id_type=MESH).wait_recv()
        for cp in copies:
            cp.wait_send()

    return pl.pallas_call(
        body, name="share_total", in_specs=[ANY] * n, out_specs=[ANY] * n,
        out_shape=[jax.ShapeDtypeStruct(t.shape, F32) for t in totals],
        scratch_shapes=[pltpu.SemaphoreType.DMA((n,)), pltpu.SemaphoreType.DMA((n,))],
        input_output_aliases={a: a for a in range(n)},
    )(*totals)


def _allreduce_small(block):
    def body(b_ref, out_ref, gath, send_sems, recv_sems):
        x, y, c, _ = _mesh_position()
        me = 4 * x + 2 * y + c
        gath[me] = b_ref[...]
        copies = []
        for k in range(1, 8):
            peer = (x ^ (k >> 2), y ^ ((k >> 1) & 1), c ^ (k & 1))
            copies.append(pltpu.make_async_remote_copy(src_ref=b_ref, dst_ref=gath.at[me], send_sem=send_sems.at[k - 1],
                                                       recv_sem=recv_sems.at[k - 1], device_id=peer, device_id_type=MESH))
        for cp in copies:
            cp.start()
        for k in range(1, 8):
            src = 4 * (x ^ (k >> 2)) + 2 * (y ^ ((k >> 1) & 1)) + (c ^ (k & 1))
            pltpu.make_async_remote_copy(src_ref=b_ref, dst_ref=gath.at[src], send_sem=send_sems.at[k - 1],
                                         recv_sem=recv_sems.at[k - 1], device_id=(x, y, c), device_id_type=MESH).wait_recv()
        for cp in copies:
            cp.wait_send()
        acc = gath[0]
        for d in range(1, 8):
            acc = acc + gath[d]
        out_ref[...] = acc

    vm = pl.BlockSpec(memory_space=pltpu.VMEM)
    return pl.pallas_call(
        body, name="allreduce_small", in_specs=[vm], out_specs=vm,
        out_shape=jax.ShapeDtypeStruct((8, D_MODEL), F32),
        scratch_shapes=[pltpu.VMEM((8, 8, D_MODEL), F32), pltpu.SemaphoreType.DMA((7,)), pltpu.SemaphoreType.DMA((7,))],
    )(block)


def _adamw(w, g, m, v, name):
    rows, cols = w.shape
    tr = 128 if rows % 128 == 0 else rows

    def body(w_ref, g_ref, m_ref, v_ref, d_ref, nm_ref, nv_ref):
        gv = g_ref[...]
        nm = ADAM_B1 * m_ref[...] + (1.0 - ADAM_B1) * gv
        nv = ADAM_B2 * v_ref[...] + (1.0 - ADAM_B2) * (gv * gv)
        m_hat = nm / (1.0 - ADAM_B1 ** ADAM_STEP)
        v_hat = nv / (1.0 - ADAM_B2 ** ADAM_STEP)
        d_ref[...] = -ADAM_LR * (m_hat / (jnp.sqrt(v_hat) + ADAM_EPS) + ADAM_WD * w_ref[...])
        nm_ref[...] = nm
        nv_ref[...] = nv

    spec = pl.BlockSpec((tr, cols), lambda i: (i, 0))
    shape = jax.ShapeDtypeStruct((rows, cols), F32)
    return pl.pallas_call(
        body, name=name, grid=(rows // tr,), in_specs=[spec] * 4, out_specs=[spec] * 3, out_shape=[shape] * 3,
        compiler_params=_params(("parallel",)),
    )(w, g, m, v)


def kernel(x, norm_w, w_in, conv_w, a_log, dt_bias, gdn_norm_w, w_up_a, w_up_b, w_out, final_norm_w, loss_target, m_norm_w, m_w_in, m_conv_w, m_a_log, m_dt_bias, m_gdn_norm_w, m_w_up_a, m_w_up_b, m_w_out, m_final_norm_w, v_norm_w, v_w_in, v_conv_w, v_a_log, v_dt_bias, v_gdn_norm_w, v_w_up_a, v_w_up_b, v_w_out, v_final_norm_w):
    conv_hi = conv_w[0].astype(MXU_DTYPE)
    conv_lo = (conv_w[0] - conv_hi.astype(F32)).astype(MXU_DTYPE)
    big = w_in[0].astype(MXU_DTYPE).reshape(2, *BIG_HALF)
    slab = _pack_slab(w_up_a[0].astype(MXU_DTYPE), w_up_b[0].astype(MXU_DTYPE), w_out[0].astype(MXU_DTYPE), conv_hi, conv_lo)
    own_shard = 2 * lax.axis_index("x") + lax.axis_index("y")
    bigs, slabs = _gather_weights([big, slab])
    bigs = lax.dynamic_update_slice(bigs, big[None], (own_shard, 0, 0, 0)).reshape(SHARDS, D_MODEL, W_IN_SHARD)
    slabs = lax.dynamic_update_slice(slabs, slab[None], (own_shard, 0, 0, 0))
    parts = [_unpack_slab(slabs[s]) for s in range(SHARDS)]
    split = BA_END - (SHARDS - 1) * W_IN_SHARD
    wp = jnp.concatenate([bigs[s] for s in range(SHARDS - 1)]
                         + [bigs[-1][:, :split], jnp.zeros((D_MODEL, SEG_GA - BA_END), MXU_DTYPE), bigs[-1][:, split:]], axis=1)
    w_up_a_full = jnp.concatenate([p[0] for p in parts], axis=1)
    w_up_b_full = jnp.concatenate([p[1] for p in parts], axis=1)
    w_out_full = jnp.concatenate([p[2] for p in parts], axis=0)
    conv_full = jnp.concatenate([p[3].astype(F32) + p[4].astype(F32) for p in parts], axis=1)

    blocks, tags = (128, HALF_ROWS), ("w_in", "slab")

    def start_reduce(d_wp, d_w_up_a, d_w_up_b, d_w_out, d_conv_w):
        d_w_in = [d_wp[:, s * W_IN_SHARD:(s + 1) * W_IN_SHARD] for s in range(SHARDS - 1)]
        d_w_in.append(jnp.concatenate([d_wp[:, (SHARDS - 1) * W_IN_SHARD:BA_END], d_wp[:, SEG_GA:]], axis=1))
        zero_conv = jnp.zeros((GDN_CONV, CONV_SHARD), F32)
        grads = [jnp.stack(d_w_in).reshape(SHARDS, 2, *BIG_HALF),
                 jnp.stack([_pack_slab(d_w_up_a[:, s * 256:(s + 1) * 256], d_w_up_b[:, s * 256:(s + 1) * 256],
                                       d_w_out[s * 256:(s + 1) * 256], d_conv_w[:, s * CONV_SHARD:(s + 1) * CONV_SHARD],
                                       zero_conv) for s in range(SHARDS)])]
        *in_flight, token = _copies_start("exchange_start", _exchange_copies, 2 * SHARDS, grads,
                                          [(SHARDS, *gr.shape[2:]) for gr in grads])
        return in_flight, token

    def continue_reduce(in_flight, after):
        send_sems, recv_sems, *passed = in_flight
        arrived = _copies_wait("exchange_wait", _exchange_copies, send_sems, recv_sems, passed, after)
        grads, from_sibling = arrived[:2], arrived[2:]
        pairs = [_pair_sum(gr, fs, blk, f"pair_sum_{tag}") for gr, fs, blk, tag in zip(grads, from_sibling, blocks, tags)]
        *in_flight, token = _copies_start("scatter_start", _scatter_copies, 2 * 3, pairs, [(3, *p.shape[1:]) for p in pairs])
        return in_flight, token

    g = _local_step(x[0], loss_target[0], norm_w, wp, conv_full, a_log, dt_bias, gdn_norm_w,
                    w_up_a_full, w_up_b_full, w_out_full, final_norm_w[None], start_reduce, continue_reduce)

    send_sems, recv_sems, *passed = g["in_flight"]
    arrived = _copies_wait("scatter_wait", _scatter_copies, send_sems, recv_sems, passed, g["grad_x"])
    pairs, from_chips = arrived[:2], arrived[2:]
    total_big, total_slab = _share_total([_chip_sum(p, fc, blk, f"chip_sum_{tag}")
                                          for p, fc, blk, tag in zip(pairs, from_chips, blocks, tags)])
    g_w_in = total_big.reshape(D_MODEL, W_IN_SHARD)
    g_w_up_a, g_w_up_b, g_w_out, g_conv, _ = _unpack_slab(total_slab)

    row2 = jnp.concatenate([g["gdn_norm_w"], g["a_log"], g["dt_bias"], g["loss"][0:1, 0:1],
                            jnp.zeros((1, D_MODEL - HEAD_DIM - 2 * HEADS - 1), F32)], axis=1)
    small = _allreduce_small(jnp.concatenate([g["norm_w"], g["final_norm_w"], row2, jnp.zeros((5, D_MODEL), F32)], axis=0))
    g_norm, g_final = small[0:1], small[1]
    g_gnw, g_alog, g_dt = small[2:3, 0:HEAD_DIM], small[2:3, HEAD_DIM:HEAD_DIM + HEADS], small[2:3, HEAD_DIM + HEADS:HEAD_DIM + 2 * HEADS]
    loss = small[2, HEAD_DIM + 2 * HEADS]

    names = ["norm_w", "w_in", "conv_w", "a_log", "dt_bias", "gdn_norm_w", "w_up_a", "w_up_b", "w_out", "final_norm_w"]
    weights = dict(zip(names, (norm_w, w_in, conv_w, a_log, dt_bias, gdn_norm_w, w_up_a, w_up_b, w_out, final_norm_w)))
    ms = dict(zip(names, (m_norm_w, m_w_in, m_conv_w, m_a_log, m_dt_bias, m_gdn_norm_w, m_w_up_a, m_w_up_b, m_w_out, m_final_norm_w)))
    vs = dict(zip(names, (v_norm_w, v_w_in, v_conv_w, v_a_log, v_dt_bias, v_gdn_norm_w, v_w_up_a, v_w_up_b, v_w_out, v_final_norm_w)))
    grads2d = dict(norm_w=g_norm, w_in=g_w_in, conv_w=g_conv, a_log=g_alog, dt_bias=g_dt, gdn_norm_w=g_gnw,
                   w_up_a=g_w_up_a, w_up_b=g_w_up_b, w_out=g_w_out, final_norm_w=g_final[None])
    grad_out, delta, new_m, new_v = [], [], [], []
    for n in names:
        shape = weights[n].shape
        two_d = grads2d[n].shape
        d, nm, nv = _adamw(weights[n].reshape(two_d), grads2d[n], ms[n].reshape(two_d), vs[n].reshape(two_d), f"adamw_{n}")
        grad_out.append(grads2d[n].reshape(shape))
        delta.append(d.reshape(shape))
        new_m.append(nm.reshape(shape))
        new_v.append(nv.reshape(shape))
    return (loss, g["grad_x"][None], *grad_out, *delta, *new_m, *new_v)
```

```python
import functools

import jax
import jax.numpy as jnp
from jax import lax
from jax.experimental import pallas as pl
from jax.experimental.pallas import tpu as pltpu

F32 = jnp.float32
MXU_DTYPE = jnp.bfloat16
HIGHEST = lax.Precision.HIGHEST

D_MODEL = 1024
HEADS = 8
HEAD_DIM = 64
WIDTH = HEADS * HEAD_DIM
NORM_EPS = 1e-6
ROPE_THETA = 10000.0
ATT_BLOCK = 128
DILATIONS = (1, 4, 16)
GDN_CHUNK = 64
GDN_CONV = 4
IN_WIDTH = 9232
SEG_A, SEG_ZA, SEG_B, SEG_ZB, SEG_BA, SEG_GA, SEG_GB, PACKED_WIDTH = 0, 4608, 5120, 6656, 7168, 7680, 8704, 9728
BA_END = 7184
VMEM_LIMIT = 56 * 1024 * 1024

ADAM_LR, ADAM_B1, ADAM_B2, ADAM_EPS, ADAM_WD, ADAM_STEP = 0.001, 0.9, 0.999, 1e-08, 0.01, 10

_NN = (((1,), (0,)), ((), ()))
_NT = (((1,), (1,)), ((), ()))
_TN = (((0,), (0,)), ((), ()))


def _params(sem):
    return pltpu.CompilerParams(dimension_semantics=sem, vmem_limit_bytes=VMEM_LIMIT)


def _mxu(a, b, dims):
    return lax.dot_general(a.astype(MXU_DTYPE), b.astype(MXU_DTYPE), dims, preferred_element_type=F32)


def _sigmoid(x):
    return 1.0 / (1.0 + jnp.exp(-x))


def _softplus(x):
    return jnp.maximum(x, 0.0) + jnp.log(1.0 + jnp.exp(-jnp.abs(x)))


def _iota(shape, axis):
    return lax.broadcasted_iota(jnp.int32, shape, axis)


def _matmul(a, b, mode, name, out_dtype=F32, tm=512, tn=512, tk=512):
    if mode == "nn":
        (m, k), (k2, n) = a.shape, b.shape
    elif mode == "nt":
        (m, k), (n, k2) = a.shape, b.shape
    else:
        (k, m), (k2, n) = a.shape, b.shape
    assert k == k2
    tm, tn, tk = min(tm, m), min(tn, n), min(tk, k)
    assert m % tm == 0 and n % tn == 0 and k % tk == 0
    nk = k // tk
    dims = {"nn": _NN, "nt": _NT, "tn": _TN}[mode]

    assert out_dtype == F32

    def body(a_ref, b_ref, o_ref):
        kk = pl.program_id(2)
        part = _mxu(a_ref[...], b_ref[...], dims)

        @pl.when(kk == 0)
        def _():
            o_ref[...] = part

        @pl.when(kk > 0)
        def _():
            o_ref[...] += part

    a_spec = pl.BlockSpec((tk, tm), lambda i, j, kk: (kk, i)) if mode == "tn" else pl.BlockSpec((tm, tk), lambda i, j, kk: (i, kk))
    b_spec = pl.BlockSpec((tn, tk), lambda i, j, kk: (j, kk)) if mode == "nt" else pl.BlockSpec((tk, tn), lambda i, j, kk: (kk, j))
    return pl.pallas_call(
        body, name=name, grid=(m // tm, n // tn, nk), in_specs=[a_spec, b_spec],
        out_specs=pl.BlockSpec((tm, tn), lambda i, j, kk: (i, j)),
        out_shape=jax.ShapeDtypeStruct((m, n), out_dtype),
        compiler_params=_params(("parallel", "parallel", "arbitrary")),
    )(a, b)


def _norm_proj(x, norm_w, wp, tm=1024, tn=PACKED_WIDTH // 4):
    t = x.shape[0]
    tm = min(tm, t)

    def body(x_ref, nw_ref, w_ref, proj_ref, ht_ref, h_scr):
        @pl.when(pl.program_id(1) == 0)
        def _():
            xf = x_ref[...]
            r = lax.rsqrt(jnp.mean(xf * xf, axis=-1, keepdims=True) + NORM_EPS)
            h = xf * r * nw_ref[...]
            h_scr[...] = h.astype(h_scr.dtype)
            ht_ref[...] = h.T.astype(ht_ref.dtype)

        proj_ref[...] = jnp.dot(h_scr[...], w_ref[...], preferred_element_type=F32)

    return pl.pallas_call(
        body, name="norm_proj", grid=(t // tm, PACKED_WIDTH // tn),
        in_specs=[pl.BlockSpec((tm, D_MODEL), lambda i, j: (i, 0)),
                  pl.BlockSpec((1, D_MODEL), lambda i, j: (0, 0)),
                  pl.BlockSpec((D_MODEL, tn), lambda i, j: (0, j))],
        out_specs=[pl.BlockSpec((tm, tn), lambda i, j: (i, j)),
                   pl.BlockSpec((D_MODEL, tm), lambda i, j: (0, i))],
        out_shape=[jax.ShapeDtypeStruct((t, PACKED_WIDTH), F32), jax.ShapeDtypeStruct((D_MODEL, t), MXU_DTYPE)],
        scratch_shapes=[pltpu.VMEM((tm, D_MODEL), MXU_DTYPE)],
        compiler_params=_params(("parallel", "arbitrary")),
    )(x, norm_w, wp)


def _rope_tables(t):
    lane = jnp.arange(128)
    inv_freq = ROPE_THETA ** (-jnp.arange(0, HEAD_DIM, 2, dtype=F32) / HEAD_DIM)
    freq = jnp.concatenate([inv_freq] * 4)
    coarse = (jnp.arange(t // 128, dtype=F32) * 128.0)[:, None] * freq[None, :]
    fine = jnp.arange(128, dtype=F32)[:, None] * freq[None, :]
    ca, sa, cb, sb = jnp.cos(coarse)[:, None, :], jnp.sin(coarse)[:, None, :], jnp.cos(fine)[None], jnp.sin(fine)[None]
    cos = (ca * cb - sa * sb).reshape(t, 128)
    sin = (sa * cb + ca * sb).reshape(t, 128)
    first_half = (lane % HEAD_DIM) < HEAD_DIM // 2
    return cos, jnp.where(first_half, -sin, 0.0), jnp.where(first_half, 0.0, sin)


def _rope_cols(x, cos, sin_lo, sin_hi, sign):
    outs = []
    for c in range(x.shape[1] // 128):
        xc = x[:, c * 128:(c + 1) * 128]
        rot = pltpu.roll(xc, 96, 1) * sin_lo + pltpu.roll(xc, 32, 1) * sin_hi
        outs.append(xc * cos + sign * rot)
    return jnp.concatenate(outs, axis=1)


def _rope_block(x, cos, sin_lo, sin_hi, sign, q_scale=None):
    q = _rope_cols(x[:, :WIDTH], cos, sin_lo, sin_hi, sign)
    k = _rope_cols(x[:, WIDTH:2 * WIDTH], cos, sin_lo, sin_hi, sign)
    return jnp.concatenate([q if q_scale is None else q * q_scale, k, x[:, 2 * WIDTH:]], axis=1)


def _tile_scratch(tm, cols):
    return pltpu.VMEM((cols // 128, tm, 128), F32)


def _store_tile(scr, y):
    for c in range(scr.shape[0]):
        scr[c] = y[:, c * 128:(c + 1) * 128]


def _load_tile(scr):
    return jnp.concatenate([scr[c] for c in range(scr.shape[0])], axis=1)


def _to_strided_view(scr, o_ref, d, width=None, col0=0):
    n, tm, _ = scr.shape
    width = n * 128 if width is None else width
    for r in range(d):
        for c in range(n):
            at = r * width + col0 + c * 128
            o_ref[:, at:at + 128] = scr[c, pl.ds(r, tm // d, stride=d), :].astype(o_ref.dtype)


def _from_strided_view(i_ref, scr, d):
    n, tm, _ = scr.shape
    for r in range(d):
        for c in range(n):
            scr[c, pl.ds(r, tm // d, stride=d), :] = i_ref[:, (r * n + c) * 128:(r * n + c + 1) * 128].astype(F32)


def _strided_spec(tm, d, cols):
    return pl.BlockSpec((tm // d, d * cols), lambda i: (i, 0))


def _rope_fwd(proj, tables, tm=512):
    t = proj.shape[0]
    cols = 3 * WIDTH

    def body(x_ref, c_ref, sl_ref, sh_ref, o0, o1, o2, scr):
        for g, (d, o_ref) in enumerate(zip(DILATIONS, (o0, o1, o2))):
            y = _rope_block(x_ref[:, g * cols:(g + 1) * cols], c_ref[...], sl_ref[...], sh_ref[...], 1.0, HEAD_DIM ** -0.5)
            if d == 1:
                o_ref[...] = y.astype(o_ref.dtype)
            else:
                _store_tile(scr, y)
                _to_strided_view(scr, o_ref, d)

    tab = pl.BlockSpec((tm, 128), lambda i: (i, 0))
    return pl.pallas_call(
        body, name="rope_fwd", grid=(t // tm,),
        in_specs=[pl.BlockSpec((tm, 3 * cols), lambda i: (i, 0)), tab, tab, tab],
        out_specs=[_strided_spec(tm, d, cols) for d in DILATIONS],
        out_shape=[jax.ShapeDtypeStruct((t // d, d * cols), MXU_DTYPE) for d in DILATIONS],
        scratch_shapes=[_tile_scratch(tm, cols)],
        compiler_params=_params(("parallel",)),
    )(proj, *tables)


def _rope_bwd(dproj, dqs, dkvs, tables, tm=512):
    t = dproj.shape[0]
    cols = 3 * WIDTH

    def body(dp_ref, q0, q1, q2, kv0, kv1, kv2, c_ref, sl_ref, sh_ref, o_ref, scr_q, scr_kv):
        for g, (d, q_ref, kv_ref) in enumerate(zip(DILATIONS, (q0, q1, q2), (kv0, kv1, kv2))):
            if d == 1:
                x = jnp.concatenate([q_ref[...], kv_ref[...]], axis=1)
            else:
                _from_strided_view(q_ref, scr_q, d)
                _from_strided_view(kv_ref, scr_kv, d)
                x = jnp.concatenate([_load_tile(scr_q), _load_tile(scr_kv)], axis=1)
            y = _rope_block(x, c_ref[...], sl_ref[...], sh_ref[...], -1.0)
            o_ref[:, g * cols:(g + 1) * cols] = y.astype(o_ref.dtype)

    tab = pl.BlockSpec((tm, 128), lambda i: (i, 0))
    return pl.pallas_call(
        body, name="rope_bwd", grid=(t // tm,),
        in_specs=[pl.BlockSpec(memory_space=pl.ANY)] + [_strided_spec(tm, d, WIDTH) for d in DILATIONS]
        + [_strided_spec(tm, d, 2 * WIDTH) for d in DILATIONS] + [tab, tab, tab],
        out_specs=pl.BlockSpec((tm, 3 * cols), lambda i: (i, 0)),
        out_shape=jax.ShapeDtypeStruct((t, PACKED_WIDTH), MXU_DTYPE),
        scratch_shapes=[_tile_scratch(tm, WIDTH), _tile_scratch(tm, 2 * WIDTH)],
        input_output_aliases={0: 0},
        compiler_params=_params(("parallel",)),
    )(dproj, *dqs, *dkvs, *tables)


def _att_masks():
    qi = _iota((ATT_BLOCK, ATT_BLOCK), 0)
    kj = _iota((ATT_BLOCK, ATT_BLOCK), 1)
    return kj <= qi, kj >= qi


def _att_fwd(qkv, d, name):
    rows = qkv.shape[0]
    nb = rows // ATT_BLOCK

    def body(q_ref, kc_ref, kp_ref, vc_ref, vp_ref, o_ref, lse_ref):
        m_cur, m_prev = _att_masks()
        hs = range(HEADS)
        sls = [slice(h * HEAD_DIM, (h + 1) * HEAD_DIM) for h in hs]

        def attend(with_prev):
            qs = [q_ref[:, sl] for sl in sls]
            s_c = [jnp.where(m_cur, _mxu(qs[h], kc_ref[:, sls[h]], _NT), -jnp.inf) for h in hs]
            if with_prev:
                s_p = [jnp.where(m_prev, _mxu(qs[h], kp_ref[:, sls[h]], _NT), -jnp.inf) for h in hs]
                m = [jnp.max(jnp.maximum(s_c[h], s_p[h]), axis=1, keepdims=True) for h in hs]
                p_c = [jnp.exp(s_c[h] - m[h]) for h in hs]
                p_p = [jnp.exp(s_p[h] - m[h]) for h in hs]
                den = [jnp.sum(p_c[h] + p_p[h], axis=1, keepdims=True) for h in hs]
                o = [_mxu(p_c[h], vc_ref[:, sls[h]], _NN) + _mxu(p_p[h], vp_ref[:, sls[h]], _NN) for h in hs]
            else:
                m = [jnp.max(s_c[h], axis=1, keepdims=True) for h in hs]
                p_c = [jnp.exp(s_c[h] - m[h]) for h in hs]
                den = [jnp.sum(p_c[h], axis=1, keepdims=True) for h in hs]
                o = [_mxu(p_c[h], vc_ref[:, sls[h]], _NN) for h in hs]
            lane = _iota((1, 128), 1)
            lse = jnp.zeros((ATT_BLOCK, 128), F32)
            for h in hs:
                o_ref[:, sls[h]] = o[h] / den[h]
                lse = jnp.where(lane == h, m[h] + jnp.log(den[h]), lse)
            lse_ref[...] = lse

        @pl.when(pl.program_id(1) > 0)
        def _():
            attend(True)

        @pl.when(pl.program_id(1) == 0)
        def _():
            attend(False)

    def cur(c):
        return pl.BlockSpec((ATT_BLOCK, WIDTH), lambda r, i: (i, 3 * r + c))

    def prev(c):
        return pl.BlockSpec((ATT_BLOCK, WIDTH), lambda r, i: (jnp.maximum(i - 1, 0), 3 * r + c))

    return pl.pallas_call(
        body, name=name, grid=(d, nb), in_specs=[cur(0), cur(1), prev(1), cur(2), prev(2)],
        out_specs=[pl.BlockSpec((ATT_BLOCK, WIDTH), lambda r, i: (i, r)), pl.BlockSpec((ATT_BLOCK, 128), lambda r, i: (i, r))],
        out_shape=[jax.ShapeDtypeStruct((rows, d * WIDTH), F32), jax.ShapeDtypeStruct((rows, d * 128), F32)],
        compiler_params=_params(("parallel", "arbitrary")),
    )(qkv, qkv, qkv, qkv, qkv)


def _att_bwd(qkv, do, lse, delta, d, name):
    rows = qkv.shape[0]
    nb = rows // ATT_BLOCK
    total = d * nb
    scale = HEAD_DIM ** -0.5

    def body(q_ref, kc_ref, kp_ref, vc_ref, vp_ref, do_ref, l_ref, dl_ref, dq_ref, dkv_ref, own):
        n = pl.program_id(0)
        i = n % nb
        m_cur, m_prev = _att_masks()

        hs = range(HEADS)
        sls = [slice(h * HEAD_DIM, (h + 1) * HEAD_DIM) for h in hs]
        col = [slice(h, h + 1) for h in hs]

        def probs(k_r, mask):
            s = [_mxu(q_ref[:, sls[h]], k_r[:, sls[h]], _NT) for h in hs]
            return [jnp.where(mask, jnp.exp(s[h] - l_ref[:, col[h]]), 0.0) for h in hs]

        def dscores(p, v_r):
            dp = [_mxu(do_ref[:, sls[h]], v_r[:, sls[h]], _NT) for h in hs]
            return [(p[h] * (dp[h] - dl_ref[:, col[h]])).astype(MXU_DTYPE) for h in hs]

        def step(with_prev):
            p_c = probs(kc_ref, m_cur)
            ds_c = dscores(p_c, vc_ref)
            dq = [_mxu(ds_c[h], kc_ref[:, sls[h]], _NN) for h in hs]
            if with_prev:
                p_p = probs(kp_ref, m_prev)
                ds_p = dscores(p_p, vp_ref)
                dq = [dq[h] + _mxu(ds_p[h], kp_ref[:, sls[h]], _NN) for h in hs]
                dk_p = [_mxu(ds_p[h], q_ref[:, sls[h]], _TN) for h in hs]
                dv_p = [_mxu(p_p[h], do_ref[:, sls[h]], _TN) for h in hs]
            dk_c = [_mxu(ds_c[h], q_ref[:, sls[h]], _TN) for h in hs]
            dv_c = [_mxu(p_c[h], do_ref[:, sls[h]], _TN) for h in hs]
            for h in hs:
                vs = slice(WIDTH + h * HEAD_DIM, WIDTH + (h + 1) * HEAD_DIM)
                dq_ref[:, sls[h]] = dq[h] * scale
                if with_prev:
                    dkv_ref[:, sls[h]] = own[:, sls[h]] + dk_p[h]
                    dkv_ref[:, vs] = own[:, vs] + dv_p[h]
            if not with_prev:
                dkv_ref[...] = own[...]
            for h in hs:
                vs = slice(WIDTH + h * HEAD_DIM, WIDTH + (h + 1) * HEAD_DIM)
                own[:, sls[h]] = dk_c[h]
                own[:, vs] = dv_c[h]

        @pl.when(n == 0)
        def _():
            own[...] = jnp.zeros_like(own)

        @pl.when((i > 0) & (n < total))
        def _():
            step(True)

        @pl.when((i == 0) & (n < total))
        def _():
            step(False)

        @pl.when(n == total)
        def _():
            dkv_ref[...] = own[...]

    def at(n, shift):
        n = jnp.minimum(n, total - 1)
        return jnp.clip(n % nb + shift, 0, nb - 1), n // nb

    def qkv_spec(c, shift):
        def index(n):
            i, r = at(n, shift)
            return i, 3 * r + c
        return pl.BlockSpec((ATT_BLOCK, WIDTH), index)

    tok = pl.BlockSpec((ATT_BLOCK, WIDTH), lambda n: at(n, 0))
    per_head = pl.BlockSpec((ATT_BLOCK, 128), lambda n: at(n, 0))
    return pl.pallas_call(
        body, name=name, grid=(total + 1,),
        in_specs=[qkv_spec(0, 0), qkv_spec(1, 0), qkv_spec(1, -1), qkv_spec(2, 0), qkv_spec(2, -1), tok, per_head, per_head],
        out_specs=[tok, pl.BlockSpec((ATT_BLOCK, 2 * WIDTH), lambda n: at(jnp.maximum(n - 1, 0), 0))],
        out_shape=[jax.ShapeDtypeStruct((rows, d * WIDTH), F32), jax.ShapeDtypeStruct((rows, d * 2 * WIDTH), F32)],
        scratch_shapes=[pltpu.VMEM((ATT_BLOCK, 2 * WIDTH), F32)],
        compiler_params=_params(("arbitrary",)),
    )(qkv, qkv, qkv, qkv, qkv, do, lse, delta)


def _att_merge(os_, lses, proj, tm=1024):
    t = proj.shape[0]

    def body(o0, o1, o2, l0, l1, l2, z_ref, oz_ref, o_ref, t0, t1, t2, s_o1, s_o2, s_l1, s_l2, s_t):
        _from_strided_view(o1, s_o1, DILATIONS[1])
        _from_strided_view(o2, s_o2, DILATIONS[2])
        _from_strided_view(l1, s_l1, DILATIONS[1])
        _from_strided_view(l2, s_l2, DILATIONS[2])
        a, b, c = l0[...], s_l1[0], s_l2[0]
        m = jnp.maximum(jnp.maximum(a, b), c)
        wa, wb, wc = jnp.exp(a - m), jnp.exp(b - m), jnp.exp(c - m)
        den = wa + wb + wc
        total = m + jnp.log(den)
        t0[...] = total
        s_t[0] = total
        _to_strided_view(s_t, t1, DILATIONS[1])
        _to_strided_view(s_t, t2, DILATIONS[2])
        spread = jnp.where(_iota((128, WIDTH), 1) // HEAD_DIM == _iota((128, WIDTH), 0), 1.0, 0.0).astype(MXU_DTYPE)
        ra, rb, rc = (_pair_sum_lanes(w / den, spread) for w in (wa, wb, wc))
        o = ra * o0[...] + rb * _load_tile(s_o1) + rc * _load_tile(s_o2)
        z = z_ref[...]
        o_ref[...] = o
        oz_ref[...] = (o * z * _sigmoid(z)).astype(oz_ref.dtype)

    tok = pl.BlockSpec((tm, WIDTH), lambda i: (i, 0))
    views = [_strided_spec(tm, d, WIDTH) for d in DILATIONS]
    per_head = [_strided_spec(tm, d, 128) for d in DILATIONS]
    return pl.pallas_call(
        body, name="att_merge", grid=(t // tm,),
        in_specs=views + per_head + [pl.BlockSpec((tm, WIDTH), lambda i: (i, SEG_ZA // WIDTH))],
        out_specs=[tok, tok] + per_head,
        out_shape=[jax.ShapeDtypeStruct((t, WIDTH), MXU_DTYPE), jax.ShapeDtypeStruct((t, WIDTH), F32)]
        + [jax.ShapeDtypeStruct((t // d, d * 128), F32) for d in DILATIONS],
        scratch_shapes=[_tile_scratch(tm, WIDTH)] * 2 + [_tile_scratch(tm, 128)] * 3,
        compiler_params=_params(("parallel",)),
    )(*os_, *lses, proj)


def _att_merge_bwd(dproj, dy, w_up, o, proj, tm=1024):
    t = proj.shape[0]

    def body(dp_ref, dy_ref, wu_ref, o_ref, z_ref, dz_ref, do0, do1, do2, dl0, dl1, dl2, s_do, s_dl):
        z, ov, g = z_ref[...], o_ref[...], _mxu(dy_ref[...], wu_ref[...], _NT)
        sg = _sigmoid(z)
        do = g * z * sg
        dz_ref[...] = (g * ov * sg * (1.0 + z * (1.0 - sg))).astype(dz_ref.dtype)
        do0[...] = do.astype(do0.dtype)
        _store_tile(s_do, do)
        prod = do * ov
        delta = jnp.zeros((tm, 128), F32)
        for p in range(HEADS // 2):
            delta += _pair_sum_lanes(prod[:, p * 128:(p + 1) * 128], _pair_to_lanes(p, 0))
        dl0[...] = delta
        s_dl[0] = delta
        for d, do_v, dl_v in ((DILATIONS[1], do1, dl1), (DILATIONS[2], do2, dl2)):
            _to_strided_view(s_do, do_v, d)
            _to_strided_view(s_dl, dl_v, d)

    tok = pl.BlockSpec((tm, WIDTH), lambda i: (i, 0))
    seg = pl.BlockSpec((tm, WIDTH), lambda i: (i, SEG_ZA // WIDTH))
    views = [_strided_spec(tm, d, WIDTH) for d in DILATIONS]
    per_head = [_strided_spec(tm, d, 128) for d in DILATIONS]
    return pl.pallas_call(
        body, name="att_merge_bwd", grid=(t // tm,),
        in_specs=[pl.BlockSpec(memory_space=pl.ANY), pl.BlockSpec((tm, D_MODEL), lambda i: (i, 0)),
                  pl.BlockSpec((WIDTH, D_MODEL), lambda i: (0, 0)), tok, seg],
        out_specs=[seg] + views + per_head,
        out_shape=[jax.ShapeDtypeStruct((t, PACKED_WIDTH), MXU_DTYPE)]
        + [jax.ShapeDtypeStruct((t // d, d * WIDTH), MXU_DTYPE) for d in DILATIONS]
        + [jax.ShapeDtypeStruct((t // d, d * 128), F32) for d in DILATIONS],
        scratch_shapes=[_tile_scratch(tm, WIDTH), _tile_scratch(tm, 128)],
        input_output_aliases={0: 0},
        compiler_params=_params(("parallel",)),
    )(dproj, dy, w_up, o, proj)


def _shift_down(x, halo, s):
    if s == 0:
        return x
    xs = pltpu.roll(x, s, 0)
    head = jnp.where(_iota((8, x.shape[1]), 0) < s, pltpu.roll(halo, s, 0), xs[0:8])
    return jnp.concatenate([head, xs[8:]], axis=0)


def _shift_up(x, nxt, s):
    if s == 0:
        return x
    n = x.shape[0]
    xs = pltpu.roll(x, n - s, 0)
    tail = jnp.where(_iota((8, x.shape[1]), 0) >= 8 - s, pltpu.roll(nxt, 8 - s, 0), xs[n - 8:])
    return jnp.concatenate([xs[:n - 8], tail], axis=0)


def _conv_fwd(proj, conv_w, tm=2048):
    t = proj.shape[0]
    cb = SEG_B // WIDTH

    def body(x_ref, halo_ref, w_ref, c_ref):
        halo = jnp.where(pl.program_id(0) > 0, halo_ref[...], 0.0)
        x = x_ref[...]
        w = w_ref[...]
        acc = jnp.zeros((tm, WIDTH), F32)
        for j in range(GDN_CONV):
            acc += _shift_down(x, halo, GDN_CONV - 1 - j) * w[j:j + 1, :]
        c_ref[...] = acc

    return pl.pallas_call(
        body, name="conv_fwd", grid=(t // tm, 3),
        in_specs=[pl.BlockSpec((tm, WIDTH), lambda i, c: (i, cb + c)),
                  pl.BlockSpec((8, WIDTH), lambda i, c: (jnp.maximum(i * (tm // 8) - 1, 0), cb + c)),
                  pl.BlockSpec((GDN_CONV, WIDTH), lambda i, c: (0, c))],
        out_specs=pl.BlockSpec((tm, WIDTH), lambda i, c: (i, c)),
        out_shape=jax.ShapeDtypeStruct((t, 3 * WIDTH), F32),
        compiler_params=_params(("parallel", "parallel")),
    )(proj, proj, conv_w)


def _conv_bwd(dproj, dc, proj, conv_w, tm=2048):
    t = proj.shape[0]
    cb = SEG_B // WIDTH
    nt = t // tm

    def body(dp_ref, dc_ref, dcn_ref, x_ref, halo_ref, w_ref, dx_ref, dw_ref):
        i = pl.program_id(1)
        w = w_ref[...]
        dcn = jnp.where(i < nt - 1, dcn_ref[...], 0.0)
        dcv = dc_ref[...]
        acc = jnp.zeros((tm, WIDTH), F32)
        for j in range(GDN_CONV):
            acc += _shift_up(dcv, dcn, GDN_CONV - 1 - j) * w[j:j + 1, :]
        dx_ref[...] = acc.astype(dx_ref.dtype)
        halo = jnp.where(i > 0, halo_ref[...], 0.0)
        x = x_ref[...]
        row8 = _iota((8, WIDTH), 0)
        part = jnp.zeros((8, WIDTH), F32)
        for j in range(GDN_CONV):
            s = jnp.sum(dcv * _shift_down(x, halo, GDN_CONV - 1 - j), axis=0, keepdims=True)
            part += jnp.where(row8 == j, s, 0.0)

        @pl.when(i == 0)
        def _():
            dw_ref[...] = jnp.zeros_like(dw_ref)

        dw_ref[...] += part

    return pl.pallas_call(
        body, name="conv_bwd", grid=(3, nt),
        in_specs=[pl.BlockSpec(memory_space=pl.ANY),
                  pl.BlockSpec((tm, WIDTH), lambda c, i: (i, c)),
                  pl.BlockSpec((8, WIDTH), lambda c, i: (jnp.minimum((i + 1) * (tm // 8), t // 8 - 1), c)),
                  pl.BlockSpec((tm, WIDTH), lambda c, i: (i, cb + c)),
                  pl.BlockSpec((8, WIDTH), lambda c, i: (jnp.maximum(i * (tm // 8) - 1, 0), cb + c)),
                  pl.BlockSpec((GDN_CONV, WIDTH), lambda c, i: (0, c))],
        out_specs=[pl.BlockSpec((tm, WIDTH), lambda c, i: (i, cb + c)),
                   pl.BlockSpec((8, WIDTH), lambda c, i: (0, c))],
        out_shape=[jax.ShapeDtypeStruct((t, PACKED_WIDTH), MXU_DTYPE), jax.ShapeDtypeStruct((8, 3 * WIDTH), F32)],
        input_output_aliases={0: 0},
        compiler_params=_params(("parallel", "arbitrary")),
    )(dproj, dc, dc, proj, proj, conv_w)


def _chunk_matrices(tm):
    r, c = _iota((tm, tm), 0), _iota((tm, tm), 1)
    same = (r // GDN_CHUNK) == (c // GDN_CHUNK)
    return jnp.where(same & (c <= r), 1.0, 0.0), jnp.where(same, 1.0, 0.0)


def _gdn_gates(ba, a_log, dt_bias):
    al = ba + dt_bias
    return _sigmoid(ba), -jnp.exp(a_log) * _softplus(al), _sigmoid(al)


SCAN_HEADS = (0, 2, 4, 6, 1, 3, 5, 7)


def _head_lane_eye():
    return jnp.where(_iota((HEADS, 128), 1) == _iota((HEADS, 128), 0) + HEADS, 1.0, 0.0)


def _pair_ones():
    return jnp.where(_iota((128, 128), 0) // HEAD_DIM == _iota((128, 128), 1) // HEAD_DIM, 1.0, 0.0).astype(MXU_DTYPE)


def _pair_to_lanes(p, base):
    return jnp.where(_iota((128, 128), 1) == base + 2 * p + _iota((128, 128), 0) // HEAD_DIM, 1.0, 0.0).astype(MXU_DTYPE)


def _pair_sum_lanes(x, sel):
    hi, lo = _split(x)
    return jnp.dot(hi, sel, preferred_element_type=F32) + jnp.dot(lo, sel, preferred_element_type=F32)


def _gdn_prep(conv, proj, a_log, dt_bias, tm=512):
    t = proj.shape[0]
    nc = tm // GDN_CHUNK

    def body(c_ref, ba_ref, al_ref, dt_ref, q_ref, k_ref, v_ref, b_ref, g_ref, gl_ref, grow_ref):
        beta, g, _ = _gdn_gates(ba_ref[:, 0:128], al_ref[...], dt_ref[...])
        lmat, cmat = _chunk_matrices(tm)
        gc = jnp.dot(lmat, g, precision=HIGHEST, preferred_element_type=F32)
        gl = jnp.dot(cmat, g, precision=HIGHEST, preferred_element_type=F32)
        grow = lax.dot_general(_head_lane_eye(), gc, _NT, precision=HIGHEST, preferred_element_type=F32)
        ones = _pair_ones()
        first = _iota((1, 128), 1) < HEAD_DIM

        def spread(x, base, p):
            return jnp.where(first, x[:, base + 2 * p:base + 2 * p + 1], x[:, base + 2 * p + 1:base + 2 * p + 2])

        for p in range(HEADS // 2):
            for seg, ref, scale in ((0, q_ref, HEAD_DIM ** -0.5), (1, k_ref, 1.0), (2, v_ref, None)):
                c = c_ref[:, seg * WIDTH + p * 128:seg * WIDTH + (p + 1) * 128]
                a = c * _sigmoid(c)
                if scale is not None:
                    a = a * (lax.rsqrt(_pair_sum_lanes(a * a, ones) + NORM_EPS) * scale)
                ref[p] = a
            b_ref[p] = spread(beta, 0, p)
            g_ref[p] = spread(gc, HEADS, p)
            gl_ref[p] = spread(gl, HEADS, p)
        for pos, h in enumerate(SCAN_HEADS):
            for cc in range(nc):
                grow_ref[pos, cc] = grow[h:h + 1, cc * GDN_CHUNK:(cc + 1) * GDN_CHUNK]

    hm = pl.BlockSpec((HEADS // 2, tm, 128), lambda i: (0, i, 0))
    small = pl.BlockSpec((1, 128), lambda i: (0, 0))
    hm_shape = jax.ShapeDtypeStruct((HEADS // 2, t, 128), F32)
    return pl.pallas_call(
        body, name="gdn_prep", grid=(t // tm,),
        in_specs=[pl.BlockSpec((tm, 3 * WIDTH), lambda i: (i, 0)),
                  pl.BlockSpec((tm, WIDTH), lambda i: (i, SEG_BA // WIDTH)), small, small],
        out_specs=[hm] * 6 + [pl.BlockSpec((HEADS, nc, 1, GDN_CHUNK), lambda i: (0, i, 0, 0))],
        out_shape=[hm_shape] * 6 + [jax.ShapeDtypeStruct((HEADS, t // GDN_CHUNK, 1, GDN_CHUNK), F32)],
        compiler_params=_params(("parallel",)),
    )(conv, proj, a_log, dt_bias)


def _gdn_prep_bwd(dproj, conv, proj, a_log, dt_bias, dq, dk, dv, db, dg, dgl, dgrow, tm=512):
    t = proj.shape[0]
    nc = tm // GDN_CHUNK

    def body(dp_ref, c_ref, ba_ref, al_ref, dt_ref, dq_ref, dk_ref, dv_ref, db_ref, dg_ref, dgl_ref, dgrow_ref,
             dba_ref, dc_ref, small_ref, row_scr):
        beta, g, sig_al = _gdn_gates(ba_ref[:, 0:128], al_ref[...], dt_ref[...])
        d_beta = jnp.zeros((tm, 128), F32)
        d_gc = jnp.zeros((tm, 128), F32)
        d_gl = jnp.zeros((tm, 128), F32)
        ones = _pair_ones()
        for pos, h in enumerate(SCAN_HEADS):
            for cc in range(nc):
                row_scr[h:h + 1, cc * GDN_CHUNK:(cc + 1) * GDN_CHUNK] = dgrow_ref[pos, cc]
        for p in range(HEADS // 2):
            d_beta += _pair_sum_lanes(db_ref[p], _pair_to_lanes(p, 0))
            d_gc += _pair_sum_lanes(dg_ref[p], _pair_to_lanes(p, HEADS))
            d_gl += _pair_sum_lanes(dgl_ref[p], _pair_to_lanes(p, HEADS))
            for seg, ref, scale in ((0, dq_ref, HEAD_DIM ** -0.5), (1, dk_ref, 1.0), (2, dv_ref, None)):
                cols = slice(seg * WIDTH + p * 128, seg * WIDTH + (p + 1) * 128)
                c = c_ref[:, cols]
                sg = _sigmoid(c)
                da = ref[p]
                if scale is not None:
                    a = c * sg
                    r = lax.rsqrt(_pair_sum_lanes(a * a, ones) + NORM_EPS)
                    da = da * scale
                    da = r * da - a * (r * r * r) * _pair_sum_lanes(da * a, ones)
                dc_ref[:, cols] = da * (sg * (1.0 + c * (1.0 - sg)))
        d_gc += lax.dot_general(row_scr[...], _head_lane_eye(), _TN, precision=HIGHEST, preferred_element_type=F32)
        lmat, cmat = _chunk_matrices(tm)
        d_g = (lax.dot_general(lmat, d_gc, _TN, precision=HIGHEST, preferred_element_type=F32)
               + lax.dot_general(cmat, d_gl, _TN, precision=HIGHEST, preferred_element_type=F32))
        d_al = d_g * (-jnp.exp(al_ref[...])) * sig_al
        d_bl = d_beta * beta * (1.0 - beta)
        dba_ref[...] = jnp.concatenate([d_bl + d_al, jnp.zeros((tm, WIDTH - 128), F32)], axis=1).astype(dba_ref.dtype)
        row8 = _iota((8, 128), 0)
        part = (jnp.where(row8 == 0, jnp.sum(d_g * g, axis=0, keepdims=True), 0.0)
                + jnp.where(row8 == 1, jnp.sum(d_al, axis=0, keepdims=True), 0.0))

        @pl.when(pl.program_id(0) == 0)
        def _():
            small_ref[...] = jnp.zeros_like(small_ref)

        small_ref[...] += part

    hm = pl.BlockSpec((HEADS // 2, tm, 128), lambda i: (0, i, 0))
    small = pl.BlockSpec((1, 128), lambda i: (0, 0))
    seg = pl.BlockSpec((tm, WIDTH), lambda i: (i, SEG_BA // WIDTH))
    return pl.pallas_call(
        body, name="gdn_prep_bwd", grid=(t // tm,),
        in_specs=[pl.BlockSpec(memory_space=pl.ANY), pl.BlockSpec((tm, 3 * WIDTH), lambda i: (i, 0)), seg, small, small]
        + [hm] * 6 + [pl.BlockSpec((HEADS, nc, 1, GDN_CHUNK), lambda i: (0, i, 0, 0))],
        out_specs=[seg, pl.BlockSpec((tm, 3 * WIDTH), lambda i: (i, 0)), pl.BlockSpec((8, 128), lambda i: (0, 0))],
        out_shape=[jax.ShapeDtypeStruct((t, PACKED_WIDTH), MXU_DTYPE), jax.ShapeDtypeStruct((t, 3 * WIDTH), F32),
                   jax.ShapeDtypeStruct((8, 128), F32)],
        scratch_shapes=[pltpu.VMEM((HEADS, tm), F32)],
        input_output_aliases={0: 0},
        compiler_params=_params(("arbitrary",)),
    )(dproj, conv, proj, a_log, dt_bias, dq, dk, dv, db, dg, dgl, dgrow)


_BNN = (((2,), (1,)), ((0,), (0,)))
_BNT = (((2,), (2,)), ((0,), (0,)))
_BTN = (((1,), (1,)), ((0,), (0,)))


@jax.custom_vjp
def _MM_NN(a, b):
    return _mxu(a, b, _BNN)


@jax.custom_vjp
def _MM_NT(a, b):
    return _mxu(a, b, _BNT)


@jax.custom_vjp
def _MM_TN(a, b):
    return _mxu(a, b, _BTN)


_MM_NN.defvjp(lambda a, b: (_mxu(a, b, _BNN), (a, b)), lambda r, g: (_mxu(g, r[1], _BNT), _mxu(r[0], g, _BTN)))
_MM_NT.defvjp(lambda a, b: (_mxu(a, b, _BNT), (a, b)), lambda r, g: (_mxu(g, r[1], _BNN), _mxu(g, r[0], _BTN)))
_MM_TN.defvjp(lambda a, b: (_mxu(a, b, _BTN), (a, b)), lambda r, g: (_mxu(r[1], g, _BNT), _mxu(r[0], g, _BNN)))


def _split(a):
    hi = a.astype(MXU_DTYPE)
    return hi, (a - hi.astype(F32)).astype(MXU_DTYPE)


def _dot3(a, b, dims):
    (ah, al), (bh, bl) = a, b
    (ca,), (cb,) = dims[0]
    return lax.dot_general(jnp.concatenate([ah, ah, al], axis=ca), jnp.concatenate([bh, bl, bh], axis=cb), dims,
                           preferred_element_type=F32)


def _unit_lower_inverse(a):
    c = GDN_CHUNK
    eye = jnp.where(_iota((c, c), 0) == _iota((c, c), 1), 1.0, 0.0)
    x = eye - a
    p = a
    for _ in range(5):
        ps = _split(p)
        p = _dot3(ps, ps, _BNN)
        x = x + _dot3(_split(x), _split(p), _BNN)
    return x


@jax.custom_vjp
def _SAVED_INVERSE(a, t_inv):
    return t_inv


def _saved_inverse_bwd(t_inv, g):
    ts = _split(t_inv)
    return -_dot3(ts, _split(_dot3(_split(g), ts, _BNT)), _BTN), jnp.zeros_like(t_inv)


_SAVED_INVERSE.defvjp(lambda a, t_inv: (t_inv, t_inv), _saved_inverse_bwd)


def _gdn_chunk(q, k, v, beta, g1, g2, gl, state, t_inv=None):
    c = GDN_CHUNK
    if t_inv is None:
        _mm_nn, _mm_nt, _mm_tn = (functools.partial(_mxu, dims=dd) for dd in (_BNN, _BNT, _BTN))
    else:
        _mm_nn, _mm_nt, _mm_tn = _MM_NN, _MM_NT, _MM_TN
    row, col = _iota((c, c), 0), _iota((c, c), 1)
    incl, strict = row >= col, row > col
    decay = jnp.where(incl, jnp.exp(jnp.where(incl, g1 - g2, 0.0)), 0.0)
    eg = jnp.exp(g1)
    kb = k * beta
    a = _mm_nt(kb, k) * jnp.where(strict, decay, 0.0)
    inv = _unit_lower_inverse(a) if t_inv is None else _SAVED_INVERSE(a, t_inv)
    u = _mm_nn(inv, v * beta)
    w = _mm_nn(inv, kb * eg)
    attn = _mm_nt(q, k) * decay
    v_new = u - _mm_nn(w, state)
    o = _mm_nn(q * eg, state) + _mm_nn(attn, v_new)
    new_state = state * jnp.exp(gl) + _mm_tn(k * jnp.exp(gl - g1), v_new)
    return (o, new_state, inv) if t_inv is None else (o, new_state)


def _unpair(x):
    return jnp.concatenate([x[..., :HEAD_DIM], x[..., HEAD_DIM:]], axis=0)


def _gdn_fwd(q, k, v, beta, g, gl, grow, cpb=4):
    t = q.shape[1]
    rows = cpb * GDN_CHUNK
    lo, hi = slice(0, HEAD_DIM), slice(HEAD_DIM, 2 * HEAD_DIM)

    def body(q_ref, k_ref, v_ref, b_ref, g_ref, gl_ref, grow_ref, o_ref, st_ref, inv_ref, state):
        @pl.when(pl.program_id(0) == 0)
        def _():
            state[...] = jnp.zeros_like(state)

        s = state[...]
        for cc in range(cpb):
            sl = slice(cc * GDN_CHUNK, (cc + 1) * GDN_CHUNK)
            st_ref[:, cc, :, lo], st_ref[:, cc, :, hi] = s[:HEADS // 2], s[HEADS // 2:]
            g2 = jnp.broadcast_to(grow_ref[:, cc], (HEADS, GDN_CHUNK, GDN_CHUNK))
            o, s, inv = _gdn_chunk(*[_unpair(r[:, sl, :]) for r in (q_ref, k_ref, v_ref, b_ref, g_ref)], g2,
                                   _unpair(gl_ref[:, sl, :]), s)
            o_ref[:, sl, lo], o_ref[:, sl, hi] = o[:HEADS // 2], o[HEADS // 2:]
            inv_ref[:, cc, :, lo], inv_ref[:, cc, :, hi] = inv[:HEADS // 2], inv[HEADS // 2:]
        state[...] = s

    hm = pl.BlockSpec((HEADS // 2, rows, 128), lambda i: (0, i, 0))
    per_chunk = pl.BlockSpec((HEADS // 2, cpb, GDN_CHUNK, 128), lambda i: (0, i, 0, 0))
    chunk_shape = jax.ShapeDtypeStruct((HEADS // 2, t // GDN_CHUNK, GDN_CHUNK, 128), F32)
    return pl.pallas_call(
        body, name="gdn_fwd", grid=(t // rows,),
        in_specs=[hm] * 6 + [pl.BlockSpec((HEADS, cpb, 1, GDN_CHUNK), lambda i: (0, i, 0, 0))],
        out_specs=[hm, per_chunk, per_chunk],
        out_shape=[jax.ShapeDtypeStruct((HEADS // 2, t, 128), F32), chunk_shape, chunk_shape],
        scratch_shapes=[pltpu.VMEM((HEADS, GDN_CHUNK, HEAD_DIM), F32)],
        compiler_params=_params(("arbitrary",)),
    )(q, k, v, beta, g, gl, grow)


def _gdn_bwd(q, k, v, beta, g, gl, grow, states, invs, do, cpb=4):
    t = q.shape[1]
    rows = cpb * GDN_CHUNK
    nsteps = t // rows
    lo, hi = slice(0, HEAD_DIM), slice(HEAD_DIM, 2 * HEAD_DIM)

    def body(q_ref, k_ref, v_ref, b_ref, g_ref, gl_ref, grow_ref, st_ref, inv_ref, do_ref,
             dq_ref, dk_ref, dv_ref, db_ref, dg_ref, dgl_ref, dgrow_ref, dstate):
        @pl.when(pl.program_id(0) == 0)
        def _():
            dstate[...] = jnp.zeros_like(dstate)

        ds = dstate[...]
        for cc in reversed(range(cpb)):
            sl = slice(cc * GDN_CHUNK, (cc + 1) * GDN_CHUNK)
            g2 = jnp.broadcast_to(grow_ref[:, cc], (HEADS, GDN_CHUNK, GDN_CHUNK))
            _, vjp = jax.vjp(_gdn_chunk, *[_unpair(r[:, sl, :]) for r in (q_ref, k_ref, v_ref, b_ref, g_ref)], g2,
                             _unpair(gl_ref[:, sl, :]), _unpair(st_ref[:, cc]), _unpair(inv_ref[:, cc]))
            gq, gk, gv, gb, gg1, gg2, ggl, ds, _ = vjp((_unpair(do_ref[:, sl, :]), ds))
            for ref, val in ((dq_ref, gq), (dk_ref, gk), (dv_ref, gv), (db_ref, gb), (dg_ref, gg1), (dgl_ref, ggl)):
                ref[:, sl, lo], ref[:, sl, hi] = val[:HEADS // 2], val[HEADS // 2:]
            dgrow_ref[:, cc] = jnp.sum(gg2, axis=1, keepdims=True)
        dstate[...] = ds

    hm = pl.BlockSpec((HEADS // 2, rows, 128), lambda i: (0, nsteps - 1 - i, 0))
    rowspec = pl.BlockSpec((HEADS, cpb, 1, GDN_CHUNK), lambda i: (0, nsteps - 1 - i, 0, 0))
    per_chunk = pl.BlockSpec((HEADS // 2, cpb, GDN_CHUNK, 128), lambda i: (0, nsteps - 1 - i, 0, 0))
    hm_shape = jax.ShapeDtypeStruct((HEADS // 2, t, 128), F32)
    return pl.pallas_call(
        body, name="gdn_bwd", grid=(nsteps,),
        in_specs=[hm] * 6 + [rowspec, per_chunk, per_chunk, hm],
        out_specs=[hm] * 6 + [rowspec],
        out_shape=[hm_shape] * 6 + [jax.ShapeDtypeStruct((HEADS, t // GDN_CHUNK, 1, GDN_CHUNK), F32)],
        scratch_shapes=[pltpu.VMEM((HEADS, GDN_CHUNK, HEAD_DIM), F32)],
        compiler_params=_params(("arbitrary",)),
    )(q, k, v, beta, g, gl, grow, states, invs, do)


def _gdn_out(o_hm, gdn_norm_w, proj, tm=1024):
    t = proj.shape[0]

    def body(o_ref, w_ref, z_ref, oz_ref):
        w = w_ref[...]
        ones = _pair_ones()
        for p in range(HEADS // 2):
            cols = slice(p * 128, (p + 1) * 128)
            o = o_ref[p]
            z = z_ref[:, cols]
            r = lax.rsqrt(_pair_sum_lanes(o * o, ones) * (1.0 / HEAD_DIM) + NORM_EPS)
            oz_ref[:, cols] = (o * r * w * (z * _sigmoid(z))).astype(oz_ref.dtype)

    tok = pl.BlockSpec((tm, WIDTH), lambda i: (i, 0))
    return pl.pallas_call(
        body, name="gdn_out", grid=(t // tm,),
        in_specs=[pl.BlockSpec((HEADS // 2, tm, 128), lambda i: (0, i, 0)), pl.BlockSpec((1, 128), lambda i: (0, 0)),
                  pl.BlockSpec((tm, WIDTH), lambda i: (i, SEG_ZB // WIDTH))],
        out_specs=tok, out_shape=jax.ShapeDtypeStruct((t, WIDTH), MXU_DTYPE),
        compiler_params=_params(("parallel",)),
    )(o_hm, jnp.tile(gdn_norm_w, (1, 2)), proj)


def _gdn_out_bwd(dproj, dy, w_up, o_hm, gdn_norm_w, proj, tm=1024):
    t = proj.shape[0]

    def body(dp_ref, dy_ref, wu_ref, o_ref, w_ref, z_ref, dz_ref, do_ref, dw_ref):
        w = w_ref[...]
        ones = _pair_ones()
        dw = jnp.zeros((1, 128), F32)
        d_oz = _mxu(dy_ref[...], wu_ref[...], _NT)
        for p in range(HEADS // 2):
            cols = slice(p * 128, (p + 1) * 128)
            o = o_ref[p]
            z, g = z_ref[:, cols], d_oz[:, cols]
            sg = _sigmoid(z)
            r = lax.rsqrt(_pair_sum_lanes(o * o, ones) * (1.0 / HEAD_DIM) + NORM_EPS)
            dz_ref[:, cols] = (g * (o * r * w) * (sg * (1.0 + z * (1.0 - sg)))).astype(dz_ref.dtype)
            dn = g * (z * sg)
            dw += jnp.sum(dn * o * r, axis=0, keepdims=True)
            dnw = dn * w
            do_ref[p] = r * dnw - o * (r * r * r) * (_pair_sum_lanes(dnw * o, ones) * (1.0 / HEAD_DIM))

        @pl.when(pl.program_id(0) == 0)
        def _():
            dw_ref[...] = jnp.zeros_like(dw_ref)

        dw_ref[...] += jnp.where(_iota((8, 128), 0) == 0, dw, 0.0)

    tok = pl.BlockSpec((tm, WIDTH), lambda i: (i, 0))
    seg = pl.BlockSpec((tm, WIDTH), lambda i: (i, SEG_ZB // WIDTH))
    hm = pl.BlockSpec((HEADS // 2, tm, 128), lambda i: (0, i, 0))
    dz, do, dw = pl.pallas_call(
        body, name="gdn_out_bwd", grid=(t // tm,),
        in_specs=[pl.BlockSpec(memory_space=pl.ANY), pl.BlockSpec((tm, D_MODEL), lambda i: (i, 0)),
                  pl.BlockSpec((WIDTH, D_MODEL), lambda i: (0, 0)), hm, pl.BlockSpec((1, 128), lambda i: (0, 0)), seg],
        out_specs=[seg, hm, pl.BlockSpec((8, 128), lambda i: (0, 0))],
        out_shape=[jax.ShapeDtypeStruct((t, PACKED_WIDTH), MXU_DTYPE), jax.ShapeDtypeStruct((HEADS // 2, t, 128), F32),
                   jax.ShapeDtypeStruct((8, 128), F32)],
        input_output_aliases={0: 0},
        compiler_params=_params(("arbitrary",)),
    )(dproj, dy, w_up, o_hm, jnp.tile(gdn_norm_w, (1, 2)), proj)
    return dz, do, dw[:, :HEAD_DIM] + dw[:, HEAD_DIM:]


def _up_merge(oz_a, oz_b, w_up_a, w_up_b, proj, tm=512):
    t = proj.shape[0]

    def body(oa_ref, ob_ref, wa_ref, wb_ref, ga0, ga1, gb0, gb1, ya_ref, yb_ref, m_ref):
        ya = jnp.dot(oa_ref[...], wa_ref[...], preferred_element_type=F32)
        yb = jnp.dot(ob_ref[...], wb_ref[...], preferred_element_type=F32)
        ya_ref[...] = ya
        yb_ref[...] = yb
        for c, (ga, gb) in enumerate(((ga0, gb0), (ga1, gb1))):
            cols = slice(c * WIDTH, (c + 1) * WIDTH)
            m_ref[:, cols] = (_sigmoid(ga[...]) * ya[:, cols] + _sigmoid(gb[...]) * yb[:, cols]).astype(m_ref.dtype)

    def gate(seg, c):
        return pl.BlockSpec((tm, WIDTH), lambda i: (i, seg // WIDTH + c))

    tok_in = pl.BlockSpec((tm, WIDTH), lambda i: (i, 0))
    weight = pl.BlockSpec((WIDTH, D_MODEL), lambda i: (0, 0))
    tok = pl.BlockSpec((tm, D_MODEL), lambda i: (i, 0))
    return pl.pallas_call(
        body, name="up_merge", grid=(t // tm,),
        in_specs=[tok_in, tok_in, weight, weight, gate(SEG_GA, 0), gate(SEG_GA, 1), gate(SEG_GB, 0), gate(SEG_GB, 1)],
        out_specs=[tok, tok, tok],
        out_shape=[jax.ShapeDtypeStruct((t, D_MODEL), F32), jax.ShapeDtypeStruct((t, D_MODEL), F32),
                   jax.ShapeDtypeStruct((t, D_MODEL), MXU_DTYPE)],
        compiler_params=_params(("parallel",)),
    )(oz_a, oz_b, w_up_a, w_up_b, proj, proj, proj, proj)


def _merge_bwd(dproj, d_m, y, proj, seg, name, tm=2048):
    t = proj.shape[0]

    def body(*refs):
        dm_ref, y_ref, g_ref, dg_ref, dy_ref = refs[-5:]
        dm = dm_ref[...]
        s = _sigmoid(g_ref[...])
        dy_ref[...] = (dm * s).astype(dy_ref.dtype)
        dg_ref[...] = (dm * y_ref[...] * s * (1.0 - s)).astype(dg_ref.dtype)

    half = pl.BlockSpec((tm, WIDTH), lambda i, c: (i, c))
    gate = pl.BlockSpec((tm, WIDTH), lambda i, c: (i, seg // WIDTH + c))
    specs, args, aliases = [half, half, gate], [d_m, y, proj], {}
    if dproj is not None:
        specs, args, aliases = [pl.BlockSpec(memory_space=pl.ANY)] + specs, [dproj] + args, {0: 0}
    return pl.pallas_call(
        body, name=name, grid=(t // tm, 2), in_specs=specs, out_specs=[gate, half],
        out_shape=[jax.ShapeDtypeStruct((t, PACKED_WIDTH), MXU_DTYPE), jax.ShapeDtypeStruct((t, D_MODEL), MXU_DTYPE)],
        input_output_aliases=aliases,
        compiler_params=_params(("parallel", "parallel")),
    )(*args)


def _out_tail(merged, w_out, x, final_w, target, tm=1024):
    t = x.shape[0]
    tm = min(tm, t)

    def body(m_ref, wo_ref, x_ref, w_ref, t_ref, dxm_ref, dx_ref, loss_ref, dw_ref):
        x2 = x_ref[...] + jnp.dot(m_ref[...], wo_ref[...], preferred_element_type=F32)
        w = w_ref[...]
        r = lax.rsqrt(jnp.mean(x2 * x2, axis=-1, keepdims=True) + NORM_EPS)
        xn = x2 * r
        err = xn * w - t_ref[...]
        dy = err * (1.0 / D_MODEL)
        dyw = dy * w
        dx2 = r * dyw - x2 * (r * r * r) * jnp.mean(dyw * x2, axis=-1, keepdims=True)
        dx_ref[...] = dx2
        dxm_ref[...] = dx2.astype(dxm_ref.dtype)
        loss = 0.5 * jnp.sum(jnp.sum(err * err, axis=-1, keepdims=True) * (1.0 / D_MODEL), axis=0, keepdims=True)
        onehot = jnp.where((_iota((8, 128), 0) == 0) & (_iota((8, 128), 1) == 0), 1.0, 0.0)

        @pl.when(pl.program_id(0) == 0)
        def _():
            loss_ref[...] = jnp.zeros_like(loss_ref)
            dw_ref[...] = jnp.zeros_like(dw_ref)

        loss_ref[...] += loss * onehot
        dw_ref[...] += jnp.where(_iota((8, D_MODEL), 0) == 0, jnp.sum(dy * xn, axis=0, keepdims=True), 0.0)

    tok = pl.BlockSpec((tm, D_MODEL), lambda i: (i, 0))
    full = pl.BlockSpec((D_MODEL, D_MODEL), lambda i: (0, 0))
    return pl.pallas_call(
        body, name="out_tail", grid=(t // tm,),
        in_specs=[tok, full, tok, pl.BlockSpec((1, D_MODEL), lambda i: (0, 0)), tok],
        out_specs=[tok, tok, pl.BlockSpec((8, 128), lambda i: (0, 0)), pl.BlockSpec((8, D_MODEL), lambda i: (0, 0))],
        out_shape=[jax.ShapeDtypeStruct((t, D_MODEL), MXU_DTYPE), jax.ShapeDtypeStruct((t, D_MODEL), F32),
                   jax.ShapeDtypeStruct((8, 128), F32), jax.ShapeDtypeStruct((8, D_MODEL), F32)],
        compiler_params=_params(("arbitrary",)),
    )(merged, w_out, x, final_w, target)


def _dh_norm_bwd(dproj, wp, x, norm_w, dx2, after, blocks, name, carry=None, tm=1024, tk=PACKED_WIDTH // 4):
    t = x.shape[0]
    nk = PACKED_WIDTH // tk
    first, end = blocks

    def body(*refs):
        dp_ref, wp_ref, x_ref, w_ref, dx2_ref = refs[:5]
        dx_ref, dw_ref, acc = refs[-3:]
        kk = pl.program_id(1)
        part = _mxu(dp_ref[...], wp_ref[...], _NT)

        @pl.when(kk == 0)
        def _():
            acc[...] = part

        @pl.when(kk > 0)
        def _():
            acc[...] += part

        @pl.when((kk == 0) & (pl.program_id(0) == 0))
        def _():
            dw_ref[...] = jnp.zeros_like(dw_ref) if carry is None else refs[-4][...]

        @pl.when(kk == nk - 1)
        def _():
            xf, w, dh_ = x_ref[...], w_ref[...], acc[...]
            r = lax.rsqrt(jnp.mean(xf * xf, axis=-1, keepdims=True) + NORM_EPS)
            dhw = dh_ * w
            dx_ref[...] = dx2_ref[...] + r * dhw - xf * (r * r * r) * jnp.mean(dhw * xf, axis=-1, keepdims=True)
            dw_ref[...] += jnp.where(_iota((8, D_MODEL), 0) == 0, jnp.sum(dh_ * xf * r, axis=0, keepdims=True), 0.0)

    tok = pl.BlockSpec((tm, D_MODEL), lambda i, kk: (first + i, 0))
    small = pl.BlockSpec((8, D_MODEL), lambda i, kk: (0, 0))
    specs = [pl.BlockSpec((tm, tk), lambda i, kk: (first + i, kk)), pl.BlockSpec((D_MODEL, tk), lambda i, kk: (0, kk)),
             tok, pl.BlockSpec((1, D_MODEL), lambda i, kk: (0, 0)), tok, pl.BlockSpec(memory_space=pl.ANY)]
    args = [dproj, wp, x, norm_w, dx2, after]
    aliases = {}
    if carry is not None:
        specs += [pl.BlockSpec(memory_space=pl.ANY), small]
        args += list(carry)
        aliases = {len(args) - 2: 0}
    return pl.pallas_call(
        body, name=name, grid=(end - first, nk), in_specs=specs, out_specs=[tok, small],
        out_shape=[jax.ShapeDtypeStruct((t, D_MODEL), F32), jax.ShapeDtypeStruct((8, D_MODEL), F32)],
        scratch_shapes=[pltpu.VMEM((tm, D_MODEL), F32)],
        input_output_aliases=aliases,
        compiler_params=_params(("arbitrary", "arbitrary")),
    )(*args)


def _local_step(x, target, norm_w, wp, conv_w, a_log, dt_bias, gdn_norm_w, w_up_a, w_up_b, w_out, final_w,
                start_reduce, continue_reduce):
    t = x.shape[0]
    tables = _rope_tables(t)
    a_log = jnp.pad(a_log, ((0, 0), (HEADS, 128 - 2 * HEADS)))
    dt_bias = jnp.pad(dt_bias, ((0, 0), (HEADS, 128 - 2 * HEADS)))

    proj, h_t = _norm_proj(x, norm_w, wp)
    qkvs = _rope_fwd(proj, tables)
    outs, lses = zip(*[_att_fwd(qkvs[gi], d, f"att_fwd{gi}") for gi, d in enumerate(DILATIONS)])
    oz_a, o_a, *lse_views = _att_merge(outs, lses, proj)
    conv = _conv_fwd(proj, conv_w)
    gq, gk, gv, gb, gg, ggl, grow = _gdn_prep(conv, proj, a_log, dt_bias)
    o_b, states, invs = _gdn_fwd(gq, gk, gv, gb, gg, ggl, grow)
    oz_b = _gdn_out(o_b, gdn_norm_w, proj)
    big = dict(tm=1024, tn=1024, tk=1024)
    y_a, y_b, merged = _up_merge(oz_a, oz_b, w_up_a, w_up_b, proj)
    dx2_m, dx2, loss_blk, d_final = _out_tail(merged, w_out, x, final_w, target)

    d_wout = _matmul(merged, dx2_m, "tn", "d_w_out", **big)
    d_m = _matmul(dx2_m, w_out, "nt", "d_merged", **big)
    dproj, dy_a = _merge_bwd(None, d_m, y_a, proj, SEG_GA, "merge_bwd_a")
    dproj, dy_b = _merge_bwd(dproj, d_m, y_b, proj, SEG_GB, "merge_bwd_b")
    d_wua = _matmul(oz_a, dy_a, "tn", "d_w_up_a", **big)
    d_wub = _matmul(oz_b, dy_b, "tn", "d_w_up_b", **big)
    dproj, *views = _att_merge_bwd(dproj, dy_a, w_up_a, o_a, proj)
    do_views, delta_views = views[:3], views[3:]
    dqs, dkvs = zip(*[_att_bwd(qkvs[gi], do_views[gi], lse_views[gi], delta_views[gi], d, f"att_bwd{gi}")
                      for gi, d in enumerate(DILATIONS)])
    dproj = _rope_bwd(dproj, dqs, dkvs, tables)
    dproj, do_b, d_gnw = _gdn_out_bwd(dproj, dy_b, w_up_b, o_b, gdn_norm_w, proj)
    dgq, dgk, dgv, dgb, dgg, dggl, dgrow = _gdn_bwd(gq, gk, gv, gb, gg, ggl, grow, states, invs, do_b)
    dproj, dconv, d_small = _gdn_prep_bwd(dproj, conv, proj, a_log, dt_bias, dgq, dgk, dgv, dgb, dgg, dggl, dgrow)
    dproj, d_convw = _conv_bwd(dproj, dconv, proj, conv_w)
    d_wp = _matmul(h_t, dproj, "nn", "d_w_in", tm=1024, tn=PACKED_WIDTH // 4, tk=1024)
    in_flight, token = start_reduce(d_wp, d_wua, d_wub, d_wout, d_convw[0:GDN_CONV])
    nblk = t // 1024
    cut = max(1, nblk // 4)
    part = _dh_norm_bwd(dproj, wp, x, norm_w, dx2, token, (0, cut), "dh_norm_bwd_a")
    in_flight, token = continue_reduce(in_flight, part[0])
    grad_x, d_norm = _dh_norm_bwd(dproj, wp, x, norm_w, dx2, token, (cut, nblk), "dh_norm_bwd_b", carry=part)
    return dict(loss=loss_blk, grad_x=grad_x, norm_w=d_norm[0:1], in_flight=in_flight,
                a_log=d_small[0:1, HEADS:2 * HEADS], dt_bias=d_small[1:2, HEADS:2 * HEADS], gdn_norm_w=d_gnw[0:1],
                final_norm_w=d_final[0:1])


SHARDS = 4
W_IN_SHARD = IN_WIDTH // SHARDS
ROWS_UP = WIDTH * (D_MODEL // SHARDS) // 128
ROWS_OUT = (D_MODEL // SHARDS) * D_MODEL // 128
CONV_SHARD = 3 * WIDTH // SHARDS
ROWS_CONV = 16
SLAB_ROWS = 2 * ROWS_UP + ROWS_OUT + 2 * ROWS_CONV
HALF_ROWS = SLAB_ROWS // 2
BIG_HALF = (D_MODEL // 2, W_IN_SHARD)
SMALL_HALF = (HALF_ROWS, 128)
MESH = pl.DeviceIdType.MESH
ANY = pl.BlockSpec(memory_space=pl.ANY)


def _pad_rows(a, rows):
    return jnp.pad(a, ((0, rows - a.shape[0]), (0, 0)))


def _pack_slab(w_up_a, w_up_b, w_out, conv, conv_lo):
    parts = [w_up_a.reshape(ROWS_UP, 128), w_up_b.reshape(ROWS_UP, 128), w_out.reshape(ROWS_OUT, 128),
             _pad_rows(conv.reshape(-1, 128), ROWS_CONV), _pad_rows(conv_lo.reshape(-1, 128), ROWS_CONV)]
    return jnp.concatenate(parts, axis=0).reshape(2, *SMALL_HALF)


def _unpack_slab(slab):
    slab = slab.reshape(SLAB_ROWS, 128)
    r0 = 0
    out = []
    for rows, shape in ((ROWS_UP, (WIDTH, D_MODEL // SHARDS)), (ROWS_UP, (WIDTH, D_MODEL // SHARDS)),
                        (ROWS_OUT, (D_MODEL // SHARDS, D_MODEL)), (ROWS_CONV, None), (ROWS_CONV, None)):
        part = slab[r0:r0 + rows]
        out.append(part[:GDN_CONV * CONV_SHARD // 128].reshape(GDN_CONV, CONV_SHARD) if shape is None else part.reshape(shape))
        r0 += rows
    return out


def _mesh_position():
    x, y, c = lax.axis_index("x"), lax.axis_index("y"), lax.axis_index("c")
    return x, y, c, [(1 - x, y), (x, 1 - y), (1 - x, 1 - y)]


def _gather_weights(shards):
    n = len(shards)

    def body(*refs):
        in_refs, out_refs, (send_sems, recv_sems) = refs[:n], refs[n:2 * n], refs[2 * n:]
        x, y, c, chips = _mesh_position()

        def half(a, chip, which):
            return out_refs[a].at[2 * chip[0] + chip[1], which]

        def copy(k, src, dst, to):
            return pltpu.make_async_remote_copy(src_ref=src, dst_ref=dst, send_sem=send_sems.at[k], recv_sem=recv_sems.at[k],
                                                device_id=to, device_id_type=MESH)

        pairs = [(a, j, chip) for a in range(n) for j, chip in enumerate(chips)]
        first = [copy(6 * a + j, in_refs[a].at[c], half(a, (x, y), c), (*chip, c)) for a, j, chip in pairs]
        for cp in first:
            cp.start()
        passed = [copy(6 * a + 3 + j, half(a, chip, c), half(a, chip, c), (x, y, 1 - c)) for a, j, chip in pairs]
        for i, (a, j, chip) in enumerate(pairs):
            copy(6 * a + j, half(a, chip, c), half(a, chip, c), (x, y, c)).wait_recv()
            passed[i].start()
        for a, j, chip in pairs:
            copy(6 * a + 3 + j, half(a, chip, 1 - c), half(a, chip, 1 - c), (x, y, c)).wait_recv()
        for cp in first + passed:
            cp.wait_send()

    return pl.pallas_call(
        body, name="gather_weights", in_specs=[ANY] * n, out_specs=[ANY] * n,
        out_shape=[jax.ShapeDtypeStruct((SHARDS, *s.shape), s.dtype) for s in shards],
        scratch_shapes=[pltpu.SemaphoreType.DMA((6 * n,)), pltpu.SemaphoreType.DMA((6 * n,))],
    )(*shards)


def _pair_sum(grads, recv, blk, name):
    _, _, rows, cols = grads.shape

    def body(c_ref, g_ref, r_ref, o_ref):
        o_ref[...] = (g_ref[0] + r_ref[...]).astype(o_ref.dtype)

    spec = pl.BlockSpec((1, blk, cols), lambda s, i, c_ref: (s, i, 0))
    return pl.pallas_call(
        body, name=name,
        grid_spec=pltpu.PrefetchScalarGridSpec(
            num_scalar_prefetch=1, grid=(SHARDS, rows // blk),
            in_specs=[pl.BlockSpec((1, 1, blk, cols), lambda s, i, c_ref: (s, c_ref[0], i, 0)), spec],
            out_specs=spec),
        out_shape=jax.ShapeDtypeStruct((SHARDS, rows, cols), MXU_DTYPE),
        compiler_params=_params(("parallel", "parallel")),
    )(lax.axis_index("c").astype(jnp.int32).reshape(1), grads, recv)


_HBM = pl.BlockSpec(memory_space=pltpu.HBM)
_SEM = pl.BlockSpec(memory_space=pltpu.SEMAPHORE)
_DATAFLOW = pltpu.SideEffectType.DATAFLOW_SIDE_EFFECTING


def _scatter_copies(p_refs, l_refs, send_sems, recv_sems):
    x, y, c, chips = _mesh_position()
    return [pltpu.make_async_remote_copy(src_ref=p_refs[a].at[2 * chip[0] + chip[1]], dst_ref=l_refs[a].at[j],
                                         send_sem=send_sems.at[3 * a + j], recv_sem=recv_sems.at[3 * a + j],
                                         device_id=(*chip, c), device_id_type=MESH)
            for a in range(len(p_refs)) for j, chip in enumerate(chips)]


def _exchange_copies(g_refs, l_refs, send_sems, recv_sems):
    x, y, c, _ = _mesh_position()
    return [pltpu.make_async_remote_copy(src_ref=g_refs[a].at[s, 1 - c], dst_ref=l_refs[a].at[s],
                                         send_sem=send_sems.at[SHARDS * a + s], recv_sem=recv_sems.at[SHARDS * a + s],
                                         device_id=(x, y, 1 - c), device_id_type=MESH)
            for a in range(len(g_refs)) for s in range(SHARDS)]


def _copies_start(name, copies, count, sources, land_shapes):
    n = len(sources)
    lands = [lax.empty(shape, src.dtype) for shape, src in zip(land_shapes, sources)]

    def body(*refs):
        s_refs, l_refs, send_sems, recv_sems, token = refs[:n], refs[n:2 * n], refs[2 * n], refs[2 * n + 1], refs[-1]
        for cp in copies(s_refs, l_refs, send_sems, recv_sems):
            cp.start()
        token[...] = jnp.zeros_like(token)

    operands = [pltpu.with_memory_space_constraint(a, pltpu.HBM) for a in (*sources, *lands)]
    return pl.pallas_call(
        body, name=name, in_specs=[_HBM] * (2 * n),
        out_shape=(pltpu.SemaphoreType.DMA((count,)), pltpu.SemaphoreType.DMA((count,)),
                   *[pltpu.HBM(a.shape, a.dtype) for a in operands], jax.ShapeDtypeStruct((8, 128), F32)),
        out_specs=(_SEM, _SEM, *[_HBM] * (2 * n), pl.BlockSpec(memory_space=pltpu.VMEM)),
        input_output_aliases={i: 2 + i for i in range(2 * n)},
        compiler_params=pltpu.CompilerParams(has_side_effects=_DATAFLOW),
    )(*operands)


def _copies_wait(name, copies, send_sems, recv_sems, passed, after):
    n = len(passed) // 2

    def body(*refs):
        s_refs, l_refs, send_s, recv_s = refs[:n], refs[n:2 * n], refs[2 * n], refs[2 * n + 1]
        for cp in copies(s_refs, l_refs, send_s, recv_s):
            cp.wait_send()
            cp.wait_recv()

    return pl.pallas_call(
        body, name=name, in_specs=[_HBM] * (2 * n) + [_SEM, _SEM, ANY],
        out_shape=[pltpu.HBM(a.shape, a.dtype) for a in passed], out_specs=[_HBM] * (2 * n),
        input_output_aliases={i: i for i in range(2 * n)},
        compiler_params=pltpu.CompilerParams(has_side_effects=_DATAFLOW),
    )(*passed, send_sems, recv_sems, after)


def _chip_sum(pairs, recv, blk, name):
    _, rows, cols = pairs.shape

    def body(pos_ref, p_ref, r_ref, o_ref):
        o_ref[0] = ((p_ref[0].astype(F32) + r_ref[0].astype(F32)) + r_ref[1].astype(F32)) + r_ref[2].astype(F32)

    pos = jnp.stack([2 * lax.axis_index("x") + lax.axis_index("y"), lax.axis_index("c")]).astype(jnp.int32)
    return pl.pallas_call(
        body, name=name,
        grid_spec=pltpu.PrefetchScalarGridSpec(
            num_scalar_prefetch=1, grid=(rows // blk,),
            in_specs=[pl.BlockSpec((1, blk, cols), lambda i, pos_ref: (pos_ref[0], i, 0)),
                      pl.BlockSpec((3, blk, cols), lambda i, pos_ref: (0, i, 0))],
            out_specs=pl.BlockSpec((1, blk, cols), lambda i, pos_ref: (pos_ref[1], i, 0))),
        out_shape=jax.ShapeDtypeStruct((2, rows, cols), F32),
        compiler_params=_params(("parallel",)),
    )(pos, pairs, recv)


def _share_total(totals):
    n = len(totals)

    def body(*refs):
        t_refs, out_refs, (send_sems, recv_sems) = refs[:n], refs[n:2 * n], refs[2 * n:]
        x, y, c, _ = _mesh_position()
        copies = [pltpu.make_async_remote_copy(src_ref=t_refs[a].at[c], dst_ref=out_refs[a].at[c], send_sem=send_sems.at[a],
                                               recv_sem=recv_sems.at[a], device_id=(x, y, 1 - c), device_id_type=MESH)
                  for a in range(n)]
        for cp in copies:
            cp.start()
        for a in range(n):
            other = out_refs[a].at[1 - c]
            pltpu.make_async_remote_copy(src_ref=other, dst_ref=other, send_sem=send_sems.at[a], recv_sem=recv_sems.at[a],
                                         device_id=(x, y, c), device_id_type=MESH).wait_recv()
        for cp in copies:
            cp.wait_send()

    return pl.pallas_call(
        body, name="share_total", in_specs=[ANY] * n, out_specs=[ANY] * n,
        out_shape=[jax.ShapeDtypeStruct(t.shape, F32) for t in totals],
        scratch_shapes=[pltpu.SemaphoreType.DMA((n,)), pltpu.SemaphoreType.DMA((n,))],
        input_output_aliases={a: a for a in range(n)},
    )(*totals)


def _allreduce_small(block):
    def body(b_ref, out_ref, gath, send_sems, recv_sems):
        x, y, c, _ = _mesh_position()
        me = 4 * x + 2 * y + c
        gath[me] = b_ref[...]
        copies = []
        for k in range(1, 8):
            peer = (x ^ (k >> 2), y ^ ((k >> 1) & 1), c ^ (k & 1))
            copies.append(pltpu.make_async_remote_copy(src_ref=b_ref, dst_ref=gath.at[me], send_sem=send_sems.at[k - 1],
                                                       recv_sem=recv_sems.at[k - 1], device_id=peer, device_id_type=MESH))
        for cp in copies:
            cp.start()
        for k in range(1, 8):
            src = 4 * (x ^ (k >> 2)) + 2 * (y ^ ((k >> 1) & 1)) + (c ^ (k & 1))
            pltpu.make_async_remote_copy(src_ref=b_ref, dst_ref=gath.at[src], send_sem=send_sems.at[k - 1],
                                         recv_sem=recv_sems.at[k - 1], device_id=(x, y, c), device_id_type=MESH).wait_recv()
        for cp in copies:
            cp.wait_send()
        acc = gath[0]
        for d in range(1, 8):
            acc = acc + gath[d]
        out_ref[...] = acc

    vm = pl.BlockSpec(memory_space=pltpu.VMEM)
    return pl.pallas_call(
        body, name="allreduce_small", in_specs=[vm], out_specs=vm,
        out_shape=jax.ShapeDtypeStruct((8, D_MODEL), F32),
        scratch_shapes=[pltpu.VMEM((8, 8, D_MODEL), F32), pltpu.SemaphoreType.DMA((7,)), pltpu.SemaphoreType.DMA((7,))],
    )(block)


def _adamw(w, g, m, v, name):
    rows, cols = w.shape
    tr = 128 if rows % 128 == 0 else rows

    def body(w_ref, g_ref, m_ref, v_ref, d_ref, nm_ref, nv_ref):
        gv = g_ref[...]
        nm = ADAM_B1 * m_ref[...] + (1.0 - ADAM_B1) * gv
        nv = ADAM_B2 * v_ref[...] + (1.0 - ADAM_B2) * (gv * gv)
        m_hat = nm / (1.0 - ADAM_B1 ** ADAM_STEP)
        v_hat = nv / (1.0 - ADAM_B2 ** ADAM_STEP)
        d_ref[...] = -ADAM_LR * (m_hat / (jnp.sqrt(v_hat) + ADAM_EPS) + ADAM_WD * w_ref[...])
        nm_ref[...] = nm
        nv_ref[...] = nv

    spec = pl.BlockSpec((tr, cols), lambda i: (i, 0))
    shape = jax.ShapeDtypeStruct((rows, cols), F32)
    return pl.pallas_call(
        body, name=name, grid=(rows // tr,), in_specs=[spec] * 4, out_specs=[spec] * 3, out_shape=[shape] * 3,
        compiler_params=_params(("parallel",)),
    )(w, g, m, v)


def kernel(x, norm_w, w_in, conv_w, a_log, dt_bias, gdn_norm_w, w_up_a, w_up_b, w_out, final_norm_w, loss_target, m_norm_w, m_w_in, m_conv_w, m_a_log, m_dt_bias, m_gdn_norm_w, m_w_up_a, m_w_up_b, m_w_out, m_final_norm_w, v_norm_w, v_w_in, v_conv_w, v_a_log, v_dt_bias, v_gdn_norm_w, v_w_up_a, v_w_up_b, v_w_out, v_final_norm_w):
    conv_hi = conv_w[0].astype(MXU_DTYPE)
    conv_lo = (conv_w[0] - conv_hi.astype(F32)).astype(MXU_DTYPE)
    big = w_in[0].astype(MXU_DTYPE).reshape(2, *BIG_HALF)
    slab = _pack_slab(w_up_a[0].astype(MXU_DTYPE), w_up_b[0].astype(MXU_DTYPE), w_out[0].astype(MXU_DTYPE), conv_hi, conv_lo)
    own_shard = 2 * lax.axis_index("x") + lax.axis_index("y")
    bigs, slabs = _gather_weights([big, slab])
    bigs = lax.dynamic_update_slice(bigs, big[None], (own_shard, 0, 0, 0)).reshape(SHARDS, D_MODEL, W_IN_SHARD)
    slabs = lax.dynamic_update_slice(slabs, slab[None], (own_shard, 0, 0, 0))
    parts = [_unpack_slab(slabs[s]) for s in range(SHARDS)]
    split = BA_END - (SHARDS - 1) * W_IN_SHARD
    wp = jnp.concatenate([bigs[s] for s in range(SHARDS - 1)]
                         + [bigs[-1][:, :split], jnp.zeros((D_MODEL, SEG_GA - BA_END), MXU_DTYPE), bigs[-1][:, split:]], axis=1)
    w_up_a_full = jnp.concatenate([p[0] for p in parts], axis=1)
    w_up_b_full = jnp.concatenate([p[1] for p in parts], axis=1)
    w_out_full = jnp.concatenate([p[2] for p in parts], axis=0)
    conv_full = jnp.concatenate([p[3].astype(F32) + p[4].astype(F32) for p in parts], axis=1)

    blocks, tags = (128, HALF_ROWS), ("w_in", "slab")

    def start_reduce(d_wp, d_w_up_a, d_w_up_b, d_w_out, d_conv_w):
        d_w_in = [d_wp[:, s * W_IN_SHARD:(s + 1) * W_IN_SHARD] for s in range(SHARDS - 1)]
        d_w_in.append(jnp.concatenate([d_wp[:, (SHARDS - 1) * W_IN_SHARD:BA_END], d_wp[:, SEG_GA:]], axis=1))
        zero_conv = jnp.zeros((GDN_CONV, CONV_SHARD), F32)
        grads = [jnp.stack(d_w_in).reshape(SHARDS, 2, *BIG_HALF),
                 jnp.stack([_pack_slab(d_w_up_a[:, s * 256:(s + 1) * 256], d_w_up_b[:, s * 256:(s + 1) * 256],
                                       d_w_out[s * 256:(s + 1) * 256], d_conv_w[:, s * CONV_SHARD:(s + 1) * CONV_SHARD],
                                       zero_conv) for s in range(SHARDS)])]
        *in_flight, token = _copies_start("exchange_start", _exchange_copies, 2 * SHARDS, grads,
                                          [(SHARDS, *gr.shape[2:]) for gr in grads])
        return in_flight, token

    def continue_reduce(in_flight, after):
        send_sems, recv_sems, *passed = in_flight
        arrived = _copies_wait("exchange_wait", _exchange_copies, send_sems, recv_sems, passed, after)
        grads, from_sibling = arrived[:2], arrived[2:]
        pairs = [_pair_sum(gr, fs, blk, f"pair_sum_{tag}") for gr, fs, blk, tag in zip(grads, from_sibling, blocks, tags)]
        *in_flight, token = _copies_start("scatter_start", _scatter_copies, 2 * 3, pairs, [(3, *p.shape[1:]) for p in pairs])
        return in_flight, token

    g = _local_step(x[0], loss_target[0], norm_w, wp, conv_full, a_log, dt_bias, gdn_norm_w,
                    w_up_a_full, w_up_b_full, w_out_full, final_norm_w[None], start_reduce, continue_reduce)

    send_sems, recv_sems, *passed = g["in_flight"]
    arrived = _copies_wait("scatter_wait", _scatter_copies, send_sems, recv_sems, passed, g["grad_x"])
    pairs, from_chips = arrived[:2], arrived[2:]
    total_big, total_slab = _share_total([_chip_sum(p, fc, blk, f"chip_sum_{tag}")
                                          for p, fc, blk, tag in zip(pairs, from_chips, blocks, tags)])
    g_w_in = total_big.reshape(D_MODEL, W_IN_SHARD)
    g_w_up_a, g_w_up_b, g_w_out, g_conv, _ = _unpack_slab(total_slab)

    row2 = jnp.concatenate([g["gdn_norm_w"], g["a_log"], g["dt_bias"], g["loss"][0:1, 0:1],
                            jnp.zeros((1, D_MODEL - HEAD_DIM - 2 * HEADS - 1), F32)], axis=1)
    small = _allreduce_small(jnp.concatenate([g["norm_w"], g["final_norm_w"], row2, jnp.zeros((5, D_MODEL), F32)], axis=0))
    g_norm, g_final = small[0:1], small[1]
    g_gnw, g_alog, g_dt = small[2:3, 0:HEAD_DIM], small[2:3, HEAD_DIM:HEAD_DIM + HEADS], small[2:3, HEAD_DIM + HEADS:HEAD_DIM + 2 * HEADS]
    loss = small[2, HEAD_DIM + 2 * HEADS]

    names = ["norm_w", "w_in", "conv_w", "a_log", "dt_bias", "gdn_norm_w", "w_up_a", "w_up_b", "w_out", "final_norm_w"]
    weights = dict(zip(names, (norm_w, w_in, conv_w, a_log, dt_bias, gdn_norm_w, w_up_a, w_up_b, w_out, final_norm_w)))
    ms = dict(zip(names, (m_norm_w, m_w_in, m_conv_w, m_a_log, m_dt_bias, m_gdn_norm_w, m_w_up_a, m_w_up_b, m_w_out, m_final_norm_w)))
    vs = dict(zip(names, (v_norm_w, v_w_in, v_conv_w, v_a_log, v_dt_bias, v_gdn_norm_w, v_w_up_a, v_w_up_b, v_w_out, v_final_norm_w)))
    grads2d = dict(norm_w=g_norm, w_in=g_w_in, conv_w=g_conv, a_log=g_alog, dt_bias=g_dt, gdn_norm_w=g_gnw,
                   w_up_a=g_w_up_a, w_up_b=g_w_up_b, w_out=g_w_out, final_norm_w=g_final[None])
    grad_out, delta, new_m, new_v = [], [], [], []
    for n in names:
        shape = weights[n].shape
        two_d = grads2d[n].shape
        d, nm, nv = _adamw(weights[n].reshape(two_d), grads2d[n], ms[n].reshape(two_d), vs[n].reshape(two_d), f"adamw_{n}")
        grad_out.append(grads2d[n].reshape(shape))
        delta.append(d.reshape(shape))
        new_m.append(nm.reshape(shape))
        new_v.append(nv.reshape(shape))
    return (loss, g["grad_x"][None], *grad_out, *delta, *new_m, *new_v)
```
